```python
import math
import jax, jax.numpy as jnp
from jax import lax
import numpy as np

D_MODEL = 1024
BATCH = 8
SEQ = 4096
DEPTH = 1

MLA_HEADS = 8
MLA_NOPE = 64
MLA_ROPE = 32
MLA_V = 64
MLA_Q_RANK = 384
MLA_KV_RANK = 256
RET_HEADS = 4
RET_DK = 64
RET_DV = 128
RET_CHUNK = 128
FF_DIM = 2816
Q_BLOCK = 128
ROPE_BASE = 10000.0
NORM_EPS = 1e-6
GN_EPS = 1e-6

IN_SIZES = (MLA_Q_RANK, MLA_KV_RANK, MLA_ROPE,
            RET_HEADS * RET_DK, RET_HEADS * RET_DK, RET_HEADS * RET_DV, RET_HEADS * RET_DV,
            D_MODEL, D_MODEL)
IN_WIDTH = sum(IN_SIZES)

kernel_name = "hybrid_mla_retention_gated_macaron"


def _rmsnorm(x, w):
    xf = x.astype(jnp.float32)
    y = xf * lax.rsqrt(jnp.mean(xf * xf, axis=-1, keepdims=True) + NORM_EPS)
    return (y * w.astype(jnp.float32)).astype(x.dtype)


def _swiglu(x, w1, w2):
    gate, up = jnp.split(x @ w1, 2, axis=-1)
    return (jax.nn.silu(gate) * up) @ w2


def _rope(x, positions):
    half = x.shape[-1] // 2
    inv_freq = ROPE_BASE ** (-jnp.arange(half, dtype=jnp.float32) / half)
    ang = positions.astype(jnp.float32)[..., None] * inv_freq
    cos = jnp.cos(ang)[:, :, None, :]
    sin = jnp.sin(ang)[:, :, None, :]
    xf = x.astype(jnp.float32)
    x1, x2 = xf[..., :half], xf[..., half:]
    return jnp.concatenate([x1 * cos - x2 * sin, x2 * cos + x1 * sin], axis=-1).astype(x.dtype)


def _split_points():
    points, acc = [], 0
    for s in IN_SIZES[:-1]:
        acc += s
        points.append(acc)
    return points


def _causal_attention(q, k, v, scale):
    b, h, s, dqk = q.shape
    nb = s // Q_BLOCK
    q_blocks = jnp.moveaxis(q.reshape(b, h, nb, Q_BLOCK, dqk), 2, 0)
    k_pos = jnp.arange(s)

    def one_block(args):
        q_blk, i = args
        scores = jnp.einsum('bhqd,bhkd->bhqk', q_blk, k).astype(jnp.float32) * scale
        q_pos = i * Q_BLOCK + jnp.arange(Q_BLOCK)
        scores = jnp.where(q_pos[:, None] >= k_pos[None, :], scores, -jnp.inf)
        p = jax.nn.softmax(scores, axis=-1)
        return jnp.einsum('bhqk,bhkd->bhqd', p.astype(v.dtype), v)

    out = lax.map(one_block, (q_blocks, jnp.arange(nb)))
    return jnp.moveaxis(out, 0, 2).reshape(b, h, s, v.shape[-1])


def _chunkwise_retention(q, k, v):
    b, h, s, dk = q.shape
    dv = v.shape[-1]
    n = s // RET_CHUNK
    log_gamma = jnp.log(1.0 - 2.0 ** (-5.0 - jnp.arange(h, dtype=jnp.float32)))
    idx = jnp.arange(RET_CHUNK, dtype=jnp.float32)
    diff = idx[:, None] - idx[None, :]
    decay_intra = jnp.where(diff >= 0, jnp.exp(jnp.maximum(diff, 0.0) * log_gamma[:, None, None]), 0.0)
    zeta = jnp.exp((RET_CHUNK - 1 - idx) * log_gamma[:, None])
    xi = jnp.exp((idx + 1.0) * log_gamma[:, None])
    chunk_decay = jnp.exp(RET_CHUNK * log_gamma)

    qc = q.reshape(b, h, n, RET_CHUNK, dk)
    kc = k.reshape(b, h, n, RET_CHUNK, dk)
    vc = v.reshape(b, h, n, RET_CHUNK, dv)

    inner = jnp.einsum('bhncd,bhnsd->bhncs', qc, kc) * decay_intra[None, :, None].astype(q.dtype)
    y_inner = jnp.einsum('bhncs,bhnse->bhnce', inner, vc)

    kv = jnp.einsum('bhnsd,bhnse->bhnde', kc * zeta[None, :, None, :, None].astype(k.dtype), vc)
    decay_c = chunk_decay.astype(kv.dtype)[None, :, None, None]

    def step(state, kv_n):
        return state * decay_c + kv_n, state

    _, r_prev = lax.scan(step, jnp.zeros_like(kv[:, :, 0]), jnp.moveaxis(kv, 2, 0))
    r_prev = jnp.moveaxis(r_prev, 0, 2)
    y_cross = jnp.einsum('bhncd,bhnde->bhnce', qc, r_prev) * xi[None, :, None, :, None].astype(q.dtype)
    y = (y_inner + y_cross).reshape(b, h, s, dv)

    yf = y.astype(jnp.float32)
    mu = jnp.mean(yf, axis=-1, keepdims=True)
    var = jnp.mean(jnp.square(yf - mu), axis=-1, keepdims=True)
    return ((yf - mu) * lax.rsqrt(var + GN_EPS)).astype(v.dtype)


def _token_mixer(h, positions, w_in, mla_q_norm_w, mla_w_uq, mla_kv_norm_w, mla_w_ukv,
                 ret_gn_w, w_branch_mla, w_branch_ret, w_out):
    b, s, _ = h.shape
    proj = h @ w_in
    c_q, c_kv, k_rope_raw, rq, rk, rv, rg, gate_mla, gate_ret = jnp.split(proj, _split_points(), axis=-1)

    q = (_rmsnorm(c_q, mla_q_norm_w) @ mla_w_uq).reshape(b, s, MLA_HEADS, MLA_NOPE + MLA_ROPE)
    q = jnp.concatenate([q[..., :MLA_NOPE], _rope(q[..., MLA_NOPE:], positions)], axis=-1)
    kv = (_rmsnorm(c_kv, mla_kv_norm_w) @ mla_w_ukv).reshape(b, s, MLA_HEADS, MLA_NOPE + MLA_V)
    k_nope, v_mla = kv[..., :MLA_NOPE], kv[..., MLA_NOPE:]
    k_rope = _rope(k_rope_raw[:, :, None, :], positions)
    k = jnp.concatenate([k_nope, jnp.broadcast_to(k_rope, (b, s, MLA_HEADS, MLA_ROPE))], axis=-1)
    o_mla = _causal_attention(q.transpose(0, 2, 1, 3), k.transpose(0, 2, 1, 3),
                              v_mla.transpose(0, 2, 1, 3), 1.0 / math.sqrt(MLA_NOPE + MLA_ROPE))
    o_mla = o_mla.transpose(0, 2, 1, 3).reshape(b, s, MLA_HEADS * MLA_V) @ w_branch_mla

    rq = _rope(rq.reshape(b, s, RET_HEADS, RET_DK), positions).transpose(0, 2, 1, 3)
    rk = (_rope(rk.reshape(b, s, RET_HEADS, RET_DK), positions) * (RET_DK ** -0.5)).transpose(0, 2, 1, 3)
    rv = rv.reshape(b, s, RET_HEADS, RET_DV).transpose(0, 2, 1, 3)
    y_ret = _chunkwise_retention(rq, rk, rv).transpose(0, 2, 1, 3).reshape(b, s, RET_HEADS * RET_DV)
    o_ret = (jax.nn.silu(rg) * (y_ret * ret_gn_w)) @ w_branch_ret

    merged = jax.nn.sigmoid(gate_mla) * o_mla + jax.nn.sigmoid(gate_ret) * o_ret
    return merged @ w_out


def _fwd_setup_inputs(seed: int = 0) -> dict:
    key = jax.random.key(seed)
    ks = jax.random.split(key, 24)
    f32 = jnp.float32

    def w(k, shape, fan_in):
        return jax.random.normal(k, (DEPTH,) + shape, f32) * (fan_in ** -0.5)

    def gain(k, n):
        return 1.0 + 0.02 * jax.random.normal(k, (DEPTH, n), f32)

    return {
        "x": jax.random.normal(ks[0], (BATCH, SEQ, D_MODEL), f32),
        "positions": jnp.broadcast_to(jnp.arange(SEQ, dtype=jnp.int32), (BATCH, SEQ)),
        "ffn1_pre_w": gain(ks[1], D_MODEL),
        "ffn1_w1": w(ks[2], (D_MODEL, 2 * FF_DIM), D_MODEL),
        "ffn1_w2": w(ks[3], (FF_DIM, D_MODEL), FF_DIM),
        "ffn1_post_w": gain(ks[4], D_MODEL),
        "mix_pre_w": gain(ks[5], D_MODEL),
        "w_in": w(ks[6], (D_MODEL, IN_WIDTH), D_MODEL),
        "mla_q_norm_w": gain(ks[7], MLA_Q_RANK),
        "mla_w_uq": w(ks[8], (MLA_Q_RANK, MLA_HEADS * (MLA_NOPE + MLA_ROPE)), MLA_Q_RANK),
        "mla_kv_norm_w": gain(ks[9], MLA_KV_RANK),
        "mla_w_ukv": w(ks[10], (MLA_KV_RANK, MLA_HEADS * (MLA_NOPE + MLA_V)), MLA_KV_RANK),
        "ret_gn_w": gain(ks[11], RET_HEADS * RET_DV),
        "w_branch_mla": w(ks[12], (MLA_HEADS * MLA_V, D_MODEL), MLA_HEADS * MLA_V),
        "w_branch_ret": w(ks[13], (RET_HEADS * RET_DV, D_MODEL), RET_HEADS * RET_DV),
        "w_out": w(ks[14], (D_MODEL, D_MODEL), D_MODEL),
        "mix_post_w": gain(ks[15], D_MODEL),
        "ffn2_pre_w": gain(ks[16], D_MODEL),
        "ffn2_w1": w(ks[17], (D_MODEL, 2 * FF_DIM), D_MODEL),
        "ffn2_w2": w(ks[18], (FF_DIM, D_MODEL), FF_DIM),
        "ffn2_post_w": gain(ks[19], D_MODEL),
    }


def _fwd_reference(x, positions, ffn1_pre_w, ffn1_w1, ffn1_w2, ffn1_post_w, mix_pre_w, w_in,
              mla_q_norm_w, mla_w_uq, mla_kv_norm_w, mla_w_ukv, ret_gn_w, w_branch_mla,
              w_branch_ret, w_out, mix_post_w, ffn2_pre_w, ffn2_w1, ffn2_w2, ffn2_post_w):
    h = x
    for l in range(DEPTH):
        f = _swiglu(_rmsnorm(h, ffn1_pre_w[l]), ffn1_w1[l], ffn1_w2[l])
        h = h + 0.5 * _rmsnorm(f, ffn1_post_w[l])
        m = _token_mixer(_rmsnorm(h, mix_pre_w[l]), positions, w_in[l], mla_q_norm_w[l], mla_w_uq[l],
                         mla_kv_norm_w[l], mla_w_ukv[l], ret_gn_w[l], w_branch_mla[l],
                         w_branch_ret[l], w_out[l])
        h = h + _rmsnorm(m, mix_post_w[l])
        f = _swiglu(_rmsnorm(h, ffn2_pre_w[l]), ffn2_w1[l], ffn2_w2[l])
        h = h + 0.5 * _rmsnorm(f, ffn2_post_w[l])
    return h


import jax as _jax
import jax.numpy as _jnp

TWIN_FORMAT = 'train_step'
FWD_PARAMS = ['x', 'positions', 'ffn1_pre_w', 'ffn1_w1', 'ffn1_w2', 'ffn1_post_w', 'mix_pre_w', 'w_in', 'mla_q_norm_w', 'mla_w_uq', 'mla_kv_norm_w', 'mla_w_ukv', 'ret_gn_w', 'w_branch_mla', 'w_branch_ret', 'w_out', 'mix_post_w', 'ffn2_pre_w', 'ffn2_w1', 'ffn2_w2', 'ffn2_post_w']
TWIN_WEIGHTS = ['ffn1_pre_w', 'ffn1_w1', 'ffn1_w2', 'ffn1_post_w', 'mix_pre_w', 'w_in', 'mla_q_norm_w', 'mla_w_uq', 'mla_kv_norm_w', 'mla_w_ukv', 'ret_gn_w', 'w_branch_mla', 'w_branch_ret', 'w_out', 'mix_post_w', 'ffn2_pre_w', 'ffn2_w1', 'ffn2_w2', 'ffn2_post_w']
TWIN_DIFF_INPUT = 'x'
TWIN_INPUTS = ['x', 'positions', 'ffn1_pre_w', 'ffn1_w1', 'ffn1_w2', 'ffn1_post_w', 'mix_pre_w', 'w_in', 'mla_q_norm_w', 'mla_w_uq', 'mla_kv_norm_w', 'mla_w_ukv', 'ret_gn_w', 'w_branch_mla', 'w_branch_ret', 'w_out', 'mix_post_w', 'ffn2_pre_w', 'ffn2_w1', 'ffn2_w2', 'ffn2_post_w', 'loss_target', 'm_ffn1_pre_w', 'm_ffn1_w1', 'm_ffn1_w2', 'm_ffn1_post_w', 'm_mix_pre_w', 'm_w_in', 'm_mla_q_norm_w', 'm_mla_w_uq', 'm_mla_kv_norm_w', 'm_mla_w_ukv', 'm_ret_gn_w', 'm_w_branch_mla', 'm_w_branch_ret', 'm_w_out', 'm_mix_post_w', 'm_ffn2_pre_w', 'm_ffn2_w1', 'm_ffn2_w2', 'm_ffn2_post_w', 'v_ffn1_pre_w', 'v_ffn1_w1', 'v_ffn1_w2', 'v_ffn1_post_w', 'v_mix_pre_w', 'v_w_in', 'v_mla_q_norm_w', 'v_mla_w_uq', 'v_mla_kv_norm_w', 'v_mla_w_ukv', 'v_ret_gn_w', 'v_w_branch_mla', 'v_w_branch_ret', 'v_w_out', 'v_mix_post_w', 'v_ffn2_pre_w', 'v_ffn2_w1', 'v_ffn2_w2', 'v_ffn2_post_w']
TWIN_OUTPUTS = ['loss', 'grad_x', 'grad_ffn1_pre_w', 'grad_ffn1_w1', 'grad_ffn1_w2', 'grad_ffn1_post_w', 'grad_mix_pre_w', 'grad_w_in', 'grad_mla_q_norm_w', 'grad_mla_w_uq', 'grad_mla_kv_norm_w', 'grad_mla_w_ukv', 'grad_ret_gn_w', 'grad_w_branch_mla', 'grad_w_branch_ret', 'grad_w_out', 'grad_mix_post_w', 'grad_ffn2_pre_w', 'grad_ffn2_w1', 'grad_ffn2_w2', 'grad_ffn2_post_w', 'delta_ffn1_pre_w', 'delta_ffn1_w1', 'delta_ffn1_w2', 'delta_ffn1_post_w', 'delta_mix_pre_w', 'delta_w_in', 'delta_mla_q_norm_w', 'delta_mla_w_uq', 'delta_mla_kv_norm_w', 'delta_mla_w_ukv', 'delta_ret_gn_w', 'delta_w_branch_mla', 'delta_w_branch_ret', 'delta_w_out', 'delta_mix_post_w', 'delta_ffn2_pre_w', 'delta_ffn2_w1', 'delta_ffn2_w2', 'delta_ffn2_post_w', 'new_m_ffn1_pre_w', 'new_m_ffn1_w1', 'new_m_ffn1_w2', 'new_m_ffn1_post_w', 'new_m_mix_pre_w', 'new_m_w_in', 'new_m_mla_q_norm_w', 'new_m_mla_w_uq', 'new_m_mla_kv_norm_w', 'new_m_mla_w_ukv', 'new_m_ret_gn_w', 'new_m_w_branch_mla', 'new_m_w_branch_ret', 'new_m_w_out', 'new_m_mix_post_w', 'new_m_ffn2_pre_w', 'new_m_ffn2_w1', 'new_m_ffn2_w2', 'new_m_ffn2_post_w', 'new_v_ffn1_pre_w', 'new_v_ffn1_w1', 'new_v_ffn1_w2', 'new_v_ffn1_post_w', 'new_v_mix_pre_w', 'new_v_w_in', 'new_v_mla_q_norm_w', 'new_v_mla_w_uq', 'new_v_mla_kv_norm_w', 'new_v_mla_w_ukv', 'new_v_ret_gn_w', 'new_v_w_branch_mla', 'new_v_w_branch_ret', 'new_v_w_out', 'new_v_mix_post_w', 'new_v_ffn2_pre_w', 'new_v_ffn2_w1', 'new_v_ffn2_w2', 'new_v_ffn2_post_w']
TWIN_LEAF_KINDS = {'loss': 'loss', 'grad_x': 'grad_x', 'grad_ffn1_pre_w': 'grad_w', 'grad_ffn1_w1': 'grad_w', 'grad_ffn1_w2': 'grad_w', 'grad_ffn1_post_w': 'grad_w', 'grad_mix_pre_w': 'grad_w', 'grad_w_in': 'grad_w', 'grad_mla_q_norm_w': 'grad_w', 'grad_mla_w_uq': 'grad_w', 'grad_mla_kv_norm_w': 'grad_w', 'grad_mla_w_ukv': 'grad_w', 'grad_ret_gn_w': 'grad_w', 'grad_w_branch_mla': 'grad_w', 'grad_w_branch_ret': 'grad_w', 'grad_w_out': 'grad_w', 'grad_mix_post_w': 'grad_w', 'grad_ffn2_pre_w': 'grad_w', 'grad_ffn2_w1': 'grad_w', 'grad_ffn2_w2': 'grad_w', 'grad_ffn2_post_w': 'grad_w', 'delta_ffn1_pre_w': 'delta_w', 'delta_ffn1_w1': 'delta_w', 'delta_ffn1_w2': 'delta_w', 'delta_ffn1_post_w': 'delta_w', 'delta_mix_pre_w': 'delta_w', 'delta_w_in': 'delta_w', 'delta_mla_q_norm_w': 'delta_w', 'delta_mla_w_uq': 'delta_w', 'delta_mla_kv_norm_w': 'delta_w', 'delta_mla_w_ukv': 'delta_w', 'delta_ret_gn_w': 'delta_w', 'delta_w_branch_mla': 'delta_w', 'delta_w_branch_ret': 'delta_w', 'delta_w_out': 'delta_w', 'delta_mix_post_w': 'delta_w', 'delta_ffn2_pre_w': 'delta_w', 'delta_ffn2_w1': 'delta_w', 'delta_ffn2_w2': 'delta_w', 'delta_ffn2_post_w': 'delta_w', 'new_m_ffn1_pre_w': 'new_m', 'new_m_ffn1_w1': 'new_m', 'new_m_ffn1_w2': 'new_m', 'new_m_ffn1_post_w': 'new_m', 'new_m_mix_pre_w': 'new_m', 'new_m_w_in': 'new_m', 'new_m_mla_q_norm_w': 'new_m', 'new_m_mla_w_uq': 'new_m', 'new_m_mla_kv_norm_w': 'new_m', 'new_m_mla_w_ukv': 'new_m', 'new_m_ret_gn_w': 'new_m', 'new_m_w_branch_mla': 'new_m', 'new_m_w_branch_ret': 'new_m', 'new_m_w_out': 'new_m', 'new_m_mix_post_w': 'new_m', 'new_m_ffn2_pre_w': 'new_m', 'new_m_ffn2_w1': 'new_m', 'new_m_ffn2_w2': 'new_m', 'new_m_ffn2_post_w': 'new_m', 'new_v_ffn1_pre_w': 'new_v', 'new_v_ffn1_w1': 'new_v', 'new_v_ffn1_w2': 'new_v', 'new_v_ffn1_post_w': 'new_v', 'new_v_mix_pre_w': 'new_v', 'new_v_w_in': 'new_v', 'new_v_mla_q_norm_w': 'new_v', 'new_v_mla_w_uq': 'new_v', 'new_v_mla_kv_norm_w': 'new_v', 'new_v_mla_w_ukv': 'new_v', 'new_v_ret_gn_w': 'new_v', 'new_v_w_branch_mla': 'new_v', 'new_v_w_branch_ret': 'new_v', 'new_v_w_out': 'new_v', 'new_v_mix_post_w': 'new_v', 'new_v_ffn2_pre_w': 'new_v', 'new_v_ffn2_w1': 'new_v', 'new_v_ffn2_w2': 'new_v', 'new_v_ffn2_post_w': 'new_v'}


def _forward(args):
    return _fwd_reference(*[args[k] for k in FWD_PARAMS])


def _output_shape():
    out = _jax.eval_shape(lambda: _forward(_fwd_setup_inputs(0)))
    return out.shape, out.dtype

N_MICROBATCH = 1
ADAM_LR = 0.001
ADAM_B1 = 0.9
ADAM_B2 = 0.999
ADAM_EPS = 1e-08
ADAM_WD = 0.01
ADAM_STEP = 10
PER_EXAMPLE_BATCH_AXIS = {'x': 0, 'positions': 0, 'loss_target': 0}
SHARED_INPUTS = []
_WEIGHT_DTYPES = {'ffn1_pre_w': _jnp.float32, 'ffn1_w1': _jnp.float32, 'ffn1_w2': _jnp.float32, 'ffn1_post_w': _jnp.float32, 'mix_pre_w': _jnp.float32, 'w_in': _jnp.float32, 'mla_q_norm_w': _jnp.float32, 'mla_w_uq': _jnp.float32, 'mla_kv_norm_w': _jnp.float32, 'mla_w_ukv': _jnp.float32, 'ret_gn_w': _jnp.float32, 'w_branch_mla': _jnp.float32, 'w_branch_ret': _jnp.float32, 'w_out': _jnp.float32, 'mix_post_w': _jnp.float32, 'ffn2_pre_w': _jnp.float32, 'ffn2_w1': _jnp.float32, 'ffn2_w2': _jnp.float32, 'ffn2_post_w': _jnp.float32}
MOMENT_SCALE = {'ffn1_pre_w': 4.965619e-01, 'ffn1_w1': 2.130101e-01, 'ffn1_w2': 3.541344e-01, 'ffn1_post_w': 7.763847e+00, 'mix_pre_w': 6.916783e-01, 'w_in': 3.340208e-01, 'mla_q_norm_w': 1.486104e-01, 'mla_w_uq': 1.044906e-01, 'mla_kv_norm_w': 2.687829e-01, 'mla_w_ukv': 1.301048e-01, 'ret_gn_w': 5.259552e-01, 'w_branch_mla': 1.053740e-01, 'w_branch_ret': 3.503101e-01, 'w_out': 3.568111e-01, 'mix_post_w': 3.201918e+01, 'ffn2_pre_w': 2.957076e-01, 'ffn2_w1': 1.238595e-01, 'ffn2_w2': 2.429999e-01, 'ffn2_post_w': 7.925083e+00}


def _to_microbatches(a, axis):
    t = _jnp.moveaxis(a, axis, 0)
    t = t.reshape((N_MICROBATCH, t.shape[0] // N_MICROBATCH) + t.shape[1:])
    return _jnp.moveaxis(t, 1, axis + 1)


def setup_inputs(seed: int = 0) -> dict:
    inp = _fwd_setup_inputs(seed)
    key = _jax.random.fold_in(_jax.random.key(seed), 7919)
    shape, _ = _output_shape()
    out = dict(inp)
    out["loss_target"] = _jax.random.normal(_jax.random.fold_in(key, 0), shape, _jnp.float32)
    for i, name in enumerate(TWIN_WEIGHTS):
        w = inp[name].astype(_jnp.float32)
        if MOMENT_SCALE is None:
            s = _jnp.sqrt(_jnp.mean(_jnp.square(w)) + 1e-30)
        else:
            s = MOMENT_SCALE[name]
        km, kv = _jax.random.split(_jax.random.fold_in(key, i + 1))
        out[name] = w
        out["m_" + name] = s * _jax.random.normal(km, w.shape, _jnp.float32)
        out["v_" + name] = (s * s) * _jax.random.uniform(kv, w.shape, _jnp.float32, 0.5, 1.5)
    if N_MICROBATCH > 1:
        for name, axis in PER_EXAMPLE_BATCH_AXIS.items():
            out[name] = _to_microbatches(out[name], axis)
    return {'x': out['x'], 'positions': out['positions'], 'ffn1_pre_w': out['ffn1_pre_w'], 'ffn1_w1': out['ffn1_w1'], 'ffn1_w2': out['ffn1_w2'], 'ffn1_post_w': out['ffn1_post_w'], 'mix_pre_w': out['mix_pre_w'], 'w_in': out['w_in'], 'mla_q_norm_w': out['mla_q_norm_w'], 'mla_w_uq': out['mla_w_uq'], 'mla_kv_norm_w': out['mla_kv_norm_w'], 'mla_w_ukv': out['mla_w_ukv'], 'ret_gn_w': out['ret_gn_w'], 'w_branch_mla': out['w_branch_mla'], 'w_branch_ret': out['w_branch_ret'], 'w_out': out['w_out'], 'mix_post_w': out['mix_post_w'], 'ffn2_pre_w': out['ffn2_pre_w'], 'ffn2_w1': out['ffn2_w1'], 'ffn2_w2': out['ffn2_w2'], 'ffn2_post_w': out['ffn2_post_w'], 'loss_target': out['loss_target'], 'm_ffn1_pre_w': out['m_ffn1_pre_w'], 'm_ffn1_w1': out['m_ffn1_w1'], 'm_ffn1_w2': out['m_ffn1_w2'], 'm_ffn1_post_w': out['m_ffn1_post_w'], 'm_mix_pre_w': out['m_mix_pre_w'], 'm_w_in': out['m_w_in'], 'm_mla_q_norm_w': out['m_mla_q_norm_w'], 'm_mla_w_uq': out['m_mla_w_uq'], 'm_mla_kv_norm_w': out['m_mla_kv_norm_w'], 'm_mla_w_ukv': out['m_mla_w_ukv'], 'm_ret_gn_w': out['m_ret_gn_w'], 'm_w_branch_mla': out['m_w_branch_mla'], 'm_w_branch_ret': out['m_w_branch_ret'], 'm_w_out': out['m_w_out'], 'm_mix_post_w': out['m_mix_post_w'], 'm_ffn2_pre_w': out['m_ffn2_pre_w'], 'm_ffn2_w1': out['m_ffn2_w1'], 'm_ffn2_w2': out['m_ffn2_w2'], 'm_ffn2_post_w': out['m_ffn2_post_w'], 'v_ffn1_pre_w': out['v_ffn1_pre_w'], 'v_ffn1_w1': out['v_ffn1_w1'], 'v_ffn1_w2': out['v_ffn1_w2'], 'v_ffn1_post_w': out['v_ffn1_post_w'], 'v_mix_pre_w': out['v_mix_pre_w'], 'v_w_in': out['v_w_in'], 'v_mla_q_norm_w': out['v_mla_q_norm_w'], 'v_mla_w_uq': out['v_mla_w_uq'], 'v_mla_kv_norm_w': out['v_mla_kv_norm_w'], 'v_mla_w_ukv': out['v_mla_w_ukv'], 'v_ret_gn_w': out['v_ret_gn_w'], 'v_w_branch_mla': out['v_w_branch_mla'], 'v_w_branch_ret': out['v_w_branch_ret'], 'v_w_out': out['v_w_out'], 'v_mix_post_w': out['v_mix_post_w'], 'v_ffn2_pre_w': out['v_ffn2_pre_w'], 'v_ffn2_w1': out['v_ffn2_w1'], 'v_ffn2_w2': out['v_ffn2_w2'], 'v_ffn2_post_w': out['v_ffn2_post_w']}


def _loss(weights, diff, rest, loss_target):
    with _jax.named_scope("forward"):
        args = {**rest, TWIN_DIFF_INPUT: diff, **{k: w.astype(_WEIGHT_DTYPES[k]) for k, w in weights.items()}}
        y = _forward(args)
    with _jax.named_scope("loss_head"):
        err = _jnp.square(y.astype(_jnp.float32) - loss_target)
        return 0.5 * _jnp.sum(_jnp.mean(err, axis=-1)) if err.ndim else 0.5 * err


def _adamw(w, g, m, v):
    m = ADAM_B1 * m + (1.0 - ADAM_B1) * g
    v = ADAM_B2 * v + (1.0 - ADAM_B2) * _jnp.square(g)
    m_hat = m / (1.0 - ADAM_B1 ** ADAM_STEP)
    v_hat = v / (1.0 - ADAM_B2 ** ADAM_STEP)
    delta = -ADAM_LR * (m_hat / (_jnp.sqrt(v_hat) + ADAM_EPS) + ADAM_WD * w)
    return delta, m, v


def reference(x, positions, ffn1_pre_w, ffn1_w1, ffn1_w2, ffn1_post_w, mix_pre_w, w_in, mla_q_norm_w, mla_w_uq, mla_kv_norm_w, mla_w_ukv, ret_gn_w, w_branch_mla, w_branch_ret, w_out, mix_post_w, ffn2_pre_w, ffn2_w1, ffn2_w2, ffn2_post_w, loss_target, m_ffn1_pre_w, m_ffn1_w1, m_ffn1_w2, m_ffn1_post_w, m_mix_pre_w, m_w_in, m_mla_q_norm_w, m_mla_w_uq, m_mla_kv_norm_w, m_mla_w_ukv, m_ret_gn_w, m_w_branch_mla, m_w_branch_ret, m_w_out, m_mix_post_w, m_ffn2_pre_w, m_ffn2_w1, m_ffn2_w2, m_ffn2_post_w, v_ffn1_pre_w, v_ffn1_w1, v_ffn1_w2, v_ffn1_post_w, v_mix_pre_w, v_w_in, v_mla_q_norm_w, v_mla_w_uq, v_mla_kv_norm_w, v_mla_w_ukv, v_ret_gn_w, v_w_branch_mla, v_w_branch_ret, v_w_out, v_mix_post_w, v_ffn2_pre_w, v_ffn2_w1, v_ffn2_w2, v_ffn2_post_w):
    given = dict(x=x, positions=positions, ffn1_pre_w=ffn1_pre_w, ffn1_w1=ffn1_w1, ffn1_w2=ffn1_w2, ffn1_post_w=ffn1_post_w, mix_pre_w=mix_pre_w, w_in=w_in, mla_q_norm_w=mla_q_norm_w, mla_w_uq=mla_w_uq, mla_kv_norm_w=mla_kv_norm_w, mla_w_ukv=mla_w_ukv, ret_gn_w=ret_gn_w, w_branch_mla=w_branch_mla, w_branch_ret=w_branch_ret, w_out=w_out, mix_post_w=mix_post_w, ffn2_pre_w=ffn2_pre_w, ffn2_w1=ffn2_w1, ffn2_w2=ffn2_w2, ffn2_post_w=ffn2_post_w, loss_target=loss_target, m_ffn1_pre_w=m_ffn1_pre_w, m_ffn1_w1=m_ffn1_w1, m_ffn1_w2=m_ffn1_w2, m_ffn1_post_w=m_ffn1_post_w, m_mix_pre_w=m_mix_pre_w, m_w_in=m_w_in, m_mla_q_norm_w=m_mla_q_norm_w, m_mla_w_uq=m_mla_w_uq, m_mla_kv_norm_w=m_mla_kv_norm_w, m_mla_w_ukv=m_mla_w_ukv, m_ret_gn_w=m_ret_gn_w, m_w_branch_mla=m_w_branch_mla, m_w_branch_ret=m_w_branch_ret, m_w_out=m_w_out, m_mix_post_w=m_mix_post_w, m_ffn2_pre_w=m_ffn2_pre_w, m_ffn2_w1=m_ffn2_w1, m_ffn2_w2=m_ffn2_w2, m_ffn2_post_w=m_ffn2_post_w, v_ffn1_pre_w=v_ffn1_pre_w, v_ffn1_w1=v_ffn1_w1, v_ffn1_w2=v_ffn1_w2, v_ffn1_post_w=v_ffn1_post_w, v_mix_pre_w=v_mix_pre_w, v_w_in=v_w_in, v_mla_q_norm_w=v_mla_q_norm_w, v_mla_w_uq=v_mla_w_uq, v_mla_kv_norm_w=v_mla_kv_norm_w, v_mla_w_ukv=v_mla_w_ukv, v_ret_gn_w=v_ret_gn_w, v_w_branch_mla=v_w_branch_mla, v_w_branch_ret=v_w_branch_ret, v_w_out=v_w_out, v_mix_post_w=v_mix_post_w, v_ffn2_pre_w=v_ffn2_pre_w, v_ffn2_w1=v_ffn2_w1, v_ffn2_w2=v_ffn2_w2, v_ffn2_post_w=v_ffn2_post_w)
    weights = {n: given[n] for n in TWIN_WEIGHTS}
    shared = {n: given[n] for n in SHARED_INPUTS}
    per_example = {n: given[n] for n in ['x', 'positions']}
    grad_fn = _jax.value_and_grad(_loss, argnums=(0, 1))

    def one_microbatch(ex, loss_target):
        ex = dict(ex)
        diff = ex.pop(TWIN_DIFF_INPUT)
        return grad_fn(weights, diff, {**shared, **ex}, loss_target)

    if N_MICROBATCH == 1:
        loss, (grad_w, grad_x) = one_microbatch(per_example, given["loss_target"])
    else:
        def body(carry, xs):
            loss_sum, grad_sum = carry
            l_k, (gw_k, gx_k) = one_microbatch(xs[0], xs[1])
            with _jax.named_scope("update"):
                return (loss_sum + l_k, _jax.tree.map(_jnp.add, grad_sum, gw_k)), gx_k

        init = (_jnp.zeros((), _jnp.float32), _jax.tree.map(_jnp.zeros_like, weights))
        (loss, grad_w), grad_x = _jax.lax.scan(body, init, (per_example, given["loss_target"]))
    with _jax.named_scope("update"):
        delta_w, new_m, new_v = {}, {}, {}
        for n in TWIN_WEIGHTS:
            delta_w[n], new_m[n], new_v[n] = _adamw(weights[n], grad_w[n], given["m_" + n], given["v_" + n])
    return (loss, grad_x, *[grad_w[n] for n in TWIN_WEIGHTS], *[delta_w[n] for n in TWIN_WEIGHTS],
            *[new_m[n] for n in TWIN_WEIGHTS], *[new_v[n] for n in TWIN_WEIGHTS])
```

```python
import math

import numpy as np
import jax
import jax.numpy as jnp
from jax import lax
from jax.experimental import pallas as pl
from jax.experimental.pallas import tpu as pltpu

F32, BF16 = jnp.float32, jnp.bfloat16

MLA_HEADS, MLA_NOPE, MLA_ROPE, MLA_V = 8, 64, 32, 64
MLA_Q_RANK, MLA_KV_RANK = 384, 256
RET_HEADS, RET_DK, RET_DV = 4, 64, 128
ROPE_BASE, NORM_EPS, GN_EPS = 10000.0, 1e-6, 1e-6
ADAM_LR, ADAM_B1, ADAM_B2, ADAM_EPS, ADAM_WD, ADAM_STEP = 0.001, 0.9, 0.999, 1e-08, 0.01, 10
ATTN_SCALE = 1.0 / math.sqrt(MLA_NOPE + MLA_ROPE)

N_DEV = 8
LANES = 128
HP = LANES
QW = MLA_HEADS * HP
RW = RET_HEADS * HP
AW = 1024
PROJ_FIXED = 4 * RW + AW
NEG = -1e30

TOKEN_TILE = 512
ATTN_TILE = 512
RET_TILE = 256
FF_TILE_CAP = 512
GRAD_TILE_CAP = 1408
MERGE_TILE = 256
VMEM_LIMIT = 56 * 1024 * 1024


def _tile(n, cap, mult=LANES):
    if n <= cap:
        return n
    best = None
    for t in range(mult, cap + 1, mult):
        if n % t == 0:
            best = t
    assert best is not None, (n, cap, mult)
    return best


def _params(sem):
    return pltpu.CompilerParams(dimension_semantics=sem, vmem_limit_bytes=VMEM_LIMIT)


def _dot(a, b):
    return lax.dot_general(a, b, (((1,), (0,)), ((), ())), preferred_element_type=F32)


def _dot_nt(a, b):
    return lax.dot_general(a, b, (((1,), (1,)), ((), ())), preferred_element_type=F32)


def _dot_tn(a, b):
    return lax.dot_general(a, b, (((0,), (0,)), ((), ())), preferred_element_type=F32)


def _sigmoid(x):
    return 1.0 / (1.0 + jnp.exp(-x))


def _rms_fwd(x, w):
    r = lax.rsqrt(jnp.mean(x * x, axis=-1, keepdims=True) + NORM_EPS)
    return x * r * w


def _rms_bwd(x, w, dy):
    r = lax.rsqrt(jnp.mean(x * x, axis=-1, keepdims=True) + NORM_EPS)
    xh = x * r
    g = dy * w
    dx = r * (g - xh * jnp.mean(g * xh, axis=-1, keepdims=True))
    return dx, jnp.sum(dy * xh, axis=0, keepdims=True)


def _rope_table(first, half):
    inv = (np.float32(ROPE_BASE) ** (-(np.arange(half, dtype=np.float32) / np.float32(half)))).astype(np.float32)
    tab = np.zeros((8, LANES), np.float32)
    tab[0, first:first + half] = inv
    tab[0, first + half:first + 2 * half] = inv
    tab[1, first:first + half] = -1.0
    tab[2, first + half:first + 2 * half] = 1.0
    return jnp.asarray(tab)


def _rope_cs(pos, tab_ref):
    ang = pos * tab_ref[0:1, :]
    s = jnp.sin(ang)
    return jnp.cos(ang), s * tab_ref[1:2, :], s * tab_ref[2:3, :]


def _rope(x, cs, half, inverse=False):
    c, s1, s2 = cs
    a = pltpu.roll(x, LANES - half, 1) * s1 + pltpu.roll(x, half, 1) * s2
    return x * c - a if inverse else x * c + a


def _ffn_fwd(h, pre_w, w1, w2, post_w, target, *, name):
    T, D = h.shape
    FF = w2.shape[0]
    tT, ck = min(TOKEN_TILE, T), _tile(FF, FF_TILE_CAP)
    nT, nk = T // tT, FF // ck
    with_loss = target is not None

    def body(*refs):
        if with_loss:
            (h_ref, pre_ref, w1g_ref, w1u_ref, w2_ref, post_ref, tgt_ref,
             u_ref, f_ref, ho_ref, dy_ref, loss_ref, a_s, acc) = refs
        else:
            (h_ref, pre_ref, w1g_ref, w1u_ref, w2_ref, post_ref,
             u_ref, f_ref, ho_ref, a_s, acc) = refs
        k = pl.program_id(1)

        @pl.when(k == 0)
        def _():
            a_s[...] = _rms_fwd(h_ref[...], pre_ref[...]).astype(BF16)
            acc[...] = jnp.zeros_like(acc)

        a = a_s[...]
        ug = _dot(a, w1g_ref[...])
        uu = _dot(a, w1u_ref[...])
        u_ref[0] = ug.astype(BF16)
        u_ref[1] = uu.astype(BF16)
        acc[...] += _dot((ug * _sigmoid(ug) * uu).astype(BF16), w2_ref[...])

        @pl.when(k == nk - 1)
        def _():
            f = acc[...]
            f_ref[...] = f
            ho = h_ref[...] + 0.5 * _rms_fwd(f, post_ref[...])
            ho_ref[...] = ho
            if with_loss:
                e = ho - tgt_ref[...]
                dy_ref[...] = e * (1.0 / D)
                loss_ref[...] = jnp.full(loss_ref.shape, (0.5 / D) * jnp.sum(e * e), F32)

    row = pl.BlockSpec((tT, D), lambda i, k: (i, 0))
    vec = pl.BlockSpec((1, D), lambda i, k: (0, 0))
    in_specs = [row, vec,
                pl.BlockSpec((D, ck), lambda i, k: (0, k)),
                pl.BlockSpec((D, ck), lambda i, k: (0, nk + k)),
                pl.BlockSpec((ck, D), lambda i, k: (k, 0)),
                vec]
    out_shape = [jax.ShapeDtypeStruct((2, T, FF), BF16),
                 jax.ShapeDtypeStruct((T, D), F32),
                 jax.ShapeDtypeStruct((T, D), F32)]
    out_specs = [pl.BlockSpec((2, tT, ck), lambda i, k: (0, i, k)), row, row]
    args = [h, pre_w, w1, w1, w2, post_w]
    if with_loss:
        in_specs.append(row)
        args.append(target)
        out_shape += [jax.ShapeDtypeStruct((T, D), F32), jax.ShapeDtypeStruct((nT * 8, LANES), F32)]
        out_specs += [row, pl.BlockSpec((8, LANES), lambda i, k: (i, 0))]
    return pl.pallas_call(
        body, name=name, grid=(nT, nk), in_specs=in_specs, out_specs=out_specs, out_shape=out_shape,
        scratch_shapes=[pltpu.VMEM((tT, D), BF16), pltpu.VMEM((tT, D), F32)],
        compiler_params=_params(("parallel", "arbitrary")),
    )(*args)


def _ffn_bwd(dho, f, post_w, h, pre_w, u, w2, w1, *, name):
    T, D = h.shape
    FF = w2.shape[0]
    tT, ck = min(TOKEN_TILE, T), _tile(FF, FF_TILE_CAP)
    nT, nk = T // tT, FF // ck

    def body(dho_ref, f_ref, post_ref, h_ref, pre_ref, u_ref, w2_ref, w1g_ref, w1u_ref,
             g_ref, du_ref, df_ref, a_ref, dh_ref, gpost_ref, gpre_ref, df_s, da_acc):
        i, k = pl.program_id(0), pl.program_id(1)

        @pl.when(jnp.logical_and(i == 0, k == 0))
        def _():
            gpost_ref[...] = jnp.zeros_like(gpost_ref)
            gpre_ref[...] = jnp.zeros_like(gpre_ref)

        @pl.when(k == 0)
        def _():
            dx, dw = _rms_bwd(f_ref[...], post_ref[...], 0.5 * dho_ref[...])
            dfb = dx.astype(BF16)
            df_s[...] = dfb
            df_ref[...] = dfb
            gpost_ref[...] += dw
            a_ref[...] = _rms_fwd(h_ref[...], pre_ref[...]).astype(BF16)
            da_acc[...] = jnp.zeros_like(da_acc)

        dg = _dot_nt(df_s[...], w2_ref[...])
        ug = u_ref[0].astype(F32)
        uu = u_ref[1].astype(F32)
        sg = _sigmoid(ug)
        sl = ug * sg
        g_ref[...] = (sl * uu).astype(BF16)
        dug = (dg * uu * (sg * (1.0 + ug * (1.0 - sg)))).astype(BF16)
        duu = (dg * sl).astype(BF16)
        du_ref[0] = dug
        du_ref[1] = duu
        da_acc[...] += _dot_nt(dug, w1g_ref[...]) + _dot_nt(duu, w1u_ref[...])

        @pl.when(k == nk - 1)
        def _():
            dx, dw = _rms_bwd(h_ref[...], pre_ref[...], da_acc[...])
            dh_ref[...] = dho_ref[...] + dx
            gpre_ref[...] += dw

    row = pl.BlockSpec((tT, D), lambda i, k: (i, 0))
    vec = pl.BlockSpec((1, D), lambda i, k: (0, 0))
    return pl.pallas_call(
        body, name=name, grid=(nT, nk),
        in_specs=[row, row, vec, row, vec,
                  pl.BlockSpec((2, tT, ck), lambda i, k: (0, i, k)),
                  pl.BlockSpec((ck, D), lambda i, k: (k, 0)),
                  pl.BlockSpec((D, ck), lambda i, k: (0, k)),
                  pl.BlockSpec((D, ck), lambda i, k: (0, nk + k))],
        out_specs=[pl.BlockSpec((tT, ck), lambda i, k: (i, k)),
                   pl.BlockSpec((2, tT, ck), lambda i, k: (0, i, k)),
                   row, row, row, vec, vec],
        out_shape=[jax.ShapeDtypeStruct((T, FF), BF16),
                   jax.ShapeDtypeStruct((2, T, FF), BF16),
                   jax.ShapeDtypeStruct((T, D), BF16),
                   jax.ShapeDtypeStruct((T, D), BF16),
                   jax.ShapeDtypeStruct((T, D), F32),
                   jax.ShapeDtypeStruct((1, D), F32),
                   jax.ShapeDtypeStruct((1, D), F32)],
        scratch_shapes=[pltpu.VMEM((tT, D), BF16), pltpu.VMEM((tT, D), F32)],
        compiler_params=_params(("arbitrary", "arbitrary")),
    )(dho, f, post_w, h, pre_w, u, w2, w1, w1)


def _matmul_tn(x, dy, *, name):
    T, K = x.shape
    P, _, N = dy.shape
    tT, tK, tN = min(TOKEN_TILE, T), _tile(K, GRAD_TILE_CAP), _tile(N, GRAD_TILE_CAP)
    nt = T // tT

    def body(x_ref, dy_ref, o_ref):
        t = pl.program_id(3)

        @pl.when(t == 0)
        def _():
            o_ref[...] = jnp.zeros_like(o_ref)

        o_ref[...] += _dot_tn(x_ref[...], dy_ref[...])

    return pl.pallas_call(
        body, name=name, grid=(P, K // tK, N // tN, nt),
        in_specs=[pl.BlockSpec((tT, tK), lambda p, a, b, t: (t, a)),
                  pl.BlockSpec((None, tT, tN), lambda p, a, b, t: (p, t, b))],
        out_specs=pl.BlockSpec((None, tK, tN), lambda p, a, b, t: (p, a, b)),
        out_shape=jax.ShapeDtypeStruct((P, K, N), F32),
        compiler_params=_params(("parallel", "parallel", "parallel", "arbitrary")),
    )(x, dy)


def _rms_matmul(h, wn, w, *, name):
    T, D = h.shape
    N = w.shape[1]
    tT, tN = min(TOKEN_TILE, T), _tile(N, 1024)

    def body(h_ref, wn_ref, w_ref, y_ref, a_ref):
        @pl.when(pl.program_id(1) == 0)
        def _():
            a_ref[...] = _rms_fwd(h_ref[...], wn_ref[...]).astype(BF16)

        y_ref[...] = _dot(a_ref[...], w_ref[...]).astype(BF16)

    return pl.pallas_call(
        body, name=name, grid=(T // tT, N // tN),
        in_specs=[pl.BlockSpec((tT, D), lambda i, j: (i, 0)),
                  pl.BlockSpec((1, D), lambda i, j: (0, 0)),
                  pl.BlockSpec((D, tN), lambda i, j: (0, j))],
        out_specs=[pl.BlockSpec((tT, tN), lambda i, j: (i, j)),
                   pl.BlockSpec((tT, D), lambda i, j: (i, 0))],
        out_shape=[jax.ShapeDtypeStruct((T, N), BF16), jax.ShapeDtypeStruct((T, D), BF16)],
        compiler_params=_params(("parallel", "arbitrary")),
    )(h, wn, w)


def _proj_bwd(dproj, w, h, wn, dres, *, name):
    T, D = h.shape
    N = w.shape[1]
    tT, tN = min(TOKEN_TILE, T), _tile(N, 1024)
    nn = N // tN

    def body(dp_ref, w_ref, h_ref, wn_ref, dres_ref, dh_ref, gw_ref, acc):
        i, j = pl.program_id(0), pl.program_id(1)

        @pl.when(jnp.logical_and(i == 0, j == 0))
        def _():
            gw_ref[...] = jnp.zeros_like(gw_ref)

        @pl.when(j == 0)
        def _():
            acc[...] = jnp.zeros_like(acc)

        acc[...] += _dot_nt(dp_ref[...], w_ref[...])

        @pl.when(j == nn - 1)
        def _():
            dx, dw = _rms_bwd(h_ref[...], wn_ref[...], acc[...])
            dh_ref[...] = dres_ref[...] + dx
            gw_ref[...] += dw

    row = pl.BlockSpec((tT, D), lambda i, j: (i, 0))
    vec = pl.BlockSpec((1, D), lambda i, j: (0, 0))
    return pl.pallas_call(
        body, name=name, grid=(T // tT, nn),
        in_specs=[pl.BlockSpec((tT, tN), lambda i, j: (i, j)),
                  pl.BlockSpec((D, tN), lambda i, j: (0, j)), row, vec, row],
        out_specs=[row, vec],
        out_shape=[jax.ShapeDtypeStruct((T, D), F32), jax.ShapeDtypeStruct((1, D), F32)],
        scratch_shapes=[pltpu.VMEM((tT, D), F32)],
        compiler_params=_params(("arbitrary", "arbitrary")),
    )(dproj, w, h, wn, dres)


def _mla_prep_fwd(proj, pos, qn_w, kvn_w, w_uq, w_kv, tab, *, name):
    T = proj.shape[0]
    tT = min(TOKEN_TILE, T)
    a_blk = PROJ_FIXED // AW - 1

    def body(a_ref, pos_ref, qnw_ref, kvnw_ref, wuq_ref, wkv_ref, tab_ref,
             q_ref, k_ref, v_ref, qn_ref, kvn_ref):
        cq = a_ref[:, 0:MLA_Q_RANK].astype(F32)
        ckv = a_ref[:, MLA_Q_RANK:MLA_Q_RANK + MLA_KV_RANK].astype(F32)
        kr = a_ref[:, 640:768].astype(F32)
        qn = _rms_fwd(cq, qnw_ref[...]).astype(BF16)
        kvn = _rms_fwd(ckv, kvnw_ref[...]).astype(BF16)
        qn_ref[...] = qn
        kvn_ref[...] = kvn
        cs = _rope_cs(pos_ref[...], tab_ref)
        q = _dot(qn, wuq_ref[...])
        kv = _dot(kvn, wkv_ref[...])
        krr = _rope(kr, cs, MLA_ROPE // 2)
        for hd in range(MLA_HEADS):
            sl = slice(hd * HP, (hd + 1) * HP)
            q_ref[:, sl] = (_rope(q[:, sl], cs, MLA_ROPE // 2) * ATTN_SCALE).astype(BF16)
            k_ref[:, sl] = (kv[:, sl] + krr).astype(BF16)
        v_ref[...] = kv[:, QW:].astype(BF16)

    def full(r, c):
        return pl.BlockSpec((r, c), lambda i: (0, 0))

    def rows(c):
        return pl.BlockSpec((tT, c), lambda i: (i, 0))

    return pl.pallas_call(
        body, name=name, grid=(T // tT,),
        in_specs=[pl.BlockSpec((tT, AW), lambda i: (i, a_blk)), rows(1),
                  full(1, MLA_Q_RANK), full(1, MLA_KV_RANK),
                  full(MLA_Q_RANK, QW), full(MLA_KV_RANK, 2 * QW), full(8, LANES)],
        out_specs=[rows(QW), rows(QW), rows(QW), rows(MLA_Q_RANK), rows(MLA_KV_RANK)],
        out_shape=[jax.ShapeDtypeStruct((T, QW), BF16)] * 3
        + [jax.ShapeDtypeStruct((T, MLA_Q_RANK), BF16), jax.ShapeDtypeStruct((T, MLA_KV_RANK), BF16)],
        compiler_params=_params(("parallel",)),
    )(proj, pos, qn_w, kvn_w, w_uq, w_kv, tab)


def _mla_prep_bwd(dq, dk, dv, proj, pos, qn_w, kvn_w, w_uq, w_kv, tab, *, name):
    T = proj.shape[0]
    tT = min(TOKEN_TILE, T)
    a_blk = PROJ_FIXED // AW - 1

    def body(dq_ref, dk_ref, dv_ref, a_ref, pos_ref, qnw_ref, kvnw_ref, wuq_ref, wkv_ref, tab_ref,
             da_ref, dql_ref, dkvl_ref, gqn_ref, gkvn_ref):
        @pl.when(pl.program_id(0) == 0)
        def _():
            gqn_ref[...] = jnp.zeros_like(gqn_ref)
            gkvn_ref[...] = jnp.zeros_like(gkvn_ref)

        cs = _rope_cs(pos_ref[...], tab_ref)
        dkr = jnp.zeros((tT, HP), F32)
        for hd in range(MLA_HEADS):
            sl = slice(hd * HP, (hd + 1) * HP)
            dql_ref[:, sl] = (_rope(dq_ref[:, sl], cs, MLA_ROPE // 2, inverse=True) * ATTN_SCALE).astype(BF16)
            dkh = dk_ref[:, sl]
            dkr = dkr + dkh
            dkvl_ref[:, sl] = dkh.astype(BF16)
        dkvl_ref[:, QW:] = dv_ref[...]
        dqn = _dot_nt(dql_ref[...], wuq_ref[...])
        dkvn = _dot_nt(dkvl_ref[...], wkv_ref[...])
        cq = a_ref[:, 0:MLA_Q_RANK].astype(F32)
        ckv = a_ref[:, MLA_Q_RANK:MLA_Q_RANK + MLA_KV_RANK].astype(F32)
        dcq, gq = _rms_bwd(cq, qnw_ref[...], dqn)
        dckv, gkv = _rms_bwd(ckv, kvnw_ref[...], dkvn)
        gqn_ref[...] += gq
        gkvn_ref[...] += gkv
        da_ref[:, 0:MLA_Q_RANK] = dcq.astype(BF16)
        da_ref[:, MLA_Q_RANK:MLA_Q_RANK + MLA_KV_RANK] = dckv.astype(BF16)
        da_ref[:, 640:768] = _rope(dkr, cs, MLA_ROPE // 2, inverse=True).astype(BF16)
        da_ref[:, 768:AW] = jnp.zeros((tT, AW - 768), BF16)

    def full(r, c):
        return pl.BlockSpec((r, c), lambda i: (0, 0))

    def rows(c):
        return pl.BlockSpec((tT, c), lambda i: (i, 0))

    return pl.pallas_call(
        body, name=name, grid=(T // tT,),
        in_specs=[rows(QW), rows(QW), rows(QW), pl.BlockSpec((tT, AW), lambda i: (i, a_blk)), rows(1),
                  full(1, MLA_Q_RANK), full(1, MLA_KV_RANK),
                  full(MLA_Q_RANK, QW), full(MLA_KV_RANK, 2 * QW), full(8, LANES)],
        out_specs=[rows(AW), rows(QW), rows(2 * QW), full(1, MLA_Q_RANK), full(1, MLA_KV_RANK)],
        out_shape=[jax.ShapeDtypeStruct((T, AW), BF16), jax.ShapeDtypeStruct((T, QW), BF16),
                   jax.ShapeDtypeStruct((T, 2 * QW), BF16),
                   jax.ShapeDtypeStruct((1, MLA_Q_RANK), F32), jax.ShapeDtypeStruct((1, MLA_KV_RANK), F32)],
        compiler_params=_params(("arbitrary",)),
    )(dq, dk, dv, proj, pos, qn_w, kvn_w, w_uq, w_kv, tab)


def _causal_mask(n):
    return lax.broadcasted_iota(jnp.int32, (n, n), 0) >= lax.broadcasted_iota(jnp.int32, (n, n), 1)


def _flash_fwd(q, k, v, *, name):
    T = q.shape[0]
    H = q.shape[1] // HP
    tq = min(ATTN_TILE, T)
    nq = T // tq

    def body(q_ref, k_ref, v_ref, o_ref, lse_ref, m_s, l_s, acc_s):
        qi, ki = pl.program_id(1), pl.program_id(2)

        @pl.when(ki == 0)
        def _():
            m_s[...] = jnp.full_like(m_s, NEG)
            l_s[...] = jnp.zeros_like(l_s)
            acc_s[...] = jnp.zeros_like(acc_s)

        def step(masked):
            s = _dot_nt(q_ref[...], k_ref[...])
            if masked:
                s = jnp.where(_causal_mask(tq), s, NEG)
            m_prev = m_s[...]
            m_new = jnp.maximum(m_prev, jnp.max(s, axis=1, keepdims=True))
            alpha = jnp.exp(m_prev - m_new)
            p = jnp.exp(s - m_new)
            l_s[...] = alpha * l_s[...] + jnp.sum(p, axis=1, keepdims=True)
            acc_s[...] = alpha * acc_s[...] + _dot(p.astype(BF16), v_ref[...])
            m_s[...] = m_new

        @pl.when(ki < qi)
        def _():
            step(False)

        @pl.when(ki == qi)
        def _():
            step(True)
            l = l_s[...]
            o_ref[...] = (acc_s[...] / l).astype(BF16)
            lse_ref[...] = jnp.broadcast_to(m_s[...] + jnp.log(l), (tq, HP))

    qspec = pl.BlockSpec((tq, HP), lambda h, i, j: (i, h))
    kspec = pl.BlockSpec((tq, HP), lambda h, i, j: (jnp.minimum(i, j), h))
    return pl.pallas_call(
        body, name=name, grid=(H, nq, nq),
        in_specs=[qspec, kspec, kspec], out_specs=[qspec, qspec],
        out_shape=[jax.ShapeDtypeStruct((T, H * HP), BF16), jax.ShapeDtypeStruct((T, H * HP), F32)],
        scratch_shapes=[pltpu.VMEM((tq, 1), F32), pltpu.VMEM((tq, 1), F32), pltpu.VMEM((tq, HP), F32)],
        compiler_params=_params(("parallel", "parallel", "arbitrary")),
    )(q, k, v)


def _flash_dq(q, k, v, do, lse, delta, *, name):
    T = q.shape[0]
    H = q.shape[1] // HP
    tq = min(ATTN_TILE, T)
    nq = T // tq

    def body(q_ref, k_ref, v_ref, do_ref, lse_ref, dl_ref, dq_ref, acc):
        qi, ki = pl.program_id(1), pl.program_id(2)

        @pl.when(ki == 0)
        def _():
            acc[...] = jnp.zeros_like(acc)

        def step(masked):
            s = _dot_nt(q_ref[...], k_ref[...])
            if masked:
                s = jnp.where(_causal_mask(tq), s, NEG)
            p = jnp.exp(s - lse_ref[:, 0:1])
            dp = _dot_nt(do_ref[...], v_ref[...])
            ds = (p * (dp - dl_ref[:, 0:1])).astype(BF16)
            acc[...] += _dot(ds, k_ref[...])

        @pl.when(ki < qi)
        def _():
            step(False)

        @pl.when(ki == qi)
        def _():
            step(True)
            dq_ref[...] = acc[...]

    qspec = pl.BlockSpec((tq, HP), lambda h, i, j: (i, h))
    kspec = pl.BlockSpec((tq, HP), lambda h, i, j: (jnp.minimum(i, j), h))
    return pl.pallas_call(
        body, name=name, grid=(H, nq, nq),
        in_specs=[qspec, kspec, kspec, qspec, qspec, qspec], out_specs=qspec,
        out_shape=jax.ShapeDtypeStruct((T, H * HP), F32),
        scratch_shapes=[pltpu.VMEM((tq, HP), F32)],
        compiler_params=_params(("parallel", "parallel", "arbitrary")),
    )(q, k, v, do, lse, delta)


def _flash_dkv(q, k, v, do, lse, delta, *, name):
    T = q.shape[0]
    H = q.shape[1] // HP
    tq = min(ATTN_TILE, T)
    nq = T // tq

    def body(q_ref, k_ref, v_ref, do_ref, lse_ref, dl_ref, dk_ref, dv_ref, dk_acc, dv_acc):
        ki, qi = pl.program_id(1), pl.program_id(2)

        @pl.when(qi == 0)
        def _():
            dk_acc[...] = jnp.zeros_like(dk_acc)
            dv_acc[...] = jnp.zeros_like(dv_acc)

        def step(masked):
            s = _dot_nt(q_ref[...], k_ref[...])
            if masked:
                s = jnp.where(_causal_mask(tq), s, NEG)
            p = jnp.exp(s - lse_ref[:, 0:1])
            dv_acc[...] += _dot_tn(p.astype(BF16), do_ref[...])
            dp = _dot_nt(do_ref[...], v_ref[...])
            ds = (p * (dp - dl_ref[:, 0:1])).astype(BF16)
            dk_acc[...] += _dot_tn(ds, q_ref[...])

        @pl.when(qi > ki)
        def _():
            step(False)

        @pl.when(qi == ki)
        def _():
            step(True)

        @pl.when(qi == nq - 1)
        def _():
            dk_ref[...] = dk_acc[...]
            dv_ref[...] = dv_acc[...].astype(BF16)

    qspec = pl.BlockSpec((tq, HP), lambda h, j, i: (jnp.maximum(i, j), h))
    kspec = pl.BlockSpec((tq, HP), lambda h, j, i: (j, h))
    return pl.pallas_call(
        body, name=name, grid=(H, nq, nq),
        in_specs=[qspec, kspec, kspec, qspec, qspec, qspec], out_specs=[kspec, kspec],
        out_shape=[jax.ShapeDtypeStruct((T, H * HP), F32), jax.ShapeDtypeStruct((T, H * HP), BF16)],
        scratch_shapes=[pltpu.VMEM((tq, HP), F32), pltpu.VMEM((tq, HP), F32)],
        compiler_params=_params(("parallel", "parallel", "arbitrary")),
    )(q, k, v, do, lse, delta)


def _ret_consts(cc, hd):
    lg = math.log(1.0 - 2.0 ** (-5.0 - hd))
    diff = (lax.broadcasted_iota(jnp.int32, (cc, cc), 0) - lax.broadcasted_iota(jnp.int32, (cc, cc), 1)).astype(F32)
    decay = jnp.where(diff >= 0, jnp.exp(jnp.maximum(diff, 0.0) * lg), 0.0)
    idx = lax.broadcasted_iota(jnp.int32, (cc, 1), 0).astype(F32)
    zeta = jnp.exp((cc - 1.0 - idx) * lg)
    xi = jnp.exp((idx + 1.0) * lg)
    return decay, zeta, xi, math.exp(cc * lg)


def _ret_fwd(proj, pos, tab, *, name):
    T = proj.shape[0]
    cc = min(RET_TILE, T)
    n = T // cc

    def body(rq_ref, rk_ref, rv_ref, pos_ref, tab_ref, y_ref, yn_ref, rprev_ref, r_s):
        @pl.when(pl.program_id(0) == 0)
        def _():
            r_s[...] = jnp.zeros_like(r_s)

        cs = _rope_cs(pos_ref[...], tab_ref)
        for hd in range(RET_HEADS):
            sl = slice(hd * HP, (hd + 1) * HP)
            decay, zeta, xi, gc = _ret_consts(cc, hd)
            q = _rope(rq_ref[:, sl].astype(F32), cs, RET_DK // 2).astype(BF16)
            kf = _rope(rk_ref[:, sl].astype(F32), cs, RET_DK // 2) * (RET_DK ** -0.5)
            k = kf.astype(BF16)
            v = rv_ref[:, sl]
            r = r_s[hd]
            rprev_ref[0, hd] = r
            inner = (_dot_nt(q, k) * decay).astype(BF16)
            y = _dot(inner, v) + _dot(q, r.astype(BF16)) * xi
            r_s[hd] = r * gc + _dot_tn((kf * zeta).astype(BF16), v)
            y_ref[:, sl] = y
            mu = jnp.mean(y, axis=-1, keepdims=True)
            yc = y - mu
            var = jnp.mean(yc * yc, axis=-1, keepdims=True)
            yn_ref[:, sl] = (yc * lax.rsqrt(var + GN_EPS)).astype(BF16)

    def blk(j):
        return pl.BlockSpec((cc, RW), lambda i: (i, j))

    return pl.pallas_call(
        body, name=name, grid=(n,),
        in_specs=[blk(0), blk(1), blk(2), pl.BlockSpec((cc, 1), lambda i: (i, 0)),
                  pl.BlockSpec((8, LANES), lambda i: (0, 0))],
        out_specs=[blk(0), blk(0), pl.BlockSpec((1, RET_HEADS, HP, RET_DV), lambda i: (i, 0, 0, 0))],
        out_shape=[jax.ShapeDtypeStruct((T, RW), F32), jax.ShapeDtypeStruct((T, RW), BF16),
                   jax.ShapeDtypeStruct((n, RET_HEADS, HP, RET_DV), F32)],
        scratch_shapes=[pltpu.VMEM((RET_HEADS, HP, RET_DV), F32)],
        compiler_params=_params(("arbitrary",)),
    )(proj, proj, proj, pos, tab)


def _ret_bwd(dyn, y, proj, pos, tab, rprev, *, name):
    T = proj.shape[0]
    cc = min(RET_TILE, T)
    n = T // cc

    def body(dyn_ref, y_ref, rq_ref, rk_ref, rv_ref, pos_ref, tab_ref, rprev_ref,
             drq_ref, drk_ref, drv_ref, dr_s):
        @pl.when(pl.program_id(0) == 0)
        def _():
            dr_s[...] = jnp.zeros_like(dr_s)

        cs = _rope_cs(pos_ref[...], tab_ref)
        for hd in range(RET_HEADS):
            sl = slice(hd * HP, (hd + 1) * HP)
            decay, zeta, xi, gc = _ret_consts(cc, hd)
            q = _rope(rq_ref[:, sl].astype(F32), cs, RET_DK // 2).astype(BF16)
            kf = _rope(rk_ref[:, sl].astype(F32), cs, RET_DK // 2) * (RET_DK ** -0.5)
            k = kf.astype(BF16)
            v = rv_ref[:, sl]
            yv = y_ref[:, sl]
            mu = jnp.mean(yv, axis=-1, keepdims=True)
            yc = yv - mu
            rs = lax.rsqrt(jnp.mean(yc * yc, axis=-1, keepdims=True) + GN_EPS)
            yn = yc * rs
            dn = dyn_ref[:, sl]
            dy = rs * (dn - jnp.mean(dn, axis=-1, keepdims=True) - yn * jnp.mean(dn * yn, axis=-1, keepdims=True))
            dyb = dy.astype(BF16)
            dyx = (dy * xi).astype(BF16)
            dr = dr_s[hd]
            drb = dr.astype(BF16)
            inner = (_dot_nt(q, k) * decay).astype(BF16)
            da = (_dot_nt(dyb, v) * decay).astype(BF16)
            dv = _dot_tn(inner, dyb) + _dot((kf * zeta).astype(BF16), drb)
            dq = _dot(da, k) + _dot_nt(dyx, rprev_ref[0, hd].astype(BF16))
            dk = _dot_tn(da, q) + _dot_nt(v, drb) * zeta
            dr_s[hd] = dr * gc + _dot_tn(q, dyx)
            drq_ref[:, sl] = _rope(dq, cs, RET_DK // 2, inverse=True).astype(BF16)
            drk_ref[:, sl] = _rope(dk * (RET_DK ** -0.5), cs, RET_DK // 2, inverse=True).astype(BF16)
            drv_ref[:, sl] = dv.astype(BF16)

    def blk(j):
        return pl.BlockSpec((cc, RW), lambda i: (n - 1 - i, j))

    return pl.pallas_call(
        body, name=name, grid=(n,),
        in_specs=[blk(0), blk(0), blk(0), blk(1), blk(2), pl.BlockSpec((cc, 1), lambda i: (n - 1 - i, 0)),
                  pl.BlockSpec((8, LANES), lambda i: (0, 0)),
                  pl.BlockSpec((1, RET_HEADS, HP, RET_DV), lambda i: (n - 1 - i, 0, 0, 0))],
        out_specs=[blk(0), blk(0), blk(0)],
        out_shape=[jax.ShapeDtypeStruct((T, RW), BF16)] * 3,
        scratch_shapes=[pltpu.VMEM((RET_HEADS, HP, RET_DV), F32)],
        compiler_params=_params(("arbitrary",)),
    )(dyn, y, proj, proj, proj, pos, tab, rprev)


def _merge_fwd(o, yn, proj, gn_w, w_bm, w_br, w_out, h, post_w, *, name):
    T, D = h.shape
    tT = min(MERGE_TILE, T)
    g_blk = PROJ_FIXED // D

    def body(o_ref, yn_ref, rg_ref, gm_ref, gr_ref, gnw_ref, wbm_ref, wbr_ref, wout_ref, h_ref, post_ref,
             omla_ref, oret_ref, m_ref, ho_ref):
        o_mla = _dot(o_ref[...], wbm_ref[...])
        rg = rg_ref[...].astype(F32)
        gated = (rg * _sigmoid(rg) * (yn_ref[...].astype(F32) * gnw_ref[...])).astype(BF16)
        o_ret = _dot(gated, wbr_ref[...])
        omla_ref[...] = o_mla.astype(BF16)
        oret_ref[...] = o_ret.astype(BF16)
        merged = _sigmoid(gm_ref[...].astype(F32)) * o_mla + _sigmoid(gr_ref[...].astype(F32)) * o_ret
        m = _dot(merged.astype(BF16), wout_ref[...])
        m_ref[...] = m
        ho_ref[...] = h_ref[...] + _rms_fwd(m, post_ref[...])

    def full(r, c):
        return pl.BlockSpec((r, c), lambda i: (0, 0))

    def rows(c, j=0):
        return pl.BlockSpec((tT, c), lambda i: (i, j))

    return pl.pallas_call(
        body, name=name, grid=(T // tT,),
        in_specs=[rows(QW), rows(RW), rows(RW, 3), rows(D, g_blk), rows(D, g_blk + 1), full(1, RW),
                  full(QW, D), full(RW, D), full(D, D), rows(D), full(1, D)],
        out_specs=[rows(D), rows(D), rows(D), rows(D)],
        out_shape=[jax.ShapeDtypeStruct((T, D), BF16), jax.ShapeDtypeStruct((T, D), BF16),
                   jax.ShapeDtypeStruct((T, D), F32), jax.ShapeDtypeStruct((T, D), F32)],
        compiler_params=_params(("parallel",)),
    )(o, yn, proj, proj, proj, gn_w, w_bm, w_br, w_out, h, post_w)


def _merge_bwd(dho, m, post_w, omla, oret, proj, yn, gn_w, o, w_out, w_bm, w_br, *, name):
    T, D = dho.shape
    tT = min(MERGE_TILE, T)
    g_blk = PROJ_FIXED // D

    def body(dho_ref, m_ref, post_ref, omla_ref, oret_ref, rg_ref, gm_ref, gr_ref, yn_ref, gnw_ref, o_ref,
             wout_ref, wbm_ref, wbr_ref,
             dm_ref, merged_ref, dgm_ref, dgr_ref, domla_ref, do_ref, delta_ref, doret_ref, gated_ref,
             drg_ref, dyn_ref, gpost_ref, ggn_ref):
        @pl.when(pl.program_id(0) == 0)
        def _():
            gpost_ref[...] = jnp.zeros_like(gpost_ref)
            ggn_ref[...] = jnp.zeros_like(ggn_ref)

        dm, gp = _rms_bwd(m_ref[...], post_ref[...], dho_ref[...])
        gpost_ref[...] += gp
        dmb = dm.astype(BF16)
        dm_ref[...] = dmb
        dmerged = _dot_nt(dmb, wout_ref[...])
        o_mla = omla_ref[...].astype(F32)
        o_ret = oret_ref[...].astype(F32)
        sgm = _sigmoid(gm_ref[...].astype(F32))
        sgr = _sigmoid(gr_ref[...].astype(F32))
        merged_ref[...] = (sgm * o_mla + sgr * o_ret).astype(BF16)
        dgm_ref[...] = (dmerged * o_mla * sgm * (1.0 - sgm)).astype(BF16)
        dgr_ref[...] = (dmerged * o_ret * sgr * (1.0 - sgr)).astype(BF16)
        domla = (dmerged * sgm).astype(BF16)
        domla_ref[...] = domla
        do = _dot_nt(domla, wbm_ref[...])
        do_ref[...] = do.astype(BF16)
        for hd in range(MLA_HEADS):
            sl = slice(hd * HP, (hd + 1) * HP)
            d = jnp.sum(do[:, sl] * o_ref[:, sl].astype(F32), axis=-1, keepdims=True)
            delta_ref[:, sl] = jnp.broadcast_to(d, (tT, HP))
        doret = (dmerged * sgr).astype(BF16)
        doret_ref[...] = doret
        dgated = _dot_nt(doret, wbr_ref[...])
        rg = rg_ref[...].astype(F32)
        sg = _sigmoid(rg)
        srg = rg * sg
        ynv = yn_ref[...].astype(F32)
        yw = ynv * gnw_ref[...]
        gated_ref[...] = (srg * yw).astype(BF16)
        drg_ref[...] = (dgated * yw * (sg * (1.0 + rg * (1.0 - sg)))).astype(BF16)
        dgs = dgated * srg
        dyn_ref[...] = dgs * gnw_ref[...]
        ggn_ref[...] += jnp.sum(dgs * ynv, axis=0, keepdims=True)

    def full(r, c):
        return pl.BlockSpec((r, c), lambda i: (0, 0))

    def rows(c, j=0):
        return pl.BlockSpec((tT, c), lambda i: (i, j))

    return pl.pallas_call(
        body, name=name, grid=(T // tT,),
        in_specs=[rows(D), rows(D), full(1, D), rows(D), rows(D), rows(RW, 3), rows(D, g_blk), rows(D, g_blk + 1),
                  rows(RW), full(1, RW), rows(QW), full(D, D), full(QW, D), full(RW, D)],
        out_specs=[rows(D), rows(D), rows(D), rows(D), rows(D), rows(QW), rows(QW), rows(D), rows(RW),
                   rows(RW), rows(RW), full(1, D), full(1, RW)],
        out_shape=[jax.ShapeDtypeStruct((T, D), BF16)] * 5
        + [jax.ShapeDtypeStruct((T, QW), BF16), jax.ShapeDtypeStruct((T, QW), F32),
           jax.ShapeDtypeStruct((T, D), BF16), jax.ShapeDtypeStruct((T, RW), BF16),
           jax.ShapeDtypeStruct((T, RW), BF16), jax.ShapeDtypeStruct((T, RW), F32),
           jax.ShapeDtypeStruct((1, D), F32), jax.ShapeDtypeStruct((1, RW), F32)],
        compiler_params=_params(("arbitrary",)),
    )(dho, m, post_w, omla, oret, proj, proj, proj, yn, gn_w, o, w_out, w_bm, w_br)


def _mesh_pos():
    return lax.axis_index("x"), lax.axis_index("y"), lax.axis_index("c")


def _all_gather(x_loc, *, name):
    R = x_loc.shape[0]

    def body(x_ref, out_ref, send_sems, recv_sems, local_sem):
        x, y, c = _mesh_pos()
        me, sibling = (x, y, c), (x, y, 1 - c)
        chips = [(1 - x, y), (x, 1 - y), (1 - x, 1 - y)]

        def slot(px, py, pc):
            return out_ref.at[4 * px + 2 * py + pc]

        def copy(k, block, to, src=None):
            return pltpu.make_async_remote_copy(
                src_ref=slot(*block) if src is None else src, dst_ref=slot(*block),
                send_sem=send_sems.at[k], recv_sem=recv_sems.at[k],
                device_id=to, device_id_type=pl.DeviceIdType.MESH)

        mine = pltpu.make_async_copy(x_ref, slot(*me), local_sem)
        mine.start()
        first = [copy(0, me, sibling, src=x_ref)]
        first += [copy(1 + j, me, (*chip, c), src=x_ref) for j, chip in enumerate(chips)]
        for cp in first:
            cp.start()
        passed = [copy(4 + j, (*chip, c), sibling) for j, chip in enumerate(chips)]
        for j, chip in enumerate(chips):
            copy(1 + j, (*chip, c), me).wait_recv()
            passed[j].start()
        copy(0, sibling, me).wait_recv()
        for j, chip in enumerate(chips):
            copy(4 + j, (*chip, 1 - c), me).wait_recv()
        for cp in first + passed:
            cp.wait_send()
        mine.wait()

    return pl.pallas_call(
        body, name=name,
        out_shape=jax.ShapeDtypeStruct((N_DEV, R, LANES), x_loc.dtype),
        in_specs=[pl.BlockSpec(memory_space=pl.ANY)],
        out_specs=pl.BlockSpec(memory_space=pl.ANY),
        scratch_shapes=[pltpu.SemaphoreType.DMA((7,)), pltpu.SemaphoreType.DMA((7,)), pltpu.SemaphoreType.DMA],
    )(x_loc)


def _exchange_grads(g, sv, *, name):
    _, R, _ = g.shape
    Rs = sv.shape[0]

    def body(g_ref, sv_ref, recv_ref, sall_ref, send_sems, recv_sems, local_sems):
        x, y, c = _mesh_pos()
        me = 4 * x + 2 * y + c
        own_big = pltpu.make_async_copy(g_ref.at[me], recv_ref.at[me], local_sems.at[0])
        own_small = pltpu.make_async_copy(sv_ref, sall_ref.at[me], local_sems.at[1])
        own_big.start()
        own_small.start()
        sends, recvs = [], []
        for r in range(1, N_DEV):
            px = 1 - x if r & 4 else x
            py = 1 - y if r & 2 else y
            pc = 1 - c if r & 1 else c
            peer, pidx = (px, py, pc), 4 * px + 2 * py + pc
            kb, ks = r - 1, N_DEV - 1 + r - 1
            sends.append(pltpu.make_async_remote_copy(
                src_ref=g_ref.at[pidx], dst_ref=recv_ref.at[me], send_sem=send_sems.at[kb],
                recv_sem=recv_sems.at[kb], device_id=peer, device_id_type=pl.DeviceIdType.MESH))
            sends.append(pltpu.make_async_remote_copy(
                src_ref=sv_ref, dst_ref=sall_ref.at[me], send_sem=send_sems.at[ks],
                recv_sem=recv_sems.at[ks], device_id=peer, device_id_type=pl.DeviceIdType.MESH))
            recvs.append(pltpu.make_async_remote_copy(
                src_ref=g_ref.at[me], dst_ref=recv_ref.at[pidx], send_sem=send_sems.at[kb],
                recv_sem=recv_sems.at[kb], device_id=peer, device_id_type=pl.DeviceIdType.MESH))
            recvs.append(pltpu.make_async_remote_copy(
                src_ref=sv_ref, dst_ref=sall_ref.at[pidx], send_sem=send_sems.at[ks],
                recv_sem=recv_sems.at[ks], device_id=peer, device_id_type=pl.DeviceIdType.MESH))
        for cp in sends:
            cp.start()
        for cp in recvs:
            cp.wait_recv()
        for cp in sends:
            cp.wait_send()
        own_big.wait()
        own_small.wait()

    return pl.pallas_call(
        body, name=name,
        out_shape=[jax.ShapeDtypeStruct((N_DEV, R, LANES), g.dtype),
                   jax.ShapeDtypeStruct((N_DEV, Rs, LANES), sv.dtype)],
        in_specs=[pl.BlockSpec(memory_space=pl.ANY), pl.BlockSpec(memory_space=pl.ANY)],
        out_specs=[pl.BlockSpec(memory_space=pl.ANY), pl.BlockSpec(memory_space=pl.ANY)],
        scratch_shapes=[pltpu.SemaphoreType.DMA((2 * (N_DEV - 1),)), pltpu.SemaphoreType.DMA((2 * (N_DEV - 1),)),
                        pltpu.SemaphoreType.DMA((2,))],
    )(g, sv)


def _adamw(w, parts, m, v, *, name):
    R = w.shape[0]
    tr = _tile(R, 1024, 16)

    def body(w_ref, p_ref, m_ref, v_ref, g_ref, d_ref, nm_ref, nv_ref):
        g = p_ref[0].astype(F32)
        for j in range(1, N_DEV):
            g = g + p_ref[j].astype(F32)
        g_ref[...] = g
        nm = ADAM_B1 * m_ref[...] + (1.0 - ADAM_B1) * g
        nv = ADAM_B2 * v_ref[...] + (1.0 - ADAM_B2) * (g * g)
        nm_ref[...] = nm
        nv_ref[...] = nv
        m_hat = nm / (1.0 - ADAM_B1 ** ADAM_STEP)
        v_hat = nv / (1.0 - ADAM_B2 ** ADAM_STEP)
        d_ref[...] = -ADAM_LR * (m_hat / (jnp.sqrt(v_hat) + ADAM_EPS) + ADAM_WD * w_ref[...])

    row = pl.BlockSpec((tr, LANES), lambda i: (i, 0))
    return pl.pallas_call(
        body, name=name, grid=(R // tr,),
        in_specs=[row, pl.BlockSpec((N_DEV, tr, LANES), lambda i: (0, i, 0)), row, row],
        out_specs=[row, row, row, row],
        out_shape=[jax.ShapeDtypeStruct((R, LANES), F32)] * 4,
        compiler_params=_params(("parallel",)),
    )(w, parts, m, v)


def _pad_last(a, width):
    return jnp.pad(a, [(0, 0)] * (a.ndim - 1) + [(0, width - a.shape[-1])])


def _flat_rows(a):
    return a.reshape(-1, LANES)


def _pad_rows(a, mult):
    r = (-a.shape[0]) % mult
    return jnp.pad(a, ((0, r), (0, 0))) if r else a


def kernel(x, positions, ffn1_pre_w, ffn1_w1, ffn1_w2, ffn1_post_w, mix_pre_w, w_in, mla_q_norm_w, mla_w_uq, mla_kv_norm_w, mla_w_ukv, ret_gn_w, w_branch_mla, w_branch_ret, w_out, mix_post_w, ffn2_pre_w, ffn2_w1, ffn2_w2, ffn2_post_w, loss_target, m_ffn1_pre_w, m_ffn1_w1, m_ffn1_w2, m_ffn1_post_w, m_mix_pre_w, m_w_in, m_mla_q_norm_w, m_mla_w_uq, m_mla_kv_norm_w, m_mla_w_ukv, m_ret_gn_w, m_w_branch_mla, m_w_branch_ret, m_w_out, m_mix_post_w, m_ffn2_pre_w, m_ffn2_w1, m_ffn2_w2, m_ffn2_post_w, v_ffn1_pre_w, v_ffn1_w1, v_ffn1_w2, v_ffn1_post_w, v_mix_pre_w, v_w_in, v_mla_q_norm_w, v_mla_w_uq, v_mla_kv_norm_w, v_mla_w_ukv, v_ret_gn_w, v_w_branch_mla, v_w_branch_ret, v_w_out, v_mix_post_w, v_ffn2_pre_w, v_ffn2_w1, v_ffn2_w2, v_ffn2_post_w):
    T, D = x.shape[1], x.shape[2]
    h0 = x[0]
    tgt = loss_target[0]
    pos = positions.reshape(T, 1).astype(F32)

    big = [("ffn1_w1", ffn1_w1, m_ffn1_w1, v_ffn1_w1, "col"), ("ffn1_w2", ffn1_w2, m_ffn1_w2, v_ffn1_w2, "row"),
           ("w_in", w_in, m_w_in, v_w_in, "col"), ("mla_w_uq", mla_w_uq, m_mla_w_uq, v_mla_w_uq, "col"),
           ("mla_w_ukv", mla_w_ukv, m_mla_w_ukv, v_mla_w_ukv, "col"),
           ("w_branch_mla", w_branch_mla, m_w_branch_mla, v_w_branch_mla, "col"),
           ("w_branch_ret", w_branch_ret, m_w_branch_ret, v_w_branch_ret, "col"),
           ("w_out", w_out, m_w_out, v_w_out, "row"),
           ("ffn2_w1", ffn2_w1, m_ffn2_w1, v_ffn2_w1, "col"), ("ffn2_w2", ffn2_w2, m_ffn2_w2, v_ffn2_w2, "row")]
    small = [("ffn1_pre_w", ffn1_pre_w, m_ffn1_pre_w, v_ffn1_pre_w), ("ffn1_post_w", ffn1_post_w, m_ffn1_post_w, v_ffn1_post_w),
             ("mix_pre_w", mix_pre_w, m_mix_pre_w, v_mix_pre_w), ("mla_q_norm_w", mla_q_norm_w, m_mla_q_norm_w, v_mla_q_norm_w),
             ("mla_kv_norm_w", mla_kv_norm_w, m_mla_kv_norm_w, v_mla_kv_norm_w), ("ret_gn_w", ret_gn_w, m_ret_gn_w, v_ret_gn_w),
             ("mix_post_w", mix_post_w, m_mix_post_w, v_mix_post_w), ("ffn2_pre_w", ffn2_pre_w, m_ffn2_pre_w, v_ffn2_pre_w),
             ("ffn2_post_w", ffn2_post_w, m_ffn2_post_w, v_ffn2_post_w)]

    offs, off = {}, 0
    for nm, w, _, _, _ in big:
        rows = w[0].size // LANES
        assert w[0].size % (16 * LANES) == 0, (nm, w.shape)
        offs[nm] = (off, rows)
        off += rows
    flat_w = jnp.concatenate([_flat_rows(w[0]) for _, w, _, _, _ in big], axis=0)
    flat_m = jnp.concatenate([_flat_rows(a[0]) for _, _, a, _, _ in big], axis=0)
    flat_v = jnp.concatenate([_flat_rows(a[0]) for _, _, _, a, _ in big], axis=0)

    gathered = _all_gather(flat_w.astype(BF16), name="weights_all_gather")

    def full_weight(nm, w, kind):
        o, rows = offs[nm]
        a, b = w.shape[1], w.shape[2]
        blk = gathered[:, o:o + rows].reshape(N_DEV, a, b)
        if kind == "col":
            return blk.transpose(1, 0, 2).reshape(a, N_DEV * b)
        return blk.reshape(N_DEV * a, b)

    fw = {nm: full_weight(nm, w, kind) for nm, w, _, _, kind in big}

    wi = fw["w_in"]
    cq_w, ckv_w, kr_w = wi[:, 0:384], wi[:, 384:640], wi[:, 640:672]
    rq_w, rk_w = wi[:, 672:928], wi[:, 928:1184]
    rv_w, rg_w = wi[:, 1184:1696], wi[:, 1696:2208]
    gm_w, gr_w = wi[:, 2208:2208 + D], wi[:, 2208 + D:2208 + 2 * D]
    zer = lambda n: jnp.zeros((D, n), BF16)
    head_pad = lambda a, h: _pad_last(a.reshape(a.shape[0], h, -1), HP).reshape(a.shape[0], h * HP)
    w_in_p = jnp.concatenate([head_pad(rq_w, RET_HEADS), head_pad(rk_w, RET_HEADS), rv_w, rg_w,
                              cq_w, ckv_w, zer(MLA_NOPE), kr_w, zer(HP - MLA_NOPE - MLA_ROPE), zer(AW - 768),
                              gm_w, gr_w], axis=1)
    w_uq_p = head_pad(fw["mla_w_uq"], MLA_HEADS)
    ukv = fw["mla_w_ukv"].reshape(MLA_KV_RANK, MLA_HEADS, MLA_NOPE + MLA_V)
    w_kv_p = jnp.concatenate([_pad_last(ukv[:, :, :MLA_NOPE], HP).reshape(MLA_KV_RANK, QW),
                              _pad_last(ukv[:, :, MLA_NOPE:], HP).reshape(MLA_KV_RANK, QW)], axis=1)
    w_bm_p = jnp.pad(fw["w_branch_mla"].reshape(MLA_HEADS, MLA_V, D), ((0, 0), (0, HP - MLA_V), (0, 0))).reshape(QW, D)
    w_br, w_o = fw["w_branch_ret"], fw["w_out"]
    tab_mla = _rope_table(MLA_NOPE, MLA_ROPE // 2)
    tab_ret = _rope_table(0, RET_DK // 2)

    u1, f1, h1 = _ffn_fwd(h0, ffn1_pre_w, fw["ffn1_w1"], fw["ffn1_w2"], ffn1_post_w, None, name="ffn1_fwd")
    proj, a1 = _rms_matmul(h1, mix_pre_w, w_in_p, name="mixer_in_proj")
    q, k, v, qn, kvn = _mla_prep_fwd(proj, pos, mla_q_norm_w, mla_kv_norm_w, w_uq_p, w_kv_p, tab_mla, name="mla_prep_fwd")
    o, lse = _flash_fwd(q, k, v, name="mla_attn_fwd")
    ypre, yn, rprev = _ret_fwd(proj, pos, tab_ret, name="retention_fwd")
    omla, oret, m, h2 = _merge_fwd(o, yn, proj, ret_gn_w, w_bm_p, w_br, w_o, h1, mix_post_w, name="merge_fwd")
    u2, f2, _, dy, lossp = _ffn_fwd(h2, ffn2_pre_w, fw["ffn2_w1"], fw["ffn2_w2"], ffn2_post_w, tgt, name="ffn2_fwd_loss")
    loss = lax.psum(jnp.sum(lossp[::8, 0]), ("x", "y", "c"))

    g2, du2, df2, a2, dh2, gpost2, gpre2 = _ffn_bwd(dy, f2, ffn2_post_w, h2, ffn2_pre_w, u2, fw["ffn2_w2"], fw["ffn2_w1"], name="ffn2_bwd")
    dw1b = _matmul_tn(a2, du2, name="ffn2_dw1")
    dw2b = _matmul_tn(g2, df2[None], name="ffn2_dw2")[0]
    (dmb, merged, dgm, dgr, domla, do, delta, doret, gated, drg, dyn, gpostm, ggn) = _merge_bwd(
        dh2, m, mix_post_w, omla, oret, proj, yn, ret_gn_w, o, w_o, w_bm_p, w_br, name="merge_bwd")
    dw_out = _matmul_tn(merged, dmb[None], name="dw_out")[0]
    dw_bm_p = _matmul_tn(o, domla[None], name="dw_branch_mla")[0]
    dw_br = _matmul_tn(gated, doret[None], name="dw_branch_ret")[0]
    dq = _flash_dq(q, k, v, do, lse, delta, name="mla_attn_dq")
    dk, dv = _flash_dkv(q, k, v, do, lse, delta, name="mla_attn_dkv")
    da, dql, dkvl, gqn, gkvn = _mla_prep_bwd(dq, dk, dv, proj, pos, mla_q_norm_w, mla_kv_norm_w, w_uq_p, w_kv_p, tab_mla, name="mla_prep_bwd")
    dw_uq_p = _matmul_tn(qn, dql[None], name="dw_uq")[0]
    dw_kv_p = _matmul_tn(kvn, dkvl[None], name="dw_ukv")[0]
    drq, drk, drv = _ret_bwd(dyn, ypre, proj, pos, tab_ret, rprev, name="retention_bwd")
    dproj = jnp.concatenate([drq, drk, drv, drg, da, dgm, dgr], axis=1)
    dw_in_p = _matmul_tn(a1, dproj[None], name="dw_in")[0]
    dh1, gmixpre = _proj_bwd(dproj, w_in_p, h1, mix_pre_w, dh2, name="mixer_in_bwd")
    g1, du1, df1, a0, dx, gpost1, gpre1 = _ffn_bwd(dh1, f1, ffn1_post_w, h0, ffn1_pre_w, u1, fw["ffn1_w2"], fw["ffn1_w1"], name="ffn1_bwd")
    dw1a = _matmul_tn(a0, du1, name="ffn1_dw1")
    dw2a = _matmul_tn(g1, df1[None], name="ffn1_dw2")[0]

    FF = dw2a.shape[0]
    unhead = lambda a, h, wd: a.reshape(a.shape[0], h, HP)[:, :, :wd].reshape(a.shape[0], h * wd)
    c0 = 4 * RW
    dw_in = jnp.concatenate([
        dw_in_p[:, c0:c0 + 384], dw_in_p[:, c0 + 384:c0 + 640], dw_in_p[:, c0 + 640 + MLA_NOPE:c0 + 640 + MLA_NOPE + MLA_ROPE],
        unhead(dw_in_p[:, 0:RW], RET_HEADS, RET_DK), unhead(dw_in_p[:, RW:2 * RW], RET_HEADS, RET_DK),
        dw_in_p[:, 2 * RW:3 * RW], dw_in_p[:, 3 * RW:4 * RW],
        dw_in_p[:, PROJ_FIXED:PROJ_FIXED + D], dw_in_p[:, PROJ_FIXED + D:PROJ_FIXED + 2 * D]], axis=1)
    dw_uq = unhead(dw_uq_p, MLA_HEADS, MLA_NOPE + MLA_ROPE)
    dkp = dw_kv_p[:, :QW].reshape(MLA_KV_RANK, MLA_HEADS, HP)[:, :, :MLA_NOPE]
    dvp = dw_kv_p[:, QW:].reshape(MLA_KV_RANK, MLA_HEADS, HP)[:, :, :MLA_V]
    dw_ukv = jnp.concatenate([dkp, dvp], axis=2).reshape(MLA_KV_RANK, MLA_HEADS * (MLA_NOPE + MLA_V))
    dw_bm = dw_bm_p.reshape(MLA_HEADS, HP, D)[:, :MLA_V].reshape(MLA_HEADS * MLA_V, D)

    def w1_shards(dw):
        return dw.reshape(2, D, 4, FF // 4).transpose(0, 2, 1, 3).reshape(N_DEV, -1, LANES)

    def col_shards(dw):
        a, b = dw.shape
        return dw.reshape(a, N_DEV, b // N_DEV).transpose(1, 0, 2).reshape(N_DEV, -1, LANES)

    def row_shards(dw):
        return dw.reshape(N_DEV, -1, LANES)

    grads = {"ffn1_w1": w1_shards(dw1a), "ffn1_w2": row_shards(dw2a), "w_in": col_shards(dw_in),
             "mla_w_uq": col_shards(dw_uq), "mla_w_ukv": col_shards(dw_ukv), "w_branch_mla": col_shards(dw_bm),
             "w_branch_ret": col_shards(dw_br), "w_out": row_shards(dw_out),
             "ffn2_w1": w1_shards(dw1b), "ffn2_w2": row_shards(dw2b)}
    g_all = jnp.concatenate([grads[nm] for nm, *_ in big], axis=1).astype(BF16)

    small_g = {"ffn1_pre_w": gpre1, "ffn1_post_w": gpost1, "mix_pre_w": gmixpre, "mla_q_norm_w": gqn,
               "mla_kv_norm_w": gkvn, "ret_gn_w": ggn, "mix_post_w": gpostm, "ffn2_pre_w": gpre2, "ffn2_post_w": gpost2}

    def small_flat(arrs):
        a = jnp.concatenate([z.reshape(-1) for z in arrs])
        a = jnp.pad(a, (0, (-a.size) % (8 * LANES)))
        return a.reshape(-1, LANES)

    sv = small_flat([small_g[nm] for nm, *_ in small])
    recv, sall = _exchange_grads(g_all, sv, name="grads_exchange")

    gb, db, nmb, nvb = _adamw(flat_w, recv, flat_m, flat_v, name="adamw_sharded")
    gs, ds, nms, nvs = _adamw(small_flat([w for _, w, _, _ in small]), sall,
                              small_flat([a for _, _, a, _ in small]), small_flat([a for _, _, _, a in small]),
                              name="adamw_replicated")

    def split_big(flat):
        out = {}
        for nm, w, _, _, _ in big:
            o_, rows = offs[nm]
            out[nm] = flat[o_:o_ + rows].reshape(w.shape)
        return out

    def split_small(flat):
        out, o_ = {}, 0
        fl = flat.reshape(-1)
        for nm, w, _, _ in small:
            out[nm] = fl[o_:o_ + w.size].reshape(w.shape)
            o_ += w.size
        return out

    order = ["ffn1_pre_w", "ffn1_w1", "ffn1_w2", "ffn1_post_w", "mix_pre_w", "w_in", "mla_q_norm_w", "mla_w_uq",
             "mla_kv_norm_w", "mla_w_ukv", "ret_gn_w", "w_branch_mla", "w_branch_ret", "w_out", "mix_post_w",
             "ffn2_pre_w", "ffn2_w1", "ffn2_w2", "ffn2_post_w"]
    outs = [loss, dx[None]]
    for fb, fs in ((gb, gs), (db, ds), (nmb, nms), (nvb, nvs)):
        both = {**split_big(fb), **split_small(fs)}
        outs += [both[nm] for nm in order]
    return tuple(outs)
```

```python
import math

import numpy as np
import jax
import jax.numpy as jnp
from jax import lax
from jax.experimental import pallas as pl
from jax.experimental.pallas import tpu as pltpu

F32, BF16 = jnp.float32, jnp.bfloat16

MLA_HEADS, MLA_NOPE, MLA_ROPE, MLA_V = 8, 64, 32, 64
MLA_Q_RANK, MLA_KV_RANK = 384, 256
RET_HEADS, RET_DK, RET_DV = 4, 64, 128
ROPE_BASE, NORM_EPS, GN_EPS = 10000.0, 1e-6, 1e-6
ADAM_LR, ADAM_B1, ADAM_B2, ADAM_EPS, ADAM_WD, ADAM_STEP = 0.001, 0.9, 0.999, 1e-08, 0.01, 10
ATTN_SCALE = 1.0 / math.sqrt(MLA_NOPE + MLA_ROPE)

N_DEV = 8
LANES = 128
HP = LANES
QW = MLA_HEADS * HP
RW = RET_HEADS * HP
AW = 1024
PROJ_FIXED = 4 * RW + AW
NEG = -1e30

TOKEN_TILE = 512
ATTN_TILE = 512
ATTN_CHAINS = 2
RET_TILE = 256
FF_TILE_CAP = 512
GRAD_TILE_CAP = 1408
MERGE_TILE = 256
VMEM_LIMIT = 56 * 1024 * 1024


def _tile(n, cap, mult=LANES):
    if n <= cap:
        return n
    best = None
    for t in range(mult, cap + 1, mult):
        if n % t == 0:
            best = t
    assert best is not None, (n, cap, mult)
    return best


def _params(sem):
    return pltpu.CompilerParams(dimension_semantics=sem, vmem_limit_bytes=VMEM_LIMIT)


def _dot(a, b):
    return lax.dot_general(a, b, (((1,), (0,)), ((), ())), preferred_element_type=F32)


def _dot_nt(a, b):
    return lax.dot_general(a, b, (((1,), (1,)), ((), ())), preferred_element_type=F32)


def _dot_tn(a, b):
    return lax.dot_general(a, b, (((0,), (0,)), ((), ())), preferred_element_type=F32)


def _sigmoid(x):
    return 1.0 / (1.0 + jnp.exp(-x))


def _rms_fwd(x, w):
    r = lax.rsqrt(jnp.mean(x * x, axis=-1, keepdims=True) + NORM_EPS)
    return x * r * w


def _rms_bwd(x, w, dy):
    r = lax.rsqrt(jnp.mean(x * x, axis=-1, keepdims=True) + NORM_EPS)
    xh = x * r
    g = dy * w
    dx = r * (g - xh * jnp.mean(g * xh, axis=-1, keepdims=True))
    return dx, jnp.sum(dy * xh, axis=0, keepdims=True)


def _rope_table(first, half):
    inv = (np.float32(ROPE_BASE) ** (-(np.arange(half, dtype=np.float32) / np.float32(half)))).astype(np.float32)
    tab = np.zeros((8, LANES), np.float32)
    tab[0, first:first + half] = inv
    tab[0, first + half:first + 2 * half] = inv
    tab[1, first:first + half] = -1.0
    tab[2, first + half:first + 2 * half] = 1.0
    return jnp.asarray(tab)


def _rope_cs(pos, tab_ref):
    ang = pos * tab_ref[0:1, :]
    s = jnp.sin(ang)
    return jnp.cos(ang), s * tab_ref[1:2, :], s * tab_ref[2:3, :]


def _rope(x, cs, half, inverse=False):
    c, s1, s2 = cs
    a = pltpu.roll(x, LANES - half, 1) * s1 + pltpu.roll(x, half, 1) * s2
    return x * c - a if inverse else x * c + a


def _ffn_fwd(h, pre_w, w1, w2, post_w, target, *, name):
    T, D = h.shape
    FF = w2.shape[0]
    tT, ck = min(TOKEN_TILE, T), _tile(FF, FF_TILE_CAP)
    nT, nk = T // tT, FF // ck
    with_loss = target is not None

    def body(*refs):
        if with_loss:
            (h_ref, pre_ref, w1g_ref, w1u_ref, w2_ref, post_ref, tgt_ref,
             u_ref, f_ref, ho_ref, dy_ref, loss_ref, a_s, acc) = refs
        else:
            (h_ref, pre_ref, w1g_ref, w1u_ref, w2_ref, post_ref,
             u_ref, f_ref, ho_ref, a_s, acc) = refs
        k = pl.program_id(1)

        @pl.when(k == 0)
        def _():
            a_s[...] = _rms_fwd(h_ref[...], pre_ref[...]).astype(BF16)
            acc[...] = jnp.zeros_like(acc)

        a = a_s[...]
        ug = _dot(a, w1g_ref[...])
        uu = _dot(a, w1u_ref[...])
        u_ref[0] = ug.astype(BF16)
        u_ref[1] = uu.astype(BF16)
        acc[...] += _dot((ug * _sigmoid(ug) * uu).astype(BF16), w2_ref[...])

        @pl.when(k == nk - 1)
        def _():
            f = acc[...]
            f_ref[...] = f
            ho = h_ref[...] + 0.5 * _rms_fwd(f, post_ref[...])
            ho_ref[...] = ho
            if with_loss:
                e = ho - tgt_ref[...]
                dy_ref[...] = e * (1.0 / D)
                loss_ref[...] = jnp.full(loss_ref.shape, (0.5 / D) * jnp.sum(e * e), F32)

    row = pl.BlockSpec((tT, D), lambda i, k: (i, 0))
    vec = pl.BlockSpec((1, D), lambda i, k: (0, 0))
    in_specs = [row, vec,
                pl.BlockSpec((D, ck), lambda i, k: (0, k)),
                pl.BlockSpec((D, ck), lambda i, k: (0, nk + k)),
                pl.BlockSpec((ck, D), lambda i, k: (k, 0)),
                vec]
    out_shape = [jax.ShapeDtypeStruct((2, T, FF), BF16),
                 jax.ShapeDtypeStruct((T, D), F32),
                 jax.ShapeDtypeStruct((T, D), F32)]
    out_specs = [pl.BlockSpec((2, tT, ck), lambda i, k: (0, i, k)), row, row]
    args = [h, pre_w, w1, w1, w2, post_w]
    if with_loss:
        in_specs.append(row)
        args.append(target)
        out_shape += [jax.ShapeDtypeStruct((T, D), F32), jax.ShapeDtypeStruct((nT * 8, LANES), F32)]
        out_specs += [row, pl.BlockSpec((8, LANES), lambda i, k: (i, 0))]
    return pl.pallas_call(
        body, name=name, grid=(nT, nk), in_specs=in_specs, out_specs=out_specs, out_shape=out_shape,
        scratch_shapes=[pltpu.VMEM((tT, D), BF16), pltpu.VMEM((tT, D), F32)],
        compiler_params=_params(("parallel", "arbitrary")),
    )(*args)


def _ffn_bwd(dho, f, post_w, h, pre_w, u, w2, w1, *, name):
    T, D = h.shape
    FF = w2.shape[0]
    tT, ck = min(TOKEN_TILE, T), _tile(FF, FF_TILE_CAP)
    nT, nk = T // tT, FF // ck

    def body(dho_ref, f_ref, post_ref, h_ref, pre_ref, u_ref, w2_ref, w1g_ref, w1u_ref,
             g_ref, du_ref, df_ref, a_ref, dh_ref, gpost_ref, gpre_ref, df_s, da_acc):
        i, k = pl.program_id(0), pl.program_id(1)

        @pl.when(jnp.logical_and(i == 0, k == 0))
        def _():
            gpost_ref[...] = jnp.zeros_like(gpost_ref)
            gpre_ref[...] = jnp.zeros_like(gpre_ref)

        @pl.when(k == 0)
        def _():
            dx, dw = _rms_bwd(f_ref[...], post_ref[...], 0.5 * dho_ref[...])
            dfb = dx.astype(BF16)
            df_s[...] = dfb
            df_ref[...] = dfb
            gpost_ref[...] += dw
            a_ref[...] = _rms_fwd(h_ref[...], pre_ref[...]).astype(BF16)
            da_acc[...] = jnp.zeros_like(da_acc)

        dg = _dot_nt(df_s[...], w2_ref[...])
        ug = u_ref[0].astype(F32)
        uu = u_ref[1].astype(F32)
        sg = _sigmoid(ug)
        sl = ug * sg
        g_ref[...] = (sl * uu).astype(BF16)
        dug = (dg * uu * (sg * (1.0 + ug * (1.0 - sg)))).astype(BF16)
        duu = (dg * sl).astype(BF16)
        du_ref[0] = dug
        du_ref[1] = duu
        da_acc[...] += _dot_nt(dug, w1g_ref[...]) + _dot_nt(duu, w1u_ref[...])

        @pl.when(k == nk - 1)
        def _():
            dx, dw = _rms_bwd(h_ref[...], pre_ref[...], da_acc[...])
            dh_ref[...] = dho_ref[...] + dx
            gpre_ref[...] += dw

    row = pl.BlockSpec((tT, D), lambda i, k: (i, 0))
    vec = pl.BlockSpec((1, D), lambda i, k: (0, 0))
    return pl.pallas_call(
        body, name=name, grid=(nT, nk),
        in_specs=[row, row, vec, row, vec,
                  pl.BlockSpec((2, tT, ck), lambda i, k: (0, i, k)),
                  pl.BlockSpec((ck, D), lambda i, k: (k, 0)),
                  pl.BlockSpec((D, ck), lambda i, k: (0, k)),
                  pl.BlockSpec((D, ck), lambda i, k: (0, nk + k))],
        out_specs=[pl.BlockSpec((tT, ck), lambda i, k: (i, k)),
                   pl.BlockSpec((2, tT, ck), lambda i, k: (0, i, k)),
                   row, row, row, vec, vec],
        out_shape=[jax.ShapeDtypeStruct((T, FF), BF16),
                   jax.ShapeDtypeStruct((2, T, FF), BF16),
                   jax.ShapeDtypeStruct((T, D), BF16),
                   jax.ShapeDtypeStruct((T, D), BF16),
                   jax.ShapeDtypeStruct((T, D), F32),
                   jax.ShapeDtypeStruct((1, D), F32),
                   jax.ShapeDtypeStruct((1, D), F32)],
        scratch_shapes=[pltpu.VMEM((tT, D), BF16), pltpu.VMEM((tT, D), F32)],
        compiler_params=_params(("arbitrary", "arbitrary")),
    )(dho, f, post_w, h, pre_w, u, w2, w1, w1)


def _matmul_tn(x, dy, *, name):
    T, K = x.shape
    P, _, N = dy.shape
    tT, tK, tN = min(TOKEN_TILE, T), _tile(K, GRAD_TILE_CAP), _tile(N, GRAD_TILE_CAP)
    nt = T // tT

    def body(x_ref, dy_ref, o_ref):
        t = pl.program_id(3)

        @pl.when(t == 0)
        def _():
            o_ref[...] = jnp.zeros_like(o_ref)

        o_ref[...] += _dot_tn(x_ref[...], dy_ref[...])

    return pl.pallas_call(
        body, name=name, grid=(P, K // tK, N // tN, nt),
        in_specs=[pl.BlockSpec((tT, tK), lambda p, a, b, t: (t, a)),
                  pl.BlockSpec((None, tT, tN), lambda p, a, b, t: (p, t, b))],
        out_specs=pl.BlockSpec((None, tK, tN), lambda p, a, b, t: (p, a, b)),
        out_shape=jax.ShapeDtypeStruct((P, K, N), F32),
        compiler_params=_params(("parallel", "parallel", "parallel", "arbitrary")),
    )(x, dy)


def _rms_matmul(h, wn, w, *, name):
    T, D = h.shape
    N = w.shape[1]
    tT, tN = min(TOKEN_TILE, T), _tile(N, 1024)

    def body(h_ref, wn_ref, w_ref, y_ref, a_ref):
        @pl.when(pl.program_id(1) == 0)
        def _():
            a_ref[...] = _rms_fwd(h_ref[...], wn_ref[...]).astype(BF16)

        y_ref[...] = _dot(a_ref[...], w_ref[...]).astype(BF16)

    return pl.pallas_call(
        body, name=name, grid=(T // tT, N // tN),
        in_specs=[pl.BlockSpec((tT, D), lambda i, j: (i, 0)),
                  pl.BlockSpec((1, D), lambda i, j: (0, 0)),
                  pl.BlockSpec((D, tN), lambda i, j: (0, j))],
        out_specs=[pl.BlockSpec((tT, tN), lambda i, j: (i, j)),
                   pl.BlockSpec((tT, D), lambda i, j: (i, 0))],
        out_shape=[jax.ShapeDtypeStruct((T, N), BF16), jax.ShapeDtypeStruct((T, D), BF16)],
        compiler_params=_params(("parallel", "arbitrary")),
    )(h, wn, w)


def _proj_bwd(dproj, w, h, wn, dres, *, name):
    T, D = h.shape
    N = w.shape[1]
    tT, tN = min(TOKEN_TILE, T), _tile(N, 1024)
    nn = N // tN

    def body(dp_ref, w_ref, h_ref, wn_ref, dres_ref, dh_ref, gw_ref, acc):
        i, j = pl.program_id(0), pl.program_id(1)

        @pl.when(jnp.logical_and(i == 0, j == 0))
        def _():
            gw_ref[...] = jnp.zeros_like(gw_ref)

        @pl.when(j == 0)
        def _():
            acc[...] = jnp.zeros_like(acc)

        acc[...] += _dot_nt(dp_ref[...], w_ref[...])

        @pl.when(j == nn - 1)
        def _():
            dx, dw = _rms_bwd(h_ref[...], wn_ref[...], acc[...])
            dh_ref[...] = dres_ref[...] + dx
            gw_ref[...] += dw

    row = pl.BlockSpec((tT, D), lambda i, j: (i, 0))
    vec = pl.BlockSpec((1, D), lambda i, j: (0, 0))
    return pl.pallas_call(
        body, name=name, grid=(T // tT, nn),
        in_specs=[pl.BlockSpec((tT, tN), lambda i, j: (i, j)),
                  pl.BlockSpec((D, tN), lambda i, j: (0, j)), row, vec, row],
        out_specs=[row, vec],
        out_shape=[jax.ShapeDtypeStruct((T, D), F32), jax.ShapeDtypeStruct((1, D), F32)],
        scratch_shapes=[pltpu.VMEM((tT, D), F32)],
        compiler_params=_params(("arbitrary", "arbitrary")),
    )(dproj, w, h, wn, dres)


def _mla_prep_fwd(proj, pos, qn_w, kvn_w, w_uq, w_kv, tab, *, name):
    T = proj.shape[0]
    tT = min(TOKEN_TILE, T)
    a_blk = PROJ_FIXED // AW - 1

    def body(a_ref, pos_ref, qnw_ref, kvnw_ref, wuq_ref, wkv_ref, tab_ref,
             q_ref, k_ref, v_ref, qn_ref, kvn_ref):
        cq = a_ref[:, 0:MLA_Q_RANK].astype(F32)
        ckv = a_ref[:, MLA_Q_RANK:MLA_Q_RANK + MLA_KV_RANK].astype(F32)
        kr = a_ref[:, 640:768].astype(F32)
        qn = _rms_fwd(cq, qnw_ref[...]).astype(BF16)
        kvn = _rms_fwd(ckv, kvnw_ref[...]).astype(BF16)
        qn_ref[...] = qn
        kvn_ref[...] = kvn
        cs = _rope_cs(pos_ref[...], tab_ref)
        q = _dot(qn, wuq_ref[...])
        kv = _dot(kvn, wkv_ref[...])
        krr = _rope(kr, cs, MLA_ROPE // 2)
        for hd in range(MLA_HEADS):
            sl = slice(hd * HP, (hd + 1) * HP)
            q_ref[:, sl] = (_rope(q[:, sl], cs, MLA_ROPE // 2) * ATTN_SCALE).astype(BF16)
            k_ref[:, sl] = (kv[:, sl] + krr).astype(BF16)
        v_ref[...] = kv[:, QW:].astype(BF16)

    def full(r, c):
        return pl.BlockSpec((r, c), lambda i: (0, 0))

    def rows(c):
        return pl.BlockSpec((tT, c), lambda i: (i, 0))

    return pl.pallas_call(
        body, name=name, grid=(T // tT,),
        in_specs=[pl.BlockSpec((tT, AW), lambda i: (i, a_blk)), rows(1),
                  full(1, MLA_Q_RANK), full(1, MLA_KV_RANK),
                  full(MLA_Q_RANK, QW), full(MLA_KV_RANK, 2 * QW), full(8, LANES)],
        out_specs=[rows(QW), rows(QW), rows(QW), rows(MLA_Q_RANK), rows(MLA_KV_RANK)],
        out_shape=[jax.ShapeDtypeStruct((T, QW), BF16)] * 3
        + [jax.ShapeDtypeStruct((T, MLA_Q_RANK), BF16), jax.ShapeDtypeStruct((T, MLA_KV_RANK), BF16)],
        compiler_params=_params(("parallel",)),
    )(proj, pos, qn_w, kvn_w, w_uq, w_kv, tab)


def _mla_prep_bwd(dq, dk, dv, proj, pos, qn_w, kvn_w, w_uq, w_kv, tab, *, name):
    T = proj.shape[0]
    tT = min(TOKEN_TILE, T)
    a_blk = PROJ_FIXED // AW - 1

    def body(dq_ref, dk_ref, dv_ref, a_ref, pos_ref, qnw_ref, kvnw_ref, wuq_ref, wkv_ref, tab_ref,
             da_ref, dql_ref, dkvl_ref, gqn_ref, gkvn_ref):
        @pl.when(pl.program_id(0) == 0)
        def _():
            gqn_ref[...] = jnp.zeros_like(gqn_ref)
            gkvn_ref[...] = jnp.zeros_like(gkvn_ref)

        cs = _rope_cs(pos_ref[...], tab_ref)
        dkr = jnp.zeros((tT, HP), F32)
        for hd in range(MLA_HEADS):
            sl = slice(hd * HP, (hd + 1) * HP)
            dql_ref[:, sl] = (_rope(dq_ref[:, sl], cs, MLA_ROPE // 2, inverse=True) * ATTN_SCALE).astype(BF16)
            dkh = dk_ref[:, sl]
            dkr = dkr + dkh
            dkvl_ref[:, sl] = dkh.astype(BF16)
        dkvl_ref[:, QW:] = dv_ref[...]
        dqn = _dot_nt(dql_ref[...], wuq_ref[...])
        dkvn = _dot_nt(dkvl_ref[...], wkv_ref[...])
        cq = a_ref[:, 0:MLA_Q_RANK].astype(F32)
        ckv = a_ref[:, MLA_Q_RANK:MLA_Q_RANK + MLA_KV_RANK].astype(F32)
        dcq, gq = _rms_bwd(cq, qnw_ref[...], dqn)
        dckv, gkv = _rms_bwd(ckv, kvnw_ref[...], dkvn)
        gqn_ref[...] += gq
        gkvn_ref[...] += gkv
        da_ref[:, 0:MLA_Q_RANK] = dcq.astype(BF16)
        da_ref[:, MLA_Q_RANK:MLA_Q_RANK + MLA_KV_RANK] = dckv.astype(BF16)
        da_ref[:, 640:768] = _rope(dkr, cs, MLA_ROPE // 2, inverse=True).astype(BF16)
        da_ref[:, 768:AW] = jnp.zeros((tT, AW - 768), BF16)

    def full(r, c):
        return pl.BlockSpec((r, c), lambda i: (0, 0))

    def rows(c):
        return pl.BlockSpec((tT, c), lambda i: (i, 0))

    return pl.pallas_call(
        body, name=name, grid=(T // tT,),
        in_specs=[rows(QW), rows(QW), rows(QW), pl.BlockSpec((tT, AW), lambda i: (i, a_blk)), rows(1),
                  full(1, MLA_Q_RANK), full(1, MLA_KV_RANK),
                  full(MLA_Q_RANK, QW), full(MLA_KV_RANK, 2 * QW), full(8, LANES)],
        out_specs=[rows(AW), rows(QW), rows(2 * QW), full(1, MLA_Q_RANK), full(1, MLA_KV_RANK)],
        out_shape=[jax.ShapeDtypeStruct((T, AW), BF16), jax.ShapeDtypeStruct((T, QW), BF16),
                   jax.ShapeDtypeStruct((T, 2 * QW), BF16),
                   jax.ShapeDtypeStruct((1, MLA_Q_RANK), F32), jax.ShapeDtypeStruct((1, MLA_KV_RANK), F32)],
        compiler_params=_params(("arbitrary",)),
    )(dq, dk, dv, proj, pos, qn_w, kvn_w, w_uq, w_kv, tab)


def _flash_fwd(q, k, v, *, name):
    T = q.shape[0]
    H = q.shape[1] // HP
    tq = min(ATTN_TILE, T)
    nq = T // tq

    sub = tq // ATTN_CHAINS

    def body(q_ref, k_ref, v_ref, o_ref, lse_ref):
        qi = pl.program_id(1)
        qs = [q_ref[c * sub:(c + 1) * sub, :] for c in range(ATTN_CHAINS)]

        def update(carry, off, masked):
            kb = k_ref[pl.ds(off, tq), :]
            vb = v_ref[pl.ds(off, tq), :]
            out = []
            for c in range(ATTN_CHAINS):
                m_prev, l_prev, acc = carry[c]
                s = _dot_nt(qs[c], kb)
                if masked:
                    rows = lax.broadcasted_iota(jnp.int32, (sub, tq), 0) + c * sub
                    s = jnp.where(rows >= lax.broadcasted_iota(jnp.int32, (sub, tq), 1), s, NEG)
                m_new = jnp.maximum(m_prev, jnp.max(s, axis=1, keepdims=True))
                alpha = jnp.exp(m_prev - m_new)
                p = jnp.exp(s - m_new)
                out.append((m_new, alpha * l_prev + jnp.sum(p, axis=1, keepdims=True),
                            alpha * acc + _dot(p.astype(BF16), vb)))
            return tuple(out)

        init = tuple((jnp.full((sub, 1), NEG, F32), jnp.zeros((sub, 1), F32), jnp.zeros((sub, HP), F32))
                     for _ in range(ATTN_CHAINS))
        carry = lax.fori_loop(0, qi, lambda j, cr: update(cr, pl.multiple_of(j * tq, tq), False), init)
        carry = update(carry, pl.multiple_of(qi * tq, tq), True)
        for c in range(ATTN_CHAINS):
            m_fin, l_fin, acc = carry[c]
            o_ref[c * sub:(c + 1) * sub, :] = (acc / l_fin).astype(BF16)
            lse_ref[c * sub:(c + 1) * sub, :] = jnp.broadcast_to(m_fin + jnp.log(l_fin), (sub, HP))

    qspec = pl.BlockSpec((tq, HP), lambda h, i: (i, h))
    kspec = pl.BlockSpec((T, HP), lambda h, i: (0, h))
    return pl.pallas_call(
        body, name=name, grid=(H, nq),
        in_specs=[qspec, kspec, kspec], out_specs=[qspec, qspec],
        out_shape=[jax.ShapeDtypeStruct((T, H * HP), BF16), jax.ShapeDtypeStruct((T, H * HP), F32)],
        compiler_params=_params(("parallel", "arbitrary")),
    )(q, k, v)


def _flash_bwd(q, k, v, do, lse, delta, *, name):
    T = q.shape[0]
    H = q.shape[1] // HP
    tq = min(ATTN_TILE, T)
    nq = T // tq
    sub = tq // ATTN_CHAINS

    def body(k_ref, v_ref, q_ref, do_ref, lse_ref, dl_ref, dq_ref, dk_ref, dv_ref):
        ki = pl.program_id(1)

        @pl.when(ki == 0)
        def _():
            dq_ref[...] = jnp.zeros_like(dq_ref)

        kb = k_ref[...]
        vb = v_ref[...]

        def step(carry, j, masked):
            dk_acc, dv_acc = carry
            for c in range(ATTN_CHAINS):
                rows = pl.ds(pl.multiple_of(j * tq + c * sub, sub), sub)
                qb = q_ref[rows, :]
                dob = do_ref[rows, :]
                s = _dot_nt(qb, kb)
                if masked:
                    ri = lax.broadcasted_iota(jnp.int32, (sub, tq), 0) + c * sub
                    s = jnp.where(ri >= lax.broadcasted_iota(jnp.int32, (sub, tq), 1), s, NEG)
                p = jnp.exp(s - lse_ref[rows, 0:1])
                dv_acc = dv_acc + _dot_tn(p.astype(BF16), dob)
                dp = _dot_nt(dob, vb)
                ds = (p * (dp - dl_ref[rows, 0:1])).astype(BF16)
                dk_acc = dk_acc + _dot_tn(ds, qb)
                dq_ref[rows, :] += _dot(ds, kb)
            return dk_acc, dv_acc

        carry = step((jnp.zeros((tq, HP), F32), jnp.zeros((tq, HP), F32)), ki, True)
        dk_acc, dv_acc = lax.fori_loop(ki + 1, nq, lambda j, cr: step(cr, j, False), carry)
        dk_ref[...] = dk_acc
        dv_ref[...] = dv_acc.astype(BF16)

    kspec = pl.BlockSpec((tq, HP), lambda h, j: (j, h))
    full = pl.BlockSpec((T, HP), lambda h, j: (0, h))
    return pl.pallas_call(
        body, name=name, grid=(H, nq),
        in_specs=[kspec, kspec, full, full, full, full], out_specs=[full, kspec, kspec],
        out_shape=[jax.ShapeDtypeStruct((T, H * HP), F32), jax.ShapeDtypeStruct((T, H * HP), F32),
                   jax.ShapeDtypeStruct((T, H * HP), BF16)],
        compiler_params=_params(("parallel", "arbitrary")),
    )(k, v, q, do, lse, delta)


def _ret_consts(cc, hd):
    lg = math.log(1.0 - 2.0 ** (-5.0 - hd))
    diff = (lax.broadcasted_iota(jnp.int32, (cc, cc), 0) - lax.broadcasted_iota(jnp.int32, (cc, cc), 1)).astype(F32)
    decay = jnp.where(diff >= 0, jnp.exp(jnp.maximum(diff, 0.0) * lg), 0.0)
    idx = lax.broadcasted_iota(jnp.int32, (cc, 1), 0).astype(F32)
    zeta = jnp.exp((cc - 1.0 - idx) * lg)
    xi = jnp.exp((idx + 1.0) * lg)
    return decay, zeta, xi, math.exp(cc * lg)


def _ret_fwd(proj, pos, tab, *, name):
    T = proj.shape[0]
    cc = min(RET_TILE, T)
    n = T // cc

    def body(rq_ref, rk_ref, rv_ref, pos_ref, tab_ref, y_ref, yn_ref, rprev_ref, r_s):
        @pl.when(pl.program_id(0) == 0)
        def _():
            r_s[...] = jnp.zeros_like(r_s)

        cs = _rope_cs(pos_ref[...], tab_ref)
        for hd in range(RET_HEADS):
            sl = slice(hd * HP, (hd + 1) * HP)
            decay, zeta, xi, gc = _ret_consts(cc, hd)
            q = _rope(rq_ref[:, sl].astype(F32), cs, RET_DK // 2).astype(BF16)
            kf = _rope(rk_ref[:, sl].astype(F32), cs, RET_DK // 2) * (RET_DK ** -0.5)
            k = kf.astype(BF16)
            v = rv_ref[:, sl]
            r = r_s[hd]
            rprev_ref[0, hd] = r
            inner = (_dot_nt(q, k) * decay).astype(BF16)
            y = _dot(inner, v) + _dot(q, r.astype(BF16)) * xi
            r_s[hd] = r * gc + _dot_tn((kf * zeta).astype(BF16), v)
            y_ref[:, sl] = y
            mu = jnp.mean(y, axis=-1, keepdims=True)
            yc = y - mu
            var = jnp.mean(yc * yc, axis=-1, keepdims=True)
            yn_ref[:, sl] = (yc * lax.rsqrt(var + GN_EPS)).astype(BF16)

    def blk(j):
        return pl.BlockSpec((cc, RW), lambda i: (i, j))

    return pl.pallas_call(
        body, name=name, grid=(n,),
        in_specs=[blk(0), blk(1), blk(2), pl.BlockSpec((cc, 1), lambda i: (i, 0)),
                  pl.BlockSpec((8, LANES), lambda i: (0, 0))],
        out_specs=[blk(0), blk(0), pl.BlockSpec((1, RET_HEADS, HP, RET_DV), lambda i: (i, 0, 0, 0))],
        out_shape=[jax.ShapeDtypeStruct((T, RW), F32), jax.ShapeDtypeStruct((T, RW), BF16),
                   jax.ShapeDtypeStruct((n, RET_HEADS, HP, RET_DV), F32)],
        scratch_shapes=[pltpu.VMEM((RET_HEADS, HP, RET_DV), F32)],
        compiler_params=_params(("arbitrary",)),
    )(proj, proj, proj, pos, tab)


def _ret_bwd(dyn, y, proj, pos, tab, rprev, *, name):
    T = proj.shape[0]
    cc = min(RET_TILE, T)
    n = T // cc

    def body(dyn_ref, y_ref, rq_ref, rk_ref, rv_ref, pos_ref, tab_ref, rprev_ref,
             drq_ref, drk_ref, drv_ref, dr_s):
        @pl.when(pl.program_id(0) == 0)
        def _():
            dr_s[...] = jnp.zeros_like(dr_s)

        cs = _rope_cs(pos_ref[...], tab_ref)
        for hd in range(RET_HEADS):
            sl = slice(hd * HP, (hd + 1) * HP)
            decay, zeta, xi, gc = _ret_consts(cc, hd)
            q = _rope(rq_ref[:, sl].astype(F32), cs, RET_DK // 2).astype(BF16)
            kf = _rope(rk_ref[:, sl].astype(F32), cs, RET_DK // 2) * (RET_DK ** -0.5)
            k = kf.astype(BF16)
            v = rv_ref[:, sl]
            yv = y_ref[:, sl]
            mu = jnp.mean(yv, axis=-1, keepdims=True)
            yc = yv - mu
            rs = lax.rsqrt(jnp.mean(yc * yc, axis=-1, keepdims=True) + GN_EPS)
            yn = yc * rs
            dn = dyn_ref[:, sl]
            dy = rs * (dn - jnp.mean(dn, axis=-1, keepdims=True) - yn * jnp.mean(dn * yn, axis=-1, keepdims=True))
            dyb = dy.astype(BF16)
            dyx = (dy * xi).astype(BF16)
            dr = dr_s[hd]
            drb = dr.astype(BF16)
            inner = (_dot_nt(q, k) * decay).astype(BF16)
            da = (_dot_nt(dyb, v) * decay).astype(BF16)
            dv = _dot_tn(inner, dyb) + _dot((kf * zeta).astype(BF16), drb)
            dq = _dot(da, k) + _dot_nt(dyx, rprev_ref[0, hd].astype(BF16))
            dk = _dot_tn(da, q) + _dot_nt(v, drb) * zeta
            dr_s[hd] = dr * gc + _dot_tn(q, dyx)
            drq_ref[:, sl] = _rope(dq, cs, RET_DK // 2, inverse=True).astype(BF16)
            drk_ref[:, sl] = _rope(dk * (RET_DK ** -0.5), cs, RET_DK // 2, inverse=True).astype(BF16)
            drv_ref[:, sl] = dv.astype(BF16)

    def blk(j):
        return pl.BlockSpec((cc, RW), lambda i: (n - 1 - i, j))

    return pl.pallas_call(
        body, name=name, grid=(n,),
        in_specs=[blk(0), blk(0), blk(0), blk(1), blk(2), pl.BlockSpec((cc, 1), lambda i: (n - 1 - i, 0)),
                  pl.BlockSpec((8, LANES), lambda i: (0, 0)),
                  pl.BlockSpec((1, RET_HEADS, HP, RET_DV), lambda i: (n - 1 - i, 0, 0, 0))],
        out_specs=[blk(0), blk(0), blk(0)],
        out_shape=[jax.ShapeDtypeStruct((T, RW), BF16)] * 3,
        scratch_shapes=[pltpu.VMEM((RET_HEADS, HP, RET_DV), F32)],
        compiler_params=_params(("arbitrary",)),
    )(dyn, y, proj, proj, proj, pos, tab, rprev)


def _merge_fwd(o, yn, proj, gn_w, w_bm, w_br, w_out, h, post_w, *, name):
    T, D = h.shape
    tT = min(MERGE_TILE, T)
    g_blk = PROJ_FIXED // D

    def body(o_ref, yn_ref, rg_ref, gm_ref, gr_ref, gnw_ref, wbm_ref, wbr_ref, wout_ref, h_ref, post_ref,
             omla_ref, oret_ref, m_ref, ho_ref):
        o_mla = _dot(o_ref[...], wbm_ref[...])
        rg = rg_ref[...].astype(F32)
        gated = (rg * _sigmoid(rg) * (yn_ref[...].astype(F32) * gnw_ref[...])).astype(BF16)
        o_ret = _dot(gated, wbr_ref[...])
        omla_ref[...] = o_mla.astype(BF16)
        oret_ref[...] = o_ret.astype(BF16)
        merged = _sigmoid(gm_ref[...].astype(F32)) * o_mla + _sigmoid(gr_ref[...].astype(F32)) * o_ret
        m = _dot(merged.astype(BF16), wout_ref[...])
        m_ref[...] = m
        ho_ref[...] = h_ref[...] + _rms_fwd(m, post_ref[...])

    def full(r, c):
        return pl.BlockSpec((r, c), lambda i: (0, 0))

    def rows(c, j=0):
        return pl.BlockSpec((tT, c), lambda i: (i, j))

    return pl.pallas_call(
        body, name=name, grid=(T // tT,),
        in_specs=[rows(QW), rows(RW), rows(RW, 3), rows(D, g_blk), rows(D, g_blk + 1), full(1, RW),
                  full(QW, D), full(RW, D), full(D, D), rows(D), full(1, D)],
        out_specs=[rows(D), rows(D), rows(D), rows(D)],
        out_shape=[jax.ShapeDtypeStruct((T, D), BF16), jax.ShapeDtypeStruct((T, D), BF16),
                   jax.ShapeDtypeStruct((T, D), F32), jax.ShapeDtypeStruct((T, D), F32)],
        compiler_params=_params(("parallel",)),
    )(o, yn, proj, proj, proj, gn_w, w_bm, w_br, w_out, h, post_w)


def _merge_bwd(dho, m, post_w, omla, oret, proj, yn, gn_w, o, w_out, w_bm, w_br, *, name):
    T, D = dho.shape
    tT = min(MERGE_TILE, T)
    g_blk = PROJ_FIXED // D

    def body(dho_ref, m_ref, post_ref, omla_ref, oret_ref, rg_ref, gm_ref, gr_ref, yn_ref, gnw_ref, o_ref,
             wout_ref, wbm_ref, wbr_ref,
             dm_ref, merged_ref, dgm_ref, dgr_ref, domla_ref, do_ref, delta_ref, doret_ref, gated_ref,
             drg_ref, dyn_ref, gpost_ref, ggn_ref):
        @pl.when(pl.program_id(0) == 0)
        def _():
            gpost_ref[...] = jnp.zeros_like(gpost_ref)
            ggn_ref[...] = jnp.zeros_like(ggn_ref)

        dm, gp = _rms_bwd(m_ref[...], post_ref[...], dho_ref[...])
        gpost_ref[...] += gp
        dmb = dm.astype(BF16)
        dm_ref[...] = dmb
        dmerged = _dot_nt(dmb, wout_ref[...])
        o_mla = omla_ref[...].astype(F32)
        o_ret = oret_ref[...].astype(F32)
        sgm = _sigmoid(gm_ref[...].astype(F32))
        sgr = _sigmoid(gr_ref[...].astype(F32))
        merged_ref[...] = (sgm * o_mla + sgr * o_ret).astype(BF16)
        dgm_ref[...] = (dmerged * o_mla * sgm * (1.0 - sgm)).astype(BF16)
        dgr_ref[...] = (dmerged * o_ret * sgr * (1.0 - sgr)).astype(BF16)
        domla = (dmerged * sgm).astype(BF16)
        domla_ref[...] = domla
        do = _dot_nt(domla, wbm_ref[...])
        do_ref[...] = do.astype(BF16)
        for hd in range(MLA_HEADS):
            sl = slice(hd * HP, (hd + 1) * HP)
            d = jnp.sum(do[:, sl] * o_ref[:, sl].astype(F32), axis=-1, keepdims=True)
            delta_ref[:, sl] = jnp.broadcast_to(d, (tT, HP))
        doret = (dmerged * sgr).astype(BF16)
        doret_ref[...] = doret
        dgated = _dot_nt(doret, wbr_ref[...])
        rg = rg_ref[...].astype(F32)
        sg = _sigmoid(rg)
        srg = rg * sg
        ynv = yn_ref[...].astype(F32)
        yw = ynv * gnw_ref[...]
        gated_ref[...] = (srg * yw).astype(BF16)
        drg_ref[...] = (dgated * yw * (sg * (1.0 + rg * (1.0 - sg)))).astype(BF16)
        dgs = dgated * srg
        dyn_ref[...] = dgs * gnw_ref[...]
        ggn_ref[...] += jnp.sum(dgs * ynv, axis=0, keepdims=True)

    def full(r, c):
        return pl.BlockSpec((r, c), lambda i: (0, 0))

    def rows(c, j=0):
        return pl.BlockSpec((tT, c), lambda i: (i, j))

    return pl.pallas_call(
        body, name=name, grid=(T // tT,),
        in_specs=[rows(D), rows(D), full(1, D), rows(D), rows(D), rows(RW, 3), rows(D, g_blk), rows(D, g_blk + 1),
                  rows(RW), full(1, RW), rows(QW), full(D, D), full(QW, D), full(RW, D)],
        out_specs=[rows(D), rows(D), rows(D), rows(D), rows(D), rows(QW), rows(QW), rows(D), rows(RW),
                   rows(RW), rows(RW), full(1, D), full(1, RW)],
        out_shape=[jax.ShapeDtypeStruct((T, D), BF16)] * 5
        + [jax.ShapeDtypeStruct((T, QW), BF16), jax.ShapeDtypeStruct((T, QW), F32),
           jax.ShapeDtypeStruct((T, D), BF16), jax.ShapeDtypeStruct((T, RW), BF16),
           jax.ShapeDtypeStruct((T, RW), BF16), jax.ShapeDtypeStruct((T, RW), F32),
           jax.ShapeDtypeStruct((1, D), F32), jax.ShapeDtypeStruct((1, RW), F32)],
        compiler_params=_params(("arbitrary",)),
    )(dho, m, post_w, omla, oret, proj, proj, proj, yn, gn_w, o, w_out, w_bm, w_br)


def _mesh_pos():
    return lax.axis_index("x"), lax.axis_index("y"), lax.axis_index("c")


def _all_gather(x_loc, *, name):
    R = x_loc.shape[0]

    def body(x_ref, out_ref, send_sems, recv_sems, local_sem):
        x, y, c = _mesh_pos()
        me, sibling = (x, y, c), (x, y, 1 - c)
        chips = [(1 - x, y), (x, 1 - y), (1 - x, 1 - y)]

        def slot(px, py, pc):
            return out_ref.at[4 * px + 2 * py + pc]

        def copy(k, block, to, src=None):
            return pltpu.make_async_remote_copy(
                src_ref=slot(*block) if src is None else src, dst_ref=slot(*block),
                send_sem=send_sems.at[k], recv_sem=recv_sems.at[k],
                device_id=to, device_id_type=pl.DeviceIdType.MESH)

        mine = pltpu.make_async_copy(x_ref, slot(*me), local_sem)
        mine.start()
        first = [copy(0, me, sibling, src=x_ref)]
        first += [copy(1 + j, me, (*chip, c), src=x_ref) for j, chip in enumerate(chips)]
        for cp in first:
            cp.start()
        passed = [copy(4 + j, (*chip, c), sibling) for j, chip in enumerate(chips)]
        for j, chip in enumerate(chips):
            copy(1 + j, (*chip, c), me).wait_recv()
            passed[j].start()
        copy(0, sibling, me).wait_recv()
        for j, chip in enumerate(chips):
            copy(4 + j, (*chip, 1 - c), me).wait_recv()
        for cp in first + passed:
            cp.wait_send()
        mine.wait()

    return pl.pallas_call(
        body, name=name,
        out_shape=jax.ShapeDtypeStruct((N_DEV, R, LANES), x_loc.dtype),
        in_specs=[pl.BlockSpec(memory_space=pl.ANY)],
        out_specs=pl.BlockSpec(memory_space=pl.ANY),
        scratch_shapes=[pltpu.SemaphoreType.DMA((7,)), pltpu.SemaphoreType.DMA((7,)), pltpu.SemaphoreType.DMA],
    )(x_loc)


def _exchange_grads(g, sv, *, name):
    _, R, _ = g.shape
    Rs = sv.shape[0]

    def body(g_ref, sv_ref, recv_ref, sall_ref, send_sems, recv_sems, local_sems):
        x, y, c = _mesh_pos()
        me = 4 * x + 2 * y + c
        own_big = pltpu.make_async_copy(g_ref.at[me], recv_ref.at[me], local_sems.at[0])
        own_small = pltpu.make_async_copy(sv_ref, sall_ref.at[me], local_sems.at[1])
        own_big.start()
        own_small.start()
        sends, recvs = [], []
        for r in range(1, N_DEV):
            px = 1 - x if r & 4 else x
            py = 1 - y if r & 2 else y
            pc = 1 - c if r & 1 else c
            peer, pidx = (px, py, pc), 4 * px + 2 * py + pc
            kb, ks = r - 1, N_DEV - 1 + r - 1
            sends.append(pltpu.make_async_remote_copy(
                src_ref=g_ref.at[pidx], dst_ref=recv_ref.at[me], send_sem=send_sems.at[kb],
                recv_sem=recv_sems.at[kb], device_id=peer, device_id_type=pl.DeviceIdType.MESH))
            sends.append(pltpu.make_async_remote_copy(
                src_ref=sv_ref, dst_ref=sall_ref.at[me], send_sem=send_sems.at[ks],
                recv_sem=recv_sems.at[ks], device_id=peer, device_id_type=pl.DeviceIdType.MESH))
            recvs.append(pltpu.make_async_remote_copy(
                src_ref=g_ref.at[me], dst_ref=recv_ref.at[pidx], send_sem=send_sems.at[kb],
                recv_sem=recv_sems.at[kb], device_id=peer, device_id_type=pl.DeviceIdType.MESH))
            recvs.append(pltpu.make_async_remote_copy(
                src_ref=sv_ref, dst_ref=sall_ref.at[pidx], send_sem=send_sems.at[ks],
                recv_sem=recv_sems.at[ks], device_id=peer, device_id_type=pl.DeviceIdType.MESH))
        for cp in sends:
            cp.start()
        for cp in recvs:
            cp.wait_recv()
        for cp in sends:
            cp.wait_send()
        own_big.wait()
        own_small.wait()

    return pl.pallas_call(
        body, name=name,
        out_shape=[jax.ShapeDtypeStruct((N_DEV, R, LANES), g.dtype),
                   jax.ShapeDtypeStruct((N_DEV, Rs, LANES), sv.dtype)],
        in_specs=[pl.BlockSpec(memory_space=pl.ANY), pl.BlockSpec(memory_space=pl.ANY)],
        out_specs=[pl.BlockSpec(memory_space=pl.ANY), pl.BlockSpec(memory_space=pl.ANY)],
        scratch_shapes=[pltpu.SemaphoreType.DMA((2 * (N_DEV - 1),)), pltpu.SemaphoreType.DMA((2 * (N_DEV - 1),)),
                        pltpu.SemaphoreType.DMA((2,))],
    )(g, sv)


def _adamw(w, parts, m, v, *, name):
    R = w.shape[0]
    tr = _tile(R, 1024, 16)

    def body(w_ref, p_ref, m_ref, v_ref, g_ref, d_ref, nm_ref, nv_ref):
        g = p_ref[0].astype(F32)
        for j in range(1, N_DEV):
            g = g + p_ref[j].astype(F32)
        g_ref[...] = g
        nm = ADAM_B1 * m_ref[...] + (1.0 - ADAM_B1) * g
        nv = ADAM_B2 * v_ref[...] + (1.0 - ADAM_B2) * (g * g)
        nm_ref[...] = nm
        nv_ref[...] = nv
        m_hat = nm / (1.0 - ADAM_B1 ** ADAM_STEP)
        v_hat = nv / (1.0 - ADAM_B2 ** ADAM_STEP)
        d_ref[...] = -ADAM_LR * (m_hat / (jnp.sqrt(v_hat) + ADAM_EPS) + ADAM_WD * w_ref[...])

    row = pl.BlockSpec((tr, LANES), lambda i: (i, 0))
    return pl.pallas_call(
        body, name=name, grid=(R // tr,),
        in_specs=[row, pl.BlockSpec((N_DEV, tr, LANES), lambda i: (0, i, 0)), row, row],
        out_specs=[row, row, row, row],
        out_shape=[jax.ShapeDtypeStruct((R, LANES), F32)] * 4,
        compiler_params=_params(("parallel",)),
    )(w, parts, m, v)


def _pad_last(a, width):
    return jnp.pad(a, [(0, 0)] * (a.ndim - 1) + [(0, width - a.shape[-1])])


def _flat_rows(a):
    return a.reshape(-1, LANES)


def _pad_rows(a, mult):
    r = (-a.shape[0]) % mult
    return jnp.pad(a, ((0, r), (0, 0))) if r else a


def kernel(x, positions, ffn1_pre_w, ffn1_w1, ffn1_w2, ffn1_post_w, mix_pre_w, w_in, mla_q_norm_w, mla_w_uq, mla_kv_norm_w, mla_w_ukv, ret_gn_w, w_branch_mla, w_branch_ret, w_out, mix_post_w, ffn2_pre_w, ffn2_w1, ffn2_w2, ffn2_post_w, loss_target, m_ffn1_pre_w, m_ffn1_w1, m_ffn1_w2, m_ffn1_post_w, m_mix_pre_w, m_w_in, m_mla_q_norm_w, m_mla_w_uq, m_mla_kv_norm_w, m_mla_w_ukv, m_ret_gn_w, m_w_branch_mla, m_w_branch_ret, m_w_out, m_mix_post_w, m_ffn2_pre_w, m_ffn2_w1, m_ffn2_w2, m_ffn2_post_w, v_ffn1_pre_w, v_ffn1_w1, v_ffn1_w2, v_ffn1_post_w, v_mix_pre_w, v_w_in, v_mla_q_norm_w, v_mla_w_uq, v_mla_kv_norm_w, v_mla_w_ukv, v_ret_gn_w, v_w_branch_mla, v_w_branch_ret, v_w_out, v_mix_post_w, v_ffn2_pre_w, v_ffn2_w1, v_ffn2_w2, v_ffn2_post_w):
    T, D = x.shape[1], x.shape[2]
    h0 = x[0]
    tgt = loss_target[0]
    pos = positions.reshape(T, 1).astype(F32)

    big = [("ffn1_w1", ffn1_w1, m_ffn1_w1, v_ffn1_w1, "col"), ("ffn1_w2", ffn1_w2, m_ffn1_w2, v_ffn1_w2, "row"),
           ("w_in", w_in, m_w_in, v_w_in, "col"), ("mla_w_uq", mla_w_uq, m_mla_w_uq, v_mla_w_uq, "col"),
           ("mla_w_ukv", mla_w_ukv, m_mla_w_ukv, v_mla_w_ukv, "col"),
           ("w_branch_mla", w_branch_mla, m_w_branch_mla, v_w_branch_mla, "col"),
           ("w_branch_ret", w_branch_ret, m_w_branch_ret, v_w_branch_ret, "col"),
           ("w_out", w_out, m_w_out, v_w_out, "row"),
           ("ffn2_w1", ffn2_w1, m_ffn2_w1, v_ffn2_w1, "col"), ("ffn2_w2", ffn2_w2, m_ffn2_w2, v_ffn2_w2, "row")]
    small = [("ffn1_pre_w", ffn1_pre_w, m_ffn1_pre_w, v_ffn1_pre_w), ("ffn1_post_w", ffn1_post_w, m_ffn1_post_w, v_ffn1_post_w),
             ("mix_pre_w", mix_pre_w, m_mix_pre_w, v_mix_pre_w), ("mla_q_norm_w", mla_q_norm_w, m_mla_q_norm_w, v_mla_q_norm_w),
             ("mla_kv_norm_w", mla_kv_norm_w, m_mla_kv_norm_w, v_mla_kv_norm_w), ("ret_gn_w", ret_gn_w, m_ret_gn_w, v_ret_gn_w),
             ("mix_post_w", mix_post_w, m_mix_post_w, v_mix_post_w), ("ffn2_pre_w", ffn2_pre_w, m_ffn2_pre_w, v_ffn2_pre_w),
             ("ffn2_post_w", ffn2_post_w, m_ffn2_post_w, v_ffn2_post_w)]

    offs, off = {}, 0
    for nm, w, _, _, _ in big:
        rows = w[0].size // LANES
        assert w[0].size % (16 * LANES) == 0, (nm, w.shape)
        offs[nm] = (off, rows)
        off += rows
    flat_w = jnp.concatenate([_flat_rows(w[0]) for _, w, _, _, _ in big], axis=0)
    flat_m = jnp.concatenate([_flat_rows(a[0]) for _, _, a, _, _ in big], axis=0)
    flat_v = jnp.concatenate([_flat_rows(a[0]) for _, _, _, a, _ in big], axis=0)

    gathered = _all_gather(flat_w.astype(BF16), name="weights_all_gather")

    def full_weight(nm, w, kind):
        o, rows = offs[nm]
        a, b = w.shape[1], w.shape[2]
        blk = gathered[:, o:o + rows].reshape(N_DEV, a, b)
        if kind == "col":
            return blk.transpose(1, 0, 2).reshape(a, N_DEV * b)
        return blk.reshape(N_DEV * a, b)

    fw = {nm: full_weight(nm, w, kind) for nm, w, _, _, kind in big}

    wi = fw["w_in"]
    cq_w, ckv_w, kr_w = wi[:, 0:384], wi[:, 384:640], wi[:, 640:672]
    rq_w, rk_w = wi[:, 672:928], wi[:, 928:1184]
    rv_w, rg_w = wi[:, 1184:1696], wi[:, 1696:2208]
    gm_w, gr_w = wi[:, 2208:2208 + D], wi[:, 2208 + D:2208 + 2 * D]
    zer = lambda n: jnp.zeros((D, n), BF16)
    head_pad = lambda a, h: _pad_last(a.reshape(a.shape[0], h, -1), HP).reshape(a.shape[0], h * HP)
    w_in_p = jnp.concatenate([head_pad(rq_w, RET_HEADS), head_pad(rk_w, RET_HEADS), rv_w, rg_w,
                              cq_w, ckv_w, zer(MLA_NOPE), kr_w, zer(HP - MLA_NOPE - MLA_ROPE), zer(AW - 768),
                              gm_w, gr_w], axis=1)
    w_uq_p = head_pad(fw["mla_w_uq"], MLA_HEADS)
    ukv = fw["mla_w_ukv"].reshape(MLA_KV_RANK, MLA_HEADS, MLA_NOPE + MLA_V)
    w_kv_p = jnp.concatenate([_pad_last(ukv[:, :, :MLA_NOPE], HP).reshape(MLA_KV_RANK, QW),
                              _pad_last(ukv[:, :, MLA_NOPE:], HP).reshape(MLA_KV_RANK, QW)], axis=1)
    w_bm_p = jnp.pad(fw["w_branch_mla"].reshape(MLA_HEADS, MLA_V, D), ((0, 0), (0, HP - MLA_V), (0, 0))).reshape(QW, D)
    w_br, w_o = fw["w_branch_ret"], fw["w_out"]
    tab_mla = _rope_table(MLA_NOPE, MLA_ROPE // 2)
    tab_ret = _rope_table(0, RET_DK // 2)

    u1, f1, h1 = _ffn_fwd(h0, ffn1_pre_w, fw["ffn1_w1"], fw["ffn1_w2"], ffn1_post_w, None, name="ffn1_fwd")
    proj, a1 = _rms_matmul(h1, mix_pre_w, w_in_p, name="mixer_in_proj")
    q, k, v, qn, kvn = _mla_prep_fwd(proj, pos, mla_q_norm_w, mla_kv_norm_w, w_uq_p, w_kv_p, tab_mla, name="mla_prep_fwd")
    o, lse = _flash_fwd(q, k, v, name="mla_attn_fwd")
    ypre, yn, rprev = _ret_fwd(proj, pos, tab_ret, name="retention_fwd")
    omla, oret, m, h2 = _merge_fwd(o, yn, proj, ret_gn_w, w_bm_p, w_br, w_o, h1, mix_post_w, name="merge_fwd")
    u2, f2, _, dy, lossp = _ffn_fwd(h2, ffn2_pre_w, fw["ffn2_w1"], fw["ffn2_w2"], ffn2_post_w, tgt, name="ffn2_fwd_loss")
    loss = lax.psum(jnp.sum(lossp[::8, 0]), ("x", "y", "c"))

    g2, du2, df2, a2, dh2, gpost2, gpre2 = _ffn_bwd(dy, f2, ffn2_post_w, h2, ffn2_pre_w, u2, fw["ffn2_w2"], fw["ffn2_w1"], name="ffn2_bwd")
    dw1b = _matmul_tn(a2, du2, name="ffn2_dw1")
    dw2b = _matmul_tn(g2, df2[None], name="ffn2_dw2")[0]
    (dmb, merged, dgm, dgr, domla, do, delta, doret, gated, drg, dyn, gpostm, ggn) = _merge_bwd(
        dh2, m, mix_post_w, omla, oret, proj, yn, ret_gn_w, o, w_o, w_bm_p, w_br, name="merge_bwd")
    dw_out = _matmul_tn(merged, dmb[None], name="dw_out")[0]
    dw_bm_p = _matmul_tn(o, domla[None], name="dw_branch_mla")[0]
    dw_br = _matmul_tn(gated, doret[None], name="dw_branch_ret")[0]
    dq, dk, dv = _flash_bwd(q, k, v, do, lse, delta, name="mla_attn_bwd")
    da, dql, dkvl, gqn, gkvn = _mla_prep_bwd(dq, dk, dv, proj, pos, mla_q_norm_w, mla_kv_norm_w, w_uq_p, w_kv_p, tab_mla, name="mla_prep_bwd")
    dw_uq_p = _matmul_tn(qn, dql[None], name="dw_uq")[0]
    dw_kv_p = _matmul_tn(kvn, dkvl[None], name="dw_ukv")[0]
    drq, drk, drv = _ret_bwd(dyn, ypre, proj, pos, tab_ret, rprev, name="retention_bwd")
    dproj = jnp.concatenate([drq, drk, drv, drg, da, dgm, dgr], axis=1)
    dw_in_p = _matmul_tn(a1, dproj[None], name="dw_in")[0]
    dh1, gmixpre = _proj_bwd(dproj, w_in_p, h1, mix_pre_w, dh2, name="mixer_in_bwd")
    g1, du1, df1, a0, dx, gpost1, gpre1 = _ffn_bwd(dh1, f1, ffn1_post_w, h0, ffn1_pre_w, u1, fw["ffn1_w2"], fw["ffn1_w1"], name="ffn1_bwd")
    dw1a = _matmul_tn(a0, du1, name="ffn1_dw1")
    dw2a = _matmul_tn(g1, df1[None], name="ffn1_dw2")[0]

    FF = dw2a.shape[0]
    unhead = lambda a, h, wd: a.reshape(a.shape[0], h, HP)[:, :, :wd].reshape(a.shape[0], h * wd)
    c0 = 4 * RW
    dw_in = jnp.concatenate([
        dw_in_p[:, c0:c0 + 384], dw_in_p[:, c0 + 384:c0 + 640], dw_in_p[:, c0 + 640 + MLA_NOPE:c0 + 640 + MLA_NOPE + MLA_ROPE],
        unhead(dw_in_p[:, 0:RW], RET_HEADS, RET_DK), unhead(dw_in_p[:, RW:2 * RW], RET_HEADS, RET_DK),
        dw_in_p[:, 2 * RW:3 * RW], dw_in_p[:, 3 * RW:4 * RW],
        dw_in_p[:, PROJ_FIXED:PROJ_FIXED + D], dw_in_p[:, PROJ_FIXED + D:PROJ_FIXED + 2 * D]], axis=1)
    dw_uq = unhead(dw_uq_p, MLA_HEADS, MLA_NOPE + MLA_ROPE)
    dkp = dw_kv_p[:, :QW].reshape(MLA_KV_RANK, MLA_HEADS, HP)[:, :, :MLA_NOPE]
    dvp = dw_kv_p[:, QW:].reshape(MLA_KV_RANK, MLA_HEADS, HP)[:, :, :MLA_V]
    dw_ukv = jnp.concatenate([dkp, dvp], axis=2).reshape(MLA_KV_RANK, MLA_HEADS * (MLA_NOPE + MLA_V))
    dw_bm = dw_bm_p.reshape(MLA_HEADS, HP, D)[:, :MLA_V].reshape(MLA_HEADS * MLA_V, D)

    def w1_shards(dw):
        return dw.reshape(2, D, 4, FF // 4).transpose(0, 2, 1, 3).reshape(N_DEV, -1, LANES)

    def col_shards(dw):
        a, b = dw.shape
        return dw.reshape(a, N_DEV, b // N_DEV).transpose(1, 0, 2).reshape(N_DEV, -1, LANES)

    def row_shards(dw):
        return dw.reshape(N_DEV, -1, LANES)

    grads = {"ffn1_w1": w1_shards(dw1a), "ffn1_w2": row_shards(dw2a), "w_in": col_shards(dw_in),
             "mla_w_uq": col_shards(dw_uq), "mla_w_ukv": col_shards(dw_ukv), "w_branch_mla": col_shards(dw_bm),
             "w_branch_ret": col_shards(dw_br), "w_out": row_shards(dw_out),
             "ffn2_w1": w1_shards(dw1b), "ffn2_w2": row_shards(dw2b)}
    g_all = jnp.concatenate([grads[nm] for nm, *_ in big], axis=1).astype(BF16)

    small_g = {"ffn1_pre_w": gpre1, "ffn1_post_w": gpost1, "mix_pre_w": gmixpre, "mla_q_norm_w": gqn,
               "mla_kv_norm_w": gkvn, "ret_gn_w": ggn, "mix_post_w": gpostm, "ffn2_pre_w": gpre2, "ffn2_post_w": gpost2}

    def small_flat(arrs):
        a = jnp.concatenate([z.reshape(-1) for z in arrs])
        a = jnp.pad(a, (0, (-a.size) % (8 * LANES)))
        return a.reshape(-1, LANES)

    sv = small_flat([small_g[nm] for nm, *_ in small])
    recv, sall = _exchange_grads(g_all, sv, name="grads_exchange")

    gb, db, nmb, nvb = _adamw(flat_w, recv, flat_m, flat_v, name="adamw_sharded")
    gs, ds, nms, nvs = _adamw(small_flat([w for _, w, _, _ in small]), sall,
                              small_flat([a for _, _, a, _ in small]), small_flat([a for _, _, _, a in small]),
                              name="adamw_replicated")

    def split_big(flat):
        out = {}
        for nm, w, _, _, _ in big:
            o_, rows = offs[nm]
            out[nm] = flat[o_:o_ + rows].reshape(w.shape)
        return out

    def split_small(flat):
        out, o_ = {}, 0
        fl = flat.reshape(-1)
        for nm, w, _, _ in small:
            out[nm] = fl[o_:o_ + w.size].reshape(w.shape)
            o_ += w.size
        return out

    order = ["ffn1_pre_w", "ffn1_w1", "ffn1_w2", "ffn1_post_w", "mix_pre_w", "w_in", "mla_q_norm_w", "mla_w_uq",
             "mla_kv_norm_w", "mla_w_ukv", "ret_gn_w", "w_branch_mla", "w_branch_ret", "w_out", "mix_post_w",
             "ffn2_pre_w", "ffn2_w1", "ffn2_w2", "ffn2_post_w"]
    outs = [loss, dx[None]]
    for fb, fs in ((gb, gs), (db, ds), (nmb, nms), (nvb, nvs)):
        both = {**split_big(fb), **split_small(fs)}
        outs += [both[nm] for nm in order]
    return tuple(outs)
```

```python
import math

import numpy as np
import jax
import jax.numpy as jnp
from jax import lax
from jax.experimental import pallas as pl
from jax.experimental.pallas import tpu as pltpu

F32, BF16 = jnp.float32, jnp.bfloat16

MLA_HEADS, MLA_NOPE, MLA_ROPE, MLA_V = 8, 64, 32, 64
MLA_Q_RANK, MLA_KV_RANK = 384, 256
RET_HEADS, RET_DK, RET_DV = 4, 64, 128
ROPE_BASE, NORM_EPS, GN_EPS = 10000.0, 1e-6, 1e-6
ADAM_LR, ADAM_B1, ADAM_B2, ADAM_EPS, ADAM_WD, ADAM_STEP = 0.001, 0.9, 0.999, 1e-08, 0.01, 10
ATTN_SCALE = 1.0 / math.sqrt(MLA_NOPE + MLA_ROPE)

N_DEV = 8
LANES = 128
HP = LANES
QW = MLA_HEADS * HP
RW = RET_HEADS * HP
AW = 1024
PROJ_FIXED = 4 * RW + AW
NEG = -1e30

TOKEN_TILE = 512
ATTN_TILE = 512
ATTN_CHAINS = 2
RET_TILE = 256
FF_TILE_CAP = 512
GRAD_TILE_CAP = 1408
MERGE_TILE = 256
VMEM_LIMIT = 56 * 1024 * 1024


def _tile(n, cap, mult=LANES):
    if n <= cap:
        return n
    best = None
    for t in range(mult, cap + 1, mult):
        if n % t == 0:
            best = t
    assert best is not None, (n, cap, mult)
    return best


def _params(sem):
    return pltpu.CompilerParams(dimension_semantics=sem, vmem_limit_bytes=VMEM_LIMIT)


def _dot(a, b):
    return lax.dot_general(a, b, (((1,), (0,)), ((), ())), preferred_element_type=F32)


def _dot_nt(a, b):
    return lax.dot_general(a, b, (((1,), (1,)), ((), ())), preferred_element_type=F32)


def _dot_tn(a, b):
    return lax.dot_general(a, b, (((0,), (0,)), ((), ())), preferred_element_type=F32)


def _sigmoid(x):
    return 1.0 / (1.0 + jnp.exp(-x))


def _rms_fwd(x, w):
    r = lax.rsqrt(jnp.mean(x * x, axis=-1, keepdims=True) + NORM_EPS)
    return x * r * w


def _rms_bwd(x, w, dy):
    r = lax.rsqrt(jnp.mean(x * x, axis=-1, keepdims=True) + NORM_EPS)
    xh = x * r
    g = dy * w
    dx = r * (g - xh * jnp.mean(g * xh, axis=-1, keepdims=True))
    return dx, jnp.sum(dy * xh, axis=0, keepdims=True)


def _rope_table(first, half):
    inv = (np.float32(ROPE_BASE) ** (-(np.arange(half, dtype=np.float32) / np.float32(half)))).astype(np.float32)
    tab = np.zeros((8, LANES), np.float32)
    tab[0, first:first + half] = inv
    tab[0, first + half:first + 2 * half] = inv
    tab[1, first:first + half] = -1.0
    tab[2, first + half:first + 2 * half] = 1.0
    return jnp.asarray(tab)


def _rope_cs(pos, tab_ref):
    ang = pos * tab_ref[0:1, :]
    s = jnp.sin(ang)
    return jnp.cos(ang), s * tab_ref[1:2, :], s * tab_ref[2:3, :]


def _rope(x, cs, half, inverse=False):
    c, s1, s2 = cs
    a = pltpu.roll(x, LANES - half, 1) * s1 + pltpu.roll(x, half, 1) * s2
    return x * c - a if inverse else x * c + a


def _ffn_fwd(h, pre_w, w1, w2, post_w, target, *, name):
    T, D = h.shape
    nk, ck = w2.shape[0], w2.shape[1]
    tT = min(TOKEN_TILE, T)
    nT = T // tT
    with_loss = target is not None

    def body(*refs):
        if with_loss:
            (h_ref, pre_ref, w1g_ref, w1u_ref, w2_ref, post_ref, tgt_ref,
             u_ref, f_ref, ho_ref, dy_ref, loss_ref, a_s, acc) = refs
        else:
            (h_ref, pre_ref, w1g_ref, w1u_ref, w2_ref, post_ref,
             u_ref, f_ref, ho_ref, a_s, acc) = refs
        k = pl.program_id(1)

        @pl.when(k == 0)
        def _():
            a_s[...] = _rms_fwd(h_ref[...], pre_ref[...]).astype(BF16)
            acc[...] = jnp.zeros_like(acc)

        a = a_s[...]
        ug = _dot(a, w1g_ref[...])
        uu = _dot(a, w1u_ref[...])
        u_ref[0] = ug.astype(BF16)
        u_ref[1] = uu.astype(BF16)
        acc[...] += _dot((ug * _sigmoid(ug) * uu).astype(BF16), w2_ref[...])

        @pl.when(k == nk - 1)
        def _():
            f = acc[...]
            f_ref[...] = f
            ho = h_ref[...] + 0.5 * _rms_fwd(f, post_ref[...])
            ho_ref[...] = ho
            if with_loss:
                e = ho - tgt_ref[...]
                dy_ref[...] = e * (1.0 / D)
                loss_ref[...] = jnp.full(loss_ref.shape, (0.5 / D) * jnp.sum(e * e), F32)

    row = pl.BlockSpec((tT, D), lambda i, k: (i, 0))
    vec = pl.BlockSpec((1, D), lambda i, k: (0, 0))
    in_specs = [row, vec,
                pl.BlockSpec((None, D, ck), lambda i, k: (k, 0, 0)),
                pl.BlockSpec((None, D, ck), lambda i, k: (nk + k, 0, 0)),
                pl.BlockSpec((None, ck, D), lambda i, k: (k, 0, 0)),
                vec]
    out_shape = [jax.ShapeDtypeStruct((2, nk, T, ck), BF16),
                 jax.ShapeDtypeStruct((T, D), F32),
                 jax.ShapeDtypeStruct((T, D), F32)]
    out_specs = [pl.BlockSpec((2, None, tT, ck), lambda i, k: (0, k, i, 0)), row, row]
    args = [h, pre_w, w1, w1, w2, post_w]
    if with_loss:
        in_specs.append(row)
        args.append(target)
        out_shape += [jax.ShapeDtypeStruct((T, D), F32), jax.ShapeDtypeStruct((nT * 8, LANES), F32)]
        out_specs += [row, pl.BlockSpec((8, LANES), lambda i, k: (i, 0))]
    return pl.pallas_call(
        body, name=name, grid=(nT, nk), in_specs=in_specs, out_specs=out_specs, out_shape=out_shape,
        scratch_shapes=[pltpu.VMEM((tT, D), BF16), pltpu.VMEM((tT, D), F32)],
        compiler_params=_params(("parallel", "arbitrary")),
    )(*args)


def _ffn_bwd(dho, f, post_w, h, pre_w, u, w2, w1, *, name):
    T, D = h.shape
    nk, ck = w2.shape[0], w2.shape[1]
    tT = min(TOKEN_TILE, T)
    nT = T // tT

    def body(dho_ref, f_ref, post_ref, h_ref, pre_ref, u_ref, w2_ref, w1g_ref, w1u_ref,
             g_ref, du_ref, df_ref, a_ref, dh_ref, gpost_ref, gpre_ref, df_s, da_acc):
        i, k = pl.program_id(0), pl.program_id(1)

        @pl.when(jnp.logical_and(i == 0, k == 0))
        def _():
            gpost_ref[...] = jnp.zeros_like(gpost_ref)
            gpre_ref[...] = jnp.zeros_like(gpre_ref)

        @pl.when(k == 0)
        def _():
            dx, dw = _rms_bwd(f_ref[...], post_ref[...], 0.5 * dho_ref[...])
            dfb = dx.astype(BF16)
            df_s[...] = dfb
            df_ref[...] = dfb
            gpost_ref[...] += dw
            a_ref[...] = _rms_fwd(h_ref[...], pre_ref[...]).astype(BF16)
            da_acc[...] = jnp.zeros_like(da_acc)

        dg = _dot_nt(df_s[...], w2_ref[...])
        ug = u_ref[0].astype(F32)
        uu = u_ref[1].astype(F32)
        sg = _sigmoid(ug)
        sl = ug * sg
        g_ref[...] = (sl * uu).astype(BF16)
        dug = (dg * uu * (sg * (1.0 + ug * (1.0 - sg)))).astype(BF16)
        duu = (dg * sl).astype(BF16)
        du_ref[0] = dug
        du_ref[1] = duu
        da_acc[...] += _dot_nt(dug, w1g_ref[...]) + _dot_nt(duu, w1u_ref[...])

        @pl.when(k == nk - 1)
        def _():
            dx, dw = _rms_bwd(h_ref[...], pre_ref[...], da_acc[...])
            dh_ref[...] = dho_ref[...] + dx
            gpre_ref[...] += dw

    row = pl.BlockSpec((tT, D), lambda i, k: (i, 0))
    vec = pl.BlockSpec((1, D), lambda i, k: (0, 0))
    return pl.pallas_call(
        body, name=name, grid=(nT, nk),
        in_specs=[row, row, vec, row, vec,
                  pl.BlockSpec((2, None, tT, ck), lambda i, k: (0, k, i, 0)),
                  pl.BlockSpec((None, ck, D), lambda i, k: (k, 0, 0)),
                  pl.BlockSpec((None, D, ck), lambda i, k: (k, 0, 0)),
                  pl.BlockSpec((None, D, ck), lambda i, k: (nk + k, 0, 0))],
        out_specs=[pl.BlockSpec((None, tT, ck), lambda i, k: (k, i, 0)),
                   pl.BlockSpec((2, None, tT, ck), lambda i, k: (0, k, i, 0)),
                   row, row, row, vec, vec],
        out_shape=[jax.ShapeDtypeStruct((nk, T, ck), BF16),
                   jax.ShapeDtypeStruct((2, nk, T, ck), BF16),
                   jax.ShapeDtypeStruct((T, D), BF16),
                   jax.ShapeDtypeStruct((T, D), BF16),
                   jax.ShapeDtypeStruct((T, D), F32),
                   jax.ShapeDtypeStruct((1, D), F32),
                   jax.ShapeDtypeStruct((1, D), F32)],
        scratch_shapes=[pltpu.VMEM((tT, D), BF16), pltpu.VMEM((tT, D), F32)],
        compiler_params=_params(("arbitrary", "arbitrary")),
    )(dho, f, post_w, h, pre_w, u, w2, w1, w1)


def _matmul_tn(x, dy, *, name):
    Px, T, K = x.shape
    Py, _, N = dy.shape
    P = max(Px, Py)
    tT, tK, tN = min(TOKEN_TILE, T), _tile(K, GRAD_TILE_CAP), _tile(N, GRAD_TILE_CAP)
    nt = T // tT

    def body(x_ref, dy_ref, o_ref, acc):
        t = pl.program_id(3)

        @pl.when(t == 0)
        def _():
            acc[...] = jnp.zeros_like(acc)

        acc[...] += _dot_tn(x_ref[...], dy_ref[...])

        @pl.when(t == nt - 1)
        def _():
            o_ref[...] = acc[...].astype(BF16)

    return pl.pallas_call(
        body, name=name, grid=(P, K // tK, N // tN, nt),
        in_specs=[pl.BlockSpec((None, tT, tK), lambda p, a, b, t: (p if Px > 1 else 0, t, a)),
                  pl.BlockSpec((None, tT, tN), lambda p, a, b, t: (p if Py > 1 else 0, t, b))],
        out_specs=pl.BlockSpec((None, tK, tN), lambda p, a, b, t: (p, a, b)),
        out_shape=jax.ShapeDtypeStruct((P, K, N), BF16),
        scratch_shapes=[pltpu.VMEM((tK, tN), F32)],
        compiler_params=_params(("parallel", "parallel", "parallel", "arbitrary")),
    )(x, dy)


def _rms_matmul(h, wn, w, *, name):
    T, D = h.shape
    N = w.shape[1]
    tT, tN = min(TOKEN_TILE, T), _tile(N, 1024)

    def body(h_ref, wn_ref, w_ref, y_ref, a_ref):
        @pl.when(pl.program_id(1) == 0)
        def _():
            a_ref[...] = _rms_fwd(h_ref[...], wn_ref[...]).astype(BF16)

        y_ref[...] = _dot(a_ref[...], w_ref[...]).astype(BF16)

    return pl.pallas_call(
        body, name=name, grid=(T // tT, N // tN),
        in_specs=[pl.BlockSpec((tT, D), lambda i, j: (i, 0)),
                  pl.BlockSpec((1, D), lambda i, j: (0, 0)),
                  pl.BlockSpec((D, tN), lambda i, j: (0, j))],
        out_specs=[pl.BlockSpec((tT, tN), lambda i, j: (i, j)),
                   pl.BlockSpec((tT, D), lambda i, j: (i, 0))],
        out_shape=[jax.ShapeDtypeStruct((T, N), BF16), jax.ShapeDtypeStruct((T, D), BF16)],
        compiler_params=_params(("parallel", "arbitrary")),
    )(h, wn, w)


def _proj_bwd(dproj, w, h, wn, dres, *, name):
    T, D = h.shape
    N = w.shape[1]
    tT, tN = min(TOKEN_TILE, T), _tile(N, 1024)
    nn = N // tN

    def body(dp_ref, w_ref, h_ref, wn_ref, dres_ref, dh_ref, gw_ref, acc):
        i, j = pl.program_id(0), pl.program_id(1)

        @pl.when(jnp.logical_and(i == 0, j == 0))
        def _():
            gw_ref[...] = jnp.zeros_like(gw_ref)

        @pl.when(j == 0)
        def _():
            acc[...] = jnp.zeros_like(acc)

        acc[...] += _dot_nt(dp_ref[...], w_ref[...])

        @pl.when(j == nn - 1)
        def _():
            dx, dw = _rms_bwd(h_ref[...], wn_ref[...], acc[...])
            dh_ref[...] = dres_ref[...] + dx
            gw_ref[...] += dw

    row = pl.BlockSpec((tT, D), lambda i, j: (i, 0))
    vec = pl.BlockSpec((1, D), lambda i, j: (0, 0))
    return pl.pallas_call(
        body, name=name, grid=(T // tT, nn),
        in_specs=[pl.BlockSpec((tT, tN), lambda i, j: (i, j)),
                  pl.BlockSpec((D, tN), lambda i, j: (0, j)), row, vec, row],
        out_specs=[row, vec],
        out_shape=[jax.ShapeDtypeStruct((T, D), F32), jax.ShapeDtypeStruct((1, D), F32)],
        scratch_shapes=[pltpu.VMEM((tT, D), F32)],
        compiler_params=_params(("arbitrary", "arbitrary")),
    )(dproj, w, h, wn, dres)


def _mla_prep_fwd(proj, pos, qn_w, kvn_w, w_uq, w_kv, tab, *, name):
    T = proj.shape[0]
    tT = min(TOKEN_TILE, T)
    a_blk = PROJ_FIXED // AW - 1

    def body(a_ref, pos_ref, qnw_ref, kvnw_ref, wuq_ref, wkv_ref, tab_ref,
             q_ref, k_ref, v_ref, qn_ref, kvn_ref):
        cq = a_ref[:, 0:MLA_Q_RANK].astype(F32)
        ckv = a_ref[:, MLA_Q_RANK:MLA_Q_RANK + MLA_KV_RANK].astype(F32)
        kr = a_ref[:, 640:768].astype(F32)
        qn = _rms_fwd(cq, qnw_ref[...]).astype(BF16)
        kvn = _rms_fwd(ckv, kvnw_ref[...]).astype(BF16)
        qn_ref[...] = qn
        kvn_ref[...] = kvn
        cs = _rope_cs(pos_ref[...], tab_ref)
        q = _dot(qn, wuq_ref[...])
        kv = _dot(kvn, wkv_ref[...])
        krr = _rope(kr, cs, MLA_ROPE // 2)
        for hd in range(MLA_HEADS):
            sl = slice(hd * HP, (hd + 1) * HP)
            q_ref[:, sl] = (_rope(q[:, sl], cs, MLA_ROPE // 2) * ATTN_SCALE).astype(BF16)
            k_ref[:, sl] = (kv[:, sl] + krr).astype(BF16)
        v_ref[...] = kv[:, QW:].astype(BF16)

    def full(r, c):
        return pl.BlockSpec((r, c), lambda i: (0, 0))

    def rows(c):
        return pl.BlockSpec((tT, c), lambda i: (i, 0))

    return pl.pallas_call(
        body, name=name, grid=(T // tT,),
        in_specs=[pl.BlockSpec((tT, AW), lambda i: (i, a_blk)), rows(1),
                  full(1, MLA_Q_RANK), full(1, MLA_KV_RANK),
                  full(MLA_Q_RANK, QW), full(MLA_KV_RANK, 2 * QW), full(8, LANES)],
        out_specs=[rows(QW), rows(QW), rows(QW), rows(MLA_Q_RANK), rows(MLA_KV_RANK)],
        out_shape=[jax.ShapeDtypeStruct((T, QW), BF16)] * 3
        + [jax.ShapeDtypeStruct((T, MLA_Q_RANK), BF16), jax.ShapeDtypeStruct((T, MLA_KV_RANK), BF16)],
        compiler_params=_params(("parallel",)),
    )(proj, pos, qn_w, kvn_w, w_uq, w_kv, tab)


def _mla_prep_bwd(dq, dk, dv, proj, pos, qn_w, kvn_w, w_uq, w_kv, tab, *, name):
    T = proj.shape[0]
    tT = min(TOKEN_TILE, T)
    a_blk = PROJ_FIXED // AW - 1

    def body(dq_ref, dk_ref, dv_ref, a_ref, pos_ref, qnw_ref, kvnw_ref, wuq_ref, wkv_ref, tab_ref,
             da_ref, dql_ref, dkvl_ref, gqn_ref, gkvn_ref):
        @pl.when(pl.program_id(0) == 0)
        def _():
            gqn_ref[...] = jnp.zeros_like(gqn_ref)
            gkvn_ref[...] = jnp.zeros_like(gkvn_ref)

        cs = _rope_cs(pos_ref[...], tab_ref)
        dkr = jnp.zeros((tT, HP), F32)
        for hd in range(MLA_HEADS):
            sl = slice(hd * HP, (hd + 1) * HP)
            dql_ref[:, sl] = (_rope(dq_ref[:, sl], cs, MLA_ROPE // 2, inverse=True) * ATTN_SCALE).astype(BF16)
            dkh = dk_ref[:, sl]
            dkr = dkr + dkh
            dkvl_ref[:, sl] = dkh.astype(BF16)
        dkvl_ref[:, QW:] = dv_ref[...]
        dqn = _dot_nt(dql_ref[...], wuq_ref[...])
        dkvn = _dot_nt(dkvl_ref[...], wkv_ref[...])
        cq = a_ref[:, 0:MLA_Q_RANK].astype(F32)
        ckv = a_ref[:, MLA_Q_RANK:MLA_Q_RANK + MLA_KV_RANK].astype(F32)
        dcq, gq = _rms_bwd(cq, qnw_ref[...], dqn)
        dckv, gkv = _rms_bwd(ckv, kvnw_ref[...], dkvn)
        gqn_ref[...] += gq
        gkvn_ref[...] += gkv
        da_ref[:, 0:MLA_Q_RANK] = dcq.astype(BF16)
        da_ref[:, MLA_Q_RANK:MLA_Q_RANK + MLA_KV_RANK] = dckv.astype(BF16)
        da_ref[:, 640:768] = _rope(dkr, cs, MLA_ROPE // 2, inverse=True).astype(BF16)
        da_ref[:, 768:AW] = jnp.zeros((tT, AW - 768), BF16)

    def full(r, c):
        return pl.BlockSpec((r, c), lambda i: (0, 0))

    def rows(c):
        return pl.BlockSpec((tT, c), lambda i: (i, 0))

    return pl.pallas_call(
        body, name=name, grid=(T // tT,),
        in_specs=[rows(QW), rows(QW), rows(QW), pl.BlockSpec((tT, AW), lambda i: (i, a_blk)), rows(1),
                  full(1, MLA_Q_RANK), full(1, MLA_KV_RANK),
                  full(MLA_Q_RANK, QW), full(MLA_KV_RANK, 2 * QW), full(8, LANES)],
        out_specs=[rows(AW), rows(QW), rows(2 * QW), full(1, MLA_Q_RANK), full(1, MLA_KV_RANK)],
        out_shape=[jax.ShapeDtypeStruct((T, AW), BF16), jax.ShapeDtypeStruct((T, QW), BF16),
                   jax.ShapeDtypeStruct((T, 2 * QW), BF16),
                   jax.ShapeDtypeStruct((1, MLA_Q_RANK), F32), jax.ShapeDtypeStruct((1, MLA_KV_RANK), F32)],
        compiler_params=_params(("arbitrary",)),
    )(dq, dk, dv, proj, pos, qn_w, kvn_w, w_uq, w_kv, tab)


def _flash_fwd(q, k, v, *, name):
    T = q.shape[0]
    H = q.shape[1] // HP
    tq = min(ATTN_TILE, T)
    nq = T // tq

    sub = tq // ATTN_CHAINS

    def body(q_ref, k_ref, v_ref, o_ref, lse_ref):
        qi = pl.program_id(1)
        qs = [q_ref[c * sub:(c + 1) * sub, :] for c in range(ATTN_CHAINS)]

        def update(carry, off, masked):
            kb = k_ref[pl.ds(off, tq), :]
            vb = v_ref[pl.ds(off, tq), :]
            out = []
            for c in range(ATTN_CHAINS):
                m_prev, l_prev, acc = carry[c]
                s = _dot_nt(qs[c], kb)
                if masked:
                    rows = lax.broadcasted_iota(jnp.int32, (sub, tq), 0) + c * sub
                    s = jnp.where(rows >= lax.broadcasted_iota(jnp.int32, (sub, tq), 1), s, NEG)
                m_new = jnp.maximum(m_prev, jnp.max(s, axis=1, keepdims=True))
                alpha = jnp.exp(m_prev - m_new)
                p = jnp.exp(s - m_new)
                out.append((m_new, alpha * l_prev + jnp.sum(p, axis=1, keepdims=True),
                            alpha * acc + _dot(p.astype(BF16), vb)))
            return tuple(out)

        init = tuple((jnp.full((sub, 1), NEG, F32), jnp.zeros((sub, 1), F32), jnp.zeros((sub, HP), F32))
                     for _ in range(ATTN_CHAINS))
        carry = lax.fori_loop(0, qi, lambda j, cr: update(cr, pl.multiple_of(j * tq, tq), False), init)
        carry = update(carry, pl.multiple_of(qi * tq, tq), True)
        for c in range(ATTN_CHAINS):
            m_fin, l_fin, acc = carry[c]
            o_ref[c * sub:(c + 1) * sub, :] = (acc / l_fin).astype(BF16)
            lse_ref[c * sub:(c + 1) * sub, :] = jnp.broadcast_to(m_fin + jnp.log(l_fin), (sub, HP))

    qspec = pl.BlockSpec((tq, HP), lambda h, i: (i, h))
    kspec = pl.BlockSpec((T, HP), lambda h, i: (0, h))
    return pl.pallas_call(
        body, name=name, grid=(H, nq),
        in_specs=[qspec, kspec, kspec], out_specs=[qspec, qspec],
        out_shape=[jax.ShapeDtypeStruct((T, H * HP), BF16), jax.ShapeDtypeStruct((T, H * HP), F32)],
        compiler_params=_params(("parallel", "arbitrary")),
    )(q, k, v)


def _flash_bwd(q, k, v, do, lse, delta, *, name):
    T = q.shape[0]
    H = q.shape[1] // HP
    tq = min(ATTN_TILE, T)
    nq = T // tq
    sub = tq // ATTN_CHAINS

    def body(k_ref, v_ref, q_ref, do_ref, lse_ref, dl_ref, dq_ref, dk_ref, dv_ref):
        ki = pl.program_id(1)

        @pl.when(ki == 0)
        def _():
            dq_ref[...] = jnp.zeros_like(dq_ref)

        kb = k_ref[...]
        vb = v_ref[...]

        def step(carry, j, masked):
            dk_acc, dv_acc = carry
            for c in range(ATTN_CHAINS):
                rows = pl.ds(pl.multiple_of(j * tq + c * sub, sub), sub)
                qb = q_ref[rows, :]
                dob = do_ref[rows, :]
                s = _dot_nt(qb, kb)
                if masked:
                    ri = lax.broadcasted_iota(jnp.int32, (sub, tq), 0) + c * sub
                    s = jnp.where(ri >= lax.broadcasted_iota(jnp.int32, (sub, tq), 1), s, NEG)
                p = jnp.exp(s - lse_ref[rows, 0:1])
                dv_acc = dv_acc + _dot_tn(p.astype(BF16), dob)
                dp = _dot_nt(dob, vb)
                ds = (p * (dp - dl_ref[rows, 0:1])).astype(BF16)
                dk_acc = dk_acc + _dot_tn(ds, qb)
                dq_ref[rows, :] += _dot(ds, kb)
            return dk_acc, dv_acc

        carry = step((jnp.zeros((tq, HP), F32), jnp.zeros((tq, HP), F32)), ki, True)
        dk_acc, dv_acc = lax.fori_loop(ki + 1, nq, lambda j, cr: step(cr, j, False), carry)
        dk_ref[...] = dk_acc
        dv_ref[...] = dv_acc.astype(BF16)

    kspec = pl.BlockSpec((tq, HP), lambda h, j: (j, h))
    full = pl.BlockSpec((T, HP), lambda h, j: (0, h))
    return pl.pallas_call(
        body, name=name, grid=(H, nq),
        in_specs=[kspec, kspec, full, full, full, full], out_specs=[full, kspec, kspec],
        out_shape=[jax.ShapeDtypeStruct((T, H * HP), F32), jax.ShapeDtypeStruct((T, H * HP), F32),
                   jax.ShapeDtypeStruct((T, H * HP), BF16)],
        compiler_params=_params(("parallel", "arbitrary")),
    )(k, v, q, do, lse, delta)


def _ret_consts(cc, hd):
    lg = math.log(1.0 - 2.0 ** (-5.0 - hd))
    diff = (lax.broadcasted_iota(jnp.int32, (cc, cc), 0) - lax.broadcasted_iota(jnp.int32, (cc, cc), 1)).astype(F32)
    decay = jnp.where(diff >= 0, jnp.exp(jnp.maximum(diff, 0.0) * lg), 0.0)
    idx = lax.broadcasted_iota(jnp.int32, (cc, 1), 0).astype(F32)
    zeta = jnp.exp((cc - 1.0 - idx) * lg)
    xi = jnp.exp((idx + 1.0) * lg)
    return decay, zeta, xi, math.exp(cc * lg)


def _ret_fwd(proj, pos, tab, *, name):
    T = proj.shape[0]
    cc = min(RET_TILE, T)
    n = T // cc

    def body(rq_ref, rk_ref, rv_ref, pos_ref, tab_ref, y_ref, yn_ref, rprev_ref, r_s):
        @pl.when(pl.program_id(0) == 0)
        def _():
            r_s[...] = jnp.zeros_like(r_s)

        cs = _rope_cs(pos_ref[...], tab_ref)
        for hd in range(RET_HEADS):
            sl = slice(hd * HP, (hd + 1) * HP)
            decay, zeta, xi, gc = _ret_consts(cc, hd)
            q = _rope(rq_ref[:, sl].astype(F32), cs, RET_DK // 2).astype(BF16)
            kf = _rope(rk_ref[:, sl].astype(F32), cs, RET_DK // 2) * (RET_DK ** -0.5)
            k = kf.astype(BF16)
            v = rv_ref[:, sl]
            r = r_s[hd]
            rprev_ref[0, hd] = r
            inner = (_dot_nt(q, k) * decay).astype(BF16)
            y = _dot(inner, v) + _dot(q, r.astype(BF16)) * xi
            r_s[hd] = r * gc + _dot_tn((kf * zeta).astype(BF16), v)
            y_ref[:, sl] = y
            mu = jnp.mean(y, axis=-1, keepdims=True)
            yc = y - mu
            var = jnp.mean(yc * yc, axis=-1, keepdims=True)
            yn_ref[:, sl] = (yc * lax.rsqrt(var + GN_EPS)).astype(BF16)

    def blk(j):
        return pl.BlockSpec((cc, RW), lambda i: (i, j))

    return pl.pallas_call(
        body, name=name, grid=(n,),
        in_specs=[blk(0), blk(1), blk(2), pl.BlockSpec((cc, 1), lambda i: (i, 0)),
                  pl.BlockSpec((8, LANES), lambda i: (0, 0))],
        out_specs=[blk(0), blk(0), pl.BlockSpec((1, RET_HEADS, HP, RET_DV), lambda i: (i, 0, 0, 0))],
        out_shape=[jax.ShapeDtypeStruct((T, RW), F32), jax.ShapeDtypeStruct((T, RW), BF16),
                   jax.ShapeDtypeStruct((n, RET_HEADS, HP, RET_DV), F32)],
        scratch_shapes=[pltpu.VMEM((RET_HEADS, HP, RET_DV), F32)],
        compiler_params=_params(("arbitrary",)),
    )(proj, proj, proj, pos, tab)


def _ret_bwd(dyn, y, proj, pos, tab, rprev, *, name):
    T = proj.shape[0]
    cc = min(RET_TILE, T)
    n = T // cc

    def body(dyn_ref, y_ref, rq_ref, rk_ref, rv_ref, pos_ref, tab_ref, rprev_ref,
             drq_ref, drk_ref, drv_ref, dr_s):
        @pl.when(pl.program_id(0) == 0)
        def _():
            dr_s[...] = jnp.zeros_like(dr_s)

        cs = _rope_cs(pos_ref[...], tab_ref)
        for hd in range(RET_HEADS):
            sl = slice(hd * HP, (hd + 1) * HP)
            decay, zeta, xi, gc = _ret_consts(cc, hd)
            q = _rope(rq_ref[:, sl].astype(F32), cs, RET_DK // 2).astype(BF16)
            kf = _rope(rk_ref[:, sl].astype(F32), cs, RET_DK // 2) * (RET_DK ** -0.5)
            k = kf.astype(BF16)
            v = rv_ref[:, sl]
            yv = y_ref[:, sl]
            mu = jnp.mean(yv, axis=-1, keepdims=True)
            yc = yv - mu
            rs = lax.rsqrt(jnp.mean(yc * yc, axis=-1, keepdims=True) + GN_EPS)
            yn = yc * rs
            dn = dyn_ref[:, sl]
            dy = rs * (dn - jnp.mean(dn, axis=-1, keepdims=True) - yn * jnp.mean(dn * yn, axis=-1, keepdims=True))
            dyb = dy.astype(BF16)
            dyx = (dy * xi).astype(BF16)
            dr = dr_s[hd]
            drb = dr.astype(BF16)
            inner = (_dot_nt(q, k) * decay).astype(BF16)
            da = (_dot_nt(dyb, v) * decay).astype(BF16)
            dv = _dot_tn(inner, dyb) + _dot((kf * zeta).astype(BF16), drb)
            dq = _dot(da, k) + _dot_nt(dyx, rprev_ref[0, hd].astype(BF16))
            dk = _dot_tn(da, q) + _dot_nt(v, drb) * zeta
            dr_s[hd] = dr * gc + _dot_tn(q, dyx)
            drq_ref[:, sl] = _rope(dq, cs, RET_DK // 2, inverse=True).astype(BF16)
            drk_ref[:, sl] = _rope(dk * (RET_DK ** -0.5), cs, RET_DK // 2, inverse=True).astype(BF16)
            drv_ref[:, sl] = dv.astype(BF16)

    def blk(j):
        return pl.BlockSpec((cc, RW), lambda i: (n - 1 - i, j))

    return pl.pallas_call(
        body, name=name, grid=(n,),
        in_specs=[blk(0), blk(0), blk(0), blk(1), blk(2), pl.BlockSpec((cc, 1), lambda i: (n - 1 - i, 0)),
                  pl.BlockSpec((8, LANES), lambda i: (0, 0)),
                  pl.BlockSpec((1, RET_HEADS, HP, RET_DV), lambda i: (n - 1 - i, 0, 0, 0))],
        out_specs=[blk(0), blk(0), blk(0)],
        out_shape=[jax.ShapeDtypeStruct((T, RW), BF16)] * 3,
        scratch_shapes=[pltpu.VMEM((RET_HEADS, HP, RET_DV), F32)],
        compiler_params=_params(("arbitrary",)),
    )(dyn, y, proj, proj, proj, pos, tab, rprev)


def _merge_fwd(o, yn, proj, gn_w, w_bm, w_br, w_out, h, post_w, *, name):
    T, D = h.shape
    tT = min(MERGE_TILE, T)
    g_blk = PROJ_FIXED // D

    def body(o_ref, yn_ref, rg_ref, gm_ref, gr_ref, gnw_ref, wbm_ref, wbr_ref, wout_ref, h_ref, post_ref,
             omla_ref, oret_ref, m_ref, ho_ref):
        o_mla = _dot(o_ref[...], wbm_ref[...])
        rg = rg_ref[...].astype(F32)
        gated = (rg * _sigmoid(rg) * (yn_ref[...].astype(F32) * gnw_ref[...])).astype(BF16)
        o_ret = _dot(gated, wbr_ref[...])
        omla_ref[...] = o_mla.astype(BF16)
        oret_ref[...] = o_ret.astype(BF16)
        merged = _sigmoid(gm_ref[...].astype(F32)) * o_mla + _sigmoid(gr_ref[...].astype(F32)) * o_ret
        m = _dot(merged.astype(BF16), wout_ref[...])
        m_ref[...] = m
        ho_ref[...] = h_ref[...] + _rms_fwd(m, post_ref[...])

    def full(r, c):
        return pl.BlockSpec((r, c), lambda i: (0, 0))

    def rows(c, j=0):
        return pl.BlockSpec((tT, c), lambda i: (i, j))

    return pl.pallas_call(
        body, name=name, grid=(T // tT,),
        in_specs=[rows(QW), rows(RW), rows(RW, 3), rows(D, g_blk), rows(D, g_blk + 1), full(1, RW),
                  full(QW, D), full(RW, D), full(D, D), rows(D), full(1, D)],
        out_specs=[rows(D), rows(D), rows(D), rows(D)],
        out_shape=[jax.ShapeDtypeStruct((T, D), BF16), jax.ShapeDtypeStruct((T, D), BF16),
                   jax.ShapeDtypeStruct((T, D), F32), jax.ShapeDtypeStruct((T, D), F32)],
        compiler_params=_params(("parallel",)),
    )(o, yn, proj, proj, proj, gn_w, w_bm, w_br, w_out, h, post_w)


def _merge_bwd(dho, m, post_w, omla, oret, proj, yn, gn_w, o, w_out, w_bm, w_br, *, name):
    T, D = dho.shape
    tT = min(MERGE_TILE, T)
    g_blk = PROJ_FIXED // D

    def body(dho_ref, m_ref, post_ref, omla_ref, oret_ref, rg_ref, gm_ref, gr_ref, yn_ref, gnw_ref, o_ref,
             wout_ref, wbm_ref, wbr_ref,
             dm_ref, merged_ref, dgm_ref, dgr_ref, domla_ref, do_ref, delta_ref, doret_ref, gated_ref,
             drg_ref, dyn_ref, gpost_ref, ggn_ref):
        @pl.when(pl.program_id(0) == 0)
        def _():
            gpost_ref[...] = jnp.zeros_like(gpost_ref)
            ggn_ref[...] = jnp.zeros_like(ggn_ref)

        dm, gp = _rms_bwd(m_ref[...], post_ref[...], dho_ref[...])
        gpost_ref[...] += gp
        dmb = dm.astype(BF16)
        dm_ref[...] = dmb
        dmerged = _dot_nt(dmb, wout_ref[...])
        o_mla = omla_ref[...].astype(F32)
        o_ret = oret_ref[...].astype(F32)
        sgm = _sigmoid(gm_ref[...].astype(F32))
        sgr = _sigmoid(gr_ref[...].astype(F32))
        merged_ref[...] = (sgm * o_mla + sgr * o_ret).astype(BF16)
        dgm_ref[...] = (dmerged * o_mla * sgm * (1.0 - sgm)).astype(BF16)
        dgr_ref[...] = (dmerged * o_ret * sgr * (1.0 - sgr)).astype(BF16)
        domla = (dmerged * sgm).astype(BF16)
        domla_ref[...] = domla
        do = _dot_nt(domla, wbm_ref[...])
        do_ref[...] = do.astype(BF16)
        for hd in range(MLA_HEADS):
            sl = slice(hd * HP, (hd + 1) * HP)
            d = jnp.sum(do[:, sl] * o_ref[:, sl].astype(F32), axis=-1, keepdims=True)
            delta_ref[:, sl] = jnp.broadcast_to(d, (tT, HP))
        doret = (dmerged * sgr).astype(BF16)
        doret_ref[...] = doret
        dgated = _dot_nt(doret, wbr_ref[...])
        rg = rg_ref[...].astype(F32)
        sg = _sigmoid(rg)
        srg = rg * sg
        ynv = yn_ref[...].astype(F32)
        yw = ynv * gnw_ref[...]
        gated_ref[...] = (srg * yw).astype(BF16)
        drg_ref[...] = (dgated * yw * (sg * (1.0 + rg * (1.0 - sg)))).astype(BF16)
        dgs = dgated * srg
        dyn_ref[...] = dgs * gnw_ref[...]
        ggn_ref[...] += jnp.sum(dgs * ynv, axis=0, keepdims=True)

    def full(r, c):
        return pl.BlockSpec((r, c), lambda i: (0, 0))

    def rows(c, j=0):
        return pl.BlockSpec((tT, c), lambda i: (i, j))

    return pl.pallas_call(
        body, name=name, grid=(T // tT,),
        in_specs=[rows(D), rows(D), full(1, D), rows(D), rows(D), rows(RW, 3), rows(D, g_blk), rows(D, g_blk + 1),
                  rows(RW), full(1, RW), rows(QW), full(D, D), full(QW, D), full(RW, D)],
        out_specs=[rows(D), rows(D), rows(D), rows(D), rows(D), rows(QW), rows(QW), rows(D), rows(RW),
                   rows(RW), rows(RW), full(1, D), full(1, RW)],
        out_shape=[jax.ShapeDtypeStruct((T, D), BF16)] * 5
        + [jax.ShapeDtypeStruct((T, QW), BF16), jax.ShapeDtypeStruct((T, QW), F32),
           jax.ShapeDtypeStruct((T, D), BF16), jax.ShapeDtypeStruct((T, RW), BF16),
           jax.ShapeDtypeStruct((T, RW), BF16), jax.ShapeDtypeStruct((T, RW), F32),
           jax.ShapeDtypeStruct((1, D), F32), jax.ShapeDtypeStruct((1, RW), F32)],
        compiler_params=_params(("arbitrary",)),
    )(dho, m, post_w, omla, oret, proj, proj, proj, yn, gn_w, o, w_out, w_bm, w_br)


def _mesh_pos():
    return lax.axis_index("x"), lax.axis_index("y"), lax.axis_index("c")


def _all_gather(shards, *, name):
    nw = len(shards)

    def body(*refs):
        x_refs, out_refs = refs[:nw], refs[nw:2 * nw]
        send_sems, recv_sems, local_sems = refs[2 * nw:]
        x, y, c = _mesh_pos()
        me, sibling = (x, y, c), (x, y, 1 - c)
        chips = [(1 - x, y), (x, 1 - y), (1 - x, 1 - y)]

        def copy(w, k, block, to, src=None):
            slot = out_refs[w].at[4 * block[0] + 2 * block[1] + block[2]]
            return pltpu.make_async_remote_copy(
                src_ref=slot if src is None else src, dst_ref=slot,
                send_sem=send_sems.at[7 * w + k], recv_sem=recv_sems.at[7 * w + k],
                device_id=to, device_id_type=pl.DeviceIdType.MESH)

        mine, sent = [], []
        for w in range(nw):
            cp = pltpu.make_async_copy(x_refs[w], out_refs[w].at[4 * x + 2 * y + c], local_sems.at[w])
            cp.start()
            mine.append(cp)
            first = [copy(w, 0, me, sibling, src=x_refs[w])]
            first += [copy(w, 1 + j, me, (*chip, c), src=x_refs[w]) for j, chip in enumerate(chips)]
            for cp in first:
                cp.start()
            sent += first
        for w in range(nw):
            for j, chip in enumerate(chips):
                copy(w, 1 + j, (*chip, c), me).wait_recv()
                cp = copy(w, 4 + j, (*chip, c), sibling)
                cp.start()
                sent.append(cp)
        for w in range(nw):
            copy(w, 0, sibling, me).wait_recv()
            for j, chip in enumerate(chips):
                copy(w, 4 + j, (*chip, 1 - c), me).wait_recv()
        for cp in sent:
            cp.wait_send()
        for cp in mine:
            cp.wait()

    anyspec = pl.BlockSpec(memory_space=pl.ANY)
    return pl.pallas_call(
        body, name=name,
        out_shape=[jax.ShapeDtypeStruct((N_DEV,) + s.shape, s.dtype) for s in shards],
        in_specs=[anyspec] * nw, out_specs=[anyspec] * nw,
        scratch_shapes=[pltpu.SemaphoreType.DMA((7 * nw,)), pltpu.SemaphoreType.DMA((7 * nw,)),
                        pltpu.SemaphoreType.DMA((nw,))],
    )(*shards)


def _exchange_grads(grads, sv, *, name):
    nw = len(grads) + 1

    def body(*refs):
        in_refs, out_refs = refs[:nw], refs[nw:2 * nw]
        send_sems, recv_sems, local_sems = refs[2 * nw:]
        x, y, c = _mesh_pos()
        me = 4 * x + 2 * y + c

        def src(w, dev):
            return in_refs[w].at[dev] if w < nw - 1 else in_refs[w]

        own = [pltpu.make_async_copy(src(w, me), out_refs[w].at[me], local_sems.at[w]) for w in range(nw)]
        for cp in own:
            cp.start()
        sends, recvs = [], []
        for w in range(nw):
            for r in range(1, N_DEV):
                px = 1 - x if r & 4 else x
                py = 1 - y if r & 2 else y
                pc = 1 - c if r & 1 else c
                peer, pidx = (px, py, pc), 4 * px + 2 * py + pc
                k = (N_DEV - 1) * w + r - 1
                sends.append(pltpu.make_async_remote_copy(
                    src_ref=src(w, pidx), dst_ref=out_refs[w].at[me], send_sem=send_sems.at[k],
                    recv_sem=recv_sems.at[k], device_id=peer, device_id_type=pl.DeviceIdType.MESH))
                recvs.append(pltpu.make_async_remote_copy(
                    src_ref=src(w, me), dst_ref=out_refs[w].at[pidx], send_sem=send_sems.at[k],
                    recv_sem=recv_sems.at[k], device_id=peer, device_id_type=pl.DeviceIdType.MESH))
        for cp in sends:
            cp.start()
        for cp in recvs:
            cp.wait_recv()
        for cp in sends:
            cp.wait_send()
        for cp in own:
            cp.wait()

    anyspec = pl.BlockSpec(memory_space=pl.ANY)
    ops = list(grads) + [sv]
    return pl.pallas_call(
        body, name=name,
        out_shape=[jax.ShapeDtypeStruct(g.shape, g.dtype) for g in grads]
        + [jax.ShapeDtypeStruct((N_DEV,) + sv.shape, sv.dtype)],
        in_specs=[anyspec] * nw, out_specs=[anyspec] * nw,
        scratch_shapes=[pltpu.SemaphoreType.DMA(((N_DEV - 1) * nw,)), pltpu.SemaphoreType.DMA(((N_DEV - 1) * nw,)),
                        pltpu.SemaphoreType.DMA((nw,))],
    )(*ops)


def _adamw(w, parts, m, v, *, name):
    R, n = w.shape
    tr = R
    for t in range(16, R, 16):
        if R % t == 0 and t * n <= 256 * 1024:
            tr = t
    if R * n <= 256 * 1024:
        tr = R

    def body(w_ref, p_ref, m_ref, v_ref, g_ref, d_ref, nm_ref, nv_ref):
        g = p_ref[0].astype(F32)
        for j in range(1, N_DEV):
            g = g + p_ref[j].astype(F32)
        g_ref[...] = g
        nm = ADAM_B1 * m_ref[...] + (1.0 - ADAM_B1) * g
        nv = ADAM_B2 * v_ref[...] + (1.0 - ADAM_B2) * (g * g)
        nm_ref[...] = nm
        nv_ref[...] = nv
        m_hat = nm / (1.0 - ADAM_B1 ** ADAM_STEP)
        v_hat = nv / (1.0 - ADAM_B2 ** ADAM_STEP)
        d_ref[...] = -ADAM_LR * (m_hat / (jnp.sqrt(v_hat) + ADAM_EPS) + ADAM_WD * w_ref[...])

    row = pl.BlockSpec((tr, n), lambda i: (i, 0))
    return pl.pallas_call(
        body, name=name, grid=(R // tr,),
        in_specs=[row, pl.BlockSpec((N_DEV, tr, n), lambda i: (0, i, 0)), row, row],
        out_specs=[row, row, row, row],
        out_shape=[jax.ShapeDtypeStruct((R, n), F32)] * 4,
        compiler_params=_params(("parallel",)),
    )(w, parts, m, v)


def _pad_last(a, width):
    return jnp.pad(a, [(0, 0)] * (a.ndim - 1) + [(0, width - a.shape[-1])])


def _cols_of(g):
    return g.transpose(1, 0, 2).reshape(g.shape[1], N_DEV * g.shape[2])


def _col_shards(w):
    return w.reshape(w.shape[0], N_DEV, w.shape[1] // N_DEV).transpose(1, 0, 2)


def kernel(x, positions, ffn1_pre_w, ffn1_w1, ffn1_w2, ffn1_post_w, mix_pre_w, w_in, mla_q_norm_w, mla_w_uq, mla_kv_norm_w, mla_w_ukv, ret_gn_w, w_branch_mla, w_branch_ret, w_out, mix_post_w, ffn2_pre_w, ffn2_w1, ffn2_w2, ffn2_post_w, loss_target, m_ffn1_pre_w, m_ffn1_w1, m_ffn1_w2, m_ffn1_post_w, m_mix_pre_w, m_w_in, m_mla_q_norm_w, m_mla_w_uq, m_mla_kv_norm_w, m_mla_w_ukv, m_ret_gn_w, m_w_branch_mla, m_w_branch_ret, m_w_out, m_mix_post_w, m_ffn2_pre_w, m_ffn2_w1, m_ffn2_w2, m_ffn2_post_w, v_ffn1_pre_w, v_ffn1_w1, v_ffn1_w2, v_ffn1_post_w, v_mix_pre_w, v_w_in, v_mla_q_norm_w, v_mla_w_uq, v_mla_kv_norm_w, v_mla_w_ukv, v_ret_gn_w, v_w_branch_mla, v_w_branch_ret, v_w_out, v_mix_post_w, v_ffn2_pre_w, v_ffn2_w1, v_ffn2_w2, v_ffn2_post_w):
    T, D = x.shape[1], x.shape[2]
    h0 = x[0]
    tgt = loss_target[0]
    pos = positions.reshape(T, 1).astype(F32)

    big = [("ffn1_w1", ffn1_w1, m_ffn1_w1, v_ffn1_w1), ("ffn1_w2", ffn1_w2, m_ffn1_w2, v_ffn1_w2),
           ("w_in", w_in, m_w_in, v_w_in), ("mla_w_uq", mla_w_uq, m_mla_w_uq, v_mla_w_uq),
           ("mla_w_ukv", mla_w_ukv, m_mla_w_ukv, v_mla_w_ukv),
           ("w_branch_mla", w_branch_mla, m_w_branch_mla, v_w_branch_mla),
           ("w_branch_ret", w_branch_ret, m_w_branch_ret, v_w_branch_ret),
           ("w_out", w_out, m_w_out, v_w_out),
           ("ffn2_w1", ffn2_w1, m_ffn2_w1, v_ffn2_w1), ("ffn2_w2", ffn2_w2, m_ffn2_w2, v_ffn2_w2)]
    small = [("ffn1_pre_w", ffn1_pre_w, m_ffn1_pre_w, v_ffn1_pre_w), ("ffn1_post_w", ffn1_post_w, m_ffn1_post_w, v_ffn1_post_w),
             ("mix_pre_w", mix_pre_w, m_mix_pre_w, v_mix_pre_w), ("mla_q_norm_w", mla_q_norm_w, m_mla_q_norm_w, v_mla_q_norm_w),
             ("mla_kv_norm_w", mla_kv_norm_w, m_mla_kv_norm_w, v_mla_kv_norm_w), ("ret_gn_w", ret_gn_w, m_ret_gn_w, v_ret_gn_w),
             ("mix_post_w", mix_post_w, m_mix_post_w, v_mix_post_w), ("ffn2_pre_w", ffn2_pre_w, m_ffn2_pre_w, v_ffn2_pre_w),
             ("ffn2_post_w", ffn2_post_w, m_ffn2_post_w, v_ffn2_post_w)]

    half = ffn1_w2.shape[1]
    hp = -(-half // LANES) * LANES

    def send_w1(w):
        return _pad_last(w[0].reshape(D, 2, half), hp).reshape(D, 2 * hp).astype(BF16)

    def send_w2(w):
        return jnp.pad(w[0], ((0, hp - half), (0, 0))).astype(BF16)

    send = {"ffn1_w1": send_w1(ffn1_w1), "ffn1_w2": send_w2(ffn1_w2),
            "ffn2_w1": send_w1(ffn2_w1), "ffn2_w2": send_w2(ffn2_w2)}
    gathered = _all_gather([send[nm] if nm in send else w[0].astype(BF16) for nm, w, _, _ in big],
                           name="weights_all_gather")
    fw = {nm: g for (nm, *_), g in zip(big, gathered)}
    w1a, w2a = fw["ffn1_w1"], fw["ffn1_w2"].reshape(N_DEV // 2, 2 * hp, D)
    w1b, w2b = fw["ffn2_w1"], fw["ffn2_w2"].reshape(N_DEV // 2, 2 * hp, D)

    wi = _cols_of(fw["w_in"])
    cq_w, ckv_w, kr_w = wi[:, 0:384], wi[:, 384:640], wi[:, 640:672]
    rq_w, rk_w = wi[:, 672:928], wi[:, 928:1184]
    rv_w, rg_w = wi[:, 1184:1696], wi[:, 1696:2208]
    gm_w, gr_w = wi[:, 2208:2208 + D], wi[:, 2208 + D:2208 + 2 * D]
    zer = lambda n: jnp.zeros((D, n), BF16)
    head_pad = lambda a, h: _pad_last(a.reshape(a.shape[0], h, -1), HP).reshape(a.shape[0], h * HP)
    w_in_p = jnp.concatenate([head_pad(rq_w, RET_HEADS), head_pad(rk_w, RET_HEADS), rv_w, rg_w,
                              cq_w, ckv_w, zer(MLA_NOPE), kr_w, zer(HP - MLA_NOPE - MLA_ROPE), zer(AW - 768),
                              gm_w, gr_w], axis=1)
    w_uq_p = _cols_of(_pad_last(fw["mla_w_uq"], HP))
    ukv = fw["mla_w_ukv"].transpose(1, 0, 2)
    w_kv_p = jnp.concatenate([_pad_last(ukv[:, :, :MLA_NOPE], HP).reshape(MLA_KV_RANK, QW),
                              _pad_last(ukv[:, :, MLA_NOPE:], HP).reshape(MLA_KV_RANK, QW)], axis=1)
    w_bm_p = jnp.pad(_cols_of(fw["w_branch_mla"]).reshape(MLA_HEADS, MLA_V, D),
                     ((0, 0), (0, HP - MLA_V), (0, 0))).reshape(QW, D)
    w_br, w_o = _cols_of(fw["w_branch_ret"]), fw["w_out"].reshape(D, D)
    tab_mla = _rope_table(MLA_NOPE, MLA_ROPE // 2)
    tab_ret = _rope_table(0, RET_DK // 2)

    u1, f1, h1 = _ffn_fwd(h0, ffn1_pre_w, w1a, w2a, ffn1_post_w, None, name="ffn1_fwd")
    proj, a1 = _rms_matmul(h1, mix_pre_w, w_in_p, name="mixer_in_proj")
    q, k, v, qn, kvn = _mla_prep_fwd(proj, pos, mla_q_norm_w, mla_kv_norm_w, w_uq_p, w_kv_p, tab_mla, name="mla_prep_fwd")
    o, lse = _flash_fwd(q, k, v, name="mla_attn_fwd")
    ypre, yn, rprev = _ret_fwd(proj, pos, tab_ret, name="retention_fwd")
    omla, oret, m, h2 = _merge_fwd(o, yn, proj, ret_gn_w, w_bm_p, w_br, w_o, h1, mix_post_w, name="merge_fwd")
    u2, f2, _, dy, lossp = _ffn_fwd(h2, ffn2_pre_w, w1b, w2b, ffn2_post_w, tgt, name="ffn2_fwd_loss")
    loss = lax.psum(jnp.sum(lossp[::8, 0]), ("x", "y", "c"))

    def ffn_grads(a, du, g, df, tag):
        dw1 = _matmul_tn(a[None], du.reshape(N_DEV, T, 2 * hp), name=tag + "_dw1")
        dw2 = _matmul_tn(g, df[None], name=tag + "_dw2")
        return dw1, dw2.reshape(N_DEV, hp, D)

    g2, du2, df2, a2, dh2, gpost2, gpre2 = _ffn_bwd(dy, f2, ffn2_post_w, h2, ffn2_pre_w, u2, w2b, w1b, name="ffn2_bwd")
    dw1b, dw2b = ffn_grads(a2, du2, g2, df2, "ffn2")
    (dmb, merged, dgm, dgr, domla, do, delta, doret, gated, drg, dyn, gpostm, ggn) = _merge_bwd(
        dh2, m, mix_post_w, omla, oret, proj, yn, ret_gn_w, o, w_o, w_bm_p, w_br, name="merge_bwd")
    dw_out = _matmul_tn(merged[None], dmb[None], name="dw_out")[0]
    dw_bm_p = _matmul_tn(o[None], domla[None], name="dw_branch_mla")[0]
    dw_br = _matmul_tn(gated[None], doret[None], name="dw_branch_ret")[0]
    dq, dk, dv = _flash_bwd(q, k, v, do, lse, delta, name="mla_attn_bwd")
    da, dql, dkvl, gqn, gkvn = _mla_prep_bwd(dq, dk, dv, proj, pos, mla_q_norm_w, mla_kv_norm_w, w_uq_p, w_kv_p, tab_mla, name="mla_prep_bwd")
    dw_uq_p = _matmul_tn(qn[None], dql[None], name="dw_uq")[0]
    dw_kv_p = _matmul_tn(kvn[None], dkvl[None], name="dw_ukv")[0]
    drq, drk, drv = _ret_bwd(dyn, ypre, proj, pos, tab_ret, rprev, name="retention_bwd")
    dproj = jnp.concatenate([drq, drk, drv, drg, da, dgm, dgr], axis=1)
    dw_in_p = _matmul_tn(a1[None], dproj[None], name="dw_in")[0]
    dh1, gmixpre = _proj_bwd(dproj, w_in_p, h1, mix_pre_w, dh2, name="mixer_in_bwd")
    g1, du1, df1, a0, dx, gpost1, gpre1 = _ffn_bwd(dh1, f1, ffn1_post_w, h0, ffn1_pre_w, u1, w2a, w1a, name="ffn1_bwd")
    dw1a, dw2a = ffn_grads(a0, du1, g1, df1, "ffn1")

    unhead = lambda a, h, wd: a.reshape(a.shape[0], h, HP)[:, :, :wd].reshape(a.shape[0], h * wd)
    c0 = 4 * RW
    dw_in = jnp.concatenate([
        dw_in_p[:, c0:c0 + 384], dw_in_p[:, c0 + 384:c0 + 640], dw_in_p[:, c0 + 640 + MLA_NOPE:c0 + 640 + MLA_NOPE + MLA_ROPE],
        unhead(dw_in_p[:, 0:RW], RET_HEADS, RET_DK), unhead(dw_in_p[:, RW:2 * RW], RET_HEADS, RET_DK),
        dw_in_p[:, 2 * RW:3 * RW], dw_in_p[:, 3 * RW:4 * RW],
        dw_in_p[:, PROJ_FIXED:PROJ_FIXED + D], dw_in_p[:, PROJ_FIXED + D:PROJ_FIXED + 2 * D]], axis=1)
    dw_uq = dw_uq_p.reshape(MLA_Q_RANK, MLA_HEADS, HP)[:, :, :MLA_NOPE + MLA_ROPE].transpose(1, 0, 2)
    dkp = dw_kv_p[:, :QW].reshape(MLA_KV_RANK, MLA_HEADS, HP)[:, :, :MLA_NOPE]
    dvp = dw_kv_p[:, QW:].reshape(MLA_KV_RANK, MLA_HEADS, HP)[:, :, :MLA_V]
    dw_ukv = jnp.concatenate([dkp, dvp], axis=2).transpose(1, 0, 2)
    dw_bm = dw_bm_p.reshape(MLA_HEADS, HP, D)[:, :MLA_V].reshape(MLA_HEADS * MLA_V, D)

    grads = {"ffn1_w1": dw1a, "ffn1_w2": dw2a, "w_in": _col_shards(dw_in), "mla_w_uq": dw_uq, "mla_w_ukv": dw_ukv,
             "w_branch_mla": _col_shards(dw_bm), "w_branch_ret": _col_shards(dw_br),
             "w_out": dw_out.reshape(N_DEV, D // N_DEV, D), "ffn2_w1": dw1b, "ffn2_w2": dw2b}

    small_g = {"ffn1_pre_w": gpre1, "ffn1_post_w": gpost1, "mix_pre_w": gmixpre, "mla_q_norm_w": gqn,
               "mla_kv_norm_w": gkvn, "ret_gn_w": ggn, "mix_post_w": gpostm, "ffn2_pre_w": gpre2, "ffn2_post_w": gpost2}

    def small_flat(arrs):
        a = jnp.concatenate([z.reshape(-1) for z in arrs])
        a = jnp.pad(a, (0, (-a.size) % (8 * LANES)))
        return a.reshape(-1, LANES)

    sv = small_flat([small_g[nm] for nm, *_ in small])
    *recv, sall = _exchange_grads([grads[nm] for nm, *_ in big], sv, name="grads_exchange")
    parts = {nm: r for (nm, *_), r in zip(big, recv)}
    for nm in ("ffn1_w1", "ffn2_w1"):
        parts[nm] = parts[nm].reshape(N_DEV, D, 2, hp)[:, :, :, :half].reshape(N_DEV, D, 2 * half)

    big_out = {nm: [a[None] for a in _adamw(w[0], parts[nm], m_[0], v_[0], name="adamw_" + nm)]
               for nm, w, m_, v_ in big}
    gs, ds, nms, nvs = _adamw(small_flat([w for _, w, _, _ in small]), sall,
                              small_flat([a for _, _, a, _ in small]), small_flat([a for _, _, _, a in small]),
                              name="adamw_replicated")

    def split_small(flat):
        out, o_ = {}, 0
        fl = flat.reshape(-1)
        for nm, w, _, _ in small:
            out[nm] = fl[o_:o_ + w.size].reshape(w.shape)
            o_ += w.size
        return out

    order = ["ffn1_pre_w", "ffn1_w1", "ffn1_w2", "ffn1_post_w", "mix_pre_w", "w_in", "mla_q_norm_w", "mla_w_uq",
             "mla_kv_norm_w", "mla_w_ukv", "ret_gn_w", "w_branch_mla", "w_branch_ret", "w_out", "mix_post_w",
             "ffn2_pre_w", "ffn2_w1", "ffn2_w2", "ffn2_post_w"]
    outs = [loss, dx[None]]
    for i, fs in enumerate((gs, ds, nms, nvs)):
        both = {**{nm: big_out[nm][i] for nm in big_out}, **split_small(fs)}
        outs += [both[nm] for nm in order]
    return tuple(outs)
```

```python
import math

import numpy as np
import jax
import jax.numpy as jnp
from jax import lax
from jax.experimental import pallas as pl
from jax.experimental.pallas import tpu as pltpu

F32, BF16 = jnp.float32, jnp.bfloat16

MLA_HEADS, MLA_NOPE, MLA_ROPE, MLA_V = 8, 64, 32, 64
MLA_Q_RANK, MLA_KV_RANK = 384, 256
RET_HEADS, RET_DK, RET_DV = 4, 64, 128
ROPE_BASE, NORM_EPS, GN_EPS = 10000.0, 1e-6, 1e-6
ADAM_LR, ADAM_B1, ADAM_B2, ADAM_EPS, ADAM_WD, ADAM_STEP = 0.001, 0.9, 0.999, 1e-08, 0.01, 10
ATTN_SCALE = 1.0 / math.sqrt(MLA_NOPE + MLA_ROPE)

N_DEV = 8
LANES = 128
HP = LANES
QW = MLA_HEADS * HP
RW = RET_HEADS * HP
AW = 1024
PROJ_FIXED = 4 * RW + AW
NEG = -1e30

TOKEN_TILE = 512
ATTN_TILE = 512
ATTN_CHAINS = 2
RET_TILE = 256
FF_TILE_CAP = 512
GRAD_TILE_CAP = 1408
MERGE_TILE = 256
VMEM_LIMIT = 56 * 1024 * 1024


def _tile(n, cap, mult=LANES):
    if n <= cap:
        return n
    best = None
    for t in range(mult, cap + 1, mult):
        if n % t == 0:
            best = t
    assert best is not None, (n, cap, mult)
    return best


def _params(sem):
    return pltpu.CompilerParams(dimension_semantics=sem, vmem_limit_bytes=VMEM_LIMIT)


def _dot(a, b):
    return lax.dot_general(a, b, (((1,), (0,)), ((), ())), preferred_element_type=F32)


def _dot_nt(a, b):
    return lax.dot_general(a, b, (((1,), (1,)), ((), ())), preferred_element_type=F32)


def _dot_tn(a, b):
    return lax.dot_general(a, b, (((0,), (0,)), ((), ())), preferred_element_type=F32)


def _sigmoid(x):
    return 1.0 / (1.0 + jnp.exp(-x))


def _rms_fwd(x, w):
    r = lax.rsqrt(jnp.mean(x * x, axis=-1, keepdims=True) + NORM_EPS)
    return x * r * w


def _rms_bwd(x, w, dy):
    r = lax.rsqrt(jnp.mean(x * x, axis=-1, keepdims=True) + NORM_EPS)
    xh = x * r
    g = dy * w
    dx = r * (g - xh * jnp.mean(g * xh, axis=-1, keepdims=True))
    return dx, jnp.sum(dy * xh, axis=0, keepdims=True)


def _rope_table(first, half):
    inv = (np.float32(ROPE_BASE) ** (-(np.arange(half, dtype=np.float32) / np.float32(half)))).astype(np.float32)
    tab = np.zeros((8, LANES), np.float32)
    tab[0, first:first + half] = inv
    tab[0, first + half:first + 2 * half] = inv
    tab[1, first:first + half] = -1.0
    tab[2, first + half:first + 2 * half] = 1.0
    return jnp.asarray(tab)


def _rope_cs(pos, tab_ref):
    ang = pos * tab_ref[0:1, :]
    s = jnp.sin(ang)
    return jnp.cos(ang), s * tab_ref[1:2, :], s * tab_ref[2:3, :]


def _rope(x, cs, half, inverse=False):
    c, s1, s2 = cs
    a = pltpu.roll(x, LANES - half, 1) * s1 + pltpu.roll(x, half, 1) * s2
    return x * c - a if inverse else x * c + a


def _call(body, *, name, grid, in_specs, out_specs, out_shape, scratch_shapes, args, exchange=None):
    sem = ("arbitrary",) * len(grid)
    if exchange is None:
        return pl.pallas_call(body, name=name, grid=grid, in_specs=in_specs, out_specs=out_specs,
                              out_shape=out_shape, scratch_shapes=scratch_shapes, compiler_params=_params(sem))(*args)
    n_in, n_out, e = len(in_specs), len(out_specs), exchange.n
    total = math.prod(grid)

    def carried(*refs):
        own = refs[:n_in] + refs[n_in + e:n_in + e + n_out] + refs[n_in + 2 * e + n_out:len(refs) - 3]
        ex_refs = (refs[n_in:n_in + e], refs[n_in + e + n_out:n_in + 2 * e + n_out], refs[len(refs) - 3:])
        step = pl.program_id(0)
        for d in range(1, len(grid)):
            step = step * grid[d] + pl.program_id(d)

        @pl.when(step == 0)
        def _():
            exchange.phase(0, *ex_refs)

        @pl.when(step == total // 2)
        def _():
            exchange.phase(1, *ex_refs)

        body(*own)

        @pl.when(step == total - 1)
        def _():
            exchange.phase(2, *ex_refs)

    anyspec = pl.BlockSpec(memory_space=pl.ANY)
    return pl.pallas_call(
        carried, name=name, grid=grid, in_specs=list(in_specs) + [anyspec] * e,
        out_specs=list(out_specs) + [anyspec] * e, out_shape=list(out_shape) + exchange.out_shape,
        scratch_shapes=list(scratch_shapes) + exchange.scratch, compiler_params=_params(sem),
    )(*args, *exchange.operands)


def _ffn_fwd(h, pre_w, w1, w2, post_w, target, *, name, exchange=None):
    T, D = h.shape
    nk, ck = w2.shape[0], w2.shape[1]
    tT = min(TOKEN_TILE, T)
    nT = T // tT
    with_loss = target is not None

    def body(*refs):
        if with_loss:
            (h_ref, pre_ref, w1g_ref, w1u_ref, w2_ref, post_ref, tgt_ref,
             u_ref, f_ref, ho_ref, dy_ref, loss_ref, a_s, acc) = refs
        else:
            (h_ref, pre_ref, w1g_ref, w1u_ref, w2_ref, post_ref,
             u_ref, f_ref, ho_ref, a_s, acc) = refs
        k = pl.program_id(1)

        @pl.when(k == 0)
        def _():
            a_s[...] = _rms_fwd(h_ref[...], pre_ref[...]).astype(BF16)
            acc[...] = jnp.zeros_like(acc)

        a = a_s[...]
        ug = _dot(a, w1g_ref[...])
        uu = _dot(a, w1u_ref[...])
        u_ref[0] = ug.astype(BF16)
        u_ref[1] = uu.astype(BF16)
        acc[...] += _dot((ug * _sigmoid(ug) * uu).astype(BF16), w2_ref[...])

        @pl.when(k == nk - 1)
        def _():
            f = acc[...]
            f_ref[...] = f
            ho = h_ref[...] + 0.5 * _rms_fwd(f, post_ref[...])
            ho_ref[...] = ho
            if with_loss:
                e = ho - tgt_ref[...]
                dy_ref[...] = e * (1.0 / D)
                loss_ref[...] = jnp.full(loss_ref.shape, (0.5 / D) * jnp.sum(e * e), F32)

    row = pl.BlockSpec((tT, D), lambda i, k: (i, 0))
    vec = pl.BlockSpec((1, D), lambda i, k: (0, 0))
    in_specs = [row, vec,
                pl.BlockSpec((None, D, ck), lambda i, k: (k, 0, 0)),
                pl.BlockSpec((None, D, ck), lambda i, k: (nk + k, 0, 0)),
                pl.BlockSpec((None, ck, D), lambda i, k: (k, 0, 0)),
                vec]
    out_shape = [jax.ShapeDtypeStruct((2, nk, T, ck), BF16),
                 jax.ShapeDtypeStruct((T, D), F32),
                 jax.ShapeDtypeStruct((T, D), F32)]
    out_specs = [pl.BlockSpec((2, None, tT, ck), lambda i, k: (0, k, i, 0)), row, row]
    args = [h, pre_w, w1, w1, w2, post_w]
    if with_loss:
        in_specs.append(row)
        args.append(target)
        out_shape += [jax.ShapeDtypeStruct((T, D), F32), jax.ShapeDtypeStruct((nT * 8, LANES), F32)]
        out_specs += [row, pl.BlockSpec((8, LANES), lambda i, k: (i, 0))]
    return _call(body, name=name, grid=(nT, nk), in_specs=in_specs, out_specs=out_specs, out_shape=out_shape,
                 scratch_shapes=[pltpu.VMEM((tT, D), BF16), pltpu.VMEM((tT, D), F32)], args=args, exchange=exchange)


def _ffn_bwd(dho, f, post_w, h, pre_w, u, w2, w1, *, name, exchange=None):
    T, D = h.shape
    nk, ck = w2.shape[0], w2.shape[1]
    tT = min(TOKEN_TILE, T)
    nT = T // tT

    def body(dho_ref, f_ref, post_ref, h_ref, pre_ref, u_ref, w2_ref, w1g_ref, w1u_ref,
             g_ref, du_ref, df_ref, a_ref, dh_ref, gpost_ref, gpre_ref, df_s, da_acc):
        i, k = pl.program_id(0), pl.program_id(1)

        @pl.when(jnp.logical_and(i == 0, k == 0))
        def _():
            gpost_ref[...] = jnp.zeros_like(gpost_ref)
            gpre_ref[...] = jnp.zeros_like(gpre_ref)

        @pl.when(k == 0)
        def _():
            dx, dw = _rms_bwd(f_ref[...], post_ref[...], 0.5 * dho_ref[...])
            dfb = dx.astype(BF16)
            df_s[...] = dfb
            df_ref[...] = dfb
            gpost_ref[...] += dw
            a_ref[...] = _rms_fwd(h_ref[...], pre_ref[...]).astype(BF16)
            da_acc[...] = jnp.zeros_like(da_acc)

        dg = _dot_nt(df_s[...], w2_ref[...])
        ug = u_ref[0].astype(F32)
        uu = u_ref[1].astype(F32)
        sg = _sigmoid(ug)
        sl = ug * sg
        g_ref[...] = (sl * uu).astype(BF16)
        dug = (dg * uu * (sg * (1.0 + ug * (1.0 - sg)))).astype(BF16)
        duu = (dg * sl).astype(BF16)
        du_ref[0] = dug
        du_ref[1] = duu
        da_acc[...] += _dot_nt(dug, w1g_ref[...]) + _dot_nt(duu, w1u_ref[...])

        @pl.when(k == nk - 1)
        def _():
            dx, dw = _rms_bwd(h_ref[...], pre_ref[...], da_acc[...])
            dh_ref[...] = dho_ref[...] + dx
            gpre_ref[...] += dw

    row = pl.BlockSpec((tT, D), lambda i, k: (i, 0))
    vec = pl.BlockSpec((1, D), lambda i, k: (0, 0))
    return _call(
        body, name=name, grid=(nT, nk),
        in_specs=[row, row, vec, row, vec,
                  pl.BlockSpec((2, None, tT, ck), lambda i, k: (0, k, i, 0)),
                  pl.BlockSpec((None, ck, D), lambda i, k: (k, 0, 0)),
                  pl.BlockSpec((None, D, ck), lambda i, k: (k, 0, 0)),
                  pl.BlockSpec((None, D, ck), lambda i, k: (nk + k, 0, 0))],
        out_specs=[pl.BlockSpec((None, tT, ck), lambda i, k: (k, i, 0)),
                   pl.BlockSpec((2, None, tT, ck), lambda i, k: (0, k, i, 0)),
                   row, row, row, vec, vec],
        out_shape=[jax.ShapeDtypeStruct((nk, T, ck), BF16),
                   jax.ShapeDtypeStruct((2, nk, T, ck), BF16),
                   jax.ShapeDtypeStruct((T, D), BF16),
                   jax.ShapeDtypeStruct((T, D), BF16),
                   jax.ShapeDtypeStruct((T, D), F32),
                   jax.ShapeDtypeStruct((1, D), F32),
                   jax.ShapeDtypeStruct((1, D), F32)],
        scratch_shapes=[pltpu.VMEM((tT, D), BF16), pltpu.VMEM((tT, D), F32)],
        args=(dho, f, post_w, h, pre_w, u, w2, w1, w1), exchange=exchange)


def _matmul_tn(x, dy, *, name, exchange=None):
    Px, T, K = x.shape
    Py, _, N = dy.shape
    P = max(Px, Py)
    tT, tK, tN = min(TOKEN_TILE, T), _tile(K, GRAD_TILE_CAP), _tile(N, GRAD_TILE_CAP)
    nt = T // tT

    def body(x_ref, dy_ref, o_ref, acc):
        t = pl.program_id(3)

        @pl.when(t == 0)
        def _():
            acc[...] = jnp.zeros_like(acc)

        acc[...] += _dot_tn(x_ref[...], dy_ref[...])

        @pl.when(t == nt - 1)
        def _():
            o_ref[...] = acc[...].astype(BF16)

    return _call(
        body, name=name, grid=(P, K // tK, N // tN, nt),
        in_specs=[pl.BlockSpec((None, tT, tK), lambda p, a, b, t: (p if Px > 1 else 0, t, a)),
                  pl.BlockSpec((None, tT, tN), lambda p, a, b, t: (p if Py > 1 else 0, t, b))],
        out_specs=[pl.BlockSpec((None, tK, tN), lambda p, a, b, t: (p, a, b))],
        out_shape=[jax.ShapeDtypeStruct((P, K, N), BF16)],
        scratch_shapes=[pltpu.VMEM((tK, tN), F32)], args=(x, dy), exchange=exchange)


def _rms_matmul(h, wn, w, *, name):
    T, D = h.shape
    N = w.shape[1]
    tT, tN = min(TOKEN_TILE, T), _tile(N, 1024)

    def body(h_ref, wn_ref, w_ref, y_ref, a_ref):
        @pl.when(pl.program_id(1) == 0)
        def _():
            a_ref[...] = _rms_fwd(h_ref[...], wn_ref[...]).astype(BF16)

        y_ref[...] = _dot(a_ref[...], w_ref[...]).astype(BF16)

    return pl.pallas_call(
        body, name=name, grid=(T // tT, N // tN),
        in_specs=[pl.BlockSpec((tT, D), lambda i, j: (i, 0)),
                  pl.BlockSpec((1, D), lambda i, j: (0, 0)),
                  pl.BlockSpec((D, tN), lambda i, j: (0, j))],
        out_specs=[pl.BlockSpec((tT, tN), lambda i, j: (i, j)),
                   pl.BlockSpec((tT, D), lambda i, j: (i, 0))],
        out_shape=[jax.ShapeDtypeStruct((T, N), BF16), jax.ShapeDtypeStruct((T, D), BF16)],
        compiler_params=_params(("parallel", "arbitrary")),
    )(h, wn, w)


def _proj_bwd(dproj, w, h, wn, dres, *, name):
    T, D = h.shape
    N = w.shape[1]
    tT, tN = min(TOKEN_TILE, T), _tile(N, 1024)
    nn = N // tN

    def body(dp_ref, w_ref, h_ref, wn_ref, dres_ref, dh_ref, gw_ref, acc):
        i, j = pl.program_id(0), pl.program_id(1)

        @pl.when(jnp.logical_and(i == 0, j == 0))
        def _():
            gw_ref[...] = jnp.zeros_like(gw_ref)

        @pl.when(j == 0)
        def _():
            acc[...] = jnp.zeros_like(acc)

        acc[...] += _dot_nt(dp_ref[...], w_ref[...])

        @pl.when(j == nn - 1)
        def _():
            dx, dw = _rms_bwd(h_ref[...], wn_ref[...], acc[...])
            dh_ref[...] = dres_ref[...] + dx
            gw_ref[...] += dw

    row = pl.BlockSpec((tT, D), lambda i, j: (i, 0))
    vec = pl.BlockSpec((1, D), lambda i, j: (0, 0))
    return pl.pallas_call(
        body, name=name, grid=(T // tT, nn),
        in_specs=[pl.BlockSpec((tT, tN), lambda i, j: (i, j)),
                  pl.BlockSpec((D, tN), lambda i, j: (0, j)), row, vec, row],
        out_specs=[row, vec],
        out_shape=[jax.ShapeDtypeStruct((T, D), F32), jax.ShapeDtypeStruct((1, D), F32)],
        scratch_shapes=[pltpu.VMEM((tT, D), F32)],
        compiler_params=_params(("arbitrary", "arbitrary")),
    )(dproj, w, h, wn, dres)


def _mla_prep_fwd(proj, pos, qn_w, kvn_w, w_uq, w_kv, tab, *, name):
    T = proj.shape[0]
    tT = min(TOKEN_TILE, T)
    a_blk = PROJ_FIXED // AW - 1

    def body(a_ref, pos_ref, qnw_ref, kvnw_ref, wuq_ref, wkv_ref, tab_ref,
             q_ref, k_ref, v_ref, qn_ref, kvn_ref):
        cq = a_ref[:, 0:MLA_Q_RANK].astype(F32)
        ckv = a_ref[:, MLA_Q_RANK:MLA_Q_RANK + MLA_KV_RANK].astype(F32)
        kr = a_ref[:, 640:768].astype(F32)
        qn = _rms_fwd(cq, qnw_ref[...]).astype(BF16)
        kvn = _rms_fwd(ckv, kvnw_ref[...]).astype(BF16)
        qn_ref[...] = qn
        kvn_ref[...] = kvn
        cs = _rope_cs(pos_ref[...], tab_ref)
        q = _dot(qn, wuq_ref[...])
        kv = _dot(kvn, wkv_ref[...])
        krr = _rope(kr, cs, MLA_ROPE // 2)
        for hd in range(MLA_HEADS):
            sl = slice(hd * HP, (hd + 1) * HP)
            q_ref[:, sl] = (_rope(q[:, sl], cs, MLA_ROPE // 2) * ATTN_SCALE).astype(BF16)
            k_ref[:, sl] = (kv[:, sl] + krr).astype(BF16)
        v_ref[...] = kv[:, QW:].astype(BF16)

    def full(r, c):
        return pl.BlockSpec((r, c), lambda i: (0, 0))

    def rows(c):
        return pl.BlockSpec((tT, c), lambda i: (i, 0))

    return pl.pallas_call(
        body, name=name, grid=(T // tT,),
        in_specs=[pl.BlockSpec((tT, AW), lambda i: (i, a_blk)), rows(1),
                  full(1, MLA_Q_RANK), full(1, MLA_KV_RANK),
                  full(MLA_Q_RANK, QW), full(MLA_KV_RANK, 2 * QW), full(8, LANES)],
        out_specs=[rows(QW), rows(QW), rows(QW), rows(MLA_Q_RANK), rows(MLA_KV_RANK)],
        out_shape=[jax.ShapeDtypeStruct((T, QW), BF16)] * 3
        + [jax.ShapeDtypeStruct((T, MLA_Q_RANK), BF16), jax.ShapeDtypeStruct((T, MLA_KV_RANK), BF16)],
        compiler_params=_params(("parallel",)),
    )(proj, pos, qn_w, kvn_w, w_uq, w_kv, tab)


def _mla_prep_bwd(dq, dk, dv, proj, pos, qn_w, kvn_w, w_uq, w_kv, tab, *, name):
    T = proj.shape[0]
    tT = min(TOKEN_TILE, T)
    a_blk = PROJ_FIXED // AW - 1

    def body(dq_ref, dk_ref, dv_ref, a_ref, pos_ref, qnw_ref, kvnw_ref, wuq_ref, wkv_ref, tab_ref,
             da_ref, dql_ref, dkvl_ref, gqn_ref, gkvn_ref):
        @pl.when(pl.program_id(0) == 0)
        def _():
            gqn_ref[...] = jnp.zeros_like(gqn_ref)
            gkvn_ref[...] = jnp.zeros_like(gkvn_ref)

        cs = _rope_cs(pos_ref[...], tab_ref)
        dkr = jnp.zeros((tT, HP), F32)
        for hd in range(MLA_HEADS):
            sl = slice(hd * HP, (hd + 1) * HP)
            dql_ref[:, sl] = (_rope(dq_ref[:, sl], cs, MLA_ROPE // 2, inverse=True) * ATTN_SCALE).astype(BF16)
            dkh = dk_ref[:, sl]
            dkr = dkr + dkh
            dkvl_ref[:, sl] = dkh.astype(BF16)
        dkvl_ref[:, QW:] = dv_ref[...]
        dqn = _dot_nt(dql_ref[...], wuq_ref[...])
        dkvn = _dot_nt(dkvl_ref[...], wkv_ref[...])
        cq = a_ref[:, 0:MLA_Q_RANK].astype(F32)
        ckv = a_ref[:, MLA_Q_RANK:MLA_Q_RANK + MLA_KV_RANK].astype(F32)
        dcq, gq = _rms_bwd(cq, qnw_ref[...], dqn)
        dckv, gkv = _rms_bwd(ckv, kvnw_ref[...], dkvn)
        gqn_ref[...] += gq
        gkvn_ref[...] += gkv
        da_ref[:, 0:MLA_Q_RANK] = dcq.astype(BF16)
        da_ref[:, MLA_Q_RANK:MLA_Q_RANK + MLA_KV_RANK] = dckv.astype(BF16)
        da_ref[:, 640:768] = _rope(dkr, cs, MLA_ROPE // 2, inverse=True).astype(BF16)
        da_ref[:, 768:AW] = jnp.zeros((tT, AW - 768), BF16)

    def full(r, c):
        return pl.BlockSpec((r, c), lambda i: (0, 0))

    def rows(c):
        return pl.BlockSpec((tT, c), lambda i: (i, 0))

    return pl.pallas_call(
        body, name=name, grid=(T // tT,),
        in_specs=[rows(QW), rows(QW), rows(QW), pl.BlockSpec((tT, AW), lambda i: (i, a_blk)), rows(1),
                  full(1, MLA_Q_RANK), full(1, MLA_KV_RANK),
                  full(MLA_Q_RANK, QW), full(MLA_KV_RANK, 2 * QW), full(8, LANES)],
        out_specs=[rows(AW), rows(QW), rows(2 * QW), full(1, MLA_Q_RANK), full(1, MLA_KV_RANK)],
        out_shape=[jax.ShapeDtypeStruct((T, AW), BF16), jax.ShapeDtypeStruct((T, QW), BF16),
                   jax.ShapeDtypeStruct((T, 2 * QW), BF16),
                   jax.ShapeDtypeStruct((1, MLA_Q_RANK), F32), jax.ShapeDtypeStruct((1, MLA_KV_RANK), F32)],
        compiler_params=_params(("arbitrary",)),
    )(dq, dk, dv, proj, pos, qn_w, kvn_w, w_uq, w_kv, tab)


def _flash_fwd(q, k, v, *, name, exchange=None):
    T = q.shape[0]
    H = q.shape[1] // HP
    tq = min(ATTN_TILE, T)
    nq = T // tq

    sub = tq // ATTN_CHAINS

    def body(q_ref, k_ref, v_ref, o_ref, lse_ref):
        qi = pl.program_id(1)
        qs = [q_ref[c * sub:(c + 1) * sub, :] for c in range(ATTN_CHAINS)]

        def update(carry, off, masked):
            kb = k_ref[pl.ds(off, tq), :]
            vb = v_ref[pl.ds(off, tq), :]
            out = []
            for c in range(ATTN_CHAINS):
                m_prev, l_prev, acc = carry[c]
                s = _dot_nt(qs[c], kb)
                if masked:
                    rows = lax.broadcasted_iota(jnp.int32, (sub, tq), 0) + c * sub
                    s = jnp.where(rows >= lax.broadcasted_iota(jnp.int32, (sub, tq), 1), s, NEG)
                m_new = jnp.maximum(m_prev, jnp.max(s, axis=1, keepdims=True))
                alpha = jnp.exp(m_prev - m_new)
                p = jnp.exp(s - m_new)
                out.append((m_new, alpha * l_prev + jnp.sum(p, axis=1, keepdims=True),
                            alpha * acc + _dot(p.astype(BF16), vb)))
            return tuple(out)

        init = tuple((jnp.full((sub, 1), NEG, F32), jnp.zeros((sub, 1), F32), jnp.zeros((sub, HP), F32))
                     for _ in range(ATTN_CHAINS))
        carry = lax.fori_loop(0, qi, lambda j, cr: update(cr, pl.multiple_of(j * tq, tq), False), init)
        carry = update(carry, pl.multiple_of(qi * tq, tq), True)
        for c in range(ATTN_CHAINS):
            m_fin, l_fin, acc = carry[c]
            o_ref[c * sub:(c + 1) * sub, :] = (acc / l_fin).astype(BF16)
            lse_ref[c * sub:(c + 1) * sub, :] = jnp.broadcast_to(m_fin + jnp.log(l_fin), (sub, HP))

    qspec = pl.BlockSpec((tq, HP), lambda h, i: (i, h))
    kspec = pl.BlockSpec((T, HP), lambda h, i: (0, h))
    return _call(
        body, name=name, grid=(H, nq),
        in_specs=[qspec, kspec, kspec], out_specs=[qspec, qspec],
        out_shape=[jax.ShapeDtypeStruct((T, H * HP), BF16), jax.ShapeDtypeStruct((T, H * HP), F32)],
        scratch_shapes=[], args=(q, k, v), exchange=exchange)


def _flash_bwd(q, k, v, do, lse, delta, *, name, exchange=None):
    T = q.shape[0]
    H = q.shape[1] // HP
    tq = min(ATTN_TILE, T)
    nq = T // tq
    sub = tq // ATTN_CHAINS

    def body(k_ref, v_ref, q_ref, do_ref, lse_ref, dl_ref, dq_ref, dk_ref, dv_ref):
        ki = pl.program_id(1)

        @pl.when(ki == 0)
        def _():
            dq_ref[...] = jnp.zeros_like(dq_ref)

        kb = k_ref[...]
        vb = v_ref[...]

        def step(carry, j, masked):
            dk_acc, dv_acc = carry
            for c in range(ATTN_CHAINS):
                rows = pl.ds(pl.multiple_of(j * tq + c * sub, sub), sub)
                qb = q_ref[rows, :]
                dob = do_ref[rows, :]
                s = _dot_nt(qb, kb)
                if masked:
                    ri = lax.broadcasted_iota(jnp.int32, (sub, tq), 0) + c * sub
                    s = jnp.where(ri >= lax.broadcasted_iota(jnp.int32, (sub, tq), 1), s, NEG)
                p = jnp.exp(s - lse_ref[rows, 0:1])
                dv_acc = dv_acc + _dot_tn(p.astype(BF16), dob)
                dp = _dot_nt(dob, vb)
                ds = (p * (dp - dl_ref[rows, 0:1])).astype(BF16)
                dk_acc = dk_acc + _dot_tn(ds, qb)
                dq_ref[rows, :] += _dot(ds, kb)
            return dk_acc, dv_acc

        carry = step((jnp.zeros((tq, HP), F32), jnp.zeros((tq, HP), F32)), ki, True)
        dk_acc, dv_acc = lax.fori_loop(ki + 1, nq, lambda j, cr: step(cr, j, False), carry)
        dk_ref[...] = dk_acc
        dv_ref[...] = dv_acc.astype(BF16)

    kspec = pl.BlockSpec((tq, HP), lambda h, j: (j, h))
    full = pl.BlockSpec((T, HP), lambda h, j: (0, h))
    return _call(
        body, name=name, grid=(H, nq),
        in_specs=[kspec, kspec, full, full, full, full], out_specs=[full, kspec, kspec],
        out_shape=[jax.ShapeDtypeStruct((T, H * HP), F32), jax.ShapeDtypeStruct((T, H * HP), F32),
                   jax.ShapeDtypeStruct((T, H * HP), BF16)],
        scratch_shapes=[], args=(k, v, q, do, lse, delta), exchange=exchange)


def _ret_consts(cc, hd):
    lg = math.log(1.0 - 2.0 ** (-5.0 - hd))
    diff = (lax.broadcasted_iota(jnp.int32, (cc, cc), 0) - lax.broadcasted_iota(jnp.int32, (cc, cc), 1)).astype(F32)
    decay = jnp.where(diff >= 0, jnp.exp(jnp.maximum(diff, 0.0) * lg), 0.0)
    idx = lax.broadcasted_iota(jnp.int32, (cc, 1), 0).astype(F32)
    zeta = jnp.exp((cc - 1.0 - idx) * lg)
    xi = jnp.exp((idx + 1.0) * lg)
    return decay, zeta, xi, math.exp(cc * lg)


def _ret_fwd(proj, pos, tab, *, name):
    T = proj.shape[0]
    cc = min(RET_TILE, T)
    n = T // cc

    def body(rq_ref, rk_ref, rv_ref, pos_ref, tab_ref, y_ref, yn_ref, rprev_ref, r_s):
        @pl.when(pl.program_id(0) == 0)
        def _():
            r_s[...] = jnp.zeros_like(r_s)

        cs = _rope_cs(pos_ref[...], tab_ref)
        for hd in range(RET_HEADS):
            sl = slice(hd * HP, (hd + 1) * HP)
            decay, zeta, xi, gc = _ret_consts(cc, hd)
            q = _rope(rq_ref[:, sl].astype(F32), cs, RET_DK // 2).astype(BF16)
            kf = _rope(rk_ref[:, sl].astype(F32), cs, RET_DK // 2) * (RET_DK ** -0.5)
            k = kf.astype(BF16)
            v = rv_ref[:, sl]
            r = r_s[hd]
            rprev_ref[0, hd] = r
            inner = (_dot_nt(q, k) * decay).astype(BF16)
            y = _dot(inner, v) + _dot(q, r.astype(BF16)) * xi
            r_s[hd] = r * gc + _dot_tn((kf * zeta).astype(BF16), v)
            y_ref[:, sl] = y
            mu = jnp.mean(y, axis=-1, keepdims=True)
            yc = y - mu
            var = jnp.mean(yc * yc, axis=-1, keepdims=True)
            yn_ref[:, sl] = (yc * lax.rsqrt(var + GN_EPS)).astype(BF16)

    def blk(j):
        return pl.BlockSpec((cc, RW), lambda i: (i, j))

    return pl.pallas_call(
        body, name=name, grid=(n,),
        in_specs=[blk(0), blk(1), blk(2), pl.BlockSpec((cc, 1), lambda i: (i, 0)),
                  pl.BlockSpec((8, LANES), lambda i: (0, 0))],
        out_specs=[blk(0), blk(0), pl.BlockSpec((1, RET_HEADS, HP, RET_DV), lambda i: (i, 0, 0, 0))],
        out_shape=[jax.ShapeDtypeStruct((T, RW), F32), jax.ShapeDtypeStruct((T, RW), BF16),
                   jax.ShapeDtypeStruct((n, RET_HEADS, HP, RET_DV), F32)],
        scratch_shapes=[pltpu.VMEM((RET_HEADS, HP, RET_DV), F32)],
        compiler_params=_params(("arbitrary",)),
    )(proj, proj, proj, pos, tab)


def _ret_bwd(dyn, y, proj, pos, tab, rprev, *, name):
    T = proj.shape[0]
    cc = min(RET_TILE, T)
    n = T // cc

    def body(dyn_ref, y_ref, rq_ref, rk_ref, rv_ref, pos_ref, tab_ref, rprev_ref,
             drq_ref, drk_ref, drv_ref, dr_s):
        @pl.when(pl.program_id(0) == 0)
        def _():
            dr_s[...] = jnp.zeros_like(dr_s)

        cs = _rope_cs(pos_ref[...], tab_ref)
        for hd in range(RET_HEADS):
            sl = slice(hd * HP, (hd + 1) * HP)
            decay, zeta, xi, gc = _ret_consts(cc, hd)
            q = _rope(rq_ref[:, sl].astype(F32), cs, RET_DK // 2).astype(BF16)
            kf = _rope(rk_ref[:, sl].astype(F32), cs, RET_DK // 2) * (RET_DK ** -0.5)
            k = kf.astype(BF16)
            v = rv_ref[:, sl]
            yv = y_ref[:, sl]
            mu = jnp.mean(yv, axis=-1, keepdims=True)
            yc = yv - mu
            rs = lax.rsqrt(jnp.mean(yc * yc, axis=-1, keepdims=True) + GN_EPS)
            yn = yc * rs
            dn = dyn_ref[:, sl]
            dy = rs * (dn - jnp.mean(dn, axis=-1, keepdims=True) - yn * jnp.mean(dn * yn, axis=-1, keepdims=True))
            dyb = dy.astype(BF16)
            dyx = (dy * xi).astype(BF16)
            dr = dr_s[hd]
            drb = dr.astype(BF16)
            inner = (_dot_nt(q, k) * decay).astype(BF16)
            da = (_dot_nt(dyb, v) * decay).astype(BF16)
            dv = _dot_tn(inner, dyb) + _dot((kf * zeta).astype(BF16), drb)
            dq = _dot(da, k) + _dot_nt(dyx, rprev_ref[0, hd].astype(BF16))
            dk = _dot_tn(da, q) + _dot_nt(v, drb) * zeta
            dr_s[hd] = dr * gc + _dot_tn(q, dyx)
            drq_ref[:, sl] = _rope(dq, cs, RET_DK // 2, inverse=True).astype(BF16)
            drk_ref[:, sl] = _rope(dk * (RET_DK ** -0.5), cs, RET_DK // 2, inverse=True).astype(BF16)
            drv_ref[:, sl] = dv.astype(BF16)

    def blk(j):
        return pl.BlockSpec((cc, RW), lambda i: (n - 1 - i, j))

    return pl.pallas_call(
        body, name=name, grid=(n,),
        in_specs=[blk(0), blk(0), blk(0), blk(1), blk(2), pl.BlockSpec((cc, 1), lambda i: (n - 1 - i, 0)),
                  pl.BlockSpec((8, LANES), lambda i: (0, 0)),
                  pl.BlockSpec((1, RET_HEADS, HP, RET_DV), lambda i: (n - 1 - i, 0, 0, 0))],
        out_specs=[blk(0), blk(0), blk(0)],
        out_shape=[jax.ShapeDtypeStruct((T, RW), BF16)] * 3,
        scratch_shapes=[pltpu.VMEM((RET_HEADS, HP, RET_DV), F32)],
        compiler_params=_params(("arbitrary",)),
    )(dyn, y, proj, proj, proj, pos, tab, rprev)


def _merge_fwd(o, yn, proj, gn_w, w_bm, w_br, w_out, h, post_w, *, name):
    T, D = h.shape
    tT = min(MERGE_TILE, T)
    g_blk = PROJ_FIXED // D

    def body(o_ref, yn_ref, rg_ref, gm_ref, gr_ref, gnw_ref, wbm_ref, wbr_ref, wout_ref, h_ref, post_ref,
             omla_ref, oret_ref, m_ref, ho_ref):
        o_mla = _dot(o_ref[...], wbm_ref[...])
        rg = rg_ref[...].astype(F32)
        gated = (rg * _sigmoid(rg) * (yn_ref[...].astype(F32) * gnw_ref[...])).astype(BF16)
        o_ret = _dot(gated, wbr_ref[...])
        omla_ref[...] = o_mla.astype(BF16)
        oret_ref[...] = o_ret.astype(BF16)
        merged = _sigmoid(gm_ref[...].astype(F32)) * o_mla + _sigmoid(gr_ref[...].astype(F32)) * o_ret
        m = _dot(merged.astype(BF16), wout_ref[...])
        m_ref[...] = m
        ho_ref[...] = h_ref[...] + _rms_fwd(m, post_ref[...])

    def full(r, c):
        return pl.BlockSpec((r, c), lambda i: (0, 0))

    def rows(c, j=0):
        return pl.BlockSpec((tT, c), lambda i: (i, j))

    return pl.pallas_call(
        body, name=name, grid=(T // tT,),
        in_specs=[rows(QW), rows(RW), rows(RW, 3), rows(D, g_blk), rows(D, g_blk + 1), full(1, RW),
                  full(QW, D), full(RW, D), full(D, D), rows(D), full(1, D)],
        out_specs=[rows(D), rows(D), rows(D), rows(D)],
        out_shape=[jax.ShapeDtypeStruct((T, D), BF16), jax.ShapeDtypeStruct((T, D), BF16),
                   jax.ShapeDtypeStruct((T, D), F32), jax.ShapeDtypeStruct((T, D), F32)],
        compiler_params=_params(("parallel",)),
    )(o, yn, proj, proj, proj, gn_w, w_bm, w_br, w_out, h, post_w)


def _merge_bwd(dho, m, post_w, omla, oret, proj, yn, gn_w, o, w_out, w_bm, w_br, *, name):
    T, D = dho.shape
    tT = min(MERGE_TILE, T)
    g_blk = PROJ_FIXED // D

    def body(dho_ref, m_ref, post_ref, omla_ref, oret_ref, rg_ref, gm_ref, gr_ref, yn_ref, gnw_ref, o_ref,
             wout_ref, wbm_ref, wbr_ref,
             dm_ref, merged_ref, dgm_ref, dgr_ref, domla_ref, do_ref, delta_ref, doret_ref, gated_ref,
             drg_ref, dyn_ref, gpost_ref, ggn_ref):
        @pl.when(pl.program_id(0) == 0)
        def _():
            gpost_ref[...] = jnp.zeros_like(gpost_ref)
            ggn_ref[...] = jnp.zeros_like(ggn_ref)

        dm, gp = _rms_bwd(m_ref[...], post_ref[...], dho_ref[...])
        gpost_ref[...] += gp
        dmb = dm.astype(BF16)
        dm_ref[...] = dmb
        dmerged = _dot_nt(dmb, wout_ref[...])
        o_mla = omla_ref[...].astype(F32)
        o_ret = oret_ref[...].astype(F32)
        sgm = _sigmoid(gm_ref[...].astype(F32))
        sgr = _sigmoid(gr_ref[...].astype(F32))
        merged_ref[...] = (sgm * o_mla + sgr * o_ret).astype(BF16)
        dgm_ref[...] = (dmerged * o_mla * sgm * (1.0 - sgm)).astype(BF16)
        dgr_ref[...] = (dmerged * o_ret * sgr * (1.0 - sgr)).astype(BF16)
        domla = (dmerged * sgm).astype(BF16)
        domla_ref[...] = domla
        do = _dot_nt(domla, wbm_ref[...])
        do_ref[...] = do.astype(BF16)
        for hd in range(MLA_HEADS):
            sl = slice(hd * HP, (hd + 1) * HP)
            d = jnp.sum(do[:, sl] * o_ref[:, sl].astype(F32), axis=-1, keepdims=True)
            delta_ref[:, sl] = jnp.broadcast_to(d, (tT, HP))
        doret = (dmerged * sgr).astype(BF16)
        doret_ref[...] = doret
        dgated = _dot_nt(doret, wbr_ref[...])
        rg = rg_ref[...].astype(F32)
        sg = _sigmoid(rg)
        srg = rg * sg
        ynv = yn_ref[...].astype(F32)
        yw = ynv * gnw_ref[...]
        gated_ref[...] = (srg * yw).astype(BF16)
        drg_ref[...] = (dgated * yw * (sg * (1.0 + rg * (1.0 - sg)))).astype(BF16)
        dgs = dgated * srg
        dyn_ref[...] = dgs * gnw_ref[...]
        ggn_ref[...] += jnp.sum(dgs * ynv, axis=0, keepdims=True)

    def full(r, c):
        return pl.BlockSpec((r, c), lambda i: (0, 0))

    def rows(c, j=0):
        return pl.BlockSpec((tT, c), lambda i: (i, j))

    return pl.pallas_call(
        body, name=name, grid=(T // tT,),
        in_specs=[rows(D), rows(D), full(1, D), rows(D), rows(D), rows(RW, 3), rows(D, g_blk), rows(D, g_blk + 1),
                  rows(RW), full(1, RW), rows(QW), full(D, D), full(QW, D), full(RW, D)],
        out_specs=[rows(D), rows(D), rows(D), rows(D), rows(D), rows(QW), rows(QW), rows(D), rows(RW),
                   rows(RW), rows(RW), full(1, D), full(1, RW)],
        out_shape=[jax.ShapeDtypeStruct((T, D), BF16)] * 5
        + [jax.ShapeDtypeStruct((T, QW), BF16), jax.ShapeDtypeStruct((T, QW), F32),
           jax.ShapeDtypeStruct((T, D), BF16), jax.ShapeDtypeStruct((T, RW), BF16),
           jax.ShapeDtypeStruct((T, RW), BF16), jax.ShapeDtypeStruct((T, RW), F32),
           jax.ShapeDtypeStruct((1, D), F32), jax.ShapeDtypeStruct((1, RW), F32)],
        compiler_params=_params(("arbitrary",)),
    )(dho, m, post_w, omla, oret, proj, proj, proj, yn, gn_w, o, w_out, w_bm, w_br)


def _mesh_pos():
    return lax.axis_index("x"), lax.axis_index("y"), lax.axis_index("c")


class _Gather:
    def __init__(self, shards):
        self.operands = list(shards)
        self.n = len(shards)
        self.out_shape = [jax.ShapeDtypeStruct((N_DEV,) + s.shape, s.dtype) for s in shards]
        self.scratch = [pltpu.SemaphoreType.DMA((7 * self.n,)), pltpu.SemaphoreType.DMA((7 * self.n,)),
                        pltpu.SemaphoreType.DMA((self.n,))]

    def phase(self, p, x_refs, out_refs, sems):
        send_sems, recv_sems, local_sems = sems
        x, y, c = _mesh_pos()
        me, sibling = (x, y, c), (x, y, 1 - c)
        chips = [(1 - x, y), (x, 1 - y), (1 - x, 1 - y)]

        def copy(w, k, block, to, src=None):
            slot = out_refs[w].at[4 * block[0] + 2 * block[1] + block[2]]
            return pltpu.make_async_remote_copy(
                src_ref=slot if src is None else src, dst_ref=slot,
                send_sem=send_sems.at[7 * w + k], recv_sem=recv_sems.at[7 * w + k],
                device_id=to, device_id_type=pl.DeviceIdType.MESH)

        for w in range(self.n):
            mine = pltpu.make_async_copy(x_refs[w], out_refs[w].at[4 * x + 2 * y + c], local_sems.at[w])
            first = [copy(w, 0, me, sibling, src=x_refs[w])]
            first += [copy(w, 1 + j, me, (*chip, c), src=x_refs[w]) for j, chip in enumerate(chips)]
            passed = [copy(w, 4 + j, (*chip, c), sibling) for j, chip in enumerate(chips)]
            if p == 0:
                mine.start()
                for cp in first:
                    cp.start()
            elif p == 1:
                for j, chip in enumerate(chips):
                    copy(w, 1 + j, (*chip, c), me).wait_recv()
                    passed[j].start()
            else:
                copy(w, 0, sibling, me).wait_recv()
                for j, chip in enumerate(chips):
                    copy(w, 4 + j, (*chip, 1 - c), me).wait_recv()
                for cp in first + passed:
                    cp.wait_send()
                mine.wait()


class _Scatter:
    def __init__(self, grads, whole=()):
        self.n_sliced = len(grads)
        self.operands = list(grads) + list(whole)
        self.n = len(self.operands)
        self.out_shape = [jax.ShapeDtypeStruct(g.shape, g.dtype) for g in grads]
        self.out_shape += [jax.ShapeDtypeStruct((N_DEV,) + a.shape, a.dtype) for a in whole]
        n_sem = (N_DEV - 1) * self.n
        self.scratch = [pltpu.SemaphoreType.DMA((n_sem,)), pltpu.SemaphoreType.DMA((n_sem,)),
                        pltpu.SemaphoreType.DMA((self.n,))]

    def phase(self, p, in_refs, out_refs, sems):
        if p == 1:
            return
        send_sems, recv_sems, local_sems = sems
        x, y, c = _mesh_pos()
        me = 4 * x + 2 * y + c

        def src(w, dev):
            return in_refs[w].at[dev] if w < self.n_sliced else in_refs[w]

        for w in range(self.n):
            own = pltpu.make_async_copy(src(w, me), out_refs[w].at[me], local_sems.at[w])
            sends, recvs = [], []
            for r in range(1, N_DEV):
                px = 1 - x if r & 4 else x
                py = 1 - y if r & 2 else y
                pc = 1 - c if r & 1 else c
                peer, pidx = (px, py, pc), 4 * px + 2 * py + pc
                k = (N_DEV - 1) * w + r - 1
                sends.append(pltpu.make_async_remote_copy(
                    src_ref=src(w, pidx), dst_ref=out_refs[w].at[me], send_sem=send_sems.at[k],
                    recv_sem=recv_sems.at[k], device_id=peer, device_id_type=pl.DeviceIdType.MESH))
                recvs.append(pltpu.make_async_remote_copy(
                    src_ref=src(w, me), dst_ref=out_refs[w].at[pidx], send_sem=send_sems.at[k],
                    recv_sem=recv_sems.at[k], device_id=peer, device_id_type=pl.DeviceIdType.MESH))
            if p == 0:
                own.start()
                for cp in sends:
                    cp.start()
            else:
                for cp in recvs:
                    cp.wait_recv()
                for cp in sends:
                    cp.wait_send()
                own.wait()


def _exchange_alone(ex, *, name):
    n = ex.n

    def body(*refs):
        for p in range(3):
            ex.phase(p, refs[:n], refs[n:2 * n], refs[2 * n:])

    anyspec = pl.BlockSpec(memory_space=pl.ANY)
    return pl.pallas_call(body, name=name, out_shape=ex.out_shape, in_specs=[anyspec] * n,
                          out_specs=[anyspec] * n, scratch_shapes=ex.scratch)(*ex.operands)


def _adamw(w, parts, m, v, *, name):
    R, n = w.shape
    tr = R
    for t in range(16, R, 16):
        if R % t == 0 and t * n <= 256 * 1024:
            tr = t
    if R * n <= 256 * 1024:
        tr = R

    def body(w_ref, p_ref, m_ref, v_ref, g_ref, d_ref, nm_ref, nv_ref):
        g = p_ref[0].astype(F32)
        for j in range(1, N_DEV):
            g = g + p_ref[j].astype(F32)
        g_ref[...] = g
        nm = ADAM_B1 * m_ref[...] + (1.0 - ADAM_B1) * g
        nv = ADAM_B2 * v_ref[...] + (1.0 - ADAM_B2) * (g * g)
        nm_ref[...] = nm
        nv_ref[...] = nv
        m_hat = nm / (1.0 - ADAM_B1 ** ADAM_STEP)
        v_hat = nv / (1.0 - ADAM_B2 ** ADAM_STEP)
        d_ref[...] = -ADAM_LR * (m_hat / (jnp.sqrt(v_hat) + ADAM_EPS) + ADAM_WD * w_ref[...])

    row = pl.BlockSpec((tr, n), lambda i: (i, 0))
    return pl.pallas_call(
        body, name=name, grid=(R // tr,),
        in_specs=[row, pl.BlockSpec((N_DEV, tr, n), lambda i: (0, i, 0)), row, row],
        out_specs=[row, row, row, row],
        out_shape=[jax.ShapeDtypeStruct((R, n), F32)] * 4,
        compiler_params=_params(("parallel",)),
    )(w, parts, m, v)


def _pad_last(a, width):
    return jnp.pad(a, [(0, 0)] * (a.ndim - 1) + [(0, width - a.shape[-1])])


def _cols_of(g):
    return g.transpose(1, 0, 2).reshape(g.shape[1], N_DEV * g.shape[2])


def _col_shards(w):
    return w.reshape(w.shape[0], N_DEV, w.shape[1] // N_DEV).transpose(1, 0, 2)


def kernel(x, positions, ffn1_pre_w, ffn1_w1, ffn1_w2, ffn1_post_w, mix_pre_w, w_in, mla_q_norm_w, mla_w_uq, mla_kv_norm_w, mla_w_ukv, ret_gn_w, w_branch_mla, w_branch_ret, w_out, mix_post_w, ffn2_pre_w, ffn2_w1, ffn2_w2, ffn2_post_w, loss_target, m_ffn1_pre_w, m_ffn1_w1, m_ffn1_w2, m_ffn1_post_w, m_mix_pre_w, m_w_in, m_mla_q_norm_w, m_mla_w_uq, m_mla_kv_norm_w, m_mla_w_ukv, m_ret_gn_w, m_w_branch_mla, m_w_branch_ret, m_w_out, m_mix_post_w, m_ffn2_pre_w, m_ffn2_w1, m_ffn2_w2, m_ffn2_post_w, v_ffn1_pre_w, v_ffn1_w1, v_ffn1_w2, v_ffn1_post_w, v_mix_pre_w, v_w_in, v_mla_q_norm_w, v_mla_w_uq, v_mla_kv_norm_w, v_mla_w_ukv, v_ret_gn_w, v_w_branch_mla, v_w_branch_ret, v_w_out, v_mix_post_w, v_ffn2_pre_w, v_ffn2_w1, v_ffn2_w2, v_ffn2_post_w):
    T, D = x.shape[1], x.shape[2]
    h0 = x[0]
    tgt = loss_target[0]
    pos = positions.reshape(T, 1).astype(F32)

    big = [("ffn1_w1", ffn1_w1, m_ffn1_w1, v_ffn1_w1), ("ffn1_w2", ffn1_w2, m_ffn1_w2, v_ffn1_w2),
           ("w_in", w_in, m_w_in, v_w_in), ("mla_w_uq", mla_w_uq, m_mla_w_uq, v_mla_w_uq),
           ("mla_w_ukv", mla_w_ukv, m_mla_w_ukv, v_mla_w_ukv),
           ("w_branch_mla", w_branch_mla, m_w_branch_mla, v_w_branch_mla),
           ("w_branch_ret", w_branch_ret, m_w_branch_ret, v_w_branch_ret),
           ("w_out", w_out, m_w_out, v_w_out),
           ("ffn2_w1", ffn2_w1, m_ffn2_w1, v_ffn2_w1), ("ffn2_w2", ffn2_w2, m_ffn2_w2, v_ffn2_w2)]
    small = [("ffn1_pre_w", ffn1_pre_w, m_ffn1_pre_w, v_ffn1_pre_w), ("ffn1_post_w", ffn1_post_w, m_ffn1_post_w, v_ffn1_post_w),
             ("mix_pre_w", mix_pre_w, m_mix_pre_w, v_mix_pre_w), ("mla_q_norm_w", mla_q_norm_w, m_mla_q_norm_w, v_mla_q_norm_w),
             ("mla_kv_norm_w", mla_kv_norm_w, m_mla_kv_norm_w, v_mla_kv_norm_w), ("ret_gn_w", ret_gn_w, m_ret_gn_w, v_ret_gn_w),
             ("mix_post_w", mix_post_w, m_mix_post_w, v_mix_post_w), ("ffn2_pre_w", ffn2_pre_w, m_ffn2_pre_w, v_ffn2_pre_w),
             ("ffn2_post_w", ffn2_post_w, m_ffn2_post_w, v_ffn2_post_w)]

    half = ffn1_w2.shape[1]
    hp = -(-half // LANES) * LANES

    def send_w1(w):
        return _pad_last(w[0].reshape(D, 2, half), hp).reshape(D, 2 * hp).astype(BF16)

    def send_w2(w):
        return jnp.pad(w[0], ((0, hp - half), (0, 0))).astype(BF16)

    mixer = ["w_in", "mla_w_uq", "mla_w_ukv", "w_branch_mla", "w_branch_ret", "w_out"]
    mixer_send = [w[0].astype(BF16) for nm, w, _, _ in big if nm in mixer]

    w1a, w2a = _exchange_alone(_Gather([send_w1(ffn1_w1), send_w2(ffn1_w2)]), name="gather_ffn1")
    w2a = w2a.reshape(N_DEV // 2, 2 * hp, D)
    u1, f1, h1, *got = _ffn_fwd(h0, ffn1_pre_w, w1a, w2a, ffn1_post_w, None, name="ffn1_fwd_gather_mixer",
                                exchange=_Gather(mixer_send))
    fw = dict(zip(mixer, got))

    wi = _cols_of(fw["w_in"])
    cq_w, ckv_w, kr_w = wi[:, 0:384], wi[:, 384:640], wi[:, 640:672]
    rq_w, rk_w = wi[:, 672:928], wi[:, 928:1184]
    rv_w, rg_w = wi[:, 1184:1696], wi[:, 1696:2208]
    gm_w, gr_w = wi[:, 2208:2208 + D], wi[:, 2208 + D:2208 + 2 * D]
    zer = lambda n: jnp.zeros((D, n), BF16)
    head_pad = lambda a, h: _pad_last(a.reshape(a.shape[0], h, -1), HP).reshape(a.shape[0], h * HP)
    w_in_p = jnp.concatenate([head_pad(rq_w, RET_HEADS), head_pad(rk_w, RET_HEADS), rv_w, rg_w,
                              cq_w, ckv_w, zer(MLA_NOPE), kr_w, zer(HP - MLA_NOPE - MLA_ROPE), zer(AW - 768),
                              gm_w, gr_w], axis=1)
    w_uq_p = _cols_of(_pad_last(fw["mla_w_uq"], HP))
    ukv = fw["mla_w_ukv"].transpose(1, 0, 2)
    w_kv_p = jnp.concatenate([_pad_last(ukv[:, :, :MLA_NOPE], HP).reshape(MLA_KV_RANK, QW),
                              _pad_last(ukv[:, :, MLA_NOPE:], HP).reshape(MLA_KV_RANK, QW)], axis=1)
    w_bm_p = jnp.pad(_cols_of(fw["w_branch_mla"]).reshape(MLA_HEADS, MLA_V, D),
                     ((0, 0), (0, HP - MLA_V), (0, 0))).reshape(QW, D)
    w_br, w_o = _cols_of(fw["w_branch_ret"]), fw["w_out"].reshape(D, D)
    tab_mla = _rope_table(MLA_NOPE, MLA_ROPE // 2)
    tab_ret = _rope_table(0, RET_DK // 2)

    proj, a1 = _rms_matmul(h1, mix_pre_w, w_in_p, name="mixer_in_proj")
    q, k, v, qn, kvn = _mla_prep_fwd(proj, pos, mla_q_norm_w, mla_kv_norm_w, w_uq_p, w_kv_p, tab_mla, name="mla_prep_fwd")
    o, lse, w1b, w2b = _flash_fwd(q, k, v, name="mla_attn_fwd_gather_ffn2",
                                  exchange=_Gather([send_w1(ffn2_w1), send_w2(ffn2_w2)]))
    w2b = w2b.reshape(N_DEV // 2, 2 * hp, D)
    ypre, yn, rprev = _ret_fwd(proj, pos, tab_ret, name="retention_fwd")
    omla, oret, m, h2 = _merge_fwd(o, yn, proj, ret_gn_w, w_bm_p, w_br, w_o, h1, mix_post_w, name="merge_fwd")
    u2, f2, _, dy, lossp = _ffn_fwd(h2, ffn2_pre_w, w1b, w2b, ffn2_post_w, tgt, name="ffn2_fwd_loss")
    loss = lax.psum(jnp.sum(lossp[::8, 0]), ("x", "y", "c"))

    def grad(x, dy, tag, exchange=None):
        return _matmul_tn(x if x.ndim == 3 else x[None], dy if dy.ndim == 3 else dy[None], name=tag, exchange=exchange)

    g2, du2, df2, a2, dh2, gpost2, gpre2 = _ffn_bwd(dy, f2, ffn2_post_w, h2, ffn2_pre_w, u2, w2b, w1b, name="ffn2_bwd")
    dw1b, = grad(a2, du2.reshape(N_DEV, T, 2 * hp), "ffn2_dw1")
    dw2b = grad(g2, df2, "ffn2_dw2")[0].reshape(N_DEV, hp, D)
    (dmb, merged, dgm, dgr, domla, do, delta, doret, gated, drg, dyn, gpostm, ggn) = _merge_bwd(
        dh2, m, mix_post_w, omla, oret, proj, yn, ret_gn_w, o, w_o, w_bm_p, w_br, name="merge_bwd")
    dw_out = grad(merged, dmb, "dw_out")[0][0]
    dw_bm_p = grad(o, domla, "dw_branch_mla")[0][0]
    dw_br = grad(gated, doret, "dw_branch_ret")[0][0]
    dq, dk, dv, *recv_ffn2 = _flash_bwd(q, k, v, do, lse, delta, name="mla_attn_bwd_scatter_ffn2",
                                        exchange=_Scatter([dw1b, dw2b]))
    da, dql, dkvl, gqn, gkvn = _mla_prep_bwd(dq, dk, dv, proj, pos, mla_q_norm_w, mla_kv_norm_w, w_uq_p, w_kv_p, tab_mla, name="mla_prep_bwd")
    dw_uq_p = grad(qn, dql, "dw_uq")[0][0]
    dw_kv_p = grad(kvn, dkvl, "dw_ukv")[0][0]
    drq, drk, drv = _ret_bwd(dyn, ypre, proj, pos, tab_ret, rprev, name="retention_bwd")
    dproj = jnp.concatenate([drq, drk, drv, drg, da, dgm, dgr], axis=1)
    dw_in_p = grad(a1, dproj, "dw_in")[0][0]
    dh1, gmixpre = _proj_bwd(dproj, w_in_p, h1, mix_pre_w, dh2, name="mixer_in_bwd")

    unhead = lambda a, h, wd: a.reshape(a.shape[0], h, HP)[:, :, :wd].reshape(a.shape[0], h * wd)
    c0 = 4 * RW
    dw_in = jnp.concatenate([
        dw_in_p[:, c0:c0 + 384], dw_in_p[:, c0 + 384:c0 + 640], dw_in_p[:, c0 + 640 + MLA_NOPE:c0 + 640 + MLA_NOPE + MLA_ROPE],
        unhead(dw_in_p[:, 0:RW], RET_HEADS, RET_DK), unhead(dw_in_p[:, RW:2 * RW], RET_HEADS, RET_DK),
        dw_in_p[:, 2 * RW:3 * RW], dw_in_p[:, 3 * RW:4 * RW],
        dw_in_p[:, PROJ_FIXED:PROJ_FIXED + D], dw_in_p[:, PROJ_FIXED + D:PROJ_FIXED + 2 * D]], axis=1)
    dw_uq = dw_uq_p.reshape(MLA_Q_RANK, MLA_HEADS, HP)[:, :, :MLA_NOPE + MLA_ROPE].transpose(1, 0, 2)
    dkp = dw_kv_p[:, :QW].reshape(MLA_KV_RANK, MLA_HEADS, HP)[:, :, :MLA_NOPE]
    dvp = dw_kv_p[:, QW:].reshape(MLA_KV_RANK, MLA_HEADS, HP)[:, :, :MLA_V]
    dw_ukv = jnp.concatenate([dkp, dvp], axis=2).transpose(1, 0, 2)
    dw_bm = dw_bm_p.reshape(MLA_HEADS, HP, D)[:, :MLA_V].reshape(MLA_HEADS * MLA_V, D)

    mixer_grads = [_col_shards(dw_in), dw_uq, dw_ukv, _col_shards(dw_bm), _col_shards(dw_br),
                   dw_out.reshape(N_DEV, D // N_DEV, D)]
    g1, du1, df1, a0, dx, gpost1, gpre1, *recv_mixer = _ffn_bwd(
        dh1, f1, ffn1_post_w, h0, ffn1_pre_w, u1, w2a, w1a, name="ffn1_bwd_scatter_mixer", exchange=_Scatter(mixer_grads))
    dw2a = grad(g1, df1, "ffn1_dw2")[0].reshape(N_DEV, hp, D)
    dw1a, recv_w2a = grad(a0, du1.reshape(N_DEV, T, 2 * hp), "ffn1_dw1_scatter_dw2", exchange=_Scatter([dw2a]))

    small_g = {"ffn1_pre_w": gpre1, "ffn1_post_w": gpost1, "mix_pre_w": gmixpre, "mla_q_norm_w": gqn,
               "mla_kv_norm_w": gkvn, "ret_gn_w": ggn, "mix_post_w": gpostm, "ffn2_pre_w": gpre2, "ffn2_post_w": gpost2}

    def small_flat(arrs):
        a = jnp.concatenate([z.reshape(-1) for z in arrs])
        a = jnp.pad(a, (0, (-a.size) % (8 * LANES)))
        return a.reshape(-1, LANES)

    sv = small_flat([small_g[nm] for nm, *_ in small])
    recv_w1a, sall = _exchange_alone(_Scatter([dw1a], whole=[sv]), name="scatter_ffn1_dw1")
    parts = dict(zip(mixer, recv_mixer))
    parts.update(ffn1_w1=recv_w1a, ffn1_w2=recv_w2a, ffn2_w1=recv_ffn2[0], ffn2_w2=recv_ffn2[1])
    for nm in ("ffn1_w1", "ffn2_w1"):
        parts[nm] = parts[nm].reshape(N_DEV, D, 2, hp)[:, :, :, :half].reshape(N_DEV, D, 2 * half)

    big_out = {nm: [a[None] for a in _adamw(w[0], parts[nm], m_[0], v_[0], name="adamw_" + nm)]
               for nm, w, m_, v_ in big}
    gs, ds, nms, nvs = _adamw(small_flat([w for _, w, _, _ in small]), sall,
                              small_flat([a for _, _, a, _ in small]), small_flat([a for _, _, _, a in small]),
                              name="adamw_replicated")

    def split_small(flat):
        out, o_ = {}, 0
        fl = flat.reshape(-1)
        for nm, w, _, _ in small:
            out[nm] = fl[o_:o_ + w.size].reshape(w.shape)
            o_ += w.size
        return out

    order = ["ffn1_pre_w", "ffn1_w1", "ffn1_w2", "ffn1_post_w", "mix_pre_w", "w_in", "mla_q_norm_w", "mla_w_uq",
             "mla_kv_norm_w", "mla_w_ukv", "ret_gn_w", "w_branch_mla", "w_branch_ret", "w_out", "mix_post_w",
             "ffn2_pre_w", "ffn2_w1", "ffn2_w2", "ffn2_post_w"]
    outs = [loss, dx[None]]
    for i, fs in enumerate((gs, ds, nms, nvs)):
        both = {**{nm: big_out[nm][i] for nm in big_out}, **split_small(fs)}
        outs += [both[nm] for nm in order]
    return tuple(outs)
```

```python
import math

import numpy as np
import jax
import jax.numpy as jnp
from jax import lax
from jax.experimental import pallas as pl
from jax.experimental.pallas import tpu as pltpu

F32, BF16 = jnp.float32, jnp.bfloat16

MLA_HEADS, MLA_NOPE, MLA_ROPE, MLA_V = 8, 64, 32, 64
MLA_Q_RANK, MLA_KV_RANK = 384, 256
RET_HEADS, RET_DK, RET_DV = 4, 64, 128
ROPE_BASE, NORM_EPS, GN_EPS = 10000.0, 1e-6, 1e-6
ADAM_LR, ADAM_B1, ADAM_B2, ADAM_EPS, ADAM_WD, ADAM_STEP = 0.001, 0.9, 0.999, 1e-08, 0.01, 10
ATTN_SCALE = 1.0 / math.sqrt(MLA_NOPE + MLA_ROPE)

N_DEV = 8
LANES = 128
HP = LANES
QW = MLA_HEADS * HP
RW = RET_HEADS * HP
AW = 1024
PROJ_FIXED = 4 * RW + AW
NEG = -1e30

TOKEN_TILE = 512
ATTN_TILE = 512
ATTN_CHAINS = 2
RET_TILE = 256
GRAD_TILE_CAP = 1408
GRAD_TOKEN_TILE = 2048
MERGE_TILE = 256
VMEM_LIMIT = 56 * 1024 * 1024


def _tile(n, cap, mult=LANES):
    if n <= cap:
        return n
    best = None
    for t in range(mult, cap + 1, mult):
        if n % t == 0:
            best = t
    assert best is not None, (n, cap, mult)
    return best


def _params(sem):
    return pltpu.CompilerParams(dimension_semantics=sem, vmem_limit_bytes=VMEM_LIMIT)


def _dot(a, b):
    return lax.dot_general(a, b, (((1,), (0,)), ((), ())), preferred_element_type=F32)


def _dot_nt(a, b):
    return lax.dot_general(a, b, (((1,), (1,)), ((), ())), preferred_element_type=F32)


def _dot_tn(a, b):
    return lax.dot_general(a, b, (((0,), (0,)), ((), ())), preferred_element_type=F32)


def _sigmoid(x):
    return pl.reciprocal(1.0 + jnp.exp(-x), approx=True)


def _rms_fwd(x, w):
    r = lax.rsqrt(jnp.mean(x * x, axis=-1, keepdims=True) + NORM_EPS)
    return x * r * w


def _rms_bwd(x, w, dy):
    r = lax.rsqrt(jnp.mean(x * x, axis=-1, keepdims=True) + NORM_EPS)
    xh = x * r
    g = dy * w
    dx = r * (g - xh * jnp.mean(g * xh, axis=-1, keepdims=True))
    return dx, jnp.sum(dy * xh, axis=0, keepdims=True)


def _rope_table(first, half):
    inv = (np.float32(ROPE_BASE) ** (-(np.arange(half, dtype=np.float32) / np.float32(half)))).astype(np.float32)
    tab = np.zeros((8, LANES), np.float32)
    tab[0, first:first + half] = inv
    tab[0, first + half:first + 2 * half] = inv
    tab[1, first:first + half] = -1.0
    tab[2, first + half:first + 2 * half] = 1.0
    return jnp.asarray(tab)


def _rope_cs(pos, tab_ref):
    ang = pos * tab_ref[0:1, :]
    s = jnp.sin(ang)
    return jnp.cos(ang), s * tab_ref[1:2, :], s * tab_ref[2:3, :]


def _rope(x, cs, half, inverse=False):
    c, s1, s2 = cs
    a = pltpu.roll(x, LANES - half, 1) * s1 + pltpu.roll(x, half, 1) * s2
    return x * c - a if inverse else x * c + a


def _call(body, *, name, grid, in_specs, out_specs, out_shape, scratch_shapes, args, exchange=None):
    sem = ("arbitrary",) * len(grid)
    if exchange is None:
        return pl.pallas_call(body, name=name, grid=grid, in_specs=in_specs, out_specs=out_specs,
                              out_shape=out_shape, scratch_shapes=scratch_shapes, compiler_params=_params(sem))(*args)
    n_in, n_out, e = len(in_specs), len(out_specs), exchange.n
    total = math.prod(grid)

    def carried(*refs):
        own = refs[:n_in] + refs[n_in + e:n_in + e + n_out] + refs[n_in + 2 * e + n_out:len(refs) - 3]
        ex_refs = (refs[n_in:n_in + e], refs[n_in + e + n_out:n_in + 2 * e + n_out], refs[len(refs) - 3:])
        step = pl.program_id(0)
        for d in range(1, len(grid)):
            step = step * grid[d] + pl.program_id(d)

        @pl.when(step == 0)
        def _():
            exchange.phase(0, *ex_refs)

        @pl.when(step == total // 2)
        def _():
            exchange.phase(1, *ex_refs)

        body(*own)

        @pl.when(step == total - 1)
        def _():
            exchange.phase(2, *ex_refs)

    anyspec = pl.BlockSpec(memory_space=pl.ANY)
    return pl.pallas_call(
        carried, name=name, grid=grid, in_specs=list(in_specs) + [anyspec] * e,
        out_specs=list(out_specs) + [anyspec] * e, out_shape=list(out_shape) + exchange.out_shape,
        scratch_shapes=list(scratch_shapes) + exchange.scratch, compiler_params=_params(sem),
    )(*args, *exchange.operands)


def _ffn_fwd(h, pre_w, w1, w2, post_w, target, *, name, exchange=None):
    T, D = h.shape
    nk, ck = w2.shape[0], w2.shape[1]
    tT = min(TOKEN_TILE, T)
    nT = T // tT
    with_loss = target is not None

    def body(*refs):
        if with_loss:
            (h_ref, pre_ref, w1g_ref, w1u_ref, w2_ref, post_ref, tgt_ref,
             u_ref, f_ref, ho_ref, dy_ref, loss_ref, a_s, acc) = refs
        else:
            (h_ref, pre_ref, w1g_ref, w1u_ref, w2_ref, post_ref,
             u_ref, f_ref, ho_ref, a_s, acc) = refs
        k = pl.program_id(1)

        @pl.when(k == 0)
        def _():
            a_s[...] = _rms_fwd(h_ref[...], pre_ref[...]).astype(BF16)
            acc[...] = jnp.zeros_like(acc)

        a = a_s[...]
        ug = _dot(a, w1g_ref[...])
        uu = _dot(a, w1u_ref[...])
        u_ref[0] = ug.astype(BF16)
        u_ref[1] = uu.astype(BF16)
        acc[...] += _dot((ug * _sigmoid(ug) * uu).astype(BF16), w2_ref[...])

        @pl.when(k == nk - 1)
        def _():
            f = acc[...]
            f_ref[...] = f
            ho = h_ref[...] + 0.5 * _rms_fwd(f, post_ref[...])
            ho_ref[...] = ho
            if with_loss:
                e = ho - tgt_ref[...]
                dy_ref[...] = e * (1.0 / D)
                loss_ref[...] = jnp.full(loss_ref.shape, (0.5 / D) * jnp.sum(e * e), F32)

    row = pl.BlockSpec((tT, D), lambda i, k: (i, 0))
    vec = pl.BlockSpec((1, D), lambda i, k: (0, 0))
    in_specs = [row, vec,
                pl.BlockSpec((None, D, ck), lambda i, k: (k, 0, 0)),
                pl.BlockSpec((None, D, ck), lambda i, k: (nk + k, 0, 0)),
                pl.BlockSpec((None, ck, D), lambda i, k: (k, 0, 0)),
                vec]
    out_shape = [jax.ShapeDtypeStruct((2, nk, T, ck), BF16),
                 jax.ShapeDtypeStruct((T, D), F32),
                 jax.ShapeDtypeStruct((T, D), F32)]
    out_specs = [pl.BlockSpec((2, None, tT, ck), lambda i, k: (0, k, i, 0)), row, row]
    args = [h, pre_w, w1, w1, w2, post_w]
    if with_loss:
        in_specs.append(row)
        args.append(target)
        out_shape += [jax.ShapeDtypeStruct((T, D), F32), jax.ShapeDtypeStruct((nT * 8, LANES), F32)]
        out_specs += [row, pl.BlockSpec((8, LANES), lambda i, k: (i, 0))]
    return _call(body, name=name, grid=(nT, nk), in_specs=in_specs, out_specs=out_specs, out_shape=out_shape,
                 scratch_shapes=[pltpu.VMEM((tT, D), BF16), pltpu.VMEM((tT, D), F32)], args=args, exchange=exchange)


def _ffn_bwd(dho, f, post_w, h, pre_w, u, w2, w1, *, name, exchange=None):
    T, D = h.shape
    nk, ck = w2.shape[0], w2.shape[1]
    tT = min(TOKEN_TILE, T)
    nT = T // tT

    def body(dho_ref, f_ref, post_ref, h_ref, pre_ref, u_ref, w2_ref, w1g_ref, w1u_ref,
             g_ref, du_ref, df_ref, a_ref, dh_ref, gpost_ref, gpre_ref, df_s, da_acc):
        i, k = pl.program_id(0), pl.program_id(1)

        @pl.when(jnp.logical_and(i == 0, k == 0))
        def _():
            gpost_ref[...] = jnp.zeros_like(gpost_ref)
            gpre_ref[...] = jnp.zeros_like(gpre_ref)

        @pl.when(k == 0)
        def _():
            dx, dw = _rms_bwd(f_ref[...], post_ref[...], 0.5 * dho_ref[...])
            dfb = dx.astype(BF16)
            df_s[...] = dfb
            df_ref[...] = dfb
            gpost_ref[...] += dw
            a_ref[...] = _rms_fwd(h_ref[...], pre_ref[...]).astype(BF16)
            da_acc[...] = jnp.zeros_like(da_acc)

        dg = _dot_nt(df_s[...], w2_ref[...])
        ug = u_ref[0].astype(F32)
        uu = u_ref[1].astype(F32)
        sg = _sigmoid(ug)
        sl = ug * sg
        g_ref[...] = (sl * uu).astype(BF16)
        dug = (dg * uu * (sg + sl * (1.0 - sg))).astype(BF16)
        duu = (dg * sl).astype(BF16)
        du_ref[0] = dug
        du_ref[1] = duu
        da_acc[...] += _dot_nt(dug, w1g_ref[...]) + _dot_nt(duu, w1u_ref[...])

        @pl.when(k == nk - 1)
        def _():
            dx, dw = _rms_bwd(h_ref[...], pre_ref[...], da_acc[...])
            dh_ref[...] = dho_ref[...] + dx
            gpre_ref[...] += dw

    row = pl.BlockSpec((tT, D), lambda i, k: (i, 0))
    vec = pl.BlockSpec((1, D), lambda i, k: (0, 0))
    return _call(
        body, name=name, grid=(nT, nk),
        in_specs=[row, row, vec, row, vec,
                  pl.BlockSpec((2, None, tT, ck), lambda i, k: (0, k, i, 0)),
                  pl.BlockSpec((None, ck, D), lambda i, k: (k, 0, 0)),
                  pl.BlockSpec((None, D, ck), lambda i, k: (k, 0, 0)),
                  pl.BlockSpec((None, D, ck), lambda i, k: (nk + k, 0, 0))],
        out_specs=[pl.BlockSpec((None, tT, ck), lambda i, k: (k, i, 0)),
                   pl.BlockSpec((2, None, tT, ck), lambda i, k: (0, k, i, 0)),
                   row, row, row, vec, vec],
        out_shape=[jax.ShapeDtypeStruct((nk, T, ck), BF16),
                   jax.ShapeDtypeStruct((2, nk, T, ck), BF16),
                   jax.ShapeDtypeStruct((T, D), BF16),
                   jax.ShapeDtypeStruct((T, D), BF16),
                   jax.ShapeDtypeStruct((T, D), F32),
                   jax.ShapeDtypeStruct((1, D), F32),
                   jax.ShapeDtypeStruct((1, D), F32)],
        scratch_shapes=[pltpu.VMEM((tT, D), BF16), pltpu.VMEM((tT, D), F32)],
        args=(dho, f, post_w, h, pre_w, u, w2, w1, w1), exchange=exchange)


def _matmul_tn(x, dy, *, name, exchange=None):
    Px, T, K = x.shape
    Py, _, N = dy.shape
    P = max(Px, Py)
    tT, tK, tN = min(GRAD_TOKEN_TILE, T), _tile(K, GRAD_TILE_CAP), _tile(N, GRAD_TILE_CAP)
    nt = T // tT

    def body(x_ref, dy_ref, o_ref, acc):
        t = pl.program_id(3)

        @pl.when(t == 0)
        def _():
            acc[...] = jnp.zeros_like(acc)

        acc[...] += _dot_tn(x_ref[...], dy_ref[...])

        @pl.when(t == nt - 1)
        def _():
            o_ref[...] = acc[...].astype(BF16)

    return _call(
        body, name=name, grid=(P, K // tK, N // tN, nt),
        in_specs=[pl.BlockSpec((None, tT, tK), lambda p, a, b, t: (p if Px > 1 else 0, t, a)),
                  pl.BlockSpec((None, tT, tN), lambda p, a, b, t: (p if Py > 1 else 0, t, b))],
        out_specs=[pl.BlockSpec((None, tK, tN), lambda p, a, b, t: (p, a, b))],
        out_shape=[jax.ShapeDtypeStruct((P, K, N), BF16)],
        scratch_shapes=[pltpu.VMEM((tK, tN), F32)], args=(x, dy), exchange=exchange)


def _rms_matmul(h, wn, w, *, name):
    T, D = h.shape
    N = w.shape[1]
    tT, tN = min(TOKEN_TILE, T), _tile(N, 1024)

    def body(h_ref, wn_ref, w_ref, y_ref, a_ref):
        @pl.when(pl.program_id(1) == 0)
        def _():
            a_ref[...] = _rms_fwd(h_ref[...], wn_ref[...]).astype(BF16)

        y_ref[...] = _dot(a_ref[...], w_ref[...]).astype(BF16)

    return pl.pallas_call(
        body, name=name, grid=(T // tT, N // tN),
        in_specs=[pl.BlockSpec((tT, D), lambda i, j: (i, 0)),
                  pl.BlockSpec((1, D), lambda i, j: (0, 0)),
                  pl.BlockSpec((D, tN), lambda i, j: (0, j))],
        out_specs=[pl.BlockSpec((tT, tN), lambda i, j: (i, j)),
                   pl.BlockSpec((tT, D), lambda i, j: (i, 0))],
        out_shape=[jax.ShapeDtypeStruct((T, N), BF16), jax.ShapeDtypeStruct((T, D), BF16)],
        compiler_params=_params(("parallel", "arbitrary")),
    )(h, wn, w)


def _proj_bwd(dproj, w, h, wn, dres, *, name):
    T, D = h.shape
    N = w.shape[1]
    tT, tN = min(TOKEN_TILE, T), _tile(N, 1024)
    nn = N // tN

    def body(dp_ref, w_ref, h_ref, wn_ref, dres_ref, dh_ref, gw_ref, acc):
        i, j = pl.program_id(0), pl.program_id(1)

        @pl.when(jnp.logical_and(i == 0, j == 0))
        def _():
            gw_ref[...] = jnp.zeros_like(gw_ref)

        @pl.when(j == 0)
        def _():
            acc[...] = jnp.zeros_like(acc)

        acc[...] += _dot_nt(dp_ref[...], w_ref[...])

        @pl.when(j == nn - 1)
        def _():
            dx, dw = _rms_bwd(h_ref[...], wn_ref[...], acc[...])
            dh_ref[...] = dres_ref[...] + dx
            gw_ref[...] += dw

    row = pl.BlockSpec((tT, D), lambda i, j: (i, 0))
    vec = pl.BlockSpec((1, D), lambda i, j: (0, 0))
    return pl.pallas_call(
        body, name=name, grid=(T // tT, nn),
        in_specs=[pl.BlockSpec((tT, tN), lambda i, j: (i, j)),
                  pl.BlockSpec((D, tN), lambda i, j: (0, j)), row, vec, row],
        out_specs=[row, vec],
        out_shape=[jax.ShapeDtypeStruct((T, D), F32), jax.ShapeDtypeStruct((1, D), F32)],
        scratch_shapes=[pltpu.VMEM((tT, D), F32)],
        compiler_params=_params(("arbitrary", "arbitrary")),
    )(dproj, w, h, wn, dres)


def _mla_prep_fwd(proj, pos, qn_w, kvn_w, w_uq, w_kv, tab, *, name):
    T = proj.shape[0]
    tT = min(TOKEN_TILE, T)
    a_blk = PROJ_FIXED // AW - 1

    def body(a_ref, pos_ref, qnw_ref, kvnw_ref, wuq_ref, wkv_ref, tab_ref,
             q_ref, k_ref, v_ref, qn_ref, kvn_ref):
        cq = a_ref[:, 0:MLA_Q_RANK].astype(F32)
        ckv = a_ref[:, MLA_Q_RANK:MLA_Q_RANK + MLA_KV_RANK].astype(F32)
        kr = a_ref[:, 640:768].astype(F32)
        qn = _rms_fwd(cq, qnw_ref[...]).astype(BF16)
        kvn = _rms_fwd(ckv, kvnw_ref[...]).astype(BF16)
        qn_ref[...] = qn
        kvn_ref[...] = kvn
        cs = _rope_cs(pos_ref[...], tab_ref)
        q = _dot(qn, wuq_ref[...])
        kv = _dot(kvn, wkv_ref[...])
        krr = _rope(kr, cs, MLA_ROPE // 2)
        for hd in range(MLA_HEADS):
            sl = slice(hd * HP, (hd + 1) * HP)
            q_ref[:, sl] = (_rope(q[:, sl], cs, MLA_ROPE // 2) * ATTN_SCALE).astype(BF16)
            k_ref[:, sl] = (kv[:, sl] + krr).astype(BF16)
        v_ref[...] = kv[:, QW:].astype(BF16)

    def full(r, c):
        return pl.BlockSpec((r, c), lambda i: (0, 0))

    def rows(c):
        return pl.BlockSpec((tT, c), lambda i: (i, 0))

    return pl.pallas_call(
        body, name=name, grid=(T // tT,),
        in_specs=[pl.BlockSpec((tT, AW), lambda i: (i, a_blk)), rows(1),
                  full(1, MLA_Q_RANK), full(1, MLA_KV_RANK),
                  full(MLA_Q_RANK, QW), full(MLA_KV_RANK, 2 * QW), full(8, LANES)],
        out_specs=[rows(QW), rows(QW), rows(QW), rows(MLA_Q_RANK), rows(MLA_KV_RANK)],
        out_shape=[jax.ShapeDtypeStruct((T, QW), BF16)] * 3
        + [jax.ShapeDtypeStruct((T, MLA_Q_RANK), BF16), jax.ShapeDtypeStruct((T, MLA_KV_RANK), BF16)],
        compiler_params=_params(("parallel",)),
    )(proj, pos, qn_w, kvn_w, w_uq, w_kv, tab)


def _mla_prep_bwd(dq, dk, dv, proj, pos, qn_w, kvn_w, w_uq, w_kv, tab, *, name):
    T = proj.shape[0]
    tT = min(TOKEN_TILE, T)
    a_blk = PROJ_FIXED // AW - 1

    def body(dq_ref, dk_ref, dv_ref, a_ref, pos_ref, qnw_ref, kvnw_ref, wuq_ref, wkv_ref, tab_ref,
             da_ref, dql_ref, dkvl_ref, gqn_ref, gkvn_ref):
        @pl.when(pl.program_id(0) == 0)
        def _():
            gqn_ref[...] = jnp.zeros_like(gqn_ref)
            gkvn_ref[...] = jnp.zeros_like(gkvn_ref)

        cs = _rope_cs(pos_ref[...], tab_ref)
        dkr = jnp.zeros((tT, HP), F32)
        for hd in range(MLA_HEADS):
            sl = slice(hd * HP, (hd + 1) * HP)
            dql_ref[:, sl] = (_rope(dq_ref[:, sl], cs, MLA_ROPE // 2, inverse=True) * ATTN_SCALE).astype(BF16)
            dkh = dk_ref[:, sl]
            dkr = dkr + dkh
            dkvl_ref[:, sl] = dkh.astype(BF16)
        dkvl_ref[:, QW:] = dv_ref[...]
        dqn = _dot_nt(dql_ref[...], wuq_ref[...])
        dkvn = _dot_nt(dkvl_ref[...], wkv_ref[...])
        cq = a_ref[:, 0:MLA_Q_RANK].astype(F32)
        ckv = a_ref[:, MLA_Q_RANK:MLA_Q_RANK + MLA_KV_RANK].astype(F32)
        dcq, gq = _rms_bwd(cq, qnw_ref[...], dqn)
        dckv, gkv = _rms_bwd(ckv, kvnw_ref[...], dkvn)
        gqn_ref[...] += gq
        gkvn_ref[...] += gkv
        da_ref[:, 0:MLA_Q_RANK] = dcq.astype(BF16)
        da_ref[:, MLA_Q_RANK:MLA_Q_RANK + MLA_KV_RANK] = dckv.astype(BF16)
        da_ref[:, 640:768] = _rope(dkr, cs, MLA_ROPE // 2, inverse=True).astype(BF16)
        da_ref[:, 768:AW] = jnp.zeros((tT, AW - 768), BF16)

    def full(r, c):
        return pl.BlockSpec((r, c), lambda i: (0, 0))

    def rows(c):
        return pl.BlockSpec((tT, c), lambda i: (i, 0))

    return pl.pallas_call(
        body, name=name, grid=(T // tT,),
        in_specs=[rows(QW), rows(QW), rows(QW), pl.BlockSpec((tT, AW), lambda i: (i, a_blk)), rows(1),
                  full(1, MLA_Q_RANK), full(1, MLA_KV_RANK),
                  full(MLA_Q_RANK, QW), full(MLA_KV_RANK, 2 * QW), full(8, LANES)],
        out_specs=[rows(AW), rows(QW), rows(2 * QW), full(1, MLA_Q_RANK), full(1, MLA_KV_RANK)],
        out_shape=[jax.ShapeDtypeStruct((T, AW), BF16), jax.ShapeDtypeStruct((T, QW), BF16),
                   jax.ShapeDtypeStruct((T, 2 * QW), BF16),
                   jax.ShapeDtypeStruct((1, MLA_Q_RANK), F32), jax.ShapeDtypeStruct((1, MLA_KV_RANK), F32)],
        compiler_params=_params(("arbitrary",)),
    )(dq, dk, dv, proj, pos, qn_w, kvn_w, w_uq, w_kv, tab)


def _flash_fwd(q, k, v, *, name, exchange=None):
    T = q.shape[0]
    H = q.shape[1] // HP
    tq = min(ATTN_TILE, T)
    nq = T // tq

    sub = tq // ATTN_CHAINS

    def body(q_ref, k_ref, v_ref, o_ref, lse_ref):
        qi = pl.program_id(1)
        qs = [q_ref[c * sub:(c + 1) * sub, :] for c in range(ATTN_CHAINS)]

        def update(carry, off, masked):
            kb = k_ref[pl.ds(off, tq), :]
            vb = v_ref[pl.ds(off, tq), :]
            out = []
            for c in range(ATTN_CHAINS):
                m_prev, l_prev, acc = carry[c]
                s = _dot_nt(qs[c], kb)
                if masked:
                    rows = lax.broadcasted_iota(jnp.int32, (sub, tq), 0) + c * sub
                    s = jnp.where(rows >= lax.broadcasted_iota(jnp.int32, (sub, tq), 1), s, NEG)
                m_new = jnp.maximum(m_prev, jnp.max(s, axis=1, keepdims=True))
                alpha = jnp.exp(m_prev - m_new)
                p = jnp.exp(s - m_new)
                out.append((m_new, alpha * l_prev + jnp.sum(p, axis=1, keepdims=True),
                            alpha * acc + _dot(p.astype(BF16), vb)))
            return tuple(out)

        init = tuple((jnp.full((sub, 1), NEG, F32), jnp.zeros((sub, 1), F32), jnp.zeros((sub, HP), F32))
                     for _ in range(ATTN_CHAINS))
        carry = lax.fori_loop(0, qi, lambda j, cr: update(cr, pl.multiple_of(j * tq, tq), False), init)
        carry = update(carry, pl.multiple_of(qi * tq, tq), True)
        for c in range(ATTN_CHAINS):
            m_fin, l_fin, acc = carry[c]
            o_ref[c * sub:(c + 1) * sub, :] = (acc / l_fin).astype(BF16)
            lse_ref[c * sub:(c + 1) * sub, :] = jnp.broadcast_to(m_fin + jnp.log(l_fin), (sub, HP))

    qspec = pl.BlockSpec((tq, HP), lambda h, i: (i, h))
    kspec = pl.BlockSpec((T, HP), lambda h, i: (0, h))
    return _call(
        body, name=name, grid=(H, nq),
        in_specs=[qspec, kspec, kspec], out_specs=[qspec, qspec],
        out_shape=[jax.ShapeDtypeStruct((T, H * HP), BF16), jax.ShapeDtypeStruct((T, H * HP), F32)],
        scratch_shapes=[], args=(q, k, v), exchange=exchange)


def _flash_bwd(q, k, v, do, lse, delta, *, name, exchange=None):
    T = q.shape[0]
    H = q.shape[1] // HP
    tq = min(ATTN_TILE, T)
    nq = T // tq
    sub = tq // ATTN_CHAINS

    def body(k_ref, v_ref, q_ref, do_ref, lse_ref, dl_ref, dq_ref, dk_ref, dv_ref):
        ki = pl.program_id(1)

        @pl.when(ki == 0)
        def _():
            dq_ref[...] = jnp.zeros_like(dq_ref)

        kb = k_ref[...]
        vb = v_ref[...]

        def step(carry, j, masked):
            dk_acc, dv_acc = carry
            for c in range(ATTN_CHAINS):
                rows = pl.ds(pl.multiple_of(j * tq + c * sub, sub), sub)
                qb = q_ref[rows, :]
                dob = do_ref[rows, :]
                s = _dot_nt(qb, kb)
                if masked:
                    ri = lax.broadcasted_iota(jnp.int32, (sub, tq), 0) + c * sub
                    s = jnp.where(ri >= lax.broadcasted_iota(jnp.int32, (sub, tq), 1), s, NEG)
                p = jnp.exp(s - lse_ref[rows, 0:1])
                dv_acc = dv_acc + _dot_tn(p.astype(BF16), dob)
                dp = _dot_nt(dob, vb)
                ds = (p * (dp - dl_ref[rows, 0:1])).astype(BF16)
                dk_acc = dk_acc + _dot_tn(ds, qb)
                dq_ref[rows, :] += _dot(ds, kb)
            return dk_acc, dv_acc

        carry = step((jnp.zeros((tq, HP), F32), jnp.zeros((tq, HP), F32)), ki, True)
        dk_acc, dv_acc = lax.fori_loop(ki + 1, nq, lambda j, cr: step(cr, j, False), carry)
        dk_ref[...] = dk_acc
        dv_ref[...] = dv_acc.astype(BF16)

    kspec = pl.BlockSpec((tq, HP), lambda h, j: (j, h))
    full = pl.BlockSpec((T, HP), lambda h, j: (0, h))
    return _call(
        body, name=name, grid=(H, nq),
        in_specs=[kspec, kspec, full, full, full, full], out_specs=[full, kspec, kspec],
        out_shape=[jax.ShapeDtypeStruct((T, H * HP), F32), jax.ShapeDtypeStruct((T, H * HP), F32),
                   jax.ShapeDtypeStruct((T, H * HP), BF16)],
        scratch_shapes=[], args=(k, v, q, do, lse, delta), exchange=exchange)


def _ret_consts(cc, hd):
    lg = math.log(1.0 - 2.0 ** (-5.0 - hd))
    diff = (lax.broadcasted_iota(jnp.int32, (cc, cc), 0) - lax.broadcasted_iota(jnp.int32, (cc, cc), 1)).astype(F32)
    decay = jnp.where(diff >= 0, jnp.exp(jnp.maximum(diff, 0.0) * lg), 0.0)
    idx = lax.broadcasted_iota(jnp.int32, (cc, 1), 0).astype(F32)
    zeta = jnp.exp((cc - 1.0 - idx) * lg)
    xi = jnp.exp((idx + 1.0) * lg)
    return decay, zeta, xi, math.exp(cc * lg)


def _ret_fwd(proj, pos, tab, *, name):
    T = proj.shape[0]
    cc = min(RET_TILE, T)
    n = T // cc

    def body(rq_ref, rk_ref, rv_ref, pos_ref, tab_ref, y_ref, yn_ref, rprev_ref, r_s):
        @pl.when(pl.program_id(0) == 0)
        def _():
            r_s[...] = jnp.zeros_like(r_s)

        cs = _rope_cs(pos_ref[...], tab_ref)
        for hd in range(RET_HEADS):
            sl = slice(hd * HP, (hd + 1) * HP)
            decay, zeta, xi, gc = _ret_consts(cc, hd)
            q = _rope(rq_ref[:, sl].astype(F32), cs, RET_DK // 2).astype(BF16)
            kf = _rope(rk_ref[:, sl].astype(F32), cs, RET_DK // 2) * (RET_DK ** -0.5)
            k = kf.astype(BF16)
            v = rv_ref[:, sl]
            r = r_s[hd]
            rprev_ref[0, hd] = r
            inner = (_dot_nt(q, k) * decay).astype(BF16)
            y = _dot(inner, v) + _dot(q, r.astype(BF16)) * xi
            r_s[hd] = r * gc + _dot_tn((kf * zeta).astype(BF16), v)
            y_ref[:, sl] = y
            mu = jnp.mean(y, axis=-1, keepdims=True)
            yc = y - mu
            var = jnp.mean(yc * yc, axis=-1, keepdims=True)
            yn_ref[:, sl] = (yc * lax.rsqrt(var + GN_EPS)).astype(BF16)

    def blk(j):
        return pl.BlockSpec((cc, RW), lambda i: (i, j))

    return pl.pallas_call(
        body, name=name, grid=(n,),
        in_specs=[blk(0), blk(1), blk(2), pl.BlockSpec((cc, 1), lambda i: (i, 0)),
                  pl.BlockSpec((8, LANES), lambda i: (0, 0))],
        out_specs=[blk(0), blk(0), pl.BlockSpec((1, RET_HEADS, HP, RET_DV), lambda i: (i, 0, 0, 0))],
        out_shape=[jax.ShapeDtypeStruct((T, RW), F32), jax.ShapeDtypeStruct((T, RW), BF16),
                   jax.ShapeDtypeStruct((n, RET_HEADS, HP, RET_DV), F32)],
        scratch_shapes=[pltpu.VMEM((RET_HEADS, HP, RET_DV), F32)],
        compiler_params=_params(("arbitrary",)),
    )(proj, proj, proj, pos, tab)


def _ret_bwd(dyn, y, proj, pos, tab, rprev, *, name):
    T = proj.shape[0]
    cc = min(RET_TILE, T)
    n = T // cc

    def body(dyn_ref, y_ref, rq_ref, rk_ref, rv_ref, pos_ref, tab_ref, rprev_ref,
             drq_ref, drk_ref, drv_ref, dr_s):
        @pl.when(pl.program_id(0) == 0)
        def _():
            dr_s[...] = jnp.zeros_like(dr_s)

        cs = _rope_cs(pos_ref[...], tab_ref)
        for hd in range(RET_HEADS):
            sl = slice(hd * HP, (hd + 1) * HP)
            decay, zeta, xi, gc = _ret_consts(cc, hd)
            q = _rope(rq_ref[:, sl].astype(F32), cs, RET_DK // 2).astype(BF16)
            kf = _rope(rk_ref[:, sl].astype(F32), cs, RET_DK // 2) * (RET_DK ** -0.5)
            k = kf.astype(BF16)
            v = rv_ref[:, sl]
            yv = y_ref[:, sl]
            mu = jnp.mean(yv, axis=-1, keepdims=True)
            yc = yv - mu
            rs = lax.rsqrt(jnp.mean(yc * yc, axis=-1, keepdims=True) + GN_EPS)
            yn = yc * rs
            dn = dyn_ref[:, sl]
            dy = rs * (dn - jnp.mean(dn, axis=-1, keepdims=True) - yn * jnp.mean(dn * yn, axis=-1, keepdims=True))
            dyb = dy.astype(BF16)
            dyx = (dy * xi).astype(BF16)
            dr = dr_s[hd]
            drb = dr.astype(BF16)
            inner = (_dot_nt(q, k) * decay).astype(BF16)
            da = (_dot_nt(dyb, v) * decay).astype(BF16)
            dv = _dot_tn(inner, dyb) + _dot((kf * zeta).astype(BF16), drb)
            dq = _dot(da, k) + _dot_nt(dyx, rprev_ref[0, hd].astype(BF16))
            dk = _dot_tn(da, q) + _dot_nt(v, drb) * zeta
            dr_s[hd] = dr * gc + _dot_tn(q, dyx)
            drq_ref[:, sl] = _rope(dq, cs, RET_DK // 2, inverse=True).astype(BF16)
            drk_ref[:, sl] = _rope(dk * (RET_DK ** -0.5), cs, RET_DK // 2, inverse=True).astype(BF16)
            drv_ref[:, sl] = dv.astype(BF16)

    def blk(j):
        return pl.BlockSpec((cc, RW), lambda i: (n - 1 - i, j))

    return pl.pallas_call(
        body, name=name, grid=(n,),
        in_specs=[blk(0), blk(0), blk(0), blk(1), blk(2), pl.BlockSpec((cc, 1), lambda i: (n - 1 - i, 0)),
                  pl.BlockSpec((8, LANES), lambda i: (0, 0)),
                  pl.BlockSpec((1, RET_HEADS, HP, RET_DV), lambda i: (n - 1 - i, 0, 0, 0))],
        out_specs=[blk(0), blk(0), blk(0)],
        out_shape=[jax.ShapeDtypeStruct((T, RW), BF16)] * 3,
        scratch_shapes=[pltpu.VMEM((RET_HEADS, HP, RET_DV), F32)],
        compiler_params=_params(("arbitrary",)),
    )(dyn, y, proj, proj, proj, pos, tab, rprev)


def _merge_fwd(o, yn, proj, gn_w, w_bm, w_br, w_out, h, post_w, *, name):
    T, D = h.shape
    tT = min(MERGE_TILE, T)
    g_blk = PROJ_FIXED // D

    def body(o_ref, yn_ref, rg_ref, gm_ref, gr_ref, gnw_ref, wbm_ref, wbr_ref, wout_ref, h_ref, post_ref,
             omla_ref, oret_ref, m_ref, ho_ref):
        o_mla = _dot(o_ref[...], wbm_ref[...])
        rg = rg_ref[...].astype(F32)
        gated = (rg * _sigmoid(rg) * (yn_ref[...].astype(F32) * gnw_ref[...])).astype(BF16)
        o_ret = _dot(gated, wbr_ref[...])
        omla_ref[...] = o_mla.astype(BF16)
        oret_ref[...] = o_ret.astype(BF16)
        merged = _sigmoid(gm_ref[...].astype(F32)) * o_mla + _sigmoid(gr_ref[...].astype(F32)) * o_ret
        m = _dot(merged.astype(BF16), wout_ref[...])
        m_ref[...] = m
        ho_ref[...] = h_ref[...] + _rms_fwd(m, post_ref[...])

    def full(r, c):
        return pl.BlockSpec((r, c), lambda i: (0, 0))

    def rows(c, j=0):
        return pl.BlockSpec((tT, c), lambda i: (i, j))

    return pl.pallas_call(
        body, name=name, grid=(T // tT,),
        in_specs=[rows(QW), rows(RW), rows(RW, 3), rows(D, g_blk), rows(D, g_blk + 1), full(1, RW),
                  full(QW, D), full(RW, D), full(D, D), rows(D), full(1, D)],
        out_specs=[rows(D), rows(D), rows(D), rows(D)],
        out_shape=[jax.ShapeDtypeStruct((T, D), BF16), jax.ShapeDtypeStruct((T, D), BF16),
                   jax.ShapeDtypeStruct((T, D), F32), jax.ShapeDtypeStruct((T, D), F32)],
        compiler_params=_params(("parallel",)),
    )(o, yn, proj, proj, proj, gn_w, w_bm, w_br, w_out, h, post_w)


def _merge_bwd(dho, m, post_w, omla, oret, proj, yn, gn_w, o, w_out, w_bm, w_br, *, name):
    T, D = dho.shape
    tT = min(MERGE_TILE, T)
    g_blk = PROJ_FIXED // D

    def body(dho_ref, m_ref, post_ref, omla_ref, oret_ref, rg_ref, gm_ref, gr_ref, yn_ref, gnw_ref, o_ref,
             wout_ref, wbm_ref, wbr_ref,
             dm_ref, merged_ref, dgm_ref, dgr_ref, domla_ref, do_ref, delta_ref, doret_ref, gated_ref,
             drg_ref, dyn_ref, gpost_ref, ggn_ref):
        @pl.when(pl.program_id(0) == 0)
        def _():
            gpost_ref[...] = jnp.zeros_like(gpost_ref)
            ggn_ref[...] = jnp.zeros_like(ggn_ref)

        dm, gp = _rms_bwd(m_ref[...], post_ref[...], dho_ref[...])
        gpost_ref[...] += gp
        dmb = dm.astype(BF16)
        dm_ref[...] = dmb
        dmerged = _dot_nt(dmb, wout_ref[...])
        o_mla = omla_ref[...].astype(F32)
        o_ret = oret_ref[...].astype(F32)
        sgm = _sigmoid(gm_ref[...].astype(F32))
        sgr = _sigmoid(gr_ref[...].astype(F32))
        merged_ref[...] = (sgm * o_mla + sgr * o_ret).astype(BF16)
        dgm_ref[...] = (dmerged * o_mla * sgm * (1.0 - sgm)).astype(BF16)
        dgr_ref[...] = (dmerged * o_ret * sgr * (1.0 - sgr)).astype(BF16)
        domla = (dmerged * sgm).astype(BF16)
        domla_ref[...] = domla
        do = _dot_nt(domla, wbm_ref[...])
        do_ref[...] = do.astype(BF16)
        for hd in range(MLA_HEADS):
            sl = slice(hd * HP, (hd + 1) * HP)
            d = jnp.sum(do[:, sl] * o_ref[:, sl].astype(F32), axis=-1, keepdims=True)
            delta_ref[:, sl] = jnp.broadcast_to(d, (tT, HP))
        doret = (dmerged * sgr).astype(BF16)
        doret_ref[...] = doret
        dgated = _dot_nt(doret, wbr_ref[...])
        rg = rg_ref[...].astype(F32)
        sg = _sigmoid(rg)
        srg = rg * sg
        ynv = yn_ref[...].astype(F32)
        yw = ynv * gnw_ref[...]
        gated_ref[...] = (srg * yw).astype(BF16)
        drg_ref[...] = (dgated * yw * (sg * (1.0 + rg * (1.0 - sg)))).astype(BF16)
        dgs = dgated * srg
        dyn_ref[...] = dgs * gnw_ref[...]
        ggn_ref[...] += jnp.sum(dgs * ynv, axis=0, keepdims=True)

    def full(r, c):
        return pl.BlockSpec((r, c), lambda i: (0, 0))

    def rows(c, j=0):
        return pl.BlockSpec((tT, c), lambda i: (i, j))

    return pl.pallas_call(
        body, name=name, grid=(T // tT,),
        in_specs=[rows(D), rows(D), full(1, D), rows(D), rows(D), rows(RW, 3), rows(D, g_blk), rows(D, g_blk + 1),
                  rows(RW), full(1, RW), rows(QW), full(D, D), full(QW, D), full(RW, D)],
        out_specs=[rows(D), rows(D), rows(D), rows(D), rows(D), rows(QW), rows(QW), rows(D), rows(RW),
                   rows(RW), rows(RW), full(1, D), full(1, RW)],
        out_shape=[jax.ShapeDtypeStruct((T, D), BF16)] * 5
        + [jax.ShapeDtypeStruct((T, QW), BF16), jax.ShapeDtypeStruct((T, QW), F32),
           jax.ShapeDtypeStruct((T, D), BF16), jax.ShapeDtypeStruct((T, RW), BF16),
           jax.ShapeDtypeStruct((T, RW), BF16), jax.ShapeDtypeStruct((T, RW), F32),
           jax.ShapeDtypeStruct((1, D), F32), jax.ShapeDtypeStruct((1, RW), F32)],
        compiler_params=_params(("arbitrary",)),
    )(dho, m, post_w, omla, oret, proj, proj, proj, yn, gn_w, o, w_out, w_bm, w_br)


def _mesh_pos():
    return lax.axis_index("x"), lax.axis_index("y"), lax.axis_index("c")


class _Gather:
    def __init__(self, shards):
        self.operands = list(shards)
        self.n = len(shards)
        self.out_shape = [jax.ShapeDtypeStruct((N_DEV,) + s.shape, s.dtype) for s in shards]
        self.scratch = [pltpu.SemaphoreType.DMA((7 * self.n,)), pltpu.SemaphoreType.DMA((7 * self.n,)),
                        pltpu.SemaphoreType.DMA((self.n,))]

    def phase(self, p, x_refs, out_refs, sems):
        send_sems, recv_sems, local_sems = sems
        x, y, c = _mesh_pos()
        me, sibling = (x, y, c), (x, y, 1 - c)
        chips = [(1 - x, y), (x, 1 - y), (1 - x, 1 - y)]

        def copy(w, k, block, to, src=None):
            slot = out_refs[w].at[4 * block[0] + 2 * block[1] + block[2]]
            return pltpu.make_async_remote_copy(
                src_ref=slot if src is None else src, dst_ref=slot,
                send_sem=send_sems.at[7 * w + k], recv_sem=recv_sems.at[7 * w + k],
                device_id=to, device_id_type=pl.DeviceIdType.MESH)

        for w in range(self.n):
            mine = pltpu.make_async_copy(x_refs[w], out_refs[w].at[4 * x + 2 * y + c], local_sems.at[w])
            first = [copy(w, 0, me, sibling, src=x_refs[w])]
            first += [copy(w, 1 + j, me, (*chip, c), src=x_refs[w]) for j, chip in enumerate(chips)]
            passed = [copy(w, 4 + j, (*chip, c), sibling) for j, chip in enumerate(chips)]
            if p == 0:
                mine.start()
                for cp in first:
                    cp.start()
            elif p == 1:
                for j, chip in enumerate(chips):
                    copy(w, 1 + j, (*chip, c), me).wait_recv()
                    passed[j].start()
            else:
                copy(w, 0, sibling, me).wait_recv()
                for j, chip in enumerate(chips):
                    copy(w, 4 + j, (*chip, 1 - c), me).wait_recv()
                for cp in first + passed:
                    cp.wait_send()
                mine.wait()


class _Scatter:
    def __init__(self, grads, whole=()):
        self.n_sliced = len(grads)
        self.operands = list(grads) + list(whole)
        self.n = len(self.operands)
        self.out_shape = [jax.ShapeDtypeStruct(g.shape, g.dtype) for g in grads]
        self.out_shape += [jax.ShapeDtypeStruct((N_DEV,) + a.shape, a.dtype) for a in whole]
        n_sem = (N_DEV - 1) * self.n
        self.scratch = [pltpu.SemaphoreType.DMA((n_sem,)), pltpu.SemaphoreType.DMA((n_sem,)),
                        pltpu.SemaphoreType.DMA((self.n,))]

    def phase(self, p, in_refs, out_refs, sems):
        if p == 1:
            return
        send_sems, recv_sems, local_sems = sems
        x, y, c = _mesh_pos()
        me = 4 * x + 2 * y + c

        def src(w, dev):
            return in_refs[w].at[dev] if w < self.n_sliced else in_refs[w]

        for w in range(self.n):
            own = pltpu.make_async_copy(src(w, me), out_refs[w].at[me], local_sems.at[w])
            sends, recvs = [], []
            for r in range(1, N_DEV):
                px = 1 - x if r & 4 else x
                py = 1 - y if r & 2 else y
                pc = 1 - c if r & 1 else c
                peer, pidx = (px, py, pc), 4 * px + 2 * py + pc
                k = (N_DEV - 1) * w + r - 1
                sends.append(pltpu.make_async_remote_copy(
                    src_ref=src(w, pidx), dst_ref=out_refs[w].at[me], send_sem=send_sems.at[k],
                    recv_sem=recv_sems.at[k], device_id=peer, device_id_type=pl.DeviceIdType.MESH))
                recvs.append(pltpu.make_async_remote_copy(
                    src_ref=src(w, me), dst_ref=out_refs[w].at[pidx], send_sem=send_sems.at[k],
                    recv_sem=recv_sems.at[k], device_id=peer, device_id_type=pl.DeviceIdType.MESH))
            if p == 0:
                own.start()
                for cp in sends:
                    cp.start()
            else:
                for cp in recvs:
                    cp.wait_recv()
                for cp in sends:
                    cp.wait_send()
                own.wait()


def _exchange_alone(ex, *, name):
    n = ex.n

    def body(*refs):
        for p in range(3):
            ex.phase(p, refs[:n], refs[n:2 * n], refs[2 * n:])

    anyspec = pl.BlockSpec(memory_space=pl.ANY)
    return pl.pallas_call(body, name=name, out_shape=ex.out_shape, in_specs=[anyspec] * n,
                          out_specs=[anyspec] * n, scratch_shapes=ex.scratch)(*ex.operands)


def _adamw(w, parts, m, v, *, name):
    R, n = w.shape
    tr = R
    for t in range(16, R, 16):
        if R % t == 0 and t * n <= 256 * 1024:
            tr = t
    if R * n <= 256 * 1024:
        tr = R

    def body(w_ref, p_ref, m_ref, v_ref, g_ref, d_ref, nm_ref, nv_ref):
        g = p_ref[0].astype(F32)
        for j in range(1, N_DEV):
            g = g + p_ref[j].astype(F32)
        g_ref[...] = g
        nm = ADAM_B1 * m_ref[...] + (1.0 - ADAM_B1) * g
        nv = ADAM_B2 * v_ref[...] + (1.0 - ADAM_B2) * (g * g)
        nm_ref[...] = nm
        nv_ref[...] = nv
        m_hat = nm / (1.0 - ADAM_B1 ** ADAM_STEP)
        v_hat = nv / (1.0 - ADAM_B2 ** ADAM_STEP)
        d_ref[...] = -ADAM_LR * (m_hat / (jnp.sqrt(v_hat) + ADAM_EPS) + ADAM_WD * w_ref[...])

    row = pl.BlockSpec((tr, n), lambda i: (i, 0))
    return pl.pallas_call(
        body, name=name, grid=(R // tr,),
        in_specs=[row, pl.BlockSpec((N_DEV, tr, n), lambda i: (0, i, 0)), row, row],
        out_specs=[row, row, row, row],
        out_shape=[jax.ShapeDtypeStruct((R, n), F32)] * 4,
        compiler_params=_params(("parallel",)),
    )(w, parts, m, v)


def _pad_last(a, width):
    return jnp.pad(a, [(0, 0)] * (a.ndim - 1) + [(0, width - a.shape[-1])])


def _cols_of(g):
    return g.transpose(1, 0, 2).reshape(g.shape[1], N_DEV * g.shape[2])


def _col_shards(w):
    return w.reshape(w.shape[0], N_DEV, w.shape[1] // N_DEV).transpose(1, 0, 2)


def kernel(x, positions, ffn1_pre_w, ffn1_w1, ffn1_w2, ffn1_post_w, mix_pre_w, w_in, mla_q_norm_w, mla_w_uq, mla_kv_norm_w, mla_w_ukv, ret_gn_w, w_branch_mla, w_branch_ret, w_out, mix_post_w, ffn2_pre_w, ffn2_w1, ffn2_w2, ffn2_post_w, loss_target, m_ffn1_pre_w, m_ffn1_w1, m_ffn1_w2, m_ffn1_post_w, m_mix_pre_w, m_w_in, m_mla_q_norm_w, m_mla_w_uq, m_mla_kv_norm_w, m_mla_w_ukv, m_ret_gn_w, m_w_branch_mla, m_w_branch_ret, m_w_out, m_mix_post_w, m_ffn2_pre_w, m_ffn2_w1, m_ffn2_w2, m_ffn2_post_w, v_ffn1_pre_w, v_ffn1_w1, v_ffn1_w2, v_ffn1_post_w, v_mix_pre_w, v_w_in, v_mla_q_norm_w, v_mla_w_uq, v_mla_kv_norm_w, v_mla_w_ukv, v_ret_gn_w, v_w_branch_mla, v_w_branch_ret, v_w_out, v_mix_post_w, v_ffn2_pre_w, v_ffn2_w1, v_ffn2_w2, v_ffn2_post_w):
    T, D = x.shape[1], x.shape[2]
    h0 = x[0]
    tgt = loss_target[0]
    pos = positions.reshape(T, 1).astype(F32)

    big = [("ffn1_w1", ffn1_w1, m_ffn1_w1, v_ffn1_w1), ("ffn1_w2", ffn1_w2, m_ffn1_w2, v_ffn1_w2),
           ("w_in", w_in, m_w_in, v_w_in), ("mla_w_uq", mla_w_uq, m_mla_w_uq, v_mla_w_uq),
           ("mla_w_ukv", mla_w_ukv, m_mla_w_ukv, v_mla_w_ukv),
           ("w_branch_mla", w_branch_mla, m_w_branch_mla, v_w_branch_mla),
           ("w_branch_ret", w_branch_ret, m_w_branch_ret, v_w_branch_ret),
           ("w_out", w_out, m_w_out, v_w_out),
           ("ffn2_w1", ffn2_w1, m_ffn2_w1, v_ffn2_w1), ("ffn2_w2", ffn2_w2, m_ffn2_w2, v_ffn2_w2)]
    small = [("ffn1_pre_w", ffn1_pre_w, m_ffn1_pre_w, v_ffn1_pre_w), ("ffn1_post_w", ffn1_post_w, m_ffn1_post_w, v_ffn1_post_w),
             ("mix_pre_w", mix_pre_w, m_mix_pre_w, v_mix_pre_w), ("mla_q_norm_w", mla_q_norm_w, m_mla_q_norm_w, v_mla_q_norm_w),
             ("mla_kv_norm_w", mla_kv_norm_w, m_mla_kv_norm_w, v_mla_kv_norm_w), ("ret_gn_w", ret_gn_w, m_ret_gn_w, v_ret_gn_w),
             ("mix_post_w", mix_post_w, m_mix_post_w, v_mix_post_w), ("ffn2_pre_w", ffn2_pre_w, m_ffn2_pre_w, v_ffn2_pre_w),
             ("ffn2_post_w", ffn2_post_w, m_ffn2_post_w, v_ffn2_post_w)]

    half = ffn1_w2.shape[1]
    hp = -(-half // LANES) * LANES

    def send_w1(w):
        return _pad_last(w[0].reshape(D, 2, half), hp).reshape(D, 2 * hp).astype(BF16)

    def send_w2(w):
        return jnp.pad(w[0], ((0, hp - half), (0, 0))).astype(BF16)

    mixer = ["w_in", "mla_w_uq", "mla_w_ukv", "w_branch_mla", "w_branch_ret", "w_out"]
    mixer_send = [w[0].astype(BF16) for nm, w, _, _ in big if nm in mixer]

    w1a, w2a = _exchange_alone(_Gather([send_w1(ffn1_w1), send_w2(ffn1_w2)]), name="gather_ffn1")
    w2a = w2a.reshape(N_DEV // 2, 2 * hp, D)
    u1, f1, h1, *got = _ffn_fwd(h0, ffn1_pre_w, w1a, w2a, ffn1_post_w, None, name="ffn1_fwd_gather_mixer",
                                exchange=_Gather(mixer_send))
    fw = dict(zip(mixer, got))

    wi = _cols_of(fw["w_in"])
    cq_w, ckv_w, kr_w = wi[:, 0:384], wi[:, 384:640], wi[:, 640:672]
    rq_w, rk_w = wi[:, 672:928], wi[:, 928:1184]
    rv_w, rg_w = wi[:, 1184:1696], wi[:, 1696:2208]
    gm_w, gr_w = wi[:, 2208:2208 + D], wi[:, 2208 + D:2208 + 2 * D]
    zer = lambda n: jnp.zeros((D, n), BF16)
    head_pad = lambda a, h: _pad_last(a.reshape(a.shape[0], h, -1), HP).reshape(a.shape[0], h * HP)
    w_in_p = jnp.concatenate([head_pad(rq_w, RET_HEADS), head_pad(rk_w, RET_HEADS), rv_w, rg_w,
                              cq_w, ckv_w, zer(MLA_NOPE), kr_w, zer(HP - MLA_NOPE - MLA_ROPE), zer(AW - 768),
                              gm_w, gr_w], axis=1)
    w_uq_p = _cols_of(_pad_last(fw["mla_w_uq"], HP))
    ukv = fw["mla_w_ukv"].transpose(1, 0, 2)
    w_kv_p = jnp.concatenate([_pad_last(ukv[:, :, :MLA_NOPE], HP).reshape(MLA_KV_RANK, QW),
                              _pad_last(ukv[:, :, MLA_NOPE:], HP).reshape(MLA_KV_RANK, QW)], axis=1)
    w_bm_p = jnp.pad(_cols_of(fw["w_branch_mla"]).reshape(MLA_HEADS, MLA_V, D),
                     ((0, 0), (0, HP - MLA_V), (0, 0))).reshape(QW, D)
    w_br, w_o = _cols_of(fw["w_branch_ret"]), fw["w_out"].reshape(D, D)
    tab_mla = _rope_table(MLA_NOPE, MLA_ROPE // 2)
    tab_ret = _rope_table(0, RET_DK // 2)

    proj, a1 = _rms_matmul(h1, mix_pre_w, w_in_p, name="mixer_in_proj")
    q, k, v, qn, kvn = _mla_prep_fwd(proj, pos, mla_q_norm_w, mla_kv_norm_w, w_uq_p, w_kv_p, tab_mla, name="mla_prep_fwd")
    o, lse, w1b, w2b = _flash_fwd(q, k, v, name="mla_attn_fwd_gather_ffn2",
                                  exchange=_Gather([send_w1(ffn2_w1), send_w2(ffn2_w2)]))
    w2b = w2b.reshape(N_DEV // 2, 2 * hp, D)
    ypre, yn, rprev = _ret_fwd(proj, pos, tab_ret, name="retention_fwd")
    omla, oret, m, h2 = _merge_fwd(o, yn, proj, ret_gn_w, w_bm_p, w_br, w_o, h1, mix_post_w, name="merge_fwd")
    u2, f2, _, dy, lossp = _ffn_fwd(h2, ffn2_pre_w, w1b, w2b, ffn2_post_w, tgt, name="ffn2_fwd_loss")
    loss = lax.psum(jnp.sum(lossp[::8, 0]), ("x", "y", "c"))

    def grad(x, dy, tag, exchange=None):
        return _matmul_tn(x if x.ndim == 3 else x[None], dy if dy.ndim == 3 else dy[None], name=tag, exchange=exchange)

    g2, du2, df2, a2, dh2, gpost2, gpre2 = _ffn_bwd(dy, f2, ffn2_post_w, h2, ffn2_pre_w, u2, w2b, w1b, name="ffn2_bwd")
    dw1b, = grad(a2, du2.reshape(N_DEV, T, 2 * hp), "ffn2_dw1")
    dw2b = grad(g2, df2, "ffn2_dw2")[0].reshape(N_DEV, hp, D)
    (dmb, merged, dgm, dgr, domla, do, delta, doret, gated, drg, dyn, gpostm, ggn) = _merge_bwd(
        dh2, m, mix_post_w, omla, oret, proj, yn, ret_gn_w, o, w_o, w_bm_p, w_br, name="merge_bwd")
    dw_out = grad(merged, dmb, "dw_out")[0][0]
    dw_bm_p = grad(o, domla, "dw_branch_mla")[0][0]
    dw_br = grad(gated, doret, "dw_branch_ret")[0][0]
    dq, dk, dv, *recv_ffn2 = _flash_bwd(q, k, v, do, lse, delta, name="mla_attn_bwd_scatter_ffn2",
                                        exchange=_Scatter([dw1b, dw2b]))
    da, dql, dkvl, gqn, gkvn = _mla_prep_bwd(dq, dk, dv, proj, pos, mla_q_norm_w, mla_kv_norm_w, w_uq_p, w_kv_p, tab_mla, name="mla_prep_bwd")
    dw_uq_p = grad(qn, dql, "dw_uq")[0][0]
    dw_kv_p = grad(kvn, dkvl, "dw_ukv")[0][0]
    drq, drk, drv = _ret_bwd(dyn, ypre, proj, pos, tab_ret, rprev, name="retention_bwd")
    dproj = jnp.concatenate([drq, drk, drv, drg, da, dgm, dgr], axis=1)
    dw_in_p = grad(a1, dproj, "dw_in")[0][0]
    dh1, gmixpre = _proj_bwd(dproj, w_in_p, h1, mix_pre_w, dh2, name="mixer_in_bwd")

    unhead = lambda a, h, wd: a.reshape(a.shape[0], h, HP)[:, :, :wd].reshape(a.shape[0], h * wd)
    c0 = 4 * RW
    dw_in = jnp.concatenate([
        dw_in_p[:, c0:c0 + 384], dw_in_p[:, c0 + 384:c0 + 640], dw_in_p[:, c0 + 640 + MLA_NOPE:c0 + 640 + MLA_NOPE + MLA_ROPE],
        unhead(dw_in_p[:, 0:RW], RET_HEADS, RET_DK), unhead(dw_in_p[:, RW:2 * RW], RET_HEADS, RET_DK),
        dw_in_p[:, 2 * RW:3 * RW], dw_in_p[:, 3 * RW:4 * RW],
        dw_in_p[:, PROJ_FIXED:PROJ_FIXED + D], dw_in_p[:, PROJ_FIXED + D:PROJ_FIXED + 2 * D]], axis=1)
    dw_uq = dw_uq_p.reshape(MLA_Q_RANK, MLA_HEADS, HP)[:, :, :MLA_NOPE + MLA_ROPE].transpose(1, 0, 2)
    dkp = dw_kv_p[:, :QW].reshape(MLA_KV_RANK, MLA_HEADS, HP)[:, :, :MLA_NOPE]
    dvp = dw_kv_p[:, QW:].reshape(MLA_KV_RANK, MLA_HEADS, HP)[:, :, :MLA_V]
    dw_ukv = jnp.concatenate([dkp, dvp], axis=2).transpose(1, 0, 2)
    dw_bm = dw_bm_p.reshape(MLA_HEADS, HP, D)[:, :MLA_V].reshape(MLA_HEADS * MLA_V, D)

    mixer_grads = [_col_shards(dw_in), dw_uq, dw_ukv, _col_shards(dw_bm), _col_shards(dw_br),
                   dw_out.reshape(N_DEV, D // N_DEV, D)]
    g1, du1, df1, a0, dx, gpost1, gpre1, *recv_mixer = _ffn_bwd(
        dh1, f1, ffn1_post_w, h0, ffn1_pre_w, u1, w2a, w1a, name="ffn1_bwd_scatter_mixer", exchange=_Scatter(mixer_grads))
    dw2a = grad(g1, df1, "ffn1_dw2")[0].reshape(N_DEV, hp, D)
    dw1a, recv_w2a = grad(a0, du1.reshape(N_DEV, T, 2 * hp), "ffn1_dw1_scatter_dw2", exchange=_Scatter([dw2a]))

    small_g = {"ffn1_pre_w": gpre1, "ffn1_post_w": gpost1, "mix_pre_w": gmixpre, "mla_q_norm_w": gqn,
               "mla_kv_norm_w": gkvn, "ret_gn_w": ggn, "mix_post_w": gpostm, "ffn2_pre_w": gpre2, "ffn2_post_w": gpost2}

    def small_flat(arrs):
        a = jnp.concatenate([z.reshape(-1) for z in arrs])
        a = jnp.pad(a, (0, (-a.size) % (8 * LANES)))
        return a.reshape(-1, LANES)

    sv = small_flat([small_g[nm] for nm, *_ in small])
    recv_w1a, sall = _exchange_alone(_Scatter([dw1a], whole=[sv]), name="scatter_ffn1_dw1")
    parts = dict(zip(mixer, recv_mixer))
    parts.update(ffn1_w1=recv_w1a, ffn1_w2=recv_w2a, ffn2_w1=recv_ffn2[0], ffn2_w2=recv_ffn2[1])
    for nm in ("ffn1_w1", "ffn2_w1"):
        parts[nm] = parts[nm].reshape(N_DEV, D, 2, hp)[:, :, :, :half].reshape(N_DEV, D, 2 * half)

    big_out = {nm: [a[None] for a in _adamw(w[0], parts[nm], m_[0], v_[0], name="adamw_" + nm)]
               for nm, w, m_, v_ in big}
    gs, ds, nms, nvs = _adamw(small_flat([w for _, w, _, _ in small]), sall,
                              small_flat([a for _, _, a, _ in small]), small_flat([a for _, _, _, a in small]),
                              name="adamw_replicated")

    def split_small(flat):
        out, o_ = {}, 0
        fl = flat.reshape(-1)
        for nm, w, _, _ in small:
            out[nm] = fl[o_:o_ + w.size].reshape(w.shape)
            o_ += w.size
        return out

    order = ["ffn1_pre_w", "ffn1_w1", "ffn1_w2", "ffn1_post_w", "mix_pre_w", "w_in", "mla_q_norm_w", "mla_w_uq",
             "mla_kv_norm_w", "mla_w_ukv", "ret_gn_w", "w_branch_mla", "w_branch_ret", "w_out", "mix_post_w",
             "ffn2_pre_w", "ffn2_w1", "ffn2_w2", "ffn2_post_w"]
    outs = [loss, dx[None]]
    for i, fs in enumerate((gs, ds, nms, nvs)):
        both = {**{nm: big_out[nm][i] for nm in big_out}, **split_small(fs)}
        outs += [both[nm] for nm in order]
    return tuple(outs)
```

```python
import math

import numpy as np
import jax
import jax.numpy as jnp
from jax import lax
from jax.experimental import pallas as pl
from jax.experimental.pallas import tpu as pltpu

F32, BF16 = jnp.float32, jnp.bfloat16

MLA_HEADS, MLA_NOPE, MLA_ROPE, MLA_V = 8, 64, 32, 64
MLA_Q_RANK, MLA_KV_RANK = 384, 256
RET_HEADS, RET_DK, RET_DV = 4, 64, 128
ROPE_BASE, NORM_EPS, GN_EPS = 10000.0, 1e-6, 1e-6
ADAM_LR, ADAM_B1, ADAM_B2, ADAM_EPS, ADAM_WD, ADAM_STEP = 0.001, 0.9, 0.999, 1e-08, 0.01, 10
ATTN_SCALE = 1.0 / math.sqrt(MLA_NOPE + MLA_ROPE)

N_DEV = 8
LANES = 128
HP = LANES
QW = MLA_HEADS * HP
RW = RET_HEADS * HP
AW = 1024
PROJ_FIXED = 4 * RW + AW
NEG = -1e30

TOKEN_TILE = 512
ATTN_TILE = 512
ATTN_CHAINS = 2
FFN_CHAINS = 2
RET_TILE = 256
GRAD_TILE_CAP = 1408
GRAD_TOKEN_TILE = 2048
MERGE_TILE = 256
VMEM_LIMIT = 56 * 1024 * 1024


def _tile(n, cap, mult=LANES):
    if n <= cap:
        return n
    best = None
    for t in range(mult, cap + 1, mult):
        if n % t == 0:
            best = t
    assert best is not None, (n, cap, mult)
    return best


def _params(sem):
    return pltpu.CompilerParams(dimension_semantics=sem, vmem_limit_bytes=VMEM_LIMIT)


def _dot(a, b):
    return lax.dot_general(a, b, (((1,), (0,)), ((), ())), preferred_element_type=F32)


def _dot_nt(a, b):
    return lax.dot_general(a, b, (((1,), (1,)), ((), ())), preferred_element_type=F32)


def _dot_tn(a, b):
    return lax.dot_general(a, b, (((0,), (0,)), ((), ())), preferred_element_type=F32)


def _sigmoid(x):
    return pl.reciprocal(1.0 + jnp.exp(-x), approx=True)


def _rms_fwd(x, w):
    r = lax.rsqrt(jnp.mean(x * x, axis=-1, keepdims=True) + NORM_EPS)
    return x * r * w


def _rms_bwd(x, w, dy):
    r = lax.rsqrt(jnp.mean(x * x, axis=-1, keepdims=True) + NORM_EPS)
    xh = x * r
    g = dy * w
    dx = r * (g - xh * jnp.mean(g * xh, axis=-1, keepdims=True))
    return dx, jnp.sum(dy * xh, axis=0, keepdims=True)


def _rope_table(first, half):
    inv = (np.float32(ROPE_BASE) ** (-(np.arange(half, dtype=np.float32) / np.float32(half)))).astype(np.float32)
    tab = np.zeros((8, LANES), np.float32)
    tab[0, first:first + half] = inv
    tab[0, first + half:first + 2 * half] = inv
    tab[1, first:first + half] = -1.0
    tab[2, first + half:first + 2 * half] = 1.0
    return jnp.asarray(tab)


def _rope_cs(pos, tab_ref):
    ang = pos * tab_ref[0:1, :]
    s = jnp.sin(ang)
    return jnp.cos(ang), s * tab_ref[1:2, :], s * tab_ref[2:3, :]


def _rope(x, cs, half, inverse=False):
    c, s1, s2 = cs
    a = pltpu.roll(x, LANES - half, 1) * s1 + pltpu.roll(x, half, 1) * s2
    return x * c - a if inverse else x * c + a


def _call(body, *, name, grid, in_specs, out_specs, out_shape, scratch_shapes, args, exchange=None):
    sem = ("arbitrary",) * len(grid)
    if exchange is None:
        return pl.pallas_call(body, name=name, grid=grid, in_specs=in_specs, out_specs=out_specs,
                              out_shape=out_shape, scratch_shapes=scratch_shapes, compiler_params=_params(sem))(*args)
    n_in, n_out, e = len(in_specs), len(out_specs), exchange.n
    total = math.prod(grid)

    def carried(*refs):
        own = refs[:n_in] + refs[n_in + e:n_in + e + n_out] + refs[n_in + 2 * e + n_out:len(refs) - 3]
        ex_refs = (refs[n_in:n_in + e], refs[n_in + e + n_out:n_in + 2 * e + n_out], refs[len(refs) - 3:])
        step = pl.program_id(0)
        for d in range(1, len(grid)):
            step = step * grid[d] + pl.program_id(d)

        @pl.when(step == 0)
        def _():
            exchange.phase(0, *ex_refs)

        @pl.when(step == total // 2)
        def _():
            exchange.phase(1, *ex_refs)

        body(*own)

        @pl.when(step == total - 1)
        def _():
            exchange.phase(2, *ex_refs)

    anyspec = pl.BlockSpec(memory_space=pl.ANY)
    return pl.pallas_call(
        carried, name=name, grid=grid, in_specs=list(in_specs) + [anyspec] * e,
        out_specs=list(out_specs) + [anyspec] * e, out_shape=list(out_shape) + exchange.out_shape,
        scratch_shapes=list(scratch_shapes) + exchange.scratch, compiler_params=_params(sem),
    )(*args, *exchange.operands)


def _ffn_fwd(h, pre_w, w1, w2, post_w, target, *, name, exchange=None):
    T, D = h.shape
    nk, ck = w2.shape[0], w2.shape[1]
    tT = min(TOKEN_TILE, T)
    nT = T // tT
    with_loss = target is not None

    def body(*refs):
        if with_loss:
            (h_ref, pre_ref, w1g_ref, w1u_ref, w2_ref, post_ref, tgt_ref,
             u_ref, f_ref, ho_ref, dy_ref, loss_ref, a_s, acc) = refs
        else:
            (h_ref, pre_ref, w1g_ref, w1u_ref, w2_ref, post_ref,
             u_ref, f_ref, ho_ref, a_s, acc) = refs
        k = pl.program_id(1)

        @pl.when(k == 0)
        def _():
            a_s[...] = _rms_fwd(h_ref[...], pre_ref[...]).astype(BF16)
            acc[...] = jnp.zeros_like(acc)

        for c in range(FFN_CHAINS):
            rs = slice(c * (tT // FFN_CHAINS), (c + 1) * (tT // FFN_CHAINS))
            a = a_s[rs, :]
            ug = _dot_nt(a, w1g_ref[...])
            uu = _dot_nt(a, w1u_ref[...])
            u_ref[0, rs, :] = ug.astype(BF16)
            u_ref[1, rs, :] = uu.astype(BF16)
            acc[rs, :] += _dot((ug * _sigmoid(ug) * uu).astype(BF16), w2_ref[...])

        @pl.when(k == nk - 1)
        def _():
            f = acc[...]
            f_ref[...] = f
            ho = h_ref[...] + 0.5 * _rms_fwd(f, post_ref[...])
            ho_ref[...] = ho
            if with_loss:
                e = ho - tgt_ref[...]
                dy_ref[...] = e * (1.0 / D)
                loss_ref[...] = jnp.full(loss_ref.shape, (0.5 / D) * jnp.sum(e * e), F32)

    row = pl.BlockSpec((tT, D), lambda i, k: (i, 0))
    vec = pl.BlockSpec((1, D), lambda i, k: (0, 0))
    in_specs = [row, vec,
                pl.BlockSpec((None, ck, D), lambda i, k: (k, 0, 0)),
                pl.BlockSpec((None, ck, D), lambda i, k: (nk + k, 0, 0)),
                pl.BlockSpec((None, ck, D), lambda i, k: (k, 0, 0)),
                vec]
    out_shape = [jax.ShapeDtypeStruct((2, nk, T, ck), BF16),
                 jax.ShapeDtypeStruct((T, D), F32),
                 jax.ShapeDtypeStruct((T, D), F32)]
    out_specs = [pl.BlockSpec((2, None, tT, ck), lambda i, k: (0, k, i, 0)), row, row]
    args = [h, pre_w, w1, w1, w2, post_w]
    if with_loss:
        in_specs.append(row)
        args.append(target)
        out_shape += [jax.ShapeDtypeStruct((T, D), F32), jax.ShapeDtypeStruct((nT * 8, LANES), F32)]
        out_specs += [row, pl.BlockSpec((8, LANES), lambda i, k: (i, 0))]
    return _call(body, name=name, grid=(nT, nk), in_specs=in_specs, out_specs=out_specs, out_shape=out_shape,
                 scratch_shapes=[pltpu.VMEM((tT, D), BF16), pltpu.VMEM((tT, D), F32)], args=args, exchange=exchange)


def _ffn_bwd(dho, f, post_w, h, pre_w, u, w2, w1, *, name, exchange=None):
    T, D = h.shape
    nk, ck = w2.shape[0], w2.shape[1]
    tT = min(TOKEN_TILE, T)
    nT = T // tT

    def body(dho_ref, f_ref, post_ref, h_ref, pre_ref, u_ref, w2_ref, w1g_ref, w1u_ref,
             g_ref, du_ref, df_ref, a_ref, dh_ref, gpost_ref, gpre_ref, df_s, da_acc):
        i, k = pl.program_id(0), pl.program_id(1)

        @pl.when(jnp.logical_and(i == 0, k == 0))
        def _():
            gpost_ref[...] = jnp.zeros_like(gpost_ref)
            gpre_ref[...] = jnp.zeros_like(gpre_ref)

        @pl.when(k == 0)
        def _():
            dx, dw = _rms_bwd(f_ref[...], post_ref[...], 0.5 * dho_ref[...])
            dfb = dx.astype(BF16)
            df_s[...] = dfb
            df_ref[...] = dfb
            gpost_ref[...] += dw
            a_ref[...] = _rms_fwd(h_ref[...], pre_ref[...]).astype(BF16)
            da_acc[...] = jnp.zeros_like(da_acc)

        for c in range(FFN_CHAINS):
            rs = slice(c * (tT // FFN_CHAINS), (c + 1) * (tT // FFN_CHAINS))
            dg = _dot_nt(df_s[rs, :], w2_ref[...])
            ug = u_ref[0, rs, :].astype(F32)
            uu = u_ref[1, rs, :].astype(F32)
            sg = _sigmoid(ug)
            sl = ug * sg
            g_ref[rs, :] = (sl * uu).astype(BF16)
            dug = (dg * uu * (sg + sl * (1.0 - sg))).astype(BF16)
            duu = (dg * sl).astype(BF16)
            du_ref[0, rs, :] = dug
            du_ref[1, rs, :] = duu
            da_acc[rs, :] += _dot(dug, w1g_ref[...]) + _dot(duu, w1u_ref[...])

        @pl.when(k == nk - 1)
        def _():
            dx, dw = _rms_bwd(h_ref[...], pre_ref[...], da_acc[...])
            dh_ref[...] = dho_ref[...] + dx
            gpre_ref[...] += dw

    row = pl.BlockSpec((tT, D), lambda i, k: (i, 0))
    vec = pl.BlockSpec((1, D), lambda i, k: (0, 0))
    return _call(
        body, name=name, grid=(nT, nk),
        in_specs=[row, row, vec, row, vec,
                  pl.BlockSpec((2, None, tT, ck), lambda i, k: (0, k, i, 0)),
                  pl.BlockSpec((None, ck, D), lambda i, k: (k, 0, 0)),
                  pl.BlockSpec((None, ck, D), lambda i, k: (k, 0, 0)),
                  pl.BlockSpec((None, ck, D), lambda i, k: (nk + k, 0, 0))],
        out_specs=[pl.BlockSpec((None, tT, ck), lambda i, k: (k, i, 0)),
                   pl.BlockSpec((2, None, tT, ck), lambda i, k: (0, k, i, 0)),
                   row, row, row, vec, vec],
        out_shape=[jax.ShapeDtypeStruct((nk, T, ck), BF16),
                   jax.ShapeDtypeStruct((2, nk, T, ck), BF16),
                   jax.ShapeDtypeStruct((T, D), BF16),
                   jax.ShapeDtypeStruct((T, D), BF16),
                   jax.ShapeDtypeStruct((T, D), F32),
                   jax.ShapeDtypeStruct((1, D), F32),
                   jax.ShapeDtypeStruct((1, D), F32)],
        scratch_shapes=[pltpu.VMEM((tT, D), BF16), pltpu.VMEM((tT, D), F32)],
        args=(dho, f, post_w, h, pre_w, u, w2, w1, w1), exchange=exchange)


def _matmul_tn(x, dy, *, name, exchange=None):
    Px, T, K = x.shape
    Py, _, N = dy.shape
    P = max(Px, Py)
    tT, tK, tN = min(GRAD_TOKEN_TILE, T), _tile(K, GRAD_TILE_CAP), _tile(N, GRAD_TILE_CAP)
    nt = T // tT

    def body(x_ref, dy_ref, o_ref, acc):
        t = pl.program_id(3)

        @pl.when(t == 0)
        def _():
            acc[...] = jnp.zeros_like(acc)

        acc[...] += _dot_tn(x_ref[...], dy_ref[...])

        @pl.when(t == nt - 1)
        def _():
            o_ref[...] = acc[...].astype(BF16)

    return _call(
        body, name=name, grid=(P, K // tK, N // tN, nt),
        in_specs=[pl.BlockSpec((None, tT, tK), lambda p, a, b, t: (p if Px > 1 else 0, t, a)),
                  pl.BlockSpec((None, tT, tN), lambda p, a, b, t: (p if Py > 1 else 0, t, b))],
        out_specs=[pl.BlockSpec((None, tK, tN), lambda p, a, b, t: (p, a, b))],
        out_shape=[jax.ShapeDtypeStruct((P, K, N), BF16)],
        scratch_shapes=[pltpu.VMEM((tK, tN), F32)], args=(x, dy), exchange=exchange)


def _rms_matmul(h, wn, w, *, name):
    T, D = h.shape
    N = w.shape[0]
    tT, tN = min(TOKEN_TILE, T), _tile(N, 1024)

    def body(h_ref, wn_ref, w_ref, y_ref, a_ref):
        @pl.when(pl.program_id(1) == 0)
        def _():
            a_ref[...] = _rms_fwd(h_ref[...], wn_ref[...]).astype(BF16)

        y_ref[...] = _dot_nt(a_ref[...], w_ref[...]).astype(BF16)

    return pl.pallas_call(
        body, name=name, grid=(T // tT, N // tN),
        in_specs=[pl.BlockSpec((tT, D), lambda i, j: (i, 0)),
                  pl.BlockSpec((1, D), lambda i, j: (0, 0)),
                  pl.BlockSpec((tN, D), lambda i, j: (j, 0))],
        out_specs=[pl.BlockSpec((tT, tN), lambda i, j: (i, j)),
                   pl.BlockSpec((tT, D), lambda i, j: (i, 0))],
        out_shape=[jax.ShapeDtypeStruct((T, N), BF16), jax.ShapeDtypeStruct((T, D), BF16)],
        compiler_params=_params(("parallel", "arbitrary")),
    )(h, wn, w)


def _proj_bwd(dproj, w, h, wn, dres, *, name):
    T, D = h.shape
    N = w.shape[0]
    tT, tN = min(TOKEN_TILE, T), _tile(N, 1024)
    nn = N // tN

    def body(dp_ref, w_ref, h_ref, wn_ref, dres_ref, dh_ref, gw_ref, acc):
        i, j = pl.program_id(0), pl.program_id(1)

        @pl.when(jnp.logical_and(i == 0, j == 0))
        def _():
            gw_ref[...] = jnp.zeros_like(gw_ref)

        @pl.when(j == 0)
        def _():
            acc[...] = jnp.zeros_like(acc)

        acc[...] += _dot(dp_ref[...], w_ref[...])

        @pl.when(j == nn - 1)
        def _():
            dx, dw = _rms_bwd(h_ref[...], wn_ref[...], acc[...])
            dh_ref[...] = dres_ref[...] + dx
            gw_ref[...] += dw

    row = pl.BlockSpec((tT, D), lambda i, j: (i, 0))
    vec = pl.BlockSpec((1, D), lambda i, j: (0, 0))
    return pl.pallas_call(
        body, name=name, grid=(T // tT, nn),
        in_specs=[pl.BlockSpec((tT, tN), lambda i, j: (i, j)),
                  pl.BlockSpec((tN, D), lambda i, j: (j, 0)), row, vec, row],
        out_specs=[row, vec],
        out_shape=[jax.ShapeDtypeStruct((T, D), F32), jax.ShapeDtypeStruct((1, D), F32)],
        scratch_shapes=[pltpu.VMEM((tT, D), F32)],
        compiler_params=_params(("arbitrary", "arbitrary")),
    )(dproj, w, h, wn, dres)


def _mla_prep_fwd(proj, pos, qn_w, kvn_w, w_uq, w_kv, tab, *, name):
    T = proj.shape[0]
    tT = min(TOKEN_TILE, T)
    a_blk = PROJ_FIXED // AW - 1

    def body(a_ref, pos_ref, qnw_ref, kvnw_ref, wuq_ref, wkv_ref, tab_ref,
             q_ref, k_ref, v_ref, qn_ref, kvn_ref):
        cq = a_ref[:, 0:MLA_Q_RANK].astype(F32)
        ckv = a_ref[:, MLA_Q_RANK:MLA_Q_RANK + MLA_KV_RANK].astype(F32)
        kr = a_ref[:, 640:768].astype(F32)
        qn = _rms_fwd(cq, qnw_ref[...]).astype(BF16)
        kvn = _rms_fwd(ckv, kvnw_ref[...]).astype(BF16)
        qn_ref[...] = qn
        kvn_ref[...] = kvn
        cs = _rope_cs(pos_ref[...], tab_ref)
        q = _dot_nt(qn, wuq_ref[...])
        kv = _dot(kvn, wkv_ref[...])
        krr = _rope(kr, cs, MLA_ROPE // 2)
        for hd in range(MLA_HEADS):
            sl = slice(hd * HP, (hd + 1) * HP)
            q_ref[:, sl] = (_rope(q[:, sl], cs, MLA_ROPE // 2) * ATTN_SCALE).astype(BF16)
            k_ref[:, sl] = (kv[:, sl] + krr).astype(BF16)
        v_ref[...] = kv[:, QW:].astype(BF16)

    def full(r, c):
        return pl.BlockSpec((r, c), lambda i: (0, 0))

    def rows(c):
        return pl.BlockSpec((tT, c), lambda i: (i, 0))

    return pl.pallas_call(
        body, name=name, grid=(T // tT,),
        in_specs=[pl.BlockSpec((tT, AW), lambda i: (i, a_blk)), rows(1),
                  full(1, MLA_Q_RANK), full(1, MLA_KV_RANK),
                  full(QW, MLA_Q_RANK), full(MLA_KV_RANK, 2 * QW), full(8, LANES)],
        out_specs=[rows(QW), rows(QW), rows(QW), rows(MLA_Q_RANK), rows(MLA_KV_RANK)],
        out_shape=[jax.ShapeDtypeStruct((T, QW), BF16)] * 3
        + [jax.ShapeDtypeStruct((T, MLA_Q_RANK), BF16), jax.ShapeDtypeStruct((T, MLA_KV_RANK), BF16)],
        compiler_params=_params(("parallel",)),
    )(proj, pos, qn_w, kvn_w, w_uq, w_kv, tab)


def _mla_prep_bwd(dq, dk, dv, proj, pos, qn_w, kvn_w, w_uq, w_kv, tab, *, name):
    T = proj.shape[0]
    tT = min(TOKEN_TILE, T)
    a_blk = PROJ_FIXED // AW - 1

    def body(dq_ref, dk_ref, dv_ref, a_ref, pos_ref, qnw_ref, kvnw_ref, wuq_ref, wkv_ref, tab_ref,
             da_ref, dql_ref, dkvl_ref, gqn_ref, gkvn_ref):
        @pl.when(pl.program_id(0) == 0)
        def _():
            gqn_ref[...] = jnp.zeros_like(gqn_ref)
            gkvn_ref[...] = jnp.zeros_like(gkvn_ref)

        cs = _rope_cs(pos_ref[...], tab_ref)
        dkr = jnp.zeros((tT, HP), F32)
        for hd in range(MLA_HEADS):
            sl = slice(hd * HP, (hd + 1) * HP)
            dql_ref[:, sl] = (_rope(dq_ref[:, sl], cs, MLA_ROPE // 2, inverse=True) * ATTN_SCALE).astype(BF16)
            dkh = dk_ref[:, sl]
            dkr = dkr + dkh
            dkvl_ref[:, sl] = dkh.astype(BF16)
        dkvl_ref[:, QW:] = dv_ref[...]
        dqn = _dot(dql_ref[...], wuq_ref[...])
        dkvn = _dot_nt(dkvl_ref[...], wkv_ref[...])
        cq = a_ref[:, 0:MLA_Q_RANK].astype(F32)
        ckv = a_ref[:, MLA_Q_RANK:MLA_Q_RANK + MLA_KV_RANK].astype(F32)
        dcq, gq = _rms_bwd(cq, qnw_ref[...], dqn)
        dckv, gkv = _rms_bwd(ckv, kvnw_ref[...], dkvn)
        gqn_ref[...] += gq
        gkvn_ref[...] += gkv
        da_ref[:, 0:MLA_Q_RANK] = dcq.astype(BF16)
        da_ref[:, MLA_Q_RANK:MLA_Q_RANK + MLA_KV_RANK] = dckv.astype(BF16)
        da_ref[:, 640:768] = _rope(dkr, cs, MLA_ROPE // 2, inverse=True).astype(BF16)
        da_ref[:, 768:AW] = jnp.zeros((tT, AW - 768), BF16)

    def full(r, c):
        return pl.BlockSpec((r, c), lambda i: (0, 0))

    def rows(c):
        return pl.BlockSpec((tT, c), lambda i: (i, 0))

    return pl.pallas_call(
        body, name=name, grid=(T // tT,),
        in_specs=[rows(QW), rows(QW), rows(QW), pl.BlockSpec((tT, AW), lambda i: (i, a_blk)), rows(1),
                  full(1, MLA_Q_RANK), full(1, MLA_KV_RANK),
                  full(QW, MLA_Q_RANK), full(MLA_KV_RANK, 2 * QW), full(8, LANES)],
        out_specs=[rows(AW), rows(QW), rows(2 * QW), full(1, MLA_Q_RANK), full(1, MLA_KV_RANK)],
        out_shape=[jax.ShapeDtypeStruct((T, AW), BF16), jax.ShapeDtypeStruct((T, QW), BF16),
                   jax.ShapeDtypeStruct((T, 2 * QW), BF16),
                   jax.ShapeDtypeStruct((1, MLA_Q_RANK), F32), jax.ShapeDtypeStruct((1, MLA_KV_RANK), F32)],
        compiler_params=_params(("arbitrary",)),
    )(dq, dk, dv, proj, pos, qn_w, kvn_w, w_uq, w_kv, tab)


def _flash_fwd(q, k, v, *, name, exchange=None):
    T = q.shape[0]
    H = q.shape[1] // HP
    tq = min(ATTN_TILE, T)
    nq = T // tq

    sub = tq // ATTN_CHAINS

    def body(q_ref, k_ref, v_ref, o_ref, lse_ref):
        qi = pl.program_id(1)
        qs = [q_ref[c * sub:(c + 1) * sub, :] for c in range(ATTN_CHAINS)]

        def update(carry, off, masked):
            kb = k_ref[pl.ds(off, tq), :]
            vb = v_ref[pl.ds(off, tq), :]
            out = []
            for c in range(ATTN_CHAINS):
                m_prev, l_prev, acc = carry[c]
                s = _dot_nt(qs[c], kb)
                if masked:
                    rows = lax.broadcasted_iota(jnp.int32, (sub, tq), 0) + c * sub
                    s = jnp.where(rows >= lax.broadcasted_iota(jnp.int32, (sub, tq), 1), s, NEG)
                m_new = jnp.maximum(m_prev, jnp.max(s, axis=1, keepdims=True))
                alpha = jnp.exp(m_prev - m_new)
                p = jnp.exp(s - m_new)
                out.append((m_new, alpha * l_prev + jnp.sum(p, axis=1, keepdims=True),
                            alpha * acc + _dot(p.astype(BF16), vb)))
            return tuple(out)

        init = tuple((jnp.full((sub, 1), NEG, F32), jnp.zeros((sub, 1), F32), jnp.zeros((sub, HP), F32))
                     for _ in range(ATTN_CHAINS))
        carry = lax.fori_loop(0, qi, lambda j, cr: update(cr, pl.multiple_of(j * tq, tq), False), init)
        carry = update(carry, pl.multiple_of(qi * tq, tq), True)
        for c in range(ATTN_CHAINS):
            m_fin, l_fin, acc = carry[c]
            o_ref[c * sub:(c + 1) * sub, :] = (acc / l_fin).astype(BF16)
            lse_ref[c * sub:(c + 1) * sub, :] = jnp.broadcast_to(m_fin + jnp.log(l_fin), (sub, HP))

    qspec = pl.BlockSpec((tq, HP), lambda h, i: (i, h))
    kspec = pl.BlockSpec((T, HP), lambda h, i: (0, h))
    return _call(
        body, name=name, grid=(H, nq),
        in_specs=[qspec, kspec, kspec], out_specs=[qspec, qspec],
        out_shape=[jax.ShapeDtypeStruct((T, H * HP), BF16), jax.ShapeDtypeStruct((T, H * HP), F32)],
        scratch_shapes=[], args=(q, k, v), exchange=exchange)


def _flash_bwd(q, k, v, do, lse, delta, *, name, exchange=None):
    T = q.shape[0]
    H = q.shape[1] // HP
    tq = min(ATTN_TILE, T)
    nq = T // tq
    sub = tq // ATTN_CHAINS

    def body(k_ref, v_ref, q_ref, do_ref, lse_ref, dl_ref, dq_ref, dk_ref, dv_ref):
        ki = pl.program_id(1)

        @pl.when(ki == 0)
        def _():
            dq_ref[...] = jnp.zeros_like(dq_ref)

        kb = k_ref[...]
        vb = v_ref[...]

        def step(carry, j, masked):
            dk_acc, dv_acc = carry
            for c in range(ATTN_CHAINS):
                rows = pl.ds(pl.multiple_of(j * tq + c * sub, sub), sub)
                qb = q_ref[rows, :]
                dob = do_ref[rows, :]
                s = _dot_nt(qb, kb)
                if masked:
                    ri = lax.broadcasted_iota(jnp.int32, (sub, tq), 0) + c * sub
                    s = jnp.where(ri >= lax.broadcasted_iota(jnp.int32, (sub, tq), 1), s, NEG)
                p = jnp.exp(s - lse_ref[rows, 0:1])
                dv_acc = dv_acc + _dot_tn(p.astype(BF16), dob)
                dp = _dot_nt(dob, vb)
                ds = (p * (dp - dl_ref[rows, 0:1])).astype(BF16)
                dk_acc = dk_acc + _dot_tn(ds, qb)
                dq_ref[rows, :] += _dot(ds, kb)
            return dk_acc, dv_acc

        carry = step((jnp.zeros((tq, HP), F32), jnp.zeros((tq, HP), F32)), ki, True)
        dk_acc, dv_acc = lax.fori_loop(ki + 1, nq, lambda j, cr: step(cr, j, False), carry)
        dk_ref[...] = dk_acc
        dv_ref[...] = dv_acc.astype(BF16)

    kspec = pl.BlockSpec((tq, HP), lambda h, j: (j, h))
    full = pl.BlockSpec((T, HP), lambda h, j: (0, h))
    return _call(
        body, name=name, grid=(H, nq),
        in_specs=[kspec, kspec, full, full, full, full], out_specs=[full, kspec, kspec],
        out_shape=[jax.ShapeDtypeStruct((T, H * HP), F32), jax.ShapeDtypeStruct((T, H * HP), F32),
                   jax.ShapeDtypeStruct((T, H * HP), BF16)],
        scratch_shapes=[], args=(k, v, q, do, lse, delta), exchange=exchange)


def _ret_consts(cc, hd):
    lg = math.log(1.0 - 2.0 ** (-5.0 - hd))
    diff = (lax.broadcasted_iota(jnp.int32, (cc, cc), 0) - lax.broadcasted_iota(jnp.int32, (cc, cc), 1)).astype(F32)
    decay = jnp.where(diff >= 0, jnp.exp(jnp.maximum(diff, 0.0) * lg), 0.0)
    idx = lax.broadcasted_iota(jnp.int32, (cc, 1), 0).astype(F32)
    zeta = jnp.exp((cc - 1.0 - idx) * lg)
    xi = jnp.exp((idx + 1.0) * lg)
    return decay, zeta, xi, math.exp(cc * lg)


def _ret_fwd(proj, pos, tab, *, name):
    T = proj.shape[0]
    cc = min(RET_TILE, T)
    n = T // cc

    def body(rq_ref, rk_ref, rv_ref, pos_ref, tab_ref, y_ref, yn_ref, rprev_ref, r_s):
        @pl.when(pl.program_id(0) == 0)
        def _():
            r_s[...] = jnp.zeros_like(r_s)

        cs = _rope_cs(pos_ref[...], tab_ref)
        for hd in range(RET_HEADS):
            sl = slice(hd * HP, (hd + 1) * HP)
            decay, zeta, xi, gc = _ret_consts(cc, hd)
            q = _rope(rq_ref[:, sl].astype(F32), cs, RET_DK // 2).astype(BF16)
            kf = _rope(rk_ref[:, sl].astype(F32), cs, RET_DK // 2) * (RET_DK ** -0.5)
            k = kf.astype(BF16)
            v = rv_ref[:, sl]
            r = r_s[hd]
            rprev_ref[0, hd] = r
            inner = (_dot_nt(q, k) * decay).astype(BF16)
            y = _dot(inner, v) + _dot(q, r.astype(BF16)) * xi
            r_s[hd] = r * gc + _dot_tn((kf * zeta).astype(BF16), v)
            y_ref[:, sl] = y
            mu = jnp.mean(y, axis=-1, keepdims=True)
            yc = y - mu
            var = jnp.mean(yc * yc, axis=-1, keepdims=True)
            yn_ref[:, sl] = (yc * lax.rsqrt(var + GN_EPS)).astype(BF16)

    def blk(j):
        return pl.BlockSpec((cc, RW), lambda i: (i, j))

    return pl.pallas_call(
        body, name=name, grid=(n,),
        in_specs=[blk(0), blk(1), blk(2), pl.BlockSpec((cc, 1), lambda i: (i, 0)),
                  pl.BlockSpec((8, LANES), lambda i: (0, 0))],
        out_specs=[blk(0), blk(0), pl.BlockSpec((1, RET_HEADS, HP, RET_DV), lambda i: (i, 0, 0, 0))],
        out_shape=[jax.ShapeDtypeStruct((T, RW), F32), jax.ShapeDtypeStruct((T, RW), BF16),
                   jax.ShapeDtypeStruct((n, RET_HEADS, HP, RET_DV), F32)],
        scratch_shapes=[pltpu.VMEM((RET_HEADS, HP, RET_DV), F32)],
        compiler_params=_params(("arbitrary",)),
    )(proj, proj, proj, pos, tab)


def _ret_bwd(dyn, y, proj, pos, tab, rprev, *, name):
    T = proj.shape[0]
    cc = min(RET_TILE, T)
    n = T // cc

    def body(dyn_ref, y_ref, rq_ref, rk_ref, rv_ref, pos_ref, tab_ref, rprev_ref,
             drq_ref, drk_ref, drv_ref, dr_s):
        @pl.when(pl.program_id(0) == 0)
        def _():
            dr_s[...] = jnp.zeros_like(dr_s)

        cs = _rope_cs(pos_ref[...], tab_ref)
        for hd in range(RET_HEADS):
            sl = slice(hd * HP, (hd + 1) * HP)
            decay, zeta, xi, gc = _ret_consts(cc, hd)
            q = _rope(rq_ref[:, sl].astype(F32), cs, RET_DK // 2).astype(BF16)
            kf = _rope(rk_ref[:, sl].astype(F32), cs, RET_DK // 2) * (RET_DK ** -0.5)
            k = kf.astype(BF16)
            v = rv_ref[:, sl]
            yv = y_ref[:, sl]
            mu = jnp.mean(yv, axis=-1, keepdims=True)
            yc = yv - mu
            rs = lax.rsqrt(jnp.mean(yc * yc, axis=-1, keepdims=True) + GN_EPS)
            yn = yc * rs
            dn = dyn_ref[:, sl]
            dy = rs * (dn - jnp.mean(dn, axis=-1, keepdims=True) - yn * jnp.mean(dn * yn, axis=-1, keepdims=True))
            dyb = dy.astype(BF16)
            dyx = (dy * xi).astype(BF16)
            dr = dr_s[hd]
            drb = dr.astype(BF16)
            inner = (_dot_nt(q, k) * decay).astype(BF16)
            da = (_dot_nt(dyb, v) * decay).astype(BF16)
            dv = _dot_tn(inner, dyb) + _dot((kf * zeta).astype(BF16), drb)
            dq = _dot(da, k) + _dot_nt(dyx, rprev_ref[0, hd].astype(BF16))
            dk = _dot_tn(da, q) + _dot_nt(v, drb) * zeta
            dr_s[hd] = dr * gc + _dot_tn(q, dyx)
            drq_ref[:, sl] = _rope(dq, cs, RET_DK // 2, inverse=True).astype(BF16)
            drk_ref[:, sl] = _rope(dk * (RET_DK ** -0.5), cs, RET_DK // 2, inverse=True).astype(BF16)
            drv_ref[:, sl] = dv.astype(BF16)

    def blk(j):
        return pl.BlockSpec((cc, RW), lambda i: (n - 1 - i, j))

    return pl.pallas_call(
        body, name=name, grid=(n,),
        in_specs=[blk(0), blk(0), blk(0), blk(1), blk(2), pl.BlockSpec((cc, 1), lambda i: (n - 1 - i, 0)),
                  pl.BlockSpec((8, LANES), lambda i: (0, 0)),
                  pl.BlockSpec((1, RET_HEADS, HP, RET_DV), lambda i: (n - 1 - i, 0, 0, 0))],
        out_specs=[blk(0), blk(0), blk(0)],
        out_shape=[jax.ShapeDtypeStruct((T, RW), BF16)] * 3,
        scratch_shapes=[pltpu.VMEM((RET_HEADS, HP, RET_DV), F32)],
        compiler_params=_params(("arbitrary",)),
    )(dyn, y, proj, proj, proj, pos, tab, rprev)


def _merge_fwd(o, yn, proj, gn_w, w_bm, w_br, w_out, h, post_w, *, name):
    T, D = h.shape
    tT = min(MERGE_TILE, T)
    g_blk = PROJ_FIXED // D

    def body(o_ref, yn_ref, rg_ref, gm_ref, gr_ref, gnw_ref, wbm_ref, wbr_ref, wout_ref, h_ref, post_ref,
             omla_ref, oret_ref, m_ref, ho_ref):
        o_mla = _dot(o_ref[...], wbm_ref[...])
        rg = rg_ref[...].astype(F32)
        gated = (rg * _sigmoid(rg) * (yn_ref[...].astype(F32) * gnw_ref[...])).astype(BF16)
        o_ret = _dot(gated, wbr_ref[...])
        omla_ref[...] = o_mla.astype(BF16)
        oret_ref[...] = o_ret.astype(BF16)
        merged = _sigmoid(gm_ref[...].astype(F32)) * o_mla + _sigmoid(gr_ref[...].astype(F32)) * o_ret
        m = _dot(merged.astype(BF16), wout_ref[...])
        m_ref[...] = m
        ho_ref[...] = h_ref[...] + _rms_fwd(m, post_ref[...])

    def full(r, c):
        return pl.BlockSpec((r, c), lambda i: (0, 0))

    def rows(c, j=0):
        return pl.BlockSpec((tT, c), lambda i: (i, j))

    return pl.pallas_call(
        body, name=name, grid=(T // tT,),
        in_specs=[rows(QW), rows(RW), rows(RW, 3), rows(D, g_blk), rows(D, g_blk + 1), full(1, RW),
                  full(QW, D), full(RW, D), full(D, D), rows(D), full(1, D)],
        out_specs=[rows(D), rows(D), rows(D), rows(D)],
        out_shape=[jax.ShapeDtypeStruct((T, D), BF16), jax.ShapeDtypeStruct((T, D), BF16),
                   jax.ShapeDtypeStruct((T, D), F32), jax.ShapeDtypeStruct((T, D), F32)],
        compiler_params=_params(("parallel",)),
    )(o, yn, proj, proj, proj, gn_w, w_bm, w_br, w_out, h, post_w)


def _merge_bwd(dho, m, post_w, omla, oret, proj, yn, gn_w, o, w_out, w_bm, w_br, *, name):
    T, D = dho.shape
    tT = min(MERGE_TILE, T)
    g_blk = PROJ_FIXED // D

    def body(dho_ref, m_ref, post_ref, omla_ref, oret_ref, rg_ref, gm_ref, gr_ref, yn_ref, gnw_ref, o_ref,
             wout_ref, wbm_ref, wbr_ref,
             dm_ref, merged_ref, dgm_ref, dgr_ref, domla_ref, do_ref, delta_ref, doret_ref, gated_ref,
             drg_ref, dyn_ref, gpost_ref, ggn_ref):
        @pl.when(pl.program_id(0) == 0)
        def _():
            gpost_ref[...] = jnp.zeros_like(gpost_ref)
            ggn_ref[...] = jnp.zeros_like(ggn_ref)

        dm, gp = _rms_bwd(m_ref[...], post_ref[...], dho_ref[...])
        gpost_ref[...] += gp
        dmb = dm.astype(BF16)
        dm_ref[...] = dmb
        dmerged = _dot_nt(dmb, wout_ref[...])
        o_mla = omla_ref[...].astype(F32)
        o_ret = oret_ref[...].astype(F32)
        sgm = _sigmoid(gm_ref[...].astype(F32))
        sgr = _sigmoid(gr_ref[...].astype(F32))
        merged_ref[...] = (sgm * o_mla + sgr * o_ret).astype(BF16)
        dgm_ref[...] = (dmerged * o_mla * sgm * (1.0 - sgm)).astype(BF16)
        dgr_ref[...] = (dmerged * o_ret * sgr * (1.0 - sgr)).astype(BF16)
        domla = (dmerged * sgm).astype(BF16)
        domla_ref[...] = domla
        do = _dot_nt(domla, wbm_ref[...])
        do_ref[...] = do.astype(BF16)
        for hd in range(MLA_HEADS):
            sl = slice(hd * HP, (hd + 1) * HP)
            d = jnp.sum(do[:, sl] * o_ref[:, sl].astype(F32), axis=-1, keepdims=True)
            delta_ref[:, sl] = jnp.broadcast_to(d, (tT, HP))
        doret = (dmerged * sgr).astype(BF16)
        doret_ref[...] = doret
        dgated = _dot_nt(doret, wbr_ref[...])
        rg = rg_ref[...].astype(F32)
        sg = _sigmoid(rg)
        srg = rg * sg
        ynv = yn_ref[...].astype(F32)
        yw = ynv * gnw_ref[...]
        gated_ref[...] = (srg * yw).astype(BF16)
        drg_ref[...] = (dgated * yw * (sg * (1.0 + rg * (1.0 - sg)))).astype(BF16)
        dgs = dgated * srg
        dyn_ref[...] = dgs * gnw_ref[...]
        ggn_ref[...] += jnp.sum(dgs * ynv, axis=0, keepdims=True)

    def full(r, c):
        return pl.BlockSpec((r, c), lambda i: (0, 0))

    def rows(c, j=0):
        return pl.BlockSpec((tT, c), lambda i: (i, j))

    return pl.pallas_call(
        body, name=name, grid=(T // tT,),
        in_specs=[rows(D), rows(D), full(1, D), rows(D), rows(D), rows(RW, 3), rows(D, g_blk), rows(D, g_blk + 1),
                  rows(RW), full(1, RW), rows(QW), full(D, D), full(QW, D), full(RW, D)],
        out_specs=[rows(D), rows(D), rows(D), rows(D), rows(D), rows(QW), rows(QW), rows(D), rows(RW),
                   rows(RW), rows(RW), full(1, D), full(1, RW)],
        out_shape=[jax.ShapeDtypeStruct((T, D), BF16)] * 5
        + [jax.ShapeDtypeStruct((T, QW), BF16), jax.ShapeDtypeStruct((T, QW), F32),
           jax.ShapeDtypeStruct((T, D), BF16), jax.ShapeDtypeStruct((T, RW), BF16),
           jax.ShapeDtypeStruct((T, RW), BF16), jax.ShapeDtypeStruct((T, RW), F32),
           jax.ShapeDtypeStruct((1, D), F32), jax.ShapeDtypeStruct((1, RW), F32)],
        compiler_params=_params(("arbitrary",)),
    )(dho, m, post_w, omla, oret, proj, proj, proj, yn, gn_w, o, w_out, w_bm, w_br)


def _mesh_pos():
    return lax.axis_index("x"), lax.axis_index("y"), lax.axis_index("c")


class _Gather:
    def __init__(self, shards):
        self.operands = list(shards)
        self.n = len(shards)
        self.out_shape = [jax.ShapeDtypeStruct((N_DEV,) + s.shape, s.dtype) for s in shards]
        self.scratch = [pltpu.SemaphoreType.DMA((7 * self.n,)), pltpu.SemaphoreType.DMA((7 * self.n,)),
                        pltpu.SemaphoreType.DMA((self.n,))]

    def phase(self, p, x_refs, out_refs, sems):
        send_sems, recv_sems, local_sems = sems
        x, y, c = _mesh_pos()
        me, sibling = (x, y, c), (x, y, 1 - c)
        chips = [(1 - x, y), (x, 1 - y), (1 - x, 1 - y)]

        def copy(w, k, block, to, src=None):
            slot = out_refs[w].at[4 * block[0] + 2 * block[1] + block[2]]
            return pltpu.make_async_remote_copy(
                src_ref=slot if src is None else src, dst_ref=slot,
                send_sem=send_sems.at[7 * w + k], recv_sem=recv_sems.at[7 * w + k],
                device_id=to, device_id_type=pl.DeviceIdType.MESH)

        for w in range(self.n):
            mine = pltpu.make_async_copy(x_refs[w], out_refs[w].at[4 * x + 2 * y + c], local_sems.at[w])
            first = [copy(w, 0, me, sibling, src=x_refs[w])]
            first += [copy(w, 1 + j, me, (*chip, c), src=x_refs[w]) for j, chip in enumerate(chips)]
            passed = [copy(w, 4 + j, (*chip, c), sibling) for j, chip in enumerate(chips)]
            if p == 0:
                mine.start()
                for cp in first:
                    cp.start()
            elif p == 1:
                for j, chip in enumerate(chips):
                    copy(w, 1 + j, (*chip, c), me).wait_recv()
                    passed[j].start()
            else:
                copy(w, 0, sibling, me).wait_recv()
                for j, chip in enumerate(chips):
                    copy(w, 4 + j, (*chip, 1 - c), me).wait_recv()
                for cp in first + passed:
                    cp.wait_send()
                mine.wait()


class _Scatter:
    def __init__(self, grads, whole=()):
        self.n_sliced = len(grads)
        self.operands = list(grads) + list(whole)
        self.n = len(self.operands)
        self.out_shape = [jax.ShapeDtypeStruct(g.shape, g.dtype) for g in grads]
        self.out_shape += [jax.ShapeDtypeStruct((N_DEV,) + a.shape, a.dtype) for a in whole]
        n_sem = (N_DEV - 1) * self.n
        self.scratch = [pltpu.SemaphoreType.DMA((n_sem,)), pltpu.SemaphoreType.DMA((n_sem,)),
                        pltpu.SemaphoreType.DMA((self.n,))]

    def phase(self, p, in_refs, out_refs, sems):
        if p == 1:
            return
        send_sems, recv_sems, local_sems = sems
        x, y, c = _mesh_pos()
        me = 4 * x + 2 * y + c

        def src(w, dev):
            return in_refs[w].at[dev] if w < self.n_sliced else in_refs[w]

        for w in range(self.n):
            own = pltpu.make_async_copy(src(w, me), out_refs[w].at[me], local_sems.at[w])
            sends, recvs = [], []
            for r in range(1, N_DEV):
                px = 1 - x if r & 4 else x
                py = 1 - y if r & 2 else y
                pc = 1 - c if r & 1 else c
                peer, pidx = (px, py, pc), 4 * px + 2 * py + pc
                k = (N_DEV - 1) * w + r - 1
                sends.append(pltpu.make_async_remote_copy(
                    src_ref=src(w, pidx), dst_ref=out_refs[w].at[me], send_sem=send_sems.at[k],
                    recv_sem=recv_sems.at[k], device_id=peer, device_id_type=pl.DeviceIdType.MESH))
                recvs.append(pltpu.make_async_remote_copy(
                    src_ref=src(w, me), dst_ref=out_refs[w].at[pidx], send_sem=send_sems.at[k],
                    recv_sem=recv_sems.at[k], device_id=peer, device_id_type=pl.DeviceIdType.MESH))
            if p == 0:
                own.start()
                for cp in sends:
                    cp.start()
            else:
                for cp in recvs:
                    cp.wait_recv()
                for cp in sends:
                    cp.wait_send()
                own.wait()


def _exchange_alone(ex, *, name):
    n = ex.n

    def body(*refs):
        for p in range(3):
            ex.phase(p, refs[:n], refs[n:2 * n], refs[2 * n:])

    anyspec = pl.BlockSpec(memory_space=pl.ANY)
    return pl.pallas_call(body, name=name, out_shape=ex.out_shape, in_specs=[anyspec] * n,
                          out_specs=[anyspec] * n, scratch_shapes=ex.scratch)(*ex.operands)


def _adamw(w, parts, m, v, *, name):
    G, R, n = w.shape
    tn = 256 if (n > 256 and n % 256 == 0) else n
    tr = R
    for t in range(16, R, 16):
        if R % t == 0 and t * tn <= 160 * 1024:
            tr = t
    if R * tn <= 160 * 1024:
        tr = R

    def body(w_ref, p_ref, m_ref, v_ref, g_ref, d_ref, nm_ref, nv_ref):
        g = p_ref[0].astype(F32)
        for j in range(1, N_DEV):
            g = g + p_ref[j].astype(F32)
        g_ref[...] = g
        nm = ADAM_B1 * m_ref[...] + (1.0 - ADAM_B1) * g
        nv = ADAM_B2 * v_ref[...] + (1.0 - ADAM_B2) * (g * g)
        nm_ref[...] = nm
        nv_ref[...] = nv
        m_hat = nm / (1.0 - ADAM_B1 ** ADAM_STEP)
        v_hat = nv / (1.0 - ADAM_B2 ** ADAM_STEP)
        d_ref[...] = -ADAM_LR * (m_hat / (jnp.sqrt(v_hat) + ADAM_EPS) + ADAM_WD * w_ref[...])

    blk = pl.BlockSpec((None, tr, tn), lambda g, i, j: (g, i, j))
    return pl.pallas_call(
        body, name=name, grid=(G, R // tr, n // tn),
        in_specs=[blk, pl.BlockSpec((N_DEV, None, tr, tn), lambda g, i, j: (0, g, i, j)), blk, blk],
        out_specs=[blk, blk, blk, blk],
        out_shape=[jax.ShapeDtypeStruct((G, R, n), F32)] * 4,
        compiler_params=_params(("parallel", "parallel", "parallel")),
    )(w, parts, m, v)


def _pad_last(a, width):
    return jnp.pad(a, [(0, 0)] * (a.ndim - 1) + [(0, width - a.shape[-1])])


def _cols_of(g):
    return g.transpose(1, 0, 2).reshape(g.shape[1], N_DEV * g.shape[2])


def _col_shards(w):
    return w.reshape(w.shape[0], N_DEV, w.shape[1] // N_DEV).transpose(1, 0, 2)


def kernel(x, positions, ffn1_pre_w, ffn1_w1, ffn1_w2, ffn1_post_w, mix_pre_w, w_in, mla_q_norm_w, mla_w_uq, mla_kv_norm_w, mla_w_ukv, ret_gn_w, w_branch_mla, w_branch_ret, w_out, mix_post_w, ffn2_pre_w, ffn2_w1, ffn2_w2, ffn2_post_w, loss_target, m_ffn1_pre_w, m_ffn1_w1, m_ffn1_w2, m_ffn1_post_w, m_mix_pre_w, m_w_in, m_mla_q_norm_w, m_mla_w_uq, m_mla_kv_norm_w, m_mla_w_ukv, m_ret_gn_w, m_w_branch_mla, m_w_branch_ret, m_w_out, m_mix_post_w, m_ffn2_pre_w, m_ffn2_w1, m_ffn2_w2, m_ffn2_post_w, v_ffn1_pre_w, v_ffn1_w1, v_ffn1_w2, v_ffn1_post_w, v_mix_pre_w, v_w_in, v_mla_q_norm_w, v_mla_w_uq, v_mla_kv_norm_w, v_mla_w_ukv, v_ret_gn_w, v_w_branch_mla, v_w_branch_ret, v_w_out, v_mix_post_w, v_ffn2_pre_w, v_ffn2_w1, v_ffn2_w2, v_ffn2_post_w):
    T, D = x.shape[1], x.shape[2]
    h0 = x[0]
    tgt = loss_target[0]
    pos = positions.reshape(T, 1).astype(F32)

    big = [("ffn1_w1", ffn1_w1, m_ffn1_w1, v_ffn1_w1), ("ffn1_w2", ffn1_w2, m_ffn1_w2, v_ffn1_w2),
           ("w_in", w_in, m_w_in, v_w_in), ("mla_w_uq", mla_w_uq, m_mla_w_uq, v_mla_w_uq),
           ("mla_w_ukv", mla_w_ukv, m_mla_w_ukv, v_mla_w_ukv),
           ("w_branch_mla", w_branch_mla, m_w_branch_mla, v_w_branch_mla),
           ("w_branch_ret", w_branch_ret, m_w_branch_ret, v_w_branch_ret),
           ("w_out", w_out, m_w_out, v_w_out),
           ("ffn2_w1", ffn2_w1, m_ffn2_w1, v_ffn2_w1), ("ffn2_w2", ffn2_w2, m_ffn2_w2, v_ffn2_w2)]
    small = [("ffn1_pre_w", ffn1_pre_w, m_ffn1_pre_w, v_ffn1_pre_w), ("ffn1_post_w", ffn1_post_w, m_ffn1_post_w, v_ffn1_post_w),
             ("mix_pre_w", mix_pre_w, m_mix_pre_w, v_mix_pre_w), ("mla_q_norm_w", mla_q_norm_w, m_mla_q_norm_w, v_mla_q_norm_w),
             ("mla_kv_norm_w", mla_kv_norm_w, m_mla_kv_norm_w, v_mla_kv_norm_w), ("ret_gn_w", ret_gn_w, m_ret_gn_w, v_ret_gn_w),
             ("mix_post_w", mix_post_w, m_mix_post_w, v_mix_post_w), ("ffn2_pre_w", ffn2_pre_w, m_ffn2_pre_w, v_ffn2_pre_w),
             ("ffn2_post_w", ffn2_post_w, m_ffn2_post_w, v_ffn2_post_w)]

    half = ffn1_w2.shape[1]
    hp = -(-half // LANES) * LANES

    def rows_view(w):
        return w[0].T

    def send_w1(w):
        return jnp.pad(rows_view(w).reshape(2, half, D), ((0, 0), (0, hp - half), (0, 0))).reshape(2 * hp, D).astype(BF16)

    def send_w2(w):
        return jnp.pad(w[0], ((0, hp - half), (0, 0))).astype(BF16)

    mixer = ["w_in", "mla_w_uq", "mla_w_ukv", "w_branch_mla", "w_branch_ret", "w_out"]
    uq_w = MLA_NOPE + MLA_ROPE
    mixer_send = [rows_view(w_in).astype(BF16), jnp.pad(rows_view(mla_w_uq), ((0, HP - uq_w), (0, 0))).astype(BF16),
                  mla_w_ukv[0].astype(BF16), w_branch_mla[0].astype(BF16), w_branch_ret[0].astype(BF16),
                  w_out[0].astype(BF16)]

    w1a, w2a = _exchange_alone(_Gather([send_w1(ffn1_w1), send_w2(ffn1_w2)]), name="gather_ffn1")
    w2a = w2a.reshape(N_DEV // 2, 2 * hp, D)
    u1, f1, h1, *got = _ffn_fwd(h0, ffn1_pre_w, w1a, w2a, ffn1_post_w, None, name="ffn1_fwd_gather_mixer",
                                exchange=_Gather(mixer_send))
    fw = dict(zip(mixer, got))

    wi = fw["w_in"].reshape(-1, D)
    cq_w, ckv_w, kr_w = wi[0:384], wi[384:640], wi[640:672]
    rq_w, rk_w = wi[672:928], wi[928:1184]
    rv_w, rg_w = wi[1184:1696], wi[1696:2208]
    gm_w, gr_w = wi[2208:2208 + D], wi[2208 + D:2208 + 2 * D]
    zer = lambda n: jnp.zeros((n, D), BF16)
    head_rows = lambda a, h: jnp.pad(a.reshape(h, -1, D), ((0, 0), (0, HP - a.shape[0] // h), (0, 0))).reshape(h * HP, D)
    w_in_p = jnp.concatenate([head_rows(rq_w, RET_HEADS), head_rows(rk_w, RET_HEADS), rv_w, rg_w,
                              cq_w, ckv_w, zer(MLA_NOPE), kr_w, zer(HP - MLA_NOPE - MLA_ROPE), zer(AW - 768),
                              gm_w, gr_w], axis=0)
    w_uq_p = fw["mla_w_uq"].reshape(QW, MLA_Q_RANK)
    ukv = fw["mla_w_ukv"].transpose(1, 0, 2)
    w_kv_p = jnp.concatenate([_pad_last(ukv[:, :, :MLA_NOPE], HP).reshape(MLA_KV_RANK, QW),
                              _pad_last(ukv[:, :, MLA_NOPE:], HP).reshape(MLA_KV_RANK, QW)], axis=1)
    w_bm_p = jnp.pad(_cols_of(fw["w_branch_mla"]).reshape(MLA_HEADS, MLA_V, D),
                     ((0, 0), (0, HP - MLA_V), (0, 0))).reshape(QW, D)
    w_br, w_o = _cols_of(fw["w_branch_ret"]), fw["w_out"].reshape(D, D)
    tab_mla = _rope_table(MLA_NOPE, MLA_ROPE // 2)
    tab_ret = _rope_table(0, RET_DK // 2)

    proj, a1 = _rms_matmul(h1, mix_pre_w, w_in_p, name="mixer_in_proj")
    q, k, v, qn, kvn = _mla_prep_fwd(proj, pos, mla_q_norm_w, mla_kv_norm_w, w_uq_p, w_kv_p, tab_mla, name="mla_prep_fwd")
    o, lse, w1b, w2b = _flash_fwd(q, k, v, name="mla_attn_fwd_gather_ffn2",
                                  exchange=_Gather([send_w1(ffn2_w1), send_w2(ffn2_w2)]))
    w2b = w2b.reshape(N_DEV // 2, 2 * hp, D)
    ypre, yn, rprev = _ret_fwd(proj, pos, tab_ret, name="retention_fwd")
    omla, oret, m, h2 = _merge_fwd(o, yn, proj, ret_gn_w, w_bm_p, w_br, w_o, h1, mix_post_w, name="merge_fwd")
    u2, f2, _, dy, lossp = _ffn_fwd(h2, ffn2_pre_w, w1b, w2b, ffn2_post_w, tgt, name="ffn2_fwd_loss")
    loss = lax.psum(jnp.sum(lossp[::8, 0]), ("x", "y", "c"))

    def grad(x, dy, tag, exchange=None):
        return _matmul_tn(x if x.ndim == 3 else x[None], dy if dy.ndim == 3 else dy[None], name=tag, exchange=exchange)

    g2, du2, df2, a2, dh2, gpost2, gpre2 = _ffn_bwd(dy, f2, ffn2_post_w, h2, ffn2_pre_w, u2, w2b, w1b, name="ffn2_bwd")
    dw1b, = grad(du2.reshape(N_DEV, T, 2 * hp), a2, "ffn2_dw1")
    dw2b = grad(g2, df2, "ffn2_dw2")[0].reshape(N_DEV, hp, D)
    (dmb, merged, dgm, dgr, domla, do, delta, doret, gated, drg, dyn, gpostm, ggn) = _merge_bwd(
        dh2, m, mix_post_w, omla, oret, proj, yn, ret_gn_w, o, w_o, w_bm_p, w_br, name="merge_bwd")
    dw_out = grad(merged, dmb, "dw_out")[0][0]
    dw_bm_p = grad(o, domla, "dw_branch_mla")[0][0]
    dw_br = grad(gated, doret, "dw_branch_ret")[0][0]
    dq, dk, dv, *recv_ffn2 = _flash_bwd(q, k, v, do, lse, delta, name="mla_attn_bwd_scatter_ffn2",
                                        exchange=_Scatter([dw1b, dw2b]))
    da, dql, dkvl, gqn, gkvn = _mla_prep_bwd(dq, dk, dv, proj, pos, mla_q_norm_w, mla_kv_norm_w, w_uq_p, w_kv_p, tab_mla, name="mla_prep_bwd")
    dw_uq_p = grad(dql, qn, "dw_uq")[0][0]
    dw_kv_p = grad(kvn, dkvl, "dw_ukv")[0][0]
    drq, drk, drv = _ret_bwd(dyn, ypre, proj, pos, tab_ret, rprev, name="retention_bwd")
    dproj = jnp.concatenate([drq, drk, drv, drg, da, dgm, dgr], axis=1)
    dw_in_p = grad(dproj, a1, "dw_in")[0][0]
    dh1, gmixpre = _proj_bwd(dproj, w_in_p, h1, mix_pre_w, dh2, name="mixer_in_bwd")

    unhead = lambda a, h, wd: a.reshape(h, HP, D)[:, :wd].reshape(h * wd, D)
    c0 = 4 * RW
    dw_in = jnp.concatenate([
        dw_in_p[c0:c0 + 384], dw_in_p[c0 + 384:c0 + 640], dw_in_p[c0 + 640 + MLA_NOPE:c0 + 640 + MLA_NOPE + MLA_ROPE],
        unhead(dw_in_p[0:RW], RET_HEADS, RET_DK), unhead(dw_in_p[RW:2 * RW], RET_HEADS, RET_DK),
        dw_in_p[2 * RW:3 * RW], dw_in_p[3 * RW:4 * RW],
        dw_in_p[PROJ_FIXED:PROJ_FIXED + D], dw_in_p[PROJ_FIXED + D:PROJ_FIXED + 2 * D]], axis=0)
    dw_uq = dw_uq_p.reshape(MLA_HEADS, HP, MLA_Q_RANK)[:, :uq_w]
    dkp = dw_kv_p[:, :QW].reshape(MLA_KV_RANK, MLA_HEADS, HP)[:, :, :MLA_NOPE]
    dvp = dw_kv_p[:, QW:].reshape(MLA_KV_RANK, MLA_HEADS, HP)[:, :, :MLA_V]
    dw_ukv = jnp.concatenate([dkp, dvp], axis=2).transpose(1, 0, 2)
    dw_bm = dw_bm_p.reshape(MLA_HEADS, HP, D)[:, :MLA_V].reshape(MLA_HEADS * MLA_V, D)

    mixer_grads = [dw_in.reshape(N_DEV, -1, D), dw_uq, dw_ukv, _col_shards(dw_bm), _col_shards(dw_br),
                   dw_out.reshape(N_DEV, D // N_DEV, D)]
    g1, du1, df1, a0, dx, gpost1, gpre1, *recv_mixer = _ffn_bwd(
        dh1, f1, ffn1_post_w, h0, ffn1_pre_w, u1, w2a, w1a, name="ffn1_bwd_scatter_mixer", exchange=_Scatter(mixer_grads))
    dw2a = grad(g1, df1, "ffn1_dw2")[0].reshape(N_DEV, hp, D)
    dw1a, recv_w2a = grad(du1.reshape(N_DEV, T, 2 * hp), a0, "ffn1_dw1_scatter_dw2", exchange=_Scatter([dw2a]))

    small_g = {"ffn1_pre_w": gpre1, "ffn1_post_w": gpost1, "mix_pre_w": gmixpre, "mla_q_norm_w": gqn,
               "mla_kv_norm_w": gkvn, "ret_gn_w": ggn, "mix_post_w": gpostm, "ffn2_pre_w": gpre2, "ffn2_post_w": gpost2}

    def small_flat(arrs):
        a = jnp.concatenate([z.reshape(-1) for z in arrs])
        a = jnp.pad(a, (0, (-a.size) % (8 * LANES)))
        return a.reshape(-1, LANES)

    sv = small_flat([small_g[nm] for nm, *_ in small])
    recv_w1a, sall = _exchange_alone(_Scatter([dw1a], whole=[sv]), name="scatter_ffn1_dw1")
    parts = dict(zip(mixer, recv_mixer))
    parts.update(ffn1_w1=recv_w1a, ffn1_w2=recv_w2a, ffn2_w1=recv_ffn2[0], ffn2_w2=recv_ffn2[1])
    as_is = (lambda a: a, lambda p: p[:, None], lambda a: a)
    views = {nm: as_is for nm, *_ in big}
    for nm in ("ffn1_w1", "ffn2_w1"):
        views[nm] = (lambda a: rows_view(a).reshape(2, half, D), lambda p: p.reshape(N_DEV, 2, hp, D),
                     lambda a: a.reshape(2 * half, D).T[None])
    for nm in ("w_in", "mla_w_uq"):
        views[nm] = (lambda a: rows_view(a)[None], lambda p: p[:, None], lambda a: a[0].T[None])
    big_out = {}
    for nm, w, m_, v_ in big:
        to_view, parts_view, back = views[nm]
        big_out[nm] = [back(a) for a in _adamw(to_view(w), parts_view(parts[nm]), to_view(m_), to_view(v_),
                                               name="adamw_" + nm)]
    flat3 = lambda arrs: small_flat(arrs)[None]
    gs, ds, nms, nvs = _adamw(flat3([w for _, w, _, _ in small]), sall[:, None],
                              flat3([a for _, _, a, _ in small]), flat3([a for _, _, _, a in small]),
                              name="adamw_replicated")

    def split_small(flat):
        out, o_ = {}, 0
        fl = flat.reshape(-1)
        for nm, w, _, _ in small:
            out[nm] = fl[o_:o_ + w.size].reshape(w.shape)
            o_ += w.size
        return out

    order = ["ffn1_pre_w", "ffn1_w1", "ffn1_w2", "ffn1_post_w", "mix_pre_w", "w_in", "mla_q_norm_w", "mla_w_uq",
             "mla_kv_norm_w", "mla_w_ukv", "ret_gn_w", "w_branch_mla", "w_branch_ret", "w_out", "mix_post_w",
             "ffn2_pre_w", "ffn2_w1", "ffn2_w2", "ffn2_post_w"]
    outs = [loss, dx[None]]
    for i, fs in enumerate((gs, ds, nms, nvs)):
        both = {**{nm: big_out[nm][i] for nm in big_out}, **split_small(fs)}
        outs += [both[nm] for nm in order]
    return tuple(outs)
```

```python
import math

import numpy as np
import jax
import jax.numpy as jnp
from jax import lax
from jax.experimental import pallas as pl
from jax.experimental.pallas import tpu as pltpu

F32, BF16 = jnp.float32, jnp.bfloat16

MLA_HEADS, MLA_NOPE, MLA_ROPE, MLA_V = 8, 64, 32, 64
MLA_Q_RANK, MLA_KV_RANK = 384, 256
RET_HEADS, RET_DK, RET_DV = 4, 64, 128
ROPE_BASE, NORM_EPS, GN_EPS = 10000.0, 1e-6, 1e-6
ADAM_LR, ADAM_B1, ADAM_B2, ADAM_EPS, ADAM_WD, ADAM_STEP = 0.001, 0.9, 0.999, 1e-08, 0.01, 10
ATTN_SCALE = 1.0 / math.sqrt(MLA_NOPE + MLA_ROPE)

N_DEV = 8
LANES = 128
HP = LANES
QW = MLA_HEADS * HP
RW = RET_HEADS * HP
AW = 1024
PROJ_FIXED = 4 * RW + AW
NEG = -1e30

TOKEN_TILE = 512
ATTN_TILE = 1024
ATTN_CHAINS = 2
FFN_CHAINS = 2
RET_TILE = 256
PROJ_TILE_CAP = 2560
GRAD_TILE_CAP = 1408
GRAD_TOKEN_TILE = 2048
MERGE_TILE = 256
VMEM_LIMIT = 56 * 1024 * 1024


def _tile(n, cap, mult=LANES):
    if n <= cap:
        return n
    best = None
    for t in range(mult, cap + 1, mult):
        if n % t == 0:
            best = t
    assert best is not None, (n, cap, mult)
    return best


def _params(sem):
    return pltpu.CompilerParams(dimension_semantics=sem, vmem_limit_bytes=VMEM_LIMIT)


def _dot(a, b):
    return lax.dot_general(a, b, (((1,), (0,)), ((), ())), preferred_element_type=F32)


def _dot_nt(a, b):
    return lax.dot_general(a, b, (((1,), (1,)), ((), ())), preferred_element_type=F32)


def _dot_tn(a, b):
    return lax.dot_general(a, b, (((0,), (0,)), ((), ())), preferred_element_type=F32)


def _sigmoid(x):
    return pl.reciprocal(1.0 + jnp.exp(-x), approx=True)


def _rms_fwd(x, w):
    r = lax.rsqrt(jnp.mean(x * x, axis=-1, keepdims=True) + NORM_EPS)
    return x * r * w


def _rms_bwd(x, w, dy):
    r = lax.rsqrt(jnp.mean(x * x, axis=-1, keepdims=True) + NORM_EPS)
    xh = x * r
    g = dy * w
    dx = r * (g - xh * jnp.mean(g * xh, axis=-1, keepdims=True))
    return dx, jnp.sum(dy * xh, axis=0, keepdims=True)


def _rope_table(first, half):
    inv = (np.float32(ROPE_BASE) ** (-(np.arange(half, dtype=np.float32) / np.float32(half)))).astype(np.float32)
    tab = np.zeros((8, LANES), np.float32)
    tab[0, first:first + half] = inv
    tab[0, first + half:first + 2 * half] = inv
    tab[1, first:first + half] = -1.0
    tab[2, first + half:first + 2 * half] = 1.0
    return jnp.asarray(tab)


def _rope_cs(pos, tab_ref):
    ang = pos * tab_ref[0:1, :]
    s = jnp.sin(ang)
    return jnp.cos(ang), s * tab_ref[1:2, :], s * tab_ref[2:3, :]


def _rope(x, cs, half, inverse=False):
    c, s1, s2 = cs
    a = pltpu.roll(x, LANES - half, 1) * s1 + pltpu.roll(x, half, 1) * s2
    return x * c - a if inverse else x * c + a


def _call(body, *, name, grid, in_specs, out_specs, out_shape, scratch_shapes, args, exchange=None):
    sem = ("arbitrary",) * len(grid)
    if exchange is None:
        return pl.pallas_call(body, name=name, grid=grid, in_specs=in_specs, out_specs=out_specs,
                              out_shape=out_shape, scratch_shapes=scratch_shapes, compiler_params=_params(sem))(*args)
    n_in, n_out, e = len(in_specs), len(out_specs), exchange.n
    total = math.prod(grid)

    def carried(*refs):
        own = refs[:n_in] + refs[n_in + e:n_in + e + n_out] + refs[n_in + 2 * e + n_out:len(refs) - 3]
        ex_refs = (refs[n_in:n_in + e], refs[n_in + e + n_out:n_in + 2 * e + n_out], refs[len(refs) - 3:])
        step = pl.program_id(0)
        for d in range(1, len(grid)):
            step = step * grid[d] + pl.program_id(d)

        @pl.when(step == 0)
        def _():
            exchange.phase(0, *ex_refs)

        @pl.when(step == total // 2)
        def _():
            exchange.phase(1, *ex_refs)

        body(*own)

        @pl.when(step == total - 1)
        def _():
            exchange.phase(2, *ex_refs)

    anyspec = pl.BlockSpec(memory_space=pl.ANY)
    return pl.pallas_call(
        carried, name=name, grid=grid, in_specs=list(in_specs) + [anyspec] * e,
        out_specs=list(out_specs) + [anyspec] * e, out_shape=list(out_shape) + exchange.out_shape,
        scratch_shapes=list(scratch_shapes) + exchange.scratch, compiler_params=_params(sem),
    )(*args, *exchange.operands)


def _ffn_fwd(h, pre_w, w1, w2, post_w, target, *, name, exchange=None):
    T, D = h.shape
    nk, ck = w2.shape[0], w2.shape[1]
    tT = min(TOKEN_TILE, T)
    nT = T // tT
    with_loss = target is not None

    def body(*refs):
        if with_loss:
            (h_ref, pre_ref, w1g_ref, w1u_ref, w2_ref, post_ref, tgt_ref,
             u_ref, f_ref, ho_ref, dy_ref, loss_ref, a_s, acc) = refs
        else:
            (h_ref, pre_ref, w1g_ref, w1u_ref, w2_ref, post_ref,
             u_ref, f_ref, ho_ref, a_s, acc) = refs
        k = pl.program_id(1)

        @pl.when(k == 0)
        def _():
            a_s[...] = _rms_fwd(h_ref[...], pre_ref[...]).astype(BF16)
            acc[...] = jnp.zeros_like(acc)

        for c in range(FFN_CHAINS):
            rs = slice(c * (tT // FFN_CHAINS), (c + 1) * (tT // FFN_CHAINS))
            a = a_s[rs, :]
            ug = _dot_nt(a, w1g_ref[...])
            uu = _dot_nt(a, w1u_ref[...])
            u_ref[0, rs, :] = ug.astype(BF16)
            u_ref[1, rs, :] = uu.astype(BF16)
            acc[rs, :] += _dot((ug * _sigmoid(ug) * uu).astype(BF16), w2_ref[...])

        @pl.when(k == nk - 1)
        def _():
            f = acc[...]
            f_ref[...] = f
            ho = h_ref[...] + 0.5 * _rms_fwd(f, post_ref[...])
            ho_ref[...] = ho
            if with_loss:
                e = ho - tgt_ref[...]
                dy_ref[...] = e * (1.0 / D)
                loss_ref[...] = jnp.full(loss_ref.shape, (0.5 / D) * jnp.sum(e * e), F32)

    row = pl.BlockSpec((tT, D), lambda i, k: (i, 0))
    vec = pl.BlockSpec((1, D), lambda i, k: (0, 0))
    in_specs = [row, vec,
                pl.BlockSpec((None, ck, D), lambda i, k: (k, 0, 0)),
                pl.BlockSpec((None, ck, D), lambda i, k: (nk + k, 0, 0)),
                pl.BlockSpec((None, ck, D), lambda i, k: (k, 0, 0)),
                vec]
    out_shape = [jax.ShapeDtypeStruct((2, nk, T, ck), BF16),
                 jax.ShapeDtypeStruct((T, D), F32),
                 jax.ShapeDtypeStruct((T, D), F32)]
    out_specs = [pl.BlockSpec((2, None, tT, ck), lambda i, k: (0, k, i, 0)), row, row]
    args = [h, pre_w, w1, w1, w2, post_w]
    if with_loss:
        in_specs.append(row)
        args.append(target)
        out_shape += [jax.ShapeDtypeStruct((T, D), F32), jax.ShapeDtypeStruct((nT * 8, LANES), F32)]
        out_specs += [row, pl.BlockSpec((8, LANES), lambda i, k: (i, 0))]
    return _call(body, name=name, grid=(nT, nk), in_specs=in_specs, out_specs=out_specs, out_shape=out_shape,
                 scratch_shapes=[pltpu.VMEM((tT, D), BF16), pltpu.VMEM((tT, D), F32)], args=args, exchange=exchange)


def _ffn_bwd(dho, f, post_w, h, pre_w, u, w2, w1, *, name, exchange=None):
    T, D = h.shape
    nk, ck = w2.shape[0], w2.shape[1]
    tT = min(TOKEN_TILE, T)
    nT = T // tT

    def body(dho_ref, f_ref, post_ref, h_ref, pre_ref, u_ref, w2_ref, w1g_ref, w1u_ref,
             g_ref, du_ref, df_ref, a_ref, dh_ref, gpost_ref, gpre_ref, df_s, da_acc):
        i, k = pl.program_id(0), pl.program_id(1)

        @pl.when(jnp.logical_and(i == 0, k == 0))
        def _():
            gpost_ref[...] = jnp.zeros_like(gpost_ref)
            gpre_ref[...] = jnp.zeros_like(gpre_ref)

        @pl.when(k == 0)
        def _():
            dx, dw = _rms_bwd(f_ref[...], post_ref[...], 0.5 * dho_ref[...])
            dfb = dx.astype(BF16)
            df_s[...] = dfb
            df_ref[...] = dfb
            gpost_ref[...] += dw
            a_ref[...] = _rms_fwd(h_ref[...], pre_ref[...]).astype(BF16)
            da_acc[...] = jnp.zeros_like(da_acc)

        for c in range(FFN_CHAINS):
            rs = slice(c * (tT // FFN_CHAINS), (c + 1) * (tT // FFN_CHAINS))
            dg = _dot_nt(df_s[rs, :], w2_ref[...])
            ug = u_ref[0, rs, :].astype(F32)
            uu = u_ref[1, rs, :].astype(F32)
            sg = _sigmoid(ug)
            sl = ug * sg
            g_ref[rs, :] = (sl * uu).astype(BF16)
            dug = (dg * uu * (sg + sl * (1.0 - sg))).astype(BF16)
            duu = (dg * sl).astype(BF16)
            du_ref[0, rs, :] = dug
            du_ref[1, rs, :] = duu
            da_acc[rs, :] += _dot(dug, w1g_ref[...]) + _dot(duu, w1u_ref[...])

        @pl.when(k == nk - 1)
        def _():
            dx, dw = _rms_bwd(h_ref[...], pre_ref[...], da_acc[...])
            dh_ref[...] = dho_ref[...] + dx
            gpre_ref[...] += dw

    row = pl.BlockSpec((tT, D), lambda i, k: (i, 0))
    vec = pl.BlockSpec((1, D), lambda i, k: (0, 0))
    return _call(
        body, name=name, grid=(nT, nk),
        in_specs=[row, row, vec, row, vec,
                  pl.BlockSpec((2, None, tT, ck), lambda i, k: (0, k, i, 0)),
                  pl.BlockSpec((None, ck, D), lambda i, k: (k, 0, 0)),
                  pl.BlockSpec((None, ck, D), lambda i, k: (k, 0, 0)),
                  pl.BlockSpec((None, ck, D), lambda i, k: (nk + k, 0, 0))],
        out_specs=[pl.BlockSpec((None, tT, ck), lambda i, k: (k, i, 0)),
                   pl.BlockSpec((2, None, tT, ck), lambda i, k: (0, k, i, 0)),
                   row, row, row, vec, vec],
        out_shape=[jax.ShapeDtypeStruct((nk, T, ck), BF16),
                   jax.ShapeDtypeStruct((2, nk, T, ck), BF16),
                   jax.ShapeDtypeStruct((T, D), BF16),
                   jax.ShapeDtypeStruct((T, D), BF16),
                   jax.ShapeDtypeStruct((T, D), F32),
                   jax.ShapeDtypeStruct((1, D), F32),
                   jax.ShapeDtypeStruct((1, D), F32)],
        scratch_shapes=[pltpu.VMEM((tT, D), BF16), pltpu.VMEM((tT, D), F32)],
        args=(dho, f, post_w, h, pre_w, u, w2, w1, w1), exchange=exchange)


def _matmul_tn(x, dy, *, name, exchange=None):
    Px, T, K = x.shape
    Py, _, N = dy.shape
    P = max(Px, Py)
    tT, tK, tN = min(GRAD_TOKEN_TILE, T), _tile(K, GRAD_TILE_CAP), _tile(N, GRAD_TILE_CAP)
    nt = T // tT

    def body(x_ref, dy_ref, o_ref, acc):
        t = pl.program_id(3)

        @pl.when(t == 0)
        def _():
            acc[...] = jnp.zeros_like(acc)

        acc[...] += _dot_tn(x_ref[...], dy_ref[...])

        @pl.when(t == nt - 1)
        def _():
            o_ref[...] = acc[...].astype(BF16)

    return _call(
        body, name=name, grid=(P, K // tK, N // tN, nt),
        in_specs=[pl.BlockSpec((None, tT, tK), lambda p, a, b, t: (p if Px > 1 else 0, t, a)),
                  pl.BlockSpec((None, tT, tN), lambda p, a, b, t: (p if Py > 1 else 0, t, b))],
        out_specs=[pl.BlockSpec((None, tK, tN), lambda p, a, b, t: (p, a, b))],
        out_shape=[jax.ShapeDtypeStruct((P, K, N), BF16)],
        scratch_shapes=[pltpu.VMEM((tK, tN), F32)], args=(x, dy), exchange=exchange)


def _rms_matmul(h, wn, w, *, name):
    T, D = h.shape
    N = w.shape[0]
    tT, tN = min(TOKEN_TILE, T), _tile(N, PROJ_TILE_CAP)

    def body(h_ref, wn_ref, w_ref, y_ref, a_ref):
        @pl.when(pl.program_id(1) == 0)
        def _():
            a_ref[...] = _rms_fwd(h_ref[...], wn_ref[...]).astype(BF16)

        y_ref[...] = _dot_nt(a_ref[...], w_ref[...]).astype(BF16)

    return pl.pallas_call(
        body, name=name, grid=(T // tT, N // tN),
        in_specs=[pl.BlockSpec((tT, D), lambda i, j: (i, 0)),
                  pl.BlockSpec((1, D), lambda i, j: (0, 0)),
                  pl.BlockSpec((tN, D), lambda i, j: (j, 0))],
        out_specs=[pl.BlockSpec((tT, tN), lambda i, j: (i, j)),
                   pl.BlockSpec((tT, D), lambda i, j: (i, 0))],
        out_shape=[jax.ShapeDtypeStruct((T, N), BF16), jax.ShapeDtypeStruct((T, D), BF16)],
        compiler_params=_params(("parallel", "arbitrary")),
    )(h, wn, w)


def _proj_bwd(dproj, w, h, wn, dres, *, name):
    T, D = h.shape
    N = w.shape[0]
    tT, tN = min(TOKEN_TILE, T), _tile(N, PROJ_TILE_CAP)
    nn = N // tN

    def body(dp_ref, w_ref, h_ref, wn_ref, dres_ref, dh_ref, gw_ref, acc):
        i, j = pl.program_id(0), pl.program_id(1)

        @pl.when(jnp.logical_and(i == 0, j == 0))
        def _():
            gw_ref[...] = jnp.zeros_like(gw_ref)

        @pl.when(j == 0)
        def _():
            acc[...] = jnp.zeros_like(acc)

        acc[...] += _dot(dp_ref[...], w_ref[...])

        @pl.when(j == nn - 1)
        def _():
            dx, dw = _rms_bwd(h_ref[...], wn_ref[...], acc[...])
            dh_ref[...] = dres_ref[...] + dx
            gw_ref[...] += dw

    row = pl.BlockSpec((tT, D), lambda i, j: (i, 0))
    vec = pl.BlockSpec((1, D), lambda i, j: (0, 0))
    return pl.pallas_call(
        body, name=name, grid=(T // tT, nn),
        in_specs=[pl.BlockSpec((tT, tN), lambda i, j: (i, j)),
                  pl.BlockSpec((tN, D), lambda i, j: (j, 0)), row, vec, row],
        out_specs=[row, vec],
        out_shape=[jax.ShapeDtypeStruct((T, D), F32), jax.ShapeDtypeStruct((1, D), F32)],
        scratch_shapes=[pltpu.VMEM((tT, D), F32)],
        compiler_params=_params(("arbitrary", "arbitrary")),
    )(dproj, w, h, wn, dres)


def _mla_prep_fwd(proj, pos, qn_w, kvn_w, w_uq, w_kv, tab, *, name):
    T = proj.shape[0]
    tT = min(TOKEN_TILE, T)
    a_blk = PROJ_FIXED // AW - 1

    def body(a_ref, pos_ref, qnw_ref, kvnw_ref, wuq_ref, wkv_ref, tab_ref,
             q_ref, k_ref, v_ref, qn_ref, kvn_ref):
        cq = a_ref[:, 0:MLA_Q_RANK].astype(F32)
        ckv = a_ref[:, MLA_Q_RANK:MLA_Q_RANK + MLA_KV_RANK].astype(F32)
        kr = a_ref[:, 640:768].astype(F32)
        qn = _rms_fwd(cq, qnw_ref[...]).astype(BF16)
        kvn = _rms_fwd(ckv, kvnw_ref[...]).astype(BF16)
        qn_ref[...] = qn
        kvn_ref[...] = kvn
        cs = _rope_cs(pos_ref[...], tab_ref)
        q = _dot_nt(qn, wuq_ref[...])
        kv = _dot(kvn, wkv_ref[...])
        krr = _rope(kr, cs, MLA_ROPE // 2)
        for hd in range(MLA_HEADS):
            sl = slice(hd * HP, (hd + 1) * HP)
            q_ref[:, sl] = (_rope(q[:, sl], cs, MLA_ROPE // 2) * ATTN_SCALE).astype(BF16)
            k_ref[:, sl] = (kv[:, sl] + krr).astype(BF16)
        v_ref[...] = kv[:, QW:].astype(BF16)

    def full(r, c):
        return pl.BlockSpec((r, c), lambda i: (0, 0))

    def rows(c):
        return pl.BlockSpec((tT, c), lambda i: (i, 0))

    return pl.pallas_call(
        body, name=name, grid=(T // tT,),
        in_specs=[pl.BlockSpec((tT, AW), lambda i: (i, a_blk)), rows(1),
                  full(1, MLA_Q_RANK), full(1, MLA_KV_RANK),
                  full(QW, MLA_Q_RANK), full(MLA_KV_RANK, 2 * QW), full(8, LANES)],
        out_specs=[rows(QW), rows(QW), rows(QW), rows(MLA_Q_RANK), rows(MLA_KV_RANK)],
        out_shape=[jax.ShapeDtypeStruct((T, QW), BF16)] * 3
        + [jax.ShapeDtypeStruct((T, MLA_Q_RANK), BF16), jax.ShapeDtypeStruct((T, MLA_KV_RANK), BF16)],
        compiler_params=_params(("parallel",)),
    )(proj, pos, qn_w, kvn_w, w_uq, w_kv, tab)


def _mla_prep_bwd(dq, dk, dv, proj, pos, qn_w, kvn_w, w_uq, w_kv, tab, *, name):
    T = proj.shape[0]
    tT = min(TOKEN_TILE, T)
    a_blk = PROJ_FIXED // AW - 1

    def body(dq_ref, dk_ref, dv_ref, a_ref, pos_ref, qnw_ref, kvnw_ref, wuq_ref, wkv_ref, tab_ref,
             da_ref, dql_ref, dkvl_ref, gqn_ref, gkvn_ref):
        @pl.when(pl.program_id(0) == 0)
        def _():
            gqn_ref[...] = jnp.zeros_like(gqn_ref)
            gkvn_ref[...] = jnp.zeros_like(gkvn_ref)

        cs = _rope_cs(pos_ref[...], tab_ref)
        dkr = jnp.zeros((tT, HP), F32)
        for hd in range(MLA_HEADS):
            sl = slice(hd * HP, (hd + 1) * HP)
            dql_ref[:, sl] = (_rope(dq_ref[:, sl], cs, MLA_ROPE // 2, inverse=True) * ATTN_SCALE).astype(BF16)
            dkh = dk_ref[:, sl]
            dkr = dkr + dkh
            dkvl_ref[:, sl] = dkh.astype(BF16)
        dkvl_ref[:, QW:] = dv_ref[...]
        dqn = _dot(dql_ref[...], wuq_ref[...])
        dkvn = _dot_nt(dkvl_ref[...], wkv_ref[...])
        cq = a_ref[:, 0:MLA_Q_RANK].astype(F32)
        ckv = a_ref[:, MLA_Q_RANK:MLA_Q_RANK + MLA_KV_RANK].astype(F32)
        dcq, gq = _rms_bwd(cq, qnw_ref[...], dqn)
        dckv, gkv = _rms_bwd(ckv, kvnw_ref[...], dkvn)
        gqn_ref[...] += gq
        gkvn_ref[...] += gkv
        da_ref[:, 0:MLA_Q_RANK] = dcq.astype(BF16)
        da_ref[:, MLA_Q_RANK:MLA_Q_RANK + MLA_KV_RANK] = dckv.astype(BF16)
        da_ref[:, 640:768] = _rope(dkr, cs, MLA_ROPE // 2, inverse=True).astype(BF16)
        da_ref[:, 768:AW] = jnp.zeros((tT, AW - 768), BF16)

    def full(r, c):
        return pl.BlockSpec((r, c), lambda i: (0, 0))

    def rows(c):
        return pl.BlockSpec((tT, c), lambda i: (i, 0))

    return pl.pallas_call(
        body, name=name, grid=(T // tT,),
        in_specs=[rows(QW), rows(QW), rows(QW), pl.BlockSpec((tT, AW), lambda i: (i, a_blk)), rows(1),
                  full(1, MLA_Q_RANK), full(1, MLA_KV_RANK),
                  full(QW, MLA_Q_RANK), full(MLA_KV_RANK, 2 * QW), full(8, LANES)],
        out_specs=[rows(AW), rows(QW), rows(2 * QW), full(1, MLA_Q_RANK), full(1, MLA_KV_RANK)],
        out_shape=[jax.ShapeDtypeStruct((T, AW), BF16), jax.ShapeDtypeStruct((T, QW), BF16),
                   jax.ShapeDtypeStruct((T, 2 * QW), BF16),
                   jax.ShapeDtypeStruct((1, MLA_Q_RANK), F32), jax.ShapeDtypeStruct((1, MLA_KV_RANK), F32)],
        compiler_params=_params(("arbitrary",)),
    )(dq, dk, dv, proj, pos, qn_w, kvn_w, w_uq, w_kv, tab)


def _flash_fwd(q, k, v, *, name, exchange=None):
    T = q.shape[0]
    H = q.shape[1] // HP
    tq = min(ATTN_TILE, T)
    nq = T // tq

    sub = tq // ATTN_CHAINS

    def body(q_ref, k_ref, v_ref, o_ref, lse_ref):
        qi = pl.program_id(1)
        qs = [q_ref[c * sub:(c + 1) * sub, :] for c in range(ATTN_CHAINS)]

        def update(carry, off, masked):
            kb = k_ref[pl.ds(off, tq), :]
            vb = v_ref[pl.ds(off, tq), :]
            out = []
            for c in range(ATTN_CHAINS):
                m_prev, l_prev, acc = carry[c]
                s = _dot_nt(qs[c], kb)
                if masked:
                    rows = lax.broadcasted_iota(jnp.int32, (sub, tq), 0) + c * sub
                    s = jnp.where(rows >= lax.broadcasted_iota(jnp.int32, (sub, tq), 1), s, NEG)
                m_new = jnp.maximum(m_prev, jnp.max(s, axis=1, keepdims=True))
                alpha = jnp.exp(m_prev - m_new)
                p = jnp.exp(s - m_new)
                out.append((m_new, alpha * l_prev + jnp.sum(p, axis=1, keepdims=True),
                            alpha * acc + _dot(p.astype(BF16), vb)))
            return tuple(out)

        init = tuple((jnp.full((sub, 1), NEG, F32), jnp.zeros((sub, 1), F32), jnp.zeros((sub, HP), F32))
                     for _ in range(ATTN_CHAINS))
        carry = lax.fori_loop(0, qi, lambda j, cr: update(cr, pl.multiple_of(j * tq, tq), False), init)
        carry = update(carry, pl.multiple_of(qi * tq, tq), True)
        for c in range(ATTN_CHAINS):
            m_fin, l_fin, acc = carry[c]
            o_ref[c * sub:(c + 1) * sub, :] = (acc / l_fin).astype(BF16)
            lse_ref[c * sub:(c + 1) * sub, :] = jnp.broadcast_to(m_fin + jnp.log(l_fin), (sub, HP))

    qspec = pl.BlockSpec((tq, HP), lambda h, i: (i, h))
    kspec = pl.BlockSpec((T, HP), lambda h, i: (0, h))
    return _call(
        body, name=name, grid=(H, nq),
        in_specs=[qspec, kspec, kspec], out_specs=[qspec, qspec],
        out_shape=[jax.ShapeDtypeStruct((T, H * HP), BF16), jax.ShapeDtypeStruct((T, H * HP), F32)],
        scratch_shapes=[], args=(q, k, v), exchange=exchange)


def _flash_bwd(q, k, v, do, lse, delta, *, name, exchange=None):
    T = q.shape[0]
    H = q.shape[1] // HP
    tq = min(ATTN_TILE, T)
    nq = T // tq
    sub = tq // ATTN_CHAINS

    def body(k_ref, v_ref, q_ref, do_ref, lse_ref, dl_ref, dq_ref, dk_ref, dv_ref):
        ki = pl.program_id(1)

        @pl.when(ki == 0)
        def _():
            dq_ref[...] = jnp.zeros_like(dq_ref)

        kb = k_ref[...]
        vb = v_ref[...]

        def step(carry, j, masked):
            dk_acc, dv_acc = carry
            for c in range(ATTN_CHAINS):
                rows = pl.ds(pl.multiple_of(j * tq + c * sub, sub), sub)
                qb = q_ref[rows, :]
                dob = do_ref[rows, :]
                s = _dot_nt(qb, kb)
                if masked:
                    ri = lax.broadcasted_iota(jnp.int32, (sub, tq), 0) + c * sub
                    s = jnp.where(ri >= lax.broadcasted_iota(jnp.int32, (sub, tq), 1), s, NEG)
                p = jnp.exp(s - lse_ref[rows, 0:1])
                dv_acc = dv_acc + _dot_tn(p.astype(BF16), dob)
                dp = _dot_nt(dob, vb)
                ds = (p * (dp - dl_ref[rows, 0:1])).astype(BF16)
                dk_acc = dk_acc + _dot_tn(ds, qb)
                dq_ref[rows, :] += _dot(ds, kb)
            return dk_acc, dv_acc

        carry = step((jnp.zeros((tq, HP), F32), jnp.zeros((tq, HP), F32)), ki, True)
        dk_acc, dv_acc = lax.fori_loop(ki + 1, nq, lambda j, cr: step(cr, j, False), carry)
        dk_ref[...] = dk_acc
        dv_ref[...] = dv_acc.astype(BF16)

    kspec = pl.BlockSpec((tq, HP), lambda h, j: (j, h))
    full = pl.BlockSpec((T, HP), lambda h, j: (0, h))
    return _call(
        body, name=name, grid=(H, nq),
        in_specs=[kspec, kspec, full, full, full, full], out_specs=[full, kspec, kspec],
        out_shape=[jax.ShapeDtypeStruct((T, H * HP), F32), jax.ShapeDtypeStruct((T, H * HP), F32),
                   jax.ShapeDtypeStruct((T, H * HP), BF16)],
        scratch_shapes=[], args=(k, v, q, do, lse, delta), exchange=exchange)


def _ret_consts(cc, hd):
    lg = math.log(1.0 - 2.0 ** (-5.0 - hd))
    diff = (lax.broadcasted_iota(jnp.int32, (cc, cc), 0) - lax.broadcasted_iota(jnp.int32, (cc, cc), 1)).astype(F32)
    decay = jnp.where(diff >= 0, jnp.exp(jnp.maximum(diff, 0.0) * lg), 0.0)
    idx = lax.broadcasted_iota(jnp.int32, (cc, 1), 0).astype(F32)
    zeta = jnp.exp((cc - 1.0 - idx) * lg)
    xi = jnp.exp((idx + 1.0) * lg)
    return decay, zeta, xi, math.exp(cc * lg)


def _ret_fwd(proj, pos, tab, *, name):
    T = proj.shape[0]
    cc = min(RET_TILE, T)
    n = T // cc

    def body(rq_ref, rk_ref, rv_ref, pos_ref, tab_ref, y_ref, yn_ref, rprev_ref, r_s):
        @pl.when(pl.program_id(0) == 0)
        def _():
            r_s[...] = jnp.zeros_like(r_s)

        cs = _rope_cs(pos_ref[...], tab_ref)
        for hd in range(RET_HEADS):
            sl = slice(hd * HP, (hd + 1) * HP)
            decay, zeta, xi, gc = _ret_consts(cc, hd)
            q = _rope(rq_ref[:, sl].astype(F32), cs, RET_DK // 2).astype(BF16)
            kf = _rope(rk_ref[:, sl].astype(F32), cs, RET_DK // 2) * (RET_DK ** -0.5)
            k = kf.astype(BF16)
            v = rv_ref[:, sl]
            r = r_s[hd]
            rprev_ref[0, hd] = r
            inner = (_dot_nt(q, k) * decay).astype(BF16)
            y = _dot(inner, v) + _dot(q, r.astype(BF16)) * xi
            r_s[hd] = r * gc + _dot_tn((kf * zeta).astype(BF16), v)
            y_ref[:, sl] = y
            mu = jnp.mean(y, axis=-1, keepdims=True)
            yc = y - mu
            var = jnp.mean(yc * yc, axis=-1, keepdims=True)
            yn_ref[:, sl] = (yc * lax.rsqrt(var + GN_EPS)).astype(BF16)

    def blk(j):
        return pl.BlockSpec((cc, RW), lambda i: (i, j))

    return pl.pallas_call(
        body, name=name, grid=(n,),
        in_specs=[blk(0), blk(1), blk(2), pl.BlockSpec((cc, 1), lambda i: (i, 0)),
                  pl.BlockSpec((8, LANES), lambda i: (0, 0))],
        out_specs=[blk(0), blk(0), pl.BlockSpec((1, RET_HEADS, HP, RET_DV), lambda i: (i, 0, 0, 0))],
        out_shape=[jax.ShapeDtypeStruct((T, RW), F32), jax.ShapeDtypeStruct((T, RW), BF16),
                   jax.ShapeDtypeStruct((n, RET_HEADS, HP, RET_DV), F32)],
        scratch_shapes=[pltpu.VMEM((RET_HEADS, HP, RET_DV), F32)],
        compiler_params=_params(("arbitrary",)),
    )(proj, proj, proj, pos, tab)


def _ret_bwd(dyn, y, proj, pos, tab, rprev, *, name):
    T = proj.shape[0]
    cc = min(RET_TILE, T)
    n = T // cc

    def body(dyn_ref, y_ref, rq_ref, rk_ref, rv_ref, pos_ref, tab_ref, rprev_ref,
             drq_ref, drk_ref, drv_ref, dr_s):
        @pl.when(pl.program_id(0) == 0)
        def _():
            dr_s[...] = jnp.zeros_like(dr_s)

        cs = _rope_cs(pos_ref[...], tab_ref)
        for hd in range(RET_HEADS):
            sl = slice(hd * HP, (hd + 1) * HP)
            decay, zeta, xi, gc = _ret_consts(cc, hd)
            q = _rope(rq_ref[:, sl].astype(F32), cs, RET_DK // 2).astype(BF16)
            kf = _rope(rk_ref[:, sl].astype(F32), cs, RET_DK // 2) * (RET_DK ** -0.5)
            k = kf.astype(BF16)
            v = rv_ref[:, sl]
            yv = y_ref[:, sl]
            mu = jnp.mean(yv, axis=-1, keepdims=True)
            yc = yv - mu
            rs = lax.rsqrt(jnp.mean(yc * yc, axis=-1, keepdims=True) + GN_EPS)
            yn = yc * rs
            dn = dyn_ref[:, sl]
            dy = rs * (dn - jnp.mean(dn, axis=-1, keepdims=True) - yn * jnp.mean(dn * yn, axis=-1, keepdims=True))
            dyb = dy.astype(BF16)
            dyx = (dy * xi).astype(BF16)
            dr = dr_s[hd]
            drb = dr.astype(BF16)
            inner = (_dot_nt(q, k) * decay).astype(BF16)
            da = (_dot_nt(dyb, v) * decay).astype(BF16)
            dv = _dot_tn(inner, dyb) + _dot((kf * zeta).astype(BF16), drb)
            dq = _dot(da, k) + _dot_nt(dyx, rprev_ref[0, hd].astype(BF16))
            dk = _dot_tn(da, q) + _dot_nt(v, drb) * zeta
            dr_s[hd] = dr * gc + _dot_tn(q, dyx)
            drq_ref[:, sl] = _rope(dq, cs, RET_DK // 2, inverse=True).astype(BF16)
            drk_ref[:, sl] = _rope(dk * (RET_DK ** -0.5), cs, RET_DK // 2, inverse=True).astype(BF16)
            drv_ref[:, sl] = dv.astype(BF16)

    def blk(j):
        return pl.BlockSpec((cc, RW), lambda i: (n - 1 - i, j))

    return pl.pallas_call(
        body, name=name, grid=(n,),
        in_specs=[blk(0), blk(0), blk(0), blk(1), blk(2), pl.BlockSpec((cc, 1), lambda i: (n - 1 - i, 0)),
                  pl.BlockSpec((8, LANES), lambda i: (0, 0)),
                  pl.BlockSpec((1, RET_HEADS, HP, RET_DV), lambda i: (n - 1 - i, 0, 0, 0))],
        out_specs=[blk(0), blk(0), blk(0)],
        out_shape=[jax.ShapeDtypeStruct((T, RW), BF16)] * 3,
        scratch_shapes=[pltpu.VMEM((RET_HEADS, HP, RET_DV), F32)],
        compiler_params=_params(("arbitrary",)),
    )(dyn, y, proj, proj, proj, pos, tab, rprev)


def _merge_fwd(o, yn, proj, gn_w, w_bm, w_br, w_out, h, post_w, *, name):
    T, D = h.shape
    tT = min(MERGE_TILE, T)
    g_blk = PROJ_FIXED // D

    def body(o_ref, yn_ref, rg_ref, gm_ref, gr_ref, gnw_ref, wbm_ref, wbr_ref, wout_ref, h_ref, post_ref,
             omla_ref, oret_ref, m_ref, ho_ref):
        o_mla = _dot(o_ref[...], wbm_ref[...])
        rg = rg_ref[...].astype(F32)
        gated = (rg * _sigmoid(rg) * (yn_ref[...].astype(F32) * gnw_ref[...])).astype(BF16)
        o_ret = _dot(gated, wbr_ref[...])
        omla_ref[...] = o_mla.astype(BF16)
        oret_ref[...] = o_ret.astype(BF16)
        merged = _sigmoid(gm_ref[...].astype(F32)) * o_mla + _sigmoid(gr_ref[...].astype(F32)) * o_ret
        m = _dot(merged.astype(BF16), wout_ref[...])
        m_ref[...] = m
        ho_ref[...] = h_ref[...] + _rms_fwd(m, post_ref[...])

    def full(r, c):
        return pl.BlockSpec((r, c), lambda i: (0, 0))

    def rows(c, j=0):
        return pl.BlockSpec((tT, c), lambda i: (i, j))

    return pl.pallas_call(
        body, name=name, grid=(T // tT,),
        in_specs=[rows(QW), rows(RW), rows(RW, 3), rows(D, g_blk), rows(D, g_blk + 1), full(1, RW),
                  full(QW, D), full(RW, D), full(D, D), rows(D), full(1, D)],
        out_specs=[rows(D), rows(D), rows(D), rows(D)],
        out_shape=[jax.ShapeDtypeStruct((T, D), BF16), jax.ShapeDtypeStruct((T, D), BF16),
                   jax.ShapeDtypeStruct((T, D), F32), jax.ShapeDtypeStruct((T, D), F32)],
        compiler_params=_params(("parallel",)),
    )(o, yn, proj, proj, proj, gn_w, w_bm, w_br, w_out, h, post_w)


def _merge_bwd(dho, m, post_w, omla, oret, proj, yn, gn_w, o, w_out, w_bm, w_br, *, name):
    T, D = dho.shape
    tT = min(MERGE_TILE, T)
    g_blk = PROJ_FIXED // D

    def body(dho_ref, m_ref, post_ref, omla_ref, oret_ref, rg_ref, gm_ref, gr_ref, yn_ref, gnw_ref, o_ref,
             wout_ref, wbm_ref, wbr_ref,
             dm_ref, merged_ref, dgm_ref, dgr_ref, domla_ref, do_ref, delta_ref, doret_ref, gated_ref,
             drg_ref, dyn_ref, gpost_ref, ggn_ref):
        @pl.when(pl.program_id(0) == 0)
        def _():
            gpost_ref[...] = jnp.zeros_like(gpost_ref)
            ggn_ref[...] = jnp.zeros_like(ggn_ref)

        dm, gp = _rms_bwd(m_ref[...], post_ref[...], dho_ref[...])
        gpost_ref[...] += gp
        dmb = dm.astype(BF16)
        dm_ref[...] = dmb
        dmerged = _dot_nt(dmb, wout_ref[...])
        o_mla = omla_ref[...].astype(F32)
        o_ret = oret_ref[...].astype(F32)
        sgm = _sigmoid(gm_ref[...].astype(F32))
        sgr = _sigmoid(gr_ref[...].astype(F32))
        merged_ref[...] = (sgm * o_mla + sgr * o_ret).astype(BF16)
        dgm_ref[...] = (dmerged * o_mla * sgm * (1.0 - sgm)).astype(BF16)
        dgr_ref[...] = (dmerged * o_ret * sgr * (1.0 - sgr)).astype(BF16)
        domla = (dmerged * sgm).astype(BF16)
        domla_ref[...] = domla
        do = _dot_nt(domla, wbm_ref[...])
        do_ref[...] = do.astype(BF16)
        for hd in range(MLA_HEADS):
            sl = slice(hd * HP, (hd + 1) * HP)
            d = jnp.sum(do[:, sl] * o_ref[:, sl].astype(F32), axis=-1, keepdims=True)
            delta_ref[:, sl] = jnp.broadcast_to(d, (tT, HP))
        doret = (dmerged * sgr).astype(BF16)
        doret_ref[...] = doret
        dgated = _dot_nt(doret, wbr_ref[...])
        rg = rg_ref[...].astype(F32)
        sg = _sigmoid(rg)
        srg = rg * sg
        ynv = yn_ref[...].astype(F32)
        yw = ynv * gnw_ref[...]
        gated_ref[...] = (srg * yw).astype(BF16)
        drg_ref[...] = (dgated * yw * (sg * (1.0 + rg * (1.0 - sg)))).astype(BF16)
        dgs = dgated * srg
        dyn_ref[...] = dgs * gnw_ref[...]
        ggn_ref[...] += jnp.sum(dgs * ynv, axis=0, keepdims=True)

    def full(r, c):
        return pl.BlockSpec((r, c), lambda i: (0, 0))

    def rows(c, j=0):
        return pl.BlockSpec((tT, c), lambda i: (i, j))

    return pl.pallas_call(
        body, name=name, grid=(T // tT,),
        in_specs=[rows(D), rows(D), full(1, D), rows(D), rows(D), rows(RW, 3), rows(D, g_blk), rows(D, g_blk + 1),
                  rows(RW), full(1, RW), rows(QW), full(D, D), full(QW, D), full(RW, D)],
        out_specs=[rows(D), rows(D), rows(D), rows(D), rows(D), rows(QW), rows(QW), rows(D), rows(RW),
                   rows(RW), rows(RW), full(1, D), full(1, RW)],
        out_shape=[jax.ShapeDtypeStruct((T, D), BF16)] * 5
        + [jax.ShapeDtypeStruct((T, QW), BF16), jax.ShapeDtypeStruct((T, QW), F32),
           jax.ShapeDtypeStruct((T, D), BF16), jax.ShapeDtypeStruct((T, RW), BF16),
           jax.ShapeDtypeStruct((T, RW), BF16), jax.ShapeDtypeStruct((T, RW), F32),
           jax.ShapeDtypeStruct((1, D), F32), jax.ShapeDtypeStruct((1, RW), F32)],
        compiler_params=_params(("arbitrary",)),
    )(dho, m, post_w, omla, oret, proj, proj, proj, yn, gn_w, o, w_out, w_bm, w_br)


def _mesh_pos():
    return lax.axis_index("x"), lax.axis_index("y"), lax.axis_index("c")


class _Gather:
    def __init__(self, shards):
        self.operands = list(shards)
        self.n = len(shards)
        self.out_shape = [jax.ShapeDtypeStruct((N_DEV,) + s.shape, s.dtype) for s in shards]
        self.scratch = [pltpu.SemaphoreType.DMA((7 * self.n,)), pltpu.SemaphoreType.DMA((7 * self.n,)),
                        pltpu.SemaphoreType.DMA((self.n,))]

    def phase(self, p, x_refs, out_refs, sems):
        send_sems, recv_sems, local_sems = sems
        x, y, c = _mesh_pos()
        me, sibling = (x, y, c), (x, y, 1 - c)
        chips = [(1 - x, y), (x, 1 - y), (1 - x, 1 - y)]

        def copy(w, k, block, to, src=None):
            slot = out_refs[w].at[4 * block[0] + 2 * block[1] + block[2]]
            return pltpu.make_async_remote_copy(
                src_ref=slot if src is None else src, dst_ref=slot,
                send_sem=send_sems.at[7 * w + k], recv_sem=recv_sems.at[7 * w + k],
                device_id=to, device_id_type=pl.DeviceIdType.MESH)

        for w in range(self.n):
            mine = pltpu.make_async_copy(x_refs[w], out_refs[w].at[4 * x + 2 * y + c], local_sems.at[w])
            first = [copy(w, 0, me, sibling, src=x_refs[w])]
            first += [copy(w, 1 + j, me, (*chip, c), src=x_refs[w]) for j, chip in enumerate(chips)]
            passed = [copy(w, 4 + j, (*chip, c), sibling) for j, chip in enumerate(chips)]
            if p == 0:
                mine.start()
                for cp in first:
                    cp.start()
            elif p == 1:
                for j, chip in enumerate(chips):
                    copy(w, 1 + j, (*chip, c), me).wait_recv()
                    passed[j].start()
            else:
                copy(w, 0, sibling, me).wait_recv()
                for j, chip in enumerate(chips):
                    copy(w, 4 + j, (*chip, 1 - c), me).wait_recv()
                for cp in first + passed:
                    cp.wait_send()
                mine.wait()


class _Scatter:
    def __init__(self, grads, whole=()):
        self.n_sliced = len(grads)
        self.operands = list(grads) + list(whole)
        self.n = len(self.operands)
        self.out_shape = [jax.ShapeDtypeStruct(g.shape, g.dtype) for g in grads]
        self.out_shape += [jax.ShapeDtypeStruct((N_DEV,) + a.shape, a.dtype) for a in whole]
        n_sem = (N_DEV - 1) * self.n
        self.scratch = [pltpu.SemaphoreType.DMA((n_sem,)), pltpu.SemaphoreType.DMA((n_sem,)),
                        pltpu.SemaphoreType.DMA((self.n,))]

    def phase(self, p, in_refs, out_refs, sems):
        if p == 1:
            return
        send_sems, recv_sems, local_sems = sems
        x, y, c = _mesh_pos()
        me = 4 * x + 2 * y + c

        def src(w, dev):
            return in_refs[w].at[dev] if w < self.n_sliced else in_refs[w]

        for w in range(self.n):
            own = pltpu.make_async_copy(src(w, me), out_refs[w].at[me], local_sems.at[w])
            sends, recvs = [], []
            for r in range(1, N_DEV):
                px = 1 - x if r & 4 else x
                py = 1 - y if r & 2 else y
                pc = 1 - c if r & 1 else c
                peer, pidx = (px, py, pc), 4 * px + 2 * py + pc
                k = (N_DEV - 1) * w + r - 1
                sends.append(pltpu.make_async_remote_copy(
                    src_ref=src(w, pidx), dst_ref=out_refs[w].at[me], send_sem=send_sems.at[k],
                    recv_sem=recv_sems.at[k], device_id=peer, device_id_type=pl.DeviceIdType.MESH))
                recvs.append(pltpu.make_async_remote_copy(
                    src_ref=src(w, me), dst_ref=out_refs[w].at[pidx], send_sem=send_sems.at[k],
                    recv_sem=recv_sems.at[k], device_id=peer, device_id_type=pl.DeviceIdType.MESH))
            if p == 0:
                own.start()
                for cp in sends:
                    cp.start()
            else:
                for cp in recvs:
                    cp.wait_recv()
                for cp in sends:
                    cp.wait_send()
                own.wait()


def _exchange_alone(ex, *, name):
    n = ex.n

    def body(*refs):
        for p in range(3):
            ex.phase(p, refs[:n], refs[n:2 * n], refs[2 * n:])

    anyspec = pl.BlockSpec(memory_space=pl.ANY)
    return pl.pallas_call(body, name=name, out_shape=ex.out_shape, in_specs=[anyspec] * n,
                          out_specs=[anyspec] * n, scratch_shapes=ex.scratch)(*ex.operands)


def _adamw(w, parts, m, v, *, name):
    G, R, n = w.shape
    tn = 256 if (n > 256 and n % 256 == 0) else n
    tr = R
    for t in range(16, R, 16):
        if R % t == 0 and t * tn <= 160 * 1024:
            tr = t
    if R * tn <= 160 * 1024:
        tr = R

    def body(w_ref, p_ref, m_ref, v_ref, g_ref, d_ref, nm_ref, nv_ref):
        g = p_ref[0].astype(F32)
        for j in range(1, N_DEV):
            g = g + p_ref[j].astype(F32)
        g_ref[...] = g
        nm = ADAM_B1 * m_ref[...] + (1.0 - ADAM_B1) * g
        nv = ADAM_B2 * v_ref[...] + (1.0 - ADAM_B2) * (g * g)
        nm_ref[...] = nm
        nv_ref[...] = nv
        m_hat = nm / (1.0 - ADAM_B1 ** ADAM_STEP)
        v_hat = nv / (1.0 - ADAM_B2 ** ADAM_STEP)
        d_ref[...] = -ADAM_LR * (m_hat / (jnp.sqrt(v_hat) + ADAM_EPS) + ADAM_WD * w_ref[...])

    blk = pl.BlockSpec((None, tr, tn), lambda g, i, j: (g, i, j))
    return pl.pallas_call(
        body, name=name, grid=(G, R // tr, n // tn),
        in_specs=[blk, pl.BlockSpec((N_DEV, None, tr, tn), lambda g, i, j: (0, g, i, j)), blk, blk],
        out_specs=[blk, blk, blk, blk],
        out_shape=[jax.ShapeDtypeStruct((G, R, n), F32)] * 4,
        compiler_params=_params(("parallel", "parallel", "parallel")),
    )(w, parts, m, v)


def _pad_last(a, width):
    return jnp.pad(a, [(0, 0)] * (a.ndim - 1) + [(0, width - a.shape[-1])])


def _cols_of(g):
    return g.transpose(1, 0, 2).reshape(g.shape[1], N_DEV * g.shape[2])


def _col_shards(w):
    return w.reshape(w.shape[0], N_DEV, w.shape[1] // N_DEV).transpose(1, 0, 2)


def kernel(x, positions, ffn1_pre_w, ffn1_w1, ffn1_w2, ffn1_post_w, mix_pre_w, w_in, mla_q_norm_w, mla_w_uq, mla_kv_norm_w, mla_w_ukv, ret_gn_w, w_branch_mla, w_branch_ret, w_out, mix_post_w, ffn2_pre_w, ffn2_w1, ffn2_w2, ffn2_post_w, loss_target, m_ffn1_pre_w, m_ffn1_w1, m_ffn1_w2, m_ffn1_post_w, m_mix_pre_w, m_w_in, m_mla_q_norm_w, m_mla_w_uq, m_mla_kv_norm_w, m_mla_w_ukv, m_ret_gn_w, m_w_branch_mla, m_w_branch_ret, m_w_out, m_mix_post_w, m_ffn2_pre_w, m_ffn2_w1, m_ffn2_w2, m_ffn2_post_w, v_ffn1_pre_w, v_ffn1_w1, v_ffn1_w2, v_ffn1_post_w, v_mix_pre_w, v_w_in, v_mla_q_norm_w, v_mla_w_uq, v_mla_kv_norm_w, v_mla_w_ukv, v_ret_gn_w, v_w_branch_mla, v_w_branch_ret, v_w_out, v_mix_post_w, v_ffn2_pre_w, v_ffn2_w1, v_ffn2_w2, v_ffn2_post_w):
    T, D = x.shape[1], x.shape[2]
    h0 = x[0]
    tgt = loss_target[0]
    pos = positions.reshape(T, 1).astype(F32)

    big = [("ffn1_w1", ffn1_w1, m_ffn1_w1, v_ffn1_w1), ("ffn1_w2", ffn1_w2, m_ffn1_w2, v_ffn1_w2),
           ("w_in", w_in, m_w_in, v_w_in), ("mla_w_uq", mla_w_uq, m_mla_w_uq, v_mla_w_uq),
           ("mla_w_ukv", mla_w_ukv, m_mla_w_ukv, v_mla_w_ukv),
           ("w_branch_mla", w_branch_mla, m_w_branch_mla, v_w_branch_mla),
           ("w_branch_ret", w_branch_ret, m_w_branch_ret, v_w_branch_ret),
           ("w_out", w_out, m_w_out, v_w_out),
           ("ffn2_w1", ffn2_w1, m_ffn2_w1, v_ffn2_w1), ("ffn2_w2", ffn2_w2, m_ffn2_w2, v_ffn2_w2)]
    small = [("ffn1_pre_w", ffn1_pre_w, m_ffn1_pre_w, v_ffn1_pre_w), ("ffn1_post_w", ffn1_post_w, m_ffn1_post_w, v_ffn1_post_w),
             ("mix_pre_w", mix_pre_w, m_mix_pre_w, v_mix_pre_w), ("mla_q_norm_w", mla_q_norm_w, m_mla_q_norm_w, v_mla_q_norm_w),
             ("mla_kv_norm_w", mla_kv_norm_w, m_mla_kv_norm_w, v_mla_kv_norm_w), ("ret_gn_w", ret_gn_w, m_ret_gn_w, v_ret_gn_w),
             ("mix_post_w", mix_post_w, m_mix_post_w, v_mix_post_w), ("ffn2_pre_w", ffn2_pre_w, m_ffn2_pre_w, v_ffn2_pre_w),
             ("ffn2_post_w", ffn2_post_w, m_ffn2_post_w, v_ffn2_post_w)]

    half = ffn1_w2.shape[1]
    hp = -(-half // LANES) * LANES

    def rows_view(w):
        return w[0].T

    def send_w1(w):
        return jnp.pad(rows_view(w).reshape(2, half, D), ((0, 0), (0, hp - half), (0, 0))).reshape(2 * hp, D).astype(BF16)

    def send_w2(w):
        return jnp.pad(w[0], ((0, hp - half), (0, 0))).astype(BF16)

    mixer = ["w_in", "mla_w_uq", "mla_w_ukv", "w_branch_mla", "w_branch_ret", "w_out"]
    uq_w = MLA_NOPE + MLA_ROPE
    mixer_send = [rows_view(w_in).astype(BF16), jnp.pad(rows_view(mla_w_uq), ((0, HP - uq_w), (0, 0))).astype(BF16),
                  mla_w_ukv[0].astype(BF16), w_branch_mla[0].astype(BF16), w_branch_ret[0].astype(BF16),
                  w_out[0].astype(BF16)]

    w1a, w2a = _exchange_alone(_Gather([send_w1(ffn1_w1), send_w2(ffn1_w2)]), name="gather_ffn1")
    w2a = w2a.reshape(N_DEV // 2, 2 * hp, D)
    u1, f1, h1, *got = _ffn_fwd(h0, ffn1_pre_w, w1a, w2a, ffn1_post_w, None, name="ffn1_fwd_gather_mixer",
                                exchange=_Gather(mixer_send))
    fw = dict(zip(mixer, got))

    wi = fw["w_in"].reshape(-1, D)
    cq_w, ckv_w, kr_w = wi[0:384], wi[384:640], wi[640:672]
    rq_w, rk_w = wi[672:928], wi[928:1184]
    rv_w, rg_w = wi[1184:1696], wi[1696:2208]
    gm_w, gr_w = wi[2208:2208 + D], wi[2208 + D:2208 + 2 * D]
    zer = lambda n: jnp.zeros((n, D), BF16)
    head_rows = lambda a, h: jnp.pad(a.reshape(h, -1, D), ((0, 0), (0, HP - a.shape[0] // h), (0, 0))).reshape(h * HP, D)
    w_in_p = jnp.concatenate([head_rows(rq_w, RET_HEADS), head_rows(rk_w, RET_HEADS), rv_w, rg_w,
                              cq_w, ckv_w, zer(MLA_NOPE), kr_w, zer(HP - MLA_NOPE - MLA_ROPE), zer(AW - 768),
                              gm_w, gr_w], axis=0)
    w_uq_p = fw["mla_w_uq"].reshape(QW, MLA_Q_RANK)
    ukv = fw["mla_w_ukv"].transpose(1, 0, 2)
    w_kv_p = jnp.concatenate([_pad_last(ukv[:, :, :MLA_NOPE], HP).reshape(MLA_KV_RANK, QW),
                              _pad_last(ukv[:, :, MLA_NOPE:], HP).reshape(MLA_KV_RANK, QW)], axis=1)
    w_bm_p = jnp.pad(_cols_of(fw["w_branch_mla"]).reshape(MLA_HEADS, MLA_V, D),
                     ((0, 0), (0, HP - MLA_V), (0, 0))).reshape(QW, D)
    w_br, w_o = _cols_of(fw["w_branch_ret"]), fw["w_out"].reshape(D, D)
    tab_mla = _rope_table(MLA_NOPE, MLA_ROPE // 2)
    tab_ret = _rope_table(0, RET_DK // 2)

    proj, a1 = _rms_matmul(h1, mix_pre_w, w_in_p, name="mixer_in_proj")
    q, k, v, qn, kvn = _mla_prep_fwd(proj, pos, mla_q_norm_w, mla_kv_norm_w, w_uq_p, w_kv_p, tab_mla, name="mla_prep_fwd")
    o, lse, w1b, w2b = _flash_fwd(q, k, v, name="mla_attn_fwd_gather_ffn2",
                                  exchange=_Gather([send_w1(ffn2_w1), send_w2(ffn2_w2)]))
    w2b = w2b.reshape(N_DEV // 2, 2 * hp, D)
    ypre, yn, rprev = _ret_fwd(proj, pos, tab_ret, name="retention_fwd")
    omla, oret, m, h2 = _merge_fwd(o, yn, proj, ret_gn_w, w_bm_p, w_br, w_o, h1, mix_post_w, name="merge_fwd")
    u2, f2, _, dy, lossp = _ffn_fwd(h2, ffn2_pre_w, w1b, w2b, ffn2_post_w, tgt, name="ffn2_fwd_loss")
    loss = lax.psum(jnp.sum(lossp[::8, 0]), ("x", "y", "c"))

    def grad(x, dy, tag, exchange=None):
        return _matmul_tn(x if x.ndim == 3 else x[None], dy if dy.ndim == 3 else dy[None], name=tag, exchange=exchange)

    g2, du2, df2, a2, dh2, gpost2, gpre2 = _ffn_bwd(dy, f2, ffn2_post_w, h2, ffn2_pre_w, u2, w2b, w1b, name="ffn2_bwd")
    dw1b, = grad(du2.reshape(N_DEV, T, 2 * hp), a2, "ffn2_dw1")
    dw2b = grad(g2, df2, "ffn2_dw2")[0].reshape(N_DEV, hp, D)
    (dmb, merged, dgm, dgr, domla, do, delta, doret, gated, drg, dyn, gpostm, ggn) = _merge_bwd(
        dh2, m, mix_post_w, omla, oret, proj, yn, ret_gn_w, o, w_o, w_bm_p, w_br, name="merge_bwd")
    dw_out = grad(merged, dmb, "dw_out")[0][0]
    dw_bm_p = grad(o, domla, "dw_branch_mla")[0][0]
    dw_br = grad(gated, doret, "dw_branch_ret")[0][0]
    dq, dk, dv, *recv_ffn2 = _flash_bwd(q, k, v, do, lse, delta, name="mla_attn_bwd_scatter_ffn2",
                                        exchange=_Scatter([dw1b, dw2b]))
    da, dql, dkvl, gqn, gkvn = _mla_prep_bwd(dq, dk, dv, proj, pos, mla_q_norm_w, mla_kv_norm_w, w_uq_p, w_kv_p, tab_mla, name="mla_prep_bwd")
    dw_uq_p = grad(dql, qn, "dw_uq")[0][0]
    dw_kv_p = grad(kvn, dkvl, "dw_ukv")[0][0]
    drq, drk, drv = _ret_bwd(dyn, ypre, proj, pos, tab_ret, rprev, name="retention_bwd")
    dproj = jnp.concatenate([drq, drk, drv, drg, da, dgm, dgr], axis=1)
    dw_in_p = grad(dproj, a1, "dw_in")[0][0]
    dh1, gmixpre = _proj_bwd(dproj, w_in_p, h1, mix_pre_w, dh2, name="mixer_in_bwd")

    unhead = lambda a, h, wd: a.reshape(h, HP, D)[:, :wd].reshape(h * wd, D)
    c0 = 4 * RW
    dw_in = jnp.concatenate([
        dw_in_p[c0:c0 + 384], dw_in_p[c0 + 384:c0 + 640], dw_in_p[c0 + 640 + MLA_NOPE:c0 + 640 + MLA_NOPE + MLA_ROPE],
        unhead(dw_in_p[0:RW], RET_HEADS, RET_DK), unhead(dw_in_p[RW:2 * RW], RET_HEADS, RET_DK),
        dw_in_p[2 * RW:3 * RW], dw_in_p[3 * RW:4 * RW],
        dw_in_p[PROJ_FIXED:PROJ_FIXED + D], dw_in_p[PROJ_FIXED + D:PROJ_FIXED + 2 * D]], axis=0).reshape(N_DEV, -1, D)
    dw_uq = dw_uq_p.reshape(MLA_HEADS, HP, MLA_Q_RANK)[:, :uq_w]
    dkp = dw_kv_p[:, :QW].reshape(MLA_KV_RANK, MLA_HEADS, HP)[:, :, :MLA_NOPE]
    dvp = dw_kv_p[:, QW:].reshape(MLA_KV_RANK, MLA_HEADS, HP)[:, :, :MLA_V]
    dw_ukv = jnp.concatenate([dkp, dvp], axis=2).transpose(1, 0, 2)
    dw_bm = dw_bm_p.reshape(MLA_HEADS, HP, D)[:, :MLA_V].reshape(MLA_HEADS * MLA_V, D)

    mixer_grads = [dw_in, dw_uq, dw_ukv, _col_shards(dw_bm), _col_shards(dw_br),
                   dw_out.reshape(N_DEV, D // N_DEV, D)]
    g1, du1, df1, a0, dx, gpost1, gpre1, *recv_mixer = _ffn_bwd(
        dh1, f1, ffn1_post_w, h0, ffn1_pre_w, u1, w2a, w1a, name="ffn1_bwd_scatter_mixer", exchange=_Scatter(mixer_grads))
    dw2a = grad(g1, df1, "ffn1_dw2")[0].reshape(N_DEV, hp, D)
    dw1a, recv_w2a = grad(du1.reshape(N_DEV, T, 2 * hp), a0, "ffn1_dw1_scatter_dw2", exchange=_Scatter([dw2a]))

    small_g = {"ffn1_pre_w": gpre1, "ffn1_post_w": gpost1, "mix_pre_w": gmixpre, "mla_q_norm_w": gqn,
               "mla_kv_norm_w": gkvn, "ret_gn_w": ggn, "mix_post_w": gpostm, "ffn2_pre_w": gpre2, "ffn2_post_w": gpost2}

    def small_flat(arrs):
        a = jnp.concatenate([z.reshape(-1) for z in arrs])
        a = jnp.pad(a, (0, (-a.size) % (8 * LANES)))
        return a.reshape(-1, LANES)

    sv = small_flat([small_g[nm] for nm, *_ in small])
    recv_w1a, sall = _exchange_alone(_Scatter([dw1a], whole=[sv]), name="scatter_ffn1_dw1")
    parts = dict(zip(mixer, recv_mixer))
    parts.update(ffn1_w1=recv_w1a, ffn1_w2=recv_w2a, ffn2_w1=recv_ffn2[0], ffn2_w2=recv_ffn2[1])
    as_is = (lambda a: a, lambda p: p[:, None], lambda a: a)
    views = {nm: as_is for nm, *_ in big}
    for nm in ("ffn1_w1", "ffn2_w1"):
        views[nm] = (lambda a: rows_view(a).reshape(2, half, D), lambda p: p.reshape(N_DEV, 2, hp, D),
                     lambda a: a.reshape(2 * half, D).T[None])
    for nm in ("w_in", "mla_w_uq"):
        views[nm] = (lambda a: rows_view(a)[None], lambda p: p[:, None], lambda a: a[0].T[None])
    big_out = {}
    for nm, w, m_, v_ in big:
        to_view, parts_view, back = views[nm]
        big_out[nm] = [back(a) for a in _adamw(to_view(w), parts_view(parts[nm]), to_view(m_), to_view(v_),
                                               name="adamw_" + nm)]
    flat3 = lambda arrs: small_flat(arrs)[None]
    gs, ds, nms, nvs = _adamw(flat3([w for _, w, _, _ in small]), sall[:, None],
                              flat3([a for _, _, a, _ in small]), flat3([a for _, _, _, a in small]),
                              name="adamw_replicated")

    def split_small(flat):
        out, o_ = {}, 0
        fl = flat.reshape(-1)
        for nm, w, _, _ in small:
            out[nm] = fl[o_:o_ + w.size].reshape(w.shape)
            o_ += w.size
        return out

    order = ["ffn1_pre_w", "ffn1_w1", "ffn1_w2", "ffn1_post_w", "mix_pre_w", "w_in", "mla_q_norm_w", "mla_w_uq",
             "mla_kv_norm_w", "mla_w_ukv", "ret_gn_w", "w_branch_mla", "w_branch_ret", "w_out", "mix_post_w",
             "ffn2_pre_w", "ffn2_w1", "ffn2_w2", "ffn2_post_w"]
    outs = [loss, dx[None]]
    for i, fs in enumerate((gs, ds, nms, nvs)):
        both = {**{nm: big_out[nm][i] for nm in big_out}, **split_small(fs)}
        outs += [both[nm] for nm in order]
    return tuple(outs)
```

```python
import math

import numpy as np
import jax
import jax.numpy as jnp
from jax import lax
from jax.experimental import pallas as pl
from jax.experimental.pallas import tpu as pltpu

F32, BF16 = jnp.float32, jnp.bfloat16

MLA_HEADS, MLA_NOPE, MLA_ROPE, MLA_V = 8, 64, 32, 64
MLA_Q_RANK, MLA_KV_RANK = 384, 256
RET_HEADS, RET_DK, RET_DV = 4, 64, 128
ROPE_BASE, NORM_EPS, GN_EPS = 10000.0, 1e-6, 1e-6
ADAM_LR, ADAM_B1, ADAM_B2, ADAM_EPS, ADAM_WD, ADAM_STEP = 0.001, 0.9, 0.999, 1e-08, 0.01, 10
ATTN_SCALE = 1.0 / math.sqrt(MLA_NOPE + MLA_ROPE)

N_DEV = 8
LANES = 128
HP = LANES
QW = MLA_HEADS * HP
RW = RET_HEADS * HP
AW = 1024
PROJ_FIXED = 4 * RW + AW
NEG = -1e30

TOKEN_TILE = 512
ATTN_TILE = 1024
ATTN_CHAINS = 2
FFN_CHAINS = 2
RET_TILE = 256
PROJ_TILE_CAP = 2560
GRAD_TILE_CAP = 1408
GRAD_TOKEN_TILE = 2048
MERGE_TILE = 256
VMEM_LIMIT = 56 * 1024 * 1024


def _tile(n, cap, mult=LANES):
    if n <= cap:
        return n
    best = None
    for t in range(mult, cap + 1, mult):
        if n % t == 0:
            best = t
    assert best is not None, (n, cap, mult)
    return best


def _params(sem):
    return pltpu.CompilerParams(dimension_semantics=sem, vmem_limit_bytes=VMEM_LIMIT)


def _dot(a, b):
    return lax.dot_general(a, b, (((1,), (0,)), ((), ())), preferred_element_type=F32)


def _dot_nt(a, b):
    return lax.dot_general(a, b, (((1,), (1,)), ((), ())), preferred_element_type=F32)


def _dot_tn(a, b):
    return lax.dot_general(a, b, (((0,), (0,)), ((), ())), preferred_element_type=F32)


def _sigmoid(x):
    return pl.reciprocal(1.0 + jnp.exp(-x), approx=True)


def _rms_fwd(x, w):
    r = lax.rsqrt(jnp.mean(x * x, axis=-1, keepdims=True) + NORM_EPS)
    return x * r * w


def _rms_bwd(x, w, dy):
    r = lax.rsqrt(jnp.mean(x * x, axis=-1, keepdims=True) + NORM_EPS)
    xh = x * r
    g = dy * w
    dx = r * (g - xh * jnp.mean(g * xh, axis=-1, keepdims=True))
    return dx, jnp.sum(dy * xh, axis=0, keepdims=True)


def _rope_table(first, half):
    inv = (np.float32(ROPE_BASE) ** (-(np.arange(half, dtype=np.float32) / np.float32(half)))).astype(np.float32)
    tab = np.zeros((8, LANES), np.float32)
    tab[0, first:first + half] = inv
    tab[0, first + half:first + 2 * half] = inv
    tab[1, first:first + half] = -1.0
    tab[2, first + half:first + 2 * half] = 1.0
    return jnp.asarray(tab)


def _rope_cs(pos, tab_ref):
    ang = pos * tab_ref[0:1, :]
    s = jnp.sin(ang)
    return jnp.cos(ang), s * tab_ref[1:2, :], s * tab_ref[2:3, :]


def _rope(x, cs, half, inverse=False):
    c, s1, s2 = cs
    a = pltpu.roll(x, LANES - half, 1) * s1 + pltpu.roll(x, half, 1) * s2
    return x * c - a if inverse else x * c + a


def _call(body, *, name, grid, in_specs, out_specs, out_shape, scratch_shapes, args, exchange=None):
    sem = ("arbitrary",) * len(grid)
    if exchange is None:
        return pl.pallas_call(body, name=name, grid=grid, in_specs=in_specs, out_specs=out_specs,
                              out_shape=out_shape, scratch_shapes=scratch_shapes, compiler_params=_params(sem))(*args)
    n_in, n_out, e = len(in_specs), len(out_specs), exchange.n
    total = math.prod(grid)

    def carried(*refs):
        own = refs[:n_in] + refs[n_in + e:n_in + e + n_out] + refs[n_in + 2 * e + n_out:len(refs) - 3]
        ex_refs = (refs[n_in:n_in + e], refs[n_in + e + n_out:n_in + 2 * e + n_out], refs[len(refs) - 3:])
        step = pl.program_id(0)
        for d in range(1, len(grid)):
            step = step * grid[d] + pl.program_id(d)

        @pl.when(step == 0)
        def _():
            exchange.phase(0, *ex_refs)

        @pl.when(step == total // 2)
        def _():
            exchange.phase(1, *ex_refs)

        body(*own)

        @pl.when(step == total - 1)
        def _():
            exchange.phase(2, *ex_refs)

    anyspec = pl.BlockSpec(memory_space=pl.ANY)
    return pl.pallas_call(
        carried, name=name, grid=grid, in_specs=list(in_specs) + [anyspec] * e,
        out_specs=list(out_specs) + [anyspec] * e, out_shape=list(out_shape) + exchange.out_shape,
        scratch_shapes=list(scratch_shapes) + exchange.scratch, compiler_params=_params(sem),
    )(*args, *exchange.operands)


def _ffn_fwd(h, pre_w, w1, w2, post_w, target, *, name, exchange=None):
    T, D = h.shape
    nk, ck = w2.shape[0], w2.shape[1]
    tT = min(TOKEN_TILE, T)
    nT = T // tT
    with_loss = target is not None

    def body(*refs):
        if with_loss:
            (h_ref, pre_ref, w1g_ref, w1u_ref, w2_ref, post_ref, tgt_ref,
             u_ref, f_ref, ho_ref, dy_ref, loss_ref, a_s, acc) = refs
        else:
            (h_ref, pre_ref, w1g_ref, w1u_ref, w2_ref, post_ref,
             u_ref, f_ref, ho_ref, a_s, acc) = refs
        k = pl.program_id(1)

        @pl.when(k == 0)
        def _():
            a_s[...] = _rms_fwd(h_ref[...], pre_ref[...]).astype(BF16)
            acc[...] = jnp.zeros_like(acc)

        for c in range(FFN_CHAINS):
            rs = slice(c * (tT // FFN_CHAINS), (c + 1) * (tT // FFN_CHAINS))
            a = a_s[rs, :]
            ug = _dot_nt(a, w1g_ref[...])
            uu = _dot_nt(a, w1u_ref[...])
            u_ref[0, rs, :] = ug.astype(BF16)
            u_ref[1, rs, :] = uu.astype(BF16)
            acc[rs, :] += _dot((ug * _sigmoid(ug) * uu).astype(BF16), w2_ref[...])

        @pl.when(k == nk - 1)
        def _():
            f = acc[...]
            f_ref[...] = f
            ho = h_ref[...] + 0.5 * _rms_fwd(f, post_ref[...])
            ho_ref[...] = ho
            if with_loss:
                e = ho - tgt_ref[...]
                dy_ref[...] = e * (1.0 / D)
                loss_ref[...] = jnp.full(loss_ref.shape, (0.5 / D) * jnp.sum(e * e), F32)

    row = pl.BlockSpec((tT, D), lambda i, k: (i, 0))
    vec = pl.BlockSpec((1, D), lambda i, k: (0, 0))
    in_specs = [row, vec,
                pl.BlockSpec((None, ck, D), lambda i, k: (k, 0, 0)),
                pl.BlockSpec((None, ck, D), lambda i, k: (nk + k, 0, 0)),
                pl.BlockSpec((None, ck, D), lambda i, k: (k, 0, 0)),
                vec]
    out_shape = [jax.ShapeDtypeStruct((2, nk, T, ck), BF16),
                 jax.ShapeDtypeStruct((T, D), F32),
                 jax.ShapeDtypeStruct((T, D), F32)]
    out_specs = [pl.BlockSpec((2, None, tT, ck), lambda i, k: (0, k, i, 0)), row, row]
    args = [h, pre_w, w1, w1, w2, post_w]
    if with_loss:
        in_specs.append(row)
        args.append(target)
        out_shape += [jax.ShapeDtypeStruct((T, D), F32), jax.ShapeDtypeStruct((nT * 8, LANES), F32)]
        out_specs += [row, pl.BlockSpec((8, LANES), lambda i, k: (i, 0))]
    return _call(body, name=name, grid=(nT, nk), in_specs=in_specs, out_specs=out_specs, out_shape=out_shape,
                 scratch_shapes=[pltpu.VMEM((tT, D), BF16), pltpu.VMEM((tT, D), F32)], args=args, exchange=exchange)


def _ffn_bwd(dho, f, post_w, h, pre_w, u, w2, w1, *, name, exchange=None):
    T, D = h.shape
    nk, ck = w2.shape[0], w2.shape[1]
    tT = min(TOKEN_TILE, T)
    nT = T // tT

    def body(dho_ref, f_ref, post_ref, h_ref, pre_ref, u_ref, w2_ref, w1g_ref, w1u_ref,
             g_ref, du_ref, df_ref, a_ref, dh_ref, gpost_ref, gpre_ref, df_s, da_acc):
        i, k = pl.program_id(0), pl.program_id(1)

        @pl.when(jnp.logical_and(i == 0, k == 0))
        def _():
            gpost_ref[...] = jnp.zeros_like(gpost_ref)
            gpre_ref[...] = jnp.zeros_like(gpre_ref)

        @pl.when(k == 0)
        def _():
            dx, dw = _rms_bwd(f_ref[...], post_ref[...], 0.5 * dho_ref[...])
            dfb = dx.astype(BF16)
            df_s[...] = dfb
            df_ref[...] = dfb
            gpost_ref[...] += dw
            a_ref[...] = _rms_fwd(h_ref[...], pre_ref[...]).astype(BF16)
            da_acc[...] = jnp.zeros_like(da_acc)

        for c in range(FFN_CHAINS):
            rs = slice(c * (tT // FFN_CHAINS), (c + 1) * (tT // FFN_CHAINS))
            dg = _dot_nt(df_s[rs, :], w2_ref[...])
            ug = u_ref[0, rs, :].astype(F32)
            uu = u_ref[1, rs, :].astype(F32)
            sg = _sigmoid(ug)
            sl = ug * sg
            g_ref[rs, :] = (sl * uu).astype(BF16)
            dug = (dg * uu * (sg + sl * (1.0 - sg))).astype(BF16)
            duu = (dg * sl).astype(BF16)
            du_ref[0, rs, :] = dug
            du_ref[1, rs, :] = duu
            da_acc[rs, :] += _dot(dug, w1g_ref[...]) + _dot(duu, w1u_ref[...])

        @pl.when(k == nk - 1)
        def _():
            dx, dw = _rms_bwd(h_ref[...], pre_ref[...], da_acc[...])
            dh_ref[...] = dho_ref[...] + dx
            gpre_ref[...] += dw

    row = pl.BlockSpec((tT, D), lambda i, k: (i, 0))
    vec = pl.BlockSpec((1, D), lambda i, k: (0, 0))
    return _call(
        body, name=name, grid=(nT, nk),
        in_specs=[row, row, vec, row, vec,
                  pl.BlockSpec((2, None, tT, ck), lambda i, k: (0, k, i, 0)),
                  pl.BlockSpec((None, ck, D), lambda i, k: (k, 0, 0)),
                  pl.BlockSpec((None, ck, D), lambda i, k: (k, 0, 0)),
                  pl.BlockSpec((None, ck, D), lambda i, k: (nk + k, 0, 0))],
        out_specs=[pl.BlockSpec((None, tT, ck), lambda i, k: (k, i, 0)),
                   pl.BlockSpec((2, None, tT, ck), lambda i, k: (0, k, i, 0)),
                   row, row, row, vec, vec],
        out_shape=[jax.ShapeDtypeStruct((nk, T, ck), BF16),
                   jax.ShapeDtypeStruct((2, nk, T, ck), BF16),
                   jax.ShapeDtypeStruct((T, D), BF16),
                   jax.ShapeDtypeStruct((T, D), BF16),
                   jax.ShapeDtypeStruct((T, D), F32),
                   jax.ShapeDtypeStruct((1, D), F32),
                   jax.ShapeDtypeStruct((1, D), F32)],
        scratch_shapes=[pltpu.VMEM((tT, D), BF16), pltpu.VMEM((tT, D), F32)],
        args=(dho, f, post_w, h, pre_w, u, w2, w1, w1), exchange=exchange)


def _matmul_tn(x, dy, *, name, exchange=None):
    Px, T, K = x.shape
    Py, _, N = dy.shape
    P = max(Px, Py)
    tT, tK, tN = min(GRAD_TOKEN_TILE, T), _tile(K, GRAD_TILE_CAP), _tile(N, GRAD_TILE_CAP)
    nt = T // tT

    def body(x_ref, dy_ref, o_ref, acc):
        t = pl.program_id(3)

        @pl.when(t == 0)
        def _():
            acc[...] = jnp.zeros_like(acc)

        acc[...] += _dot_tn(x_ref[...], dy_ref[...])

        @pl.when(t == nt - 1)
        def _():
            o_ref[...] = acc[...].astype(BF16)

    return _call(
        body, name=name, grid=(P, K // tK, N // tN, nt),
        in_specs=[pl.BlockSpec((None, tT, tK), lambda p, a, b, t: (p if Px > 1 else 0, t, a)),
                  pl.BlockSpec((None, tT, tN), lambda p, a, b, t: (p if Py > 1 else 0, t, b))],
        out_specs=[pl.BlockSpec((None, tK, tN), lambda p, a, b, t: (p, a, b))],
        out_shape=[jax.ShapeDtypeStruct((P, K, N), BF16)],
        scratch_shapes=[pltpu.VMEM((tK, tN), F32)], args=(x, dy), exchange=exchange)


def _rms_matmul(h, wn, w, *, name):
    T, D = h.shape
    N = w.shape[0]
    tT, tN = min(TOKEN_TILE, T), _tile(N, PROJ_TILE_CAP)

    def body(h_ref, wn_ref, w_ref, y_ref, a_ref):
        @pl.when(pl.program_id(1) == 0)
        def _():
            a_ref[...] = _rms_fwd(h_ref[...], wn_ref[...]).astype(BF16)

        y_ref[...] = _dot_nt(a_ref[...], w_ref[...]).astype(BF16)

    return pl.pallas_call(
        body, name=name, grid=(T // tT, N // tN),
        in_specs=[pl.BlockSpec((tT, D), lambda i, j: (i, 0)),
                  pl.BlockSpec((1, D), lambda i, j: (0, 0)),
                  pl.BlockSpec((tN, D), lambda i, j: (j, 0))],
        out_specs=[pl.BlockSpec((tT, tN), lambda i, j: (i, j)),
                   pl.BlockSpec((tT, D), lambda i, j: (i, 0))],
        out_shape=[jax.ShapeDtypeStruct((T, N), BF16), jax.ShapeDtypeStruct((T, D), BF16)],
        compiler_params=_params(("parallel", "arbitrary")),
    )(h, wn, w)


def _proj_bwd(dproj, w, h, wn, dres, *, name):
    T, D = h.shape
    N = w.shape[0]
    tT, tN = min(TOKEN_TILE, T), _tile(N, PROJ_TILE_CAP)
    nn = N // tN

    def body(dp_ref, w_ref, h_ref, wn_ref, dres_ref, dh_ref, gw_ref, acc):
        i, j = pl.program_id(0), pl.program_id(1)

        @pl.when(jnp.logical_and(i == 0, j == 0))
        def _():
            gw_ref[...] = jnp.zeros_like(gw_ref)

        @pl.when(j == 0)
        def _():
            acc[...] = jnp.zeros_like(acc)

        acc[...] += _dot(dp_ref[...], w_ref[...])

        @pl.when(j == nn - 1)
        def _():
            dx, dw = _rms_bwd(h_ref[...], wn_ref[...], acc[...])
            dh_ref[...] = dres_ref[...] + dx
            gw_ref[...] += dw

    row = pl.BlockSpec((tT, D), lambda i, j: (i, 0))
    vec = pl.BlockSpec((1, D), lambda i, j: (0, 0))
    return pl.pallas_call(
        body, name=name, grid=(T // tT, nn),
        in_specs=[pl.BlockSpec((tT, tN), lambda i, j: (i, j)),
                  pl.BlockSpec((tN, D), lambda i, j: (j, 0)), row, vec, row],
        out_specs=[row, vec],
        out_shape=[jax.ShapeDtypeStruct((T, D), F32), jax.ShapeDtypeStruct((1, D), F32)],
        scratch_shapes=[pltpu.VMEM((tT, D), F32)],
        compiler_params=_params(("arbitrary", "arbitrary")),
    )(dproj, w, h, wn, dres)


def _mla_prep_fwd(proj, pos, qn_w, kvn_w, w_uq, w_kv, tab, *, name):
    T = proj.shape[0]
    tT = min(TOKEN_TILE, T)
    a_blk = PROJ_FIXED // AW - 1

    def body(a_ref, pos_ref, qnw_ref, kvnw_ref, wuq_ref, wkv_ref, tab_ref,
             q_ref, k_ref, v_ref, qn_ref, kvn_ref):
        cq = a_ref[:, 0:MLA_Q_RANK].astype(F32)
        ckv = a_ref[:, MLA_Q_RANK:MLA_Q_RANK + MLA_KV_RANK].astype(F32)
        kr = a_ref[:, 640:768].astype(F32)
        qn = _rms_fwd(cq, qnw_ref[...]).astype(BF16)
        kvn = _rms_fwd(ckv, kvnw_ref[...]).astype(BF16)
        qn_ref[...] = qn
        kvn_ref[...] = kvn
        cs = _rope_cs(pos_ref[...], tab_ref)
        q = _dot_nt(qn, wuq_ref[...])
        kv = _dot(kvn, wkv_ref[...])
        krr = _rope(kr, cs, MLA_ROPE // 2)
        for hd in range(MLA_HEADS):
            sl = slice(hd * HP, (hd + 1) * HP)
            q_ref[:, sl] = (_rope(q[:, sl], cs, MLA_ROPE // 2) * ATTN_SCALE).astype(BF16)
            k_ref[:, sl] = (kv[:, sl] + krr).astype(BF16)
        v_ref[...] = kv[:, QW:].astype(BF16)

    def full(r, c):
        return pl.BlockSpec((r, c), lambda i: (0, 0))

    def rows(c):
        return pl.BlockSpec((tT, c), lambda i: (i, 0))

    return pl.pallas_call(
        body, name=name, grid=(T // tT,),
        in_specs=[pl.BlockSpec((tT, AW), lambda i: (i, a_blk)), rows(1),
                  full(1, MLA_Q_RANK), full(1, MLA_KV_RANK),
                  full(QW, MLA_Q_RANK), full(MLA_KV_RANK, 2 * QW), full(8, LANES)],
        out_specs=[rows(QW), rows(QW), rows(QW), rows(MLA_Q_RANK), rows(MLA_KV_RANK)],
        out_shape=[jax.ShapeDtypeStruct((T, QW), BF16)] * 3
        + [jax.ShapeDtypeStruct((T, MLA_Q_RANK), BF16), jax.ShapeDtypeStruct((T, MLA_KV_RANK), BF16)],
        compiler_params=_params(("parallel",)),
    )(proj, pos, qn_w, kvn_w, w_uq, w_kv, tab)


def _mla_prep_bwd(dq, dk, dv, proj, pos, qn_w, kvn_w, w_uq, w_kv, tab, *, name):
    T = proj.shape[0]
    tT = min(TOKEN_TILE, T)
    a_blk = PROJ_FIXED // AW - 1

    def body(dq_ref, dk_ref, dv_ref, a_ref, pos_ref, qnw_ref, kvnw_ref, wuq_ref, wkv_ref, tab_ref,
             da_ref, dql_ref, dkvl_ref, gqn_ref, gkvn_ref):
        @pl.when(pl.program_id(0) == 0)
        def _():
            gqn_ref[...] = jnp.zeros_like(gqn_ref)
            gkvn_ref[...] = jnp.zeros_like(gkvn_ref)

        cs = _rope_cs(pos_ref[...], tab_ref)
        dkr = jnp.zeros((tT, HP), F32)
        for hd in range(MLA_HEADS):
            sl = slice(hd * HP, (hd + 1) * HP)
            dql_ref[:, sl] = (_rope(dq_ref[:, sl], cs, MLA_ROPE // 2, inverse=True) * ATTN_SCALE).astype(BF16)
            dkh = dk_ref[:, sl]
            dkr = dkr + dkh
            dkvl_ref[:, sl] = dkh.astype(BF16)
        dkvl_ref[:, QW:] = dv_ref[...]
        dqn = _dot(dql_ref[...], wuq_ref[...])
        dkvn = _dot_nt(dkvl_ref[...], wkv_ref[...])
        cq = a_ref[:, 0:MLA_Q_RANK].astype(F32)
        ckv = a_ref[:, MLA_Q_RANK:MLA_Q_RANK + MLA_KV_RANK].astype(F32)
        dcq, gq = _rms_bwd(cq, qnw_ref[...], dqn)
        dckv, gkv = _rms_bwd(ckv, kvnw_ref[...], dkvn)
        gqn_ref[...] += gq
        gkvn_ref[...] += gkv
        da_ref[:, 0:MLA_Q_RANK] = dcq.astype(BF16)
        da_ref[:, MLA_Q_RANK:MLA_Q_RANK + MLA_KV_RANK] = dckv.astype(BF16)
        da_ref[:, 640:768] = _rope(dkr, cs, MLA_ROPE // 2, inverse=True).astype(BF16)
        da_ref[:, 768:AW] = jnp.zeros((tT, AW - 768), BF16)

    def full(r, c):
        return pl.BlockSpec((r, c), lambda i: (0, 0))

    def rows(c):
        return pl.BlockSpec((tT, c), lambda i: (i, 0))

    return pl.pallas_call(
        body, name=name, grid=(T // tT,),
        in_specs=[rows(QW), rows(QW), rows(QW), pl.BlockSpec((tT, AW), lambda i: (i, a_blk)), rows(1),
                  full(1, MLA_Q_RANK), full(1, MLA_KV_RANK),
                  full(QW, MLA_Q_RANK), full(MLA_KV_RANK, 2 * QW), full(8, LANES)],
        out_specs=[rows(AW), rows(QW), rows(2 * QW), full(1, MLA_Q_RANK), full(1, MLA_KV_RANK)],
        out_shape=[jax.ShapeDtypeStruct((T, AW), BF16), jax.ShapeDtypeStruct((T, QW), BF16),
                   jax.ShapeDtypeStruct((T, 2 * QW), BF16),
                   jax.ShapeDtypeStruct((1, MLA_Q_RANK), F32), jax.ShapeDtypeStruct((1, MLA_KV_RANK), F32)],
        compiler_params=_params(("arbitrary",)),
    )(dq, dk, dv, proj, pos, qn_w, kvn_w, w_uq, w_kv, tab)


def _flash_fwd(q, k, v, *, name, exchange=None):
    T = q.shape[0]
    H = q.shape[1] // HP
    tq = min(ATTN_TILE, T)
    nq = T // tq

    sub = tq // ATTN_CHAINS

    def body(q_ref, k_ref, v_ref, o_ref, lse_ref):
        qi = pl.program_id(1)
        qs = [q_ref[c * sub:(c + 1) * sub, :] for c in range(ATTN_CHAINS)]

        def update(carry, off, masked):
            out = []
            for c in range(ATTN_CHAINS):
                m_prev, l_prev, acc = carry[c]
                nk = (c + 1) * sub if masked else tq
                kb = k_ref[pl.ds(off, nk), :]
                vb = v_ref[pl.ds(off, nk), :]
                s = _dot_nt(qs[c], kb)
                if masked:
                    rows = lax.broadcasted_iota(jnp.int32, (sub, nk), 0) + c * sub
                    s = jnp.where(rows >= lax.broadcasted_iota(jnp.int32, (sub, nk), 1), s, NEG)
                m_new = jnp.maximum(m_prev, jnp.max(s, axis=1, keepdims=True))
                alpha = jnp.exp(m_prev - m_new)
                p = jnp.exp(s - m_new)
                out.append((m_new, alpha * l_prev + jnp.sum(p, axis=1, keepdims=True),
                            alpha * acc + _dot(p.astype(BF16), vb)))
            return tuple(out)

        init = tuple((jnp.full((sub, 1), NEG, F32), jnp.zeros((sub, 1), F32), jnp.zeros((sub, HP), F32))
                     for _ in range(ATTN_CHAINS))
        carry = lax.fori_loop(0, qi, lambda j, cr: update(cr, pl.multiple_of(j * tq, tq), False), init)
        carry = update(carry, pl.multiple_of(qi * tq, tq), True)
        for c in range(ATTN_CHAINS):
            m_fin, l_fin, acc = carry[c]
            o_ref[c * sub:(c + 1) * sub, :] = (acc / l_fin).astype(BF16)
            lse_ref[c * sub:(c + 1) * sub, :] = jnp.broadcast_to(m_fin + jnp.log(l_fin), (sub, HP))

    qspec = pl.BlockSpec((tq, HP), lambda h, i: (i, h))
    kspec = pl.BlockSpec((T, HP), lambda h, i: (0, h))
    return _call(
        body, name=name, grid=(H, nq),
        in_specs=[qspec, kspec, kspec], out_specs=[qspec, qspec],
        out_shape=[jax.ShapeDtypeStruct((T, H * HP), BF16), jax.ShapeDtypeStruct((T, H * HP), F32)],
        scratch_shapes=[], args=(q, k, v), exchange=exchange)


def _flash_bwd(q, k, v, do, lse, delta, *, name, exchange=None):
    T = q.shape[0]
    H = q.shape[1] // HP
    tq = min(ATTN_TILE, T)
    nq = T // tq
    sub = tq // ATTN_CHAINS

    def body(k_ref, v_ref, q_ref, do_ref, lse_ref, dl_ref, dq_ref, dk_ref, dv_ref):
        ki = pl.program_id(1)

        @pl.when(ki == 0)
        def _():
            dq_ref[...] = jnp.zeros_like(dq_ref)

        def grow(a):
            return a if a.shape[0] == tq else jnp.concatenate([a, jnp.zeros((tq - a.shape[0], HP), F32)], axis=0)

        def step(carry, j, masked):
            dk_acc, dv_acc = carry
            for c in range(ATTN_CHAINS):
                rows = pl.ds(pl.multiple_of(j * tq + c * sub, sub), sub)
                nk = (c + 1) * sub if masked else tq
                kb = k_ref[0:nk, :]
                vb = v_ref[0:nk, :]
                qb = q_ref[rows, :]
                dob = do_ref[rows, :]
                s = _dot_nt(qb, kb)
                if masked:
                    ri = lax.broadcasted_iota(jnp.int32, (sub, nk), 0) + c * sub
                    s = jnp.where(ri >= lax.broadcasted_iota(jnp.int32, (sub, nk), 1), s, NEG)
                p = jnp.exp(s - lse_ref[rows, 0:1])
                dv_acc = dv_acc + grow(_dot_tn(p.astype(BF16), dob))
                dp = _dot_nt(dob, vb)
                ds = (p * (dp - dl_ref[rows, 0:1])).astype(BF16)
                dk_acc = dk_acc + grow(_dot_tn(ds, qb))
                dq_ref[rows, :] += _dot(ds, kb)
            return dk_acc, dv_acc

        carry = step((jnp.zeros((tq, HP), F32), jnp.zeros((tq, HP), F32)), ki, True)
        dk_acc, dv_acc = lax.fori_loop(ki + 1, nq, lambda j, cr: step(cr, j, False), carry)
        dk_ref[...] = dk_acc
        dv_ref[...] = dv_acc.astype(BF16)

    kspec = pl.BlockSpec((tq, HP), lambda h, j: (j, h))
    full = pl.BlockSpec((T, HP), lambda h, j: (0, h))
    return _call(
        body, name=name, grid=(H, nq),
        in_specs=[kspec, kspec, full, full, full, full], out_specs=[full, kspec, kspec],
        out_shape=[jax.ShapeDtypeStruct((T, H * HP), F32), jax.ShapeDtypeStruct((T, H * HP), F32),
                   jax.ShapeDtypeStruct((T, H * HP), BF16)],
        scratch_shapes=[], args=(k, v, q, do, lse, delta), exchange=exchange)


def _ret_consts(cc, hd):
    lg = math.log(1.0 - 2.0 ** (-5.0 - hd))
    diff = (lax.broadcasted_iota(jnp.int32, (cc, cc), 0) - lax.broadcasted_iota(jnp.int32, (cc, cc), 1)).astype(F32)
    decay = jnp.where(diff >= 0, jnp.exp(jnp.maximum(diff, 0.0) * lg), 0.0)
    idx = lax.broadcasted_iota(jnp.int32, (cc, 1), 0).astype(F32)
    zeta = jnp.exp((cc - 1.0 - idx) * lg)
    xi = jnp.exp((idx + 1.0) * lg)
    return decay, zeta, xi, math.exp(cc * lg)


def _ret_fwd(proj, pos, tab, *, name):
    T = proj.shape[0]
    cc = min(RET_TILE, T)
    n = T // cc

    def body(rq_ref, rk_ref, rv_ref, pos_ref, tab_ref, y_ref, yn_ref, rprev_ref, r_s):
        @pl.when(pl.program_id(0) == 0)
        def _():
            r_s[...] = jnp.zeros_like(r_s)

        cs = _rope_cs(pos_ref[...], tab_ref)
        for hd in range(RET_HEADS):
            sl = slice(hd * HP, (hd + 1) * HP)
            decay, zeta, xi, gc = _ret_consts(cc, hd)
            q = _rope(rq_ref[:, sl].astype(F32), cs, RET_DK // 2).astype(BF16)
            kf = _rope(rk_ref[:, sl].astype(F32), cs, RET_DK // 2) * (RET_DK ** -0.5)
            k = kf.astype(BF16)
            v = rv_ref[:, sl]
            r = r_s[hd]
            rprev_ref[0, hd] = r
            inner = (_dot_nt(q, k) * decay).astype(BF16)
            y = _dot(inner, v) + _dot(q, r.astype(BF16)) * xi
            r_s[hd] = r * gc + _dot_tn((kf * zeta).astype(BF16), v)
            y_ref[:, sl] = y
            mu = jnp.mean(y, axis=-1, keepdims=True)
            yc = y - mu
            var = jnp.mean(yc * yc, axis=-1, keepdims=True)
            yn_ref[:, sl] = (yc * lax.rsqrt(var + GN_EPS)).astype(BF16)

    def blk(j):
        return pl.BlockSpec((cc, RW), lambda i: (i, j))

    return pl.pallas_call(
        body, name=name, grid=(n,),
        in_specs=[blk(0), blk(1), blk(2), pl.BlockSpec((cc, 1), lambda i: (i, 0)),
                  pl.BlockSpec((8, LANES), lambda i: (0, 0))],
        out_specs=[blk(0), blk(0), pl.BlockSpec((1, RET_HEADS, HP, RET_DV), lambda i: (i, 0, 0, 0))],
        out_shape=[jax.ShapeDtypeStruct((T, RW), F32), jax.ShapeDtypeStruct((T, RW), BF16),
                   jax.ShapeDtypeStruct((n, RET_HEADS, HP, RET_DV), F32)],
        scratch_shapes=[pltpu.VMEM((RET_HEADS, HP, RET_DV), F32)],
        compiler_params=_params(("arbitrary",)),
    )(proj, proj, proj, pos, tab)


def _ret_bwd(dyn, y, proj, pos, tab, rprev, *, name):
    T = proj.shape[0]
    cc = min(RET_TILE, T)
    n = T // cc

    def body(dyn_ref, y_ref, rq_ref, rk_ref, rv_ref, pos_ref, tab_ref, rprev_ref,
             drq_ref, drk_ref, drv_ref, dr_s):
        @pl.when(pl.program_id(0) == 0)
        def _():
            dr_s[...] = jnp.zeros_like(dr_s)

        cs = _rope_cs(pos_ref[...], tab_ref)
        for hd in range(RET_HEADS):
            sl = slice(hd * HP, (hd + 1) * HP)
            decay, zeta, xi, gc = _ret_consts(cc, hd)
            q = _rope(rq_ref[:, sl].astype(F32), cs, RET_DK // 2).astype(BF16)
            kf = _rope(rk_ref[:, sl].astype(F32), cs, RET_DK // 2) * (RET_DK ** -0.5)
            k = kf.astype(BF16)
            v = rv_ref[:, sl]
            yv = y_ref[:, sl]
            mu = jnp.mean(yv, axis=-1, keepdims=True)
            yc = yv - mu
            rs = lax.rsqrt(jnp.mean(yc * yc, axis=-1, keepdims=True) + GN_EPS)
            yn = yc * rs
            dn = dyn_ref[:, sl]
            dy = rs * (dn - jnp.mean(dn, axis=-1, keepdims=True) - yn * jnp.mean(dn * yn, axis=-1, keepdims=True))
            dyb = dy.astype(BF16)
            dyx = (dy * xi).astype(BF16)
            dr = dr_s[hd]
            drb = dr.astype(BF16)
            inner = (_dot_nt(q, k) * decay).astype(BF16)
            da = (_dot_nt(dyb, v) * decay).astype(BF16)
            dv = _dot_tn(inner, dyb) + _dot((kf * zeta).astype(BF16), drb)
            dq = _dot(da, k) + _dot_nt(dyx, rprev_ref[0, hd].astype(BF16))
            dk = _dot_tn(da, q) + _dot_nt(v, drb) * zeta
            dr_s[hd] = dr * gc + _dot_tn(q, dyx)
            drq_ref[:, sl] = _rope(dq, cs, RET_DK // 2, inverse=True).astype(BF16)
            drk_ref[:, sl] = _rope(dk * (RET_DK ** -0.5), cs, RET_DK // 2, inverse=True).astype(BF16)
            drv_ref[:, sl] = dv.astype(BF16)

    def blk(j):
        return pl.BlockSpec((cc, RW), lambda i: (n - 1 - i, j))

    return pl.pallas_call(
        body, name=name, grid=(n,),
        in_specs=[blk(0), blk(0), blk(0), blk(1), blk(2), pl.BlockSpec((cc, 1), lambda i: (n - 1 - i, 0)),
                  pl.BlockSpec((8, LANES), lambda i: (0, 0)),
                  pl.BlockSpec((1, RET_HEADS, HP, RET_DV), lambda i: (n - 1 - i, 0, 0, 0))],
        out_specs=[blk(0), blk(0), blk(0)],
        out_shape=[jax.ShapeDtypeStruct((T, RW), BF16)] * 3,
        scratch_shapes=[pltpu.VMEM((RET_HEADS, HP, RET_DV), F32)],
        compiler_params=_params(("arbitrary",)),
    )(dyn, y, proj, proj, proj, pos, tab, rprev)


def _merge_fwd(o, yn, proj, gn_w, w_bm, w_br, w_out, h, post_w, *, name):
    T, D = h.shape
    tT = min(MERGE_TILE, T)
    g_blk = PROJ_FIXED // D

    def body(o_ref, yn_ref, rg_ref, gm_ref, gr_ref, gnw_ref, wbm_ref, wbr_ref, wout_ref, h_ref, post_ref,
             omla_ref, oret_ref, m_ref, ho_ref):
        o_mla = _dot(o_ref[...], wbm_ref[...])
        rg = rg_ref[...].astype(F32)
        gated = (rg * _sigmoid(rg) * (yn_ref[...].astype(F32) * gnw_ref[...])).astype(BF16)
        o_ret = _dot(gated, wbr_ref[...])
        omla_ref[...] = o_mla.astype(BF16)
        oret_ref[...] = o_ret.astype(BF16)
        merged = _sigmoid(gm_ref[...].astype(F32)) * o_mla + _sigmoid(gr_ref[...].astype(F32)) * o_ret
        m = _dot(merged.astype(BF16), wout_ref[...])
        m_ref[...] = m
        ho_ref[...] = h_ref[...] + _rms_fwd(m, post_ref[...])

    def full(r, c):
        return pl.BlockSpec((r, c), lambda i: (0, 0))

    def rows(c, j=0):
        return pl.BlockSpec((tT, c), lambda i: (i, j))

    return pl.pallas_call(
        body, name=name, grid=(T // tT,),
        in_specs=[rows(QW), rows(RW), rows(RW, 3), rows(D, g_blk), rows(D, g_blk + 1), full(1, RW),
                  full(QW, D), full(RW, D), full(D, D), rows(D), full(1, D)],
        out_specs=[rows(D), rows(D), rows(D), rows(D)],
        out_shape=[jax.ShapeDtypeStruct((T, D), BF16), jax.ShapeDtypeStruct((T, D), BF16),
                   jax.ShapeDtypeStruct((T, D), F32), jax.ShapeDtypeStruct((T, D), F32)],
        compiler_params=_params(("parallel",)),
    )(o, yn, proj, proj, proj, gn_w, w_bm, w_br, w_out, h, post_w)


def _merge_bwd(dho, m, post_w, omla, oret, proj, yn, gn_w, o, w_out, w_bm, w_br, *, name):
    T, D = dho.shape
    tT = min(MERGE_TILE, T)
    g_blk = PROJ_FIXED // D

    def body(dho_ref, m_ref, post_ref, omla_ref, oret_ref, rg_ref, gm_ref, gr_ref, yn_ref, gnw_ref, o_ref,
             wout_ref, wbm_ref, wbr_ref,
             dm_ref, merged_ref, dgm_ref, dgr_ref, domla_ref, do_ref, delta_ref, doret_ref, gated_ref,
             drg_ref, dyn_ref, gpost_ref, ggn_ref):
        @pl.when(pl.program_id(0) == 0)
        def _():
            gpost_ref[...] = jnp.zeros_like(gpost_ref)
            ggn_ref[...] = jnp.zeros_like(ggn_ref)

        dm, gp = _rms_bwd(m_ref[...], post_ref[...], dho_ref[...])
        gpost_ref[...] += gp
        dmb = dm.astype(BF16)
        dm_ref[...] = dmb
        dmerged = _dot_nt(dmb, wout_ref[...])
        o_mla = omla_ref[...].astype(F32)
        o_ret = oret_ref[...].astype(F32)
        sgm = _sigmoid(gm_ref[...].astype(F32))
        sgr = _sigmoid(gr_ref[...].astype(F32))
        merged_ref[...] = (sgm * o_mla + sgr * o_ret).astype(BF16)
        dgm_ref[...] = (dmerged * o_mla * sgm * (1.0 - sgm)).astype(BF16)
        dgr_ref[...] = (dmerged * o_ret * sgr * (1.0 - sgr)).astype(BF16)
        domla = (dmerged * sgm).astype(BF16)
        domla_ref[...] = domla
        do = _dot_nt(domla, wbm_ref[...])
        do_ref[...] = do.astype(BF16)
        for hd in range(MLA_HEADS):
            sl = slice(hd * HP, (hd + 1) * HP)
            d = jnp.sum(do[:, sl] * o_ref[:, sl].astype(F32), axis=-1, keepdims=True)
            delta_ref[:, sl] = jnp.broadcast_to(d, (tT, HP))
        doret = (dmerged * sgr).astype(BF16)
        doret_ref[...] = doret
        dgated = _dot_nt(doret, wbr_ref[...])
        rg = rg_ref[...].astype(F32)
        sg = _sigmoid(rg)
        srg = rg * sg
        ynv = yn_ref[...].astype(F32)
        yw = ynv * gnw_ref[...]
        gated_ref[...] = (srg * yw).astype(BF16)
        drg_ref[...] = (dgated * yw * (sg * (1.0 + rg * (1.0 - sg)))).astype(BF16)
        dgs = dgated * srg
        dyn_ref[...] = dgs * gnw_ref[...]
        ggn_ref[...] += jnp.sum(dgs * ynv, axis=0, keepdims=True)

    def full(r, c):
        return pl.BlockSpec((r, c), lambda i: (0, 0))

    def rows(c, j=0):
        return pl.BlockSpec((tT, c), lambda i: (i, j))

    return pl.pallas_call(
        body, name=name, grid=(T // tT,),
        in_specs=[rows(D), rows(D), full(1, D), rows(D), rows(D), rows(RW, 3), rows(D, g_blk), rows(D, g_blk + 1),
                  rows(RW), full(1, RW), rows(QW), full(D, D), full(QW, D), full(RW, D)],
        out_specs=[rows(D), rows(D), rows(D), rows(D), rows(D), rows(QW), rows(QW), rows(D), rows(RW),
                   rows(RW), rows(RW), full(1, D), full(1, RW)],
        out_shape=[jax.ShapeDtypeStruct((T, D), BF16)] * 5
        + [jax.ShapeDtypeStruct((T, QW), BF16), jax.ShapeDtypeStruct((T, QW), F32),
           jax.ShapeDtypeStruct((T, D), BF16), jax.ShapeDtypeStruct((T, RW), BF16),
           jax.ShapeDtypeStruct((T, RW), BF16), jax.ShapeDtypeStruct((T, RW), F32),
           jax.ShapeDtypeStruct((1, D), F32), jax.ShapeDtypeStruct((1, RW), F32)],
        compiler_params=_params(("arbitrary",)),
    )(dho, m, post_w, omla, oret, proj, proj, proj, yn, gn_w, o, w_out, w_bm, w_br)


def _mesh_pos():
    return lax.axis_index("x"), lax.axis_index("y"), lax.axis_index("c")


class _Gather:
    def __init__(self, shards):
        self.operands = list(shards)
        self.n = len(shards)
        self.out_shape = [jax.ShapeDtypeStruct((N_DEV,) + s.shape, s.dtype) for s in shards]
        self.scratch = [pltpu.SemaphoreType.DMA((7 * self.n,)), pltpu.SemaphoreType.DMA((7 * self.n,)),
                        pltpu.SemaphoreType.DMA((self.n,))]

    def phase(self, p, x_refs, out_refs, sems):
        send_sems, recv_sems, local_sems = sems
        x, y, c = _mesh_pos()
        me, sibling = (x, y, c), (x, y, 1 - c)
        chips = [(1 - x, y), (x, 1 - y), (1 - x, 1 - y)]

        def copy(w, k, block, to, src=None):
            slot = out_refs[w].at[4 * block[0] + 2 * block[1] + block[2]]
            return pltpu.make_async_remote_copy(
                src_ref=slot if src is None else src, dst_ref=slot,
                send_sem=send_sems.at[7 * w + k], recv_sem=recv_sems.at[7 * w + k],
                device_id=to, device_id_type=pl.DeviceIdType.MESH)

        for w in range(self.n):
            mine = pltpu.make_async_copy(x_refs[w], out_refs[w].at[4 * x + 2 * y + c], local_sems.at[w])
            first = [copy(w, 0, me, sibling, src=x_refs[w])]
            first += [copy(w, 1 + j, me, (*chip, c), src=x_refs[w]) for j, chip in enumerate(chips)]
            passed = [copy(w, 4 + j, (*chip, c), sibling) for j, chip in enumerate(chips)]
            if p == 0:
                mine.start()
                for cp in first:
                    cp.start()
            elif p == 1:
                for j, chip in enumerate(chips):
                    copy(w, 1 + j, (*chip, c), me).wait_recv()
                    passed[j].start()
            else:
                copy(w, 0, sibling, me).wait_recv()
                for j, chip in enumerate(chips):
                    copy(w, 4 + j, (*chip, 1 - c), me).wait_recv()
                for cp in first + passed:
                    cp.wait_send()
                mine.wait()


class _Scatter:
    def __init__(self, grads, whole=()):
        self.n_sliced = len(grads)
        self.operands = list(grads) + list(whole)
        self.n = len(self.operands)
        self.out_shape = [jax.ShapeDtypeStruct(g.shape, g.dtype) for g in grads]
        self.out_shape += [jax.ShapeDtypeStruct((N_DEV,) + a.shape, a.dtype) for a in whole]
        n_sem = (N_DEV - 1) * self.n
        self.scratch = [pltpu.SemaphoreType.DMA((n_sem,)), pltpu.SemaphoreType.DMA((n_sem,)),
                        pltpu.SemaphoreType.DMA((self.n,))]

    def phase(self, p, in_refs, out_refs, sems):
        if p == 1:
            return
        send_sems, recv_sems, local_sems = sems
        x, y, c = _mesh_pos()
        me = 4 * x + 2 * y + c

        def src(w, dev):
            return in_refs[w].at[dev] if w < self.n_sliced else in_refs[w]

        for w in range(self.n):
            own = pltpu.make_async_copy(src(w, me), out_refs[w].at[me], local_sems.at[w])
            sends, recvs = [], []
            for r in range(1, N_DEV):
                px = 1 - x if r & 4 else x
                py = 1 - y if r & 2 else y
                pc = 1 - c if r & 1 else c
                peer, pidx = (px, py, pc), 4 * px + 2 * py + pc
                k = (N_DEV - 1) * w + r - 1
                sends.append(pltpu.make_async_remote_copy(
                    src_ref=src(w, pidx), dst_ref=out_refs[w].at[me], send_sem=send_sems.at[k],
                    recv_sem=recv_sems.at[k], device_id=peer, device_id_type=pl.DeviceIdType.MESH))
                recvs.append(pltpu.make_async_remote_copy(
                    src_ref=src(w, me), dst_ref=out_refs[w].at[pidx], send_sem=send_sems.at[k],
                    recv_sem=recv_sems.at[k], device_id=peer, device_id_type=pl.DeviceIdType.MESH))
            if p == 0:
                own.start()
                for cp in sends:
                    cp.start()
            else:
                for cp in recvs:
                    cp.wait_recv()
                for cp in sends:
                    cp.wait_send()
                own.wait()


def _exchange_alone(ex, *, name):
    n = ex.n

    def body(*refs):
        for p in range(3):
            ex.phase(p, refs[:n], refs[n:2 * n], refs[2 * n:])

    anyspec = pl.BlockSpec(memory_space=pl.ANY)
    return pl.pallas_call(body, name=name, out_shape=ex.out_shape, in_specs=[anyspec] * n,
                          out_specs=[anyspec] * n, scratch_shapes=ex.scratch)(*ex.operands)


def _adam_step(w_ref, p_ref, m_ref, v_ref, g_ref, d_ref, nm_ref, nv_ref):
    g = p_ref[0].astype(F32)
    for j in range(1, N_DEV):
        g = g + p_ref[j].astype(F32)
    g_ref[...] = g
    nm = ADAM_B1 * m_ref[...] + (1.0 - ADAM_B1) * g
    nv = ADAM_B2 * v_ref[...] + (1.0 - ADAM_B2) * (g * g)
    nm_ref[...] = nm
    nv_ref[...] = nv
    m_hat = nm / (1.0 - ADAM_B1 ** ADAM_STEP)
    v_hat = nv / (1.0 - ADAM_B2 ** ADAM_STEP)
    d_ref[...] = -ADAM_LR * (m_hat / (jnp.sqrt(v_hat) + ADAM_EPS) + ADAM_WD * w_ref[...])


def _adamw_vectors(ws, parts, ms, vs, *, name):
    n = len(ws)

    def body(*refs):
        w_refs, p_refs, m_refs, v_refs = (refs[i * n:(i + 1) * n] for i in range(4))
        outs = refs[4 * n:]
        for i in range(n):
            _adam_step(w_refs[i], p_refs[i], m_refs[i], v_refs[i], *outs[4 * i:4 * i + 4])

    return pl.pallas_call(
        body, name=name,
        out_shape=[jax.ShapeDtypeStruct(w.shape, F32) for w in ws for _ in range(4)],
    )(*ws, *parts, *ms, *vs)


def _adamw(w, parts, m, v, *, name):
    G, R, n = w.shape
    tn = 256 if (n > 256 and n % 256 == 0) else n
    tr = R
    for t in range(16, R, 16):
        if R % t == 0 and t * tn <= 160 * 1024:
            tr = t
    if R * tn <= 160 * 1024:
        tr = R

    def body(w_ref, p_ref, m_ref, v_ref, g_ref, d_ref, nm_ref, nv_ref):
        _adam_step(w_ref, p_ref, m_ref, v_ref, g_ref, d_ref, nm_ref, nv_ref)

    blk = pl.BlockSpec((None, tr, tn), lambda g, i, j: (g, i, j))
    return pl.pallas_call(
        body, name=name, grid=(G, R // tr, n // tn),
        in_specs=[blk, pl.BlockSpec((N_DEV, None, tr, tn), lambda g, i, j: (0, g, i, j)), blk, blk],
        out_specs=[blk, blk, blk, blk],
        out_shape=[jax.ShapeDtypeStruct((G, R, n), F32)] * 4,
        compiler_params=_params(("parallel", "parallel", "parallel")),
    )(w, parts, m, v)


def _pad_last(a, width):
    return jnp.pad(a, [(0, 0)] * (a.ndim - 1) + [(0, width - a.shape[-1])])


def _cols_of(g):
    return g.transpose(1, 0, 2).reshape(g.shape[1], N_DEV * g.shape[2])


def _col_shards(w):
    return w.reshape(w.shape[0], N_DEV, w.shape[1] // N_DEV).transpose(1, 0, 2)


def kernel(x, positions, ffn1_pre_w, ffn1_w1, ffn1_w2, ffn1_post_w, mix_pre_w, w_in, mla_q_norm_w, mla_w_uq, mla_kv_norm_w, mla_w_ukv, ret_gn_w, w_branch_mla, w_branch_ret, w_out, mix_post_w, ffn2_pre_w, ffn2_w1, ffn2_w2, ffn2_post_w, loss_target, m_ffn1_pre_w, m_ffn1_w1, m_ffn1_w2, m_ffn1_post_w, m_mix_pre_w, m_w_in, m_mla_q_norm_w, m_mla_w_uq, m_mla_kv_norm_w, m_mla_w_ukv, m_ret_gn_w, m_w_branch_mla, m_w_branch_ret, m_w_out, m_mix_post_w, m_ffn2_pre_w, m_ffn2_w1, m_ffn2_w2, m_ffn2_post_w, v_ffn1_pre_w, v_ffn1_w1, v_ffn1_w2, v_ffn1_post_w, v_mix_pre_w, v_w_in, v_mla_q_norm_w, v_mla_w_uq, v_mla_kv_norm_w, v_mla_w_ukv, v_ret_gn_w, v_w_branch_mla, v_w_branch_ret, v_w_out, v_mix_post_w, v_ffn2_pre_w, v_ffn2_w1, v_ffn2_w2, v_ffn2_post_w):
    T, D = x.shape[1], x.shape[2]
    h0 = x[0]
    tgt = loss_target[0]
    pos = positions.reshape(T, 1).astype(F32)

    big = [("ffn1_w1", ffn1_w1, m_ffn1_w1, v_ffn1_w1), ("ffn1_w2", ffn1_w2, m_ffn1_w2, v_ffn1_w2),
           ("w_in", w_in, m_w_in, v_w_in), ("mla_w_uq", mla_w_uq, m_mla_w_uq, v_mla_w_uq),
           ("mla_w_ukv", mla_w_ukv, m_mla_w_ukv, v_mla_w_ukv),
           ("w_branch_mla", w_branch_mla, m_w_branch_mla, v_w_branch_mla),
           ("w_branch_ret", w_branch_ret, m_w_branch_ret, v_w_branch_ret),
           ("w_out", w_out, m_w_out, v_w_out),
           ("ffn2_w1", ffn2_w1, m_ffn2_w1, v_ffn2_w1), ("ffn2_w2", ffn2_w2, m_ffn2_w2, v_ffn2_w2)]
    small = [("ffn1_pre_w", ffn1_pre_w, m_ffn1_pre_w, v_ffn1_pre_w), ("ffn1_post_w", ffn1_post_w, m_ffn1_post_w, v_ffn1_post_w),
             ("mix_pre_w", mix_pre_w, m_mix_pre_w, v_mix_pre_w), ("mla_q_norm_w", mla_q_norm_w, m_mla_q_norm_w, v_mla_q_norm_w),
             ("mla_kv_norm_w", mla_kv_norm_w, m_mla_kv_norm_w, v_mla_kv_norm_w), ("ret_gn_w", ret_gn_w, m_ret_gn_w, v_ret_gn_w),
             ("mix_post_w", mix_post_w, m_mix_post_w, v_mix_post_w), ("ffn2_pre_w", ffn2_pre_w, m_ffn2_pre_w, v_ffn2_pre_w),
             ("ffn2_post_w", ffn2_post_w, m_ffn2_post_w, v_ffn2_post_w)]

    half = ffn1_w2.shape[1]
    hp = -(-half // LANES) * LANES

    def rows_view(w):
        return w[0].T

    def send_w1(w):
        return jnp.pad(rows_view(w).reshape(2, half, D), ((0, 0), (0, hp - half), (0, 0))).reshape(2 * hp, D).astype(BF16)

    def send_w2(w):
        return jnp.pad(w[0], ((0, hp - half), (0, 0))).astype(BF16)

    mixer = ["w_in", "mla_w_uq", "mla_w_ukv", "w_branch_mla", "w_branch_ret", "w_out"]
    uq_w = MLA_NOPE + MLA_ROPE
    mixer_send = [rows_view(w_in).astype(BF16), jnp.pad(rows_view(mla_w_uq), ((0, HP - uq_w), (0, 0))).astype(BF16),
                  mla_w_ukv[0].astype(BF16), w_branch_mla[0].astype(BF16), w_branch_ret[0].astype(BF16),
                  w_out[0].astype(BF16)]

    w1a, w2a = _exchange_alone(_Gather([send_w1(ffn1_w1), send_w2(ffn1_w2)]), name="gather_ffn1")
    w2a = w2a.reshape(N_DEV // 2, 2 * hp, D)
    u1, f1, h1, *got = _ffn_fwd(h0, ffn1_pre_w, w1a, w2a, ffn1_post_w, None, name="ffn1_fwd_gather_mixer",
                                exchange=_Gather(mixer_send))
    fw = dict(zip(mixer, got))

    wi = fw["w_in"].reshape(-1, D)
    cq_w, ckv_w, kr_w = wi[0:384], wi[384:640], wi[640:672]
    rq_w, rk_w = wi[672:928], wi[928:1184]
    rv_w, rg_w = wi[1184:1696], wi[1696:2208]
    gm_w, gr_w = wi[2208:2208 + D], wi[2208 + D:2208 + 2 * D]
    zer = lambda n: jnp.zeros((n, D), BF16)
    head_rows = lambda a, h: jnp.pad(a.reshape(h, -1, D), ((0, 0), (0, HP - a.shape[0] // h), (0, 0))).reshape(h * HP, D)
    w_in_p = jnp.concatenate([head_rows(rq_w, RET_HEADS), head_rows(rk_w, RET_HEADS), rv_w, rg_w,
                              cq_w, ckv_w, zer(MLA_NOPE), kr_w, zer(HP - MLA_NOPE - MLA_ROPE), zer(AW - 768),
                              gm_w, gr_w], axis=0)
    w_uq_p = fw["mla_w_uq"].reshape(QW, MLA_Q_RANK)
    ukv = fw["mla_w_ukv"].transpose(1, 0, 2)
    w_kv_p = jnp.concatenate([_pad_last(ukv[:, :, :MLA_NOPE], HP).reshape(MLA_KV_RANK, QW),
                              _pad_last(ukv[:, :, MLA_NOPE:], HP).reshape(MLA_KV_RANK, QW)], axis=1)
    w_bm_p = jnp.pad(_cols_of(fw["w_branch_mla"]).reshape(MLA_HEADS, MLA_V, D),
                     ((0, 0), (0, HP - MLA_V), (0, 0))).reshape(QW, D)
    w_br, w_o = _cols_of(fw["w_branch_ret"]), fw["w_out"].reshape(D, D)
    tab_mla = _rope_table(MLA_NOPE, MLA_ROPE // 2)
    tab_ret = _rope_table(0, RET_DK // 2)

    proj, a1 = _rms_matmul(h1, mix_pre_w, w_in_p, name="mixer_in_proj")
    q, k, v, qn, kvn = _mla_prep_fwd(proj, pos, mla_q_norm_w, mla_kv_norm_w, w_uq_p, w_kv_p, tab_mla, name="mla_prep_fwd")
    o, lse, w1b, w2b = _flash_fwd(q, k, v, name="mla_attn_fwd_gather_ffn2",
                                  exchange=_Gather([send_w1(ffn2_w1), send_w2(ffn2_w2)]))
    w2b = w2b.reshape(N_DEV // 2, 2 * hp, D)
    ypre, yn, rprev = _ret_fwd(proj, pos, tab_ret, name="retention_fwd")
    omla, oret, m, h2 = _merge_fwd(o, yn, proj, ret_gn_w, w_bm_p, w_br, w_o, h1, mix_post_w, name="merge_fwd")
    u2, f2, _, dy, lossp = _ffn_fwd(h2, ffn2_pre_w, w1b, w2b, ffn2_post_w, tgt, name="ffn2_fwd_loss")
    loss = lax.psum(jnp.sum(lossp[::8, 0]), ("x", "y", "c"))

    def grad(x, dy, tag, exchange=None):
        return _matmul_tn(x if x.ndim == 3 else x[None], dy if dy.ndim == 3 else dy[None], name=tag, exchange=exchange)

    g2, du2, df2, a2, dh2, gpost2, gpre2 = _ffn_bwd(dy, f2, ffn2_post_w, h2, ffn2_pre_w, u2, w2b, w1b, name="ffn2_bwd")
    dw1b, = grad(du2.reshape(N_DEV, T, 2 * hp), a2, "ffn2_dw1")
    dw2b = grad(g2, df2, "ffn2_dw2")[0].reshape(N_DEV, hp, D)
    (dmb, merged, dgm, dgr, domla, do, delta, doret, gated, drg, dyn, gpostm, ggn) = _merge_bwd(
        dh2, m, mix_post_w, omla, oret, proj, yn, ret_gn_w, o, w_o, w_bm_p, w_br, name="merge_bwd")
    dw_out = grad(merged, dmb, "dw_out")[0][0]
    dw_bm_p = grad(o, domla, "dw_branch_mla")[0][0]
    dw_br = grad(gated, doret, "dw_branch_ret")[0][0]
    dq, dk, dv, *recv_ffn2 = _flash_bwd(q, k, v, do, lse, delta, name="mla_attn_bwd_scatter_ffn2",
                                        exchange=_Scatter([dw1b, dw2b]))
    da, dql, dkvl, gqn, gkvn = _mla_prep_bwd(dq, dk, dv, proj, pos, mla_q_norm_w, mla_kv_norm_w, w_uq_p, w_kv_p, tab_mla, name="mla_prep_bwd")
    dw_uq_p = grad(dql, qn, "dw_uq")[0][0]
    dw_kv_p = grad(kvn, dkvl, "dw_ukv")[0][0]
    drq, drk, drv = _ret_bwd(dyn, ypre, proj, pos, tab_ret, rprev, name="retention_bwd")
    dproj = jnp.concatenate([drq, drk, drv, drg, da, dgm, dgr], axis=1)
    dw_in_p = grad(dproj, a1, "dw_in")[0][0]
    dh1, gmixpre = _proj_bwd(dproj, w_in_p, h1, mix_pre_w, dh2, name="mixer_in_bwd")

    unhead = lambda a, h, wd: a.reshape(h, HP, D)[:, :wd].reshape(h * wd, D)
    c0 = 4 * RW
    dw_in = jnp.concatenate([
        dw_in_p[c0:c0 + 384], dw_in_p[c0 + 384:c0 + 640], dw_in_p[c0 + 640 + MLA_NOPE:c0 + 640 + MLA_NOPE + MLA_ROPE],
        unhead(dw_in_p[0:RW], RET_HEADS, RET_DK), unhead(dw_in_p[RW:2 * RW], RET_HEADS, RET_DK),
        dw_in_p[2 * RW:3 * RW], dw_in_p[3 * RW:4 * RW],
        dw_in_p[PROJ_FIXED:PROJ_FIXED + D], dw_in_p[PROJ_FIXED + D:PROJ_FIXED + 2 * D]], axis=0).reshape(N_DEV, -1, D)
    dw_uq = dw_uq_p.reshape(MLA_HEADS, HP, MLA_Q_RANK)[:, :uq_w]
    dkp = dw_kv_p[:, :QW].reshape(MLA_KV_RANK, MLA_HEADS, HP)[:, :, :MLA_NOPE]
    dvp = dw_kv_p[:, QW:].reshape(MLA_KV_RANK, MLA_HEADS, HP)[:, :, :MLA_V]
    dw_ukv = jnp.concatenate([dkp, dvp], axis=2).transpose(1, 0, 2)
    dw_bm = dw_bm_p.reshape(MLA_HEADS, HP, D)[:, :MLA_V].reshape(MLA_HEADS * MLA_V, D)

    mixer_grads = [dw_in, dw_uq, dw_ukv, _col_shards(dw_bm), _col_shards(dw_br),
                   dw_out.reshape(N_DEV, D // N_DEV, D)]
    g1, du1, df1, a0, dx, gpost1, gpre1, *recv_mixer = _ffn_bwd(
        dh1, f1, ffn1_post_w, h0, ffn1_pre_w, u1, w2a, w1a, name="ffn1_bwd_scatter_mixer", exchange=_Scatter(mixer_grads))
    dw2a = grad(g1, df1, "ffn1_dw2")[0].reshape(N_DEV, hp, D)
    dw1a, recv_w2a = grad(du1.reshape(N_DEV, T, 2 * hp), a0, "ffn1_dw1_scatter_dw2", exchange=_Scatter([dw2a]))

    small_g = {"ffn1_pre_w": gpre1, "ffn1_post_w": gpost1, "mix_pre_w": gmixpre, "mla_q_norm_w": gqn,
               "mla_kv_norm_w": gkvn, "ret_gn_w": ggn, "mix_post_w": gpostm, "ffn2_pre_w": gpre2, "ffn2_post_w": gpost2}
    recv_w1a, *small_parts = _exchange_alone(_Scatter([dw1a], whole=[small_g[nm] for nm, *_ in small]),
                                             name="scatter_ffn1_dw1")
    parts = dict(zip(mixer, recv_mixer))
    parts.update(ffn1_w1=recv_w1a, ffn1_w2=recv_w2a, ffn2_w1=recv_ffn2[0], ffn2_w2=recv_ffn2[1])
    as_is = (lambda a: a, lambda p: p[:, None], lambda a: a)
    views = {nm: as_is for nm, *_ in big}
    for nm in ("ffn1_w1", "ffn2_w1"):
        views[nm] = (lambda a: rows_view(a).reshape(2, half, D), lambda p: p.reshape(N_DEV, 2, hp, D),
                     lambda a: a.reshape(2 * half, D).T[None])
    for nm in ("w_in", "mla_w_uq"):
        views[nm] = (lambda a: rows_view(a)[None], lambda p: p[:, None], lambda a: a[0].T[None])
    big_out = {}
    for nm, w, m_, v_ in big:
        to_view, parts_view, back = views[nm]
        big_out[nm] = [back(a) for a in _adamw(to_view(w), parts_view(parts[nm]), to_view(m_), to_view(v_),
                                               name="adamw_" + nm)]
    small_out = _adamw_vectors([w for _, w, _, _ in small], small_parts, [a for _, _, a, _ in small],
                               [a for _, _, _, a in small], name="adamw_replicated")

    order = ["ffn1_pre_w", "ffn1_w1", "ffn1_w2", "ffn1_post_w", "mix_pre_w", "w_in", "mla_q_norm_w", "mla_w_uq",
             "mla_kv_norm_w", "mla_w_ukv", "ret_gn_w", "w_branch_mla", "w_branch_ret", "w_out", "mix_post_w",
             "ffn2_pre_w", "ffn2_w1", "ffn2_w2", "ffn2_post_w"]
    outs = [loss, dx[None]]
    for i in range(4):
        both = {nm: big_out[nm][i] for nm in big_out}
        both.update({nm: small_out[4 * j + i] for j, (nm, *_) in enumerate(small)})
        outs += [both[nm] for nm in order]
    return tuple(outs)
```

```python
import math

import numpy as np
import jax
import jax.numpy as jnp
from jax import lax
from jax.experimental import pallas as pl
from jax.experimental.pallas import tpu as pltpu

F32, BF16 = jnp.float32, jnp.bfloat16

MLA_HEADS, MLA_NOPE, MLA_ROPE, MLA_V = 8, 64, 32, 64
MLA_Q_RANK, MLA_KV_RANK = 384, 256
RET_HEADS, RET_DK, RET_DV = 4, 64, 128
ROPE_BASE, NORM_EPS, GN_EPS = 10000.0, 1e-6, 1e-6
ADAM_LR, ADAM_B1, ADAM_B2, ADAM_EPS, ADAM_WD, ADAM_STEP = 0.001, 0.9, 0.999, 1e-08, 0.01, 10
ATTN_SCALE = 1.0 / math.sqrt(MLA_NOPE + MLA_ROPE)

N_DEV = 8
LANES = 128
HP = LANES
QW = MLA_HEADS * HP
RW = RET_HEADS * HP
AW = 1024
PROJ_FIXED = 4 * RW + AW
NEG = -1e30

TOKEN_TILE = 512
ATTN_TILE = 1024
ATTN_CHAINS = 2
FFN_CHAINS = 2
RET_TILE = 256
PROJ_TILE_CAP = 2560
GRAD_TILE_CAP = 1408
GRAD_TOKEN_TILE = 2048
MERGE_TILE = 256
VMEM_LIMIT = 56 * 1024 * 1024


def _tile(n, cap, mult=LANES):
    if n <= cap:
        return n
    best = None
    for t in range(mult, cap + 1, mult):
        if n % t == 0:
            best = t
    assert best is not None, (n, cap, mult)
    return best


def _params(sem):
    return pltpu.CompilerParams(dimension_semantics=sem, vmem_limit_bytes=VMEM_LIMIT)


def _dot(a, b):
    return lax.dot_general(a, b, (((1,), (0,)), ((), ())), preferred_element_type=F32)


def _dot_nt(a, b):
    return lax.dot_general(a, b, (((1,), (1,)), ((), ())), preferred_element_type=F32)


def _dot_tn(a, b):
    return lax.dot_general(a, b, (((0,), (0,)), ((), ())), preferred_element_type=F32)


def _sigmoid(x):
    return pl.reciprocal(1.0 + jnp.exp(-x), approx=True)


def _rms_fwd(x, w):
    r = lax.rsqrt(jnp.mean(x * x, axis=-1, keepdims=True) + NORM_EPS)
    return x * r * w


def _rms_bwd(x, w, dy):
    r = lax.rsqrt(jnp.mean(x * x, axis=-1, keepdims=True) + NORM_EPS)
    xh = x * r
    g = dy * w
    dx = r * (g - xh * jnp.mean(g * xh, axis=-1, keepdims=True))
    return dx, jnp.sum(dy * xh, axis=0, keepdims=True)


def _rope_table(first, half):
    inv = (np.float32(ROPE_BASE) ** (-(np.arange(half, dtype=np.float32) / np.float32(half)))).astype(np.float32)
    tab = np.zeros((8, LANES), np.float32)
    tab[0, first:first + half] = inv
    tab[0, first + half:first + 2 * half] = inv
    tab[1, first:first + half] = -1.0
    tab[2, first + half:first + 2 * half] = 1.0
    return jnp.asarray(tab)


def _rope_cs(pos, tab_ref):
    ang = pos * tab_ref[0:1, :]
    s = jnp.sin(ang)
    return jnp.cos(ang), s * tab_ref[1:2, :], s * tab_ref[2:3, :]


def _rope(x, cs, half, inverse=False):
    c, s1, s2 = cs
    a = pltpu.roll(x, LANES - half, 1) * s1 + pltpu.roll(x, half, 1) * s2
    return x * c - a if inverse else x * c + a


def _call(body, *, name, grid, in_specs, out_specs, out_shape, scratch_shapes, args, exchange=None):
    sem = ("arbitrary",) * len(grid)
    if exchange is None:
        return pl.pallas_call(body, name=name, grid=grid, in_specs=in_specs, out_specs=out_specs,
                              out_shape=out_shape, scratch_shapes=scratch_shapes, compiler_params=_params(sem))(*args)
    n_in, n_out, e = len(in_specs), len(out_specs), exchange.n
    total = math.prod(grid)

    def carried(*refs):
        own = refs[:n_in] + refs[n_in + e:n_in + e + n_out] + refs[n_in + 2 * e + n_out:len(refs) - 3]
        ex_refs = (refs[n_in:n_in + e], refs[n_in + e + n_out:n_in + 2 * e + n_out], refs[len(refs) - 3:])
        step = pl.program_id(0)
        for d in range(1, len(grid)):
            step = step * grid[d] + pl.program_id(d)

        @pl.when(step == 0)
        def _():
            exchange.phase(0, *ex_refs)

        @pl.when(step == total // 2)
        def _():
            exchange.phase(1, *ex_refs)

        body(*own)

        @pl.when(step == total - 1)
        def _():
            exchange.phase(2, *ex_refs)

    anyspec = pl.BlockSpec(memory_space=pl.ANY)
    return pl.pallas_call(
        carried, name=name, grid=grid, in_specs=list(in_specs) + [anyspec] * e,
        out_specs=list(out_specs) + [anyspec] * e, out_shape=list(out_shape) + exchange.out_shape,
        scratch_shapes=list(scratch_shapes) + exchange.scratch, compiler_params=_params(sem),
    )(*args, *exchange.operands)


def _ffn_fwd(h, pre_w, w1, w2, post_w, target, *, name, exchange=None):
    T, D = h.shape
    nk, ck = w2.shape[0], w2.shape[1]
    tT = min(TOKEN_TILE, T)
    nT = T // tT
    with_loss = target is not None

    def body(*refs):
        if with_loss:
            (h_ref, pre_ref, w1g_ref, w1u_ref, w2_ref, post_ref, tgt_ref,
             u_ref, f_ref, ho_ref, dy_ref, loss_ref, a_s, acc) = refs
        else:
            (h_ref, pre_ref, w1g_ref, w1u_ref, w2_ref, post_ref,
             u_ref, f_ref, ho_ref, a_s, acc) = refs
        k = pl.program_id(1)

        @pl.when(k == 0)
        def _():
            a_s[...] = _rms_fwd(h_ref[...], pre_ref[...]).astype(BF16)
            acc[...] = jnp.zeros_like(acc)

        for c in range(FFN_CHAINS):
            rs = slice(c * (tT // FFN_CHAINS), (c + 1) * (tT // FFN_CHAINS))
            a = a_s[rs, :]
            ug = _dot_nt(a, w1g_ref[...])
            uu = _dot_nt(a, w1u_ref[...])
            u_ref[0, rs, :] = ug.astype(BF16)
            u_ref[1, rs, :] = uu.astype(BF16)
            acc[rs, :] += _dot((ug * _sigmoid(ug) * uu).astype(BF16), w2_ref[...])

        @pl.when(k == nk - 1)
        def _():
            f = acc[...]
            f_ref[...] = f
            ho = h_ref[...] + 0.5 * _rms_fwd(f, post_ref[...])
            ho_ref[...] = ho
            if with_loss:
                e = ho - tgt_ref[...]
                dy_ref[...] = e * (1.0 / D)
                loss_ref[...] = jnp.full(loss_ref.shape, (0.5 / D) * jnp.sum(e * e), F32)

    row = pl.BlockSpec((tT, D), lambda i, k: (i, 0))
    vec = pl.BlockSpec((1, D), lambda i, k: (0, 0))
    in_specs = [row, vec,
                pl.BlockSpec((None, ck, D), lambda i, k: (k, 0, 0)),
                pl.BlockSpec((None, ck, D), lambda i, k: (nk + k, 0, 0)),
                pl.BlockSpec((None, ck, D), lambda i, k: (k, 0, 0)),
                vec]
    out_shape = [jax.ShapeDtypeStruct((2, nk, T, ck), BF16),
                 jax.ShapeDtypeStruct((T, D), F32),
                 jax.ShapeDtypeStruct((T, D), F32)]
    out_specs = [pl.BlockSpec((2, None, tT, ck), lambda i, k: (0, k, i, 0)), row, row]
    args = [h, pre_w, w1, w1, w2, post_w]
    if with_loss:
        in_specs.append(row)
        args.append(target)
        out_shape += [jax.ShapeDtypeStruct((T, D), F32), jax.ShapeDtypeStruct((nT * 8, LANES), F32)]
        out_specs += [row, pl.BlockSpec((8, LANES), lambda i, k: (i, 0))]
    return _call(body, name=name, grid=(nT, nk), in_specs=in_specs, out_specs=out_specs, out_shape=out_shape,
                 scratch_shapes=[pltpu.VMEM((tT, D), BF16), pltpu.VMEM((tT, D), F32)], args=args, exchange=exchange)


def _ffn_bwd(dho, f, post_w, h, pre_w, u, w2, w1, *, name, exchange=None):
    T, D = h.shape
    nk, ck = w2.shape[0], w2.shape[1]
    tT = min(TOKEN_TILE, T)
    nT = T // tT

    def body(dho_ref, f_ref, post_ref, h_ref, pre_ref, u_ref, w2_ref, w1g_ref, w1u_ref,
             g_ref, du_ref, df_ref, a_ref, dh_ref, gpost_ref, gpre_ref, df_s, da_acc):
        i, k = pl.program_id(0), pl.program_id(1)

        @pl.when(jnp.logical_and(i == 0, k == 0))
        def _():
            gpost_ref[...] = jnp.zeros_like(gpost_ref)
            gpre_ref[...] = jnp.zeros_like(gpre_ref)

        @pl.when(k == 0)
        def _():
            dx, dw = _rms_bwd(f_ref[...], post_ref[...], 0.5 * dho_ref[...])
            dfb = dx.astype(BF16)
            df_s[...] = dfb
            df_ref[...] = dfb
            gpost_ref[...] += dw
            a_ref[...] = _rms_fwd(h_ref[...], pre_ref[...]).astype(BF16)
            da_acc[...] = jnp.zeros_like(da_acc)

        groups = [slice(c * (tT // FFN_CHAINS), (c + 1) * (tT // FFN_CHAINS)) for c in range(FFN_CHAINS)]
        dgs = [_dot_nt(df_s[rs, :], w2_ref[...]) for rs in groups]
        for rs, dg in zip(groups, dgs):
            ug = u_ref[0, rs, :].astype(F32)
            uu = u_ref[1, rs, :].astype(F32)
            sg = _sigmoid(ug)
            sl = ug * sg
            g_ref[rs, :] = (sl * uu).astype(BF16)
            dug = (dg * uu * (sg + sl * (1.0 - sg))).astype(BF16)
            duu = (dg * sl).astype(BF16)
            du_ref[0, rs, :] = dug
            du_ref[1, rs, :] = duu
            da_acc[rs, :] += _dot(dug, w1g_ref[...]) + _dot(duu, w1u_ref[...])

        @pl.when(k == nk - 1)
        def _():
            dx, dw = _rms_bwd(h_ref[...], pre_ref[...], da_acc[...])
            dh_ref[...] = dho_ref[...] + dx
            gpre_ref[...] += dw

    row = pl.BlockSpec((tT, D), lambda i, k: (i, 0))
    vec = pl.BlockSpec((1, D), lambda i, k: (0, 0))
    return _call(
        body, name=name, grid=(nT, nk),
        in_specs=[row, row, vec, row, vec,
                  pl.BlockSpec((2, None, tT, ck), lambda i, k: (0, k, i, 0)),
                  pl.BlockSpec((None, ck, D), lambda i, k: (k, 0, 0)),
                  pl.BlockSpec((None, ck, D), lambda i, k: (k, 0, 0)),
                  pl.BlockSpec((None, ck, D), lambda i, k: (nk + k, 0, 0))],
        out_specs=[pl.BlockSpec((None, tT, ck), lambda i, k: (k, i, 0)),
                   pl.BlockSpec((2, None, tT, ck), lambda i, k: (0, k, i, 0)),
                   row, row, row, vec, vec],
        out_shape=[jax.ShapeDtypeStruct((nk, T, ck), BF16),
                   jax.ShapeDtypeStruct((2, nk, T, ck), BF16),
                   jax.ShapeDtypeStruct((T, D), BF16),
                   jax.ShapeDtypeStruct((T, D), BF16),
                   jax.ShapeDtypeStruct((T, D), F32),
                   jax.ShapeDtypeStruct((1, D), F32),
                   jax.ShapeDtypeStruct((1, D), F32)],
        scratch_shapes=[pltpu.VMEM((tT, D), BF16), pltpu.VMEM((tT, D), F32)],
        args=(dho, f, post_w, h, pre_w, u, w2, w1, w1), exchange=exchange)


def _matmul_tn(x, dy, *, name, exchange=None):
    Px, T, K = x.shape
    Py, _, N = dy.shape
    P = max(Px, Py)
    tT, tK, tN = min(GRAD_TOKEN_TILE, T), _tile(K, GRAD_TILE_CAP), _tile(N, GRAD_TILE_CAP)
    nt = T // tT

    def body(x_ref, dy_ref, o_ref, acc):
        t = pl.program_id(3)

        @pl.when(t == 0)
        def _():
            acc[...] = jnp.zeros_like(acc)

        acc[...] += _dot_tn(x_ref[...], dy_ref[...])

        @pl.when(t == nt - 1)
        def _():
            o_ref[...] = acc[...].astype(BF16)

    return _call(
        body, name=name, grid=(P, K // tK, N // tN, nt),
        in_specs=[pl.BlockSpec((None, tT, tK), lambda p, a, b, t: (p if Px > 1 else 0, t, a)),
                  pl.BlockSpec((None, tT, tN), lambda p, a, b, t: (p if Py > 1 else 0, t, b))],
        out_specs=[pl.BlockSpec((None, tK, tN), lambda p, a, b, t: (p, a, b))],
        out_shape=[jax.ShapeDtypeStruct((P, K, N), BF16)],
        scratch_shapes=[pltpu.VMEM((tK, tN), F32)], args=(x, dy), exchange=exchange)


def _rms_matmul(h, wn, w, *, name):
    T, D = h.shape
    N = w.shape[0]
    tT, tN = min(TOKEN_TILE, T), _tile(N, PROJ_TILE_CAP)

    def body(h_ref, wn_ref, w_ref, y_ref, a_ref):
        @pl.when(pl.program_id(1) == 0)
        def _():
            a_ref[...] = _rms_fwd(h_ref[...], wn_ref[...]).astype(BF16)

        y_ref[...] = _dot_nt(a_ref[...], w_ref[...]).astype(BF16)

    return pl.pallas_call(
        body, name=name, grid=(T // tT, N // tN),
        in_specs=[pl.BlockSpec((tT, D), lambda i, j: (i, 0)),
                  pl.BlockSpec((1, D), lambda i, j: (0, 0)),
                  pl.BlockSpec((tN, D), lambda i, j: (j, 0))],
        out_specs=[pl.BlockSpec((tT, tN), lambda i, j: (i, j)),
                   pl.BlockSpec((tT, D), lambda i, j: (i, 0))],
        out_shape=[jax.ShapeDtypeStruct((T, N), BF16), jax.ShapeDtypeStruct((T, D), BF16)],
        compiler_params=_params(("parallel", "arbitrary")),
    )(h, wn, w)


def _proj_bwd(dproj, w, h, wn, dres, *, name):
    T, D = h.shape
    N = w.shape[0]
    tT, tN = min(TOKEN_TILE, T), _tile(N, PROJ_TILE_CAP)
    nn = N // tN

    def body(dp_ref, w_ref, h_ref, wn_ref, dres_ref, dh_ref, gw_ref, acc):
        i, j = pl.program_id(0), pl.program_id(1)

        @pl.when(jnp.logical_and(i == 0, j == 0))
        def _():
            gw_ref[...] = jnp.zeros_like(gw_ref)

        @pl.when(j == 0)
        def _():
            acc[...] = jnp.zeros_like(acc)

        acc[...] += _dot(dp_ref[...], w_ref[...])

        @pl.when(j == nn - 1)
        def _():
            dx, dw = _rms_bwd(h_ref[...], wn_ref[...], acc[...])
            dh_ref[...] = dres_ref[...] + dx
            gw_ref[...] += dw

    row = pl.BlockSpec((tT, D), lambda i, j: (i, 0))
    vec = pl.BlockSpec((1, D), lambda i, j: (0, 0))
    return pl.pallas_call(
        body, name=name, grid=(T // tT, nn),
        in_specs=[pl.BlockSpec((tT, tN), lambda i, j: (i, j)),
                  pl.BlockSpec((tN, D), lambda i, j: (j, 0)), row, vec, row],
        out_specs=[row, vec],
        out_shape=[jax.ShapeDtypeStruct((T, D), F32), jax.ShapeDtypeStruct((1, D), F32)],
        scratch_shapes=[pltpu.VMEM((tT, D), F32)],
        compiler_params=_params(("arbitrary", "arbitrary")),
    )(dproj, w, h, wn, dres)


def _mla_prep_fwd(proj, pos, qn_w, kvn_w, w_uq, w_kv, tab, *, name):
    T = proj.shape[0]
    tT = min(TOKEN_TILE, T)
    a_blk = PROJ_FIXED // AW - 1

    def body(a_ref, pos_ref, qnw_ref, kvnw_ref, wuq_ref, wkv_ref, tab_ref,
             q_ref, k_ref, v_ref, qn_ref, kvn_ref):
        cq = a_ref[:, 0:MLA_Q_RANK].astype(F32)
        ckv = a_ref[:, MLA_Q_RANK:MLA_Q_RANK + MLA_KV_RANK].astype(F32)
        kr = a_ref[:, 640:768].astype(F32)
        qn = _rms_fwd(cq, qnw_ref[...]).astype(BF16)
        kvn = _rms_fwd(ckv, kvnw_ref[...]).astype(BF16)
        qn_ref[...] = qn
        kvn_ref[...] = kvn
        cs = _rope_cs(pos_ref[...], tab_ref)
        q = _dot_nt(qn, wuq_ref[...])
        kv = _dot(kvn, wkv_ref[...])
        krr = _rope(kr, cs, MLA_ROPE // 2)
        for hd in range(MLA_HEADS):
            sl = slice(hd * HP, (hd + 1) * HP)
            q_ref[:, sl] = (_rope(q[:, sl], cs, MLA_ROPE // 2) * ATTN_SCALE).astype(BF16)
            k_ref[:, sl] = (kv[:, sl] + krr).astype(BF16)
        v_ref[...] = kv[:, QW:].astype(BF16)

    def full(r, c):
        return pl.BlockSpec((r, c), lambda i: (0, 0))

    def rows(c):
        return pl.BlockSpec((tT, c), lambda i: (i, 0))

    return pl.pallas_call(
        body, name=name, grid=(T // tT,),
        in_specs=[pl.BlockSpec((tT, AW), lambda i: (i, a_blk)), rows(1),
                  full(1, MLA_Q_RANK), full(1, MLA_KV_RANK),
                  full(QW, MLA_Q_RANK), full(MLA_KV_RANK, 2 * QW), full(8, LANES)],
        out_specs=[rows(QW), rows(QW), rows(QW), rows(MLA_Q_RANK), rows(MLA_KV_RANK)],
        out_shape=[jax.ShapeDtypeStruct((T, QW), BF16)] * 3
        + [jax.ShapeDtypeStruct((T, MLA_Q_RANK), BF16), jax.ShapeDtypeStruct((T, MLA_KV_RANK), BF16)],
        compiler_params=_params(("parallel",)),
    )(proj, pos, qn_w, kvn_w, w_uq, w_kv, tab)


def _mla_prep_bwd(dq, dk, dv, proj, pos, qn_w, kvn_w, w_uq, w_kv, tab, *, name):
    T = proj.shape[0]
    tT = min(TOKEN_TILE, T)
    a_blk = PROJ_FIXED // AW - 1

    def body(dq_ref, dk_ref, dv_ref, a_ref, pos_ref, qnw_ref, kvnw_ref, wuq_ref, wkv_ref, tab_ref,
             da_ref, dql_ref, dkvl_ref, gqn_ref, gkvn_ref):
        @pl.when(pl.program_id(0) == 0)
        def _():
            gqn_ref[...] = jnp.zeros_like(gqn_ref)
            gkvn_ref[...] = jnp.zeros_like(gkvn_ref)

        cs = _rope_cs(pos_ref[...], tab_ref)
        dkr = jnp.zeros((tT, HP), F32)
        for hd in range(MLA_HEADS):
            sl = slice(hd * HP, (hd + 1) * HP)
            dql_ref[:, sl] = (_rope(dq_ref[:, sl], cs, MLA_ROPE // 2, inverse=True) * ATTN_SCALE).astype(BF16)
            dkh = dk_ref[:, sl]
            dkr = dkr + dkh
            dkvl_ref[:, sl] = dkh.astype(BF16)
        dkvl_ref[:, QW:] = dv_ref[...]
        dqn = _dot(dql_ref[...], wuq_ref[...])
        dkvn = _dot_nt(dkvl_ref[...], wkv_ref[...])
        cq = a_ref[:, 0:MLA_Q_RANK].astype(F32)
        ckv = a_ref[:, MLA_Q_RANK:MLA_Q_RANK + MLA_KV_RANK].astype(F32)
        dcq, gq = _rms_bwd(cq, qnw_ref[...], dqn)
        dckv, gkv = _rms_bwd(ckv, kvnw_ref[...], dkvn)
        gqn_ref[...] += gq
        gkvn_ref[...] += gkv
        da_ref[:, 0:MLA_Q_RANK] = dcq.astype(BF16)
        da_ref[:, MLA_Q_RANK:MLA_Q_RANK + MLA_KV_RANK] = dckv.astype(BF16)
        da_ref[:, 640:768] = _rope(dkr, cs, MLA_ROPE // 2, inverse=True).astype(BF16)
        da_ref[:, 768:AW] = jnp.zeros((tT, AW - 768), BF16)

    def full(r, c):
        return pl.BlockSpec((r, c), lambda i: (0, 0))

    def rows(c):
        return pl.BlockSpec((tT, c), lambda i: (i, 0))

    return pl.pallas_call(
        body, name=name, grid=(T // tT,),
        in_specs=[rows(QW), rows(QW), rows(QW), pl.BlockSpec((tT, AW), lambda i: (i, a_blk)), rows(1),
                  full(1, MLA_Q_RANK), full(1, MLA_KV_RANK),
                  full(QW, MLA_Q_RANK), full(MLA_KV_RANK, 2 * QW), full(8, LANES)],
        out_specs=[rows(AW), rows(QW), rows(2 * QW), full(1, MLA_Q_RANK), full(1, MLA_KV_RANK)],
        out_shape=[jax.ShapeDtypeStruct((T, AW), BF16), jax.ShapeDtypeStruct((T, QW), BF16),
                   jax.ShapeDtypeStruct((T, 2 * QW), BF16),
                   jax.ShapeDtypeStruct((1, MLA_Q_RANK), F32), jax.ShapeDtypeStruct((1, MLA_KV_RANK), F32)],
        compiler_params=_params(("arbitrary",)),
    )(dq, dk, dv, proj, pos, qn_w, kvn_w, w_uq, w_kv, tab)


def _flash_fwd(q, k, v, *, name, exchange=None):
    T = q.shape[0]
    H = q.shape[1] // HP
    tq = min(ATTN_TILE, T)
    nq = T // tq

    sub = tq // ATTN_CHAINS

    def body(q_ref, k_ref, v_ref, o_ref, lse_ref):
        qi = pl.program_id(1)
        qs = [q_ref[c * sub:(c + 1) * sub, :] for c in range(ATTN_CHAINS)]

        def update(carry, off, masked):
            nks = [(c + 1) * sub if masked else tq for c in range(ATTN_CHAINS)]
            scores = [_dot_nt(qs[c], k_ref[pl.ds(off, nks[c]), :]) for c in range(ATTN_CHAINS)]
            out = []
            for c in range(ATTN_CHAINS):
                m_prev, l_prev, acc = carry[c]
                nk, s = nks[c], scores[c]
                vb = v_ref[pl.ds(off, nk), :]
                if masked:
                    rows = lax.broadcasted_iota(jnp.int32, (sub, nk), 0) + c * sub
                    s = jnp.where(rows >= lax.broadcasted_iota(jnp.int32, (sub, nk), 1), s, NEG)
                m_new = jnp.maximum(m_prev, jnp.max(s, axis=1, keepdims=True))
                alpha = jnp.exp(m_prev - m_new)
                p = jnp.exp(s - m_new)
                out.append((m_new, alpha * l_prev + jnp.sum(p, axis=1, keepdims=True),
                            alpha * acc + _dot(p.astype(BF16), vb)))
            return tuple(out)

        init = tuple((jnp.full((sub, 1), NEG, F32), jnp.zeros((sub, 1), F32), jnp.zeros((sub, HP), F32))
                     for _ in range(ATTN_CHAINS))
        carry = lax.fori_loop(0, qi, lambda j, cr: update(cr, pl.multiple_of(j * tq, tq), False), init)
        carry = update(carry, pl.multiple_of(qi * tq, tq), True)
        for c in range(ATTN_CHAINS):
            m_fin, l_fin, acc = carry[c]
            o_ref[c * sub:(c + 1) * sub, :] = (acc / l_fin).astype(BF16)
            lse_ref[c * sub:(c + 1) * sub, :] = jnp.broadcast_to(m_fin + jnp.log(l_fin), (sub, HP))

    qspec = pl.BlockSpec((tq, HP), lambda h, i: (i, h))
    kspec = pl.BlockSpec((T, HP), lambda h, i: (0, h))
    return _call(
        body, name=name, grid=(H, nq),
        in_specs=[qspec, kspec, kspec], out_specs=[qspec, qspec],
        out_shape=[jax.ShapeDtypeStruct((T, H * HP), BF16), jax.ShapeDtypeStruct((T, H * HP), F32)],
        scratch_shapes=[], args=(q, k, v), exchange=exchange)


def _flash_bwd(q, k, v, do, lse, delta, *, name, exchange=None):
    T = q.shape[0]
    H = q.shape[1] // HP
    tq = min(ATTN_TILE, T)
    nq = T // tq
    sub = tq // ATTN_CHAINS

    def body(k_ref, v_ref, q_ref, do_ref, lse_ref, dl_ref, dq_ref, dk_ref, dv_ref):
        ki = pl.program_id(1)

        @pl.when(ki == 0)
        def _():
            dq_ref[...] = jnp.zeros_like(dq_ref)

        def grow(a):
            return a if a.shape[0] == tq else jnp.concatenate([a, jnp.zeros((tq - a.shape[0], HP), F32)], axis=0)

        def step(carry, j, masked):
            dk_acc, dv_acc = carry
            nks = [(c + 1) * sub if masked else tq for c in range(ATTN_CHAINS)]
            rws = [pl.ds(pl.multiple_of(j * tq + c * sub, sub), sub) for c in range(ATTN_CHAINS)]
            scores = [_dot_nt(q_ref[rws[c], :], k_ref[0:nks[c], :]) for c in range(ATTN_CHAINS)]
            dps = [_dot_nt(do_ref[rws[c], :], v_ref[0:nks[c], :]) for c in range(ATTN_CHAINS)]
            for c in range(ATTN_CHAINS):
                rows, nk, s, dp = rws[c], nks[c], scores[c], dps[c]
                kb = k_ref[0:nk, :]
                qb = q_ref[rows, :]
                dob = do_ref[rows, :]
                if masked:
                    ri = lax.broadcasted_iota(jnp.int32, (sub, nk), 0) + c * sub
                    s = jnp.where(ri >= lax.broadcasted_iota(jnp.int32, (sub, nk), 1), s, NEG)
                p = jnp.exp(s - lse_ref[rows, 0:1])
                dv_acc = dv_acc + grow(_dot_tn(p.astype(BF16), dob))
                ds = (p * (dp - dl_ref[rows, 0:1])).astype(BF16)
                dk_acc = dk_acc + grow(_dot_tn(ds, qb))
                dq_ref[rows, :] += _dot(ds, kb)
            return dk_acc, dv_acc

        carry = step((jnp.zeros((tq, HP), F32), jnp.zeros((tq, HP), F32)), ki, True)
        dk_acc, dv_acc = lax.fori_loop(ki + 1, nq, lambda j, cr: step(cr, j, False), carry)
        dk_ref[...] = dk_acc
        dv_ref[...] = dv_acc.astype(BF16)

    kspec = pl.BlockSpec((tq, HP), lambda h, j: (j, h))
    full = pl.BlockSpec((T, HP), lambda h, j: (0, h))
    return _call(
        body, name=name, grid=(H, nq),
        in_specs=[kspec, kspec, full, full, full, full], out_specs=[full, kspec, kspec],
        out_shape=[jax.ShapeDtypeStruct((T, H * HP), F32), jax.ShapeDtypeStruct((T, H * HP), F32),
                   jax.ShapeDtypeStruct((T, H * HP), BF16)],
        scratch_shapes=[], args=(k, v, q, do, lse, delta), exchange=exchange)


def _ret_consts(cc, hd):
    lg = math.log(1.0 - 2.0 ** (-5.0 - hd))
    diff = (lax.broadcasted_iota(jnp.int32, (cc, cc), 0) - lax.broadcasted_iota(jnp.int32, (cc, cc), 1)).astype(F32)
    decay = jnp.where(diff >= 0, jnp.exp(jnp.maximum(diff, 0.0) * lg), 0.0)
    idx = lax.broadcasted_iota(jnp.int32, (cc, 1), 0).astype(F32)
    zeta = jnp.exp((cc - 1.0 - idx) * lg)
    xi = jnp.exp((idx + 1.0) * lg)
    return decay, zeta, xi, math.exp(cc * lg)


def _ret_fwd(proj, pos, tab, *, name):
    T = proj.shape[0]
    cc = min(RET_TILE, T)
    n = T // cc

    def body(rq_ref, rk_ref, rv_ref, pos_ref, tab_ref, y_ref, yn_ref, rprev_ref, r_s):
        @pl.when(pl.program_id(0) == 0)
        def _():
            r_s[...] = jnp.zeros_like(r_s)

        cs = _rope_cs(pos_ref[...], tab_ref)
        for hd in range(RET_HEADS):
            sl = slice(hd * HP, (hd + 1) * HP)
            decay, zeta, xi, gc = _ret_consts(cc, hd)
            q = _rope(rq_ref[:, sl].astype(F32), cs, RET_DK // 2).astype(BF16)
            kf = _rope(rk_ref[:, sl].astype(F32), cs, RET_DK // 2) * (RET_DK ** -0.5)
            k = kf.astype(BF16)
            v = rv_ref[:, sl]
            r = r_s[hd]
            rprev_ref[0, hd] = r
            inner = (_dot_nt(q, k) * decay).astype(BF16)
            y = _dot(inner, v) + _dot(q, r.astype(BF16)) * xi
            r_s[hd] = r * gc + _dot_tn((kf * zeta).astype(BF16), v)
            y_ref[:, sl] = y
            mu = jnp.mean(y, axis=-1, keepdims=True)
            yc = y - mu
            var = jnp.mean(yc * yc, axis=-1, keepdims=True)
            yn_ref[:, sl] = (yc * lax.rsqrt(var + GN_EPS)).astype(BF16)

    def blk(j):
        return pl.BlockSpec((cc, RW), lambda i: (i, j))

    return pl.pallas_call(
        body, name=name, grid=(n,),
        in_specs=[blk(0), blk(1), blk(2), pl.BlockSpec((cc, 1), lambda i: (i, 0)),
                  pl.BlockSpec((8, LANES), lambda i: (0, 0))],
        out_specs=[blk(0), blk(0), pl.BlockSpec((1, RET_HEADS, HP, RET_DV), lambda i: (i, 0, 0, 0))],
        out_shape=[jax.ShapeDtypeStruct((T, RW), F32), jax.ShapeDtypeStruct((T, RW), BF16),
                   jax.ShapeDtypeStruct((n, RET_HEADS, HP, RET_DV), F32)],
        scratch_shapes=[pltpu.VMEM((RET_HEADS, HP, RET_DV), F32)],
        compiler_params=_params(("arbitrary",)),
    )(proj, proj, proj, pos, tab)


def _ret_bwd(dyn, y, proj, pos, tab, rprev, *, name):
    T = proj.shape[0]
    cc = min(RET_TILE, T)
    n = T // cc

    def body(dyn_ref, y_ref, rq_ref, rk_ref, rv_ref, pos_ref, tab_ref, rprev_ref,
             drq_ref, drk_ref, drv_ref, dr_s):
        @pl.when(pl.program_id(0) == 0)
        def _():
            dr_s[...] = jnp.zeros_like(dr_s)

        cs = _rope_cs(pos_ref[...], tab_ref)
        for hd in range(RET_HEADS):
            sl = slice(hd * HP, (hd + 1) * HP)
            decay, zeta, xi, gc = _ret_consts(cc, hd)
            q = _rope(rq_ref[:, sl].astype(F32), cs, RET_DK // 2).astype(BF16)
            kf = _rope(rk_ref[:, sl].astype(F32), cs, RET_DK // 2) * (RET_DK ** -0.5)
            k = kf.astype(BF16)
            v = rv_ref[:, sl]
            yv = y_ref[:, sl]
            mu = jnp.mean(yv, axis=-1, keepdims=True)
            yc = yv - mu
            rs = lax.rsqrt(jnp.mean(yc * yc, axis=-1, keepdims=True) + GN_EPS)
            yn = yc * rs
            dn = dyn_ref[:, sl]
            dy = rs * (dn - jnp.mean(dn, axis=-1, keepdims=True) - yn * jnp.mean(dn * yn, axis=-1, keepdims=True))
            dyb = dy.astype(BF16)
            dyx = (dy * xi).astype(BF16)
            dr = dr_s[hd]
            drb = dr.astype(BF16)
            inner = (_dot_nt(q, k) * decay).astype(BF16)
            da = (_dot_nt(dyb, v) * decay).astype(BF16)
            dv = _dot_tn(inner, dyb) + _dot((kf * zeta).astype(BF16), drb)
            dq = _dot(da, k) + _dot_nt(dyx, rprev_ref[0, hd].astype(BF16))
            dk = _dot_tn(da, q) + _dot_nt(v, drb) * zeta
            dr_s[hd] = dr * gc + _dot_tn(q, dyx)
            drq_ref[:, sl] = _rope(dq, cs, RET_DK // 2, inverse=True).astype(BF16)
            drk_ref[:, sl] = _rope(dk * (RET_DK ** -0.5), cs, RET_DK // 2, inverse=True).astype(BF16)
            drv_ref[:, sl] = dv.astype(BF16)

    def blk(j):
        return pl.BlockSpec((cc, RW), lambda i: (n - 1 - i, j))

    return pl.pallas_call(
        body, name=name, grid=(n,),
        in_specs=[blk(0), blk(0), blk(0), blk(1), blk(2), pl.BlockSpec((cc, 1), lambda i: (n - 1 - i, 0)),
                  pl.BlockSpec((8, LANES), lambda i: (0, 0)),
                  pl.BlockSpec((1, RET_HEADS, HP, RET_DV), lambda i: (n - 1 - i, 0, 0, 0))],
        out_specs=[blk(0), blk(0), blk(0)],
        out_shape=[jax.ShapeDtypeStruct((T, RW), BF16)] * 3,
        scratch_shapes=[pltpu.VMEM((RET_HEADS, HP, RET_DV), F32)],
        compiler_params=_params(("arbitrary",)),
    )(dyn, y, proj, proj, proj, pos, tab, rprev)


def _merge_fwd(o, yn, proj, gn_w, w_bm, w_br, w_out, h, post_w, *, name):
    T, D = h.shape
    tT = min(MERGE_TILE, T)
    g_blk = PROJ_FIXED // D

    def body(o_ref, yn_ref, rg_ref, gm_ref, gr_ref, gnw_ref, wbm_ref, wbr_ref, wout_ref, h_ref, post_ref,
             omla_ref, oret_ref, m_ref, ho_ref):
        o_mla = _dot(o_ref[...], wbm_ref[...])
        rg = rg_ref[...].astype(F32)
        gated = (rg * _sigmoid(rg) * (yn_ref[...].astype(F32) * gnw_ref[...])).astype(BF16)
        o_ret = _dot(gated, wbr_ref[...])
        omla_ref[...] = o_mla.astype(BF16)
        oret_ref[...] = o_ret.astype(BF16)
        merged = _sigmoid(gm_ref[...].astype(F32)) * o_mla + _sigmoid(gr_ref[...].astype(F32)) * o_ret
        m = _dot(merged.astype(BF16), wout_ref[...])
        m_ref[...] = m
        ho_ref[...] = h_ref[...] + _rms_fwd(m, post_ref[...])

    def full(r, c):
        return pl.BlockSpec((r, c), lambda i: (0, 0))

    def rows(c, j=0):
        return pl.BlockSpec((tT, c), lambda i: (i, j))

    return pl.pallas_call(
        body, name=name, grid=(T // tT,),
        in_specs=[rows(QW), rows(RW), rows(RW, 3), rows(D, g_blk), rows(D, g_blk + 1), full(1, RW),
                  full(QW, D), full(RW, D), full(D, D), rows(D), full(1, D)],
        out_specs=[rows(D), rows(D), rows(D), rows(D)],
        out_shape=[jax.ShapeDtypeStruct((T, D), BF16), jax.ShapeDtypeStruct((T, D), BF16),
                   jax.ShapeDtypeStruct((T, D), F32), jax.ShapeDtypeStruct((T, D), F32)],
        compiler_params=_params(("parallel",)),
    )(o, yn, proj, proj, proj, gn_w, w_bm, w_br, w_out, h, post_w)


def _merge_bwd(dho, m, post_w, omla, oret, proj, yn, gn_w, o, w_out, w_bm, w_br, *, name):
    T, D = dho.shape
    tT = min(MERGE_TILE, T)
    g_blk = PROJ_FIXED // D

    def body(dho_ref, m_ref, post_ref, omla_ref, oret_ref, rg_ref, gm_ref, gr_ref, yn_ref, gnw_ref, o_ref,
             wout_ref, wbm_ref, wbr_ref,
             dm_ref, merged_ref, dgm_ref, dgr_ref, domla_ref, do_ref, delta_ref, doret_ref, gated_ref,
             drg_ref, dyn_ref, gpost_ref, ggn_ref):
        @pl.when(pl.program_id(0) == 0)
        def _():
            gpost_ref[...] = jnp.zeros_like(gpost_ref)
            ggn_ref[...] = jnp.zeros_like(ggn_ref)

        dm, gp = _rms_bwd(m_ref[...], post_ref[...], dho_ref[...])
        gpost_ref[...] += gp
        dmb = dm.astype(BF16)
        dm_ref[...] = dmb
        dmerged = _dot_nt(dmb, wout_ref[...])
        o_mla = omla_ref[...].astype(F32)
        o_ret = oret_ref[...].astype(F32)
        sgm = _sigmoid(gm_ref[...].astype(F32))
        sgr = _sigmoid(gr_ref[...].astype(F32))
        merged_ref[...] = (sgm * o_mla + sgr * o_ret).astype(BF16)
        dgm_ref[...] = (dmerged * o_mla * sgm * (1.0 - sgm)).astype(BF16)
        dgr_ref[...] = (dmerged * o_ret * sgr * (1.0 - sgr)).astype(BF16)
        domla = (dmerged * sgm).astype(BF16)
        domla_ref[...] = domla
        do = _dot_nt(domla, wbm_ref[...])
        do_ref[...] = do.astype(BF16)
        for hd in range(MLA_HEADS):
            sl = slice(hd * HP, (hd + 1) * HP)
            d = jnp.sum(do[:, sl] * o_ref[:, sl].astype(F32), axis=-1, keepdims=True)
            delta_ref[:, sl] = jnp.broadcast_to(d, (tT, HP))
        doret = (dmerged * sgr).astype(BF16)
        doret_ref[...] = doret
        dgated = _dot_nt(doret, wbr_ref[...])
        rg = rg_ref[...].astype(F32)
        sg = _sigmoid(rg)
        srg = rg * sg
        ynv = yn_ref[...].astype(F32)
        yw = ynv * gnw_ref[...]
        gated_ref[...] = (srg * yw).astype(BF16)
        drg_ref[...] = (dgated * yw * (sg * (1.0 + rg * (1.0 - sg)))).astype(BF16)
        dgs = dgated * srg
        dyn_ref[...] = dgs * gnw_ref[...]
        ggn_ref[...] += jnp.sum(dgs * ynv, axis=0, keepdims=True)

    def full(r, c):
        return pl.BlockSpec((r, c), lambda i: (0, 0))

    def rows(c, j=0):
        return pl.BlockSpec((tT, c), lambda i: (i, j))

    return pl.pallas_call(
        body, name=name, grid=(T // tT,),
        in_specs=[rows(D), rows(D), full(1, D), rows(D), rows(D), rows(RW, 3), rows(D, g_blk), rows(D, g_blk + 1),
                  rows(RW), full(1, RW), rows(QW), full(D, D), full(QW, D), full(RW, D)],
        out_specs=[rows(D), rows(D), rows(D), rows(D), rows(D), rows(QW), rows(QW), rows(D), rows(RW),
                   rows(RW), rows(RW), full(1, D), full(1, RW)],
        out_shape=[jax.ShapeDtypeStruct((T, D), BF16)] * 5
        + [jax.ShapeDtypeStruct((T, QW), BF16), jax.ShapeDtypeStruct((T, QW), F32),
           jax.ShapeDtypeStruct((T, D), BF16), jax.ShapeDtypeStruct((T, RW), BF16),
           jax.ShapeDtypeStruct((T, RW), BF16), jax.ShapeDtypeStruct((T, RW), F32),
           jax.ShapeDtypeStruct((1, D), F32), jax.ShapeDtypeStruct((1, RW), F32)],
        compiler_params=_params(("arbitrary",)),
    )(dho, m, post_w, omla, oret, proj, proj, proj, yn, gn_w, o, w_out, w_bm, w_br)


def _mesh_pos():
    return lax.axis_index("x"), lax.axis_index("y"), lax.axis_index("c")


class _Gather:
    def __init__(self, shards):
        self.operands = list(shards)
        self.n = len(shards)
        self.out_shape = [jax.ShapeDtypeStruct((N_DEV,) + s.shape, s.dtype) for s in shards]
        self.scratch = [pltpu.SemaphoreType.DMA((7 * self.n,)), pltpu.SemaphoreType.DMA((7 * self.n,)),
                        pltpu.SemaphoreType.DMA((self.n,))]

    def phase(self, p, x_refs, out_refs, sems):
        send_sems, recv_sems, local_sems = sems
        x, y, c = _mesh_pos()
        me, sibling = (x, y, c), (x, y, 1 - c)
        chips = [(1 - x, y), (x, 1 - y), (1 - x, 1 - y)]

        def copy(w, k, block, to, src=None):
            slot = out_refs[w].at[4 * block[0] + 2 * block[1] + block[2]]
            return pltpu.make_async_remote_copy(
                src_ref=slot if src is None else src, dst_ref=slot,
                send_sem=send_sems.at[7 * w + k], recv_sem=recv_sems.at[7 * w + k],
                device_id=to, device_id_type=pl.DeviceIdType.MESH)

        for w in range(self.n):
            mine = pltpu.make_async_copy(x_refs[w], out_refs[w].at[4 * x + 2 * y + c], local_sems.at[w])
            first = [copy(w, 0, me, sibling, src=x_refs[w])]
            first += [copy(w, 1 + j, me, (*chip, c), src=x_refs[w]) for j, chip in enumerate(chips)]
            passed = [copy(w, 4 + j, (*chip, c), sibling) for j, chip in enumerate(chips)]
            if p == 0:
                mine.start()
                for cp in first:
                    cp.start()
            elif p == 1:
                for j, chip in enumerate(chips):
                    copy(w, 1 + j, (*chip, c), me).wait_recv()
                    passed[j].start()
            else:
                copy(w, 0, sibling, me).wait_recv()
                for j, chip in enumerate(chips):
                    copy(w, 4 + j, (*chip, 1 - c), me).wait_recv()
                for cp in first + passed:
                    cp.wait_send()
                mine.wait()


class _Scatter:
    def __init__(self, grads, whole=()):
        self.n_sliced = len(grads)
        self.operands = list(grads) + list(whole)
        self.n = len(self.operands)
        self.out_shape = [jax.ShapeDtypeStruct(g.shape, g.dtype) for g in grads]
        self.out_shape += [jax.ShapeDtypeStruct((N_DEV,) + a.shape, a.dtype) for a in whole]
        n_sem = (N_DEV - 1) * self.n
        self.scratch = [pltpu.SemaphoreType.DMA((n_sem,)), pltpu.SemaphoreType.DMA((n_sem,)),
                        pltpu.SemaphoreType.DMA((self.n,))]

    def phase(self, p, in_refs, out_refs, sems):
        if p == 1:
            return
        send_sems, recv_sems, local_sems = sems
        x, y, c = _mesh_pos()
        me = 4 * x + 2 * y + c

        def src(w, dev):
            return in_refs[w].at[dev] if w < self.n_sliced else in_refs[w]

        for w in range(self.n):
            own = pltpu.make_async_copy(src(w, me), out_refs[w].at[me], local_sems.at[w])
            sends, recvs = [], []
            for r in range(1, N_DEV):
                px = 1 - x if r & 4 else x
                py = 1 - y if r & 2 else y
                pc = 1 - c if r & 1 else c
                peer, pidx = (px, py, pc), 4 * px + 2 * py + pc
                k = (N_DEV - 1) * w + r - 1
                sends.append(pltpu.make_async_remote_copy(
                    src_ref=src(w, pidx), dst_ref=out_refs[w].at[me], send_sem=send_sems.at[k],
                    recv_sem=recv_sems.at[k], device_id=peer, device_id_type=pl.DeviceIdType.MESH))
                recvs.append(pltpu.make_async_remote_copy(
                    src_ref=src(w, me), dst_ref=out_refs[w].at[pidx], send_sem=send_sems.at[k],
                    recv_sem=recv_sems.at[k], device_id=peer, device_id_type=pl.DeviceIdType.MESH))
            if p == 0:
                own.start()
                for cp in sends:
                    cp.start()
            else:
                for cp in recvs:
                    cp.wait_recv()
                for cp in sends:
                    cp.wait_send()
                own.wait()


def _exchange_alone(ex, *, name):
    n = ex.n

    def body(*refs):
        for p in range(3):
            ex.phase(p, refs[:n], refs[n:2 * n], refs[2 * n:])

    anyspec = pl.BlockSpec(memory_space=pl.ANY)
    return pl.pallas_call(body, name=name, out_shape=ex.out_shape, in_specs=[anyspec] * n,
                          out_specs=[anyspec] * n, scratch_shapes=ex.scratch)(*ex.operands)


def _adam_step(w_ref, p_ref, m_ref, v_ref, g_ref, d_ref, nm_ref, nv_ref):
    g = p_ref[0].astype(F32)
    for j in range(1, N_DEV):
        g = g + p_ref[j].astype(F32)
    g_ref[...] = g
    nm = ADAM_B1 * m_ref[...] + (1.0 - ADAM_B1) * g
    nv = ADAM_B2 * v_ref[...] + (1.0 - ADAM_B2) * (g * g)
    nm_ref[...] = nm
    nv_ref[...] = nv
    m_hat = nm / (1.0 - ADAM_B1 ** ADAM_STEP)
    v_hat = nv / (1.0 - ADAM_B2 ** ADAM_STEP)
    d_ref[...] = -ADAM_LR * (m_hat / (jnp.sqrt(v_hat) + ADAM_EPS) + ADAM_WD * w_ref[...])


def _adamw_vectors(ws, parts, ms, vs, *, name):
    n = len(ws)

    def body(*refs):
        w_refs, p_refs, m_refs, v_refs = (refs[i * n:(i + 1) * n] for i in range(4))
        outs = refs[4 * n:]
        for i in range(n):
            _adam_step(w_refs[i], p_refs[i], m_refs[i], v_refs[i], *outs[4 * i:4 * i + 4])

    return pl.pallas_call(
        body, name=name,
        out_shape=[jax.ShapeDtypeStruct(w.shape, F32) for w in ws for _ in range(4)],
    )(*ws, *parts, *ms, *vs)


def _adamw(w, parts, m, v, *, name):
    G, R, n = w.shape
    tn = 256 if (n > 256 and n % 256 == 0) else n
    tr = R
    for t in range(16, R, 16):
        if R % t == 0 and t * tn <= 160 * 1024:
            tr = t
    if R * tn <= 160 * 1024:
        tr = R

    def body(w_ref, p_ref, m_ref, v_ref, g_ref, d_ref, nm_ref, nv_ref):
        _adam_step(w_ref, p_ref, m_ref, v_ref, g_ref, d_ref, nm_ref, nv_ref)

    blk = pl.BlockSpec((None, tr, tn), lambda g, i, j: (g, i, j))
    return pl.pallas_call(
        body, name=name, grid=(G, R // tr, n // tn),
        in_specs=[blk, pl.BlockSpec((N_DEV, None, tr, tn), lambda g, i, j: (0, g, i, j)), blk, blk],
        out_specs=[blk, blk, blk, blk],
        out_shape=[jax.ShapeDtypeStruct((G, R, n), F32)] * 4,
        compiler_params=_params(("parallel", "parallel", "parallel")),
    )(w, parts, m, v)


def _pad_last(a, width):
    return jnp.pad(a, [(0, 0)] * (a.ndim - 1) + [(0, width - a.shape[-1])])


def _cols_of(g):
    return g.transpose(1, 0, 2).reshape(g.shape[1], N_DEV * g.shape[2])


def _col_shards(w):
    return w.reshape(w.shape[0], N_DEV, w.shape[1] // N_DEV).transpose(1, 0, 2)


def kernel(x, positions, ffn1_pre_w, ffn1_w1, ffn1_w2, ffn1_post_w, mix_pre_w, w_in, mla_q_norm_w, mla_w_uq, mla_kv_norm_w, mla_w_ukv, ret_gn_w, w_branch_mla, w_branch_ret, w_out, mix_post_w, ffn2_pre_w, ffn2_w1, ffn2_w2, ffn2_post_w, loss_target, m_ffn1_pre_w, m_ffn1_w1, m_ffn1_w2, m_ffn1_post_w, m_mix_pre_w, m_w_in, m_mla_q_norm_w, m_mla_w_uq, m_mla_kv_norm_w, m_mla_w_ukv, m_ret_gn_w, m_w_branch_mla, m_w_branch_ret, m_w_out, m_mix_post_w, m_ffn2_pre_w, m_ffn2_w1, m_ffn2_w2, m_ffn2_post_w, v_ffn1_pre_w, v_ffn1_w1, v_ffn1_w2, v_ffn1_post_w, v_mix_pre_w, v_w_in, v_mla_q_norm_w, v_mla_w_uq, v_mla_kv_norm_w, v_mla_w_ukv, v_ret_gn_w, v_w_branch_mla, v_w_branch_ret, v_w_out, v_mix_post_w, v_ffn2_pre_w, v_ffn2_w1, v_ffn2_w2, v_ffn2_post_w):
    T, D = x.shape[1], x.shape[2]
    h0 = x[0]
    tgt = loss_target[0]
    pos = positions.reshape(T, 1).astype(F32)

    big = [("ffn1_w1", ffn1_w1, m_ffn1_w1, v_ffn1_w1), ("ffn1_w2", ffn1_w2, m_ffn1_w2, v_ffn1_w2),
           ("w_in", w_in, m_w_in, v_w_in), ("mla_w_uq", mla_w_uq, m_mla_w_uq, v_mla_w_uq),
           ("mla_w_ukv", mla_w_ukv, m_mla_w_ukv, v_mla_w_ukv),
           ("w_branch_mla", w_branch_mla, m_w_branch_mla, v_w_branch_mla),
           ("w_branch_ret", w_branch_ret, m_w_branch_ret, v_w_branch_ret),
           ("w_out", w_out, m_w_out, v_w_out),
           ("ffn2_w1", ffn2_w1, m_ffn2_w1, v_ffn2_w1), ("ffn2_w2", ffn2_w2, m_ffn2_w2, v_ffn2_w2)]
    small = [("ffn1_pre_w", ffn1_pre_w, m_ffn1_pre_w, v_ffn1_pre_w), ("ffn1_post_w", ffn1_post_w, m_ffn1_post_w, v_ffn1_post_w),
             ("mix_pre_w", mix_pre_w, m_mix_pre_w, v_mix_pre_w), ("mla_q_norm_w", mla_q_norm_w, m_mla_q_norm_w, v_mla_q_norm_w),
             ("mla_kv_norm_w", mla_kv_norm_w, m_mla_kv_norm_w, v_mla_kv_norm_w), ("ret_gn_w", ret_gn_w, m_ret_gn_w, v_ret_gn_w),
             ("mix_post_w", mix_post_w, m_mix_post_w, v_mix_post_w), ("ffn2_pre_w", ffn2_pre_w, m_ffn2_pre_w, v_ffn2_pre_w),
             ("ffn2_post_w", ffn2_post_w, m_ffn2_post_w, v_ffn2_post_w)]

    half = ffn1_w2.shape[1]
    hp = -(-half // LANES) * LANES

    def rows_view(w):
        return w[0].T

    def send_w1(w):
        return jnp.pad(rows_view(w).reshape(2, half, D), ((0, 0), (0, hp - half), (0, 0))).reshape(2 * hp, D).astype(BF16)

    def send_w2(w):
        return jnp.pad(w[0], ((0, hp - half), (0, 0))).astype(BF16)

    mixer = ["w_in", "mla_w_uq", "mla_w_ukv", "w_branch_mla", "w_branch_ret", "w_out"]
    uq_w = MLA_NOPE + MLA_ROPE
    mixer_send = [rows_view(w_in).astype(BF16), jnp.pad(rows_view(mla_w_uq), ((0, HP - uq_w), (0, 0))).astype(BF16),
                  mla_w_ukv[0].astype(BF16), w_branch_mla[0].astype(BF16), w_branch_ret[0].astype(BF16),
                  w_out[0].astype(BF16)]

    w1a, w2a = _exchange_alone(_Gather([send_w1(ffn1_w1), send_w2(ffn1_w2)]), name="gather_ffn1")
    w2a = w2a.reshape(N_DEV // 2, 2 * hp, D)
    u1, f1, h1, *got = _ffn_fwd(h0, ffn1_pre_w, w1a, w2a, ffn1_post_w, None, name="ffn1_fwd_gather_mixer",
                                exchange=_Gather(mixer_send))
    fw = dict(zip(mixer, got))

    wi = fw["w_in"].reshape(-1, D)
    cq_w, ckv_w, kr_w = wi[0:384], wi[384:640], wi[640:672]
    rq_w, rk_w = wi[672:928], wi[928:1184]
    rv_w, rg_w = wi[1184:1696], wi[1696:2208]
    gm_w, gr_w = wi[2208:2208 + D], wi[2208 + D:2208 + 2 * D]
    zer = lambda n: jnp.zeros((n, D), BF16)
    head_rows = lambda a, h: jnp.pad(a.reshape(h, -1, D), ((0, 0), (0, HP - a.shape[0] // h), (0, 0))).reshape(h * HP, D)
    w_in_p = jnp.concatenate([head_rows(rq_w, RET_HEADS), head_rows(rk_w, RET_HEADS), rv_w, rg_w,
                              cq_w, ckv_w, zer(MLA_NOPE), kr_w, zer(HP - MLA_NOPE - MLA_ROPE), zer(AW - 768),
                              gm_w, gr_w], axis=0)
    w_uq_p = fw["mla_w_uq"].reshape(QW, MLA_Q_RANK)
    ukv = fw["mla_w_ukv"].transpose(1, 0, 2)
    w_kv_p = jnp.concatenate([_pad_last(ukv[:, :, :MLA_NOPE], HP).reshape(MLA_KV_RANK, QW),
                              _pad_last(ukv[:, :, MLA_NOPE:], HP).reshape(MLA_KV_RANK, QW)], axis=1)
    w_bm_p = jnp.pad(_cols_of(fw["w_branch_mla"]).reshape(MLA_HEADS, MLA_V, D),
                     ((0, 0), (0, HP - MLA_V), (0, 0))).reshape(QW, D)
    w_br, w_o = _cols_of(fw["w_branch_ret"]), fw["w_out"].reshape(D, D)
    tab_mla = _rope_table(MLA_NOPE, MLA_ROPE // 2)
    tab_ret = _rope_table(0, RET_DK // 2)

    proj, a1 = _rms_matmul(h1, mix_pre_w, w_in_p, name="mixer_in_proj")
    q, k, v, qn, kvn = _mla_prep_fwd(proj, pos, mla_q_norm_w, mla_kv_norm_w, w_uq_p, w_kv_p, tab_mla, name="mla_prep_fwd")
    o, lse, w1b, w2b = _flash_fwd(q, k, v, name="mla_attn_fwd_gather_ffn2",
                                  exchange=_Gather([send_w1(ffn2_w1), send_w2(ffn2_w2)]))
    w2b = w2b.reshape(N_DEV // 2, 2 * hp, D)
    ypre, yn, rprev = _ret_fwd(proj, pos, tab_ret, name="retention_fwd")
    omla, oret, m, h2 = _merge_fwd(o, yn, proj, ret_gn_w, w_bm_p, w_br, w_o, h1, mix_post_w, name="merge_fwd")
    u2, f2, _, dy, lossp = _ffn_fwd(h2, ffn2_pre_w, w1b, w2b, ffn2_post_w, tgt, name="ffn2_fwd_loss")
    loss = lax.psum(jnp.sum(lossp[::8, 0]), ("x", "y", "c"))

    def grad(x, dy, tag, exchange=None):
        return _matmul_tn(x if x.ndim == 3 else x[None], dy if dy.ndim == 3 else dy[None], name=tag, exchange=exchange)

    g2, du2, df2, a2, dh2, gpost2, gpre2 = _ffn_bwd(dy, f2, ffn2_post_w, h2, ffn2_pre_w, u2, w2b, w1b, name="ffn2_bwd")
    dw1b, = grad(du2.reshape(N_DEV, T, 2 * hp), a2, "ffn2_dw1")
    dw2b = grad(g2, df2, "ffn2_dw2")[0].reshape(N_DEV, hp, D)
    (dmb, merged, dgm, dgr, domla, do, delta, doret, gated, drg, dyn, gpostm, ggn) = _merge_bwd(
        dh2, m, mix_post_w, omla, oret, proj, yn, ret_gn_w, o, w_o, w_bm_p, w_br, name="merge_bwd")
    dw_out = grad(merged, dmb, "dw_out")[0][0]
    dw_bm_p = grad(o, domla, "dw_branch_mla")[0][0]
    dw_br = grad(gated, doret, "dw_branch_ret")[0][0]
    dq, dk, dv, *recv_ffn2 = _flash_bwd(q, k, v, do, lse, delta, name="mla_attn_bwd_scatter_ffn2",
                                        exchange=_Scatter([dw1b, dw2b]))
    da, dql, dkvl, gqn, gkvn = _mla_prep_bwd(dq, dk, dv, proj, pos, mla_q_norm_w, mla_kv_norm_w, w_uq_p, w_kv_p, tab_mla, name="mla_prep_bwd")
    dw_uq_p = grad(dql, qn, "dw_uq")[0][0]
    dw_kv_p = grad(kvn, dkvl, "dw_ukv")[0][0]
    drq, drk, drv = _ret_bwd(dyn, ypre, proj, pos, tab_ret, rprev, name="retention_bwd")
    dproj = jnp.concatenate([drq, drk, drv, drg, da, dgm, dgr], axis=1)
    dw_in_p = grad(dproj, a1, "dw_in")[0][0]
    dh1, gmixpre = _proj_bwd(dproj, w_in_p, h1, mix_pre_w, dh2, name="mixer_in_bwd")

    unhead = lambda a, h, wd: a.reshape(h, HP, D)[:, :wd].reshape(h * wd, D)
    c0 = 4 * RW
    dw_in = jnp.concatenate([
        dw_in_p[c0:c0 + 384], dw_in_p[c0 + 384:c0 + 640], dw_in_p[c0 + 640 + MLA_NOPE:c0 + 640 + MLA_NOPE + MLA_ROPE],
        unhead(dw_in_p[0:RW], RET_HEADS, RET_DK), unhead(dw_in_p[RW:2 * RW], RET_HEADS, RET_DK),
        dw_in_p[2 * RW:3 * RW], dw_in_p[3 * RW:4 * RW],
        dw_in_p[PROJ_FIXED:PROJ_FIXED + D], dw_in_p[PROJ_FIXED + D:PROJ_FIXED + 2 * D]], axis=0).reshape(N_DEV, -1, D)
    dw_uq = dw_uq_p.reshape(MLA_HEADS, HP, MLA_Q_RANK)[:, :uq_w]
    dkp = dw_kv_p[:, :QW].reshape(MLA_KV_RANK, MLA_HEADS, HP)[:, :, :MLA_NOPE]
    dvp = dw_kv_p[:, QW:].reshape(MLA_KV_RANK, MLA_HEADS, HP)[:, :, :MLA_V]
    dw_ukv = jnp.concatenate([dkp, dvp], axis=2).transpose(1, 0, 2)
    dw_bm = dw_bm_p.reshape(MLA_HEADS, HP, D)[:, :MLA_V].reshape(MLA_HEADS * MLA_V, D)

    mixer_grads = [dw_in, dw_uq, dw_ukv, _col_shards(dw_bm), _col_shards(dw_br),
                   dw_out.reshape(N_DEV, D // N_DEV, D)]
    g1, du1, df1, a0, dx, gpost1, gpre1, *recv_mixer = _ffn_bwd(
        dh1, f1, ffn1_post_w, h0, ffn1_pre_w, u1, w2a, w1a, name="ffn1_bwd_scatter_mixer", exchange=_Scatter(mixer_grads))
    dw2a = grad(g1, df1, "ffn1_dw2")[0].reshape(N_DEV, hp, D)
    dw1a, recv_w2a = grad(du1.reshape(N_DEV, T, 2 * hp), a0, "ffn1_dw1_scatter_dw2", exchange=_Scatter([dw2a]))

    small_g = {"ffn1_pre_w": gpre1, "ffn1_post_w": gpost1, "mix_pre_w": gmixpre, "mla_q_norm_w": gqn,
               "mla_kv_norm_w": gkvn, "ret_gn_w": ggn, "mix_post_w": gpostm, "ffn2_pre_w": gpre2, "ffn2_post_w": gpost2}
    recv_w1a, *small_parts = _exchange_alone(_Scatter([dw1a], whole=[small_g[nm] for nm, *_ in small]),
                                             name="scatter_ffn1_dw1")
    parts = dict(zip(mixer, recv_mixer))
    parts.update(ffn1_w1=recv_w1a, ffn1_w2=recv_w2a, ffn2_w1=recv_ffn2[0], ffn2_w2=recv_ffn2[1])
    as_is = (lambda a: a, lambda p: p[:, None], lambda a: a)
    views = {nm: as_is for nm, *_ in big}
    for nm in ("ffn1_w1", "ffn2_w1"):
        views[nm] = (lambda a: rows_view(a).reshape(2, half, D), lambda p: p.reshape(N_DEV, 2, hp, D),
                     lambda a: a.reshape(2 * half, D).T[None])
    for nm in ("w_in", "mla_w_uq"):
        views[nm] = (lambda a: rows_view(a)[None], lambda p: p[:, None], lambda a: a[0].T[None])
    big_out = {}
    for nm, w, m_, v_ in big:
        to_view, parts_view, back = views[nm]
        big_out[nm] = [back(a) for a in _adamw(to_view(w), parts_view(parts[nm]), to_view(m_), to_view(v_),
                                               name="adamw_" + nm)]
    small_out = _adamw_vectors([w for _, w, _, _ in small], small_parts, [a for _, _, a, _ in small],
                               [a for _, _, _, a in small], name="adamw_replicated")

    order = ["ffn1_pre_w", "ffn1_w1", "ffn1_w2", "ffn1_post_w", "mix_pre_w", "w_in", "mla_q_norm_w", "mla_w_uq",
             "mla_kv_norm_w", "mla_w_ukv", "ret_gn_w", "w_branch_mla", "w_branch_ret", "w_out", "mix_post_w",
             "ffn2_pre_w", "ffn2_w1", "ffn2_w2", "ffn2_post_w"]
    outs = [loss, dx[None]]
    for i in range(4):
        both = {nm: big_out[nm][i] for nm in big_out}
        both.update({nm: small_out[4 * j + i] for j, (nm, *_) in enumerate(small)})
        outs += [both[nm] for nm in order]
    return tuple(outs)
```

```python
import math

import numpy as np
import jax
import jax.numpy as jnp
from jax import lax
from jax.experimental import pallas as pl
from jax.experimental.pallas import tpu as pltpu

F32, BF16 = jnp.float32, jnp.bfloat16

MLA_HEADS, MLA_NOPE, MLA_ROPE, MLA_V = 8, 64, 32, 64
MLA_Q_RANK, MLA_KV_RANK = 384, 256
RET_HEADS, RET_DK, RET_DV = 4, 64, 128
ROPE_BASE, NORM_EPS, GN_EPS = 10000.0, 1e-6, 1e-6
ADAM_LR, ADAM_B1, ADAM_B2, ADAM_EPS, ADAM_WD, ADAM_STEP = 0.001, 0.9, 0.999, 1e-08, 0.01, 10
ATTN_SCALE = 1.0 / math.sqrt(MLA_NOPE + MLA_ROPE)

N_DEV = 8
LANES = 128
HP = LANES
QW = MLA_HEADS * HP
RW = RET_HEADS * HP
AW = 1024
PROJ_FIXED = 4 * RW + AW
NEG = -1e30

TOKEN_TILE = 512
ATTN_TILE = 1024
ATTN_CHAINS = 2
FFN_CHAINS = 2
RET_TILE = 256
PROJ_TILE_CAP = 2560
GRAD_TILE_CAP = 1408
GRAD_TOKEN_TILE = 2048
MERGE_TILE = 256
VMEM_LIMIT = 56 * 1024 * 1024


def _tile(n, cap, mult=LANES):
    if n <= cap:
        return n
    best = None
    for t in range(mult, cap + 1, mult):
        if n % t == 0:
            best = t
    assert best is not None, (n, cap, mult)
    return best


def _params(sem):
    return pltpu.CompilerParams(dimension_semantics=sem, vmem_limit_bytes=VMEM_LIMIT)


def _dot(a, b):
    return lax.dot_general(a, b, (((1,), (0,)), ((), ())), preferred_element_type=F32)


def _dot_nt(a, b):
    return lax.dot_general(a, b, (((1,), (1,)), ((), ())), preferred_element_type=F32)


def _dot_tn(a, b):
    return lax.dot_general(a, b, (((0,), (0,)), ((), ())), preferred_element_type=F32)


def _sigmoid(x):
    return pl.reciprocal(1.0 + jnp.exp(-x), approx=True)


def _rms_fwd(x, w):
    r = lax.rsqrt(jnp.mean(x * x, axis=-1, keepdims=True) + NORM_EPS)
    return x * r * w


def _rms_bwd(x, w, dy):
    r = lax.rsqrt(jnp.mean(x * x, axis=-1, keepdims=True) + NORM_EPS)
    xh = x * r
    g = dy * w
    dx = r * (g - xh * jnp.mean(g * xh, axis=-1, keepdims=True))
    return dx, jnp.sum(dy * xh, axis=0, keepdims=True)


def _rope_table(first, half):
    inv = (np.float32(ROPE_BASE) ** (-(np.arange(half, dtype=np.float32) / np.float32(half)))).astype(np.float32)
    tab = np.zeros((8, LANES), np.float32)
    tab[0, first:first + half] = inv
    tab[0, first + half:first + 2 * half] = inv
    tab[1, first:first + half] = -1.0
    tab[2, first + half:first + 2 * half] = 1.0
    return jnp.asarray(tab)


def _rope_cs(pos, tab_ref):
    ang = pos * tab_ref[0:1, :]
    s = jnp.sin(ang)
    return jnp.cos(ang), s * tab_ref[1:2, :], s * tab_ref[2:3, :]


def _rope(x, cs, half, inverse=False):
    c, s1, s2 = cs
    a = pltpu.roll(x, LANES - half, 1) * s1 + pltpu.roll(x, half, 1) * s2
    return x * c - a if inverse else x * c + a


def _call(body, *, name, grid, in_specs, out_specs, out_shape, scratch_shapes, args, exchange=None):
    sem = ("arbitrary",) * len(grid)
    if exchange is None:
        return pl.pallas_call(body, name=name, grid=grid, in_specs=in_specs, out_specs=out_specs,
                              out_shape=out_shape, scratch_shapes=scratch_shapes, compiler_params=_params(sem))(*args)
    n_in, n_out, e = len(in_specs), len(out_specs), exchange.n
    total = math.prod(grid)

    def carried(*refs):
        own = refs[:n_in] + refs[n_in + e:n_in + e + n_out] + refs[n_in + 2 * e + n_out:len(refs) - 3]
        ex_refs = (refs[n_in:n_in + e], refs[n_in + e + n_out:n_in + 2 * e + n_out], refs[len(refs) - 3:])
        step = pl.program_id(0)
        for d in range(1, len(grid)):
            step = step * grid[d] + pl.program_id(d)

        @pl.when(step == 0)
        def _():
            exchange.phase(0, *ex_refs)

        @pl.when(step == (3 * total) // 4)
        def _():
            exchange.phase(1, *ex_refs)

        body(*own)

        @pl.when(step == total - 1)
        def _():
            exchange.phase(2, *ex_refs)

    anyspec = pl.BlockSpec(memory_space=pl.ANY)
    return pl.pallas_call(
        carried, name=name, grid=grid, in_specs=list(in_specs) + [anyspec] * e,
        out_specs=list(out_specs) + [anyspec] * e, out_shape=list(out_shape) + exchange.out_shape,
        scratch_shapes=list(scratch_shapes) + exchange.scratch, compiler_params=_params(sem),
    )(*args, *exchange.operands)


def _ffn_fwd(h, pre_w, w1, w2, post_w, target, *, name, exchange=None):
    T, D = h.shape
    nk, ck = w2.shape[0], w2.shape[1]
    tT = min(TOKEN_TILE, T)
    nT = T // tT
    with_loss = target is not None

    def body(*refs):
        if with_loss:
            (h_ref, pre_ref, w1g_ref, w1u_ref, w2_ref, post_ref, tgt_ref,
             u_ref, f_ref, ho_ref, dy_ref, loss_ref, a_s, acc) = refs
        else:
            (h_ref, pre_ref, w1g_ref, w1u_ref, w2_ref, post_ref,
             u_ref, f_ref, ho_ref, a_s, acc) = refs
        k = pl.program_id(1)

        @pl.when(k == 0)
        def _():
            a_s[...] = _rms_fwd(h_ref[...], pre_ref[...]).astype(BF16)
            acc[...] = jnp.zeros_like(acc)

        for c in range(FFN_CHAINS):
            rs = slice(c * (tT // FFN_CHAINS), (c + 1) * (tT // FFN_CHAINS))
            a = a_s[rs, :]
            ug = _dot_nt(a, w1g_ref[...])
            uu = _dot_nt(a, w1u_ref[...])
            u_ref[0, rs, :] = ug.astype(BF16)
            u_ref[1, rs, :] = uu.astype(BF16)
            acc[rs, :] += _dot((ug * _sigmoid(ug) * uu).astype(BF16), w2_ref[...])

        @pl.when(k == nk - 1)
        def _():
            f = acc[...]
            f_ref[...] = f
            ho = h_ref[...] + 0.5 * _rms_fwd(f, post_ref[...])
            ho_ref[...] = ho
            if with_loss:
                e = ho - tgt_ref[...]
                dy_ref[...] = e * (1.0 / D)
                loss_ref[...] = jnp.full(loss_ref.shape, (0.5 / D) * jnp.sum(e * e), F32)

    row = pl.BlockSpec((tT, D), lambda i, k: (i, 0))
    vec = pl.BlockSpec((1, D), lambda i, k: (0, 0))
    in_specs = [row, vec,
                pl.BlockSpec((None, ck, D), lambda i, k: (k, 0, 0)),
                pl.BlockSpec((None, ck, D), lambda i, k: (nk + k, 0, 0)),
                pl.BlockSpec((None, ck, D), lambda i, k: (k, 0, 0)),
                vec]
    out_shape = [jax.ShapeDtypeStruct((2, nk, T, ck), BF16),
                 jax.ShapeDtypeStruct((T, D), F32),
                 jax.ShapeDtypeStruct((T, D), F32)]
    out_specs = [pl.BlockSpec((2, None, tT, ck), lambda i, k: (0, k, i, 0)), row, row]
    args = [h, pre_w, w1, w1, w2, post_w]
    if with_loss:
        in_specs.append(row)
        args.append(target)
        out_shape += [jax.ShapeDtypeStruct((T, D), F32), jax.ShapeDtypeStruct((nT * 8, LANES), F32)]
        out_specs += [row, pl.BlockSpec((8, LANES), lambda i, k: (i, 0))]
    return _call(body, name=name, grid=(nT, nk), in_specs=in_specs, out_specs=out_specs, out_shape=out_shape,
                 scratch_shapes=[pltpu.VMEM((tT, D), BF16), pltpu.VMEM((tT, D), F32)], args=args, exchange=exchange)


def _ffn_bwd(dho, f, post_w, h, pre_w, u, w2, w1, *, name, exchange=None):
    T, D = h.shape
    nk, ck = w2.shape[0], w2.shape[1]
    tT = min(TOKEN_TILE, T)
    nT = T // tT

    def body(dho_ref, f_ref, post_ref, h_ref, pre_ref, u_ref, w2_ref, w1g_ref, w1u_ref,
             g_ref, du_ref, df_ref, a_ref, dh_ref, gpost_ref, gpre_ref, df_s, da_acc):
        i, k = pl.program_id(0), pl.program_id(1)

        @pl.when(jnp.logical_and(i == 0, k == 0))
        def _():
            gpost_ref[...] = jnp.zeros_like(gpost_ref)
            gpre_ref[...] = jnp.zeros_like(gpre_ref)

        @pl.when(k == 0)
        def _():
            dx, dw = _rms_bwd(f_ref[...], post_ref[...], 0.5 * dho_ref[...])
            dfb = dx.astype(BF16)
            df_s[...] = dfb
            df_ref[...] = dfb
            gpost_ref[...] += dw
            a_ref[...] = _rms_fwd(h_ref[...], pre_ref[...]).astype(BF16)
            da_acc[...] = jnp.zeros_like(da_acc)

        groups = [slice(c * (tT // FFN_CHAINS), (c + 1) * (tT // FFN_CHAINS)) for c in range(FFN_CHAINS)]
        dgs = [_dot_nt(df_s[rs, :], w2_ref[...]) for rs in groups]
        for rs, dg in zip(groups, dgs):
            ug = u_ref[0, rs, :].astype(F32)
            uu = u_ref[1, rs, :].astype(F32)
            sg = _sigmoid(ug)
            sl = ug * sg
            g_ref[rs, :] = (sl * uu).astype(BF16)
            dug = (dg * uu * (sg + sl * (1.0 - sg))).astype(BF16)
            duu = (dg * sl).astype(BF16)
            du_ref[0, rs, :] = dug
            du_ref[1, rs, :] = duu
            da_acc[rs, :] += _dot(dug, w1g_ref[...]) + _dot(duu, w1u_ref[...])

        @pl.when(k == nk - 1)
        def _():
            dx, dw = _rms_bwd(h_ref[...], pre_ref[...], da_acc[...])
            dh_ref[...] = dho_ref[...] + dx
            gpre_ref[...] += dw

    row = pl.BlockSpec((tT, D), lambda i, k: (i, 0))
    vec = pl.BlockSpec((1, D), lambda i, k: (0, 0))
    return _call(
        body, name=name, grid=(nT, nk),
        in_specs=[row, row, vec, row, vec,
                  pl.BlockSpec((2, None, tT, ck), lambda i, k: (0, k, i, 0)),
                  pl.BlockSpec((None, ck, D), lambda i, k: (k, 0, 0)),
                  pl.BlockSpec((None, ck, D), lambda i, k: (k, 0, 0)),
                  pl.BlockSpec((None, ck, D), lambda i, k: (nk + k, 0, 0))],
        out_specs=[pl.BlockSpec((None, tT, ck), lambda i, k: (k, i, 0)),
                   pl.BlockSpec((2, None, tT, ck), lambda i, k: (0, k, i, 0)),
                   row, row, row, vec, vec],
        out_shape=[jax.ShapeDtypeStruct((nk, T, ck), BF16),
                   jax.ShapeDtypeStruct((2, nk, T, ck), BF16),
                   jax.ShapeDtypeStruct((T, D), BF16),
                   jax.ShapeDtypeStruct((T, D), BF16),
                   jax.ShapeDtypeStruct((T, D), F32),
                   jax.ShapeDtypeStruct((1, D), F32),
                   jax.ShapeDtypeStruct((1, D), F32)],
        scratch_shapes=[pltpu.VMEM((tT, D), BF16), pltpu.VMEM((tT, D), F32)],
        args=(dho, f, post_w, h, pre_w, u, w2, w1, w1), exchange=exchange)


def _matmul_tn(x, dy, *, name, exchange=None):
    Px, T, K = x.shape
    Py, _, N = dy.shape
    P = max(Px, Py)
    tT, tK, tN = min(GRAD_TOKEN_TILE, T), _tile(K, GRAD_TILE_CAP), _tile(N, GRAD_TILE_CAP)
    nt = T // tT

    def body(x_ref, dy_ref, o_ref, acc):
        t = pl.program_id(3)

        @pl.when(t == 0)
        def _():
            acc[...] = jnp.zeros_like(acc)

        acc[...] += _dot_tn(x_ref[...], dy_ref[...])

        @pl.when(t == nt - 1)
        def _():
            o_ref[...] = acc[...].astype(BF16)

    return _call(
        body, name=name, grid=(P, K // tK, N // tN, nt),
        in_specs=[pl.BlockSpec((None, tT, tK), lambda p, a, b, t: (p if Px > 1 else 0, t, a)),
                  pl.BlockSpec((None, tT, tN), lambda p, a, b, t: (p if Py > 1 else 0, t, b))],
        out_specs=[pl.BlockSpec((None, tK, tN), lambda p, a, b, t: (p, a, b))],
        out_shape=[jax.ShapeDtypeStruct((P, K, N), BF16)],
        scratch_shapes=[pltpu.VMEM((tK, tN), F32)], args=(x, dy), exchange=exchange)


def _rms_matmul(h, wn, w, *, name):
    T, D = h.shape
    N = w.shape[0]
    tT, tN = min(TOKEN_TILE, T), _tile(N, PROJ_TILE_CAP)

    def body(h_ref, wn_ref, w_ref, y_ref, a_ref):
        @pl.when(pl.program_id(1) == 0)
        def _():
            a_ref[...] = _rms_fwd(h_ref[...], wn_ref[...]).astype(BF16)

        y_ref[...] = _dot_nt(a_ref[...], w_ref[...]).astype(BF16)

    return pl.pallas_call(
        body, name=name, grid=(T // tT, N // tN),
        in_specs=[pl.BlockSpec((tT, D), lambda i, j: (i, 0)),
                  pl.BlockSpec((1, D), lambda i, j: (0, 0)),
                  pl.BlockSpec((tN, D), lambda i, j: (j, 0))],
        out_specs=[pl.BlockSpec((tT, tN), lambda i, j: (i, j)),
                   pl.BlockSpec((tT, D), lambda i, j: (i, 0))],
        out_shape=[jax.ShapeDtypeStruct((T, N), BF16), jax.ShapeDtypeStruct((T, D), BF16)],
        compiler_params=_params(("parallel", "arbitrary")),
    )(h, wn, w)


def _proj_bwd(dproj, w, h, wn, dres, *, name, exchange=None):
    T, D = h.shape
    N = w.shape[0]
    tT, tN = min(TOKEN_TILE, T), _tile(N, PROJ_TILE_CAP)
    nn = N // tN

    def body(dp_ref, w_ref, h_ref, wn_ref, dres_ref, dh_ref, gw_ref, acc):
        i, j = pl.program_id(0), pl.program_id(1)

        @pl.when(jnp.logical_and(i == 0, j == 0))
        def _():
            gw_ref[...] = jnp.zeros_like(gw_ref)

        @pl.when(j == 0)
        def _():
            acc[...] = jnp.zeros_like(acc)

        acc[...] += _dot(dp_ref[...], w_ref[...])

        @pl.when(j == nn - 1)
        def _():
            dx, dw = _rms_bwd(h_ref[...], wn_ref[...], acc[...])
            dh_ref[...] = dres_ref[...] + dx
            gw_ref[...] += dw

    row = pl.BlockSpec((tT, D), lambda i, j: (i, 0))
    vec = pl.BlockSpec((1, D), lambda i, j: (0, 0))
    return _call(
        body, name=name, grid=(T // tT, nn),
        in_specs=[pl.BlockSpec((tT, tN), lambda i, j: (i, j)),
                  pl.BlockSpec((tN, D), lambda i, j: (j, 0)), row, vec, row],
        out_specs=[row, vec],
        out_shape=[jax.ShapeDtypeStruct((T, D), F32), jax.ShapeDtypeStruct((1, D), F32)],
        scratch_shapes=[pltpu.VMEM((tT, D), F32)], args=(dproj, w, h, wn, dres), exchange=exchange)


def _mla_prep_fwd(proj, pos, qn_w, kvn_w, w_uq, w_kv, tab, *, name):
    T = proj.shape[0]
    tT = min(TOKEN_TILE, T)
    a_blk = PROJ_FIXED // AW - 1

    def body(a_ref, pos_ref, qnw_ref, kvnw_ref, wuq_ref, wkv_ref, tab_ref,
             q_ref, k_ref, v_ref, qn_ref, kvn_ref):
        cq = a_ref[:, 0:MLA_Q_RANK].astype(F32)
        ckv = a_ref[:, MLA_Q_RANK:MLA_Q_RANK + MLA_KV_RANK].astype(F32)
        kr = a_ref[:, 640:768].astype(F32)
        qn = _rms_fwd(cq, qnw_ref[...]).astype(BF16)
        kvn = _rms_fwd(ckv, kvnw_ref[...]).astype(BF16)
        qn_ref[...] = qn
        kvn_ref[...] = kvn
        cs = _rope_cs(pos_ref[...], tab_ref)
        q = _dot_nt(qn, wuq_ref[...])
        kv = _dot(kvn, wkv_ref[...])
        krr = _rope(kr, cs, MLA_ROPE // 2)
        for hd in range(MLA_HEADS):
            sl = slice(hd * HP, (hd + 1) * HP)
            q_ref[:, sl] = (_rope(q[:, sl], cs, MLA_ROPE // 2) * ATTN_SCALE).astype(BF16)
            k_ref[:, sl] = (kv[:, sl] + krr).astype(BF16)
        v_ref[...] = kv[:, QW:].astype(BF16)

    def full(r, c):
        return pl.BlockSpec((r, c), lambda i: (0, 0))

    def rows(c):
        return pl.BlockSpec((tT, c), lambda i: (i, 0))

    return pl.pallas_call(
        body, name=name, grid=(T // tT,),
        in_specs=[pl.BlockSpec((tT, AW), lambda i: (i, a_blk)), rows(1),
                  full(1, MLA_Q_RANK), full(1, MLA_KV_RANK),
                  full(QW, MLA_Q_RANK), full(MLA_KV_RANK, 2 * QW), full(8, LANES)],
        out_specs=[rows(QW), rows(QW), rows(QW), rows(MLA_Q_RANK), rows(MLA_KV_RANK)],
        out_shape=[jax.ShapeDtypeStruct((T, QW), BF16)] * 3
        + [jax.ShapeDtypeStruct((T, MLA_Q_RANK), BF16), jax.ShapeDtypeStruct((T, MLA_KV_RANK), BF16)],
        compiler_params=_params(("parallel",)),
    )(proj, pos, qn_w, kvn_w, w_uq, w_kv, tab)


def _mla_prep_bwd(dq, dk, dv, proj, pos, qn_w, kvn_w, w_uq, w_kv, tab, *, name):
    T = proj.shape[0]
    tT = min(TOKEN_TILE, T)
    a_blk = PROJ_FIXED // AW - 1

    def body(dq_ref, dk_ref, dv_ref, a_ref, pos_ref, qnw_ref, kvnw_ref, wuq_ref, wkv_ref, tab_ref,
             da_ref, dql_ref, dkvl_ref, gqn_ref, gkvn_ref):
        @pl.when(pl.program_id(0) == 0)
        def _():
            gqn_ref[...] = jnp.zeros_like(gqn_ref)
            gkvn_ref[...] = jnp.zeros_like(gkvn_ref)

        cs = _rope_cs(pos_ref[...], tab_ref)
        dkr = jnp.zeros((tT, HP), F32)
        for hd in range(MLA_HEADS):
            sl = slice(hd * HP, (hd + 1) * HP)
            dql_ref[:, sl] = (_rope(dq_ref[:, sl], cs, MLA_ROPE // 2, inverse=True) * ATTN_SCALE).astype(BF16)
            dkh = dk_ref[:, sl]
            dkr = dkr + dkh
            dkvl_ref[:, sl] = dkh.astype(BF16)
        dkvl_ref[:, QW:] = dv_ref[...]
        dqn = _dot(dql_ref[...], wuq_ref[...])
        dkvn = _dot_nt(dkvl_ref[...], wkv_ref[...])
        cq = a_ref[:, 0:MLA_Q_RANK].astype(F32)
        ckv = a_ref[:, MLA_Q_RANK:MLA_Q_RANK + MLA_KV_RANK].astype(F32)
        dcq, gq = _rms_bwd(cq, qnw_ref[...], dqn)
        dckv, gkv = _rms_bwd(ckv, kvnw_ref[...], dkvn)
        gqn_ref[...] += gq
        gkvn_ref[...] += gkv
        da_ref[:, 0:MLA_Q_RANK] = dcq.astype(BF16)
        da_ref[:, MLA_Q_RANK:MLA_Q_RANK + MLA_KV_RANK] = dckv.astype(BF16)
        da_ref[:, 640:768] = _rope(dkr, cs, MLA_ROPE // 2, inverse=True).astype(BF16)
        da_ref[:, 768:AW] = jnp.zeros((tT, AW - 768), BF16)

    def full(r, c):
        return pl.BlockSpec((r, c), lambda i: (0, 0))

    def rows(c):
        return pl.BlockSpec((tT, c), lambda i: (i, 0))

    return pl.pallas_call(
        body, name=name, grid=(T // tT,),
        in_specs=[rows(QW), rows(QW), rows(QW), pl.BlockSpec((tT, AW), lambda i: (i, a_blk)), rows(1),
                  full(1, MLA_Q_RANK), full(1, MLA_KV_RANK),
                  full(QW, MLA_Q_RANK), full(MLA_KV_RANK, 2 * QW), full(8, LANES)],
        out_specs=[rows(AW), rows(QW), rows(2 * QW), full(1, MLA_Q_RANK), full(1, MLA_KV_RANK)],
        out_shape=[jax.ShapeDtypeStruct((T, AW), BF16), jax.ShapeDtypeStruct((T, QW), BF16),
                   jax.ShapeDtypeStruct((T, 2 * QW), BF16),
                   jax.ShapeDtypeStruct((1, MLA_Q_RANK), F32), jax.ShapeDtypeStruct((1, MLA_KV_RANK), F32)],
        compiler_params=_params(("arbitrary",)),
    )(dq, dk, dv, proj, pos, qn_w, kvn_w, w_uq, w_kv, tab)


def _flash_fwd(q, k, v, *, name, exchange=None):
    T = q.shape[0]
    H = q.shape[1] // HP
    tq = min(ATTN_TILE, T)
    nq = T // tq

    sub = tq // ATTN_CHAINS

    def body(q_ref, k_ref, v_ref, o_ref, lse_ref):
        qi = pl.program_id(1)
        qs = [q_ref[c * sub:(c + 1) * sub, :] for c in range(ATTN_CHAINS)]

        def update(carry, off, masked):
            nks = [(c + 1) * sub if masked else tq for c in range(ATTN_CHAINS)]
            scores = [_dot_nt(qs[c], k_ref[pl.ds(off, nks[c]), :]) for c in range(ATTN_CHAINS)]
            out = []
            for c in range(ATTN_CHAINS):
                m_prev, l_prev, acc = carry[c]
                nk, s = nks[c], scores[c]
                vb = v_ref[pl.ds(off, nk), :]
                if masked:
                    rows = lax.broadcasted_iota(jnp.int32, (sub, nk), 0) + c * sub
                    s = jnp.where(rows >= lax.broadcasted_iota(jnp.int32, (sub, nk), 1), s, NEG)
                m_new = jnp.maximum(m_prev, jnp.max(s, axis=1, keepdims=True))
                alpha = jnp.exp(m_prev - m_new)
                p = jnp.exp(s - m_new)
                out.append((m_new, alpha * l_prev + jnp.sum(p, axis=1, keepdims=True),
                            alpha * acc + _dot(p.astype(BF16), vb)))
            return tuple(out)

        init = tuple((jnp.full((sub, 1), NEG, F32), jnp.zeros((sub, 1), F32), jnp.zeros((sub, HP), F32))
                     for _ in range(ATTN_CHAINS))
        carry = lax.fori_loop(0, qi, lambda j, cr: update(cr, pl.multiple_of(j * tq, tq), False), init)
        carry = update(carry, pl.multiple_of(qi * tq, tq), True)
        for c in range(ATTN_CHAINS):
            m_fin, l_fin, acc = carry[c]
            o_ref[c * sub:(c + 1) * sub, :] = (acc / l_fin).astype(BF16)
            lse_ref[c * sub:(c + 1) * sub, :] = jnp.broadcast_to(m_fin + jnp.log(l_fin), (sub, HP))

    qspec = pl.BlockSpec((tq, HP), lambda h, i: (i, h))
    kspec = pl.BlockSpec((T, HP), lambda h, i: (0, h))
    return _call(
        body, name=name, grid=(H, nq),
        in_specs=[qspec, kspec, kspec], out_specs=[qspec, qspec],
        out_shape=[jax.ShapeDtypeStruct((T, H * HP), BF16), jax.ShapeDtypeStruct((T, H * HP), F32)],
        scratch_shapes=[], args=(q, k, v), exchange=exchange)


def _flash_bwd(q, k, v, do, lse, delta, *, name, exchange=None):
    T = q.shape[0]
    H = q.shape[1] // HP
    tq = min(ATTN_TILE, T)
    nq = T // tq
    sub = tq // ATTN_CHAINS

    def body(k_ref, v_ref, q_ref, do_ref, lse_ref, dl_ref, dq_ref, dk_ref, dv_ref):
        ki = pl.program_id(1)

        @pl.when(ki == 0)
        def _():
            dq_ref[...] = jnp.zeros_like(dq_ref)

        def grow(a):
            return a if a.shape[0] == tq else jnp.concatenate([a, jnp.zeros((tq - a.shape[0], HP), F32)], axis=0)

        def step(carry, j, masked):
            dk_acc, dv_acc = carry
            nks = [(c + 1) * sub if masked else tq for c in range(ATTN_CHAINS)]
            rws = [pl.ds(pl.multiple_of(j * tq + c * sub, sub), sub) for c in range(ATTN_CHAINS)]
            scores = [_dot_nt(q_ref[rws[c], :], k_ref[0:nks[c], :]) for c in range(ATTN_CHAINS)]
            dps = [_dot_nt(do_ref[rws[c], :], v_ref[0:nks[c], :]) for c in range(ATTN_CHAINS)]
            for c in range(ATTN_CHAINS):
                rows, nk, s, dp = rws[c], nks[c], scores[c], dps[c]
                kb = k_ref[0:nk, :]
                qb = q_ref[rows, :]
                dob = do_ref[rows, :]
                if masked:
                    ri = lax.broadcasted_iota(jnp.int32, (sub, nk), 0) + c * sub
                    s = jnp.where(ri >= lax.broadcasted_iota(jnp.int32, (sub, nk), 1), s, NEG)
                p = jnp.exp(s - lse_ref[rows, 0:1])
                dv_acc = dv_acc + grow(_dot_tn(p.astype(BF16), dob))
                ds = (p * (dp - dl_ref[rows, 0:1])).astype(BF16)
                dk_acc = dk_acc + grow(_dot_tn(ds, qb))
                dq_ref[rows, :] += _dot(ds, kb)
            return dk_acc, dv_acc

        carry = step((jnp.zeros((tq, HP), F32), jnp.zeros((tq, HP), F32)), ki, True)
        dk_acc, dv_acc = lax.fori_loop(ki + 1, nq, lambda j, cr: step(cr, j, False), carry)
        dk_ref[...] = dk_acc
        dv_ref[...] = dv_acc.astype(BF16)

    kspec = pl.BlockSpec((tq, HP), lambda h, j: (j, h))
    full = pl.BlockSpec((T, HP), lambda h, j: (0, h))
    return _call(
        body, name=name, grid=(H, nq),
        in_specs=[kspec, kspec, full, full, full, full], out_specs=[full, kspec, kspec],
        out_shape=[jax.ShapeDtypeStruct((T, H * HP), F32), jax.ShapeDtypeStruct((T, H * HP), F32),
                   jax.ShapeDtypeStruct((T, H * HP), BF16)],
        scratch_shapes=[], args=(k, v, q, do, lse, delta), exchange=exchange)


def _ret_consts(cc, hd):
    lg = math.log(1.0 - 2.0 ** (-5.0 - hd))
    diff = (lax.broadcasted_iota(jnp.int32, (cc, cc), 0) - lax.broadcasted_iota(jnp.int32, (cc, cc), 1)).astype(F32)
    decay = jnp.where(diff >= 0, jnp.exp(jnp.maximum(diff, 0.0) * lg), 0.0)
    idx = lax.broadcasted_iota(jnp.int32, (cc, 1), 0).astype(F32)
    zeta = jnp.exp((cc - 1.0 - idx) * lg)
    xi = jnp.exp((idx + 1.0) * lg)
    return decay, zeta, xi, math.exp(cc * lg)


def _ret_fwd(proj, pos, tab, *, name):
    T = proj.shape[0]
    cc = min(RET_TILE, T)
    n = T // cc

    def body(rq_ref, rk_ref, rv_ref, pos_ref, tab_ref, y_ref, yn_ref, rprev_ref, r_s):
        @pl.when(pl.program_id(0) == 0)
        def _():
            r_s[...] = jnp.zeros_like(r_s)

        cs = _rope_cs(pos_ref[...], tab_ref)
        for hd in range(RET_HEADS):
            sl = slice(hd * HP, (hd + 1) * HP)
            decay, zeta, xi, gc = _ret_consts(cc, hd)
            q = _rope(rq_ref[:, sl].astype(F32), cs, RET_DK // 2).astype(BF16)
            kf = _rope(rk_ref[:, sl].astype(F32), cs, RET_DK // 2) * (RET_DK ** -0.5)
            k = kf.astype(BF16)
            v = rv_ref[:, sl]
            r = r_s[hd]
            rprev_ref[0, hd] = r
            inner = (_dot_nt(q, k) * decay).astype(BF16)
            y = _dot(inner, v) + _dot(q, r.astype(BF16)) * xi
            r_s[hd] = r * gc + _dot_tn((kf * zeta).astype(BF16), v)
            y_ref[:, sl] = y
            mu = jnp.mean(y, axis=-1, keepdims=True)
            yc = y - mu
            var = jnp.mean(yc * yc, axis=-1, keepdims=True)
            yn_ref[:, sl] = (yc * lax.rsqrt(var + GN_EPS)).astype(BF16)

    def blk(j):
        return pl.BlockSpec((cc, RW), lambda i: (i, j))

    return pl.pallas_call(
        body, name=name, grid=(n,),
        in_specs=[blk(0), blk(1), blk(2), pl.BlockSpec((cc, 1), lambda i: (i, 0)),
                  pl.BlockSpec((8, LANES), lambda i: (0, 0))],
        out_specs=[blk(0), blk(0), pl.BlockSpec((1, RET_HEADS, HP, RET_DV), lambda i: (i, 0, 0, 0))],
        out_shape=[jax.ShapeDtypeStruct((T, RW), F32), jax.ShapeDtypeStruct((T, RW), BF16),
                   jax.ShapeDtypeStruct((n, RET_HEADS, HP, RET_DV), F32)],
        scratch_shapes=[pltpu.VMEM((RET_HEADS, HP, RET_DV), F32)],
        compiler_params=_params(("arbitrary",)),
    )(proj, proj, proj, pos, tab)


def _ret_bwd(dyn, y, proj, pos, tab, rprev, *, name):
    T = proj.shape[0]
    cc = min(RET_TILE, T)
    n = T // cc

    def body(dyn_ref, y_ref, rq_ref, rk_ref, rv_ref, pos_ref, tab_ref, rprev_ref,
             drq_ref, drk_ref, drv_ref, dr_s):
        @pl.when(pl.program_id(0) == 0)
        def _():
            dr_s[...] = jnp.zeros_like(dr_s)

        cs = _rope_cs(pos_ref[...], tab_ref)
        for hd in range(RET_HEADS):
            sl = slice(hd * HP, (hd + 1) * HP)
            decay, zeta, xi, gc = _ret_consts(cc, hd)
            q = _rope(rq_ref[:, sl].astype(F32), cs, RET_DK // 2).astype(BF16)
            kf = _rope(rk_ref[:, sl].astype(F32), cs, RET_DK // 2) * (RET_DK ** -0.5)
            k = kf.astype(BF16)
            v = rv_ref[:, sl]
            yv = y_ref[:, sl]
            mu = jnp.mean(yv, axis=-1, keepdims=True)
            yc = yv - mu
            rs = lax.rsqrt(jnp.mean(yc * yc, axis=-1, keepdims=True) + GN_EPS)
            yn = yc * rs
            dn = dyn_ref[:, sl]
            dy = rs * (dn - jnp.mean(dn, axis=-1, keepdims=True) - yn * jnp.mean(dn * yn, axis=-1, keepdims=True))
            dyb = dy.astype(BF16)
            dyx = (dy * xi).astype(BF16)
            dr = dr_s[hd]
            drb = dr.astype(BF16)
            inner = (_dot_nt(q, k) * decay).astype(BF16)
            da = (_dot_nt(dyb, v) * decay).astype(BF16)
            dv = _dot_tn(inner, dyb) + _dot((kf * zeta).astype(BF16), drb)
            dq = _dot(da, k) + _dot_nt(dyx, rprev_ref[0, hd].astype(BF16))
            dk = _dot_tn(da, q) + _dot_nt(v, drb) * zeta
            dr_s[hd] = dr * gc + _dot_tn(q, dyx)
            drq_ref[:, sl] = _rope(dq, cs, RET_DK // 2, inverse=True).astype(BF16)
            drk_ref[:, sl] = _rope(dk * (RET_DK ** -0.5), cs, RET_DK // 2, inverse=True).astype(BF16)
            drv_ref[:, sl] = dv.astype(BF16)

    def blk(j):
        return pl.BlockSpec((cc, RW), lambda i: (n - 1 - i, j))

    return pl.pallas_call(
        body, name=name, grid=(n,),
        in_specs=[blk(0), blk(0), blk(0), blk(1), blk(2), pl.BlockSpec((cc, 1), lambda i: (n - 1 - i, 0)),
                  pl.BlockSpec((8, LANES), lambda i: (0, 0)),
                  pl.BlockSpec((1, RET_HEADS, HP, RET_DV), lambda i: (n - 1 - i, 0, 0, 0))],
        out_specs=[blk(0), blk(0), blk(0)],
        out_shape=[jax.ShapeDtypeStruct((T, RW), BF16)] * 3,
        scratch_shapes=[pltpu.VMEM((RET_HEADS, HP, RET_DV), F32)],
        compiler_params=_params(("arbitrary",)),
    )(dyn, y, proj, proj, proj, pos, tab, rprev)


def _merge_fwd(o, yn, proj, gn_w, w_bm, w_br, w_out, h, post_w, *, name):
    T, D = h.shape
    tT = min(MERGE_TILE, T)
    g_blk = PROJ_FIXED // D

    def body(o_ref, yn_ref, rg_ref, gm_ref, gr_ref, gnw_ref, wbm_ref, wbr_ref, wout_ref, h_ref, post_ref,
             omla_ref, oret_ref, m_ref, ho_ref):
        o_mla = _dot(o_ref[...], wbm_ref[...])
        rg = rg_ref[...].astype(F32)
        gated = (rg * _sigmoid(rg) * (yn_ref[...].astype(F32) * gnw_ref[...])).astype(BF16)
        o_ret = _dot(gated, wbr_ref[...])
        omla_ref[...] = o_mla.astype(BF16)
        oret_ref[...] = o_ret.astype(BF16)
        merged = _sigmoid(gm_ref[...].astype(F32)) * o_mla + _sigmoid(gr_ref[...].astype(F32)) * o_ret
        m = _dot(merged.astype(BF16), wout_ref[...])
        m_ref[...] = m
        ho_ref[...] = h_ref[...] + _rms_fwd(m, post_ref[...])

    def full(r, c):
        return pl.BlockSpec((r, c), lambda i: (0, 0))

    def rows(c, j=0):
        return pl.BlockSpec((tT, c), lambda i: (i, j))

    return pl.pallas_call(
        body, name=name, grid=(T // tT,),
        in_specs=[rows(QW), rows(RW), rows(RW, 3), rows(D, g_blk), rows(D, g_blk + 1), full(1, RW),
                  full(QW, D), full(RW, D), full(D, D), rows(D), full(1, D)],
        out_specs=[rows(D), rows(D), rows(D), rows(D)],
        out_shape=[jax.ShapeDtypeStruct((T, D), BF16), jax.ShapeDtypeStruct((T, D), BF16),
                   jax.ShapeDtypeStruct((T, D), F32), jax.ShapeDtypeStruct((T, D), F32)],
        compiler_params=_params(("parallel",)),
    )(o, yn, proj, proj, proj, gn_w, w_bm, w_br, w_out, h, post_w)


def _merge_bwd(dho, m, post_w, omla, oret, proj, yn, gn_w, o, w_out, w_bm, w_br, *, name):
    T, D = dho.shape
    tT = min(MERGE_TILE, T)
    g_blk = PROJ_FIXED // D

    def body(dho_ref, m_ref, post_ref, omla_ref, oret_ref, rg_ref, gm_ref, gr_ref, yn_ref, gnw_ref, o_ref,
             wout_ref, wbm_ref, wbr_ref,
             dm_ref, merged_ref, dgm_ref, dgr_ref, domla_ref, do_ref, delta_ref, doret_ref, gated_ref,
             drg_ref, dyn_ref, gpost_ref, ggn_ref):
        @pl.when(pl.program_id(0) == 0)
        def _():
            gpost_ref[...] = jnp.zeros_like(gpost_ref)
            ggn_ref[...] = jnp.zeros_like(ggn_ref)

        dm, gp = _rms_bwd(m_ref[...], post_ref[...], dho_ref[...])
        gpost_ref[...] += gp
        dmb = dm.astype(BF16)
        dm_ref[...] = dmb
        dmerged = _dot_nt(dmb, wout_ref[...])
        o_mla = omla_ref[...].astype(F32)
        o_ret = oret_ref[...].astype(F32)
        sgm = _sigmoid(gm_ref[...].astype(F32))
        sgr = _sigmoid(gr_ref[...].astype(F32))
        merged_ref[...] = (sgm * o_mla + sgr * o_ret).astype(BF16)
        dgm_ref[...] = (dmerged * o_mla * sgm * (1.0 - sgm)).astype(BF16)
        dgr_ref[...] = (dmerged * o_ret * sgr * (1.0 - sgr)).astype(BF16)
        domla = (dmerged * sgm).astype(BF16)
        domla_ref[...] = domla
        do = _dot_nt(domla, wbm_ref[...])
        do_ref[...] = do.astype(BF16)
        for hd in range(MLA_HEADS):
            sl = slice(hd * HP, (hd + 1) * HP)
            d = jnp.sum(do[:, sl] * o_ref[:, sl].astype(F32), axis=-1, keepdims=True)
            delta_ref[:, sl] = jnp.broadcast_to(d, (tT, HP))
        doret = (dmerged * sgr).astype(BF16)
        doret_ref[...] = doret
        dgated = _dot_nt(doret, wbr_ref[...])
        rg = rg_ref[...].astype(F32)
        sg = _sigmoid(rg)
        srg = rg * sg
        ynv = yn_ref[...].astype(F32)
        yw = ynv * gnw_ref[...]
        gated_ref[...] = (srg * yw).astype(BF16)
        drg_ref[...] = (dgated * yw * (sg * (1.0 + rg * (1.0 - sg)))).astype(BF16)
        dgs = dgated * srg
        dyn_ref[...] = dgs * gnw_ref[...]
        ggn_ref[...] += jnp.sum(dgs * ynv, axis=0, keepdims=True)

    def full(r, c):
        return pl.BlockSpec((r, c), lambda i: (0, 0))

    def rows(c, j=0):
        return pl.BlockSpec((tT, c), lambda i: (i, j))

    return pl.pallas_call(
        body, name=name, grid=(T // tT,),
        in_specs=[rows(D), rows(D), full(1, D), rows(D), rows(D), rows(RW, 3), rows(D, g_blk), rows(D, g_blk + 1),
                  rows(RW), full(1, RW), rows(QW), full(D, D), full(QW, D), full(RW, D)],
        out_specs=[rows(D), rows(D), rows(D), rows(D), rows(D), rows(QW), rows(QW), rows(D), rows(RW),
                   rows(RW), rows(RW), full(1, D), full(1, RW)],
        out_shape=[jax.ShapeDtypeStruct((T, D), BF16)] * 5
        + [jax.ShapeDtypeStruct((T, QW), BF16), jax.ShapeDtypeStruct((T, QW), F32),
           jax.ShapeDtypeStruct((T, D), BF16), jax.ShapeDtypeStruct((T, RW), BF16),
           jax.ShapeDtypeStruct((T, RW), BF16), jax.ShapeDtypeStruct((T, RW), F32),
           jax.ShapeDtypeStruct((1, D), F32), jax.ShapeDtypeStruct((1, RW), F32)],
        compiler_params=_params(("arbitrary",)),
    )(dho, m, post_w, omla, oret, proj, proj, proj, yn, gn_w, o, w_out, w_bm, w_br)


def _mesh_pos():
    return lax.axis_index("x"), lax.axis_index("y"), lax.axis_index("c")


class _Gather:
    def __init__(self, shards):
        self.operands = list(shards)
        self.n = len(shards)
        self.out_shape = [jax.ShapeDtypeStruct((N_DEV,) + s.shape, s.dtype) for s in shards]
        self.scratch = [pltpu.SemaphoreType.DMA((7 * self.n,)), pltpu.SemaphoreType.DMA((7 * self.n,)),
                        pltpu.SemaphoreType.DMA((self.n,))]

    def phase(self, p, x_refs, out_refs, sems):
        send_sems, recv_sems, local_sems = sems
        x, y, c = _mesh_pos()
        me, sibling = (x, y, c), (x, y, 1 - c)
        chips = [(1 - x, y), (x, 1 - y), (1 - x, 1 - y)]

        def copy(w, k, block, to, src=None):
            slot = out_refs[w].at[4 * block[0] + 2 * block[1] + block[2]]
            return pltpu.make_async_remote_copy(
                src_ref=slot if src is None else src, dst_ref=slot,
                send_sem=send_sems.at[7 * w + k], recv_sem=recv_sems.at[7 * w + k],
                device_id=to, device_id_type=pl.DeviceIdType.MESH)

        for w in range(self.n):
            mine = pltpu.make_async_copy(x_refs[w], out_refs[w].at[4 * x + 2 * y + c], local_sems.at[w])
            first = [copy(w, 0, me, sibling, src=x_refs[w])]
            first += [copy(w, 1 + j, me, (*chip, c), src=x_refs[w]) for j, chip in enumerate(chips)]
            passed = [copy(w, 4 + j, (*chip, c), sibling) for j, chip in enumerate(chips)]
            if p == 0:
                mine.start()
                for cp in first:
                    cp.start()
            elif p == 1:
                for j, chip in enumerate(chips):
                    copy(w, 1 + j, (*chip, c), me).wait_recv()
                    passed[j].start()
            else:
                copy(w, 0, sibling, me).wait_recv()
                for j, chip in enumerate(chips):
                    copy(w, 4 + j, (*chip, 1 - c), me).wait_recv()
                for cp in first + passed:
                    cp.wait_send()
                mine.wait()


class _Scatter:
    def __init__(self, grads, whole=()):
        self.n_sliced = len(grads)
        self.operands = list(grads) + list(whole)
        self.n = len(self.operands)
        self.out_shape = [jax.ShapeDtypeStruct(g.shape, g.dtype) for g in grads]
        self.out_shape += [jax.ShapeDtypeStruct((N_DEV,) + a.shape, a.dtype) for a in whole]
        n_sem = (N_DEV - 1) * self.n
        self.scratch = [pltpu.SemaphoreType.DMA((n_sem,)), pltpu.SemaphoreType.DMA((n_sem,)),
                        pltpu.SemaphoreType.DMA((self.n,))]

    def phase(self, p, in_refs, out_refs, sems):
        if p == 1:
            return
        send_sems, recv_sems, local_sems = sems
        x, y, c = _mesh_pos()
        me = 4 * x + 2 * y + c

        def src(w, dev):
            return in_refs[w].at[dev] if w < self.n_sliced else in_refs[w]

        for w in range(self.n):
            own = pltpu.make_async_copy(src(w, me), out_refs[w].at[me], local_sems.at[w])
            sends, recvs = [], []
            for r in range(1, N_DEV):
                px = 1 - x if r & 4 else x
                py = 1 - y if r & 2 else y
                pc = 1 - c if r & 1 else c
                peer, pidx = (px, py, pc), 4 * px + 2 * py + pc
                k = (N_DEV - 1) * w + r - 1
                sends.append(pltpu.make_async_remote_copy(
                    src_ref=src(w, pidx), dst_ref=out_refs[w].at[me], send_sem=send_sems.at[k],
                    recv_sem=recv_sems.at[k], device_id=peer, device_id_type=pl.DeviceIdType.MESH))
                recvs.append(pltpu.make_async_remote_copy(
                    src_ref=src(w, me), dst_ref=out_refs[w].at[pidx], send_sem=send_sems.at[k],
                    recv_sem=recv_sems.at[k], device_id=peer, device_id_type=pl.DeviceIdType.MESH))
            if p == 0:
                own.start()
                for cp in sends:
                    cp.start()
            else:
                for cp in recvs:
                    cp.wait_recv()
                for cp in sends:
                    cp.wait_send()
                own.wait()


def _exchange_alone(ex, *, name):
    n = ex.n

    def body(*refs):
        for p in range(3):
            ex.phase(p, refs[:n], refs[n:2 * n], refs[2 * n:])

    anyspec = pl.BlockSpec(memory_space=pl.ANY)
    return pl.pallas_call(body, name=name, out_shape=ex.out_shape, in_specs=[anyspec] * n,
                          out_specs=[anyspec] * n, scratch_shapes=ex.scratch)(*ex.operands)


def _adam_step(w_ref, p_ref, m_ref, v_ref, g_ref, d_ref, nm_ref, nv_ref):
    g = p_ref[0].astype(F32)
    for j in range(1, N_DEV):
        g = g + p_ref[j].astype(F32)
    g_ref[...] = g
    nm = ADAM_B1 * m_ref[...] + (1.0 - ADAM_B1) * g
    nv = ADAM_B2 * v_ref[...] + (1.0 - ADAM_B2) * (g * g)
    nm_ref[...] = nm
    nv_ref[...] = nv
    m_hat = nm / (1.0 - ADAM_B1 ** ADAM_STEP)
    v_hat = nv / (1.0 - ADAM_B2 ** ADAM_STEP)
    d_ref[...] = -ADAM_LR * (m_hat / (jnp.sqrt(v_hat) + ADAM_EPS) + ADAM_WD * w_ref[...])


def _adamw_vectors(ws, parts, ms, vs, *, name):
    n = len(ws)

    def body(*refs):
        w_refs, p_refs, m_refs, v_refs = (refs[i * n:(i + 1) * n] for i in range(4))
        outs = refs[4 * n:]
        for i in range(n):
            _adam_step(w_refs[i], p_refs[i], m_refs[i], v_refs[i], *outs[4 * i:4 * i + 4])

    return pl.pallas_call(
        body, name=name,
        out_shape=[jax.ShapeDtypeStruct(w.shape, F32) for w in ws for _ in range(4)],
    )(*ws, *parts, *ms, *vs)


def _adamw(w, parts, m, v, *, name):
    G, R, n = w.shape
    tn = 256 if (n > 256 and n % 256 == 0) else n
    tr = R
    for t in range(16, R, 16):
        if R % t == 0 and t * tn <= 160 * 1024:
            tr = t
    if R * tn <= 160 * 1024:
        tr = R

    def body(w_ref, p_ref, m_ref, v_ref, g_ref, d_ref, nm_ref, nv_ref):
        _adam_step(w_ref, p_ref, m_ref, v_ref, g_ref, d_ref, nm_ref, nv_ref)

    blk = pl.BlockSpec((None, tr, tn), lambda g, i, j: (g, i, j))
    return pl.pallas_call(
        body, name=name, grid=(G, R // tr, n // tn),
        in_specs=[blk, pl.BlockSpec((N_DEV, None, tr, tn), lambda g, i, j: (0, g, i, j)), blk, blk],
        out_specs=[blk, blk, blk, blk],
        out_shape=[jax.ShapeDtypeStruct((G, R, n), F32)] * 4,
        compiler_params=_params(("parallel", "parallel", "parallel")),
    )(w, parts, m, v)


def _pad_last(a, width):
    return jnp.pad(a, [(0, 0)] * (a.ndim - 1) + [(0, width - a.shape[-1])])


def _cols_of(g):
    return g.transpose(1, 0, 2).reshape(g.shape[1], N_DEV * g.shape[2])


def _col_shards(w):
    return w.reshape(w.shape[0], N_DEV, w.shape[1] // N_DEV).transpose(1, 0, 2)


def kernel(x, positions, ffn1_pre_w, ffn1_w1, ffn1_w2, ffn1_post_w, mix_pre_w, w_in, mla_q_norm_w, mla_w_uq, mla_kv_norm_w, mla_w_ukv, ret_gn_w, w_branch_mla, w_branch_ret, w_out, mix_post_w, ffn2_pre_w, ffn2_w1, ffn2_w2, ffn2_post_w, loss_target, m_ffn1_pre_w, m_ffn1_w1, m_ffn1_w2, m_ffn1_post_w, m_mix_pre_w, m_w_in, m_mla_q_norm_w, m_mla_w_uq, m_mla_kv_norm_w, m_mla_w_ukv, m_ret_gn_w, m_w_branch_mla, m_w_branch_ret, m_w_out, m_mix_post_w, m_ffn2_pre_w, m_ffn2_w1, m_ffn2_w2, m_ffn2_post_w, v_ffn1_pre_w, v_ffn1_w1, v_ffn1_w2, v_ffn1_post_w, v_mix_pre_w, v_w_in, v_mla_q_norm_w, v_mla_w_uq, v_mla_kv_norm_w, v_mla_w_ukv, v_ret_gn_w, v_w_branch_mla, v_w_branch_ret, v_w_out, v_mix_post_w, v_ffn2_pre_w, v_ffn2_w1, v_ffn2_w2, v_ffn2_post_w):
    T, D = x.shape[1], x.shape[2]
    h0 = x[0]
    tgt = loss_target[0]
    pos = positions.reshape(T, 1).astype(F32)

    big = [("ffn1_w1", ffn1_w1, m_ffn1_w1, v_ffn1_w1), ("ffn1_w2", ffn1_w2, m_ffn1_w2, v_ffn1_w2),
           ("w_in", w_in, m_w_in, v_w_in), ("mla_w_uq", mla_w_uq, m_mla_w_uq, v_mla_w_uq),
           ("mla_w_ukv", mla_w_ukv, m_mla_w_ukv, v_mla_w_ukv),
           ("w_branch_mla", w_branch_mla, m_w_branch_mla, v_w_branch_mla),
           ("w_branch_ret", w_branch_ret, m_w_branch_ret, v_w_branch_ret),
           ("w_out", w_out, m_w_out, v_w_out),
           ("ffn2_w1", ffn2_w1, m_ffn2_w1, v_ffn2_w1), ("ffn2_w2", ffn2_w2, m_ffn2_w2, v_ffn2_w2)]
    small = [("ffn1_pre_w", ffn1_pre_w, m_ffn1_pre_w, v_ffn1_pre_w), ("ffn1_post_w", ffn1_post_w, m_ffn1_post_w, v_ffn1_post_w),
             ("mix_pre_w", mix_pre_w, m_mix_pre_w, v_mix_pre_w), ("mla_q_norm_w", mla_q_norm_w, m_mla_q_norm_w, v_mla_q_norm_w),
             ("mla_kv_norm_w", mla_kv_norm_w, m_mla_kv_norm_w, v_mla_kv_norm_w), ("ret_gn_w", ret_gn_w, m_ret_gn_w, v_ret_gn_w),
             ("mix_post_w", mix_post_w, m_mix_post_w, v_mix_post_w), ("ffn2_pre_w", ffn2_pre_w, m_ffn2_pre_w, v_ffn2_pre_w),
             ("ffn2_post_w", ffn2_post_w, m_ffn2_post_w, v_ffn2_post_w)]

    half = ffn1_w2.shape[1]
    hp = -(-half // LANES) * LANES

    def rows_view(w):
        return w[0].T

    def send_w1(w):
        return jnp.pad(rows_view(w).reshape(2, half, D), ((0, 0), (0, hp - half), (0, 0))).reshape(2 * hp, D).astype(BF16)

    def send_w2(w):
        return jnp.pad(w[0], ((0, hp - half), (0, 0))).astype(BF16)

    mixer = ["w_in", "mla_w_uq", "mla_w_ukv", "w_branch_mla", "w_branch_ret", "w_out"]
    uq_w = MLA_NOPE + MLA_ROPE
    mixer_send = [rows_view(w_in).astype(BF16), jnp.pad(rows_view(mla_w_uq), ((0, HP - uq_w), (0, 0))).astype(BF16),
                  mla_w_ukv[0].astype(BF16), w_branch_mla[0].astype(BF16), w_branch_ret[0].astype(BF16),
                  w_out[0].astype(BF16)]

    w1a, w2a = _exchange_alone(_Gather([send_w1(ffn1_w1), send_w2(ffn1_w2)]), name="gather_ffn1")
    w2a = w2a.reshape(N_DEV // 2, 2 * hp, D)
    u1, f1, h1, *got = _ffn_fwd(h0, ffn1_pre_w, w1a, w2a, ffn1_post_w, None, name="ffn1_fwd_gather_mixer",
                                exchange=_Gather(mixer_send))
    fw = dict(zip(mixer, got))

    wi = fw["w_in"].reshape(-1, D)
    cq_w, ckv_w, kr_w = wi[0:384], wi[384:640], wi[640:672]
    rq_w, rk_w = wi[672:928], wi[928:1184]
    rv_w, rg_w = wi[1184:1696], wi[1696:2208]
    gm_w, gr_w = wi[2208:2208 + D], wi[2208 + D:2208 + 2 * D]
    zer = lambda n: jnp.zeros((n, D), BF16)
    head_rows = lambda a, h: jnp.pad(a.reshape(h, -1, D), ((0, 0), (0, HP - a.shape[0] // h), (0, 0))).reshape(h * HP, D)
    w_in_p = jnp.concatenate([head_rows(rq_w, RET_HEADS), head_rows(rk_w, RET_HEADS), rv_w, rg_w,
                              cq_w, ckv_w, zer(MLA_NOPE), kr_w, zer(HP - MLA_NOPE - MLA_ROPE), zer(AW - 768),
                              gm_w, gr_w], axis=0)
    w_uq_p = fw["mla_w_uq"].reshape(QW, MLA_Q_RANK)
    ukv = fw["mla_w_ukv"].transpose(1, 0, 2)
    w_kv_p = jnp.concatenate([_pad_last(ukv[:, :, :MLA_NOPE], HP).reshape(MLA_KV_RANK, QW),
                              _pad_last(ukv[:, :, MLA_NOPE:], HP).reshape(MLA_KV_RANK, QW)], axis=1)
    w_bm_p = jnp.pad(_cols_of(fw["w_branch_mla"]).reshape(MLA_HEADS, MLA_V, D),
                     ((0, 0), (0, HP - MLA_V), (0, 0))).reshape(QW, D)
    w_br, w_o = _cols_of(fw["w_branch_ret"]), fw["w_out"].reshape(D, D)
    tab_mla = _rope_table(MLA_NOPE, MLA_ROPE // 2)
    tab_ret = _rope_table(0, RET_DK // 2)

    proj, a1 = _rms_matmul(h1, mix_pre_w, w_in_p, name="mixer_in_proj")
    q, k, v, qn, kvn = _mla_prep_fwd(proj, pos, mla_q_norm_w, mla_kv_norm_w, w_uq_p, w_kv_p, tab_mla, name="mla_prep_fwd")
    o, lse, w1b, w2b = _flash_fwd(q, k, v, name="mla_attn_fwd_gather_ffn2",
                                  exchange=_Gather([send_w1(ffn2_w1), send_w2(ffn2_w2)]))
    w2b = w2b.reshape(N_DEV // 2, 2 * hp, D)
    ypre, yn, rprev = _ret_fwd(proj, pos, tab_ret, name="retention_fwd")
    omla, oret, m, h2 = _merge_fwd(o, yn, proj, ret_gn_w, w_bm_p, w_br, w_o, h1, mix_post_w, name="merge_fwd")
    u2, f2, _, dy, lossp = _ffn_fwd(h2, ffn2_pre_w, w1b, w2b, ffn2_post_w, tgt, name="ffn2_fwd_loss")
    loss = lax.psum(jnp.sum(lossp[::8, 0]), ("x", "y", "c"))

    def grad(x, dy, tag, exchange=None):
        return _matmul_tn(x if x.ndim == 3 else x[None], dy if dy.ndim == 3 else dy[None], name=tag, exchange=exchange)

    g2, du2, df2, a2, dh2, gpost2, gpre2 = _ffn_bwd(dy, f2, ffn2_post_w, h2, ffn2_pre_w, u2, w2b, w1b, name="ffn2_bwd")
    dw1b, = grad(du2.reshape(N_DEV, T, 2 * hp), a2, "ffn2_dw1")
    dw2b = grad(g2, df2, "ffn2_dw2")[0].reshape(N_DEV, hp, D)
    (dmb, merged, dgm, dgr, domla, do, delta, doret, gated, drg, dyn, gpostm, ggn) = _merge_bwd(
        dh2, m, mix_post_w, omla, oret, proj, yn, ret_gn_w, o, w_o, w_bm_p, w_br, name="merge_bwd")
    dw_out = grad(merged, dmb, "dw_out")[0][0]
    dw_bm_p = grad(o, domla, "dw_branch_mla")[0][0]
    dw_br = grad(gated, doret, "dw_branch_ret")[0][0]
    dq, dk, dv, *recv_ffn2 = _flash_bwd(q, k, v, do, lse, delta, name="mla_attn_bwd_scatter_ffn2",
                                        exchange=_Scatter([dw1b, dw2b]))
    da, dql, dkvl, gqn, gkvn = _mla_prep_bwd(dq, dk, dv, proj, pos, mla_q_norm_w, mla_kv_norm_w, w_uq_p, w_kv_p, tab_mla, name="mla_prep_bwd")
    dw_uq_p = grad(dql, qn, "dw_uq")[0][0]
    dw_kv_p = grad(kvn, dkvl, "dw_ukv")[0][0]
    drq, drk, drv = _ret_bwd(dyn, ypre, proj, pos, tab_ret, rprev, name="retention_bwd")
    dproj = jnp.concatenate([drq, drk, drv, drg, da, dgm, dgr], axis=1)
    dw_in_p = grad(dproj, a1, "dw_in")[0][0]

    dw_uq = dw_uq_p.reshape(MLA_HEADS, HP, MLA_Q_RANK)[:, :uq_w]
    dkp = dw_kv_p[:, :QW].reshape(MLA_KV_RANK, MLA_HEADS, HP)[:, :, :MLA_NOPE]
    dvp = dw_kv_p[:, QW:].reshape(MLA_KV_RANK, MLA_HEADS, HP)[:, :, :MLA_V]
    dw_ukv = jnp.concatenate([dkp, dvp], axis=2).transpose(1, 0, 2)
    dw_bm = dw_bm_p.reshape(MLA_HEADS, HP, D)[:, :MLA_V].reshape(MLA_HEADS * MLA_V, D)
    small_mixer_grads = [dw_uq, dw_ukv, _col_shards(dw_bm), _col_shards(dw_br), dw_out.reshape(N_DEV, D // N_DEV, D)]
    dh1, gmixpre, *recv_small_mixer = _proj_bwd(dproj, w_in_p, h1, mix_pre_w, dh2, name="mixer_in_bwd_scatter_small",
                                                exchange=_Scatter(small_mixer_grads))
    unhead = lambda a, h, wd: a.reshape(h, HP, D)[:, :wd].reshape(h * wd, D)
    c0 = 4 * RW
    dw_in = jnp.concatenate([
        dw_in_p[c0:c0 + 384], dw_in_p[c0 + 384:c0 + 640], dw_in_p[c0 + 640 + MLA_NOPE:c0 + 640 + MLA_NOPE + MLA_ROPE],
        unhead(dw_in_p[0:RW], RET_HEADS, RET_DK), unhead(dw_in_p[RW:2 * RW], RET_HEADS, RET_DK),
        dw_in_p[2 * RW:3 * RW], dw_in_p[3 * RW:4 * RW],
        dw_in_p[PROJ_FIXED:PROJ_FIXED + D], dw_in_p[PROJ_FIXED + D:PROJ_FIXED + 2 * D]], axis=0).reshape(N_DEV, -1, D)
    g1, du1, df1, a0, dx, gpost1, gpre1, recv_w_in = _ffn_bwd(
        dh1, f1, ffn1_post_w, h0, ffn1_pre_w, u1, w2a, w1a, name="ffn1_bwd_scatter_w_in", exchange=_Scatter([dw_in]))
    recv_mixer = [recv_w_in] + recv_small_mixer
    dw2a = grad(g1, df1, "ffn1_dw2")[0].reshape(N_DEV, hp, D)
    dw1a, recv_w2a = grad(du1.reshape(N_DEV, T, 2 * hp), a0, "ffn1_dw1_scatter_dw2", exchange=_Scatter([dw2a]))

    small_g = {"ffn1_pre_w": gpre1, "ffn1_post_w": gpost1, "mix_pre_w": gmixpre, "mla_q_norm_w": gqn,
               "mla_kv_norm_w": gkvn, "ret_gn_w": ggn, "mix_post_w": gpostm, "ffn2_pre_w": gpre2, "ffn2_post_w": gpost2}
    recv_w1a, *small_parts = _exchange_alone(_Scatter([dw1a], whole=[small_g[nm] for nm, *_ in small]),
                                             name="scatter_ffn1_dw1")
    parts = dict(zip(mixer, recv_mixer))
    parts.update(ffn1_w1=recv_w1a, ffn1_w2=recv_w2a, ffn2_w1=recv_ffn2[0], ffn2_w2=recv_ffn2[1])
    as_is = (lambda a: a, lambda p: p[:, None], lambda a: a)
    views = {nm: as_is for nm, *_ in big}
    for nm in ("ffn1_w1", "ffn2_w1"):
        views[nm] = (lambda a: rows_view(a).reshape(2, half, D), lambda p: p.reshape(N_DEV, 2, hp, D),
                     lambda a: a.reshape(2 * half, D).T[None])
    for nm in ("w_in", "mla_w_uq"):
        views[nm] = (lambda a: rows_view(a)[None], lambda p: p[:, None], lambda a: a[0].T[None])
    big_out = {}
    for nm, w, m_, v_ in big:
        to_view, parts_view, back = views[nm]
        big_out[nm] = [back(a) for a in _adamw(to_view(w), parts_view(parts[nm]), to_view(m_), to_view(v_),
                                               name="adamw_" + nm)]
    small_out = _adamw_vectors([w for _, w, _, _ in small], small_parts, [a for _, _, a, _ in small],
                               [a for _, _, _, a in small], name="adamw_replicated")

    order = ["ffn1_pre_w", "ffn1_w1", "ffn1_w2", "ffn1_post_w", "mix_pre_w", "w_in", "mla_q_norm_w", "mla_w_uq",
             "mla_kv_norm_w", "mla_w_ukv", "ret_gn_w", "w_branch_mla", "w_branch_ret", "w_out", "mix_post_w",
             "ffn2_pre_w", "ffn2_w1", "ffn2_w2", "ffn2_post_w"]
    outs = [loss, dx[None]]
    for i in range(4):
        both = {nm: big_out[nm][i] for nm in big_out}
        both.update({nm: small_out[4 * j + i] for j, (nm, *_) in enumerate(small)})
        outs += [both[nm] for nm in order]
    return tuple(outs)
```

```python
import math

import numpy as np
import jax
import jax.numpy as jnp
from jax import lax
from jax.experimental import pallas as pl
from jax.experimental.pallas import tpu as pltpu

F32, BF16 = jnp.float32, jnp.bfloat16

MLA_HEADS, MLA_NOPE, MLA_ROPE, MLA_V = 8, 64, 32, 64
MLA_Q_RANK, MLA_KV_RANK = 384, 256
RET_HEADS, RET_DK, RET_DV = 4, 64, 128
ROPE_BASE, NORM_EPS, GN_EPS = 10000.0, 1e-6, 1e-6
ADAM_LR, ADAM_B1, ADAM_B2, ADAM_EPS, ADAM_WD, ADAM_STEP = 0.001, 0.9, 0.999, 1e-08, 0.01, 10
ATTN_SCALE = 1.0 / math.sqrt(MLA_NOPE + MLA_ROPE)

N_DEV = 8
LANES = 128
HP = LANES
QW = MLA_HEADS * HP
RW = RET_HEADS * HP
AW = 1024
PROJ_FIXED = 4 * RW + AW
NEG = -1e30

TOKEN_TILE = 512
ATTN_TILE = 1024
ATTN_CHAINS = 2
FFN_CHAINS = 2
RET_TILE = 256
PROJ_TILE_CAP = 2560
GRAD_TILE_CAP = 1408
GRAD_TOKEN_TILE = 2048
MERGE_TILE = 256
VMEM_LIMIT = 56 * 1024 * 1024


def _tile(n, cap, mult=LANES):
    if n <= cap:
        return n
    best = None
    for t in range(mult, cap + 1, mult):
        if n % t == 0:
            best = t
    assert best is not None, (n, cap, mult)
    return best


def _params(sem):
    return pltpu.CompilerParams(dimension_semantics=sem, vmem_limit_bytes=VMEM_LIMIT)


def _dot(a, b):
    return lax.dot_general(a, b, (((1,), (0,)), ((), ())), preferred_element_type=F32)


def _dot_nt(a, b):
    return lax.dot_general(a, b, (((1,), (1,)), ((), ())), preferred_element_type=F32)


def _dot_tn(a, b):
    return lax.dot_general(a, b, (((0,), (0,)), ((), ())), preferred_element_type=F32)


def _sigmoid(x):
    return pl.reciprocal(1.0 + jnp.exp(-x), approx=True)


def _rms_fwd(x, w):
    r = lax.rsqrt(jnp.mean(x * x, axis=-1, keepdims=True) + NORM_EPS)
    return x * r * w


def _rms_bwd(x, w, dy):
    r = lax.rsqrt(jnp.mean(x * x, axis=-1, keepdims=True) + NORM_EPS)
    xh = x * r
    g = dy * w
    dx = r * (g - xh * jnp.mean(g * xh, axis=-1, keepdims=True))
    return dx, jnp.sum(dy * xh, axis=0, keepdims=True)


def _rope_table(first, half):
    inv = (np.float32(ROPE_BASE) ** (-(np.arange(half, dtype=np.float32) / np.float32(half)))).astype(np.float32)
    tab = np.zeros((8, LANES), np.float32)
    tab[0, first:first + half] = inv
    tab[0, first + half:first + 2 * half] = inv
    tab[1, first:first + half] = -1.0
    tab[2, first + half:first + 2 * half] = 1.0
    return jnp.asarray(tab)


def _rope_cs(pos, tab_ref):
    ang = pos * tab_ref[0:1, :]
    s = jnp.sin(ang)
    return jnp.cos(ang), s * tab_ref[1:2, :], s * tab_ref[2:3, :]


def _rope(x, cs, half, inverse=False):
    c, s1, s2 = cs
    a = pltpu.roll(x, LANES - half, 1) * s1 + pltpu.roll(x, half, 1) * s2
    return x * c - a if inverse else x * c + a


def _call(body, *, name, grid, in_specs, out_specs, out_shape, scratch_shapes, args, exchange=None):
    sem = ("arbitrary",) * len(grid)
    if exchange is None:
        return pl.pallas_call(body, name=name, grid=grid, in_specs=in_specs, out_specs=out_specs,
                              out_shape=out_shape, scratch_shapes=scratch_shapes, compiler_params=_params(sem))(*args)
    n_in, n_out, e = len(in_specs), len(out_specs), exchange.n
    total = math.prod(grid)

    def carried(*refs):
        own = refs[:n_in] + refs[n_in + e:n_in + e + n_out] + refs[n_in + 2 * e + n_out:len(refs) - 3]
        ex_refs = (refs[n_in:n_in + e], refs[n_in + e + n_out:n_in + 2 * e + n_out], refs[len(refs) - 3:])
        step = pl.program_id(0)
        for d in range(1, len(grid)):
            step = step * grid[d] + pl.program_id(d)

        @pl.when(step == 0)
        def _():
            exchange.phase(0, *ex_refs)

        @pl.when(step == (3 * total) // 4)
        def _():
            exchange.phase(1, *ex_refs)

        body(*own)

        @pl.when(step == total - 1)
        def _():
            exchange.phase(2, *ex_refs)

    anyspec = pl.BlockSpec(memory_space=pl.ANY)
    return pl.pallas_call(
        carried, name=name, grid=grid, in_specs=list(in_specs) + [anyspec] * e,
        out_specs=list(out_specs) + [anyspec] * e, out_shape=list(out_shape) + exchange.out_shape,
        scratch_shapes=list(scratch_shapes) + exchange.scratch, compiler_params=_params(sem),
    )(*args, *exchange.operands)


def _ffn_fwd(h, pre_w, w1, w2, post_w, target, *, name, exchange=None):
    T, D = h.shape
    nk, ck = w2.shape[0], w2.shape[1]
    tT = min(TOKEN_TILE, T)
    nT = T // tT
    with_loss = target is not None

    def body(*refs):
        if with_loss:
            (h_ref, pre_ref, w1g_ref, w1u_ref, w2_ref, post_ref, tgt_ref,
             u_ref, f_ref, ho_ref, dy_ref, loss_ref, a_s, acc) = refs
        else:
            (h_ref, pre_ref, w1g_ref, w1u_ref, w2_ref, post_ref,
             u_ref, f_ref, ho_ref, a_s, acc) = refs
        k = pl.program_id(1)

        @pl.when(k == 0)
        def _():
            a_s[...] = _rms_fwd(h_ref[...], pre_ref[...]).astype(BF16)
            acc[...] = jnp.zeros_like(acc)

        for c in range(FFN_CHAINS):
            rs = slice(c * (tT // FFN_CHAINS), (c + 1) * (tT // FFN_CHAINS))
            a = a_s[rs, :]
            ug = _dot_nt(a, w1g_ref[...])
            uu = _dot_nt(a, w1u_ref[...])
            u_ref[0, rs, :] = ug.astype(BF16)
            u_ref[1, rs, :] = uu.astype(BF16)
            acc[rs, :] += _dot((ug * _sigmoid(ug) * uu).astype(BF16), w2_ref[...])

        @pl.when(k == nk - 1)
        def _():
            f = acc[...]
            f_ref[...] = f
            ho = h_ref[...] + 0.5 * _rms_fwd(f, post_ref[...])
            ho_ref[...] = ho
            if with_loss:
                e = ho - tgt_ref[...]
                dy_ref[...] = e * (1.0 / D)
                loss_ref[...] = jnp.full(loss_ref.shape, (0.5 / D) * jnp.sum(e * e), F32)

    row = pl.BlockSpec((tT, D), lambda i, k: (i, 0))
    vec = pl.BlockSpec((1, D), lambda i, k: (0, 0))
    in_specs = [row, vec,
                pl.BlockSpec((None, ck, D), lambda i, k: (k, 0, 0)),
                pl.BlockSpec((None, ck, D), lambda i, k: (nk + k, 0, 0)),
                pl.BlockSpec((None, ck, D), lambda i, k: (k, 0, 0)),
                vec]
    out_shape = [jax.ShapeDtypeStruct((2, nk, T, ck), BF16),
                 jax.ShapeDtypeStruct((T, D), F32),
                 jax.ShapeDtypeStruct((T, D), F32)]
    out_specs = [pl.BlockSpec((2, None, tT, ck), lambda i, k: (0, k, i, 0)), row, row]
    args = [h, pre_w, w1, w1, w2, post_w]
    if with_loss:
        in_specs.append(row)
        args.append(target)
        out_shape += [jax.ShapeDtypeStruct((T, D), F32), jax.ShapeDtypeStruct((nT * 8, LANES), F32)]
        out_specs += [row, pl.BlockSpec((8, LANES), lambda i, k: (i, 0))]
    return _call(body, name=name, grid=(nT, nk), in_specs=in_specs, out_specs=out_specs, out_shape=out_shape,
                 scratch_shapes=[pltpu.VMEM((tT, D), BF16), pltpu.VMEM((tT, D), F32)], args=args, exchange=exchange)


def _ffn_bwd(dho, f, post_w, h, pre_w, u, w2, w1, *, name, exchange=None):
    T, D = h.shape
    nk, ck = w2.shape[0], w2.shape[1]
    tT = min(TOKEN_TILE, T)
    nT = T // tT

    def body(dho_ref, f_ref, post_ref, h_ref, pre_ref, u_ref, w2_ref, w1g_ref, w1u_ref,
             g_ref, du_ref, df_ref, a_ref, dh_ref, gpost_ref, gpre_ref, df_s, da_acc):
        i, k = pl.program_id(0), pl.program_id(1)

        @pl.when(jnp.logical_and(i == 0, k == 0))
        def _():
            gpost_ref[...] = jnp.zeros_like(gpost_ref)
            gpre_ref[...] = jnp.zeros_like(gpre_ref)

        @pl.when(k == 0)
        def _():
            dx, dw = _rms_bwd(f_ref[...], post_ref[...], 0.5 * dho_ref[...])
            dfb = dx.astype(BF16)
            df_s[...] = dfb
            df_ref[...] = dfb
            gpost_ref[...] += dw
            a_ref[...] = _rms_fwd(h_ref[...], pre_ref[...]).astype(BF16)
            da_acc[...] = jnp.zeros_like(da_acc)

        groups = [slice(c * (tT // FFN_CHAINS), (c + 1) * (tT // FFN_CHAINS)) for c in range(FFN_CHAINS)]
        dgs = [_dot_nt(df_s[rs, :], w2_ref[...]) for rs in groups]
        for rs, dg in zip(groups, dgs):
            ug = u_ref[0, rs, :].astype(F32)
            uu = u_ref[1, rs, :].astype(F32)
            sg = _sigmoid(ug)
            sl = ug * sg
            g_ref[rs, :] = (sl * uu).astype(BF16)
            dug = (dg * uu * (sg + sl * (1.0 - sg))).astype(BF16)
            duu = (dg * sl).astype(BF16)
            du_ref[0, rs, :] = dug
            du_ref[1, rs, :] = duu
            da_acc[rs, :] += _dot(dug, w1g_ref[...]) + _dot(duu, w1u_ref[...])

        @pl.when(k == nk - 1)
        def _():
            dx, dw = _rms_bwd(h_ref[...], pre_ref[...], da_acc[...])
            dh_ref[...] = dho_ref[...] + dx
            gpre_ref[...] += dw

    row = pl.BlockSpec((tT, D), lambda i, k: (i, 0))
    vec = pl.BlockSpec((1, D), lambda i, k: (0, 0))
    return _call(
        body, name=name, grid=(nT, nk),
        in_specs=[row, row, vec, row, vec,
                  pl.BlockSpec((2, None, tT, ck), lambda i, k: (0, k, i, 0)),
                  pl.BlockSpec((None, ck, D), lambda i, k: (k, 0, 0)),
                  pl.BlockSpec((None, ck, D), lambda i, k: (k, 0, 0)),
                  pl.BlockSpec((None, ck, D), lambda i, k: (nk + k, 0, 0))],
        out_specs=[pl.BlockSpec((None, tT, ck), lambda i, k: (k, i, 0)),
                   pl.BlockSpec((2, None, tT, ck), lambda i, k: (0, k, i, 0)),
                   row, row, row, vec, vec],
        out_shape=[jax.ShapeDtypeStruct((nk, T, ck), BF16),
                   jax.ShapeDtypeStruct((2, nk, T, ck), BF16),
                   jax.ShapeDtypeStruct((T, D), BF16),
                   jax.ShapeDtypeStruct((T, D), BF16),
                   jax.ShapeDtypeStruct((T, D), F32),
                   jax.ShapeDtypeStruct((1, D), F32),
                   jax.ShapeDtypeStruct((1, D), F32)],
        scratch_shapes=[pltpu.VMEM((tT, D), BF16), pltpu.VMEM((tT, D), F32)],
        args=(dho, f, post_w, h, pre_w, u, w2, w1, w1), exchange=exchange)


def _matmul_tn(x, dy, *, name, exchange=None):
    Px, T, K = x.shape
    Py, _, N = dy.shape
    P = max(Px, Py)
    tT, tK, tN = min(GRAD_TOKEN_TILE, T), _tile(K, GRAD_TILE_CAP), _tile(N, GRAD_TILE_CAP)
    nt = T // tT

    def body(x_ref, dy_ref, o_ref, acc):
        t = pl.program_id(3)

        @pl.when(t == 0)
        def _():
            acc[...] = jnp.zeros_like(acc)

        acc[...] += _dot_tn(x_ref[...], dy_ref[...])

        @pl.when(t == nt - 1)
        def _():
            o_ref[...] = acc[...].astype(BF16)

    return _call(
        body, name=name, grid=(P, K // tK, N // tN, nt),
        in_specs=[pl.BlockSpec((None, tT, tK), lambda p, a, b, t: (p if Px > 1 else 0, t, a)),
                  pl.BlockSpec((None, tT, tN), lambda p, a, b, t: (p if Py > 1 else 0, t, b))],
        out_specs=[pl.BlockSpec((None, tK, tN), lambda p, a, b, t: (p, a, b))],
        out_shape=[jax.ShapeDtypeStruct((P, K, N), BF16)],
        scratch_shapes=[pltpu.VMEM((tK, tN), F32)], args=(x, dy), exchange=exchange)


def _rms_matmul(h, wn, w, *, name):
    T, D = h.shape
    N = w.shape[0]
    tT, tN = min(TOKEN_TILE, T), _tile(N, PROJ_TILE_CAP)

    def body(h_ref, wn_ref, w_ref, y_ref, a_ref):
        @pl.when(pl.program_id(1) == 0)
        def _():
            a_ref[...] = _rms_fwd(h_ref[...], wn_ref[...]).astype(BF16)

        y_ref[...] = _dot_nt(a_ref[...], w_ref[...]).astype(BF16)

    return pl.pallas_call(
        body, name=name, grid=(T // tT, N // tN),
        in_specs=[pl.BlockSpec((tT, D), lambda i, j: (i, 0)),
                  pl.BlockSpec((1, D), lambda i, j: (0, 0)),
                  pl.BlockSpec((tN, D), lambda i, j: (j, 0))],
        out_specs=[pl.BlockSpec((tT, tN), lambda i, j: (i, j)),
                   pl.BlockSpec((tT, D), lambda i, j: (i, 0))],
        out_shape=[jax.ShapeDtypeStruct((T, N), BF16), jax.ShapeDtypeStruct((T, D), BF16)],
        compiler_params=_params(("parallel", "arbitrary")),
    )(h, wn, w)


def _proj_bwd(dproj, w, h, wn, dres, *, name, exchange=None):
    T, D = h.shape
    N = w.shape[0]
    tT, tN = min(TOKEN_TILE, T), _tile(N, PROJ_TILE_CAP)
    nn = N // tN

    def body(dp_ref, w_ref, h_ref, wn_ref, dres_ref, dh_ref, gw_ref, acc):
        i, j = pl.program_id(0), pl.program_id(1)

        @pl.when(jnp.logical_and(i == 0, j == 0))
        def _():
            gw_ref[...] = jnp.zeros_like(gw_ref)

        @pl.when(j == 0)
        def _():
            acc[...] = jnp.zeros_like(acc)

        acc[...] += _dot(dp_ref[...], w_ref[...])

        @pl.when(j == nn - 1)
        def _():
            dx, dw = _rms_bwd(h_ref[...], wn_ref[...], acc[...])
            dh_ref[...] = dres_ref[...] + dx
            gw_ref[...] += dw

    row = pl.BlockSpec((tT, D), lambda i, j: (i, 0))
    vec = pl.BlockSpec((1, D), lambda i, j: (0, 0))
    return _call(
        body, name=name, grid=(T // tT, nn),
        in_specs=[pl.BlockSpec((tT, tN), lambda i, j: (i, j)),
                  pl.BlockSpec((tN, D), lambda i, j: (j, 0)), row, vec, row],
        out_specs=[row, vec],
        out_shape=[jax.ShapeDtypeStruct((T, D), F32), jax.ShapeDtypeStruct((1, D), F32)],
        scratch_shapes=[pltpu.VMEM((tT, D), F32)], args=(dproj, w, h, wn, dres), exchange=exchange)


def _mla_prep_fwd(proj, pos, qn_w, kvn_w, w_uq, w_kv, tab, *, name):
    T = proj.shape[0]
    tT = min(TOKEN_TILE, T)
    a_blk = PROJ_FIXED // AW - 1

    def body(a_ref, pos_ref, qnw_ref, kvnw_ref, wuq_ref, wkv_ref, tab_ref,
             q_ref, k_ref, v_ref, qn_ref, kvn_ref):
        cq = a_ref[:, 0:MLA_Q_RANK].astype(F32)
        ckv = a_ref[:, MLA_Q_RANK:MLA_Q_RANK + MLA_KV_RANK].astype(F32)
        kr = a_ref[:, 640:768].astype(F32)
        qn = _rms_fwd(cq, qnw_ref[...]).astype(BF16)
        kvn = _rms_fwd(ckv, kvnw_ref[...]).astype(BF16)
        qn_ref[...] = qn
        kvn_ref[...] = kvn
        cs = _rope_cs(pos_ref[...], tab_ref)
        q = _dot_nt(qn, wuq_ref[...])
        kv = _dot(kvn, wkv_ref[...])
        krr = _rope(kr, cs, MLA_ROPE // 2)
        for hd in range(MLA_HEADS):
            sl = slice(hd * HP, (hd + 1) * HP)
            q_ref[:, sl] = (_rope(q[:, sl], cs, MLA_ROPE // 2) * ATTN_SCALE).astype(BF16)
            k_ref[:, sl] = (kv[:, sl] + krr).astype(BF16)
        v_ref[...] = kv[:, QW:].astype(BF16)

    def full(r, c):
        return pl.BlockSpec((r, c), lambda i: (0, 0))

    def rows(c):
        return pl.BlockSpec((tT, c), lambda i: (i, 0))

    return pl.pallas_call(
        body, name=name, grid=(T // tT,),
        in_specs=[pl.BlockSpec((tT, AW), lambda i: (i, a_blk)), rows(1),
                  full(1, MLA_Q_RANK), full(1, MLA_KV_RANK),
                  full(QW, MLA_Q_RANK), full(MLA_KV_RANK, 2 * QW), full(8, LANES)],
        out_specs=[rows(QW), rows(QW), rows(QW), rows(MLA_Q_RANK), rows(MLA_KV_RANK)],
        out_shape=[jax.ShapeDtypeStruct((T, QW), BF16)] * 3
        + [jax.ShapeDtypeStruct((T, MLA_Q_RANK), BF16), jax.ShapeDtypeStruct((T, MLA_KV_RANK), BF16)],
        compiler_params=_params(("parallel",)),
    )(proj, pos, qn_w, kvn_w, w_uq, w_kv, tab)


def _mla_prep_bwd(dq, dk, dv, proj, pos, qn_w, kvn_w, w_uq, w_kv, tab, *, name):
    T = proj.shape[0]
    tT = min(TOKEN_TILE, T)
    a_blk = PROJ_FIXED // AW - 1

    def body(dq_ref, dk_ref, dv_ref, a_ref, pos_ref, qnw_ref, kvnw_ref, wuq_ref, wkv_ref, tab_ref,
             da_ref, dql_ref, dkvl_ref, gqn_ref, gkvn_ref):
        @pl.when(pl.program_id(0) == 0)
        def _():
            gqn_ref[...] = jnp.zeros_like(gqn_ref)
            gkvn_ref[...] = jnp.zeros_like(gkvn_ref)

        cs = _rope_cs(pos_ref[...], tab_ref)
        dkr = jnp.zeros((tT, HP), F32)
        for hd in range(MLA_HEADS):
            sl = slice(hd * HP, (hd + 1) * HP)
            dql_ref[:, sl] = (_rope(dq_ref[:, sl], cs, MLA_ROPE // 2, inverse=True) * ATTN_SCALE).astype(BF16)
            dkh = dk_ref[:, sl]
            dkr = dkr + dkh
            dkvl_ref[:, sl] = dkh.astype(BF16)
        dkvl_ref[:, QW:] = dv_ref[...]
        dqn = _dot(dql_ref[...], wuq_ref[...])
        dkvn = _dot_nt(dkvl_ref[...], wkv_ref[...])
        cq = a_ref[:, 0:MLA_Q_RANK].astype(F32)
        ckv = a_ref[:, MLA_Q_RANK:MLA_Q_RANK + MLA_KV_RANK].astype(F32)
        dcq, gq = _rms_bwd(cq, qnw_ref[...], dqn)
        dckv, gkv = _rms_bwd(ckv, kvnw_ref[...], dkvn)
        gqn_ref[...] += gq
        gkvn_ref[...] += gkv
        da_ref[:, 0:MLA_Q_RANK] = dcq.astype(BF16)
        da_ref[:, MLA_Q_RANK:MLA_Q_RANK + MLA_KV_RANK] = dckv.astype(BF16)
        da_ref[:, 640:768] = _rope(dkr, cs, MLA_ROPE // 2, inverse=True).astype(BF16)
        da_ref[:, 768:AW] = jnp.zeros((tT, AW - 768), BF16)

    def full(r, c):
        return pl.BlockSpec((r, c), lambda i: (0, 0))

    def rows(c):
        return pl.BlockSpec((tT, c), lambda i: (i, 0))

    return pl.pallas_call(
        body, name=name, grid=(T // tT,),
        in_specs=[rows(QW), rows(QW), rows(QW), pl.BlockSpec((tT, AW), lambda i: (i, a_blk)), rows(1),
                  full(1, MLA_Q_RANK), full(1, MLA_KV_RANK),
                  full(QW, MLA_Q_RANK), full(MLA_KV_RANK, 2 * QW), full(8, LANES)],
        out_specs=[rows(AW), rows(QW), rows(2 * QW), full(1, MLA_Q_RANK), full(1, MLA_KV_RANK)],
        out_shape=[jax.ShapeDtypeStruct((T, AW), BF16), jax.ShapeDtypeStruct((T, QW), BF16),
                   jax.ShapeDtypeStruct((T, 2 * QW), BF16),
                   jax.ShapeDtypeStruct((1, MLA_Q_RANK), F32), jax.ShapeDtypeStruct((1, MLA_KV_RANK), F32)],
        compiler_params=_params(("arbitrary",)),
    )(dq, dk, dv, proj, pos, qn_w, kvn_w, w_uq, w_kv, tab)


def _flash_fwd(q, k, v, *, name, exchange=None):
    T = q.shape[0]
    H = q.shape[1] // HP
    tq = min(ATTN_TILE, T)
    nq = T // tq

    sub = tq // ATTN_CHAINS

    def body(q_ref, k_ref, v_ref, o_ref, lse_ref):
        qi = pl.program_id(1)
        qs = [q_ref[c * sub:(c + 1) * sub, :] for c in range(ATTN_CHAINS)]

        def update(carry, off, masked):
            nks = [(c + 1) * sub if masked else tq for c in range(ATTN_CHAINS)]
            scores = [_dot_nt(qs[c], k_ref[pl.ds(off, nks[c]), :]) for c in range(ATTN_CHAINS)]
            out = []
            for c in range(ATTN_CHAINS):
                m_prev, l_prev, acc = carry[c]
                nk, s = nks[c], scores[c]
                vb = v_ref[pl.ds(off, nk), :]
                if masked:
                    rows = lax.broadcasted_iota(jnp.int32, (sub, nk), 0) + c * sub
                    s = jnp.where(rows >= lax.broadcasted_iota(jnp.int32, (sub, nk), 1), s, NEG)
                m_new = jnp.maximum(m_prev, jnp.max(s, axis=1, keepdims=True))
                alpha = jnp.exp(m_prev - m_new)
                p = jnp.exp(s - m_new)
                out.append((m_new, alpha * l_prev + jnp.sum(p, axis=1, keepdims=True),
                            alpha * acc + _dot(p.astype(BF16), vb)))
            return tuple(out)

        init = tuple((jnp.full((sub, 1), NEG, F32), jnp.zeros((sub, 1), F32), jnp.zeros((sub, HP), F32))
                     for _ in range(ATTN_CHAINS))
        carry = lax.fori_loop(0, qi, lambda j, cr: update(cr, pl.multiple_of(j * tq, tq), False), init)
        carry = update(carry, pl.multiple_of(qi * tq, tq), True)
        for c in range(ATTN_CHAINS):
            m_fin, l_fin, acc = carry[c]
            o_ref[c * sub:(c + 1) * sub, :] = (acc / l_fin).astype(BF16)
            lse_ref[c * sub:(c + 1) * sub, :] = jnp.broadcast_to(m_fin + jnp.log(l_fin), (sub, HP))

    qspec = pl.BlockSpec((tq, HP), lambda h, i: (i, h))
    kspec = pl.BlockSpec((T, HP), lambda h, i: (0, h))
    return _call(
        body, name=name, grid=(H, nq),
        in_specs=[qspec, kspec, kspec], out_specs=[qspec, qspec],
        out_shape=[jax.ShapeDtypeStruct((T, H * HP), BF16), jax.ShapeDtypeStruct((T, H * HP), F32)],
        scratch_shapes=[], args=(q, k, v), exchange=exchange)


def _flash_bwd(q, k, v, do, lse, delta, *, name, exchange=None):
    T = q.shape[0]
    H = q.shape[1] // HP
    tq = min(ATTN_TILE, T)
    nq = T // tq
    sub = tq // ATTN_CHAINS

    def body(k_ref, v_ref, q_ref, do_ref, lse_ref, dl_ref, dq_ref, dk_ref, dv_ref):
        ki = pl.program_id(1)

        @pl.when(ki == 0)
        def _():
            dq_ref[...] = jnp.zeros_like(dq_ref)

        def grow(a):
            return a if a.shape[0] == tq else jnp.concatenate([a, jnp.zeros((tq - a.shape[0], HP), F32)], axis=0)

        def step(carry, j, masked):
            dk_acc, dv_acc = carry
            nks = [(c + 1) * sub if masked else tq for c in range(ATTN_CHAINS)]
            rws = [pl.ds(pl.multiple_of(j * tq + c * sub, sub), sub) for c in range(ATTN_CHAINS)]
            scores = [_dot_nt(q_ref[rws[c], :], k_ref[0:nks[c], :]) for c in range(ATTN_CHAINS)]
            dps = [_dot_nt(do_ref[rws[c], :], v_ref[0:nks[c], :]) for c in range(ATTN_CHAINS)]
            for c in range(ATTN_CHAINS):
                rows, nk, s, dp = rws[c], nks[c], scores[c], dps[c]
                kb = k_ref[0:nk, :]
                qb = q_ref[rows, :]
                dob = do_ref[rows, :]
                if masked:
                    ri = lax.broadcasted_iota(jnp.int32, (sub, nk), 0) + c * sub
                    s = jnp.where(ri >= lax.broadcasted_iota(jnp.int32, (sub, nk), 1), s, NEG)
                p = jnp.exp(s - lse_ref[rows, 0:1])
                dv_acc = dv_acc + grow(_dot_tn(p.astype(BF16), dob))
                ds = (p * (dp - dl_ref[rows, 0:1])).astype(BF16)
                dk_acc = dk_acc + grow(_dot_tn(ds, qb))
                dq_ref[rows, :] += _dot(ds, kb)
            return dk_acc, dv_acc

        carry = step((jnp.zeros((tq, HP), F32), jnp.zeros((tq, HP), F32)), ki, True)
        dk_acc, dv_acc = lax.fori_loop(ki + 1, nq, lambda j, cr: step(cr, j, False), carry)
        dk_ref[...] = dk_acc
        dv_ref[...] = dv_acc.astype(BF16)

    kspec = pl.BlockSpec((tq, HP), lambda h, j: (j, h))
    full = pl.BlockSpec((T, HP), lambda h, j: (0, h))
    return _call(
        body, name=name, grid=(H, nq),
        in_specs=[kspec, kspec, full, full, full, full], out_specs=[full, kspec, kspec],
        out_shape=[jax.ShapeDtypeStruct((T, H * HP), F32), jax.ShapeDtypeStruct((T, H * HP), F32),
                   jax.ShapeDtypeStruct((T, H * HP), BF16)],
        scratch_shapes=[], args=(k, v, q, do, lse, delta), exchange=exchange)


def _ret_consts(cc, hd):
    lg = math.log(1.0 - 2.0 ** (-5.0 - hd))
    diff = (lax.broadcasted_iota(jnp.int32, (cc, cc), 0) - lax.broadcasted_iota(jnp.int32, (cc, cc), 1)).astype(F32)
    decay = jnp.where(diff >= 0, jnp.exp(jnp.maximum(diff, 0.0) * lg), 0.0)
    idx = lax.broadcasted_iota(jnp.int32, (cc, 1), 0).astype(F32)
    zeta = jnp.exp((cc - 1.0 - idx) * lg)
    xi = jnp.exp((idx + 1.0) * lg)
    return decay, zeta, xi, math.exp(cc * lg)


def _ret_fwd(proj, pos, tab, *, name):
    T = proj.shape[0]
    cc = min(RET_TILE, T)
    n = T // cc

    def body(rq_ref, rk_ref, rv_ref, pos_ref, tab_ref, y_ref, yn_ref, rprev_ref, r_s):
        @pl.when(pl.program_id(0) == 0)
        def _():
            r_s[...] = jnp.zeros_like(r_s)

        cs = _rope_cs(pos_ref[...], tab_ref)
        for hd in range(RET_HEADS):
            sl = slice(hd * HP, (hd + 1) * HP)
            decay, zeta, xi, gc = _ret_consts(cc, hd)
            q = _rope(rq_ref[:, sl].astype(F32), cs, RET_DK // 2).astype(BF16)
            kf = _rope(rk_ref[:, sl].astype(F32), cs, RET_DK // 2) * (RET_DK ** -0.5)
            k = kf.astype(BF16)
            v = rv_ref[:, sl]
            r = r_s[hd]
            rprev_ref[0, hd] = r
            inner = (_dot_nt(q, k) * decay).astype(BF16)
            y = _dot(inner, v) + _dot(q, r.astype(BF16)) * xi
            r_s[hd] = r * gc + _dot_tn((kf * zeta).astype(BF16), v)
            y_ref[:, sl] = y
            mu = jnp.mean(y, axis=-1, keepdims=True)
            yc = y - mu
            var = jnp.mean(yc * yc, axis=-1, keepdims=True)
            yn_ref[:, sl] = (yc * lax.rsqrt(var + GN_EPS)).astype(BF16)

    def blk(j):
        return pl.BlockSpec((cc, RW), lambda i: (i, j))

    return pl.pallas_call(
        body, name=name, grid=(n,),
        in_specs=[blk(0), blk(1), blk(2), pl.BlockSpec((cc, 1), lambda i: (i, 0)),
                  pl.BlockSpec((8, LANES), lambda i: (0, 0))],
        out_specs=[blk(0), blk(0), pl.BlockSpec((1, RET_HEADS, HP, RET_DV), lambda i: (i, 0, 0, 0))],
        out_shape=[jax.ShapeDtypeStruct((T, RW), F32), jax.ShapeDtypeStruct((T, RW), BF16),
                   jax.ShapeDtypeStruct((n, RET_HEADS, HP, RET_DV), F32)],
        scratch_shapes=[pltpu.VMEM((RET_HEADS, HP, RET_DV), F32)],
        compiler_params=_params(("arbitrary",)),
    )(proj, proj, proj, pos, tab)


def _ret_bwd(dyn, y, proj, pos, tab, rprev, *, name):
    T = proj.shape[0]
    cc = min(RET_TILE, T)
    n = T // cc

    def body(dyn_ref, y_ref, rq_ref, rk_ref, rv_ref, pos_ref, tab_ref, rprev_ref,
             drq_ref, drk_ref, drv_ref, dr_s):
        @pl.when(pl.program_id(0) == 0)
        def _():
            dr_s[...] = jnp.zeros_like(dr_s)

        cs = _rope_cs(pos_ref[...], tab_ref)
        for hd in range(RET_HEADS):
            sl = slice(hd * HP, (hd + 1) * HP)
            decay, zeta, xi, gc = _ret_consts(cc, hd)
            q = _rope(rq_ref[:, sl].astype(F32), cs, RET_DK // 2).astype(BF16)
            kf = _rope(rk_ref[:, sl].astype(F32), cs, RET_DK // 2) * (RET_DK ** -0.5)
            k = kf.astype(BF16)
            v = rv_ref[:, sl]
            yv = y_ref[:, sl]
            mu = jnp.mean(yv, axis=-1, keepdims=True)
            yc = yv - mu
            rs = lax.rsqrt(jnp.mean(yc * yc, axis=-1, keepdims=True) + GN_EPS)
            yn = yc * rs
            dn = dyn_ref[:, sl]
            dy = rs * (dn - jnp.mean(dn, axis=-1, keepdims=True) - yn * jnp.mean(dn * yn, axis=-1, keepdims=True))
            dyb = dy.astype(BF16)
            dyx = (dy * xi).astype(BF16)
            dr = dr_s[hd]
            drb = dr.astype(BF16)
            inner = (_dot_nt(q, k) * decay).astype(BF16)
            da = (_dot_nt(dyb, v) * decay).astype(BF16)
            dv = _dot_tn(inner, dyb) + _dot((kf * zeta).astype(BF16), drb)
            dq = _dot(da, k) + _dot_nt(dyx, rprev_ref[0, hd].astype(BF16))
            dk = _dot_tn(da, q) + _dot_nt(v, drb) * zeta
            dr_s[hd] = dr * gc + _dot_tn(q, dyx)
            drq_ref[:, sl] = _rope(dq, cs, RET_DK // 2, inverse=True).astype(BF16)
            drk_ref[:, sl] = _rope(dk * (RET_DK ** -0.5), cs, RET_DK // 2, inverse=True).astype(BF16)
            drv_ref[:, sl] = dv.astype(BF16)

    def blk(j):
        return pl.BlockSpec((cc, RW), lambda i: (n - 1 - i, j))

    return pl.pallas_call(
        body, name=name, grid=(n,),
        in_specs=[blk(0), blk(0), blk(0), blk(1), blk(2), pl.BlockSpec((cc, 1), lambda i: (n - 1 - i, 0)),
                  pl.BlockSpec((8, LANES), lambda i: (0, 0)),
                  pl.BlockSpec((1, RET_HEADS, HP, RET_DV), lambda i: (n - 1 - i, 0, 0, 0))],
        out_specs=[blk(0), blk(0), blk(0)],
        out_shape=[jax.ShapeDtypeStruct((T, RW), BF16)] * 3,
        scratch_shapes=[pltpu.VMEM((RET_HEADS, HP, RET_DV), F32)],
        compiler_params=_params(("arbitrary",)),
    )(dyn, y, proj, proj, proj, pos, tab, rprev)


def _merge_fwd(o, yn, proj, gn_w, w_bm, w_br, w_out, h, post_w, *, name):
    T, D = h.shape
    tT = min(MERGE_TILE, T)
    g_blk = PROJ_FIXED // D

    def body(o_ref, yn_ref, rg_ref, gm_ref, gr_ref, gnw_ref, wbm_ref, wbr_ref, wout_ref, h_ref, post_ref,
             omla_ref, oret_ref, m_ref, ho_ref):
        o_mla = _dot(o_ref[...], wbm_ref[...])
        rg = rg_ref[...].astype(F32)
        gated = (rg * _sigmoid(rg) * (yn_ref[...].astype(F32) * gnw_ref[...])).astype(BF16)
        o_ret = _dot(gated, wbr_ref[...])
        omla_ref[...] = o_mla.astype(BF16)
        oret_ref[...] = o_ret.astype(BF16)
        merged = _sigmoid(gm_ref[...].astype(F32)) * o_mla + _sigmoid(gr_ref[...].astype(F32)) * o_ret
        m = _dot(merged.astype(BF16), wout_ref[...])
        m_ref[...] = m
        ho_ref[...] = h_ref[...] + _rms_fwd(m, post_ref[...])

    def full(r, c):
        return pl.BlockSpec((r, c), lambda i: (0, 0))

    def rows(c, j=0):
        return pl.BlockSpec((tT, c), lambda i: (i, j))

    return pl.pallas_call(
        body, name=name, grid=(T // tT,),
        in_specs=[rows(QW), rows(RW), rows(RW, 3), rows(D, g_blk), rows(D, g_blk + 1), full(1, RW),
                  full(QW, D), full(RW, D), full(D, D), rows(D), full(1, D)],
        out_specs=[rows(D), rows(D), rows(D), rows(D)],
        out_shape=[jax.ShapeDtypeStruct((T, D), BF16), jax.ShapeDtypeStruct((T, D), BF16),
                   jax.ShapeDtypeStruct((T, D), F32), jax.ShapeDtypeStruct((T, D), F32)],
        compiler_params=_params(("parallel",)),
    )(o, yn, proj, proj, proj, gn_w, w_bm, w_br, w_out, h, post_w)


def _merge_bwd(dho, m, post_w, omla, oret, proj, yn, gn_w, o, w_out, w_bm, w_br, *, name):
    T, D = dho.shape
    tT = min(MERGE_TILE, T)
    g_blk = PROJ_FIXED // D

    def body(dho_ref, m_ref, post_ref, omla_ref, oret_ref, rg_ref, gm_ref, gr_ref, yn_ref, gnw_ref, o_ref,
             wout_ref, wbm_ref, wbr_ref,
             dm_ref, merged_ref, dgm_ref, dgr_ref, domla_ref, do_ref, delta_ref, doret_ref, gated_ref,
             drg_ref, dyn_ref, gpost_ref, ggn_ref):
        @pl.when(pl.program_id(0) == 0)
        def _():
            gpost_ref[...] = jnp.zeros_like(gpost_ref)
            ggn_ref[...] = jnp.zeros_like(ggn_ref)

        dm, gp = _rms_bwd(m_ref[...], post_ref[...], dho_ref[...])
        gpost_ref[...] += gp
        dmb = dm.astype(BF16)
        dm_ref[...] = dmb
        dmerged = _dot_nt(dmb, wout_ref[...])
        o_mla = omla_ref[...].astype(F32)
        o_ret = oret_ref[...].astype(F32)
        sgm = _sigmoid(gm_ref[...].astype(F32))
        sgr = _sigmoid(gr_ref[...].astype(F32))
        merged_ref[...] = (sgm * o_mla + sgr * o_ret).astype(BF16)
        dgm_ref[...] = (dmerged * o_mla * sgm * (1.0 - sgm)).astype(BF16)
        dgr_ref[...] = (dmerged * o_ret * sgr * (1.0 - sgr)).astype(BF16)
        domla = (dmerged * sgm).astype(BF16)
        domla_ref[...] = domla
        do = _dot_nt(domla, wbm_ref[...])
        do_ref[...] = do.astype(BF16)
        for hd in range(MLA_HEADS):
            sl = slice(hd * HP, (hd + 1) * HP)
            d = jnp.sum(do[:, sl] * o_ref[:, sl].astype(F32), axis=-1, keepdims=True)
            delta_ref[:, sl] = jnp.broadcast_to(d, (tT, HP))
        doret = (dmerged * sgr).astype(BF16)
        doret_ref[...] = doret
        dgated = _dot_nt(doret, wbr_ref[...])
        rg = rg_ref[...].astype(F32)
        sg = _sigmoid(rg)
        srg = rg * sg
        ynv = yn_ref[...].astype(F32)
        yw = ynv * gnw_ref[...]
        gated_ref[...] = (srg * yw).astype(BF16)
        drg_ref[...] = (dgated * yw * (sg * (1.0 + rg * (1.0 - sg)))).astype(BF16)
        dgs = dgated * srg
        dyn_ref[...] = dgs * gnw_ref[...]
        ggn_ref[...] += jnp.sum(dgs * ynv, axis=0, keepdims=True)

    def full(r, c):
        return pl.BlockSpec((r, c), lambda i: (0, 0))

    def rows(c, j=0):
        return pl.BlockSpec((tT, c), lambda i: (i, j))

    return pl.pallas_call(
        body, name=name, grid=(T // tT,),
        in_specs=[rows(D), rows(D), full(1, D), rows(D), rows(D), rows(RW, 3), rows(D, g_blk), rows(D, g_blk + 1),
                  rows(RW), full(1, RW), rows(QW), full(D, D), full(QW, D), full(RW, D)],
        out_specs=[rows(D), rows(D), rows(D), rows(D), rows(D), rows(QW), rows(QW), rows(D), rows(RW),
                   rows(RW), rows(RW), full(1, D), full(1, RW)],
        out_shape=[jax.ShapeDtypeStruct((T, D), BF16)] * 5
        + [jax.ShapeDtypeStruct((T, QW), BF16), jax.ShapeDtypeStruct((T, QW), F32),
           jax.ShapeDtypeStruct((T, D), BF16), jax.ShapeDtypeStruct((T, RW), BF16),
           jax.ShapeDtypeStruct((T, RW), BF16), jax.ShapeDtypeStruct((T, RW), F32),
           jax.ShapeDtypeStruct((1, D), F32), jax.ShapeDtypeStruct((1, RW), F32)],
        compiler_params=_params(("arbitrary",)),
    )(dho, m, post_w, omla, oret, proj, proj, proj, yn, gn_w, o, w_out, w_bm, w_br)


def _mesh_pos():
    return lax.axis_index("x"), lax.axis_index("y"), lax.axis_index("c")


class _Gather:
    def __init__(self, shards):
        self.operands = list(shards)
        self.n = len(shards)
        self.out_shape = [jax.ShapeDtypeStruct((N_DEV,) + s.shape, s.dtype) for s in shards]
        self.scratch = [pltpu.SemaphoreType.DMA((7 * self.n,)), pltpu.SemaphoreType.DMA((7 * self.n,)),
                        pltpu.SemaphoreType.DMA((self.n,))]

    def phase(self, p, x_refs, out_refs, sems):
        send_sems, recv_sems, local_sems = sems
        x, y, c = _mesh_pos()
        me, sibling = (x, y, c), (x, y, 1 - c)
        chips = [(1 - x, y), (x, 1 - y), (1 - x, 1 - y)]

        def copy(w, k, block, to, src=None):
            slot = out_refs[w].at[4 * block[0] + 2 * block[1] + block[2]]
            return pltpu.make_async_remote_copy(
                src_ref=slot if src is None else src, dst_ref=slot,
                send_sem=send_sems.at[7 * w + k], recv_sem=recv_sems.at[7 * w + k],
                device_id=to, device_id_type=pl.DeviceIdType.MESH)

        for w in range(self.n):
            mine = pltpu.make_async_copy(x_refs[w], out_refs[w].at[4 * x + 2 * y + c], local_sems.at[w])
            first = [copy(w, 0, me, sibling, src=x_refs[w])]
            first += [copy(w, 1 + j, me, (*chip, c), src=x_refs[w]) for j, chip in enumerate(chips)]
            passed = [copy(w, 4 + j, (*chip, c), sibling) for j, chip in enumerate(chips)]
            if p == 0:
                mine.start()
                for cp in first:
                    cp.start()
            elif p == 1:
                for j, chip in enumerate(chips):
                    copy(w, 1 + j, (*chip, c), me).wait_recv()
                    passed[j].start()
            else:
                copy(w, 0, sibling, me).wait_recv()
                for j, chip in enumerate(chips):
                    copy(w, 4 + j, (*chip, 1 - c), me).wait_recv()
                for cp in first + passed:
                    cp.wait_send()
                mine.wait()


class _Scatter:
    def __init__(self, grads, whole=()):
        self.n_sliced = len(grads)
        self.operands = list(grads) + list(whole)
        self.n = len(self.operands)
        self.out_shape = [jax.ShapeDtypeStruct(g.shape, g.dtype) for g in grads]
        self.out_shape += [jax.ShapeDtypeStruct((N_DEV,) + a.shape, a.dtype) for a in whole]
        n_sem = (N_DEV - 1) * self.n
        self.scratch = [pltpu.SemaphoreType.DMA((n_sem,)), pltpu.SemaphoreType.DMA((n_sem,)),
                        pltpu.SemaphoreType.DMA((self.n,))]

    def phase(self, p, in_refs, out_refs, sems):
        if p == 1:
            return
        send_sems, recv_sems, local_sems = sems
        x, y, c = _mesh_pos()
        me = 4 * x + 2 * y + c

        def src(w, dev):
            return in_refs[w].at[dev] if w < self.n_sliced else in_refs[w]

        for w in range(self.n):
            own = None if local_sems is None else pltpu.make_async_copy(src(w, me), out_refs[w].at[me], local_sems.at[w])
            sends, recvs = [], []
            for r in range(1, N_DEV):
                px = 1 - x if r & 4 else x
                py = 1 - y if r & 2 else y
                pc = 1 - c if r & 1 else c
                peer, pidx = (px, py, pc), 4 * px + 2 * py + pc
                k = (N_DEV - 1) * w + r - 1
                sends.append(pltpu.make_async_remote_copy(
                    src_ref=src(w, pidx), dst_ref=out_refs[w].at[me], send_sem=send_sems.at[k],
                    recv_sem=recv_sems.at[k], device_id=peer, device_id_type=pl.DeviceIdType.MESH))
                recvs.append(pltpu.make_async_remote_copy(
                    src_ref=src(w, me), dst_ref=out_refs[w].at[pidx], send_sem=send_sems.at[k],
                    recv_sem=recv_sems.at[k], device_id=peer, device_id_type=pl.DeviceIdType.MESH))
            if p == 0:
                if own is not None:
                    own.start()
                for cp in sends:
                    cp.start()
            else:
                for cp in recvs:
                    cp.wait_recv()
                for cp in sends:
                    cp.wait_send()
                if own is not None:
                    own.wait()

    def own_slices(self, in_refs, out_refs, local_sems):
        x, y, c = _mesh_pos()
        me = 4 * x + 2 * y + c
        copies = [pltpu.make_async_copy(in_refs[w].at[me] if w < self.n_sliced else in_refs[w],
                                        out_refs[w].at[me], local_sems.at[w]) for w in range(self.n)]
        for cp in copies:
            cp.start()
        for cp in copies:
            cp.wait()


def _scatter_behind(ex, work, *, name):
    n = ex.n
    n_sem = (N_DEV - 1) * n
    anyspec = pl.BlockSpec(memory_space=pl.ANY)
    hbm = pl.BlockSpec(memory_space=pltpu.HBM)
    sem = pl.BlockSpec(memory_space=pltpu.SEMAPHORE)
    effect = pltpu.CompilerParams(has_side_effects=pltpu.SideEffectType.DATAFLOW_SIDE_EFFECTING)
    in_hbm = lambda a: pltpu.with_memory_space_constraint(a, pltpu.HBM)
    buffers = [pltpu.HBM(a.shape, a.dtype) for a in ex.operands] + [pltpu.HBM(s.shape, s.dtype) for s in ex.out_shape]

    def own_body(*refs):
        ex.own_slices(refs[:n], refs[n:2 * n], refs[2 * n])

    lands = pl.pallas_call(own_body, name=name + "_own", out_shape=ex.out_shape, in_specs=[anyspec] * n,
                           out_specs=[anyspec] * n, scratch_shapes=[pltpu.SemaphoreType.DMA((n,))])(*ex.operands)

    def start_body(*refs):
        ex.phase(0, refs[:n], refs[n:2 * n], (refs[2 * n], refs[2 * n + 1], None))
        refs[-1][...] = jnp.zeros_like(refs[-1])

    started = pl.pallas_call(
        start_body, name=name + "_start",
        out_shape=[pltpu.SemaphoreType.DMA((n_sem,)), pltpu.SemaphoreType.DMA((n_sem,))] + buffers
        + [jax.ShapeDtypeStruct((8, LANES), F32)],
        in_specs=[hbm] * (2 * n), out_specs=[sem, sem] + [hbm] * (2 * n) + [pl.BlockSpec(memory_space=pltpu.VMEM)],
        input_output_aliases={i: 2 + i for i in range(2 * n)}, compiler_params=effect,
    )(*[in_hbm(a) for a in ex.operands], *[in_hbm(a) for a in lands])
    send_sems, recv_sems, token = started[0], started[1], started[-1]
    after, result = work(token)

    def wait_body(*refs):
        ex.phase(2, refs[:n], refs[n:2 * n], (refs[2 * n], refs[2 * n + 1], None))

    done = pl.pallas_call(
        wait_body, name=name + "_wait", out_shape=buffers,
        in_specs=[hbm] * (2 * n) + [sem, sem] + [anyspec] * len(after), out_specs=[hbm] * (2 * n),
        input_output_aliases={i: i for i in range(2 * n)}, compiler_params=effect,
    )(*started[2:2 + 2 * n], send_sems, recv_sems, *after)
    return done[n:], result


def _exchange_alone(ex, *, name):
    n = ex.n

    def body(*refs):
        for p in range(3):
            ex.phase(p, refs[:n], refs[n:2 * n], refs[2 * n:])

    anyspec = pl.BlockSpec(memory_space=pl.ANY)
    return pl.pallas_call(body, name=name, out_shape=ex.out_shape, in_specs=[anyspec] * n,
                          out_specs=[anyspec] * n, scratch_shapes=ex.scratch)(*ex.operands)


def _adam_step(w_ref, p_ref, m_ref, v_ref, g_ref, d_ref, nm_ref, nv_ref):
    g = p_ref[0].astype(F32)
    for j in range(1, N_DEV):
        g = g + p_ref[j].astype(F32)
    g_ref[...] = g
    nm = ADAM_B1 * m_ref[...] + (1.0 - ADAM_B1) * g
    nv = ADAM_B2 * v_ref[...] + (1.0 - ADAM_B2) * (g * g)
    nm_ref[...] = nm
    nv_ref[...] = nv
    m_hat = nm / (1.0 - ADAM_B1 ** ADAM_STEP)
    v_hat = nv / (1.0 - ADAM_B2 ** ADAM_STEP)
    d_ref[...] = -ADAM_LR * (m_hat / (jnp.sqrt(v_hat) + ADAM_EPS) + ADAM_WD * w_ref[...])


def _adamw_vectors(ws, parts, ms, vs, *, name):
    n = len(ws)

    def body(*refs):
        w_refs, p_refs, m_refs, v_refs = (refs[i * n:(i + 1) * n] for i in range(4))
        outs = refs[4 * n:]
        for i in range(n):
            _adam_step(w_refs[i], p_refs[i], m_refs[i], v_refs[i], *outs[4 * i:4 * i + 4])

    return pl.pallas_call(
        body, name=name,
        out_shape=[jax.ShapeDtypeStruct(w.shape, F32) for w in ws for _ in range(4)],
    )(*ws, *parts, *ms, *vs)


def _adamw(w, parts, m, v, after, *, name):
    G, R, n = w.shape
    tn = 256 if (n > 256 and n % 256 == 0) else n
    tr = R
    for t in range(16, R, 16):
        if R % t == 0 and t * tn <= 160 * 1024:
            tr = t
    if R * tn <= 160 * 1024:
        tr = R

    def body(w_ref, p_ref, m_ref, v_ref, after_ref, g_ref, d_ref, nm_ref, nv_ref):
        _adam_step(w_ref, p_ref, m_ref, v_ref, g_ref, d_ref, nm_ref, nv_ref)

    blk = pl.BlockSpec((None, tr, tn), lambda g, i, j: (g, i, j))
    return pl.pallas_call(
        body, name=name, grid=(G, R // tr, n // tn),
        in_specs=[blk, pl.BlockSpec((N_DEV, None, tr, tn), lambda g, i, j: (0, g, i, j)), blk, blk,
                  pl.BlockSpec((8, LANES), lambda g, i, j: (0, 0))],
        out_specs=[blk, blk, blk, blk],
        out_shape=[jax.ShapeDtypeStruct((G, R, n), F32)] * 4,
        compiler_params=_params(("parallel", "parallel", "parallel")),
    )(w, parts, m, v, after)


def _pad_last(a, width):
    return jnp.pad(a, [(0, 0)] * (a.ndim - 1) + [(0, width - a.shape[-1])])


def _cols_of(g):
    return g.transpose(1, 0, 2).reshape(g.shape[1], N_DEV * g.shape[2])


def _col_shards(w):
    return w.reshape(w.shape[0], N_DEV, w.shape[1] // N_DEV).transpose(1, 0, 2)


def kernel(x, positions, ffn1_pre_w, ffn1_w1, ffn1_w2, ffn1_post_w, mix_pre_w, w_in, mla_q_norm_w, mla_w_uq, mla_kv_norm_w, mla_w_ukv, ret_gn_w, w_branch_mla, w_branch_ret, w_out, mix_post_w, ffn2_pre_w, ffn2_w1, ffn2_w2, ffn2_post_w, loss_target, m_ffn1_pre_w, m_ffn1_w1, m_ffn1_w2, m_ffn1_post_w, m_mix_pre_w, m_w_in, m_mla_q_norm_w, m_mla_w_uq, m_mla_kv_norm_w, m_mla_w_ukv, m_ret_gn_w, m_w_branch_mla, m_w_branch_ret, m_w_out, m_mix_post_w, m_ffn2_pre_w, m_ffn2_w1, m_ffn2_w2, m_ffn2_post_w, v_ffn1_pre_w, v_ffn1_w1, v_ffn1_w2, v_ffn1_post_w, v_mix_pre_w, v_w_in, v_mla_q_norm_w, v_mla_w_uq, v_mla_kv_norm_w, v_mla_w_ukv, v_ret_gn_w, v_w_branch_mla, v_w_branch_ret, v_w_out, v_mix_post_w, v_ffn2_pre_w, v_ffn2_w1, v_ffn2_w2, v_ffn2_post_w):
    T, D = x.shape[1], x.shape[2]
    h0 = x[0]
    tgt = loss_target[0]
    pos = positions.reshape(T, 1).astype(F32)

    big = [("ffn1_w1", ffn1_w1, m_ffn1_w1, v_ffn1_w1), ("ffn1_w2", ffn1_w2, m_ffn1_w2, v_ffn1_w2),
           ("w_in", w_in, m_w_in, v_w_in), ("mla_w_uq", mla_w_uq, m_mla_w_uq, v_mla_w_uq),
           ("mla_w_ukv", mla_w_ukv, m_mla_w_ukv, v_mla_w_ukv),
           ("w_branch_mla", w_branch_mla, m_w_branch_mla, v_w_branch_mla),
           ("w_branch_ret", w_branch_ret, m_w_branch_ret, v_w_branch_ret),
           ("w_out", w_out, m_w_out, v_w_out),
           ("ffn2_w1", ffn2_w1, m_ffn2_w1, v_ffn2_w1), ("ffn2_w2", ffn2_w2, m_ffn2_w2, v_ffn2_w2)]
    small = [("ffn1_pre_w", ffn1_pre_w, m_ffn1_pre_w, v_ffn1_pre_w), ("ffn1_post_w", ffn1_post_w, m_ffn1_post_w, v_ffn1_post_w),
             ("mix_pre_w", mix_pre_w, m_mix_pre_w, v_mix_pre_w), ("mla_q_norm_w", mla_q_norm_w, m_mla_q_norm_w, v_mla_q_norm_w),
             ("mla_kv_norm_w", mla_kv_norm_w, m_mla_kv_norm_w, v_mla_kv_norm_w), ("ret_gn_w", ret_gn_w, m_ret_gn_w, v_ret_gn_w),
             ("mix_post_w", mix_post_w, m_mix_post_w, v_mix_post_w), ("ffn2_pre_w", ffn2_pre_w, m_ffn2_pre_w, v_ffn2_pre_w),
             ("ffn2_post_w", ffn2_post_w, m_ffn2_post_w, v_ffn2_post_w)]

    half = ffn1_w2.shape[1]
    hp = -(-half // LANES) * LANES

    def rows_view(w):
        return w[0].T

    def send_w1(w):
        return jnp.pad(rows_view(w).reshape(2, half, D), ((0, 0), (0, hp - half), (0, 0))).reshape(2 * hp, D).astype(BF16)

    def send_w2(w):
        return jnp.pad(w[0], ((0, hp - half), (0, 0))).astype(BF16)

    mixer = ["w_in", "mla_w_uq", "mla_w_ukv", "w_branch_mla", "w_branch_ret", "w_out"]
    uq_w = MLA_NOPE + MLA_ROPE
    mixer_send = [rows_view(w_in).astype(BF16), jnp.pad(rows_view(mla_w_uq), ((0, HP - uq_w), (0, 0))).astype(BF16),
                  mla_w_ukv[0].astype(BF16), w_branch_mla[0].astype(BF16), w_branch_ret[0].astype(BF16),
                  w_out[0].astype(BF16)]

    w1a, w2a = _exchange_alone(_Gather([send_w1(ffn1_w1), send_w2(ffn1_w2)]), name="gather_ffn1")
    w2a = w2a.reshape(N_DEV // 2, 2 * hp, D)
    u1, f1, h1, *got = _ffn_fwd(h0, ffn1_pre_w, w1a, w2a, ffn1_post_w, None, name="ffn1_fwd_gather_mixer",
                                exchange=_Gather(mixer_send))
    fw = dict(zip(mixer, got))

    wi = fw["w_in"].reshape(-1, D)
    cq_w, ckv_w, kr_w = wi[0:384], wi[384:640], wi[640:672]
    rq_w, rk_w = wi[672:928], wi[928:1184]
    rv_w, rg_w = wi[1184:1696], wi[1696:2208]
    gm_w, gr_w = wi[2208:2208 + D], wi[2208 + D:2208 + 2 * D]
    zer = lambda n: jnp.zeros((n, D), BF16)
    head_rows = lambda a, h: jnp.pad(a.reshape(h, -1, D), ((0, 0), (0, HP - a.shape[0] // h), (0, 0))).reshape(h * HP, D)
    w_in_p = jnp.concatenate([head_rows(rq_w, RET_HEADS), head_rows(rk_w, RET_HEADS), rv_w, rg_w,
                              cq_w, ckv_w, zer(MLA_NOPE), kr_w, zer(HP - MLA_NOPE - MLA_ROPE), zer(AW - 768),
                              gm_w, gr_w], axis=0)
    w_uq_p = fw["mla_w_uq"].reshape(QW, MLA_Q_RANK)
    ukv = fw["mla_w_ukv"].transpose(1, 0, 2)
    w_kv_p = jnp.concatenate([_pad_last(ukv[:, :, :MLA_NOPE], HP).reshape(MLA_KV_RANK, QW),
                              _pad_last(ukv[:, :, MLA_NOPE:], HP).reshape(MLA_KV_RANK, QW)], axis=1)
    w_bm_p = jnp.pad(_cols_of(fw["w_branch_mla"]).reshape(MLA_HEADS, MLA_V, D),
                     ((0, 0), (0, HP - MLA_V), (0, 0))).reshape(QW, D)
    w_br, w_o = _cols_of(fw["w_branch_ret"]), fw["w_out"].reshape(D, D)
    tab_mla = _rope_table(MLA_NOPE, MLA_ROPE // 2)
    tab_ret = _rope_table(0, RET_DK // 2)

    proj, a1 = _rms_matmul(h1, mix_pre_w, w_in_p, name="mixer_in_proj")
    q, k, v, qn, kvn = _mla_prep_fwd(proj, pos, mla_q_norm_w, mla_kv_norm_w, w_uq_p, w_kv_p, tab_mla, name="mla_prep_fwd")
    o, lse, w1b, w2b = _flash_fwd(q, k, v, name="mla_attn_fwd_gather_ffn2",
                                  exchange=_Gather([send_w1(ffn2_w1), send_w2(ffn2_w2)]))
    w2b = w2b.reshape(N_DEV // 2, 2 * hp, D)
    ypre, yn, rprev = _ret_fwd(proj, pos, tab_ret, name="retention_fwd")
    omla, oret, m, h2 = _merge_fwd(o, yn, proj, ret_gn_w, w_bm_p, w_br, w_o, h1, mix_post_w, name="merge_fwd")
    u2, f2, _, dy, lossp = _ffn_fwd(h2, ffn2_pre_w, w1b, w2b, ffn2_post_w, tgt, name="ffn2_fwd_loss")
    loss = lax.psum(jnp.sum(lossp[::8, 0]), ("x", "y", "c"))

    def grad(x, dy, tag, exchange=None):
        return _matmul_tn(x if x.ndim == 3 else x[None], dy if dy.ndim == 3 else dy[None], name=tag, exchange=exchange)

    g2, du2, df2, a2, dh2, gpost2, gpre2 = _ffn_bwd(dy, f2, ffn2_post_w, h2, ffn2_pre_w, u2, w2b, w1b, name="ffn2_bwd")
    dw1b, = grad(du2.reshape(N_DEV, T, 2 * hp), a2, "ffn2_dw1")
    dw2b = grad(g2, df2, "ffn2_dw2")[0].reshape(N_DEV, hp, D)
    (dmb, merged, dgm, dgr, domla, do, delta, doret, gated, drg, dyn, gpostm, ggn) = _merge_bwd(
        dh2, m, mix_post_w, omla, oret, proj, yn, ret_gn_w, o, w_o, w_bm_p, w_br, name="merge_bwd")
    dw_out = grad(merged, dmb, "dw_out")[0][0]
    dw_bm_p = grad(o, domla, "dw_branch_mla")[0][0]
    dw_br = grad(gated, doret, "dw_branch_ret")[0][0]
    dq, dk, dv, *recv_ffn2 = _flash_bwd(q, k, v, do, lse, delta, name="mla_attn_bwd_scatter_ffn2",
                                        exchange=_Scatter([dw1b, dw2b]))
    da, dql, dkvl, gqn, gkvn = _mla_prep_bwd(dq, dk, dv, proj, pos, mla_q_norm_w, mla_kv_norm_w, w_uq_p, w_kv_p, tab_mla, name="mla_prep_bwd")
    dw_uq_p = grad(dql, qn, "dw_uq")[0][0]
    dw_kv_p = grad(kvn, dkvl, "dw_ukv")[0][0]
    drq, drk, drv = _ret_bwd(dyn, ypre, proj, pos, tab_ret, rprev, name="retention_bwd")
    dproj = jnp.concatenate([drq, drk, drv, drg, da, dgm, dgr], axis=1)
    dw_in_p = grad(dproj, a1, "dw_in")[0][0]

    dw_uq = dw_uq_p.reshape(MLA_HEADS, HP, MLA_Q_RANK)[:, :uq_w]
    dkp = dw_kv_p[:, :QW].reshape(MLA_KV_RANK, MLA_HEADS, HP)[:, :, :MLA_NOPE]
    dvp = dw_kv_p[:, QW:].reshape(MLA_KV_RANK, MLA_HEADS, HP)[:, :, :MLA_V]
    dw_ukv = jnp.concatenate([dkp, dvp], axis=2).transpose(1, 0, 2)
    dw_bm = dw_bm_p.reshape(MLA_HEADS, HP, D)[:, :MLA_V].reshape(MLA_HEADS * MLA_V, D)
    small_mixer_grads = [dw_uq, dw_ukv, _col_shards(dw_bm), _col_shards(dw_br), dw_out.reshape(N_DEV, D // N_DEV, D)]
    dh1, gmixpre, *recv_small_mixer = _proj_bwd(dproj, w_in_p, h1, mix_pre_w, dh2, name="mixer_in_bwd_scatter_small",
                                                exchange=_Scatter(small_mixer_grads))
    unhead = lambda a, h, wd: a.reshape(h, HP, D)[:, :wd].reshape(h * wd, D)
    c0 = 4 * RW
    dw_in = jnp.concatenate([
        dw_in_p[c0:c0 + 384], dw_in_p[c0 + 384:c0 + 640], dw_in_p[c0 + 640 + MLA_NOPE:c0 + 640 + MLA_NOPE + MLA_ROPE],
        unhead(dw_in_p[0:RW], RET_HEADS, RET_DK), unhead(dw_in_p[RW:2 * RW], RET_HEADS, RET_DK),
        dw_in_p[2 * RW:3 * RW], dw_in_p[3 * RW:4 * RW],
        dw_in_p[PROJ_FIXED:PROJ_FIXED + D], dw_in_p[PROJ_FIXED + D:PROJ_FIXED + 2 * D]], axis=0).reshape(N_DEV, -1, D)
    g1, du1, df1, a0, dx, gpost1, gpre1, recv_w_in = _ffn_bwd(
        dh1, f1, ffn1_post_w, h0, ffn1_pre_w, u1, w2a, w1a, name="ffn1_bwd_scatter_w_in", exchange=_Scatter([dw_in]))
    recv_mixer = [recv_w_in] + recv_small_mixer
    dw2a = grad(g1, df1, "ffn1_dw2")[0].reshape(N_DEV, hp, D)
    dw1a, recv_w2a = grad(du1.reshape(N_DEV, T, 2 * hp), a0, "ffn1_dw1_scatter_dw2", exchange=_Scatter([dw2a]))

    small_g = {"ffn1_pre_w": gpre1, "ffn1_post_w": gpost1, "mix_pre_w": gmixpre, "mla_q_norm_w": gqn,
               "mla_kv_norm_w": gkvn, "ret_gn_w": ggn, "mix_post_w": gpostm, "ffn2_pre_w": gpre2, "ffn2_post_w": gpost2}
    parts = dict(zip(mixer, recv_mixer))
    parts.update(ffn1_w2=recv_w2a, ffn2_w1=recv_ffn2[0], ffn2_w2=recv_ffn2[1])
    as_is = (lambda a: a, lambda p: p[:, None], lambda a: a)
    views = {nm: as_is for nm, *_ in big}
    for nm in ("ffn1_w1", "ffn2_w1"):
        views[nm] = (lambda a: rows_view(a).reshape(2, half, D), lambda p: p.reshape(N_DEV, 2, hp, D),
                     lambda a: a.reshape(2 * half, D).T[None])
    for nm in ("w_in", "mla_w_uq"):
        views[nm] = (lambda a: rows_view(a)[None], lambda p: p[:, None], lambda a: a[0].T[None])

    def update(nm, w, m_, v_, after):
        to_view, parts_view, back = views[nm]
        return [back(a) for a in _adamw(to_view(w), parts_view(parts[nm]), to_view(m_), to_view(v_), after,
                                        name="adamw_" + nm)]

    def other_updates(token):
        done = {nm: update(nm, w, m_, v_, token) for nm, w, m_, v_ in big if nm != "ffn1_w1"}
        return [d[0] for d in done.values()], done

    (recv_w1a, *small_parts), big_out = _scatter_behind(
        _Scatter([dw1a], whole=[small_g[nm] for nm, *_ in small]), other_updates, name="scatter_ffn1_dw1")
    parts["ffn1_w1"] = recv_w1a
    big_out["ffn1_w1"] = update("ffn1_w1", ffn1_w1, m_ffn1_w1, v_ffn1_w1, jnp.zeros((8, LANES), F32))
    small_out = _adamw_vectors([w for _, w, _, _ in small], small_parts, [a for _, _, a, _ in small],
                               [a for _, _, _, a in small], name="adamw_replicated")

    order = ["ffn1_pre_w", "ffn1_w1", "ffn1_w2", "ffn1_post_w", "mix_pre_w", "w_in", "mla_q_norm_w", "mla_w_uq",
             "mla_kv_norm_w", "mla_w_ukv", "ret_gn_w", "w_branch_mla", "w_branch_ret", "w_out", "mix_post_w",
             "ffn2_pre_w", "ffn2_w1", "ffn2_w2", "ffn2_post_w"]
    outs = [loss, dx[None]]
    for i in range(4):
        both = {nm: big_out[nm][i] for nm in big_out}
        both.update({nm: small_out[4 * j + i] for j, (nm, *_) in enumerate(small)})
        outs += [both[nm] for nm in order]
    return tuple(outs)
```

```python
import math

import numpy as np
import jax
import jax.numpy as jnp
from jax import lax
from jax.experimental import pallas as pl
from jax.experimental.pallas import tpu as pltpu

F32, BF16 = jnp.float32, jnp.bfloat16

MLA_HEADS, MLA_NOPE, MLA_ROPE, MLA_V = 8, 64, 32, 64
MLA_Q_RANK, MLA_KV_RANK = 384, 256
RET_HEADS, RET_DK, RET_DV = 4, 64, 128
ROPE_BASE, NORM_EPS, GN_EPS = 10000.0, 1e-6, 1e-6
ADAM_LR, ADAM_B1, ADAM_B2, ADAM_EPS, ADAM_WD, ADAM_STEP = 0.001, 0.9, 0.999, 1e-08, 0.01, 10
ATTN_SCALE = 1.0 / math.sqrt(MLA_NOPE + MLA_ROPE)

N_DEV = 8
LANES = 128
HP = LANES
QW = MLA_HEADS * HP
RW = RET_HEADS * HP
AW = 1024
PROJ_FIXED = 4 * RW + AW
NEG = -1e30

TOKEN_TILE = 512
ATTN_TILE = 1024
ATTN_CHAINS = 2
FFN_CHAINS = 2
RET_TILE = 256
PROJ_TILE_CAP = 2560
GRAD_TILE_CAP = 1408
GRAD_TOKEN_TILE = 2048
MERGE_TILE = 256
VMEM_LIMIT = 56 * 1024 * 1024


def _tile(n, cap, mult=LANES):
    if n <= cap:
        return n
    best = None
    for t in range(mult, cap + 1, mult):
        if n % t == 0:
            best = t
    assert best is not None, (n, cap, mult)
    return best


def _params(sem):
    return pltpu.CompilerParams(dimension_semantics=sem, vmem_limit_bytes=VMEM_LIMIT)


def _dot(a, b):
    return lax.dot_general(a, b, (((1,), (0,)), ((), ())), preferred_element_type=F32)


def _dot_nt(a, b):
    return lax.dot_general(a, b, (((1,), (1,)), ((), ())), preferred_element_type=F32)


def _dot_tn(a, b):
    return lax.dot_general(a, b, (((0,), (0,)), ((), ())), preferred_element_type=F32)


def _sigmoid(x):
    return pl.reciprocal(1.0 + jnp.exp(-x), approx=True)


def _rms_fwd(x, w):
    r = lax.rsqrt(jnp.mean(x * x, axis=-1, keepdims=True) + NORM_EPS)
    return x * r * w


def _rms_bwd(x, w, dy):
    r = lax.rsqrt(jnp.mean(x * x, axis=-1, keepdims=True) + NORM_EPS)
    xh = x * r
    g = dy * w
    dx = r * (g - xh * jnp.mean(g * xh, axis=-1, keepdims=True))
    return dx, jnp.sum(dy * xh, axis=0, keepdims=True)


def _rope_table(first, half):
    inv = (np.float32(ROPE_BASE) ** (-(np.arange(half, dtype=np.float32) / np.float32(half)))).astype(np.float32)
    tab = np.zeros((8, LANES), np.float32)
    tab[0, first:first + half] = inv
    tab[0, first + half:first + 2 * half] = inv
    tab[1, first:first + half] = -1.0
    tab[2, first + half:first + 2 * half] = 1.0
    return jnp.asarray(tab)


def _rope_cs(pos, tab_ref):
    ang = pos * tab_ref[0:1, :]
    s = jnp.sin(ang)
    return jnp.cos(ang), s * tab_ref[1:2, :], s * tab_ref[2:3, :]


def _rope(x, cs, half, inverse=False):
    c, s1, s2 = cs
    a = pltpu.roll(x, LANES - half, 1) * s1 + pltpu.roll(x, half, 1) * s2
    return x * c - a if inverse else x * c + a


def _call(body, *, name, grid, in_specs, out_specs, out_shape, scratch_shapes, args, exchange=None):
    sem = ("arbitrary",) * len(grid)
    if exchange is None:
        return pl.pallas_call(body, name=name, grid=grid, in_specs=in_specs, out_specs=out_specs,
                              out_shape=out_shape, scratch_shapes=scratch_shapes, compiler_params=_params(sem))(*args)
    n_in, n_out, e = len(in_specs), len(out_specs), exchange.n
    total = math.prod(grid)

    def carried(*refs):
        own = refs[:n_in] + refs[n_in + e:n_in + e + n_out] + refs[n_in + 2 * e + n_out:len(refs) - 3]
        ex_refs = (refs[n_in:n_in + e], refs[n_in + e + n_out:n_in + 2 * e + n_out], refs[len(refs) - 3:])
        step = pl.program_id(0)
        for d in range(1, len(grid)):
            step = step * grid[d] + pl.program_id(d)

        @pl.when(step == 0)
        def _():
            exchange.phase(0, *ex_refs)

        @pl.when(step == (3 * total) // 4)
        def _():
            exchange.phase(1, *ex_refs)

        body(*own)

        @pl.when(step == total - 1)
        def _():
            exchange.phase(2, *ex_refs)

    anyspec = pl.BlockSpec(memory_space=pl.ANY)
    return pl.pallas_call(
        carried, name=name, grid=grid, in_specs=list(in_specs) + [anyspec] * e,
        out_specs=list(out_specs) + [anyspec] * e, out_shape=list(out_shape) + exchange.out_shape,
        scratch_shapes=list(scratch_shapes) + exchange.scratch, compiler_params=_params(sem),
    )(*args, *exchange.operands)


def _ffn_fwd(h, pre_w, w1, w2, post_w, target, *, name, exchange=None):
    T, D = h.shape
    nk, ck = w2.shape[0], w2.shape[1]
    tT = min(TOKEN_TILE, T)
    nT = T // tT
    with_loss = target is not None

    def body(*refs):
        if with_loss:
            (h_ref, pre_ref, w1g_ref, w1u_ref, w2_ref, post_ref, tgt_ref,
             u_ref, f_ref, ho_ref, dy_ref, loss_ref, a_s, acc) = refs
        else:
            (h_ref, pre_ref, w1g_ref, w1u_ref, w2_ref, post_ref,
             u_ref, f_ref, ho_ref, a_s, acc) = refs
        k = pl.program_id(1)

        @pl.when(k == 0)
        def _():
            a_s[...] = _rms_fwd(h_ref[...], pre_ref[...]).astype(BF16)
            acc[...] = jnp.zeros_like(acc)

        for c in range(FFN_CHAINS):
            rs = slice(c * (tT // FFN_CHAINS), (c + 1) * (tT // FFN_CHAINS))
            a = a_s[rs, :]
            ug = _dot_nt(a, w1g_ref[...])
            uu = _dot_nt(a, w1u_ref[...])
            u_ref[0, rs, :] = ug.astype(BF16)
            u_ref[1, rs, :] = uu.astype(BF16)
            acc[rs, :] += _dot((ug * _sigmoid(ug) * uu).astype(BF16), w2_ref[...])

        @pl.when(k == nk - 1)
        def _():
            f = acc[...]
            f_ref[...] = f
            ho = h_ref[...] + 0.5 * _rms_fwd(f, post_ref[...])
            ho_ref[...] = ho
            if with_loss:
                e = ho - tgt_ref[...]
                dy_ref[...] = e * (1.0 / D)
                loss_ref[...] = jnp.full(loss_ref.shape, (0.5 / D) * jnp.sum(e * e), F32)

    row = pl.BlockSpec((tT, D), lambda i, k: (i, 0))
    vec = pl.BlockSpec((1, D), lambda i, k: (0, 0))
    in_specs = [row, vec,
                pl.BlockSpec((None, ck, D), lambda i, k: (k, 0, 0)),
                pl.BlockSpec((None, ck, D), lambda i, k: (nk + k, 0, 0)),
                pl.BlockSpec((None, ck, D), lambda i, k: (k, 0, 0)),
                vec]
    out_shape = [jax.ShapeDtypeStruct((2, nk, T, ck), BF16),
                 jax.ShapeDtypeStruct((T, D), F32),
                 jax.ShapeDtypeStruct((T, D), F32)]
    out_specs = [pl.BlockSpec((2, None, tT, ck), lambda i, k: (0, k, i, 0)), row, row]
    args = [h, pre_w, w1, w1, w2, post_w]
    if with_loss:
        in_specs.append(row)
        args.append(target)
        out_shape += [jax.ShapeDtypeStruct((T, D), F32), jax.ShapeDtypeStruct((nT * 8, LANES), F32)]
        out_specs += [row, pl.BlockSpec((8, LANES), lambda i, k: (i, 0))]
    return _call(body, name=name, grid=(nT, nk), in_specs=in_specs, out_specs=out_specs, out_shape=out_shape,
                 scratch_shapes=[pltpu.VMEM((tT, D), BF16), pltpu.VMEM((tT, D), F32)], args=args, exchange=exchange)


def _ffn_bwd(dho, f, post_w, h, pre_w, u, w2, w1, *, name, exchange=None):
    T, D = h.shape
    nk, ck = w2.shape[0], w2.shape[1]
    tT = min(TOKEN_TILE, T)
    nT = T // tT

    def body(dho_ref, f_ref, post_ref, h_ref, pre_ref, u_ref, w2_ref, w1g_ref, w1u_ref,
             g_ref, du_ref, df_ref, a_ref, dh_ref, gpost_ref, gpre_ref, df_s, da_acc):
        i, k = pl.program_id(0), pl.program_id(1)

        @pl.when(jnp.logical_and(i == 0, k == 0))
        def _():
            gpost_ref[...] = jnp.zeros_like(gpost_ref)
            gpre_ref[...] = jnp.zeros_like(gpre_ref)

        @pl.when(k == 0)
        def _():
            dx, dw = _rms_bwd(f_ref[...], post_ref[...], 0.5 * dho_ref[...])
            dfb = dx.astype(BF16)
            df_s[...] = dfb
            df_ref[...] = dfb
            gpost_ref[...] += dw
            a_ref[...] = _rms_fwd(h_ref[...], pre_ref[...]).astype(BF16)
            da_acc[...] = jnp.zeros_like(da_acc)

        groups = [slice(c * (tT // FFN_CHAINS), (c + 1) * (tT // FFN_CHAINS)) for c in range(FFN_CHAINS)]
        dgs = [_dot_nt(df_s[rs, :], w2_ref[...]) for rs in groups]
        for rs, dg in zip(groups, dgs):
            ug = u_ref[0, rs, :].astype(F32)
            uu = u_ref[1, rs, :].astype(F32)
            sg = _sigmoid(ug)
            sl = ug * sg
            g_ref[rs, :] = (sl * uu).astype(BF16)
            dug = (dg * uu * (sg + sl * (1.0 - sg))).astype(BF16)
            duu = (dg * sl).astype(BF16)
            du_ref[0, rs, :] = dug
            du_ref[1, rs, :] = duu
            da_acc[rs, :] += _dot(dug, w1g_ref[...]) + _dot(duu, w1u_ref[...])

        @pl.when(k == nk - 1)
        def _():
            dx, dw = _rms_bwd(h_ref[...], pre_ref[...], da_acc[...])
            dh_ref[...] = dho_ref[...] + dx
            gpre_ref[...] += dw

    row = pl.BlockSpec((tT, D), lambda i, k: (i, 0))
    vec = pl.BlockSpec((1, D), lambda i, k: (0, 0))
    return _call(
        body, name=name, grid=(nT, nk),
        in_specs=[row, row, vec, row, vec,
                  pl.BlockSpec((2, None, tT, ck), lambda i, k: (0, k, i, 0)),
                  pl.BlockSpec((None, ck, D), lambda i, k: (k, 0, 0)),
                  pl.BlockSpec((None, ck, D), lambda i, k: (k, 0, 0)),
                  pl.BlockSpec((None, ck, D), lambda i, k: (nk + k, 0, 0))],
        out_specs=[pl.BlockSpec((None, tT, ck), lambda i, k: (k, i, 0)),
                   pl.BlockSpec((2, None, tT, ck), lambda i, k: (0, k, i, 0)),
                   row, row, row, vec, vec],
        out_shape=[jax.ShapeDtypeStruct((nk, T, ck), BF16),
                   jax.ShapeDtypeStruct((2, nk, T, ck), BF16),
                   jax.ShapeDtypeStruct((T, D), BF16),
                   jax.ShapeDtypeStruct((T, D), BF16),
                   jax.ShapeDtypeStruct((T, D), F32),
                   jax.ShapeDtypeStruct((1, D), F32),
                   jax.ShapeDtypeStruct((1, D), F32)],
        scratch_shapes=[pltpu.VMEM((tT, D), BF16), pltpu.VMEM((tT, D), F32)],
        args=(dho, f, post_w, h, pre_w, u, w2, w1, w1), exchange=exchange)


def _matmul_tn(x, dy, *, name, exchange=None):
    Px, T, K = x.shape
    Py, _, N = dy.shape
    P = max(Px, Py)
    tT, tK, tN = min(GRAD_TOKEN_TILE, T), _tile(K, GRAD_TILE_CAP), _tile(N, GRAD_TILE_CAP)
    nt = T // tT

    def body(x_ref, dy_ref, o_ref, acc):
        t = pl.program_id(3)

        @pl.when(t == 0)
        def _():
            acc[...] = jnp.zeros_like(acc)

        acc[...] += _dot_tn(x_ref[...], dy_ref[...])

        @pl.when(t == nt - 1)
        def _():
            o_ref[...] = acc[...].astype(BF16)

    return _call(
        body, name=name, grid=(P, K // tK, N // tN, nt),
        in_specs=[pl.BlockSpec((None, tT, tK), lambda p, a, b, t: (p if Px > 1 else 0, t, a)),
                  pl.BlockSpec((None, tT, tN), lambda p, a, b, t: (p if Py > 1 else 0, t, b))],
        out_specs=[pl.BlockSpec((None, tK, tN), lambda p, a, b, t: (p, a, b))],
        out_shape=[jax.ShapeDtypeStruct((P, K, N), BF16)],
        scratch_shapes=[pltpu.VMEM((tK, tN), F32)], args=(x, dy), exchange=exchange)


def _rms_matmul(h, wn, w, *, name):
    T, D = h.shape
    N = w.shape[0]
    tT, tN = min(TOKEN_TILE, T), _tile(N, PROJ_TILE_CAP)

    def body(h_ref, wn_ref, w_ref, y_ref, a_ref):
        @pl.when(pl.program_id(1) == 0)
        def _():
            a_ref[...] = _rms_fwd(h_ref[...], wn_ref[...]).astype(BF16)

        y_ref[...] = _dot_nt(a_ref[...], w_ref[...]).astype(BF16)

    return pl.pallas_call(
        body, name=name, grid=(T // tT, N // tN),
        in_specs=[pl.BlockSpec((tT, D), lambda i, j: (i, 0)),
                  pl.BlockSpec((1, D), lambda i, j: (0, 0)),
                  pl.BlockSpec((tN, D), lambda i, j: (j, 0))],
        out_specs=[pl.BlockSpec((tT, tN), lambda i, j: (i, j)),
                   pl.BlockSpec((tT, D), lambda i, j: (i, 0))],
        out_shape=[jax.ShapeDtypeStruct((T, N), BF16), jax.ShapeDtypeStruct((T, D), BF16)],
        compiler_params=_params(("parallel", "arbitrary")),
    )(h, wn, w)


def _proj_bwd(dproj, w, h, wn, dres, *, name, exchange=None):
    T, D = h.shape
    N = w.shape[0]
    tT, tN = min(TOKEN_TILE, T), _tile(N, PROJ_TILE_CAP)
    nn = N // tN

    def body(dp_ref, w_ref, h_ref, wn_ref, dres_ref, dh_ref, gw_ref, acc):
        i, j = pl.program_id(0), pl.program_id(1)

        @pl.when(jnp.logical_and(i == 0, j == 0))
        def _():
            gw_ref[...] = jnp.zeros_like(gw_ref)

        @pl.when(j == 0)
        def _():
            acc[...] = jnp.zeros_like(acc)

        acc[...] += _dot(dp_ref[...], w_ref[...])

        @pl.when(j == nn - 1)
        def _():
            dx, dw = _rms_bwd(h_ref[...], wn_ref[...], acc[...])
            dh_ref[...] = dres_ref[...] + dx
            gw_ref[...] += dw

    row = pl.BlockSpec((tT, D), lambda i, j: (i, 0))
    vec = pl.BlockSpec((1, D), lambda i, j: (0, 0))
    return _call(
        body, name=name, grid=(T // tT, nn),
        in_specs=[pl.BlockSpec((tT, tN), lambda i, j: (i, j)),
                  pl.BlockSpec((tN, D), lambda i, j: (j, 0)), row, vec, row],
        out_specs=[row, vec],
        out_shape=[jax.ShapeDtypeStruct((T, D), F32), jax.ShapeDtypeStruct((1, D), F32)],
        scratch_shapes=[pltpu.VMEM((tT, D), F32)], args=(dproj, w, h, wn, dres), exchange=exchange)


def _mla_prep_fwd(proj, pos, qn_w, kvn_w, w_uq, w_kv, tab, *, name):
    T = proj.shape[0]
    tT = min(TOKEN_TILE, T)
    a_blk = PROJ_FIXED // AW - 1

    def body(a_ref, pos_ref, qnw_ref, kvnw_ref, wuq_ref, wkv_ref, tab_ref,
             q_ref, k_ref, v_ref, qn_ref, kvn_ref):
        cq = a_ref[:, 0:MLA_Q_RANK].astype(F32)
        ckv = a_ref[:, MLA_Q_RANK:MLA_Q_RANK + MLA_KV_RANK].astype(F32)
        kr = a_ref[:, 640:768].astype(F32)
        qn = _rms_fwd(cq, qnw_ref[...]).astype(BF16)
        kvn = _rms_fwd(ckv, kvnw_ref[...]).astype(BF16)
        qn_ref[...] = qn
        kvn_ref[...] = kvn
        cs = _rope_cs(pos_ref[...], tab_ref)
        q = _dot_nt(qn, wuq_ref[...])
        kv = _dot(kvn, wkv_ref[...])
        krr = _rope(kr, cs, MLA_ROPE // 2)
        for hd in range(MLA_HEADS):
            sl = slice(hd * HP, (hd + 1) * HP)
            q_ref[:, sl] = (_rope(q[:, sl], cs, MLA_ROPE // 2) * ATTN_SCALE).astype(BF16)
            k_ref[:, sl] = (kv[:, sl] + krr).astype(BF16)
        v_ref[...] = kv[:, QW:].astype(BF16)

    def full(r, c):
        return pl.BlockSpec((r, c), lambda i: (0, 0))

    def rows(c):
        return pl.BlockSpec((tT, c), lambda i: (i, 0))

    return pl.pallas_call(
        body, name=name, grid=(T // tT,),
        in_specs=[pl.BlockSpec((tT, AW), lambda i: (i, a_blk)), rows(1),
                  full(1, MLA_Q_RANK), full(1, MLA_KV_RANK),
                  full(QW, MLA_Q_RANK), full(MLA_KV_RANK, 2 * QW), full(8, LANES)],
        out_specs=[rows(QW), rows(QW), rows(QW), rows(MLA_Q_RANK), rows(MLA_KV_RANK)],
        out_shape=[jax.ShapeDtypeStruct((T, QW), BF16)] * 3
        + [jax.ShapeDtypeStruct((T, MLA_Q_RANK), BF16), jax.ShapeDtypeStruct((T, MLA_KV_RANK), BF16)],
        compiler_params=_params(("parallel",)),
    )(proj, pos, qn_w, kvn_w, w_uq, w_kv, tab)


def _mla_prep_bwd(dq, dk, dv, proj, pos, qn_w, kvn_w, w_uq, w_kv, tab, *, name):
    T = proj.shape[0]
    tT = min(TOKEN_TILE, T)
    a_blk = PROJ_FIXED // AW - 1

    def body(dq_ref, dk_ref, dv_ref, a_ref, pos_ref, qnw_ref, kvnw_ref, wuq_ref, wkv_ref, tab_ref,
             da_ref, dql_ref, dkvl_ref, gqn_ref, gkvn_ref):
        @pl.when(pl.program_id(0) == 0)
        def _():
            gqn_ref[...] = jnp.zeros_like(gqn_ref)
            gkvn_ref[...] = jnp.zeros_like(gkvn_ref)

        cs = _rope_cs(pos_ref[...], tab_ref)
        dkr = jnp.zeros((tT, HP), F32)
        for hd in range(MLA_HEADS):
            sl = slice(hd * HP, (hd + 1) * HP)
            dql_ref[:, sl] = (_rope(dq_ref[:, sl], cs, MLA_ROPE // 2, inverse=True) * ATTN_SCALE).astype(BF16)
            dkh = dk_ref[:, sl]
            dkr = dkr + dkh
            dkvl_ref[:, sl] = dkh.astype(BF16)
        dkvl_ref[:, QW:] = dv_ref[...]
        dqn = _dot(dql_ref[...], wuq_ref[...])
        dkvn = _dot_nt(dkvl_ref[...], wkv_ref[...])
        cq = a_ref[:, 0:MLA_Q_RANK].astype(F32)
        ckv = a_ref[:, MLA_Q_RANK:MLA_Q_RANK + MLA_KV_RANK].astype(F32)
        dcq, gq = _rms_bwd(cq, qnw_ref[...], dqn)
        dckv, gkv = _rms_bwd(ckv, kvnw_ref[...], dkvn)
        gqn_ref[...] += gq
        gkvn_ref[...] += gkv
        da_ref[:, 0:MLA_Q_RANK] = dcq.astype(BF16)
        da_ref[:, MLA_Q_RANK:MLA_Q_RANK + MLA_KV_RANK] = dckv.astype(BF16)
        da_ref[:, 640:768] = _rope(dkr, cs, MLA_ROPE // 2, inverse=True).astype(BF16)
        da_ref[:, 768:AW] = jnp.zeros((tT, AW - 768), BF16)

    def full(r, c):
        return pl.BlockSpec((r, c), lambda i: (0, 0))

    def rows(c):
        return pl.BlockSpec((tT, c), lambda i: (i, 0))

    return pl.pallas_call(
        body, name=name, grid=(T // tT,),
        in_specs=[rows(QW), rows(QW), rows(QW), pl.BlockSpec((tT, AW), lambda i: (i, a_blk)), rows(1),
                  full(1, MLA_Q_RANK), full(1, MLA_KV_RANK),
                  full(QW, MLA_Q_RANK), full(MLA_KV_RANK, 2 * QW), full(8, LANES)],
        out_specs=[rows(AW), rows(QW), rows(2 * QW), full(1, MLA_Q_RANK), full(1, MLA_KV_RANK)],
        out_shape=[jax.ShapeDtypeStruct((T, AW), BF16), jax.ShapeDtypeStruct((T, QW), BF16),
                   jax.ShapeDtypeStruct((T, 2 * QW), BF16),
                   jax.ShapeDtypeStruct((1, MLA_Q_RANK), F32), jax.ShapeDtypeStruct((1, MLA_KV_RANK), F32)],
        compiler_params=_params(("arbitrary",)),
    )(dq, dk, dv, proj, pos, qn_w, kvn_w, w_uq, w_kv, tab)


def _flash_fwd(q, k, v, *, name, exchange=None):
    T = q.shape[0]
    H = q.shape[1] // HP
    tq = min(ATTN_TILE, T)
    nq = T // tq

    sub = tq // ATTN_CHAINS

    def body(q_ref, k_ref, v_ref, o_ref, lse_ref):
        qi = pl.program_id(1)
        qs = [q_ref[c * sub:(c + 1) * sub, :] for c in range(ATTN_CHAINS)]

        def update(carry, off, masked):
            nks = [(c + 1) * sub if masked else tq for c in range(ATTN_CHAINS)]
            scores = [_dot_nt(qs[c], k_ref[pl.ds(off, nks[c]), :]) for c in range(ATTN_CHAINS)]
            out = []
            for c in range(ATTN_CHAINS):
                m_prev, l_prev, acc = carry[c]
                nk, s = nks[c], scores[c]
                vb = v_ref[pl.ds(off, nk), :]
                if masked:
                    rows = lax.broadcasted_iota(jnp.int32, (sub, nk), 0) + c * sub
                    s = jnp.where(rows >= lax.broadcasted_iota(jnp.int32, (sub, nk), 1), s, NEG)
                m_new = jnp.maximum(m_prev, jnp.max(s, axis=1, keepdims=True))
                alpha = jnp.exp(m_prev - m_new)
                p = jnp.exp(s - m_new)
                out.append((m_new, alpha * l_prev + jnp.sum(p, axis=1, keepdims=True),
                            alpha * acc + _dot(p.astype(BF16), vb)))
            return tuple(out)

        init = tuple((jnp.full((sub, 1), NEG, F32), jnp.zeros((sub, 1), F32), jnp.zeros((sub, HP), F32))
                     for _ in range(ATTN_CHAINS))
        carry = lax.fori_loop(0, qi, lambda j, cr: update(cr, pl.multiple_of(j * tq, tq), False), init)
        carry = update(carry, pl.multiple_of(qi * tq, tq), True)
        for c in range(ATTN_CHAINS):
            m_fin, l_fin, acc = carry[c]
            o_ref[c * sub:(c + 1) * sub, :] = (acc / l_fin).astype(BF16)
            lse_ref[c * sub:(c + 1) * sub, :] = jnp.broadcast_to(m_fin + jnp.log(l_fin), (sub, HP))

    qspec = pl.BlockSpec((tq, HP), lambda h, i: (i, h))
    kspec = pl.BlockSpec((T, HP), lambda h, i: (0, h))
    return _call(
        body, name=name, grid=(H, nq),
        in_specs=[qspec, kspec, kspec], out_specs=[qspec, qspec],
        out_shape=[jax.ShapeDtypeStruct((T, H * HP), BF16), jax.ShapeDtypeStruct((T, H * HP), F32)],
        scratch_shapes=[], args=(q, k, v), exchange=exchange)


def _flash_bwd(q, k, v, do, lse, delta, *, name, exchange=None):
    T = q.shape[0]
    H = q.shape[1] // HP
    tq = min(ATTN_TILE, T)
    nq = T // tq
    sub = tq // ATTN_CHAINS

    def body(k_ref, v_ref, q_ref, do_ref, lse_ref, dl_ref, dq_ref, dk_ref, dv_ref):
        ki = pl.program_id(1)

        @pl.when(ki == 0)
        def _():
            dq_ref[...] = jnp.zeros_like(dq_ref)

        def grow(a):
            return a if a.shape[0] == tq else jnp.concatenate([a, jnp.zeros((tq - a.shape[0], HP), F32)], axis=0)

        def step(carry, j, masked):
            dk_acc, dv_acc = carry
            nks = [(c + 1) * sub if masked else tq for c in range(ATTN_CHAINS)]
            rws = [pl.ds(pl.multiple_of(j * tq + c * sub, sub), sub) for c in range(ATTN_CHAINS)]
            scores = [_dot_nt(q_ref[rws[c], :], k_ref[0:nks[c], :]) for c in range(ATTN_CHAINS)]
            dps = [_dot_nt(do_ref[rws[c], :], v_ref[0:nks[c], :]) for c in range(ATTN_CHAINS)]
            for c in range(ATTN_CHAINS):
                rows, nk, s, dp = rws[c], nks[c], scores[c], dps[c]
                kb = k_ref[0:nk, :]
                qb = q_ref[rows, :]
                dob = do_ref[rows, :]
                if masked:
                    ri = lax.broadcasted_iota(jnp.int32, (sub, nk), 0) + c * sub
                    s = jnp.where(ri >= lax.broadcasted_iota(jnp.int32, (sub, nk), 1), s, NEG)
                p = jnp.exp(s - lse_ref[rows, 0:1])
                dv_acc = dv_acc + grow(_dot_tn(p.astype(BF16), dob))
                ds = (p * (dp - dl_ref[rows, 0:1])).astype(BF16)
                dk_acc = dk_acc + grow(_dot_tn(ds, qb))
                dq_ref[rows, :] += _dot(ds, kb)
            return dk_acc, dv_acc

        carry = step((jnp.zeros((tq, HP), F32), jnp.zeros((tq, HP), F32)), ki, True)
        dk_acc, dv_acc = lax.fori_loop(ki + 1, nq, lambda j, cr: step(cr, j, False), carry)
        dk_ref[...] = dk_acc
        dv_ref[...] = dv_acc.astype(BF16)

    kspec = pl.BlockSpec((tq, HP), lambda h, j: (j, h))
    full = pl.BlockSpec((T, HP), lambda h, j: (0, h))
    return _call(
        body, name=name, grid=(H, nq),
        in_specs=[kspec, kspec, full, full, full, full], out_specs=[full, kspec, kspec],
        out_shape=[jax.ShapeDtypeStruct((T, H * HP), F32), jax.ShapeDtypeStruct((T, H * HP), F32),
                   jax.ShapeDtypeStruct((T, H * HP), BF16)],
        scratch_shapes=[], args=(k, v, q, do, lse, delta), exchange=exchange)


def _ret_consts(cc, hd):
    lg = math.log(1.0 - 2.0 ** (-5.0 - hd))
    diff = (lax.broadcasted_iota(jnp.int32, (cc, cc), 0) - lax.broadcasted_iota(jnp.int32, (cc, cc), 1)).astype(F32)
    decay = jnp.where(diff >= 0, jnp.exp(jnp.maximum(diff, 0.0) * lg), 0.0)
    idx = lax.broadcasted_iota(jnp.int32, (cc, 1), 0).astype(F32)
    zeta = jnp.exp((cc - 1.0 - idx) * lg)
    xi = jnp.exp((idx + 1.0) * lg)
    return decay, zeta, xi, math.exp(cc * lg)


def _ret_fwd(proj, pos, tab, *, name):
    T = proj.shape[0]
    cc = min(RET_TILE, T)
    n = T // cc

    def body(rq_ref, rk_ref, rv_ref, pos_ref, tab_ref, y_ref, yn_ref, rprev_ref, r_s):
        @pl.when(pl.program_id(0) == 0)
        def _():
            r_s[...] = jnp.zeros_like(r_s)

        cs = _rope_cs(pos_ref[...], tab_ref)
        for hd in range(RET_HEADS):
            sl = slice(hd * HP, (hd + 1) * HP)
            decay, zeta, xi, gc = _ret_consts(cc, hd)
            q = _rope(rq_ref[:, sl].astype(F32), cs, RET_DK // 2).astype(BF16)
            kf = _rope(rk_ref[:, sl].astype(F32), cs, RET_DK // 2) * (RET_DK ** -0.5)
            k = kf.astype(BF16)
            v = rv_ref[:, sl]
            r = r_s[hd]
            rprev_ref[0, hd] = r
            inner = (_dot_nt(q, k) * decay).astype(BF16)
            y = _dot(inner, v) + _dot(q, r.astype(BF16)) * xi
            r_s[hd] = r * gc + _dot_tn((kf * zeta).astype(BF16), v)
            y_ref[:, sl] = y
            mu = jnp.mean(y, axis=-1, keepdims=True)
            yc = y - mu
            var = jnp.mean(yc * yc, axis=-1, keepdims=True)
            yn_ref[:, sl] = (yc * lax.rsqrt(var + GN_EPS)).astype(BF16)

    def blk(j):
        return pl.BlockSpec((cc, RW), lambda i: (i, j))

    return pl.pallas_call(
        body, name=name, grid=(n,),
        in_specs=[blk(0), blk(1), blk(2), pl.BlockSpec((cc, 1), lambda i: (i, 0)),
                  pl.BlockSpec((8, LANES), lambda i: (0, 0))],
        out_specs=[blk(0), blk(0), pl.BlockSpec((1, RET_HEADS, HP, RET_DV), lambda i: (i, 0, 0, 0))],
        out_shape=[jax.ShapeDtypeStruct((T, RW), F32), jax.ShapeDtypeStruct((T, RW), BF16),
                   jax.ShapeDtypeStruct((n, RET_HEADS, HP, RET_DV), F32)],
        scratch_shapes=[pltpu.VMEM((RET_HEADS, HP, RET_DV), F32)],
        compiler_params=_params(("arbitrary",)),
    )(proj, proj, proj, pos, tab)


def _ret_bwd(dyn, y, proj, pos, tab, rprev, *, name):
    T = proj.shape[0]
    cc = min(RET_TILE, T)
    n = T // cc

    def body(dyn_ref, y_ref, rq_ref, rk_ref, rv_ref, pos_ref, tab_ref, rprev_ref,
             drq_ref, drk_ref, drv_ref, dr_s):
        @pl.when(pl.program_id(0) == 0)
        def _():
            dr_s[...] = jnp.zeros_like(dr_s)

        cs = _rope_cs(pos_ref[...], tab_ref)
        for hd in range(RET_HEADS):
            sl = slice(hd * HP, (hd + 1) * HP)
            decay, zeta, xi, gc = _ret_consts(cc, hd)
            q = _rope(rq_ref[:, sl].astype(F32), cs, RET_DK // 2).astype(BF16)
            kf = _rope(rk_ref[:, sl].astype(F32), cs, RET_DK // 2) * (RET_DK ** -0.5)
            k = kf.astype(BF16)
            v = rv_ref[:, sl]
            yv = y_ref[:, sl]
            mu = jnp.mean(yv, axis=-1, keepdims=True)
            yc = yv - mu
            rs = lax.rsqrt(jnp.mean(yc * yc, axis=-1, keepdims=True) + GN_EPS)
            yn = yc * rs
            dn = dyn_ref[:, sl]
            dy = rs * (dn - jnp.mean(dn, axis=-1, keepdims=True) - yn * jnp.mean(dn * yn, axis=-1, keepdims=True))
            dyb = dy.astype(BF16)
            dyx = (dy * xi).astype(BF16)
            dr = dr_s[hd]
            drb = dr.astype(BF16)
            inner = (_dot_nt(q, k) * decay).astype(BF16)
            da = (_dot_nt(dyb, v) * decay).astype(BF16)
            dv = _dot_tn(inner, dyb) + _dot((kf * zeta).astype(BF16), drb)
            dq = _dot(da, k) + _dot_nt(dyx, rprev_ref[0, hd].astype(BF16))
            dk = _dot_tn(da, q) + _dot_nt(v, drb) * zeta
            dr_s[hd] = dr * gc + _dot_tn(q, dyx)
            drq_ref[:, sl] = _rope(dq, cs, RET_DK // 2, inverse=True).astype(BF16)
            drk_ref[:, sl] = _rope(dk * (RET_DK ** -0.5), cs, RET_DK // 2, inverse=True).astype(BF16)
            drv_ref[:, sl] = dv.astype(BF16)

    def blk(j):
        return pl.BlockSpec((cc, RW), lambda i: (n - 1 - i, j))

    return pl.pallas_call(
        body, name=name, grid=(n,),
        in_specs=[blk(0), blk(0), blk(0), blk(1), blk(2), pl.BlockSpec((cc, 1), lambda i: (n - 1 - i, 0)),
                  pl.BlockSpec((8, LANES), lambda i: (0, 0)),
                  pl.BlockSpec((1, RET_HEADS, HP, RET_DV), lambda i: (n - 1 - i, 0, 0, 0))],
        out_specs=[blk(0), blk(0), blk(0)],
        out_shape=[jax.ShapeDtypeStruct((T, RW), BF16)] * 3,
        scratch_shapes=[pltpu.VMEM((RET_HEADS, HP, RET_DV), F32)],
        compiler_params=_params(("arbitrary",)),
    )(dyn, y, proj, proj, proj, pos, tab, rprev)


def _merge_fwd(o, yn, proj, gn_w, w_bm, w_br, w_out, h, post_w, *, name):
    T, D = h.shape
    tT = min(MERGE_TILE, T)
    g_blk = PROJ_FIXED // D

    def body(o_ref, yn_ref, rg_ref, gm_ref, gr_ref, gnw_ref, wbm_ref, wbr_ref, wout_ref, h_ref, post_ref,
             omla_ref, oret_ref, m_ref, ho_ref):
        o_mla = _dot(o_ref[...], wbm_ref[...])
        rg = rg_ref[...].astype(F32)
        gated = (rg * _sigmoid(rg) * (yn_ref[...].astype(F32) * gnw_ref[...])).astype(BF16)
        o_ret = _dot(gated, wbr_ref[...])
        omla_ref[...] = o_mla.astype(BF16)
        oret_ref[...] = o_ret.astype(BF16)
        merged = _sigmoid(gm_ref[...].astype(F32)) * o_mla + _sigmoid(gr_ref[...].astype(F32)) * o_ret
        m = _dot(merged.astype(BF16), wout_ref[...])
        m_ref[...] = m
        ho_ref[...] = h_ref[...] + _rms_fwd(m, post_ref[...])

    def full(r, c):
        return pl.BlockSpec((r, c), lambda i: (0, 0))

    def rows(c, j=0):
        return pl.BlockSpec((tT, c), lambda i: (i, j))

    return pl.pallas_call(
        body, name=name, grid=(T // tT,),
        in_specs=[rows(QW), rows(RW), rows(RW, 3), rows(D, g_blk), rows(D, g_blk + 1), full(1, RW),
                  full(QW, D), full(RW, D), full(D, D), rows(D), full(1, D)],
        out_specs=[rows(D), rows(D), rows(D), rows(D)],
        out_shape=[jax.ShapeDtypeStruct((T, D), BF16), jax.ShapeDtypeStruct((T, D), BF16),
                   jax.ShapeDtypeStruct((T, D), F32), jax.ShapeDtypeStruct((T, D), F32)],
        compiler_params=_params(("parallel",)),
    )(o, yn, proj, proj, proj, gn_w, w_bm, w_br, w_out, h, post_w)


def _merge_bwd(dho, m, post_w, omla, oret, proj, yn, gn_w, o, w_out, w_bm, w_br, *, name):
    T, D = dho.shape
    tT = min(MERGE_TILE, T)
    g_blk = PROJ_FIXED // D

    def body(dho_ref, m_ref, post_ref, omla_ref, oret_ref, rg_ref, gm_ref, gr_ref, yn_ref, gnw_ref, o_ref,
             wout_ref, wbm_ref, wbr_ref,
             dm_ref, merged_ref, dgm_ref, dgr_ref, domla_ref, do_ref, delta_ref, doret_ref, gated_ref,
             drg_ref, dyn_ref, gpost_ref, ggn_ref):
        @pl.when(pl.program_id(0) == 0)
        def _():
            gpost_ref[...] = jnp.zeros_like(gpost_ref)
            ggn_ref[...] = jnp.zeros_like(ggn_ref)

        dm, gp = _rms_bwd(m_ref[...], post_ref[...], dho_ref[...])
        gpost_ref[...] += gp
        dmb = dm.astype(BF16)
        dm_ref[...] = dmb
        dmerged = _dot_nt(dmb, wout_ref[...])
        o_mla = omla_ref[...].astype(F32)
        o_ret = oret_ref[...].astype(F32)
        sgm = _sigmoid(gm_ref[...].astype(F32))
        sgr = _sigmoid(gr_ref[...].astype(F32))
        merged_ref[...] = (sgm * o_mla + sgr * o_ret).astype(BF16)
        dgm_ref[...] = (dmerged * o_mla * sgm * (1.0 - sgm)).astype(BF16)
        dgr_ref[...] = (dmerged * o_ret * sgr * (1.0 - sgr)).astype(BF16)
        domla = (dmerged * sgm).astype(BF16)
        domla_ref[...] = domla
        do = _dot_nt(domla, wbm_ref[...])
        do_ref[...] = do.astype(BF16)
        for hd in range(MLA_HEADS):
            sl = slice(hd * HP, (hd + 1) * HP)
            d = jnp.sum(do[:, sl] * o_ref[:, sl].astype(F32), axis=-1, keepdims=True)
            delta_ref[:, sl] = jnp.broadcast_to(d, (tT, HP))
        doret = (dmerged * sgr).astype(BF16)
        doret_ref[...] = doret
        dgated = _dot_nt(doret, wbr_ref[...])
        rg = rg_ref[...].astype(F32)
        sg = _sigmoid(rg)
        srg = rg * sg
        ynv = yn_ref[...].astype(F32)
        yw = ynv * gnw_ref[...]
        gated_ref[...] = (srg * yw).astype(BF16)
        drg_ref[...] = (dgated * yw * (sg * (1.0 + rg * (1.0 - sg)))).astype(BF16)
        dgs = dgated * srg
        dyn_ref[...] = dgs * gnw_ref[...]
        ggn_ref[...] += jnp.sum(dgs * ynv, axis=0, keepdims=True)

    def full(r, c):
        return pl.BlockSpec((r, c), lambda i: (0, 0))

    def rows(c, j=0):
        return pl.BlockSpec((tT, c), lambda i: (i, j))

    return pl.pallas_call(
        body, name=name, grid=(T // tT,),
        in_specs=[rows(D), rows(D), full(1, D), rows(D), rows(D), rows(RW, 3), rows(D, g_blk), rows(D, g_blk + 1),
                  rows(RW), full(1, RW), rows(QW), full(D, D), full(QW, D), full(RW, D)],
        out_specs=[rows(D), rows(D), rows(D), rows(D), rows(D), rows(QW), rows(QW), rows(D), rows(RW),
                   rows(RW), rows(RW), full(1, D), full(1, RW)],
        out_shape=[jax.ShapeDtypeStruct((T, D), BF16)] * 5
        + [jax.ShapeDtypeStruct((T, QW), BF16), jax.ShapeDtypeStruct((T, QW), F32),
           jax.ShapeDtypeStruct((T, D), BF16), jax.ShapeDtypeStruct((T, RW), BF16),
           jax.ShapeDtypeStruct((T, RW), BF16), jax.ShapeDtypeStruct((T, RW), F32),
           jax.ShapeDtypeStruct((1, D), F32), jax.ShapeDtypeStruct((1, RW), F32)],
        compiler_params=_params(("arbitrary",)),
    )(dho, m, post_w, omla, oret, proj, proj, proj, yn, gn_w, o, w_out, w_bm, w_br)


def _mesh_pos():
    return lax.axis_index("x"), lax.axis_index("y"), lax.axis_index("c")


class _Gather:
    def __init__(self, shards):
        self.operands = list(shards)
        self.n = len(shards)
        self.out_shape = [jax.ShapeDtypeStruct((N_DEV,) + s.shape, s.dtype) for s in shards]
        self.scratch = [pltpu.SemaphoreType.DMA((7 * self.n,)), pltpu.SemaphoreType.DMA((7 * self.n,)),
                        pltpu.SemaphoreType.DMA((self.n,))]

    def phase(self, p, x_refs, out_refs, sems):
        send_sems, recv_sems, local_sems = sems
        x, y, c = _mesh_pos()
        me, sibling = (x, y, c), (x, y, 1 - c)
        chips = [(1 - x, y), (x, 1 - y), (1 - x, 1 - y)]

        def copy(w, k, block, to, src=None):
            slot = out_refs[w].at[4 * block[0] + 2 * block[1] + block[2]]
            return pltpu.make_async_remote_copy(
                src_ref=slot if src is None else src, dst_ref=slot,
                send_sem=send_sems.at[7 * w + k], recv_sem=recv_sems.at[7 * w + k],
                device_id=to, device_id_type=pl.DeviceIdType.MESH)

        for w in range(self.n):
            mine = pltpu.make_async_copy(x_refs[w], out_refs[w].at[4 * x + 2 * y + c], local_sems.at[w])
            first = [copy(w, 0, me, sibling, src=x_refs[w])]
            first += [copy(w, 1 + j, me, (*chip, c), src=x_refs[w]) for j, chip in enumerate(chips)]
            passed = [copy(w, 4 + j, (*chip, c), sibling) for j, chip in enumerate(chips)]
            if p == 0:
                mine.start()
                for cp in first:
                    cp.start()
            elif p == 1:
                for j, chip in enumerate(chips):
                    copy(w, 1 + j, (*chip, c), me).wait_recv()
                    passed[j].start()
            else:
                copy(w, 0, sibling, me).wait_recv()
                for j, chip in enumerate(chips):
                    copy(w, 4 + j, (*chip, 1 - c), me).wait_recv()
                for cp in first + passed:
                    cp.wait_send()
                mine.wait()


class _Scatter:
    def __init__(self, grads, whole=()):
        self.n_sliced = len(grads)
        self.operands = list(grads) + list(whole)
        self.n = len(self.operands)
        self.out_shape = [jax.ShapeDtypeStruct(g.shape, g.dtype) for g in grads]
        self.out_shape += [jax.ShapeDtypeStruct((N_DEV,) + a.shape, a.dtype) for a in whole]
        n_sem = (N_DEV - 1) * self.n
        self.scratch = [pltpu.SemaphoreType.DMA((n_sem,)), pltpu.SemaphoreType.DMA((n_sem,)),
                        pltpu.SemaphoreType.DMA((self.n,))]

    def phase(self, p, in_refs, out_refs, sems):
        if p == 1:
            return
        send_sems, recv_sems, local_sems = sems
        x, y, c = _mesh_pos()
        me = 4 * x + 2 * y + c

        def src(w, dev):
            return in_refs[w].at[dev] if w < self.n_sliced else in_refs[w]

        for w in range(self.n):
            own = None if local_sems is None else pltpu.make_async_copy(src(w, me), out_refs[w].at[me], local_sems.at[w])
            sends, recvs = [], []
            for r in range(1, N_DEV):
                px = 1 - x if r & 4 else x
                py = 1 - y if r & 2 else y
                pc = 1 - c if r & 1 else c
                peer, pidx = (px, py, pc), 4 * px + 2 * py + pc
                k = (N_DEV - 1) * w + r - 1
                sends.append(pltpu.make_async_remote_copy(
                    src_ref=src(w, pidx), dst_ref=out_refs[w].at[me], send_sem=send_sems.at[k],
                    recv_sem=recv_sems.at[k], device_id=peer, device_id_type=pl.DeviceIdType.MESH))
                recvs.append(pltpu.make_async_remote_copy(
                    src_ref=src(w, me), dst_ref=out_refs[w].at[pidx], send_sem=send_sems.at[k],
                    recv_sem=recv_sems.at[k], device_id=peer, device_id_type=pl.DeviceIdType.MESH))
            if p == 0:
                if own is not None:
                    own.start()
                for cp in sends:
                    cp.start()
            else:
                for cp in recvs:
                    cp.wait_recv()
                for cp in sends:
                    cp.wait_send()
                if own is not None:
                    own.wait()


def _scatter_behind(ex, work, *, name):
    n = ex.n
    n_sem = (N_DEV - 1) * n
    anyspec = pl.BlockSpec(memory_space=pl.ANY)
    hbm = pl.BlockSpec(memory_space=pltpu.HBM)
    sem = pl.BlockSpec(memory_space=pltpu.SEMAPHORE)
    effect = pltpu.CompilerParams(has_side_effects=pltpu.SideEffectType.DATAFLOW_SIDE_EFFECTING)
    in_hbm = lambda a: pltpu.with_memory_space_constraint(a, pltpu.HBM)
    buffers = [pltpu.HBM(a.shape, a.dtype) for a in ex.operands] + [pltpu.HBM(s.shape, s.dtype) for s in ex.out_shape]

    me = 4 * lax.axis_index("x") + 2 * lax.axis_index("y") + lax.axis_index("c")
    lands = []
    for w, (a, s) in enumerate(zip(ex.operands, ex.out_shape)):
        mine = lax.dynamic_index_in_dim(a, me, 0, keepdims=True) if w < ex.n_sliced else a[None]
        lands.append(lax.dynamic_update_slice_in_dim(jnp.zeros(s.shape, s.dtype), mine, me, 0))

    def start_body(*refs):
        ex.phase(0, refs[:n], refs[n:2 * n], (refs[2 * n], refs[2 * n + 1], None))
        refs[-1][...] = jnp.zeros_like(refs[-1])

    started = pl.pallas_call(
        start_body, name=name + "_start",
        out_shape=[pltpu.SemaphoreType.DMA((n_sem,)), pltpu.SemaphoreType.DMA((n_sem,))] + buffers
        + [jax.ShapeDtypeStruct((8, LANES), F32)],
        in_specs=[hbm] * (2 * n), out_specs=[sem, sem] + [hbm] * (2 * n) + [pl.BlockSpec(memory_space=pltpu.VMEM)],
        input_output_aliases={i: 2 + i for i in range(2 * n)}, compiler_params=effect,
    )(*[in_hbm(a) for a in ex.operands], *[in_hbm(a) for a in lands])
    send_sems, recv_sems, token = started[0], started[1], started[-1]
    after, result = work(token)

    def wait_body(*refs):
        ex.phase(2, refs[:n], refs[n:2 * n], (refs[2 * n], refs[2 * n + 1], None))

    done = pl.pallas_call(
        wait_body, name=name + "_wait", out_shape=buffers,
        in_specs=[hbm] * (2 * n) + [sem, sem] + [anyspec] * len(after), out_specs=[hbm] * (2 * n),
        input_output_aliases={i: i for i in range(2 * n)}, compiler_params=effect,
    )(*started[2:2 + 2 * n], send_sems, recv_sems, *after)
    return done[n:], result


def _exchange_alone(ex, *, name):
    n = ex.n

    def body(*refs):
        for p in range(3):
            ex.phase(p, refs[:n], refs[n:2 * n], refs[2 * n:])

    anyspec = pl.BlockSpec(memory_space=pl.ANY)
    return pl.pallas_call(body, name=name, out_shape=ex.out_shape, in_specs=[anyspec] * n,
                          out_specs=[anyspec] * n, scratch_shapes=ex.scratch)(*ex.operands)


def _adam_step(w_ref, p_ref, m_ref, v_ref, g_ref, d_ref, nm_ref, nv_ref):
    g = p_ref[0].astype(F32)
    for j in range(1, N_DEV):
        g = g + p_ref[j].astype(F32)
    g_ref[...] = g
    nm = ADAM_B1 * m_ref[...] + (1.0 - ADAM_B1) * g
    nv = ADAM_B2 * v_ref[...] + (1.0 - ADAM_B2) * (g * g)
    nm_ref[...] = nm
    nv_ref[...] = nv
    m_hat = nm / (1.0 - ADAM_B1 ** ADAM_STEP)
    v_hat = nv / (1.0 - ADAM_B2 ** ADAM_STEP)
    d_ref[...] = -ADAM_LR * (m_hat / (jnp.sqrt(v_hat) + ADAM_EPS) + ADAM_WD * w_ref[...])


def _adamw_vectors(ws, parts, ms, vs, *, name):
    n = len(ws)

    def body(*refs):
        w_refs, p_refs, m_refs, v_refs = (refs[i * n:(i + 1) * n] for i in range(4))
        outs = refs[4 * n:]
        for i in range(n):
            _adam_step(w_refs[i], p_refs[i], m_refs[i], v_refs[i], *outs[4 * i:4 * i + 4])

    return pl.pallas_call(
        body, name=name,
        out_shape=[jax.ShapeDtypeStruct(w.shape, F32) for w in ws for _ in range(4)],
    )(*ws, *parts, *ms, *vs)


def _adamw(w, parts, m, v, after, *, name):
    G, R, n = w.shape
    tn = 256 if (n > 256 and n % 256 == 0) else n
    tr = R
    for t in range(16, R, 16):
        if R % t == 0 and t * tn <= 160 * 1024:
            tr = t
    if R * tn <= 160 * 1024:
        tr = R

    def body(w_ref, p_ref, m_ref, v_ref, after_ref, g_ref, d_ref, nm_ref, nv_ref):
        _adam_step(w_ref, p_ref, m_ref, v_ref, g_ref, d_ref, nm_ref, nv_ref)

    blk = pl.BlockSpec((None, tr, tn), lambda g, i, j: (g, i, j))
    return pl.pallas_call(
        body, name=name, grid=(G, R // tr, n // tn),
        in_specs=[blk, pl.BlockSpec((N_DEV, None, tr, tn), lambda g, i, j: (0, g, i, j)), blk, blk,
                  pl.BlockSpec((8, LANES), lambda g, i, j: (0, 0))],
        out_specs=[blk, blk, blk, blk],
        out_shape=[jax.ShapeDtypeStruct((G, R, n), F32)] * 4,
        compiler_params=_params(("parallel", "parallel", "parallel")),
    )(w, parts, m, v, after)


def _pad_last(a, width):
    return jnp.pad(a, [(0, 0)] * (a.ndim - 1) + [(0, width - a.shape[-1])])


def _cols_of(g):
    return g.transpose(1, 0, 2).reshape(g.shape[1], N_DEV * g.shape[2])


def _col_shards(w):
    return w.reshape(w.shape[0], N_DEV, w.shape[1] // N_DEV).transpose(1, 0, 2)


def kernel(x, positions, ffn1_pre_w, ffn1_w1, ffn1_w2, ffn1_post_w, mix_pre_w, w_in, mla_q_norm_w, mla_w_uq, mla_kv_norm_w, mla_w_ukv, ret_gn_w, w_branch_mla, w_branch_ret, w_out, mix_post_w, ffn2_pre_w, ffn2_w1, ffn2_w2, ffn2_post_w, loss_target, m_ffn1_pre_w, m_ffn1_w1, m_ffn1_w2, m_ffn1_post_w, m_mix_pre_w, m_w_in, m_mla_q_norm_w, m_mla_w_uq, m_mla_kv_norm_w, m_mla_w_ukv, m_ret_gn_w, m_w_branch_mla, m_w_branch_ret, m_w_out, m_mix_post_w, m_ffn2_pre_w, m_ffn2_w1, m_ffn2_w2, m_ffn2_post_w, v_ffn1_pre_w, v_ffn1_w1, v_ffn1_w2, v_ffn1_post_w, v_mix_pre_w, v_w_in, v_mla_q_norm_w, v_mla_w_uq, v_mla_kv_norm_w, v_mla_w_ukv, v_ret_gn_w, v_w_branch_mla, v_w_branch_ret, v_w_out, v_mix_post_w, v_ffn2_pre_w, v_ffn2_w1, v_ffn2_w2, v_ffn2_post_w):
    T, D = x.shape[1], x.shape[2]
    h0 = x[0]
    tgt = loss_target[0]
    pos = positions.reshape(T, 1).astype(F32)

    big = [("ffn1_w1", ffn1_w1, m_ffn1_w1, v_ffn1_w1), ("ffn1_w2", ffn1_w2, m_ffn1_w2, v_ffn1_w2),
           ("w_in", w_in, m_w_in, v_w_in), ("mla_w_uq", mla_w_uq, m_mla_w_uq, v_mla_w_uq),
           ("mla_w_ukv", mla_w_ukv, m_mla_w_ukv, v_mla_w_ukv),
           ("w_branch_mla", w_branch_mla, m_w_branch_mla, v_w_branch_mla),
           ("w_branch_ret", w_branch_ret, m_w_branch_ret, v_w_branch_ret),
           ("w_out", w_out, m_w_out, v_w_out),
           ("ffn2_w1", ffn2_w1, m_ffn2_w1, v_ffn2_w1), ("ffn2_w2", ffn2_w2, m_ffn2_w2, v_ffn2_w2)]
    small = [("ffn1_pre_w", ffn1_pre_w, m_ffn1_pre_w, v_ffn1_pre_w), ("ffn1_post_w", ffn1_post_w, m_ffn1_post_w, v_ffn1_post_w),
             ("mix_pre_w", mix_pre_w, m_mix_pre_w, v_mix_pre_w), ("mla_q_norm_w", mla_q_norm_w, m_mla_q_norm_w, v_mla_q_norm_w),
             ("mla_kv_norm_w", mla_kv_norm_w, m_mla_kv_norm_w, v_mla_kv_norm_w), ("ret_gn_w", ret_gn_w, m_ret_gn_w, v_ret_gn_w),
             ("mix_post_w", mix_post_w, m_mix_post_w, v_mix_post_w), ("ffn2_pre_w", ffn2_pre_w, m_ffn2_pre_w, v_ffn2_pre_w),
             ("ffn2_post_w", ffn2_post_w, m_ffn2_post_w, v_ffn2_post_w)]

    half = ffn1_w2.shape[1]
    hp = -(-half // LANES) * LANES

    def rows_view(w):
        return w[0].T

    def send_w1(w):
        return jnp.pad(rows_view(w).reshape(2, half, D), ((0, 0), (0, hp - half), (0, 0))).reshape(2 * hp, D).astype(BF16)

    def send_w2(w):
        return jnp.pad(w[0], ((0, hp - half), (0, 0))).astype(BF16)

    mixer = ["w_in", "mla_w_uq", "mla_w_ukv", "w_branch_mla", "w_branch_ret", "w_out"]
    uq_w = MLA_NOPE + MLA_ROPE
    mixer_send = [rows_view(w_in).astype(BF16), jnp.pad(rows_view(mla_w_uq), ((0, HP - uq_w), (0, 0))).astype(BF16),
                  mla_w_ukv[0].astype(BF16), w_branch_mla[0].astype(BF16), w_branch_ret[0].astype(BF16),
                  w_out[0].astype(BF16)]

    w1a, w2a = _exchange_alone(_Gather([send_w1(ffn1_w1), send_w2(ffn1_w2)]), name="gather_ffn1")
    w2a = w2a.reshape(N_DEV // 2, 2 * hp, D)
    u1, f1, h1, *got = _ffn_fwd(h0, ffn1_pre_w, w1a, w2a, ffn1_post_w, None, name="ffn1_fwd_gather_mixer",
                                exchange=_Gather(mixer_send))
    fw = dict(zip(mixer, got))

    wi = fw["w_in"].reshape(-1, D)
    cq_w, ckv_w, kr_w = wi[0:384], wi[384:640], wi[640:672]
    rq_w, rk_w = wi[672:928], wi[928:1184]
    rv_w, rg_w = wi[1184:1696], wi[1696:2208]
    gm_w, gr_w = wi[2208:2208 + D], wi[2208 + D:2208 + 2 * D]
    zer = lambda n: jnp.zeros((n, D), BF16)
    head_rows = lambda a, h: jnp.pad(a.reshape(h, -1, D), ((0, 0), (0, HP - a.shape[0] // h), (0, 0))).reshape(h * HP, D)
    w_in_p = jnp.concatenate([head_rows(rq_w, RET_HEADS), head_rows(rk_w, RET_HEADS), rv_w, rg_w,
                              cq_w, ckv_w, zer(MLA_NOPE), kr_w, zer(HP - MLA_NOPE - MLA_ROPE), zer(AW - 768),
                              gm_w, gr_w], axis=0)
    w_uq_p = fw["mla_w_uq"].reshape(QW, MLA_Q_RANK)
    ukv = fw["mla_w_ukv"].transpose(1, 0, 2)
    w_kv_p = jnp.concatenate([_pad_last(ukv[:, :, :MLA_NOPE], HP).reshape(MLA_KV_RANK, QW),
                              _pad_last(ukv[:, :, MLA_NOPE:], HP).reshape(MLA_KV_RANK, QW)], axis=1)
    w_bm_p = jnp.pad(_cols_of(fw["w_branch_mla"]).reshape(MLA_HEADS, MLA_V, D),
                     ((0, 0), (0, HP - MLA_V), (0, 0))).reshape(QW, D)
    w_br, w_o = _cols_of(fw["w_branch_ret"]), fw["w_out"].reshape(D, D)
    tab_mla = _rope_table(MLA_NOPE, MLA_ROPE // 2)
    tab_ret = _rope_table(0, RET_DK // 2)

    proj, a1 = _rms_matmul(h1, mix_pre_w, w_in_p, name="mixer_in_proj")
    q, k, v, qn, kvn = _mla_prep_fwd(proj, pos, mla_q_norm_w, mla_kv_norm_w, w_uq_p, w_kv_p, tab_mla, name="mla_prep_fwd")
    o, lse, w1b, w2b = _flash_fwd(q, k, v, name="mla_attn_fwd_gather_ffn2",
                                  exchange=_Gather([send_w1(ffn2_w1), send_w2(ffn2_w2)]))
    w2b = w2b.reshape(N_DEV // 2, 2 * hp, D)
    ypre, yn, rprev = _ret_fwd(proj, pos, tab_ret, name="retention_fwd")
    omla, oret, m, h2 = _merge_fwd(o, yn, proj, ret_gn_w, w_bm_p, w_br, w_o, h1, mix_post_w, name="merge_fwd")
    u2, f2, _, dy, lossp = _ffn_fwd(h2, ffn2_pre_w, w1b, w2b, ffn2_post_w, tgt, name="ffn2_fwd_loss")
    loss = lax.psum(jnp.sum(lossp[::8, 0]), ("x", "y", "c"))

    def grad(x, dy, tag, exchange=None):
        return _matmul_tn(x if x.ndim == 3 else x[None], dy if dy.ndim == 3 else dy[None], name=tag, exchange=exchange)

    g2, du2, df2, a2, dh2, gpost2, gpre2 = _ffn_bwd(dy, f2, ffn2_post_w, h2, ffn2_pre_w, u2, w2b, w1b, name="ffn2_bwd")
    dw1b, = grad(du2.reshape(N_DEV, T, 2 * hp), a2, "ffn2_dw1")
    dw2b = grad(g2, df2, "ffn2_dw2")[0].reshape(N_DEV, hp, D)
    (dmb, merged, dgm, dgr, domla, do, delta, doret, gated, drg, dyn, gpostm, ggn) = _merge_bwd(
        dh2, m, mix_post_w, omla, oret, proj, yn, ret_gn_w, o, w_o, w_bm_p, w_br, name="merge_bwd")
    dw_out = grad(merged, dmb, "dw_out")[0][0]
    dw_bm_p = grad(o, domla, "dw_branch_mla")[0][0]
    dw_br = grad(gated, doret, "dw_branch_ret")[0][0]
    dq, dk, dv, *recv_ffn2 = _flash_bwd(q, k, v, do, lse, delta, name="mla_attn_bwd_scatter_ffn2",
                                        exchange=_Scatter([dw1b, dw2b]))
    da, dql, dkvl, gqn, gkvn = _mla_prep_bwd(dq, dk, dv, proj, pos, mla_q_norm_w, mla_kv_norm_w, w_uq_p, w_kv_p, tab_mla, name="mla_prep_bwd")
    dw_uq_p = grad(dql, qn, "dw_uq")[0][0]
    dw_kv_p = grad(kvn, dkvl, "dw_ukv")[0][0]
    drq, drk, drv = _ret_bwd(dyn, ypre, proj, pos, tab_ret, rprev, name="retention_bwd")
    dproj = jnp.concatenate([drq, drk, drv, drg, da, dgm, dgr], axis=1)
    dw_in_p = grad(dproj, a1, "dw_in")[0][0]

    dw_uq = dw_uq_p.reshape(MLA_HEADS, HP, MLA_Q_RANK)[:, :uq_w]
    dkp = dw_kv_p[:, :QW].reshape(MLA_KV_RANK, MLA_HEADS, HP)[:, :, :MLA_NOPE]
    dvp = dw_kv_p[:, QW:].reshape(MLA_KV_RANK, MLA_HEADS, HP)[:, :, :MLA_V]
    dw_ukv = jnp.concatenate([dkp, dvp], axis=2).transpose(1, 0, 2)
    dw_bm = dw_bm_p.reshape(MLA_HEADS, HP, D)[:, :MLA_V].reshape(MLA_HEADS * MLA_V, D)
    small_mixer_grads = [dw_uq, dw_ukv, _col_shards(dw_bm), _col_shards(dw_br), dw_out.reshape(N_DEV, D // N_DEV, D)]
    dh1, gmixpre, *recv_small_mixer = _proj_bwd(dproj, w_in_p, h1, mix_pre_w, dh2, name="mixer_in_bwd_scatter_small",
                                                exchange=_Scatter(small_mixer_grads))
    unhead = lambda a, h, wd: a.reshape(h, HP, D)[:, :wd].reshape(h * wd, D)
    c0 = 4 * RW
    dw_in = jnp.concatenate([
        dw_in_p[c0:c0 + 384], dw_in_p[c0 + 384:c0 + 640], dw_in_p[c0 + 640 + MLA_NOPE:c0 + 640 + MLA_NOPE + MLA_ROPE],
        unhead(dw_in_p[0:RW], RET_HEADS, RET_DK), unhead(dw_in_p[RW:2 * RW], RET_HEADS, RET_DK),
        dw_in_p[2 * RW:3 * RW], dw_in_p[3 * RW:4 * RW],
        dw_in_p[PROJ_FIXED:PROJ_FIXED + D], dw_in_p[PROJ_FIXED + D:PROJ_FIXED + 2 * D]], axis=0).reshape(N_DEV, -1, D)
    g1, du1, df1, a0, dx, gpost1, gpre1, recv_w_in = _ffn_bwd(
        dh1, f1, ffn1_post_w, h0, ffn1_pre_w, u1, w2a, w1a, name="ffn1_bwd_scatter_w_in", exchange=_Scatter([dw_in]))
    recv_mixer = [recv_w_in] + recv_small_mixer
    dw2a = grad(g1, df1, "ffn1_dw2")[0].reshape(N_DEV, hp, D)
    dw1a, recv_w2a = grad(du1.reshape(N_DEV, T, 2 * hp), a0, "ffn1_dw1_scatter_dw2", exchange=_Scatter([dw2a]))

    small_g = {"ffn1_pre_w": gpre1, "ffn1_post_w": gpost1, "mix_pre_w": gmixpre, "mla_q_norm_w": gqn,
               "mla_kv_norm_w": gkvn, "ret_gn_w": ggn, "mix_post_w": gpostm, "ffn2_pre_w": gpre2, "ffn2_post_w": gpost2}
    parts = dict(zip(mixer, recv_mixer))
    parts.update(ffn1_w2=recv_w2a, ffn2_w1=recv_ffn2[0], ffn2_w2=recv_ffn2[1])
    as_is = (lambda a: a, lambda p: p[:, None], lambda a: a)
    views = {nm: as_is for nm, *_ in big}
    for nm in ("ffn1_w1", "ffn2_w1"):
        views[nm] = (lambda a: rows_view(a).reshape(2, half, D), lambda p: p.reshape(N_DEV, 2, hp, D),
                     lambda a: a.reshape(2 * half, D).T[None])
    for nm in ("w_in", "mla_w_uq"):
        views[nm] = (lambda a: rows_view(a)[None], lambda p: p[:, None], lambda a: a[0].T[None])

    def update(nm, w, m_, v_, after):
        to_view, parts_view, back = views[nm]
        return [back(a) for a in _adamw(to_view(w), parts_view(parts[nm]), to_view(m_), to_view(v_), after,
                                        name="adamw_" + nm)]

    def other_updates(token):
        done = {nm: update(nm, w, m_, v_, token) for nm, w, m_, v_ in big if nm != "ffn1_w1"}
        return [d[0] for d in done.values()], done

    (recv_w1a, *small_parts), big_out = _scatter_behind(
        _Scatter([dw1a], whole=[small_g[nm] for nm, *_ in small]), other_updates, name="scatter_ffn1_dw1")
    parts["ffn1_w1"] = recv_w1a
    big_out["ffn1_w1"] = update("ffn1_w1", ffn1_w1, m_ffn1_w1, v_ffn1_w1, jnp.zeros((8, LANES), F32))
    small_out = _adamw_vectors([w for _, w, _, _ in small], small_parts, [a for _, _, a, _ in small],
                               [a for _, _, _, a in small], name="adamw_replicated")

    order = ["ffn1_pre_w", "ffn1_w1", "ffn1_w2", "ffn1_post_w", "mix_pre_w", "w_in", "mla_q_norm_w", "mla_w_uq",
             "mla_kv_norm_w", "mla_w_ukv", "ret_gn_w", "w_branch_mla", "w_branch_ret", "w_out", "mix_post_w",
             "ffn2_pre_w", "ffn2_w1", "ffn2_w2", "ffn2_post_w"]
    outs = [loss, dx[None]]
    for i in range(4):
        both = {nm: big_out[nm][i] for nm in big_out}
        both.update({nm: small_out[4 * j + i] for j, (nm, *_) in enumerate(small)})
        outs += [both[nm] for nm in order]
    return tuple(outs)
```

```python
import math

import numpy as np
import jax
import jax.numpy as jnp
from jax import lax
from jax.experimental import pallas as pl
from jax.experimental.pallas import tpu as pltpu

F32, BF16 = jnp.float32, jnp.bfloat16

MLA_HEADS, MLA_NOPE, MLA_ROPE, MLA_V = 8, 64, 32, 64
MLA_Q_RANK, MLA_KV_RANK = 384, 256
RET_HEADS, RET_DK, RET_DV = 4, 64, 128
ROPE_BASE, NORM_EPS, GN_EPS = 10000.0, 1e-6, 1e-6
ADAM_LR, ADAM_B1, ADAM_B2, ADAM_EPS, ADAM_WD, ADAM_STEP = 0.001, 0.9, 0.999, 1e-08, 0.01, 10
ATTN_SCALE = 1.0 / math.sqrt(MLA_NOPE + MLA_ROPE)

N_DEV = 8
LANES = 128
HP = LANES
QW = MLA_HEADS * HP
RW = RET_HEADS * HP
AW = 1024
PROJ_FIXED = 4 * RW + AW
NEG = -1e30

TOKEN_TILE = 512
ATTN_TILE = 1024
ATTN_CHAINS = 2
FFN_CHAINS = 2
RET_TILE = 256
PROJ_TILE_CAP = 2560
GRAD_TILE_CAP = 1408
GRAD_TOKEN_TILE = 2048
MERGE_TILE = 256
VMEM_LIMIT = 56 * 1024 * 1024


def _tile(n, cap, mult=LANES):
    if n <= cap:
        return n
    best = None
    for t in range(mult, cap + 1, mult):
        if n % t == 0:
            best = t
    assert best is not None, (n, cap, mult)
    return best


def _params(sem):
    return pltpu.CompilerParams(dimension_semantics=sem, vmem_limit_bytes=VMEM_LIMIT)


def _dot(a, b):
    return lax.dot_general(a, b, (((1,), (0,)), ((), ())), preferred_element_type=F32)


def _dot_nt(a, b):
    return lax.dot_general(a, b, (((1,), (1,)), ((), ())), preferred_element_type=F32)


def _dot_tn(a, b):
    return lax.dot_general(a, b, (((0,), (0,)), ((), ())), preferred_element_type=F32)


def _sigmoid(x):
    return pl.reciprocal(1.0 + jnp.exp(-x), approx=True)


def _rms_fwd(x, w):
    r = lax.rsqrt(jnp.mean(x * x, axis=-1, keepdims=True) + NORM_EPS)
    return x * r * w


def _rms_bwd(x, w, dy):
    r = lax.rsqrt(jnp.mean(x * x, axis=-1, keepdims=True) + NORM_EPS)
    xh = x * r
    g = dy * w
    dx = r * (g - xh * jnp.mean(g * xh, axis=-1, keepdims=True))
    return dx, jnp.sum(dy * xh, axis=0, keepdims=True)


def _rope_table(first, half):
    inv = (np.float32(ROPE_BASE) ** (-(np.arange(half, dtype=np.float32) / np.float32(half)))).astype(np.float32)
    tab = np.zeros((8, LANES), np.float32)
    tab[0, first:first + half] = inv
    tab[0, first + half:first + 2 * half] = inv
    tab[1, first:first + half] = -1.0
    tab[2, first + half:first + 2 * half] = 1.0
    return jnp.asarray(tab)


def _rope_cs(pos, tab_ref):
    ang = pos * tab_ref[0:1, :]
    s = jnp.sin(ang)
    return jnp.cos(ang), s * tab_ref[1:2, :], s * tab_ref[2:3, :]


def _rope(x, cs, half, inverse=False):
    c, s1, s2 = cs
    a = pltpu.roll(x, LANES - half, 1) * s1 + pltpu.roll(x, half, 1) * s2
    return x * c - a if inverse else x * c + a


def _call(body, *, name, grid, in_specs, out_specs, out_shape, scratch_shapes, args, exchange=None, after=None):
    sem = ("arbitrary",) * len(grid)
    anyspec = pl.BlockSpec(memory_space=pl.ANY)
    if exchange is None and after is not None:
        n_own = len(in_specs)

        def behind(*refs):
            body(*refs[:n_own], *refs[n_own + 1:])

        return pl.pallas_call(behind, name=name, grid=grid, in_specs=list(in_specs) + [anyspec], out_specs=out_specs,
                              out_shape=out_shape, scratch_shapes=scratch_shapes, compiler_params=_params(sem))(*args, after)
    if exchange is None:
        return pl.pallas_call(body, name=name, grid=grid, in_specs=in_specs, out_specs=out_specs,
                              out_shape=out_shape, scratch_shapes=scratch_shapes, compiler_params=_params(sem))(*args)
    n_in, n_out, e = len(in_specs), len(out_specs), exchange.n
    total = math.prod(grid)

    def carried(*refs):
        own = refs[:n_in] + refs[n_in + e:n_in + e + n_out] + refs[n_in + 2 * e + n_out:len(refs) - 3]
        ex_refs = (refs[n_in:n_in + e], refs[n_in + e + n_out:n_in + 2 * e + n_out], refs[len(refs) - 3:])
        step = pl.program_id(0)
        for d in range(1, len(grid)):
            step = step * grid[d] + pl.program_id(d)

        @pl.when(step == 0)
        def _():
            exchange.phase(0, *ex_refs)

        @pl.when(step == (3 * total) // 4)
        def _():
            exchange.phase(1, *ex_refs)

        body(*own)

        @pl.when(step == total - 1)
        def _():
            exchange.phase(2, *ex_refs)

    return pl.pallas_call(
        carried, name=name, grid=grid, in_specs=list(in_specs) + [anyspec] * e,
        out_specs=list(out_specs) + [anyspec] * e, out_shape=list(out_shape) + exchange.out_shape,
        scratch_shapes=list(scratch_shapes) + exchange.scratch, compiler_params=_params(sem),
    )(*args, *exchange.operands)


def _ffn_fwd(h, pre_w, w1, w2, post_w, target, *, name, exchange=None):
    T, D = h.shape
    nk, ck = w2.shape[0], w2.shape[1]
    tT = min(TOKEN_TILE, T)
    nT = T // tT
    with_loss = target is not None

    def body(*refs):
        if with_loss:
            (h_ref, pre_ref, w1g_ref, w1u_ref, w2_ref, post_ref, tgt_ref,
             u_ref, f_ref, ho_ref, dy_ref, loss_ref, a_s, acc) = refs
        else:
            (h_ref, pre_ref, w1g_ref, w1u_ref, w2_ref, post_ref,
             u_ref, f_ref, ho_ref, a_s, acc) = refs
        k = pl.program_id(1)

        @pl.when(k == 0)
        def _():
            a_s[...] = _rms_fwd(h_ref[...], pre_ref[...]).astype(BF16)
            acc[...] = jnp.zeros_like(acc)

        for c in range(FFN_CHAINS):
            rs = slice(c * (tT // FFN_CHAINS), (c + 1) * (tT // FFN_CHAINS))
            a = a_s[rs, :]
            ug = _dot_nt(a, w1g_ref[...])
            uu = _dot_nt(a, w1u_ref[...])
            u_ref[0, rs, :] = ug.astype(BF16)
            u_ref[1, rs, :] = uu.astype(BF16)
            acc[rs, :] += _dot((ug * _sigmoid(ug) * uu).astype(BF16), w2_ref[...])

        @pl.when(k == nk - 1)
        def _():
            f = acc[...]
            f_ref[...] = f
            ho = h_ref[...] + 0.5 * _rms_fwd(f, post_ref[...])
            ho_ref[...] = ho
            if with_loss:
                e = ho - tgt_ref[...]
                dy_ref[...] = e * (1.0 / D)
                loss_ref[...] = jnp.full(loss_ref.shape, (0.5 / D) * jnp.sum(e * e), F32)

    row = pl.BlockSpec((tT, D), lambda i, k: (i, 0))
    vec = pl.BlockSpec((1, D), lambda i, k: (0, 0))
    in_specs = [row, vec,
                pl.BlockSpec((None, ck, D), lambda i, k: (k, 0, 0)),
                pl.BlockSpec((None, ck, D), lambda i, k: (nk + k, 0, 0)),
                pl.BlockSpec((None, ck, D), lambda i, k: (k, 0, 0)),
                vec]
    out_shape = [jax.ShapeDtypeStruct((2, nk, T, ck), BF16),
                 jax.ShapeDtypeStruct((T, D), F32),
                 jax.ShapeDtypeStruct((T, D), F32)]
    out_specs = [pl.BlockSpec((2, None, tT, ck), lambda i, k: (0, k, i, 0)), row, row]
    args = [h, pre_w, w1, w1, w2, post_w]
    if with_loss:
        in_specs.append(row)
        args.append(target)
        out_shape += [jax.ShapeDtypeStruct((T, D), F32), jax.ShapeDtypeStruct((nT * 8, LANES), F32)]
        out_specs += [row, pl.BlockSpec((8, LANES), lambda i, k: (i, 0))]
    return _call(body, name=name, grid=(nT, nk), in_specs=in_specs, out_specs=out_specs, out_shape=out_shape,
                 scratch_shapes=[pltpu.VMEM((tT, D), BF16), pltpu.VMEM((tT, D), F32)], args=args, exchange=exchange)


def _ffn_bwd(dho, f, post_w, h, pre_w, u, w2, w1, *, name, exchange=None, after=None):
    T, D = h.shape
    nk, ck = w2.shape[0], w2.shape[1]
    tT = min(TOKEN_TILE, T)
    nT = T // tT

    def body(dho_ref, f_ref, post_ref, h_ref, pre_ref, u_ref, w2_ref, w1g_ref, w1u_ref,
             g_ref, du_ref, df_ref, a_ref, dh_ref, gpost_ref, gpre_ref, df_s, da_acc):
        i, k = pl.program_id(0), pl.program_id(1)

        @pl.when(jnp.logical_and(i == 0, k == 0))
        def _():
            gpost_ref[...] = jnp.zeros_like(gpost_ref)
            gpre_ref[...] = jnp.zeros_like(gpre_ref)

        @pl.when(k == 0)
        def _():
            dx, dw = _rms_bwd(f_ref[...], post_ref[...], 0.5 * dho_ref[...])
            dfb = dx.astype(BF16)
            df_s[...] = dfb
            df_ref[...] = dfb
            gpost_ref[...] += dw
            a_ref[...] = _rms_fwd(h_ref[...], pre_ref[...]).astype(BF16)
            da_acc[...] = jnp.zeros_like(da_acc)

        groups = [slice(c * (tT // FFN_CHAINS), (c + 1) * (tT // FFN_CHAINS)) for c in range(FFN_CHAINS)]
        dgs = [_dot_nt(df_s[rs, :], w2_ref[...]) for rs in groups]
        for rs, dg in zip(groups, dgs):
            ug = u_ref[0, rs, :].astype(F32)
            uu = u_ref[1, rs, :].astype(F32)
            sg = _sigmoid(ug)
            sl = ug * sg
            g_ref[rs, :] = (sl * uu).astype(BF16)
            dug = (dg * uu * (sg + sl * (1.0 - sg))).astype(BF16)
            duu = (dg * sl).astype(BF16)
            du_ref[0, rs, :] = dug
            du_ref[1, rs, :] = duu
            da_acc[rs, :] += _dot(dug, w1g_ref[...]) + _dot(duu, w1u_ref[...])

        @pl.when(k == nk - 1)
        def _():
            dx, dw = _rms_bwd(h_ref[...], pre_ref[...], da_acc[...])
            dh_ref[...] = dho_ref[...] + dx
            gpre_ref[...] += dw

    row = pl.BlockSpec((tT, D), lambda i, k: (i, 0))
    vec = pl.BlockSpec((1, D), lambda i, k: (0, 0))
    return _call(
        body, name=name, grid=(nT, nk),
        in_specs=[row, row, vec, row, vec,
                  pl.BlockSpec((2, None, tT, ck), lambda i, k: (0, k, i, 0)),
                  pl.BlockSpec((None, ck, D), lambda i, k: (k, 0, 0)),
                  pl.BlockSpec((None, ck, D), lambda i, k: (k, 0, 0)),
                  pl.BlockSpec((None, ck, D), lambda i, k: (nk + k, 0, 0))],
        out_specs=[pl.BlockSpec((None, tT, ck), lambda i, k: (k, i, 0)),
                   pl.BlockSpec((2, None, tT, ck), lambda i, k: (0, k, i, 0)),
                   row, row, row, vec, vec],
        out_shape=[jax.ShapeDtypeStruct((nk, T, ck), BF16),
                   jax.ShapeDtypeStruct((2, nk, T, ck), BF16),
                   jax.ShapeDtypeStruct((T, D), BF16),
                   jax.ShapeDtypeStruct((T, D), BF16),
                   jax.ShapeDtypeStruct((T, D), F32),
                   jax.ShapeDtypeStruct((1, D), F32),
                   jax.ShapeDtypeStruct((1, D), F32)],
        scratch_shapes=[pltpu.VMEM((tT, D), BF16), pltpu.VMEM((tT, D), F32)],
        args=(dho, f, post_w, h, pre_w, u, w2, w1, w1), exchange=exchange, after=after)


def _matmul_tn(x, dy, *, name, exchange=None, after=None):
    Px, T, K = x.shape
    Py, _, N = dy.shape
    P = max(Px, Py)
    tT, tK, tN = min(GRAD_TOKEN_TILE, T), _tile(K, GRAD_TILE_CAP), _tile(N, GRAD_TILE_CAP)
    nt = T // tT

    def body(x_ref, dy_ref, o_ref, acc):
        t = pl.program_id(3)

        @pl.when(t == 0)
        def _():
            acc[...] = jnp.zeros_like(acc)

        acc[...] += _dot_tn(x_ref[...], dy_ref[...])

        @pl.when(t == nt - 1)
        def _():
            o_ref[...] = acc[...].astype(BF16)

    return _call(
        body, name=name, grid=(P, K // tK, N // tN, nt),
        in_specs=[pl.BlockSpec((None, tT, tK), lambda p, a, b, t: (p if Px > 1 else 0, t, a)),
                  pl.BlockSpec((None, tT, tN), lambda p, a, b, t: (p if Py > 1 else 0, t, b))],
        out_specs=[pl.BlockSpec((None, tK, tN), lambda p, a, b, t: (p, a, b))],
        out_shape=[jax.ShapeDtypeStruct((P, K, N), BF16)],
        scratch_shapes=[pltpu.VMEM((tK, tN), F32)], args=(x, dy), exchange=exchange, after=after)


def _rms_matmul(h, wn, w, *, name):
    T, D = h.shape
    N = w.shape[0]
    tT, tN = min(TOKEN_TILE, T), _tile(N, PROJ_TILE_CAP)

    def body(h_ref, wn_ref, w_ref, y_ref, a_ref):
        @pl.when(pl.program_id(1) == 0)
        def _():
            a_ref[...] = _rms_fwd(h_ref[...], wn_ref[...]).astype(BF16)

        y_ref[...] = _dot_nt(a_ref[...], w_ref[...]).astype(BF16)

    return pl.pallas_call(
        body, name=name, grid=(T // tT, N // tN),
        in_specs=[pl.BlockSpec((tT, D), lambda i, j: (i, 0)),
                  pl.BlockSpec((1, D), lambda i, j: (0, 0)),
                  pl.BlockSpec((tN, D), lambda i, j: (j, 0))],
        out_specs=[pl.BlockSpec((tT, tN), lambda i, j: (i, j)),
                   pl.BlockSpec((tT, D), lambda i, j: (i, 0))],
        out_shape=[jax.ShapeDtypeStruct((T, N), BF16), jax.ShapeDtypeStruct((T, D), BF16)],
        compiler_params=_params(("parallel", "arbitrary")),
    )(h, wn, w)


def _proj_bwd(dproj, w, h, wn, dres, *, name, exchange=None, after=None):
    T, D = h.shape
    N = w.shape[0]
    tT, tN = min(TOKEN_TILE, T), _tile(N, PROJ_TILE_CAP)
    nn = N // tN

    def body(dp_ref, w_ref, h_ref, wn_ref, dres_ref, dh_ref, gw_ref, acc):
        i, j = pl.program_id(0), pl.program_id(1)

        @pl.when(jnp.logical_and(i == 0, j == 0))
        def _():
            gw_ref[...] = jnp.zeros_like(gw_ref)

        @pl.when(j == 0)
        def _():
            acc[...] = jnp.zeros_like(acc)

        acc[...] += _dot(dp_ref[...], w_ref[...])

        @pl.when(j == nn - 1)
        def _():
            dx, dw = _rms_bwd(h_ref[...], wn_ref[...], acc[...])
            dh_ref[...] = dres_ref[...] + dx
            gw_ref[...] += dw

    row = pl.BlockSpec((tT, D), lambda i, j: (i, 0))
    vec = pl.BlockSpec((1, D), lambda i, j: (0, 0))
    return _call(
        body, name=name, grid=(T // tT, nn),
        in_specs=[pl.BlockSpec((tT, tN), lambda i, j: (i, j)),
                  pl.BlockSpec((tN, D), lambda i, j: (j, 0)), row, vec, row],
        out_specs=[row, vec],
        out_shape=[jax.ShapeDtypeStruct((T, D), F32), jax.ShapeDtypeStruct((1, D), F32)],
        scratch_shapes=[pltpu.VMEM((tT, D), F32)], args=(dproj, w, h, wn, dres), exchange=exchange, after=after)


def _mla_prep_fwd(proj, pos, qn_w, kvn_w, w_uq, w_kv, tab, *, name):
    T = proj.shape[0]
    tT = min(TOKEN_TILE, T)
    a_blk = PROJ_FIXED // AW - 1

    def body(a_ref, pos_ref, qnw_ref, kvnw_ref, wuq_ref, wkv_ref, tab_ref,
             q_ref, k_ref, v_ref, qn_ref, kvn_ref):
        cq = a_ref[:, 0:MLA_Q_RANK].astype(F32)
        ckv = a_ref[:, MLA_Q_RANK:MLA_Q_RANK + MLA_KV_RANK].astype(F32)
        kr = a_ref[:, 640:768].astype(F32)
        qn = _rms_fwd(cq, qnw_ref[...]).astype(BF16)
        kvn = _rms_fwd(ckv, kvnw_ref[...]).astype(BF16)
        qn_ref[...] = qn
        kvn_ref[...] = kvn
        cs = _rope_cs(pos_ref[...], tab_ref)
        q = _dot_nt(qn, wuq_ref[...])
        kv = _dot(kvn, wkv_ref[...])
        krr = _rope(kr, cs, MLA_ROPE // 2)
        for hd in range(MLA_HEADS):
            sl = slice(hd * HP, (hd + 1) * HP)
            q_ref[:, sl] = (_rope(q[:, sl], cs, MLA_ROPE // 2) * ATTN_SCALE).astype(BF16)
            k_ref[:, sl] = (kv[:, sl] + krr).astype(BF16)
        v_ref[...] = kv[:, QW:].astype(BF16)

    def full(r, c):
        return pl.BlockSpec((r, c), lambda i: (0, 0))

    def rows(c):
        return pl.BlockSpec((tT, c), lambda i: (i, 0))

    return pl.pallas_call(
        body, name=name, grid=(T // tT,),
        in_specs=[pl.BlockSpec((tT, AW), lambda i: (i, a_blk)), rows(1),
                  full(1, MLA_Q_RANK), full(1, MLA_KV_RANK),
                  full(QW, MLA_Q_RANK), full(MLA_KV_RANK, 2 * QW), full(8, LANES)],
        out_specs=[rows(QW), rows(QW), rows(QW), rows(MLA_Q_RANK), rows(MLA_KV_RANK)],
        out_shape=[jax.ShapeDtypeStruct((T, QW), BF16)] * 3
        + [jax.ShapeDtypeStruct((T, MLA_Q_RANK), BF16), jax.ShapeDtypeStruct((T, MLA_KV_RANK), BF16)],
        compiler_params=_params(("parallel",)),
    )(proj, pos, qn_w, kvn_w, w_uq, w_kv, tab)


def _mla_prep_bwd(dq, dk, dv, proj, pos, qn_w, kvn_w, w_uq, w_kv, tab, *, name):
    T = proj.shape[0]
    tT = min(TOKEN_TILE, T)
    a_blk = PROJ_FIXED // AW - 1

    def body(dq_ref, dk_ref, dv_ref, a_ref, pos_ref, qnw_ref, kvnw_ref, wuq_ref, wkv_ref, tab_ref,
             da_ref, dql_ref, dkvl_ref, gqn_ref, gkvn_ref):
        @pl.when(pl.program_id(0) == 0)
        def _():
            gqn_ref[...] = jnp.zeros_like(gqn_ref)
            gkvn_ref[...] = jnp.zeros_like(gkvn_ref)

        cs = _rope_cs(pos_ref[...], tab_ref)
        dkr = jnp.zeros((tT, HP), F32)
        for hd in range(MLA_HEADS):
            sl = slice(hd * HP, (hd + 1) * HP)
            dql_ref[:, sl] = (_rope(dq_ref[:, sl], cs, MLA_ROPE // 2, inverse=True) * ATTN_SCALE).astype(BF16)
            dkh = dk_ref[:, sl]
            dkr = dkr + dkh
            dkvl_ref[:, sl] = dkh.astype(BF16)
        dkvl_ref[:, QW:] = dv_ref[...]
        dqn = _dot(dql_ref[...], wuq_ref[...])
        dkvn = _dot_nt(dkvl_ref[...], wkv_ref[...])
        cq = a_ref[:, 0:MLA_Q_RANK].astype(F32)
        ckv = a_ref[:, MLA_Q_RANK:MLA_Q_RANK + MLA_KV_RANK].astype(F32)
        dcq, gq = _rms_bwd(cq, qnw_ref[...], dqn)
        dckv, gkv = _rms_bwd(ckv, kvnw_ref[...], dkvn)
        gqn_ref[...] += gq
        gkvn_ref[...] += gkv
        da_ref[:, 0:MLA_Q_RANK] = dcq.astype(BF16)
        da_ref[:, MLA_Q_RANK:MLA_Q_RANK + MLA_KV_RANK] = dckv.astype(BF16)
        da_ref[:, 640:768] = _rope(dkr, cs, MLA_ROPE // 2, inverse=True).astype(BF16)
        da_ref[:, 768:AW] = jnp.zeros((tT, AW - 768), BF16)

    def full(r, c):
        return pl.BlockSpec((r, c), lambda i: (0, 0))

    def rows(c):
        return pl.BlockSpec((tT, c), lambda i: (i, 0))

    return pl.pallas_call(
        body, name=name, grid=(T // tT,),
        in_specs=[rows(QW), rows(QW), rows(QW), pl.BlockSpec((tT, AW), lambda i: (i, a_blk)), rows(1),
                  full(1, MLA_Q_RANK), full(1, MLA_KV_RANK),
                  full(QW, MLA_Q_RANK), full(MLA_KV_RANK, 2 * QW), full(8, LANES)],
        out_specs=[rows(AW), rows(QW), rows(2 * QW), full(1, MLA_Q_RANK), full(1, MLA_KV_RANK)],
        out_shape=[jax.ShapeDtypeStruct((T, AW), BF16), jax.ShapeDtypeStruct((T, QW), BF16),
                   jax.ShapeDtypeStruct((T, 2 * QW), BF16),
                   jax.ShapeDtypeStruct((1, MLA_Q_RANK), F32), jax.ShapeDtypeStruct((1, MLA_KV_RANK), F32)],
        compiler_params=_params(("arbitrary",)),
    )(dq, dk, dv, proj, pos, qn_w, kvn_w, w_uq, w_kv, tab)


def _flash_fwd(q, k, v, *, name, exchange=None):
    T = q.shape[0]
    H = q.shape[1] // HP
    tq = min(ATTN_TILE, T)
    nq = T // tq

    sub = tq // ATTN_CHAINS

    def body(q_ref, k_ref, v_ref, o_ref, lse_ref):
        qi = pl.program_id(1)
        qs = [q_ref[c * sub:(c + 1) * sub, :] for c in range(ATTN_CHAINS)]

        def update(carry, off, masked):
            nks = [(c + 1) * sub if masked else tq for c in range(ATTN_CHAINS)]
            scores = [_dot_nt(qs[c], k_ref[pl.ds(off, nks[c]), :]) for c in range(ATTN_CHAINS)]
            out = []
            for c in range(ATTN_CHAINS):
                m_prev, l_prev, acc = carry[c]
                nk, s = nks[c], scores[c]
                vb = v_ref[pl.ds(off, nk), :]
                if masked:
                    rows = lax.broadcasted_iota(jnp.int32, (sub, nk), 0) + c * sub
                    s = jnp.where(rows >= lax.broadcasted_iota(jnp.int32, (sub, nk), 1), s, NEG)
                m_new = jnp.maximum(m_prev, jnp.max(s, axis=1, keepdims=True))
                alpha = jnp.exp(m_prev - m_new)
                p = jnp.exp(s - m_new)
                out.append((m_new, alpha * l_prev + jnp.sum(p, axis=1, keepdims=True),
                            alpha * acc + _dot(p.astype(BF16), vb)))
            return tuple(out)

        init = tuple((jnp.full((sub, 1), NEG, F32), jnp.zeros((sub, 1), F32), jnp.zeros((sub, HP), F32))
                     for _ in range(ATTN_CHAINS))
        carry = lax.fori_loop(0, qi, lambda j, cr: update(cr, pl.multiple_of(j * tq, tq), False), init)
        carry = update(carry, pl.multiple_of(qi * tq, tq), True)
        for c in range(ATTN_CHAINS):
            m_fin, l_fin, acc = carry[c]
            o_ref[c * sub:(c + 1) * sub, :] = (acc / l_fin).astype(BF16)
            lse_ref[c * sub:(c + 1) * sub, :] = jnp.broadcast_to(m_fin + jnp.log(l_fin), (sub, HP))

    qspec = pl.BlockSpec((tq, HP), lambda h, i: (i, h))
    kspec = pl.BlockSpec((T, HP), lambda h, i: (0, h))
    return _call(
        body, name=name, grid=(H, nq),
        in_specs=[qspec, kspec, kspec], out_specs=[qspec, qspec],
        out_shape=[jax.ShapeDtypeStruct((T, H * HP), BF16), jax.ShapeDtypeStruct((T, H * HP), F32)],
        scratch_shapes=[], args=(q, k, v), exchange=exchange)


def _flash_bwd(q, k, v, do, lse, delta, *, name, exchange=None, after=None):
    T = q.shape[0]
    H = q.shape[1] // HP
    tq = min(ATTN_TILE, T)
    nq = T // tq
    sub = tq // ATTN_CHAINS

    def body(k_ref, v_ref, q_ref, do_ref, lse_ref, dl_ref, dq_ref, dk_ref, dv_ref):
        ki = pl.program_id(1)

        @pl.when(ki == 0)
        def _():
            dq_ref[...] = jnp.zeros_like(dq_ref)

        def grow(a):
            return a if a.shape[0] == tq else jnp.concatenate([a, jnp.zeros((tq - a.shape[0], HP), F32)], axis=0)

        def step(carry, j, masked):
            dk_acc, dv_acc = carry
            nks = [(c + 1) * sub if masked else tq for c in range(ATTN_CHAINS)]
            rws = [pl.ds(pl.multiple_of(j * tq + c * sub, sub), sub) for c in range(ATTN_CHAINS)]
            scores = [_dot_nt(q_ref[rws[c], :], k_ref[0:nks[c], :]) for c in range(ATTN_CHAINS)]
            dps = [_dot_nt(do_ref[rws[c], :], v_ref[0:nks[c], :]) for c in range(ATTN_CHAINS)]
            for c in range(ATTN_CHAINS):
                rows, nk, s, dp = rws[c], nks[c], scores[c], dps[c]
                kb = k_ref[0:nk, :]
                qb = q_ref[rows, :]
                dob = do_ref[rows, :]
                if masked:
                    ri = lax.broadcasted_iota(jnp.int32, (sub, nk), 0) + c * sub
                    s = jnp.where(ri >= lax.broadcasted_iota(jnp.int32, (sub, nk), 1), s, NEG)
                p = jnp.exp(s - lse_ref[rows, 0:1])
                dv_acc = dv_acc + grow(_dot_tn(p.astype(BF16), dob))
                ds = (p * (dp - dl_ref[rows, 0:1])).astype(BF16)
                dk_acc = dk_acc + grow(_dot_tn(ds, qb))
                dq_ref[rows, :] += _dot(ds, kb)
            return dk_acc, dv_acc

        carry = step((jnp.zeros((tq, HP), F32), jnp.zeros((tq, HP), F32)), ki, True)
        dk_acc, dv_acc = lax.fori_loop(ki + 1, nq, lambda j, cr: step(cr, j, False), carry)
        dk_ref[...] = dk_acc
        dv_ref[...] = dv_acc.astype(BF16)

    kspec = pl.BlockSpec((tq, HP), lambda h, j: (j, h))
    full = pl.BlockSpec((T, HP), lambda h, j: (0, h))
    return _call(
        body, name=name, grid=(H, nq),
        in_specs=[kspec, kspec, full, full, full, full], out_specs=[full, kspec, kspec],
        out_shape=[jax.ShapeDtypeStruct((T, H * HP), F32), jax.ShapeDtypeStruct((T, H * HP), F32),
                   jax.ShapeDtypeStruct((T, H * HP), BF16)],
        scratch_shapes=[], args=(k, v, q, do, lse, delta), exchange=exchange, after=after)


def _ret_consts(cc, hd):
    lg = math.log(1.0 - 2.0 ** (-5.0 - hd))
    diff = (lax.broadcasted_iota(jnp.int32, (cc, cc), 0) - lax.broadcasted_iota(jnp.int32, (cc, cc), 1)).astype(F32)
    decay = jnp.where(diff >= 0, jnp.exp(jnp.maximum(diff, 0.0) * lg), 0.0)
    idx = lax.broadcasted_iota(jnp.int32, (cc, 1), 0).astype(F32)
    zeta = jnp.exp((cc - 1.0 - idx) * lg)
    xi = jnp.exp((idx + 1.0) * lg)
    return decay, zeta, xi, math.exp(cc * lg)


def _ret_fwd(proj, pos, tab, *, name):
    T = proj.shape[0]
    cc = min(RET_TILE, T)
    n = T // cc

    def body(rq_ref, rk_ref, rv_ref, pos_ref, tab_ref, y_ref, yn_ref, rprev_ref, r_s):
        @pl.when(pl.program_id(0) == 0)
        def _():
            r_s[...] = jnp.zeros_like(r_s)

        cs = _rope_cs(pos_ref[...], tab_ref)
        for hd in range(RET_HEADS):
            sl = slice(hd * HP, (hd + 1) * HP)
            decay, zeta, xi, gc = _ret_consts(cc, hd)
            q = _rope(rq_ref[:, sl].astype(F32), cs, RET_DK // 2).astype(BF16)
            kf = _rope(rk_ref[:, sl].astype(F32), cs, RET_DK // 2) * (RET_DK ** -0.5)
            k = kf.astype(BF16)
            v = rv_ref[:, sl]
            r = r_s[hd]
            rprev_ref[0, hd] = r
            inner = (_dot_nt(q, k) * decay).astype(BF16)
            y = _dot(inner, v) + _dot(q, r.astype(BF16)) * xi
            r_s[hd] = r * gc + _dot_tn((kf * zeta).astype(BF16), v)
            y_ref[:, sl] = y
            mu = jnp.mean(y, axis=-1, keepdims=True)
            yc = y - mu
            var = jnp.mean(yc * yc, axis=-1, keepdims=True)
            yn_ref[:, sl] = (yc * lax.rsqrt(var + GN_EPS)).astype(BF16)

    def blk(j):
        return pl.BlockSpec((cc, RW), lambda i: (i, j))

    return pl.pallas_call(
        body, name=name, grid=(n,),
        in_specs=[blk(0), blk(1), blk(2), pl.BlockSpec((cc, 1), lambda i: (i, 0)),
                  pl.BlockSpec((8, LANES), lambda i: (0, 0))],
        out_specs=[blk(0), blk(0), pl.BlockSpec((1, RET_HEADS, HP, RET_DV), lambda i: (i, 0, 0, 0))],
        out_shape=[jax.ShapeDtypeStruct((T, RW), F32), jax.ShapeDtypeStruct((T, RW), BF16),
                   jax.ShapeDtypeStruct((n, RET_HEADS, HP, RET_DV), F32)],
        scratch_shapes=[pltpu.VMEM((RET_HEADS, HP, RET_DV), F32)],
        compiler_params=_params(("arbitrary",)),
    )(proj, proj, proj, pos, tab)


def _ret_bwd(dyn, y, proj, pos, tab, rprev, *, name):
    T = proj.shape[0]
    cc = min(RET_TILE, T)
    n = T // cc

    def body(dyn_ref, y_ref, rq_ref, rk_ref, rv_ref, pos_ref, tab_ref, rprev_ref,
             drq_ref, drk_ref, drv_ref, dr_s):
        @pl.when(pl.program_id(0) == 0)
        def _():
            dr_s[...] = jnp.zeros_like(dr_s)

        cs = _rope_cs(pos_ref[...], tab_ref)
        for hd in range(RET_HEADS):
            sl = slice(hd * HP, (hd + 1) * HP)
            decay, zeta, xi, gc = _ret_consts(cc, hd)
            q = _rope(rq_ref[:, sl].astype(F32), cs, RET_DK // 2).astype(BF16)
            kf = _rope(rk_ref[:, sl].astype(F32), cs, RET_DK // 2) * (RET_DK ** -0.5)
            k = kf.astype(BF16)
            v = rv_ref[:, sl]
            yv = y_ref[:, sl]
            mu = jnp.mean(yv, axis=-1, keepdims=True)
            yc = yv - mu
            rs = lax.rsqrt(jnp.mean(yc * yc, axis=-1, keepdims=True) + GN_EPS)
            yn = yc * rs
            dn = dyn_ref[:, sl]
            dy = rs * (dn - jnp.mean(dn, axis=-1, keepdims=True) - yn * jnp.mean(dn * yn, axis=-1, keepdims=True))
            dyb = dy.astype(BF16)
            dyx = (dy * xi).astype(BF16)
            dr = dr_s[hd]
            drb = dr.astype(BF16)
            inner = (_dot_nt(q, k) * decay).astype(BF16)
            da = (_dot_nt(dyb, v) * decay).astype(BF16)
            dv = _dot_tn(inner, dyb) + _dot((kf * zeta).astype(BF16), drb)
            dq = _dot(da, k) + _dot_nt(dyx, rprev_ref[0, hd].astype(BF16))
            dk = _dot_tn(da, q) + _dot_nt(v, drb) * zeta
            dr_s[hd] = dr * gc + _dot_tn(q, dyx)
            drq_ref[:, sl] = _rope(dq, cs, RET_DK // 2, inverse=True).astype(BF16)
            drk_ref[:, sl] = _rope(dk * (RET_DK ** -0.5), cs, RET_DK // 2, inverse=True).astype(BF16)
            drv_ref[:, sl] = dv.astype(BF16)

    def blk(j):
        return pl.BlockSpec((cc, RW), lambda i: (n - 1 - i, j))

    return pl.pallas_call(
        body, name=name, grid=(n,),
        in_specs=[blk(0), blk(0), blk(0), blk(1), blk(2), pl.BlockSpec((cc, 1), lambda i: (n - 1 - i, 0)),
                  pl.BlockSpec((8, LANES), lambda i: (0, 0)),
                  pl.BlockSpec((1, RET_HEADS, HP, RET_DV), lambda i: (n - 1 - i, 0, 0, 0))],
        out_specs=[blk(0), blk(0), blk(0)],
        out_shape=[jax.ShapeDtypeStruct((T, RW), BF16)] * 3,
        scratch_shapes=[pltpu.VMEM((RET_HEADS, HP, RET_DV), F32)],
        compiler_params=_params(("arbitrary",)),
    )(dyn, y, proj, proj, proj, pos, tab, rprev)


def _merge_fwd(o, yn, proj, gn_w, w_bm, w_br, w_out, h, post_w, *, name):
    T, D = h.shape
    tT = min(MERGE_TILE, T)
    g_blk = PROJ_FIXED // D

    def body(o_ref, yn_ref, rg_ref, gm_ref, gr_ref, gnw_ref, wbm_ref, wbr_ref, wout_ref, h_ref, post_ref,
             omla_ref, oret_ref, m_ref, ho_ref):
        o_mla = _dot(o_ref[...], wbm_ref[...])
        rg = rg_ref[...].astype(F32)
        gated = (rg * _sigmoid(rg) * (yn_ref[...].astype(F32) * gnw_ref[...])).astype(BF16)
        o_ret = _dot(gated, wbr_ref[...])
        omla_ref[...] = o_mla.astype(BF16)
        oret_ref[...] = o_ret.astype(BF16)
        merged = _sigmoid(gm_ref[...].astype(F32)) * o_mla + _sigmoid(gr_ref[...].astype(F32)) * o_ret
        m = _dot(merged.astype(BF16), wout_ref[...])
        m_ref[...] = m
        ho_ref[...] = h_ref[...] + _rms_fwd(m, post_ref[...])

    def full(r, c):
        return pl.BlockSpec((r, c), lambda i: (0, 0))

    def rows(c, j=0):
        return pl.BlockSpec((tT, c), lambda i: (i, j))

    return pl.pallas_call(
        body, name=name, grid=(T // tT,),
        in_specs=[rows(QW), rows(RW), rows(RW, 3), rows(D, g_blk), rows(D, g_blk + 1), full(1, RW),
                  full(QW, D), full(RW, D), full(D, D), rows(D), full(1, D)],
        out_specs=[rows(D), rows(D), rows(D), rows(D)],
        out_shape=[jax.ShapeDtypeStruct((T, D), BF16), jax.ShapeDtypeStruct((T, D), BF16),
                   jax.ShapeDtypeStruct((T, D), F32), jax.ShapeDtypeStruct((T, D), F32)],
        compiler_params=_params(("parallel",)),
    )(o, yn, proj, proj, proj, gn_w, w_bm, w_br, w_out, h, post_w)


def _merge_bwd(dho, m, post_w, omla, oret, proj, yn, gn_w, o, w_out, w_bm, w_br, *, name):
    T, D = dho.shape
    tT = min(MERGE_TILE, T)
    g_blk = PROJ_FIXED // D

    def body(dho_ref, m_ref, post_ref, omla_ref, oret_ref, rg_ref, gm_ref, gr_ref, yn_ref, gnw_ref, o_ref,
             wout_ref, wbm_ref, wbr_ref,
             dm_ref, merged_ref, dgm_ref, dgr_ref, domla_ref, do_ref, delta_ref, doret_ref, gated_ref,
             drg_ref, dyn_ref, gpost_ref, ggn_ref):
        @pl.when(pl.program_id(0) == 0)
        def _():
            gpost_ref[...] = jnp.zeros_like(gpost_ref)
            ggn_ref[...] = jnp.zeros_like(ggn_ref)

        dm, gp = _rms_bwd(m_ref[...], post_ref[...], dho_ref[...])
        gpost_ref[...] += gp
        dmb = dm.astype(BF16)
        dm_ref[...] = dmb
        dmerged = _dot_nt(dmb, wout_ref[...])
        o_mla = omla_ref[...].astype(F32)
        o_ret = oret_ref[...].astype(F32)
        sgm = _sigmoid(gm_ref[...].astype(F32))
        sgr = _sigmoid(gr_ref[...].astype(F32))
        merged_ref[...] = (sgm * o_mla + sgr * o_ret).astype(BF16)
        dgm_ref[...] = (dmerged * o_mla * sgm * (1.0 - sgm)).astype(BF16)
        dgr_ref[...] = (dmerged * o_ret * sgr * (1.0 - sgr)).astype(BF16)
        domla = (dmerged * sgm).astype(BF16)
        domla_ref[...] = domla
        do = _dot_nt(domla, wbm_ref[...])
        do_ref[...] = do.astype(BF16)
        for hd in range(MLA_HEADS):
            sl = slice(hd * HP, (hd + 1) * HP)
            d = jnp.sum(do[:, sl] * o_ref[:, sl].astype(F32), axis=-1, keepdims=True)
            delta_ref[:, sl] = jnp.broadcast_to(d, (tT, HP))
        doret = (dmerged * sgr).astype(BF16)
        doret_ref[...] = doret
        dgated = _dot_nt(doret, wbr_ref[...])
        rg = rg_ref[...].astype(F32)
        sg = _sigmoid(rg)
        srg = rg * sg
        ynv = yn_ref[...].astype(F32)
        yw = ynv * gnw_ref[...]
        gated_ref[...] = (srg * yw).astype(BF16)
        drg_ref[...] = (dgated * yw * (sg * (1.0 + rg * (1.0 - sg)))).astype(BF16)
        dgs = dgated * srg
        dyn_ref[...] = dgs * gnw_ref[...]
        ggn_ref[...] += jnp.sum(dgs * ynv, axis=0, keepdims=True)

    def full(r, c):
        return pl.BlockSpec((r, c), lambda i: (0, 0))

    def rows(c, j=0):
        return pl.BlockSpec((tT, c), lambda i: (i, j))

    return pl.pallas_call(
        body, name=name, grid=(T // tT,),
        in_specs=[rows(D), rows(D), full(1, D), rows(D), rows(D), rows(RW, 3), rows(D, g_blk), rows(D, g_blk + 1),
                  rows(RW), full(1, RW), rows(QW), full(D, D), full(QW, D), full(RW, D)],
        out_specs=[rows(D), rows(D), rows(D), rows(D), rows(D), rows(QW), rows(QW), rows(D), rows(RW),
                   rows(RW), rows(RW), full(1, D), full(1, RW)],
        out_shape=[jax.ShapeDtypeStruct((T, D), BF16)] * 5
        + [jax.ShapeDtypeStruct((T, QW), BF16), jax.ShapeDtypeStruct((T, QW), F32),
           jax.ShapeDtypeStruct((T, D), BF16), jax.ShapeDtypeStruct((T, RW), BF16),
           jax.ShapeDtypeStruct((T, RW), BF16), jax.ShapeDtypeStruct((T, RW), F32),
           jax.ShapeDtypeStruct((1, D), F32), jax.ShapeDtypeStruct((1, RW), F32)],
        compiler_params=_params(("arbitrary",)),
    )(dho, m, post_w, omla, oret, proj, proj, proj, yn, gn_w, o, w_out, w_bm, w_br)


def _mesh_pos():
    return lax.axis_index("x"), lax.axis_index("y"), lax.axis_index("c")


class _Gather:
    def __init__(self, shards):
        self.operands = list(shards)
        self.n = len(shards)
        self.out_shape = [jax.ShapeDtypeStruct((N_DEV,) + s.shape, s.dtype) for s in shards]
        self.scratch = [pltpu.SemaphoreType.DMA((7 * self.n,)), pltpu.SemaphoreType.DMA((7 * self.n,)),
                        pltpu.SemaphoreType.DMA((self.n,))]

    def phase(self, p, x_refs, out_refs, sems):
        send_sems, recv_sems, local_sems = sems
        x, y, c = _mesh_pos()
        me, sibling = (x, y, c), (x, y, 1 - c)
        chips = [(1 - x, y), (x, 1 - y), (1 - x, 1 - y)]

        def copy(w, k, block, to, src=None):
            slot = out_refs[w].at[4 * block[0] + 2 * block[1] + block[2]]
            return pltpu.make_async_remote_copy(
                src_ref=slot if src is None else src, dst_ref=slot,
                send_sem=send_sems.at[7 * w + k], recv_sem=recv_sems.at[7 * w + k],
                device_id=to, device_id_type=pl.DeviceIdType.MESH)

        for w in range(self.n):
            mine = pltpu.make_async_copy(x_refs[w], out_refs[w].at[4 * x + 2 * y + c], local_sems.at[w])
            first = [copy(w, 0, me, sibling, src=x_refs[w])]
            first += [copy(w, 1 + j, me, (*chip, c), src=x_refs[w]) for j, chip in enumerate(chips)]
            passed = [copy(w, 4 + j, (*chip, c), sibling) for j, chip in enumerate(chips)]
            if p == 0:
                mine.start()
                for cp in first:
                    cp.start()
            elif p == 1:
                for j, chip in enumerate(chips):
                    copy(w, 1 + j, (*chip, c), me).wait_recv()
                    passed[j].start()
            else:
                copy(w, 0, sibling, me).wait_recv()
                for j, chip in enumerate(chips):
                    copy(w, 4 + j, (*chip, 1 - c), me).wait_recv()
                for cp in first + passed:
                    cp.wait_send()
                mine.wait()


class _Scatter:
    def __init__(self, grads, whole=()):
        self.n_sliced = len(grads)
        self.operands = list(grads) + list(whole)
        self.n = len(self.operands)
        self.out_shape = [jax.ShapeDtypeStruct(g.shape, g.dtype) for g in grads]
        self.out_shape += [jax.ShapeDtypeStruct((N_DEV,) + a.shape, a.dtype) for a in whole]
        n_sem = (N_DEV - 1) * self.n
        self.scratch = [pltpu.SemaphoreType.DMA((n_sem,)), pltpu.SemaphoreType.DMA((n_sem,)),
                        pltpu.SemaphoreType.DMA((self.n,))]

    def phase(self, p, in_refs, out_refs, sems):
        if p == 1:
            return
        send_sems, recv_sems, local_sems = sems
        x, y, c = _mesh_pos()
        me = 4 * x + 2 * y + c

        def src(w, dev):
            return in_refs[w].at[dev] if w < self.n_sliced else in_refs[w]

        for w in range(self.n):
            own = None if local_sems is None else pltpu.make_async_copy(src(w, me), out_refs[w].at[me], local_sems.at[w])
            sends, recvs = [], []
            for r in range(1, N_DEV):
                px = 1 - x if r & 4 else x
                py = 1 - y if r & 2 else y
                pc = 1 - c if r & 1 else c
                peer, pidx = (px, py, pc), 4 * px + 2 * py + pc
                k = (N_DEV - 1) * w + r - 1
                sends.append(pltpu.make_async_remote_copy(
                    src_ref=src(w, pidx), dst_ref=out_refs[w].at[me], send_sem=send_sems.at[k],
                    recv_sem=recv_sems.at[k], device_id=peer, device_id_type=pl.DeviceIdType.MESH))
                recvs.append(pltpu.make_async_remote_copy(
                    src_ref=src(w, me), dst_ref=out_refs[w].at[pidx], send_sem=send_sems.at[k],
                    recv_sem=recv_sems.at[k], device_id=peer, device_id_type=pl.DeviceIdType.MESH))
            if p == 0:
                if own is not None:
                    own.start()
                for cp in sends:
                    cp.start()
            else:
                for cp in recvs:
                    cp.wait_recv()
                for cp in sends:
                    cp.wait_send()
                if own is not None:
                    own.wait()


class _SplitScatter:
    def __init__(self, ex, name):
        self.ex, self.name = ex, name

    def _specs(self):
        ex = self.ex
        hbm = pl.BlockSpec(memory_space=pltpu.HBM)
        sem = pl.BlockSpec(memory_space=pltpu.SEMAPHORE)
        effect = pltpu.CompilerParams(has_side_effects=pltpu.SideEffectType.DATAFLOW_SIDE_EFFECTING)
        buffers = [pltpu.HBM(a.shape, a.dtype) for a in ex.operands] + [pltpu.HBM(s.shape, s.dtype) for s in ex.out_shape]
        return hbm, sem, effect, buffers

    def start(self):
        ex, n = self.ex, self.ex.n
        n_sem = (N_DEV - 1) * n
        hbm, sem, effect, buffers = self._specs()
        in_hbm = lambda a: pltpu.with_memory_space_constraint(a, pltpu.HBM)

        me = 4 * lax.axis_index("x") + 2 * lax.axis_index("y") + lax.axis_index("c")
        lands = []
        for w, (a, s) in enumerate(zip(ex.operands, ex.out_shape)):
            mine = lax.dynamic_index_in_dim(a, me, 0, keepdims=True) if w < ex.n_sliced else a[None]
            lands.append(lax.dynamic_update_slice_in_dim(jnp.zeros(s.shape, s.dtype), mine, me, 0))

        def start_body(*refs):
            ex.phase(0, refs[:n], refs[n:2 * n], (refs[2 * n], refs[2 * n + 1], None))
            refs[-1][...] = jnp.zeros_like(refs[-1])

        self.started = pl.pallas_call(
            start_body, name=self.name + "_start",
            out_shape=[pltpu.SemaphoreType.DMA((n_sem,)), pltpu.SemaphoreType.DMA((n_sem,))] + buffers
            + [jax.ShapeDtypeStruct((8, LANES), F32)],
            in_specs=[hbm] * (2 * n), out_specs=[sem, sem] + [hbm] * (2 * n) + [pl.BlockSpec(memory_space=pltpu.VMEM)],
            input_output_aliases={i: 2 + i for i in range(2 * n)}, compiler_params=effect,
        )(*[in_hbm(a) for a in ex.operands], *[in_hbm(a) for a in lands])
        return self.started[-1]

    def wait(self, after):
        ex, n = self.ex, self.ex.n
        hbm, sem, effect, buffers = self._specs()
        anyspec = pl.BlockSpec(memory_space=pl.ANY)

        def wait_body(*refs):
            ex.phase(2, refs[:n], refs[n:2 * n], (refs[2 * n], refs[2 * n + 1], None))

        done = pl.pallas_call(
            wait_body, name=self.name + "_wait", out_shape=buffers,
            in_specs=[hbm] * (2 * n) + [sem, sem] + [anyspec] * len(after), out_specs=[hbm] * (2 * n),
            input_output_aliases={i: i for i in range(2 * n)}, compiler_params=effect,
        )(*self.started[2:2 + 2 * n], self.started[0], self.started[1], *after)
        return done[n:]


def _exchange_alone(ex, *, name):
    n = ex.n

    def body(*refs):
        for p in range(3):
            ex.phase(p, refs[:n], refs[n:2 * n], refs[2 * n:])

    anyspec = pl.BlockSpec(memory_space=pl.ANY)
    return pl.pallas_call(body, name=name, out_shape=ex.out_shape, in_specs=[anyspec] * n,
                          out_specs=[anyspec] * n, scratch_shapes=ex.scratch)(*ex.operands)


def _adam_step(w_ref, p_ref, m_ref, v_ref, g_ref, d_ref, nm_ref, nv_ref):
    g = p_ref[0].astype(F32)
    for j in range(1, N_DEV):
        g = g + p_ref[j].astype(F32)
    g_ref[...] = g
    nm = ADAM_B1 * m_ref[...] + (1.0 - ADAM_B1) * g
    nv = ADAM_B2 * v_ref[...] + (1.0 - ADAM_B2) * (g * g)
    nm_ref[...] = nm
    nv_ref[...] = nv
    m_hat = nm / (1.0 - ADAM_B1 ** ADAM_STEP)
    v_hat = nv / (1.0 - ADAM_B2 ** ADAM_STEP)
    d_ref[...] = -ADAM_LR * (m_hat / (jnp.sqrt(v_hat) + ADAM_EPS) + ADAM_WD * w_ref[...])


def _adamw_vectors(ws, parts, ms, vs, *, name):
    n = len(ws)

    def body(*refs):
        w_refs, p_refs, m_refs, v_refs = (refs[i * n:(i + 1) * n] for i in range(4))
        outs = refs[4 * n:]
        for i in range(n):
            _adam_step(w_refs[i], p_refs[i], m_refs[i], v_refs[i], *outs[4 * i:4 * i + 4])

    return pl.pallas_call(
        body, name=name,
        out_shape=[jax.ShapeDtypeStruct(w.shape, F32) for w in ws for _ in range(4)],
    )(*ws, *parts, *ms, *vs)


def _adamw(w, parts, m, v, after, *, name):
    G, R, n = w.shape
    tn = 256 if (n > 256 and n % 256 == 0) else n
    tr = R
    for t in range(16, R, 16):
        if R % t == 0 and t * tn <= 160 * 1024:
            tr = t
    if R * tn <= 160 * 1024:
        tr = R

    def body(w_ref, p_ref, m_ref, v_ref, after_ref, g_ref, d_ref, nm_ref, nv_ref):
        _adam_step(w_ref, p_ref, m_ref, v_ref, g_ref, d_ref, nm_ref, nv_ref)

    blk = pl.BlockSpec((None, tr, tn), lambda g, i, j: (g, i, j))
    return pl.pallas_call(
        body, name=name, grid=(G, R // tr, n // tn),
        in_specs=[blk, pl.BlockSpec((N_DEV, None, tr, tn), lambda g, i, j: (0, g, i, j)), blk, blk,
                  pl.BlockSpec((8, LANES), lambda g, i, j: (0, 0))],
        out_specs=[blk, blk, blk, blk],
        out_shape=[jax.ShapeDtypeStruct((G, R, n), F32)] * 4,
        compiler_params=_params(("parallel", "parallel", "parallel")),
    )(w, parts, m, v, after)


def _pad_last(a, width):
    return jnp.pad(a, [(0, 0)] * (a.ndim - 1) + [(0, width - a.shape[-1])])


def _cols_of(g):
    return g.transpose(1, 0, 2).reshape(g.shape[1], N_DEV * g.shape[2])


def _col_shards(w):
    return w.reshape(w.shape[0], N_DEV, w.shape[1] // N_DEV).transpose(1, 0, 2)


def kernel(x, positions, ffn1_pre_w, ffn1_w1, ffn1_w2, ffn1_post_w, mix_pre_w, w_in, mla_q_norm_w, mla_w_uq, mla_kv_norm_w, mla_w_ukv, ret_gn_w, w_branch_mla, w_branch_ret, w_out, mix_post_w, ffn2_pre_w, ffn2_w1, ffn2_w2, ffn2_post_w, loss_target, m_ffn1_pre_w, m_ffn1_w1, m_ffn1_w2, m_ffn1_post_w, m_mix_pre_w, m_w_in, m_mla_q_norm_w, m_mla_w_uq, m_mla_kv_norm_w, m_mla_w_ukv, m_ret_gn_w, m_w_branch_mla, m_w_branch_ret, m_w_out, m_mix_post_w, m_ffn2_pre_w, m_ffn2_w1, m_ffn2_w2, m_ffn2_post_w, v_ffn1_pre_w, v_ffn1_w1, v_ffn1_w2, v_ffn1_post_w, v_mix_pre_w, v_w_in, v_mla_q_norm_w, v_mla_w_uq, v_mla_kv_norm_w, v_mla_w_ukv, v_ret_gn_w, v_w_branch_mla, v_w_branch_ret, v_w_out, v_mix_post_w, v_ffn2_pre_w, v_ffn2_w1, v_ffn2_w2, v_ffn2_post_w):
    T, D = x.shape[1], x.shape[2]
    h0 = x[0]
    tgt = loss_target[0]
    pos = positions.reshape(T, 1).astype(F32)

    big = [("ffn1_w1", ffn1_w1, m_ffn1_w1, v_ffn1_w1), ("ffn1_w2", ffn1_w2, m_ffn1_w2, v_ffn1_w2),
           ("w_in", w_in, m_w_in, v_w_in), ("mla_w_uq", mla_w_uq, m_mla_w_uq, v_mla_w_uq),
           ("mla_w_ukv", mla_w_ukv, m_mla_w_ukv, v_mla_w_ukv),
           ("w_branch_mla", w_branch_mla, m_w_branch_mla, v_w_branch_mla),
           ("w_branch_ret", w_branch_ret, m_w_branch_ret, v_w_branch_ret),
           ("w_out", w_out, m_w_out, v_w_out),
           ("ffn2_w1", ffn2_w1, m_ffn2_w1, v_ffn2_w1), ("ffn2_w2", ffn2_w2, m_ffn2_w2, v_ffn2_w2)]
    small = [("ffn1_pre_w", ffn1_pre_w, m_ffn1_pre_w, v_ffn1_pre_w), ("ffn1_post_w", ffn1_post_w, m_ffn1_post_w, v_ffn1_post_w),
             ("mix_pre_w", mix_pre_w, m_mix_pre_w, v_mix_pre_w), ("mla_q_norm_w", mla_q_norm_w, m_mla_q_norm_w, v_mla_q_norm_w),
             ("mla_kv_norm_w", mla_kv_norm_w, m_mla_kv_norm_w, v_mla_kv_norm_w), ("ret_gn_w", ret_gn_w, m_ret_gn_w, v_ret_gn_w),
             ("mix_post_w", mix_post_w, m_mix_post_w, v_mix_post_w), ("ffn2_pre_w", ffn2_pre_w, m_ffn2_pre_w, v_ffn2_pre_w),
             ("ffn2_post_w", ffn2_post_w, m_ffn2_post_w, v_ffn2_post_w)]

    half = ffn1_w2.shape[1]
    hp = -(-half // LANES) * LANES

    def rows_view(w):
        return w[0].T

    def send_w1(w):
        return jnp.pad(rows_view(w).reshape(2, half, D), ((0, 0), (0, hp - half), (0, 0))).reshape(2 * hp, D).astype(BF16)

    def send_w2(w):
        return jnp.pad(w[0], ((0, hp - half), (0, 0))).astype(BF16)

    mixer = ["w_in", "mla_w_uq", "mla_w_ukv", "w_branch_mla", "w_branch_ret", "w_out"]
    uq_w = MLA_NOPE + MLA_ROPE
    mixer_send = [rows_view(w_in).astype(BF16), jnp.pad(rows_view(mla_w_uq), ((0, HP - uq_w), (0, 0))).astype(BF16),
                  mla_w_ukv[0].astype(BF16), w_branch_mla[0].astype(BF16), w_branch_ret[0].astype(BF16),
                  w_out[0].astype(BF16)]

    w1a, w2a = _exchange_alone(_Gather([send_w1(ffn1_w1), send_w2(ffn1_w2)]), name="gather_ffn1")
    w2a = w2a.reshape(N_DEV // 2, 2 * hp, D)
    u1, f1, h1, *got = _ffn_fwd(h0, ffn1_pre_w, w1a, w2a, ffn1_post_w, None, name="ffn1_fwd_gather_mixer",
                                exchange=_Gather(mixer_send))
    fw = dict(zip(mixer, got))

    wi = fw["w_in"].reshape(-1, D)
    cq_w, ckv_w, kr_w = wi[0:384], wi[384:640], wi[640:672]
    rq_w, rk_w = wi[672:928], wi[928:1184]
    rv_w, rg_w = wi[1184:1696], wi[1696:2208]
    gm_w, gr_w = wi[2208:2208 + D], wi[2208 + D:2208 + 2 * D]
    zer = lambda n: jnp.zeros((n, D), BF16)
    head_rows = lambda a, h: jnp.pad(a.reshape(h, -1, D), ((0, 0), (0, HP - a.shape[0] // h), (0, 0))).reshape(h * HP, D)
    w_in_p = jnp.concatenate([head_rows(rq_w, RET_HEADS), head_rows(rk_w, RET_HEADS), rv_w, rg_w,
                              cq_w, ckv_w, zer(MLA_NOPE), kr_w, zer(HP - MLA_NOPE - MLA_ROPE), zer(AW - 768),
                              gm_w, gr_w], axis=0)
    w_uq_p = fw["mla_w_uq"].reshape(QW, MLA_Q_RANK)
    ukv = fw["mla_w_ukv"].transpose(1, 0, 2)
    w_kv_p = jnp.concatenate([_pad_last(ukv[:, :, :MLA_NOPE], HP).reshape(MLA_KV_RANK, QW),
                              _pad_last(ukv[:, :, MLA_NOPE:], HP).reshape(MLA_KV_RANK, QW)], axis=1)
    w_bm_p = jnp.pad(_cols_of(fw["w_branch_mla"]).reshape(MLA_HEADS, MLA_V, D),
                     ((0, 0), (0, HP - MLA_V), (0, 0))).reshape(QW, D)
    w_br, w_o = _cols_of(fw["w_branch_ret"]), fw["w_out"].reshape(D, D)
    tab_mla = _rope_table(MLA_NOPE, MLA_ROPE // 2)
    tab_ret = _rope_table(0, RET_DK // 2)

    proj, a1 = _rms_matmul(h1, mix_pre_w, w_in_p, name="mixer_in_proj")
    q, k, v, qn, kvn = _mla_prep_fwd(proj, pos, mla_q_norm_w, mla_kv_norm_w, w_uq_p, w_kv_p, tab_mla, name="mla_prep_fwd")
    o, lse, w1b, w2b = _flash_fwd(q, k, v, name="mla_attn_fwd_gather_ffn2",
                                  exchange=_Gather([send_w1(ffn2_w1), send_w2(ffn2_w2)]))
    w2b = w2b.reshape(N_DEV // 2, 2 * hp, D)
    ypre, yn, rprev = _ret_fwd(proj, pos, tab_ret, name="retention_fwd")
    omla, oret, m, h2 = _merge_fwd(o, yn, proj, ret_gn_w, w_bm_p, w_br, w_o, h1, mix_post_w, name="merge_fwd")
    u2, f2, _, dy, lossp = _ffn_fwd(h2, ffn2_pre_w, w1b, w2b, ffn2_post_w, tgt, name="ffn2_fwd_loss")
    loss = lax.psum(jnp.sum(lossp[::8, 0]), ("x", "y", "c"))

    def grad(x, dy, tag, after=None):
        return _matmul_tn(x if x.ndim == 3 else x[None], dy if dy.ndim == 3 else dy[None], name=tag, after=after)

    g2, du2, df2, a2, dh2, gpost2, gpre2 = _ffn_bwd(dy, f2, ffn2_post_w, h2, ffn2_pre_w, u2, w2b, w1b, name="ffn2_bwd")
    dw1b, = grad(du2.reshape(N_DEV, T, 2 * hp), a2, "ffn2_dw1")
    dw2b = grad(g2, df2, "ffn2_dw2")[0].reshape(N_DEV, hp, D)
    (dmb, merged, dgm, dgr, domla, do, delta, doret, gated, drg, dyn, gpostm, ggn) = _merge_bwd(
        dh2, m, mix_post_w, omla, oret, proj, yn, ret_gn_w, o, w_o, w_bm_p, w_br, name="merge_bwd")
    dw_out = grad(merged, dmb, "dw_out")[0][0]
    dw_bm_p = grad(o, domla, "dw_branch_mla")[0][0]
    dw_br = grad(gated, doret, "dw_branch_ret")[0][0]
    sc_ffn2 = _SplitScatter(_Scatter([dw1b, dw2b]), "scatter_ffn2")
    dq, dk, dv = _flash_bwd(q, k, v, do, lse, delta, name="mla_attn_bwd", after=sc_ffn2.start())
    da, dql, dkvl, gqn, gkvn = _mla_prep_bwd(dq, dk, dv, proj, pos, mla_q_norm_w, mla_kv_norm_w, w_uq_p, w_kv_p, tab_mla, name="mla_prep_bwd")
    dw_uq_p = grad(dql, qn, "dw_uq")[0][0]
    dw_kv_p = grad(kvn, dkvl, "dw_ukv")[0][0]
    drq, drk, drv = _ret_bwd(dyn, ypre, proj, pos, tab_ret, rprev, name="retention_bwd")
    dproj = jnp.concatenate([drq, drk, drv, drg, da, dgm, dgr], axis=1)
    dw_in_p = grad(dproj, a1, "dw_in")[0][0]

    dw_uq = dw_uq_p.reshape(MLA_HEADS, HP, MLA_Q_RANK)[:, :uq_w]
    dkp = dw_kv_p[:, :QW].reshape(MLA_KV_RANK, MLA_HEADS, HP)[:, :, :MLA_NOPE]
    dvp = dw_kv_p[:, QW:].reshape(MLA_KV_RANK, MLA_HEADS, HP)[:, :, :MLA_V]
    dw_ukv = jnp.concatenate([dkp, dvp], axis=2).transpose(1, 0, 2)
    dw_bm = dw_bm_p.reshape(MLA_HEADS, HP, D)[:, :MLA_V].reshape(MLA_HEADS * MLA_V, D)
    small_mixer_grads = [dw_uq, dw_ukv, _col_shards(dw_bm), _col_shards(dw_br), dw_out.reshape(N_DEV, D // N_DEV, D)]
    sc_small = _SplitScatter(_Scatter(small_mixer_grads), "scatter_mixer_small")
    dh1, gmixpre = _proj_bwd(dproj, w_in_p, h1, mix_pre_w, dh2, name="mixer_in_bwd", after=sc_small.start())
    unhead = lambda a, h, wd: a.reshape(h, HP, D)[:, :wd].reshape(h * wd, D)
    c0 = 4 * RW
    dw_in = jnp.concatenate([
        dw_in_p[c0:c0 + 384], dw_in_p[c0 + 384:c0 + 640], dw_in_p[c0 + 640 + MLA_NOPE:c0 + 640 + MLA_NOPE + MLA_ROPE],
        unhead(dw_in_p[0:RW], RET_HEADS, RET_DK), unhead(dw_in_p[RW:2 * RW], RET_HEADS, RET_DK),
        dw_in_p[2 * RW:3 * RW], dw_in_p[3 * RW:4 * RW],
        dw_in_p[PROJ_FIXED:PROJ_FIXED + D], dw_in_p[PROJ_FIXED + D:PROJ_FIXED + 2 * D]], axis=0).reshape(N_DEV, -1, D)
    sc_w_in = _SplitScatter(_Scatter([dw_in]), "scatter_w_in")
    g1, du1, df1, a0, dx, gpost1, gpre1 = _ffn_bwd(
        dh1, f1, ffn1_post_w, h0, ffn1_pre_w, u1, w2a, w1a, name="ffn1_bwd", after=sc_w_in.start())
    dw2a = grad(g1, df1, "ffn1_dw2")[0].reshape(N_DEV, hp, D)
    sc_dw2a = _SplitScatter(_Scatter([dw2a]), "scatter_ffn1_dw2")
    dw1a, = grad(du1.reshape(N_DEV, T, 2 * hp), a0, "ffn1_dw1", after=sc_dw2a.start())

    small_g = {"ffn1_pre_w": gpre1, "ffn1_post_w": gpost1, "mix_pre_w": gmixpre, "mla_q_norm_w": gqn,
               "mla_kv_norm_w": gkvn, "ret_gn_w": ggn, "mix_post_w": gpostm, "ffn2_pre_w": gpre2, "ffn2_post_w": gpost2}
    sc_last = _SplitScatter(_Scatter([dw1a], whole=[small_g[nm] for nm, *_ in small]), "scatter_ffn1_dw1")
    token = sc_last.start()
    recv_ffn2 = sc_ffn2.wait([token])
    recv_mixer = sc_w_in.wait([token]) + sc_small.wait([token])
    recv_w2a, = sc_dw2a.wait([token])
    parts = dict(zip(mixer, recv_mixer))
    parts.update(ffn1_w2=recv_w2a, ffn2_w1=recv_ffn2[0], ffn2_w2=recv_ffn2[1])
    as_is = (lambda a: a, lambda p: p[:, None], lambda a: a)
    views = {nm: as_is for nm, *_ in big}
    for nm in ("ffn1_w1", "ffn2_w1"):
        views[nm] = (lambda a: rows_view(a).reshape(2, half, D), lambda p: p.reshape(N_DEV, 2, hp, D),
                     lambda a: a.reshape(2 * half, D).T[None])
    for nm in ("w_in", "mla_w_uq"):
        views[nm] = (lambda a: rows_view(a)[None], lambda p: p[:, None], lambda a: a[0].T[None])

    def update(nm, w, m_, v_, after):
        to_view, parts_view, back = views[nm]
        return [back(a) for a in _adamw(to_view(w), parts_view(parts[nm]), to_view(m_), to_view(v_), after,
                                        name="adamw_" + nm)]

    big_out = {nm: update(nm, w, m_, v_, token) for nm, w, m_, v_ in big if nm != "ffn1_w1"}
    recv_w1a, *small_parts = sc_last.wait([d[0] for d in big_out.values()])
    parts["ffn1_w1"] = recv_w1a
    big_out["ffn1_w1"] = update("ffn1_w1", ffn1_w1, m_ffn1_w1, v_ffn1_w1, jnp.zeros((8, LANES), F32))
    small_out = _adamw_vectors([w for _, w, _, _ in small], small_parts, [a for _, _, a, _ in small],
                               [a for _, _, _, a in small], name="adamw_replicated")

    order = ["ffn1_pre_w", "ffn1_w1", "ffn1_w2", "ffn1_post_w", "mix_pre_w", "w_in", "mla_q_norm_w", "mla_w_uq",
             "mla_kv_norm_w", "mla_w_ukv", "ret_gn_w", "w_branch_mla", "w_branch_ret", "w_out", "mix_post_w",
             "ffn2_pre_w", "ffn2_w1", "ffn2_w2", "ffn2_post_w"]
    outs = [loss, dx[None]]
    for i in range(4):
        both = {nm: big_out[nm][i] for nm in big_out}
        both.update({nm: small_out[4 * j + i] for j, (nm, *_) in enumerate(small)})
        outs += [both[nm] for nm in order]
    return tuple(outs)
```

```python
import math

import numpy as np
import jax
import jax.numpy as jnp
from jax import lax
from jax.experimental import pallas as pl
from jax.experimental.pallas import tpu as pltpu

F32, BF16 = jnp.float32, jnp.bfloat16

MLA_HEADS, MLA_NOPE, MLA_ROPE, MLA_V = 8, 64, 32, 64
MLA_Q_RANK, MLA_KV_RANK = 384, 256
RET_HEADS, RET_DK, RET_DV = 4, 64, 128
ROPE_BASE, NORM_EPS, GN_EPS = 10000.0, 1e-6, 1e-6
ADAM_LR, ADAM_B1, ADAM_B2, ADAM_EPS, ADAM_WD, ADAM_STEP = 0.001, 0.9, 0.999, 1e-08, 0.01, 10
ATTN_SCALE = 1.0 / math.sqrt(MLA_NOPE + MLA_ROPE)

N_DEV = 8
LANES = 128
HP = LANES
QW = MLA_HEADS * HP
RW = RET_HEADS * HP
AW = 1024
PROJ_FIXED = 4 * RW + AW
NEG = -1e30

TOKEN_TILE = 512
ATTN_TILE = 1024
ATTN_CHAINS = 2
FFN_CHAINS = 2
RET_TILE = 256
PROJ_TILE_CAP = 2560
GRAD_TILE_CAP = 1408
GRAD_TOKEN_TILE = 2048
MERGE_TILE = 256
VMEM_LIMIT = 56 * 1024 * 1024


def _tile(n, cap, mult=LANES):
    if n <= cap:
        return n
    best = None
    for t in range(mult, cap + 1, mult):
        if n % t == 0:
            best = t
    assert best is not None, (n, cap, mult)
    return best


def _params(sem):
    return pltpu.CompilerParams(dimension_semantics=sem, vmem_limit_bytes=VMEM_LIMIT)


def _dot(a, b):
    return lax.dot_general(a, b, (((1,), (0,)), ((), ())), preferred_element_type=F32)


def _dot_nt(a, b):
    return lax.dot_general(a, b, (((1,), (1,)), ((), ())), preferred_element_type=F32)


def _dot_tn(a, b):
    return lax.dot_general(a, b, (((0,), (0,)), ((), ())), preferred_element_type=F32)


def _sigmoid(x):
    return pl.reciprocal(1.0 + jnp.exp(-x), approx=True)


def _rms_fwd(x, w):
    r = lax.rsqrt(jnp.mean(x * x, axis=-1, keepdims=True) + NORM_EPS)
    return x * r * w


def _rms_bwd(x, w, dy):
    r = lax.rsqrt(jnp.mean(x * x, axis=-1, keepdims=True) + NORM_EPS)
    xh = x * r
    g = dy * w
    dx = r * (g - xh * jnp.mean(g * xh, axis=-1, keepdims=True))
    return dx, jnp.sum(dy * xh, axis=0, keepdims=True)


def _rope_table(first, half):
    inv = (np.float32(ROPE_BASE) ** (-(np.arange(half, dtype=np.float32) / np.float32(half)))).astype(np.float32)
    tab = np.zeros((8, LANES), np.float32)
    tab[0, first:first + half] = inv
    tab[0, first + half:first + 2 * half] = inv
    tab[1, first:first + half] = -1.0
    tab[2, first + half:first + 2 * half] = 1.0
    return jnp.asarray(tab)


def _rope_cs(pos, tab_ref):
    ang = pos * tab_ref[0:1, :]
    s = jnp.sin(ang)
    return jnp.cos(ang), s * tab_ref[1:2, :], s * tab_ref[2:3, :]


def _rope(x, cs, half, inverse=False):
    c, s1, s2 = cs
    a = pltpu.roll(x, LANES - half, 1) * s1 + pltpu.roll(x, half, 1) * s2
    return x * c - a if inverse else x * c + a


def _call(body, *, name, grid, in_specs, out_specs, out_shape, scratch_shapes, args, exchange=None, after=None):
    sem = ("arbitrary",) * len(grid)
    anyspec = pl.BlockSpec(memory_space=pl.ANY)
    if exchange is None and after is not None:
        n_own = len(in_specs)

        def behind(*refs):
            body(*refs[:n_own], *refs[n_own + 1:])

        return pl.pallas_call(behind, name=name, grid=grid, in_specs=list(in_specs) + [anyspec], out_specs=out_specs,
                              out_shape=out_shape, scratch_shapes=scratch_shapes, compiler_params=_params(sem))(*args, after)
    if exchange is None:
        return pl.pallas_call(body, name=name, grid=grid, in_specs=in_specs, out_specs=out_specs,
                              out_shape=out_shape, scratch_shapes=scratch_shapes, compiler_params=_params(sem))(*args)
    n_in, n_out, e = len(in_specs), len(out_specs), exchange.n
    total = math.prod(grid)

    def carried(*refs):
        own = refs[:n_in] + refs[n_in + e:n_in + e + n_out] + refs[n_in + 2 * e + n_out:len(refs) - 3]
        ex_refs = (refs[n_in:n_in + e], refs[n_in + e + n_out:n_in + 2 * e + n_out], refs[len(refs) - 3:])
        step = pl.program_id(0)
        for d in range(1, len(grid)):
            step = step * grid[d] + pl.program_id(d)

        @pl.when(step == 0)
        def _():
            exchange.phase(0, *ex_refs)

        @pl.when(step == (3 * total) // 4)
        def _():
            exchange.phase(1, *ex_refs)

        body(*own)

        @pl.when(step == total - 1)
        def _():
            exchange.phase(2, *ex_refs)

    return pl.pallas_call(
        carried, name=name, grid=grid, in_specs=list(in_specs) + [anyspec] * e,
        out_specs=list(out_specs) + [anyspec] * e, out_shape=list(out_shape) + exchange.out_shape,
        scratch_shapes=list(scratch_shapes) + exchange.scratch, compiler_params=_params(sem),
    )(*args, *exchange.operands)


def _ffn_fwd(h, pre_w, w1, w2, post_w, target, *, name, exchange=None):
    T, D = h.shape
    nk, ck = w2.shape[0], w2.shape[1]
    tT = min(TOKEN_TILE, T)
    nT = T // tT
    with_loss = target is not None

    def body(*refs):
        if with_loss:
            (h_ref, pre_ref, w1g_ref, w1u_ref, w2_ref, post_ref, tgt_ref,
             u_ref, f_ref, ho_ref, dy_ref, loss_ref, a_s, acc) = refs
        else:
            (h_ref, pre_ref, w1g_ref, w1u_ref, w2_ref, post_ref,
             u_ref, f_ref, ho_ref, a_s, acc) = refs
        k = pl.program_id(1)

        @pl.when(k == 0)
        def _():
            a_s[...] = _rms_fwd(h_ref[...], pre_ref[...]).astype(BF16)
            acc[...] = jnp.zeros_like(acc)

        for c in range(FFN_CHAINS):
            rs = slice(c * (tT // FFN_CHAINS), (c + 1) * (tT // FFN_CHAINS))
            a = a_s[rs, :]
            ug = _dot_nt(a, w1g_ref[...])
            uu = _dot_nt(a, w1u_ref[...])
            u_ref[0, rs, :] = ug.astype(BF16)
            u_ref[1, rs, :] = uu.astype(BF16)
            acc[rs, :] += _dot((ug * _sigmoid(ug) * uu).astype(BF16), w2_ref[...])

        @pl.when(k == nk - 1)
        def _():
            f = acc[...]
            f_ref[...] = f
            ho = h_ref[...] + 0.5 * _rms_fwd(f, post_ref[...])
            ho_ref[...] = ho
            if with_loss:
                e = ho - tgt_ref[...]
                dy_ref[...] = e * (1.0 / D)
                loss_ref[...] = jnp.full(loss_ref.shape, (0.5 / D) * jnp.sum(e * e), F32)

    row = pl.BlockSpec((tT, D), lambda i, k: (i, 0))
    vec = pl.BlockSpec((1, D), lambda i, k: (0, 0))
    in_specs = [row, vec,
                pl.BlockSpec((None, ck, D), lambda i, k: (k, 0, 0)),
                pl.BlockSpec((None, ck, D), lambda i, k: (nk + k, 0, 0)),
                pl.BlockSpec((None, ck, D), lambda i, k: (k, 0, 0)),
                vec]
    out_shape = [jax.ShapeDtypeStruct((2, nk, T, ck), BF16),
                 jax.ShapeDtypeStruct((T, D), F32),
                 jax.ShapeDtypeStruct((T, D), F32)]
    out_specs = [pl.BlockSpec((2, None, tT, ck), lambda i, k: (0, k, i, 0)), row, row]
    args = [h, pre_w, w1, w1, w2, post_w]
    if with_loss:
        in_specs.append(row)
        args.append(target)
        out_shape += [jax.ShapeDtypeStruct((T, D), F32), jax.ShapeDtypeStruct((nT * 8, LANES), F32)]
        out_specs += [row, pl.BlockSpec((8, LANES), lambda i, k: (i, 0))]
    return _call(body, name=name, grid=(nT, nk), in_specs=in_specs, out_specs=out_specs, out_shape=out_shape,
                 scratch_shapes=[pltpu.VMEM((tT, D), BF16), pltpu.VMEM((tT, D), F32)], args=args, exchange=exchange)


def _ffn_bwd(dho, f, post_w, h, pre_w, u, w2, w1, *, name, exchange=None, after=None):
    T, D = h.shape
    nk, ck = w2.shape[0], w2.shape[1]
    tT = min(TOKEN_TILE, T)
    nT = T // tT

    def body(dho_ref, f_ref, post_ref, h_ref, pre_ref, u_ref, w2_ref, w1g_ref, w1u_ref,
             g_ref, du_ref, df_ref, a_ref, dh_ref, gpost_ref, gpre_ref, df_s, da_acc):
        i, k = pl.program_id(0), pl.program_id(1)

        @pl.when(jnp.logical_and(i == 0, k == 0))
        def _():
            gpost_ref[...] = jnp.zeros_like(gpost_ref)
            gpre_ref[...] = jnp.zeros_like(gpre_ref)

        @pl.when(k == 0)
        def _():
            dx, dw = _rms_bwd(f_ref[...], post_ref[...], 0.5 * dho_ref[...])
            dfb = dx.astype(BF16)
            df_s[...] = dfb
            df_ref[...] = dfb
            gpost_ref[...] += dw
            a_ref[...] = _rms_fwd(h_ref[...], pre_ref[...]).astype(BF16)
            da_acc[...] = jnp.zeros_like(da_acc)

        groups = [slice(c * (tT // FFN_CHAINS), (c + 1) * (tT // FFN_CHAINS)) for c in range(FFN_CHAINS)]
        dgs = [_dot_nt(df_s[rs, :], w2_ref[...]) for rs in groups]
        for rs, dg in zip(groups, dgs):
            ug = u_ref[0, rs, :].astype(F32)
            uu = u_ref[1, rs, :].astype(F32)
            sg = _sigmoid(ug)
            sl = ug * sg
            g_ref[rs, :] = (sl * uu).astype(BF16)
            dug = (dg * uu * (sg + sl * (1.0 - sg))).astype(BF16)
            duu = (dg * sl).astype(BF16)
            du_ref[0, rs, :] = dug
            du_ref[1, rs, :] = duu
            da_acc[rs, :] += _dot(dug, w1g_ref[...]) + _dot(duu, w1u_ref[...])

        @pl.when(k == nk - 1)
        def _():
            dx, dw = _rms_bwd(h_ref[...], pre_ref[...], da_acc[...])
            dh_ref[...] = dho_ref[...] + dx
            gpre_ref[...] += dw

    row = pl.BlockSpec((tT, D), lambda i, k: (i, 0))
    vec = pl.BlockSpec((1, D), lambda i, k: (0, 0))
    return _call(
        body, name=name, grid=(nT, nk),
        in_specs=[row, row, vec, row, vec,
                  pl.BlockSpec((2, None, tT, ck), lambda i, k: (0, k, i, 0)),
                  pl.BlockSpec((None, ck, D), lambda i, k: (k, 0, 0)),
                  pl.BlockSpec((None, ck, D), lambda i, k: (k, 0, 0)),
                  pl.BlockSpec((None, ck, D), lambda i, k: (nk + k, 0, 0))],
        out_specs=[pl.BlockSpec((None, tT, ck), lambda i, k: (k, i, 0)),
                   pl.BlockSpec((2, None, tT, ck), lambda i, k: (0, k, i, 0)),
                   row, row, row, vec, vec],
        out_shape=[jax.ShapeDtypeStruct((nk, T, ck), BF16),
                   jax.ShapeDtypeStruct((2, nk, T, ck), BF16),
                   jax.ShapeDtypeStruct((T, D), BF16),
                   jax.ShapeDtypeStruct((T, D), BF16),
                   jax.ShapeDtypeStruct((T, D), F32),
                   jax.ShapeDtypeStruct((1, D), F32),
                   jax.ShapeDtypeStruct((1, D), F32)],
        scratch_shapes=[pltpu.VMEM((tT, D), BF16), pltpu.VMEM((tT, D), F32)],
        args=(dho, f, post_w, h, pre_w, u, w2, w1, w1), exchange=exchange, after=after)


def _matmul_tn(x, dy, *, name, exchange=None, after=None):
    Px, T, K = x.shape
    Py, _, N = dy.shape
    P = max(Px, Py)
    tT, tK, tN = min(GRAD_TOKEN_TILE, T), _tile(K, GRAD_TILE_CAP), _tile(N, GRAD_TILE_CAP)
    nt = T // tT

    def body(x_ref, dy_ref, o_ref, acc):
        t = pl.program_id(3)

        @pl.when(t == 0)
        def _():
            acc[...] = jnp.zeros_like(acc)

        acc[...] += _dot_tn(x_ref[...], dy_ref[...])

        @pl.when(t == nt - 1)
        def _():
            o_ref[...] = acc[...].astype(BF16)

    return _call(
        body, name=name, grid=(P, K // tK, N // tN, nt),
        in_specs=[pl.BlockSpec((None, tT, tK), lambda p, a, b, t: (p if Px > 1 else 0, t, a)),
                  pl.BlockSpec((None, tT, tN), lambda p, a, b, t: (p if Py > 1 else 0, t, b))],
        out_specs=[pl.BlockSpec((None, tK, tN), lambda p, a, b, t: (p, a, b))],
        out_shape=[jax.ShapeDtypeStruct((P, K, N), BF16)],
        scratch_shapes=[pltpu.VMEM((tK, tN), F32)], args=(x, dy), exchange=exchange, after=after)


def _rms_matmul(h, wn, w, *, name):
    T, D = h.shape
    N = w.shape[0]
    tT, tN = min(TOKEN_TILE, T), _tile(N, PROJ_TILE_CAP)

    def body(h_ref, wn_ref, w_ref, y_ref, a_ref):
        @pl.when(pl.program_id(1) == 0)
        def _():
            a_ref[...] = _rms_fwd(h_ref[...], wn_ref[...]).astype(BF16)

        y_ref[...] = _dot_nt(a_ref[...], w_ref[...]).astype(BF16)

    return pl.pallas_call(
        body, name=name, grid=(T // tT, N // tN),
        in_specs=[pl.BlockSpec((tT, D), lambda i, j: (i, 0)),
                  pl.BlockSpec((1, D), lambda i, j: (0, 0)),
                  pl.BlockSpec((tN, D), lambda i, j: (j, 0))],
        out_specs=[pl.BlockSpec((tT, tN), lambda i, j: (i, j)),
                   pl.BlockSpec((tT, D), lambda i, j: (i, 0))],
        out_shape=[jax.ShapeDtypeStruct((T, N), BF16), jax.ShapeDtypeStruct((T, D), BF16)],
        compiler_params=_params(("parallel", "arbitrary")),
    )(h, wn, w)


def _proj_bwd(dproj, w, h, wn, dres, *, name, exchange=None, after=None):
    T, D = h.shape
    N = w.shape[0]
    tT, tN = min(TOKEN_TILE, T), _tile(N, PROJ_TILE_CAP)
    nn = N // tN

    def body(dp_ref, w_ref, h_ref, wn_ref, dres_ref, dh_ref, gw_ref, acc):
        i, j = pl.program_id(0), pl.program_id(1)

        @pl.when(jnp.logical_and(i == 0, j == 0))
        def _():
            gw_ref[...] = jnp.zeros_like(gw_ref)

        @pl.when(j == 0)
        def _():
            acc[...] = jnp.zeros_like(acc)

        acc[...] += _dot(dp_ref[...], w_ref[...])

        @pl.when(j == nn - 1)
        def _():
            dx, dw = _rms_bwd(h_ref[...], wn_ref[...], acc[...])
            dh_ref[...] = dres_ref[...] + dx
            gw_ref[...] += dw

    row = pl.BlockSpec((tT, D), lambda i, j: (i, 0))
    vec = pl.BlockSpec((1, D), lambda i, j: (0, 0))
    return _call(
        body, name=name, grid=(T // tT, nn),
        in_specs=[pl.BlockSpec((tT, tN), lambda i, j: (i, j)),
                  pl.BlockSpec((tN, D), lambda i, j: (j, 0)), row, vec, row],
        out_specs=[row, vec],
        out_shape=[jax.ShapeDtypeStruct((T, D), F32), jax.ShapeDtypeStruct((1, D), F32)],
        scratch_shapes=[pltpu.VMEM((tT, D), F32)], args=(dproj, w, h, wn, dres), exchange=exchange, after=after)


def _mla_prep_fwd(proj, pos, qn_w, kvn_w, w_uq, w_kv, tab, *, name):
    T = proj.shape[0]
    tT = min(TOKEN_TILE, T)
    a_blk = PROJ_FIXED // AW - 1

    def body(a_ref, pos_ref, qnw_ref, kvnw_ref, wuq_ref, wkv_ref, tab_ref,
             q_ref, k_ref, v_ref, qn_ref, kvn_ref):
        cq = a_ref[:, 0:MLA_Q_RANK].astype(F32)
        ckv = a_ref[:, MLA_Q_RANK:MLA_Q_RANK + MLA_KV_RANK].astype(F32)
        kr = a_ref[:, 640:768].astype(F32)
        qn = _rms_fwd(cq, qnw_ref[...]).astype(BF16)
        kvn = _rms_fwd(ckv, kvnw_ref[...]).astype(BF16)
        qn_ref[...] = qn
        kvn_ref[...] = kvn
        cs = _rope_cs(pos_ref[...], tab_ref)
        q = _dot_nt(qn, wuq_ref[...])
        kv = _dot(kvn, wkv_ref[...])
        krr = _rope(kr, cs, MLA_ROPE // 2)
        for hd in range(MLA_HEADS):
            sl = slice(hd * HP, (hd + 1) * HP)
            q_ref[:, sl] = (_rope(q[:, sl], cs, MLA_ROPE // 2) * ATTN_SCALE).astype(BF16)
            k_ref[:, sl] = (kv[:, sl] + krr).astype(BF16)
        v_ref[...] = kv[:, QW:].astype(BF16)

    def full(r, c):
        return pl.BlockSpec((r, c), lambda i: (0, 0))

    def rows(c):
        return pl.BlockSpec((tT, c), lambda i: (i, 0))

    return pl.pallas_call(
        body, name=name, grid=(T // tT,),
        in_specs=[pl.BlockSpec((tT, AW), lambda i: (i, a_blk)), rows(1),
                  full(1, MLA_Q_RANK), full(1, MLA_KV_RANK),
                  full(QW, MLA_Q_RANK), full(MLA_KV_RANK, 2 * QW), full(8, LANES)],
        out_specs=[rows(QW), rows(QW), rows(QW), rows(MLA_Q_RANK), rows(MLA_KV_RANK)],
        out_shape=[jax.ShapeDtypeStruct((T, QW), BF16)] * 3
        + [jax.ShapeDtypeStruct((T, MLA_Q_RANK), BF16), jax.ShapeDtypeStruct((T, MLA_KV_RANK), BF16)],
        compiler_params=_params(("parallel",)),
    )(proj, pos, qn_w, kvn_w, w_uq, w_kv, tab)


def _mla_prep_bwd(dq, dk, dv, proj, pos, qn_w, kvn_w, w_uq, w_kv, tab, *, name):
    T = proj.shape[0]
    tT = min(TOKEN_TILE, T)
    a_blk = PROJ_FIXED // AW - 1

    def body(dq_ref, dk_ref, dv_ref, a_ref, pos_ref, qnw_ref, kvnw_ref, wuq_ref, wkv_ref, tab_ref,
             da_ref, dql_ref, dkvl_ref, gqn_ref, gkvn_ref):
        @pl.when(pl.program_id(0) == 0)
        def _():
            gqn_ref[...] = jnp.zeros_like(gqn_ref)
            gkvn_ref[...] = jnp.zeros_like(gkvn_ref)

        cs = _rope_cs(pos_ref[...], tab_ref)
        dkr = jnp.zeros((tT, HP), F32)
        for hd in range(MLA_HEADS):
            sl = slice(hd * HP, (hd + 1) * HP)
            dql_ref[:, sl] = (_rope(dq_ref[:, sl], cs, MLA_ROPE // 2, inverse=True) * ATTN_SCALE).astype(BF16)
            dkh = dk_ref[:, sl]
            dkr = dkr + dkh
            dkvl_ref[:, sl] = dkh.astype(BF16)
        dkvl_ref[:, QW:] = dv_ref[...]
        dqn = _dot(dql_ref[...], wuq_ref[...])
        dkvn = _dot_nt(dkvl_ref[...], wkv_ref[...])
        cq = a_ref[:, 0:MLA_Q_RANK].astype(F32)
        ckv = a_ref[:, MLA_Q_RANK:MLA_Q_RANK + MLA_KV_RANK].astype(F32)
        dcq, gq = _rms_bwd(cq, qnw_ref[...], dqn)
        dckv, gkv = _rms_bwd(ckv, kvnw_ref[...], dkvn)
        gqn_ref[...] += gq
        gkvn_ref[...] += gkv
        da_ref[:, 0:MLA_Q_RANK] = dcq.astype(BF16)
        da_ref[:, MLA_Q_RANK:MLA_Q_RANK + MLA_KV_RANK] = dckv.astype(BF16)
        da_ref[:, 640:768] = _rope(dkr, cs, MLA_ROPE // 2, inverse=True).astype(BF16)
        da_ref[:, 768:AW] = jnp.zeros((tT, AW - 768), BF16)

    def full(r, c):
        return pl.BlockSpec((r, c), lambda i: (0, 0))

    def rows(c):
        return pl.BlockSpec((tT, c), lambda i: (i, 0))

    return pl.pallas_call(
        body, name=name, grid=(T // tT,),
        in_specs=[rows(QW), rows(QW), rows(QW), pl.BlockSpec((tT, AW), lambda i: (i, a_blk)), rows(1),
                  full(1, MLA_Q_RANK), full(1, MLA_KV_RANK),
                  full(QW, MLA_Q_RANK), full(MLA_KV_RANK, 2 * QW), full(8, LANES)],
        out_specs=[rows(AW), rows(QW), rows(2 * QW), full(1, MLA_Q_RANK), full(1, MLA_KV_RANK)],
        out_shape=[jax.ShapeDtypeStruct((T, AW), BF16), jax.ShapeDtypeStruct((T, QW), BF16),
                   jax.ShapeDtypeStruct((T, 2 * QW), BF16),
                   jax.ShapeDtypeStruct((1, MLA_Q_RANK), F32), jax.ShapeDtypeStruct((1, MLA_KV_RANK), F32)],
        compiler_params=_params(("arbitrary",)),
    )(dq, dk, dv, proj, pos, qn_w, kvn_w, w_uq, w_kv, tab)


def _flash_fwd(q, k, v, *, name, exchange=None):
    T = q.shape[0]
    H = q.shape[1] // HP
    tq = min(ATTN_TILE, T)
    nq = T // tq

    sub = tq // ATTN_CHAINS

    def body(q_ref, k_ref, v_ref, o_ref, lse_ref):
        qi = pl.program_id(1)
        qs = [q_ref[c * sub:(c + 1) * sub, :] for c in range(ATTN_CHAINS)]

        def update(carry, off, masked):
            nks = [(c + 1) * sub if masked else tq for c in range(ATTN_CHAINS)]
            scores = [_dot_nt(qs[c], k_ref[pl.ds(off, nks[c]), :]) for c in range(ATTN_CHAINS)]
            out = []
            for c in range(ATTN_CHAINS):
                m_prev, l_prev, acc = carry[c]
                nk, s = nks[c], scores[c]
                vb = v_ref[pl.ds(off, nk), :]
                if masked:
                    rows = lax.broadcasted_iota(jnp.int32, (sub, nk), 0) + c * sub
                    s = jnp.where(rows >= lax.broadcasted_iota(jnp.int32, (sub, nk), 1), s, NEG)
                m_new = jnp.maximum(m_prev, jnp.max(s, axis=1, keepdims=True))
                alpha = jnp.exp(m_prev - m_new)
                p = jnp.exp(s - m_new)
                out.append((m_new, alpha * l_prev + jnp.sum(p, axis=1, keepdims=True),
                            alpha * acc + _dot(p.astype(BF16), vb)))
            return tuple(out)

        init = tuple((jnp.full((sub, 1), NEG, F32), jnp.zeros((sub, 1), F32), jnp.zeros((sub, HP), F32))
                     for _ in range(ATTN_CHAINS))
        carry = lax.fori_loop(0, qi, lambda j, cr: update(cr, pl.multiple_of(j * tq, tq), False), init)
        carry = update(carry, pl.multiple_of(qi * tq, tq), True)
        for c in range(ATTN_CHAINS):
            m_fin, l_fin, acc = carry[c]
            o_ref[c * sub:(c + 1) * sub, :] = (acc / l_fin).astype(BF16)
            lse_ref[c * sub:(c + 1) * sub, :] = jnp.broadcast_to(m_fin + jnp.log(l_fin), (sub, HP))

    qspec = pl.BlockSpec((tq, HP), lambda h, i: (i, h))
    kspec = pl.BlockSpec((T, HP), lambda h, i: (0, h))
    return _call(
        body, name=name, grid=(H, nq),
        in_specs=[qspec, kspec, kspec], out_specs=[qspec, qspec],
        out_shape=[jax.ShapeDtypeStruct((T, H * HP), BF16), jax.ShapeDtypeStruct((T, H * HP), F32)],
        scratch_shapes=[], args=(q, k, v), exchange=exchange)


def _flash_bwd(q, k, v, do, lse, delta, *, name, exchange=None, after=None):
    T = q.shape[0]
    H = q.shape[1] // HP
    tq = min(ATTN_TILE, T)
    nq = T // tq
    sub = tq // ATTN_CHAINS

    def body(k_ref, v_ref, q_ref, do_ref, lse_ref, dl_ref, dq_ref, dk_ref, dv_ref):
        ki = pl.program_id(1)

        @pl.when(ki == 0)
        def _():
            dq_ref[...] = jnp.zeros_like(dq_ref)

        def grow(a):
            return a if a.shape[0] == tq else jnp.concatenate([a, jnp.zeros((tq - a.shape[0], HP), F32)], axis=0)

        def step(carry, j, masked):
            dk_acc, dv_acc = carry
            nks = [(c + 1) * sub if masked else tq for c in range(ATTN_CHAINS)]
            rws = [pl.ds(pl.multiple_of(j * tq + c * sub, sub), sub) for c in range(ATTN_CHAINS)]
            scores = [_dot_nt(q_ref[rws[c], :], k_ref[0:nks[c], :]) for c in range(ATTN_CHAINS)]
            dps = [_dot_nt(do_ref[rws[c], :], v_ref[0:nks[c], :]) for c in range(ATTN_CHAINS)]
            for c in range(ATTN_CHAINS):
                rows, nk, s, dp = rws[c], nks[c], scores[c], dps[c]
                kb = k_ref[0:nk, :]
                qb = q_ref[rows, :]
                dob = do_ref[rows, :]
                if masked:
                    ri = lax.broadcasted_iota(jnp.int32, (sub, nk), 0) + c * sub
                    s = jnp.where(ri >= lax.broadcasted_iota(jnp.int32, (sub, nk), 1), s, NEG)
                p = jnp.exp(s - lse_ref[rows, 0:1])
                dv_acc = dv_acc + grow(_dot_tn(p.astype(BF16), dob))
                ds = (p * (dp - dl_ref[rows, 0:1])).astype(BF16)
                dk_acc = dk_acc + grow(_dot_tn(ds, qb))
                dq_ref[rows, :] += _dot(ds, kb)
            return dk_acc, dv_acc

        carry = step((jnp.zeros((tq, HP), F32), jnp.zeros((tq, HP), F32)), ki, True)
        dk_acc, dv_acc = lax.fori_loop(ki + 1, nq, lambda j, cr: step(cr, j, False), carry)
        dk_ref[...] = dk_acc
        dv_ref[...] = dv_acc.astype(BF16)

    kspec = pl.BlockSpec((tq, HP), lambda h, j: (j, h))
    full = pl.BlockSpec((T, HP), lambda h, j: (0, h))
    return _call(
        body, name=name, grid=(H, nq),
        in_specs=[kspec, kspec, full, full, full, full], out_specs=[full, kspec, kspec],
        out_shape=[jax.ShapeDtypeStruct((T, H * HP), F32), jax.ShapeDtypeStruct((T, H * HP), F32),
                   jax.ShapeDtypeStruct((T, H * HP), BF16)],
        scratch_shapes=[], args=(k, v, q, do, lse, delta), exchange=exchange, after=after)


def _ret_consts(cc, hd):
    lg = math.log(1.0 - 2.0 ** (-5.0 - hd))
    diff = (lax.broadcasted_iota(jnp.int32, (cc, cc), 0) - lax.broadcasted_iota(jnp.int32, (cc, cc), 1)).astype(F32)
    decay = jnp.where(diff >= 0, jnp.exp(jnp.maximum(diff, 0.0) * lg), 0.0)
    idx = lax.broadcasted_iota(jnp.int32, (cc, 1), 0).astype(F32)
    zeta = jnp.exp((cc - 1.0 - idx) * lg)
    xi = jnp.exp((idx + 1.0) * lg)
    return decay, zeta, xi, math.exp(cc * lg)


def _ret_fwd(proj, pos, tab, *, name):
    T = proj.shape[0]
    cc = min(RET_TILE, T)
    n = T // cc

    def body(rq_ref, rk_ref, rv_ref, pos_ref, tab_ref, y_ref, yn_ref, rprev_ref, r_s):
        @pl.when(pl.program_id(0) == 0)
        def _():
            r_s[...] = jnp.zeros_like(r_s)

        cs = _rope_cs(pos_ref[...], tab_ref)
        for hd in range(RET_HEADS):
            sl = slice(hd * HP, (hd + 1) * HP)
            decay, zeta, xi, gc = _ret_consts(cc, hd)
            q = _rope(rq_ref[:, sl].astype(F32), cs, RET_DK // 2).astype(BF16)
            kf = _rope(rk_ref[:, sl].astype(F32), cs, RET_DK // 2) * (RET_DK ** -0.5)
            k = kf.astype(BF16)
            v = rv_ref[:, sl]
            r = r_s[hd]
            rprev_ref[0, hd] = r
            inner = (_dot_nt(q, k) * decay).astype(BF16)
            y = _dot(inner, v) + _dot(q, r.astype(BF16)) * xi
            r_s[hd] = r * gc + _dot_tn((kf * zeta).astype(BF16), v)
            y_ref[:, sl] = y
            mu = jnp.mean(y, axis=-1, keepdims=True)
            yc = y - mu
            var = jnp.mean(yc * yc, axis=-1, keepdims=True)
            yn_ref[:, sl] = (yc * lax.rsqrt(var + GN_EPS)).astype(BF16)

    def blk(j):
        return pl.BlockSpec((cc, RW), lambda i: (i, j))

    return pl.pallas_call(
        body, name=name, grid=(n,),
        in_specs=[blk(0), blk(1), blk(2), pl.BlockSpec((cc, 1), lambda i: (i, 0)),
                  pl.BlockSpec((8, LANES), lambda i: (0, 0))],
        out_specs=[blk(0), blk(0), pl.BlockSpec((1, RET_HEADS, HP, RET_DV), lambda i: (i, 0, 0, 0))],
        out_shape=[jax.ShapeDtypeStruct((T, RW), F32), jax.ShapeDtypeStruct((T, RW), BF16),
                   jax.ShapeDtypeStruct((n, RET_HEADS, HP, RET_DV), F32)],
        scratch_shapes=[pltpu.VMEM((RET_HEADS, HP, RET_DV), F32)],
        compiler_params=_params(("arbitrary",)),
    )(proj, proj, proj, pos, tab)


def _ret_bwd(dyn, y, proj, pos, tab, rprev, *, name):
    T = proj.shape[0]
    cc = min(RET_TILE, T)
    n = T // cc

    def body(dyn_ref, y_ref, rq_ref, rk_ref, rv_ref, pos_ref, tab_ref, rprev_ref,
             drq_ref, drk_ref, drv_ref, dr_s):
        @pl.when(pl.program_id(0) == 0)
        def _():
            dr_s[...] = jnp.zeros_like(dr_s)

        cs = _rope_cs(pos_ref[...], tab_ref)
        for hd in range(RET_HEADS):
            sl = slice(hd * HP, (hd + 1) * HP)
            decay, zeta, xi, gc = _ret_consts(cc, hd)
            q = _rope(rq_ref[:, sl].astype(F32), cs, RET_DK // 2).astype(BF16)
            kf = _rope(rk_ref[:, sl].astype(F32), cs, RET_DK // 2) * (RET_DK ** -0.5)
            k = kf.astype(BF16)
            v = rv_ref[:, sl]
            yv = y_ref[:, sl]
            mu = jnp.mean(yv, axis=-1, keepdims=True)
            yc = yv - mu
            rs = lax.rsqrt(jnp.mean(yc * yc, axis=-1, keepdims=True) + GN_EPS)
            yn = yc * rs
            dn = dyn_ref[:, sl]
            dy = rs * (dn - jnp.mean(dn, axis=-1, keepdims=True) - yn * jnp.mean(dn * yn, axis=-1, keepdims=True))
            dyb = dy.astype(BF16)
            dyx = (dy * xi).astype(BF16)
            dr = dr_s[hd]
            drb = dr.astype(BF16)
            inner = (_dot_nt(q, k) * decay).astype(BF16)
            da = (_dot_nt(dyb, v) * decay).astype(BF16)
            dv = _dot_tn(inner, dyb) + _dot((kf * zeta).astype(BF16), drb)
            dq = _dot(da, k) + _dot_nt(dyx, rprev_ref[0, hd].astype(BF16))
            dk = _dot_tn(da, q) + _dot_nt(v, drb) * zeta
            dr_s[hd] = dr * gc + _dot_tn(q, dyx)
            drq_ref[:, sl] = _rope(dq, cs, RET_DK // 2, inverse=True).astype(BF16)
            drk_ref[:, sl] = _rope(dk * (RET_DK ** -0.5), cs, RET_DK // 2, inverse=True).astype(BF16)
            drv_ref[:, sl] = dv.astype(BF16)

    def blk(j):
        return pl.BlockSpec((cc, RW), lambda i: (n - 1 - i, j))

    return pl.pallas_call(
        body, name=name, grid=(n,),
        in_specs=[blk(0), blk(0), blk(0), blk(1), blk(2), pl.BlockSpec((cc, 1), lambda i: (n - 1 - i, 0)),
                  pl.BlockSpec((8, LANES), lambda i: (0, 0)),
                  pl.BlockSpec((1, RET_HEADS, HP, RET_DV), lambda i: (n - 1 - i, 0, 0, 0))],
        out_specs=[blk(0), blk(0), blk(0)],
        out_shape=[jax.ShapeDtypeStruct((T, RW), BF16)] * 3,
        scratch_shapes=[pltpu.VMEM((RET_HEADS, HP, RET_DV), F32)],
        compiler_params=_params(("arbitrary",)),
    )(dyn, y, proj, proj, proj, pos, tab, rprev)


def _merge_fwd(o, yn, proj, gn_w, w_bm, w_br, w_out, h, post_w, *, name):
    T, D = h.shape
    tT = min(MERGE_TILE, T)
    g_blk = PROJ_FIXED // D

    def body(o_ref, yn_ref, rg_ref, gm_ref, gr_ref, gnw_ref, wbm_ref, wbr_ref, wout_ref, h_ref, post_ref,
             omla_ref, oret_ref, m_ref, ho_ref):
        o_mla = _dot(o_ref[...], wbm_ref[...])
        rg = rg_ref[...].astype(F32)
        gated = (rg * _sigmoid(rg) * (yn_ref[...].astype(F32) * gnw_ref[...])).astype(BF16)
        o_ret = _dot(gated, wbr_ref[...])
        omla_ref[...] = o_mla.astype(BF16)
        oret_ref[...] = o_ret.astype(BF16)
        merged = _sigmoid(gm_ref[...].astype(F32)) * o_mla + _sigmoid(gr_ref[...].astype(F32)) * o_ret
        m = _dot(merged.astype(BF16), wout_ref[...])
        m_ref[...] = m
        ho_ref[...] = h_ref[...] + _rms_fwd(m, post_ref[...])

    def full(r, c):
        return pl.BlockSpec((r, c), lambda i: (0, 0))

    def rows(c, j=0):
        return pl.BlockSpec((tT, c), lambda i: (i, j))

    return pl.pallas_call(
        body, name=name, grid=(T // tT,),
        in_specs=[rows(QW), rows(RW), rows(RW, 3), rows(D, g_blk), rows(D, g_blk + 1), full(1, RW),
                  full(QW, D), full(RW, D), full(D, D), rows(D), full(1, D)],
        out_specs=[rows(D), rows(D), rows(D), rows(D)],
        out_shape=[jax.ShapeDtypeStruct((T, D), BF16), jax.ShapeDtypeStruct((T, D), BF16),
                   jax.ShapeDtypeStruct((T, D), F32), jax.ShapeDtypeStruct((T, D), F32)],
        compiler_params=_params(("parallel",)),
    )(o, yn, proj, proj, proj, gn_w, w_bm, w_br, w_out, h, post_w)


def _merge_bwd(dho, m, post_w, omla, oret, proj, yn, gn_w, o, w_out, w_bm, w_br, *, name):
    T, D = dho.shape
    tT = min(MERGE_TILE, T)
    g_blk = PROJ_FIXED // D

    def body(dho_ref, m_ref, post_ref, omla_ref, oret_ref, rg_ref, gm_ref, gr_ref, yn_ref, gnw_ref, o_ref,
             wout_ref, wbm_ref, wbr_ref,
             dm_ref, merged_ref, dgm_ref, dgr_ref, domla_ref, do_ref, delta_ref, doret_ref, gated_ref,
             drg_ref, dyn_ref, gpost_ref, ggn_ref):
        @pl.when(pl.program_id(0) == 0)
        def _():
            gpost_ref[...] = jnp.zeros_like(gpost_ref)
            ggn_ref[...] = jnp.zeros_like(ggn_ref)

        dm, gp = _rms_bwd(m_ref[...], post_ref[...], dho_ref[...])
        gpost_ref[...] += gp
        dmb = dm.astype(BF16)
        dm_ref[...] = dmb
        dmerged = _dot_nt(dmb, wout_ref[...])
        o_mla = omla_ref[...].astype(F32)
        o_ret = oret_ref[...].astype(F32)
        sgm = _sigmoid(gm_ref[...].astype(F32))
        sgr = _sigmoid(gr_ref[...].astype(F32))
        merged_ref[...] = (sgm * o_mla + sgr * o_ret).astype(BF16)
        dgm_ref[...] = (dmerged * o_mla * sgm * (1.0 - sgm)).astype(BF16)
        dgr_ref[...] = (dmerged * o_ret * sgr * (1.0 - sgr)).astype(BF16)
        domla = (dmerged * sgm).astype(BF16)
        domla_ref[...] = domla
        do = _dot_nt(domla, wbm_ref[...])
        do_ref[...] = do.astype(BF16)
        for hd in range(MLA_HEADS):
            sl = slice(hd * HP, (hd + 1) * HP)
            d = jnp.sum(do[:, sl] * o_ref[:, sl].astype(F32), axis=-1, keepdims=True)
            delta_ref[:, sl] = jnp.broadcast_to(d, (tT, HP))
        doret = (dmerged * sgr).astype(BF16)
        doret_ref[...] = doret
        dgated = _dot_nt(doret, wbr_ref[...])
        rg = rg_ref[...].astype(F32)
        sg = _sigmoid(rg)
        srg = rg * sg
        ynv = yn_ref[...].astype(F32)
        yw = ynv * gnw_ref[...]
        gated_ref[...] = (srg * yw).astype(BF16)
        drg_ref[...] = (dgated * yw * (sg * (1.0 + rg * (1.0 - sg)))).astype(BF16)
        dgs = dgated * srg
        dyn_ref[...] = dgs * gnw_ref[...]
        ggn_ref[...] += jnp.sum(dgs * ynv, axis=0, keepdims=True)

    def full(r, c):
        return pl.BlockSpec((r, c), lambda i: (0, 0))

    def rows(c, j=0):
        return pl.BlockSpec((tT, c), lambda i: (i, j))

    return pl.pallas_call(
        body, name=name, grid=(T // tT,),
        in_specs=[rows(D), rows(D), full(1, D), rows(D), rows(D), rows(RW, 3), rows(D, g_blk), rows(D, g_blk + 1),
                  rows(RW), full(1, RW), rows(QW), full(D, D), full(QW, D), full(RW, D)],
        out_specs=[rows(D), rows(D), rows(D), rows(D), rows(D), rows(QW), rows(QW), rows(D), rows(RW),
                   rows(RW), rows(RW), full(1, D), full(1, RW)],
        out_shape=[jax.ShapeDtypeStruct((T, D), BF16)] * 5
        + [jax.ShapeDtypeStruct((T, QW), BF16), jax.ShapeDtypeStruct((T, QW), F32),
           jax.ShapeDtypeStruct((T, D), BF16), jax.ShapeDtypeStruct((T, RW), BF16),
           jax.ShapeDtypeStruct((T, RW), BF16), jax.ShapeDtypeStruct((T, RW), F32),
           jax.ShapeDtypeStruct((1, D), F32), jax.ShapeDtypeStruct((1, RW), F32)],
        compiler_params=_params(("arbitrary",)),
    )(dho, m, post_w, omla, oret, proj, proj, proj, yn, gn_w, o, w_out, w_bm, w_br)


def _mesh_pos():
    return lax.axis_index("x"), lax.axis_index("y"), lax.axis_index("c")


class _Gather:
    def __init__(self, shards):
        self.operands = list(shards)
        self.n = len(shards)
        self.out_shape = [jax.ShapeDtypeStruct((N_DEV,) + s.shape, s.dtype) for s in shards]
        self.scratch = [pltpu.SemaphoreType.DMA((7 * self.n,)), pltpu.SemaphoreType.DMA((7 * self.n,)),
                        pltpu.SemaphoreType.DMA((self.n,))]

    def phase(self, p, x_refs, out_refs, sems):
        send_sems, recv_sems, local_sems = sems
        x, y, c = _mesh_pos()
        me, sibling = (x, y, c), (x, y, 1 - c)
        chips = [(1 - x, y), (x, 1 - y), (1 - x, 1 - y)]

        def copy(w, k, block, to, src=None):
            slot = out_refs[w].at[4 * block[0] + 2 * block[1] + block[2]]
            return pltpu.make_async_remote_copy(
                src_ref=slot if src is None else src, dst_ref=slot,
                send_sem=send_sems.at[7 * w + k], recv_sem=recv_sems.at[7 * w + k],
                device_id=to, device_id_type=pl.DeviceIdType.MESH)

        for w in range(self.n):
            mine = pltpu.make_async_copy(x_refs[w], out_refs[w].at[4 * x + 2 * y + c], local_sems.at[w])
            first = [copy(w, 0, me, sibling, src=x_refs[w])]
            first += [copy(w, 1 + j, me, (*chip, c), src=x_refs[w]) for j, chip in enumerate(chips)]
            passed = [copy(w, 4 + j, (*chip, c), sibling) for j, chip in enumerate(chips)]
            if p == 0:
                mine.start()
                for cp in first:
                    cp.start()
            elif p == 1:
                for j, chip in enumerate(chips):
                    copy(w, 1 + j, (*chip, c), me).wait_recv()
                    passed[j].start()
            else:
                copy(w, 0, sibling, me).wait_recv()
                for j, chip in enumerate(chips):
                    copy(w, 4 + j, (*chip, 1 - c), me).wait_recv()
                for cp in first + passed:
                    cp.wait_send()
                mine.wait()


class _Scatter:
    def __init__(self, grads, whole=()):
        self.n_sliced = len(grads)
        self.operands = list(grads) + list(whole)
        self.n = len(self.operands)
        self.out_shape = [jax.ShapeDtypeStruct(g.shape, g.dtype) for g in grads]
        self.out_shape += [jax.ShapeDtypeStruct((N_DEV,) + a.shape, a.dtype) for a in whole]
        n_sem = (N_DEV - 1) * self.n
        self.scratch = [pltpu.SemaphoreType.DMA((n_sem,)), pltpu.SemaphoreType.DMA((n_sem,)),
                        pltpu.SemaphoreType.DMA((self.n,))]

    def phase(self, p, in_refs, out_refs, sems):
        if p == 1:
            return
        send_sems, recv_sems, local_sems = sems
        x, y, c = _mesh_pos()
        me = 4 * x + 2 * y + c

        def src(w, dev):
            return in_refs[w].at[dev] if w < self.n_sliced else in_refs[w]

        for w in range(self.n):
            own = None if local_sems is None else pltpu.make_async_copy(src(w, me), out_refs[w].at[me], local_sems.at[w])
            sends, recvs = [], []
            for r in range(1, N_DEV):
                px = 1 - x if r & 4 else x
                py = 1 - y if r & 2 else y
                pc = 1 - c if r & 1 else c
                peer, pidx = (px, py, pc), 4 * px + 2 * py + pc
                k = (N_DEV - 1) * w + r - 1
                sends.append(pltpu.make_async_remote_copy(
                    src_ref=src(w, pidx), dst_ref=out_refs[w].at[me], send_sem=send_sems.at[k],
                    recv_sem=recv_sems.at[k], device_id=peer, device_id_type=pl.DeviceIdType.MESH))
                recvs.append(pltpu.make_async_remote_copy(
                    src_ref=src(w, me), dst_ref=out_refs[w].at[pidx], send_sem=send_sems.at[k],
                    recv_sem=recv_sems.at[k], device_id=peer, device_id_type=pl.DeviceIdType.MESH))
            if p == 0:
                if own is not None:
                    own.start()
                for cp in sends:
                    cp.start()
            else:
                for cp in recvs:
                    cp.wait_recv()
                for cp in sends:
                    cp.wait_send()
                if own is not None:
                    own.wait()


class _SplitScatter:
    def __init__(self, ex, name):
        self.ex, self.name = ex, name

    def _specs(self):
        ex = self.ex
        hbm = pl.BlockSpec(memory_space=pltpu.HBM)
        sem = pl.BlockSpec(memory_space=pltpu.SEMAPHORE)
        effect = pltpu.CompilerParams(has_side_effects=pltpu.SideEffectType.DATAFLOW_SIDE_EFFECTING)
        buffers = [pltpu.HBM(a.shape, a.dtype) for a in ex.operands] + [pltpu.HBM(s.shape, s.dtype) for s in ex.out_shape]
        return hbm, sem, effect, buffers

    def start(self):
        ex, n = self.ex, self.ex.n
        n_sem = (N_DEV - 1) * n
        hbm, sem, effect, buffers = self._specs()
        in_hbm = lambda a: pltpu.with_memory_space_constraint(a, pltpu.HBM)

        me = 4 * lax.axis_index("x") + 2 * lax.axis_index("y") + lax.axis_index("c")
        lands = []
        for w, (a, s) in enumerate(zip(ex.operands, ex.out_shape)):
            mine = lax.dynamic_index_in_dim(a, me, 0, keepdims=True) if w < ex.n_sliced else a[None]
            lands.append(lax.dynamic_update_slice_in_dim(lax.empty(s.shape, s.dtype), mine, me, 0))

        def start_body(*refs):
            ex.phase(0, refs[:n], refs[n:2 * n], (refs[2 * n], refs[2 * n + 1], None))
            refs[-1][...] = jnp.zeros_like(refs[-1])

        self.started = pl.pallas_call(
            start_body, name=self.name + "_start",
            out_shape=[pltpu.SemaphoreType.DMA((n_sem,)), pltpu.SemaphoreType.DMA((n_sem,))] + buffers
            + [jax.ShapeDtypeStruct((8, LANES), F32)],
            in_specs=[hbm] * (2 * n), out_specs=[sem, sem] + [hbm] * (2 * n) + [pl.BlockSpec(memory_space=pltpu.VMEM)],
            input_output_aliases={i: 2 + i for i in range(2 * n)}, compiler_params=effect,
        )(*[in_hbm(a) for a in ex.operands], *[in_hbm(a) for a in lands])
        return self.started[-1]

    def wait(self, after):
        ex, n = self.ex, self.ex.n
        hbm, sem, effect, buffers = self._specs()
        anyspec = pl.BlockSpec(memory_space=pl.ANY)

        def wait_body(*refs):
            ex.phase(2, refs[:n], refs[n:2 * n], (refs[2 * n], refs[2 * n + 1], None))

        done = pl.pallas_call(
            wait_body, name=self.name + "_wait", out_shape=buffers,
            in_specs=[hbm] * (2 * n) + [sem, sem] + [anyspec] * len(after), out_specs=[hbm] * (2 * n),
            input_output_aliases={i: i for i in range(2 * n)}, compiler_params=effect,
        )(*self.started[2:2 + 2 * n], self.started[0], self.started[1], *after)
        return done[n:]


def _exchange_alone(ex, *, name):
    n = ex.n

    def body(*refs):
        for p in range(3):
            ex.phase(p, refs[:n], refs[n:2 * n], refs[2 * n:])

    anyspec = pl.BlockSpec(memory_space=pl.ANY)
    return pl.pallas_call(body, name=name, out_shape=ex.out_shape, in_specs=[anyspec] * n,
                          out_specs=[anyspec] * n, scratch_shapes=ex.scratch)(*ex.operands)


def _adam_step(w_ref, p_ref, m_ref, v_ref, g_ref, d_ref, nm_ref, nv_ref):
    g = p_ref[0].astype(F32)
    for j in range(1, N_DEV):
        g = g + p_ref[j].astype(F32)
    g_ref[...] = g
    nm = ADAM_B1 * m_ref[...] + (1.0 - ADAM_B1) * g
    nv = ADAM_B2 * v_ref[...] + (1.0 - ADAM_B2) * (g * g)
    nm_ref[...] = nm
    nv_ref[...] = nv
    m_hat = nm / (1.0 - ADAM_B1 ** ADAM_STEP)
    v_hat = nv / (1.0 - ADAM_B2 ** ADAM_STEP)
    d_ref[...] = -ADAM_LR * (m_hat / (jnp.sqrt(v_hat) + ADAM_EPS) + ADAM_WD * w_ref[...])


def _adamw_vectors(ws, parts, ms, vs, *, name):
    n = len(ws)

    def body(*refs):
        w_refs, p_refs, m_refs, v_refs = (refs[i * n:(i + 1) * n] for i in range(4))
        outs = refs[4 * n:]
        for i in range(n):
            _adam_step(w_refs[i], p_refs[i], m_refs[i], v_refs[i], *outs[4 * i:4 * i + 4])

    return pl.pallas_call(
        body, name=name,
        out_shape=[jax.ShapeDtypeStruct(w.shape, F32) for w in ws for _ in range(4)],
    )(*ws, *parts, *ms, *vs)


def _adamw(w, parts, m, v, after, *, name):
    G, R, n = w.shape
    tn = 256 if (n > 256 and n % 256 == 0) else n
    tr = R
    for t in range(16, R, 16):
        if R % t == 0 and t * tn <= 160 * 1024:
            tr = t
    if R * tn <= 160 * 1024:
        tr = R

    def body(w_ref, p_ref, m_ref, v_ref, after_ref, g_ref, d_ref, nm_ref, nv_ref):
        _adam_step(w_ref, p_ref, m_ref, v_ref, g_ref, d_ref, nm_ref, nv_ref)

    blk = pl.BlockSpec((None, tr, tn), lambda g, i, j: (g, i, j))
    return pl.pallas_call(
        body, name=name, grid=(G, R // tr, n // tn),
        in_specs=[blk, pl.BlockSpec((N_DEV, None, tr, tn), lambda g, i, j: (0, g, i, j)), blk, blk,
                  pl.BlockSpec((8, LANES), lambda g, i, j: (0, 0))],
        out_specs=[blk, blk, blk, blk],
        out_shape=[jax.ShapeDtypeStruct((G, R, n), F32)] * 4,
        compiler_params=_params(("parallel", "parallel", "parallel")),
    )(w, parts, m, v, after)


def _pad_last(a, width):
    return jnp.pad(a, [(0, 0)] * (a.ndim - 1) + [(0, width - a.shape[-1])])


def _cols_of(g):
    return g.transpose(1, 0, 2).reshape(g.shape[1], N_DEV * g.shape[2])


def _col_shards(w):
    return w.reshape(w.shape[0], N_DEV, w.shape[1] // N_DEV).transpose(1, 0, 2)


def kernel(x, positions, ffn1_pre_w, ffn1_w1, ffn1_w2, ffn1_post_w, mix_pre_w, w_in, mla_q_norm_w, mla_w_uq, mla_kv_norm_w, mla_w_ukv, ret_gn_w, w_branch_mla, w_branch_ret, w_out, mix_post_w, ffn2_pre_w, ffn2_w1, ffn2_w2, ffn2_post_w, loss_target, m_ffn1_pre_w, m_ffn1_w1, m_ffn1_w2, m_ffn1_post_w, m_mix_pre_w, m_w_in, m_mla_q_norm_w, m_mla_w_uq, m_mla_kv_norm_w, m_mla_w_ukv, m_ret_gn_w, m_w_branch_mla, m_w_branch_ret, m_w_out, m_mix_post_w, m_ffn2_pre_w, m_ffn2_w1, m_ffn2_w2, m_ffn2_post_w, v_ffn1_pre_w, v_ffn1_w1, v_ffn1_w2, v_ffn1_post_w, v_mix_pre_w, v_w_in, v_mla_q_norm_w, v_mla_w_uq, v_mla_kv_norm_w, v_mla_w_ukv, v_ret_gn_w, v_w_branch_mla, v_w_branch_ret, v_w_out, v_mix_post_w, v_ffn2_pre_w, v_ffn2_w1, v_ffn2_w2, v_ffn2_post_w):
    T, D = x.shape[1], x.shape[2]
    h0 = x[0]
    tgt = loss_target[0]
    pos = positions.reshape(T, 1).astype(F32)

    big = [("ffn1_w1", ffn1_w1, m_ffn1_w1, v_ffn1_w1), ("ffn1_w2", ffn1_w2, m_ffn1_w2, v_ffn1_w2),
           ("w_in", w_in, m_w_in, v_w_in), ("mla_w_uq", mla_w_uq, m_mla_w_uq, v_mla_w_uq),
           ("mla_w_ukv", mla_w_ukv, m_mla_w_ukv, v_mla_w_ukv),
           ("w_branch_mla", w_branch_mla, m_w_branch_mla, v_w_branch_mla),
           ("w_branch_ret", w_branch_ret, m_w_branch_ret, v_w_branch_ret),
           ("w_out", w_out, m_w_out, v_w_out),
           ("ffn2_w1", ffn2_w1, m_ffn2_w1, v_ffn2_w1), ("ffn2_w2", ffn2_w2, m_ffn2_w2, v_ffn2_w2)]
    small = [("ffn1_pre_w", ffn1_pre_w, m_ffn1_pre_w, v_ffn1_pre_w), ("ffn1_post_w", ffn1_post_w, m_ffn1_post_w, v_ffn1_post_w),
             ("mix_pre_w", mix_pre_w, m_mix_pre_w, v_mix_pre_w), ("mla_q_norm_w", mla_q_norm_w, m_mla_q_norm_w, v_mla_q_norm_w),
             ("mla_kv_norm_w", mla_kv_norm_w, m_mla_kv_norm_w, v_mla_kv_norm_w), ("ret_gn_w", ret_gn_w, m_ret_gn_w, v_ret_gn_w),
             ("mix_post_w", mix_post_w, m_mix_post_w, v_mix_post_w), ("ffn2_pre_w", ffn2_pre_w, m_ffn2_pre_w, v_ffn2_pre_w),
             ("ffn2_post_w", ffn2_post_w, m_ffn2_post_w, v_ffn2_post_w)]

    half = ffn1_w2.shape[1]
    hp = -(-half // LANES) * LANES

    def rows_view(w):
        return w[0].T

    def send_w1(w):
        return jnp.pad(rows_view(w).reshape(2, half, D), ((0, 0), (0, hp - half), (0, 0))).reshape(2 * hp, D).astype(BF16)

    def send_w2(w):
        return jnp.pad(w[0], ((0, hp - half), (0, 0))).astype(BF16)

    mixer = ["w_in", "mla_w_uq", "mla_w_ukv", "w_branch_mla", "w_branch_ret", "w_out"]
    uq_w = MLA_NOPE + MLA_ROPE
    mixer_send = [rows_view(w_in).astype(BF16), jnp.pad(rows_view(mla_w_uq), ((0, HP - uq_w), (0, 0))).astype(BF16),
                  mla_w_ukv[0].astype(BF16), w_branch_mla[0].astype(BF16), w_branch_ret[0].astype(BF16),
                  w_out[0].astype(BF16)]

    w1a, w2a = _exchange_alone(_Gather([send_w1(ffn1_w1), send_w2(ffn1_w2)]), name="gather_ffn1")
    w2a = w2a.reshape(N_DEV // 2, 2 * hp, D)
    u1, f1, h1, *got = _ffn_fwd(h0, ffn1_pre_w, w1a, w2a, ffn1_post_w, None, name="ffn1_fwd_gather_mixer",
                                exchange=_Gather(mixer_send))
    fw = dict(zip(mixer, got))

    wi = fw["w_in"].reshape(-1, D)
    cq_w, ckv_w, kr_w = wi[0:384], wi[384:640], wi[640:672]
    rq_w, rk_w = wi[672:928], wi[928:1184]
    rv_w, rg_w = wi[1184:1696], wi[1696:2208]
    gm_w, gr_w = wi[2208:2208 + D], wi[2208 + D:2208 + 2 * D]
    zer = lambda n: jnp.zeros((n, D), BF16)
    head_rows = lambda a, h: jnp.pad(a.reshape(h, -1, D), ((0, 0), (0, HP - a.shape[0] // h), (0, 0))).reshape(h * HP, D)
    w_in_p = jnp.concatenate([head_rows(rq_w, RET_HEADS), head_rows(rk_w, RET_HEADS), rv_w, rg_w,
                              cq_w, ckv_w, zer(MLA_NOPE), kr_w, zer(HP - MLA_NOPE - MLA_ROPE), zer(AW - 768),
                              gm_w, gr_w], axis=0)
    w_uq_p = fw["mla_w_uq"].reshape(QW, MLA_Q_RANK)
    ukv = fw["mla_w_ukv"].transpose(1, 0, 2)
    w_kv_p = jnp.concatenate([_pad_last(ukv[:, :, :MLA_NOPE], HP).reshape(MLA_KV_RANK, QW),
                              _pad_last(ukv[:, :, MLA_NOPE:], HP).reshape(MLA_KV_RANK, QW)], axis=1)
    w_bm_p = jnp.pad(_cols_of(fw["w_branch_mla"]).reshape(MLA_HEADS, MLA_V, D),
                     ((0, 0), (0, HP - MLA_V), (0, 0))).reshape(QW, D)
    w_br, w_o = _cols_of(fw["w_branch_ret"]), fw["w_out"].reshape(D, D)
    tab_mla = _rope_table(MLA_NOPE, MLA_ROPE // 2)
    tab_ret = _rope_table(0, RET_DK // 2)

    proj, a1 = _rms_matmul(h1, mix_pre_w, w_in_p, name="mixer_in_proj")
    q, k, v, qn, kvn = _mla_prep_fwd(proj, pos, mla_q_norm_w, mla_kv_norm_w, w_uq_p, w_kv_p, tab_mla, name="mla_prep_fwd")
    o, lse, w1b, w2b = _flash_fwd(q, k, v, name="mla_attn_fwd_gather_ffn2",
                                  exchange=_Gather([send_w1(ffn2_w1), send_w2(ffn2_w2)]))
    w2b = w2b.reshape(N_DEV // 2, 2 * hp, D)
    ypre, yn, rprev = _ret_fwd(proj, pos, tab_ret, name="retention_fwd")
    omla, oret, m, h2 = _merge_fwd(o, yn, proj, ret_gn_w, w_bm_p, w_br, w_o, h1, mix_post_w, name="merge_fwd")
    u2, f2, _, dy, lossp = _ffn_fwd(h2, ffn2_pre_w, w1b, w2b, ffn2_post_w, tgt, name="ffn2_fwd_loss")
    loss = lax.psum(jnp.sum(lossp[::8, 0]), ("x", "y", "c"))

    def grad(x, dy, tag, after=None):
        return _matmul_tn(x if x.ndim == 3 else x[None], dy if dy.ndim == 3 else dy[None], name=tag, after=after)

    g2, du2, df2, a2, dh2, gpost2, gpre2 = _ffn_bwd(dy, f2, ffn2_post_w, h2, ffn2_pre_w, u2, w2b, w1b, name="ffn2_bwd")
    dw1b, = grad(du2.reshape(N_DEV, T, 2 * hp), a2, "ffn2_dw1")
    dw2b = grad(g2, df2, "ffn2_dw2")[0].reshape(N_DEV, hp, D)
    (dmb, merged, dgm, dgr, domla, do, delta, doret, gated, drg, dyn, gpostm, ggn) = _merge_bwd(
        dh2, m, mix_post_w, omla, oret, proj, yn, ret_gn_w, o, w_o, w_bm_p, w_br, name="merge_bwd")
    dw_out = grad(merged, dmb, "dw_out")[0][0]
    dw_bm_p = grad(o, domla, "dw_branch_mla")[0][0]
    dw_br = grad(gated, doret, "dw_branch_ret")[0][0]
    sc_ffn2 = _SplitScatter(_Scatter([dw1b, dw2b]), "scatter_ffn2")
    dq, dk, dv = _flash_bwd(q, k, v, do, lse, delta, name="mla_attn_bwd", after=sc_ffn2.start())
    da, dql, dkvl, gqn, gkvn = _mla_prep_bwd(dq, dk, dv, proj, pos, mla_q_norm_w, mla_kv_norm_w, w_uq_p, w_kv_p, tab_mla, name="mla_prep_bwd")
    dw_uq_p = grad(dql, qn, "dw_uq")[0][0]
    dw_kv_p = grad(kvn, dkvl, "dw_ukv")[0][0]
    drq, drk, drv = _ret_bwd(dyn, ypre, proj, pos, tab_ret, rprev, name="retention_bwd")
    dproj = jnp.concatenate([drq, drk, drv, drg, da, dgm, dgr], axis=1)
    dw_in_p = grad(dproj, a1, "dw_in")[0][0]

    dw_uq = dw_uq_p.reshape(MLA_HEADS, HP, MLA_Q_RANK)[:, :uq_w]
    dkp = dw_kv_p[:, :QW].reshape(MLA_KV_RANK, MLA_HEADS, HP)[:, :, :MLA_NOPE]
    dvp = dw_kv_p[:, QW:].reshape(MLA_KV_RANK, MLA_HEADS, HP)[:, :, :MLA_V]
    dw_ukv = jnp.concatenate([dkp, dvp], axis=2).transpose(1, 0, 2)
    dw_bm = dw_bm_p.reshape(MLA_HEADS, HP, D)[:, :MLA_V].reshape(MLA_HEADS * MLA_V, D)
    small_mixer_grads = [dw_uq, dw_ukv, _col_shards(dw_bm), _col_shards(dw_br), dw_out.reshape(N_DEV, D // N_DEV, D)]
    sc_small = _SplitScatter(_Scatter(small_mixer_grads), "scatter_mixer_small")
    dh1, gmixpre = _proj_bwd(dproj, w_in_p, h1, mix_pre_w, dh2, name="mixer_in_bwd", after=sc_small.start())
    unhead = lambda a, h, wd: a.reshape(h, HP, D)[:, :wd].reshape(h * wd, D)
    c0 = 4 * RW
    dw_in = jnp.concatenate([
        dw_in_p[c0:c0 + 384], dw_in_p[c0 + 384:c0 + 640], dw_in_p[c0 + 640 + MLA_NOPE:c0 + 640 + MLA_NOPE + MLA_ROPE],
        unhead(dw_in_p[0:RW], RET_HEADS, RET_DK), unhead(dw_in_p[RW:2 * RW], RET_HEADS, RET_DK),
        dw_in_p[2 * RW:3 * RW], dw_in_p[3 * RW:4 * RW],
        dw_in_p[PROJ_FIXED:PROJ_FIXED + D], dw_in_p[PROJ_FIXED + D:PROJ_FIXED + 2 * D]], axis=0).reshape(N_DEV, -1, D)
    sc_w_in = _SplitScatter(_Scatter([dw_in]), "scatter_w_in")
    g1, du1, df1, a0, dx, gpost1, gpre1 = _ffn_bwd(
        dh1, f1, ffn1_post_w, h0, ffn1_pre_w, u1, w2a, w1a, name="ffn1_bwd", after=sc_w_in.start())
    dw2a = grad(g1, df1, "ffn1_dw2")[0].reshape(N_DEV, hp, D)
    sc_dw2a = _SplitScatter(_Scatter([dw2a]), "scatter_ffn1_dw2")
    dw1a, = grad(du1.reshape(N_DEV, T, 2 * hp), a0, "ffn1_dw1", after=sc_dw2a.start())

    small_g = {"ffn1_pre_w": gpre1, "ffn1_post_w": gpost1, "mix_pre_w": gmixpre, "mla_q_norm_w": gqn,
               "mla_kv_norm_w": gkvn, "ret_gn_w": ggn, "mix_post_w": gpostm, "ffn2_pre_w": gpre2, "ffn2_post_w": gpost2}
    sc_last = _SplitScatter(_Scatter([dw1a], whole=[small_g[nm] for nm, *_ in small]), "scatter_ffn1_dw1")
    token = sc_last.start()
    recv_ffn2 = sc_ffn2.wait([token])
    recv_mixer = sc_w_in.wait([token]) + sc_small.wait([token])
    recv_w2a, = sc_dw2a.wait([token])
    parts = dict(zip(mixer, recv_mixer))
    parts.update(ffn1_w2=recv_w2a, ffn2_w1=recv_ffn2[0], ffn2_w2=recv_ffn2[1])
    as_is = (lambda a: a, lambda p: p[:, None], lambda a: a)
    views = {nm: as_is for nm, *_ in big}
    for nm in ("ffn1_w1", "ffn2_w1"):
        views[nm] = (lambda a: rows_view(a).reshape(2, half, D), lambda p: p.reshape(N_DEV, 2, hp, D),
                     lambda a: a.reshape(2 * half, D).T[None])
    for nm in ("w_in", "mla_w_uq"):
        views[nm] = (lambda a: rows_view(a)[None], lambda p: p[:, None], lambda a: a[0].T[None])

    def update(nm, w, m_, v_, after):
        to_view, parts_view, back = views[nm]
        return [back(a) for a in _adamw(to_view(w), parts_view(parts[nm]), to_view(m_), to_view(v_), after,
                                        name="adamw_" + nm)]

    big_out = {nm: update(nm, w, m_, v_, token) for nm, w, m_, v_ in big if nm != "ffn1_w1"}
    recv_w1a, *small_parts = sc_last.wait([d[0] for d in big_out.values()])
    parts["ffn1_w1"] = recv_w1a
    big_out["ffn1_w1"] = update("ffn1_w1", ffn1_w1, m_ffn1_w1, v_ffn1_w1, jnp.zeros((8, LANES), F32))
    small_out = _adamw_vectors([w for _, w, _, _ in small], small_parts, [a for _, _, a, _ in small],
                               [a for _, _, _, a in small], name="adamw_replicated")

    order = ["ffn1_pre_w", "ffn1_w1", "ffn1_w2", "ffn1_post_w", "mix_pre_w", "w_in", "mla_q_norm_w", "mla_w_uq",
             "mla_kv_norm_w", "mla_w_ukv", "ret_gn_w", "w_branch_mla", "w_branch_ret", "w_out", "mix_post_w",
             "ffn2_pre_w", "ffn2_w1", "ffn2_w2", "ffn2_post_w"]
    outs = [loss, dx[None]]
    for i in range(4):
        both = {nm: big_out[nm][i] for nm in big_out}
        both.update({nm: small_out[4 * j + i] for j, (nm, *_) in enumerate(small)})
        outs += [both[nm] for nm in order]
    return tuple(outs)
```

```python
import math

import numpy as np
import jax
import jax.numpy as jnp
from jax import lax
from jax.experimental import pallas as pl
from jax.experimental.pallas import tpu as pltpu

F32, BF16 = jnp.float32, jnp.bfloat16

MLA_HEADS, MLA_NOPE, MLA_ROPE, MLA_V = 8, 64, 32, 64
MLA_Q_RANK, MLA_KV_RANK = 384, 256
RET_HEADS, RET_DK, RET_DV = 4, 64, 128
ROPE_BASE, NORM_EPS, GN_EPS = 10000.0, 1e-6, 1e-6
ADAM_LR, ADAM_B1, ADAM_B2, ADAM_EPS, ADAM_WD, ADAM_STEP = 0.001, 0.9, 0.999, 1e-08, 0.01, 10
ATTN_SCALE = 1.0 / math.sqrt(MLA_NOPE + MLA_ROPE)

N_DEV = 8
LANES = 128
HP = LANES
QW = MLA_HEADS * HP
RW = RET_HEADS * HP
AW = 1024
PROJ_FIXED = 4 * RW + AW
NEG = -1e30

TOKEN_TILE = 512
ATTN_TILE = 1024
ATTN_CHAINS = 2
FFN_CHAINS = 2
RET_TILE = 256
PROJ_TILE_CAP = 2560
GRAD_TILE_CAP = 1408
GRAD_TOKEN_TILE = 2048
MERGE_TILE = 256
VMEM_LIMIT = 56 * 1024 * 1024


def _tile(n, cap, mult=LANES):
    if n <= cap:
        return n
    best = None
    for t in range(mult, cap + 1, mult):
        if n % t == 0:
            best = t
    assert best is not None, (n, cap, mult)
    return best


def _params(sem):
    return pltpu.CompilerParams(dimension_semantics=sem, vmem_limit_bytes=VMEM_LIMIT)


def _dot(a, b):
    return lax.dot_general(a, b, (((1,), (0,)), ((), ())), preferred_element_type=F32)


def _dot_nt(a, b):
    return lax.dot_general(a, b, (((1,), (1,)), ((), ())), preferred_element_type=F32)


def _dot_tn(a, b):
    return lax.dot_general(a, b, (((0,), (0,)), ((), ())), preferred_element_type=F32)


def _sigmoid(x):
    return pl.reciprocal(1.0 + jnp.exp(-x), approx=True)


def _rms_fwd(x, w):
    r = lax.rsqrt(jnp.mean(x * x, axis=-1, keepdims=True) + NORM_EPS)
    return x * r * w


def _rms_bwd(x, w, dy):
    r = lax.rsqrt(jnp.mean(x * x, axis=-1, keepdims=True) + NORM_EPS)
    xh = x * r
    g = dy * w
    dx = r * (g - xh * jnp.mean(g * xh, axis=-1, keepdims=True))
    return dx, jnp.sum(dy * xh, axis=0, keepdims=True)


def _rope_table(first, half):
    inv = (np.float32(ROPE_BASE) ** (-(np.arange(half, dtype=np.float32) / np.float32(half)))).astype(np.float32)
    tab = np.zeros((8, LANES), np.float32)
    tab[0, first:first + half] = inv
    tab[0, first + half:first + 2 * half] = inv
    tab[1, first:first + half] = -1.0
    tab[2, first + half:first + 2 * half] = 1.0
    return jnp.asarray(tab)


def _rope_tables(pos, tabs, *, name):
    T = pos.shape[0]
    tT = min(TOKEN_TILE, T)
    n = len(tabs)

    def body(pos_ref, *refs):
        for tab_ref, o_ref in zip(refs[:n], refs[n:]):
            ang = pos_ref[...] * tab_ref[0:1, :]
            s = jnp.sin(ang)
            o_ref[0] = jnp.cos(ang)
            o_ref[1] = s * tab_ref[1:2, :]
            o_ref[2] = s * tab_ref[2:3, :]

    return pl.pallas_call(
        body, name=name, grid=(T // tT,),
        in_specs=[pl.BlockSpec((tT, 1), lambda i: (i, 0))] + [pl.BlockSpec((8, LANES), lambda i: (0, 0))] * n,
        out_specs=[pl.BlockSpec((3, tT, LANES), lambda i: (0, i, 0))] * n,
        out_shape=[jax.ShapeDtypeStruct((3, T, LANES), F32)] * n,
        compiler_params=_params(("parallel",)),
    )(pos, *tabs)


def _rope_cs(cs_ref):
    return cs_ref[0], cs_ref[1], cs_ref[2]


def _rope(x, cs, half, inverse=False):
    c, s1, s2 = cs
    a = pltpu.roll(x, LANES - half, 1) * s1 + pltpu.roll(x, half, 1) * s2
    return x * c - a if inverse else x * c + a


def _call(body, *, name, grid, in_specs, out_specs, out_shape, scratch_shapes, args, exchange=None, after=None):
    sem = ("arbitrary",) * len(grid)
    anyspec = pl.BlockSpec(memory_space=pl.ANY)
    if exchange is None and after is not None:
        n_own = len(in_specs)

        def behind(*refs):
            body(*refs[:n_own], *refs[n_own + 1:])

        return pl.pallas_call(behind, name=name, grid=grid, in_specs=list(in_specs) + [anyspec], out_specs=out_specs,
                              out_shape=out_shape, scratch_shapes=scratch_shapes, compiler_params=_params(sem))(*args, after)
    if exchange is None:
        return pl.pallas_call(body, name=name, grid=grid, in_specs=in_specs, out_specs=out_specs,
                              out_shape=out_shape, scratch_shapes=scratch_shapes, compiler_params=_params(sem))(*args)
    n_in, n_out, e = len(in_specs), len(out_specs), exchange.n
    total = math.prod(grid)

    def carried(*refs):
        own = refs[:n_in] + refs[n_in + e:n_in + e + n_out] + refs[n_in + 2 * e + n_out:len(refs) - 3]
        ex_refs = (refs[n_in:n_in + e], refs[n_in + e + n_out:n_in + 2 * e + n_out], refs[len(refs) - 3:])
        step = pl.program_id(0)
        for d in range(1, len(grid)):
            step = step * grid[d] + pl.program_id(d)

        @pl.when(step == 0)
        def _():
            exchange.phase(0, *ex_refs)

        @pl.when(step == (3 * total) // 4)
        def _():
            exchange.phase(1, *ex_refs)

        body(*own)

        @pl.when(step == total - 1)
        def _():
            exchange.phase(2, *ex_refs)

    return pl.pallas_call(
        carried, name=name, grid=grid, in_specs=list(in_specs) + [anyspec] * e,
        out_specs=list(out_specs) + [anyspec] * e, out_shape=list(out_shape) + exchange.out_shape,
        scratch_shapes=list(scratch_shapes) + exchange.scratch, compiler_params=_params(sem),
    )(*args, *exchange.operands)


def _ffn_fwd(h, pre_w, w1, w2, post_w, target, *, name, exchange=None):
    T, D = h.shape
    nk, ck = w2.shape[0], w2.shape[1]
    tT = min(TOKEN_TILE, T)
    nT = T // tT
    with_loss = target is not None

    def body(*refs):
        if with_loss:
            (h_ref, pre_ref, w1g_ref, w1u_ref, w2_ref, post_ref, tgt_ref,
             u_ref, f_ref, ho_ref, dy_ref, loss_ref, a_s, acc) = refs
        else:
            (h_ref, pre_ref, w1g_ref, w1u_ref, w2_ref, post_ref,
             u_ref, f_ref, ho_ref, a_s, acc) = refs
        k = pl.program_id(1)

        @pl.when(k == 0)
        def _():
            a_s[...] = _rms_fwd(h_ref[...], pre_ref[...]).astype(BF16)
            acc[...] = jnp.zeros_like(acc)

        for c in range(FFN_CHAINS):
            rs = slice(c * (tT // FFN_CHAINS), (c + 1) * (tT // FFN_CHAINS))
            a = a_s[rs, :]
            ug = _dot_nt(a, w1g_ref[...])
            uu = _dot_nt(a, w1u_ref[...])
            u_ref[0, rs, :] = ug.astype(BF16)
            u_ref[1, rs, :] = uu.astype(BF16)
            acc[rs, :] += _dot((ug * _sigmoid(ug) * uu).astype(BF16), w2_ref[...])

        @pl.when(k == nk - 1)
        def _():
            f = acc[...]
            f_ref[...] = f
            ho = h_ref[...] + 0.5 * _rms_fwd(f, post_ref[...])
            ho_ref[...] = ho
            if with_loss:
                e = ho - tgt_ref[...]
                dy_ref[...] = e * (1.0 / D)
                loss_ref[...] = jnp.full(loss_ref.shape, (0.5 / D) * jnp.sum(e * e), F32)

    row = pl.BlockSpec((tT, D), lambda i, k: (i, 0))
    vec = pl.BlockSpec((1, D), lambda i, k: (0, 0))
    in_specs = [row, vec,
                pl.BlockSpec((None, ck, D), lambda i, k: (k, 0, 0)),
                pl.BlockSpec((None, ck, D), lambda i, k: (nk + k, 0, 0)),
                pl.BlockSpec((None, ck, D), lambda i, k: (k, 0, 0)),
                vec]
    out_shape = [jax.ShapeDtypeStruct((2, nk, T, ck), BF16),
                 jax.ShapeDtypeStruct((T, D), F32),
                 jax.ShapeDtypeStruct((T, D), F32)]
    out_specs = [pl.BlockSpec((2, None, tT, ck), lambda i, k: (0, k, i, 0)), row, row]
    args = [h, pre_w, w1, w1, w2, post_w]
    if with_loss:
        in_specs.append(row)
        args.append(target)
        out_shape += [jax.ShapeDtypeStruct((T, D), F32), jax.ShapeDtypeStruct((nT * 8, LANES), F32)]
        out_specs += [row, pl.BlockSpec((8, LANES), lambda i, k: (i, 0))]
    return _call(body, name=name, grid=(nT, nk), in_specs=in_specs, out_specs=out_specs, out_shape=out_shape,
                 scratch_shapes=[pltpu.VMEM((tT, D), BF16), pltpu.VMEM((tT, D), F32)], args=args, exchange=exchange)


def _ffn_bwd(dho, f, post_w, h, pre_w, u, w2, w1, *, name, exchange=None, after=None):
    T, D = h.shape
    nk, ck = w2.shape[0], w2.shape[1]
    tT = min(TOKEN_TILE, T)
    nT = T // tT

    def body(dho_ref, f_ref, post_ref, h_ref, pre_ref, u_ref, w2_ref, w1g_ref, w1u_ref,
             g_ref, du_ref, df_ref, a_ref, dh_ref, gpost_ref, gpre_ref, df_s, da_acc):
        i, k = pl.program_id(0), pl.program_id(1)

        @pl.when(jnp.logical_and(i == 0, k == 0))
        def _():
            gpost_ref[...] = jnp.zeros_like(gpost_ref)
            gpre_ref[...] = jnp.zeros_like(gpre_ref)

        @pl.when(k == 0)
        def _():
            dx, dw = _rms_bwd(f_ref[...], post_ref[...], 0.5 * dho_ref[...])
            dfb = dx.astype(BF16)
            df_s[...] = dfb
            df_ref[...] = dfb
            gpost_ref[...] += dw
            a_ref[...] = _rms_fwd(h_ref[...], pre_ref[...]).astype(BF16)
            da_acc[...] = jnp.zeros_like(da_acc)

        groups = [slice(c * (tT // FFN_CHAINS), (c + 1) * (tT // FFN_CHAINS)) for c in range(FFN_CHAINS)]
        dgs = [_dot_nt(df_s[rs, :], w2_ref[...]) for rs in groups]
        for rs, dg in zip(groups, dgs):
            ug = u_ref[0, rs, :].astype(F32)
            uu = u_ref[1, rs, :].astype(F32)
            sg = _sigmoid(ug)
            sl = ug * sg
            g_ref[rs, :] = (sl * uu).astype(BF16)
            dug = (dg * uu * (sg + sl * (1.0 - sg))).astype(BF16)
            duu = (dg * sl).astype(BF16)
            du_ref[0, rs, :] = dug
            du_ref[1, rs, :] = duu
            da_acc[rs, :] += _dot(dug, w1g_ref[...]) + _dot(duu, w1u_ref[...])

        @pl.when(k == nk - 1)
        def _():
            dx, dw = _rms_bwd(h_ref[...], pre_ref[...], da_acc[...])
            dh_ref[...] = dho_ref[...] + dx
            gpre_ref[...] += dw

    row = pl.BlockSpec((tT, D), lambda i, k: (i, 0))
    vec = pl.BlockSpec((1, D), lambda i, k: (0, 0))
    return _call(
        body, name=name, grid=(nT, nk),
        in_specs=[row, row, vec, row, vec,
                  pl.BlockSpec((2, None, tT, ck), lambda i, k: (0, k, i, 0)),
                  pl.BlockSpec((None, ck, D), lambda i, k: (k, 0, 0)),
                  pl.BlockSpec((None, ck, D), lambda i, k: (k, 0, 0)),
                  pl.BlockSpec((None, ck, D), lambda i, k: (nk + k, 0, 0))],
        out_specs=[pl.BlockSpec((None, tT, ck), lambda i, k: (k, i, 0)),
                   pl.BlockSpec((2, None, tT, ck), lambda i, k: (0, k, i, 0)),
                   row, row, row, vec, vec],
        out_shape=[jax.ShapeDtypeStruct((nk, T, ck), BF16),
                   jax.ShapeDtypeStruct((2, nk, T, ck), BF16),
                   jax.ShapeDtypeStruct((T, D), BF16),
                   jax.ShapeDtypeStruct((T, D), BF16),
                   jax.ShapeDtypeStruct((T, D), F32),
                   jax.ShapeDtypeStruct((1, D), F32),
                   jax.ShapeDtypeStruct((1, D), F32)],
        scratch_shapes=[pltpu.VMEM((tT, D), BF16), pltpu.VMEM((tT, D), F32)],
        args=(dho, f, post_w, h, pre_w, u, w2, w1, w1), exchange=exchange, after=after)


def _matmul_tn(x, dy, *, name, exchange=None, after=None):
    Px, T, K = x.shape
    Py, _, N = dy.shape
    P = max(Px, Py)
    tT, tK, tN = min(GRAD_TOKEN_TILE, T), _tile(K, GRAD_TILE_CAP), _tile(N, GRAD_TILE_CAP)
    nt = T // tT

    def body(x_ref, dy_ref, o_ref, acc):
        t = pl.program_id(3)

        @pl.when(t == 0)
        def _():
            acc[...] = jnp.zeros_like(acc)

        acc[...] += _dot_tn(x_ref[...], dy_ref[...])

        @pl.when(t == nt - 1)
        def _():
            o_ref[...] = acc[...].astype(BF16)

    return _call(
        body, name=name, grid=(P, K // tK, N // tN, nt),
        in_specs=[pl.BlockSpec((None, tT, tK), lambda p, a, b, t: (p if Px > 1 else 0, t, a)),
                  pl.BlockSpec((None, tT, tN), lambda p, a, b, t: (p if Py > 1 else 0, t, b))],
        out_specs=[pl.BlockSpec((None, tK, tN), lambda p, a, b, t: (p, a, b))],
        out_shape=[jax.ShapeDtypeStruct((P, K, N), BF16)],
        scratch_shapes=[pltpu.VMEM((tK, tN), F32)], args=(x, dy), exchange=exchange, after=after)


def _rms_matmul(h, wn, w, *, name):
    T, D = h.shape
    N = w.shape[0]
    tT, tN = min(TOKEN_TILE, T), _tile(N, PROJ_TILE_CAP)

    def body(h_ref, wn_ref, w_ref, y_ref, a_ref):
        @pl.when(pl.program_id(1) == 0)
        def _():
            a_ref[...] = _rms_fwd(h_ref[...], wn_ref[...]).astype(BF16)

        y_ref[...] = _dot_nt(a_ref[...], w_ref[...]).astype(BF16)

    return pl.pallas_call(
        body, name=name, grid=(T // tT, N // tN),
        in_specs=[pl.BlockSpec((tT, D), lambda i, j: (i, 0)),
                  pl.BlockSpec((1, D), lambda i, j: (0, 0)),
                  pl.BlockSpec((tN, D), lambda i, j: (j, 0))],
        out_specs=[pl.BlockSpec((tT, tN), lambda i, j: (i, j)),
                   pl.BlockSpec((tT, D), lambda i, j: (i, 0))],
        out_shape=[jax.ShapeDtypeStruct((T, N), BF16), jax.ShapeDtypeStruct((T, D), BF16)],
        compiler_params=_params(("parallel", "arbitrary")),
    )(h, wn, w)


def _proj_bwd(dproj, w, h, wn, dres, *, name, exchange=None, after=None):
    T, D = h.shape
    N = w.shape[0]
    tT, tN = min(TOKEN_TILE, T), _tile(N, PROJ_TILE_CAP)
    nn = N // tN

    def body(dp_ref, w_ref, h_ref, wn_ref, dres_ref, dh_ref, gw_ref, acc):
        i, j = pl.program_id(0), pl.program_id(1)

        @pl.when(jnp.logical_and(i == 0, j == 0))
        def _():
            gw_ref[...] = jnp.zeros_like(gw_ref)

        @pl.when(j == 0)
        def _():
            acc[...] = jnp.zeros_like(acc)

        acc[...] += _dot(dp_ref[...], w_ref[...])

        @pl.when(j == nn - 1)
        def _():
            dx, dw = _rms_bwd(h_ref[...], wn_ref[...], acc[...])
            dh_ref[...] = dres_ref[...] + dx
            gw_ref[...] += dw

    row = pl.BlockSpec((tT, D), lambda i, j: (i, 0))
    vec = pl.BlockSpec((1, D), lambda i, j: (0, 0))
    return _call(
        body, name=name, grid=(T // tT, nn),
        in_specs=[pl.BlockSpec((tT, tN), lambda i, j: (i, j)),
                  pl.BlockSpec((tN, D), lambda i, j: (j, 0)), row, vec, row],
        out_specs=[row, vec],
        out_shape=[jax.ShapeDtypeStruct((T, D), F32), jax.ShapeDtypeStruct((1, D), F32)],
        scratch_shapes=[pltpu.VMEM((tT, D), F32)], args=(dproj, w, h, wn, dres), exchange=exchange, after=after)


def _mla_prep_fwd(proj, cs, qn_w, kvn_w, w_uq, w_kv, *, name):
    T = proj.shape[0]
    tT = min(TOKEN_TILE, T)
    a_blk = PROJ_FIXED // AW - 1

    def body(a_ref, cs_ref, qnw_ref, kvnw_ref, wuq_ref, wkv_ref,
             q_ref, k_ref, v_ref, qn_ref, kvn_ref):
        cq = a_ref[:, 0:MLA_Q_RANK].astype(F32)
        ckv = a_ref[:, MLA_Q_RANK:MLA_Q_RANK + MLA_KV_RANK].astype(F32)
        kr = a_ref[:, 640:768].astype(F32)
        qn = _rms_fwd(cq, qnw_ref[...]).astype(BF16)
        kvn = _rms_fwd(ckv, kvnw_ref[...]).astype(BF16)
        qn_ref[...] = qn
        kvn_ref[...] = kvn
        cs = _rope_cs(cs_ref)
        q = _dot_nt(qn, wuq_ref[...])
        kv = _dot(kvn, wkv_ref[...])
        krr = _rope(kr, cs, MLA_ROPE // 2)
        for hd in range(MLA_HEADS):
            sl = slice(hd * HP, (hd + 1) * HP)
            q_ref[:, sl] = (_rope(q[:, sl], cs, MLA_ROPE // 2) * ATTN_SCALE).astype(BF16)
            k_ref[:, sl] = (kv[:, sl] + krr).astype(BF16)
        v_ref[...] = kv[:, QW:].astype(BF16)

    def full(r, c):
        return pl.BlockSpec((r, c), lambda i: (0, 0))

    def rows(c):
        return pl.BlockSpec((tT, c), lambda i: (i, 0))

    return pl.pallas_call(
        body, name=name, grid=(T // tT,),
        in_specs=[pl.BlockSpec((tT, AW), lambda i: (i, a_blk)), pl.BlockSpec((3, tT, LANES), lambda i: (0, i, 0)),
                  full(1, MLA_Q_RANK), full(1, MLA_KV_RANK),
                  full(QW, MLA_Q_RANK), full(MLA_KV_RANK, 2 * QW)],
        out_specs=[rows(QW), rows(QW), rows(QW), rows(MLA_Q_RANK), rows(MLA_KV_RANK)],
        out_shape=[jax.ShapeDtypeStruct((T, QW), BF16)] * 3
        + [jax.ShapeDtypeStruct((T, MLA_Q_RANK), BF16), jax.ShapeDtypeStruct((T, MLA_KV_RANK), BF16)],
        compiler_params=_params(("parallel",)),
    )(proj, cs, qn_w, kvn_w, w_uq, w_kv)


def _mla_prep_bwd(dq, dk, dv, proj, cs, qn_w, kvn_w, w_uq, w_kv, *, name):
    T = proj.shape[0]
    tT = min(TOKEN_TILE, T)
    a_blk = PROJ_FIXED // AW - 1

    def body(dq_ref, dk_ref, dv_ref, a_ref, cs_ref, qnw_ref, kvnw_ref, wuq_ref, wkv_ref,
             da_ref, dql_ref, dkvl_ref, gqn_ref, gkvn_ref):
        @pl.when(pl.program_id(0) == 0)
        def _():
            gqn_ref[...] = jnp.zeros_like(gqn_ref)
            gkvn_ref[...] = jnp.zeros_like(gkvn_ref)

        cs = _rope_cs(cs_ref)
        dkr = jnp.zeros((tT, HP), F32)
        for hd in range(MLA_HEADS):
            sl = slice(hd * HP, (hd + 1) * HP)
            dql_ref[:, sl] = (_rope(dq_ref[:, sl], cs, MLA_ROPE // 2, inverse=True) * ATTN_SCALE).astype(BF16)
            dkh = dk_ref[:, sl]
            dkr = dkr + dkh
            dkvl_ref[:, sl] = dkh.astype(BF16)
        dkvl_ref[:, QW:] = dv_ref[...]
        dqn = _dot(dql_ref[...], wuq_ref[...])
        dkvn = _dot_nt(dkvl_ref[...], wkv_ref[...])
        cq = a_ref[:, 0:MLA_Q_RANK].astype(F32)
        ckv = a_ref[:, MLA_Q_RANK:MLA_Q_RANK + MLA_KV_RANK].astype(F32)
        dcq, gq = _rms_bwd(cq, qnw_ref[...], dqn)
        dckv, gkv = _rms_bwd(ckv, kvnw_ref[...], dkvn)
        gqn_ref[...] += gq
        gkvn_ref[...] += gkv
        da_ref[:, 0:MLA_Q_RANK] = dcq.astype(BF16)
        da_ref[:, MLA_Q_RANK:MLA_Q_RANK + MLA_KV_RANK] = dckv.astype(BF16)
        da_ref[:, 640:768] = _rope(dkr, cs, MLA_ROPE // 2, inverse=True).astype(BF16)
        da_ref[:, 768:AW] = jnp.zeros((tT, AW - 768), BF16)

    def full(r, c):
        return pl.BlockSpec((r, c), lambda i: (0, 0))

    def rows(c):
        return pl.BlockSpec((tT, c), lambda i: (i, 0))

    return pl.pallas_call(
        body, name=name, grid=(T // tT,),
        in_specs=[rows(QW), rows(QW), rows(QW), pl.BlockSpec((tT, AW), lambda i: (i, a_blk)),
                  pl.BlockSpec((3, tT, LANES), lambda i: (0, i, 0)),
                  full(1, MLA_Q_RANK), full(1, MLA_KV_RANK),
                  full(QW, MLA_Q_RANK), full(MLA_KV_RANK, 2 * QW)],
        out_specs=[rows(AW), rows(QW), rows(2 * QW), full(1, MLA_Q_RANK), full(1, MLA_KV_RANK)],
        out_shape=[jax.ShapeDtypeStruct((T, AW), BF16), jax.ShapeDtypeStruct((T, QW), BF16),
                   jax.ShapeDtypeStruct((T, 2 * QW), BF16),
                   jax.ShapeDtypeStruct((1, MLA_Q_RANK), F32), jax.ShapeDtypeStruct((1, MLA_KV_RANK), F32)],
        compiler_params=_params(("arbitrary",)),
    )(dq, dk, dv, proj, cs, qn_w, kvn_w, w_uq, w_kv)


def _flash_fwd(q, k, v, *, name, exchange=None):
    T = q.shape[0]
    H = q.shape[1] // HP
    tq = min(ATTN_TILE, T)
    nq = T // tq

    sub = tq // ATTN_CHAINS

    def body(q_ref, k_ref, v_ref, o_ref, lse_ref):
        qi = pl.program_id(1)
        qs = [q_ref[c * sub:(c + 1) * sub, :] for c in range(ATTN_CHAINS)]

        def update(carry, off, masked):
            nks = [(c + 1) * sub if masked else tq for c in range(ATTN_CHAINS)]
            scores = [_dot_nt(qs[c], k_ref[pl.ds(off, nks[c]), :]) for c in range(ATTN_CHAINS)]
            out = []
            for c in range(ATTN_CHAINS):
                m_prev, l_prev, acc = carry[c]
                nk, s = nks[c], scores[c]
                vb = v_ref[pl.ds(off, nk), :]
                if masked:
                    rows = lax.broadcasted_iota(jnp.int32, (sub, nk), 0) + c * sub
                    s = jnp.where(rows >= lax.broadcasted_iota(jnp.int32, (sub, nk), 1), s, NEG)
                m_new = jnp.maximum(m_prev, jnp.max(s, axis=1, keepdims=True))
                alpha = jnp.exp(m_prev - m_new)
                p = jnp.exp(s - m_new)
                out.append((m_new, alpha * l_prev + jnp.sum(p, axis=1, keepdims=True),
                            alpha * acc + _dot(p.astype(BF16), vb)))
            return tuple(out)

        init = tuple((jnp.full((sub, 1), NEG, F32), jnp.zeros((sub, 1), F32), jnp.zeros((sub, HP), F32))
                     for _ in range(ATTN_CHAINS))
        carry = lax.fori_loop(0, qi, lambda j, cr: update(cr, pl.multiple_of(j * tq, tq), False), init)
        carry = update(carry, pl.multiple_of(qi * tq, tq), True)
        for c in range(ATTN_CHAINS):
            m_fin, l_fin, acc = carry[c]
            o_ref[c * sub:(c + 1) * sub, :] = (acc / l_fin).astype(BF16)
            lse_ref[c * sub:(c + 1) * sub, :] = jnp.broadcast_to(m_fin + jnp.log(l_fin), (sub, HP))

    qspec = pl.BlockSpec((tq, HP), lambda h, i: (i, h))
    kspec = pl.BlockSpec((T, HP), lambda h, i: (0, h))
    return _call(
        body, name=name, grid=(H, nq),
        in_specs=[qspec, kspec, kspec], out_specs=[qspec, qspec],
        out_shape=[jax.ShapeDtypeStruct((T, H * HP), BF16), jax.ShapeDtypeStruct((T, H * HP), F32)],
        scratch_shapes=[], args=(q, k, v), exchange=exchange)


def _flash_bwd(q, k, v, do, lse, delta, *, name, exchange=None, after=None):
    T = q.shape[0]
    H = q.shape[1] // HP
    tq = min(ATTN_TILE, T)
    nq = T // tq
    sub = tq // ATTN_CHAINS

    def body(k_ref, v_ref, q_ref, do_ref, lse_ref, dl_ref, dq_ref, dk_ref, dv_ref):
        ki = pl.program_id(1)

        @pl.when(ki == 0)
        def _():
            dq_ref[...] = jnp.zeros_like(dq_ref)

        def grow(a):
            return a if a.shape[0] == tq else jnp.concatenate([a, jnp.zeros((tq - a.shape[0], HP), F32)], axis=0)

        def step(carry, j, masked):
            dk_acc, dv_acc = carry
            nks = [(c + 1) * sub if masked else tq for c in range(ATTN_CHAINS)]
            rws = [pl.ds(pl.multiple_of(j * tq + c * sub, sub), sub) for c in range(ATTN_CHAINS)]
            scores = [_dot_nt(q_ref[rws[c], :], k_ref[0:nks[c], :]) for c in range(ATTN_CHAINS)]
            dps = [_dot_nt(do_ref[rws[c], :], v_ref[0:nks[c], :]) for c in range(ATTN_CHAINS)]
            for c in range(ATTN_CHAINS):
                rows, nk, s, dp = rws[c], nks[c], scores[c], dps[c]
                kb = k_ref[0:nk, :]
                qb = q_ref[rows, :]
                dob = do_ref[rows, :]
                if masked:
                    ri = lax.broadcasted_iota(jnp.int32, (sub, nk), 0) + c * sub
                    s = jnp.where(ri >= lax.broadcasted_iota(jnp.int32, (sub, nk), 1), s, NEG)
                p = jnp.exp(s - lse_ref[rows, 0:1])
                dv_acc = dv_acc + grow(_dot_tn(p.astype(BF16), dob))
                ds = (p * (dp - dl_ref[rows, 0:1])).astype(BF16)
                dk_acc = dk_acc + grow(_dot_tn(ds, qb))
                dq_ref[rows, :] += _dot(ds, kb)
            return dk_acc, dv_acc

        carry = step((jnp.zeros((tq, HP), F32), jnp.zeros((tq, HP), F32)), ki, True)
        dk_acc, dv_acc = lax.fori_loop(ki + 1, nq, lambda j, cr: step(cr, j, False), carry)
        dk_ref[...] = dk_acc
        dv_ref[...] = dv_acc.astype(BF16)

    kspec = pl.BlockSpec((tq, HP), lambda h, j: (j, h))
    full = pl.BlockSpec((T, HP), lambda h, j: (0, h))
    return _call(
        body, name=name, grid=(H, nq),
        in_specs=[kspec, kspec, full, full, full, full], out_specs=[full, kspec, kspec],
        out_shape=[jax.ShapeDtypeStruct((T, H * HP), F32), jax.ShapeDtypeStruct((T, H * HP), F32),
                   jax.ShapeDtypeStruct((T, H * HP), BF16)],
        scratch_shapes=[], args=(k, v, q, do, lse, delta), exchange=exchange, after=after)


def _ret_tables(cc):
    idx = np.arange(cc, dtype=np.float64)
    diff = idx[:, None] - idx[None, :]
    dec, zx = [], []
    for hd in range(RET_HEADS):
        lg = math.log(1.0 - 2.0 ** (-5.0 - hd))
        dec.append(np.where(diff >= 0, np.exp(np.maximum(diff, 0.0) * lg), 0.0))
        zx.append(np.stack([np.exp((cc - 1.0 - idx) * lg), np.exp((idx + 1.0) * lg)])[:, :, None])
    return jnp.asarray(np.stack(dec), F32), jnp.asarray(np.stack(zx), F32)


def _ret_consts(dec_ref, zx_ref, cc, hd):
    return dec_ref[hd], zx_ref[hd, 0], zx_ref[hd, 1], math.exp(cc * math.log(1.0 - 2.0 ** (-5.0 - hd)))


def _ret_fwd(proj, cs, *, name):
    T = proj.shape[0]
    cc = min(RET_TILE, T)
    n = T // cc
    dec, zx = _ret_tables(cc)

    def body(rq_ref, rk_ref, rv_ref, cs_ref, dec_ref, zx_ref, y_ref, yn_ref, rprev_ref, r_s):
        @pl.when(pl.program_id(0) == 0)
        def _():
            r_s[...] = jnp.zeros_like(r_s)

        cs = _rope_cs(cs_ref)
        for hd in range(RET_HEADS):
            sl = slice(hd * HP, (hd + 1) * HP)
            decay, zeta, xi, gc = _ret_consts(dec_ref, zx_ref, cc, hd)
            q = _rope(rq_ref[:, sl].astype(F32), cs, RET_DK // 2).astype(BF16)
            kf = _rope(rk_ref[:, sl].astype(F32), cs, RET_DK // 2) * (RET_DK ** -0.5)
            k = kf.astype(BF16)
            v = rv_ref[:, sl]
            r = r_s[hd]
            rprev_ref[0, hd] = r
            inner = (_dot_nt(q, k) * decay).astype(BF16)
            y = _dot(inner, v) + _dot(q, r.astype(BF16)) * xi
            r_s[hd] = r * gc + _dot_tn((kf * zeta).astype(BF16), v)
            y_ref[:, sl] = y
            mu = jnp.mean(y, axis=-1, keepdims=True)
            yc = y - mu
            var = jnp.mean(yc * yc, axis=-1, keepdims=True)
            yn_ref[:, sl] = (yc * lax.rsqrt(var + GN_EPS)).astype(BF16)

    def blk(j):
        return pl.BlockSpec((cc, RW), lambda i: (i, j))

    return pl.pallas_call(
        body, name=name, grid=(n,),
        in_specs=[blk(0), blk(1), blk(2), pl.BlockSpec((3, cc, LANES), lambda i: (0, i, 0)),
                  pl.BlockSpec((RET_HEADS, cc, cc), lambda i: (0, 0, 0)),
                  pl.BlockSpec((RET_HEADS, 2, cc, 1), lambda i: (0, 0, 0, 0))],
        out_specs=[blk(0), blk(0), pl.BlockSpec((1, RET_HEADS, HP, RET_DV), lambda i: (i, 0, 0, 0))],
        out_shape=[jax.ShapeDtypeStruct((T, RW), F32), jax.ShapeDtypeStruct((T, RW), BF16),
                   jax.ShapeDtypeStruct((n, RET_HEADS, HP, RET_DV), F32)],
        scratch_shapes=[pltpu.VMEM((RET_HEADS, HP, RET_DV), F32)],
        compiler_params=_params(("arbitrary",)),
    )(proj, proj, proj, cs, dec, zx)


def _ret_bwd(dyn, y, proj, cs, rprev, *, name):
    T = proj.shape[0]
    cc = min(RET_TILE, T)
    n = T // cc
    dec, zx = _ret_tables(cc)

    def body(dyn_ref, y_ref, rq_ref, rk_ref, rv_ref, cs_ref, dec_ref, zx_ref, rprev_ref,
             drq_ref, drk_ref, drv_ref, dr_s):
        @pl.when(pl.program_id(0) == 0)
        def _():
            dr_s[...] = jnp.zeros_like(dr_s)

        cs = _rope_cs(cs_ref)
        for hd in range(RET_HEADS):
            sl = slice(hd * HP, (hd + 1) * HP)
            decay, zeta, xi, gc = _ret_consts(dec_ref, zx_ref, cc, hd)
            q = _rope(rq_ref[:, sl].astype(F32), cs, RET_DK // 2).astype(BF16)
            kf = _rope(rk_ref[:, sl].astype(F32), cs, RET_DK // 2) * (RET_DK ** -0.5)
            k = kf.astype(BF16)
            v = rv_ref[:, sl]
            yv = y_ref[:, sl]
            mu = jnp.mean(yv, axis=-1, keepdims=True)
            yc = yv - mu
            rs = lax.rsqrt(jnp.mean(yc * yc, axis=-1, keepdims=True) + GN_EPS)
            yn = yc * rs
            dn = dyn_ref[:, sl]
            dy = rs * (dn - jnp.mean(dn, axis=-1, keepdims=True) - yn * jnp.mean(dn * yn, axis=-1, keepdims=True))
            dyb = dy.astype(BF16)
            dyx = (dy * xi).astype(BF16)
            dr = dr_s[hd]
            drb = dr.astype(BF16)
            inner = (_dot_nt(q, k) * decay).astype(BF16)
            da = (_dot_nt(dyb, v) * decay).astype(BF16)
            dv = _dot_tn(inner, dyb) + _dot((kf * zeta).astype(BF16), drb)
            dq = _dot(da, k) + _dot_nt(dyx, rprev_ref[0, hd].astype(BF16))
            dk = _dot_tn(da, q) + _dot_nt(v, drb) * zeta
            dr_s[hd] = dr * gc + _dot_tn(q, dyx)
            drq_ref[:, sl] = _rope(dq, cs, RET_DK // 2, inverse=True).astype(BF16)
            drk_ref[:, sl] = _rope(dk * (RET_DK ** -0.5), cs, RET_DK // 2, inverse=True).astype(BF16)
            drv_ref[:, sl] = dv.astype(BF16)

    def blk(j):
        return pl.BlockSpec((cc, RW), lambda i: (n - 1 - i, j))

    return pl.pallas_call(
        body, name=name, grid=(n,),
        in_specs=[blk(0), blk(0), blk(0), blk(1), blk(2), pl.BlockSpec((3, cc, LANES), lambda i: (0, n - 1 - i, 0)),
                  pl.BlockSpec((RET_HEADS, cc, cc), lambda i: (0, 0, 0)),
                  pl.BlockSpec((RET_HEADS, 2, cc, 1), lambda i: (0, 0, 0, 0)),
                  pl.BlockSpec((1, RET_HEADS, HP, RET_DV), lambda i: (n - 1 - i, 0, 0, 0))],
        out_specs=[blk(0), blk(0), blk(0)],
        out_shape=[jax.ShapeDtypeStruct((T, RW), BF16)] * 3,
        scratch_shapes=[pltpu.VMEM((RET_HEADS, HP, RET_DV), F32)],
        compiler_params=_params(("arbitrary",)),
    )(dyn, y, proj, proj, proj, cs, dec, zx, rprev)


def _merge_fwd(o, yn, proj, gn_w, w_bm, w_br, w_out, h, post_w, *, name):
    T, D = h.shape
    tT = min(MERGE_TILE, T)
    g_blk = PROJ_FIXED // D

    def body(o_ref, yn_ref, rg_ref, gm_ref, gr_ref, gnw_ref, wbm_ref, wbr_ref, wout_ref, h_ref, post_ref,
             omla_ref, oret_ref, m_ref, ho_ref):
        o_mla = _dot(o_ref[...], wbm_ref[...])
        rg = rg_ref[...].astype(F32)
        gated = (rg * _sigmoid(rg) * (yn_ref[...].astype(F32) * gnw_ref[...])).astype(BF16)
        o_ret = _dot(gated, wbr_ref[...])
        omla_ref[...] = o_mla.astype(BF16)
        oret_ref[...] = o_ret.astype(BF16)
        merged = _sigmoid(gm_ref[...].astype(F32)) * o_mla + _sigmoid(gr_ref[...].astype(F32)) * o_ret
        m = _dot(merged.astype(BF16), wout_ref[...])
        m_ref[...] = m
        ho_ref[...] = h_ref[...] + _rms_fwd(m, post_ref[...])

    def full(r, c):
        return pl.BlockSpec((r, c), lambda i: (0, 0))

    def rows(c, j=0):
        return pl.BlockSpec((tT, c), lambda i: (i, j))

    return pl.pallas_call(
        body, name=name, grid=(T // tT,),
        in_specs=[rows(QW), rows(RW), rows(RW, 3), rows(D, g_blk), rows(D, g_blk + 1), full(1, RW),
                  full(QW, D), full(RW, D), full(D, D), rows(D), full(1, D)],
        out_specs=[rows(D), rows(D), rows(D), rows(D)],
        out_shape=[jax.ShapeDtypeStruct((T, D), BF16), jax.ShapeDtypeStruct((T, D), BF16),
                   jax.ShapeDtypeStruct((T, D), F32), jax.ShapeDtypeStruct((T, D), F32)],
        compiler_params=_params(("parallel",)),
    )(o, yn, proj, proj, proj, gn_w, w_bm, w_br, w_out, h, post_w)


def _merge_bwd(dho, m, post_w, omla, oret, proj, yn, gn_w, o, w_out, w_bm, w_br, *, name):
    T, D = dho.shape
    tT = min(MERGE_TILE, T)
    g_blk = PROJ_FIXED // D

    def body(dho_ref, m_ref, post_ref, omla_ref, oret_ref, rg_ref, gm_ref, gr_ref, yn_ref, gnw_ref, o_ref,
             wout_ref, wbm_ref, wbr_ref,
             dm_ref, merged_ref, dgm_ref, dgr_ref, domla_ref, do_ref, delta_ref, doret_ref, gated_ref,
             drg_ref, dyn_ref, gpost_ref, ggn_ref):
        @pl.when(pl.program_id(0) == 0)
        def _():
            gpost_ref[...] = jnp.zeros_like(gpost_ref)
            ggn_ref[...] = jnp.zeros_like(ggn_ref)

        dm, gp = _rms_bwd(m_ref[...], post_ref[...], dho_ref[...])
        gpost_ref[...] += gp
        dmb = dm.astype(BF16)
        dm_ref[...] = dmb
        dmerged = _dot_nt(dmb, wout_ref[...])
        o_mla = omla_ref[...].astype(F32)
        o_ret = oret_ref[...].astype(F32)
        sgm = _sigmoid(gm_ref[...].astype(F32))
        sgr = _sigmoid(gr_ref[...].astype(F32))
        merged_ref[...] = (sgm * o_mla + sgr * o_ret).astype(BF16)
        dgm_ref[...] = (dmerged * o_mla * sgm * (1.0 - sgm)).astype(BF16)
        dgr_ref[...] = (dmerged * o_ret * sgr * (1.0 - sgr)).astype(BF16)
        domla = (dmerged * sgm).astype(BF16)
        domla_ref[...] = domla
        do = _dot_nt(domla, wbm_ref[...])
        do_ref[...] = do.astype(BF16)
        for hd in range(MLA_HEADS):
            sl = slice(hd * HP, (hd + 1) * HP)
            d = jnp.sum(do[:, sl] * o_ref[:, sl].astype(F32), axis=-1, keepdims=True)
            delta_ref[:, sl] = jnp.broadcast_to(d, (tT, HP))
        doret = (dmerged * sgr).astype(BF16)
        doret_ref[...] = doret
        dgated = _dot_nt(doret, wbr_ref[...])
        rg = rg_ref[...].astype(F32)
        sg = _sigmoid(rg)
        srg = rg * sg
        ynv = yn_ref[...].astype(F32)
        yw = ynv * gnw_ref[...]
        gated_ref[...] = (srg * yw).astype(BF16)
        drg_ref[...] = (dgated * yw * (sg * (1.0 + rg * (1.0 - sg)))).astype(BF16)
        dgs = dgated * srg
        dyn_ref[...] = dgs * gnw_ref[...]
        ggn_ref[...] += jnp.sum(dgs * ynv, axis=0, keepdims=True)

    def full(r, c):
        return pl.BlockSpec((r, c), lambda i: (0, 0))

    def rows(c, j=0):
        return pl.BlockSpec((tT, c), lambda i: (i, j))

    return pl.pallas_call(
        body, name=name, grid=(T // tT,),
        in_specs=[rows(D), rows(D), full(1, D), rows(D), rows(D), rows(RW, 3), rows(D, g_blk), rows(D, g_blk + 1),
                  rows(RW), full(1, RW), rows(QW), full(D, D), full(QW, D), full(RW, D)],
        out_specs=[rows(D), rows(D), rows(D), rows(D), rows(D), rows(QW), rows(QW), rows(D), rows(RW),
                   rows(RW), rows(RW), full(1, D), full(1, RW)],
        out_shape=[jax.ShapeDtypeStruct((T, D), BF16)] * 5
        + [jax.ShapeDtypeStruct((T, QW), BF16), jax.ShapeDtypeStruct((T, QW), F32),
           jax.ShapeDtypeStruct((T, D), BF16), jax.ShapeDtypeStruct((T, RW), BF16),
           jax.ShapeDtypeStruct((T, RW), BF16), jax.ShapeDtypeStruct((T, RW), F32),
           jax.ShapeDtypeStruct((1, D), F32), jax.ShapeDtypeStruct((1, RW), F32)],
        compiler_params=_params(("arbitrary",)),
    )(dho, m, post_w, omla, oret, proj, proj, proj, yn, gn_w, o, w_out, w_bm, w_br)


def _mesh_pos():
    return lax.axis_index("x"), lax.axis_index("y"), lax.axis_index("c")


class _Gather:
    def __init__(self, shards):
        self.operands = list(shards)
        self.n = len(shards)
        self.out_shape = [jax.ShapeDtypeStruct((N_DEV,) + s.shape, s.dtype) for s in shards]
        self.scratch = [pltpu.SemaphoreType.DMA((7 * self.n,)), pltpu.SemaphoreType.DMA((7 * self.n,)),
                        pltpu.SemaphoreType.DMA((self.n,))]

    def phase(self, p, x_refs, out_refs, sems):
        send_sems, recv_sems, local_sems = sems
        x, y, c = _mesh_pos()
        me, sibling = (x, y, c), (x, y, 1 - c)
        chips = [(1 - x, y), (x, 1 - y), (1 - x, 1 - y)]

        def copy(w, k, block, to, src=None):
            slot = out_refs[w].at[4 * block[0] + 2 * block[1] + block[2]]
            return pltpu.make_async_remote_copy(
                src_ref=slot if src is None else src, dst_ref=slot,
                send_sem=send_sems.at[7 * w + k], recv_sem=recv_sems.at[7 * w + k],
                device_id=to, device_id_type=pl.DeviceIdType.MESH)

        for w in range(self.n):
            mine = pltpu.make_async_copy(x_refs[w], out_refs[w].at[4 * x + 2 * y + c], local_sems.at[w])
            first = [copy(w, 0, me, sibling, src=x_refs[w])]
            first += [copy(w, 1 + j, me, (*chip, c), src=x_refs[w]) for j, chip in enumerate(chips)]
            passed = [copy(w, 4 + j, (*chip, c), sibling) for j, chip in enumerate(chips)]
            if p == 0:
                mine.start()
                for cp in first:
                    cp.start()
            elif p == 1:
                for j, chip in enumerate(chips):
                    copy(w, 1 + j, (*chip, c), me).wait_recv()
                    passed[j].start()
            else:
                copy(w, 0, sibling, me).wait_recv()
                for j, chip in enumerate(chips):
                    copy(w, 4 + j, (*chip, 1 - c), me).wait_recv()
                for cp in first + passed:
                    cp.wait_send()
                mine.wait()


class _Scatter:
    def __init__(self, grads, whole=()):
        self.n_sliced = len(grads)
        self.operands = list(grads) + list(whole)
        self.n = len(self.operands)
        self.out_shape = [jax.ShapeDtypeStruct(g.shape, g.dtype) for g in grads]
        self.out_shape += [jax.ShapeDtypeStruct((N_DEV,) + a.shape, a.dtype) for a in whole]
        n_sem = (N_DEV - 1) * self.n
        self.scratch = [pltpu.SemaphoreType.DMA((n_sem,)), pltpu.SemaphoreType.DMA((n_sem,)),
                        pltpu.SemaphoreType.DMA((self.n,))]

    def phase(self, p, in_refs, out_refs, sems):
        if p == 1:
            return
        send_sems, recv_sems, local_sems = sems
        x, y, c = _mesh_pos()
        me = 4 * x + 2 * y + c

        def src(w, dev):
            return in_refs[w].at[dev] if w < self.n_sliced else in_refs[w]

        for w in range(self.n):
            own = None if local_sems is None else pltpu.make_async_copy(src(w, me), out_refs[w].at[me], local_sems.at[w])
            sends, recvs = [], []
            for r in range(1, N_DEV):
                px = 1 - x if r & 4 else x
                py = 1 - y if r & 2 else y
                pc = 1 - c if r & 1 else c
                peer, pidx = (px, py, pc), 4 * px + 2 * py + pc
                k = (N_DEV - 1) * w + r - 1
                sends.append(pltpu.make_async_remote_copy(
                    src_ref=src(w, pidx), dst_ref=out_refs[w].at[me], send_sem=send_sems.at[k],
                    recv_sem=recv_sems.at[k], device_id=peer, device_id_type=pl.DeviceIdType.MESH))
                recvs.append(pltpu.make_async_remote_copy(
                    src_ref=src(w, me), dst_ref=out_refs[w].at[pidx], send_sem=send_sems.at[k],
                    recv_sem=recv_sems.at[k], device_id=peer, device_id_type=pl.DeviceIdType.MESH))
            if p == 0:
                if own is not None:
                    own.start()
                for cp in sends:
                    cp.start()
            else:
                for cp in recvs:
                    cp.wait_recv()
                for cp in sends:
                    cp.wait_send()
                if own is not None:
                    own.wait()


class _SplitScatter:
    def __init__(self, ex, name):
        self.ex, self.name = ex, name

    def _specs(self):
        ex = self.ex
        hbm = pl.BlockSpec(memory_space=pltpu.HBM)
        sem = pl.BlockSpec(memory_space=pltpu.SEMAPHORE)
        effect = pltpu.CompilerParams(has_side_effects=pltpu.SideEffectType.DATAFLOW_SIDE_EFFECTING)
        buffers = [pltpu.HBM(a.shape, a.dtype) for a in ex.operands] + [pltpu.HBM(s.shape, s.dtype) for s in ex.out_shape]
        return hbm, sem, effect, buffers

    def start(self):
        ex, n = self.ex, self.ex.n
        n_sem = (N_DEV - 1) * n
        hbm, sem, effect, buffers = self._specs()
        in_hbm = lambda a: pltpu.with_memory_space_constraint(a, pltpu.HBM)

        me = 4 * lax.axis_index("x") + 2 * lax.axis_index("y") + lax.axis_index("c")
        lands = []
        for w, (a, s) in enumerate(zip(ex.operands, ex.out_shape)):
            mine = lax.dynamic_index_in_dim(a, me, 0, keepdims=True) if w < ex.n_sliced else a[None]
            lands.append(lax.dynamic_update_slice_in_dim(lax.empty(s.shape, s.dtype), mine, me, 0))

        def start_body(*refs):
            ex.phase(0, refs[:n], refs[n:2 * n], (refs[2 * n], refs[2 * n + 1], None))
            refs[-1][...] = jnp.zeros_like(refs[-1])

        self.started = pl.pallas_call(
            start_body, name=self.name + "_start",
            out_shape=[pltpu.SemaphoreType.DMA((n_sem,)), pltpu.SemaphoreType.DMA((n_sem,))] + buffers
            + [jax.ShapeDtypeStruct((8, LANES), F32)],
            in_specs=[hbm] * (2 * n), out_specs=[sem, sem] + [hbm] * (2 * n) + [pl.BlockSpec(memory_space=pltpu.VMEM)],
            input_output_aliases={i: 2 + i for i in range(2 * n)}, compiler_params=effect,
        )(*[in_hbm(a) for a in ex.operands], *[in_hbm(a) for a in lands])
        return self.started[-1]

    def wait(self, after):
        ex, n = self.ex, self.ex.n
        hbm, sem, effect, buffers = self._specs()
        anyspec = pl.BlockSpec(memory_space=pl.ANY)

        def wait_body(*refs):
            ex.phase(2, refs[:n], refs[n:2 * n], (refs[2 * n], refs[2 * n + 1], None))

        done = pl.pallas_call(
            wait_body, name=self.name + "_wait", out_shape=buffers,
            in_specs=[hbm] * (2 * n) + [sem, sem] + [anyspec] * len(after), out_specs=[hbm] * (2 * n),
            input_output_aliases={i: i for i in range(2 * n)}, compiler_params=effect,
        )(*self.started[2:2 + 2 * n], self.started[0], self.started[1], *after)
        return done[n:]


def _exchange_alone(ex, *, name):
    n = ex.n

    def body(*refs):
        for p in range(3):
            ex.phase(p, refs[:n], refs[n:2 * n], refs[2 * n:])

    anyspec = pl.BlockSpec(memory_space=pl.ANY)
    return pl.pallas_call(body, name=name, out_shape=ex.out_shape, in_specs=[anyspec] * n,
                          out_specs=[anyspec] * n, scratch_shapes=ex.scratch)(*ex.operands)


def _adam_step(w_ref, p_ref, m_ref, v_ref, g_ref, d_ref, nm_ref, nv_ref):
    g = p_ref[0].astype(F32)
    for j in range(1, N_DEV):
        g = g + p_ref[j].astype(F32)
    g_ref[...] = g
    nm = ADAM_B1 * m_ref[...] + (1.0 - ADAM_B1) * g
    nv = ADAM_B2 * v_ref[...] + (1.0 - ADAM_B2) * (g * g)
    nm_ref[...] = nm
    nv_ref[...] = nv
    m_hat = nm / (1.0 - ADAM_B1 ** ADAM_STEP)
    v_hat = nv / (1.0 - ADAM_B2 ** ADAM_STEP)
    d_ref[...] = -ADAM_LR * (m_hat / (jnp.sqrt(v_hat) + ADAM_EPS) + ADAM_WD * w_ref[...])


def _adamw_vectors(ws, parts, ms, vs, *, name):
    n = len(ws)

    def body(*refs):
        w_refs, p_refs, m_refs, v_refs = (refs[i * n:(i + 1) * n] for i in range(4))
        outs = refs[4 * n:]
        for i in range(n):
            _adam_step(w_refs[i], p_refs[i], m_refs[i], v_refs[i], *outs[4 * i:4 * i + 4])

    return pl.pallas_call(
        body, name=name,
        out_shape=[jax.ShapeDtypeStruct(w.shape, F32) for w in ws for _ in range(4)],
    )(*ws, *parts, *ms, *vs)


def _adamw(w, parts, m, v, after, *, name):
    G, R, n = w.shape
    tn = 256 if (n > 256 and n % 256 == 0) else n
    tr = R
    for t in range(16, R, 16):
        if R % t == 0 and t * tn <= 160 * 1024:
            tr = t
    if R * tn <= 160 * 1024:
        tr = R

    def body(w_ref, p_ref, m_ref, v_ref, after_ref, g_ref, d_ref, nm_ref, nv_ref):
        _adam_step(w_ref, p_ref, m_ref, v_ref, g_ref, d_ref, nm_ref, nv_ref)

    blk = pl.BlockSpec((None, tr, tn), lambda g, i, j: (g, i, j))
    return pl.pallas_call(
        body, name=name, grid=(G, R // tr, n // tn),
        in_specs=[blk, pl.BlockSpec((N_DEV, None, tr, tn), lambda g, i, j: (0, g, i, j)), blk, blk,
                  pl.BlockSpec((8, LANES), lambda g, i, j: (0, 0))],
        out_specs=[blk, blk, blk, blk],
        out_shape=[jax.ShapeDtypeStruct((G, R, n), F32)] * 4,
        compiler_params=_params(("parallel", "parallel", "parallel")),
    )(w, parts, m, v, after)


def _pad_last(a, width):
    return jnp.pad(a, [(0, 0)] * (a.ndim - 1) + [(0, width - a.shape[-1])])


def _cols_of(g):
    return g.transpose(1, 0, 2).reshape(g.shape[1], N_DEV * g.shape[2])


def _col_shards(w):
    return w.reshape(w.shape[0], N_DEV, w.shape[1] // N_DEV).transpose(1, 0, 2)


def kernel(x, positions, ffn1_pre_w, ffn1_w1, ffn1_w2, ffn1_post_w, mix_pre_w, w_in, mla_q_norm_w, mla_w_uq, mla_kv_norm_w, mla_w_ukv, ret_gn_w, w_branch_mla, w_branch_ret, w_out, mix_post_w, ffn2_pre_w, ffn2_w1, ffn2_w2, ffn2_post_w, loss_target, m_ffn1_pre_w, m_ffn1_w1, m_ffn1_w2, m_ffn1_post_w, m_mix_pre_w, m_w_in, m_mla_q_norm_w, m_mla_w_uq, m_mla_kv_norm_w, m_mla_w_ukv, m_ret_gn_w, m_w_branch_mla, m_w_branch_ret, m_w_out, m_mix_post_w, m_ffn2_pre_w, m_ffn2_w1, m_ffn2_w2, m_ffn2_post_w, v_ffn1_pre_w, v_ffn1_w1, v_ffn1_w2, v_ffn1_post_w, v_mix_pre_w, v_w_in, v_mla_q_norm_w, v_mla_w_uq, v_mla_kv_norm_w, v_mla_w_ukv, v_ret_gn_w, v_w_branch_mla, v_w_branch_ret, v_w_out, v_mix_post_w, v_ffn2_pre_w, v_ffn2_w1, v_ffn2_w2, v_ffn2_post_w):
    T, D = x.shape[1], x.shape[2]
    h0 = x[0]
    tgt = loss_target[0]
    pos = positions.reshape(T, 1).astype(F32)

    big = [("ffn1_w1", ffn1_w1, m_ffn1_w1, v_ffn1_w1), ("ffn1_w2", ffn1_w2, m_ffn1_w2, v_ffn1_w2),
           ("w_in", w_in, m_w_in, v_w_in), ("mla_w_uq", mla_w_uq, m_mla_w_uq, v_mla_w_uq),
           ("mla_w_ukv", mla_w_ukv, m_mla_w_ukv, v_mla_w_ukv),
           ("w_branch_mla", w_branch_mla, m_w_branch_mla, v_w_branch_mla),
           ("w_branch_ret", w_branch_ret, m_w_branch_ret, v_w_branch_ret),
           ("w_out", w_out, m_w_out, v_w_out),
           ("ffn2_w1", ffn2_w1, m_ffn2_w1, v_ffn2_w1), ("ffn2_w2", ffn2_w2, m_ffn2_w2, v_ffn2_w2)]
    small = [("ffn1_pre_w", ffn1_pre_w, m_ffn1_pre_w, v_ffn1_pre_w), ("ffn1_post_w", ffn1_post_w, m_ffn1_post_w, v_ffn1_post_w),
             ("mix_pre_w", mix_pre_w, m_mix_pre_w, v_mix_pre_w), ("mla_q_norm_w", mla_q_norm_w, m_mla_q_norm_w, v_mla_q_norm_w),
             ("mla_kv_norm_w", mla_kv_norm_w, m_mla_kv_norm_w, v_mla_kv_norm_w), ("ret_gn_w", ret_gn_w, m_ret_gn_w, v_ret_gn_w),
             ("mix_post_w", mix_post_w, m_mix_post_w, v_mix_post_w), ("ffn2_pre_w", ffn2_pre_w, m_ffn2_pre_w, v_ffn2_pre_w),
             ("ffn2_post_w", ffn2_post_w, m_ffn2_post_w, v_ffn2_post_w)]

    half = ffn1_w2.shape[1]
    hp = -(-half // LANES) * LANES

    def rows_view(w):
        return w[0].T

    def send_w1(w):
        return jnp.pad(rows_view(w).reshape(2, half, D), ((0, 0), (0, hp - half), (0, 0))).reshape(2 * hp, D).astype(BF16)

    def send_w2(w):
        return jnp.pad(w[0], ((0, hp - half), (0, 0))).astype(BF16)

    mixer = ["w_in", "mla_w_uq", "mla_w_ukv", "w_branch_mla", "w_branch_ret", "w_out"]
    uq_w = MLA_NOPE + MLA_ROPE
    mixer_send = [rows_view(w_in).astype(BF16), jnp.pad(rows_view(mla_w_uq), ((0, HP - uq_w), (0, 0))).astype(BF16),
                  mla_w_ukv[0].astype(BF16), w_branch_mla[0].astype(BF16), w_branch_ret[0].astype(BF16),
                  w_out[0].astype(BF16)]

    w1a, w2a = _exchange_alone(_Gather([send_w1(ffn1_w1), send_w2(ffn1_w2)]), name="gather_ffn1")
    w2a = w2a.reshape(N_DEV // 2, 2 * hp, D)
    u1, f1, h1, *got = _ffn_fwd(h0, ffn1_pre_w, w1a, w2a, ffn1_post_w, None, name="ffn1_fwd_gather_mixer",
                                exchange=_Gather(mixer_send))
    fw = dict(zip(mixer, got))

    wi = fw["w_in"].reshape(-1, D)
    cq_w, ckv_w, kr_w = wi[0:384], wi[384:640], wi[640:672]
    rq_w, rk_w = wi[672:928], wi[928:1184]
    rv_w, rg_w = wi[1184:1696], wi[1696:2208]
    gm_w, gr_w = wi[2208:2208 + D], wi[2208 + D:2208 + 2 * D]
    zer = lambda n: jnp.zeros((n, D), BF16)
    head_rows = lambda a, h: jnp.pad(a.reshape(h, -1, D), ((0, 0), (0, HP - a.shape[0] // h), (0, 0))).reshape(h * HP, D)
    w_in_p = jnp.concatenate([head_rows(rq_w, RET_HEADS), head_rows(rk_w, RET_HEADS), rv_w, rg_w,
                              cq_w, ckv_w, zer(MLA_NOPE), kr_w, zer(HP - MLA_NOPE - MLA_ROPE), zer(AW - 768),
                              gm_w, gr_w], axis=0)
    w_uq_p = fw["mla_w_uq"].reshape(QW, MLA_Q_RANK)
    ukv = fw["mla_w_ukv"].transpose(1, 0, 2)
    w_kv_p = jnp.concatenate([_pad_last(ukv[:, :, :MLA_NOPE], HP).reshape(MLA_KV_RANK, QW),
                              _pad_last(ukv[:, :, MLA_NOPE:], HP).reshape(MLA_KV_RANK, QW)], axis=1)
    w_bm_p = jnp.pad(_cols_of(fw["w_branch_mla"]).reshape(MLA_HEADS, MLA_V, D),
                     ((0, 0), (0, HP - MLA_V), (0, 0))).reshape(QW, D)
    w_br, w_o = _cols_of(fw["w_branch_ret"]), fw["w_out"].reshape(D, D)
    cs_mla, cs_ret = _rope_tables(pos, [_rope_table(MLA_NOPE, MLA_ROPE // 2), _rope_table(0, RET_DK // 2)],
                                  name="rope_tables")

    proj, a1 = _rms_matmul(h1, mix_pre_w, w_in_p, name="mixer_in_proj")
    q, k, v, qn, kvn = _mla_prep_fwd(proj, cs_mla, mla_q_norm_w, mla_kv_norm_w, w_uq_p, w_kv_p, name="mla_prep_fwd")
    o, lse, w1b, w2b = _flash_fwd(q, k, v, name="mla_attn_fwd_gather_ffn2",
                                  exchange=_Gather([send_w1(ffn2_w1), send_w2(ffn2_w2)]))
    w2b = w2b.reshape(N_DEV // 2, 2 * hp, D)
    ypre, yn, rprev = _ret_fwd(proj, cs_ret, name="retention_fwd")
    omla, oret, m, h2 = _merge_fwd(o, yn, proj, ret_gn_w, w_bm_p, w_br, w_o, h1, mix_post_w, name="merge_fwd")
    u2, f2, _, dy, lossp = _ffn_fwd(h2, ffn2_pre_w, w1b, w2b, ffn2_post_w, tgt, name="ffn2_fwd_loss")
    loss = lax.psum(jnp.sum(lossp[::8, 0]), ("x", "y", "c"))

    def grad(x, dy, tag, after=None):
        return _matmul_tn(x if x.ndim == 3 else x[None], dy if dy.ndim == 3 else dy[None], name=tag, after=after)

    g2, du2, df2, a2, dh2, gpost2, gpre2 = _ffn_bwd(dy, f2, ffn2_post_w, h2, ffn2_pre_w, u2, w2b, w1b, name="ffn2_bwd")
    dw1b, = grad(du2.reshape(N_DEV, T, 2 * hp), a2, "ffn2_dw1")
    dw2b = grad(g2, df2, "ffn2_dw2")[0].reshape(N_DEV, hp, D)
    (dmb, merged, dgm, dgr, domla, do, delta, doret, gated, drg, dyn, gpostm, ggn) = _merge_bwd(
        dh2, m, mix_post_w, omla, oret, proj, yn, ret_gn_w, o, w_o, w_bm_p, w_br, name="merge_bwd")
    dw_out = grad(merged, dmb, "dw_out")[0][0]
    dw_bm_p = grad(o, domla, "dw_branch_mla")[0][0]
    dw_br = grad(gated, doret, "dw_branch_ret")[0][0]
    sc_ffn2 = _SplitScatter(_Scatter([dw1b, dw2b]), "scatter_ffn2")
    dq, dk, dv = _flash_bwd(q, k, v, do, lse, delta, name="mla_attn_bwd", after=sc_ffn2.start())
    da, dql, dkvl, gqn, gkvn = _mla_prep_bwd(dq, dk, dv, proj, cs_mla, mla_q_norm_w, mla_kv_norm_w, w_uq_p, w_kv_p, name="mla_prep_bwd")
    dw_uq_p = grad(dql, qn, "dw_uq")[0][0]
    dw_kv_p = grad(kvn, dkvl, "dw_ukv")[0][0]
    drq, drk, drv = _ret_bwd(dyn, ypre, proj, cs_ret, rprev, name="retention_bwd")
    dproj = jnp.concatenate([drq, drk, drv, drg, da, dgm, dgr], axis=1)
    dw_in_p = grad(dproj, a1, "dw_in")[0][0]

    dw_uq = dw_uq_p.reshape(MLA_HEADS, HP, MLA_Q_RANK)[:, :uq_w]
    dkp = dw_kv_p[:, :QW].reshape(MLA_KV_RANK, MLA_HEADS, HP)[:, :, :MLA_NOPE]
    dvp = dw_kv_p[:, QW:].reshape(MLA_KV_RANK, MLA_HEADS, HP)[:, :, :MLA_V]
    dw_ukv = jnp.concatenate([dkp, dvp], axis=2).transpose(1, 0, 2)
    dw_bm = dw_bm_p.reshape(MLA_HEADS, HP, D)[:, :MLA_V].reshape(MLA_HEADS * MLA_V, D)
    small_mixer_grads = [dw_uq, dw_ukv, _col_shards(dw_bm), _col_shards(dw_br), dw_out.reshape(N_DEV, D // N_DEV, D)]
    sc_small = _SplitScatter(_Scatter(small_mixer_grads), "scatter_mixer_small")
    dh1, gmixpre = _proj_bwd(dproj, w_in_p, h1, mix_pre_w, dh2, name="mixer_in_bwd", after=sc_small.start())
    unhead = lambda a, h, wd: a.reshape(h, HP, D)[:, :wd].reshape(h * wd, D)
    c0 = 4 * RW
    dw_in = jnp.concatenate([
        dw_in_p[c0:c0 + 384], dw_in_p[c0 + 384:c0 + 640], dw_in_p[c0 + 640 + MLA_NOPE:c0 + 640 + MLA_NOPE + MLA_ROPE],
        unhead(dw_in_p[0:RW], RET_HEADS, RET_DK), unhead(dw_in_p[RW:2 * RW], RET_HEADS, RET_DK),
        dw_in_p[2 * RW:3 * RW], dw_in_p[3 * RW:4 * RW],
        dw_in_p[PROJ_FIXED:PROJ_FIXED + D], dw_in_p[PROJ_FIXED + D:PROJ_FIXED + 2 * D]], axis=0).reshape(N_DEV, -1, D)
    sc_w_in = _SplitScatter(_Scatter([dw_in]), "scatter_w_in")
    g1, du1, df1, a0, dx, gpost1, gpre1 = _ffn_bwd(
        dh1, f1, ffn1_post_w, h0, ffn1_pre_w, u1, w2a, w1a, name="ffn1_bwd", after=sc_w_in.start())
    dw2a = grad(g1, df1, "ffn1_dw2")[0].reshape(N_DEV, hp, D)
    sc_dw2a = _SplitScatter(_Scatter([dw2a]), "scatter_ffn1_dw2")
    dw1a, = grad(du1.reshape(N_DEV, T, 2 * hp), a0, "ffn1_dw1", after=sc_dw2a.start())

    small_g = {"ffn1_pre_w": gpre1, "ffn1_post_w": gpost1, "mix_pre_w": gmixpre, "mla_q_norm_w": gqn,
               "mla_kv_norm_w": gkvn, "ret_gn_w": ggn, "mix_post_w": gpostm, "ffn2_pre_w": gpre2, "ffn2_post_w": gpost2}
    sc_last = _SplitScatter(_Scatter([dw1a], whole=[small_g[nm] for nm, *_ in small]), "scatter_ffn1_dw1")
    token = sc_last.start()
    recv_ffn2 = sc_ffn2.wait([token])
    recv_mixer = sc_w_in.wait([token]) + sc_small.wait([token])
    recv_w2a, = sc_dw2a.wait([token])
    parts = dict(zip(mixer, recv_mixer))
    parts.update(ffn1_w2=recv_w2a, ffn2_w1=recv_ffn2[0], ffn2_w2=recv_ffn2[1])
    as_is = (lambda a: a, lambda p: p[:, None], lambda a: a)
    views = {nm: as_is for nm, *_ in big}
    for nm in ("ffn1_w1", "ffn2_w1"):
        views[nm] = (lambda a: rows_view(a).reshape(2, half, D), lambda p: p.reshape(N_DEV, 2, hp, D),
                     lambda a: a.reshape(2 * half, D).T[None])
    for nm in ("w_in", "mla_w_uq"):
        views[nm] = (lambda a: rows_view(a)[None], lambda p: p[:, None], lambda a: a[0].T[None])

    def update(nm, w, m_, v_, after):
        to_view, parts_view, back = views[nm]
        return [back(a) for a in _adamw(to_view(w), parts_view(parts[nm]), to_view(m_), to_view(v_), after,
                                        name="adamw_" + nm)]

    big_out = {nm: update(nm, w, m_, v_, token) for nm, w, m_, v_ in big if nm != "ffn1_w1"}
    recv_w1a, *small_parts = sc_last.wait([d[0] for d in big_out.values()])
    parts["ffn1_w1"] = recv_w1a
    big_out["ffn1_w1"] = update("ffn1_w1", ffn1_w1, m_ffn1_w1, v_ffn1_w1, jnp.zeros((8, LANES), F32))
    small_out = _adamw_vectors([w for _, w, _, _ in small], small_parts, [a for _, _, a, _ in small],
                               [a for _, _, _, a in small], name="adamw_replicated")

    order = ["ffn1_pre_w", "ffn1_w1", "ffn1_w2", "ffn1_post_w", "mix_pre_w", "w_in", "mla_q_norm_w", "mla_w_uq",
             "mla_kv_norm_w", "mla_w_ukv", "ret_gn_w", "w_branch_mla", "w_branch_ret", "w_out", "mix_post_w",
             "ffn2_pre_w", "ffn2_w1", "ffn2_w2", "ffn2_post_w"]
    outs = [loss, dx[None]]
    for i in range(4):
        both = {nm: big_out[nm][i] for nm in big_out}
        both.update({nm: small_out[4 * j + i] for j, (nm, *_) in enumerate(small)})
        outs += [both[nm] for nm in order]
    return tuple(outs)
```

```python
import math

import numpy as np
import jax
import jax.numpy as jnp
from jax import lax
from jax.experimental import pallas as pl
from jax.experimental.pallas import tpu as pltpu

F32, BF16 = jnp.float32, jnp.bfloat16

MLA_HEADS, MLA_NOPE, MLA_ROPE, MLA_V = 8, 64, 32, 64
MLA_Q_RANK, MLA_KV_RANK = 384, 256
RET_HEADS, RET_DK, RET_DV = 4, 64, 128
ROPE_BASE, NORM_EPS, GN_EPS = 10000.0, 1e-6, 1e-6
ADAM_LR, ADAM_B1, ADAM_B2, ADAM_EPS, ADAM_WD, ADAM_STEP = 0.001, 0.9, 0.999, 1e-08, 0.01, 10
ATTN_SCALE = 1.0 / math.sqrt(MLA_NOPE + MLA_ROPE)

N_DEV = 8
LANES = 128
HP = LANES
QW = MLA_HEADS * HP
RW = RET_HEADS * HP
AW = 1024
PROJ_FIXED = 4 * RW + AW
NEG = -1e30

TOKEN_TILE = 512
ATTN_TILE = 1024
ATTN_CHAINS = 2
FFN_CHAINS = 2
RET_TILE = 256
PROJ_TILE_CAP = 2560
GRAD_TILE_CAP = 1408
GRAD_TOKEN_TILE = 2048
MERGE_TILE = 256
VMEM_LIMIT = 56 * 1024 * 1024


def _tile(n, cap, mult=LANES):
    if n <= cap:
        return n
    best = None
    for t in range(mult, cap + 1, mult):
        if n % t == 0:
            best = t
    assert best is not None, (n, cap, mult)
    return best


def _params(sem):
    return pltpu.CompilerParams(dimension_semantics=sem, vmem_limit_bytes=VMEM_LIMIT)


def _dot(a, b):
    return lax.dot_general(a, b, (((1,), (0,)), ((), ())), preferred_element_type=F32)


def _dot_nt(a, b):
    return lax.dot_general(a, b, (((1,), (1,)), ((), ())), preferred_element_type=F32)


def _dot_tn(a, b):
    return lax.dot_general(a, b, (((0,), (0,)), ((), ())), preferred_element_type=F32)


def _sigmoid(x):
    return pl.reciprocal(1.0 + jnp.exp(-x), approx=True)


def _rms_fwd(x, w):
    r = lax.rsqrt(jnp.mean(x * x, axis=-1, keepdims=True) + NORM_EPS)
    return x * r * w


def _rms_bwd(x, w, dy):
    r = lax.rsqrt(jnp.mean(x * x, axis=-1, keepdims=True) + NORM_EPS)
    xh = x * r
    g = dy * w
    dx = r * (g - xh * jnp.mean(g * xh, axis=-1, keepdims=True))
    return dx, jnp.sum(dy * xh, axis=0, keepdims=True)


def _rope_table(first, half):
    inv = (np.float32(ROPE_BASE) ** (-(np.arange(half, dtype=np.float32) / np.float32(half)))).astype(np.float32)
    tab = np.zeros((8, LANES), np.float32)
    tab[0, first:first + half] = inv
    tab[0, first + half:first + 2 * half] = inv
    tab[1, first:first + half] = -1.0
    tab[2, first + half:first + 2 * half] = 1.0
    return jnp.asarray(tab)


def _rope_cs(pos, tab_ref):
    ang = pos * tab_ref[0:1, :]
    s = jnp.sin(ang)
    return jnp.cos(ang), s * tab_ref[1:2, :], s * tab_ref[2:3, :]


def _rope(x, cs, half, inverse=False):
    c, s1, s2 = cs
    a = pltpu.roll(x, LANES - half, 1) * s1 + pltpu.roll(x, half, 1) * s2
    return x * c - a if inverse else x * c + a


def _call(body, *, name, grid, in_specs, out_specs, out_shape, scratch_shapes, args, exchange=None, after=None):
    sem = ("arbitrary",) * len(grid)
    anyspec = pl.BlockSpec(memory_space=pl.ANY)
    if exchange is None and after is not None:
        n_own = len(in_specs)

        def behind(*refs):
            body(*refs[:n_own], *refs[n_own + 1:])

        return pl.pallas_call(behind, name=name, grid=grid, in_specs=list(in_specs) + [anyspec], out_specs=out_specs,
                              out_shape=out_shape, scratch_shapes=scratch_shapes, compiler_params=_params(sem))(*args, after)
    if exchange is None:
        return pl.pallas_call(body, name=name, grid=grid, in_specs=in_specs, out_specs=out_specs,
                              out_shape=out_shape, scratch_shapes=scratch_shapes, compiler_params=_params(sem))(*args)
    n_in, n_out, e = len(in_specs), len(out_specs), exchange.n
    total = math.prod(grid)

    def carried(*refs):
        own = refs[:n_in] + refs[n_in + e:n_in + e + n_out] + refs[n_in + 2 * e + n_out:len(refs) - 3]
        ex_refs = (refs[n_in:n_in + e], refs[n_in + e + n_out:n_in + 2 * e + n_out], refs[len(refs) - 3:])
        step = pl.program_id(0)
        for d in range(1, len(grid)):
            step = step * grid[d] + pl.program_id(d)

        @pl.when(step == 0)
        def _():
            exchange.phase(0, *ex_refs)

        @pl.when(step == (3 * total) // 4)
        def _():
            exchange.phase(1, *ex_refs)

        body(*own)

        @pl.when(step == total - 1)
        def _():
            exchange.phase(2, *ex_refs)

    return pl.pallas_call(
        carried, name=name, grid=grid, in_specs=list(in_specs) + [anyspec] * e,
        out_specs=list(out_specs) + [anyspec] * e, out_shape=list(out_shape) + exchange.out_shape,
        scratch_shapes=list(scratch_shapes) + exchange.scratch, compiler_params=_params(sem),
    )(*args, *exchange.operands)


def _ffn_fwd(h, pre_w, w1, w2, post_w, target, *, name, exchange=None):
    T, D = h.shape
    nk, ck = w2.shape[0], w2.shape[1]
    tT = min(TOKEN_TILE, T)
    nT = T // tT
    with_loss = target is not None

    def body(*refs):
        if with_loss:
            (h_ref, pre_ref, w1g_ref, w1u_ref, w2_ref, post_ref, tgt_ref,
             u_ref, f_ref, ho_ref, dy_ref, loss_ref, a_s, acc) = refs
        else:
            (h_ref, pre_ref, w1g_ref, w1u_ref, w2_ref, post_ref,
             u_ref, f_ref, ho_ref, a_s, acc) = refs
        k = pl.program_id(1)

        @pl.when(k == 0)
        def _():
            a_s[...] = _rms_fwd(h_ref[...], pre_ref[...]).astype(BF16)
            acc[...] = jnp.zeros_like(acc)

        for c in range(FFN_CHAINS):
            rs = slice(c * (tT // FFN_CHAINS), (c + 1) * (tT // FFN_CHAINS))
            a = a_s[rs, :]
            ug = _dot_nt(a, w1g_ref[...])
            uu = _dot_nt(a, w1u_ref[...])
            u_ref[0, rs, :] = ug.astype(BF16)
            u_ref[1, rs, :] = uu.astype(BF16)
            acc[rs, :] += _dot((ug * _sigmoid(ug) * uu).astype(BF16), w2_ref[...])

        @pl.when(k == nk - 1)
        def _():
            f = acc[...]
            f_ref[...] = f
            ho = h_ref[...] + 0.5 * _rms_fwd(f, post_ref[...])
            ho_ref[...] = ho
            if with_loss:
                e = ho - tgt_ref[...]
                dy_ref[...] = e * (1.0 / D)
                loss_ref[...] = jnp.full(loss_ref.shape, (0.5 / D) * jnp.sum(e * e), F32)

    row = pl.BlockSpec((tT, D), lambda i, k: (i, 0))
    vec = pl.BlockSpec((1, D), lambda i, k: (0, 0))
    in_specs = [row, vec,
                pl.BlockSpec((None, ck, D), lambda i, k: (k, 0, 0)),
                pl.BlockSpec((None, ck, D), lambda i, k: (nk + k, 0, 0)),
                pl.BlockSpec((None, ck, D), lambda i, k: (k, 0, 0)),
                vec]
    out_shape = [jax.ShapeDtypeStruct((2, nk, T, ck), BF16),
                 jax.ShapeDtypeStruct((T, D), F32),
                 jax.ShapeDtypeStruct((T, D), F32)]
    out_specs = [pl.BlockSpec((2, None, tT, ck), lambda i, k: (0, k, i, 0)), row, row]
    args = [h, pre_w, w1, w1, w2, post_w]
    if with_loss:
        in_specs.append(row)
        args.append(target)
        out_shape += [jax.ShapeDtypeStruct((T, D), F32), jax.ShapeDtypeStruct((nT * 8, LANES), F32)]
        out_specs += [row, pl.BlockSpec((8, LANES), lambda i, k: (i, 0))]
    return _call(body, name=name, grid=(nT, nk), in_specs=in_specs, out_specs=out_specs, out_shape=out_shape,
                 scratch_shapes=[pltpu.VMEM((tT, D), BF16), pltpu.VMEM((tT, D), F32)], args=args, exchange=exchange)


def _ffn_bwd(dho, f, post_w, h, pre_w, u, w2, w1, *, name, exchange=None, after=None):
    T, D = h.shape
    nk, ck = w2.shape[0], w2.shape[1]
    tT = min(TOKEN_TILE, T)
    nT = T // tT

    def body(dho_ref, f_ref, post_ref, h_ref, pre_ref, u_ref, w2_ref, w1g_ref, w1u_ref,
             g_ref, du_ref, df_ref, a_ref, dh_ref, gpost_ref, gpre_ref, df_s, da_acc):
        i, k = pl.program_id(0), pl.program_id(1)

        @pl.when(jnp.logical_and(i == 0, k == 0))
        def _():
            gpost_ref[...] = jnp.zeros_like(gpost_ref)
            gpre_ref[...] = jnp.zeros_like(gpre_ref)

        @pl.when(k == 0)
        def _():
            dx, dw = _rms_bwd(f_ref[...], post_ref[...], 0.5 * dho_ref[...])
            dfb = dx.astype(BF16)
            df_s[...] = dfb
            df_ref[...] = dfb
            gpost_ref[...] += dw
            a_ref[...] = _rms_fwd(h_ref[...], pre_ref[...]).astype(BF16)
            da_acc[...] = jnp.zeros_like(da_acc)

        groups = [slice(c * (tT // FFN_CHAINS), (c + 1) * (tT // FFN_CHAINS)) for c in range(FFN_CHAINS)]
        dgs = [_dot_nt(df_s[rs, :], w2_ref[...]) for rs in groups]
        for rs, dg in zip(groups, dgs):
            ug = u_ref[0, rs, :].astype(F32)
            uu = u_ref[1, rs, :].astype(F32)
            sg = _sigmoid(ug)
            sl = ug * sg
            g_ref[rs, :] = (sl * uu).astype(BF16)
            dug = (dg * uu * (sg + sl * (1.0 - sg))).astype(BF16)
            duu = (dg * sl).astype(BF16)
            du_ref[0, rs, :] = dug
            du_ref[1, rs, :] = duu
            da_acc[rs, :] += _dot(dug, w1g_ref[...]) + _dot(duu, w1u_ref[...])

        @pl.when(k == nk - 1)
        def _():
            dx, dw = _rms_bwd(h_ref[...], pre_ref[...], da_acc[...])
            dh_ref[...] = dho_ref[...] + dx
            gpre_ref[...] += dw

    row = pl.BlockSpec((tT, D), lambda i, k: (i, 0))
    vec = pl.BlockSpec((1, D), lambda i, k: (0, 0))
    return _call(
        body, name=name, grid=(nT, nk),
        in_specs=[row, row, vec, row, vec,
                  pl.BlockSpec((2, None, tT, ck), lambda i, k: (0, k, i, 0)),
                  pl.BlockSpec((None, ck, D), lambda i, k: (k, 0, 0)),
                  pl.BlockSpec((None, ck, D), lambda i, k: (k, 0, 0)),
                  pl.BlockSpec((None, ck, D), lambda i, k: (nk + k, 0, 0))],
        out_specs=[pl.BlockSpec((None, tT, ck), lambda i, k: (k, i, 0)),
                   pl.BlockSpec((2, None, tT, ck), lambda i, k: (0, k, i, 0)),
                   row, row, row, vec, vec],
        out_shape=[jax.ShapeDtypeStruct((nk, T, ck), BF16),
                   jax.ShapeDtypeStruct((2, nk, T, ck), BF16),
                   jax.ShapeDtypeStruct((T, D), BF16),
                   jax.ShapeDtypeStruct((T, D), BF16),
                   jax.ShapeDtypeStruct((T, D), F32),
                   jax.ShapeDtypeStruct((1, D), F32),
                   jax.ShapeDtypeStruct((1, D), F32)],
        scratch_shapes=[pltpu.VMEM((tT, D), BF16), pltpu.VMEM((tT, D), F32)],
        args=(dho, f, post_w, h, pre_w, u, w2, w1, w1), exchange=exchange, after=after)


def _matmul_tn(x, dy, *, name, exchange=None, after=None):
    Px, T, K = x.shape
    Py, _, N = dy.shape
    P = max(Px, Py)
    tT, tK, tN = min(GRAD_TOKEN_TILE, T), _tile(K, GRAD_TILE_CAP), _tile(N, GRAD_TILE_CAP)
    nt = T // tT

    def body(x_ref, dy_ref, o_ref, acc):
        t = pl.program_id(3)

        @pl.when(t == 0)
        def _():
            acc[...] = jnp.zeros_like(acc)

        acc[...] += _dot_tn(x_ref[...], dy_ref[...])

        @pl.when(t == nt - 1)
        def _():
            o_ref[...] = acc[...].astype(BF16)

    return _call(
        body, name=name, grid=(P, K // tK, N // tN, nt),
        in_specs=[pl.BlockSpec((None, tT, tK), lambda p, a, b, t: (p if Px > 1 else 0, t, a)),
                  pl.BlockSpec((None, tT, tN), lambda p, a, b, t: (p if Py > 1 else 0, t, b))],
        out_specs=[pl.BlockSpec((None, tK, tN), lambda p, a, b, t: (p, a, b))],
        out_shape=[jax.ShapeDtypeStruct((P, K, N), BF16)],
        scratch_shapes=[pltpu.VMEM((tK, tN), F32)], args=(x, dy), exchange=exchange, after=after)


def _rms_matmul(h, wn, w, *, name):
    T, D = h.shape
    N = w.shape[0]
    tT, tN = min(TOKEN_TILE, T), _tile(N, PROJ_TILE_CAP)

    def body(h_ref, wn_ref, w_ref, y_ref, a_ref):
        @pl.when(pl.program_id(1) == 0)
        def _():
            a_ref[...] = _rms_fwd(h_ref[...], wn_ref[...]).astype(BF16)

        y_ref[...] = _dot_nt(a_ref[...], w_ref[...]).astype(BF16)

    return pl.pallas_call(
        body, name=name, grid=(T // tT, N // tN),
        in_specs=[pl.BlockSpec((tT, D), lambda i, j: (i, 0)),
                  pl.BlockSpec((1, D), lambda i, j: (0, 0)),
                  pl.BlockSpec((tN, D), lambda i, j: (j, 0))],
        out_specs=[pl.BlockSpec((tT, tN), lambda i, j: (i, j)),
                   pl.BlockSpec((tT, D), lambda i, j: (i, 0))],
        out_shape=[jax.ShapeDtypeStruct((T, N), BF16), jax.ShapeDtypeStruct((T, D), BF16)],
        compiler_params=_params(("parallel", "arbitrary")),
    )(h, wn, w)


def _proj_bwd(dproj, w, h, wn, dres, *, name, exchange=None, after=None):
    T, D = h.shape
    N = w.shape[0]
    tT, tN = min(TOKEN_TILE, T), _tile(N, PROJ_TILE_CAP)
    nn = N // tN

    def body(dp_ref, w_ref, h_ref, wn_ref, dres_ref, dh_ref, gw_ref, acc):
        i, j = pl.program_id(0), pl.program_id(1)

        @pl.when(jnp.logical_and(i == 0, j == 0))
        def _():
            gw_ref[...] = jnp.zeros_like(gw_ref)

        @pl.when(j == 0)
        def _():
            acc[...] = jnp.zeros_like(acc)

        acc[...] += _dot(dp_ref[...], w_ref[...])

        @pl.when(j == nn - 1)
        def _():
            dx, dw = _rms_bwd(h_ref[...], wn_ref[...], acc[...])
            dh_ref[...] = dres_ref[...] + dx
            gw_ref[...] += dw

    row = pl.BlockSpec((tT, D), lambda i, j: (i, 0))
    vec = pl.BlockSpec((1, D), lambda i, j: (0, 0))
    return _call(
        body, name=name, grid=(T // tT, nn),
        in_specs=[pl.BlockSpec((tT, tN), lambda i, j: (i, j)),
                  pl.BlockSpec((tN, D), lambda i, j: (j, 0)), row, vec, row],
        out_specs=[row, vec],
        out_shape=[jax.ShapeDtypeStruct((T, D), F32), jax.ShapeDtypeStruct((1, D), F32)],
        scratch_shapes=[pltpu.VMEM((tT, D), F32)], args=(dproj, w, h, wn, dres), exchange=exchange, after=after)


def _mla_prep_fwd(proj, pos, qn_w, kvn_w, w_uq, w_kv, tab, *, name):
    T = proj.shape[0]
    tT = min(TOKEN_TILE, T)
    a_blk = PROJ_FIXED // AW - 1

    def body(a_ref, pos_ref, qnw_ref, kvnw_ref, wuq_ref, wkv_ref, tab_ref,
             q_ref, k_ref, v_ref, qn_ref, kvn_ref):
        cq = a_ref[:, 0:MLA_Q_RANK].astype(F32)
        ckv = a_ref[:, MLA_Q_RANK:MLA_Q_RANK + MLA_KV_RANK].astype(F32)
        kr = a_ref[:, 640:768].astype(F32)
        qn = _rms_fwd(cq, qnw_ref[...]).astype(BF16)
        kvn = _rms_fwd(ckv, kvnw_ref[...]).astype(BF16)
        qn_ref[...] = qn
        kvn_ref[...] = kvn
        cs = _rope_cs(pos_ref[...], tab_ref)
        q = _dot_nt(qn, wuq_ref[...])
        kv = _dot(kvn, wkv_ref[...])
        krr = _rope(kr, cs, MLA_ROPE // 2)
        for hd in range(MLA_HEADS):
            sl = slice(hd * HP, (hd + 1) * HP)
            q_ref[:, sl] = (_rope(q[:, sl], cs, MLA_ROPE // 2) * ATTN_SCALE).astype(BF16)
            k_ref[:, sl] = (kv[:, sl] + krr).astype(BF16)
        v_ref[...] = kv[:, QW:].astype(BF16)

    def full(r, c):
        return pl.BlockSpec((r, c), lambda i: (0, 0))

    def rows(c):
        return pl.BlockSpec((tT, c), lambda i: (i, 0))

    return pl.pallas_call(
        body, name=name, grid=(T // tT,),
        in_specs=[pl.BlockSpec((tT, AW), lambda i: (i, a_blk)), rows(1),
                  full(1, MLA_Q_RANK), full(1, MLA_KV_RANK),
                  full(QW, MLA_Q_RANK), full(MLA_KV_RANK, 2 * QW), full(8, LANES)],
        out_specs=[rows(QW), rows(QW), rows(QW), rows(MLA_Q_RANK), rows(MLA_KV_RANK)],
        out_shape=[jax.ShapeDtypeStruct((T, QW), BF16)] * 3
        + [jax.ShapeDtypeStruct((T, MLA_Q_RANK), BF16), jax.ShapeDtypeStruct((T, MLA_KV_RANK), BF16)],
        compiler_params=_params(("parallel",)),
    )(proj, pos, qn_w, kvn_w, w_uq, w_kv, tab)


def _mla_prep_bwd(dq, dk, dv, proj, pos, qn_w, kvn_w, w_uq, w_kv, tab, *, name):
    T = proj.shape[0]
    tT = min(TOKEN_TILE, T)
    a_blk = PROJ_FIXED // AW - 1

    def body(dq_ref, dk_ref, dv_ref, a_ref, pos_ref, qnw_ref, kvnw_ref, wuq_ref, wkv_ref, tab_ref,
             da_ref, dql_ref, dkvl_ref, gqn_ref, gkvn_ref):
        @pl.when(pl.program_id(0) == 0)
        def _():
            gqn_ref[...] = jnp.zeros_like(gqn_ref)
            gkvn_ref[...] = jnp.zeros_like(gkvn_ref)

        cs = _rope_cs(pos_ref[...], tab_ref)
        dkr = jnp.zeros((tT, HP), F32)
        for hd in range(MLA_HEADS):
            sl = slice(hd * HP, (hd + 1) * HP)
            dql_ref[:, sl] = (_rope(dq_ref[:, sl], cs, MLA_ROPE // 2, inverse=True) * ATTN_SCALE).astype(BF16)
            dkh = dk_ref[:, sl]
            dkr = dkr + dkh
            dkvl_ref[:, sl] = dkh.astype(BF16)
        dkvl_ref[:, QW:] = dv_ref[...]
        dqn = _dot(dql_ref[...], wuq_ref[...])
        dkvn = _dot_nt(dkvl_ref[...], wkv_ref[...])
        cq = a_ref[:, 0:MLA_Q_RANK].astype(F32)
        ckv = a_ref[:, MLA_Q_RANK:MLA_Q_RANK + MLA_KV_RANK].astype(F32)
        dcq, gq = _rms_bwd(cq, qnw_ref[...], dqn)
        dckv, gkv = _rms_bwd(ckv, kvnw_ref[...], dkvn)
        gqn_ref[...] += gq
        gkvn_ref[...] += gkv
        da_ref[:, 0:MLA_Q_RANK] = dcq.astype(BF16)
        da_ref[:, MLA_Q_RANK:MLA_Q_RANK + MLA_KV_RANK] = dckv.astype(BF16)
        da_ref[:, 640:768] = _rope(dkr, cs, MLA_ROPE // 2, inverse=True).astype(BF16)
        da_ref[:, 768:AW] = jnp.zeros((tT, AW - 768), BF16)

    def full(r, c):
        return pl.BlockSpec((r, c), lambda i: (0, 0))

    def rows(c):
        return pl.BlockSpec((tT, c), lambda i: (i, 0))

    return pl.pallas_call(
        body, name=name, grid=(T // tT,),
        in_specs=[rows(QW), rows(QW), rows(QW), pl.BlockSpec((tT, AW), lambda i: (i, a_blk)), rows(1),
                  full(1, MLA_Q_RANK), full(1, MLA_KV_RANK),
                  full(QW, MLA_Q_RANK), full(MLA_KV_RANK, 2 * QW), full(8, LANES)],
        out_specs=[rows(AW), rows(QW), rows(2 * QW), full(1, MLA_Q_RANK), full(1, MLA_KV_RANK)],
        out_shape=[jax.ShapeDtypeStruct((T, AW), BF16), jax.ShapeDtypeStruct((T, QW), BF16),
                   jax.ShapeDtypeStruct((T, 2 * QW), BF16),
                   jax.ShapeDtypeStruct((1, MLA_Q_RANK), F32), jax.ShapeDtypeStruct((1, MLA_KV_RANK), F32)],
        compiler_params=_params(("arbitrary",)),
    )(dq, dk, dv, proj, pos, qn_w, kvn_w, w_uq, w_kv, tab)


def _flash_fwd(q, k, v, *, name, exchange=None):
    T = q.shape[0]
    H = q.shape[1] // HP
    tq = min(ATTN_TILE, T)
    nq = T // tq

    sub = tq // ATTN_CHAINS

    def body(q_ref, k_ref, v_ref, o_ref, lse_ref):
        qi = pl.program_id(1)
        qs = [q_ref[c * sub:(c + 1) * sub, :] for c in range(ATTN_CHAINS)]

        def update(carry, off, masked):
            nks = [(c + 1) * sub if masked else tq for c in range(ATTN_CHAINS)]
            scores = [_dot_nt(qs[c], k_ref[pl.ds(off, nks[c]), :]) for c in range(ATTN_CHAINS)]
            out = []
            for c in range(ATTN_CHAINS):
                m_prev, l_prev, acc = carry[c]
                nk, s = nks[c], scores[c]
                vb = v_ref[pl.ds(off, nk), :]
                if masked:
                    rows = lax.broadcasted_iota(jnp.int32, (sub, nk), 0) + c * sub
                    s = jnp.where(rows >= lax.broadcasted_iota(jnp.int32, (sub, nk), 1), s, NEG)
                m_new = jnp.maximum(m_prev, jnp.max(s, axis=1, keepdims=True))
                alpha = jnp.exp(m_prev - m_new)
                p = jnp.exp(s - m_new)
                out.append((m_new, alpha * l_prev + jnp.sum(p, axis=1, keepdims=True),
                            alpha * acc + _dot(p.astype(BF16), vb)))
            return tuple(out)

        init = tuple((jnp.full((sub, 1), NEG, F32), jnp.zeros((sub, 1), F32), jnp.zeros((sub, HP), F32))
                     for _ in range(ATTN_CHAINS))
        carry = lax.fori_loop(0, qi, lambda j, cr: update(cr, pl.multiple_of(j * tq, tq), False), init)
        carry = update(carry, pl.multiple_of(qi * tq, tq), True)
        for c in range(ATTN_CHAINS):
            m_fin, l_fin, acc = carry[c]
            o_ref[c * sub:(c + 1) * sub, :] = (acc / l_fin).astype(BF16)
            lse_ref[c * sub:(c + 1) * sub, :] = jnp.broadcast_to(m_fin + jnp.log(l_fin), (sub, HP))

    qspec = pl.BlockSpec((tq, HP), lambda h, i: (i, h))
    kspec = pl.BlockSpec((T, HP), lambda h, i: (0, h))
    return _call(
        body, name=name, grid=(H, nq),
        in_specs=[qspec, kspec, kspec], out_specs=[qspec, qspec],
        out_shape=[jax.ShapeDtypeStruct((T, H * HP), BF16), jax.ShapeDtypeStruct((T, H * HP), F32)],
        scratch_shapes=[], args=(q, k, v), exchange=exchange)


def _flash_bwd(q, k, v, do, lse, delta, *, name, exchange=None, after=None):
    T = q.shape[0]
    H = q.shape[1] // HP
    tq = min(ATTN_TILE, T)
    nq = T // tq
    sub = tq // ATTN_CHAINS

    def body(k_ref, v_ref, q_ref, do_ref, lse_ref, dl_ref, dq_ref, dk_ref, dv_ref):
        ki = pl.program_id(1)

        @pl.when(ki == 0)
        def _():
            dq_ref[...] = jnp.zeros_like(dq_ref)

        def grow(a):
            return a if a.shape[0] == tq else jnp.concatenate([a, jnp.zeros((tq - a.shape[0], HP), F32)], axis=0)

        def step(carry, j, masked):
            dk_acc, dv_acc = carry
            nks = [(c + 1) * sub if masked else tq for c in range(ATTN_CHAINS)]
            rws = [pl.ds(pl.multiple_of(j * tq + c * sub, sub), sub) for c in range(ATTN_CHAINS)]
            scores = [_dot_nt(q_ref[rws[c], :], k_ref[0:nks[c], :]) for c in range(ATTN_CHAINS)]
            dps = [_dot_nt(do_ref[rws[c], :], v_ref[0:nks[c], :]) for c in range(ATTN_CHAINS)]
            for c in range(ATTN_CHAINS):
                rows, nk, s, dp = rws[c], nks[c], scores[c], dps[c]
                kb = k_ref[0:nk, :]
                qb = q_ref[rows, :]
                dob = do_ref[rows, :]
                if masked:
                    ri = lax.broadcasted_iota(jnp.int32, (sub, nk), 0) + c * sub
                    s = jnp.where(ri >= lax.broadcasted_iota(jnp.int32, (sub, nk), 1), s, NEG)
                p = jnp.exp(s - lse_ref[rows, 0:1])
                dv_acc = dv_acc + grow(_dot_tn(p.astype(BF16), dob))
                ds = (p * (dp - dl_ref[rows, 0:1])).astype(BF16)
                dk_acc = dk_acc + grow(_dot_tn(ds, qb))
                dq_ref[rows, :] += _dot(ds, kb)
            return dk_acc, dv_acc

        carry = step((jnp.zeros((tq, HP), F32), jnp.zeros((tq, HP), F32)), ki, True)
        dk_acc, dv_acc = lax.fori_loop(ki + 1, nq, lambda j, cr: step(cr, j, False), carry)
        dk_ref[...] = dk_acc
        dv_ref[...] = dv_acc.astype(BF16)

    kspec = pl.BlockSpec((tq, HP), lambda h, j: (j, h))
    full = pl.BlockSpec((T, HP), lambda h, j: (0, h))
    return _call(
        body, name=name, grid=(H, nq),
        in_specs=[kspec, kspec, full, full, full, full], out_specs=[full, kspec, kspec],
        out_shape=[jax.ShapeDtypeStruct((T, H * HP), F32), jax.ShapeDtypeStruct((T, H * HP), F32),
                   jax.ShapeDtypeStruct((T, H * HP), BF16)],
        scratch_shapes=[], args=(k, v, q, do, lse, delta), exchange=exchange, after=after)


def _ret_consts(cc, hd):
    lg = math.log(1.0 - 2.0 ** (-5.0 - hd))
    diff = (lax.broadcasted_iota(jnp.int32, (cc, cc), 0) - lax.broadcasted_iota(jnp.int32, (cc, cc), 1)).astype(F32)
    decay = jnp.where(diff >= 0, jnp.exp(jnp.maximum(diff, 0.0) * lg), 0.0)
    idx = lax.broadcasted_iota(jnp.int32, (cc, 1), 0).astype(F32)
    zeta = jnp.exp((cc - 1.0 - idx) * lg)
    xi = jnp.exp((idx + 1.0) * lg)
    return decay, zeta, xi, math.exp(cc * lg)


def _ret_fwd(proj, pos, tab, *, name):
    T = proj.shape[0]
    cc = min(RET_TILE, T)
    n = T // cc

    def body(rq_ref, rk_ref, rv_ref, pos_ref, tab_ref, y_ref, yn_ref, rprev_ref, r_s):
        @pl.when(pl.program_id(0) == 0)
        def _():
            r_s[...] = jnp.zeros_like(r_s)

        cs = _rope_cs(pos_ref[...], tab_ref)
        for hd in range(RET_HEADS):
            sl = slice(hd * HP, (hd + 1) * HP)
            decay, zeta, xi, gc = _ret_consts(cc, hd)
            q = _rope(rq_ref[:, sl].astype(F32), cs, RET_DK // 2).astype(BF16)
            kf = _rope(rk_ref[:, sl].astype(F32), cs, RET_DK // 2) * (RET_DK ** -0.5)
            k = kf.astype(BF16)
            v = rv_ref[:, sl]
            r = r_s[hd]
            rprev_ref[0, hd] = r
            inner = (_dot_nt(q, k) * decay).astype(BF16)
            y = _dot(inner, v) + _dot(q, r.astype(BF16)) * xi
            r_s[hd] = r * gc + _dot_tn((kf * zeta).astype(BF16), v)
            y_ref[:, sl] = y
            mu = jnp.mean(y, axis=-1, keepdims=True)
            yc = y - mu
            var = jnp.mean(yc * yc, axis=-1, keepdims=True)
            yn_ref[:, sl] = (yc * lax.rsqrt(var + GN_EPS)).astype(BF16)

    def blk(j):
        return pl.BlockSpec((cc, RW), lambda i: (i, j))

    return pl.pallas_call(
        body, name=name, grid=(n,),
        in_specs=[blk(0), blk(1), blk(2), pl.BlockSpec((cc, 1), lambda i: (i, 0)),
                  pl.BlockSpec((8, LANES), lambda i: (0, 0))],
        out_specs=[blk(0), blk(0), pl.BlockSpec((1, RET_HEADS, HP, RET_DV), lambda i: (i, 0, 0, 0))],
        out_shape=[jax.ShapeDtypeStruct((T, RW), F32), jax.ShapeDtypeStruct((T, RW), BF16),
                   jax.ShapeDtypeStruct((n, RET_HEADS, HP, RET_DV), F32)],
        scratch_shapes=[pltpu.VMEM((RET_HEADS, HP, RET_DV), F32)],
        compiler_params=_params(("arbitrary",)),
    )(proj, proj, proj, pos, tab)


def _ret_bwd(dyn, y, proj, pos, tab, rprev, *, name):
    T = proj.shape[0]
    cc = min(RET_TILE, T)
    n = T // cc

    def body(dyn_ref, y_ref, rq_ref, rk_ref, rv_ref, pos_ref, tab_ref, rprev_ref,
             drq_ref, drk_ref, drv_ref, dr_s):
        @pl.when(pl.program_id(0) == 0)
        def _():
            dr_s[...] = jnp.zeros_like(dr_s)

        cs = _rope_cs(pos_ref[...], tab_ref)
        for hd in range(RET_HEADS):
            sl = slice(hd * HP, (hd + 1) * HP)
            decay, zeta, xi, gc = _ret_consts(cc, hd)
            q = _rope(rq_ref[:, sl].astype(F32), cs, RET_DK // 2).astype(BF16)
            kf = _rope(rk_ref[:, sl].astype(F32), cs, RET_DK // 2) * (RET_DK ** -0.5)
            k = kf.astype(BF16)
            v = rv_ref[:, sl]
            yv = y_ref[:, sl]
            mu = jnp.mean(yv, axis=-1, keepdims=True)
            yc = yv - mu
            rs = lax.rsqrt(jnp.mean(yc * yc, axis=-1, keepdims=True) + GN_EPS)
            yn = yc * rs
            dn = dyn_ref[:, sl]
            dy = rs * (dn - jnp.mean(dn, axis=-1, keepdims=True) - yn * jnp.mean(dn * yn, axis=-1, keepdims=True))
            dyb = dy.astype(BF16)
            dyx = (dy * xi).astype(BF16)
            dr = dr_s[hd]
            drb = dr.astype(BF16)
            inner = (_dot_nt(q, k) * decay).astype(BF16)
            da = (_dot_nt(dyb, v) * decay).astype(BF16)
            dv = _dot_tn(inner, dyb) + _dot((kf * zeta).astype(BF16), drb)
            dq = _dot(da, k) + _dot_nt(dyx, rprev_ref[0, hd].astype(BF16))
            dk = _dot_tn(da, q) + _dot_nt(v, drb) * zeta
            dr_s[hd] = dr * gc + _dot_tn(q, dyx)
            drq_ref[:, sl] = _rope(dq, cs, RET_DK // 2, inverse=True).astype(BF16)
            drk_ref[:, sl] = _rope(dk * (RET_DK ** -0.5), cs, RET_DK // 2, inverse=True).astype(BF16)
            drv_ref[:, sl] = dv.astype(BF16)

    def blk(j):
        return pl.BlockSpec((cc, RW), lambda i: (n - 1 - i, j))

    return pl.pallas_call(
        body, name=name, grid=(n,),
        in_specs=[blk(0), blk(0), blk(0), blk(1), blk(2), pl.BlockSpec((cc, 1), lambda i: (n - 1 - i, 0)),
                  pl.BlockSpec((8, LANES), lambda i: (0, 0)),
                  pl.BlockSpec((1, RET_HEADS, HP, RET_DV), lambda i: (n - 1 - i, 0, 0, 0))],
        out_specs=[blk(0), blk(0), blk(0)],
        out_shape=[jax.ShapeDtypeStruct((T, RW), BF16)] * 3,
        scratch_shapes=[pltpu.VMEM((RET_HEADS, HP, RET_DV), F32)],
        compiler_params=_params(("arbitrary",)),
    )(dyn, y, proj, proj, proj, pos, tab, rprev)


def _merge_fwd(o, yn, proj, gn_w, w_bm, w_br, w_out, h, post_w, *, name):
    T, D = h.shape
    tT = min(TOKEN_TILE, T)
    g_blk = PROJ_FIXED // D

    def body(o_ref, yn_ref, rg_ref, gm_ref, gr_ref, gnw_ref, wbm_ref, wbr_ref, wout_ref, h_ref, post_ref,
             omla_ref, oret_ref, m_ref, ho_ref):
        groups = [slice(c * (tT // FFN_CHAINS), (c + 1) * (tT // FFN_CHAINS)) for c in range(FFN_CHAINS)]
        o_mlas = [_dot(o_ref[rs, :], wbm_ref[...]) for rs in groups]
        for rs, o_mla in zip(groups, o_mlas):
            rg = rg_ref[rs, :].astype(F32)
            gated = (rg * _sigmoid(rg) * (yn_ref[rs, :].astype(F32) * gnw_ref[...])).astype(BF16)
            o_ret = _dot(gated, wbr_ref[...])
            omla_ref[rs, :] = o_mla.astype(BF16)
            oret_ref[rs, :] = o_ret.astype(BF16)
            merged = _sigmoid(gm_ref[rs, :].astype(F32)) * o_mla + _sigmoid(gr_ref[rs, :].astype(F32)) * o_ret
            m = _dot(merged.astype(BF16), wout_ref[...])
            m_ref[rs, :] = m
            ho_ref[rs, :] = h_ref[rs, :] + _rms_fwd(m, post_ref[...])

    def full(r, c):
        return pl.BlockSpec((r, c), lambda i: (0, 0))

    def rows(c, j=0):
        return pl.BlockSpec((tT, c), lambda i: (i, j))

    return pl.pallas_call(
        body, name=name, grid=(T // tT,),
        in_specs=[rows(QW), rows(RW), rows(RW, 3), rows(D, g_blk), rows(D, g_blk + 1), full(1, RW),
                  full(QW, D), full(RW, D), full(D, D), rows(D), full(1, D)],
        out_specs=[rows(D), rows(D), rows(D), rows(D)],
        out_shape=[jax.ShapeDtypeStruct((T, D), BF16), jax.ShapeDtypeStruct((T, D), BF16),
                   jax.ShapeDtypeStruct((T, D), F32), jax.ShapeDtypeStruct((T, D), F32)],
        compiler_params=_params(("parallel",)),
    )(o, yn, proj, proj, proj, gn_w, w_bm, w_br, w_out, h, post_w)


def _merge_bwd(dho, m, post_w, omla, oret, proj, yn, gn_w, o, w_out, w_bm, w_br, *, name):
    T, D = dho.shape
    tT = min(MERGE_TILE, T)
    g_blk = PROJ_FIXED // D

    def body(dho_ref, m_ref, post_ref, omla_ref, oret_ref, rg_ref, gm_ref, gr_ref, yn_ref, gnw_ref, o_ref,
             wout_ref, wbm_ref, wbr_ref,
             dm_ref, merged_ref, dgm_ref, dgr_ref, domla_ref, do_ref, delta_ref, doret_ref, gated_ref,
             drg_ref, dyn_ref, gpost_ref, ggn_ref):
        @pl.when(pl.program_id(0) == 0)
        def _():
            gpost_ref[...] = jnp.zeros_like(gpost_ref)
            ggn_ref[...] = jnp.zeros_like(ggn_ref)

        dm, gp = _rms_bwd(m_ref[...], post_ref[...], dho_ref[...])
        gpost_ref[...] += gp
        dmb = dm.astype(BF16)
        dm_ref[...] = dmb
        dmerged = _dot_nt(dmb, wout_ref[...])
        o_mla = omla_ref[...].astype(F32)
        o_ret = oret_ref[...].astype(F32)
        sgm = _sigmoid(gm_ref[...].astype(F32))
        sgr = _sigmoid(gr_ref[...].astype(F32))
        merged_ref[...] = (sgm * o_mla + sgr * o_ret).astype(BF16)
        dgm_ref[...] = (dmerged * o_mla * sgm * (1.0 - sgm)).astype(BF16)
        dgr_ref[...] = (dmerged * o_ret * sgr * (1.0 - sgr)).astype(BF16)
        domla = (dmerged * sgm).astype(BF16)
        domla_ref[...] = domla
        do = _dot_nt(domla, wbm_ref[...])
        do_ref[...] = do.astype(BF16)
        for hd in range(MLA_HEADS):
            sl = slice(hd * HP, (hd + 1) * HP)
            d = jnp.sum(do[:, sl] * o_ref[:, sl].astype(F32), axis=-1, keepdims=True)
            delta_ref[:, sl] = jnp.broadcast_to(d, (tT, HP))
        doret = (dmerged * sgr).astype(BF16)
        doret_ref[...] = doret
        dgated = _dot_nt(doret, wbr_ref[...])
        rg = rg_ref[...].astype(F32)
        sg = _sigmoid(rg)
        srg = rg * sg
        ynv = yn_ref[...].astype(F32)
        yw = ynv * gnw_ref[...]
        gated_ref[...] = (srg * yw).astype(BF16)
        drg_ref[...] = (dgated * yw * (sg * (1.0 + rg * (1.0 - sg)))).astype(BF16)
        dgs = dgated * srg
        dyn_ref[...] = dgs * gnw_ref[...]
        ggn_ref[...] += jnp.sum(dgs * ynv, axis=0, keepdims=True)

    def full(r, c):
        return pl.BlockSpec((r, c), lambda i: (0, 0))

    def rows(c, j=0):
        return pl.BlockSpec((tT, c), lambda i: (i, j))

    return pl.pallas_call(
        body, name=name, grid=(T // tT,),
        in_specs=[rows(D), rows(D), full(1, D), rows(D), rows(D), rows(RW, 3), rows(D, g_blk), rows(D, g_blk + 1),
                  rows(RW), full(1, RW), rows(QW), full(D, D), full(QW, D), full(RW, D)],
        out_specs=[rows(D), rows(D), rows(D), rows(D), rows(D), rows(QW), rows(QW), rows(D), rows(RW),
                   rows(RW), rows(RW), full(1, D), full(1, RW)],
        out_shape=[jax.ShapeDtypeStruct((T, D), BF16)] * 5
        + [jax.ShapeDtypeStruct((T, QW), BF16), jax.ShapeDtypeStruct((T, QW), F32),
           jax.ShapeDtypeStruct((T, D), BF16), jax.ShapeDtypeStruct((T, RW), BF16),
           jax.ShapeDtypeStruct((T, RW), BF16), jax.ShapeDtypeStruct((T, RW), F32),
           jax.ShapeDtypeStruct((1, D), F32), jax.ShapeDtypeStruct((1, RW), F32)],
        compiler_params=_params(("arbitrary",)),
    )(dho, m, post_w, omla, oret, proj, proj, proj, yn, gn_w, o, w_out, w_bm, w_br)


def _mesh_pos():
    return lax.axis_index("x"), lax.axis_index("y"), lax.axis_index("c")


class _Gather:
    def __init__(self, shards):
        self.operands = list(shards)
        self.n = len(shards)
        self.out_shape = [jax.ShapeDtypeStruct((N_DEV,) + s.shape, s.dtype) for s in shards]
        self.scratch = [pltpu.SemaphoreType.DMA((7 * self.n,)), pltpu.SemaphoreType.DMA((7 * self.n,)),
                        pltpu.SemaphoreType.DMA((self.n,))]

    def phase(self, p, x_refs, out_refs, sems):
        send_sems, recv_sems, local_sems = sems
        x, y, c = _mesh_pos()
        me, sibling = (x, y, c), (x, y, 1 - c)
        chips = [(1 - x, y), (x, 1 - y), (1 - x, 1 - y)]

        def copy(w, k, block, to, src=None):
            slot = out_refs[w].at[4 * block[0] + 2 * block[1] + block[2]]
            return pltpu.make_async_remote_copy(
                src_ref=slot if src is None else src, dst_ref=slot,
                send_sem=send_sems.at[7 * w + k], recv_sem=recv_sems.at[7 * w + k],
                device_id=to, device_id_type=pl.DeviceIdType.MESH)

        for w in range(self.n):
            mine = pltpu.make_async_copy(x_refs[w], out_refs[w].at[4 * x + 2 * y + c], local_sems.at[w])
            first = [copy(w, 0, me, sibling, src=x_refs[w])]
            first += [copy(w, 1 + j, me, (*chip, c), src=x_refs[w]) for j, chip in enumerate(chips)]
            passed = [copy(w, 4 + j, (*chip, c), sibling) for j, chip in enumerate(chips)]
            if p == 0:
                mine.start()
                for cp in first:
                    cp.start()
            elif p == 1:
                for j, chip in enumerate(chips):
                    copy(w, 1 + j, (*chip, c), me).wait_recv()
                    passed[j].start()
            else:
                copy(w, 0, sibling, me).wait_recv()
                for j, chip in enumerate(chips):
                    copy(w, 4 + j, (*chip, 1 - c), me).wait_recv()
                for cp in first + passed:
                    cp.wait_send()
                mine.wait()


class _Scatter:
    def __init__(self, grads, whole=()):
        self.n_sliced = len(grads)
        self.operands = list(grads) + list(whole)
        self.n = len(self.operands)
        self.out_shape = [jax.ShapeDtypeStruct(g.shape, g.dtype) for g in grads]
        self.out_shape += [jax.ShapeDtypeStruct((N_DEV,) + a.shape, a.dtype) for a in whole]
        n_sem = (N_DEV - 1) * self.n
        self.scratch = [pltpu.SemaphoreType.DMA((n_sem,)), pltpu.SemaphoreType.DMA((n_sem,)),
                        pltpu.SemaphoreType.DMA((self.n,))]

    def phase(self, p, in_refs, out_refs, sems):
        if p == 1:
            return
        send_sems, recv_sems, local_sems = sems
        x, y, c = _mesh_pos()
        me = 4 * x + 2 * y + c

        def src(w, dev):
            return in_refs[w].at[dev] if w < self.n_sliced else in_refs[w]

        for w in range(self.n):
            own = None if local_sems is None else pltpu.make_async_copy(src(w, me), out_refs[w].at[me], local_sems.at[w])
            sends, recvs = [], []
            for r in range(1, N_DEV):
                px = 1 - x if r & 4 else x
                py = 1 - y if r & 2 else y
                pc = 1 - c if r & 1 else c
                peer, pidx = (px, py, pc), 4 * px + 2 * py + pc
                k = (N_DEV - 1) * w + r - 1
                sends.append(pltpu.make_async_remote_copy(
                    src_ref=src(w, pidx), dst_ref=out_refs[w].at[me], send_sem=send_sems.at[k],
                    recv_sem=recv_sems.at[k], device_id=peer, device_id_type=pl.DeviceIdType.MESH))
                recvs.append(pltpu.make_async_remote_copy(
                    src_ref=src(w, me), dst_ref=out_refs[w].at[pidx], send_sem=send_sems.at[k],
                    recv_sem=recv_sems.at[k], device_id=peer, device_id_type=pl.DeviceIdType.MESH))
            if p == 0:
                if own is not None:
                    own.start()
                for cp in sends:
                    cp.start()
            else:
                for cp in recvs:
                    cp.wait_recv()
                for cp in sends:
                    cp.wait_send()
                if own is not None:
                    own.wait()


class _SplitScatter:
    def __init__(self, ex, name):
        self.ex, self.name = ex, name

    def _specs(self):
        ex = self.ex
        hbm = pl.BlockSpec(memory_space=pltpu.HBM)
        sem = pl.BlockSpec(memory_space=pltpu.SEMAPHORE)
        effect = pltpu.CompilerParams(has_side_effects=pltpu.SideEffectType.DATAFLOW_SIDE_EFFECTING)
        buffers = [pltpu.HBM(a.shape, a.dtype) for a in ex.operands] + [pltpu.HBM(s.shape, s.dtype) for s in ex.out_shape]
        return hbm, sem, effect, buffers

    def start(self):
        ex, n = self.ex, self.ex.n
        n_sem = (N_DEV - 1) * n
        hbm, sem, effect, buffers = self._specs()
        in_hbm = lambda a: pltpu.with_memory_space_constraint(a, pltpu.HBM)

        me = 4 * lax.axis_index("x") + 2 * lax.axis_index("y") + lax.axis_index("c")
        lands = []
        for w, (a, s) in enumerate(zip(ex.operands, ex.out_shape)):
            mine = lax.dynamic_index_in_dim(a, me, 0, keepdims=True) if w < ex.n_sliced else a[None]
            lands.append(lax.dynamic_update_slice_in_dim(lax.empty(s.shape, s.dtype), mine, me, 0))

        def start_body(*refs):
            ex.phase(0, refs[:n], refs[n:2 * n], (refs[2 * n], refs[2 * n + 1], None))
            refs[-1][...] = jnp.zeros_like(refs[-1])

        self.started = pl.pallas_call(
            start_body, name=self.name + "_start",
            out_shape=[pltpu.SemaphoreType.DMA((n_sem,)), pltpu.SemaphoreType.DMA((n_sem,))] + buffers
            + [jax.ShapeDtypeStruct((8, LANES), F32)],
            in_specs=[hbm] * (2 * n), out_specs=[sem, sem] + [hbm] * (2 * n) + [pl.BlockSpec(memory_space=pltpu.VMEM)],
            input_output_aliases={i: 2 + i for i in range(2 * n)}, compiler_params=effect,
        )(*[in_hbm(a) for a in ex.operands], *[in_hbm(a) for a in lands])
        return self.started[-1]

    def wait(self, after):
        ex, n = self.ex, self.ex.n
        hbm, sem, effect, buffers = self._specs()
        anyspec = pl.BlockSpec(memory_space=pl.ANY)

        def wait_body(*refs):
            ex.phase(2, refs[:n], refs[n:2 * n], (refs[2 * n], refs[2 * n + 1], None))

        done = pl.pallas_call(
            wait_body, name=self.name + "_wait", out_shape=buffers,
            in_specs=[hbm] * (2 * n) + [sem, sem] + [anyspec] * len(after), out_specs=[hbm] * (2 * n),
            input_output_aliases={i: i for i in range(2 * n)}, compiler_params=effect,
        )(*self.started[2:2 + 2 * n], self.started[0], self.started[1], *after)
        return done[n:]


def _exchange_alone(ex, *, name):
    n = ex.n

    def body(*refs):
        for p in range(3):
            ex.phase(p, refs[:n], refs[n:2 * n], refs[2 * n:])

    anyspec = pl.BlockSpec(memory_space=pl.ANY)
    return pl.pallas_call(body, name=name, out_shape=ex.out_shape, in_specs=[anyspec] * n,
                          out_specs=[anyspec] * n, scratch_shapes=ex.scratch)(*ex.operands)


def _adam_step(w_ref, p_ref, m_ref, v_ref, g_ref, d_ref, nm_ref, nv_ref):
    g = p_ref[0].astype(F32)
    for j in range(1, N_DEV):
        g = g + p_ref[j].astype(F32)
    g_ref[...] = g
    nm = ADAM_B1 * m_ref[...] + (1.0 - ADAM_B1) * g
    nv = ADAM_B2 * v_ref[...] + (1.0 - ADAM_B2) * (g * g)
    nm_ref[...] = nm
    nv_ref[...] = nv
    m_hat = nm / (1.0 - ADAM_B1 ** ADAM_STEP)
    v_hat = nv / (1.0 - ADAM_B2 ** ADAM_STEP)
    d_ref[...] = -ADAM_LR * (m_hat / (jnp.sqrt(v_hat) + ADAM_EPS) + ADAM_WD * w_ref[...])


def _adamw_vectors(ws, parts, ms, vs, *, name):
    n = len(ws)

    def body(*refs):
        w_refs, p_refs, m_refs, v_refs = (refs[i * n:(i + 1) * n] for i in range(4))
        outs = refs[4 * n:]
        for i in range(n):
            _adam_step(w_refs[i], p_refs[i], m_refs[i], v_refs[i], *outs[4 * i:4 * i + 4])

    return pl.pallas_call(
        body, name=name,
        out_shape=[jax.ShapeDtypeStruct(w.shape, F32) for w in ws for _ in range(4)],
    )(*ws, *parts, *ms, *vs)


def _adamw(w, parts, m, v, after, *, name):
    G, R, n = w.shape
    tn = 256 if (n > 256 and n % 256 == 0) else n
    tr = R
    for t in range(16, R, 16):
        if R % t == 0 and t * tn <= 160 * 1024:
            tr = t
    if R * tn <= 160 * 1024:
        tr = R

    def body(w_ref, p_ref, m_ref, v_ref, after_ref, g_ref, d_ref, nm_ref, nv_ref):
        _adam_step(w_ref, p_ref, m_ref, v_ref, g_ref, d_ref, nm_ref, nv_ref)

    blk = pl.BlockSpec((None, tr, tn), lambda g, i, j: (g, i, j))
    return pl.pallas_call(
        body, name=name, grid=(G, R // tr, n // tn),
        in_specs=[blk, pl.BlockSpec((N_DEV, None, tr, tn), lambda g, i, j: (0, g, i, j)), blk, blk,
                  pl.BlockSpec((8, LANES), lambda g, i, j: (0, 0))],
        out_specs=[blk, blk, blk, blk],
        out_shape=[jax.ShapeDtypeStruct((G, R, n), F32)] * 4,
        compiler_params=_params(("parallel", "parallel", "parallel")),
    )(w, parts, m, v, after)


def _pad_last(a, width):
    return jnp.pad(a, [(0, 0)] * (a.ndim - 1) + [(0, width - a.shape[-1])])


def _cols_of(g):
    return g.transpose(1, 0, 2).reshape(g.shape[1], N_DEV * g.shape[2])


def _col_shards(w):
    return w.reshape(w.shape[0], N_DEV, w.shape[1] // N_DEV).transpose(1, 0, 2)


def kernel(x, positions, ffn1_pre_w, ffn1_w1, ffn1_w2, ffn1_post_w, mix_pre_w, w_in, mla_q_norm_w, mla_w_uq, mla_kv_norm_w, mla_w_ukv, ret_gn_w, w_branch_mla, w_branch_ret, w_out, mix_post_w, ffn2_pre_w, ffn2_w1, ffn2_w2, ffn2_post_w, loss_target, m_ffn1_pre_w, m_ffn1_w1, m_ffn1_w2, m_ffn1_post_w, m_mix_pre_w, m_w_in, m_mla_q_norm_w, m_mla_w_uq, m_mla_kv_norm_w, m_mla_w_ukv, m_ret_gn_w, m_w_branch_mla, m_w_branch_ret, m_w_out, m_mix_post_w, m_ffn2_pre_w, m_ffn2_w1, m_ffn2_w2, m_ffn2_post_w, v_ffn1_pre_w, v_ffn1_w1, v_ffn1_w2, v_ffn1_post_w, v_mix_pre_w, v_w_in, v_mla_q_norm_w, v_mla_w_uq, v_mla_kv_norm_w, v_mla_w_ukv, v_ret_gn_w, v_w_branch_mla, v_w_branch_ret, v_w_out, v_mix_post_w, v_ffn2_pre_w, v_ffn2_w1, v_ffn2_w2, v_ffn2_post_w):
    T, D = x.shape[1], x.shape[2]
    h0 = x[0]
    tgt = loss_target[0]
    pos = positions.reshape(T, 1).astype(F32)

    big = [("ffn1_w1", ffn1_w1, m_ffn1_w1, v_ffn1_w1), ("ffn1_w2", ffn1_w2, m_ffn1_w2, v_ffn1_w2),
           ("w_in", w_in, m_w_in, v_w_in), ("mla_w_uq", mla_w_uq, m_mla_w_uq, v_mla_w_uq),
           ("mla_w_ukv", mla_w_ukv, m_mla_w_ukv, v_mla_w_ukv),
           ("w_branch_mla", w_branch_mla, m_w_branch_mla, v_w_branch_mla),
           ("w_branch_ret", w_branch_ret, m_w_branch_ret, v_w_branch_ret),
           ("w_out", w_out, m_w_out, v_w_out),
           ("ffn2_w1", ffn2_w1, m_ffn2_w1, v_ffn2_w1), ("ffn2_w2", ffn2_w2, m_ffn2_w2, v_ffn2_w2)]
    small = [("ffn1_pre_w", ffn1_pre_w, m_ffn1_pre_w, v_ffn1_pre_w), ("ffn1_post_w", ffn1_post_w, m_ffn1_post_w, v_ffn1_post_w),
             ("mix_pre_w", mix_pre_w, m_mix_pre_w, v_mix_pre_w), ("mla_q_norm_w", mla_q_norm_w, m_mla_q_norm_w, v_mla_q_norm_w),
             ("mla_kv_norm_w", mla_kv_norm_w, m_mla_kv_norm_w, v_mla_kv_norm_w), ("ret_gn_w", ret_gn_w, m_ret_gn_w, v_ret_gn_w),
             ("mix_post_w", mix_post_w, m_mix_post_w, v_mix_post_w), ("ffn2_pre_w", ffn2_pre_w, m_ffn2_pre_w, v_ffn2_pre_w),
             ("ffn2_post_w", ffn2_post_w, m_ffn2_post_w, v_ffn2_post_w)]

    half = ffn1_w2.shape[1]
    hp = -(-half // LANES) * LANES

    def rows_view(w):
        return w[0].T

    def send_w1(w):
        return jnp.pad(rows_view(w).reshape(2, half, D), ((0, 0), (0, hp - half), (0, 0))).reshape(2 * hp, D).astype(BF16)

    def send_w2(w):
        return jnp.pad(w[0], ((0, hp - half), (0, 0))).astype(BF16)

    mixer = ["w_in", "mla_w_uq", "mla_w_ukv", "w_branch_mla", "w_branch_ret", "w_out"]
    uq_w = MLA_NOPE + MLA_ROPE
    mixer_send = [rows_view(w_in).astype(BF16), jnp.pad(rows_view(mla_w_uq), ((0, HP - uq_w), (0, 0))).astype(BF16),
                  mla_w_ukv[0].astype(BF16), w_branch_mla[0].astype(BF16), w_branch_ret[0].astype(BF16),
                  w_out[0].astype(BF16)]

    w1a, w2a = _exchange_alone(_Gather([send_w1(ffn1_w1), send_w2(ffn1_w2)]), name="gather_ffn1")
    w2a = w2a.reshape(N_DEV // 2, 2 * hp, D)
    u1, f1, h1, *got = _ffn_fwd(h0, ffn1_pre_w, w1a, w2a, ffn1_post_w, None, name="ffn1_fwd_gather_mixer",
                                exchange=_Gather(mixer_send))
    fw = dict(zip(mixer, got))

    wi = fw["w_in"].reshape(-1, D)
    cq_w, ckv_w, kr_w = wi[0:384], wi[384:640], wi[640:672]
    rq_w, rk_w = wi[672:928], wi[928:1184]
    rv_w, rg_w = wi[1184:1696], wi[1696:2208]
    gm_w, gr_w = wi[2208:2208 + D], wi[2208 + D:2208 + 2 * D]
    zer = lambda n: jnp.zeros((n, D), BF16)
    head_rows = lambda a, h: jnp.pad(a.reshape(h, -1, D), ((0, 0), (0, HP - a.shape[0] // h), (0, 0))).reshape(h * HP, D)
    w_in_p = jnp.concatenate([head_rows(rq_w, RET_HEADS), head_rows(rk_w, RET_HEADS), rv_w, rg_w,
                              cq_w, ckv_w, zer(MLA_NOPE), kr_w, zer(HP - MLA_NOPE - MLA_ROPE), zer(AW - 768),
                              gm_w, gr_w], axis=0)
    w_uq_p = fw["mla_w_uq"].reshape(QW, MLA_Q_RANK)
    ukv = fw["mla_w_ukv"].transpose(1, 0, 2)
    w_kv_p = jnp.concatenate([_pad_last(ukv[:, :, :MLA_NOPE], HP).reshape(MLA_KV_RANK, QW),
                              _pad_last(ukv[:, :, MLA_NOPE:], HP).reshape(MLA_KV_RANK, QW)], axis=1)
    w_bm_p = jnp.pad(_cols_of(fw["w_branch_mla"]).reshape(MLA_HEADS, MLA_V, D),
                     ((0, 0), (0, HP - MLA_V), (0, 0))).reshape(QW, D)
    w_br, w_o = _cols_of(fw["w_branch_ret"]), fw["w_out"].reshape(D, D)
    tab_mla = _rope_table(MLA_NOPE, MLA_ROPE // 2)
    tab_ret = _rope_table(0, RET_DK // 2)

    proj, a1 = _rms_matmul(h1, mix_pre_w, w_in_p, name="mixer_in_proj")
    q, k, v, qn, kvn = _mla_prep_fwd(proj, pos, mla_q_norm_w, mla_kv_norm_w, w_uq_p, w_kv_p, tab_mla, name="mla_prep_fwd")
    o, lse, w1b, w2b = _flash_fwd(q, k, v, name="mla_attn_fwd_gather_ffn2",
                                  exchange=_Gather([send_w1(ffn2_w1), send_w2(ffn2_w2)]))
    w2b = w2b.reshape(N_DEV // 2, 2 * hp, D)
    ypre, yn, rprev = _ret_fwd(proj, pos, tab_ret, name="retention_fwd")
    omla, oret, m, h2 = _merge_fwd(o, yn, proj, ret_gn_w, w_bm_p, w_br, w_o, h1, mix_post_w, name="merge_fwd")
    u2, f2, _, dy, lossp = _ffn_fwd(h2, ffn2_pre_w, w1b, w2b, ffn2_post_w, tgt, name="ffn2_fwd_loss")

    def grad(x, dy, tag, after=None):
        return _matmul_tn(x if x.ndim == 3 else x[None], dy if dy.ndim == 3 else dy[None], name=tag, after=after)

    g2, du2, df2, a2, dh2, gpost2, gpre2 = _ffn_bwd(dy, f2, ffn2_post_w, h2, ffn2_pre_w, u2, w2b, w1b, name="ffn2_bwd")
    dw1b, = grad(du2.reshape(N_DEV, T, 2 * hp), a2, "ffn2_dw1")
    dw2b = grad(g2, df2, "ffn2_dw2")[0].reshape(N_DEV, hp, D)
    (dmb, merged, dgm, dgr, domla, do, delta, doret, gated, drg, dyn, gpostm, ggn) = _merge_bwd(
        dh2, m, mix_post_w, omla, oret, proj, yn, ret_gn_w, o, w_o, w_bm_p, w_br, name="merge_bwd")
    dw_out = grad(merged, dmb, "dw_out")[0][0]
    dw_bm_p = grad(o, domla, "dw_branch_mla")[0][0]
    dw_br = grad(gated, doret, "dw_branch_ret")[0][0]
    sc_ffn2 = _SplitScatter(_Scatter([dw1b, dw2b]), "scatter_ffn2")
    dq, dk, dv = _flash_bwd(q, k, v, do, lse, delta, name="mla_attn_bwd", after=sc_ffn2.start())
    da, dql, dkvl, gqn, gkvn = _mla_prep_bwd(dq, dk, dv, proj, pos, mla_q_norm_w, mla_kv_norm_w, w_uq_p, w_kv_p, tab_mla, name="mla_prep_bwd")
    dw_uq_p = grad(dql, qn, "dw_uq")[0][0]
    dw_kv_p = grad(kvn, dkvl, "dw_ukv")[0][0]
    drq, drk, drv = _ret_bwd(dyn, ypre, proj, pos, tab_ret, rprev, name="retention_bwd")
    dproj = jnp.concatenate([drq, drk, drv, drg, da, dgm, dgr], axis=1)
    dw_in_p = grad(dproj, a1, "dw_in")[0][0]

    dw_uq = dw_uq_p.reshape(MLA_HEADS, HP, MLA_Q_RANK)[:, :uq_w]
    dkp = dw_kv_p[:, :QW].reshape(MLA_KV_RANK, MLA_HEADS, HP)[:, :, :MLA_NOPE]
    dvp = dw_kv_p[:, QW:].reshape(MLA_KV_RANK, MLA_HEADS, HP)[:, :, :MLA_V]
    dw_ukv = jnp.concatenate([dkp, dvp], axis=2).transpose(1, 0, 2)
    dw_bm = dw_bm_p.reshape(MLA_HEADS, HP, D)[:, :MLA_V].reshape(MLA_HEADS * MLA_V, D)
    small_mixer_grads = [dw_uq, dw_ukv, _col_shards(dw_bm), _col_shards(dw_br), dw_out.reshape(N_DEV, D // N_DEV, D)]
    sc_small = _SplitScatter(_Scatter(small_mixer_grads), "scatter_mixer_small")
    dh1, gmixpre = _proj_bwd(dproj, w_in_p, h1, mix_pre_w, dh2, name="mixer_in_bwd", after=sc_small.start())
    unhead = lambda a, h, wd: a.reshape(h, HP, D)[:, :wd].reshape(h * wd, D)
    c0 = 4 * RW
    dw_in = jnp.concatenate([
        dw_in_p[c0:c0 + 384], dw_in_p[c0 + 384:c0 + 640], dw_in_p[c0 + 640 + MLA_NOPE:c0 + 640 + MLA_NOPE + MLA_ROPE],
        unhead(dw_in_p[0:RW], RET_HEADS, RET_DK), unhead(dw_in_p[RW:2 * RW], RET_HEADS, RET_DK),
        dw_in_p[2 * RW:3 * RW], dw_in_p[3 * RW:4 * RW],
        dw_in_p[PROJ_FIXED:PROJ_FIXED + D], dw_in_p[PROJ_FIXED + D:PROJ_FIXED + 2 * D]], axis=0).reshape(N_DEV, -1, D)
    sc_w_in = _SplitScatter(_Scatter([dw_in]), "scatter_w_in")
    g1, du1, df1, a0, dx, gpost1, gpre1 = _ffn_bwd(
        dh1, f1, ffn1_post_w, h0, ffn1_pre_w, u1, w2a, w1a, name="ffn1_bwd", after=sc_w_in.start())
    dw2a = grad(g1, df1, "ffn1_dw2")[0].reshape(N_DEV, hp, D)
    sc_dw2a = _SplitScatter(_Scatter([dw2a]), "scatter_ffn1_dw2")
    dw1a, = grad(du1.reshape(N_DEV, T, 2 * hp), a0, "ffn1_dw1", after=sc_dw2a.start())

    small_g = {"ffn1_pre_w": gpre1, "ffn1_post_w": gpost1, "mix_pre_w": gmixpre, "mla_q_norm_w": gqn,
               "mla_kv_norm_w": gkvn, "ret_gn_w": ggn, "mix_post_w": gpostm, "ffn2_pre_w": gpre2, "ffn2_post_w": gpost2}
    sc_last = _SplitScatter(_Scatter([dw1a], whole=[small_g[nm] for nm, *_ in small] + [lossp]), "scatter_ffn1_dw1")
    token = sc_last.start()
    recv_ffn2 = sc_ffn2.wait([token])
    recv_mixer = sc_w_in.wait([token]) + sc_small.wait([token])
    recv_w2a, = sc_dw2a.wait([token])
    parts = dict(zip(mixer, recv_mixer))
    parts.update(ffn1_w2=recv_w2a, ffn2_w1=recv_ffn2[0], ffn2_w2=recv_ffn2[1])
    as_is = (lambda a: a, lambda p: p[:, None], lambda a: a)
    views = {nm: as_is for nm, *_ in big}
    for nm in ("ffn1_w1", "ffn2_w1"):
        views[nm] = (lambda a: rows_view(a).reshape(2, half, D), lambda p: p.reshape(N_DEV, 2, hp, D),
                     lambda a: a.reshape(2 * half, D).T[None])
    for nm in ("w_in", "mla_w_uq"):
        views[nm] = (lambda a: rows_view(a)[None], lambda p: p[:, None], lambda a: a[0].T[None])

    def update(nm, w, m_, v_, after):
        to_view, parts_view, back = views[nm]
        return [back(a) for a in _adamw(to_view(w), parts_view(parts[nm]), to_view(m_), to_view(v_), after,
                                        name="adamw_" + nm)]

    big_out = {nm: update(nm, w, m_, v_, token) for nm, w, m_, v_ in big if nm != "ffn1_w1"}
    recv_w1a, *small_parts, loss_parts = sc_last.wait([d[0] for d in big_out.values()])
    loss = jnp.sum(loss_parts[:, ::8, 0])
    parts["ffn1_w1"] = recv_w1a
    big_out["ffn1_w1"] = update("ffn1_w1", ffn1_w1, m_ffn1_w1, v_ffn1_w1, jnp.zeros((8, LANES), F32))
    small_out = _adamw_vectors([w for _, w, _, _ in small], small_parts, [a for _, _, a, _ in small],
                               [a for _, _, _, a in small], name="adamw_replicated")

    order = ["ffn1_pre_w", "ffn1_w1", "ffn1_w2", "ffn1_post_w", "mix_pre_w", "w_in", "mla_q_norm_w", "mla_w_uq",
             "mla_kv_norm_w", "mla_w_ukv", "ret_gn_w", "w_branch_mla", "w_branch_ret", "w_out", "mix_post_w",
             "ffn2_pre_w", "ffn2_w1", "ffn2_w2", "ffn2_post_w"]
    outs = [loss, dx[None]]
    for i in range(4):
        both = {nm: big_out[nm][i] for nm in big_out}
        both.update({nm: small_out[4 * j + i] for j, (nm, *_) in enumerate(small)})
        outs += [both[nm] for nm in order]
    return tuple(outs)
```

```python
import math

import numpy as np
import jax
import jax.numpy as jnp
from jax import lax
from jax.experimental import pallas as pl
from jax.experimental.pallas import tpu as pltpu

F32, BF16 = jnp.float32, jnp.bfloat16

MLA_HEADS, MLA_NOPE, MLA_ROPE, MLA_V = 8, 64, 32, 64
MLA_Q_RANK, MLA_KV_RANK = 384, 256
RET_HEADS, RET_DK, RET_DV = 4, 64, 128
ROPE_BASE, NORM_EPS, GN_EPS = 10000.0, 1e-6, 1e-6
ADAM_LR, ADAM_B1, ADAM_B2, ADAM_EPS, ADAM_WD, ADAM_STEP = 0.001, 0.9, 0.999, 1e-08, 0.01, 10
ATTN_SCALE = 1.0 / math.sqrt(MLA_NOPE + MLA_ROPE)

N_DEV = 8
LANES = 128
HP = LANES
QW = MLA_HEADS * HP
RW = RET_HEADS * HP
AW = 1024
PROJ_FIXED = 4 * RW + AW
NEG = -1e30

TOKEN_TILE = 512
FFN_FWD_TILE = 1024
ATTN_TILE = 1024
ATTN_CHAINS = 2
FFN_CHAINS = 2
RET_TILE = 256
PROJ_TILE_CAP = 2560
GRAD_TILE_CAP = 1408
GRAD_TOKEN_TILE = 2048
MERGE_TILE = 256
VMEM_LIMIT = 56 * 1024 * 1024


def _tile(n, cap, mult=LANES):
    if n <= cap:
        return n
    best = None
    for t in range(mult, cap + 1, mult):
        if n % t == 0:
            best = t
    assert best is not None, (n, cap, mult)
    return best


def _params(sem):
    return pltpu.CompilerParams(dimension_semantics=sem, vmem_limit_bytes=VMEM_LIMIT)


def _dot(a, b):
    return lax.dot_general(a, b, (((1,), (0,)), ((), ())), preferred_element_type=F32)


def _dot_nt(a, b):
    return lax.dot_general(a, b, (((1,), (1,)), ((), ())), preferred_element_type=F32)


def _dot_tn(a, b):
    return lax.dot_general(a, b, (((0,), (0,)), ((), ())), preferred_element_type=F32)


def _sigmoid(x):
    return pl.reciprocal(1.0 + jnp.exp(-x), approx=True)


def _rms_fwd(x, w):
    r = lax.rsqrt(jnp.mean(x * x, axis=-1, keepdims=True) + NORM_EPS)
    return x * r * w


def _rms_bwd(x, w, dy):
    r = lax.rsqrt(jnp.mean(x * x, axis=-1, keepdims=True) + NORM_EPS)
    xh = x * r
    g = dy * w
    dx = r * (g - xh * jnp.mean(g * xh, axis=-1, keepdims=True))
    return dx, jnp.sum(dy * xh, axis=0, keepdims=True)


def _rope_table(first, half):
    inv = (np.float32(ROPE_BASE) ** (-(np.arange(half, dtype=np.float32) / np.float32(half)))).astype(np.float32)
    tab = np.zeros((8, LANES), np.float32)
    tab[0, first:first + half] = inv
    tab[0, first + half:first + 2 * half] = inv
    tab[1, first:first + half] = -1.0
    tab[2, first + half:first + 2 * half] = 1.0
    return jnp.asarray(tab)


def _rope_cs(pos, tab_ref):
    ang = pos * tab_ref[0:1, :]
    s = jnp.sin(ang)
    return jnp.cos(ang), s * tab_ref[1:2, :], s * tab_ref[2:3, :]


def _rope(x, cs, half, inverse=False):
    c, s1, s2 = cs
    a = pltpu.roll(x, LANES - half, 1) * s1 + pltpu.roll(x, half, 1) * s2
    return x * c - a if inverse else x * c + a


def _call(body, *, name, grid, in_specs, out_specs, out_shape, scratch_shapes, args, exchange=None, after=None):
    sem = ("arbitrary",) * len(grid)
    anyspec = pl.BlockSpec(memory_space=pl.ANY)
    if exchange is None and after is not None:
        n_own = len(in_specs)

        def behind(*refs):
            body(*refs[:n_own], *refs[n_own + 1:])

        return pl.pallas_call(behind, name=name, grid=grid, in_specs=list(in_specs) + [anyspec], out_specs=out_specs,
                              out_shape=out_shape, scratch_shapes=scratch_shapes, compiler_params=_params(sem))(*args, after)
    if exchange is None:
        return pl.pallas_call(body, name=name, grid=grid, in_specs=in_specs, out_specs=out_specs,
                              out_shape=out_shape, scratch_shapes=scratch_shapes, compiler_params=_params(sem))(*args)
    n_in, n_out, e = len(in_specs), len(out_specs), exchange.n
    total = math.prod(grid)

    def carried(*refs):
        own = refs[:n_in] + refs[n_in + e:n_in + e + n_out] + refs[n_in + 2 * e + n_out:len(refs) - 3]
        ex_refs = (refs[n_in:n_in + e], refs[n_in + e + n_out:n_in + 2 * e + n_out], refs[len(refs) - 3:])
        step = pl.program_id(0)
        for d in range(1, len(grid)):
            step = step * grid[d] + pl.program_id(d)

        @pl.when(step == 0)
        def _():
            exchange.phase(0, *ex_refs)

        @pl.when(step == (3 * total) // 4)
        def _():
            exchange.phase(1, *ex_refs)

        body(*own)

        @pl.when(step == total - 1)
        def _():
            exchange.phase(2, *ex_refs)

    return pl.pallas_call(
        carried, name=name, grid=grid, in_specs=list(in_specs) + [anyspec] * e,
        out_specs=list(out_specs) + [anyspec] * e, out_shape=list(out_shape) + exchange.out_shape,
        scratch_shapes=list(scratch_shapes) + exchange.scratch, compiler_params=_params(sem),
    )(*args, *exchange.operands)


def _ffn_fwd(h, pre_w, w1, w2, post_w, target, *, name, exchange=None):
    T, D = h.shape
    nk, ck = w2.shape[0], w2.shape[1]
    tT = min(FFN_FWD_TILE, T)
    nT = T // tT
    with_loss = target is not None

    def body(*refs):
        if with_loss:
            (h_ref, pre_ref, w1g_ref, w1u_ref, w2_ref, post_ref, tgt_ref,
             u_ref, f_ref, ho_ref, dy_ref, loss_ref, a_s, acc) = refs
        else:
            (h_ref, pre_ref, w1g_ref, w1u_ref, w2_ref, post_ref,
             u_ref, f_ref, ho_ref, a_s, acc) = refs
        k = pl.program_id(1)

        @pl.when(k == 0)
        def _():
            a_s[...] = _rms_fwd(h_ref[...], pre_ref[...]).astype(BF16)
            acc[...] = jnp.zeros_like(acc)

        for c in range(FFN_CHAINS):
            rs = slice(c * (tT // FFN_CHAINS), (c + 1) * (tT // FFN_CHAINS))
            a = a_s[rs, :]
            ug = _dot_nt(a, w1g_ref[...])
            uu = _dot_nt(a, w1u_ref[...])
            u_ref[0, rs, :] = ug.astype(BF16)
            u_ref[1, rs, :] = uu.astype(BF16)
            acc[rs, :] += _dot((ug * _sigmoid(ug) * uu).astype(BF16), w2_ref[...])

        @pl.when(k == nk - 1)
        def _():
            f = acc[...]
            f_ref[...] = f
            ho = h_ref[...] + 0.5 * _rms_fwd(f, post_ref[...])
            ho_ref[...] = ho
            if with_loss:
                e = ho - tgt_ref[...]
                dy_ref[...] = e * (1.0 / D)
                loss_ref[...] = jnp.full(loss_ref.shape, (0.5 / D) * jnp.sum(e * e), F32)

    row = pl.BlockSpec((tT, D), lambda i, k: (i, 0), pipeline_mode=pl.Buffered(1))
    vec = pl.BlockSpec((1, D), lambda i, k: (0, 0))
    in_specs = [row, vec,
                pl.BlockSpec((None, ck, D), lambda i, k: (k, 0, 0)),
                pl.BlockSpec((None, ck, D), lambda i, k: (nk + k, 0, 0)),
                pl.BlockSpec((None, ck, D), lambda i, k: (k, 0, 0)),
                vec]
    out_shape = [jax.ShapeDtypeStruct((2, nk, T, ck), BF16),
                 jax.ShapeDtypeStruct((T, D), F32),
                 jax.ShapeDtypeStruct((T, D), F32)]
    out_specs = [pl.BlockSpec((2, None, tT, ck), lambda i, k: (0, k, i, 0)), row, row]
    args = [h, pre_w, w1, w1, w2, post_w]
    if with_loss:
        in_specs.append(row)
        args.append(target)
        out_shape += [jax.ShapeDtypeStruct((T, D), F32), jax.ShapeDtypeStruct((nT * 8, LANES), F32)]
        out_specs += [row, pl.BlockSpec((8, LANES), lambda i, k: (i, 0))]
    return _call(body, name=name, grid=(nT, nk), in_specs=in_specs, out_specs=out_specs, out_shape=out_shape,
                 scratch_shapes=[pltpu.VMEM((tT, D), BF16), pltpu.VMEM((tT, D), F32)], args=args, exchange=exchange)


def _ffn_bwd(dho, f, post_w, h, pre_w, u, w2, w1, *, name, exchange=None, after=None):
    T, D = h.shape
    nk, ck = w2.shape[0], w2.shape[1]
    tT = min(TOKEN_TILE, T)
    nT = T // tT

    def body(dho_ref, f_ref, post_ref, h_ref, pre_ref, u_ref, w2_ref, w1g_ref, w1u_ref,
             g_ref, du_ref, df_ref, a_ref, dh_ref, gpost_ref, gpre_ref, df_s, da_acc):
        i, k = pl.program_id(0), pl.program_id(1)

        @pl.when(jnp.logical_and(i == 0, k == 0))
        def _():
            gpost_ref[...] = jnp.zeros_like(gpost_ref)
            gpre_ref[...] = jnp.zeros_like(gpre_ref)

        @pl.when(k == 0)
        def _():
            dx, dw = _rms_bwd(f_ref[...], post_ref[...], 0.5 * dho_ref[...])
            dfb = dx.astype(BF16)
            df_s[...] = dfb
            df_ref[...] = dfb
            gpost_ref[...] += dw
            a_ref[...] = _rms_fwd(h_ref[...], pre_ref[...]).astype(BF16)
            da_acc[...] = jnp.zeros_like(da_acc)

        groups = [slice(c * (tT // FFN_CHAINS), (c + 1) * (tT // FFN_CHAINS)) for c in range(FFN_CHAINS)]
        dgs = [_dot_nt(df_s[rs, :], w2_ref[...]) for rs in groups]
        for rs, dg in zip(groups, dgs):
            ug = u_ref[0, rs, :].astype(F32)
            uu = u_ref[1, rs, :].astype(F32)
            sg = _sigmoid(ug)
            sl = ug * sg
            g_ref[rs, :] = (sl * uu).astype(BF16)
            dug = (dg * uu * (sg + sl * (1.0 - sg))).astype(BF16)
            duu = (dg * sl).astype(BF16)
            du_ref[0, rs, :] = dug
            du_ref[1, rs, :] = duu
            da_acc[rs, :] += _dot(dug, w1g_ref[...]) + _dot(duu, w1u_ref[...])

        @pl.when(k == nk - 1)
        def _():
            dx, dw = _rms_bwd(h_ref[...], pre_ref[...], da_acc[...])
            dh_ref[...] = dho_ref[...] + dx
            gpre_ref[...] += dw

    row = pl.BlockSpec((tT, D), lambda i, k: (i, 0))
    vec = pl.BlockSpec((1, D), lambda i, k: (0, 0))
    return _call(
        body, name=name, grid=(nT, nk),
        in_specs=[row, row, vec, row, vec,
                  pl.BlockSpec((2, None, tT, ck), lambda i, k: (0, k, i, 0)),
                  pl.BlockSpec((None, ck, D), lambda i, k: (k, 0, 0)),
                  pl.BlockSpec((None, ck, D), lambda i, k: (k, 0, 0)),
                  pl.BlockSpec((None, ck, D), lambda i, k: (nk + k, 0, 0))],
        out_specs=[pl.BlockSpec((None, tT, ck), lambda i, k: (k, i, 0)),
                   pl.BlockSpec((2, None, tT, ck), lambda i, k: (0, k, i, 0)),
                   row, row, row, vec, vec],
        out_shape=[jax.ShapeDtypeStruct((nk, T, ck), BF16),
                   jax.ShapeDtypeStruct((2, nk, T, ck), BF16),
                   jax.ShapeDtypeStruct((T, D), BF16),
                   jax.ShapeDtypeStruct((T, D), BF16),
                   jax.ShapeDtypeStruct((T, D), F32),
                   jax.ShapeDtypeStruct((1, D), F32),
                   jax.ShapeDtypeStruct((1, D), F32)],
        scratch_shapes=[pltpu.VMEM((tT, D), BF16), pltpu.VMEM((tT, D), F32)],
        args=(dho, f, post_w, h, pre_w, u, w2, w1, w1), exchange=exchange, after=after)


def _matmul_tn(x, dy, *, name, exchange=None, after=None):
    Px, T, K = x.shape
    Py, _, N = dy.shape
    P = max(Px, Py)
    tT, tK, tN = min(GRAD_TOKEN_TILE, T), _tile(K, GRAD_TILE_CAP), _tile(N, GRAD_TILE_CAP)
    nt = T // tT

    def body(x_ref, dy_ref, o_ref, acc):
        t = pl.program_id(3)

        @pl.when(t == 0)
        def _():
            acc[...] = jnp.zeros_like(acc)

        acc[...] += _dot_tn(x_ref[...], dy_ref[...])

        @pl.when(t == nt - 1)
        def _():
            o_ref[...] = acc[...].astype(BF16)

    return _call(
        body, name=name, grid=(P, K // tK, N // tN, nt),
        in_specs=[pl.BlockSpec((None, tT, tK), lambda p, a, b, t: (p if Px > 1 else 0, t, a)),
                  pl.BlockSpec((None, tT, tN), lambda p, a, b, t: (p if Py > 1 else 0, t, b))],
        out_specs=[pl.BlockSpec((None, tK, tN), lambda p, a, b, t: (p, a, b))],
        out_shape=[jax.ShapeDtypeStruct((P, K, N), BF16)],
        scratch_shapes=[pltpu.VMEM((tK, tN), F32)], args=(x, dy), exchange=exchange, after=after)


def _rms_matmul(h, wn, w, *, name):
    T, D = h.shape
    N = w.shape[0]
    tT, tN = min(TOKEN_TILE, T), _tile(N, PROJ_TILE_CAP)

    def body(h_ref, wn_ref, w_ref, y_ref, a_ref):
        @pl.when(pl.program_id(1) == 0)
        def _():
            a_ref[...] = _rms_fwd(h_ref[...], wn_ref[...]).astype(BF16)

        y_ref[...] = _dot_nt(a_ref[...], w_ref[...]).astype(BF16)

    return pl.pallas_call(
        body, name=name, grid=(T // tT, N // tN),
        in_specs=[pl.BlockSpec((tT, D), lambda i, j: (i, 0)),
                  pl.BlockSpec((1, D), lambda i, j: (0, 0)),
                  pl.BlockSpec((tN, D), lambda i, j: (j, 0))],
        out_specs=[pl.BlockSpec((tT, tN), lambda i, j: (i, j)),
                   pl.BlockSpec((tT, D), lambda i, j: (i, 0))],
        out_shape=[jax.ShapeDtypeStruct((T, N), BF16), jax.ShapeDtypeStruct((T, D), BF16)],
        compiler_params=_params(("parallel", "arbitrary")),
    )(h, wn, w)


def _proj_bwd(dproj, w, h, wn, dres, *, name, exchange=None, after=None):
    T, D = h.shape
    N = w.shape[0]
    tT, tN = min(TOKEN_TILE, T), _tile(N, PROJ_TILE_CAP)
    nn = N // tN

    def body(dp_ref, w_ref, h_ref, wn_ref, dres_ref, dh_ref, gw_ref, acc):
        i, j = pl.program_id(0), pl.program_id(1)

        @pl.when(jnp.logical_and(i == 0, j == 0))
        def _():
            gw_ref[...] = jnp.zeros_like(gw_ref)

        @pl.when(j == 0)
        def _():
            acc[...] = jnp.zeros_like(acc)

        acc[...] += _dot(dp_ref[...], w_ref[...])

        @pl.when(j == nn - 1)
        def _():
            dx, dw = _rms_bwd(h_ref[...], wn_ref[...], acc[...])
            dh_ref[...] = dres_ref[...] + dx
            gw_ref[...] += dw

    row = pl.BlockSpec((tT, D), lambda i, j: (i, 0))
    vec = pl.BlockSpec((1, D), lambda i, j: (0, 0))
    return _call(
        body, name=name, grid=(T // tT, nn),
        in_specs=[pl.BlockSpec((tT, tN), lambda i, j: (i, j)),
                  pl.BlockSpec((tN, D), lambda i, j: (j, 0)), row, vec, row],
        out_specs=[row, vec],
        out_shape=[jax.ShapeDtypeStruct((T, D), F32), jax.ShapeDtypeStruct((1, D), F32)],
        scratch_shapes=[pltpu.VMEM((tT, D), F32)], args=(dproj, w, h, wn, dres), exchange=exchange, after=after)


def _mla_prep_fwd(proj, pos, qn_w, kvn_w, w_uq, w_kv, tab, *, name):
    T = proj.shape[0]
    tT = min(TOKEN_TILE, T)
    a_blk = PROJ_FIXED // AW - 1

    def body(a_ref, pos_ref, qnw_ref, kvnw_ref, wuq_ref, wkv_ref, tab_ref,
             q_ref, k_ref, v_ref, qn_ref, kvn_ref):
        cq = a_ref[:, 0:MLA_Q_RANK].astype(F32)
        ckv = a_ref[:, MLA_Q_RANK:MLA_Q_RANK + MLA_KV_RANK].astype(F32)
        kr = a_ref[:, 640:768].astype(F32)
        qn = _rms_fwd(cq, qnw_ref[...]).astype(BF16)
        kvn = _rms_fwd(ckv, kvnw_ref[...]).astype(BF16)
        qn_ref[...] = qn
        kvn_ref[...] = kvn
        cs = _rope_cs(pos_ref[...], tab_ref)
        q = _dot_nt(qn, wuq_ref[...])
        kv = _dot(kvn, wkv_ref[...])
        krr = _rope(kr, cs, MLA_ROPE // 2)
        for hd in range(MLA_HEADS):
            sl = slice(hd * HP, (hd + 1) * HP)
            q_ref[:, sl] = (_rope(q[:, sl], cs, MLA_ROPE // 2) * ATTN_SCALE).astype(BF16)
            k_ref[:, sl] = (kv[:, sl] + krr).astype(BF16)
        v_ref[...] = kv[:, QW:].astype(BF16)

    def full(r, c):
        return pl.BlockSpec((r, c), lambda i: (0, 0))

    def rows(c):
        return pl.BlockSpec((tT, c), lambda i: (i, 0))

    return pl.pallas_call(
        body, name=name, grid=(T // tT,),
        in_specs=[pl.BlockSpec((tT, AW), lambda i: (i, a_blk)), rows(1),
                  full(1, MLA_Q_RANK), full(1, MLA_KV_RANK),
                  full(QW, MLA_Q_RANK), full(MLA_KV_RANK, 2 * QW), full(8, LANES)],
        out_specs=[rows(QW), rows(QW), rows(QW), rows(MLA_Q_RANK), rows(MLA_KV_RANK)],
        out_shape=[jax.ShapeDtypeStruct((T, QW), BF16)] * 3
        + [jax.ShapeDtypeStruct((T, MLA_Q_RANK), BF16), jax.ShapeDtypeStruct((T, MLA_KV_RANK), BF16)],
        compiler_params=_params(("parallel",)),
    )(proj, pos, qn_w, kvn_w, w_uq, w_kv, tab)


def _mla_prep_bwd(dq, dk, dv, proj, pos, qn_w, kvn_w, w_uq, w_kv, tab, *, name):
    T = proj.shape[0]
    tT = min(TOKEN_TILE, T)
    a_blk = PROJ_FIXED // AW - 1

    def body(dq_ref, dk_ref, dv_ref, a_ref, pos_ref, qnw_ref, kvnw_ref, wuq_ref, wkv_ref, tab_ref,
             da_ref, dql_ref, dkvl_ref, gqn_ref, gkvn_ref):
        @pl.when(pl.program_id(0) == 0)
        def _():
            gqn_ref[...] = jnp.zeros_like(gqn_ref)
            gkvn_ref[...] = jnp.zeros_like(gkvn_ref)

        cs = _rope_cs(pos_ref[...], tab_ref)
        dkr = jnp.zeros((tT, HP), F32)
        for hd in range(MLA_HEADS):
            sl = slice(hd * HP, (hd + 1) * HP)
            dql_ref[:, sl] = (_rope(dq_ref[:, sl], cs, MLA_ROPE // 2, inverse=True) * ATTN_SCALE).astype(BF16)
            dkh = dk_ref[:, sl]
            dkr = dkr + dkh
            dkvl_ref[:, sl] = dkh.astype(BF16)
        dkvl_ref[:, QW:] = dv_ref[...]
        dqn = _dot(dql_ref[...], wuq_ref[...])
        dkvn = _dot_nt(dkvl_ref[...], wkv_ref[...])
        cq = a_ref[:, 0:MLA_Q_RANK].astype(F32)
        ckv = a_ref[:, MLA_Q_RANK:MLA_Q_RANK + MLA_KV_RANK].astype(F32)
        dcq, gq = _rms_bwd(cq, qnw_ref[...], dqn)
        dckv, gkv = _rms_bwd(ckv, kvnw_ref[...], dkvn)
        gqn_ref[...] += gq
        gkvn_ref[...] += gkv
        da_ref[:, 0:MLA_Q_RANK] = dcq.astype(BF16)
        da_ref[:, MLA_Q_RANK:MLA_Q_RANK + MLA_KV_RANK] = dckv.astype(BF16)
        da_ref[:, 640:768] = _rope(dkr, cs, MLA_ROPE // 2, inverse=True).astype(BF16)
        da_ref[:, 768:AW] = jnp.zeros((tT, AW - 768), BF16)

    def full(r, c):
        return pl.BlockSpec((r, c), lambda i: (0, 0))

    def rows(c):
        return pl.BlockSpec((tT, c), lambda i: (i, 0))

    return pl.pallas_call(
        body, name=name, grid=(T // tT,),
        in_specs=[rows(QW), rows(QW), rows(QW), pl.BlockSpec((tT, AW), lambda i: (i, a_blk)), rows(1),
                  full(1, MLA_Q_RANK), full(1, MLA_KV_RANK),
                  full(QW, MLA_Q_RANK), full(MLA_KV_RANK, 2 * QW), full(8, LANES)],
        out_specs=[rows(AW), rows(QW), rows(2 * QW), full(1, MLA_Q_RANK), full(1, MLA_KV_RANK)],
        out_shape=[jax.ShapeDtypeStruct((T, AW), BF16), jax.ShapeDtypeStruct((T, QW), BF16),
                   jax.ShapeDtypeStruct((T, 2 * QW), BF16),
                   jax.ShapeDtypeStruct((1, MLA_Q_RANK), F32), jax.ShapeDtypeStruct((1, MLA_KV_RANK), F32)],
        compiler_params=_params(("arbitrary",)),
    )(dq, dk, dv, proj, pos, qn_w, kvn_w, w_uq, w_kv, tab)


def _flash_fwd(q, k, v, *, name, exchange=None):
    T = q.shape[0]
    H = q.shape[1] // HP
    tq = min(ATTN_TILE, T)
    nq = T // tq

    sub = tq // ATTN_CHAINS

    def body(q_ref, k_ref, v_ref, o_ref, lse_ref):
        qi = pl.program_id(1)
        qs = [q_ref[c * sub:(c + 1) * sub, :] for c in range(ATTN_CHAINS)]

        def update(carry, off, masked):
            nks = [(c + 1) * sub if masked else tq for c in range(ATTN_CHAINS)]
            scores = [_dot_nt(qs[c], k_ref[pl.ds(off, nks[c]), :]) for c in range(ATTN_CHAINS)]
            out = []
            for c in range(ATTN_CHAINS):
                m_prev, l_prev, acc = carry[c]
                nk, s = nks[c], scores[c]
                vb = v_ref[pl.ds(off, nk), :]
                if masked:
                    rows = lax.broadcasted_iota(jnp.int32, (sub, nk), 0) + c * sub
                    s = jnp.where(rows >= lax.broadcasted_iota(jnp.int32, (sub, nk), 1), s, NEG)
                m_new = jnp.maximum(m_prev, jnp.max(s, axis=1, keepdims=True))
                alpha = jnp.exp(m_prev - m_new)
                p = jnp.exp(s - m_new)
                out.append((m_new, alpha * l_prev + jnp.sum(p, axis=1, keepdims=True),
                            alpha * acc + _dot(p.astype(BF16), vb)))
            return tuple(out)

        init = tuple((jnp.full((sub, 1), NEG, F32), jnp.zeros((sub, 1), F32), jnp.zeros((sub, HP), F32))
                     for _ in range(ATTN_CHAINS))
        carry = lax.fori_loop(0, qi, lambda j, cr: update(cr, pl.multiple_of(j * tq, tq), False), init)
        carry = update(carry, pl.multiple_of(qi * tq, tq), True)
        for c in range(ATTN_CHAINS):
            m_fin, l_fin, acc = carry[c]
            o_ref[c * sub:(c + 1) * sub, :] = (acc / l_fin).astype(BF16)
            lse_ref[c * sub:(c + 1) * sub, :] = jnp.broadcast_to(m_fin + jnp.log(l_fin), (sub, HP))

    qspec = pl.BlockSpec((tq, HP), lambda h, i: (i, h))
    kspec = pl.BlockSpec((T, HP), lambda h, i: (0, h))
    return _call(
        body, name=name, grid=(H, nq),
        in_specs=[qspec, kspec, kspec], out_specs=[qspec, qspec],
        out_shape=[jax.ShapeDtypeStruct((T, H * HP), BF16), jax.ShapeDtypeStruct((T, H * HP), F32)],
        scratch_shapes=[], args=(q, k, v), exchange=exchange)


def _flash_bwd(q, k, v, do, lse, delta, *, name, exchange=None, after=None):
    T = q.shape[0]
    H = q.shape[1] // HP
    tq = min(ATTN_TILE, T)
    nq = T // tq
    sub = tq // ATTN_CHAINS

    def body(k_ref, v_ref, q_ref, do_ref, lse_ref, dl_ref, dq_ref, dk_ref, dv_ref):
        ki = pl.program_id(1)

        @pl.when(ki == 0)
        def _():
            dq_ref[...] = jnp.zeros_like(dq_ref)

        def grow(a):
            return a if a.shape[0] == tq else jnp.concatenate([a, jnp.zeros((tq - a.shape[0], HP), F32)], axis=0)

        def step(carry, j, masked):
            dk_acc, dv_acc = carry
            nks = [(c + 1) * sub if masked else tq for c in range(ATTN_CHAINS)]
            rws = [pl.ds(pl.multiple_of(j * tq + c * sub, sub), sub) for c in range(ATTN_CHAINS)]
            scores = [_dot_nt(q_ref[rws[c], :], k_ref[0:nks[c], :]) for c in range(ATTN_CHAINS)]
            dps = [_dot_nt(do_ref[rws[c], :], v_ref[0:nks[c], :]) for c in range(ATTN_CHAINS)]
            for c in range(ATTN_CHAINS):
                rows, nk, s, dp = rws[c], nks[c], scores[c], dps[c]
                kb = k_ref[0:nk, :]
                qb = q_ref[rows, :]
                dob = do_ref[rows, :]
                if masked:
                    ri = lax.broadcasted_iota(jnp.int32, (sub, nk), 0) + c * sub
                    s = jnp.where(ri >= lax.broadcasted_iota(jnp.int32, (sub, nk), 1), s, NEG)
                p = jnp.exp(s - lse_ref[rows, 0:1])
                dv_acc = dv_acc + grow(_dot_tn(p.astype(BF16), dob))
                ds = (p * (dp - dl_ref[rows, 0:1])).astype(BF16)
                dk_acc = dk_acc + grow(_dot_tn(ds, qb))
                dq_ref[rows, :] += _dot(ds, kb)
            return dk_acc, dv_acc

        carry = step((jnp.zeros((tq, HP), F32), jnp.zeros((tq, HP), F32)), ki, True)
        dk_acc, dv_acc = lax.fori_loop(ki + 1, nq, lambda j, cr: step(cr, j, False), carry)
        dk_ref[...] = dk_acc
        dv_ref[...] = dv_acc.astype(BF16)

    kspec = pl.BlockSpec((tq, HP), lambda h, j: (j, h))
    full = pl.BlockSpec((T, HP), lambda h, j: (0, h))
    return _call(
        body, name=name, grid=(H, nq),
        in_specs=[kspec, kspec, full, full, full, full], out_specs=[full, kspec, kspec],
        out_shape=[jax.ShapeDtypeStruct((T, H * HP), F32), jax.ShapeDtypeStruct((T, H * HP), F32),
                   jax.ShapeDtypeStruct((T, H * HP), BF16)],
        scratch_shapes=[], args=(k, v, q, do, lse, delta), exchange=exchange, after=after)


def _ret_consts(cc, hd):
    lg = math.log(1.0 - 2.0 ** (-5.0 - hd))
    diff = (lax.broadcasted_iota(jnp.int32, (cc, cc), 0) - lax.broadcasted_iota(jnp.int32, (cc, cc), 1)).astype(F32)
    decay = jnp.where(diff >= 0, jnp.exp(jnp.maximum(diff, 0.0) * lg), 0.0)
    idx = lax.broadcasted_iota(jnp.int32, (cc, 1), 0).astype(F32)
    zeta = jnp.exp((cc - 1.0 - idx) * lg)
    xi = jnp.exp((idx + 1.0) * lg)
    return decay, zeta, xi, math.exp(cc * lg)


def _ret_fwd(proj, pos, tab, *, name):
    T = proj.shape[0]
    cc = min(RET_TILE, T)
    n = T // cc

    def body(rq_ref, rk_ref, rv_ref, pos_ref, tab_ref, y_ref, yn_ref, rprev_ref, r_s):
        @pl.when(pl.program_id(0) == 0)
        def _():
            r_s[...] = jnp.zeros_like(r_s)

        cs = _rope_cs(pos_ref[...], tab_ref)
        for hd in range(RET_HEADS):
            sl = slice(hd * HP, (hd + 1) * HP)
            decay, zeta, xi, gc = _ret_consts(cc, hd)
            q = _rope(rq_ref[:, sl].astype(F32), cs, RET_DK // 2).astype(BF16)
            kf = _rope(rk_ref[:, sl].astype(F32), cs, RET_DK // 2) * (RET_DK ** -0.5)
            k = kf.astype(BF16)
            v = rv_ref[:, sl]
            r = r_s[hd]
            rprev_ref[0, hd] = r
            inner = (_dot_nt(q, k) * decay).astype(BF16)
            y = _dot(inner, v) + _dot(q, r.astype(BF16)) * xi
            r_s[hd] = r * gc + _dot_tn((kf * zeta).astype(BF16), v)
            y_ref[:, sl] = y
            mu = jnp.mean(y, axis=-1, keepdims=True)
            yc = y - mu
            var = jnp.mean(yc * yc, axis=-1, keepdims=True)
            yn_ref[:, sl] = (yc * lax.rsqrt(var + GN_EPS)).astype(BF16)

    def blk(j):
        return pl.BlockSpec((cc, RW), lambda i: (i, j))

    return pl.pallas_call(
        body, name=name, grid=(n,),
        in_specs=[blk(0), blk(1), blk(2), pl.BlockSpec((cc, 1), lambda i: (i, 0)),
                  pl.BlockSpec((8, LANES), lambda i: (0, 0))],
        out_specs=[blk(0), blk(0), pl.BlockSpec((1, RET_HEADS, HP, RET_DV), lambda i: (i, 0, 0, 0))],
        out_shape=[jax.ShapeDtypeStruct((T, RW), F32), jax.ShapeDtypeStruct((T, RW), BF16),
                   jax.ShapeDtypeStruct((n, RET_HEADS, HP, RET_DV), F32)],
        scratch_shapes=[pltpu.VMEM((RET_HEADS, HP, RET_DV), F32)],
        compiler_params=_params(("arbitrary",)),
    )(proj, proj, proj, pos, tab)


def _ret_bwd(dyn, y, proj, pos, tab, rprev, *, name):
    T = proj.shape[0]
    cc = min(RET_TILE, T)
    n = T // cc

    def body(dyn_ref, y_ref, rq_ref, rk_ref, rv_ref, pos_ref, tab_ref, rprev_ref,
             drq_ref, drk_ref, drv_ref, dr_s):
        @pl.when(pl.program_id(0) == 0)
        def _():
            dr_s[...] = jnp.zeros_like(dr_s)

        cs = _rope_cs(pos_ref[...], tab_ref)
        for hd in range(RET_HEADS):
            sl = slice(hd * HP, (hd + 1) * HP)
            decay, zeta, xi, gc = _ret_consts(cc, hd)
            q = _rope(rq_ref[:, sl].astype(F32), cs, RET_DK // 2).astype(BF16)
            kf = _rope(rk_ref[:, sl].astype(F32), cs, RET_DK // 2) * (RET_DK ** -0.5)
            k = kf.astype(BF16)
            v = rv_ref[:, sl]
            yv = y_ref[:, sl]
            mu = jnp.mean(yv, axis=-1, keepdims=True)
            yc = yv - mu
            rs = lax.rsqrt(jnp.mean(yc * yc, axis=-1, keepdims=True) + GN_EPS)
            yn = yc * rs
            dn = dyn_ref[:, sl]
            dy = rs * (dn - jnp.mean(dn, axis=-1, keepdims=True) - yn * jnp.mean(dn * yn, axis=-1, keepdims=True))
            dyb = dy.astype(BF16)
            dyx = (dy * xi).astype(BF16)
            dr = dr_s[hd]
            drb = dr.astype(BF16)
            inner = (_dot_nt(q, k) * decay).astype(BF16)
            da = (_dot_nt(dyb, v) * decay).astype(BF16)
            dv = _dot_tn(inner, dyb) + _dot((kf * zeta).astype(BF16), drb)
            dq = _dot(da, k) + _dot_nt(dyx, rprev_ref[0, hd].astype(BF16))
            dk = _dot_tn(da, q) + _dot_nt(v, drb) * zeta
            dr_s[hd] = dr * gc + _dot_tn(q, dyx)
            drq_ref[:, sl] = _rope(dq, cs, RET_DK // 2, inverse=True).astype(BF16)
            drk_ref[:, sl] = _rope(dk * (RET_DK ** -0.5), cs, RET_DK // 2, inverse=True).astype(BF16)
            drv_ref[:, sl] = dv.astype(BF16)

    def blk(j):
        return pl.BlockSpec((cc, RW), lambda i: (n - 1 - i, j))

    return pl.pallas_call(
        body, name=name, grid=(n,),
        in_specs=[blk(0), blk(0), blk(0), blk(1), blk(2), pl.BlockSpec((cc, 1), lambda i: (n - 1 - i, 0)),
                  pl.BlockSpec((8, LANES), lambda i: (0, 0)),
                  pl.BlockSpec((1, RET_HEADS, HP, RET_DV), lambda i: (n - 1 - i, 0, 0, 0))],
        out_specs=[blk(0), blk(0), blk(0)],
        out_shape=[jax.ShapeDtypeStruct((T, RW), BF16)] * 3,
        scratch_shapes=[pltpu.VMEM((RET_HEADS, HP, RET_DV), F32)],
        compiler_params=_params(("arbitrary",)),
    )(dyn, y, proj, proj, proj, pos, tab, rprev)


def _merge_fwd(o, yn, proj, gn_w, w_bm, w_br, w_out, h, post_w, *, name):
    T, D = h.shape
    tT = min(TOKEN_TILE, T)
    g_blk = PROJ_FIXED // D

    def body(o_ref, yn_ref, rg_ref, gm_ref, gr_ref, gnw_ref, wbm_ref, wbr_ref, wout_ref, h_ref, post_ref,
             omla_ref, oret_ref, m_ref, ho_ref):
        groups = [slice(c * (tT // FFN_CHAINS), (c + 1) * (tT // FFN_CHAINS)) for c in range(FFN_CHAINS)]
        o_mlas = [_dot(o_ref[rs, :], wbm_ref[...]) for rs in groups]
        for rs, o_mla in zip(groups, o_mlas):
            rg = rg_ref[rs, :].astype(F32)
            gated = (rg * _sigmoid(rg) * (yn_ref[rs, :].astype(F32) * gnw_ref[...])).astype(BF16)
            o_ret = _dot(gated, wbr_ref[...])
            omla_ref[rs, :] = o_mla.astype(BF16)
            oret_ref[rs, :] = o_ret.astype(BF16)
            merged = _sigmoid(gm_ref[rs, :].astype(F32)) * o_mla + _sigmoid(gr_ref[rs, :].astype(F32)) * o_ret
            m = _dot(merged.astype(BF16), wout_ref[...])
            m_ref[rs, :] = m
            ho_ref[rs, :] = h_ref[rs, :] + _rms_fwd(m, post_ref[...])

    def full(r, c):
        return pl.BlockSpec((r, c), lambda i: (0, 0))

    def rows(c, j=0):
        return pl.BlockSpec((tT, c), lambda i: (i, j))

    return pl.pallas_call(
        body, name=name, grid=(T // tT,),
        in_specs=[rows(QW), rows(RW), rows(RW, 3), rows(D, g_blk), rows(D, g_blk + 1), full(1, RW),
                  full(QW, D), full(RW, D), full(D, D), rows(D), full(1, D)],
        out_specs=[rows(D), rows(D), rows(D), rows(D)],
        out_shape=[jax.ShapeDtypeStruct((T, D), BF16), jax.ShapeDtypeStruct((T, D), BF16),
                   jax.ShapeDtypeStruct((T, D), F32), jax.ShapeDtypeStruct((T, D), F32)],
        compiler_params=_params(("parallel",)),
    )(o, yn, proj, proj, proj, gn_w, w_bm, w_br, w_out, h, post_w)


def _merge_bwd(dho, m, post_w, omla, oret, proj, yn, gn_w, o, w_out, w_bm, w_br, *, name):
    T, D = dho.shape
    tT = min(MERGE_TILE, T)
    g_blk = PROJ_FIXED // D

    def body(dho_ref, m_ref, post_ref, omla_ref, oret_ref, rg_ref, gm_ref, gr_ref, yn_ref, gnw_ref, o_ref,
             wout_ref, wbm_ref, wbr_ref,
             dm_ref, merged_ref, dgm_ref, dgr_ref, domla_ref, do_ref, delta_ref, doret_ref, gated_ref,
             drg_ref, dyn_ref, gpost_ref, ggn_ref):
        @pl.when(pl.program_id(0) == 0)
        def _():
            gpost_ref[...] = jnp.zeros_like(gpost_ref)
            ggn_ref[...] = jnp.zeros_like(ggn_ref)

        dm, gp = _rms_bwd(m_ref[...], post_ref[...], dho_ref[...])
        gpost_ref[...] += gp
        dmb = dm.astype(BF16)
        dm_ref[...] = dmb
        dmerged = _dot_nt(dmb, wout_ref[...])
        o_mla = omla_ref[...].astype(F32)
        o_ret = oret_ref[...].astype(F32)
        sgm = _sigmoid(gm_ref[...].astype(F32))
        sgr = _sigmoid(gr_ref[...].astype(F32))
        merged_ref[...] = (sgm * o_mla + sgr * o_ret).astype(BF16)
        dgm_ref[...] = (dmerged * o_mla * sgm * (1.0 - sgm)).astype(BF16)
        dgr_ref[...] = (dmerged * o_ret * sgr * (1.0 - sgr)).astype(BF16)
        domla = (dmerged * sgm).astype(BF16)
        domla_ref[...] = domla
        do = _dot_nt(domla, wbm_ref[...])
        do_ref[...] = do.astype(BF16)
        for hd in range(MLA_HEADS):
            sl = slice(hd * HP, (hd + 1) * HP)
            d = jnp.sum(do[:, sl] * o_ref[:, sl].astype(F32), axis=-1, keepdims=True)
            delta_ref[:, sl] = jnp.broadcast_to(d, (tT, HP))
        doret = (dmerged * sgr).astype(BF16)
        doret_ref[...] = doret
        dgated = _dot_nt(doret, wbr_ref[...])
        rg = rg_ref[...].astype(F32)
        sg = _sigmoid(rg)
        srg = rg * sg
        ynv = yn_ref[...].astype(F32)
        yw = ynv * gnw_ref[...]
        gated_ref[...] = (srg * yw).astype(BF16)
        drg_ref[...] = (dgated * yw * (sg * (1.0 + rg * (1.0 - sg)))).astype(BF16)
        dgs = dgated * srg
        dyn_ref[...] = dgs * gnw_ref[...]
        ggn_ref[...] += jnp.sum(dgs * ynv, axis=0, keepdims=True)

    def full(r, c):
        return pl.BlockSpec((r, c), lambda i: (0, 0))

    def rows(c, j=0):
        return pl.BlockSpec((tT, c), lambda i: (i, j))

    return pl.pallas_call(
        body, name=name, grid=(T // tT,),
        in_specs=[rows(D), rows(D), full(1, D), rows(D), rows(D), rows(RW, 3), rows(D, g_blk), rows(D, g_blk + 1),
                  rows(RW), full(1, RW), rows(QW), full(D, D), full(QW, D), full(RW, D)],
        out_specs=[rows(D), rows(D), rows(D), rows(D), rows(D), rows(QW), rows(QW), rows(D), rows(RW),
                   rows(RW), rows(RW), full(1, D), full(1, RW)],
        out_shape=[jax.ShapeDtypeStruct((T, D), BF16)] * 5
        + [jax.ShapeDtypeStruct((T, QW), BF16), jax.ShapeDtypeStruct((T, QW), F32),
           jax.ShapeDtypeStruct((T, D), BF16), jax.ShapeDtypeStruct((T, RW), BF16),
           jax.ShapeDtypeStruct((T, RW), BF16), jax.ShapeDtypeStruct((T, RW), F32),
           jax.ShapeDtypeStruct((1, D), F32), jax.ShapeDtypeStruct((1, RW), F32)],
        compiler_params=_params(("arbitrary",)),
    )(dho, m, post_w, omla, oret, proj, proj, proj, yn, gn_w, o, w_out, w_bm, w_br)


def _mesh_pos():
    return lax.axis_index("x"), lax.axis_index("y"), lax.axis_index("c")


class _Gather:
    def __init__(self, shards):
        self.operands = list(shards)
        self.n = len(shards)
        self.out_shape = [jax.ShapeDtypeStruct((N_DEV,) + s.shape, s.dtype) for s in shards]
        self.scratch = [pltpu.SemaphoreType.DMA((7 * self.n,)), pltpu.SemaphoreType.DMA((7 * self.n,)),
                        pltpu.SemaphoreType.DMA((self.n,))]

    def phase(self, p, x_refs, out_refs, sems):
        send_sems, recv_sems, local_sems = sems
        x, y, c = _mesh_pos()
        me, sibling = (x, y, c), (x, y, 1 - c)
        chips = [(1 - x, y), (x, 1 - y), (1 - x, 1 - y)]

        def copy(w, k, block, to, src=None):
            slot = out_refs[w].at[4 * block[0] + 2 * block[1] + block[2]]
            return pltpu.make_async_remote_copy(
                src_ref=slot if src is None else src, dst_ref=slot,
                send_sem=send_sems.at[7 * w + k], recv_sem=recv_sems.at[7 * w + k],
                device_id=to, device_id_type=pl.DeviceIdType.MESH)

        for w in range(self.n):
            mine = pltpu.make_async_copy(x_refs[w], out_refs[w].at[4 * x + 2 * y + c], local_sems.at[w])
            first = [copy(w, 0, me, sibling, src=x_refs[w])]
            first += [copy(w, 1 + j, me, (*chip, c), src=x_refs[w]) for j, chip in enumerate(chips)]
            passed = [copy(w, 4 + j, (*chip, c), sibling) for j, chip in enumerate(chips)]
            if p == 0:
                mine.start()
                for cp in first:
                    cp.start()
            elif p == 1:
                for j, chip in enumerate(chips):
                    copy(w, 1 + j, (*chip, c), me).wait_recv()
                    passed[j].start()
            else:
                copy(w, 0, sibling, me).wait_recv()
                for j, chip in enumerate(chips):
                    copy(w, 4 + j, (*chip, 1 - c), me).wait_recv()
                for cp in first + passed:
                    cp.wait_send()
                mine.wait()


class _Scatter:
    def __init__(self, grads, whole=()):
        self.n_sliced = len(grads)
        self.operands = list(grads) + list(whole)
        self.n = len(self.operands)
        self.out_shape = [jax.ShapeDtypeStruct(g.shape, g.dtype) for g in grads]
        self.out_shape += [jax.ShapeDtypeStruct((N_DEV,) + a.shape, a.dtype) for a in whole]
        n_sem = (N_DEV - 1) * self.n
        self.scratch = [pltpu.SemaphoreType.DMA((n_sem,)), pltpu.SemaphoreType.DMA((n_sem,)),
                        pltpu.SemaphoreType.DMA((self.n,))]

    def phase(self, p, in_refs, out_refs, sems):
        if p == 1:
            return
        send_sems, recv_sems, local_sems = sems
        x, y, c = _mesh_pos()
        me = 4 * x + 2 * y + c

        def src(w, dev):
            return in_refs[w].at[dev] if w < self.n_sliced else in_refs[w]

        for w in range(self.n):
            own = None if local_sems is None else pltpu.make_async_copy(src(w, me), out_refs[w].at[me], local_sems.at[w])
            sends, recvs = [], []
            for r in range(1, N_DEV):
                px = 1 - x if r & 4 else x
                py = 1 - y if r & 2 else y
                pc = 1 - c if r & 1 else c
                peer, pidx = (px, py, pc), 4 * px + 2 * py + pc
                k = (N_DEV - 1) * w + r - 1
                sends.append(pltpu.make_async_remote_copy(
                    src_ref=src(w, pidx), dst_ref=out_refs[w].at[me], send_sem=send_sems.at[k],
                    recv_sem=recv_sems.at[k], device_id=peer, device_id_type=pl.DeviceIdType.MESH))
                recvs.append(pltpu.make_async_remote_copy(
                    src_ref=src(w, me), dst_ref=out_refs[w].at[pidx], send_sem=send_sems.at[k],
                    recv_sem=recv_sems.at[k], device_id=peer, device_id_type=pl.DeviceIdType.MESH))
            if p == 0:
                if own is not None:
                    own.start()
                for cp in sends:
                    cp.start()
            else:
                for cp in recvs:
                    cp.wait_recv()
                for cp in sends:
                    cp.wait_send()
                if own is not None:
                    own.wait()


class _SplitScatter:
    def __init__(self, ex, name):
        self.ex, self.name = ex, name

    def _specs(self):
        ex = self.ex
        hbm = pl.BlockSpec(memory_space=pltpu.HBM)
        sem = pl.BlockSpec(memory_space=pltpu.SEMAPHORE)
        effect = pltpu.CompilerParams(has_side_effects=pltpu.SideEffectType.DATAFLOW_SIDE_EFFECTING)
        buffers = [pltpu.HBM(a.shape, a.dtype) for a in ex.operands] + [pltpu.HBM(s.shape, s.dtype) for s in ex.out_shape]
        return hbm, sem, effect, buffers

    def start(self):
        ex, n = self.ex, self.ex.n
        n_sem = (N_DEV - 1) * n
        hbm, sem, effect, buffers = self._specs()
        in_hbm = lambda a: pltpu.with_memory_space_constraint(a, pltpu.HBM)

        me = 4 * lax.axis_index("x") + 2 * lax.axis_index("y") + lax.axis_index("c")
        lands = []
        for w, (a, s) in enumerate(zip(ex.operands, ex.out_shape)):
            mine = lax.dynamic_index_in_dim(a, me, 0, keepdims=True) if w < ex.n_sliced else a[None]
            lands.append(lax.dynamic_update_slice_in_dim(lax.empty(s.shape, s.dtype), mine, me, 0))

        def start_body(*refs):
            ex.phase(0, refs[:n], refs[n:2 * n], (refs[2 * n], refs[2 * n + 1], None))
            refs[-1][...] = jnp.zeros_like(refs[-1])

        self.started = pl.pallas_call(
            start_body, name=self.name + "_start",
            out_shape=[pltpu.SemaphoreType.DMA((n_sem,)), pltpu.SemaphoreType.DMA((n_sem,))] + buffers
            + [jax.ShapeDtypeStruct((8, LANES), F32)],
            in_specs=[hbm] * (2 * n), out_specs=[sem, sem] + [hbm] * (2 * n) + [pl.BlockSpec(memory_space=pltpu.VMEM)],
            input_output_aliases={i: 2 + i for i in range(2 * n)}, compiler_params=effect,
        )(*[in_hbm(a) for a in ex.operands], *[in_hbm(a) for a in lands])
        return self.started[-1]

    def wait(self, after):
        ex, n = self.ex, self.ex.n
        hbm, sem, effect, buffers = self._specs()
        anyspec = pl.BlockSpec(memory_space=pl.ANY)

        def wait_body(*refs):
            ex.phase(2, refs[:n], refs[n:2 * n], (refs[2 * n], refs[2 * n + 1], None))

        done = pl.pallas_call(
            wait_body, name=self.name + "_wait", out_shape=buffers,
            in_specs=[hbm] * (2 * n) + [sem, sem] + [anyspec] * len(after), out_specs=[hbm] * (2 * n),
            input_output_aliases={i: i for i in range(2 * n)}, compiler_params=effect,
        )(*self.started[2:2 + 2 * n], self.started[0], self.started[1], *after)
        return done[n:]


def _exchange_alone(ex, *, name):
    n = ex.n

    def body(*refs):
        for p in range(3):
            ex.phase(p, refs[:n], refs[n:2 * n], refs[2 * n:])

    anyspec = pl.BlockSpec(memory_space=pl.ANY)
    return pl.pallas_call(body, name=name, out_shape=ex.out_shape, in_specs=[anyspec] * n,
                          out_specs=[anyspec] * n, scratch_shapes=ex.scratch)(*ex.operands)


def _adam_step(w_ref, p_ref, m_ref, v_ref, g_ref, d_ref, nm_ref, nv_ref):
    g = p_ref[0].astype(F32)
    for j in range(1, N_DEV):
        g = g + p_ref[j].astype(F32)
    g_ref[...] = g
    nm = ADAM_B1 * m_ref[...] + (1.0 - ADAM_B1) * g
    nv = ADAM_B2 * v_ref[...] + (1.0 - ADAM_B2) * (g * g)
    nm_ref[...] = nm
    nv_ref[...] = nv
    m_hat = nm / (1.0 - ADAM_B1 ** ADAM_STEP)
    v_hat = nv / (1.0 - ADAM_B2 ** ADAM_STEP)
    d_ref[...] = -ADAM_LR * (m_hat / (jnp.sqrt(v_hat) + ADAM_EPS) + ADAM_WD * w_ref[...])


def _adamw_vectors(ws, parts, ms, vs, *, name):
    n = len(ws)

    def body(*refs):
        w_refs, p_refs, m_refs, v_refs = (refs[i * n:(i + 1) * n] for i in range(4))
        outs = refs[4 * n:]
        for i in range(n):
            _adam_step(w_refs[i], p_refs[i], m_refs[i], v_refs[i], *outs[4 * i:4 * i + 4])

    return pl.pallas_call(
        body, name=name,
        out_shape=[jax.ShapeDtypeStruct(w.shape, F32) for w in ws for _ in range(4)],
    )(*ws, *parts, *ms, *vs)


def _adamw(w, parts, m, v, after, *, name):
    G, R, n = w.shape
    tn = 256 if (n > 256 and n % 256 == 0) else n
    tr = R
    for t in range(16, R, 16):
        if R % t == 0 and t * tn <= 160 * 1024:
            tr = t
    if R * tn <= 160 * 1024:
        tr = R

    def body(w_ref, p_ref, m_ref, v_ref, after_ref, g_ref, d_ref, nm_ref, nv_ref):
        _adam_step(w_ref, p_ref, m_ref, v_ref, g_ref, d_ref, nm_ref, nv_ref)

    blk = pl.BlockSpec((None, tr, tn), lambda g, i, j: (g, i, j))
    return pl.pallas_call(
        body, name=name, grid=(G, R // tr, n // tn),
        in_specs=[blk, pl.BlockSpec((N_DEV, None, tr, tn), lambda g, i, j: (0, g, i, j)), blk, blk,
                  pl.BlockSpec((8, LANES), lambda g, i, j: (0, 0))],
        out_specs=[blk, blk, blk, blk],
        out_shape=[jax.ShapeDtypeStruct((G, R, n), F32)] * 4,
        compiler_params=_params(("parallel", "parallel", "parallel")),
    )(w, parts, m, v, after)


def _pad_last(a, width):
    return jnp.pad(a, [(0, 0)] * (a.ndim - 1) + [(0, width - a.shape[-1])])


def _cols_of(g):
    return g.transpose(1, 0, 2).reshape(g.shape[1], N_DEV * g.shape[2])


def _col_shards(w):
    return w.reshape(w.shape[0], N_DEV, w.shape[1] // N_DEV).transpose(1, 0, 2)


def kernel(x, positions, ffn1_pre_w, ffn1_w1, ffn1_w2, ffn1_post_w, mix_pre_w, w_in, mla_q_norm_w, mla_w_uq, mla_kv_norm_w, mla_w_ukv, ret_gn_w, w_branch_mla, w_branch_ret, w_out, mix_post_w, ffn2_pre_w, ffn2_w1, ffn2_w2, ffn2_post_w, loss_target, m_ffn1_pre_w, m_ffn1_w1, m_ffn1_w2, m_ffn1_post_w, m_mix_pre_w, m_w_in, m_mla_q_norm_w, m_mla_w_uq, m_mla_kv_norm_w, m_mla_w_ukv, m_ret_gn_w, m_w_branch_mla, m_w_branch_ret, m_w_out, m_mix_post_w, m_ffn2_pre_w, m_ffn2_w1, m_ffn2_w2, m_ffn2_post_w, v_ffn1_pre_w, v_ffn1_w1, v_ffn1_w2, v_ffn1_post_w, v_mix_pre_w, v_w_in, v_mla_q_norm_w, v_mla_w_uq, v_mla_kv_norm_w, v_mla_w_ukv, v_ret_gn_w, v_w_branch_mla, v_w_branch_ret, v_w_out, v_mix_post_w, v_ffn2_pre_w, v_ffn2_w1, v_ffn2_w2, v_ffn2_post_w):
    T, D = x.shape[1], x.shape[2]
    h0 = x[0]
    tgt = loss_target[0]
    pos = positions.reshape(T, 1).astype(F32)

    big = [("ffn1_w1", ffn1_w1, m_ffn1_w1, v_ffn1_w1), ("ffn1_w2", ffn1_w2, m_ffn1_w2, v_ffn1_w2),
           ("w_in", w_in, m_w_in, v_w_in), ("mla_w_uq", mla_w_uq, m_mla_w_uq, v_mla_w_uq),
           ("mla_w_ukv", mla_w_ukv, m_mla_w_ukv, v_mla_w_ukv),
           ("w_branch_mla", w_branch_mla, m_w_branch_mla, v_w_branch_mla),
           ("w_branch_ret", w_branch_ret, m_w_branch_ret, v_w_branch_ret),
           ("w_out", w_out, m_w_out, v_w_out),
           ("ffn2_w1", ffn2_w1, m_ffn2_w1, v_ffn2_w1), ("ffn2_w2", ffn2_w2, m_ffn2_w2, v_ffn2_w2)]
    small = [("ffn1_pre_w", ffn1_pre_w, m_ffn1_pre_w, v_ffn1_pre_w), ("ffn1_post_w", ffn1_post_w, m_ffn1_post_w, v_ffn1_post_w),
             ("mix_pre_w", mix_pre_w, m_mix_pre_w, v_mix_pre_w), ("mla_q_norm_w", mla_q_norm_w, m_mla_q_norm_w, v_mla_q_norm_w),
             ("mla_kv_norm_w", mla_kv_norm_w, m_mla_kv_norm_w, v_mla_kv_norm_w), ("ret_gn_w", ret_gn_w, m_ret_gn_w, v_ret_gn_w),
             ("mix_post_w", mix_post_w, m_mix_post_w, v_mix_post_w), ("ffn2_pre_w", ffn2_pre_w, m_ffn2_pre_w, v_ffn2_pre_w),
             ("ffn2_post_w", ffn2_post_w, m_ffn2_post_w, v_ffn2_post_w)]

    half = ffn1_w2.shape[1]
    hp = -(-half // LANES) * LANES

    def rows_view(w):
        return w[0].T

    def send_w1(w):
        return jnp.pad(rows_view(w).reshape(2, half, D), ((0, 0), (0, hp - half), (0, 0))).reshape(2 * hp, D).astype(BF16)

    def send_w2(w):
        return jnp.pad(w[0], ((0, hp - half), (0, 0))).astype(BF16)

    mixer = ["w_in", "mla_w_uq", "mla_w_ukv", "w_branch_mla", "w_branch_ret", "w_out"]
    uq_w = MLA_NOPE + MLA_ROPE
    mixer_send = [rows_view(w_in).astype(BF16), jnp.pad(rows_view(mla_w_uq), ((0, HP - uq_w), (0, 0))).astype(BF16),
                  mla_w_ukv[0].astype(BF16), w_branch_mla[0].astype(BF16), w_branch_ret[0].astype(BF16),
                  w_out[0].astype(BF16)]

    w1a, w2a = _exchange_alone(_Gather([send_w1(ffn1_w1), send_w2(ffn1_w2)]), name="gather_ffn1")
    w2a = w2a.reshape(N_DEV // 2, 2 * hp, D)
    u1, f1, h1, *got = _ffn_fwd(h0, ffn1_pre_w, w1a, w2a, ffn1_post_w, None, name="ffn1_fwd_gather_mixer",
                                exchange=_Gather(mixer_send))
    fw = dict(zip(mixer, got))

    wi = fw["w_in"].reshape(-1, D)
    cq_w, ckv_w, kr_w = wi[0:384], wi[384:640], wi[640:672]
    rq_w, rk_w = wi[672:928], wi[928:1184]
    rv_w, rg_w = wi[1184:1696], wi[1696:2208]
    gm_w, gr_w = wi[2208:2208 + D], wi[2208 + D:2208 + 2 * D]
    zer = lambda n: jnp.zeros((n, D), BF16)
    head_rows = lambda a, h: jnp.pad(a.reshape(h, -1, D), ((0, 0), (0, HP - a.shape[0] // h), (0, 0))).reshape(h * HP, D)
    w_in_p = jnp.concatenate([head_rows(rq_w, RET_HEADS), head_rows(rk_w, RET_HEADS), rv_w, rg_w,
                              cq_w, ckv_w, zer(MLA_NOPE), kr_w, zer(HP - MLA_NOPE - MLA_ROPE), zer(AW - 768),
                              gm_w, gr_w], axis=0)
    w_uq_p = fw["mla_w_uq"].reshape(QW, MLA_Q_RANK)
    ukv = fw["mla_w_ukv"].transpose(1, 0, 2)
    w_kv_p = jnp.concatenate([_pad_last(ukv[:, :, :MLA_NOPE], HP).reshape(MLA_KV_RANK, QW),
                              _pad_last(ukv[:, :, MLA_NOPE:], HP).reshape(MLA_KV_RANK, QW)], axis=1)
    w_bm_p = jnp.pad(_cols_of(fw["w_branch_mla"]).reshape(MLA_HEADS, MLA_V, D),
                     ((0, 0), (0, HP - MLA_V), (0, 0))).reshape(QW, D)
    w_br, w_o = _cols_of(fw["w_branch_ret"]), fw["w_out"].reshape(D, D)
    tab_mla = _rope_table(MLA_NOPE, MLA_ROPE // 2)
    tab_ret = _rope_table(0, RET_DK // 2)

    proj, a1 = _rms_matmul(h1, mix_pre_w, w_in_p, name="mixer_in_proj")
    q, k, v, qn, kvn = _mla_prep_fwd(proj, pos, mla_q_norm_w, mla_kv_norm_w, w_uq_p, w_kv_p, tab_mla, name="mla_prep_fwd")
    o, lse, w1b, w2b = _flash_fwd(q, k, v, name="mla_attn_fwd_gather_ffn2",
                                  exchange=_Gather([send_w1(ffn2_w1), send_w2(ffn2_w2)]))
    w2b = w2b.reshape(N_DEV // 2, 2 * hp, D)
    ypre, yn, rprev = _ret_fwd(proj, pos, tab_ret, name="retention_fwd")
    omla, oret, m, h2 = _merge_fwd(o, yn, proj, ret_gn_w, w_bm_p, w_br, w_o, h1, mix_post_w, name="merge_fwd")
    u2, f2, _, dy, lossp = _ffn_fwd(h2, ffn2_pre_w, w1b, w2b, ffn2_post_w, tgt, name="ffn2_fwd_loss")

    def grad(x, dy, tag, after=None):
        return _matmul_tn(x if x.ndim == 3 else x[None], dy if dy.ndim == 3 else dy[None], name=tag, after=after)

    g2, du2, df2, a2, dh2, gpost2, gpre2 = _ffn_bwd(dy, f2, ffn2_post_w, h2, ffn2_pre_w, u2, w2b, w1b, name="ffn2_bwd")
    dw1b, = grad(du2.reshape(N_DEV, T, 2 * hp), a2, "ffn2_dw1")
    dw2b = grad(g2, df2, "ffn2_dw2")[0].reshape(N_DEV, hp, D)
    (dmb, merged, dgm, dgr, domla, do, delta, doret, gated, drg, dyn, gpostm, ggn) = _merge_bwd(
        dh2, m, mix_post_w, omla, oret, proj, yn, ret_gn_w, o, w_o, w_bm_p, w_br, name="merge_bwd")
    dw_out = grad(merged, dmb, "dw_out")[0][0]
    dw_bm_p = grad(o, domla, "dw_branch_mla")[0][0]
    dw_br = grad(gated, doret, "dw_branch_ret")[0][0]
    sc_ffn2 = _SplitScatter(_Scatter([dw1b, dw2b]), "scatter_ffn2")
    dq, dk, dv = _flash_bwd(q, k, v, do, lse, delta, name="mla_attn_bwd", after=sc_ffn2.start())
    da, dql, dkvl, gqn, gkvn = _mla_prep_bwd(dq, dk, dv, proj, pos, mla_q_norm_w, mla_kv_norm_w, w_uq_p, w_kv_p, tab_mla, name="mla_prep_bwd")
    dw_uq_p = grad(dql, qn, "dw_uq")[0][0]
    dw_kv_p = grad(kvn, dkvl, "dw_ukv")[0][0]
    drq, drk, drv = _ret_bwd(dyn, ypre, proj, pos, tab_ret, rprev, name="retention_bwd")
    dproj = jnp.concatenate([drq, drk, drv, drg, da, dgm, dgr], axis=1)
    dw_in_p = grad(dproj, a1, "dw_in")[0][0]

    dw_uq = dw_uq_p.reshape(MLA_HEADS, HP, MLA_Q_RANK)[:, :uq_w]
    dkp = dw_kv_p[:, :QW].reshape(MLA_KV_RANK, MLA_HEADS, HP)[:, :, :MLA_NOPE]
    dvp = dw_kv_p[:, QW:].reshape(MLA_KV_RANK, MLA_HEADS, HP)[:, :, :MLA_V]
    dw_ukv = jnp.concatenate([dkp, dvp], axis=2).transpose(1, 0, 2)
    dw_bm = dw_bm_p.reshape(MLA_HEADS, HP, D)[:, :MLA_V].reshape(MLA_HEADS * MLA_V, D)
    small_mixer_grads = [dw_uq, dw_ukv, _col_shards(dw_bm), _col_shards(dw_br), dw_out.reshape(N_DEV, D // N_DEV, D)]
    sc_small = _SplitScatter(_Scatter(small_mixer_grads), "scatter_mixer_small")
    dh1, gmixpre = _proj_bwd(dproj, w_in_p, h1, mix_pre_w, dh2, name="mixer_in_bwd", after=sc_small.start())
    unhead = lambda a, h, wd: a.reshape(h, HP, D)[:, :wd].reshape(h * wd, D)
    c0 = 4 * RW
    dw_in = jnp.concatenate([
        dw_in_p[c0:c0 + 384], dw_in_p[c0 + 384:c0 + 640], dw_in_p[c0 + 640 + MLA_NOPE:c0 + 640 + MLA_NOPE + MLA_ROPE],
        unhead(dw_in_p[0:RW], RET_HEADS, RET_DK), unhead(dw_in_p[RW:2 * RW], RET_HEADS, RET_DK),
        dw_in_p[2 * RW:3 * RW], dw_in_p[3 * RW:4 * RW],
        dw_in_p[PROJ_FIXED:PROJ_FIXED + D], dw_in_p[PROJ_FIXED + D:PROJ_FIXED + 2 * D]], axis=0).reshape(N_DEV, -1, D)
    sc_w_in = _SplitScatter(_Scatter([dw_in]), "scatter_w_in")
    g1, du1, df1, a0, dx, gpost1, gpre1 = _ffn_bwd(
        dh1, f1, ffn1_post_w, h0, ffn1_pre_w, u1, w2a, w1a, name="ffn1_bwd", after=sc_w_in.start())
    dw2a = grad(g1, df1, "ffn1_dw2")[0].reshape(N_DEV, hp, D)
    sc_dw2a = _SplitScatter(_Scatter([dw2a]), "scatter_ffn1_dw2")
    dw1a, = grad(du1.reshape(N_DEV, T, 2 * hp), a0, "ffn1_dw1", after=sc_dw2a.start())

    small_g = {"ffn1_pre_w": gpre1, "ffn1_post_w": gpost1, "mix_pre_w": gmixpre, "mla_q_norm_w": gqn,
               "mla_kv_norm_w": gkvn, "ret_gn_w": ggn, "mix_post_w": gpostm, "ffn2_pre_w": gpre2, "ffn2_post_w": gpost2}
    sc_last = _SplitScatter(_Scatter([dw1a], whole=[small_g[nm] for nm, *_ in small] + [lossp]), "scatter_ffn1_dw1")
    token = sc_last.start()
    recv_ffn2 = sc_ffn2.wait([token])
    recv_mixer = sc_w_in.wait([token]) + sc_small.wait([token])
    recv_w2a, = sc_dw2a.wait([token])
    parts = dict(zip(mixer, recv_mixer))
    parts.update(ffn1_w2=recv_w2a, ffn2_w1=recv_ffn2[0], ffn2_w2=recv_ffn2[1])
    as_is = (lambda a: a, lambda p: p[:, None], lambda a: a)
    views = {nm: as_is for nm, *_ in big}
    for nm in ("ffn1_w1", "ffn2_w1"):
        views[nm] = (lambda a: rows_view(a).reshape(2, half, D), lambda p: p.reshape(N_DEV, 2, hp, D),
                     lambda a: a.reshape(2 * half, D).T[None])
    for nm in ("w_in", "mla_w_uq"):
        views[nm] = (lambda a: rows_view(a)[None], lambda p: p[:, None], lambda a: a[0].T[None])

    def update(nm, w, m_, v_, after):
        to_view, parts_view, back = views[nm]
        return [back(a) for a in _adamw(to_view(w), parts_view(parts[nm]), to_view(m_), to_view(v_), after,
                                        name="adamw_" + nm)]

    big_out = {nm: update(nm, w, m_, v_, token) for nm, w, m_, v_ in big if nm != "ffn1_w1"}
    recv_w1a, *small_parts, loss_parts = sc_last.wait([d[0] for d in big_out.values()])
    loss = jnp.sum(loss_parts[:, ::8, 0])
    parts["ffn1_w1"] = recv_w1a
    big_out["ffn1_w1"] = update("ffn1_w1", ffn1_w1, m_ffn1_w1, v_ffn1_w1, jnp.zeros((8, LANES), F32))
    small_out = _adamw_vectors([w for _, w, _, _ in small], small_parts, [a for _, _, a, _ in small],
                               [a for _, _, _, a in small], name="adamw_replicated")

    order = ["ffn1_pre_w", "ffn1_w1", "ffn1_w2", "ffn1_post_w", "mix_pre_w", "w_in", "mla_q_norm_w", "mla_w_uq",
             "mla_kv_norm_w", "mla_w_ukv", "ret_gn_w", "w_branch_mla", "w_branch_ret", "w_out", "mix_post_w",
             "ffn2_pre_w", "ffn2_w1", "ffn2_w2", "ffn2_post_w"]
    outs = [loss, dx[None]]
    for i in range(4):
        both = {nm: big_out[nm][i] for nm in big_out}
        both.update({nm: small_out[4 * j + i] for j, (nm, *_) in enumerate(small)})
        outs += [both[nm] for nm in order]
    return tuple(outs)
```

```python
import math

import numpy as np
import jax
import jax.numpy as jnp
from jax import lax
from jax.experimental import pallas as pl
from jax.experimental.pallas import tpu as pltpu

F32, BF16 = jnp.float32, jnp.bfloat16

MLA_HEADS, MLA_NOPE, MLA_ROPE, MLA_V = 8, 64, 32, 64
MLA_Q_RANK, MLA_KV_RANK = 384, 256
RET_HEADS, RET_DK, RET_DV = 4, 64, 128
ROPE_BASE, NORM_EPS, GN_EPS = 10000.0, 1e-6, 1e-6
ADAM_LR, ADAM_B1, ADAM_B2, ADAM_EPS, ADAM_WD, ADAM_STEP = 0.001, 0.9, 0.999, 1e-08, 0.01, 10
ATTN_SCALE = 1.0 / math.sqrt(MLA_NOPE + MLA_ROPE)

N_DEV = 8
LANES = 128
HP = LANES
QW = MLA_HEADS * HP
RW = RET_HEADS * HP
AW = 1024
PROJ_FIXED = 4 * RW + AW
NEG = -1e30

TOKEN_TILE = 512
ATTN_TILE = 1024
ATTN_CHAINS = 2
FFN_CHAINS = 2
RET_TILE = 256
PROJ_TILE_CAP = 2560
GRAD_TILE_CAP = 1408
GRAD_TOKEN_TILE = 2048
MERGE_TILE = 256
VMEM_LIMIT = 56 * 1024 * 1024


def _tile(n, cap, mult=LANES):
    if n <= cap:
        return n
    best = None
    for t in range(mult, cap + 1, mult):
        if n % t == 0:
            best = t
    assert best is not None, (n, cap, mult)
    return best


def _params(sem):
    return pltpu.CompilerParams(dimension_semantics=sem, vmem_limit_bytes=VMEM_LIMIT)


def _dot(a, b):
    return lax.dot_general(a, b, (((1,), (0,)), ((), ())), preferred_element_type=F32)


def _dot_nt(a, b):
    return lax.dot_general(a, b, (((1,), (1,)), ((), ())), preferred_element_type=F32)


def _dot_tn(a, b):
    return lax.dot_general(a, b, (((0,), (0,)), ((), ())), preferred_element_type=F32)


def _sigmoid(x):
    return pl.reciprocal(1.0 + jnp.exp(-x), approx=True)


def _rms_fwd(x, w):
    r = lax.rsqrt(jnp.mean(x * x, axis=-1, keepdims=True) + NORM_EPS)
    return x * r * w


def _rms_bwd(x, w, dy):
    r = lax.rsqrt(jnp.mean(x * x, axis=-1, keepdims=True) + NORM_EPS)
    xh = x * r
    g = dy * w
    dx = r * (g - xh * jnp.mean(g * xh, axis=-1, keepdims=True))
    return dx, jnp.sum(dy * xh, axis=0, keepdims=True)


def _rope_table(first, half):
    inv = (np.float32(ROPE_BASE) ** (-(np.arange(half, dtype=np.float32) / np.float32(half)))).astype(np.float32)
    tab = np.zeros((8, LANES), np.float32)
    tab[0, first:first + half] = inv
    tab[0, first + half:first + 2 * half] = inv
    tab[1, first:first + half] = -1.0
    tab[2, first + half:first + 2 * half] = 1.0
    return jnp.asarray(tab)


def _rope_cs(pos, tab_ref):
    ang = pos * tab_ref[0:1, :]
    s = jnp.sin(ang)
    return jnp.cos(ang), s * tab_ref[1:2, :], s * tab_ref[2:3, :]


def _rope(x, cs, half, inverse=False):
    c, s1, s2 = cs
    a = pltpu.roll(x, LANES - half, 1) * s1 + pltpu.roll(x, half, 1) * s2
    return x * c - a if inverse else x * c + a


def _call(body, *, name, grid, in_specs, out_specs, out_shape, scratch_shapes, args, exchange=None, after=None):
    sem = ("arbitrary",) * len(grid)
    anyspec = pl.BlockSpec(memory_space=pl.ANY)
    if exchange is None and after is not None:
        n_own = len(in_specs)

        def behind(*refs):
            body(*refs[:n_own], *refs[n_own + 1:])

        return pl.pallas_call(behind, name=name, grid=grid, in_specs=list(in_specs) + [anyspec], out_specs=out_specs,
                              out_shape=out_shape, scratch_shapes=scratch_shapes, compiler_params=_params(sem))(*args, after)
    if exchange is None:
        return pl.pallas_call(body, name=name, grid=grid, in_specs=in_specs, out_specs=out_specs,
                              out_shape=out_shape, scratch_shapes=scratch_shapes, compiler_params=_params(sem))(*args)
    n_in, n_out, e = len(in_specs), len(out_specs), exchange.n
    total = math.prod(grid)

    def carried(*refs):
        own = refs[:n_in] + refs[n_in + e:n_in + e + n_out] + refs[n_in + 2 * e + n_out:len(refs) - 3]
        ex_refs = (refs[n_in:n_in + e], refs[n_in + e + n_out:n_in + 2 * e + n_out], refs[len(refs) - 3:])
        step = pl.program_id(0)
        for d in range(1, len(grid)):
            step = step * grid[d] + pl.program_id(d)

        @pl.when(step == 0)
        def _():
            exchange.phase(0, *ex_refs)

        @pl.when(step == (3 * total) // 4)
        def _():
            exchange.phase(1, *ex_refs)

        body(*own)

        @pl.when(step == total - 1)
        def _():
            exchange.phase(2, *ex_refs)

    return pl.pallas_call(
        carried, name=name, grid=grid, in_specs=list(in_specs) + [anyspec] * e,
        out_specs=list(out_specs) + [anyspec] * e, out_shape=list(out_shape) + exchange.out_shape,
        scratch_shapes=list(scratch_shapes) + exchange.scratch, compiler_params=_params(sem),
    )(*args, *exchange.operands)


def _ffn_fwd(h, pre_w, w1, w2, post_w, target, *, name, exchange=None):
    T, D = h.shape
    nk, ck = w2.shape[0], w2.shape[1]
    tT = min(TOKEN_TILE, T)
    nT = T // tT
    with_loss = target is not None

    def body(*refs):
        if with_loss:
            (h_ref, pre_ref, w1g_ref, w1u_ref, w2_ref, post_ref, tgt_ref,
             u_ref, f_ref, ho_ref, dy_ref, loss_ref, a_s, acc) = refs
        else:
            (h_ref, pre_ref, w1g_ref, w1u_ref, w2_ref, post_ref,
             u_ref, f_ref, ho_ref, a_s, acc) = refs
        k = pl.program_id(1)

        @pl.when(k == 0)
        def _():
            a_s[...] = _rms_fwd(h_ref[...], pre_ref[...]).astype(BF16)
            acc[...] = jnp.zeros_like(acc)

        for c in range(FFN_CHAINS):
            rs = slice(c * (tT // FFN_CHAINS), (c + 1) * (tT // FFN_CHAINS))
            a = a_s[rs, :]
            ug = _dot_nt(a, w1g_ref[...])
            uu = _dot_nt(a, w1u_ref[...])
            u_ref[0, rs, :] = ug.astype(BF16)
            u_ref[1, rs, :] = uu.astype(BF16)
            acc[rs, :] += _dot((ug * _sigmoid(ug) * uu).astype(BF16), w2_ref[...])

        @pl.when(k == nk - 1)
        def _():
            f = acc[...]
            f_ref[...] = f
            ho = h_ref[...] + 0.5 * _rms_fwd(f, post_ref[...])
            ho_ref[...] = ho
            if with_loss:
                e = ho - tgt_ref[...]
                dy_ref[...] = e * (1.0 / D)
                loss_ref[...] = jnp.full(loss_ref.shape, (0.5 / D) * jnp.sum(e * e), F32)

    row = pl.BlockSpec((tT, D), lambda i, k: (i, 0))
    vec = pl.BlockSpec((1, D), lambda i, k: (0, 0))
    in_specs = [row, vec,
                pl.BlockSpec((None, ck, D), lambda i, k: (k, 0, 0)),
                pl.BlockSpec((None, ck, D), lambda i, k: (nk + k, 0, 0)),
                pl.BlockSpec((None, ck, D), lambda i, k: (k, 0, 0)),
                vec]
    out_shape = [jax.ShapeDtypeStruct((2, nk, T, ck), BF16),
                 jax.ShapeDtypeStruct((T, D), F32),
                 jax.ShapeDtypeStruct((T, D), F32)]
    out_specs = [pl.BlockSpec((2, None, tT, ck), lambda i, k: (0, k, i, 0)), row, row]
    args = [h, pre_w, w1, w1, w2, post_w]
    if with_loss:
        in_specs.append(row)
        args.append(target)
        out_shape += [jax.ShapeDtypeStruct((T, D), F32), jax.ShapeDtypeStruct((nT * 8, LANES), F32)]
        out_specs += [row, pl.BlockSpec((8, LANES), lambda i, k: (i, 0))]
    return _call(body, name=name, grid=(nT, nk), in_specs=in_specs, out_specs=out_specs, out_shape=out_shape,
                 scratch_shapes=[pltpu.VMEM((tT, D), BF16), pltpu.VMEM((tT, D), F32)], args=args, exchange=exchange)


def _ffn_bwd(dho, f, post_w, h, pre_w, u, w2, w1, *, name, exchange=None, after=None):
    T, D = h.shape
    nk, ck = w2.shape[0], w2.shape[1]
    tT = min(TOKEN_TILE, T)
    nT = T // tT

    def body(dho_ref, f_ref, post_ref, h_ref, pre_ref, u_ref, w2_ref, w1g_ref, w1u_ref,
             g_ref, du_ref, df_ref, a_ref, dh_ref, gpost_ref, gpre_ref, df_s, da_acc):
        i, k = pl.program_id(0), pl.program_id(1)

        @pl.when(jnp.logical_and(i == 0, k == 0))
        def _():
            gpost_ref[...] = jnp.zeros_like(gpost_ref)
            gpre_ref[...] = jnp.zeros_like(gpre_ref)

        @pl.when(k == 0)
        def _():
            dx, dw = _rms_bwd(f_ref[...], post_ref[...], 0.5 * dho_ref[...])
            dfb = dx.astype(BF16)
            df_s[...] = dfb
            df_ref[...] = dfb
            gpost_ref[...] += dw
            a_ref[...] = _rms_fwd(h_ref[...], pre_ref[...]).astype(BF16)
            da_acc[...] = jnp.zeros_like(da_acc)

        groups = [slice(c * (tT // FFN_CHAINS), (c + 1) * (tT // FFN_CHAINS)) for c in range(FFN_CHAINS)]
        dgs = [_dot_nt(df_s[rs, :], w2_ref[...]) for rs in groups]
        for rs, dg in zip(groups, dgs):
            ug = u_ref[0, rs, :].astype(F32)
            uu = u_ref[1, rs, :].astype(F32)
            sg = _sigmoid(ug)
            sl = ug * sg
            g_ref[rs, :] = (sl * uu).astype(BF16)
            dug = (dg * uu * (sg + sl * (1.0 - sg))).astype(BF16)
            duu = (dg * sl).astype(BF16)
            du_ref[0, rs, :] = dug
            du_ref[1, rs, :] = duu
            da_acc[rs, :] += _dot(dug, w1g_ref[...]) + _dot(duu, w1u_ref[...])

        @pl.when(k == nk - 1)
        def _():
            dx, dw = _rms_bwd(h_ref[...], pre_ref[...], da_acc[...])
            dh_ref[...] = dho_ref[...] + dx
            gpre_ref[...] += dw

    row = pl.BlockSpec((tT, D), lambda i, k: (i, 0))
    vec = pl.BlockSpec((1, D), lambda i, k: (0, 0))
    return _call(
        body, name=name, grid=(nT, nk),
        in_specs=[row, row, vec, row, vec,
                  pl.BlockSpec((2, None, tT, ck), lambda i, k: (0, k, i, 0)),
                  pl.BlockSpec((None, ck, D), lambda i, k: (k, 0, 0)),
                  pl.BlockSpec((None, ck, D), lambda i, k: (k, 0, 0)),
                  pl.BlockSpec((None, ck, D), lambda i, k: (nk + k, 0, 0))],
        out_specs=[pl.BlockSpec((None, tT, ck), lambda i, k: (k, i, 0)),
                   pl.BlockSpec((2, None, tT, ck), lambda i, k: (0, k, i, 0)),
                   row, row, row, vec, vec],
        out_shape=[jax.ShapeDtypeStruct((nk, T, ck), BF16),
                   jax.ShapeDtypeStruct((2, nk, T, ck), BF16),
                   jax.ShapeDtypeStruct((T, D), BF16),
                   jax.ShapeDtypeStruct((T, D), BF16),
                   jax.ShapeDtypeStruct((T, D), F32),
                   jax.ShapeDtypeStruct((1, D), F32),
                   jax.ShapeDtypeStruct((1, D), F32)],
        scratch_shapes=[pltpu.VMEM((tT, D), BF16), pltpu.VMEM((tT, D), F32)],
        args=(dho, f, post_w, h, pre_w, u, w2, w1, w1), exchange=exchange, after=after)


def _matmul_tn(x, dy, *, name, exchange=None, after=None):
    Px, T, K = x.shape
    Py, _, N = dy.shape
    P = max(Px, Py)
    tT, tK, tN = min(GRAD_TOKEN_TILE, T), _tile(K, GRAD_TILE_CAP), _tile(N, GRAD_TILE_CAP)
    nt = T // tT

    def body(x_ref, dy_ref, o_ref, acc):
        t = pl.program_id(3)

        @pl.when(t == 0)
        def _():
            acc[...] = jnp.zeros_like(acc)

        acc[...] += _dot_tn(x_ref[...], dy_ref[...])

        @pl.when(t == nt - 1)
        def _():
            o_ref[...] = acc[...].astype(BF16)

    return _call(
        body, name=name, grid=(P, K // tK, N // tN, nt),
        in_specs=[pl.BlockSpec((None, tT, tK), lambda p, a, b, t: (p if Px > 1 else 0, t, a)),
                  pl.BlockSpec((None, tT, tN), lambda p, a, b, t: (p if Py > 1 else 0, t, b))],
        out_specs=[pl.BlockSpec((None, tK, tN), lambda p, a, b, t: (p, a, b))],
        out_shape=[jax.ShapeDtypeStruct((P, K, N), BF16)],
        scratch_shapes=[pltpu.VMEM((tK, tN), F32)], args=(x, dy), exchange=exchange, after=after)


def _rms_matmul(h, wn, w, *, name):
    T, D = h.shape
    N = w.shape[0]
    tT, tN = min(TOKEN_TILE, T), _tile(N, PROJ_TILE_CAP)

    def body(h_ref, wn_ref, w_ref, y_ref, a_ref):
        @pl.when(pl.program_id(1) == 0)
        def _():
            a_ref[...] = _rms_fwd(h_ref[...], wn_ref[...]).astype(BF16)

        y_ref[...] = _dot_nt(a_ref[...], w_ref[...]).astype(BF16)

    return pl.pallas_call(
        body, name=name, grid=(T // tT, N // tN),
        in_specs=[pl.BlockSpec((tT, D), lambda i, j: (i, 0)),
                  pl.BlockSpec((1, D), lambda i, j: (0, 0)),
                  pl.BlockSpec((tN, D), lambda i, j: (j, 0))],
        out_specs=[pl.BlockSpec((tT, tN), lambda i, j: (i, j)),
                   pl.BlockSpec((tT, D), lambda i, j: (i, 0))],
        out_shape=[jax.ShapeDtypeStruct((T, N), BF16), jax.ShapeDtypeStruct((T, D), BF16)],
        compiler_params=_params(("parallel", "arbitrary")),
    )(h, wn, w)


def _proj_bwd(dproj, w, h, wn, dres, *, name, exchange=None, after=None):
    T, D = h.shape
    N = w.shape[0]
    tT, tN = min(TOKEN_TILE, T), _tile(N, PROJ_TILE_CAP)
    nn = N // tN

    def body(dp_ref, w_ref, h_ref, wn_ref, dres_ref, dh_ref, gw_ref, acc):
        i, j = pl.program_id(0), pl.program_id(1)

        @pl.when(jnp.logical_and(i == 0, j == 0))
        def _():
            gw_ref[...] = jnp.zeros_like(gw_ref)

        @pl.when(j == 0)
        def _():
            acc[...] = jnp.zeros_like(acc)

        acc[...] += _dot(dp_ref[...], w_ref[...])

        @pl.when(j == nn - 1)
        def _():
            dx, dw = _rms_bwd(h_ref[...], wn_ref[...], acc[...])
            dh_ref[...] = dres_ref[...] + dx
            gw_ref[...] += dw

    row = pl.BlockSpec((tT, D), lambda i, j: (i, 0))
    vec = pl.BlockSpec((1, D), lambda i, j: (0, 0))
    return _call(
        body, name=name, grid=(T // tT, nn),
        in_specs=[pl.BlockSpec((tT, tN), lambda i, j: (i, j)),
                  pl.BlockSpec((tN, D), lambda i, j: (j, 0)), row, vec, row],
        out_specs=[row, vec],
        out_shape=[jax.ShapeDtypeStruct((T, D), F32), jax.ShapeDtypeStruct((1, D), F32)],
        scratch_shapes=[pltpu.VMEM((tT, D), F32)], args=(dproj, w, h, wn, dres), exchange=exchange, after=after)


def _mla_prep_fwd(proj, pos, qn_w, kvn_w, w_uq, w_kv, tab, *, name):
    T = proj.shape[0]
    tT = min(TOKEN_TILE, T)
    a_blk = PROJ_FIXED // AW - 1

    def body(a_ref, pos_ref, qnw_ref, kvnw_ref, wuq_ref, wkv_ref, tab_ref,
             q_ref, k_ref, v_ref, qn_ref, kvn_ref):
        cq = a_ref[:, 0:MLA_Q_RANK].astype(F32)
        ckv = a_ref[:, MLA_Q_RANK:MLA_Q_RANK + MLA_KV_RANK].astype(F32)
        kr = a_ref[:, 640:768].astype(F32)
        qn = _rms_fwd(cq, qnw_ref[...]).astype(BF16)
        kvn = _rms_fwd(ckv, kvnw_ref[...]).astype(BF16)
        qn_ref[...] = qn
        kvn_ref[...] = kvn
        cs = _rope_cs(pos_ref[...], tab_ref)
        q = _dot_nt(qn, wuq_ref[...])
        kv = _dot(kvn, wkv_ref[...])
        krr = _rope(kr, cs, MLA_ROPE // 2)
        for hd in range(MLA_HEADS):
            sl = slice(hd * HP, (hd + 1) * HP)
            q_ref[:, sl] = (_rope(q[:, sl], cs, MLA_ROPE // 2) * ATTN_SCALE).astype(BF16)
            k_ref[:, sl] = (kv[:, sl] + krr).astype(BF16)
        v_ref[...] = kv[:, QW:].astype(BF16)

    def full(r, c):
        return pl.BlockSpec((r, c), lambda i: (0, 0))

    def rows(c):
        return pl.BlockSpec((tT, c), lambda i: (i, 0))

    return pl.pallas_call(
        body, name=name, grid=(T // tT,),
        in_specs=[pl.BlockSpec((tT, AW), lambda i: (i, a_blk)), rows(1),
                  full(1, MLA_Q_RANK), full(1, MLA_KV_RANK),
                  full(QW, MLA_Q_RANK), full(MLA_KV_RANK, 2 * QW), full(8, LANES)],
        out_specs=[rows(QW), rows(QW), rows(QW), rows(MLA_Q_RANK), rows(MLA_KV_RANK)],
        out_shape=[jax.ShapeDtypeStruct((T, QW), BF16)] * 3
        + [jax.ShapeDtypeStruct((T, MLA_Q_RANK), BF16), jax.ShapeDtypeStruct((T, MLA_KV_RANK), BF16)],
        compiler_params=_params(("parallel",)),
    )(proj, pos, qn_w, kvn_w, w_uq, w_kv, tab)


def _mla_prep_bwd(dq, dk, dv, proj, pos, qn_w, kvn_w, w_uq, w_kv, tab, *, name):
    T = proj.shape[0]
    tT = min(TOKEN_TILE, T)
    a_blk = PROJ_FIXED // AW - 1

    def body(dq_ref, dk_ref, dv_ref, a_ref, pos_ref, qnw_ref, kvnw_ref, wuq_ref, wkv_ref, tab_ref,
             da_ref, dql_ref, dkvl_ref, gqn_ref, gkvn_ref):
        @pl.when(pl.program_id(0) == 0)
        def _():
            gqn_ref[...] = jnp.zeros_like(gqn_ref)
            gkvn_ref[...] = jnp.zeros_like(gkvn_ref)

        cs = _rope_cs(pos_ref[...], tab_ref)
        dkr = jnp.zeros((tT, HP), F32)
        for hd in range(MLA_HEADS):
            sl = slice(hd * HP, (hd + 1) * HP)
            dql_ref[:, sl] = (_rope(dq_ref[:, sl], cs, MLA_ROPE // 2, inverse=True) * ATTN_SCALE).astype(BF16)
            dkh = dk_ref[:, sl]
            dkr = dkr + dkh
            dkvl_ref[:, sl] = dkh.astype(BF16)
        dkvl_ref[:, QW:] = dv_ref[...]
        dqn = _dot(dql_ref[...], wuq_ref[...])
        dkvn = _dot_nt(dkvl_ref[...], wkv_ref[...])
        cq = a_ref[:, 0:MLA_Q_RANK].astype(F32)
        ckv = a_ref[:, MLA_Q_RANK:MLA_Q_RANK + MLA_KV_RANK].astype(F32)
        dcq, gq = _rms_bwd(cq, qnw_ref[...], dqn)
        dckv, gkv = _rms_bwd(ckv, kvnw_ref[...], dkvn)
        gqn_ref[...] += gq
        gkvn_ref[...] += gkv
        da_ref[:, 0:MLA_Q_RANK] = dcq.astype(BF16)
        da_ref[:, MLA_Q_RANK:MLA_Q_RANK + MLA_KV_RANK] = dckv.astype(BF16)
        da_ref[:, 640:768] = _rope(dkr, cs, MLA_ROPE // 2, inverse=True).astype(BF16)
        da_ref[:, 768:AW] = jnp.zeros((tT, AW - 768), BF16)

    def full(r, c):
        return pl.BlockSpec((r, c), lambda i: (0, 0))

    def rows(c):
        return pl.BlockSpec((tT, c), lambda i: (i, 0))

    return pl.pallas_call(
        body, name=name, grid=(T // tT,),
        in_specs=[rows(QW), rows(QW), rows(QW), pl.BlockSpec((tT, AW), lambda i: (i, a_blk)), rows(1),
                  full(1, MLA_Q_RANK), full(1, MLA_KV_RANK),
                  full(QW, MLA_Q_RANK), full(MLA_KV_RANK, 2 * QW), full(8, LANES)],
        out_specs=[rows(AW), rows(QW), rows(2 * QW), full(1, MLA_Q_RANK), full(1, MLA_KV_RANK)],
        out_shape=[jax.ShapeDtypeStruct((T, AW), BF16), jax.ShapeDtypeStruct((T, QW), BF16),
                   jax.ShapeDtypeStruct((T, 2 * QW), BF16),
                   jax.ShapeDtypeStruct((1, MLA_Q_RANK), F32), jax.ShapeDtypeStruct((1, MLA_KV_RANK), F32)],
        compiler_params=_params(("arbitrary",)),
    )(dq, dk, dv, proj, pos, qn_w, kvn_w, w_uq, w_kv, tab)


def _flash_fwd(q, k, v, *, name, exchange=None):
    T = q.shape[0]
    H = q.shape[1] // HP
    tq = min(ATTN_TILE, T)
    nq = T // tq

    sub = tq // ATTN_CHAINS

    def body(q_ref, k_ref, v_ref, o_ref, lse_ref):
        qi = pl.program_id(1)
        qs = [q_ref[c * sub:(c + 1) * sub, :] for c in range(ATTN_CHAINS)]

        def update(carry, off, masked):
            nks = [(c + 1) * sub if masked else tq for c in range(ATTN_CHAINS)]
            scores = [_dot_nt(qs[c], k_ref[pl.ds(off, nks[c]), :]) for c in range(ATTN_CHAINS)]
            out = []
            for c in range(ATTN_CHAINS):
                m_prev, l_prev, acc = carry[c]
                nk, s = nks[c], scores[c]
                vb = v_ref[pl.ds(off, nk), :]
                if masked:
                    rows = lax.broadcasted_iota(jnp.int32, (sub, nk), 0) + c * sub
                    s = jnp.where(rows >= lax.broadcasted_iota(jnp.int32, (sub, nk), 1), s, NEG)
                m_new = jnp.maximum(m_prev, jnp.max(s, axis=1, keepdims=True))
                alpha = jnp.exp(m_prev - m_new)
                p = jnp.exp(s - m_new)
                out.append((m_new, alpha * l_prev + jnp.sum(p, axis=1, keepdims=True),
                            alpha * acc + _dot(p.astype(BF16), vb)))
            return tuple(out)

        init = tuple((jnp.full((sub, 1), NEG, F32), jnp.zeros((sub, 1), F32), jnp.zeros((sub, HP), F32))
                     for _ in range(ATTN_CHAINS))
        carry = lax.fori_loop(0, qi, lambda j, cr: update(cr, pl.multiple_of(j * tq, tq), False), init)
        carry = update(carry, pl.multiple_of(qi * tq, tq), True)
        for c in range(ATTN_CHAINS):
            m_fin, l_fin, acc = carry[c]
            o_ref[c * sub:(c + 1) * sub, :] = (acc / l_fin).astype(BF16)
            lse_ref[c * sub:(c + 1) * sub, :] = jnp.broadcast_to(m_fin + jnp.log(l_fin), (sub, HP))

    qspec = pl.BlockSpec((tq, HP), lambda h, i: (i, h))
    kspec = pl.BlockSpec((T, HP), lambda h, i: (0, h))
    return _call(
        body, name=name, grid=(H, nq),
        in_specs=[qspec, kspec, kspec], out_specs=[qspec, qspec],
        out_shape=[jax.ShapeDtypeStruct((T, H * HP), BF16), jax.ShapeDtypeStruct((T, H * HP), F32)],
        scratch_shapes=[], args=(q, k, v), exchange=exchange)


def _flash_bwd(q, k, v, do, lse, delta, *, name, exchange=None, after=None):
    T = q.shape[0]
    H = q.shape[1] // HP
    tq = min(ATTN_TILE, T)
    nq = T // tq
    sub = tq // ATTN_CHAINS

    def body(k_ref, v_ref, q_ref, do_ref, lse_ref, dl_ref, dq_ref, dk_ref, dv_ref):
        ki = pl.program_id(1)

        @pl.when(ki == 0)
        def _():
            dq_ref[...] = jnp.zeros_like(dq_ref)

        def grow(a):
            return a if a.shape[0] == tq else jnp.concatenate([a, jnp.zeros((tq - a.shape[0], HP), F32)], axis=0)

        def step(carry, j, masked):
            dk_acc, dv_acc = carry
            nks = [(c + 1) * sub if masked else tq for c in range(ATTN_CHAINS)]
            rws = [pl.ds(pl.multiple_of(j * tq + c * sub, sub), sub) for c in range(ATTN_CHAINS)]
            scores = [_dot_nt(q_ref[rws[c], :], k_ref[0:nks[c], :]) for c in range(ATTN_CHAINS)]
            dps = [_dot_nt(do_ref[rws[c], :], v_ref[0:nks[c], :]) for c in range(ATTN_CHAINS)]
            for c in range(ATTN_CHAINS):
                rows, nk, s, dp = rws[c], nks[c], scores[c], dps[c]
                kb = k_ref[0:nk, :]
                qb = q_ref[rows, :]
                dob = do_ref[rows, :]
                if masked:
                    ri = lax.broadcasted_iota(jnp.int32, (sub, nk), 0) + c * sub
                    s = jnp.where(ri >= lax.broadcasted_iota(jnp.int32, (sub, nk), 1), s, NEG)
                p = jnp.exp(s - lse_ref[rows, 0:1])
                dv_acc = dv_acc + grow(_dot_tn(p.astype(BF16), dob))
                ds = (p * (dp - dl_ref[rows, 0:1])).astype(BF16)
                dk_acc = dk_acc + grow(_dot_tn(ds, qb))
                dq_ref[rows, :] += _dot(ds, kb)
            return dk_acc, dv_acc

        carry = step((jnp.zeros((tq, HP), F32), jnp.zeros((tq, HP), F32)), ki, True)
        dk_acc, dv_acc = lax.fori_loop(ki + 1, nq, lambda j, cr: step(cr, j, False), carry)
        dk_ref[...] = dk_acc
        dv_ref[...] = dv_acc.astype(BF16)

    kspec = pl.BlockSpec((tq, HP), lambda h, j: (j, h))
    full = pl.BlockSpec((T, HP), lambda h, j: (0, h))
    return _call(
        body, name=name, grid=(H, nq),
        in_specs=[kspec, kspec, full, full, full, full], out_specs=[full, kspec, kspec],
        out_shape=[jax.ShapeDtypeStruct((T, H * HP), F32), jax.ShapeDtypeStruct((T, H * HP), F32),
                   jax.ShapeDtypeStruct((T, H * HP), BF16)],
        scratch_shapes=[], args=(k, v, q, do, lse, delta), exchange=exchange, after=after)


def _ret_consts(cc, hd):
    lg = math.log(1.0 - 2.0 ** (-5.0 - hd))
    diff = (lax.broadcasted_iota(jnp.int32, (cc, cc), 0) - lax.broadcasted_iota(jnp.int32, (cc, cc), 1)).astype(F32)
    decay = jnp.where(diff >= 0, jnp.exp(jnp.maximum(diff, 0.0) * lg), 0.0)
    idx = lax.broadcasted_iota(jnp.int32, (cc, 1), 0).astype(F32)
    zeta = jnp.exp((cc - 1.0 - idx) * lg)
    xi = jnp.exp((idx + 1.0) * lg)
    return decay, zeta, xi, math.exp(cc * lg)


def _ret_fwd(proj, pos, tab, *, name):
    T = proj.shape[0]
    cc = min(RET_TILE, T)
    n = T // cc

    def body(rq_ref, rk_ref, rv_ref, pos_ref, tab_ref, y_ref, yn_ref, rprev_ref, r_s):
        @pl.when(pl.program_id(0) == 0)
        def _():
            r_s[...] = jnp.zeros_like(r_s)

        cs = _rope_cs(pos_ref[...], tab_ref)
        for hd in range(RET_HEADS):
            sl = slice(hd * HP, (hd + 1) * HP)
            decay, zeta, xi, gc = _ret_consts(cc, hd)
            q = _rope(rq_ref[:, sl].astype(F32), cs, RET_DK // 2).astype(BF16)
            kf = _rope(rk_ref[:, sl].astype(F32), cs, RET_DK // 2) * (RET_DK ** -0.5)
            k = kf.astype(BF16)
            v = rv_ref[:, sl]
            r = r_s[hd]
            rprev_ref[0, hd] = r
            inner = (_dot_nt(q, k) * decay).astype(BF16)
            y = _dot(inner, v) + _dot(q, r.astype(BF16)) * xi
            r_s[hd] = r * gc + _dot_tn((kf * zeta).astype(BF16), v)
            y_ref[:, sl] = y
            mu = jnp.mean(y, axis=-1, keepdims=True)
            yc = y - mu
            var = jnp.mean(yc * yc, axis=-1, keepdims=True)
            yn_ref[:, sl] = (yc * lax.rsqrt(var + GN_EPS)).astype(BF16)

    def blk(j):
        return pl.BlockSpec((cc, RW), lambda i: (i, j))

    return pl.pallas_call(
        body, name=name, grid=(n,),
        in_specs=[blk(0), blk(1), blk(2), pl.BlockSpec((cc, 1), lambda i: (i, 0)),
                  pl.BlockSpec((8, LANES), lambda i: (0, 0))],
        out_specs=[blk(0), blk(0), pl.BlockSpec((1, RET_HEADS, HP, RET_DV), lambda i: (i, 0, 0, 0))],
        out_shape=[jax.ShapeDtypeStruct((T, RW), F32), jax.ShapeDtypeStruct((T, RW), BF16),
                   jax.ShapeDtypeStruct((n, RET_HEADS, HP, RET_DV), F32)],
        scratch_shapes=[pltpu.VMEM((RET_HEADS, HP, RET_DV), F32)],
        compiler_params=_params(("arbitrary",)),
    )(proj, proj, proj, pos, tab)


def _ret_bwd(dyn, y, proj, pos, tab, rprev, *, name):
    T = proj.shape[0]
    cc = min(RET_TILE, T)
    n = T // cc

    def body(dyn_ref, y_ref, rq_ref, rk_ref, rv_ref, pos_ref, tab_ref, rprev_ref,
             drq_ref, drk_ref, drv_ref, dr_s):
        @pl.when(pl.program_id(0) == 0)
        def _():
            dr_s[...] = jnp.zeros_like(dr_s)

        cs = _rope_cs(pos_ref[...], tab_ref)
        for hd in range(RET_HEADS):
            sl = slice(hd * HP, (hd + 1) * HP)
            decay, zeta, xi, gc = _ret_consts(cc, hd)
            q = _rope(rq_ref[:, sl].astype(F32), cs, RET_DK // 2).astype(BF16)
            kf = _rope(rk_ref[:, sl].astype(F32), cs, RET_DK // 2) * (RET_DK ** -0.5)
            k = kf.astype(BF16)
            v = rv_ref[:, sl]
            yv = y_ref[:, sl]
            mu = jnp.mean(yv, axis=-1, keepdims=True)
            yc = yv - mu
            rs = lax.rsqrt(jnp.mean(yc * yc, axis=-1, keepdims=True) + GN_EPS)
            yn = yc * rs
            dn = dyn_ref[:, sl]
            dy = rs * (dn - jnp.mean(dn, axis=-1, keepdims=True) - yn * jnp.mean(dn * yn, axis=-1, keepdims=True))
            dyb = dy.astype(BF16)
            dyx = (dy * xi).astype(BF16)
            dr = dr_s[hd]
            drb = dr.astype(BF16)
            inner = (_dot_nt(q, k) * decay).astype(BF16)
            da = (_dot_nt(dyb, v) * decay).astype(BF16)
            dv = _dot_tn(inner, dyb) + _dot((kf * zeta).astype(BF16), drb)
            dq = _dot(da, k) + _dot_nt(dyx, rprev_ref[0, hd].astype(BF16))
            dk = _dot_tn(da, q) + _dot_nt(v, drb) * zeta
            dr_s[hd] = dr * gc + _dot_tn(q, dyx)
            drq_ref[:, sl] = _rope(dq, cs, RET_DK // 2, inverse=True).astype(BF16)
            drk_ref[:, sl] = _rope(dk * (RET_DK ** -0.5), cs, RET_DK // 2, inverse=True).astype(BF16)
            drv_ref[:, sl] = dv.astype(BF16)

    def blk(j):
        return pl.BlockSpec((cc, RW), lambda i: (n - 1 - i, j))

    return pl.pallas_call(
        body, name=name, grid=(n,),
        in_specs=[blk(0), blk(0), blk(0), blk(1), blk(2), pl.BlockSpec((cc, 1), lambda i: (n - 1 - i, 0)),
                  pl.BlockSpec((8, LANES), lambda i: (0, 0)),
                  pl.BlockSpec((1, RET_HEADS, HP, RET_DV), lambda i: (n - 1 - i, 0, 0, 0))],
        out_specs=[blk(0), blk(0), blk(0)],
        out_shape=[jax.ShapeDtypeStruct((T, RW), BF16)] * 3,
        scratch_shapes=[pltpu.VMEM((RET_HEADS, HP, RET_DV), F32)],
        compiler_params=_params(("arbitrary",)),
    )(dyn, y, proj, proj, proj, pos, tab, rprev)


def _merge_fwd(o, yn, proj, gn_w, w_bm, w_br, w_out, h, post_w, *, name):
    T, D = h.shape
    tT = min(TOKEN_TILE, T)
    g_blk = PROJ_FIXED // D

    def body(o_ref, yn_ref, rg_ref, gm_ref, gr_ref, gnw_ref, wbm_ref, wbr_ref, wout_ref, h_ref, post_ref,
             omla_ref, oret_ref, m_ref, ho_ref):
        groups = [slice(c * (tT // FFN_CHAINS), (c + 1) * (tT // FFN_CHAINS)) for c in range(FFN_CHAINS)]
        o_mlas = [_dot(o_ref[rs, :], wbm_ref[...]) for rs in groups]
        for rs, o_mla in zip(groups, o_mlas):
            rg = rg_ref[rs, :].astype(F32)
            gated = (rg * _sigmoid(rg) * (yn_ref[rs, :].astype(F32) * gnw_ref[...])).astype(BF16)
            o_ret = _dot(gated, wbr_ref[...])
            omla_ref[rs, :] = o_mla.astype(BF16)
            oret_ref[rs, :] = o_ret.astype(BF16)
            merged = _sigmoid(gm_ref[rs, :].astype(F32)) * o_mla + _sigmoid(gr_ref[rs, :].astype(F32)) * o_ret
            m = _dot(merged.astype(BF16), wout_ref[...])
            m_ref[rs, :] = m
            ho_ref[rs, :] = h_ref[rs, :] + _rms_fwd(m, post_ref[...])

    def full(r, c):
        return pl.BlockSpec((r, c), lambda i: (0, 0))

    def rows(c, j=0):
        return pl.BlockSpec((tT, c), lambda i: (i, j))

    return pl.pallas_call(
        body, name=name, grid=(T // tT,),
        in_specs=[rows(QW), rows(RW), rows(RW, 3), rows(D, g_blk), rows(D, g_blk + 1), full(1, RW),
                  full(QW, D), full(RW, D), full(D, D), rows(D), full(1, D)],
        out_specs=[rows(D), rows(D), rows(D), rows(D)],
        out_shape=[jax.ShapeDtypeStruct((T, D), BF16), jax.ShapeDtypeStruct((T, D), BF16),
                   jax.ShapeDtypeStruct((T, D), F32), jax.ShapeDtypeStruct((T, D), F32)],
        compiler_params=_params(("parallel",)),
    )(o, yn, proj, proj, proj, gn_w, w_bm, w_br, w_out, h, post_w)


def _merge_bwd(dho, m, post_w, omla, oret, proj, yn, gn_w, o, w_out, w_bm, w_br, *, name):
    T, D = dho.shape
    tT = min(MERGE_TILE, T)
    g_blk = PROJ_FIXED // D

    nT = T // tT

    def body(dho_ref, m_ref, post_ref, omla_ref, oret_ref, rg_ref, gm_ref, gr_ref, yn_ref, gnw_ref, o_ref,
             wout_ref, wbm_ref, wbr_ref,
             dgm_ref, dgr_ref, do_ref, delta_ref, drg_ref, dyn_ref, gpost_ref, ggn_ref,
             dwout_ref, dwbm_ref, dwbr_ref, acc_out, acc_bm, acc_br):
        @pl.when(pl.program_id(0) == 0)
        def _():
            gpost_ref[...] = jnp.zeros_like(gpost_ref)
            ggn_ref[...] = jnp.zeros_like(ggn_ref)
            acc_out[...] = jnp.zeros_like(acc_out)
            acc_bm[...] = jnp.zeros_like(acc_bm)
            acc_br[...] = jnp.zeros_like(acc_br)

        dm, gp = _rms_bwd(m_ref[...], post_ref[...], dho_ref[...])
        gpost_ref[...] += gp
        dmb = dm.astype(BF16)
        dmerged = _dot_nt(dmb, wout_ref[...])
        o_mla = omla_ref[...].astype(F32)
        o_ret = oret_ref[...].astype(F32)
        sgm = _sigmoid(gm_ref[...].astype(F32))
        sgr = _sigmoid(gr_ref[...].astype(F32))
        acc_out[...] += _dot_tn((sgm * o_mla + sgr * o_ret).astype(BF16), dmb)
        dgm_ref[...] = (dmerged * o_mla * sgm * (1.0 - sgm)).astype(BF16)
        dgr_ref[...] = (dmerged * o_ret * sgr * (1.0 - sgr)).astype(BF16)
        domla = (dmerged * sgm).astype(BF16)
        acc_bm[...] += _dot_tn(o_ref[...], domla)
        do = _dot_nt(domla, wbm_ref[...])
        do_ref[...] = do.astype(BF16)
        for hd in range(MLA_HEADS):
            sl = slice(hd * HP, (hd + 1) * HP)
            d = jnp.sum(do[:, sl] * o_ref[:, sl].astype(F32), axis=-1, keepdims=True)
            delta_ref[:, sl] = jnp.broadcast_to(d, (tT, HP))
        doret = (dmerged * sgr).astype(BF16)
        dgated = _dot_nt(doret, wbr_ref[...])
        rg = rg_ref[...].astype(F32)
        sg = _sigmoid(rg)
        srg = rg * sg
        ynv = yn_ref[...].astype(F32)
        yw = ynv * gnw_ref[...]
        acc_br[...] += _dot_tn((srg * yw).astype(BF16), doret)
        drg_ref[...] = (dgated * yw * (sg * (1.0 + rg * (1.0 - sg)))).astype(BF16)
        dgs = dgated * srg
        dyn_ref[...] = dgs * gnw_ref[...]
        ggn_ref[...] += jnp.sum(dgs * ynv, axis=0, keepdims=True)

        @pl.when(pl.program_id(0) == nT - 1)
        def _():
            dwout_ref[...] = acc_out[...].astype(BF16)
            dwbm_ref[...] = acc_bm[...].astype(BF16)
            dwbr_ref[...] = acc_br[...].astype(BF16)

    def full(r, c):
        return pl.BlockSpec((r, c), lambda i: (0, 0), pipeline_mode=pl.Buffered(1))

    def rows(c, j=0):
        return pl.BlockSpec((tT, c), lambda i: (i, j))

    return pl.pallas_call(
        body, name=name, grid=(nT,),
        in_specs=[rows(D), rows(D), full(1, D), rows(D), rows(D), rows(RW, 3), rows(D, g_blk), rows(D, g_blk + 1),
                  rows(RW), full(1, RW), rows(QW), full(D, D), full(QW, D), full(RW, D)],
        out_specs=[rows(D), rows(D), rows(QW), rows(QW), rows(RW), rows(RW), full(1, D), full(1, RW),
                   full(D, D), full(QW, D), full(RW, D)],
        out_shape=[jax.ShapeDtypeStruct((T, D), BF16)] * 2
        + [jax.ShapeDtypeStruct((T, QW), BF16), jax.ShapeDtypeStruct((T, QW), F32),
           jax.ShapeDtypeStruct((T, RW), BF16), jax.ShapeDtypeStruct((T, RW), F32),
           jax.ShapeDtypeStruct((1, D), F32), jax.ShapeDtypeStruct((1, RW), F32),
           jax.ShapeDtypeStruct((D, D), BF16), jax.ShapeDtypeStruct((QW, D), BF16), jax.ShapeDtypeStruct((RW, D), BF16)],
        scratch_shapes=[pltpu.VMEM((D, D), F32), pltpu.VMEM((QW, D), F32), pltpu.VMEM((RW, D), F32)],
        compiler_params=_params(("arbitrary",)),
    )(dho, m, post_w, omla, oret, proj, proj, proj, yn, gn_w, o, w_out, w_bm, w_br)


def _mesh_pos():
    return lax.axis_index("x"), lax.axis_index("y"), lax.axis_index("c")


class _Gather:
    def __init__(self, shards):
        self.operands = list(shards)
        self.n = len(shards)
        self.out_shape = [jax.ShapeDtypeStruct((N_DEV,) + s.shape, s.dtype) for s in shards]
        self.scratch = [pltpu.SemaphoreType.DMA((7 * self.n,)), pltpu.SemaphoreType.DMA((7 * self.n,)),
                        pltpu.SemaphoreType.DMA((self.n,))]

    def phase(self, p, x_refs, out_refs, sems):
        send_sems, recv_sems, local_sems = sems
        x, y, c = _mesh_pos()
        me, sibling = (x, y, c), (x, y, 1 - c)
        chips = [(1 - x, y), (x, 1 - y), (1 - x, 1 - y)]

        def copy(w, k, block, to, src=None):
            slot = out_refs[w].at[4 * block[0] + 2 * block[1] + block[2]]
            return pltpu.make_async_remote_copy(
                src_ref=slot if src is None else src, dst_ref=slot,
                send_sem=send_sems.at[7 * w + k], recv_sem=recv_sems.at[7 * w + k],
                device_id=to, device_id_type=pl.DeviceIdType.MESH)

        for w in range(self.n):
            mine = pltpu.make_async_copy(x_refs[w], out_refs[w].at[4 * x + 2 * y + c], local_sems.at[w])
            first = [copy(w, 0, me, sibling, src=x_refs[w])]
            first += [copy(w, 1 + j, me, (*chip, c), src=x_refs[w]) for j, chip in enumerate(chips)]
            passed = [copy(w, 4 + j, (*chip, c), sibling) for j, chip in enumerate(chips)]
            if p == 0:
                mine.start()
                for cp in first:
                    cp.start()
            elif p == 1:
                for j, chip in enumerate(chips):
                    copy(w, 1 + j, (*chip, c), me).wait_recv()
                    passed[j].start()
            else:
                copy(w, 0, sibling, me).wait_recv()
                for j, chip in enumerate(chips):
                    copy(w, 4 + j, (*chip, 1 - c), me).wait_recv()
                for cp in first + passed:
                    cp.wait_send()
                mine.wait()


class _Scatter:
    def __init__(self, grads, whole=()):
        self.n_sliced = len(grads)
        self.operands = list(grads) + list(whole)
        self.n = len(self.operands)
        self.out_shape = [jax.ShapeDtypeStruct(g.shape, g.dtype) for g in grads]
        self.out_shape += [jax.ShapeDtypeStruct((N_DEV,) + a.shape, a.dtype) for a in whole]
        n_sem = (N_DEV - 1) * self.n
        self.scratch = [pltpu.SemaphoreType.DMA((n_sem,)), pltpu.SemaphoreType.DMA((n_sem,)),
                        pltpu.SemaphoreType.DMA((self.n,))]

    def phase(self, p, in_refs, out_refs, sems):
        if p == 1:
            return
        send_sems, recv_sems, local_sems = sems
        x, y, c = _mesh_pos()
        me = 4 * x + 2 * y + c

        def src(w, dev):
            return in_refs[w].at[dev] if w < self.n_sliced else in_refs[w]

        for w in range(self.n):
            own = None if local_sems is None else pltpu.make_async_copy(src(w, me), out_refs[w].at[me], local_sems.at[w])
            sends, recvs = [], []
            for r in range(1, N_DEV):
                px = 1 - x if r & 4 else x
                py = 1 - y if r & 2 else y
                pc = 1 - c if r & 1 else c
                peer, pidx = (px, py, pc), 4 * px + 2 * py + pc
                k = (N_DEV - 1) * w + r - 1
                sends.append(pltpu.make_async_remote_copy(
                    src_ref=src(w, pidx), dst_ref=out_refs[w].at[me], send_sem=send_sems.at[k],
                    recv_sem=recv_sems.at[k], device_id=peer, device_id_type=pl.DeviceIdType.MESH))
                recvs.append(pltpu.make_async_remote_copy(
                    src_ref=src(w, me), dst_ref=out_refs[w].at[pidx], send_sem=send_sems.at[k],
                    recv_sem=recv_sems.at[k], device_id=peer, device_id_type=pl.DeviceIdType.MESH))
            if p == 0:
                if own is not None:
                    own.start()
                for cp in sends:
                    cp.start()
            else:
                for cp in recvs:
                    cp.wait_recv()
                for cp in sends:
                    cp.wait_send()
                if own is not None:
                    own.wait()


class _SplitScatter:
    def __init__(self, ex, name):
        self.ex, self.name = ex, name

    def _specs(self):
        ex = self.ex
        hbm = pl.BlockSpec(memory_space=pltpu.HBM)
        sem = pl.BlockSpec(memory_space=pltpu.SEMAPHORE)
        effect = pltpu.CompilerParams(has_side_effects=pltpu.SideEffectType.DATAFLOW_SIDE_EFFECTING)
        buffers = [pltpu.HBM(a.shape, a.dtype) for a in ex.operands] + [pltpu.HBM(s.shape, s.dtype) for s in ex.out_shape]
        return hbm, sem, effect, buffers

    def start(self):
        ex, n = self.ex, self.ex.n
        n_sem = (N_DEV - 1) * n
        hbm, sem, effect, buffers = self._specs()
        in_hbm = lambda a: pltpu.with_memory_space_constraint(a, pltpu.HBM)

        me = 4 * lax.axis_index("x") + 2 * lax.axis_index("y") + lax.axis_index("c")
        lands = []
        for w, (a, s) in enumerate(zip(ex.operands, ex.out_shape)):
            mine = lax.dynamic_index_in_dim(a, me, 0, keepdims=True) if w < ex.n_sliced else a[None]
            lands.append(lax.dynamic_update_slice_in_dim(lax.empty(s.shape, s.dtype), mine, me, 0))

        def start_body(*refs):
            ex.phase(0, refs[:n], refs[n:2 * n], (refs[2 * n], refs[2 * n + 1], None))
            refs[-1][...] = jnp.zeros_like(refs[-1])

        self.started = pl.pallas_call(
            start_body, name=self.name + "_start",
            out_shape=[pltpu.SemaphoreType.DMA((n_sem,)), pltpu.SemaphoreType.DMA((n_sem,))] + buffers
            + [jax.ShapeDtypeStruct((8, LANES), F32)],
            in_specs=[hbm] * (2 * n), out_specs=[sem, sem] + [hbm] * (2 * n) + [pl.BlockSpec(memory_space=pltpu.VMEM)],
            input_output_aliases={i: 2 + i for i in range(2 * n)}, compiler_params=effect,
        )(*[in_hbm(a) for a in ex.operands], *[in_hbm(a) for a in lands])
        return self.started[-1]

    def wait(self, after):
        ex, n = self.ex, self.ex.n
        hbm, sem, effect, buffers = self._specs()
        anyspec = pl.BlockSpec(memory_space=pl.ANY)

        def wait_body(*refs):
            ex.phase(2, refs[:n], refs[n:2 * n], (refs[2 * n], refs[2 * n + 1], None))

        done = pl.pallas_call(
            wait_body, name=self.name + "_wait", out_shape=buffers,
            in_specs=[hbm] * (2 * n) + [sem, sem] + [anyspec] * len(after), out_specs=[hbm] * (2 * n),
            input_output_aliases={i: i for i in range(2 * n)}, compiler_params=effect,
        )(*self.started[2:2 + 2 * n], self.started[0], self.started[1], *after)
        return done[n:]


def _exchange_alone(ex, *, name):
    n = ex.n

    def body(*refs):
        for p in range(3):
            ex.phase(p, refs[:n], refs[n:2 * n], refs[2 * n:])

    anyspec = pl.BlockSpec(memory_space=pl.ANY)
    return pl.pallas_call(body, name=name, out_shape=ex.out_shape, in_specs=[anyspec] * n,
                          out_specs=[anyspec] * n, scratch_shapes=ex.scratch)(*ex.operands)


def _adam_step(w_ref, p_ref, m_ref, v_ref, g_ref, d_ref, nm_ref, nv_ref):
    g = p_ref[0].astype(F32)
    for j in range(1, N_DEV):
        g = g + p_ref[j].astype(F32)
    g_ref[...] = g
    nm = ADAM_B1 * m_ref[...] + (1.0 - ADAM_B1) * g
    nv = ADAM_B2 * v_ref[...] + (1.0 - ADAM_B2) * (g * g)
    nm_ref[...] = nm
    nv_ref[...] = nv
    m_hat = nm / (1.0 - ADAM_B1 ** ADAM_STEP)
    v_hat = nv / (1.0 - ADAM_B2 ** ADAM_STEP)
    d_ref[...] = -ADAM_LR * (m_hat / (jnp.sqrt(v_hat) + ADAM_EPS) + ADAM_WD * w_ref[...])


def _adamw_vectors(ws, parts, ms, vs, *, name):
    n = len(ws)

    def body(*refs):
        w_refs, p_refs, m_refs, v_refs = (refs[i * n:(i + 1) * n] for i in range(4))
        outs = refs[4 * n:]
        for i in range(n):
            _adam_step(w_refs[i], p_refs[i], m_refs[i], v_refs[i], *outs[4 * i:4 * i + 4])

    return pl.pallas_call(
        body, name=name,
        out_shape=[jax.ShapeDtypeStruct(w.shape, F32) for w in ws for _ in range(4)],
    )(*ws, *parts, *ms, *vs)


def _adamw(w, parts, m, v, after, *, name):
    G, R, n = w.shape
    tn = 256 if (n > 256 and n % 256 == 0) else n
    tr = R
    for t in range(16, R, 16):
        if R % t == 0 and t * tn <= 160 * 1024:
            tr = t
    if R * tn <= 160 * 1024:
        tr = R

    def body(w_ref, p_ref, m_ref, v_ref, after_ref, g_ref, d_ref, nm_ref, nv_ref):
        _adam_step(w_ref, p_ref, m_ref, v_ref, g_ref, d_ref, nm_ref, nv_ref)

    blk = pl.BlockSpec((None, tr, tn), lambda g, i, j: (g, i, j))
    return pl.pallas_call(
        body, name=name, grid=(G, R // tr, n // tn),
        in_specs=[blk, pl.BlockSpec((N_DEV, None, tr, tn), lambda g, i, j: (0, g, i, j)), blk, blk,
                  pl.BlockSpec((8, LANES), lambda g, i, j: (0, 0))],
        out_specs=[blk, blk, blk, blk],
        out_shape=[jax.ShapeDtypeStruct((G, R, n), F32)] * 4,
        compiler_params=_params(("parallel", "parallel", "parallel")),
    )(w, parts, m, v, after)


def _pad_last(a, width):
    return jnp.pad(a, [(0, 0)] * (a.ndim - 1) + [(0, width - a.shape[-1])])


def _cols_of(g):
    return g.transpose(1, 0, 2).reshape(g.shape[1], N_DEV * g.shape[2])


def _col_shards(w):
    return w.reshape(w.shape[0], N_DEV, w.shape[1] // N_DEV).transpose(1, 0, 2)


def kernel(x, positions, ffn1_pre_w, ffn1_w1, ffn1_w2, ffn1_post_w, mix_pre_w, w_in, mla_q_norm_w, mla_w_uq, mla_kv_norm_w, mla_w_ukv, ret_gn_w, w_branch_mla, w_branch_ret, w_out, mix_post_w, ffn2_pre_w, ffn2_w1, ffn2_w2, ffn2_post_w, loss_target, m_ffn1_pre_w, m_ffn1_w1, m_ffn1_w2, m_ffn1_post_w, m_mix_pre_w, m_w_in, m_mla_q_norm_w, m_mla_w_uq, m_mla_kv_norm_w, m_mla_w_ukv, m_ret_gn_w, m_w_branch_mla, m_w_branch_ret, m_w_out, m_mix_post_w, m_ffn2_pre_w, m_ffn2_w1, m_ffn2_w2, m_ffn2_post_w, v_ffn1_pre_w, v_ffn1_w1, v_ffn1_w2, v_ffn1_post_w, v_mix_pre_w, v_w_in, v_mla_q_norm_w, v_mla_w_uq, v_mla_kv_norm_w, v_mla_w_ukv, v_ret_gn_w, v_w_branch_mla, v_w_branch_ret, v_w_out, v_mix_post_w, v_ffn2_pre_w, v_ffn2_w1, v_ffn2_w2, v_ffn2_post_w):
    T, D = x.shape[1], x.shape[2]
    h0 = x[0]
    tgt = loss_target[0]
    pos = positions.reshape(T, 1).astype(F32)

    big = [("ffn1_w1", ffn1_w1, m_ffn1_w1, v_ffn1_w1), ("ffn1_w2", ffn1_w2, m_ffn1_w2, v_ffn1_w2),
           ("w_in", w_in, m_w_in, v_w_in), ("mla_w_uq", mla_w_uq, m_mla_w_uq, v_mla_w_uq),
           ("mla_w_ukv", mla_w_ukv, m_mla_w_ukv, v_mla_w_ukv),
           ("w_branch_mla", w_branch_mla, m_w_branch_mla, v_w_branch_mla),
           ("w_branch_ret", w_branch_ret, m_w_branch_ret, v_w_branch_ret),
           ("w_out", w_out, m_w_out, v_w_out),
           ("ffn2_w1", ffn2_w1, m_ffn2_w1, v_ffn2_w1), ("ffn2_w2", ffn2_w2, m_ffn2_w2, v_ffn2_w2)]
    small = [("ffn1_pre_w", ffn1_pre_w, m_ffn1_pre_w, v_ffn1_pre_w), ("ffn1_post_w", ffn1_post_w, m_ffn1_post_w, v_ffn1_post_w),
             ("mix_pre_w", mix_pre_w, m_mix_pre_w, v_mix_pre_w), ("mla_q_norm_w", mla_q_norm_w, m_mla_q_norm_w, v_mla_q_norm_w),
             ("mla_kv_norm_w", mla_kv_norm_w, m_mla_kv_norm_w, v_mla_kv_norm_w), ("ret_gn_w", ret_gn_w, m_ret_gn_w, v_ret_gn_w),
             ("mix_post_w", mix_post_w, m_mix_post_w, v_mix_post_w), ("ffn2_pre_w", ffn2_pre_w, m_ffn2_pre_w, v_ffn2_pre_w),
             ("ffn2_post_w", ffn2_post_w, m_ffn2_post_w, v_ffn2_post_w)]

    half = ffn1_w2.shape[1]
    hp = -(-half // LANES) * LANES

    def rows_view(w):
        return w[0].T

    def send_w1(w):
        return jnp.pad(rows_view(w).reshape(2, half, D), ((0, 0), (0, hp - half), (0, 0))).reshape(2 * hp, D).astype(BF16)

    def send_w2(w):
        return jnp.pad(w[0], ((0, hp - half), (0, 0))).astype(BF16)

    mixer = ["w_in", "mla_w_uq", "mla_w_ukv", "w_branch_mla", "w_branch_ret", "w_out"]
    uq_w = MLA_NOPE + MLA_ROPE
    mixer_send = [rows_view(w_in).astype(BF16), jnp.pad(rows_view(mla_w_uq), ((0, HP - uq_w), (0, 0))).astype(BF16),
                  mla_w_ukv[0].astype(BF16), w_branch_mla[0].astype(BF16), w_branch_ret[0].astype(BF16),
                  w_out[0].astype(BF16)]

    w1a, w2a = _exchange_alone(_Gather([send_w1(ffn1_w1), send_w2(ffn1_w2)]), name="gather_ffn1")
    w2a = w2a.reshape(N_DEV // 2, 2 * hp, D)
    u1, f1, h1, *got = _ffn_fwd(h0, ffn1_pre_w, w1a, w2a, ffn1_post_w, None, name="ffn1_fwd_gather_mixer",
                                exchange=_Gather(mixer_send))
    fw = dict(zip(mixer, got))

    wi = fw["w_in"].reshape(-1, D)
    cq_w, ckv_w, kr_w = wi[0:384], wi[384:640], wi[640:672]
    rq_w, rk_w = wi[672:928], wi[928:1184]
    rv_w, rg_w = wi[1184:1696], wi[1696:2208]
    gm_w, gr_w = wi[2208:2208 + D], wi[2208 + D:2208 + 2 * D]
    zer = lambda n: jnp.zeros((n, D), BF16)
    head_rows = lambda a, h: jnp.pad(a.reshape(h, -1, D), ((0, 0), (0, HP - a.shape[0] // h), (0, 0))).reshape(h * HP, D)
    w_in_p = jnp.concatenate([head_rows(rq_w, RET_HEADS), head_rows(rk_w, RET_HEADS), rv_w, rg_w,
                              cq_w, ckv_w, zer(MLA_NOPE), kr_w, zer(HP - MLA_NOPE - MLA_ROPE), zer(AW - 768),
                              gm_w, gr_w], axis=0)
    w_uq_p = fw["mla_w_uq"].reshape(QW, MLA_Q_RANK)
    ukv = fw["mla_w_ukv"].transpose(1, 0, 2)
    w_kv_p = jnp.concatenate([_pad_last(ukv[:, :, :MLA_NOPE], HP).reshape(MLA_KV_RANK, QW),
                              _pad_last(ukv[:, :, MLA_NOPE:], HP).reshape(MLA_KV_RANK, QW)], axis=1)
    w_bm_p = jnp.pad(_cols_of(fw["w_branch_mla"]).reshape(MLA_HEADS, MLA_V, D),
                     ((0, 0), (0, HP - MLA_V), (0, 0))).reshape(QW, D)
    w_br, w_o = _cols_of(fw["w_branch_ret"]), fw["w_out"].reshape(D, D)
    tab_mla = _rope_table(MLA_NOPE, MLA_ROPE // 2)
    tab_ret = _rope_table(0, RET_DK // 2)

    proj, a1 = _rms_matmul(h1, mix_pre_w, w_in_p, name="mixer_in_proj")
    q, k, v, qn, kvn = _mla_prep_fwd(proj, pos, mla_q_norm_w, mla_kv_norm_w, w_uq_p, w_kv_p, tab_mla, name="mla_prep_fwd")
    o, lse, w1b, w2b = _flash_fwd(q, k, v, name="mla_attn_fwd_gather_ffn2",
                                  exchange=_Gather([send_w1(ffn2_w1), send_w2(ffn2_w2)]))
    w2b = w2b.reshape(N_DEV // 2, 2 * hp, D)
    ypre, yn, rprev = _ret_fwd(proj, pos, tab_ret, name="retention_fwd")
    omla, oret, m, h2 = _merge_fwd(o, yn, proj, ret_gn_w, w_bm_p, w_br, w_o, h1, mix_post_w, name="merge_fwd")
    u2, f2, _, dy, lossp = _ffn_fwd(h2, ffn2_pre_w, w1b, w2b, ffn2_post_w, tgt, name="ffn2_fwd_loss")

    def grad(x, dy, tag, after=None):
        return _matmul_tn(x if x.ndim == 3 else x[None], dy if dy.ndim == 3 else dy[None], name=tag, after=after)

    g2, du2, df2, a2, dh2, gpost2, gpre2 = _ffn_bwd(dy, f2, ffn2_post_w, h2, ffn2_pre_w, u2, w2b, w1b, name="ffn2_bwd")
    dw1b, = grad(du2.reshape(N_DEV, T, 2 * hp), a2, "ffn2_dw1")
    dw2b = grad(g2, df2, "ffn2_dw2")[0].reshape(N_DEV, hp, D)
    (dgm, dgr, do, delta, drg, dyn, gpostm, ggn, dw_out, dw_bm_p, dw_br) = _merge_bwd(
        dh2, m, mix_post_w, omla, oret, proj, yn, ret_gn_w, o, w_o, w_bm_p, w_br, name="merge_bwd")
    sc_ffn2 = _SplitScatter(_Scatter([dw1b, dw2b]), "scatter_ffn2")
    dq, dk, dv = _flash_bwd(q, k, v, do, lse, delta, name="mla_attn_bwd", after=sc_ffn2.start())
    da, dql, dkvl, gqn, gkvn = _mla_prep_bwd(dq, dk, dv, proj, pos, mla_q_norm_w, mla_kv_norm_w, w_uq_p, w_kv_p, tab_mla, name="mla_prep_bwd")
    dw_uq_p = grad(dql, qn, "dw_uq")[0][0]
    dw_kv_p = grad(kvn, dkvl, "dw_ukv")[0][0]
    drq, drk, drv = _ret_bwd(dyn, ypre, proj, pos, tab_ret, rprev, name="retention_bwd")
    dproj = jnp.concatenate([drq, drk, drv, drg, da, dgm, dgr], axis=1)
    dw_in_p = grad(dproj, a1, "dw_in")[0][0]

    dw_uq = dw_uq_p.reshape(MLA_HEADS, HP, MLA_Q_RANK)[:, :uq_w]
    dkp = dw_kv_p[:, :QW].reshape(MLA_KV_RANK, MLA_HEADS, HP)[:, :, :MLA_NOPE]
    dvp = dw_kv_p[:, QW:].reshape(MLA_KV_RANK, MLA_HEADS, HP)[:, :, :MLA_V]
    dw_ukv = jnp.concatenate([dkp, dvp], axis=2).transpose(1, 0, 2)
    dw_bm = dw_bm_p.reshape(MLA_HEADS, HP, D)[:, :MLA_V].reshape(MLA_HEADS * MLA_V, D)
    small_mixer_grads = [dw_uq, dw_ukv, _col_shards(dw_bm), _col_shards(dw_br), dw_out.reshape(N_DEV, D // N_DEV, D)]
    sc_small = _SplitScatter(_Scatter(small_mixer_grads), "scatter_mixer_small")
    dh1, gmixpre = _proj_bwd(dproj, w_in_p, h1, mix_pre_w, dh2, name="mixer_in_bwd", after=sc_small.start())
    unhead = lambda a, h, wd: a.reshape(h, HP, D)[:, :wd].reshape(h * wd, D)
    c0 = 4 * RW
    dw_in = jnp.concatenate([
        dw_in_p[c0:c0 + 384], dw_in_p[c0 + 384:c0 + 640], dw_in_p[c0 + 640 + MLA_NOPE:c0 + 640 + MLA_NOPE + MLA_ROPE],
        unhead(dw_in_p[0:RW], RET_HEADS, RET_DK), unhead(dw_in_p[RW:2 * RW], RET_HEADS, RET_DK),
        dw_in_p[2 * RW:3 * RW], dw_in_p[3 * RW:4 * RW],
        dw_in_p[PROJ_FIXED:PROJ_FIXED + D], dw_in_p[PROJ_FIXED + D:PROJ_FIXED + 2 * D]], axis=0).reshape(N_DEV, -1, D)
    sc_w_in = _SplitScatter(_Scatter([dw_in]), "scatter_w_in")
    g1, du1, df1, a0, dx, gpost1, gpre1 = _ffn_bwd(
        dh1, f1, ffn1_post_w, h0, ffn1_pre_w, u1, w2a, w1a, name="ffn1_bwd", after=sc_w_in.start())
    dw2a = grad(g1, df1, "ffn1_dw2")[0].reshape(N_DEV, hp, D)
    sc_dw2a = _SplitScatter(_Scatter([dw2a]), "scatter_ffn1_dw2")
    dw1a, = grad(du1.reshape(N_DEV, T, 2 * hp), a0, "ffn1_dw1", after=sc_dw2a.start())

    small_g = {"ffn1_pre_w": gpre1, "ffn1_post_w": gpost1, "mix_pre_w": gmixpre, "mla_q_norm_w": gqn,
               "mla_kv_norm_w": gkvn, "ret_gn_w": ggn, "mix_post_w": gpostm, "ffn2_pre_w": gpre2, "ffn2_post_w": gpost2}
    sc_last = _SplitScatter(_Scatter([dw1a], whole=[small_g[nm] for nm, *_ in small] + [lossp]), "scatter_ffn1_dw1")
    token = sc_last.start()
    recv_ffn2 = sc_ffn2.wait([token])
    recv_mixer = sc_w_in.wait([token]) + sc_small.wait([token])
    recv_w2a, = sc_dw2a.wait([token])
    parts = dict(zip(mixer, recv_mixer))
    parts.update(ffn1_w2=recv_w2a, ffn2_w1=recv_ffn2[0], ffn2_w2=recv_ffn2[1])
    as_is = (lambda a: a, lambda p: p[:, None], lambda a: a)
    views = {nm: as_is for nm, *_ in big}
    for nm in ("ffn1_w1", "ffn2_w1"):
        views[nm] = (lambda a: rows_view(a).reshape(2, half, D), lambda p: p.reshape(N_DEV, 2, hp, D),
                     lambda a: a.reshape(2 * half, D).T[None])
    for nm in ("w_in", "mla_w_uq"):
        views[nm] = (lambda a: rows_view(a)[None], lambda p: p[:, None], lambda a: a[0].T[None])

    def update(nm, w, m_, v_, after):
        to_view, parts_view, back = views[nm]
        return [back(a) for a in _adamw(to_view(w), parts_view(parts[nm]), to_view(m_), to_view(v_), after,
                                        name="adamw_" + nm)]

    big_out = {nm: update(nm, w, m_, v_, token) for nm, w, m_, v_ in big if nm != "ffn1_w1"}
    recv_w1a, *small_parts, loss_parts = sc_last.wait([d[0] for d in big_out.values()])
    loss = jnp.sum(loss_parts[:, ::8, 0])
    parts["ffn1_w1"] = recv_w1a
    big_out["ffn1_w1"] = update("ffn1_w1", ffn1_w1, m_ffn1_w1, v_ffn1_w1, jnp.zeros((8, LANES), F32))
    small_out = _adamw_vectors([w for _, w, _, _ in small], small_parts, [a for _, _, a, _ in small],
                               [a for _, _, _, a in small], name="adamw_replicated")

    order = ["ffn1_pre_w", "ffn1_w1", "ffn1_w2", "ffn1_post_w", "mix_pre_w", "w_in", "mla_q_norm_w", "mla_w_uq",
             "mla_kv_norm_w", "mla_w_ukv", "ret_gn_w", "w_branch_mla", "w_branch_ret", "w_out", "mix_post_w",
             "ffn2_pre_w", "ffn2_w1", "ffn2_w2", "ffn2_post_w"]
    outs = [loss, dx[None]]
    for i in range(4):
        both = {nm: big_out[nm][i] for nm in big_out}
        both.update({nm: small_out[4 * j + i] for j, (nm, *_) in enumerate(small)})
        outs += [both[nm] for nm in order]
    return tuple(outs)
```

```python
import math

import numpy as np
import jax
import jax.numpy as jnp
from jax import lax
from jax.experimental import pallas as pl
from jax.experimental.pallas import tpu as pltpu

F32, BF16 = jnp.float32, jnp.bfloat16

MLA_HEADS, MLA_NOPE, MLA_ROPE, MLA_V = 8, 64, 32, 64
MLA_Q_RANK, MLA_KV_RANK = 384, 256
RET_HEADS, RET_DK, RET_DV = 4, 64, 128
ROPE_BASE, NORM_EPS, GN_EPS = 10000.0, 1e-6, 1e-6
ADAM_LR, ADAM_B1, ADAM_B2, ADAM_EPS, ADAM_WD, ADAM_STEP = 0.001, 0.9, 0.999, 1e-08, 0.01, 10
ATTN_SCALE = 1.0 / math.sqrt(MLA_NOPE + MLA_ROPE)

N_DEV = 8
LANES = 128
HP = LANES
QW = MLA_HEADS * HP
RW = RET_HEADS * HP
AW = 1024
PROJ_FIXED = 4 * RW + AW
NEG = -1e30

TOKEN_TILE = 512
ATTN_TILE = 1024
ATTN_CHAINS = 2
FFN_CHAINS = 2
RET_TILE = 256
PROJ_TILE_CAP = 2560
GRAD_TILE_CAP = 1408
GRAD_TOKEN_TILE = 2048
MERGE_TILE = 256
VMEM_LIMIT = 56 * 1024 * 1024


def _tile(n, cap, mult=LANES):
    if n <= cap:
        return n
    best = None
    for t in range(mult, cap + 1, mult):
        if n % t == 0:
            best = t
    assert best is not None, (n, cap, mult)
    return best


def _params(sem):
    return pltpu.CompilerParams(dimension_semantics=sem, vmem_limit_bytes=VMEM_LIMIT)


def _dot(a, b):
    return lax.dot_general(a, b, (((1,), (0,)), ((), ())), preferred_element_type=F32)


def _dot_nt(a, b):
    return lax.dot_general(a, b, (((1,), (1,)), ((), ())), preferred_element_type=F32)


def _dot_tn(a, b):
    return lax.dot_general(a, b, (((0,), (0,)), ((), ())), preferred_element_type=F32)


def _sigmoid(x):
    return pl.reciprocal(1.0 + jnp.exp(-x), approx=True)


def _rms_fwd(x, w):
    r = lax.rsqrt(jnp.mean(x * x, axis=-1, keepdims=True) + NORM_EPS)
    return x * r * w


def _rms_bwd(x, w, dy):
    r = lax.rsqrt(jnp.mean(x * x, axis=-1, keepdims=True) + NORM_EPS)
    xh = x * r
    g = dy * w
    dx = r * (g - xh * jnp.mean(g * xh, axis=-1, keepdims=True))
    return dx, jnp.sum(dy * xh, axis=0, keepdims=True)


def _rope_table(first, half):
    inv = (np.float32(ROPE_BASE) ** (-(np.arange(half, dtype=np.float32) / np.float32(half)))).astype(np.float32)
    tab = np.zeros((8, LANES), np.float32)
    tab[0, first:first + half] = inv
    tab[0, first + half:first + 2 * half] = inv
    tab[1, first:first + half] = -1.0
    tab[2, first + half:first + 2 * half] = 1.0
    return jnp.asarray(tab)


def _rope_cs(pos, tab_ref):
    ang = pos * tab_ref[0:1, :]
    s = jnp.sin(ang)
    return jnp.cos(ang), s * tab_ref[1:2, :], s * tab_ref[2:3, :]


def _rope(x, cs, half, inverse=False):
    c, s1, s2 = cs
    a = pltpu.roll(x, LANES - half, 1) * s1 + pltpu.roll(x, half, 1) * s2
    return x * c - a if inverse else x * c + a


def _call(body, *, name, grid, in_specs, out_specs, out_shape, scratch_shapes, args, exchange=None, after=None):
    sem = ("arbitrary",) * len(grid)
    anyspec = pl.BlockSpec(memory_space=pl.ANY)
    if exchange is None and after is not None:
        n_own = len(in_specs)

        def behind(*refs):
            body(*refs[:n_own], *refs[n_own + 1:])

        return pl.pallas_call(behind, name=name, grid=grid, in_specs=list(in_specs) + [anyspec], out_specs=out_specs,
                              out_shape=out_shape, scratch_shapes=scratch_shapes, compiler_params=_params(sem))(*args, after)
    if exchange is None:
        return pl.pallas_call(body, name=name, grid=grid, in_specs=in_specs, out_specs=out_specs,
                              out_shape=out_shape, scratch_shapes=scratch_shapes, compiler_params=_params(sem))(*args)
    n_in, n_out, e = len(in_specs), len(out_specs), exchange.n
    total = math.prod(grid)

    def carried(*refs):
        own = refs[:n_in] + refs[n_in + e:n_in + e + n_out] + refs[n_in + 2 * e + n_out:len(refs) - 3]
        ex_refs = (refs[n_in:n_in + e], refs[n_in + e + n_out:n_in + 2 * e + n_out], refs[len(refs) - 3:])
        step = pl.program_id(0)
        for d in range(1, len(grid)):
            step = step * grid[d] + pl.program_id(d)

        @pl.when(step == 0)
        def _():
            exchange.phase(0, *ex_refs)

        @pl.when(step == (3 * total) // 4)
        def _():
            exchange.phase(1, *ex_refs)

        body(*own)

        @pl.when(step == total - 1)
        def _():
            exchange.phase(2, *ex_refs)

    return pl.pallas_call(
        carried, name=name, grid=grid, in_specs=list(in_specs) + [anyspec] * e,
        out_specs=list(out_specs) + [anyspec] * e, out_shape=list(out_shape) + exchange.out_shape,
        scratch_shapes=list(scratch_shapes) + exchange.scratch, compiler_params=_params(sem),
    )(*args, *exchange.operands)


def _ffn_fwd(h, pre_w, w1, w2, post_w, target, *, name, exchange=None):
    T, D = h.shape
    nk, ck = w2.shape[0], w2.shape[1]
    tT = min(TOKEN_TILE, T)
    nT = T // tT
    with_loss = target is not None

    def body(*refs):
        if with_loss:
            (h_ref, pre_ref, w1g_ref, w1u_ref, w2_ref, post_ref, tgt_ref,
             u_ref, f_ref, ho_ref, dy_ref, loss_ref, a_s, acc) = refs
        else:
            (h_ref, pre_ref, w1g_ref, w1u_ref, w2_ref, post_ref,
             u_ref, f_ref, ho_ref, a_s, acc) = refs
        k = pl.program_id(1)

        @pl.when(k == 0)
        def _():
            a_s[...] = _rms_fwd(h_ref[...], pre_ref[...]).astype(BF16)
            acc[...] = jnp.zeros_like(acc)

        for c in range(FFN_CHAINS):
            rs = slice(c * (tT // FFN_CHAINS), (c + 1) * (tT // FFN_CHAINS))
            a = a_s[rs, :]
            ug = _dot_nt(a, w1g_ref[...])
            uu = _dot_nt(a, w1u_ref[...])
            u_ref[0, rs, :] = ug.astype(BF16)
            u_ref[1, rs, :] = uu.astype(BF16)
            acc[rs, :] += _dot((ug * _sigmoid(ug) * uu).astype(BF16), w2_ref[...])

        @pl.when(k == nk - 1)
        def _():
            f = acc[...]
            f_ref[...] = f
            ho = h_ref[...] + 0.5 * _rms_fwd(f, post_ref[...])
            ho_ref[...] = ho
            if with_loss:
                e = ho - tgt_ref[...]
                dy_ref[...] = e * (1.0 / D)
                loss_ref[...] = jnp.full(loss_ref.shape, (0.5 / D) * jnp.sum(e * e), F32)

    row = pl.BlockSpec((tT, D), lambda i, k: (i, 0))
    vec = pl.BlockSpec((1, D), lambda i, k: (0, 0))
    in_specs = [row, vec,
                pl.BlockSpec((None, ck, D), lambda i, k: (k, 0, 0)),
                pl.BlockSpec((None, ck, D), lambda i, k: (nk + k, 0, 0)),
                pl.BlockSpec((None, ck, D), lambda i, k: (k, 0, 0)),
                vec]
    out_shape = [jax.ShapeDtypeStruct((2, nk, T, ck), BF16),
                 jax.ShapeDtypeStruct((T, D), F32),
                 jax.ShapeDtypeStruct((T, D), F32)]
    out_specs = [pl.BlockSpec((2, None, tT, ck), lambda i, k: (0, k, i, 0)), row, row]
    args = [h, pre_w, w1, w1, w2, post_w]
    if with_loss:
        in_specs.append(row)
        args.append(target)
        out_shape += [jax.ShapeDtypeStruct((T, D), F32), jax.ShapeDtypeStruct((nT * 8, LANES), F32)]
        out_specs += [row, pl.BlockSpec((8, LANES), lambda i, k: (i, 0))]
    return _call(body, name=name, grid=(nT, nk), in_specs=in_specs, out_specs=out_specs, out_shape=out_shape,
                 scratch_shapes=[pltpu.VMEM((tT, D), BF16), pltpu.VMEM((tT, D), F32)], args=args, exchange=exchange)


def _ffn_bwd(dho, f, post_w, h, pre_w, u, w2, w1, *, name, exchange=None, after=None):
    T, D = h.shape
    nk, ck = w2.shape[0], w2.shape[1]
    tT = min(TOKEN_TILE, T)
    nT = T // tT

    def body(dho_ref, f_ref, post_ref, h_ref, pre_ref, u_ref, w2_ref, w1g_ref, w1u_ref,
             g_ref, du_ref, df_ref, a_ref, dh_ref, gpost_ref, gpre_ref, df_s, da_acc):
        i, k = pl.program_id(0), pl.program_id(1)

        @pl.when(jnp.logical_and(i == 0, k == 0))
        def _():
            gpost_ref[...] = jnp.zeros_like(gpost_ref)
            gpre_ref[...] = jnp.zeros_like(gpre_ref)

        @pl.when(k == 0)
        def _():
            dx, dw = _rms_bwd(f_ref[...], post_ref[...], 0.5 * dho_ref[...])
            dfb = dx.astype(BF16)
            df_s[...] = dfb
            df_ref[...] = dfb
            gpost_ref[...] += dw
            a_ref[...] = _rms_fwd(h_ref[...], pre_ref[...]).astype(BF16)
            da_acc[...] = jnp.zeros_like(da_acc)

        groups = [slice(c * (tT // FFN_CHAINS), (c + 1) * (tT // FFN_CHAINS)) for c in range(FFN_CHAINS)]
        dgs = [_dot_nt(df_s[rs, :], w2_ref[...]) for rs in groups]
        for rs, dg in zip(groups, dgs):
            ug = u_ref[0, rs, :].astype(F32)
            uu = u_ref[1, rs, :].astype(F32)
            sg = _sigmoid(ug)
            sl = ug * sg
            g_ref[rs, :] = (sl * uu).astype(BF16)
            dug = (dg * uu * (sg + sl * (1.0 - sg))).astype(BF16)
            duu = (dg * sl).astype(BF16)
            du_ref[0, rs, :] = dug
            du_ref[1, rs, :] = duu
            da_acc[rs, :] += _dot(dug, w1g_ref[...]) + _dot(duu, w1u_ref[...])

        @pl.when(k == nk - 1)
        def _():
            dx, dw = _rms_bwd(h_ref[...], pre_ref[...], da_acc[...])
            dh_ref[...] = dho_ref[...] + dx
            gpre_ref[...] += dw

    row = pl.BlockSpec((tT, D), lambda i, k: (i, 0))
    vec = pl.BlockSpec((1, D), lambda i, k: (0, 0))
    return _call(
        body, name=name, grid=(nT, nk),
        in_specs=[row, row, vec, row, vec,
                  pl.BlockSpec((2, None, tT, ck), lambda i, k: (0, k, i, 0)),
                  pl.BlockSpec((None, ck, D), lambda i, k: (k, 0, 0)),
                  pl.BlockSpec((None, ck, D), lambda i, k: (k, 0, 0)),
                  pl.BlockSpec((None, ck, D), lambda i, k: (nk + k, 0, 0))],
        out_specs=[pl.BlockSpec((None, tT, ck), lambda i, k: (k, i, 0)),
                   pl.BlockSpec((2, None, tT, ck), lambda i, k: (0, k, i, 0)),
                   row, row, row, vec, vec],
        out_shape=[jax.ShapeDtypeStruct((nk, T, ck), BF16),
                   jax.ShapeDtypeStruct((2, nk, T, ck), BF16),
                   jax.ShapeDtypeStruct((T, D), BF16),
                   jax.ShapeDtypeStruct((T, D), BF16),
                   jax.ShapeDtypeStruct((T, D), F32),
                   jax.ShapeDtypeStruct((1, D), F32),
                   jax.ShapeDtypeStruct((1, D), F32)],
        scratch_shapes=[pltpu.VMEM((tT, D), BF16), pltpu.VMEM((tT, D), F32)],
        args=(dho, f, post_w, h, pre_w, u, w2, w1, w1), exchange=exchange, after=after)


def _matmul_tn(x, dy, *, name, exchange=None, after=None):
    Px, T, K = x.shape
    Py, _, N = dy.shape
    P = max(Px, Py)
    tT, tK, tN = min(GRAD_TOKEN_TILE, T), _tile(K, GRAD_TILE_CAP), _tile(N, GRAD_TILE_CAP)
    nt = T // tT

    def body(x_ref, dy_ref, o_ref, acc):
        t = pl.program_id(3)

        @pl.when(t == 0)
        def _():
            acc[...] = jnp.zeros_like(acc)

        acc[...] += _dot_tn(x_ref[...], dy_ref[...])

        @pl.when(t == nt - 1)
        def _():
            o_ref[...] = acc[...].astype(BF16)

    return _call(
        body, name=name, grid=(P, K // tK, N // tN, nt),
        in_specs=[pl.BlockSpec((None, tT, tK), lambda p, a, b, t: (p if Px > 1 else 0, t, a)),
                  pl.BlockSpec((None, tT, tN), lambda p, a, b, t: (p if Py > 1 else 0, t, b))],
        out_specs=[pl.BlockSpec((None, tK, tN), lambda p, a, b, t: (p, a, b))],
        out_shape=[jax.ShapeDtypeStruct((P, K, N), BF16)],
        scratch_shapes=[pltpu.VMEM((tK, tN), F32)], args=(x, dy), exchange=exchange, after=after)


def _rms_matmul(h, wn, w, *, name):
    T, D = h.shape
    N = w.shape[0]
    tT, tN = min(TOKEN_TILE, T), _tile(N, PROJ_TILE_CAP)

    def body(h_ref, wn_ref, w_ref, y_ref, a_ref):
        @pl.when(pl.program_id(1) == 0)
        def _():
            a_ref[...] = _rms_fwd(h_ref[...], wn_ref[...]).astype(BF16)

        y_ref[...] = _dot_nt(a_ref[...], w_ref[...]).astype(BF16)

    return pl.pallas_call(
        body, name=name, grid=(T // tT, N // tN),
        in_specs=[pl.BlockSpec((tT, D), lambda i, j: (i, 0)),
                  pl.BlockSpec((1, D), lambda i, j: (0, 0)),
                  pl.BlockSpec((tN, D), lambda i, j: (j, 0))],
        out_specs=[pl.BlockSpec((tT, tN), lambda i, j: (i, j)),
                   pl.BlockSpec((tT, D), lambda i, j: (i, 0))],
        out_shape=[jax.ShapeDtypeStruct((T, N), BF16), jax.ShapeDtypeStruct((T, D), BF16)],
        compiler_params=_params(("parallel", "arbitrary")),
    )(h, wn, w)


def _proj_bwd(dproj, w, h, wn, dres, *, name, exchange=None, after=None):
    T, D = h.shape
    N = w.shape[0]
    tT, tN = min(TOKEN_TILE, T), _tile(N, PROJ_TILE_CAP)
    nn = N // tN

    def body(dp_ref, w_ref, h_ref, wn_ref, dres_ref, dh_ref, gw_ref, acc):
        i, j = pl.program_id(0), pl.program_id(1)

        @pl.when(jnp.logical_and(i == 0, j == 0))
        def _():
            gw_ref[...] = jnp.zeros_like(gw_ref)

        @pl.when(j == 0)
        def _():
            acc[...] = jnp.zeros_like(acc)

        acc[...] += _dot(dp_ref[...], w_ref[...])

        @pl.when(j == nn - 1)
        def _():
            dx, dw = _rms_bwd(h_ref[...], wn_ref[...], acc[...])
            dh_ref[...] = dres_ref[...] + dx
            gw_ref[...] += dw

    row = pl.BlockSpec((tT, D), lambda i, j: (i, 0))
    vec = pl.BlockSpec((1, D), lambda i, j: (0, 0))
    return _call(
        body, name=name, grid=(T // tT, nn),
        in_specs=[pl.BlockSpec((tT, tN), lambda i, j: (i, j)),
                  pl.BlockSpec((tN, D), lambda i, j: (j, 0)), row, vec, row],
        out_specs=[row, vec],
        out_shape=[jax.ShapeDtypeStruct((T, D), F32), jax.ShapeDtypeStruct((1, D), F32)],
        scratch_shapes=[pltpu.VMEM((tT, D), F32)], args=(dproj, w, h, wn, dres), exchange=exchange, after=after)


def _mla_prep_fwd(proj, pos, qn_w, kvn_w, w_uq, w_kv, tab, *, name):
    T = proj.shape[0]
    tT = min(TOKEN_TILE, T)
    a_blk = PROJ_FIXED // AW - 1

    def body(a_ref, pos_ref, qnw_ref, kvnw_ref, wuq_ref, wkv_ref, tab_ref,
             q_ref, k_ref, v_ref):
        cq = a_ref[:, 0:MLA_Q_RANK].astype(F32)
        ckv = a_ref[:, MLA_Q_RANK:MLA_Q_RANK + MLA_KV_RANK].astype(F32)
        kr = a_ref[:, 640:768].astype(F32)
        qn = _rms_fwd(cq, qnw_ref[...]).astype(BF16)
        kvn = _rms_fwd(ckv, kvnw_ref[...]).astype(BF16)
        cs = _rope_cs(pos_ref[...], tab_ref)
        q = _dot_nt(qn, wuq_ref[...])
        kv = _dot(kvn, wkv_ref[...])
        krr = _rope(kr, cs, MLA_ROPE // 2)
        for hd in range(MLA_HEADS):
            sl = slice(hd * HP, (hd + 1) * HP)
            q_ref[:, sl] = (_rope(q[:, sl], cs, MLA_ROPE // 2) * ATTN_SCALE).astype(BF16)
            k_ref[:, sl] = (kv[:, sl] + krr).astype(BF16)
        v_ref[...] = kv[:, QW:].astype(BF16)

    def full(r, c):
        return pl.BlockSpec((r, c), lambda i: (0, 0))

    def rows(c):
        return pl.BlockSpec((tT, c), lambda i: (i, 0))

    return pl.pallas_call(
        body, name=name, grid=(T // tT,),
        in_specs=[pl.BlockSpec((tT, AW), lambda i: (i, a_blk)), rows(1),
                  full(1, MLA_Q_RANK), full(1, MLA_KV_RANK),
                  full(QW, MLA_Q_RANK), full(MLA_KV_RANK, 2 * QW), full(8, LANES)],
        out_specs=[rows(QW), rows(QW), rows(QW)],
        out_shape=[jax.ShapeDtypeStruct((T, QW), BF16)] * 3,
        compiler_params=_params(("parallel",)),
    )(proj, pos, qn_w, kvn_w, w_uq, w_kv, tab)


def _mla_prep_bwd(dq, dk, dv, proj, pos, qn_w, kvn_w, w_uq, w_kv, tab, *, name):
    T = proj.shape[0]
    tT = min(TOKEN_TILE, T)
    nT = T // tT
    a_blk = PROJ_FIXED // AW - 1

    def body(dq_ref, dk_ref, dv_ref, a_ref, pos_ref, qnw_ref, kvnw_ref, wuq_ref, wkv_ref, tab_ref,
             da_ref, gqn_ref, gkvn_ref, dwuq_ref, dwkv_ref, dql_ref, dkvl_ref, acc_uq, acc_kv):
        @pl.when(pl.program_id(0) == 0)
        def _():
            gqn_ref[...] = jnp.zeros_like(gqn_ref)
            gkvn_ref[...] = jnp.zeros_like(gkvn_ref)
            acc_uq[...] = jnp.zeros_like(acc_uq)
            acc_kv[...] = jnp.zeros_like(acc_kv)

        cs = _rope_cs(pos_ref[...], tab_ref)
        dkr = jnp.zeros((tT, HP), F32)
        for hd in range(MLA_HEADS):
            sl = slice(hd * HP, (hd + 1) * HP)
            dql_ref[:, sl] = (_rope(dq_ref[:, sl], cs, MLA_ROPE // 2, inverse=True) * ATTN_SCALE).astype(BF16)
            dkh = dk_ref[:, sl]
            dkr = dkr + dkh
            dkvl_ref[:, sl] = dkh.astype(BF16)
        dkvl_ref[:, QW:] = dv_ref[...]
        dqn = _dot(dql_ref[...], wuq_ref[...])
        dkvn = _dot_nt(dkvl_ref[...], wkv_ref[...])
        cq = a_ref[:, 0:MLA_Q_RANK].astype(F32)
        ckv = a_ref[:, MLA_Q_RANK:MLA_Q_RANK + MLA_KV_RANK].astype(F32)
        dcq, gq = _rms_bwd(cq, qnw_ref[...], dqn)
        dckv, gkv = _rms_bwd(ckv, kvnw_ref[...], dkvn)
        gqn_ref[...] += gq
        gkvn_ref[...] += gkv
        da_ref[:, 0:MLA_Q_RANK] = dcq.astype(BF16)
        da_ref[:, MLA_Q_RANK:MLA_Q_RANK + MLA_KV_RANK] = dckv.astype(BF16)
        da_ref[:, 640:768] = _rope(dkr, cs, MLA_ROPE // 2, inverse=True).astype(BF16)
        da_ref[:, 768:AW] = jnp.zeros((tT, AW - 768), BF16)
        acc_uq[...] += _dot_tn(dql_ref[...], _rms_fwd(cq, qnw_ref[...]).astype(BF16))
        acc_kv[...] += _dot_tn(_rms_fwd(ckv, kvnw_ref[...]).astype(BF16), dkvl_ref[...])

        @pl.when(pl.program_id(0) == nT - 1)
        def _():
            dwuq_ref[...] = acc_uq[...].astype(BF16)
            dwkv_ref[...] = acc_kv[...].astype(BF16)

    def full(r, c):
        return pl.BlockSpec((r, c), lambda i: (0, 0))

    def rows(c):
        return pl.BlockSpec((tT, c), lambda i: (i, 0))

    return pl.pallas_call(
        body, name=name, grid=(nT,),
        in_specs=[rows(QW), rows(QW), rows(QW), pl.BlockSpec((tT, AW), lambda i: (i, a_blk)), rows(1),
                  full(1, MLA_Q_RANK), full(1, MLA_KV_RANK),
                  full(QW, MLA_Q_RANK), full(MLA_KV_RANK, 2 * QW), full(8, LANES)],
        out_specs=[rows(AW), full(1, MLA_Q_RANK), full(1, MLA_KV_RANK),
                   full(QW, MLA_Q_RANK), full(MLA_KV_RANK, 2 * QW)],
        out_shape=[jax.ShapeDtypeStruct((T, AW), BF16),
                   jax.ShapeDtypeStruct((1, MLA_Q_RANK), F32), jax.ShapeDtypeStruct((1, MLA_KV_RANK), F32),
                   jax.ShapeDtypeStruct((QW, MLA_Q_RANK), BF16), jax.ShapeDtypeStruct((MLA_KV_RANK, 2 * QW), BF16)],
        scratch_shapes=[pltpu.VMEM((tT, QW), BF16), pltpu.VMEM((tT, 2 * QW), BF16),
                        pltpu.VMEM((QW, MLA_Q_RANK), F32), pltpu.VMEM((MLA_KV_RANK, 2 * QW), F32)],
        compiler_params=_params(("arbitrary",)),
    )(dq, dk, dv, proj, pos, qn_w, kvn_w, w_uq, w_kv, tab)


def _flash_fwd(q, k, v, *, name, exchange=None):
    T = q.shape[0]
    H = q.shape[1] // HP
    tq = min(ATTN_TILE, T)
    nq = T // tq

    sub = tq // ATTN_CHAINS

    def body(q_ref, k_ref, v_ref, o_ref, lse_ref):
        qi = pl.program_id(1)
        qs = [q_ref[c * sub:(c + 1) * sub, :] for c in range(ATTN_CHAINS)]

        def update(carry, off, masked):
            nks = [(c + 1) * sub if masked else tq for c in range(ATTN_CHAINS)]
            scores = [_dot_nt(qs[c], k_ref[pl.ds(off, nks[c]), :]) for c in range(ATTN_CHAINS)]
            out = []
            for c in range(ATTN_CHAINS):
                m_prev, l_prev, acc = carry[c]
                nk, s = nks[c], scores[c]
                vb = v_ref[pl.ds(off, nk), :]
                if masked:
                    rows = lax.broadcasted_iota(jnp.int32, (sub, nk), 0) + c * sub
                    s = jnp.where(rows >= lax.broadcasted_iota(jnp.int32, (sub, nk), 1), s, NEG)
                m_new = jnp.maximum(m_prev, jnp.max(s, axis=1, keepdims=True))
                alpha = jnp.exp(m_prev - m_new)
                p = jnp.exp(s - m_new)
                out.append((m_new, alpha * l_prev + jnp.sum(p, axis=1, keepdims=True),
                            alpha * acc + _dot(p.astype(BF16), vb)))
            return tuple(out)

        init = tuple((jnp.full((sub, 1), NEG, F32), jnp.zeros((sub, 1), F32), jnp.zeros((sub, HP), F32))
                     for _ in range(ATTN_CHAINS))
        carry = lax.fori_loop(0, qi, lambda j, cr: update(cr, pl.multiple_of(j * tq, tq), False), init)
        carry = update(carry, pl.multiple_of(qi * tq, tq), True)
        for c in range(ATTN_CHAINS):
            m_fin, l_fin, acc = carry[c]
            o_ref[c * sub:(c + 1) * sub, :] = (acc / l_fin).astype(BF16)
            lse_ref[c * sub:(c + 1) * sub, :] = jnp.broadcast_to(m_fin + jnp.log(l_fin), (sub, HP))

    qspec = pl.BlockSpec((tq, HP), lambda h, i: (i, h))
    kspec = pl.BlockSpec((T, HP), lambda h, i: (0, h))
    return _call(
        body, name=name, grid=(H, nq),
        in_specs=[qspec, kspec, kspec], out_specs=[qspec, qspec],
        out_shape=[jax.ShapeDtypeStruct((T, H * HP), BF16), jax.ShapeDtypeStruct((T, H * HP), F32)],
        scratch_shapes=[], args=(q, k, v), exchange=exchange)


def _flash_bwd(q, k, v, do, lse, delta, *, name, exchange=None, after=None):
    T = q.shape[0]
    H = q.shape[1] // HP
    tq = min(ATTN_TILE, T)
    nq = T // tq
    sub = tq // ATTN_CHAINS

    def body(k_ref, v_ref, q_ref, do_ref, lse_ref, dl_ref, dq_ref, dk_ref, dv_ref):
        ki = pl.program_id(1)

        @pl.when(ki == 0)
        def _():
            dq_ref[...] = jnp.zeros_like(dq_ref)

        def grow(a):
            return a if a.shape[0] == tq else jnp.concatenate([a, jnp.zeros((tq - a.shape[0], HP), F32)], axis=0)

        def step(carry, j, masked):
            dk_acc, dv_acc = carry
            nks = [(c + 1) * sub if masked else tq for c in range(ATTN_CHAINS)]
            rws = [pl.ds(pl.multiple_of(j * tq + c * sub, sub), sub) for c in range(ATTN_CHAINS)]
            scores = [_dot_nt(q_ref[rws[c], :], k_ref[0:nks[c], :]) for c in range(ATTN_CHAINS)]
            dps = [_dot_nt(do_ref[rws[c], :], v_ref[0:nks[c], :]) for c in range(ATTN_CHAINS)]
            for c in range(ATTN_CHAINS):
                rows, nk, s, dp = rws[c], nks[c], scores[c], dps[c]
                kb = k_ref[0:nk, :]
                qb = q_ref[rows, :]
                dob = do_ref[rows, :]
                if masked:
                    ri = lax.broadcasted_iota(jnp.int32, (sub, nk), 0) + c * sub
                    s = jnp.where(ri >= lax.broadcasted_iota(jnp.int32, (sub, nk), 1), s, NEG)
                p = jnp.exp(s - lse_ref[rows, 0:1])
                dv_acc = dv_acc + grow(_dot_tn(p.astype(BF16), dob))
                ds = (p * (dp - dl_ref[rows, 0:1])).astype(BF16)
                dk_acc = dk_acc + grow(_dot_tn(ds, qb))
                dq_ref[rows, :] += _dot(ds, kb)
            return dk_acc, dv_acc

        carry = step((jnp.zeros((tq, HP), F32), jnp.zeros((tq, HP), F32)), ki, True)
        dk_acc, dv_acc = lax.fori_loop(ki + 1, nq, lambda j, cr: step(cr, j, False), carry)
        dk_ref[...] = dk_acc
        dv_ref[...] = dv_acc.astype(BF16)

    kspec = pl.BlockSpec((tq, HP), lambda h, j: (j, h))
    full = pl.BlockSpec((T, HP), lambda h, j: (0, h))
    return _call(
        body, name=name, grid=(H, nq),
        in_specs=[kspec, kspec, full, full, full, full], out_specs=[full, kspec, kspec],
        out_shape=[jax.ShapeDtypeStruct((T, H * HP), F32), jax.ShapeDtypeStruct((T, H * HP), F32),
                   jax.ShapeDtypeStruct((T, H * HP), BF16)],
        scratch_shapes=[], args=(k, v, q, do, lse, delta), exchange=exchange, after=after)


def _ret_consts(cc, hd):
    lg = math.log(1.0 - 2.0 ** (-5.0 - hd))
    diff = (lax.broadcasted_iota(jnp.int32, (cc, cc), 0) - lax.broadcasted_iota(jnp.int32, (cc, cc), 1)).astype(F32)
    decay = jnp.where(diff >= 0, jnp.exp(jnp.maximum(diff, 0.0) * lg), 0.0)
    idx = lax.broadcasted_iota(jnp.int32, (cc, 1), 0).astype(F32)
    zeta = jnp.exp((cc - 1.0 - idx) * lg)
    xi = jnp.exp((idx + 1.0) * lg)
    return decay, zeta, xi, math.exp(cc * lg)


def _ret_fwd(proj, pos, tab, *, name):
    T = proj.shape[0]
    cc = min(RET_TILE, T)
    n = T // cc

    def body(rq_ref, rk_ref, rv_ref, pos_ref, tab_ref, y_ref, yn_ref, rprev_ref, r_s):
        @pl.when(pl.program_id(0) == 0)
        def _():
            r_s[...] = jnp.zeros_like(r_s)

        cs = _rope_cs(pos_ref[...], tab_ref)
        for hd in range(RET_HEADS):
            sl = slice(hd * HP, (hd + 1) * HP)
            decay, zeta, xi, gc = _ret_consts(cc, hd)
            q = _rope(rq_ref[:, sl].astype(F32), cs, RET_DK // 2).astype(BF16)
            kf = _rope(rk_ref[:, sl].astype(F32), cs, RET_DK // 2) * (RET_DK ** -0.5)
            k = kf.astype(BF16)
            v = rv_ref[:, sl]
            r = r_s[hd]
            rprev_ref[0, hd] = r
            inner = (_dot_nt(q, k) * decay).astype(BF16)
            y = _dot(inner, v) + _dot(q, r.astype(BF16)) * xi
            r_s[hd] = r * gc + _dot_tn((kf * zeta).astype(BF16), v)
            y_ref[:, sl] = y
            mu = jnp.mean(y, axis=-1, keepdims=True)
            yc = y - mu
            var = jnp.mean(yc * yc, axis=-1, keepdims=True)
            yn_ref[:, sl] = (yc * lax.rsqrt(var + GN_EPS)).astype(BF16)

    def blk(j):
        return pl.BlockSpec((cc, RW), lambda i: (i, j))

    return pl.pallas_call(
        body, name=name, grid=(n,),
        in_specs=[blk(0), blk(1), blk(2), pl.BlockSpec((cc, 1), lambda i: (i, 0)),
                  pl.BlockSpec((8, LANES), lambda i: (0, 0))],
        out_specs=[blk(0), blk(0), pl.BlockSpec((1, RET_HEADS, HP, RET_DV), lambda i: (i, 0, 0, 0))],
        out_shape=[jax.ShapeDtypeStruct((T, RW), F32), jax.ShapeDtypeStruct((T, RW), BF16),
                   jax.ShapeDtypeStruct((n, RET_HEADS, HP, RET_DV), F32)],
        scratch_shapes=[pltpu.VMEM((RET_HEADS, HP, RET_DV), F32)],
        compiler_params=_params(("arbitrary",)),
    )(proj, proj, proj, pos, tab)


def _ret_bwd(dyn, y, proj, pos, tab, rprev, *, name):
    T = proj.shape[0]
    cc = min(RET_TILE, T)
    n = T // cc

    def body(dyn_ref, y_ref, rq_ref, rk_ref, rv_ref, pos_ref, tab_ref, rprev_ref,
             drq_ref, drk_ref, drv_ref, dr_s):
        @pl.when(pl.program_id(0) == 0)
        def _():
            dr_s[...] = jnp.zeros_like(dr_s)

        cs = _rope_cs(pos_ref[...], tab_ref)
        for hd in range(RET_HEADS):
            sl = slice(hd * HP, (hd + 1) * HP)
            decay, zeta, xi, gc = _ret_consts(cc, hd)
            q = _rope(rq_ref[:, sl].astype(F32), cs, RET_DK // 2).astype(BF16)
            kf = _rope(rk_ref[:, sl].astype(F32), cs, RET_DK // 2) * (RET_DK ** -0.5)
            k = kf.astype(BF16)
            v = rv_ref[:, sl]
            yv = y_ref[:, sl]
            mu = jnp.mean(yv, axis=-1, keepdims=True)
            yc = yv - mu
            rs = lax.rsqrt(jnp.mean(yc * yc, axis=-1, keepdims=True) + GN_EPS)
            yn = yc * rs
            dn = dyn_ref[:, sl]
            dy = rs * (dn - jnp.mean(dn, axis=-1, keepdims=True) - yn * jnp.mean(dn * yn, axis=-1, keepdims=True))
            dyb = dy.astype(BF16)
            dyx = (dy * xi).astype(BF16)
            dr = dr_s[hd]
            drb = dr.astype(BF16)
            inner = (_dot_nt(q, k) * decay).astype(BF16)
            da = (_dot_nt(dyb, v) * decay).astype(BF16)
            dv = _dot_tn(inner, dyb) + _dot((kf * zeta).astype(BF16), drb)
            dq = _dot(da, k) + _dot_nt(dyx, rprev_ref[0, hd].astype(BF16))
            dk = _dot_tn(da, q) + _dot_nt(v, drb) * zeta
            dr_s[hd] = dr * gc + _dot_tn(q, dyx)
            drq_ref[:, sl] = _rope(dq, cs, RET_DK // 2, inverse=True).astype(BF16)
            drk_ref[:, sl] = _rope(dk * (RET_DK ** -0.5), cs, RET_DK // 2, inverse=True).astype(BF16)
            drv_ref[:, sl] = dv.astype(BF16)

    def blk(j):
        return pl.BlockSpec((cc, RW), lambda i: (n - 1 - i, j))

    return pl.pallas_call(
        body, name=name, grid=(n,),
        in_specs=[blk(0), blk(0), blk(0), blk(1), blk(2), pl.BlockSpec((cc, 1), lambda i: (n - 1 - i, 0)),
                  pl.BlockSpec((8, LANES), lambda i: (0, 0)),
                  pl.BlockSpec((1, RET_HEADS, HP, RET_DV), lambda i: (n - 1 - i, 0, 0, 0))],
        out_specs=[blk(0), blk(0), blk(0)],
        out_shape=[jax.ShapeDtypeStruct((T, RW), BF16)] * 3,
        scratch_shapes=[pltpu.VMEM((RET_HEADS, HP, RET_DV), F32)],
        compiler_params=_params(("arbitrary",)),
    )(dyn, y, proj, proj, proj, pos, tab, rprev)


def _merge_fwd(o, yn, proj, gn_w, w_bm, w_br, w_out, h, post_w, *, name):
    T, D = h.shape
    tT = min(TOKEN_TILE, T)
    g_blk = PROJ_FIXED // D

    def body(o_ref, yn_ref, rg_ref, gm_ref, gr_ref, gnw_ref, wbm_ref, wbr_ref, wout_ref, h_ref, post_ref,
             omla_ref, oret_ref, m_ref, ho_ref):
        groups = [slice(c * (tT // FFN_CHAINS), (c + 1) * (tT // FFN_CHAINS)) for c in range(FFN_CHAINS)]
        o_mlas = [_dot(o_ref[rs, :], wbm_ref[...]) for rs in groups]
        for rs, o_mla in zip(groups, o_mlas):
            rg = rg_ref[rs, :].astype(F32)
            gated = (rg * _sigmoid(rg) * (yn_ref[rs, :].astype(F32) * gnw_ref[...])).astype(BF16)
            o_ret = _dot(gated, wbr_ref[...])
            omla_ref[rs, :] = o_mla.astype(BF16)
            oret_ref[rs, :] = o_ret.astype(BF16)
            merged = _sigmoid(gm_ref[rs, :].astype(F32)) * o_mla + _sigmoid(gr_ref[rs, :].astype(F32)) * o_ret
            m = _dot(merged.astype(BF16), wout_ref[...])
            m_ref[rs, :] = m
            ho_ref[rs, :] = h_ref[rs, :] + _rms_fwd(m, post_ref[...])

    def full(r, c):
        return pl.BlockSpec((r, c), lambda i: (0, 0))

    def rows(c, j=0):
        return pl.BlockSpec((tT, c), lambda i: (i, j))

    return pl.pallas_call(
        body, name=name, grid=(T // tT,),
        in_specs=[rows(QW), rows(RW), rows(RW, 3), rows(D, g_blk), rows(D, g_blk + 1), full(1, RW),
                  full(QW, D), full(RW, D), full(D, D), rows(D), full(1, D)],
        out_specs=[rows(D), rows(D), rows(D), rows(D)],
        out_shape=[jax.ShapeDtypeStruct((T, D), BF16), jax.ShapeDtypeStruct((T, D), BF16),
                   jax.ShapeDtypeStruct((T, D), F32), jax.ShapeDtypeStruct((T, D), F32)],
        compiler_params=_params(("parallel",)),
    )(o, yn, proj, proj, proj, gn_w, w_bm, w_br, w_out, h, post_w)


def _merge_bwd(dho, m, post_w, omla, oret, proj, yn, gn_w, o, w_out, w_bm, w_br, *, name):
    T, D = dho.shape
    tT = min(MERGE_TILE, T)
    g_blk = PROJ_FIXED // D

    nT = T // tT

    def body(dho_ref, m_ref, post_ref, omla_ref, oret_ref, rg_ref, gm_ref, gr_ref, yn_ref, gnw_ref, o_ref,
             wout_ref, wbm_ref, wbr_ref,
             dgm_ref, dgr_ref, do_ref, delta_ref, drg_ref, dyn_ref, gpost_ref, ggn_ref,
             dwout_ref, dwbm_ref, dwbr_ref, acc_out, acc_bm, acc_br):
        @pl.when(pl.program_id(0) == 0)
        def _():
            gpost_ref[...] = jnp.zeros_like(gpost_ref)
            ggn_ref[...] = jnp.zeros_like(ggn_ref)
            acc_out[...] = jnp.zeros_like(acc_out)
            acc_bm[...] = jnp.zeros_like(acc_bm)
            acc_br[...] = jnp.zeros_like(acc_br)

        dm, gp = _rms_bwd(m_ref[...], post_ref[...], dho_ref[...])
        gpost_ref[...] += gp
        dmb = dm.astype(BF16)
        dmerged = _dot_nt(dmb, wout_ref[...])
        o_mla = omla_ref[...].astype(F32)
        o_ret = oret_ref[...].astype(F32)
        sgm = _sigmoid(gm_ref[...].astype(F32))
        sgr = _sigmoid(gr_ref[...].astype(F32))
        acc_out[...] += _dot_tn((sgm * o_mla + sgr * o_ret).astype(BF16), dmb)
        dgm_ref[...] = (dmerged * o_mla * sgm * (1.0 - sgm)).astype(BF16)
        dgr_ref[...] = (dmerged * o_ret * sgr * (1.0 - sgr)).astype(BF16)
        domla = (dmerged * sgm).astype(BF16)
        acc_bm[...] += _dot_tn(o_ref[...], domla)
        do = _dot_nt(domla, wbm_ref[...])
        do_ref[...] = do.astype(BF16)
        for hd in range(MLA_HEADS):
            sl = slice(hd * HP, (hd + 1) * HP)
            d = jnp.sum(do[:, sl] * o_ref[:, sl].astype(F32), axis=-1, keepdims=True)
            delta_ref[:, sl] = jnp.broadcast_to(d, (tT, HP))
        doret = (dmerged * sgr).astype(BF16)
        dgated = _dot_nt(doret, wbr_ref[...])
        rg = rg_ref[...].astype(F32)
        sg = _sigmoid(rg)
        srg = rg * sg
        ynv = yn_ref[...].astype(F32)
        yw = ynv * gnw_ref[...]
        acc_br[...] += _dot_tn((srg * yw).astype(BF16), doret)
        drg_ref[...] = (dgated * yw * (sg * (1.0 + rg * (1.0 - sg)))).astype(BF16)
        dgs = dgated * srg
        dyn_ref[...] = dgs * gnw_ref[...]
        ggn_ref[...] += jnp.sum(dgs * ynv, axis=0, keepdims=True)

        @pl.when(pl.program_id(0) == nT - 1)
        def _():
            dwout_ref[...] = acc_out[...].astype(BF16)
            dwbm_ref[...] = acc_bm[...].astype(BF16)
            dwbr_ref[...] = acc_br[...].astype(BF16)

    def full(r, c):
        return pl.BlockSpec((r, c), lambda i: (0, 0), pipeline_mode=pl.Buffered(1))

    def rows(c, j=0):
        return pl.BlockSpec((tT, c), lambda i: (i, j))

    return pl.pallas_call(
        body, name=name, grid=(nT,),
        in_specs=[rows(D), rows(D), full(1, D), rows(D), rows(D), rows(RW, 3), rows(D, g_blk), rows(D, g_blk + 1),
                  rows(RW), full(1, RW), rows(QW), full(D, D), full(QW, D), full(RW, D)],
        out_specs=[rows(D), rows(D), rows(QW), rows(QW), rows(RW), rows(RW), full(1, D), full(1, RW),
                   full(D, D), full(QW, D), full(RW, D)],
        out_shape=[jax.ShapeDtypeStruct((T, D), BF16)] * 2
        + [jax.ShapeDtypeStruct((T, QW), BF16), jax.ShapeDtypeStruct((T, QW), F32),
           jax.ShapeDtypeStruct((T, RW), BF16), jax.ShapeDtypeStruct((T, RW), F32),
           jax.ShapeDtypeStruct((1, D), F32), jax.ShapeDtypeStruct((1, RW), F32),
           jax.ShapeDtypeStruct((D, D), BF16), jax.ShapeDtypeStruct((QW, D), BF16), jax.ShapeDtypeStruct((RW, D), BF16)],
        scratch_shapes=[pltpu.VMEM((D, D), F32), pltpu.VMEM((QW, D), F32), pltpu.VMEM((RW, D), F32)],
        compiler_params=_params(("arbitrary",)),
    )(dho, m, post_w, omla, oret, proj, proj, proj, yn, gn_w, o, w_out, w_bm, w_br)


def _mesh_pos():
    return lax.axis_index("x"), lax.axis_index("y"), lax.axis_index("c")


class _Gather:
    def __init__(self, shards):
        self.operands = list(shards)
        self.n = len(shards)
        self.out_shape = [jax.ShapeDtypeStruct((N_DEV,) + s.shape, s.dtype) for s in shards]
        self.scratch = [pltpu.SemaphoreType.DMA((7 * self.n,)), pltpu.SemaphoreType.DMA((7 * self.n,)),
                        pltpu.SemaphoreType.DMA((self.n,))]

    def phase(self, p, x_refs, out_refs, sems):
        send_sems, recv_sems, local_sems = sems
        x, y, c = _mesh_pos()
        me, sibling = (x, y, c), (x, y, 1 - c)
        chips = [(1 - x, y), (x, 1 - y), (1 - x, 1 - y)]

        def copy(w, k, block, to, src=None):
            slot = out_refs[w].at[4 * block[0] + 2 * block[1] + block[2]]
            return pltpu.make_async_remote_copy(
                src_ref=slot if src is None else src, dst_ref=slot,
                send_sem=send_sems.at[7 * w + k], recv_sem=recv_sems.at[7 * w + k],
                device_id=to, device_id_type=pl.DeviceIdType.MESH)

        for w in range(self.n):
            mine = pltpu.make_async_copy(x_refs[w], out_refs[w].at[4 * x + 2 * y + c], local_sems.at[w])
            first = [copy(w, 0, me, sibling, src=x_refs[w])]
            first += [copy(w, 1 + j, me, (*chip, c), src=x_refs[w]) for j, chip in enumerate(chips)]
            passed = [copy(w, 4 + j, (*chip, c), sibling) for j, chip in enumerate(chips)]
            if p == 0:
                mine.start()
                for cp in first:
                    cp.start()
            elif p == 1:
                for j, chip in enumerate(chips):
                    copy(w, 1 + j, (*chip, c), me).wait_recv()
                    passed[j].start()
            else:
                copy(w, 0, sibling, me).wait_recv()
                for j, chip in enumerate(chips):
                    copy(w, 4 + j, (*chip, 1 - c), me).wait_recv()
                for cp in first + passed:
                    cp.wait_send()
                mine.wait()


class _Scatter:
    def __init__(self, grads, whole=()):
        self.n_sliced = len(grads)
        self.operands = list(grads) + list(whole)
        self.n = len(self.operands)
        self.out_shape = [jax.ShapeDtypeStruct(g.shape, g.dtype) for g in grads]
        self.out_shape += [jax.ShapeDtypeStruct((N_DEV,) + a.shape, a.dtype) for a in whole]
        n_sem = (N_DEV - 1) * self.n
        self.scratch = [pltpu.SemaphoreType.DMA((n_sem,)), pltpu.SemaphoreType.DMA((n_sem,)),
                        pltpu.SemaphoreType.DMA((self.n,))]

    def phase(self, p, in_refs, out_refs, sems):
        if p == 1:
            return
        send_sems, recv_sems, local_sems = sems
        x, y, c = _mesh_pos()
        me = 4 * x + 2 * y + c

        def src(w, dev):
            return in_refs[w].at[dev] if w < self.n_sliced else in_refs[w]

        for w in range(self.n):
            own = None if local_sems is None else pltpu.make_async_copy(src(w, me), out_refs[w].at[me], local_sems.at[w])
            sends, recvs = [], []
            for r in range(1, N_DEV):
                px = 1 - x if r & 4 else x
                py = 1 - y if r & 2 else y
                pc = 1 - c if r & 1 else c
                peer, pidx = (px, py, pc), 4 * px + 2 * py + pc
                k = (N_DEV - 1) * w + r - 1
                sends.append(pltpu.make_async_remote_copy(
                    src_ref=src(w, pidx), dst_ref=out_refs[w].at[me], send_sem=send_sems.at[k],
                    recv_sem=recv_sems.at[k], device_id=peer, device_id_type=pl.DeviceIdType.MESH))
                recvs.append(pltpu.make_async_remote_copy(
                    src_ref=src(w, me), dst_ref=out_refs[w].at[pidx], send_sem=send_sems.at[k],
                    recv_sem=recv_sems.at[k], device_id=peer, device_id_type=pl.DeviceIdType.MESH))
            if p == 0:
                if own is not None:
                    own.start()
                for cp in sends:
                    cp.start()
            else:
                for cp in recvs:
                    cp.wait_recv()
                for cp in sends:
                    cp.wait_send()
                if own is not None:
                    own.wait()


class _SplitScatter:
    def __init__(self, ex, name):
        self.ex, self.name = ex, name

    def _specs(self):
        ex = self.ex
        hbm = pl.BlockSpec(memory_space=pltpu.HBM)
        sem = pl.BlockSpec(memory_space=pltpu.SEMAPHORE)
        effect = pltpu.CompilerParams(has_side_effects=pltpu.SideEffectType.DATAFLOW_SIDE_EFFECTING)
        buffers = [pltpu.HBM(a.shape, a.dtype) for a in ex.operands] + [pltpu.HBM(s.shape, s.dtype) for s in ex.out_shape]
        return hbm, sem, effect, buffers

    def start(self):
        ex, n = self.ex, self.ex.n
        n_sem = (N_DEV - 1) * n
        hbm, sem, effect, buffers = self._specs()
        in_hbm = lambda a: pltpu.with_memory_space_constraint(a, pltpu.HBM)

        me = 4 * lax.axis_index("x") + 2 * lax.axis_index("y") + lax.axis_index("c")
        lands = []
        for w, (a, s) in enumerate(zip(ex.operands, ex.out_shape)):
            mine = lax.dynamic_index_in_dim(a, me, 0, keepdims=True) if w < ex.n_sliced else a[None]
            lands.append(lax.dynamic_update_slice_in_dim(lax.empty(s.shape, s.dtype), mine, me, 0))

        def start_body(*refs):
            ex.phase(0, refs[:n], refs[n:2 * n], (refs[2 * n], refs[2 * n + 1], None))
            refs[-1][...] = jnp.zeros_like(refs[-1])

        self.started = pl.pallas_call(
            start_body, name=self.name + "_start",
            out_shape=[pltpu.SemaphoreType.DMA((n_sem,)), pltpu.SemaphoreType.DMA((n_sem,))] + buffers
            + [jax.ShapeDtypeStruct((8, LANES), F32)],
            in_specs=[hbm] * (2 * n), out_specs=[sem, sem] + [hbm] * (2 * n) + [pl.BlockSpec(memory_space=pltpu.VMEM)],
            input_output_aliases={i: 2 + i for i in range(2 * n)}, compiler_params=effect,
        )(*[in_hbm(a) for a in ex.operands], *[in_hbm(a) for a in lands])
        return self.started[-1]

    def wait(self, after):
        ex, n = self.ex, self.ex.n
        hbm, sem, effect, buffers = self._specs()
        anyspec = pl.BlockSpec(memory_space=pl.ANY)

        def wait_body(*refs):
            ex.phase(2, refs[:n], refs[n:2 * n], (refs[2 * n], refs[2 * n + 1], None))

        done = pl.pallas_call(
            wait_body, name=self.name + "_wait", out_shape=buffers,
            in_specs=[hbm] * (2 * n) + [sem, sem] + [anyspec] * len(after), out_specs=[hbm] * (2 * n),
            input_output_aliases={i: i for i in range(2 * n)}, compiler_params=effect,
        )(*self.started[2:2 + 2 * n], self.started[0], self.started[1], *after)
        return done[n:]


def _exchange_alone(ex, *, name):
    n = ex.n

    def body(*refs):
        for p in range(3):
            ex.phase(p, refs[:n], refs[n:2 * n], refs[2 * n:])

    anyspec = pl.BlockSpec(memory_space=pl.ANY)
    return pl.pallas_call(body, name=name, out_shape=ex.out_shape, in_specs=[anyspec] * n,
                          out_specs=[anyspec] * n, scratch_shapes=ex.scratch)(*ex.operands)


def _adam_step(w_ref, p_ref, m_ref, v_ref, g_ref, d_ref, nm_ref, nv_ref):
    g = p_ref[0].astype(F32)
    for j in range(1, N_DEV):
        g = g + p_ref[j].astype(F32)
    g_ref[...] = g
    nm = ADAM_B1 * m_ref[...] + (1.0 - ADAM_B1) * g
    nv = ADAM_B2 * v_ref[...] + (1.0 - ADAM_B2) * (g * g)
    nm_ref[...] = nm
    nv_ref[...] = nv
    m_hat = nm / (1.0 - ADAM_B1 ** ADAM_STEP)
    v_hat = nv / (1.0 - ADAM_B2 ** ADAM_STEP)
    d_ref[...] = -ADAM_LR * (m_hat / (jnp.sqrt(v_hat) + ADAM_EPS) + ADAM_WD * w_ref[...])


def _adamw_vectors(ws, parts, ms, vs, *, name):
    n = len(ws)

    def body(*refs):
        w_refs, p_refs, m_refs, v_refs = (refs[i * n:(i + 1) * n] for i in range(4))
        outs = refs[4 * n:]
        for i in range(n):
            _adam_step(w_refs[i], p_refs[i], m_refs[i], v_refs[i], *outs[4 * i:4 * i + 4])

    return pl.pallas_call(
        body, name=name,
        out_shape=[jax.ShapeDtypeStruct(w.shape, F32) for w in ws for _ in range(4)],
    )(*ws, *parts, *ms, *vs)


def _adamw(w, parts, m, v, after, *, name):
    G, R, n = w.shape
    tn = 256 if (n > 256 and n % 256 == 0) else n
    tr = R
    for t in range(16, R, 16):
        if R % t == 0 and t * tn <= 160 * 1024:
            tr = t
    if R * tn <= 160 * 1024:
        tr = R

    def body(w_ref, p_ref, m_ref, v_ref, after_ref, g_ref, d_ref, nm_ref, nv_ref):
        _adam_step(w_ref, p_ref, m_ref, v_ref, g_ref, d_ref, nm_ref, nv_ref)

    blk = pl.BlockSpec((None, tr, tn), lambda g, i, j: (g, i, j))
    return pl.pallas_call(
        body, name=name, grid=(G, R // tr, n // tn),
        in_specs=[blk, pl.BlockSpec((N_DEV, None, tr, tn), lambda g, i, j: (0, g, i, j)), blk, blk,
                  pl.BlockSpec((8, LANES), lambda g, i, j: (0, 0))],
        out_specs=[blk, blk, blk, blk],
        out_shape=[jax.ShapeDtypeStruct((G, R, n), F32)] * 4,
        compiler_params=_params(("parallel", "parallel", "parallel")),
    )(w, parts, m, v, after)


def _pad_last(a, width):
    return jnp.pad(a, [(0, 0)] * (a.ndim - 1) + [(0, width - a.shape[-1])])


def _cols_of(g):
    return g.transpose(1, 0, 2).reshape(g.shape[1], N_DEV * g.shape[2])


def _col_shards(w):
    return w.reshape(w.shape[0], N_DEV, w.shape[1] // N_DEV).transpose(1, 0, 2)


def kernel(x, positions, ffn1_pre_w, ffn1_w1, ffn1_w2, ffn1_post_w, mix_pre_w, w_in, mla_q_norm_w, mla_w_uq, mla_kv_norm_w, mla_w_ukv, ret_gn_w, w_branch_mla, w_branch_ret, w_out, mix_post_w, ffn2_pre_w, ffn2_w1, ffn2_w2, ffn2_post_w, loss_target, m_ffn1_pre_w, m_ffn1_w1, m_ffn1_w2, m_ffn1_post_w, m_mix_pre_w, m_w_in, m_mla_q_norm_w, m_mla_w_uq, m_mla_kv_norm_w, m_mla_w_ukv, m_ret_gn_w, m_w_branch_mla, m_w_branch_ret, m_w_out, m_mix_post_w, m_ffn2_pre_w, m_ffn2_w1, m_ffn2_w2, m_ffn2_post_w, v_ffn1_pre_w, v_ffn1_w1, v_ffn1_w2, v_ffn1_post_w, v_mix_pre_w, v_w_in, v_mla_q_norm_w, v_mla_w_uq, v_mla_kv_norm_w, v_mla_w_ukv, v_ret_gn_w, v_w_branch_mla, v_w_branch_ret, v_w_out, v_mix_post_w, v_ffn2_pre_w, v_ffn2_w1, v_ffn2_w2, v_ffn2_post_w):
    T, D = x.shape[1], x.shape[2]
    h0 = x[0]
    tgt = loss_target[0]
    pos = positions.reshape(T, 1).astype(F32)

    big = [("ffn1_w1", ffn1_w1, m_ffn1_w1, v_ffn1_w1), ("ffn1_w2", ffn1_w2, m_ffn1_w2, v_ffn1_w2),
           ("w_in", w_in, m_w_in, v_w_in), ("mla_w_uq", mla_w_uq, m_mla_w_uq, v_mla_w_uq),
           ("mla_w_ukv", mla_w_ukv, m_mla_w_ukv, v_mla_w_ukv),
           ("w_branch_mla", w_branch_mla, m_w_branch_mla, v_w_branch_mla),
           ("w_branch_ret", w_branch_ret, m_w_branch_ret, v_w_branch_ret),
           ("w_out", w_out, m_w_out, v_w_out),
           ("ffn2_w1", ffn2_w1, m_ffn2_w1, v_ffn2_w1), ("ffn2_w2", ffn2_w2, m_ffn2_w2, v_ffn2_w2)]
    small = [("ffn1_pre_w", ffn1_pre_w, m_ffn1_pre_w, v_ffn1_pre_w), ("ffn1_post_w", ffn1_post_w, m_ffn1_post_w, v_ffn1_post_w),
             ("mix_pre_w", mix_pre_w, m_mix_pre_w, v_mix_pre_w), ("mla_q_norm_w", mla_q_norm_w, m_mla_q_norm_w, v_mla_q_norm_w),
             ("mla_kv_norm_w", mla_kv_norm_w, m_mla_kv_norm_w, v_mla_kv_norm_w), ("ret_gn_w", ret_gn_w, m_ret_gn_w, v_ret_gn_w),
             ("mix_post_w", mix_post_w, m_mix_post_w, v_mix_post_w), ("ffn2_pre_w", ffn2_pre_w, m_ffn2_pre_w, v_ffn2_pre_w),
             ("ffn2_post_w", ffn2_post_w, m_ffn2_post_w, v_ffn2_post_w)]

    half = ffn1_w2.shape[1]
    hp = -(-half // LANES) * LANES

    def rows_view(w):
        return w[0].T

    def send_w1(w):
        return jnp.pad(rows_view(w).reshape(2, half, D), ((0, 0), (0, hp - half), (0, 0))).reshape(2 * hp, D).astype(BF16)

    def send_w2(w):
        return jnp.pad(w[0], ((0, hp - half), (0, 0))).astype(BF16)

    mixer = ["w_in", "mla_w_uq", "mla_w_ukv", "w_branch_mla", "w_branch_ret", "w_out"]
    uq_w = MLA_NOPE + MLA_ROPE
    mixer_send = [rows_view(w_in).astype(BF16), jnp.pad(rows_view(mla_w_uq), ((0, HP - uq_w), (0, 0))).astype(BF16),
                  mla_w_ukv[0].astype(BF16), w_branch_mla[0].astype(BF16), w_branch_ret[0].astype(BF16),
                  w_out[0].astype(BF16)]

    w1a, w2a = _exchange_alone(_Gather([send_w1(ffn1_w1), send_w2(ffn1_w2)]), name="gather_ffn1")
    w2a = w2a.reshape(N_DEV // 2, 2 * hp, D)
    u1, f1, h1, *got = _ffn_fwd(h0, ffn1_pre_w, w1a, w2a, ffn1_post_w, None, name="ffn1_fwd_gather_mixer",
                                exchange=_Gather(mixer_send))
    fw = dict(zip(mixer, got))

    wi = fw["w_in"].reshape(-1, D)
    cq_w, ckv_w, kr_w = wi[0:384], wi[384:640], wi[640:672]
    rq_w, rk_w = wi[672:928], wi[928:1184]
    rv_w, rg_w = wi[1184:1696], wi[1696:2208]
    gm_w, gr_w = wi[2208:2208 + D], wi[2208 + D:2208 + 2 * D]
    zer = lambda n: jnp.zeros((n, D), BF16)
    head_rows = lambda a, h: jnp.pad(a.reshape(h, -1, D), ((0, 0), (0, HP - a.shape[0] // h), (0, 0))).reshape(h * HP, D)
    w_in_p = jnp.concatenate([head_rows(rq_w, RET_HEADS), head_rows(rk_w, RET_HEADS), rv_w, rg_w,
                              cq_w, ckv_w, zer(MLA_NOPE), kr_w, zer(HP - MLA_NOPE - MLA_ROPE), zer(AW - 768),
                              gm_w, gr_w], axis=0)
    w_uq_p = fw["mla_w_uq"].reshape(QW, MLA_Q_RANK)
    ukv = fw["mla_w_ukv"].transpose(1, 0, 2)
    w_kv_p = jnp.concatenate([_pad_last(ukv[:, :, :MLA_NOPE], HP).reshape(MLA_KV_RANK, QW),
                              _pad_last(ukv[:, :, MLA_NOPE:], HP).reshape(MLA_KV_RANK, QW)], axis=1)
    w_bm_p = jnp.pad(_cols_of(fw["w_branch_mla"]).reshape(MLA_HEADS, MLA_V, D),
                     ((0, 0), (0, HP - MLA_V), (0, 0))).reshape(QW, D)
    w_br, w_o = _cols_of(fw["w_branch_ret"]), fw["w_out"].reshape(D, D)
    tab_mla = _rope_table(MLA_NOPE, MLA_ROPE // 2)
    tab_ret = _rope_table(0, RET_DK // 2)

    proj, a1 = _rms_matmul(h1, mix_pre_w, w_in_p, name="mixer_in_proj")
    q, k, v = _mla_prep_fwd(proj, pos, mla_q_norm_w, mla_kv_norm_w, w_uq_p, w_kv_p, tab_mla, name="mla_prep_fwd")
    o, lse, w1b, w2b = _flash_fwd(q, k, v, name="mla_attn_fwd_gather_ffn2",
                                  exchange=_Gather([send_w1(ffn2_w1), send_w2(ffn2_w2)]))
    w2b = w2b.reshape(N_DEV // 2, 2 * hp, D)
    ypre, yn, rprev = _ret_fwd(proj, pos, tab_ret, name="retention_fwd")
    omla, oret, m, h2 = _merge_fwd(o, yn, proj, ret_gn_w, w_bm_p, w_br, w_o, h1, mix_post_w, name="merge_fwd")
    u2, f2, _, dy, lossp = _ffn_fwd(h2, ffn2_pre_w, w1b, w2b, ffn2_post_w, tgt, name="ffn2_fwd_loss")

    def grad(x, dy, tag, after=None):
        return _matmul_tn(x if x.ndim == 3 else x[None], dy if dy.ndim == 3 else dy[None], name=tag, after=after)

    g2, du2, df2, a2, dh2, gpost2, gpre2 = _ffn_bwd(dy, f2, ffn2_post_w, h2, ffn2_pre_w, u2, w2b, w1b, name="ffn2_bwd")
    dw1b, = grad(du2.reshape(N_DEV, T, 2 * hp), a2, "ffn2_dw1")
    dw2b = grad(g2, df2, "ffn2_dw2")[0].reshape(N_DEV, hp, D)
    (dgm, dgr, do, delta, drg, dyn, gpostm, ggn, dw_out, dw_bm_p, dw_br) = _merge_bwd(
        dh2, m, mix_post_w, omla, oret, proj, yn, ret_gn_w, o, w_o, w_bm_p, w_br, name="merge_bwd")
    sc_ffn2 = _SplitScatter(_Scatter([dw1b, dw2b]), "scatter_ffn2")
    dq, dk, dv = _flash_bwd(q, k, v, do, lse, delta, name="mla_attn_bwd", after=sc_ffn2.start())
    da, gqn, gkvn, dw_uq_p, dw_kv_p = _mla_prep_bwd(dq, dk, dv, proj, pos, mla_q_norm_w, mla_kv_norm_w, w_uq_p, w_kv_p, tab_mla, name="mla_prep_bwd")
    drq, drk, drv = _ret_bwd(dyn, ypre, proj, pos, tab_ret, rprev, name="retention_bwd")
    dproj = jnp.concatenate([drq, drk, drv, drg, da, dgm, dgr], axis=1)
    dw_in_p = grad(dproj, a1, "dw_in")[0][0]

    dw_uq = dw_uq_p.reshape(MLA_HEADS, HP, MLA_Q_RANK)[:, :uq_w]
    dkp = dw_kv_p[:, :QW].reshape(MLA_KV_RANK, MLA_HEADS, HP)[:, :, :MLA_NOPE]
    dvp = dw_kv_p[:, QW:].reshape(MLA_KV_RANK, MLA_HEADS, HP)[:, :, :MLA_V]
    dw_ukv = jnp.concatenate([dkp, dvp], axis=2).transpose(1, 0, 2)
    dw_bm = dw_bm_p.reshape(MLA_HEADS, HP, D)[:, :MLA_V].reshape(MLA_HEADS * MLA_V, D)
    small_mixer_grads = [dw_uq, dw_ukv, _col_shards(dw_bm), _col_shards(dw_br), dw_out.reshape(N_DEV, D // N_DEV, D)]
    sc_small = _SplitScatter(_Scatter(small_mixer_grads), "scatter_mixer_small")
    dh1, gmixpre = _proj_bwd(dproj, w_in_p, h1, mix_pre_w, dh2, name="mixer_in_bwd", after=sc_small.start())
    unhead = lambda a, h, wd: a.reshape(h, HP, D)[:, :wd].reshape(h * wd, D)
    c0 = 4 * RW
    dw_in = jnp.concatenate([
        dw_in_p[c0:c0 + 384], dw_in_p[c0 + 384:c0 + 640], dw_in_p[c0 + 640 + MLA_NOPE:c0 + 640 + MLA_NOPE + MLA_ROPE],
        unhead(dw_in_p[0:RW], RET_HEADS, RET_DK), unhead(dw_in_p[RW:2 * RW], RET_HEADS, RET_DK),
        dw_in_p[2 * RW:3 * RW], dw_in_p[3 * RW:4 * RW],
        dw_in_p[PROJ_FIXED:PROJ_FIXED + D], dw_in_p[PROJ_FIXED + D:PROJ_FIXED + 2 * D]], axis=0).reshape(N_DEV, -1, D)
    sc_w_in = _SplitScatter(_Scatter([dw_in]), "scatter_w_in")
    g1, du1, df1, a0, dx, gpost1, gpre1 = _ffn_bwd(
        dh1, f1, ffn1_post_w, h0, ffn1_pre_w, u1, w2a, w1a, name="ffn1_bwd", after=sc_w_in.start())
    dw2a = grad(g1, df1, "ffn1_dw2")[0].reshape(N_DEV, hp, D)
    sc_dw2a = _SplitScatter(_Scatter([dw2a]), "scatter_ffn1_dw2")
    dw1a, = grad(du1.reshape(N_DEV, T, 2 * hp), a0, "ffn1_dw1", after=sc_dw2a.start())

    small_g = {"ffn1_pre_w": gpre1, "ffn1_post_w": gpost1, "mix_pre_w": gmixpre, "mla_q_norm_w": gqn,
               "mla_kv_norm_w": gkvn, "ret_gn_w": ggn, "mix_post_w": gpostm, "ffn2_pre_w": gpre2, "ffn2_post_w": gpost2}
    sc_last = _SplitScatter(_Scatter([dw1a], whole=[small_g[nm] for nm, *_ in small] + [lossp]), "scatter_ffn1_dw1")
    token = sc_last.start()
    recv_ffn2 = sc_ffn2.wait([token])
    recv_mixer = sc_w_in.wait([token]) + sc_small.wait([token])
    recv_w2a, = sc_dw2a.wait([token])
    parts = dict(zip(mixer, recv_mixer))
    parts.update(ffn1_w2=recv_w2a, ffn2_w1=recv_ffn2[0], ffn2_w2=recv_ffn2[1])
    as_is = (lambda a: a, lambda p: p[:, None], lambda a: a)
    views = {nm: as_is for nm, *_ in big}
    for nm in ("ffn1_w1", "ffn2_w1"):
        views[nm] = (lambda a: rows_view(a).reshape(2, half, D), lambda p: p.reshape(N_DEV, 2, hp, D),
                     lambda a: a.reshape(2 * half, D).T[None])
    for nm in ("w_in", "mla_w_uq"):
        views[nm] = (lambda a: rows_view(a)[None], lambda p: p[:, None], lambda a: a[0].T[None])

    def update(nm, w, m_, v_, after):
        to_view, parts_view, back = views[nm]
        return [back(a) for a in _adamw(to_view(w), parts_view(parts[nm]), to_view(m_), to_view(v_), after,
                                        name="adamw_" + nm)]

    big_out = {nm: update(nm, w, m_, v_, token) for nm, w, m_, v_ in big if nm != "ffn1_w1"}
    recv_w1a, *small_parts, loss_parts = sc_last.wait([d[0] for d in big_out.values()])
    loss = jnp.sum(loss_parts[:, ::8, 0])
    parts["ffn1_w1"] = recv_w1a
    big_out["ffn1_w1"] = update("ffn1_w1", ffn1_w1, m_ffn1_w1, v_ffn1_w1, jnp.zeros((8, LANES), F32))
    small_out = _adamw_vectors([w for _, w, _, _ in small], small_parts, [a for _, _, a, _ in small],
                               [a for _, _, _, a in small], name="adamw_replicated")

    order = ["ffn1_pre_w", "ffn1_w1", "ffn1_w2", "ffn1_post_w", "mix_pre_w", "w_in", "mla_q_norm_w", "mla_w_uq",
             "mla_kv_norm_w", "mla_w_ukv", "ret_gn_w", "w_branch_mla", "w_branch_ret", "w_out", "mix_post_w",
             "ffn2_pre_w", "ffn2_w1", "ffn2_w2", "ffn2_post_w"]
    outs = [loss, dx[None]]
    for i in range(4):
        both = {nm: big_out[nm][i] for nm in big_out}
        both.update({nm: small_out[4 * j + i] for j, (nm, *_) in enumerate(small)})
        outs += [both[nm] for nm in order]
    return tuple(outs)
```

```python
import math

import numpy as np
import jax
import jax.numpy as jnp
from jax import lax
from jax.experimental import pallas as pl
from jax.experimental.pallas import tpu as pltpu

F32, BF16 = jnp.float32, jnp.bfloat16

MLA_HEADS, MLA_NOPE, MLA_ROPE, MLA_V = 8, 64, 32, 64
MLA_Q_RANK, MLA_KV_RANK = 384, 256
RET_HEADS, RET_DK, RET_DV = 4, 64, 128
ROPE_BASE, NORM_EPS, GN_EPS = 10000.0, 1e-6, 1e-6
ADAM_LR, ADAM_B1, ADAM_B2, ADAM_EPS, ADAM_WD, ADAM_STEP = 0.001, 0.9, 0.999, 1e-08, 0.01, 10
ATTN_SCALE = 1.0 / math.sqrt(MLA_NOPE + MLA_ROPE)

N_DEV = 8
LANES = 128
HP = LANES
QW = MLA_HEADS * HP
RW = RET_HEADS * HP
AW = 1024
PROJ_FIXED = 4 * RW + AW
NEG = -1e30

TOKEN_TILE = 512
ATTN_TILE = 1024
ATTN_CHAINS = 2
FFN_CHAINS = 2
RET_TILE = 256
PROJ_TILE_CAP = 2560
GRAD_TILE_CAP = 1408
GRAD_TOKEN_TILE = 2048
ADAM_BLOCK_CAP = 192 * 1024
MERGE_TILE = 256
VMEM_LIMIT = 56 * 1024 * 1024


def _tile(n, cap, mult=LANES):
    if n <= cap:
        return n
    best = None
    for t in range(mult, cap + 1, mult):
        if n % t == 0:
            best = t
    assert best is not None, (n, cap, mult)
    return best


def _params(sem):
    return pltpu.CompilerParams(dimension_semantics=sem, vmem_limit_bytes=VMEM_LIMIT)


def _dot(a, b):
    return lax.dot_general(a, b, (((1,), (0,)), ((), ())), preferred_element_type=F32)


def _dot_nt(a, b):
    return lax.dot_general(a, b, (((1,), (1,)), ((), ())), preferred_element_type=F32)


def _dot_tn(a, b):
    return lax.dot_general(a, b, (((0,), (0,)), ((), ())), preferred_element_type=F32)


def _sigmoid(x):
    return pl.reciprocal(1.0 + jnp.exp(-x), approx=True)


def _rms_fwd(x, w):
    r = lax.rsqrt(jnp.mean(x * x, axis=-1, keepdims=True) + NORM_EPS)
    return x * r * w


def _rms_bwd(x, w, dy):
    r = lax.rsqrt(jnp.mean(x * x, axis=-1, keepdims=True) + NORM_EPS)
    xh = x * r
    g = dy * w
    dx = r * (g - xh * jnp.mean(g * xh, axis=-1, keepdims=True))
    return dx, jnp.sum(dy * xh, axis=0, keepdims=True)


def _rope_table(first, half):
    inv = (np.float32(ROPE_BASE) ** (-(np.arange(half, dtype=np.float32) / np.float32(half)))).astype(np.float32)
    tab = np.zeros((8, LANES), np.float32)
    tab[0, first:first + half] = inv
    tab[0, first + half:first + 2 * half] = inv
    tab[1, first:first + half] = -1.0
    tab[2, first + half:first + 2 * half] = 1.0
    return jnp.asarray(tab)


def _rope_cs(pos, tab_ref):
    ang = pos * tab_ref[0:1, :]
    s = jnp.sin(ang)
    return jnp.cos(ang), s * tab_ref[1:2, :], s * tab_ref[2:3, :]


def _rope(x, cs, half, inverse=False):
    c, s1, s2 = cs
    a = pltpu.roll(x, LANES - half, 1) * s1 + pltpu.roll(x, half, 1) * s2
    return x * c - a if inverse else x * c + a


def _call(body, *, name, grid, in_specs, out_specs, out_shape, scratch_shapes, args, exchange=None, after=None):
    sem = ("arbitrary",) * len(grid)
    anyspec = pl.BlockSpec(memory_space=pl.ANY)
    if exchange is None and after is not None:
        n_own = len(in_specs)

        def behind(*refs):
            body(*refs[:n_own], *refs[n_own + 1:])

        return pl.pallas_call(behind, name=name, grid=grid, in_specs=list(in_specs) + [anyspec], out_specs=out_specs,
                              out_shape=out_shape, scratch_shapes=scratch_shapes, compiler_params=_params(sem))(*args, after)
    if exchange is None:
        return pl.pallas_call(body, name=name, grid=grid, in_specs=in_specs, out_specs=out_specs,
                              out_shape=out_shape, scratch_shapes=scratch_shapes, compiler_params=_params(sem))(*args)
    n_in, n_out, e = len(in_specs), len(out_specs), exchange.n
    total = math.prod(grid)

    def carried(*refs):
        own = refs[:n_in] + refs[n_in + e:n_in + e + n_out] + refs[n_in + 2 * e + n_out:len(refs) - 3]
        ex_refs = (refs[n_in:n_in + e], refs[n_in + e + n_out:n_in + 2 * e + n_out], refs[len(refs) - 3:])
        step = pl.program_id(0)
        for d in range(1, len(grid)):
            step = step * grid[d] + pl.program_id(d)

        @pl.when(step == 0)
        def _():
            exchange.phase(0, *ex_refs)

        @pl.when(step == (3 * total) // 4)
        def _():
            exchange.phase(1, *ex_refs)

        body(*own)

        @pl.when(step == total - 1)
        def _():
            exchange.phase(2, *ex_refs)

    return pl.pallas_call(
        carried, name=name, grid=grid, in_specs=list(in_specs) + [anyspec] * e,
        out_specs=list(out_specs) + [anyspec] * e, out_shape=list(out_shape) + exchange.out_shape,
        scratch_shapes=list(scratch_shapes) + exchange.scratch, compiler_params=_params(sem),
    )(*args, *exchange.operands)


def _ffn_fwd(h, pre_w, w1, w2, post_w, target, *, name, exchange=None):
    T, D = h.shape
    nk, ck = w2.shape[0], w2.shape[1]
    tT = min(TOKEN_TILE, T)
    nT = T // tT
    with_loss = target is not None

    def body(*refs):
        if with_loss:
            (h_ref, pre_ref, w1g_ref, w1u_ref, w2_ref, post_ref, tgt_ref,
             u_ref, f_ref, ho_ref, a_s, dy_ref, loss_ref, acc) = refs
        else:
            (h_ref, pre_ref, w1g_ref, w1u_ref, w2_ref, post_ref,
             u_ref, f_ref, ho_ref, a_s, acc) = refs
        k = pl.program_id(1)

        @pl.when(k == 0)
        def _():
            a_s[...] = _rms_fwd(h_ref[...], pre_ref[...]).astype(BF16)
            acc[...] = jnp.zeros_like(acc)

        for c in range(FFN_CHAINS):
            rs = slice(c * (tT // FFN_CHAINS), (c + 1) * (tT // FFN_CHAINS))
            a = a_s[rs, :]
            ug = _dot_nt(a, w1g_ref[...])
            uu = _dot_nt(a, w1u_ref[...])
            u_ref[0, rs, :] = ug.astype(BF16)
            u_ref[1, rs, :] = uu.astype(BF16)
            acc[rs, :] += _dot((ug * _sigmoid(ug) * uu).astype(BF16), w2_ref[...])

        @pl.when(k == nk - 1)
        def _():
            f = acc[...]
            f_ref[...] = f
            ho = h_ref[...] + 0.5 * _rms_fwd(f, post_ref[...])
            ho_ref[...] = ho
            if with_loss:
                e = ho - tgt_ref[...]
                dy_ref[...] = e * (1.0 / D)
                loss_ref[...] = jnp.full(loss_ref.shape, (0.5 / D) * jnp.sum(e * e), F32)

    row = pl.BlockSpec((tT, D), lambda i, k: (i, 0))
    vec = pl.BlockSpec((1, D), lambda i, k: (0, 0))
    in_specs = [row, vec,
                pl.BlockSpec((None, ck, D), lambda i, k: (k, 0, 0)),
                pl.BlockSpec((None, ck, D), lambda i, k: (nk + k, 0, 0)),
                pl.BlockSpec((None, ck, D), lambda i, k: (k, 0, 0)),
                vec]
    out_shape = [jax.ShapeDtypeStruct((2, nk, T, ck), BF16),
                 jax.ShapeDtypeStruct((T, D), F32),
                 jax.ShapeDtypeStruct((T, D), F32),
                 jax.ShapeDtypeStruct((T, D), BF16)]
    out_specs = [pl.BlockSpec((2, None, tT, ck), lambda i, k: (0, k, i, 0)), row, row, row]
    args = [h, pre_w, w1, w1, w2, post_w]
    if with_loss:
        in_specs.append(row)
        args.append(target)
        out_shape += [jax.ShapeDtypeStruct((T, D), F32), jax.ShapeDtypeStruct((nT * 8, LANES), F32)]
        out_specs += [row, pl.BlockSpec((8, LANES), lambda i, k: (i, 0))]
    return _call(body, name=name, grid=(nT, nk), in_specs=in_specs, out_specs=out_specs, out_shape=out_shape,
                 scratch_shapes=[pltpu.VMEM((tT, D), F32)], args=args, exchange=exchange)


def _ffn_bwd(dho, f, post_w, h, pre_w, u, w2, w1, *, name, exchange=None, after=None):
    T, D = h.shape
    nk, ck = w2.shape[0], w2.shape[1]
    tT = min(TOKEN_TILE, T)
    nT = T // tT

    def body(dho_ref, f_ref, post_ref, h_ref, pre_ref, u_ref, w2_ref, w1g_ref, w1u_ref,
             g_ref, du_ref, df_s, dh_ref, gpost_ref, gpre_ref, da_acc):
        i, k = pl.program_id(0), pl.program_id(1)

        @pl.when(jnp.logical_and(i == 0, k == 0))
        def _():
            gpost_ref[...] = jnp.zeros_like(gpost_ref)
            gpre_ref[...] = jnp.zeros_like(gpre_ref)

        @pl.when(k == 0)
        def _():
            dx, dw = _rms_bwd(f_ref[...], post_ref[...], 0.5 * dho_ref[...])
            df_s[...] = dx.astype(BF16)
            gpost_ref[...] += dw
            da_acc[...] = jnp.zeros_like(da_acc)

        groups = [slice(c * (tT // FFN_CHAINS), (c + 1) * (tT // FFN_CHAINS)) for c in range(FFN_CHAINS)]
        dgs = [_dot_nt(df_s[rs, :], w2_ref[...]) for rs in groups]
        for rs, dg in zip(groups, dgs):
            ug = u_ref[0, rs, :].astype(F32)
            uu = u_ref[1, rs, :].astype(F32)
            sg = _sigmoid(ug)
            sl = ug * sg
            g_ref[rs, :] = (sl * uu).astype(BF16)
            dug = (dg * uu * (sg + sl * (1.0 - sg))).astype(BF16)
            duu = (dg * sl).astype(BF16)
            du_ref[0, rs, :] = dug
            du_ref[1, rs, :] = duu
            da_acc[rs, :] += _dot(dug, w1g_ref[...]) + _dot(duu, w1u_ref[...])

        @pl.when(k == nk - 1)
        def _():
            dx, dw = _rms_bwd(h_ref[...], pre_ref[...], da_acc[...])
            dh_ref[...] = dho_ref[...] + dx
            gpre_ref[...] += dw

    row = pl.BlockSpec((tT, D), lambda i, k: (i, 0))
    vec = pl.BlockSpec((1, D), lambda i, k: (0, 0))
    return _call(
        body, name=name, grid=(nT, nk),
        in_specs=[row, row, vec, row, vec,
                  pl.BlockSpec((2, None, tT, ck), lambda i, k: (0, k, i, 0)),
                  pl.BlockSpec((None, ck, D), lambda i, k: (k, 0, 0)),
                  pl.BlockSpec((None, ck, D), lambda i, k: (k, 0, 0)),
                  pl.BlockSpec((None, ck, D), lambda i, k: (nk + k, 0, 0))],
        out_specs=[pl.BlockSpec((None, tT, ck), lambda i, k: (k, i, 0)),
                   pl.BlockSpec((2, None, tT, ck), lambda i, k: (0, k, i, 0)),
                   row, row, vec, vec],
        out_shape=[jax.ShapeDtypeStruct((nk, T, ck), BF16),
                   jax.ShapeDtypeStruct((2, nk, T, ck), BF16),
                   jax.ShapeDtypeStruct((T, D), BF16),
                   jax.ShapeDtypeStruct((T, D), F32),
                   jax.ShapeDtypeStruct((1, D), F32),
                   jax.ShapeDtypeStruct((1, D), F32)],
        scratch_shapes=[pltpu.VMEM((tT, D), F32)],
        args=(dho, f, post_w, h, pre_w, u, w2, w1, w1), exchange=exchange, after=after)


def _matmul_tn(x, dy, *, name, exchange=None, after=None):
    Px, T, K = x.shape
    Py, _, N = dy.shape
    P = max(Px, Py)
    tT, tK, tN = min(GRAD_TOKEN_TILE, T), _tile(K, GRAD_TILE_CAP), _tile(N, GRAD_TILE_CAP)
    nt = T // tT

    def body(x_ref, dy_ref, o_ref, acc):
        t = pl.program_id(3)

        @pl.when(t == 0)
        def _():
            acc[...] = jnp.zeros_like(acc)

        acc[...] += _dot_tn(x_ref[...], dy_ref[...])

        @pl.when(t == nt - 1)
        def _():
            o_ref[...] = acc[...].astype(BF16)

    return _call(
        body, name=name, grid=(P, K // tK, N // tN, nt),
        in_specs=[pl.BlockSpec((None, tT, tK), lambda p, a, b, t: (p if Px > 1 else 0, t, a)),
                  pl.BlockSpec((None, tT, tN), lambda p, a, b, t: (p if Py > 1 else 0, t, b))],
        out_specs=[pl.BlockSpec((None, tK, tN), lambda p, a, b, t: (p, a, b))],
        out_shape=[jax.ShapeDtypeStruct((P, K, N), BF16)],
        scratch_shapes=[pltpu.VMEM((tK, tN), F32)], args=(x, dy), exchange=exchange, after=after)


def _rms_matmul(h, wn, w, *, name):
    T, D = h.shape
    N = w.shape[0]
    tT, tN = min(TOKEN_TILE, T), _tile(N, PROJ_TILE_CAP)

    def body(h_ref, wn_ref, w_ref, y_ref, a_ref):
        @pl.when(pl.program_id(1) == 0)
        def _():
            a_ref[...] = _rms_fwd(h_ref[...], wn_ref[...]).astype(BF16)

        y_ref[...] = _dot_nt(a_ref[...], w_ref[...]).astype(BF16)

    return pl.pallas_call(
        body, name=name, grid=(T // tT, N // tN),
        in_specs=[pl.BlockSpec((tT, D), lambda i, j: (i, 0)),
                  pl.BlockSpec((1, D), lambda i, j: (0, 0)),
                  pl.BlockSpec((tN, D), lambda i, j: (j, 0))],
        out_specs=[pl.BlockSpec((tT, tN), lambda i, j: (i, j)),
                   pl.BlockSpec((tT, D), lambda i, j: (i, 0))],
        out_shape=[jax.ShapeDtypeStruct((T, N), BF16), jax.ShapeDtypeStruct((T, D), BF16)],
        compiler_params=_params(("parallel", "arbitrary")),
    )(h, wn, w)


def _proj_bwd(dproj, w, h, wn, dres, *, name, exchange=None, after=None):
    T, D = h.shape
    N = w.shape[0]
    tT, tN = min(TOKEN_TILE, T), _tile(N, PROJ_TILE_CAP)
    nn = N // tN

    def body(dp_ref, w_ref, h_ref, wn_ref, dres_ref, dh_ref, gw_ref, acc):
        i, j = pl.program_id(0), pl.program_id(1)

        @pl.when(jnp.logical_and(i == 0, j == 0))
        def _():
            gw_ref[...] = jnp.zeros_like(gw_ref)

        @pl.when(j == 0)
        def _():
            acc[...] = jnp.zeros_like(acc)

        acc[...] += _dot(dp_ref[...], w_ref[...])

        @pl.when(j == nn - 1)
        def _():
            dx, dw = _rms_bwd(h_ref[...], wn_ref[...], acc[...])
            dh_ref[...] = dres_ref[...] + dx
            gw_ref[...] += dw

    row = pl.BlockSpec((tT, D), lambda i, j: (i, 0))
    vec = pl.BlockSpec((1, D), lambda i, j: (0, 0))
    return _call(
        body, name=name, grid=(T // tT, nn),
        in_specs=[pl.BlockSpec((tT, tN), lambda i, j: (i, j)),
                  pl.BlockSpec((tN, D), lambda i, j: (j, 0)), row, vec, row],
        out_specs=[row, vec],
        out_shape=[jax.ShapeDtypeStruct((T, D), F32), jax.ShapeDtypeStruct((1, D), F32)],
        scratch_shapes=[pltpu.VMEM((tT, D), F32)], args=(dproj, w, h, wn, dres), exchange=exchange, after=after)


def _mla_prep_fwd(proj, pos, qn_w, kvn_w, w_uq, w_kv, tab, *, name):
    T = proj.shape[0]
    tT = min(TOKEN_TILE, T)
    a_blk = PROJ_FIXED // AW - 1

    def body(a_ref, pos_ref, qnw_ref, kvnw_ref, wuq_ref, wkv_ref, tab_ref,
             q_ref, k_ref, v_ref):
        cq = a_ref[:, 0:MLA_Q_RANK].astype(F32)
        ckv = a_ref[:, MLA_Q_RANK:MLA_Q_RANK + MLA_KV_RANK].astype(F32)
        kr = a_ref[:, 640:768].astype(F32)
        qn = _rms_fwd(cq, qnw_ref[...]).astype(BF16)
        kvn = _rms_fwd(ckv, kvnw_ref[...]).astype(BF16)
        cs = _rope_cs(pos_ref[...], tab_ref)
        q = _dot_nt(qn, wuq_ref[...])
        kv = _dot(kvn, wkv_ref[...])
        krr = _rope(kr, cs, MLA_ROPE // 2)
        for hd in range(MLA_HEADS):
            sl = slice(hd * HP, (hd + 1) * HP)
            q_ref[:, sl] = (_rope(q[:, sl], cs, MLA_ROPE // 2) * ATTN_SCALE).astype(BF16)
            k_ref[:, sl] = (kv[:, sl] + krr).astype(BF16)
        v_ref[...] = kv[:, QW:].astype(BF16)

    def full(r, c):
        return pl.BlockSpec((r, c), lambda i: (0, 0))

    def rows(c):
        return pl.BlockSpec((tT, c), lambda i: (i, 0))

    return pl.pallas_call(
        body, name=name, grid=(T // tT,),
        in_specs=[pl.BlockSpec((tT, AW), lambda i: (i, a_blk)), rows(1),
                  full(1, MLA_Q_RANK), full(1, MLA_KV_RANK),
                  full(QW, MLA_Q_RANK), full(MLA_KV_RANK, 2 * QW), full(8, LANES)],
        out_specs=[rows(QW), rows(QW), rows(QW)],
        out_shape=[jax.ShapeDtypeStruct((T, QW), BF16)] * 3,
        compiler_params=_params(("parallel",)),
    )(proj, pos, qn_w, kvn_w, w_uq, w_kv, tab)


def _mla_prep_bwd(dq, dk, dv, proj, pos, qn_w, kvn_w, w_uq, w_kv, tab, *, name):
    T = proj.shape[0]
    tT = min(TOKEN_TILE, T)
    nT = T // tT
    a_blk = PROJ_FIXED // AW - 1

    def body(dq_ref, dk_ref, dv_ref, a_ref, pos_ref, qnw_ref, kvnw_ref, wuq_ref, wkv_ref, tab_ref,
             da_ref, gqn_ref, gkvn_ref, dwuq_ref, dwkv_ref, dql_ref, dkvl_ref, acc_uq, acc_kv):
        @pl.when(pl.program_id(0) == 0)
        def _():
            gqn_ref[...] = jnp.zeros_like(gqn_ref)
            gkvn_ref[...] = jnp.zeros_like(gkvn_ref)
            acc_uq[...] = jnp.zeros_like(acc_uq)
            acc_kv[...] = jnp.zeros_like(acc_kv)

        cs = _rope_cs(pos_ref[...], tab_ref)
        dkr = jnp.zeros((tT, HP), F32)
        for hd in range(MLA_HEADS):
            sl = slice(hd * HP, (hd + 1) * HP)
            dql_ref[:, sl] = (_rope(dq_ref[:, sl], cs, MLA_ROPE // 2, inverse=True) * ATTN_SCALE).astype(BF16)
            dkh = dk_ref[:, sl]
            dkr = dkr + dkh
            dkvl_ref[:, sl] = dkh.astype(BF16)
        dkvl_ref[:, QW:] = dv_ref[...]
        dqn = _dot(dql_ref[...], wuq_ref[...])
        dkvn = _dot_nt(dkvl_ref[...], wkv_ref[...])
        cq = a_ref[:, 0:MLA_Q_RANK].astype(F32)
        ckv = a_ref[:, MLA_Q_RANK:MLA_Q_RANK + MLA_KV_RANK].astype(F32)
        dcq, gq = _rms_bwd(cq, qnw_ref[...], dqn)
        dckv, gkv = _rms_bwd(ckv, kvnw_ref[...], dkvn)
        gqn_ref[...] += gq
        gkvn_ref[...] += gkv
        da_ref[:, 0:MLA_Q_RANK] = dcq.astype(BF16)
        da_ref[:, MLA_Q_RANK:MLA_Q_RANK + MLA_KV_RANK] = dckv.astype(BF16)
        da_ref[:, 640:768] = _rope(dkr, cs, MLA_ROPE // 2, inverse=True).astype(BF16)
        da_ref[:, 768:AW] = jnp.zeros((tT, AW - 768), BF16)
        acc_uq[...] += _dot_tn(dql_ref[...], _rms_fwd(cq, qnw_ref[...]).astype(BF16))
        acc_kv[...] += _dot_tn(_rms_fwd(ckv, kvnw_ref[...]).astype(BF16), dkvl_ref[...])

        @pl.when(pl.program_id(0) == nT - 1)
        def _():
            dwuq_ref[...] = acc_uq[...].astype(BF16)
            dwkv_ref[...] = acc_kv[...].astype(BF16)

    def full(r, c):
        return pl.BlockSpec((r, c), lambda i: (0, 0))

    def rows(c):
        return pl.BlockSpec((tT, c), lambda i: (i, 0))

    return pl.pallas_call(
        body, name=name, grid=(nT,),
        in_specs=[rows(QW), rows(QW), rows(QW), pl.BlockSpec((tT, AW), lambda i: (i, a_blk)), rows(1),
                  full(1, MLA_Q_RANK), full(1, MLA_KV_RANK),
                  full(QW, MLA_Q_RANK), full(MLA_KV_RANK, 2 * QW), full(8, LANES)],
        out_specs=[rows(AW), full(1, MLA_Q_RANK), full(1, MLA_KV_RANK),
                   full(QW, MLA_Q_RANK), full(MLA_KV_RANK, 2 * QW)],
        out_shape=[jax.ShapeDtypeStruct((T, AW), BF16),
                   jax.ShapeDtypeStruct((1, MLA_Q_RANK), F32), jax.ShapeDtypeStruct((1, MLA_KV_RANK), F32),
                   jax.ShapeDtypeStruct((QW, MLA_Q_RANK), BF16), jax.ShapeDtypeStruct((MLA_KV_RANK, 2 * QW), BF16)],
        scratch_shapes=[pltpu.VMEM((tT, QW), BF16), pltpu.VMEM((tT, 2 * QW), BF16),
                        pltpu.VMEM((QW, MLA_Q_RANK), F32), pltpu.VMEM((MLA_KV_RANK, 2 * QW), F32)],
        compiler_params=_params(("arbitrary",)),
    )(dq, dk, dv, proj, pos, qn_w, kvn_w, w_uq, w_kv, tab)


def _flash_fwd(q, k, v, *, name, exchange=None):
    T = q.shape[0]
    H = q.shape[1] // HP
    tq = min(ATTN_TILE, T)
    nq = T // tq

    sub = tq // ATTN_CHAINS

    def body(q_ref, k_ref, v_ref, o_ref, lse_ref):
        qi = pl.program_id(1)
        qs = [q_ref[c * sub:(c + 1) * sub, :] for c in range(ATTN_CHAINS)]

        def update(carry, off, masked):
            nks = [(c + 1) * sub if masked else tq for c in range(ATTN_CHAINS)]
            scores = [_dot_nt(qs[c], k_ref[pl.ds(off, nks[c]), :]) for c in range(ATTN_CHAINS)]
            out = []
            for c in range(ATTN_CHAINS):
                m_prev, l_prev, acc = carry[c]
                nk, s = nks[c], scores[c]
                vb = v_ref[pl.ds(off, nk), :]
                if masked:
                    rows = lax.broadcasted_iota(jnp.int32, (sub, nk), 0) + c * sub
                    s = jnp.where(rows >= lax.broadcasted_iota(jnp.int32, (sub, nk), 1), s, NEG)
                m_new = jnp.maximum(m_prev, jnp.max(s, axis=1, keepdims=True))
                alpha = jnp.exp(m_prev - m_new)
                p = jnp.exp(s - m_new)
                out.append((m_new, alpha * l_prev + jnp.sum(p, axis=1, keepdims=True),
                            alpha * acc + _dot(p.astype(BF16), vb)))
            return tuple(out)

        init = tuple((jnp.full((sub, 1), NEG, F32), jnp.zeros((sub, 1), F32), jnp.zeros((sub, HP), F32))
                     for _ in range(ATTN_CHAINS))
        carry = lax.fori_loop(0, qi, lambda j, cr: update(cr, pl.multiple_of(j * tq, tq), False), init)
        carry = update(carry, pl.multiple_of(qi * tq, tq), True)
        for c in range(ATTN_CHAINS):
            m_fin, l_fin, acc = carry[c]
            o_ref[c * sub:(c + 1) * sub, :] = (acc / l_fin).astype(BF16)
            lse_ref[c * sub:(c + 1) * sub, :] = jnp.broadcast_to(m_fin + jnp.log(l_fin), (sub, HP))

    qspec = pl.BlockSpec((tq, HP), lambda h, i: (i, h))
    kspec = pl.BlockSpec((T, HP), lambda h, i: (0, h))
    return _call(
        body, name=name, grid=(H, nq),
        in_specs=[qspec, kspec, kspec], out_specs=[qspec, qspec],
        out_shape=[jax.ShapeDtypeStruct((T, H * HP), BF16), jax.ShapeDtypeStruct((T, H * HP), F32)],
        scratch_shapes=[], args=(q, k, v), exchange=exchange)


def _flash_bwd(q, k, v, do, lse, delta, *, name, exchange=None, after=None):
    T = q.shape[0]
    H = q.shape[1] // HP
    tq = min(ATTN_TILE, T)
    nq = T // tq
    sub = tq // ATTN_CHAINS

    def body(k_ref, v_ref, q_ref, do_ref, lse_ref, dl_ref, dq_ref, dk_ref, dv_ref):
        ki = pl.program_id(1)

        @pl.when(ki == 0)
        def _():
            dq_ref[...] = jnp.zeros_like(dq_ref)

        def grow(a):
            return a if a.shape[0] == tq else jnp.concatenate([a, jnp.zeros((tq - a.shape[0], HP), F32)], axis=0)

        def step(carry, j, masked):
            dk_acc, dv_acc = carry
            nks = [(c + 1) * sub if masked else tq for c in range(ATTN_CHAINS)]
            rws = [pl.ds(pl.multiple_of(j * tq + c * sub, sub), sub) for c in range(ATTN_CHAINS)]
            scores = [_dot_nt(q_ref[rws[c], :], k_ref[0:nks[c], :]) for c in range(ATTN_CHAINS)]
            dps = [_dot_nt(do_ref[rws[c], :], v_ref[0:nks[c], :]) for c in range(ATTN_CHAINS)]
            for c in range(ATTN_CHAINS):
                rows, nk, s, dp = rws[c], nks[c], scores[c], dps[c]
                kb = k_ref[0:nk, :]
                qb = q_ref[rows, :]
                dob = do_ref[rows, :]
                if masked:
                    ri = lax.broadcasted_iota(jnp.int32, (sub, nk), 0) + c * sub
                    s = jnp.where(ri >= lax.broadcasted_iota(jnp.int32, (sub, nk), 1), s, NEG)
                p = jnp.exp(s - lse_ref[rows, 0:1])
                dv_acc = dv_acc + grow(_dot_tn(p.astype(BF16), dob))
                ds = (p * (dp - dl_ref[rows, 0:1])).astype(BF16)
                dk_acc = dk_acc + grow(_dot_tn(ds, qb))
                dq_ref[rows, :] += _dot(ds, kb)
            return dk_acc, dv_acc

        carry = step((jnp.zeros((tq, HP), F32), jnp.zeros((tq, HP), F32)), ki, True)
        dk_acc, dv_acc = lax.fori_loop(ki + 1, nq, lambda j, cr: step(cr, j, False), carry)
        dk_ref[...] = dk_acc
        dv_ref[...] = dv_acc.astype(BF16)

    kspec = pl.BlockSpec((tq, HP), lambda h, j: (j, h))
    full = pl.BlockSpec((T, HP), lambda h, j: (0, h))
    return _call(
        body, name=name, grid=(H, nq),
        in_specs=[kspec, kspec, full, full, full, full], out_specs=[full, kspec, kspec],
        out_shape=[jax.ShapeDtypeStruct((T, H * HP), F32), jax.ShapeDtypeStruct((T, H * HP), F32),
                   jax.ShapeDtypeStruct((T, H * HP), BF16)],
        scratch_shapes=[], args=(k, v, q, do, lse, delta), exchange=exchange, after=after)


def _ret_consts(cc, hd):
    lg = math.log(1.0 - 2.0 ** (-5.0 - hd))
    diff = (lax.broadcasted_iota(jnp.int32, (cc, cc), 0) - lax.broadcasted_iota(jnp.int32, (cc, cc), 1)).astype(F32)
    decay = jnp.where(diff >= 0, jnp.exp(jnp.maximum(diff, 0.0) * lg), 0.0)
    idx = lax.broadcasted_iota(jnp.int32, (cc, 1), 0).astype(F32)
    zeta = jnp.exp((cc - 1.0 - idx) * lg)
    xi = jnp.exp((idx + 1.0) * lg)
    return decay, zeta, xi, math.exp(cc * lg)


def _ret_fwd(proj, pos, tab, *, name):
    T = proj.shape[0]
    cc = min(RET_TILE, T)
    n = T // cc

    def body(rq_ref, rk_ref, rv_ref, pos_ref, tab_ref, y_ref, yn_ref, rprev_ref, r_s):
        @pl.when(pl.program_id(0) == 0)
        def _():
            r_s[...] = jnp.zeros_like(r_s)

        cs = _rope_cs(pos_ref[...], tab_ref)
        for hd in range(RET_HEADS):
            sl = slice(hd * HP, (hd + 1) * HP)
            decay, zeta, xi, gc = _ret_consts(cc, hd)
            q = _rope(rq_ref[:, sl].astype(F32), cs, RET_DK // 2).astype(BF16)
            kf = _rope(rk_ref[:, sl].astype(F32), cs, RET_DK // 2) * (RET_DK ** -0.5)
            k = kf.astype(BF16)
            v = rv_ref[:, sl]
            r = r_s[hd]
            rprev_ref[0, hd] = r
            inner = (_dot_nt(q, k) * decay).astype(BF16)
            y = _dot(inner, v) + _dot(q, r.astype(BF16)) * xi
            r_s[hd] = r * gc + _dot_tn((kf * zeta).astype(BF16), v)
            y_ref[:, sl] = y
            mu = jnp.mean(y, axis=-1, keepdims=True)
            yc = y - mu
            var = jnp.mean(yc * yc, axis=-1, keepdims=True)
            yn_ref[:, sl] = (yc * lax.rsqrt(var + GN_EPS)).astype(BF16)

    def blk(j):
        return pl.BlockSpec((cc, RW), lambda i: (i, j))

    return pl.pallas_call(
        body, name=name, grid=(n,),
        in_specs=[blk(0), blk(1), blk(2), pl.BlockSpec((cc, 1), lambda i: (i, 0)),
                  pl.BlockSpec((8, LANES), lambda i: (0, 0))],
        out_specs=[blk(0), blk(0), pl.BlockSpec((1, RET_HEADS, HP, RET_DV), lambda i: (i, 0, 0, 0))],
        out_shape=[jax.ShapeDtypeStruct((T, RW), F32), jax.ShapeDtypeStruct((T, RW), BF16),
                   jax.ShapeDtypeStruct((n, RET_HEADS, HP, RET_DV), F32)],
        scratch_shapes=[pltpu.VMEM((RET_HEADS, HP, RET_DV), F32)],
        compiler_params=_params(("arbitrary",)),
    )(proj, proj, proj, pos, tab)


def _ret_bwd(dyn, y, proj, pos, tab, rprev, *, name):
    T = proj.shape[0]
    cc = min(RET_TILE, T)
    n = T // cc

    def body(dyn_ref, y_ref, rq_ref, rk_ref, rv_ref, pos_ref, tab_ref, rprev_ref,
             drq_ref, drk_ref, drv_ref, dr_s):
        @pl.when(pl.program_id(0) == 0)
        def _():
            dr_s[...] = jnp.zeros_like(dr_s)

        cs = _rope_cs(pos_ref[...], tab_ref)
        for hd in range(RET_HEADS):
            sl = slice(hd * HP, (hd + 1) * HP)
            decay, zeta, xi, gc = _ret_consts(cc, hd)
            q = _rope(rq_ref[:, sl].astype(F32), cs, RET_DK // 2).astype(BF16)
            kf = _rope(rk_ref[:, sl].astype(F32), cs, RET_DK // 2) * (RET_DK ** -0.5)
            k = kf.astype(BF16)
            v = rv_ref[:, sl]
            yv = y_ref[:, sl]
            mu = jnp.mean(yv, axis=-1, keepdims=True)
            yc = yv - mu
            rs = lax.rsqrt(jnp.mean(yc * yc, axis=-1, keepdims=True) + GN_EPS)
            yn = yc * rs
            dn = dyn_ref[:, sl]
            dy = rs * (dn - jnp.mean(dn, axis=-1, keepdims=True) - yn * jnp.mean(dn * yn, axis=-1, keepdims=True))
            dyb = dy.astype(BF16)
            dyx = (dy * xi).astype(BF16)
            dr = dr_s[hd]
            drb = dr.astype(BF16)
            inner = (_dot_nt(q, k) * decay).astype(BF16)
            da = (_dot_nt(dyb, v) * decay).astype(BF16)
            dv = _dot_tn(inner, dyb) + _dot((kf * zeta).astype(BF16), drb)
            dq = _dot(da, k) + _dot_nt(dyx, rprev_ref[0, hd].astype(BF16))
            dk = _dot_tn(da, q) + _dot_nt(v, drb) * zeta
            dr_s[hd] = dr * gc + _dot_tn(q, dyx)
            drq_ref[:, sl] = _rope(dq, cs, RET_DK // 2, inverse=True).astype(BF16)
            drk_ref[:, sl] = _rope(dk * (RET_DK ** -0.5), cs, RET_DK // 2, inverse=True).astype(BF16)
            drv_ref[:, sl] = dv.astype(BF16)

    def blk(j):
        return pl.BlockSpec((cc, RW), lambda i: (n - 1 - i, j))

    return pl.pallas_call(
        body, name=name, grid=(n,),
        in_specs=[blk(0), blk(0), blk(0), blk(1), blk(2), pl.BlockSpec((cc, 1), lambda i: (n - 1 - i, 0)),
                  pl.BlockSpec((8, LANES), lambda i: (0, 0)),
                  pl.BlockSpec((1, RET_HEADS, HP, RET_DV), lambda i: (n - 1 - i, 0, 0, 0))],
        out_specs=[blk(0), blk(0), blk(0)],
        out_shape=[jax.ShapeDtypeStruct((T, RW), BF16)] * 3,
        scratch_shapes=[pltpu.VMEM((RET_HEADS, HP, RET_DV), F32)],
        compiler_params=_params(("arbitrary",)),
    )(dyn, y, proj, proj, proj, pos, tab, rprev)


def _merge_fwd(o, yn, proj, gn_w, w_bm, w_br, w_out, h, post_w, *, name):
    T, D = h.shape
    tT = min(TOKEN_TILE, T)
    g_blk = PROJ_FIXED // D

    def body(o_ref, yn_ref, rg_ref, gm_ref, gr_ref, gnw_ref, wbm_ref, wbr_ref, wout_ref, h_ref, post_ref,
             omla_ref, oret_ref, m_ref, ho_ref):
        groups = [slice(c * (tT // FFN_CHAINS), (c + 1) * (tT // FFN_CHAINS)) for c in range(FFN_CHAINS)]
        o_mlas = [_dot(o_ref[rs, :], wbm_ref[...]) for rs in groups]
        for rs, o_mla in zip(groups, o_mlas):
            rg = rg_ref[rs, :].astype(F32)
            gated = (rg * _sigmoid(rg) * (yn_ref[rs, :].astype(F32) * gnw_ref[...])).astype(BF16)
            o_ret = _dot(gated, wbr_ref[...])
            omla_ref[rs, :] = o_mla.astype(BF16)
            oret_ref[rs, :] = o_ret.astype(BF16)
            merged = _sigmoid(gm_ref[rs, :].astype(F32)) * o_mla + _sigmoid(gr_ref[rs, :].astype(F32)) * o_ret
            m = _dot(merged.astype(BF16), wout_ref[...])
            m_ref[rs, :] = m
            ho_ref[rs, :] = h_ref[rs, :] + _rms_fwd(m, post_ref[...])

    def full(r, c):
        return pl.BlockSpec((r, c), lambda i: (0, 0))

    def rows(c, j=0):
        return pl.BlockSpec((tT, c), lambda i: (i, j))

    return pl.pallas_call(
        body, name=name, grid=(T // tT,),
        in_specs=[rows(QW), rows(RW), rows(RW, 3), rows(D, g_blk), rows(D, g_blk + 1), full(1, RW),
                  full(QW, D), full(RW, D), full(D, D), rows(D), full(1, D)],
        out_specs=[rows(D), rows(D), rows(D), rows(D)],
        out_shape=[jax.ShapeDtypeStruct((T, D), BF16), jax.ShapeDtypeStruct((T, D), BF16),
                   jax.ShapeDtypeStruct((T, D), F32), jax.ShapeDtypeStruct((T, D), F32)],
        compiler_params=_params(("parallel",)),
    )(o, yn, proj, proj, proj, gn_w, w_bm, w_br, w_out, h, post_w)


def _merge_bwd(dho, m, post_w, omla, oret, proj, yn, gn_w, o, w_out, w_bm, w_br, *, name):
    T, D = dho.shape
    tT = min(MERGE_TILE, T)
    g_blk = PROJ_FIXED // D

    nT = T // tT

    def body(dho_ref, m_ref, post_ref, omla_ref, oret_ref, rg_ref, gm_ref, gr_ref, yn_ref, gnw_ref, o_ref,
             wout_ref, wbm_ref, wbr_ref,
             dgm_ref, dgr_ref, do_ref, delta_ref, drg_ref, dyn_ref, gpost_ref, ggn_ref,
             dwout_ref, dwbm_ref, dwbr_ref, acc_out, acc_bm, acc_br):
        @pl.when(pl.program_id(0) == 0)
        def _():
            gpost_ref[...] = jnp.zeros_like(gpost_ref)
            ggn_ref[...] = jnp.zeros_like(ggn_ref)
            acc_out[...] = jnp.zeros_like(acc_out)
            acc_bm[...] = jnp.zeros_like(acc_bm)
            acc_br[...] = jnp.zeros_like(acc_br)

        dm, gp = _rms_bwd(m_ref[...], post_ref[...], dho_ref[...])
        gpost_ref[...] += gp
        dmb = dm.astype(BF16)
        dmerged = _dot_nt(dmb, wout_ref[...])
        o_mla = omla_ref[...].astype(F32)
        o_ret = oret_ref[...].astype(F32)
        sgm = _sigmoid(gm_ref[...].astype(F32))
        sgr = _sigmoid(gr_ref[...].astype(F32))
        acc_out[...] += _dot_tn((sgm * o_mla + sgr * o_ret).astype(BF16), dmb)
        dgm_ref[...] = (dmerged * o_mla * sgm * (1.0 - sgm)).astype(BF16)
        dgr_ref[...] = (dmerged * o_ret * sgr * (1.0 - sgr)).astype(BF16)
        domla = (dmerged * sgm).astype(BF16)
        acc_bm[...] += _dot_tn(o_ref[...], domla)
        do = _dot_nt(domla, wbm_ref[...])
        do_ref[...] = do.astype(BF16)
        for hd in range(MLA_HEADS):
            sl = slice(hd * HP, (hd + 1) * HP)
            d = jnp.sum(do[:, sl] * o_ref[:, sl].astype(F32), axis=-1, keepdims=True)
            delta_ref[:, sl] = jnp.broadcast_to(d, (tT, HP))
        doret = (dmerged * sgr).astype(BF16)
        dgated = _dot_nt(doret, wbr_ref[...])
        rg = rg_ref[...].astype(F32)
        sg = _sigmoid(rg)
        srg = rg * sg
        ynv = yn_ref[...].astype(F32)
        yw = ynv * gnw_ref[...]
        acc_br[...] += _dot_tn((srg * yw).astype(BF16), doret)
        drg_ref[...] = (dgated * yw * (sg * (1.0 + rg * (1.0 - sg)))).astype(BF16)
        dgs = dgated * srg
        dyn_ref[...] = dgs * gnw_ref[...]
        ggn_ref[...] += jnp.sum(dgs * ynv, axis=0, keepdims=True)

        @pl.when(pl.program_id(0) == nT - 1)
        def _():
            dwout_ref[...] = acc_out[...].astype(BF16)
            dwbm_ref[...] = acc_bm[...].astype(BF16)
            dwbr_ref[...] = acc_br[...].astype(BF16)

    def full(r, c):
        return pl.BlockSpec((r, c), lambda i: (0, 0), pipeline_mode=pl.Buffered(1))

    def rows(c, j=0):
        return pl.BlockSpec((tT, c), lambda i: (i, j))

    return pl.pallas_call(
        body, name=name, grid=(nT,),
        in_specs=[rows(D), rows(D), full(1, D), rows(D), rows(D), rows(RW, 3), rows(D, g_blk), rows(D, g_blk + 1),
                  rows(RW), full(1, RW), rows(QW), full(D, D), full(QW, D), full(RW, D)],
        out_specs=[rows(D), rows(D), rows(QW), rows(QW), rows(RW), rows(RW), full(1, D), full(1, RW),
                   full(D, D), full(QW, D), full(RW, D)],
        out_shape=[jax.ShapeDtypeStruct((T, D), BF16)] * 2
        + [jax.ShapeDtypeStruct((T, QW), BF16), jax.ShapeDtypeStruct((T, QW), F32),
           jax.ShapeDtypeStruct((T, RW), BF16), jax.ShapeDtypeStruct((T, RW), F32),
           jax.ShapeDtypeStruct((1, D), F32), jax.ShapeDtypeStruct((1, RW), F32),
           jax.ShapeDtypeStruct((D, D), BF16), jax.ShapeDtypeStruct((QW, D), BF16), jax.ShapeDtypeStruct((RW, D), BF16)],
        scratch_shapes=[pltpu.VMEM((D, D), F32), pltpu.VMEM((QW, D), F32), pltpu.VMEM((RW, D), F32)],
        compiler_params=_params(("arbitrary",)),
    )(dho, m, post_w, omla, oret, proj, proj, proj, yn, gn_w, o, w_out, w_bm, w_br)


def _mesh_pos():
    return lax.axis_index("x"), lax.axis_index("y"), lax.axis_index("c")


class _Gather:
    def __init__(self, shards):
        self.operands = list(shards)
        self.n = len(shards)
        self.out_shape = [jax.ShapeDtypeStruct((N_DEV,) + s.shape, s.dtype) for s in shards]
        self.scratch = [pltpu.SemaphoreType.DMA((7 * self.n,)), pltpu.SemaphoreType.DMA((7 * self.n,)),
                        pltpu.SemaphoreType.DMA((self.n,))]

    def phase(self, p, x_refs, out_refs, sems):
        send_sems, recv_sems, local_sems = sems
        x, y, c = _mesh_pos()
        me, sibling = (x, y, c), (x, y, 1 - c)
        chips = [(1 - x, y), (x, 1 - y), (1 - x, 1 - y)]

        def copy(w, k, block, to, src=None):
            slot = out_refs[w].at[4 * block[0] + 2 * block[1] + block[2]]
            return pltpu.make_async_remote_copy(
                src_ref=slot if src is None else src, dst_ref=slot,
                send_sem=send_sems.at[7 * w + k], recv_sem=recv_sems.at[7 * w + k],
                device_id=to, device_id_type=pl.DeviceIdType.MESH)

        for w in range(self.n):
            mine = pltpu.make_async_copy(x_refs[w], out_refs[w].at[4 * x + 2 * y + c], local_sems.at[w])
            first = [copy(w, 0, me, sibling, src=x_refs[w])]
            first += [copy(w, 1 + j, me, (*chip, c), src=x_refs[w]) for j, chip in enumerate(chips)]
            passed = [copy(w, 4 + j, (*chip, c), sibling) for j, chip in enumerate(chips)]
            if p == 0:
                mine.start()
                for cp in first:
                    cp.start()
            elif p == 1:
                for j, chip in enumerate(chips):
                    copy(w, 1 + j, (*chip, c), me).wait_recv()
                    passed[j].start()
            else:
                copy(w, 0, sibling, me).wait_recv()
                for j, chip in enumerate(chips):
                    copy(w, 4 + j, (*chip, 1 - c), me).wait_recv()
                for cp in first + passed:
                    cp.wait_send()
                mine.wait()


class _Scatter:
    def __init__(self, grads, whole=()):
        self.n_sliced = len(grads)
        self.operands = list(grads) + list(whole)
        self.n = len(self.operands)
        self.out_shape = [jax.ShapeDtypeStruct(g.shape, g.dtype) for g in grads]
        self.out_shape += [jax.ShapeDtypeStruct((N_DEV,) + a.shape, a.dtype) for a in whole]
        n_sem = (N_DEV - 1) * self.n
        self.scratch = [pltpu.SemaphoreType.DMA((n_sem,)), pltpu.SemaphoreType.DMA((n_sem,)),
                        pltpu.SemaphoreType.DMA((self.n,))]

    def phase(self, p, in_refs, out_refs, sems):
        if p == 1:
            return
        send_sems, recv_sems, local_sems = sems
        x, y, c = _mesh_pos()
        me = 4 * x + 2 * y + c

        def src(w, dev):
            return in_refs[w].at[dev] if w < self.n_sliced else in_refs[w]

        for w in range(self.n):
            own = None if local_sems is None else pltpu.make_async_copy(src(w, me), out_refs[w].at[me], local_sems.at[w])
            sends, recvs = [], []
            for r in range(1, N_DEV):
                px = 1 - x if r & 4 else x
                py = 1 - y if r & 2 else y
                pc = 1 - c if r & 1 else c
                peer, pidx = (px, py, pc), 4 * px + 2 * py + pc
                k = (N_DEV - 1) * w + r - 1
                sends.append(pltpu.make_async_remote_copy(
                    src_ref=src(w, pidx), dst_ref=out_refs[w].at[me], send_sem=send_sems.at[k],
                    recv_sem=recv_sems.at[k], device_id=peer, device_id_type=pl.DeviceIdType.MESH))
                recvs.append(pltpu.make_async_remote_copy(
                    src_ref=src(w, me), dst_ref=out_refs[w].at[pidx], send_sem=send_sems.at[k],
                    recv_sem=recv_sems.at[k], device_id=peer, device_id_type=pl.DeviceIdType.MESH))
            if p == 0:
                if own is not None:
                    own.start()
                for cp in sends:
                    cp.start()
            else:
                for cp in recvs:
                    cp.wait_recv()
                for cp in sends:
                    cp.wait_send()
                if own is not None:
                    own.wait()


class _SplitScatter:
    def __init__(self, ex, name):
        self.ex, self.name = ex, name

    def _specs(self):
        ex = self.ex
        hbm = pl.BlockSpec(memory_space=pltpu.HBM)
        sem = pl.BlockSpec(memory_space=pltpu.SEMAPHORE)
        effect = pltpu.CompilerParams(has_side_effects=pltpu.SideEffectType.DATAFLOW_SIDE_EFFECTING)
        buffers = [pltpu.HBM(a.shape, a.dtype) for a in ex.operands] + [pltpu.HBM(s.shape, s.dtype) for s in ex.out_shape]
        return hbm, sem, effect, buffers

    def start(self):
        ex, n = self.ex, self.ex.n
        n_sem = (N_DEV - 1) * n
        hbm, sem, effect, buffers = self._specs()
        in_hbm = lambda a: pltpu.with_memory_space_constraint(a, pltpu.HBM)

        me = 4 * lax.axis_index("x") + 2 * lax.axis_index("y") + lax.axis_index("c")
        lands = []
        for w, (a, s) in enumerate(zip(ex.operands, ex.out_shape)):
            mine = lax.dynamic_index_in_dim(a, me, 0, keepdims=True) if w < ex.n_sliced else a[None]
            lands.append(lax.dynamic_update_slice_in_dim(lax.empty(s.shape, s.dtype), mine, me, 0))

        def start_body(*refs):
            ex.phase(0, refs[:n], refs[n:2 * n], (refs[2 * n], refs[2 * n + 1], None))
            refs[-1][...] = jnp.zeros_like(refs[-1])

        self.started = pl.pallas_call(
            start_body, name=self.name + "_start",
            out_shape=[pltpu.SemaphoreType.DMA((n_sem,)), pltpu.SemaphoreType.DMA((n_sem,))] + buffers
            + [jax.ShapeDtypeStruct((8, LANES), F32)],
            in_specs=[hbm] * (2 * n), out_specs=[sem, sem] + [hbm] * (2 * n) + [pl.BlockSpec(memory_space=pltpu.VMEM)],
            input_output_aliases={i: 2 + i for i in range(2 * n)}, compiler_params=effect,
        )(*[in_hbm(a) for a in ex.operands], *[in_hbm(a) for a in lands])
        return self.started[-1]

    def wait(self, after):
        ex, n = self.ex, self.ex.n
        hbm, sem, effect, buffers = self._specs()
        anyspec = pl.BlockSpec(memory_space=pl.ANY)

        def wait_body(*refs):
            ex.phase(2, refs[:n], refs[n:2 * n], (refs[2 * n], refs[2 * n + 1], None))

        done = pl.pallas_call(
            wait_body, name=self.name + "_wait", out_shape=buffers,
            in_specs=[hbm] * (2 * n) + [sem, sem] + [anyspec] * len(after), out_specs=[hbm] * (2 * n),
            input_output_aliases={i: i for i in range(2 * n)}, compiler_params=effect,
        )(*self.started[2:2 + 2 * n], self.started[0], self.started[1], *after)
        return done[n:]


def _exchange_alone(ex, *, name):
    n = ex.n

    def body(*refs):
        for p in range(3):
            ex.phase(p, refs[:n], refs[n:2 * n], refs[2 * n:])

    anyspec = pl.BlockSpec(memory_space=pl.ANY)
    return pl.pallas_call(body, name=name, out_shape=ex.out_shape, in_specs=[anyspec] * n,
                          out_specs=[anyspec] * n, scratch_shapes=ex.scratch)(*ex.operands)


def _adam_step(w_ref, p_ref, m_ref, v_ref, g_ref, d_ref, nm_ref, nv_ref):
    g = p_ref[0].astype(F32)
    for j in range(1, N_DEV):
        g = g + p_ref[j].astype(F32)
    g_ref[...] = g
    nm = ADAM_B1 * m_ref[...] + (1.0 - ADAM_B1) * g
    nv = ADAM_B2 * v_ref[...] + (1.0 - ADAM_B2) * (g * g)
    nm_ref[...] = nm
    nv_ref[...] = nv
    m_hat = nm / (1.0 - ADAM_B1 ** ADAM_STEP)
    v_hat = nv / (1.0 - ADAM_B2 ** ADAM_STEP)
    d_ref[...] = -ADAM_LR * (m_hat / (jnp.sqrt(v_hat) + ADAM_EPS) + ADAM_WD * w_ref[...])


def _adamw_vectors(ws, parts, ms, vs, *, name):
    n = len(ws)

    def body(*refs):
        w_refs, p_refs, m_refs, v_refs = (refs[i * n:(i + 1) * n] for i in range(4))
        outs = refs[4 * n:]
        for i in range(n):
            _adam_step(w_refs[i], p_refs[i], m_refs[i], v_refs[i], *outs[4 * i:4 * i + 4])

    return pl.pallas_call(
        body, name=name,
        out_shape=[jax.ShapeDtypeStruct(w.shape, F32) for w in ws for _ in range(4)],
    )(*ws, *parts, *ms, *vs)


def _adamw(w, parts, m, v, after, *, name):
    G, R, n = w.shape
    tn = 512 if (n > 512 and n % 512 == 0) else n
    tr = R
    for t in range(16, R, 16):
        if R % t == 0 and t * tn <= ADAM_BLOCK_CAP:
            tr = t
    if R * tn <= ADAM_BLOCK_CAP:
        tr = R

    def body(w_ref, p_ref, m_ref, v_ref, after_ref, g_ref, d_ref, nm_ref, nv_ref):
        _adam_step(w_ref, p_ref, m_ref, v_ref, g_ref, d_ref, nm_ref, nv_ref)

    blk = pl.BlockSpec((None, tr, tn), lambda g, i, j: (g, i, j))
    return pl.pallas_call(
        body, name=name, grid=(G, R // tr, n // tn),
        in_specs=[blk, pl.BlockSpec((N_DEV, None, tr, tn), lambda g, i, j: (0, g, i, j)), blk, blk,
                  pl.BlockSpec((8, LANES), lambda g, i, j: (0, 0))],
        out_specs=[blk, blk, blk, blk],
        out_shape=[jax.ShapeDtypeStruct((G, R, n), F32)] * 4,
        compiler_params=_params(("parallel", "parallel", "parallel")),
    )(w, parts, m, v, after)


def _pad_last(a, width):
    return jnp.pad(a, [(0, 0)] * (a.ndim - 1) + [(0, width - a.shape[-1])])


def _cols_of(g):
    return g.transpose(1, 0, 2).reshape(g.shape[1], N_DEV * g.shape[2])


def _col_shards(w):
    return w.reshape(w.shape[0], N_DEV, w.shape[1] // N_DEV).transpose(1, 0, 2)


def kernel(x, positions, ffn1_pre_w, ffn1_w1, ffn1_w2, ffn1_post_w, mix_pre_w, w_in, mla_q_norm_w, mla_w_uq, mla_kv_norm_w, mla_w_ukv, ret_gn_w, w_branch_mla, w_branch_ret, w_out, mix_post_w, ffn2_pre_w, ffn2_w1, ffn2_w2, ffn2_post_w, loss_target, m_ffn1_pre_w, m_ffn1_w1, m_ffn1_w2, m_ffn1_post_w, m_mix_pre_w, m_w_in, m_mla_q_norm_w, m_mla_w_uq, m_mla_kv_norm_w, m_mla_w_ukv, m_ret_gn_w, m_w_branch_mla, m_w_branch_ret, m_w_out, m_mix_post_w, m_ffn2_pre_w, m_ffn2_w1, m_ffn2_w2, m_ffn2_post_w, v_ffn1_pre_w, v_ffn1_w1, v_ffn1_w2, v_ffn1_post_w, v_mix_pre_w, v_w_in, v_mla_q_norm_w, v_mla_w_uq, v_mla_kv_norm_w, v_mla_w_ukv, v_ret_gn_w, v_w_branch_mla, v_w_branch_ret, v_w_out, v_mix_post_w, v_ffn2_pre_w, v_ffn2_w1, v_ffn2_w2, v_ffn2_post_w):
    T, D = x.shape[1], x.shape[2]
    h0 = x[0]
    tgt = loss_target[0]
    pos = positions.reshape(T, 1).astype(F32)

    big = [("ffn1_w1", ffn1_w1, m_ffn1_w1, v_ffn1_w1), ("ffn1_w2", ffn1_w2, m_ffn1_w2, v_ffn1_w2),
           ("w_in", w_in, m_w_in, v_w_in), ("mla_w_uq", mla_w_uq, m_mla_w_uq, v_mla_w_uq),
           ("mla_w_ukv", mla_w_ukv, m_mla_w_ukv, v_mla_w_ukv),
           ("w_branch_mla", w_branch_mla, m_w_branch_mla, v_w_branch_mla),
           ("w_branch_ret", w_branch_ret, m_w_branch_ret, v_w_branch_ret),
           ("w_out", w_out, m_w_out, v_w_out),
           ("ffn2_w1", ffn2_w1, m_ffn2_w1, v_ffn2_w1), ("ffn2_w2", ffn2_w2, m_ffn2_w2, v_ffn2_w2)]
    small = [("ffn1_pre_w", ffn1_pre_w, m_ffn1_pre_w, v_ffn1_pre_w), ("ffn1_post_w", ffn1_post_w, m_ffn1_post_w, v_ffn1_post_w),
             ("mix_pre_w", mix_pre_w, m_mix_pre_w, v_mix_pre_w), ("mla_q_norm_w", mla_q_norm_w, m_mla_q_norm_w, v_mla_q_norm_w),
             ("mla_kv_norm_w", mla_kv_norm_w, m_mla_kv_norm_w, v_mla_kv_norm_w), ("ret_gn_w", ret_gn_w, m_ret_gn_w, v_ret_gn_w),
             ("mix_post_w", mix_post_w, m_mix_post_w, v_mix_post_w), ("ffn2_pre_w", ffn2_pre_w, m_ffn2_pre_w, v_ffn2_pre_w),
             ("ffn2_post_w", ffn2_post_w, m_ffn2_post_w, v_ffn2_post_w)]

    half = ffn1_w2.shape[1]
    hp = -(-half // LANES) * LANES

    def rows_view(w):
        return w[0].T

    def send_w1(w):
        return jnp.pad(rows_view(w).reshape(2, half, D), ((0, 0), (0, hp - half), (0, 0))).reshape(2 * hp, D).astype(BF16)

    def send_w2(w):
        return jnp.pad(w[0], ((0, hp - half), (0, 0))).astype(BF16)

    mixer = ["w_in", "mla_w_uq", "mla_w_ukv", "w_branch_mla", "w_branch_ret", "w_out"]
    uq_w = MLA_NOPE + MLA_ROPE
    mixer_send = [rows_view(w_in).astype(BF16), jnp.pad(rows_view(mla_w_uq), ((0, HP - uq_w), (0, 0))).astype(BF16),
                  mla_w_ukv[0].astype(BF16), w_branch_mla[0].astype(BF16), w_branch_ret[0].astype(BF16),
                  w_out[0].astype(BF16)]

    w1a, w2a = _exchange_alone(_Gather([send_w1(ffn1_w1), send_w2(ffn1_w2)]), name="gather_ffn1")
    w2a = w2a.reshape(N_DEV // 2, 2 * hp, D)
    u1, f1, h1, a0, *got = _ffn_fwd(h0, ffn1_pre_w, w1a, w2a, ffn1_post_w, None, name="ffn1_fwd_gather_mixer",
                                exchange=_Gather(mixer_send))
    fw = dict(zip(mixer, got))

    wi = fw["w_in"].reshape(-1, D)
    cq_w, ckv_w, kr_w = wi[0:384], wi[384:640], wi[640:672]
    rq_w, rk_w = wi[672:928], wi[928:1184]
    rv_w, rg_w = wi[1184:1696], wi[1696:2208]
    gm_w, gr_w = wi[2208:2208 + D], wi[2208 + D:2208 + 2 * D]
    zer = lambda n: jnp.zeros((n, D), BF16)
    head_rows = lambda a, h: jnp.pad(a.reshape(h, -1, D), ((0, 0), (0, HP - a.shape[0] // h), (0, 0))).reshape(h * HP, D)
    w_in_p = jnp.concatenate([head_rows(rq_w, RET_HEADS), head_rows(rk_w, RET_HEADS), rv_w, rg_w,
                              cq_w, ckv_w, zer(MLA_NOPE), kr_w, zer(HP - MLA_NOPE - MLA_ROPE), zer(AW - 768),
                              gm_w, gr_w], axis=0)
    w_uq_p = fw["mla_w_uq"].reshape(QW, MLA_Q_RANK)
    ukv = fw["mla_w_ukv"].transpose(1, 0, 2)
    w_kv_p = jnp.concatenate([_pad_last(ukv[:, :, :MLA_NOPE], HP).reshape(MLA_KV_RANK, QW),
                              _pad_last(ukv[:, :, MLA_NOPE:], HP).reshape(MLA_KV_RANK, QW)], axis=1)
    w_bm_p = jnp.pad(_cols_of(fw["w_branch_mla"]).reshape(MLA_HEADS, MLA_V, D),
                     ((0, 0), (0, HP - MLA_V), (0, 0))).reshape(QW, D)
    w_br, w_o = _cols_of(fw["w_branch_ret"]), fw["w_out"].reshape(D, D)
    tab_mla = _rope_table(MLA_NOPE, MLA_ROPE // 2)
    tab_ret = _rope_table(0, RET_DK // 2)

    proj, a1 = _rms_matmul(h1, mix_pre_w, w_in_p, name="mixer_in_proj")
    q, k, v = _mla_prep_fwd(proj, pos, mla_q_norm_w, mla_kv_norm_w, w_uq_p, w_kv_p, tab_mla, name="mla_prep_fwd")
    o, lse, w1b, w2b = _flash_fwd(q, k, v, name="mla_attn_fwd_gather_ffn2",
                                  exchange=_Gather([send_w1(ffn2_w1), send_w2(ffn2_w2)]))
    w2b = w2b.reshape(N_DEV // 2, 2 * hp, D)
    ypre, yn, rprev = _ret_fwd(proj, pos, tab_ret, name="retention_fwd")
    omla, oret, m, h2 = _merge_fwd(o, yn, proj, ret_gn_w, w_bm_p, w_br, w_o, h1, mix_post_w, name="merge_fwd")
    u2, f2, _, a2, dy, lossp = _ffn_fwd(h2, ffn2_pre_w, w1b, w2b, ffn2_post_w, tgt, name="ffn2_fwd_loss")

    def grad(x, dy, tag, after=None):
        return _matmul_tn(x if x.ndim == 3 else x[None], dy if dy.ndim == 3 else dy[None], name=tag, after=after)

    g2, du2, df2, dh2, gpost2, gpre2 = _ffn_bwd(dy, f2, ffn2_post_w, h2, ffn2_pre_w, u2, w2b, w1b, name="ffn2_bwd")
    dw1b, = grad(du2.reshape(N_DEV, T, 2 * hp), a2, "ffn2_dw1")
    dw2b = grad(g2, df2, "ffn2_dw2")[0].reshape(N_DEV, hp, D)
    (dgm, dgr, do, delta, drg, dyn, gpostm, ggn, dw_out, dw_bm_p, dw_br) = _merge_bwd(
        dh2, m, mix_post_w, omla, oret, proj, yn, ret_gn_w, o, w_o, w_bm_p, w_br, name="merge_bwd")
    sc_ffn2 = _SplitScatter(_Scatter([dw1b, dw2b]), "scatter_ffn2")
    dq, dk, dv = _flash_bwd(q, k, v, do, lse, delta, name="mla_attn_bwd", after=sc_ffn2.start())
    da, gqn, gkvn, dw_uq_p, dw_kv_p = _mla_prep_bwd(dq, dk, dv, proj, pos, mla_q_norm_w, mla_kv_norm_w, w_uq_p, w_kv_p, tab_mla, name="mla_prep_bwd")
    drq, drk, drv = _ret_bwd(dyn, ypre, proj, pos, tab_ret, rprev, name="retention_bwd")
    dproj = jnp.concatenate([drq, drk, drv, drg, da, dgm, dgr], axis=1)
    dw_in_p = grad(dproj, a1, "dw_in")[0][0]

    dw_uq = dw_uq_p.reshape(MLA_HEADS, HP, MLA_Q_RANK)[:, :uq_w]
    dkp = dw_kv_p[:, :QW].reshape(MLA_KV_RANK, MLA_HEADS, HP)[:, :, :MLA_NOPE]
    dvp = dw_kv_p[:, QW:].reshape(MLA_KV_RANK, MLA_HEADS, HP)[:, :, :MLA_V]
    dw_ukv = jnp.concatenate([dkp, dvp], axis=2).transpose(1, 0, 2)
    dw_bm = dw_bm_p.reshape(MLA_HEADS, HP, D)[:, :MLA_V].reshape(MLA_HEADS * MLA_V, D)
    small_mixer_grads = [dw_uq, dw_ukv, _col_shards(dw_bm), _col_shards(dw_br), dw_out.reshape(N_DEV, D // N_DEV, D)]
    sc_small = _SplitScatter(_Scatter(small_mixer_grads), "scatter_mixer_small")
    dh1, gmixpre = _proj_bwd(dproj, w_in_p, h1, mix_pre_w, dh2, name="mixer_in_bwd", after=sc_small.start())
    unhead = lambda a, h, wd: a.reshape(h, HP, D)[:, :wd].reshape(h * wd, D)
    c0 = 4 * RW
    dw_in = jnp.concatenate([
        dw_in_p[c0:c0 + 384], dw_in_p[c0 + 384:c0 + 640], dw_in_p[c0 + 640 + MLA_NOPE:c0 + 640 + MLA_NOPE + MLA_ROPE],
        unhead(dw_in_p[0:RW], RET_HEADS, RET_DK), unhead(dw_in_p[RW:2 * RW], RET_HEADS, RET_DK),
        dw_in_p[2 * RW:3 * RW], dw_in_p[3 * RW:4 * RW],
        dw_in_p[PROJ_FIXED:PROJ_FIXED + D], dw_in_p[PROJ_FIXED + D:PROJ_FIXED + 2 * D]], axis=0).reshape(N_DEV, -1, D)
    sc_w_in = _SplitScatter(_Scatter([dw_in]), "scatter_w_in")
    g1, du1, df1, dx, gpost1, gpre1 = _ffn_bwd(
        dh1, f1, ffn1_post_w, h0, ffn1_pre_w, u1, w2a, w1a, name="ffn1_bwd", after=sc_w_in.start())
    dw2a = grad(g1, df1, "ffn1_dw2")[0].reshape(N_DEV, hp, D)
    sc_dw2a = _SplitScatter(_Scatter([dw2a]), "scatter_ffn1_dw2")
    dw1a, = grad(du1.reshape(N_DEV, T, 2 * hp), a0, "ffn1_dw1", after=sc_dw2a.start())

    small_g = {"ffn1_pre_w": gpre1, "ffn1_post_w": gpost1, "mix_pre_w": gmixpre, "mla_q_norm_w": gqn,
               "mla_kv_norm_w": gkvn, "ret_gn_w": ggn, "mix_post_w": gpostm, "ffn2_pre_w": gpre2, "ffn2_post_w": gpost2}
    sc_last = _SplitScatter(_Scatter([dw1a], whole=[small_g[nm] for nm, *_ in small] + [lossp]), "scatter_ffn1_dw1")
    token = sc_last.start()
    recv_ffn2 = sc_ffn2.wait([token])
    recv_mixer = sc_w_in.wait([token]) + sc_small.wait([token])
    recv_w2a, = sc_dw2a.wait([token])
    parts = dict(zip(mixer, recv_mixer))
    parts.update(ffn1_w2=recv_w2a, ffn2_w1=recv_ffn2[0], ffn2_w2=recv_ffn2[1])
    as_is = (lambda a: a, lambda p: p[:, None], lambda a: a)
    views = {nm: as_is for nm, *_ in big}
    for nm in ("ffn1_w1", "ffn2_w1"):
        views[nm] = (lambda a: rows_view(a).reshape(2, half, D), lambda p: p.reshape(N_DEV, 2, hp, D),
                     lambda a: a.reshape(2 * half, D).T[None])
    for nm in ("w_in", "mla_w_uq"):
        views[nm] = (lambda a: rows_view(a)[None], lambda p: p[:, None], lambda a: a[0].T[None])

    def update(nm, w, m_, v_, after):
        to_view, parts_view, back = views[nm]
        return [back(a) for a in _adamw(to_view(w), parts_view(parts[nm]), to_view(m_), to_view(v_), after,
                                        name="adamw_" + nm)]

    big_out = {nm: update(nm, w, m_, v_, token) for nm, w, m_, v_ in big if nm != "ffn1_w1"}
    recv_w1a, *small_parts, loss_parts = sc_last.wait([d[0] for d in big_out.values()])
    loss = jnp.sum(loss_parts[:, ::8, 0])
    parts["ffn1_w1"] = recv_w1a
    big_out["ffn1_w1"] = update("ffn1_w1", ffn1_w1, m_ffn1_w1, v_ffn1_w1, jnp.zeros((8, LANES), F32))
    small_out = _adamw_vectors([w for _, w, _, _ in small], small_parts, [a for _, _, a, _ in small],
                               [a for _, _, _, a in small], name="adamw_replicated")

    order = ["ffn1_pre_w", "ffn1_w1", "ffn1_w2", "ffn1_post_w", "mix_pre_w", "w_in", "mla_q_norm_w", "mla_w_uq",
             "mla_kv_norm_w", "mla_w_ukv", "ret_gn_w", "w_branch_mla", "w_branch_ret", "w_out", "mix_post_w",
             "ffn2_pre_w", "ffn2_w1", "ffn2_w2", "ffn2_post_w"]
    outs = [loss, dx[None]]
    for i in range(4):
        both = {nm: big_out[nm][i] for nm in big_out}
        both.update({nm: small_out[4 * j + i] for j, (nm, *_) in enumerate(small)})
        outs += [both[nm] for nm in order]
    return tuple(outs)
```

```python
import math

import numpy as np
import jax
import jax.numpy as jnp
from jax import lax
from jax.experimental import pallas as pl
from jax.experimental.pallas import tpu as pltpu

F32, BF16 = jnp.float32, jnp.bfloat16

MLA_HEADS, MLA_NOPE, MLA_ROPE, MLA_V = 8, 64, 32, 64
MLA_Q_RANK, MLA_KV_RANK = 384, 256
RET_HEADS, RET_DK, RET_DV = 4, 64, 128
ROPE_BASE, NORM_EPS, GN_EPS = 10000.0, 1e-6, 1e-6
ADAM_LR, ADAM_B1, ADAM_B2, ADAM_EPS, ADAM_WD, ADAM_STEP = 0.001, 0.9, 0.999, 1e-08, 0.01, 10
ATTN_SCALE = 1.0 / math.sqrt(MLA_NOPE + MLA_ROPE)

N_DEV = 8
LANES = 128
HP = LANES
QW = MLA_HEADS * HP
RW = RET_HEADS * HP
AW = 1024
PROJ_FIXED = 4 * RW + AW
NEG = -1e30

TOKEN_TILE = 512
ATTN_TILE = 1024
ATTN_CHAINS = 2
FFN_CHAINS = 2
RET_TILE = 256
PROJ_TILE_CAP = 2560
GRAD_TILE_CAP = 1408
GRAD_TOKEN_TILE = 2048
ADAM_BLOCK_CAP = 192 * 1024
MERGE_TILE = 256
VMEM_LIMIT = 56 * 1024 * 1024


def _tile(n, cap, mult=LANES):
    if n <= cap:
        return n
    best = None
    for t in range(mult, cap + 1, mult):
        if n % t == 0:
            best = t
    assert best is not None, (n, cap, mult)
    return best


def _params(sem):
    return pltpu.CompilerParams(dimension_semantics=sem, vmem_limit_bytes=VMEM_LIMIT)


def _dot(a, b):
    return lax.dot_general(a, b, (((1,), (0,)), ((), ())), preferred_element_type=F32)


def _dot_nt(a, b):
    return lax.dot_general(a, b, (((1,), (1,)), ((), ())), preferred_element_type=F32)


def _dot_tn(a, b):
    return lax.dot_general(a, b, (((0,), (0,)), ((), ())), preferred_element_type=F32)


def _sigmoid(x):
    return pl.reciprocal(1.0 + jnp.exp(-x), approx=True)


def _rms_fwd(x, w):
    r = lax.rsqrt(jnp.mean(x * x, axis=-1, keepdims=True) + NORM_EPS)
    return x * r * w


def _rms_bwd(x, w, dy):
    r = lax.rsqrt(jnp.mean(x * x, axis=-1, keepdims=True) + NORM_EPS)
    xh = x * r
    g = dy * w
    dx = r * (g - xh * jnp.mean(g * xh, axis=-1, keepdims=True))
    return dx, jnp.sum(dy * xh, axis=0, keepdims=True)


def _rope_table(first, half):
    inv = (np.float32(ROPE_BASE) ** (-(np.arange(half, dtype=np.float32) / np.float32(half)))).astype(np.float32)
    tab = np.zeros((8, LANES), np.float32)
    tab[0, first:first + half] = inv
    tab[0, first + half:first + 2 * half] = inv
    tab[1, first:first + half] = -1.0
    tab[2, first + half:first + 2 * half] = 1.0
    return jnp.asarray(tab)


def _rope_cs(pos, tab_ref):
    ang = pos * tab_ref[0:1, :]
    s = jnp.sin(ang)
    return jnp.cos(ang), s * tab_ref[1:2, :], s * tab_ref[2:3, :]


def _rope(x, cs, half, inverse=False):
    c, s1, s2 = cs
    a = pltpu.roll(x, LANES - half, 1) * s1 + pltpu.roll(x, half, 1) * s2
    return x * c - a if inverse else x * c + a


def _call(body, *, name, grid, in_specs, out_specs, out_shape, scratch_shapes, args, exchange=None, after=None):
    sem = ("arbitrary",) * len(grid)
    anyspec = pl.BlockSpec(memory_space=pl.ANY)
    if exchange is None and after is not None:
        n_own = len(in_specs)

        def behind(*refs):
            body(*refs[:n_own], *refs[n_own + 1:])

        return pl.pallas_call(behind, name=name, grid=grid, in_specs=list(in_specs) + [anyspec], out_specs=out_specs,
                              out_shape=out_shape, scratch_shapes=scratch_shapes, compiler_params=_params(sem))(*args, after)
    if exchange is None:
        return pl.pallas_call(body, name=name, grid=grid, in_specs=in_specs, out_specs=out_specs,
                              out_shape=out_shape, scratch_shapes=scratch_shapes, compiler_params=_params(sem))(*args)
    n_in, n_out, e = len(in_specs), len(out_specs), exchange.n
    total = math.prod(grid)

    def carried(*refs):
        own = refs[:n_in] + refs[n_in + e:n_in + e + n_out] + refs[n_in + 2 * e + n_out:len(refs) - 3]
        ex_refs = (refs[n_in:n_in + e], refs[n_in + e + n_out:n_in + 2 * e + n_out], refs[len(refs) - 3:])
        step = pl.program_id(0)
        for d in range(1, len(grid)):
            step = step * grid[d] + pl.program_id(d)

        @pl.when(step == 0)
        def _():
            exchange.phase(0, *ex_refs)

        @pl.when(step == (3 * total) // 4)
        def _():
            exchange.phase(1, *ex_refs)

        body(*own)

        @pl.when(step == total - 1)
        def _():
            exchange.phase(2, *ex_refs)

    return pl.pallas_call(
        carried, name=name, grid=grid, in_specs=list(in_specs) + [anyspec] * e,
        out_specs=list(out_specs) + [anyspec] * e, out_shape=list(out_shape) + exchange.out_shape,
        scratch_shapes=list(scratch_shapes) + exchange.scratch, compiler_params=_params(sem),
    )(*args, *exchange.operands)


def _ffn_fwd(h, pre_w, w1, w2, post_w, target, *, name, exchange=None):
    T, D = h.shape
    nk, ck = w2.shape[0], w2.shape[1]
    tT = min(TOKEN_TILE, T)
    nT = T // tT
    with_loss = target is not None

    def body(*refs):
        if with_loss:
            (h_ref, pre_ref, w1g_ref, w1u_ref, w2_ref, post_ref, tgt_ref,
             u_ref, f_ref, ho_ref, a_s, dy_ref, loss_ref, acc) = refs
        else:
            (h_ref, pre_ref, w1g_ref, w1u_ref, w2_ref, post_ref,
             u_ref, f_ref, ho_ref, a_s, acc) = refs
        k = pl.program_id(1)

        @pl.when(k == 0)
        def _():
            a_s[...] = _rms_fwd(h_ref[...], pre_ref[...]).astype(BF16)
            acc[...] = jnp.zeros_like(acc)

        for c in range(FFN_CHAINS):
            rs = slice(c * (tT // FFN_CHAINS), (c + 1) * (tT // FFN_CHAINS))
            a = a_s[rs, :]
            ug = _dot_nt(a, w1g_ref[...])
            uu = _dot_nt(a, w1u_ref[...])
            u_ref[0, rs, :] = ug.astype(BF16)
            u_ref[1, rs, :] = uu.astype(BF16)
            acc[rs, :] += _dot((ug * _sigmoid(ug) * uu).astype(BF16), w2_ref[...])

        @pl.when(k == nk - 1)
        def _():
            f = acc[...]
            f_ref[...] = f
            ho = h_ref[...] + 0.5 * _rms_fwd(f, post_ref[...])
            ho_ref[...] = ho
            if with_loss:
                e = ho - tgt_ref[...]
                dy_ref[...] = e * (1.0 / D)
                loss_ref[...] = jnp.full(loss_ref.shape, (0.5 / D) * jnp.sum(e * e), F32)

    row = pl.BlockSpec((tT, D), lambda i, k: (i, 0))
    vec = pl.BlockSpec((1, D), lambda i, k: (0, 0))
    in_specs = [row, vec,
                pl.BlockSpec((None, ck, D), lambda i, k: (k, 0, 0)),
                pl.BlockSpec((None, ck, D), lambda i, k: (nk + k, 0, 0)),
                pl.BlockSpec((None, ck, D), lambda i, k: (k, 0, 0)),
                vec]
    out_shape = [jax.ShapeDtypeStruct((2, nk, T, ck), BF16),
                 jax.ShapeDtypeStruct((T, D), F32),
                 jax.ShapeDtypeStruct((T, D), F32),
                 jax.ShapeDtypeStruct((T, D), BF16)]
    out_specs = [pl.BlockSpec((2, None, tT, ck), lambda i, k: (0, k, i, 0)), row, row, row]
    args = [h, pre_w, w1, w1, w2, post_w]
    if with_loss:
        in_specs.append(row)
        args.append(target)
        out_shape += [jax.ShapeDtypeStruct((T, D), F32), jax.ShapeDtypeStruct((nT * 8, LANES), F32)]
        out_specs += [row, pl.BlockSpec((8, LANES), lambda i, k: (i, 0))]
    return _call(body, name=name, grid=(nT, nk), in_specs=in_specs, out_specs=out_specs, out_shape=out_shape,
                 scratch_shapes=[pltpu.VMEM((tT, D), F32)], args=args, exchange=exchange)


def _ffn_bwd(dho, f, post_w, h, pre_w, u, w2, w1, *, name, exchange=None, after=None):
    T, D = h.shape
    nk, ck = w2.shape[0], w2.shape[1]
    tT = min(TOKEN_TILE, T)
    nT = T // tT

    def body(dho_ref, f_ref, post_ref, h_ref, pre_ref, u_ref, w2_ref, w1g_ref, w1u_ref,
             g_ref, du_ref, df_s, dh_ref, gpost_ref, gpre_ref, da_acc):
        i, k = pl.program_id(0), pl.program_id(1)

        @pl.when(jnp.logical_and(i == 0, k == 0))
        def _():
            gpost_ref[...] = jnp.zeros_like(gpost_ref)
            gpre_ref[...] = jnp.zeros_like(gpre_ref)

        @pl.when(k == 0)
        def _():
            dx, dw = _rms_bwd(f_ref[...], post_ref[...], 0.5 * dho_ref[...])
            df_s[...] = dx.astype(BF16)
            gpost_ref[...] += dw
            da_acc[...] = jnp.zeros_like(da_acc)

        groups = [slice(c * (tT // FFN_CHAINS), (c + 1) * (tT // FFN_CHAINS)) for c in range(FFN_CHAINS)]
        dgs = [_dot_nt(df_s[rs, :], w2_ref[...]) for rs in groups]
        for rs, dg in zip(groups, dgs):
            ug = u_ref[0, rs, :].astype(F32)
            uu = u_ref[1, rs, :].astype(F32)
            sg = _sigmoid(ug)
            sl = ug * sg
            g_ref[rs, :] = (sl * uu).astype(BF16)
            dug = (dg * uu * (sg + sl * (1.0 - sg))).astype(BF16)
            duu = (dg * sl).astype(BF16)
            du_ref[0, rs, :] = dug
            du_ref[1, rs, :] = duu
            da_acc[rs, :] += _dot(dug, w1g_ref[...]) + _dot(duu, w1u_ref[...])

        @pl.when(k == nk - 1)
        def _():
            dx, dw = _rms_bwd(h_ref[...], pre_ref[...], da_acc[...])
            dh_ref[...] = dho_ref[...] + dx
            gpre_ref[...] += dw

    row = pl.BlockSpec((tT, D), lambda i, k: (i, 0))
    vec = pl.BlockSpec((1, D), lambda i, k: (0, 0))
    return _call(
        body, name=name, grid=(nT, nk),
        in_specs=[row, row, vec, row, vec,
                  pl.BlockSpec((2, None, tT, ck), lambda i, k: (0, k, i, 0)),
                  pl.BlockSpec((None, ck, D), lambda i, k: (k, 0, 0)),
                  pl.BlockSpec((None, ck, D), lambda i, k: (k, 0, 0)),
                  pl.BlockSpec((None, ck, D), lambda i, k: (nk + k, 0, 0))],
        out_specs=[pl.BlockSpec((None, tT, ck), lambda i, k: (k, i, 0)),
                   pl.BlockSpec((2, None, tT, ck), lambda i, k: (0, k, i, 0)),
                   row, row, vec, vec],
        out_shape=[jax.ShapeDtypeStruct((nk, T, ck), BF16),
                   jax.ShapeDtypeStruct((2, nk, T, ck), BF16),
                   jax.ShapeDtypeStruct((T, D), BF16),
                   jax.ShapeDtypeStruct((T, D), F32),
                   jax.ShapeDtypeStruct((1, D), F32),
                   jax.ShapeDtypeStruct((1, D), F32)],
        scratch_shapes=[pltpu.VMEM((tT, D), F32)],
        args=(dho, f, post_w, h, pre_w, u, w2, w1, w1), exchange=exchange, after=after)


def _matmul_tn(x, dy, *, name, exchange=None, after=None):
    Px, T, K = x.shape
    Py, _, N = dy.shape
    P = max(Px, Py)
    tT, tK, tN = min(GRAD_TOKEN_TILE, T), _tile(K, GRAD_TILE_CAP), _tile(N, GRAD_TILE_CAP)
    nt = T // tT

    def body(x_ref, dy_ref, o_ref, acc):
        t = pl.program_id(3)

        @pl.when(t == 0)
        def _():
            acc[...] = jnp.zeros_like(acc)

        acc[...] += _dot_tn(x_ref[...], dy_ref[...])

        @pl.when(t == nt - 1)
        def _():
            o_ref[...] = acc[...].astype(BF16)

    return _call(
        body, name=name, grid=(P, K // tK, N // tN, nt),
        in_specs=[pl.BlockSpec((None, tT, tK), lambda p, a, b, t: (p if Px > 1 else 0, t, a)),
                  pl.BlockSpec((None, tT, tN), lambda p, a, b, t: (p if Py > 1 else 0, t, b))],
        out_specs=[pl.BlockSpec((None, tK, tN), lambda p, a, b, t: (p, a, b))],
        out_shape=[jax.ShapeDtypeStruct((P, K, N), BF16)],
        scratch_shapes=[pltpu.VMEM((tK, tN), F32)], args=(x, dy), exchange=exchange, after=after)


def _rms_matmul(h, wn, w, *, name):
    T, D = h.shape
    N = w.shape[0]
    tT, tN = min(TOKEN_TILE, T), _tile(N, PROJ_TILE_CAP)

    def body(h_ref, wn_ref, w_ref, y_ref, a_ref):
        @pl.when(pl.program_id(1) == 0)
        def _():
            a_ref[...] = _rms_fwd(h_ref[...], wn_ref[...]).astype(BF16)

        y_ref[...] = _dot_nt(a_ref[...], w_ref[...]).astype(BF16)

    return pl.pallas_call(
        body, name=name, grid=(T // tT, N // tN),
        in_specs=[pl.BlockSpec((tT, D), lambda i, j: (i, 0)),
                  pl.BlockSpec((1, D), lambda i, j: (0, 0)),
                  pl.BlockSpec((tN, D), lambda i, j: (j, 0))],
        out_specs=[pl.BlockSpec((tT, tN), lambda i, j: (i, j)),
                   pl.BlockSpec((tT, D), lambda i, j: (i, 0))],
        out_shape=[jax.ShapeDtypeStruct((T, N), BF16), jax.ShapeDtypeStruct((T, D), BF16)],
        compiler_params=_params(("parallel", "arbitrary")),
    )(h, wn, w)


def _proj_bwd(dproj, w, h, wn, dres, *, name, exchange=None, after=None):
    T, D = h.shape
    N = w.shape[0]
    tT, tN = min(TOKEN_TILE, T), _tile(N, PROJ_TILE_CAP)
    nn = N // tN

    def body(dp_ref, w_ref, h_ref, wn_ref, dres_ref, dh_ref, gw_ref, acc):
        i, j = pl.program_id(0), pl.program_id(1)

        @pl.when(jnp.logical_and(i == 0, j == 0))
        def _():
            gw_ref[...] = jnp.zeros_like(gw_ref)

        @pl.when(j == 0)
        def _():
            acc[...] = jnp.zeros_like(acc)

        acc[...] += _dot(dp_ref[...], w_ref[...])

        @pl.when(j == nn - 1)
        def _():
            dx, dw = _rms_bwd(h_ref[...], wn_ref[...], acc[...])
            dh_ref[...] = dres_ref[...] + dx
            gw_ref[...] += dw

    row = pl.BlockSpec((tT, D), lambda i, j: (i, 0))
    vec = pl.BlockSpec((1, D), lambda i, j: (0, 0))
    return _call(
        body, name=name, grid=(T // tT, nn),
        in_specs=[pl.BlockSpec((tT, tN), lambda i, j: (i, j)),
                  pl.BlockSpec((tN, D), lambda i, j: (j, 0)), row, vec, row],
        out_specs=[row, vec],
        out_shape=[jax.ShapeDtypeStruct((T, D), F32), jax.ShapeDtypeStruct((1, D), F32)],
        scratch_shapes=[pltpu.VMEM((tT, D), F32)], args=(dproj, w, h, wn, dres), exchange=exchange, after=after)


def _mla_prep_fwd(proj, pos, qn_w, kvn_w, w_uq, w_kv, tab, *, name):
    T = proj.shape[0]
    tT = min(TOKEN_TILE, T)
    a_blk = PROJ_FIXED // AW - 1

    def body(a_ref, pos_ref, qnw_ref, kvnw_ref, wuq_ref, wkv_ref, tab_ref,
             q_ref, k_ref, v_ref):
        cq = a_ref[:, 0:MLA_Q_RANK].astype(F32)
        ckv = a_ref[:, MLA_Q_RANK:MLA_Q_RANK + MLA_KV_RANK].astype(F32)
        kr = a_ref[:, 640:768].astype(F32)
        qn = _rms_fwd(cq, qnw_ref[...]).astype(BF16)
        kvn = _rms_fwd(ckv, kvnw_ref[...]).astype(BF16)
        cs = _rope_cs(pos_ref[...], tab_ref)
        q = _dot_nt(qn, wuq_ref[...])
        kv = _dot(kvn, wkv_ref[...])
        krr = _rope(kr, cs, MLA_ROPE // 2)
        for hd in range(MLA_HEADS):
            sl = slice(hd * HP, (hd + 1) * HP)
            q_ref[:, sl] = (_rope(q[:, sl], cs, MLA_ROPE // 2) * ATTN_SCALE).astype(BF16)
            k_ref[:, sl] = (kv[:, sl] + krr).astype(BF16)
        v_ref[...] = kv[:, QW:].astype(BF16)

    def full(r, c):
        return pl.BlockSpec((r, c), lambda i: (0, 0))

    def rows(c):
        return pl.BlockSpec((tT, c), lambda i: (i, 0))

    return pl.pallas_call(
        body, name=name, grid=(T // tT,),
        in_specs=[pl.BlockSpec((tT, AW), lambda i: (i, a_blk)), rows(1),
                  full(1, MLA_Q_RANK), full(1, MLA_KV_RANK),
                  full(QW, MLA_Q_RANK), full(MLA_KV_RANK, 2 * QW), full(8, LANES)],
        out_specs=[rows(QW), rows(QW), rows(QW)],
        out_shape=[jax.ShapeDtypeStruct((T, QW), BF16)] * 3,
        compiler_params=_params(("parallel",)),
    )(proj, pos, qn_w, kvn_w, w_uq, w_kv, tab)


def _mla_prep_bwd(dq, dk, dv, proj, pos, qn_w, kvn_w, w_uq, w_kv, tab, *, name):
    T = proj.shape[0]
    tT = min(TOKEN_TILE, T)
    nT = T // tT
    a_blk = PROJ_FIXED // AW - 1

    def body(dq_ref, dk_ref, dv_ref, a_ref, pos_ref, qnw_ref, kvnw_ref, wuq_ref, wkv_ref, tab_ref,
             da_ref, gqn_ref, gkvn_ref, dwuq_ref, dwkv_ref, dql_ref, dkvl_ref, acc_uq, acc_kv):
        @pl.when(pl.program_id(0) == 0)
        def _():
            gqn_ref[...] = jnp.zeros_like(gqn_ref)
            gkvn_ref[...] = jnp.zeros_like(gkvn_ref)
            acc_uq[...] = jnp.zeros_like(acc_uq)
            acc_kv[...] = jnp.zeros_like(acc_kv)

        cs = _rope_cs(pos_ref[...], tab_ref)
        dkr = jnp.zeros((tT, HP), F32)
        for hd in range(MLA_HEADS):
            sl = slice(hd * HP, (hd + 1) * HP)
            dql_ref[:, sl] = (_rope(dq_ref[:, sl], cs, MLA_ROPE // 2, inverse=True) * ATTN_SCALE).astype(BF16)
            dkh = dk_ref[:, sl]
            dkr = dkr + dkh
            dkvl_ref[:, sl] = dkh.astype(BF16)
        dkvl_ref[:, QW:] = dv_ref[...]
        dqn = _dot(dql_ref[...], wuq_ref[...])
        dkvn = _dot_nt(dkvl_ref[...], wkv_ref[...])
        cq = a_ref[:, 0:MLA_Q_RANK].astype(F32)
        ckv = a_ref[:, MLA_Q_RANK:MLA_Q_RANK + MLA_KV_RANK].astype(F32)
        dcq, gq = _rms_bwd(cq, qnw_ref[...], dqn)
        dckv, gkv = _rms_bwd(ckv, kvnw_ref[...], dkvn)
        gqn_ref[...] += gq
        gkvn_ref[...] += gkv
        da_ref[:, 0:MLA_Q_RANK] = dcq.astype(BF16)
        da_ref[:, MLA_Q_RANK:MLA_Q_RANK + MLA_KV_RANK] = dckv.astype(BF16)
        da_ref[:, 640:768] = _rope(dkr, cs, MLA_ROPE // 2, inverse=True).astype(BF16)
        da_ref[:, 768:AW] = jnp.zeros((tT, AW - 768), BF16)
        acc_uq[...] += _dot_tn(dql_ref[...], _rms_fwd(cq, qnw_ref[...]).astype(BF16))
        acc_kv[...] += _dot_tn(_rms_fwd(ckv, kvnw_ref[...]).astype(BF16), dkvl_ref[...])

        @pl.when(pl.program_id(0) == nT - 1)
        def _():
            dwuq_ref[...] = acc_uq[...].astype(BF16)
            dwkv_ref[...] = acc_kv[...].astype(BF16)

    def full(r, c):
        return pl.BlockSpec((r, c), lambda i: (0, 0))

    def rows(c):
        return pl.BlockSpec((tT, c), lambda i: (i, 0))

    return pl.pallas_call(
        body, name=name, grid=(nT,),
        in_specs=[rows(QW), rows(QW), rows(QW), pl.BlockSpec((tT, AW), lambda i: (i, a_blk)), rows(1),
                  full(1, MLA_Q_RANK), full(1, MLA_KV_RANK),
                  full(QW, MLA_Q_RANK), full(MLA_KV_RANK, 2 * QW), full(8, LANES)],
        out_specs=[rows(AW), full(1, MLA_Q_RANK), full(1, MLA_KV_RANK),
                   full(QW, MLA_Q_RANK), full(MLA_KV_RANK, 2 * QW)],
        out_shape=[jax.ShapeDtypeStruct((T, AW), BF16),
                   jax.ShapeDtypeStruct((1, MLA_Q_RANK), F32), jax.ShapeDtypeStruct((1, MLA_KV_RANK), F32),
                   jax.ShapeDtypeStruct((QW, MLA_Q_RANK), BF16), jax.ShapeDtypeStruct((MLA_KV_RANK, 2 * QW), BF16)],
        scratch_shapes=[pltpu.VMEM((tT, QW), BF16), pltpu.VMEM((tT, 2 * QW), BF16),
                        pltpu.VMEM((QW, MLA_Q_RANK), F32), pltpu.VMEM((MLA_KV_RANK, 2 * QW), F32)],
        compiler_params=_params(("arbitrary",)),
    )(dq, dk, dv, proj, pos, qn_w, kvn_w, w_uq, w_kv, tab)


def _flash_fwd(q, k, v, *, name, exchange=None):
    T = q.shape[0]
    H = q.shape[1] // HP
    tq = min(ATTN_TILE, T)
    nq = T // tq

    sub = tq // ATTN_CHAINS

    def body(q_ref, k_ref, v_ref, o_ref, lse_ref):
        qi = pl.program_id(1)
        qs = [q_ref[c * sub:(c + 1) * sub, :] for c in range(ATTN_CHAINS)]

        def update(carry, off, masked):
            nks = [(c + 1) * sub if masked else tq for c in range(ATTN_CHAINS)]
            scores = [_dot_nt(qs[c], k_ref[pl.ds(off, nks[c]), :]) for c in range(ATTN_CHAINS)]
            out = []
            for c in range(ATTN_CHAINS):
                m_prev, l_prev, acc = carry[c]
                nk, s = nks[c], scores[c]
                vb = v_ref[pl.ds(off, nk), :]
                if masked:
                    rows = lax.broadcasted_iota(jnp.int32, (sub, nk), 0) + c * sub
                    s = jnp.where(rows >= lax.broadcasted_iota(jnp.int32, (sub, nk), 1), s, NEG)
                m_new = jnp.maximum(m_prev, jnp.max(s, axis=1, keepdims=True))
                alpha = jnp.exp(m_prev - m_new)
                p = jnp.exp(s - m_new)
                out.append((m_new, alpha * l_prev + jnp.sum(p, axis=1, keepdims=True),
                            alpha * acc + _dot(p.astype(BF16), vb)))
            return tuple(out)

        init = tuple((jnp.full((sub, 1), NEG, F32), jnp.zeros((sub, 1), F32), jnp.zeros((sub, HP), F32))
                     for _ in range(ATTN_CHAINS))
        carry = lax.fori_loop(0, qi, lambda j, cr: update(cr, pl.multiple_of(j * tq, tq), False), init)
        carry = update(carry, pl.multiple_of(qi * tq, tq), True)
        for c in range(ATTN_CHAINS):
            m_fin, l_fin, acc = carry[c]
            o_ref[c * sub:(c + 1) * sub, :] = (acc / l_fin).astype(BF16)
            lse_ref[c * sub:(c + 1) * sub, :] = jnp.broadcast_to(m_fin + jnp.log(l_fin), (sub, HP))

    qspec = pl.BlockSpec((tq, HP), lambda h, i: (i, h))
    kspec = pl.BlockSpec((T, HP), lambda h, i: (0, h))
    return _call(
        body, name=name, grid=(H, nq),
        in_specs=[qspec, kspec, kspec], out_specs=[qspec, qspec],
        out_shape=[jax.ShapeDtypeStruct((T, H * HP), BF16), jax.ShapeDtypeStruct((T, H * HP), F32)],
        scratch_shapes=[], args=(q, k, v), exchange=exchange)


def _flash_bwd(q, k, v, do, lse, delta, *, name, exchange=None, after=None):
    T = q.shape[0]
    H = q.shape[1] // HP
    tq = min(ATTN_TILE, T)
    nq = T // tq
    sub = tq // ATTN_CHAINS

    def body(k_ref, v_ref, q_ref, do_ref, lse_ref, dl_ref, dq_ref, dk_ref, dv_ref):
        ki = pl.program_id(1)

        @pl.when(ki == 0)
        def _():
            dq_ref[...] = jnp.zeros_like(dq_ref)

        def grow(a):
            return a if a.shape[0] == tq else jnp.concatenate([a, jnp.zeros((tq - a.shape[0], HP), F32)], axis=0)

        def step(carry, j, masked):
            dk_acc, dv_acc = carry
            nks = [(c + 1) * sub if masked else tq for c in range(ATTN_CHAINS)]
            rws = [pl.ds(pl.multiple_of(j * tq + c * sub, sub), sub) for c in range(ATTN_CHAINS)]
            scores = [_dot_nt(q_ref[rws[c], :], k_ref[0:nks[c], :]) for c in range(ATTN_CHAINS)]
            dps = [_dot_nt(do_ref[rws[c], :], v_ref[0:nks[c], :]) for c in range(ATTN_CHAINS)]
            for c in range(ATTN_CHAINS):
                rows, nk, s, dp = rws[c], nks[c], scores[c], dps[c]
                kb = k_ref[0:nk, :]
                qb = q_ref[rows, :]
                dob = do_ref[rows, :]
                if masked:
                    ri = lax.broadcasted_iota(jnp.int32, (sub, nk), 0) + c * sub
                    s = jnp.where(ri >= lax.broadcasted_iota(jnp.int32, (sub, nk), 1), s, NEG)
                p = jnp.exp(s - lse_ref[rows, 0:1])
                dv_acc = dv_acc + grow(_dot_tn(p.astype(BF16), dob))
                ds = (p * (dp - dl_ref[rows, 0:1])).astype(BF16)
                dk_acc = dk_acc + grow(_dot_tn(ds, qb))
                dq_ref[rows, :] += _dot(ds, kb)
            return dk_acc, dv_acc

        carry = step((jnp.zeros((tq, HP), F32), jnp.zeros((tq, HP), F32)), ki, True)
        dk_acc, dv_acc = lax.fori_loop(ki + 1, nq, lambda j, cr: step(cr, j, False), carry)
        dk_ref[...] = dk_acc
        dv_ref[...] = dv_acc.astype(BF16)

    kspec = pl.BlockSpec((tq, HP), lambda h, j: (j, h))
    full = pl.BlockSpec((T, HP), lambda h, j: (0, h))
    return _call(
        body, name=name, grid=(H, nq),
        in_specs=[kspec, kspec, full, full, full, full], out_specs=[full, kspec, kspec],
        out_shape=[jax.ShapeDtypeStruct((T, H * HP), F32), jax.ShapeDtypeStruct((T, H * HP), F32),
                   jax.ShapeDtypeStruct((T, H * HP), BF16)],
        scratch_shapes=[], args=(k, v, q, do, lse, delta), exchange=exchange, after=after)


def _ret_consts(cc, hd):
    lg = math.log(1.0 - 2.0 ** (-5.0 - hd))
    diff = (lax.broadcasted_iota(jnp.int32, (cc, cc), 0) - lax.broadcasted_iota(jnp.int32, (cc, cc), 1)).astype(F32)
    decay = jnp.where(diff >= 0, jnp.exp(jnp.maximum(diff, 0.0) * lg), 0.0)
    idx = lax.broadcasted_iota(jnp.int32, (cc, 1), 0).astype(F32)
    zeta = jnp.exp((cc - 1.0 - idx) * lg)
    xi = jnp.exp((idx + 1.0) * lg)
    return decay, zeta, xi, math.exp(cc * lg)


def _ret_fwd(proj, pos, tab, *, name):
    T = proj.shape[0]
    cc = min(RET_TILE, T)
    n = T // cc

    def body(rq_ref, rk_ref, rv_ref, pos_ref, tab_ref, y_ref, yn_ref, rprev_ref, r_s):
        @pl.when(pl.program_id(0) == 0)
        def _():
            r_s[...] = jnp.zeros_like(r_s)

        cs = _rope_cs(pos_ref[...], tab_ref)
        for hd in range(RET_HEADS):
            sl = slice(hd * HP, (hd + 1) * HP)
            decay, zeta, xi, gc = _ret_consts(cc, hd)
            q = _rope(rq_ref[:, sl].astype(F32), cs, RET_DK // 2).astype(BF16)
            kf = _rope(rk_ref[:, sl].astype(F32), cs, RET_DK // 2) * (RET_DK ** -0.5)
            k = kf.astype(BF16)
            v = rv_ref[:, sl]
            r = r_s[hd]
            rprev_ref[0, hd] = r
            inner = (_dot_nt(q, k) * decay).astype(BF16)
            y = _dot(inner, v) + _dot(q, r.astype(BF16)) * xi
            r_s[hd] = r * gc + _dot_tn((kf * zeta).astype(BF16), v)
            y_ref[:, sl] = y
            mu = jnp.mean(y, axis=-1, keepdims=True)
            yc = y - mu
            var = jnp.mean(yc * yc, axis=-1, keepdims=True)
            yn_ref[:, sl] = (yc * lax.rsqrt(var + GN_EPS)).astype(BF16)

    def blk(j):
        return pl.BlockSpec((cc, RW), lambda i: (i, j))

    return pl.pallas_call(
        body, name=name, grid=(n,),
        in_specs=[blk(0), blk(1), blk(2), pl.BlockSpec((cc, 1), lambda i: (i, 0)),
                  pl.BlockSpec((8, LANES), lambda i: (0, 0))],
        out_specs=[blk(0), blk(0), pl.BlockSpec((1, RET_HEADS, HP, RET_DV), lambda i: (i, 0, 0, 0))],
        out_shape=[jax.ShapeDtypeStruct((T, RW), F32), jax.ShapeDtypeStruct((T, RW), BF16),
                   jax.ShapeDtypeStruct((n, RET_HEADS, HP, RET_DV), F32)],
        scratch_shapes=[pltpu.VMEM((RET_HEADS, HP, RET_DV), F32)],
        compiler_params=_params(("arbitrary",)),
    )(proj, proj, proj, pos, tab)


def _ret_bwd(dyn, y, proj, pos, tab, rprev, *, name):
    T = proj.shape[0]
    cc = min(RET_TILE, T)
    n = T // cc

    def body(dyn_ref, y_ref, rq_ref, rk_ref, rv_ref, pos_ref, tab_ref, rprev_ref,
             drq_ref, drk_ref, drv_ref, dr_s):
        @pl.when(pl.program_id(0) == 0)
        def _():
            dr_s[...] = jnp.zeros_like(dr_s)

        cs = _rope_cs(pos_ref[...], tab_ref)
        for hd in range(RET_HEADS):
            sl = slice(hd * HP, (hd + 1) * HP)
            decay, zeta, xi, gc = _ret_consts(cc, hd)
            q = _rope(rq_ref[:, sl].astype(F32), cs, RET_DK // 2).astype(BF16)
            kf = _rope(rk_ref[:, sl].astype(F32), cs, RET_DK // 2) * (RET_DK ** -0.5)
            k = kf.astype(BF16)
            v = rv_ref[:, sl]
            yv = y_ref[:, sl]
            mu = jnp.mean(yv, axis=-1, keepdims=True)
            yc = yv - mu
            rs = lax.rsqrt(jnp.mean(yc * yc, axis=-1, keepdims=True) + GN_EPS)
            yn = yc * rs
            dn = dyn_ref[:, sl]
            dy = rs * (dn - jnp.mean(dn, axis=-1, keepdims=True) - yn * jnp.mean(dn * yn, axis=-1, keepdims=True))
            dyb = dy.astype(BF16)
            dyx = (dy * xi).astype(BF16)
            dr = dr_s[hd]
            drb = dr.astype(BF16)
            inner = (_dot_nt(q, k) * decay).astype(BF16)
            da = (_dot_nt(dyb, v) * decay).astype(BF16)
            dv = _dot_tn(inner, dyb) + _dot((kf * zeta).astype(BF16), drb)
            dq = _dot(da, k) + _dot_nt(dyx, rprev_ref[0, hd].astype(BF16))
            dk = _dot_tn(da, q) + _dot_nt(v, drb) * zeta
            dr_s[hd] = dr * gc + _dot_tn(q, dyx)
            drq_ref[:, sl] = _rope(dq, cs, RET_DK // 2, inverse=True).astype(BF16)
            drk_ref[:, sl] = _rope(dk * (RET_DK ** -0.5), cs, RET_DK // 2, inverse=True).astype(BF16)
            drv_ref[:, sl] = dv.astype(BF16)

    def blk(j):
        return pl.BlockSpec((cc, RW), lambda i: (n - 1 - i, j))

    return pl.pallas_call(
        body, name=name, grid=(n,),
        in_specs=[blk(0), blk(0), blk(0), blk(1), blk(2), pl.BlockSpec((cc, 1), lambda i: (n - 1 - i, 0)),
                  pl.BlockSpec((8, LANES), lambda i: (0, 0)),
                  pl.BlockSpec((1, RET_HEADS, HP, RET_DV), lambda i: (n - 1 - i, 0, 0, 0))],
        out_specs=[blk(0), blk(0), blk(0)],
        out_shape=[jax.ShapeDtypeStruct((T, RW), BF16)] * 3,
        scratch_shapes=[pltpu.VMEM((RET_HEADS, HP, RET_DV), F32)],
        compiler_params=_params(("arbitrary",)),
    )(dyn, y, proj, proj, proj, pos, tab, rprev)


def _merge_fwd(o, yn, proj, gn_w, w_bm, w_br, w_out, h, post_w, *, name):
    T, D = h.shape
    tT = min(TOKEN_TILE, T)
    g_blk = PROJ_FIXED // D

    def body(o_ref, yn_ref, rg_ref, gm_ref, gr_ref, gnw_ref, wbm_ref, wbr_ref, wout_ref, h_ref, post_ref,
             omla_ref, oret_ref, m_ref, ho_ref):
        groups = [slice(c * (tT // FFN_CHAINS), (c + 1) * (tT // FFN_CHAINS)) for c in range(FFN_CHAINS)]
        o_mlas = [_dot(o_ref[rs, :], wbm_ref[...]) for rs in groups]
        for rs, o_mla in zip(groups, o_mlas):
            rg = rg_ref[rs, :].astype(F32)
            gated = (rg * _sigmoid(rg) * (yn_ref[rs, :].astype(F32) * gnw_ref[...])).astype(BF16)
            o_ret = _dot(gated, wbr_ref[...])
            omla_ref[rs, :] = o_mla.astype(BF16)
            oret_ref[rs, :] = o_ret.astype(BF16)
            merged = _sigmoid(gm_ref[rs, :].astype(F32)) * o_mla + _sigmoid(gr_ref[rs, :].astype(F32)) * o_ret
            m = _dot(merged.astype(BF16), wout_ref[...])
            m_ref[rs, :] = m
            ho_ref[rs, :] = h_ref[rs, :] + _rms_fwd(m, post_ref[...])

    def full(r, c):
        return pl.BlockSpec((r, c), lambda i: (0, 0))

    def rows(c, j=0):
        return pl.BlockSpec((tT, c), lambda i: (i, j))

    return pl.pallas_call(
        body, name=name, grid=(T // tT,),
        in_specs=[rows(QW), rows(RW), rows(RW, 3), rows(D, g_blk), rows(D, g_blk + 1), full(1, RW),
                  full(QW, D), full(RW, D), full(D, D), rows(D), full(1, D)],
        out_specs=[rows(D), rows(D), rows(D), rows(D)],
        out_shape=[jax.ShapeDtypeStruct((T, D), BF16), jax.ShapeDtypeStruct((T, D), BF16),
                   jax.ShapeDtypeStruct((T, D), F32), jax.ShapeDtypeStruct((T, D), F32)],
        compiler_params=_params(("parallel",)),
    )(o, yn, proj, proj, proj, gn_w, w_bm, w_br, w_out, h, post_w)


def _merge_bwd(dho, m, post_w, omla, oret, proj, yn, gn_w, o, w_out, w_bm, w_br, *, name):
    T, D = dho.shape
    tT = min(MERGE_TILE, T)
    g_blk = PROJ_FIXED // D

    nT = T // tT

    def body(dho_ref, m_ref, post_ref, omla_ref, oret_ref, rg_ref, gm_ref, gr_ref, yn_ref, gnw_ref, o_ref,
             wout_ref, wbm_ref, wbr_ref,
             dgm_ref, dgr_ref, do_ref, delta_ref, drg_ref, dyn_ref, gpost_ref, ggn_ref,
             dwout_ref, dwbm_ref, dwbr_ref, acc_out, acc_bm, acc_br):
        @pl.when(pl.program_id(0) == 0)
        def _():
            gpost_ref[...] = jnp.zeros_like(gpost_ref)
            ggn_ref[...] = jnp.zeros_like(ggn_ref)
            acc_out[...] = jnp.zeros_like(acc_out)
            acc_bm[...] = jnp.zeros_like(acc_bm)
            acc_br[...] = jnp.zeros_like(acc_br)

        dm, gp = _rms_bwd(m_ref[...], post_ref[...], dho_ref[...])
        gpost_ref[...] += gp
        dmb = dm.astype(BF16)
        dmerged = _dot_nt(dmb, wout_ref[...])
        o_mla = omla_ref[...].astype(F32)
        o_ret = oret_ref[...].astype(F32)
        sgm = _sigmoid(gm_ref[...].astype(F32))
        sgr = _sigmoid(gr_ref[...].astype(F32))
        acc_out[...] += _dot_tn((sgm * o_mla + sgr * o_ret).astype(BF16), dmb)
        dgm_ref[...] = (dmerged * o_mla * sgm * (1.0 - sgm)).astype(BF16)
        dgr_ref[...] = (dmerged * o_ret * sgr * (1.0 - sgr)).astype(BF16)
        domla = (dmerged * sgm).astype(BF16)
        acc_bm[...] += _dot_tn(o_ref[...], domla)
        do = _dot_nt(domla, wbm_ref[...])
        do_ref[...] = do.astype(BF16)
        for hd in range(MLA_HEADS):
            sl = slice(hd * HP, (hd + 1) * HP)
            d = jnp.sum(do[:, sl] * o_ref[:, sl].astype(F32), axis=-1, keepdims=True)
            delta_ref[:, sl] = jnp.broadcast_to(d, (tT, HP))
        doret = (dmerged * sgr).astype(BF16)
        dgated = _dot_nt(doret, wbr_ref[...])
        rg = rg_ref[...].astype(F32)
        sg = _sigmoid(rg)
        srg = rg * sg
        ynv = yn_ref[...].astype(F32)
        yw = ynv * gnw_ref[...]
        acc_br[...] += _dot_tn((srg * yw).astype(BF16), doret)
        drg_ref[...] = (dgated * yw * (sg * (1.0 + rg * (1.0 - sg)))).astype(BF16)
        dgs = dgated * srg
        dyn_ref[...] = dgs * gnw_ref[...]
        ggn_ref[...] += jnp.sum(dgs * ynv, axis=0, keepdims=True)

        @pl.when(pl.program_id(0) == nT - 1)
        def _():
            dwout_ref[...] = acc_out[...].astype(BF16)
            dwbm_ref[...] = acc_bm[...].astype(BF16)
            dwbr_ref[...] = acc_br[...].astype(BF16)

    def full(r, c):
        return pl.BlockSpec((r, c), lambda i: (0, 0), pipeline_mode=pl.Buffered(1))

    def rows(c, j=0):
        return pl.BlockSpec((tT, c), lambda i: (i, j))

    return pl.pallas_call(
        body, name=name, grid=(nT,),
        in_specs=[rows(D), rows(D), full(1, D), rows(D), rows(D), rows(RW, 3), rows(D, g_blk), rows(D, g_blk + 1),
                  rows(RW), full(1, RW), rows(QW), full(D, D), full(QW, D), full(RW, D)],
        out_specs=[rows(D), rows(D), rows(QW), rows(QW), rows(RW), rows(RW), full(1, D), full(1, RW),
                   full(D, D), full(QW, D), full(RW, D)],
        out_shape=[jax.ShapeDtypeStruct((T, D), BF16)] * 2
        + [jax.ShapeDtypeStruct((T, QW), BF16), jax.ShapeDtypeStruct((T, QW), F32),
           jax.ShapeDtypeStruct((T, RW), BF16), jax.ShapeDtypeStruct((T, RW), F32),
           jax.ShapeDtypeStruct((1, D), F32), jax.ShapeDtypeStruct((1, RW), F32),
           jax.ShapeDtypeStruct((D, D), BF16), jax.ShapeDtypeStruct((QW, D), BF16), jax.ShapeDtypeStruct((RW, D), BF16)],
        scratch_shapes=[pltpu.VMEM((D, D), F32), pltpu.VMEM((QW, D), F32), pltpu.VMEM((RW, D), F32)],
        compiler_params=_params(("arbitrary",)),
    )(dho, m, post_w, omla, oret, proj, proj, proj, yn, gn_w, o, w_out, w_bm, w_br)


def _mesh_pos():
    return lax.axis_index("x"), lax.axis_index("y"), lax.axis_index("c")


class _Gather:
    def __init__(self, shards):
        self.operands = list(shards)
        self.n = len(shards)
        self.out_shape = [jax.ShapeDtypeStruct((N_DEV,) + s.shape, s.dtype) for s in shards]
        self.scratch = [pltpu.SemaphoreType.DMA((7 * self.n,)), pltpu.SemaphoreType.DMA((7 * self.n,)),
                        pltpu.SemaphoreType.DMA((self.n,))]

    def phase(self, p, x_refs, out_refs, sems):
        send_sems, recv_sems, local_sems = sems
        x, y, c = _mesh_pos()
        me, sibling = (x, y, c), (x, y, 1 - c)
        chips = [(1 - x, y), (x, 1 - y), (1 - x, 1 - y)]

        def copy(w, k, block, to, src=None):
            slot = out_refs[w].at[4 * block[0] + 2 * block[1] + block[2]]
            return pltpu.make_async_remote_copy(
                src_ref=slot if src is None else src, dst_ref=slot,
                send_sem=send_sems.at[7 * w + k], recv_sem=recv_sems.at[7 * w + k],
                device_id=to, device_id_type=pl.DeviceIdType.MESH)

        for w in range(self.n):
            mine = pltpu.make_async_copy(x_refs[w], out_refs[w].at[4 * x + 2 * y + c], local_sems.at[w])
            first = [copy(w, 0, me, sibling, src=x_refs[w])]
            first += [copy(w, 1 + j, me, (*chip, c), src=x_refs[w]) for j, chip in enumerate(chips)]
            passed = [copy(w, 4 + j, (*chip, c), sibling) for j, chip in enumerate(chips)]
            if p == 0:
                mine.start()
                for cp in first:
                    cp.start()
            elif p == 1:
                for j, chip in enumerate(chips):
                    copy(w, 1 + j, (*chip, c), me).wait_recv()
                    passed[j].start()
            else:
                copy(w, 0, sibling, me).wait_recv()
                for j, chip in enumerate(chips):
                    copy(w, 4 + j, (*chip, 1 - c), me).wait_recv()
                for cp in first + passed:
                    cp.wait_send()
                mine.wait()


class _Scatter:
    def __init__(self, grads, whole=(), axes=("x", "y", "c")):
        self.axes = axes
        self.slots = 2 ** len(axes)
        self.flips = [r for r in range(1, N_DEV)
                      if not (r & 4 and "x" not in axes) and not (r & 2 and "y" not in axes) and not (r & 1 and "c" not in axes)]
        self.n_sliced = len(grads)
        self.operands = list(grads) + list(whole)
        self.n = len(self.operands)
        self.out_shape = [jax.ShapeDtypeStruct(g.shape, g.dtype) for g in grads]
        self.out_shape += [jax.ShapeDtypeStruct((self.slots,) + a.shape, a.dtype) for a in whole]
        n_sem = len(self.flips) * self.n
        self.scratch = [pltpu.SemaphoreType.DMA((n_sem,)), pltpu.SemaphoreType.DMA((n_sem,)),
                        pltpu.SemaphoreType.DMA((self.n,))]

    def _slot(self, x, y, c):
        idx = 0
        for name, coord in (("x", x), ("y", y), ("c", c)):
            if name in self.axes:
                idx = 2 * idx + coord
        return idx

    def phase(self, p, in_refs, out_refs, sems):
        if p == 1:
            return
        send_sems, recv_sems, local_sems = sems
        x, y, c = _mesh_pos()
        me = self._slot(x, y, c)

        def src(w, dev):
            return in_refs[w].at[dev] if w < self.n_sliced else in_refs[w]

        for w in range(self.n):
            own = None if local_sems is None else pltpu.make_async_copy(src(w, me), out_refs[w].at[me], local_sems.at[w])
            sends, recvs = [], []
            for j, r in enumerate(self.flips):
                px = 1 - x if r & 4 else x
                py = 1 - y if r & 2 else y
                pc = 1 - c if r & 1 else c
                peer, pidx = (px, py, pc), self._slot(px, py, pc)
                k = len(self.flips) * w + j
                sends.append(pltpu.make_async_remote_copy(
                    src_ref=src(w, pidx), dst_ref=out_refs[w].at[me], send_sem=send_sems.at[k],
                    recv_sem=recv_sems.at[k], device_id=peer, device_id_type=pl.DeviceIdType.MESH))
                recvs.append(pltpu.make_async_remote_copy(
                    src_ref=src(w, me), dst_ref=out_refs[w].at[pidx], send_sem=send_sems.at[k],
                    recv_sem=recv_sems.at[k], device_id=peer, device_id_type=pl.DeviceIdType.MESH))
            if p == 0:
                if own is not None:
                    own.start()
                for cp in sends:
                    cp.start()
            else:
                for cp in recvs:
                    cp.wait_recv()
                for cp in sends:
                    cp.wait_send()
                if own is not None:
                    own.wait()


class _SplitScatter:
    def __init__(self, ex, name):
        self.ex, self.name = ex, name

    def _specs(self):
        ex = self.ex
        hbm = pl.BlockSpec(memory_space=pltpu.HBM)
        sem = pl.BlockSpec(memory_space=pltpu.SEMAPHORE)
        effect = pltpu.CompilerParams(has_side_effects=pltpu.SideEffectType.DATAFLOW_SIDE_EFFECTING)
        buffers = [pltpu.HBM(a.shape, a.dtype) for a in ex.operands] + [pltpu.HBM(s.shape, s.dtype) for s in ex.out_shape]
        return hbm, sem, effect, buffers

    def start(self):
        ex, n = self.ex, self.ex.n
        n_sem = len(ex.flips) * n
        hbm, sem, effect, buffers = self._specs()
        in_hbm = lambda a: pltpu.with_memory_space_constraint(a, pltpu.HBM)

        me = ex._slot(*_mesh_pos())
        lands = []
        for w, (a, s) in enumerate(zip(ex.operands, ex.out_shape)):
            mine = lax.dynamic_index_in_dim(a, me, 0, keepdims=True) if w < ex.n_sliced else a[None]
            lands.append(lax.dynamic_update_slice_in_dim(lax.empty(s.shape, s.dtype), mine, me, 0))

        def start_body(*refs):
            ex.phase(0, refs[:n], refs[n:2 * n], (refs[2 * n], refs[2 * n + 1], None))
            refs[-1][...] = jnp.zeros_like(refs[-1])

        self.started = pl.pallas_call(
            start_body, name=self.name + "_start",
            out_shape=[pltpu.SemaphoreType.DMA((n_sem,)), pltpu.SemaphoreType.DMA((n_sem,))] + buffers
            + [jax.ShapeDtypeStruct((8, LANES), F32)],
            in_specs=[hbm] * (2 * n), out_specs=[sem, sem] + [hbm] * (2 * n) + [pl.BlockSpec(memory_space=pltpu.VMEM)],
            input_output_aliases={i: 2 + i for i in range(2 * n)}, compiler_params=effect,
        )(*[in_hbm(a) for a in ex.operands], *[in_hbm(a) for a in lands])
        return self.started[-1]

    def wait(self, after):
        ex, n = self.ex, self.ex.n
        hbm, sem, effect, buffers = self._specs()
        anyspec = pl.BlockSpec(memory_space=pl.ANY)

        def wait_body(*refs):
            ex.phase(2, refs[:n], refs[n:2 * n], (refs[2 * n], refs[2 * n + 1], None))

        done = pl.pallas_call(
            wait_body, name=self.name + "_wait", out_shape=buffers,
            in_specs=[hbm] * (2 * n) + [sem, sem] + [anyspec] * len(after), out_specs=[hbm] * (2 * n),
            input_output_aliases={i: i for i in range(2 * n)}, compiler_params=effect,
        )(*self.started[2:2 + 2 * n], self.started[0], self.started[1], *after)
        return done[n:]


def _pair_sum(pair, *, name):
    _, G, K, n = pair.shape
    tk = _tile(K, 256, 16)

    def body(p_ref, o_ref):
        o_ref[...] = (p_ref[0].astype(F32) + p_ref[1].astype(F32)).astype(o_ref.dtype)

    return pl.pallas_call(
        body, name=name, grid=(G, K // tk),
        in_specs=[pl.BlockSpec((2, None, tk, n), lambda g, i: (0, g, i, 0))],
        out_specs=pl.BlockSpec((None, tk, n), lambda g, i: (g, i, 0)),
        out_shape=jax.ShapeDtypeStruct((G, K, n), pair.dtype),
        compiler_params=_params(("parallel", "parallel")),
    )(pair)


def _exchange_alone(ex, *, name):
    n = ex.n

    def body(*refs):
        for p in range(3):
            ex.phase(p, refs[:n], refs[n:2 * n], refs[2 * n:])

    anyspec = pl.BlockSpec(memory_space=pl.ANY)
    return pl.pallas_call(body, name=name, out_shape=ex.out_shape, in_specs=[anyspec] * n,
                          out_specs=[anyspec] * n, scratch_shapes=ex.scratch)(*ex.operands)


def _adam_step(w_ref, p_ref, m_ref, v_ref, g_ref, d_ref, nm_ref, nv_ref):
    g = p_ref[0].astype(F32)
    for j in range(1, p_ref.shape[0]):
        g = g + p_ref[j].astype(F32)
    g_ref[...] = g
    nm = ADAM_B1 * m_ref[...] + (1.0 - ADAM_B1) * g
    nv = ADAM_B2 * v_ref[...] + (1.0 - ADAM_B2) * (g * g)
    nm_ref[...] = nm
    nv_ref[...] = nv
    m_hat = nm / (1.0 - ADAM_B1 ** ADAM_STEP)
    v_hat = nv / (1.0 - ADAM_B2 ** ADAM_STEP)
    d_ref[...] = -ADAM_LR * (m_hat / (jnp.sqrt(v_hat) + ADAM_EPS) + ADAM_WD * w_ref[...])


def _adamw_vectors(ws, parts, ms, vs, *, name):
    n = len(ws)

    def body(*refs):
        w_refs, p_refs, m_refs, v_refs = (refs[i * n:(i + 1) * n] for i in range(4))
        outs = refs[4 * n:]
        for i in range(n):
            _adam_step(w_refs[i], p_refs[i], m_refs[i], v_refs[i], *outs[4 * i:4 * i + 4])

    return pl.pallas_call(
        body, name=name,
        out_shape=[jax.ShapeDtypeStruct(w.shape, F32) for w in ws for _ in range(4)],
    )(*ws, *parts, *ms, *vs)


def _adamw(w, parts, m, v, after, *, name):
    G, R, n = w.shape
    tn = 512 if (n > 512 and n % 512 == 0) else n
    tr = R
    for t in range(16, R, 16):
        if R % t == 0 and t * tn <= ADAM_BLOCK_CAP:
            tr = t
    if R * tn <= ADAM_BLOCK_CAP:
        tr = R

    def body(w_ref, p_ref, m_ref, v_ref, after_ref, g_ref, d_ref, nm_ref, nv_ref):
        _adam_step(w_ref, p_ref, m_ref, v_ref, g_ref, d_ref, nm_ref, nv_ref)

    blk = pl.BlockSpec((None, tr, tn), lambda g, i, j: (g, i, j))
    return pl.pallas_call(
        body, name=name, grid=(G, R // tr, n // tn),
        in_specs=[blk, pl.BlockSpec((parts.shape[0], None, tr, tn), lambda g, i, j: (0, g, i, j)), blk, blk,
                  pl.BlockSpec((8, LANES), lambda g, i, j: (0, 0))],
        out_specs=[blk, blk, blk, blk],
        out_shape=[jax.ShapeDtypeStruct((G, R, n), F32)] * 4,
        compiler_params=_params(("parallel", "parallel", "parallel")),
    )(w, parts, m, v, after)


def _pad_last(a, width):
    return jnp.pad(a, [(0, 0)] * (a.ndim - 1) + [(0, width - a.shape[-1])])


def _cols_of(g):
    return g.transpose(1, 0, 2).reshape(g.shape[1], N_DEV * g.shape[2])


def _col_shards(w):
    return w.reshape(w.shape[0], N_DEV, w.shape[1] // N_DEV).transpose(1, 0, 2)


def kernel(x, positions, ffn1_pre_w, ffn1_w1, ffn1_w2, ffn1_post_w, mix_pre_w, w_in, mla_q_norm_w, mla_w_uq, mla_kv_norm_w, mla_w_ukv, ret_gn_w, w_branch_mla, w_branch_ret, w_out, mix_post_w, ffn2_pre_w, ffn2_w1, ffn2_w2, ffn2_post_w, loss_target, m_ffn1_pre_w, m_ffn1_w1, m_ffn1_w2, m_ffn1_post_w, m_mix_pre_w, m_w_in, m_mla_q_norm_w, m_mla_w_uq, m_mla_kv_norm_w, m_mla_w_ukv, m_ret_gn_w, m_w_branch_mla, m_w_branch_ret, m_w_out, m_mix_post_w, m_ffn2_pre_w, m_ffn2_w1, m_ffn2_w2, m_ffn2_post_w, v_ffn1_pre_w, v_ffn1_w1, v_ffn1_w2, v_ffn1_post_w, v_mix_pre_w, v_w_in, v_mla_q_norm_w, v_mla_w_uq, v_mla_kv_norm_w, v_mla_w_ukv, v_ret_gn_w, v_w_branch_mla, v_w_branch_ret, v_w_out, v_mix_post_w, v_ffn2_pre_w, v_ffn2_w1, v_ffn2_w2, v_ffn2_post_w):
    T, D = x.shape[1], x.shape[2]
    h0 = x[0]
    tgt = loss_target[0]
    pos = positions.reshape(T, 1).astype(F32)

    big = [("ffn1_w1", ffn1_w1, m_ffn1_w1, v_ffn1_w1), ("ffn1_w2", ffn1_w2, m_ffn1_w2, v_ffn1_w2),
           ("w_in", w_in, m_w_in, v_w_in), ("mla_w_uq", mla_w_uq, m_mla_w_uq, v_mla_w_uq),
           ("mla_w_ukv", mla_w_ukv, m_mla_w_ukv, v_mla_w_ukv),
           ("w_branch_mla", w_branch_mla, m_w_branch_mla, v_w_branch_mla),
           ("w_branch_ret", w_branch_ret, m_w_branch_ret, v_w_branch_ret),
           ("w_out", w_out, m_w_out, v_w_out),
           ("ffn2_w1", ffn2_w1, m_ffn2_w1, v_ffn2_w1), ("ffn2_w2", ffn2_w2, m_ffn2_w2, v_ffn2_w2)]
    small = [("ffn1_pre_w", ffn1_pre_w, m_ffn1_pre_w, v_ffn1_pre_w), ("ffn1_post_w", ffn1_post_w, m_ffn1_post_w, v_ffn1_post_w),
             ("mix_pre_w", mix_pre_w, m_mix_pre_w, v_mix_pre_w), ("mla_q_norm_w", mla_q_norm_w, m_mla_q_norm_w, v_mla_q_norm_w),
             ("mla_kv_norm_w", mla_kv_norm_w, m_mla_kv_norm_w, v_mla_kv_norm_w), ("ret_gn_w", ret_gn_w, m_ret_gn_w, v_ret_gn_w),
             ("mix_post_w", mix_post_w, m_mix_post_w, v_mix_post_w), ("ffn2_pre_w", ffn2_pre_w, m_ffn2_pre_w, v_ffn2_pre_w),
             ("ffn2_post_w", ffn2_post_w, m_ffn2_post_w, v_ffn2_post_w)]

    half = ffn1_w2.shape[1]
    hp = -(-half // LANES) * LANES

    def rows_view(w):
        return w[0].T

    def send_w1(w):
        return jnp.pad(rows_view(w).reshape(2, half, D), ((0, 0), (0, hp - half), (0, 0))).reshape(2 * hp, D).astype(BF16)

    def send_w2(w):
        return jnp.pad(w[0], ((0, hp - half), (0, 0))).astype(BF16)

    mixer = ["w_in", "mla_w_uq", "mla_w_ukv", "w_branch_mla", "w_branch_ret", "w_out"]
    uq_w = MLA_NOPE + MLA_ROPE
    mixer_send = [rows_view(w_in).astype(BF16), jnp.pad(rows_view(mla_w_uq), ((0, HP - uq_w), (0, 0))).astype(BF16),
                  mla_w_ukv[0].astype(BF16), w_branch_mla[0].astype(BF16), w_branch_ret[0].astype(BF16),
                  w_out[0].astype(BF16)]

    w1a, w2a = _exchange_alone(_Gather([send_w1(ffn1_w1), send_w2(ffn1_w2)]), name="gather_ffn1")
    w2a = w2a.reshape(N_DEV // 2, 2 * hp, D)
    u1, f1, h1, a0, *got = _ffn_fwd(h0, ffn1_pre_w, w1a, w2a, ffn1_post_w, None, name="ffn1_fwd_gather_mixer",
                                exchange=_Gather(mixer_send))
    fw = dict(zip(mixer, got))

    wi = fw["w_in"].reshape(-1, D)
    cq_w, ckv_w, kr_w = wi[0:384], wi[384:640], wi[640:672]
    rq_w, rk_w = wi[672:928], wi[928:1184]
    rv_w, rg_w = wi[1184:1696], wi[1696:2208]
    gm_w, gr_w = wi[2208:2208 + D], wi[2208 + D:2208 + 2 * D]
    zer = lambda n: jnp.zeros((n, D), BF16)
    head_rows = lambda a, h: jnp.pad(a.reshape(h, -1, D), ((0, 0), (0, HP - a.shape[0] // h), (0, 0))).reshape(h * HP, D)
    w_in_p = jnp.concatenate([head_rows(rq_w, RET_HEADS), head_rows(rk_w, RET_HEADS), rv_w, rg_w,
                              cq_w, ckv_w, zer(MLA_NOPE), kr_w, zer(HP - MLA_NOPE - MLA_ROPE), zer(AW - 768),
                              gm_w, gr_w], axis=0)
    w_uq_p = fw["mla_w_uq"].reshape(QW, MLA_Q_RANK)
    ukv = fw["mla_w_ukv"].transpose(1, 0, 2)
    w_kv_p = jnp.concatenate([_pad_last(ukv[:, :, :MLA_NOPE], HP).reshape(MLA_KV_RANK, QW),
                              _pad_last(ukv[:, :, MLA_NOPE:], HP).reshape(MLA_KV_RANK, QW)], axis=1)
    w_bm_p = jnp.pad(_cols_of(fw["w_branch_mla"]).reshape(MLA_HEADS, MLA_V, D),
                     ((0, 0), (0, HP - MLA_V), (0, 0))).reshape(QW, D)
    w_br, w_o = _cols_of(fw["w_branch_ret"]), fw["w_out"].reshape(D, D)
    tab_mla = _rope_table(MLA_NOPE, MLA_ROPE // 2)
    tab_ret = _rope_table(0, RET_DK // 2)

    proj, a1 = _rms_matmul(h1, mix_pre_w, w_in_p, name="mixer_in_proj")
    q, k, v = _mla_prep_fwd(proj, pos, mla_q_norm_w, mla_kv_norm_w, w_uq_p, w_kv_p, tab_mla, name="mla_prep_fwd")
    o, lse, w1b, w2b = _flash_fwd(q, k, v, name="mla_attn_fwd_gather_ffn2",
                                  exchange=_Gather([send_w1(ffn2_w1), send_w2(ffn2_w2)]))
    w2b = w2b.reshape(N_DEV // 2, 2 * hp, D)
    ypre, yn, rprev = _ret_fwd(proj, pos, tab_ret, name="retention_fwd")
    omla, oret, m, h2 = _merge_fwd(o, yn, proj, ret_gn_w, w_bm_p, w_br, w_o, h1, mix_post_w, name="merge_fwd")
    u2, f2, _, a2, dy, lossp = _ffn_fwd(h2, ffn2_pre_w, w1b, w2b, ffn2_post_w, tgt, name="ffn2_fwd_loss")

    def grad(x, dy, tag, after=None):
        return _matmul_tn(x if x.ndim == 3 else x[None], dy if dy.ndim == 3 else dy[None], name=tag, after=after)

    g2, du2, df2, dh2, gpost2, gpre2 = _ffn_bwd(dy, f2, ffn2_post_w, h2, ffn2_pre_w, u2, w2b, w1b, name="ffn2_bwd")
    dw1b, = grad(du2.reshape(N_DEV, T, 2 * hp), a2, "ffn2_dw1")
    dw2b = grad(g2, df2, "ffn2_dw2")[0].reshape(N_DEV, hp, D)
    (dgm, dgr, do, delta, drg, dyn, gpostm, ggn, dw_out, dw_bm_p, dw_br) = _merge_bwd(
        dh2, m, mix_post_w, omla, oret, proj, yn, ret_gn_w, o, w_o, w_bm_p, w_br, name="merge_bwd")
    sc_ffn2 = _SplitScatter(_Scatter([dw1b, dw2b]), "scatter_ffn2")
    dq, dk, dv = _flash_bwd(q, k, v, do, lse, delta, name="mla_attn_bwd", after=sc_ffn2.start())
    da, gqn, gkvn, dw_uq_p, dw_kv_p = _mla_prep_bwd(dq, dk, dv, proj, pos, mla_q_norm_w, mla_kv_norm_w, w_uq_p, w_kv_p, tab_mla, name="mla_prep_bwd")
    drq, drk, drv = _ret_bwd(dyn, ypre, proj, pos, tab_ret, rprev, name="retention_bwd")
    dproj = jnp.concatenate([drq, drk, drv, drg, da, dgm, dgr], axis=1)
    dw_in_p = grad(dproj, a1, "dw_in")[0][0]

    dw_uq = dw_uq_p.reshape(MLA_HEADS, HP, MLA_Q_RANK)[:, :uq_w]
    dkp = dw_kv_p[:, :QW].reshape(MLA_KV_RANK, MLA_HEADS, HP)[:, :, :MLA_NOPE]
    dvp = dw_kv_p[:, QW:].reshape(MLA_KV_RANK, MLA_HEADS, HP)[:, :, :MLA_V]
    dw_ukv = jnp.concatenate([dkp, dvp], axis=2).transpose(1, 0, 2)
    dw_bm = dw_bm_p.reshape(MLA_HEADS, HP, D)[:, :MLA_V].reshape(MLA_HEADS * MLA_V, D)
    small_mixer_grads = [dw_uq, dw_ukv, _col_shards(dw_bm), _col_shards(dw_br), dw_out.reshape(N_DEV, D // N_DEV, D)]
    sc_small = _SplitScatter(_Scatter(small_mixer_grads), "scatter_mixer_small")
    dh1, gmixpre = _proj_bwd(dproj, w_in_p, h1, mix_pre_w, dh2, name="mixer_in_bwd", after=sc_small.start())
    unhead = lambda a, h, wd: a.reshape(h, HP, D)[:, :wd].reshape(h * wd, D)
    c0 = 4 * RW
    dw_in = jnp.concatenate([
        dw_in_p[c0:c0 + 384], dw_in_p[c0 + 384:c0 + 640], dw_in_p[c0 + 640 + MLA_NOPE:c0 + 640 + MLA_NOPE + MLA_ROPE],
        unhead(dw_in_p[0:RW], RET_HEADS, RET_DK), unhead(dw_in_p[RW:2 * RW], RET_HEADS, RET_DK),
        dw_in_p[2 * RW:3 * RW], dw_in_p[3 * RW:4 * RW],
        dw_in_p[PROJ_FIXED:PROJ_FIXED + D], dw_in_p[PROJ_FIXED + D:PROJ_FIXED + 2 * D]], axis=0).reshape(N_DEV, -1, D)
    sc_w_in = _SplitScatter(_Scatter([dw_in]), "scatter_w_in")
    g1, du1, df1, dx, gpost1, gpre1 = _ffn_bwd(
        dh1, f1, ffn1_post_w, h0, ffn1_pre_w, u1, w2a, w1a, name="ffn1_bwd", after=sc_w_in.start())
    dw2a = grad(g1, df1, "ffn1_dw2")[0].reshape(N_DEV, hp, D)
    sc_dw2a = _SplitScatter(_Scatter([dw2a]), "scatter_ffn1_dw2")
    dw1a, = grad(du1.reshape(N_DEV, T, 2 * hp), a0, "ffn1_dw1", after=sc_dw2a.start())

    small_g = {"ffn1_pre_w": gpre1, "ffn1_post_w": gpost1, "mix_pre_w": gmixpre, "mla_q_norm_w": gqn,
               "mla_kv_norm_w": gkvn, "ret_gn_w": ggn, "mix_post_w": gpostm, "ffn2_pre_w": gpre2, "ffn2_post_w": gpost2}
    sc_vec = _SplitScatter(_Scatter([], whole=[small_g[nm] for nm, *_ in small] + [lossp]), "scatter_vectors")
    token_vec = sc_vec.start()
    by_core = dw1a.reshape(N_DEV // 2, 2, 2 * hp, D).transpose(1, 0, 2, 3)
    swapped, = _exchange_alone(_Scatter([by_core], axes=("c",)), name="swap_ffn1_dw1_cores")
    sc_last = _SplitScatter(_Scatter([_pair_sum(swapped, name="add_ffn1_dw1_cores")], axes=("x", "y")), "scatter_ffn1_dw1")
    token = sc_last.start()
    recv_ffn2 = sc_ffn2.wait([token])
    recv_mixer = sc_w_in.wait([token]) + sc_small.wait([token])
    recv_w2a, = sc_dw2a.wait([token])
    parts = dict(zip(mixer, recv_mixer))
    parts.update(ffn1_w2=recv_w2a, ffn2_w1=recv_ffn2[0], ffn2_w2=recv_ffn2[1])
    as_is = (lambda a: a, lambda p: p[:, None], lambda a: a)
    views = {nm: as_is for nm, *_ in big}
    for nm in ("ffn1_w1", "ffn2_w1"):
        views[nm] = (lambda a: rows_view(a).reshape(2, half, D), lambda p: p.reshape(p.shape[0], 2, hp, D),
                     lambda a: a.reshape(2 * half, D).T[None])
    for nm in ("w_in", "mla_w_uq"):
        views[nm] = (lambda a: rows_view(a)[None], lambda p: p[:, None], lambda a: a[0].T[None])

    def update(nm, w, m_, v_, after):
        to_view, parts_view, back = views[nm]
        return [back(a) for a in _adamw(to_view(w), parts_view(parts[nm]), to_view(m_), to_view(v_), after,
                                        name="adamw_" + nm)]

    big_out = {nm: update(nm, w, m_, v_, token) for nm, w, m_, v_ in big if nm != "ffn1_w1"}
    shadow = [d[0] for d in big_out.values()]
    recv_w1a, = sc_last.wait(shadow)
    *small_parts, loss_parts = sc_vec.wait(shadow + [token_vec])
    loss = jnp.sum(loss_parts[:, ::8, 0])
    parts["ffn1_w1"] = recv_w1a
    big_out["ffn1_w1"] = update("ffn1_w1", ffn1_w1, m_ffn1_w1, v_ffn1_w1, jnp.zeros((8, LANES), F32))
    small_out = _adamw_vectors([w for _, w, _, _ in small], small_parts, [a for _, _, a, _ in small],
                               [a for _, _, _, a in small], name="adamw_replicated")

    order = ["ffn1_pre_w", "ffn1_w1", "ffn1_w2", "ffn1_post_w", "mix_pre_w", "w_in", "mla_q_norm_w", "mla_w_uq",
             "mla_kv_norm_w", "mla_w_ukv", "ret_gn_w", "w_branch_mla", "w_branch_ret", "w_out", "mix_post_w",
             "ffn2_pre_w", "ffn2_w1", "ffn2_w2", "ffn2_post_w"]
    outs = [loss, dx[None]]
    for i in range(4):
        both = {nm: big_out[nm][i] for nm in big_out}
        both.update({nm: small_out[4 * j + i] for j, (nm, *_) in enumerate(small)})
        outs += [both[nm] for nm in order]
    return tuple(outs)
```

```python
import math

import numpy as np
import jax
import jax.numpy as jnp
from jax import lax
from jax.experimental import pallas as pl
from jax.experimental.pallas import tpu as pltpu

F32, BF16 = jnp.float32, jnp.bfloat16

MLA_HEADS, MLA_NOPE, MLA_ROPE, MLA_V = 8, 64, 32, 64
MLA_Q_RANK, MLA_KV_RANK = 384, 256
RET_HEADS, RET_DK, RET_DV = 4, 64, 128
ROPE_BASE, NORM_EPS, GN_EPS = 10000.0, 1e-6, 1e-6
ADAM_LR, ADAM_B1, ADAM_B2, ADAM_EPS, ADAM_WD, ADAM_STEP = 0.001, 0.9, 0.999, 1e-08, 0.01, 10
ATTN_SCALE = 1.0 / math.sqrt(MLA_NOPE + MLA_ROPE)

N_DEV = 8
LANES = 128
HP = LANES
QW = MLA_HEADS * HP
RW = RET_HEADS * HP
AW = 1024
PROJ_FIXED = 4 * RW + AW
NEG = -1e30

TOKEN_TILE = 512
ATTN_TILE = 1024
ATTN_CHAINS = 2
FFN_CHAINS = 2
RET_TILE = 256
PROJ_TILE_CAP = 2560
GRAD_TILE_CAP = 1408
GRAD_TOKEN_TILE = 2048
ADAM_BLOCK_CAP = 192 * 1024
MERGE_TILE = 256
VMEM_LIMIT = 56 * 1024 * 1024


def _tile(n, cap, mult=LANES):
    if n <= cap:
        return n
    best = None
    for t in range(mult, cap + 1, mult):
        if n % t == 0:
            best = t
    assert best is not None, (n, cap, mult)
    return best


def _params(sem):
    return pltpu.CompilerParams(dimension_semantics=sem, vmem_limit_bytes=VMEM_LIMIT)


def _dot(a, b):
    return lax.dot_general(a, b, (((1,), (0,)), ((), ())), preferred_element_type=F32)


def _dot_nt(a, b):
    return lax.dot_general(a, b, (((1,), (1,)), ((), ())), preferred_element_type=F32)


def _dot_tn(a, b):
    return lax.dot_general(a, b, (((0,), (0,)), ((), ())), preferred_element_type=F32)


def _sigmoid(x):
    return pl.reciprocal(1.0 + jnp.exp(-x), approx=True)


def _rms_fwd(x, w):
    r = lax.rsqrt(jnp.mean(x * x, axis=-1, keepdims=True) + NORM_EPS)
    return x * r * w


def _rms_bwd(x, w, dy):
    r = lax.rsqrt(jnp.mean(x * x, axis=-1, keepdims=True) + NORM_EPS)
    xh = x * r
    g = dy * w
    dx = r * (g - xh * jnp.mean(g * xh, axis=-1, keepdims=True))
    return dx, jnp.sum(dy * xh, axis=0, keepdims=True)


def _rope_table(first, half):
    inv = (np.float32(ROPE_BASE) ** (-(np.arange(half, dtype=np.float32) / np.float32(half)))).astype(np.float32)
    tab = np.zeros((8, LANES), np.float32)
    tab[0, first:first + half] = inv
    tab[0, first + half:first + 2 * half] = inv
    tab[1, first:first + half] = -1.0
    tab[2, first + half:first + 2 * half] = 1.0
    return jnp.asarray(tab)


def _rope_cs(pos, tab_ref):
    ang = pos * tab_ref[0:1, :]
    s = jnp.sin(ang)
    return jnp.cos(ang), s * tab_ref[1:2, :], s * tab_ref[2:3, :]


def _rope(x, cs, half, inverse=False):
    c, s1, s2 = cs
    a = pltpu.roll(x, LANES - half, 1) * s1 + pltpu.roll(x, half, 1) * s2
    return x * c - a if inverse else x * c + a


def _call(body, *, name, grid, in_specs, out_specs, out_shape, scratch_shapes, args, exchange=None, after=None):
    sem = ("arbitrary",) * len(grid)
    anyspec = pl.BlockSpec(memory_space=pl.ANY)
    if exchange is None and after is not None:
        n_own = len(in_specs)

        def behind(*refs):
            body(*refs[:n_own], *refs[n_own + 1:])

        return pl.pallas_call(behind, name=name, grid=grid, in_specs=list(in_specs) + [anyspec], out_specs=out_specs,
                              out_shape=out_shape, scratch_shapes=scratch_shapes, compiler_params=_params(sem))(*args, after)
    if exchange is None:
        return pl.pallas_call(body, name=name, grid=grid, in_specs=in_specs, out_specs=out_specs,
                              out_shape=out_shape, scratch_shapes=scratch_shapes, compiler_params=_params(sem))(*args)
    n_in, n_out, e = len(in_specs), len(out_specs), exchange.n
    total = math.prod(grid)

    def carried(*refs):
        own = refs[:n_in] + refs[n_in + e:n_in + e + n_out] + refs[n_in + 2 * e + n_out:len(refs) - 3]
        ex_refs = (refs[n_in:n_in + e], refs[n_in + e + n_out:n_in + 2 * e + n_out], refs[len(refs) - 3:])
        step = pl.program_id(0)
        for d in range(1, len(grid)):
            step = step * grid[d] + pl.program_id(d)

        @pl.when(step == 0)
        def _():
            exchange.phase(0, *ex_refs)

        @pl.when(step == (3 * total) // 4)
        def _():
            exchange.phase(1, *ex_refs)

        body(*own)

        @pl.when(step == total - 1)
        def _():
            exchange.phase(2, *ex_refs)

    return pl.pallas_call(
        carried, name=name, grid=grid, in_specs=list(in_specs) + [anyspec] * e,
        out_specs=list(out_specs) + [anyspec] * e, out_shape=list(out_shape) + exchange.out_shape,
        scratch_shapes=list(scratch_shapes) + exchange.scratch, compiler_params=_params(sem),
    )(*args, *exchange.operands)


def _ffn_fwd(h, pre_w, w1, w2, post_w, target, *, name, exchange=None):
    T, D = h.shape
    nk, ck = w2.shape[0], w2.shape[1]
    tT = min(TOKEN_TILE, T)
    nT = T // tT
    with_loss = target is not None

    def body(*refs):
        if with_loss:
            (h_ref, pre_ref, w1g_ref, w1u_ref, w2_ref, post_ref, tgt_ref,
             u_ref, f_ref, ho_ref, a_s, dy_ref, loss_ref, acc) = refs
        else:
            (h_ref, pre_ref, w1g_ref, w1u_ref, w2_ref, post_ref,
             u_ref, f_ref, ho_ref, a_s, acc) = refs
        k = pl.program_id(1)

        @pl.when(k == 0)
        def _():
            a_s[...] = _rms_fwd(h_ref[...], pre_ref[...]).astype(BF16)
            acc[...] = jnp.zeros_like(acc)

        for c in range(FFN_CHAINS):
            rs = slice(c * (tT // FFN_CHAINS), (c + 1) * (tT // FFN_CHAINS))
            a = a_s[rs, :]
            ug = _dot_nt(a, w1g_ref[...])
            uu = _dot_nt(a, w1u_ref[...])
            u_ref[0, rs, :] = ug.astype(BF16)
            u_ref[1, rs, :] = uu.astype(BF16)
            acc[rs, :] += _dot((ug * _sigmoid(ug) * uu).astype(BF16), w2_ref[...])

        @pl.when(k == nk - 1)
        def _():
            f = acc[...]
            f_ref[...] = f
            ho = h_ref[...] + 0.5 * _rms_fwd(f, post_ref[...])
            ho_ref[...] = ho
            if with_loss:
                e = ho - tgt_ref[...]
                dy_ref[...] = e * (1.0 / D)
                loss_ref[...] = jnp.full(loss_ref.shape, (0.5 / D) * jnp.sum(e * e), F32)

    row = pl.BlockSpec((tT, D), lambda i, k: (i, 0))
    vec = pl.BlockSpec((1, D), lambda i, k: (0, 0))
    in_specs = [row, vec,
                pl.BlockSpec((None, ck, D), lambda i, k: (k, 0, 0)),
                pl.BlockSpec((None, ck, D), lambda i, k: (nk + k, 0, 0)),
                pl.BlockSpec((None, ck, D), lambda i, k: (k, 0, 0)),
                vec]
    out_shape = [jax.ShapeDtypeStruct((2, nk, T, ck), BF16),
                 jax.ShapeDtypeStruct((T, D), F32),
                 jax.ShapeDtypeStruct((T, D), F32),
                 jax.ShapeDtypeStruct((T, D), BF16)]
    out_specs = [pl.BlockSpec((2, None, tT, ck), lambda i, k: (0, k, i, 0)), row, row, row]
    args = [h, pre_w, w1, w1, w2, post_w]
    if with_loss:
        in_specs.append(row)
        args.append(target)
        out_shape += [jax.ShapeDtypeStruct((T, D), F32), jax.ShapeDtypeStruct((nT * 8, LANES), F32)]
        out_specs += [row, pl.BlockSpec((8, LANES), lambda i, k: (i, 0))]
    return _call(body, name=name, grid=(nT, nk), in_specs=in_specs, out_specs=out_specs, out_shape=out_shape,
                 scratch_shapes=[pltpu.VMEM((tT, D), F32)], args=args, exchange=exchange)


def _ffn_bwd(dho, f, post_w, h, pre_w, u, w2, w1, *, name, exchange=None, after=None):
    T, D = h.shape
    nk, ck = w2.shape[0], w2.shape[1]
    tT = min(TOKEN_TILE, T)
    nT = T // tT

    def body(dho_ref, f_ref, post_ref, h_ref, pre_ref, u_ref, w2_ref, w1g_ref, w1u_ref,
             g_ref, du_ref, df_s, dh_ref, gpost_ref, gpre_ref, da_acc):
        i, k = pl.program_id(0), pl.program_id(1)

        @pl.when(jnp.logical_and(i == 0, k == 0))
        def _():
            gpost_ref[...] = jnp.zeros_like(gpost_ref)
            gpre_ref[...] = jnp.zeros_like(gpre_ref)

        @pl.when(k == 0)
        def _():
            dx, dw = _rms_bwd(f_ref[...], post_ref[...], 0.5 * dho_ref[...])
            df_s[...] = dx.astype(BF16)
            gpost_ref[...] += dw
            da_acc[...] = jnp.zeros_like(da_acc)

        groups = [slice(c * (tT // FFN_CHAINS), (c + 1) * (tT // FFN_CHAINS)) for c in range(FFN_CHAINS)]
        dgs = [_dot_nt(df_s[rs, :], w2_ref[...]) for rs in groups]
        for rs, dg in zip(groups, dgs):
            ug = u_ref[0, rs, :].astype(F32)
            uu = u_ref[1, rs, :].astype(F32)
            sg = _sigmoid(ug)
            sl = ug * sg
            g_ref[rs, :] = (sl * uu).astype(BF16)
            dug = (dg * uu * (sg + sl * (1.0 - sg))).astype(BF16)
            duu = (dg * sl).astype(BF16)
            du_ref[0, rs, :] = dug
            du_ref[1, rs, :] = duu
            da_acc[rs, :] += _dot(dug, w1g_ref[...]) + _dot(duu, w1u_ref[...])

        @pl.when(k == nk - 1)
        def _():
            dx, dw = _rms_bwd(h_ref[...], pre_ref[...], da_acc[...])
            dh_ref[...] = dho_ref[...] + dx
            gpre_ref[...] += dw

    row = pl.BlockSpec((tT, D), lambda i, k: (i, 0))
    vec = pl.BlockSpec((1, D), lambda i, k: (0, 0))
    return _call(
        body, name=name, grid=(nT, nk),
        in_specs=[row, row, vec, row, vec,
                  pl.BlockSpec((2, None, tT, ck), lambda i, k: (0, k, i, 0)),
                  pl.BlockSpec((None, ck, D), lambda i, k: (k, 0, 0)),
                  pl.BlockSpec((None, ck, D), lambda i, k: (k, 0, 0)),
                  pl.BlockSpec((None, ck, D), lambda i, k: (nk + k, 0, 0))],
        out_specs=[pl.BlockSpec((None, tT, ck), lambda i, k: (k, i, 0)),
                   pl.BlockSpec((2, None, tT, ck), lambda i, k: (0, k, i, 0)),
                   row, row, vec, vec],
        out_shape=[jax.ShapeDtypeStruct((nk, T, ck), BF16),
                   jax.ShapeDtypeStruct((2, nk, T, ck), BF16),
                   jax.ShapeDtypeStruct((T, D), BF16),
                   jax.ShapeDtypeStruct((T, D), F32),
                   jax.ShapeDtypeStruct((1, D), F32),
                   jax.ShapeDtypeStruct((1, D), F32)],
        scratch_shapes=[pltpu.VMEM((tT, D), F32)],
        args=(dho, f, post_w, h, pre_w, u, w2, w1, w1), exchange=exchange, after=after)


def _matmul_tn(x, dy, *, name, exchange=None, after=None, out_index=lambda p: p):
    Px, T, K = x.shape
    Py, _, N = dy.shape
    P = max(Px, Py)
    tT, tK, tN = min(GRAD_TOKEN_TILE, T), _tile(K, GRAD_TILE_CAP), _tile(N, GRAD_TILE_CAP)
    nt = T // tT

    def body(x_ref, dy_ref, o_ref, acc):
        t = pl.program_id(3)

        @pl.when(t == 0)
        def _():
            acc[...] = jnp.zeros_like(acc)

        acc[...] += _dot_tn(x_ref[...], dy_ref[...])

        @pl.when(t == nt - 1)
        def _():
            o_ref[...] = acc[...].astype(BF16)

    return _call(
        body, name=name, grid=(P, K // tK, N // tN, nt),
        in_specs=[pl.BlockSpec((None, tT, tK), lambda p, a, b, t: (p if Px > 1 else 0, t, a)),
                  pl.BlockSpec((None, tT, tN), lambda p, a, b, t: (p if Py > 1 else 0, t, b))],
        out_specs=[pl.BlockSpec((None, tK, tN), lambda p, a, b, t: (out_index(p), a, b))],
        out_shape=[jax.ShapeDtypeStruct((P, K, N), BF16)],
        scratch_shapes=[pltpu.VMEM((tK, tN), F32)], args=(x, dy), exchange=exchange, after=after)


def _rms_matmul(h, wn, w, *, name):
    T, D = h.shape
    N = w.shape[0]
    tT, tN = min(TOKEN_TILE, T), _tile(N, PROJ_TILE_CAP)

    def body(h_ref, wn_ref, w_ref, y_ref, a_ref):
        @pl.when(pl.program_id(1) == 0)
        def _():
            a_ref[...] = _rms_fwd(h_ref[...], wn_ref[...]).astype(BF16)

        y_ref[...] = _dot_nt(a_ref[...], w_ref[...]).astype(BF16)

    return pl.pallas_call(
        body, name=name, grid=(T // tT, N // tN),
        in_specs=[pl.BlockSpec((tT, D), lambda i, j: (i, 0)),
                  pl.BlockSpec((1, D), lambda i, j: (0, 0)),
                  pl.BlockSpec((tN, D), lambda i, j: (j, 0))],
        out_specs=[pl.BlockSpec((tT, tN), lambda i, j: (i, j)),
                   pl.BlockSpec((tT, D), lambda i, j: (i, 0))],
        out_shape=[jax.ShapeDtypeStruct((T, N), BF16), jax.ShapeDtypeStruct((T, D), BF16)],
        compiler_params=_params(("parallel", "arbitrary")),
    )(h, wn, w)


def _proj_bwd(dproj, w, h, wn, dres, *, name, exchange=None, after=None):
    T, D = h.shape
    N = w.shape[0]
    tT, tN = min(TOKEN_TILE, T), _tile(N, PROJ_TILE_CAP)
    nn = N // tN

    def body(dp_ref, w_ref, h_ref, wn_ref, dres_ref, dh_ref, gw_ref, acc):
        i, j = pl.program_id(0), pl.program_id(1)

        @pl.when(jnp.logical_and(i == 0, j == 0))
        def _():
            gw_ref[...] = jnp.zeros_like(gw_ref)

        @pl.when(j == 0)
        def _():
            acc[...] = jnp.zeros_like(acc)

        acc[...] += _dot(dp_ref[...], w_ref[...])

        @pl.when(j == nn - 1)
        def _():
            dx, dw = _rms_bwd(h_ref[...], wn_ref[...], acc[...])
            dh_ref[...] = dres_ref[...] + dx
            gw_ref[...] += dw

    row = pl.BlockSpec((tT, D), lambda i, j: (i, 0))
    vec = pl.BlockSpec((1, D), lambda i, j: (0, 0))
    return _call(
        body, name=name, grid=(T // tT, nn),
        in_specs=[pl.BlockSpec((tT, tN), lambda i, j: (i, j)),
                  pl.BlockSpec((tN, D), lambda i, j: (j, 0)), row, vec, row],
        out_specs=[row, vec],
        out_shape=[jax.ShapeDtypeStruct((T, D), F32), jax.ShapeDtypeStruct((1, D), F32)],
        scratch_shapes=[pltpu.VMEM((tT, D), F32)], args=(dproj, w, h, wn, dres), exchange=exchange, after=after)


def _mla_prep_fwd(proj, pos, qn_w, kvn_w, w_uq, w_kv, tab, *, name):
    T = proj.shape[0]
    tT = min(TOKEN_TILE, T)
    a_blk = PROJ_FIXED // AW - 1

    def body(a_ref, pos_ref, qnw_ref, kvnw_ref, wuq_ref, wkv_ref, tab_ref,
             q_ref, k_ref, v_ref):
        cq = a_ref[:, 0:MLA_Q_RANK].astype(F32)
        ckv = a_ref[:, MLA_Q_RANK:MLA_Q_RANK + MLA_KV_RANK].astype(F32)
        kr = a_ref[:, 640:768].astype(F32)
        qn = _rms_fwd(cq, qnw_ref[...]).astype(BF16)
        kvn = _rms_fwd(ckv, kvnw_ref[...]).astype(BF16)
        cs = _rope_cs(pos_ref[...], tab_ref)
        q = _dot_nt(qn, wuq_ref[...])
        kv = _dot(kvn, wkv_ref[...])
        krr = _rope(kr, cs, MLA_ROPE // 2)
        for hd in range(MLA_HEADS):
            sl = slice(hd * HP, (hd + 1) * HP)
            q_ref[:, sl] = (_rope(q[:, sl], cs, MLA_ROPE // 2) * ATTN_SCALE).astype(BF16)
            k_ref[:, sl] = (kv[:, sl] + krr).astype(BF16)
        v_ref[...] = kv[:, QW:].astype(BF16)

    def full(r, c):
        return pl.BlockSpec((r, c), lambda i: (0, 0))

    def rows(c):
        return pl.BlockSpec((tT, c), lambda i: (i, 0))

    return pl.pallas_call(
        body, name=name, grid=(T // tT,),
        in_specs=[pl.BlockSpec((tT, AW), lambda i: (i, a_blk)), rows(1),
                  full(1, MLA_Q_RANK), full(1, MLA_KV_RANK),
                  full(QW, MLA_Q_RANK), full(MLA_KV_RANK, 2 * QW), full(8, LANES)],
        out_specs=[rows(QW), rows(QW), rows(QW)],
        out_shape=[jax.ShapeDtypeStruct((T, QW), BF16)] * 3,
        compiler_params=_params(("parallel",)),
    )(proj, pos, qn_w, kvn_w, w_uq, w_kv, tab)


def _mla_prep_bwd(dq, dk, dv, proj, pos, qn_w, kvn_w, w_uq, w_kv, tab, *, name):
    T = proj.shape[0]
    tT = min(TOKEN_TILE, T)
    nT = T // tT
    a_blk = PROJ_FIXED // AW - 1

    def body(dq_ref, dk_ref, dv_ref, a_ref, pos_ref, qnw_ref, kvnw_ref, wuq_ref, wkv_ref, tab_ref,
             da_ref, gqn_ref, gkvn_ref, dwuq_ref, dwkv_ref, dql_ref, dkvl_ref, acc_uq, acc_kv):
        @pl.when(pl.program_id(0) == 0)
        def _():
            gqn_ref[...] = jnp.zeros_like(gqn_ref)
            gkvn_ref[...] = jnp.zeros_like(gkvn_ref)
            acc_uq[...] = jnp.zeros_like(acc_uq)
            acc_kv[...] = jnp.zeros_like(acc_kv)

        cs = _rope_cs(pos_ref[...], tab_ref)
        dkr = jnp.zeros((tT, HP), F32)
        for hd in range(MLA_HEADS):
            sl = slice(hd * HP, (hd + 1) * HP)
            dql_ref[:, sl] = (_rope(dq_ref[:, sl], cs, MLA_ROPE // 2, inverse=True) * ATTN_SCALE).astype(BF16)
            dkh = dk_ref[:, sl]
            dkr = dkr + dkh
            dkvl_ref[:, sl] = dkh.astype(BF16)
        dkvl_ref[:, QW:] = dv_ref[...]
        dqn = _dot(dql_ref[...], wuq_ref[...])
        dkvn = _dot_nt(dkvl_ref[...], wkv_ref[...])
        cq = a_ref[:, 0:MLA_Q_RANK].astype(F32)
        ckv = a_ref[:, MLA_Q_RANK:MLA_Q_RANK + MLA_KV_RANK].astype(F32)
        dcq, gq = _rms_bwd(cq, qnw_ref[...], dqn)
        dckv, gkv = _rms_bwd(ckv, kvnw_ref[...], dkvn)
        gqn_ref[...] += gq
        gkvn_ref[...] += gkv
        da_ref[:, 0:MLA_Q_RANK] = dcq.astype(BF16)
        da_ref[:, MLA_Q_RANK:MLA_Q_RANK + MLA_KV_RANK] = dckv.astype(BF16)
        da_ref[:, 640:768] = _rope(dkr, cs, MLA_ROPE // 2, inverse=True).astype(BF16)
        da_ref[:, 768:AW] = jnp.zeros((tT, AW - 768), BF16)
        acc_uq[...] += _dot_tn(dql_ref[...], _rms_fwd(cq, qnw_ref[...]).astype(BF16))
        acc_kv[...] += _dot_tn(_rms_fwd(ckv, kvnw_ref[...]).astype(BF16), dkvl_ref[...])

        @pl.when(pl.program_id(0) == nT - 1)
        def _():
            dwuq_ref[...] = acc_uq[...].astype(BF16)
            dwkv_ref[...] = acc_kv[...].astype(BF16)

    def full(r, c):
        return pl.BlockSpec((r, c), lambda i: (0, 0))

    def rows(c):
        return pl.BlockSpec((tT, c), lambda i: (i, 0))

    return pl.pallas_call(
        body, name=name, grid=(nT,),
        in_specs=[rows(QW), rows(QW), rows(QW), pl.BlockSpec((tT, AW), lambda i: (i, a_blk)), rows(1),
                  full(1, MLA_Q_RANK), full(1, MLA_KV_RANK),
                  full(QW, MLA_Q_RANK), full(MLA_KV_RANK, 2 * QW), full(8, LANES)],
        out_specs=[rows(AW), full(1, MLA_Q_RANK), full(1, MLA_KV_RANK),
                   full(QW, MLA_Q_RANK), full(MLA_KV_RANK, 2 * QW)],
        out_shape=[jax.ShapeDtypeStruct((T, AW), BF16),
                   jax.ShapeDtypeStruct((1, MLA_Q_RANK), F32), jax.ShapeDtypeStruct((1, MLA_KV_RANK), F32),
                   jax.ShapeDtypeStruct((QW, MLA_Q_RANK), BF16), jax.ShapeDtypeStruct((MLA_KV_RANK, 2 * QW), BF16)],
        scratch_shapes=[pltpu.VMEM((tT, QW), BF16), pltpu.VMEM((tT, 2 * QW), BF16),
                        pltpu.VMEM((QW, MLA_Q_RANK), F32), pltpu.VMEM((MLA_KV_RANK, 2 * QW), F32)],
        compiler_params=_params(("arbitrary",)),
    )(dq, dk, dv, proj, pos, qn_w, kvn_w, w_uq, w_kv, tab)


def _flash_fwd(q, k, v, *, name, exchange=None):
    T = q.shape[0]
    H = q.shape[1] // HP
    tq = min(ATTN_TILE, T)
    nq = T // tq

    sub = tq // ATTN_CHAINS

    def body(q_ref, k_ref, v_ref, o_ref, lse_ref):
        qi = pl.program_id(1)
        qs = [q_ref[c * sub:(c + 1) * sub, :] for c in range(ATTN_CHAINS)]

        def update(carry, off, masked):
            nks = [(c + 1) * sub if masked else tq for c in range(ATTN_CHAINS)]
            scores = [_dot_nt(qs[c], k_ref[pl.ds(off, nks[c]), :]) for c in range(ATTN_CHAINS)]
            out = []
            for c in range(ATTN_CHAINS):
                m_prev, l_prev, acc = carry[c]
                nk, s = nks[c], scores[c]
                vb = v_ref[pl.ds(off, nk), :]
                if masked:
                    rows = lax.broadcasted_iota(jnp.int32, (sub, nk), 0) + c * sub
                    s = jnp.where(rows >= lax.broadcasted_iota(jnp.int32, (sub, nk), 1), s, NEG)
                m_new = jnp.maximum(m_prev, jnp.max(s, axis=1, keepdims=True))
                alpha = jnp.exp(m_prev - m_new)
                p = jnp.exp(s - m_new)
                out.append((m_new, alpha * l_prev + jnp.sum(p, axis=1, keepdims=True),
                            alpha * acc + _dot(p.astype(BF16), vb)))
            return tuple(out)

        init = tuple((jnp.full((sub, 1), NEG, F32), jnp.zeros((sub, 1), F32), jnp.zeros((sub, HP), F32))
                     for _ in range(ATTN_CHAINS))
        carry = lax.fori_loop(0, qi, lambda j, cr: update(cr, pl.multiple_of(j * tq, tq), False), init)
        carry = update(carry, pl.multiple_of(qi * tq, tq), True)
        for c in range(ATTN_CHAINS):
            m_fin, l_fin, acc = carry[c]
            o_ref[c * sub:(c + 1) * sub, :] = (acc / l_fin).astype(BF16)
            lse_ref[c * sub:(c + 1) * sub, :] = jnp.broadcast_to(m_fin + jnp.log(l_fin), (sub, HP))

    qspec = pl.BlockSpec((tq, HP), lambda h, i: (i, h))
    kspec = pl.BlockSpec((T, HP), lambda h, i: (0, h))
    return _call(
        body, name=name, grid=(H, nq),
        in_specs=[qspec, kspec, kspec], out_specs=[qspec, qspec],
        out_shape=[jax.ShapeDtypeStruct((T, H * HP), BF16), jax.ShapeDtypeStruct((T, H * HP), F32)],
        scratch_shapes=[], args=(q, k, v), exchange=exchange)


def _flash_bwd(q, k, v, do, lse, delta, *, name, exchange=None, after=None):
    T = q.shape[0]
    H = q.shape[1] // HP
    tq = min(ATTN_TILE, T)
    nq = T // tq
    sub = tq // ATTN_CHAINS

    def body(k_ref, v_ref, q_ref, do_ref, lse_ref, dl_ref, dq_ref, dk_ref, dv_ref):
        ki = pl.program_id(1)

        @pl.when(ki == 0)
        def _():
            dq_ref[...] = jnp.zeros_like(dq_ref)

        def grow(a):
            return a if a.shape[0] == tq else jnp.concatenate([a, jnp.zeros((tq - a.shape[0], HP), F32)], axis=0)

        def step(carry, j, masked):
            dk_acc, dv_acc = carry
            nks = [(c + 1) * sub if masked else tq for c in range(ATTN_CHAINS)]
            rws = [pl.ds(pl.multiple_of(j * tq + c * sub, sub), sub) for c in range(ATTN_CHAINS)]
            scores = [_dot_nt(q_ref[rws[c], :], k_ref[0:nks[c], :]) for c in range(ATTN_CHAINS)]
            dps = [_dot_nt(do_ref[rws[c], :], v_ref[0:nks[c], :]) for c in range(ATTN_CHAINS)]
            for c in range(ATTN_CHAINS):
                rows, nk, s, dp = rws[c], nks[c], scores[c], dps[c]
                kb = k_ref[0:nk, :]
                qb = q_ref[rows, :]
                dob = do_ref[rows, :]
                if masked:
                    ri = lax.broadcasted_iota(jnp.int32, (sub, nk), 0) + c * sub
                    s = jnp.where(ri >= lax.broadcasted_iota(jnp.int32, (sub, nk), 1), s, NEG)
                p = jnp.exp(s - lse_ref[rows, 0:1])
                dv_acc = dv_acc + grow(_dot_tn(p.astype(BF16), dob))
                ds = (p * (dp - dl_ref[rows, 0:1])).astype(BF16)
                dk_acc = dk_acc + grow(_dot_tn(ds, qb))
                dq_ref[rows, :] += _dot(ds, kb)
            return dk_acc, dv_acc

        carry = step((jnp.zeros((tq, HP), F32), jnp.zeros((tq, HP), F32)), ki, True)
        dk_acc, dv_acc = lax.fori_loop(ki + 1, nq, lambda j, cr: step(cr, j, False), carry)
        dk_ref[...] = dk_acc
        dv_ref[...] = dv_acc.astype(BF16)

    kspec = pl.BlockSpec((tq, HP), lambda h, j: (j, h))
    full = pl.BlockSpec((T, HP), lambda h, j: (0, h))
    return _call(
        body, name=name, grid=(H, nq),
        in_specs=[kspec, kspec, full, full, full, full], out_specs=[full, kspec, kspec],
        out_shape=[jax.ShapeDtypeStruct((T, H * HP), F32), jax.ShapeDtypeStruct((T, H * HP), F32),
                   jax.ShapeDtypeStruct((T, H * HP), BF16)],
        scratch_shapes=[], args=(k, v, q, do, lse, delta), exchange=exchange, after=after)


def _ret_consts(cc, hd):
    lg = math.log(1.0 - 2.0 ** (-5.0 - hd))
    diff = (lax.broadcasted_iota(jnp.int32, (cc, cc), 0) - lax.broadcasted_iota(jnp.int32, (cc, cc), 1)).astype(F32)
    decay = jnp.where(diff >= 0, jnp.exp(jnp.maximum(diff, 0.0) * lg), 0.0)
    idx = lax.broadcasted_iota(jnp.int32, (cc, 1), 0).astype(F32)
    zeta = jnp.exp((cc - 1.0 - idx) * lg)
    xi = jnp.exp((idx + 1.0) * lg)
    return decay, zeta, xi, math.exp(cc * lg)


def _ret_fwd(proj, pos, tab, *, name):
    T = proj.shape[0]
    cc = min(RET_TILE, T)
    n = T // cc

    def body(rq_ref, rk_ref, rv_ref, pos_ref, tab_ref, y_ref, yn_ref, rprev_ref, r_s):
        @pl.when(pl.program_id(0) == 0)
        def _():
            r_s[...] = jnp.zeros_like(r_s)

        cs = _rope_cs(pos_ref[...], tab_ref)
        for hd in range(RET_HEADS):
            sl = slice(hd * HP, (hd + 1) * HP)
            decay, zeta, xi, gc = _ret_consts(cc, hd)
            q = _rope(rq_ref[:, sl].astype(F32), cs, RET_DK // 2).astype(BF16)
            kf = _rope(rk_ref[:, sl].astype(F32), cs, RET_DK // 2) * (RET_DK ** -0.5)
            k = kf.astype(BF16)
            v = rv_ref[:, sl]
            r = r_s[hd]
            rprev_ref[0, hd] = r
            inner = (_dot_nt(q, k) * decay).astype(BF16)
            y = _dot(inner, v) + _dot(q, r.astype(BF16)) * xi
            r_s[hd] = r * gc + _dot_tn((kf * zeta).astype(BF16), v)
            y_ref[:, sl] = y
            mu = jnp.mean(y, axis=-1, keepdims=True)
            yc = y - mu
            var = jnp.mean(yc * yc, axis=-1, keepdims=True)
            yn_ref[:, sl] = (yc * lax.rsqrt(var + GN_EPS)).astype(BF16)

    def blk(j):
        return pl.BlockSpec((cc, RW), lambda i: (i, j))

    return pl.pallas_call(
        body, name=name, grid=(n,),
        in_specs=[blk(0), blk(1), blk(2), pl.BlockSpec((cc, 1), lambda i: (i, 0)),
                  pl.BlockSpec((8, LANES), lambda i: (0, 0))],
        out_specs=[blk(0), blk(0), pl.BlockSpec((1, RET_HEADS, HP, RET_DV), lambda i: (i, 0, 0, 0))],
        out_shape=[jax.ShapeDtypeStruct((T, RW), F32), jax.ShapeDtypeStruct((T, RW), BF16),
                   jax.ShapeDtypeStruct((n, RET_HEADS, HP, RET_DV), F32)],
        scratch_shapes=[pltpu.VMEM((RET_HEADS, HP, RET_DV), F32)],
        compiler_params=_params(("arbitrary",)),
    )(proj, proj, proj, pos, tab)


def _ret_bwd(dyn, y, proj, pos, tab, rprev, *, name):
    T = proj.shape[0]
    cc = min(RET_TILE, T)
    n = T // cc

    def body(dyn_ref, y_ref, rq_ref, rk_ref, rv_ref, pos_ref, tab_ref, rprev_ref,
             drq_ref, drk_ref, drv_ref, dr_s):
        @pl.when(pl.program_id(0) == 0)
        def _():
            dr_s[...] = jnp.zeros_like(dr_s)

        cs = _rope_cs(pos_ref[...], tab_ref)
        for hd in range(RET_HEADS):
            sl = slice(hd * HP, (hd + 1) * HP)
            decay, zeta, xi, gc = _ret_consts(cc, hd)
            q = _rope(rq_ref[:, sl].astype(F32), cs, RET_DK // 2).astype(BF16)
            kf = _rope(rk_ref[:, sl].astype(F32), cs, RET_DK // 2) * (RET_DK ** -0.5)
            k = kf.astype(BF16)
            v = rv_ref[:, sl]
            yv = y_ref[:, sl]
            mu = jnp.mean(yv, axis=-1, keepdims=True)
            yc = yv - mu
            rs = lax.rsqrt(jnp.mean(yc * yc, axis=-1, keepdims=True) + GN_EPS)
            yn = yc * rs
            dn = dyn_ref[:, sl]
            dy = rs * (dn - jnp.mean(dn, axis=-1, keepdims=True) - yn * jnp.mean(dn * yn, axis=-1, keepdims=True))
            dyb = dy.astype(BF16)
            dyx = (dy * xi).astype(BF16)
            dr = dr_s[hd]
            drb = dr.astype(BF16)
            inner = (_dot_nt(q, k) * decay).astype(BF16)
            da = (_dot_nt(dyb, v) * decay).astype(BF16)
            dv = _dot_tn(inner, dyb) + _dot((kf * zeta).astype(BF16), drb)
            dq = _dot(da, k) + _dot_nt(dyx, rprev_ref[0, hd].astype(BF16))
            dk = _dot_tn(da, q) + _dot_nt(v, drb) * zeta
            dr_s[hd] = dr * gc + _dot_tn(q, dyx)
            drq_ref[:, sl] = _rope(dq, cs, RET_DK // 2, inverse=True).astype(BF16)
            drk_ref[:, sl] = _rope(dk * (RET_DK ** -0.5), cs, RET_DK // 2, inverse=True).astype(BF16)
            drv_ref[:, sl] = dv.astype(BF16)

    def blk(j):
        return pl.BlockSpec((cc, RW), lambda i: (n - 1 - i, j))

    return pl.pallas_call(
        body, name=name, grid=(n,),
        in_specs=[blk(0), blk(0), blk(0), blk(1), blk(2), pl.BlockSpec((cc, 1), lambda i: (n - 1 - i, 0)),
                  pl.BlockSpec((8, LANES), lambda i: (0, 0)),
                  pl.BlockSpec((1, RET_HEADS, HP, RET_DV), lambda i: (n - 1 - i, 0, 0, 0))],
        out_specs=[blk(0), blk(0), blk(0)],
        out_shape=[jax.ShapeDtypeStruct((T, RW), BF16)] * 3,
        scratch_shapes=[pltpu.VMEM((RET_HEADS, HP, RET_DV), F32)],
        compiler_params=_params(("arbitrary",)),
    )(dyn, y, proj, proj, proj, pos, tab, rprev)


def _merge_fwd(o, yn, proj, gn_w, w_bm, w_br, w_out, h, post_w, *, name):
    T, D = h.shape
    tT = min(TOKEN_TILE, T)
    g_blk = PROJ_FIXED // D

    def body(o_ref, yn_ref, rg_ref, gm_ref, gr_ref, gnw_ref, wbm_ref, wbr_ref, wout_ref, h_ref, post_ref,
             omla_ref, oret_ref, m_ref, ho_ref):
        groups = [slice(c * (tT // FFN_CHAINS), (c + 1) * (tT // FFN_CHAINS)) for c in range(FFN_CHAINS)]
        o_mlas = [_dot(o_ref[rs, :], wbm_ref[...]) for rs in groups]
        for rs, o_mla in zip(groups, o_mlas):
            rg = rg_ref[rs, :].astype(F32)
            gated = (rg * _sigmoid(rg) * (yn_ref[rs, :].astype(F32) * gnw_ref[...])).astype(BF16)
            o_ret = _dot(gated, wbr_ref[...])
            omla_ref[rs, :] = o_mla.astype(BF16)
            oret_ref[rs, :] = o_ret.astype(BF16)
            merged = _sigmoid(gm_ref[rs, :].astype(F32)) * o_mla + _sigmoid(gr_ref[rs, :].astype(F32)) * o_ret
            m = _dot(merged.astype(BF16), wout_ref[...])
            m_ref[rs, :] = m
            ho_ref[rs, :] = h_ref[rs, :] + _rms_fwd(m, post_ref[...])

    def full(r, c):
        return pl.BlockSpec((r, c), lambda i: (0, 0))

    def rows(c, j=0):
        return pl.BlockSpec((tT, c), lambda i: (i, j))

    return pl.pallas_call(
        body, name=name, grid=(T // tT,),
        in_specs=[rows(QW), rows(RW), rows(RW, 3), rows(D, g_blk), rows(D, g_blk + 1), full(1, RW),
                  full(QW, D), full(RW, D), full(D, D), rows(D), full(1, D)],
        out_specs=[rows(D), rows(D), rows(D), rows(D)],
        out_shape=[jax.ShapeDtypeStruct((T, D), BF16), jax.ShapeDtypeStruct((T, D), BF16),
                   jax.ShapeDtypeStruct((T, D), F32), jax.ShapeDtypeStruct((T, D), F32)],
        compiler_params=_params(("parallel",)),
    )(o, yn, proj, proj, proj, gn_w, w_bm, w_br, w_out, h, post_w)


def _merge_bwd(dho, m, post_w, omla, oret, proj, yn, gn_w, o, w_out, w_bm, w_br, *, name):
    T, D = dho.shape
    tT = min(MERGE_TILE, T)
    g_blk = PROJ_FIXED // D

    nT = T // tT

    def body(dho_ref, m_ref, post_ref, omla_ref, oret_ref, rg_ref, gm_ref, gr_ref, yn_ref, gnw_ref, o_ref,
             wout_ref, wbm_ref, wbr_ref,
             dgm_ref, dgr_ref, do_ref, delta_ref, drg_ref, dyn_ref, gpost_ref, ggn_ref,
             dwout_ref, dwbm_ref, dwbr_ref, acc_out, acc_bm, acc_br):
        @pl.when(pl.program_id(0) == 0)
        def _():
            gpost_ref[...] = jnp.zeros_like(gpost_ref)
            ggn_ref[...] = jnp.zeros_like(ggn_ref)
            acc_out[...] = jnp.zeros_like(acc_out)
            acc_bm[...] = jnp.zeros_like(acc_bm)
            acc_br[...] = jnp.zeros_like(acc_br)

        dm, gp = _rms_bwd(m_ref[...], post_ref[...], dho_ref[...])
        gpost_ref[...] += gp
        dmb = dm.astype(BF16)
        dmerged = _dot_nt(dmb, wout_ref[...])
        o_mla = omla_ref[...].astype(F32)
        o_ret = oret_ref[...].astype(F32)
        sgm = _sigmoid(gm_ref[...].astype(F32))
        sgr = _sigmoid(gr_ref[...].astype(F32))
        acc_out[...] += _dot_tn((sgm * o_mla + sgr * o_ret).astype(BF16), dmb)
        dgm_ref[...] = (dmerged * o_mla * sgm * (1.0 - sgm)).astype(BF16)
        dgr_ref[...] = (dmerged * o_ret * sgr * (1.0 - sgr)).astype(BF16)
        domla = (dmerged * sgm).astype(BF16)
        acc_bm[...] += _dot_tn(o_ref[...], domla)
        do = _dot_nt(domla, wbm_ref[...])
        do_ref[...] = do.astype(BF16)
        for hd in range(MLA_HEADS):
            sl = slice(hd * HP, (hd + 1) * HP)
            d = jnp.sum(do[:, sl] * o_ref[:, sl].astype(F32), axis=-1, keepdims=True)
            delta_ref[:, sl] = jnp.broadcast_to(d, (tT, HP))
        doret = (dmerged * sgr).astype(BF16)
        dgated = _dot_nt(doret, wbr_ref[...])
        rg = rg_ref[...].astype(F32)
        sg = _sigmoid(rg)
        srg = rg * sg
        ynv = yn_ref[...].astype(F32)
        yw = ynv * gnw_ref[...]
        acc_br[...] += _dot_tn((srg * yw).astype(BF16), doret)
        drg_ref[...] = (dgated * yw * (sg * (1.0 + rg * (1.0 - sg)))).astype(BF16)
        dgs = dgated * srg
        dyn_ref[...] = dgs * gnw_ref[...]
        ggn_ref[...] += jnp.sum(dgs * ynv, axis=0, keepdims=True)

        @pl.when(pl.program_id(0) == nT - 1)
        def _():
            dwout_ref[...] = acc_out[...].astype(BF16)
            dwbm_ref[...] = acc_bm[...].astype(BF16)
            dwbr_ref[...] = acc_br[...].astype(BF16)

    def full(r, c):
        return pl.BlockSpec((r, c), lambda i: (0, 0), pipeline_mode=pl.Buffered(1))

    def rows(c, j=0):
        return pl.BlockSpec((tT, c), lambda i: (i, j))

    return pl.pallas_call(
        body, name=name, grid=(nT,),
        in_specs=[rows(D), rows(D), full(1, D), rows(D), rows(D), rows(RW, 3), rows(D, g_blk), rows(D, g_blk + 1),
                  rows(RW), full(1, RW), rows(QW), full(D, D), full(QW, D), full(RW, D)],
        out_specs=[rows(D), rows(D), rows(QW), rows(QW), rows(RW), rows(RW), full(1, D), full(1, RW),
                   full(D, D), full(QW, D), full(RW, D)],
        out_shape=[jax.ShapeDtypeStruct((T, D), BF16)] * 2
        + [jax.ShapeDtypeStruct((T, QW), BF16), jax.ShapeDtypeStruct((T, QW), F32),
           jax.ShapeDtypeStruct((T, RW), BF16), jax.ShapeDtypeStruct((T, RW), F32),
           jax.ShapeDtypeStruct((1, D), F32), jax.ShapeDtypeStruct((1, RW), F32),
           jax.ShapeDtypeStruct((D, D), BF16), jax.ShapeDtypeStruct((QW, D), BF16), jax.ShapeDtypeStruct((RW, D), BF16)],
        scratch_shapes=[pltpu.VMEM((D, D), F32), pltpu.VMEM((QW, D), F32), pltpu.VMEM((RW, D), F32)],
        compiler_params=_params(("arbitrary",)),
    )(dho, m, post_w, omla, oret, proj, proj, proj, yn, gn_w, o, w_out, w_bm, w_br)


def _mesh_pos():
    return lax.axis_index("x"), lax.axis_index("y"), lax.axis_index("c")


class _Gather:
    def __init__(self, shards):
        self.operands = list(shards)
        self.n = len(shards)
        self.out_shape = [jax.ShapeDtypeStruct((N_DEV,) + s.shape, s.dtype) for s in shards]
        self.scratch = [pltpu.SemaphoreType.DMA((7 * self.n,)), pltpu.SemaphoreType.DMA((7 * self.n,)),
                        pltpu.SemaphoreType.DMA((self.n,))]

    def phase(self, p, x_refs, out_refs, sems):
        send_sems, recv_sems, local_sems = sems
        x, y, c = _mesh_pos()
        me, sibling = (x, y, c), (x, y, 1 - c)
        chips = [(1 - x, y), (x, 1 - y), (1 - x, 1 - y)]

        def copy(w, k, block, to, src=None):
            slot = out_refs[w].at[4 * block[0] + 2 * block[1] + block[2]]
            return pltpu.make_async_remote_copy(
                src_ref=slot if src is None else src, dst_ref=slot,
                send_sem=send_sems.at[7 * w + k], recv_sem=recv_sems.at[7 * w + k],
                device_id=to, device_id_type=pl.DeviceIdType.MESH)

        for w in range(self.n):
            mine = pltpu.make_async_copy(x_refs[w], out_refs[w].at[4 * x + 2 * y + c], local_sems.at[w])
            first = [copy(w, 0, me, sibling, src=x_refs[w])]
            first += [copy(w, 1 + j, me, (*chip, c), src=x_refs[w]) for j, chip in enumerate(chips)]
            passed = [copy(w, 4 + j, (*chip, c), sibling) for j, chip in enumerate(chips)]
            if p == 0:
                mine.start()
                for cp in first:
                    cp.start()
            elif p == 1:
                for j, chip in enumerate(chips):
                    copy(w, 1 + j, (*chip, c), me).wait_recv()
                    passed[j].start()
            else:
                copy(w, 0, sibling, me).wait_recv()
                for j, chip in enumerate(chips):
                    copy(w, 4 + j, (*chip, 1 - c), me).wait_recv()
                for cp in first + passed:
                    cp.wait_send()
                mine.wait()


class _Scatter:
    def __init__(self, grads, whole=(), axes=("x", "y", "c")):
        self.axes = axes
        self.slots = 2 ** len(axes)
        self.flips = [r for r in range(1, N_DEV)
                      if not (r & 4 and "x" not in axes) and not (r & 2 and "y" not in axes) and not (r & 1 and "c" not in axes)]
        self.n_sliced = len(grads)
        self.operands = list(grads) + list(whole)
        self.n = len(self.operands)
        self.out_shape = [jax.ShapeDtypeStruct(g.shape, g.dtype) for g in grads]
        self.out_shape += [jax.ShapeDtypeStruct((self.slots,) + a.shape, a.dtype) for a in whole]
        n_sem = len(self.flips) * self.n
        self.scratch = [pltpu.SemaphoreType.DMA((n_sem,)), pltpu.SemaphoreType.DMA((n_sem,)),
                        pltpu.SemaphoreType.DMA((self.n,))]

    def _slot(self, x, y, c):
        idx = 0
        for name, coord in (("x", x), ("y", y), ("c", c)):
            if name in self.axes:
                idx = 2 * idx + coord
        return idx

    def phase(self, p, in_refs, out_refs, sems):
        if p == 1:
            return
        send_sems, recv_sems, local_sems = sems
        x, y, c = _mesh_pos()
        me = self._slot(x, y, c)

        def src(w, dev):
            return in_refs[w].at[dev] if w < self.n_sliced else in_refs[w]

        for w in range(self.n):
            own = None if local_sems is None else pltpu.make_async_copy(src(w, me), out_refs[w].at[me], local_sems.at[w])
            sends, recvs = [], []
            for j, r in enumerate(self.flips):
                px = 1 - x if r & 4 else x
                py = 1 - y if r & 2 else y
                pc = 1 - c if r & 1 else c
                peer, pidx = (px, py, pc), self._slot(px, py, pc)
                k = len(self.flips) * w + j
                sends.append(pltpu.make_async_remote_copy(
                    src_ref=src(w, pidx), dst_ref=out_refs[w].at[me], send_sem=send_sems.at[k],
                    recv_sem=recv_sems.at[k], device_id=peer, device_id_type=pl.DeviceIdType.MESH))
                recvs.append(pltpu.make_async_remote_copy(
                    src_ref=src(w, me), dst_ref=out_refs[w].at[pidx], send_sem=send_sems.at[k],
                    recv_sem=recv_sems.at[k], device_id=peer, device_id_type=pl.DeviceIdType.MESH))
            if p == 0:
                if own is not None:
                    own.start()
                for cp in sends:
                    cp.start()
            else:
                for cp in recvs:
                    cp.wait_recv()
                for cp in sends:
                    cp.wait_send()
                if own is not None:
                    own.wait()


class _SplitScatter:
    def __init__(self, ex, name):
        self.ex, self.name = ex, name

    def _specs(self):
        ex = self.ex
        hbm = pl.BlockSpec(memory_space=pltpu.HBM)
        sem = pl.BlockSpec(memory_space=pltpu.SEMAPHORE)
        effect = pltpu.CompilerParams(has_side_effects=pltpu.SideEffectType.DATAFLOW_SIDE_EFFECTING)
        buffers = [pltpu.HBM(a.shape, a.dtype) for a in ex.operands] + [pltpu.HBM(s.shape, s.dtype) for s in ex.out_shape]
        return hbm, sem, effect, buffers

    def start(self):
        ex, n = self.ex, self.ex.n
        n_sem = len(ex.flips) * n
        hbm, sem, effect, buffers = self._specs()
        in_hbm = lambda a: pltpu.with_memory_space_constraint(a, pltpu.HBM)

        me = ex._slot(*_mesh_pos())
        lands = []
        for w, (a, s) in enumerate(zip(ex.operands, ex.out_shape)):
            mine = lax.dynamic_index_in_dim(a, me, 0, keepdims=True) if w < ex.n_sliced else a[None]
            lands.append(lax.dynamic_update_slice_in_dim(lax.empty(s.shape, s.dtype), mine, me, 0))

        def start_body(*refs):
            ex.phase(0, refs[:n], refs[n:2 * n], (refs[2 * n], refs[2 * n + 1], None))
            refs[-1][...] = jnp.zeros_like(refs[-1])

        self.started = pl.pallas_call(
            start_body, name=self.name + "_start",
            out_shape=[pltpu.SemaphoreType.DMA((n_sem,)), pltpu.SemaphoreType.DMA((n_sem,))] + buffers
            + [jax.ShapeDtypeStruct((8, LANES), F32)],
            in_specs=[hbm] * (2 * n), out_specs=[sem, sem] + [hbm] * (2 * n) + [pl.BlockSpec(memory_space=pltpu.VMEM)],
            input_output_aliases={i: 2 + i for i in range(2 * n)}, compiler_params=effect,
        )(*[in_hbm(a) for a in ex.operands], *[in_hbm(a) for a in lands])
        return self.started[-1]

    def wait(self, after):
        ex, n = self.ex, self.ex.n
        hbm, sem, effect, buffers = self._specs()
        anyspec = pl.BlockSpec(memory_space=pl.ANY)

        def wait_body(*refs):
            ex.phase(2, refs[:n], refs[n:2 * n], (refs[2 * n], refs[2 * n + 1], None))

        done = pl.pallas_call(
            wait_body, name=self.name + "_wait", out_shape=buffers,
            in_specs=[hbm] * (2 * n) + [sem, sem] + [anyspec] * len(after), out_specs=[hbm] * (2 * n),
            input_output_aliases={i: i for i in range(2 * n)}, compiler_params=effect,
        )(*self.started[2:2 + 2 * n], self.started[0], self.started[1], *after)
        return done[n:]


def _pair_sum(pair, *, name):
    _, G, K, n = pair.shape
    tk = _tile(K, 256, 16)

    def body(p_ref, o_ref):
        o_ref[...] = (p_ref[0].astype(F32) + p_ref[1].astype(F32)).astype(o_ref.dtype)

    return pl.pallas_call(
        body, name=name, grid=(G, K // tk),
        in_specs=[pl.BlockSpec((2, None, tk, n), lambda g, i: (0, g, i, 0))],
        out_specs=pl.BlockSpec((None, tk, n), lambda g, i: (g, i, 0)),
        out_shape=jax.ShapeDtypeStruct((G, K, n), pair.dtype),
        compiler_params=_params(("parallel", "parallel")),
    )(pair)


def _exchange_alone(ex, *, name):
    n = ex.n

    def body(*refs):
        for p in range(3):
            ex.phase(p, refs[:n], refs[n:2 * n], refs[2 * n:])

    anyspec = pl.BlockSpec(memory_space=pl.ANY)
    return pl.pallas_call(body, name=name, out_shape=ex.out_shape, in_specs=[anyspec] * n,
                          out_specs=[anyspec] * n, scratch_shapes=ex.scratch)(*ex.operands)


def _adam_step(w_ref, p_ref, m_ref, v_ref, g_ref, d_ref, nm_ref, nv_ref):
    g = p_ref[0].astype(F32)
    for j in range(1, p_ref.shape[0]):
        g = g + p_ref[j].astype(F32)
    g_ref[...] = g
    nm = ADAM_B1 * m_ref[...] + (1.0 - ADAM_B1) * g
    nv = ADAM_B2 * v_ref[...] + (1.0 - ADAM_B2) * (g * g)
    nm_ref[...] = nm
    nv_ref[...] = nv
    m_hat = nm / (1.0 - ADAM_B1 ** ADAM_STEP)
    v_hat = nv / (1.0 - ADAM_B2 ** ADAM_STEP)
    d_ref[...] = -ADAM_LR * (m_hat / (jnp.sqrt(v_hat) + ADAM_EPS) + ADAM_WD * w_ref[...])


def _adamw_vectors(ws, parts, ms, vs, *, name):
    n = len(ws)

    def body(*refs):
        w_refs, p_refs, m_refs, v_refs = (refs[i * n:(i + 1) * n] for i in range(4))
        outs = refs[4 * n:]
        for i in range(n):
            _adam_step(w_refs[i], p_refs[i], m_refs[i], v_refs[i], *outs[4 * i:4 * i + 4])

    return pl.pallas_call(
        body, name=name,
        out_shape=[jax.ShapeDtypeStruct(w.shape, F32) for w in ws for _ in range(4)],
    )(*ws, *parts, *ms, *vs)


def _adamw(w, parts, m, v, after, *, name):
    G, R, n = w.shape
    tn = 512 if (n > 512 and n % 512 == 0) else n
    tr = R
    for t in range(16, R, 16):
        if R % t == 0 and t * tn <= ADAM_BLOCK_CAP:
            tr = t
    if R * tn <= ADAM_BLOCK_CAP:
        tr = R

    def body(w_ref, p_ref, m_ref, v_ref, after_ref, g_ref, d_ref, nm_ref, nv_ref):
        _adam_step(w_ref, p_ref, m_ref, v_ref, g_ref, d_ref, nm_ref, nv_ref)

    blk = pl.BlockSpec((None, tr, tn), lambda g, i, j: (g, i, j))
    return pl.pallas_call(
        body, name=name, grid=(G, R // tr, n // tn),
        in_specs=[blk, pl.BlockSpec((parts.shape[0], None, tr, tn), lambda g, i, j: (0, g, i, j)), blk, blk,
                  pl.BlockSpec((8, LANES), lambda g, i, j: (0, 0))],
        out_specs=[blk, blk, blk, blk],
        out_shape=[jax.ShapeDtypeStruct((G, R, n), F32)] * 4,
        compiler_params=_params(("parallel", "parallel", "parallel")),
    )(w, parts, m, v, after)


def _pad_last(a, width):
    return jnp.pad(a, [(0, 0)] * (a.ndim - 1) + [(0, width - a.shape[-1])])


def _cols_of(g):
    return g.transpose(1, 0, 2).reshape(g.shape[1], N_DEV * g.shape[2])


def _col_shards(w):
    return w.reshape(w.shape[0], N_DEV, w.shape[1] // N_DEV).transpose(1, 0, 2)


def kernel(x, positions, ffn1_pre_w, ffn1_w1, ffn1_w2, ffn1_post_w, mix_pre_w, w_in, mla_q_norm_w, mla_w_uq, mla_kv_norm_w, mla_w_ukv, ret_gn_w, w_branch_mla, w_branch_ret, w_out, mix_post_w, ffn2_pre_w, ffn2_w1, ffn2_w2, ffn2_post_w, loss_target, m_ffn1_pre_w, m_ffn1_w1, m_ffn1_w2, m_ffn1_post_w, m_mix_pre_w, m_w_in, m_mla_q_norm_w, m_mla_w_uq, m_mla_kv_norm_w, m_mla_w_ukv, m_ret_gn_w, m_w_branch_mla, m_w_branch_ret, m_w_out, m_mix_post_w, m_ffn2_pre_w, m_ffn2_w1, m_ffn2_w2, m_ffn2_post_w, v_ffn1_pre_w, v_ffn1_w1, v_ffn1_w2, v_ffn1_post_w, v_mix_pre_w, v_w_in, v_mla_q_norm_w, v_mla_w_uq, v_mla_kv_norm_w, v_mla_w_ukv, v_ret_gn_w, v_w_branch_mla, v_w_branch_ret, v_w_out, v_mix_post_w, v_ffn2_pre_w, v_ffn2_w1, v_ffn2_w2, v_ffn2_post_w):
    T, D = x.shape[1], x.shape[2]
    h0 = x[0]
    tgt = loss_target[0]
    pos = positions.reshape(T, 1).astype(F32)

    big = [("ffn1_w1", ffn1_w1, m_ffn1_w1, v_ffn1_w1), ("ffn1_w2", ffn1_w2, m_ffn1_w2, v_ffn1_w2),
           ("w_in", w_in, m_w_in, v_w_in), ("mla_w_uq", mla_w_uq, m_mla_w_uq, v_mla_w_uq),
           ("mla_w_ukv", mla_w_ukv, m_mla_w_ukv, v_mla_w_ukv),
           ("w_branch_mla", w_branch_mla, m_w_branch_mla, v_w_branch_mla),
           ("w_branch_ret", w_branch_ret, m_w_branch_ret, v_w_branch_ret),
           ("w_out", w_out, m_w_out, v_w_out),
           ("ffn2_w1", ffn2_w1, m_ffn2_w1, v_ffn2_w1), ("ffn2_w2", ffn2_w2, m_ffn2_w2, v_ffn2_w2)]
    small = [("ffn1_pre_w", ffn1_pre_w, m_ffn1_pre_w, v_ffn1_pre_w), ("ffn1_post_w", ffn1_post_w, m_ffn1_post_w, v_ffn1_post_w),
             ("mix_pre_w", mix_pre_w, m_mix_pre_w, v_mix_pre_w), ("mla_q_norm_w", mla_q_norm_w, m_mla_q_norm_w, v_mla_q_norm_w),
             ("mla_kv_norm_w", mla_kv_norm_w, m_mla_kv_norm_w, v_mla_kv_norm_w), ("ret_gn_w", ret_gn_w, m_ret_gn_w, v_ret_gn_w),
             ("mix_post_w", mix_post_w, m_mix_post_w, v_mix_post_w), ("ffn2_pre_w", ffn2_pre_w, m_ffn2_pre_w, v_ffn2_pre_w),
             ("ffn2_post_w", ffn2_post_w, m_ffn2_post_w, v_ffn2_post_w)]

    half = ffn1_w2.shape[1]
    hp = -(-half // LANES) * LANES

    def rows_view(w):
        return w[0].T

    def send_w1(w):
        return jnp.pad(rows_view(w).reshape(2, half, D), ((0, 0), (0, hp - half), (0, 0))).reshape(2 * hp, D).astype(BF16)

    def send_w2(w):
        return jnp.pad(w[0], ((0, hp - half), (0, 0))).astype(BF16)

    mixer = ["w_in", "mla_w_uq", "mla_w_ukv", "w_branch_mla", "w_branch_ret", "w_out"]
    uq_w = MLA_NOPE + MLA_ROPE
    mixer_send = [rows_view(w_in).astype(BF16), jnp.pad(rows_view(mla_w_uq), ((0, HP - uq_w), (0, 0))).astype(BF16),
                  mla_w_ukv[0].astype(BF16), w_branch_mla[0].astype(BF16), w_branch_ret[0].astype(BF16),
                  w_out[0].astype(BF16)]

    w1a, w2a = _exchange_alone(_Gather([send_w1(ffn1_w1), send_w2(ffn1_w2)]), name="gather_ffn1")
    w2a = w2a.reshape(N_DEV // 2, 2 * hp, D)
    u1, f1, h1, a0, *got = _ffn_fwd(h0, ffn1_pre_w, w1a, w2a, ffn1_post_w, None, name="ffn1_fwd_gather_mixer",
                                exchange=_Gather(mixer_send))
    fw = dict(zip(mixer, got))

    wi = fw["w_in"].reshape(-1, D)
    cq_w, ckv_w, kr_w = wi[0:384], wi[384:640], wi[640:672]
    rq_w, rk_w = wi[672:928], wi[928:1184]
    rv_w, rg_w = wi[1184:1696], wi[1696:2208]
    gm_w, gr_w = wi[2208:2208 + D], wi[2208 + D:2208 + 2 * D]
    zer = lambda n: jnp.zeros((n, D), BF16)
    head_rows = lambda a, h: jnp.pad(a.reshape(h, -1, D), ((0, 0), (0, HP - a.shape[0] // h), (0, 0))).reshape(h * HP, D)
    w_in_p = jnp.concatenate([head_rows(rq_w, RET_HEADS), head_rows(rk_w, RET_HEADS), rv_w, rg_w,
                              cq_w, ckv_w, zer(MLA_NOPE), kr_w, zer(HP - MLA_NOPE - MLA_ROPE), zer(AW - 768),
                              gm_w, gr_w], axis=0)
    w_uq_p = fw["mla_w_uq"].reshape(QW, MLA_Q_RANK)
    ukv = fw["mla_w_ukv"].transpose(1, 0, 2)
    w_kv_p = jnp.concatenate([_pad_last(ukv[:, :, :MLA_NOPE], HP).reshape(MLA_KV_RANK, QW),
                              _pad_last(ukv[:, :, MLA_NOPE:], HP).reshape(MLA_KV_RANK, QW)], axis=1)
    w_bm_p = jnp.pad(_cols_of(fw["w_branch_mla"]).reshape(MLA_HEADS, MLA_V, D),
                     ((0, 0), (0, HP - MLA_V), (0, 0))).reshape(QW, D)
    w_br, w_o = _cols_of(fw["w_branch_ret"]), fw["w_out"].reshape(D, D)
    tab_mla = _rope_table(MLA_NOPE, MLA_ROPE // 2)
    tab_ret = _rope_table(0, RET_DK // 2)

    proj, a1 = _rms_matmul(h1, mix_pre_w, w_in_p, name="mixer_in_proj")
    q, k, v = _mla_prep_fwd(proj, pos, mla_q_norm_w, mla_kv_norm_w, w_uq_p, w_kv_p, tab_mla, name="mla_prep_fwd")
    o, lse, w1b, w2b = _flash_fwd(q, k, v, name="mla_attn_fwd_gather_ffn2",
                                  exchange=_Gather([send_w1(ffn2_w1), send_w2(ffn2_w2)]))
    w2b = w2b.reshape(N_DEV // 2, 2 * hp, D)
    ypre, yn, rprev = _ret_fwd(proj, pos, tab_ret, name="retention_fwd")
    omla, oret, m, h2 = _merge_fwd(o, yn, proj, ret_gn_w, w_bm_p, w_br, w_o, h1, mix_post_w, name="merge_fwd")
    u2, f2, _, a2, dy, lossp = _ffn_fwd(h2, ffn2_pre_w, w1b, w2b, ffn2_post_w, tgt, name="ffn2_fwd_loss")

    def grad(x, dy, tag, after=None, **kw):
        return _matmul_tn(x if x.ndim == 3 else x[None], dy if dy.ndim == 3 else dy[None], name=tag, after=after, **kw)

    g2, du2, df2, dh2, gpost2, gpre2 = _ffn_bwd(dy, f2, ffn2_post_w, h2, ffn2_pre_w, u2, w2b, w1b, name="ffn2_bwd")
    dw1b, = grad(du2.reshape(N_DEV, T, 2 * hp), a2, "ffn2_dw1")
    dw2b = grad(g2, df2, "ffn2_dw2")[0].reshape(N_DEV, hp, D)
    (dgm, dgr, do, delta, drg, dyn, gpostm, ggn, dw_out, dw_bm_p, dw_br) = _merge_bwd(
        dh2, m, mix_post_w, omla, oret, proj, yn, ret_gn_w, o, w_o, w_bm_p, w_br, name="merge_bwd")
    sc_ffn2 = _SplitScatter(_Scatter([dw1b, dw2b]), "scatter_ffn2")
    dq, dk, dv = _flash_bwd(q, k, v, do, lse, delta, name="mla_attn_bwd", after=sc_ffn2.start())
    da, gqn, gkvn, dw_uq_p, dw_kv_p = _mla_prep_bwd(dq, dk, dv, proj, pos, mla_q_norm_w, mla_kv_norm_w, w_uq_p, w_kv_p, tab_mla, name="mla_prep_bwd")
    drq, drk, drv = _ret_bwd(dyn, ypre, proj, pos, tab_ret, rprev, name="retention_bwd")
    dproj = jnp.concatenate([drq, drk, drv, drg, da, dgm, dgr], axis=1)
    dw_in_p = grad(dproj, a1, "dw_in")[0][0]

    dw_uq = dw_uq_p.reshape(MLA_HEADS, HP, MLA_Q_RANK)[:, :uq_w]
    dkp = dw_kv_p[:, :QW].reshape(MLA_KV_RANK, MLA_HEADS, HP)[:, :, :MLA_NOPE]
    dvp = dw_kv_p[:, QW:].reshape(MLA_KV_RANK, MLA_HEADS, HP)[:, :, :MLA_V]
    dw_ukv = jnp.concatenate([dkp, dvp], axis=2).transpose(1, 0, 2)
    dw_bm = dw_bm_p.reshape(MLA_HEADS, HP, D)[:, :MLA_V].reshape(MLA_HEADS * MLA_V, D)
    small_mixer_grads = [dw_uq, dw_ukv, _col_shards(dw_bm), _col_shards(dw_br), dw_out.reshape(N_DEV, D // N_DEV, D)]
    sc_small = _SplitScatter(_Scatter(small_mixer_grads), "scatter_mixer_small")
    dh1, gmixpre = _proj_bwd(dproj, w_in_p, h1, mix_pre_w, dh2, name="mixer_in_bwd", after=sc_small.start())
    unhead = lambda a, h, wd: a.reshape(h, HP, D)[:, :wd].reshape(h * wd, D)
    c0 = 4 * RW
    dw_in = jnp.concatenate([
        dw_in_p[c0:c0 + 384], dw_in_p[c0 + 384:c0 + 640], dw_in_p[c0 + 640 + MLA_NOPE:c0 + 640 + MLA_NOPE + MLA_ROPE],
        unhead(dw_in_p[0:RW], RET_HEADS, RET_DK), unhead(dw_in_p[RW:2 * RW], RET_HEADS, RET_DK),
        dw_in_p[2 * RW:3 * RW], dw_in_p[3 * RW:4 * RW],
        dw_in_p[PROJ_FIXED:PROJ_FIXED + D], dw_in_p[PROJ_FIXED + D:PROJ_FIXED + 2 * D]], axis=0).reshape(N_DEV, -1, D)
    sc_w_in = _SplitScatter(_Scatter([dw_in]), "scatter_w_in")
    g1, du1, df1, dx, gpost1, gpre1 = _ffn_bwd(
        dh1, f1, ffn1_post_w, h0, ffn1_pre_w, u1, w2a, w1a, name="ffn1_bwd", after=sc_w_in.start())
    dw2a = grad(g1, df1, "ffn1_dw2")[0].reshape(N_DEV, hp, D)
    sc_dw2a = _SplitScatter(_Scatter([dw2a]), "scatter_ffn1_dw2")
    dw1a, = grad(du1.reshape(N_DEV, T, 2 * hp), a0, "ffn1_dw1", after=sc_dw2a.start(),
                 out_index=lambda p: (p % 2) * (N_DEV // 2) + p // 2)

    small_g = {"ffn1_pre_w": gpre1, "ffn1_post_w": gpost1, "mix_pre_w": gmixpre, "mla_q_norm_w": gqn,
               "mla_kv_norm_w": gkvn, "ret_gn_w": ggn, "mix_post_w": gpostm, "ffn2_pre_w": gpre2, "ffn2_post_w": gpost2}
    sc_vec = _SplitScatter(_Scatter([], whole=[small_g[nm] for nm, *_ in small] + [lossp]), "scatter_vectors")
    token_vec = sc_vec.start()
    by_core = dw1a.reshape(2, N_DEV // 2, 2 * hp, D)
    swapped, = _exchange_alone(_Scatter([by_core], axes=("c",)), name="swap_ffn1_dw1_cores")
    sc_last = _SplitScatter(_Scatter([_pair_sum(swapped, name="add_ffn1_dw1_cores")], axes=("x", "y")), "scatter_ffn1_dw1")
    token = sc_last.start()
    recv_ffn2 = sc_ffn2.wait([token])
    recv_mixer = sc_w_in.wait([token]) + sc_small.wait([token])
    recv_w2a, = sc_dw2a.wait([token])
    parts = dict(zip(mixer, recv_mixer))
    parts.update(ffn1_w2=recv_w2a, ffn2_w1=recv_ffn2[0], ffn2_w2=recv_ffn2[1])
    as_is = (lambda a: a, lambda p: p[:, None], lambda a: a)
    views = {nm: as_is for nm, *_ in big}
    for nm in ("ffn1_w1", "ffn2_w1"):
        views[nm] = (lambda a: rows_view(a).reshape(2, half, D), lambda p: p.reshape(p.shape[0], 2, hp, D),
                     lambda a: a.reshape(2 * half, D).T[None])
    for nm in ("w_in", "mla_w_uq"):
        views[nm] = (lambda a: rows_view(a)[None], lambda p: p[:, None], lambda a: a[0].T[None])

    def update(nm, w, m_, v_, after):
        to_view, parts_view, back = views[nm]
        return [back(a) for a in _adamw(to_view(w), parts_view(parts[nm]), to_view(m_), to_view(v_), after,
                                        name="adamw_" + nm)]

    big_out = {nm: update(nm, w, m_, v_, token) for nm, w, m_, v_ in big if nm != "ffn1_w1"}
    shadow = [d[0] for d in big_out.values()]
    recv_w1a, = sc_last.wait(shadow)
    *small_parts, loss_parts = sc_vec.wait(shadow + [token_vec])
    loss = jnp.sum(loss_parts[:, ::8, 0])
    parts["ffn1_w1"] = recv_w1a
    big_out["ffn1_w1"] = update("ffn1_w1", ffn1_w1, m_ffn1_w1, v_ffn1_w1, jnp.zeros((8, LANES), F32))
    small_out = _adamw_vectors([w for _, w, _, _ in small], small_parts, [a for _, _, a, _ in small],
                               [a for _, _, _, a in small], name="adamw_replicated")

    order = ["ffn1_pre_w", "ffn1_w1", "ffn1_w2", "ffn1_post_w", "mix_pre_w", "w_in", "mla_q_norm_w", "mla_w_uq",
             "mla_kv_norm_w", "mla_w_ukv", "ret_gn_w", "w_branch_mla", "w_branch_ret", "w_out", "mix_post_w",
             "ffn2_pre_w", "ffn2_w1", "ffn2_w2", "ffn2_post_w"]
    outs = [loss, dx[None]]
    for i in range(4):
        both = {nm: big_out[nm][i] for nm in big_out}
        both.update({nm: small_out[4 * j + i] for j, (nm, *_) in enumerate(small)})
        outs += [both[nm] for nm in order]
    return tuple(outs)
```

```python
import math

import numpy as np
import jax
import jax.numpy as jnp
from jax import lax
from jax.experimental import pallas as pl
from jax.experimental.pallas import tpu as pltpu

F32, BF16 = jnp.float32, jnp.bfloat16

MLA_HEADS, MLA_NOPE, MLA_ROPE, MLA_V = 8, 64, 32, 64
MLA_Q_RANK, MLA_KV_RANK = 384, 256
RET_HEADS, RET_DK, RET_DV = 4, 64, 128
ROPE_BASE, NORM_EPS, GN_EPS = 10000.0, 1e-6, 1e-6
ADAM_LR, ADAM_B1, ADAM_B2, ADAM_EPS, ADAM_WD, ADAM_STEP = 0.001, 0.9, 0.999, 1e-08, 0.01, 10
ATTN_SCALE = 1.0 / math.sqrt(MLA_NOPE + MLA_ROPE)

N_DEV = 8
LANES = 128
HP = LANES
QW = MLA_HEADS * HP
RW = RET_HEADS * HP
AW = 1024
PROJ_FIXED = 4 * RW + AW
NEG = -1e30

TOKEN_TILE = 512
ATTN_TILE = 1024
ATTN_CHAINS = 2
FFN_CHAINS = 2
RET_TILE = 256
PROJ_TILE_CAP = 2560
GRAD_TILE_CAP = 1408
GRAD_TOKEN_TILE = 4096
ADAM_BLOCK_CAP = 192 * 1024
MERGE_TILE = 256
VMEM_LIMIT = 56 * 1024 * 1024


def _tile(n, cap, mult=LANES):
    if n <= cap:
        return n
    best = None
    for t in range(mult, cap + 1, mult):
        if n % t == 0:
            best = t
    assert best is not None, (n, cap, mult)
    return best


def _params(sem):
    return pltpu.CompilerParams(dimension_semantics=sem, vmem_limit_bytes=VMEM_LIMIT)


def _dot(a, b):
    return lax.dot_general(a, b, (((1,), (0,)), ((), ())), preferred_element_type=F32)


def _dot_nt(a, b):
    return lax.dot_general(a, b, (((1,), (1,)), ((), ())), preferred_element_type=F32)


def _dot_tn(a, b):
    return lax.dot_general(a, b, (((0,), (0,)), ((), ())), preferred_element_type=F32)


def _sigmoid(x):
    return pl.reciprocal(1.0 + jnp.exp(-x), approx=True)


def _rms_fwd(x, w):
    r = lax.rsqrt(jnp.mean(x * x, axis=-1, keepdims=True) + NORM_EPS)
    return x * r * w


def _rms_bwd(x, w, dy):
    r = lax.rsqrt(jnp.mean(x * x, axis=-1, keepdims=True) + NORM_EPS)
    xh = x * r
    g = dy * w
    dx = r * (g - xh * jnp.mean(g * xh, axis=-1, keepdims=True))
    return dx, jnp.sum(dy * xh, axis=0, keepdims=True)


def _rope_table(first, half):
    inv = (np.float32(ROPE_BASE) ** (-(np.arange(half, dtype=np.float32) / np.float32(half)))).astype(np.float32)
    tab = np.zeros((8, LANES), np.float32)
    tab[0, first:first + half] = inv
    tab[0, first + half:first + 2 * half] = inv
    tab[1, first:first + half] = -1.0
    tab[2, first + half:first + 2 * half] = 1.0
    return jnp.asarray(tab)


def _rope_cs(pos, tab_ref):
    ang = pos * tab_ref[0:1, :]
    s = jnp.sin(ang)
    return jnp.cos(ang), s * tab_ref[1:2, :], s * tab_ref[2:3, :]


def _rope(x, cs, half, inverse=False):
    c, s1, s2 = cs
    a = pltpu.roll(x, LANES - half, 1) * s1 + pltpu.roll(x, half, 1) * s2
    return x * c - a if inverse else x * c + a


def _call(body, *, name, grid, in_specs, out_specs, out_shape, scratch_shapes, args, exchange=None, after=None):
    sem = ("arbitrary",) * len(grid)
    anyspec = pl.BlockSpec(memory_space=pl.ANY)
    if exchange is None and after is not None:
        n_own = len(in_specs)

        def behind(*refs):
            body(*refs[:n_own], *refs[n_own + 1:])

        return pl.pallas_call(behind, name=name, grid=grid, in_specs=list(in_specs) + [anyspec], out_specs=out_specs,
                              out_shape=out_shape, scratch_shapes=scratch_shapes, compiler_params=_params(sem))(*args, after)
    if exchange is None:
        return pl.pallas_call(body, name=name, grid=grid, in_specs=in_specs, out_specs=out_specs,
                              out_shape=out_shape, scratch_shapes=scratch_shapes, compiler_params=_params(sem))(*args)
    n_in, n_out, e = len(in_specs), len(out_specs), exchange.n
    total = math.prod(grid)

    def carried(*refs):
        own = refs[:n_in] + refs[n_in + e:n_in + e + n_out] + refs[n_in + 2 * e + n_out:len(refs) - 3]
        ex_refs = (refs[n_in:n_in + e], refs[n_in + e + n_out:n_in + 2 * e + n_out], refs[len(refs) - 3:])
        step = pl.program_id(0)
        for d in range(1, len(grid)):
            step = step * grid[d] + pl.program_id(d)

        @pl.when(step == 0)
        def _():
            exchange.phase(0, *ex_refs)

        @pl.when(step == (3 * total) // 4)
        def _():
            exchange.phase(1, *ex_refs)

        body(*own)

        @pl.when(step == total - 1)
        def _():
            exchange.phase(2, *ex_refs)

    return pl.pallas_call(
        carried, name=name, grid=grid, in_specs=list(in_specs) + [anyspec] * e,
        out_specs=list(out_specs) + [anyspec] * e, out_shape=list(out_shape) + exchange.out_shape,
        scratch_shapes=list(scratch_shapes) + exchange.scratch, compiler_params=_params(sem),
    )(*args, *exchange.operands)


def _ffn_fwd(h, pre_w, w1, w2, post_w, target, *, name, exchange=None):
    T, D = h.shape
    nk, ck = w2.shape[0], w2.shape[1]
    tT = min(TOKEN_TILE, T)
    nT = T // tT
    with_loss = target is not None

    def body(*refs):
        if with_loss:
            (h_ref, pre_ref, w1g_ref, w1u_ref, w2_ref, post_ref, tgt_ref,
             u_ref, f_ref, ho_ref, a_s, dy_ref, loss_ref, acc) = refs
        else:
            (h_ref, pre_ref, w1g_ref, w1u_ref, w2_ref, post_ref,
             u_ref, f_ref, ho_ref, a_s, acc) = refs
        k = pl.program_id(1)

        @pl.when(k == 0)
        def _():
            a_s[...] = _rms_fwd(h_ref[...], pre_ref[...]).astype(BF16)
            acc[...] = jnp.zeros_like(acc)

        for c in range(FFN_CHAINS):
            rs = slice(c * (tT // FFN_CHAINS), (c + 1) * (tT // FFN_CHAINS))
            a = a_s[rs, :]
            ug = _dot_nt(a, w1g_ref[...])
            uu = _dot_nt(a, w1u_ref[...])
            u_ref[0, rs, :] = ug.astype(BF16)
            u_ref[1, rs, :] = uu.astype(BF16)
            acc[rs, :] += _dot((ug * _sigmoid(ug) * uu).astype(BF16), w2_ref[...])

        @pl.when(k == nk - 1)
        def _():
            f = acc[...]
            f_ref[...] = f
            ho = h_ref[...] + 0.5 * _rms_fwd(f, post_ref[...])
            ho_ref[...] = ho
            if with_loss:
                e = ho - tgt_ref[...]
                dy_ref[...] = e * (1.0 / D)
                loss_ref[...] = jnp.full(loss_ref.shape, (0.5 / D) * jnp.sum(e * e), F32)

    row = pl.BlockSpec((tT, D), lambda i, k: (i, 0))
    vec = pl.BlockSpec((1, D), lambda i, k: (0, 0))
    in_specs = [row, vec,
                pl.BlockSpec((None, ck, D), lambda i, k: (k, 0, 0)),
                pl.BlockSpec((None, ck, D), lambda i, k: (nk + k, 0, 0)),
                pl.BlockSpec((None, ck, D), lambda i, k: (k, 0, 0)),
                vec]
    out_shape = [jax.ShapeDtypeStruct((2, nk, T, ck), BF16),
                 jax.ShapeDtypeStruct((T, D), F32),
                 jax.ShapeDtypeStruct((T, D), F32),
                 jax.ShapeDtypeStruct((T, D), BF16)]
    out_specs = [pl.BlockSpec((2, None, tT, ck), lambda i, k: (0, k, i, 0)), row, row, row]
    args = [h, pre_w, w1, w1, w2, post_w]
    if with_loss:
        in_specs.append(row)
        args.append(target)
        out_shape += [jax.ShapeDtypeStruct((T, D), F32), jax.ShapeDtypeStruct((nT * 8, LANES), F32)]
        out_specs += [row, pl.BlockSpec((8, LANES), lambda i, k: (i, 0))]
    return _call(body, name=name, grid=(nT, nk), in_specs=in_specs, out_specs=out_specs, out_shape=out_shape,
                 scratch_shapes=[pltpu.VMEM((tT, D), F32)], args=args, exchange=exchange)


def _ffn_bwd(dho, f, post_w, h, pre_w, u, w2, w1, *, name, exchange=None, after=None):
    T, D = h.shape
    nk, ck = w2.shape[0], w2.shape[1]
    tT = min(TOKEN_TILE, T)
    nT = T // tT

    def body(dho_ref, f_ref, post_ref, h_ref, pre_ref, u_ref, w2_ref, w1g_ref, w1u_ref,
             g_ref, du_ref, df_s, dh_ref, gpost_ref, gpre_ref, da_acc):
        i, k = pl.program_id(0), pl.program_id(1)

        @pl.when(jnp.logical_and(i == 0, k == 0))
        def _():
            gpost_ref[...] = jnp.zeros_like(gpost_ref)
            gpre_ref[...] = jnp.zeros_like(gpre_ref)

        @pl.when(k == 0)
        def _():
            dx, dw = _rms_bwd(f_ref[...], post_ref[...], 0.5 * dho_ref[...])
            df_s[...] = dx.astype(BF16)
            gpost_ref[...] += dw
            da_acc[...] = jnp.zeros_like(da_acc)

        groups = [slice(c * (tT // FFN_CHAINS), (c + 1) * (tT // FFN_CHAINS)) for c in range(FFN_CHAINS)]
        dgs = [_dot_nt(df_s[rs, :], w2_ref[...]) for rs in groups]
        for rs, dg in zip(groups, dgs):
            ug = u_ref[0, rs, :].astype(F32)
            uu = u_ref[1, rs, :].astype(F32)
            sg = _sigmoid(ug)
            sl = ug * sg
            g_ref[rs, :] = (sl * uu).astype(BF16)
            dug = (dg * uu * (sg + sl * (1.0 - sg))).astype(BF16)
            duu = (dg * sl).astype(BF16)
            du_ref[0, rs, :] = dug
            du_ref[1, rs, :] = duu
            da_acc[rs, :] += _dot(dug, w1g_ref[...]) + _dot(duu, w1u_ref[...])

        @pl.when(k == nk - 1)
        def _():
            dx, dw = _rms_bwd(h_ref[...], pre_ref[...], da_acc[...])
            dh_ref[...] = dho_ref[...] + dx
            gpre_ref[...] += dw

    row = pl.BlockSpec((tT, D), lambda i, k: (i, 0))
    vec = pl.BlockSpec((1, D), lambda i, k: (0, 0))
    return _call(
        body, name=name, grid=(nT, nk),
        in_specs=[row, row, vec, row, vec,
                  pl.BlockSpec((2, None, tT, ck), lambda i, k: (0, k, i, 0)),
                  pl.BlockSpec((None, ck, D), lambda i, k: (k, 0, 0)),
                  pl.BlockSpec((None, ck, D), lambda i, k: (k, 0, 0)),
                  pl.BlockSpec((None, ck, D), lambda i, k: (nk + k, 0, 0))],
        out_specs=[pl.BlockSpec((None, tT, ck), lambda i, k: (k, i, 0)),
                   pl.BlockSpec((2, None, tT, ck), lambda i, k: (0, k, i, 0)),
                   row, row, vec, vec],
        out_shape=[jax.ShapeDtypeStruct((nk, T, ck), BF16),
                   jax.ShapeDtypeStruct((2, nk, T, ck), BF16),
                   jax.ShapeDtypeStruct((T, D), BF16),
                   jax.ShapeDtypeStruct((T, D), F32),
                   jax.ShapeDtypeStruct((1, D), F32),
                   jax.ShapeDtypeStruct((1, D), F32)],
        scratch_shapes=[pltpu.VMEM((tT, D), F32)],
        args=(dho, f, post_w, h, pre_w, u, w2, w1, w1), exchange=exchange, after=after)


def _matmul_tn(x, dy, *, name, exchange=None, after=None):
    Px, T, K = x.shape
    Py, _, N = dy.shape
    P = max(Px, Py)
    tT, tK, tN = min(GRAD_TOKEN_TILE, T), _tile(K, GRAD_TILE_CAP), _tile(N, GRAD_TILE_CAP)
    nt = T // tT

    def body(x_ref, dy_ref, o_ref, acc):
        t = pl.program_id(3)

        @pl.when(t == 0)
        def _():
            acc[...] = jnp.zeros_like(acc)

        acc[...] += _dot_tn(x_ref[...], dy_ref[...])

        @pl.when(t == nt - 1)
        def _():
            o_ref[...] = acc[...].astype(BF16)

    return _call(
        body, name=name, grid=(P, K // tK, N // tN, nt),
        in_specs=[pl.BlockSpec((None, tT, tK), lambda p, a, b, t: (p if Px > 1 else 0, t, a)),
                  pl.BlockSpec((None, tT, tN), lambda p, a, b, t: (p if Py > 1 else 0, t, b))],
        out_specs=[pl.BlockSpec((None, tK, tN), lambda p, a, b, t: (p, a, b))],
        out_shape=[jax.ShapeDtypeStruct((P, K, N), BF16)],
        scratch_shapes=[pltpu.VMEM((tK, tN), F32)], args=(x, dy), exchange=exchange, after=after)


def _rms_matmul(h, wn, w, *, name):
    T, D = h.shape
    N = w.shape[0]
    tT, tN = min(TOKEN_TILE, T), _tile(N, PROJ_TILE_CAP)

    def body(h_ref, wn_ref, w_ref, y_ref, a_ref):
        @pl.when(pl.program_id(1) == 0)
        def _():
            a_ref[...] = _rms_fwd(h_ref[...], wn_ref[...]).astype(BF16)

        y_ref[...] = _dot_nt(a_ref[...], w_ref[...]).astype(BF16)

    return pl.pallas_call(
        body, name=name, grid=(T // tT, N // tN),
        in_specs=[pl.BlockSpec((tT, D), lambda i, j: (i, 0)),
                  pl.BlockSpec((1, D), lambda i, j: (0, 0)),
                  pl.BlockSpec((tN, D), lambda i, j: (j, 0))],
        out_specs=[pl.BlockSpec((tT, tN), lambda i, j: (i, j)),
                   pl.BlockSpec((tT, D), lambda i, j: (i, 0))],
        out_shape=[jax.ShapeDtypeStruct((T, N), BF16), jax.ShapeDtypeStruct((T, D), BF16)],
        compiler_params=_params(("parallel", "arbitrary")),
    )(h, wn, w)


def _proj_bwd(dproj, w, h, wn, dres, *, name, exchange=None, after=None):
    T, D = h.shape
    N = w.shape[0]
    tT, tN = min(TOKEN_TILE, T), _tile(N, PROJ_TILE_CAP)
    nn = N // tN

    def body(dp_ref, w_ref, h_ref, wn_ref, dres_ref, dh_ref, gw_ref, acc):
        i, j = pl.program_id(0), pl.program_id(1)

        @pl.when(jnp.logical_and(i == 0, j == 0))
        def _():
            gw_ref[...] = jnp.zeros_like(gw_ref)

        @pl.when(j == 0)
        def _():
            acc[...] = jnp.zeros_like(acc)

        acc[...] += _dot(dp_ref[...], w_ref[...])

        @pl.when(j == nn - 1)
        def _():
            dx, dw = _rms_bwd(h_ref[...], wn_ref[...], acc[...])
            dh_ref[...] = dres_ref[...] + dx
            gw_ref[...] += dw

    row = pl.BlockSpec((tT, D), lambda i, j: (i, 0))
    vec = pl.BlockSpec((1, D), lambda i, j: (0, 0))
    return _call(
        body, name=name, grid=(T // tT, nn),
        in_specs=[pl.BlockSpec((tT, tN), lambda i, j: (i, j)),
                  pl.BlockSpec((tN, D), lambda i, j: (j, 0)), row, vec, row],
        out_specs=[row, vec],
        out_shape=[jax.ShapeDtypeStruct((T, D), F32), jax.ShapeDtypeStruct((1, D), F32)],
        scratch_shapes=[pltpu.VMEM((tT, D), F32)], args=(dproj, w, h, wn, dres), exchange=exchange, after=after)


def _mla_prep_fwd(proj, pos, qn_w, kvn_w, w_uq, w_kv, tab, *, name):
    T = proj.shape[0]
    tT = min(TOKEN_TILE, T)
    a_blk = PROJ_FIXED // AW - 1

    def body(a_ref, pos_ref, qnw_ref, kvnw_ref, wuq_ref, wkv_ref, tab_ref,
             q_ref, k_ref, v_ref):
        cq = a_ref[:, 0:MLA_Q_RANK].astype(F32)
        ckv = a_ref[:, MLA_Q_RANK:MLA_Q_RANK + MLA_KV_RANK].astype(F32)
        kr = a_ref[:, 640:768].astype(F32)
        qn = _rms_fwd(cq, qnw_ref[...]).astype(BF16)
        kvn = _rms_fwd(ckv, kvnw_ref[...]).astype(BF16)
        cs = _rope_cs(pos_ref[...], tab_ref)
        q = _dot_nt(qn, wuq_ref[...])
        kv = _dot(kvn, wkv_ref[...])
        krr = _rope(kr, cs, MLA_ROPE // 2)
        for hd in range(MLA_HEADS):
            sl = slice(hd * HP, (hd + 1) * HP)
            q_ref[:, sl] = (_rope(q[:, sl], cs, MLA_ROPE // 2) * ATTN_SCALE).astype(BF16)
            k_ref[:, sl] = (kv[:, sl] + krr).astype(BF16)
        v_ref[...] = kv[:, QW:].astype(BF16)

    def full(r, c):
        return pl.BlockSpec((r, c), lambda i: (0, 0))

    def rows(c):
        return pl.BlockSpec((tT, c), lambda i: (i, 0))

    return pl.pallas_call(
        body, name=name, grid=(T // tT,),
        in_specs=[pl.BlockSpec((tT, AW), lambda i: (i, a_blk)), rows(1),
                  full(1, MLA_Q_RANK), full(1, MLA_KV_RANK),
                  full(QW, MLA_Q_RANK), full(MLA_KV_RANK, 2 * QW), full(8, LANES)],
        out_specs=[rows(QW), rows(QW), rows(QW)],
        out_shape=[jax.ShapeDtypeStruct((T, QW), BF16)] * 3,
        compiler_params=_params(("parallel",)),
    )(proj, pos, qn_w, kvn_w, w_uq, w_kv, tab)


def _mla_prep_bwd(dq, dk, dv, proj, pos, qn_w, kvn_w, w_uq, w_kv, tab, *, name):
    T = proj.shape[0]
    tT = min(TOKEN_TILE, T)
    nT = T // tT
    a_blk = PROJ_FIXED // AW - 1

    def body(dq_ref, dk_ref, dv_ref, a_ref, pos_ref, qnw_ref, kvnw_ref, wuq_ref, wkv_ref, tab_ref,
             da_ref, gqn_ref, gkvn_ref, dwuq_ref, dwkv_ref, dql_ref, dkvl_ref, acc_uq, acc_kv):
        @pl.when(pl.program_id(0) == 0)
        def _():
            gqn_ref[...] = jnp.zeros_like(gqn_ref)
            gkvn_ref[...] = jnp.zeros_like(gkvn_ref)
            acc_uq[...] = jnp.zeros_like(acc_uq)
            acc_kv[...] = jnp.zeros_like(acc_kv)

        cs = _rope_cs(pos_ref[...], tab_ref)
        dkr = jnp.zeros((tT, HP), F32)
        for hd in range(MLA_HEADS):
            sl = slice(hd * HP, (hd + 1) * HP)
            dql_ref[:, sl] = (_rope(dq_ref[:, sl], cs, MLA_ROPE // 2, inverse=True) * ATTN_SCALE).astype(BF16)
            dkh = dk_ref[:, sl]
            dkr = dkr + dkh
            dkvl_ref[:, sl] = dkh.astype(BF16)
        dkvl_ref[:, QW:] = dv_ref[...]
        dqn = _dot(dql_ref[...], wuq_ref[...])
        dkvn = _dot_nt(dkvl_ref[...], wkv_ref[...])
        cq = a_ref[:, 0:MLA_Q_RANK].astype(F32)
        ckv = a_ref[:, MLA_Q_RANK:MLA_Q_RANK + MLA_KV_RANK].astype(F32)
        dcq, gq = _rms_bwd(cq, qnw_ref[...], dqn)
        dckv, gkv = _rms_bwd(ckv, kvnw_ref[...], dkvn)
        gqn_ref[...] += gq
        gkvn_ref[...] += gkv
        da_ref[:, 0:MLA_Q_RANK] = dcq.astype(BF16)
        da_ref[:, MLA_Q_RANK:MLA_Q_RANK + MLA_KV_RANK] = dckv.astype(BF16)
        da_ref[:, 640:768] = _rope(dkr, cs, MLA_ROPE // 2, inverse=True).astype(BF16)
        da_ref[:, 768:AW] = jnp.zeros((tT, AW - 768), BF16)
        acc_uq[...] += _dot_tn(dql_ref[...], _rms_fwd(cq, qnw_ref[...]).astype(BF16))
        acc_kv[...] += _dot_tn(_rms_fwd(ckv, kvnw_ref[...]).astype(BF16), dkvl_ref[...])

        @pl.when(pl.program_id(0) == nT - 1)
        def _():
            dwuq_ref[...] = acc_uq[...].astype(BF16)
            dwkv_ref[...] = acc_kv[...].astype(BF16)

    def full(r, c):
        return pl.BlockSpec((r, c), lambda i: (0, 0))

    def rows(c):
        return pl.BlockSpec((tT, c), lambda i: (i, 0))

    return pl.pallas_call(
        body, name=name, grid=(nT,),
        in_specs=[rows(QW), rows(QW), rows(QW), pl.BlockSpec((tT, AW), lambda i: (i, a_blk)), rows(1),
                  full(1, MLA_Q_RANK), full(1, MLA_KV_RANK),
                  full(QW, MLA_Q_RANK), full(MLA_KV_RANK, 2 * QW), full(8, LANES)],
        out_specs=[rows(AW), full(1, MLA_Q_RANK), full(1, MLA_KV_RANK),
                   full(QW, MLA_Q_RANK), full(MLA_KV_RANK, 2 * QW)],
        out_shape=[jax.ShapeDtypeStruct((T, AW), BF16),
                   jax.ShapeDtypeStruct((1, MLA_Q_RANK), F32), jax.ShapeDtypeStruct((1, MLA_KV_RANK), F32),
                   jax.ShapeDtypeStruct((QW, MLA_Q_RANK), BF16), jax.ShapeDtypeStruct((MLA_KV_RANK, 2 * QW), BF16)],
        scratch_shapes=[pltpu.VMEM((tT, QW), BF16), pltpu.VMEM((tT, 2 * QW), BF16),
                        pltpu.VMEM((QW, MLA_Q_RANK), F32), pltpu.VMEM((MLA_KV_RANK, 2 * QW), F32)],
        compiler_params=_params(("arbitrary",)),
    )(dq, dk, dv, proj, pos, qn_w, kvn_w, w_uq, w_kv, tab)


def _flash_fwd(q, k, v, *, name, exchange=None):
    T = q.shape[0]
    H = q.shape[1] // HP
    tq = min(ATTN_TILE, T)
    nq = T // tq

    sub = tq // ATTN_CHAINS

    def body(q_ref, k_ref, v_ref, o_ref, lse_ref):
        qi = pl.program_id(1)
        qs = [q_ref[c * sub:(c + 1) * sub, :] for c in range(ATTN_CHAINS)]

        def update(carry, off, masked):
            nks = [(c + 1) * sub if masked else tq for c in range(ATTN_CHAINS)]
            scores = [_dot_nt(qs[c], k_ref[pl.ds(off, nks[c]), :]) for c in range(ATTN_CHAINS)]
            out = []
            for c in range(ATTN_CHAINS):
                m_prev, l_prev, acc = carry[c]
                nk, s = nks[c], scores[c]
                vb = v_ref[pl.ds(off, nk), :]
                if masked:
                    rows = lax.broadcasted_iota(jnp.int32, (sub, nk), 0) + c * sub
                    s = jnp.where(rows >= lax.broadcasted_iota(jnp.int32, (sub, nk), 1), s, NEG)
                m_new = jnp.maximum(m_prev, jnp.max(s, axis=1, keepdims=True))
                alpha = jnp.exp(m_prev - m_new)
                p = jnp.exp(s - m_new)
                out.append((m_new, alpha * l_prev + jnp.sum(p, axis=1, keepdims=True),
                            alpha * acc + _dot(p.astype(BF16), vb)))
            return tuple(out)

        init = tuple((jnp.full((sub, 1), NEG, F32), jnp.zeros((sub, 1), F32), jnp.zeros((sub, HP), F32))
                     for _ in range(ATTN_CHAINS))
        carry = lax.fori_loop(0, qi, lambda j, cr: update(cr, pl.multiple_of(j * tq, tq), False), init)
        carry = update(carry, pl.multiple_of(qi * tq, tq), True)
        for c in range(ATTN_CHAINS):
            m_fin, l_fin, acc = carry[c]
            o_ref[c * sub:(c + 1) * sub, :] = (acc / l_fin).astype(BF16)
            lse_ref[c * sub:(c + 1) * sub, :] = jnp.broadcast_to(m_fin + jnp.log(l_fin), (sub, HP))

    qspec = pl.BlockSpec((tq, HP), lambda h, i: (i, h))
    kspec = pl.BlockSpec((T, HP), lambda h, i: (0, h))
    return _call(
        body, name=name, grid=(H, nq),
        in_specs=[qspec, kspec, kspec], out_specs=[qspec, qspec],
        out_shape=[jax.ShapeDtypeStruct((T, H * HP), BF16), jax.ShapeDtypeStruct((T, H * HP), F32)],
        scratch_shapes=[], args=(q, k, v), exchange=exchange)


def _flash_bwd(q, k, v, do, lse, delta, *, name, exchange=None, after=None):
    T = q.shape[0]
    H = q.shape[1] // HP
    tq = min(ATTN_TILE, T)
    nq = T // tq
    sub = tq // ATTN_CHAINS

    def body(k_ref, v_ref, q_ref, do_ref, lse_ref, dl_ref, dq_ref, dk_ref, dv_ref):
        ki = pl.program_id(1)

        @pl.when(ki == 0)
        def _():
            dq_ref[...] = jnp.zeros_like(dq_ref)

        def grow(a):
            return a if a.shape[0] == tq else jnp.concatenate([a, jnp.zeros((tq - a.shape[0], HP), F32)], axis=0)

        def step(carry, j, masked):
            dk_acc, dv_acc = carry
            nks = [(c + 1) * sub if masked else tq for c in range(ATTN_CHAINS)]
            rws = [pl.ds(pl.multiple_of(j * tq + c * sub, sub), sub) for c in range(ATTN_CHAINS)]
            scores = [_dot_nt(q_ref[rws[c], :], k_ref[0:nks[c], :]) for c in range(ATTN_CHAINS)]
            dps = [_dot_nt(do_ref[rws[c], :], v_ref[0:nks[c], :]) for c in range(ATTN_CHAINS)]
            for c in range(ATTN_CHAINS):
                rows, nk, s, dp = rws[c], nks[c], scores[c], dps[c]
                kb = k_ref[0:nk, :]
                qb = q_ref[rows, :]
                dob = do_ref[rows, :]
                if masked:
                    ri = lax.broadcasted_iota(jnp.int32, (sub, nk), 0) + c * sub
                    s = jnp.where(ri >= lax.broadcasted_iota(jnp.int32, (sub, nk), 1), s, NEG)
                p = jnp.exp(s - lse_ref[rows, 0:1])
                dv_acc = dv_acc + grow(_dot_tn(p.astype(BF16), dob))
                ds = (p * (dp - dl_ref[rows, 0:1])).astype(BF16)
                dk_acc = dk_acc + grow(_dot_tn(ds, qb))
                dq_ref[rows, :] += _dot(ds, kb)
            return dk_acc, dv_acc

        carry = step((jnp.zeros((tq, HP), F32), jnp.zeros((tq, HP), F32)), ki, True)
        dk_acc, dv_acc = lax.fori_loop(ki + 1, nq, lambda j, cr: step(cr, j, False), carry)
        dk_ref[...] = dk_acc
        dv_ref[...] = dv_acc.astype(BF16)

    kspec = pl.BlockSpec((tq, HP), lambda h, j: (j, h))
    full = pl.BlockSpec((T, HP), lambda h, j: (0, h))
    return _call(
        body, name=name, grid=(H, nq),
        in_specs=[kspec, kspec, full, full, full, full], out_specs=[full, kspec, kspec],
        out_shape=[jax.ShapeDtypeStruct((T, H * HP), F32), jax.ShapeDtypeStruct((T, H * HP), F32),
                   jax.ShapeDtypeStruct((T, H * HP), BF16)],
        scratch_shapes=[], args=(k, v, q, do, lse, delta), exchange=exchange, after=after)


def _ret_consts(cc, hd):
    lg = math.log(1.0 - 2.0 ** (-5.0 - hd))
    diff = (lax.broadcasted_iota(jnp.int32, (cc, cc), 0) - lax.broadcasted_iota(jnp.int32, (cc, cc), 1)).astype(F32)
    decay = jnp.where(diff >= 0, jnp.exp(jnp.maximum(diff, 0.0) * lg), 0.0)
    idx = lax.broadcasted_iota(jnp.int32, (cc, 1), 0).astype(F32)
    zeta = jnp.exp((cc - 1.0 - idx) * lg)
    xi = jnp.exp((idx + 1.0) * lg)
    return decay, zeta, xi, math.exp(cc * lg)


def _ret_fwd(proj, pos, tab, *, name):
    T = proj.shape[0]
    cc = min(RET_TILE, T)
    n = T // cc

    def body(rq_ref, rk_ref, rv_ref, pos_ref, tab_ref, y_ref, yn_ref, rprev_ref, r_s):
        @pl.when(pl.program_id(0) == 0)
        def _():
            r_s[...] = jnp.zeros_like(r_s)

        cs = _rope_cs(pos_ref[...], tab_ref)
        for hd in range(RET_HEADS):
            sl = slice(hd * HP, (hd + 1) * HP)
            decay, zeta, xi, gc = _ret_consts(cc, hd)
            q = _rope(rq_ref[:, sl].astype(F32), cs, RET_DK // 2).astype(BF16)
            kf = _rope(rk_ref[:, sl].astype(F32), cs, RET_DK // 2) * (RET_DK ** -0.5)
            k = kf.astype(BF16)
            v = rv_ref[:, sl]
            r = r_s[hd]
            rprev_ref[0, hd] = r
            inner = (_dot_nt(q, k) * decay).astype(BF16)
            y = _dot(inner, v) + _dot(q, r.astype(BF16)) * xi
            r_s[hd] = r * gc + _dot_tn((kf * zeta).astype(BF16), v)
            y_ref[:, sl] = y
            mu = jnp.mean(y, axis=-1, keepdims=True)
            yc = y - mu
            var = jnp.mean(yc * yc, axis=-1, keepdims=True)
            yn_ref[:, sl] = (yc * lax.rsqrt(var + GN_EPS)).astype(BF16)

    def blk(j):
        return pl.BlockSpec((cc, RW), lambda i: (i, j))

    return pl.pallas_call(
        body, name=name, grid=(n,),
        in_specs=[blk(0), blk(1), blk(2), pl.BlockSpec((cc, 1), lambda i: (i, 0)),
                  pl.BlockSpec((8, LANES), lambda i: (0, 0))],
        out_specs=[blk(0), blk(0), pl.BlockSpec((1, RET_HEADS, HP, RET_DV), lambda i: (i, 0, 0, 0))],
        out_shape=[jax.ShapeDtypeStruct((T, RW), F32), jax.ShapeDtypeStruct((T, RW), BF16),
                   jax.ShapeDtypeStruct((n, RET_HEADS, HP, RET_DV), F32)],
        scratch_shapes=[pltpu.VMEM((RET_HEADS, HP, RET_DV), F32)],
        compiler_params=_params(("arbitrary",)),
    )(proj, proj, proj, pos, tab)


def _ret_bwd(dyn, y, proj, pos, tab, rprev, *, name):
    T = proj.shape[0]
    cc = min(RET_TILE, T)
    n = T // cc

    def body(dyn_ref, y_ref, rq_ref, rk_ref, rv_ref, pos_ref, tab_ref, rprev_ref,
             drq_ref, drk_ref, drv_ref, dr_s):
        @pl.when(pl.program_id(0) == 0)
        def _():
            dr_s[...] = jnp.zeros_like(dr_s)

        cs = _rope_cs(pos_ref[...], tab_ref)
        for hd in range(RET_HEADS):
            sl = slice(hd * HP, (hd + 1) * HP)
            decay, zeta, xi, gc = _ret_consts(cc, hd)
            q = _rope(rq_ref[:, sl].astype(F32), cs, RET_DK // 2).astype(BF16)
            kf = _rope(rk_ref[:, sl].astype(F32), cs, RET_DK // 2) * (RET_DK ** -0.5)
            k = kf.astype(BF16)
            v = rv_ref[:, sl]
            yv = y_ref[:, sl]
            mu = jnp.mean(yv, axis=-1, keepdims=True)
            yc = yv - mu
            rs = lax.rsqrt(jnp.mean(yc * yc, axis=-1, keepdims=True) + GN_EPS)
            yn = yc * rs
            dn = dyn_ref[:, sl]
            dy = rs * (dn - jnp.mean(dn, axis=-1, keepdims=True) - yn * jnp.mean(dn * yn, axis=-1, keepdims=True))
            dyb = dy.astype(BF16)
            dyx = (dy * xi).astype(BF16)
            dr = dr_s[hd]
            drb = dr.astype(BF16)
            inner = (_dot_nt(q, k) * decay).astype(BF16)
            da = (_dot_nt(dyb, v) * decay).astype(BF16)
            dv = _dot_tn(inner, dyb) + _dot((kf * zeta).astype(BF16), drb)
            dq = _dot(da, k) + _dot_nt(dyx, rprev_ref[0, hd].astype(BF16))
            dk = _dot_tn(da, q) + _dot_nt(v, drb) * zeta
            dr_s[hd] = dr * gc + _dot_tn(q, dyx)
            drq_ref[:, sl] = _rope(dq, cs, RET_DK // 2, inverse=True).astype(BF16)
            drk_ref[:, sl] = _rope(dk * (RET_DK ** -0.5), cs, RET_DK // 2, inverse=True).astype(BF16)
            drv_ref[:, sl] = dv.astype(BF16)

    def blk(j):
        return pl.BlockSpec((cc, RW), lambda i: (n - 1 - i, j))

    return pl.pallas_call(
        body, name=name, grid=(n,),
        in_specs=[blk(0), blk(0), blk(0), blk(1), blk(2), pl.BlockSpec((cc, 1), lambda i: (n - 1 - i, 0)),
                  pl.BlockSpec((8, LANES), lambda i: (0, 0)),
                  pl.BlockSpec((1, RET_HEADS, HP, RET_DV), lambda i: (n - 1 - i, 0, 0, 0))],
        out_specs=[blk(0), blk(0), blk(0)],
        out_shape=[jax.ShapeDtypeStruct((T, RW), BF16)] * 3,
        scratch_shapes=[pltpu.VMEM((RET_HEADS, HP, RET_DV), F32)],
        compiler_params=_params(("arbitrary",)),
    )(dyn, y, proj, proj, proj, pos, tab, rprev)


def _merge_fwd(o, yn, proj, gn_w, w_bm, w_br, w_out, h, post_w, *, name):
    T, D = h.shape
    tT = min(TOKEN_TILE, T)
    g_blk = PROJ_FIXED // D

    def body(o_ref, yn_ref, rg_ref, gm_ref, gr_ref, gnw_ref, wbm_ref, wbr_ref, wout_ref, h_ref, post_ref,
             omla_ref, oret_ref, m_ref, ho_ref):
        groups = [slice(c * (tT // FFN_CHAINS), (c + 1) * (tT // FFN_CHAINS)) for c in range(FFN_CHAINS)]
        o_mlas = [_dot(o_ref[rs, :], wbm_ref[...]) for rs in groups]
        for rs, o_mla in zip(groups, o_mlas):
            rg = rg_ref[rs, :].astype(F32)
            gated = (rg * _sigmoid(rg) * (yn_ref[rs, :].astype(F32) * gnw_ref[...])).astype(BF16)
            o_ret = _dot(gated, wbr_ref[...])
            omla_ref[rs, :] = o_mla.astype(BF16)
            oret_ref[rs, :] = o_ret.astype(BF16)
            merged = _sigmoid(gm_ref[rs, :].astype(F32)) * o_mla + _sigmoid(gr_ref[rs, :].astype(F32)) * o_ret
            m = _dot(merged.astype(BF16), wout_ref[...])
            m_ref[rs, :] = m
            ho_ref[rs, :] = h_ref[rs, :] + _rms_fwd(m, post_ref[...])

    def full(r, c):
        return pl.BlockSpec((r, c), lambda i: (0, 0))

    def rows(c, j=0):
        return pl.BlockSpec((tT, c), lambda i: (i, j))

    return pl.pallas_call(
        body, name=name, grid=(T // tT,),
        in_specs=[rows(QW), rows(RW), rows(RW, 3), rows(D, g_blk), rows(D, g_blk + 1), full(1, RW),
                  full(QW, D), full(RW, D), full(D, D), rows(D), full(1, D)],
        out_specs=[rows(D), rows(D), rows(D), rows(D)],
        out_shape=[jax.ShapeDtypeStruct((T, D), BF16), jax.ShapeDtypeStruct((T, D), BF16),
                   jax.ShapeDtypeStruct((T, D), F32), jax.ShapeDtypeStruct((T, D), F32)],
        compiler_params=_params(("parallel",)),
    )(o, yn, proj, proj, proj, gn_w, w_bm, w_br, w_out, h, post_w)


def _merge_bwd(dho, m, post_w, omla, oret, proj, yn, gn_w, o, w_out, w_bm, w_br, *, name):
    T, D = dho.shape
    tT = min(MERGE_TILE, T)
    g_blk = PROJ_FIXED // D

    nT = T // tT

    def body(dho_ref, m_ref, post_ref, omla_ref, oret_ref, rg_ref, gm_ref, gr_ref, yn_ref, gnw_ref, o_ref,
             wout_ref, wbm_ref, wbr_ref,
             dgm_ref, dgr_ref, do_ref, delta_ref, drg_ref, dyn_ref, gpost_ref, ggn_ref,
             dwout_ref, dwbm_ref, dwbr_ref, acc_out, acc_bm, acc_br):
        @pl.when(pl.program_id(0) == 0)
        def _():
            gpost_ref[...] = jnp.zeros_like(gpost_ref)
            ggn_ref[...] = jnp.zeros_like(ggn_ref)
            acc_out[...] = jnp.zeros_like(acc_out)
            acc_bm[...] = jnp.zeros_like(acc_bm)
            acc_br[...] = jnp.zeros_like(acc_br)

        dm, gp = _rms_bwd(m_ref[...], post_ref[...], dho_ref[...])
        gpost_ref[...] += gp
        dmb = dm.astype(BF16)
        dmerged = _dot_nt(dmb, wout_ref[...])
        o_mla = omla_ref[...].astype(F32)
        o_ret = oret_ref[...].astype(F32)
        sgm = _sigmoid(gm_ref[...].astype(F32))
        sgr = _sigmoid(gr_ref[...].astype(F32))
        acc_out[...] += _dot_tn((sgm * o_mla + sgr * o_ret).astype(BF16), dmb)
        dgm_ref[...] = (dmerged * o_mla * sgm * (1.0 - sgm)).astype(BF16)
        dgr_ref[...] = (dmerged * o_ret * sgr * (1.0 - sgr)).astype(BF16)
        domla = (dmerged * sgm).astype(BF16)
        acc_bm[...] += _dot_tn(o_ref[...], domla)
        do = _dot_nt(domla, wbm_ref[...])
        do_ref[...] = do.astype(BF16)
        for hd in range(MLA_HEADS):
            sl = slice(hd * HP, (hd + 1) * HP)
            d = jnp.sum(do[:, sl] * o_ref[:, sl].astype(F32), axis=-1, keepdims=True)
            delta_ref[:, sl] = jnp.broadcast_to(d, (tT, HP))
        doret = (dmerged * sgr).astype(BF16)
        dgated = _dot_nt(doret, wbr_ref[...])
        rg = rg_ref[...].astype(F32)
        sg = _sigmoid(rg)
        srg = rg * sg
        ynv = yn_ref[...].astype(F32)
        yw = ynv * gnw_ref[...]
        acc_br[...] += _dot_tn((srg * yw).astype(BF16), doret)
        drg_ref[...] = (dgated * yw * (sg * (1.0 + rg * (1.0 - sg)))).astype(BF16)
        dgs = dgated * srg
        dyn_ref[...] = dgs * gnw_ref[...]
        ggn_ref[...] += jnp.sum(dgs * ynv, axis=0, keepdims=True)

        @pl.when(pl.program_id(0) == nT - 1)
        def _():
            dwout_ref[...] = acc_out[...].astype(BF16)
            dwbm_ref[...] = acc_bm[...].astype(BF16)
            dwbr_ref[...] = acc_br[...].astype(BF16)

    def full(r, c):
        return pl.BlockSpec((r, c), lambda i: (0, 0), pipeline_mode=pl.Buffered(1))

    def rows(c, j=0):
        return pl.BlockSpec((tT, c), lambda i: (i, j))

    return pl.pallas_call(
        body, name=name, grid=(nT,),
        in_specs=[rows(D), rows(D), full(1, D), rows(D), rows(D), rows(RW, 3), rows(D, g_blk), rows(D, g_blk + 1),
                  rows(RW), full(1, RW), rows(QW), full(D, D), full(QW, D), full(RW, D)],
        out_specs=[rows(D), rows(D), rows(QW), rows(QW), rows(RW), rows(RW), full(1, D), full(1, RW),
                   full(D, D), full(QW, D), full(RW, D)],
        out_shape=[jax.ShapeDtypeStruct((T, D), BF16)] * 2
        + [jax.ShapeDtypeStruct((T, QW), BF16), jax.ShapeDtypeStruct((T, QW), F32),
           jax.ShapeDtypeStruct((T, RW), BF16), jax.ShapeDtypeStruct((T, RW), F32),
           jax.ShapeDtypeStruct((1, D), F32), jax.ShapeDtypeStruct((1, RW), F32),
           jax.ShapeDtypeStruct((D, D), BF16), jax.ShapeDtypeStruct((QW, D), BF16), jax.ShapeDtypeStruct((RW, D), BF16)],
        scratch_shapes=[pltpu.VMEM((D, D), F32), pltpu.VMEM((QW, D), F32), pltpu.VMEM((RW, D), F32)],
        compiler_params=_params(("arbitrary",)),
    )(dho, m, post_w, omla, oret, proj, proj, proj, yn, gn_w, o, w_out, w_bm, w_br)


def _mesh_pos():
    return lax.axis_index("x"), lax.axis_index("y"), lax.axis_index("c")


class _Gather:
    def __init__(self, shards):
        self.operands = list(shards)
        self.n = len(shards)
        self.out_shape = [jax.ShapeDtypeStruct((N_DEV,) + s.shape, s.dtype) for s in shards]
        self.scratch = [pltpu.SemaphoreType.DMA((7 * self.n,)), pltpu.SemaphoreType.DMA((7 * self.n,)),
                        pltpu.SemaphoreType.DMA((self.n,))]

    def phase(self, p, x_refs, out_refs, sems):
        send_sems, recv_sems, local_sems = sems
        x, y, c = _mesh_pos()
        me, sibling = (x, y, c), (x, y, 1 - c)
        chips = [(1 - x, y), (x, 1 - y), (1 - x, 1 - y)]

        def copy(w, k, block, to, src=None):
            slot = out_refs[w].at[4 * block[0] + 2 * block[1] + block[2]]
            return pltpu.make_async_remote_copy(
                src_ref=slot if src is None else src, dst_ref=slot,
                send_sem=send_sems.at[7 * w + k], recv_sem=recv_sems.at[7 * w + k],
                device_id=to, device_id_type=pl.DeviceIdType.MESH)

        for w in range(self.n):
            mine = pltpu.make_async_copy(x_refs[w], out_refs[w].at[4 * x + 2 * y + c], local_sems.at[w])
            first = [copy(w, 0, me, sibling, src=x_refs[w])]
            first += [copy(w, 1 + j, me, (*chip, c), src=x_refs[w]) for j, chip in enumerate(chips)]
            passed = [copy(w, 4 + j, (*chip, c), sibling) for j, chip in enumerate(chips)]
            if p == 0:
                mine.start()
                for cp in first:
                    cp.start()
            elif p == 1:
                for j, chip in enumerate(chips):
                    copy(w, 1 + j, (*chip, c), me).wait_recv()
                    passed[j].start()
            else:
                copy(w, 0, sibling, me).wait_recv()
                for j, chip in enumerate(chips):
                    copy(w, 4 + j, (*chip, 1 - c), me).wait_recv()
                for cp in first + passed:
                    cp.wait_send()
                mine.wait()


class _Scatter:
    def __init__(self, grads, whole=()):
        self.n_sliced = len(grads)
        self.operands = list(grads) + list(whole)
        self.n = len(self.operands)
        self.out_shape = [jax.ShapeDtypeStruct(g.shape, g.dtype) for g in grads]
        self.out_shape += [jax.ShapeDtypeStruct((N_DEV,) + a.shape, a.dtype) for a in whole]
        n_sem = (N_DEV - 1) * self.n
        self.scratch = [pltpu.SemaphoreType.DMA((n_sem,)), pltpu.SemaphoreType.DMA((n_sem,)),
                        pltpu.SemaphoreType.DMA((self.n,))]

    def phase(self, p, in_refs, out_refs, sems):
        if p == 1:
            return
        send_sems, recv_sems, local_sems = sems
        x, y, c = _mesh_pos()
        me = 4 * x + 2 * y + c

        def src(w, dev):
            return in_refs[w].at[dev] if w < self.n_sliced else in_refs[w]

        for w in range(self.n):
            own = None if local_sems is None else pltpu.make_async_copy(src(w, me), out_refs[w].at[me], local_sems.at[w])
            sends, recvs = [], []
            for r in range(1, N_DEV):
                px = 1 - x if r & 4 else x
                py = 1 - y if r & 2 else y
                pc = 1 - c if r & 1 else c
                peer, pidx = (px, py, pc), 4 * px + 2 * py + pc
                k = (N_DEV - 1) * w + r - 1
                sends.append(pltpu.make_async_remote_copy(
                    src_ref=src(w, pidx), dst_ref=out_refs[w].at[me], send_sem=send_sems.at[k],
                    recv_sem=recv_sems.at[k], device_id=peer, device_id_type=pl.DeviceIdType.MESH))
                recvs.append(pltpu.make_async_remote_copy(
                    src_ref=src(w, me), dst_ref=out_refs[w].at[pidx], send_sem=send_sems.at[k],
                    recv_sem=recv_sems.at[k], device_id=peer, device_id_type=pl.DeviceIdType.MESH))
            if p == 0:
                if own is not None:
                    own.start()
                for cp in sends:
                    cp.start()
            else:
                for cp in recvs:
                    cp.wait_recv()
                for cp in sends:
                    cp.wait_send()
                if own is not None:
                    own.wait()


class _SplitScatter:
    def __init__(self, ex, name):
        self.ex, self.name = ex, name

    def _specs(self):
        ex = self.ex
        hbm = pl.BlockSpec(memory_space=pltpu.HBM)
        sem = pl.BlockSpec(memory_space=pltpu.SEMAPHORE)
        effect = pltpu.CompilerParams(has_side_effects=pltpu.SideEffectType.DATAFLOW_SIDE_EFFECTING)
        buffers = [pltpu.HBM(a.shape, a.dtype) for a in ex.operands] + [pltpu.HBM(s.shape, s.dtype) for s in ex.out_shape]
        return hbm, sem, effect, buffers

    def start(self):
        ex, n = self.ex, self.ex.n
        n_sem = (N_DEV - 1) * n
        hbm, sem, effect, buffers = self._specs()
        in_hbm = lambda a: pltpu.with_memory_space_constraint(a, pltpu.HBM)

        me = 4 * lax.axis_index("x") + 2 * lax.axis_index("y") + lax.axis_index("c")
        lands = []
        for w, (a, s) in enumerate(zip(ex.operands, ex.out_shape)):
            mine = lax.dynamic_index_in_dim(a, me, 0, keepdims=True) if w < ex.n_sliced else a[None]
            lands.append(lax.dynamic_update_slice_in_dim(lax.empty(s.shape, s.dtype), mine, me, 0))

        def start_body(*refs):
            ex.phase(0, refs[:n], refs[n:2 * n], (refs[2 * n], refs[2 * n + 1], None))
            refs[-1][...] = jnp.zeros_like(refs[-1])

        self.started = pl.pallas_call(
            start_body, name=self.name + "_start",
            out_shape=[pltpu.SemaphoreType.DMA((n_sem,)), pltpu.SemaphoreType.DMA((n_sem,))] + buffers
            + [jax.ShapeDtypeStruct((8, LANES), F32)],
            in_specs=[hbm] * (2 * n), out_specs=[sem, sem] + [hbm] * (2 * n) + [pl.BlockSpec(memory_space=pltpu.VMEM)],
            input_output_aliases={i: 2 + i for i in range(2 * n)}, compiler_params=effect,
        )(*[in_hbm(a) for a in ex.operands], *[in_hbm(a) for a in lands])
        return self.started[-1]

    def wait(self, after):
        ex, n = self.ex, self.ex.n
        hbm, sem, effect, buffers = self._specs()
        anyspec = pl.BlockSpec(memory_space=pl.ANY)

        def wait_body(*refs):
            ex.phase(2, refs[:n], refs[n:2 * n], (refs[2 * n], refs[2 * n + 1], None))

        done = pl.pallas_call(
            wait_body, name=self.name + "_wait", out_shape=buffers,
            in_specs=[hbm] * (2 * n) + [sem, sem] + [anyspec] * len(after), out_specs=[hbm] * (2 * n),
            input_output_aliases={i: i for i in range(2 * n)}, compiler_params=effect,
        )(*self.started[2:2 + 2 * n], self.started[0], self.started[1], *after)
        return done[n:]


def _exchange_alone(ex, *, name):
    n = ex.n

    def body(*refs):
        for p in range(3):
            ex.phase(p, refs[:n], refs[n:2 * n], refs[2 * n:])

    anyspec = pl.BlockSpec(memory_space=pl.ANY)
    return pl.pallas_call(body, name=name, out_shape=ex.out_shape, in_specs=[anyspec] * n,
                          out_specs=[anyspec] * n, scratch_shapes=ex.scratch)(*ex.operands)


def _adam_step(w_ref, p_ref, m_ref, v_ref, g_ref, d_ref, nm_ref, nv_ref):
    g = p_ref[0].astype(F32)
    for j in range(1, N_DEV):
        g = g + p_ref[j].astype(F32)
    g_ref[...] = g
    nm = ADAM_B1 * m_ref[...] + (1.0 - ADAM_B1) * g
    nv = ADAM_B2 * v_ref[...] + (1.0 - ADAM_B2) * (g * g)
    nm_ref[...] = nm
    nv_ref[...] = nv
    m_hat = nm / (1.0 - ADAM_B1 ** ADAM_STEP)
    v_hat = nv / (1.0 - ADAM_B2 ** ADAM_STEP)
    d_ref[...] = -ADAM_LR * (m_hat / (jnp.sqrt(v_hat) + ADAM_EPS) + ADAM_WD * w_ref[...])


def _adamw_vectors(ws, parts, ms, vs, *, name):
    n = len(ws)

    def body(*refs):
        w_refs, p_refs, m_refs, v_refs = (refs[i * n:(i + 1) * n] for i in range(4))
        outs = refs[4 * n:]
        for i in range(n):
            _adam_step(w_refs[i], p_refs[i], m_refs[i], v_refs[i], *outs[4 * i:4 * i + 4])

    return pl.pallas_call(
        body, name=name,
        out_shape=[jax.ShapeDtypeStruct(w.shape, F32) for w in ws for _ in range(4)],
    )(*ws, *parts, *ms, *vs)


def _adamw(w, parts, m, v, after, *, name):
    G, R, n = w.shape
    tn = 512 if (n > 512 and n % 512 == 0) else n
    tr = R
    for t in range(16, R, 16):
        if R % t == 0 and t * tn <= ADAM_BLOCK_CAP:
            tr = t
    if R * tn <= ADAM_BLOCK_CAP:
        tr = R

    def body(w_ref, p_ref, m_ref, v_ref, after_ref, g_ref, d_ref, nm_ref, nv_ref):
        _adam_step(w_ref, p_ref, m_ref, v_ref, g_ref, d_ref, nm_ref, nv_ref)

    blk = pl.BlockSpec((None, tr, tn), lambda g, i, j: (g, i, j))
    return pl.pallas_call(
        body, name=name, grid=(G, R // tr, n // tn),
        in_specs=[blk, pl.BlockSpec((N_DEV, None, tr, tn), lambda g, i, j: (0, g, i, j)), blk, blk,
                  pl.BlockSpec((8, LANES), lambda g, i, j: (0, 0))],
        out_specs=[blk, blk, blk, blk],
        out_shape=[jax.ShapeDtypeStruct((G, R, n), F32)] * 4,
        compiler_params=_params(("parallel", "parallel", "parallel")),
    )(w, parts, m, v, after)


def _pad_last(a, width):
    return jnp.pad(a, [(0, 0)] * (a.ndim - 1) + [(0, width - a.shape[-1])])


def _cols_of(g):
    return g.transpose(1, 0, 2).reshape(g.shape[1], N_DEV * g.shape[2])


def _col_shards(w):
    return w.reshape(w.shape[0], N_DEV, w.shape[1] // N_DEV).transpose(1, 0, 2)


def kernel(x, positions, ffn1_pre_w, ffn1_w1, ffn1_w2, ffn1_post_w, mix_pre_w, w_in, mla_q_norm_w, mla_w_uq, mla_kv_norm_w, mla_w_ukv, ret_gn_w, w_branch_mla, w_branch_ret, w_out, mix_post_w, ffn2_pre_w, ffn2_w1, ffn2_w2, ffn2_post_w, loss_target, m_ffn1_pre_w, m_ffn1_w1, m_ffn1_w2, m_ffn1_post_w, m_mix_pre_w, m_w_in, m_mla_q_norm_w, m_mla_w_uq, m_mla_kv_norm_w, m_mla_w_ukv, m_ret_gn_w, m_w_branch_mla, m_w_branch_ret, m_w_out, m_mix_post_w, m_ffn2_pre_w, m_ffn2_w1, m_ffn2_w2, m_ffn2_post_w, v_ffn1_pre_w, v_ffn1_w1, v_ffn1_w2, v_ffn1_post_w, v_mix_pre_w, v_w_in, v_mla_q_norm_w, v_mla_w_uq, v_mla_kv_norm_w, v_mla_w_ukv, v_ret_gn_w, v_w_branch_mla, v_w_branch_ret, v_w_out, v_mix_post_w, v_ffn2_pre_w, v_ffn2_w1, v_ffn2_w2, v_ffn2_post_w):
    T, D = x.shape[1], x.shape[2]
    h0 = x[0]
    tgt = loss_target[0]
    pos = positions.reshape(T, 1).astype(F32)

    big = [("ffn1_w1", ffn1_w1, m_ffn1_w1, v_ffn1_w1), ("ffn1_w2", ffn1_w2, m_ffn1_w2, v_ffn1_w2),
           ("w_in", w_in, m_w_in, v_w_in), ("mla_w_uq", mla_w_uq, m_mla_w_uq, v_mla_w_uq),
           ("mla_w_ukv", mla_w_ukv, m_mla_w_ukv, v_mla_w_ukv),
           ("w_branch_mla", w_branch_mla, m_w_branch_mla, v_w_branch_mla),
           ("w_branch_ret", w_branch_ret, m_w_branch_ret, v_w_branch_ret),
           ("w_out", w_out, m_w_out, v_w_out),
           ("ffn2_w1", ffn2_w1, m_ffn2_w1, v_ffn2_w1), ("ffn2_w2", ffn2_w2, m_ffn2_w2, v_ffn2_w2)]
    small = [("ffn1_pre_w", ffn1_pre_w, m_ffn1_pre_w, v_ffn1_pre_w), ("ffn1_post_w", ffn1_post_w, m_ffn1_post_w, v_ffn1_post_w),
             ("mix_pre_w", mix_pre_w, m_mix_pre_w, v_mix_pre_w), ("mla_q_norm_w", mla_q_norm_w, m_mla_q_norm_w, v_mla_q_norm_w),
             ("mla_kv_norm_w", mla_kv_norm_w, m_mla_kv_norm_w, v_mla_kv_norm_w), ("ret_gn_w", ret_gn_w, m_ret_gn_w, v_ret_gn_w),
             ("mix_post_w", mix_post_w, m_mix_post_w, v_mix_post_w), ("ffn2_pre_w", ffn2_pre_w, m_ffn2_pre_w, v_ffn2_pre_w),
             ("ffn2_post_w", ffn2_post_w, m_ffn2_post_w, v_ffn2_post_w)]

    half = ffn1_w2.shape[1]
    hp = -(-half // LANES) * LANES

    def rows_view(w):
        return w[0].T

    def send_w1(w):
        return jnp.pad(rows_view(w).reshape(2, half, D), ((0, 0), (0, hp - half), (0, 0))).reshape(2 * hp, D).astype(BF16)

    def send_w2(w):
        return jnp.pad(w[0], ((0, hp - half), (0, 0))).astype(BF16)

    mixer = ["w_in", "mla_w_uq", "mla_w_ukv", "w_branch_mla", "w_branch_ret", "w_out"]
    uq_w = MLA_NOPE + MLA_ROPE
    mixer_send = [rows_view(w_in).astype(BF16), jnp.pad(rows_view(mla_w_uq), ((0, HP - uq_w), (0, 0))).astype(BF16),
                  mla_w_ukv[0].astype(BF16), w_branch_mla[0].astype(BF16), w_branch_ret[0].astype(BF16),
                  w_out[0].astype(BF16)]

    w1a, w2a = _exchange_alone(_Gather([send_w1(ffn1_w1), send_w2(ffn1_w2)]), name="gather_ffn1")
    w2a = w2a.reshape(N_DEV // 2, 2 * hp, D)
    u1, f1, h1, a0, *got = _ffn_fwd(h0, ffn1_pre_w, w1a, w2a, ffn1_post_w, None, name="ffn1_fwd_gather_mixer",
                                exchange=_Gather(mixer_send))
    fw = dict(zip(mixer, got))

    wi = fw["w_in"].reshape(-1, D)
    cq_w, ckv_w, kr_w = wi[0:384], wi[384:640], wi[640:672]
    rq_w, rk_w = wi[672:928], wi[928:1184]
    rv_w, rg_w = wi[1184:1696], wi[1696:2208]
    gm_w, gr_w = wi[2208:2208 + D], wi[2208 + D:2208 + 2 * D]
    zer = lambda n: jnp.zeros((n, D), BF16)
    head_rows = lambda a, h: jnp.pad(a.reshape(h, -1, D), ((0, 0), (0, HP - a.shape[0] // h), (0, 0))).reshape(h * HP, D)
    w_in_p = jnp.concatenate([head_rows(rq_w, RET_HEADS), head_rows(rk_w, RET_HEADS), rv_w, rg_w,
                              cq_w, ckv_w, zer(MLA_NOPE), kr_w, zer(HP - MLA_NOPE - MLA_ROPE), zer(AW - 768),
                              gm_w, gr_w], axis=0)
    w_uq_p = fw["mla_w_uq"].reshape(QW, MLA_Q_RANK)
    ukv = fw["mla_w_ukv"].transpose(1, 0, 2)
    w_kv_p = jnp.concatenate([_pad_last(ukv[:, :, :MLA_NOPE], HP).reshape(MLA_KV_RANK, QW),
                              _pad_last(ukv[:, :, MLA_NOPE:], HP).reshape(MLA_KV_RANK, QW)], axis=1)
    w_bm_p = jnp.pad(_cols_of(fw["w_branch_mla"]).reshape(MLA_HEADS, MLA_V, D),
                     ((0, 0), (0, HP - MLA_V), (0, 0))).reshape(QW, D)
    w_br, w_o = _cols_of(fw["w_branch_ret"]), fw["w_out"].reshape(D, D)
    tab_mla = _rope_table(MLA_NOPE, MLA_ROPE // 2)
    tab_ret = _rope_table(0, RET_DK // 2)

    proj, a1 = _rms_matmul(h1, mix_pre_w, w_in_p, name="mixer_in_proj")
    q, k, v = _mla_prep_fwd(proj, pos, mla_q_norm_w, mla_kv_norm_w, w_uq_p, w_kv_p, tab_mla, name="mla_prep_fwd")
    o, lse, w1b, w2b = _flash_fwd(q, k, v, name="mla_attn_fwd_gather_ffn2",
                                  exchange=_Gather([send_w1(ffn2_w1), send_w2(ffn2_w2)]))
    w2b = w2b.reshape(N_DEV // 2, 2 * hp, D)
    ypre, yn, rprev = _ret_fwd(proj, pos, tab_ret, name="retention_fwd")
    omla, oret, m, h2 = _merge_fwd(o, yn, proj, ret_gn_w, w_bm_p, w_br, w_o, h1, mix_post_w, name="merge_fwd")
    u2, f2, _, a2, dy, lossp = _ffn_fwd(h2, ffn2_pre_w, w1b, w2b, ffn2_post_w, tgt, name="ffn2_fwd_loss")

    def grad(x, dy, tag, after=None):
        return _matmul_tn(x if x.ndim == 3 else x[None], dy if dy.ndim == 3 else dy[None], name=tag, after=after)

    g2, du2, df2, dh2, gpost2, gpre2 = _ffn_bwd(dy, f2, ffn2_post_w, h2, ffn2_pre_w, u2, w2b, w1b, name="ffn2_bwd")
    dw1b, = grad(du2.reshape(N_DEV, T, 2 * hp), a2, "ffn2_dw1")
    dw2b = grad(g2, df2, "ffn2_dw2")[0].reshape(N_DEV, hp, D)
    (dgm, dgr, do, delta, drg, dyn, gpostm, ggn, dw_out, dw_bm_p, dw_br) = _merge_bwd(
        dh2, m, mix_post_w, omla, oret, proj, yn, ret_gn_w, o, w_o, w_bm_p, w_br, name="merge_bwd")
    sc_ffn2 = _SplitScatter(_Scatter([dw1b, dw2b]), "scatter_ffn2")
    dq, dk, dv = _flash_bwd(q, k, v, do, lse, delta, name="mla_attn_bwd", after=sc_ffn2.start())
    da, gqn, gkvn, dw_uq_p, dw_kv_p = _mla_prep_bwd(dq, dk, dv, proj, pos, mla_q_norm_w, mla_kv_norm_w, w_uq_p, w_kv_p, tab_mla, name="mla_prep_bwd")
    drq, drk, drv = _ret_bwd(dyn, ypre, proj, pos, tab_ret, rprev, name="retention_bwd")
    dproj = jnp.concatenate([drq, drk, drv, drg, da, dgm, dgr], axis=1)
    dw_in_p = grad(dproj, a1, "dw_in")[0][0]

    dw_uq = dw_uq_p.reshape(MLA_HEADS, HP, MLA_Q_RANK)[:, :uq_w]
    dkp = dw_kv_p[:, :QW].reshape(MLA_KV_RANK, MLA_HEADS, HP)[:, :, :MLA_NOPE]
    dvp = dw_kv_p[:, QW:].reshape(MLA_KV_RANK, MLA_HEADS, HP)[:, :, :MLA_V]
    dw_ukv = jnp.concatenate([dkp, dvp], axis=2).transpose(1, 0, 2)
    dw_bm = dw_bm_p.reshape(MLA_HEADS, HP, D)[:, :MLA_V].reshape(MLA_HEADS * MLA_V, D)
    small_mixer_grads = [dw_uq, dw_ukv, _col_shards(dw_bm), _col_shards(dw_br), dw_out.reshape(N_DEV, D // N_DEV, D)]
    sc_small = _SplitScatter(_Scatter(small_mixer_grads), "scatter_mixer_small")
    dh1, gmixpre = _proj_bwd(dproj, w_in_p, h1, mix_pre_w, dh2, name="mixer_in_bwd", after=sc_small.start())
    unhead = lambda a, h, wd: a.reshape(h, HP, D)[:, :wd].reshape(h * wd, D)
    c0 = 4 * RW
    dw_in = jnp.concatenate([
        dw_in_p[c0:c0 + 384], dw_in_p[c0 + 384:c0 + 640], dw_in_p[c0 + 640 + MLA_NOPE:c0 + 640 + MLA_NOPE + MLA_ROPE],
        unhead(dw_in_p[0:RW], RET_HEADS, RET_DK), unhead(dw_in_p[RW:2 * RW], RET_HEADS, RET_DK),
        dw_in_p[2 * RW:3 * RW], dw_in_p[3 * RW:4 * RW],
        dw_in_p[PROJ_FIXED:PROJ_FIXED + D], dw_in_p[PROJ_FIXED + D:PROJ_FIXED + 2 * D]], axis=0).reshape(N_DEV, -1, D)
    sc_w_in = _SplitScatter(_Scatter([dw_in]), "scatter_w_in")
    g1, du1, df1, dx, gpost1, gpre1 = _ffn_bwd(
        dh1, f1, ffn1_post_w, h0, ffn1_pre_w, u1, w2a, w1a, name="ffn1_bwd", after=sc_w_in.start())
    dw2a = grad(g1, df1, "ffn1_dw2")[0].reshape(N_DEV, hp, D)
    sc_dw2a = _SplitScatter(_Scatter([dw2a]), "scatter_ffn1_dw2")
    dw1a, = grad(du1.reshape(N_DEV, T, 2 * hp), a0, "ffn1_dw1", after=sc_dw2a.start())

    small_g = {"ffn1_pre_w": gpre1, "ffn1_post_w": gpost1, "mix_pre_w": gmixpre, "mla_q_norm_w": gqn,
               "mla_kv_norm_w": gkvn, "ret_gn_w": ggn, "mix_post_w": gpostm, "ffn2_pre_w": gpre2, "ffn2_post_w": gpost2}
    sc_last = _SplitScatter(_Scatter([dw1a], whole=[small_g[nm] for nm, *_ in small] + [lossp]), "scatter_ffn1_dw1")
    token = sc_last.start()
    recv_ffn2 = sc_ffn2.wait([token])
    recv_mixer = sc_w_in.wait([token]) + sc_small.wait([token])
    recv_w2a, = sc_dw2a.wait([token])
    parts = dict(zip(mixer, recv_mixer))
    parts.update(ffn1_w2=recv_w2a, ffn2_w1=recv_ffn2[0], ffn2_w2=recv_ffn2[1])
    as_is = (lambda a: a, lambda p: p[:, None], lambda a: a)
    views = {nm: as_is for nm, *_ in big}
    for nm in ("ffn1_w1", "ffn2_w1"):
        views[nm] = (lambda a: rows_view(a).reshape(2, half, D), lambda p: p.reshape(N_DEV, 2, hp, D),
                     lambda a: a.reshape(2 * half, D).T[None])
    for nm in ("w_in", "mla_w_uq"):
        views[nm] = (lambda a: rows_view(a)[None], lambda p: p[:, None], lambda a: a[0].T[None])

    def update(nm, w, m_, v_, after):
        to_view, parts_view, back = views[nm]
        return [back(a) for a in _adamw(to_view(w), parts_view(parts[nm]), to_view(m_), to_view(v_), after,
                                        name="adamw_" + nm)]

    big_out = {nm: update(nm, w, m_, v_, token) for nm, w, m_, v_ in big if nm != "ffn1_w1"}
    recv_w1a, *small_parts, loss_parts = sc_last.wait([d[0] for d in big_out.values()])
    loss = jnp.sum(loss_parts[:, ::8, 0])
    parts["ffn1_w1"] = recv_w1a
    big_out["ffn1_w1"] = update("ffn1_w1", ffn1_w1, m_ffn1_w1, v_ffn1_w1, jnp.zeros((8, LANES), F32))
    small_out = _adamw_vectors([w for _, w, _, _ in small], small_parts, [a for _, _, a, _ in small],
                               [a for _, _, _, a in small], name="adamw_replicated")

    order = ["ffn1_pre_w", "ffn1_w1", "ffn1_w2", "ffn1_post_w", "mix_pre_w", "w_in", "mla_q_norm_w", "mla_w_uq",
             "mla_kv_norm_w", "mla_w_ukv", "ret_gn_w", "w_branch_mla", "w_branch_ret", "w_out", "mix_post_w",
             "ffn2_pre_w", "ffn2_w1", "ffn2_w2", "ffn2_post_w"]
    outs = [loss, dx[None]]
    for i in range(4):
        both = {nm: big_out[nm][i] for nm in big_out}
        both.update({nm: small_out[4 * j + i] for j, (nm, *_) in enumerate(small)})
        outs += [both[nm] for nm in order]
    return tuple(outs)
```

```python
import math

import numpy as np
import jax
import jax.numpy as jnp
from jax import lax
from jax.experimental import pallas as pl
from jax.experimental.pallas import tpu as pltpu

F32, BF16 = jnp.float32, jnp.bfloat16

MLA_HEADS, MLA_NOPE, MLA_ROPE, MLA_V = 8, 64, 32, 64
MLA_Q_RANK, MLA_KV_RANK = 384, 256
RET_HEADS, RET_DK, RET_DV = 4, 64, 128
ROPE_BASE, NORM_EPS, GN_EPS = 10000.0, 1e-6, 1e-6
ADAM_LR, ADAM_B1, ADAM_B2, ADAM_EPS, ADAM_WD, ADAM_STEP = 0.001, 0.9, 0.999, 1e-08, 0.01, 10
ATTN_SCALE = 1.0 / math.sqrt(MLA_NOPE + MLA_ROPE)

N_DEV = 8
LANES = 128
HP = LANES
QW = MLA_HEADS * HP
RW = RET_HEADS * HP
AW = 1024
PROJ_FIXED = 4 * RW + AW
NEG = -1e30

TOKEN_TILE = 512
ATTN_TILE = 1024
ATTN_CHAINS = 2
FFN_CHAINS = 2
RET_TILE = 256
PROJ_TILE_CAP = 2560
GRAD_TILE_CAP = 1408
GRAD_TOKEN_TILE = 4096
ADAM_BLOCK_CAP = 192 * 1024
MERGE_TILE = 256
VMEM_LIMIT = 56 * 1024 * 1024


def _tile(n, cap, mult=LANES):
    if n <= cap:
        return n
    best = None
    for t in range(mult, cap + 1, mult):
        if n % t == 0:
            best = t
    assert best is not None, (n, cap, mult)
    return best


def _params(sem):
    return pltpu.CompilerParams(dimension_semantics=sem, vmem_limit_bytes=VMEM_LIMIT)


def _dot(a, b):
    return lax.dot_general(a, b, (((1,), (0,)), ((), ())), preferred_element_type=F32)


def _dot_nt(a, b):
    return lax.dot_general(a, b, (((1,), (1,)), ((), ())), preferred_element_type=F32)


def _dot_tn(a, b):
    return lax.dot_general(a, b, (((0,), (0,)), ((), ())), preferred_element_type=F32)


def _sigmoid(x):
    return pl.reciprocal(1.0 + jnp.exp(-x), approx=True)


def _rms_fwd(x, w):
    r = lax.rsqrt(jnp.mean(x * x, axis=-1, keepdims=True) + NORM_EPS)
    return x * r * w


def _rms_bwd(x, w, dy):
    r = lax.rsqrt(jnp.mean(x * x, axis=-1, keepdims=True) + NORM_EPS)
    xh = x * r
    g = dy * w
    dx = r * (g - xh * jnp.mean(g * xh, axis=-1, keepdims=True))
    return dx, jnp.sum(dy * xh, axis=0, keepdims=True)


def _rope_table(first, half):
    inv = (np.float32(ROPE_BASE) ** (-(np.arange(half, dtype=np.float32) / np.float32(half)))).astype(np.float32)
    tab = np.zeros((8, LANES), np.float32)
    tab[0, first:first + half] = inv
    tab[0, first + half:first + 2 * half] = inv
    tab[1, first:first + half] = -1.0
    tab[2, first + half:first + 2 * half] = 1.0
    return jnp.asarray(tab)


def _rope_cs(pos, tab_ref):
    ang = pos * tab_ref[0:1, :]
    s = jnp.sin(ang)
    return jnp.cos(ang), s * tab_ref[1:2, :], s * tab_ref[2:3, :]


def _rope(x, cs, half, inverse=False):
    c, s1, s2 = cs
    a = pltpu.roll(x, LANES - half, 1) * s1 + pltpu.roll(x, half, 1) * s2
    return x * c - a if inverse else x * c + a


def _call(body, *, name, grid, in_specs, out_specs, out_shape, scratch_shapes, args, exchange=None, after=None):
    sem = ("arbitrary",) * len(grid)
    anyspec = pl.BlockSpec(memory_space=pl.ANY)
    if exchange is None and after is not None:
        n_own = len(in_specs)

        def behind(*refs):
            body(*refs[:n_own], *refs[n_own + 1:])

        return pl.pallas_call(behind, name=name, grid=grid, in_specs=list(in_specs) + [anyspec], out_specs=out_specs,
                              out_shape=out_shape, scratch_shapes=scratch_shapes, compiler_params=_params(sem))(*args, after)
    if exchange is None:
        return pl.pallas_call(body, name=name, grid=grid, in_specs=in_specs, out_specs=out_specs,
                              out_shape=out_shape, scratch_shapes=scratch_shapes, compiler_params=_params(sem))(*args)
    n_in, n_out, e = len(in_specs), len(out_specs), exchange.n
    total = math.prod(grid)

    def carried(*refs):
        own = refs[:n_in] + refs[n_in + e:n_in + e + n_out] + refs[n_in + 2 * e + n_out:len(refs) - 3]
        ex_refs = (refs[n_in:n_in + e], refs[n_in + e + n_out:n_in + 2 * e + n_out], refs[len(refs) - 3:])
        step = pl.program_id(0)
        for d in range(1, len(grid)):
            step = step * grid[d] + pl.program_id(d)

        @pl.when(step == 0)
        def _():
            exchange.phase(0, *ex_refs)

        @pl.when(step == (3 * total) // 4)
        def _():
            exchange.phase(1, *ex_refs)

        body(*own)

        @pl.when(step == total - 1)
        def _():
            exchange.phase(2, *ex_refs)

    return pl.pallas_call(
        carried, name=name, grid=grid, in_specs=list(in_specs) + [anyspec] * e,
        out_specs=list(out_specs) + [anyspec] * e, out_shape=list(out_shape) + exchange.out_shape,
        scratch_shapes=list(scratch_shapes) + exchange.scratch, compiler_params=_params(sem),
    )(*args, *exchange.operands)


def _ffn_fwd(h, pre_w, w1, w2, post_w, target, *, name, exchange=None):
    T, D = h.shape
    nk, ck = w2.shape[0], w2.shape[1]
    tT = min(TOKEN_TILE, T)
    nT = T // tT
    with_loss = target is not None

    def body(*refs):
        if with_loss:
            (h_ref, pre_ref, w1g_ref, w1u_ref, w2_ref, post_ref, tgt_ref,
             u_ref, f_ref, ho_ref, a_s, dy_ref, loss_ref, acc) = refs
        else:
            (h_ref, pre_ref, w1g_ref, w1u_ref, w2_ref, post_ref,
             u_ref, f_ref, ho_ref, a_s, acc) = refs
        k = pl.program_id(1)

        @pl.when(k == 0)
        def _():
            a_s[...] = _rms_fwd(h_ref[...], pre_ref[...]).astype(BF16)
            acc[...] = jnp.zeros_like(acc)

        for c in range(FFN_CHAINS):
            rs = slice(c * (tT // FFN_CHAINS), (c + 1) * (tT // FFN_CHAINS))
            a = a_s[rs, :]
            ug = _dot_nt(a, w1g_ref[...])
            uu = _dot_nt(a, w1u_ref[...])
            u_ref[0, rs, :] = ug.astype(BF16)
            u_ref[1, rs, :] = uu.astype(BF16)
            acc[rs, :] += _dot((ug * _sigmoid(ug) * uu).astype(BF16), w2_ref[...])

        @pl.when(k == nk - 1)
        def _():
            f = acc[...]
            f_ref[...] = f
            ho = h_ref[...] + 0.5 * _rms_fwd(f, post_ref[...])
            ho_ref[...] = ho
            if with_loss:
                e = ho - tgt_ref[...]
                dy_ref[...] = e * (1.0 / D)
                loss_ref[...] = jnp.full(loss_ref.shape, (0.5 / D) * jnp.sum(e * e), F32)

    row = pl.BlockSpec((tT, D), lambda i, k: (i, 0))
    vec = pl.BlockSpec((1, D), lambda i, k: (0, 0))
    in_specs = [row, vec,
                pl.BlockSpec((None, ck, D), lambda i, k: (k, 0, 0)),
                pl.BlockSpec((None, ck, D), lambda i, k: (nk + k, 0, 0)),
                pl.BlockSpec((None, ck, D), lambda i, k: (k, 0, 0)),
                vec]
    out_shape = [jax.ShapeDtypeStruct((2, nk, T, ck), BF16),
                 jax.ShapeDtypeStruct((T, D), F32),
                 jax.ShapeDtypeStruct((T, D), F32),
                 jax.ShapeDtypeStruct((T, D), BF16)]
    out_specs = [pl.BlockSpec((2, None, tT, ck), lambda i, k: (0, k, i, 0)), row, row, row]
    args = [h, pre_w, w1, w1, w2, post_w]
    if with_loss:
        in_specs.append(row)
        args.append(target)
        out_shape += [jax.ShapeDtypeStruct((T, D), F32), jax.ShapeDtypeStruct((nT * 8, LANES), F32)]
        out_specs += [row, pl.BlockSpec((8, LANES), lambda i, k: (i, 0))]
    return _call(body, name=name, grid=(nT, nk), in_specs=in_specs, out_specs=out_specs, out_shape=out_shape,
                 scratch_shapes=[pltpu.VMEM((tT, D), F32)], args=args, exchange=exchange)


def _ffn_bwd(dho, f, post_w, h, pre_w, u, w2, w1, *, name, exchange=None, after=None):
    T, D = h.shape
    nk, ck = w2.shape[0], w2.shape[1]
    tT = min(TOKEN_TILE, T)
    nT = T // tT

    def body(dho_ref, f_ref, post_ref, h_ref, pre_ref, u_ref, w2_ref, w1g_ref, w1u_ref,
             g_ref, du_ref, df_s, dh_ref, gpost_ref, gpre_ref, da_acc):
        i, k = pl.program_id(0), pl.program_id(1)

        @pl.when(jnp.logical_and(i == 0, k == 0))
        def _():
            gpost_ref[...] = jnp.zeros_like(gpost_ref)
            gpre_ref[...] = jnp.zeros_like(gpre_ref)

        @pl.when(k == 0)
        def _():
            dx, dw = _rms_bwd(f_ref[...], post_ref[...], 0.5 * dho_ref[...])
            df_s[...] = dx.astype(BF16)
            gpost_ref[...] += dw
            da_acc[...] = jnp.zeros_like(da_acc)

        groups = [slice(c * (tT // FFN_CHAINS), (c + 1) * (tT // FFN_CHAINS)) for c in range(FFN_CHAINS)]
        dgs = [_dot_nt(df_s[rs, :], w2_ref[...]) for rs in groups]
        for rs, dg in zip(groups, dgs):
            ug = u_ref[0, rs, :].astype(F32)
            uu = u_ref[1, rs, :].astype(F32)
            sg = _sigmoid(ug)
            sl = ug * sg
            g_ref[rs, :] = (sl * uu).astype(BF16)
            dug = (dg * uu * (sg + sl * (1.0 - sg))).astype(BF16)
            duu = (dg * sl).astype(BF16)
            du_ref[0, rs, :] = dug
            du_ref[1, rs, :] = duu
            da_acc[rs, :] += _dot(dug, w1g_ref[...]) + _dot(duu, w1u_ref[...])

        @pl.when(k == nk - 1)
        def _():
            dx, dw = _rms_bwd(h_ref[...], pre_ref[...], da_acc[...])
            dh_ref[...] = dho_ref[...] + dx
            gpre_ref[...] += dw

    row = pl.BlockSpec((tT, D), lambda i, k: (i, 0))
    vec = pl.BlockSpec((1, D), lambda i, k: (0, 0))
    return _call(
        body, name=name, grid=(nT, nk),
        in_specs=[row, row, vec, row, vec,
                  pl.BlockSpec((2, None, tT, ck), lambda i, k: (0, k, i, 0)),
                  pl.BlockSpec((None, ck, D), lambda i, k: (k, 0, 0)),
                  pl.BlockSpec((None, ck, D), lambda i, k: (k, 0, 0)),
                  pl.BlockSpec((None, ck, D), lambda i, k: (nk + k, 0, 0))],
        out_specs=[pl.BlockSpec((None, tT, ck), lambda i, k: (k, i, 0)),
                   pl.BlockSpec((2, None, tT, ck), lambda i, k: (0, k, i, 0)),
                   row, row, vec, vec],
        out_shape=[jax.ShapeDtypeStruct((nk, T, ck), BF16),
                   jax.ShapeDtypeStruct((2, nk, T, ck), BF16),
                   jax.ShapeDtypeStruct((T, D), BF16),
                   jax.ShapeDtypeStruct((T, D), F32),
                   jax.ShapeDtypeStruct((1, D), F32),
                   jax.ShapeDtypeStruct((1, D), F32)],
        scratch_shapes=[pltpu.VMEM((tT, D), F32)],
        args=(dho, f, post_w, h, pre_w, u, w2, w1, w1), exchange=exchange, after=after)


def _matmul_tn(x, dy, *, name, exchange=None, after=None):
    Px, T, K = x.shape
    Py, _, N = dy.shape
    P = max(Px, Py)
    tT, tK, tN = min(GRAD_TOKEN_TILE, T), _tile(K, GRAD_TILE_CAP), _tile(N, GRAD_TILE_CAP)
    nt = T // tT

    def body(x_ref, dy_ref, o_ref, acc):
        if nt == 1:
            o_ref[...] = _dot_tn(x_ref[...], dy_ref[...]).astype(BF16)
            return
        t = pl.program_id(3)

        @pl.when(t == 0)
        def _():
            acc[...] = jnp.zeros_like(acc)

        acc[...] += _dot_tn(x_ref[...], dy_ref[...])

        @pl.when(t == nt - 1)
        def _():
            o_ref[...] = acc[...].astype(BF16)

    return _call(
        body, name=name, grid=(P, K // tK, N // tN, nt),
        in_specs=[pl.BlockSpec((None, tT, tK), lambda p, a, b, t: (p if Px > 1 else 0, t, a)),
                  pl.BlockSpec((None, tT, tN), lambda p, a, b, t: (p if Py > 1 else 0, t, b))],
        out_specs=[pl.BlockSpec((None, tK, tN), lambda p, a, b, t: (p, a, b))],
        out_shape=[jax.ShapeDtypeStruct((P, K, N), BF16)],
        scratch_shapes=[pltpu.VMEM((tK, tN) if nt > 1 else (8, LANES), F32)], args=(x, dy),
        exchange=exchange, after=after)


def _rms_matmul(h, wn, w, *, name):
    T, D = h.shape
    N = w.shape[0]
    tT, tN = min(TOKEN_TILE, T), _tile(N, PROJ_TILE_CAP)

    def body(h_ref, wn_ref, w_ref, y_ref, a_ref):
        @pl.when(pl.program_id(1) == 0)
        def _():
            a_ref[...] = _rms_fwd(h_ref[...], wn_ref[...]).astype(BF16)

        y_ref[...] = _dot_nt(a_ref[...], w_ref[...]).astype(BF16)

    return pl.pallas_call(
        body, name=name, grid=(T // tT, N // tN),
        in_specs=[pl.BlockSpec((tT, D), lambda i, j: (i, 0)),
                  pl.BlockSpec((1, D), lambda i, j: (0, 0)),
                  pl.BlockSpec((tN, D), lambda i, j: (j, 0))],
        out_specs=[pl.BlockSpec((tT, tN), lambda i, j: (i, j)),
                   pl.BlockSpec((tT, D), lambda i, j: (i, 0))],
        out_shape=[jax.ShapeDtypeStruct((T, N), BF16), jax.ShapeDtypeStruct((T, D), BF16)],
        compiler_params=_params(("parallel", "arbitrary")),
    )(h, wn, w)


def _proj_bwd(dproj, w, h, wn, dres, *, name, exchange=None, after=None):
    T, D = h.shape
    N = w.shape[0]
    tT, tN = min(TOKEN_TILE, T), _tile(N, PROJ_TILE_CAP)
    nn = N // tN

    def body(dp_ref, w_ref, h_ref, wn_ref, dres_ref, dh_ref, gw_ref, acc):
        i, j = pl.program_id(0), pl.program_id(1)

        @pl.when(jnp.logical_and(i == 0, j == 0))
        def _():
            gw_ref[...] = jnp.zeros_like(gw_ref)

        @pl.when(j == 0)
        def _():
            acc[...] = jnp.zeros_like(acc)

        acc[...] += _dot(dp_ref[...], w_ref[...])

        @pl.when(j == nn - 1)
        def _():
            dx, dw = _rms_bwd(h_ref[...], wn_ref[...], acc[...])
            dh_ref[...] = dres_ref[...] + dx
            gw_ref[...] += dw

    row = pl.BlockSpec((tT, D), lambda i, j: (i, 0))
    vec = pl.BlockSpec((1, D), lambda i, j: (0, 0))
    return _call(
        body, name=name, grid=(T // tT, nn),
        in_specs=[pl.BlockSpec((tT, tN), lambda i, j: (i, j)),
                  pl.BlockSpec((tN, D), lambda i, j: (j, 0)), row, vec, row],
        out_specs=[row, vec],
        out_shape=[jax.ShapeDtypeStruct((T, D), F32), jax.ShapeDtypeStruct((1, D), F32)],
        scratch_shapes=[pltpu.VMEM((tT, D), F32)], args=(dproj, w, h, wn, dres), exchange=exchange, after=after)


def _mla_prep_fwd(proj, pos, qn_w, kvn_w, w_uq, w_kv, tab, *, name):
    T = proj.shape[0]
    tT = min(TOKEN_TILE, T)
    a_blk = PROJ_FIXED // AW - 1

    def body(a_ref, pos_ref, qnw_ref, kvnw_ref, wuq_ref, wkv_ref, tab_ref,
             q_ref, k_ref, v_ref):
        cq = a_ref[:, 0:MLA_Q_RANK].astype(F32)
        ckv = a_ref[:, MLA_Q_RANK:MLA_Q_RANK + MLA_KV_RANK].astype(F32)
        kr = a_ref[:, 640:768].astype(F32)
        qn = _rms_fwd(cq, qnw_ref[...]).astype(BF16)
        kvn = _rms_fwd(ckv, kvnw_ref[...]).astype(BF16)
        cs = _rope_cs(pos_ref[...], tab_ref)
        q = _dot_nt(qn, wuq_ref[...])
        kv = _dot(kvn, wkv_ref[...])
        krr = _rope(kr, cs, MLA_ROPE // 2)
        for hd in range(MLA_HEADS):
            sl = slice(hd * HP, (hd + 1) * HP)
            q_ref[:, sl] = (_rope(q[:, sl], cs, MLA_ROPE // 2) * ATTN_SCALE).astype(BF16)
            k_ref[:, sl] = (kv[:, sl] + krr).astype(BF16)
        v_ref[...] = kv[:, QW:].astype(BF16)

    def full(r, c):
        return pl.BlockSpec((r, c), lambda i: (0, 0))

    def rows(c):
        return pl.BlockSpec((tT, c), lambda i: (i, 0))

    return pl.pallas_call(
        body, name=name, grid=(T // tT,),
        in_specs=[pl.BlockSpec((tT, AW), lambda i: (i, a_blk)), rows(1),
                  full(1, MLA_Q_RANK), full(1, MLA_KV_RANK),
                  full(QW, MLA_Q_RANK), full(MLA_KV_RANK, 2 * QW), full(8, LANES)],
        out_specs=[rows(QW), rows(QW), rows(QW)],
        out_shape=[jax.ShapeDtypeStruct((T, QW), BF16)] * 3,
        compiler_params=_params(("parallel",)),
    )(proj, pos, qn_w, kvn_w, w_uq, w_kv, tab)


def _mla_prep_bwd(dq, dk, dv, proj, pos, qn_w, kvn_w, w_uq, w_kv, tab, *, name):
    T = proj.shape[0]
    tT = min(TOKEN_TILE, T)
    nT = T // tT
    a_blk = PROJ_FIXED // AW - 1

    def body(dq_ref, dk_ref, dv_ref, a_ref, pos_ref, qnw_ref, kvnw_ref, wuq_ref, wkv_ref, tab_ref,
             da_ref, gqn_ref, gkvn_ref, dwuq_ref, dwkv_ref, dql_ref, dkvl_ref, acc_uq, acc_kv):
        @pl.when(pl.program_id(0) == 0)
        def _():
            gqn_ref[...] = jnp.zeros_like(gqn_ref)
            gkvn_ref[...] = jnp.zeros_like(gkvn_ref)
            acc_uq[...] = jnp.zeros_like(acc_uq)
            acc_kv[...] = jnp.zeros_like(acc_kv)

        cs = _rope_cs(pos_ref[...], tab_ref)
        dkr = jnp.zeros((tT, HP), F32)
        for hd in range(MLA_HEADS):
            sl = slice(hd * HP, (hd + 1) * HP)
            dql_ref[:, sl] = (_rope(dq_ref[:, sl], cs, MLA_ROPE // 2, inverse=True) * ATTN_SCALE).astype(BF16)
            dkh = dk_ref[:, sl]
            dkr = dkr + dkh
            dkvl_ref[:, sl] = dkh.astype(BF16)
        dkvl_ref[:, QW:] = dv_ref[...]
        dqn = _dot(dql_ref[...], wuq_ref[...])
        dkvn = _dot_nt(dkvl_ref[...], wkv_ref[...])
        cq = a_ref[:, 0:MLA_Q_RANK].astype(F32)
        ckv = a_ref[:, MLA_Q_RANK:MLA_Q_RANK + MLA_KV_RANK].astype(F32)
        dcq, gq = _rms_bwd(cq, qnw_ref[...], dqn)
        dckv, gkv = _rms_bwd(ckv, kvnw_ref[...], dkvn)
        gqn_ref[...] += gq
        gkvn_ref[...] += gkv
        da_ref[:, 0:MLA_Q_RANK] = dcq.astype(BF16)
        da_ref[:, MLA_Q_RANK:MLA_Q_RANK + MLA_KV_RANK] = dckv.astype(BF16)
        da_ref[:, 640:768] = _rope(dkr, cs, MLA_ROPE // 2, inverse=True).astype(BF16)
        da_ref[:, 768:AW] = jnp.zeros((tT, AW - 768), BF16)
        acc_uq[...] += _dot_tn(dql_ref[...], _rms_fwd(cq, qnw_ref[...]).astype(BF16))
        acc_kv[...] += _dot_tn(_rms_fwd(ckv, kvnw_ref[...]).astype(BF16), dkvl_ref[...])

        @pl.when(pl.program_id(0) == nT - 1)
        def _():
            dwuq_ref[...] = acc_uq[...].astype(BF16)
            dwkv_ref[...] = acc_kv[...].astype(BF16)

    def full(r, c):
        return pl.BlockSpec((r, c), lambda i: (0, 0))

    def rows(c):
        return pl.BlockSpec((tT, c), lambda i: (i, 0))

    return pl.pallas_call(
        body, name=name, grid=(nT,),
        in_specs=[rows(QW), rows(QW), rows(QW), pl.BlockSpec((tT, AW), lambda i: (i, a_blk)), rows(1),
                  full(1, MLA_Q_RANK), full(1, MLA_KV_RANK),
                  full(QW, MLA_Q_RANK), full(MLA_KV_RANK, 2 * QW), full(8, LANES)],
        out_specs=[rows(AW), full(1, MLA_Q_RANK), full(1, MLA_KV_RANK),
                   full(QW, MLA_Q_RANK), full(MLA_KV_RANK, 2 * QW)],
        out_shape=[jax.ShapeDtypeStruct((T, AW), BF16),
                   jax.ShapeDtypeStruct((1, MLA_Q_RANK), F32), jax.ShapeDtypeStruct((1, MLA_KV_RANK), F32),
                   jax.ShapeDtypeStruct((QW, MLA_Q_RANK), BF16), jax.ShapeDtypeStruct((MLA_KV_RANK, 2 * QW), BF16)],
        scratch_shapes=[pltpu.VMEM((tT, QW), BF16), pltpu.VMEM((tT, 2 * QW), BF16),
                        pltpu.VMEM((QW, MLA_Q_RANK), F32), pltpu.VMEM((MLA_KV_RANK, 2 * QW), F32)],
        compiler_params=_params(("arbitrary",)),
    )(dq, dk, dv, proj, pos, qn_w, kvn_w, w_uq, w_kv, tab)


def _flash_fwd(q, k, v, *, name, exchange=None):
    T = q.shape[0]
    H = q.shape[1] // HP
    tq = min(ATTN_TILE, T)
    nq = T // tq

    sub = tq // ATTN_CHAINS

    def body(q_ref, k_ref, v_ref, o_ref, lse_ref):
        qi = pl.program_id(1)
        qs = [q_ref[c * sub:(c + 1) * sub, :] for c in range(ATTN_CHAINS)]

        def update(carry, off, masked):
            nks = [(c + 1) * sub if masked else tq for c in range(ATTN_CHAINS)]
            scores = [_dot_nt(qs[c], k_ref[pl.ds(off, nks[c]), :]) for c in range(ATTN_CHAINS)]
            out = []
            for c in range(ATTN_CHAINS):
                m_prev, l_prev, acc = carry[c]
                nk, s = nks[c], scores[c]
                vb = v_ref[pl.ds(off, nk), :]
                if masked:
                    rows = lax.broadcasted_iota(jnp.int32, (sub, nk), 0) + c * sub
                    s = jnp.where(rows >= lax.broadcasted_iota(jnp.int32, (sub, nk), 1), s, NEG)
                m_new = jnp.maximum(m_prev, jnp.max(s, axis=1, keepdims=True))
                alpha = jnp.exp(m_prev - m_new)
                p = jnp.exp(s - m_new)
                out.append((m_new, alpha * l_prev + jnp.sum(p, axis=1, keepdims=True),
                            alpha * acc + _dot(p.astype(BF16), vb)))
            return tuple(out)

        init = tuple((jnp.full((sub, 1), NEG, F32), jnp.zeros((sub, 1), F32), jnp.zeros((sub, HP), F32))
                     for _ in range(ATTN_CHAINS))
        carry = lax.fori_loop(0, qi, lambda j, cr: update(cr, pl.multiple_of(j * tq, tq), False), init)
        carry = update(carry, pl.multiple_of(qi * tq, tq), True)
        for c in range(ATTN_CHAINS):
            m_fin, l_fin, acc = carry[c]
            o_ref[c * sub:(c + 1) * sub, :] = (acc / l_fin).astype(BF16)
            lse_ref[c * sub:(c + 1) * sub, :] = jnp.broadcast_to(m_fin + jnp.log(l_fin), (sub, HP))

    qspec = pl.BlockSpec((tq, HP), lambda h, i: (i, h))
    kspec = pl.BlockSpec((T, HP), lambda h, i: (0, h))
    return _call(
        body, name=name, grid=(H, nq),
        in_specs=[qspec, kspec, kspec], out_specs=[qspec, qspec],
        out_shape=[jax.ShapeDtypeStruct((T, H * HP), BF16), jax.ShapeDtypeStruct((T, H * HP), F32)],
        scratch_shapes=[], args=(q, k, v), exchange=exchange)


def _flash_bwd(q, k, v, do, lse, delta, *, name, exchange=None, after=None):
    T = q.shape[0]
    H = q.shape[1] // HP
    tq = min(ATTN_TILE, T)
    nq = T // tq
    sub = tq // ATTN_CHAINS

    def body(k_ref, v_ref, q_ref, do_ref, lse_ref, dl_ref, dq_ref, dk_ref, dv_ref):
        ki = pl.program_id(1)

        @pl.when(ki == 0)
        def _():
            dq_ref[...] = jnp.zeros_like(dq_ref)

        def grow(a):
            return a if a.shape[0] == tq else jnp.concatenate([a, jnp.zeros((tq - a.shape[0], HP), F32)], axis=0)

        def step(carry, j, masked):
            dk_acc, dv_acc = carry
            nks = [(c + 1) * sub if masked else tq for c in range(ATTN_CHAINS)]
            rws = [pl.ds(pl.multiple_of(j * tq + c * sub, sub), sub) for c in range(ATTN_CHAINS)]
            scores = [_dot_nt(q_ref[rws[c], :], k_ref[0:nks[c], :]) for c in range(ATTN_CHAINS)]
            dps = [_dot_nt(do_ref[rws[c], :], v_ref[0:nks[c], :]) for c in range(ATTN_CHAINS)]
            for c in range(ATTN_CHAINS):
                rows, nk, s, dp = rws[c], nks[c], scores[c], dps[c]
                kb = k_ref[0:nk, :]
                qb = q_ref[rows, :]
                dob = do_ref[rows, :]
                if masked:
                    ri = lax.broadcasted_iota(jnp.int32, (sub, nk), 0) + c * sub
                    s = jnp.where(ri >= lax.broadcasted_iota(jnp.int32, (sub, nk), 1), s, NEG)
                p = jnp.exp(s - lse_ref[rows, 0:1])
                dv_acc = dv_acc + grow(_dot_tn(p.astype(BF16), dob))
                ds = (p * (dp - dl_ref[rows, 0:1])).astype(BF16)
                dk_acc = dk_acc + grow(_dot_tn(ds, qb))
                dq_ref[rows, :] += _dot(ds, kb)
            return dk_acc, dv_acc

        carry = step((jnp.zeros((tq, HP), F32), jnp.zeros((tq, HP), F32)), ki, True)
        dk_acc, dv_acc = lax.fori_loop(ki + 1, nq, lambda j, cr: step(cr, j, False), carry)
        dk_ref[...] = dk_acc
        dv_ref[...] = dv_acc.astype(BF16)

    kspec = pl.BlockSpec((tq, HP), lambda h, j: (j, h))
    full = pl.BlockSpec((T, HP), lambda h, j: (0, h))
    return _call(
        body, name=name, grid=(H, nq),
        in_specs=[kspec, kspec, full, full, full, full], out_specs=[full, kspec, kspec],
        out_shape=[jax.ShapeDtypeStruct((T, H * HP), F32), jax.ShapeDtypeStruct((T, H * HP), F32),
                   jax.ShapeDtypeStruct((T, H * HP), BF16)],
        scratch_shapes=[], args=(k, v, q, do, lse, delta), exchange=exchange, after=after)


def _ret_consts(cc, hd):
    lg = math.log(1.0 - 2.0 ** (-5.0 - hd))
    diff = (lax.broadcasted_iota(jnp.int32, (cc, cc), 0) - lax.broadcasted_iota(jnp.int32, (cc, cc), 1)).astype(F32)
    decay = jnp.where(diff >= 0, jnp.exp(jnp.maximum(diff, 0.0) * lg), 0.0)
    idx = lax.broadcasted_iota(jnp.int32, (cc, 1), 0).astype(F32)
    zeta = jnp.exp((cc - 1.0 - idx) * lg)
    xi = jnp.exp((idx + 1.0) * lg)
    return decay, zeta, xi, math.exp(cc * lg)


def _ret_fwd(proj, pos, tab, *, name):
    T = proj.shape[0]
    cc = min(RET_TILE, T)
    n = T // cc

    def body(rq_ref, rk_ref, rv_ref, pos_ref, tab_ref, y_ref, yn_ref, rprev_ref, r_s):
        @pl.when(pl.program_id(0) == 0)
        def _():
            r_s[...] = jnp.zeros_like(r_s)

        cs = _rope_cs(pos_ref[...], tab_ref)
        for hd in range(RET_HEADS):
            sl = slice(hd * HP, (hd + 1) * HP)
            decay, zeta, xi, gc = _ret_consts(cc, hd)
            q = _rope(rq_ref[:, sl].astype(F32), cs, RET_DK // 2).astype(BF16)
            kf = _rope(rk_ref[:, sl].astype(F32), cs, RET_DK // 2) * (RET_DK ** -0.5)
            k = kf.astype(BF16)
            v = rv_ref[:, sl]
            r = r_s[hd]
            rprev_ref[0, hd] = r
            inner = (_dot_nt(q, k) * decay).astype(BF16)
            y = _dot(inner, v) + _dot(q, r.astype(BF16)) * xi
            r_s[hd] = r * gc + _dot_tn((kf * zeta).astype(BF16), v)
            y_ref[:, sl] = y
            mu = jnp.mean(y, axis=-1, keepdims=True)
            yc = y - mu
            var = jnp.mean(yc * yc, axis=-1, keepdims=True)
            yn_ref[:, sl] = (yc * lax.rsqrt(var + GN_EPS)).astype(BF16)

    def blk(j):
        return pl.BlockSpec((cc, RW), lambda i: (i, j))

    return pl.pallas_call(
        body, name=name, grid=(n,),
        in_specs=[blk(0), blk(1), blk(2), pl.BlockSpec((cc, 1), lambda i: (i, 0)),
                  pl.BlockSpec((8, LANES), lambda i: (0, 0))],
        out_specs=[blk(0), blk(0), pl.BlockSpec((1, RET_HEADS, HP, RET_DV), lambda i: (i, 0, 0, 0))],
        out_shape=[jax.ShapeDtypeStruct((T, RW), F32), jax.ShapeDtypeStruct((T, RW), BF16),
                   jax.ShapeDtypeStruct((n, RET_HEADS, HP, RET_DV), F32)],
        scratch_shapes=[pltpu.VMEM((RET_HEADS, HP, RET_DV), F32)],
        compiler_params=_params(("arbitrary",)),
    )(proj, proj, proj, pos, tab)


def _ret_bwd(dyn, y, proj, pos, tab, rprev, *, name):
    T = proj.shape[0]
    cc = min(RET_TILE, T)
    n = T // cc

    def body(dyn_ref, y_ref, rq_ref, rk_ref, rv_ref, pos_ref, tab_ref, rprev_ref,
             drq_ref, drk_ref, drv_ref, dr_s):
        @pl.when(pl.program_id(0) == 0)
        def _():
            dr_s[...] = jnp.zeros_like(dr_s)

        cs = _rope_cs(pos_ref[...], tab_ref)
        for hd in range(RET_HEADS):
            sl = slice(hd * HP, (hd + 1) * HP)
            decay, zeta, xi, gc = _ret_consts(cc, hd)
            q = _rope(rq_ref[:, sl].astype(F32), cs, RET_DK // 2).astype(BF16)
            kf = _rope(rk_ref[:, sl].astype(F32), cs, RET_DK // 2) * (RET_DK ** -0.5)
            k = kf.astype(BF16)
            v = rv_ref[:, sl]
            yv = y_ref[:, sl]
            mu = jnp.mean(yv, axis=-1, keepdims=True)
            yc = yv - mu
            rs = lax.rsqrt(jnp.mean(yc * yc, axis=-1, keepdims=True) + GN_EPS)
            yn = yc * rs
            dn = dyn_ref[:, sl]
            dy = rs * (dn - jnp.mean(dn, axis=-1, keepdims=True) - yn * jnp.mean(dn * yn, axis=-1, keepdims=True))
            dyb = dy.astype(BF16)
            dyx = (dy * xi).astype(BF16)
            dr = dr_s[hd]
            drb = dr.astype(BF16)
            inner = (_dot_nt(q, k) * decay).astype(BF16)
            da = (_dot_nt(dyb, v) * decay).astype(BF16)
            dv = _dot_tn(inner, dyb) + _dot((kf * zeta).astype(BF16), drb)
            dq = _dot(da, k) + _dot_nt(dyx, rprev_ref[0, hd].astype(BF16))
            dk = _dot_tn(da, q) + _dot_nt(v, drb) * zeta
            dr_s[hd] = dr * gc + _dot_tn(q, dyx)
            drq_ref[:, sl] = _rope(dq, cs, RET_DK // 2, inverse=True).astype(BF16)
            drk_ref[:, sl] = _rope(dk * (RET_DK ** -0.5), cs, RET_DK // 2, inverse=True).astype(BF16)
            drv_ref[:, sl] = dv.astype(BF16)

    def blk(j):
        return pl.BlockSpec((cc, RW), lambda i: (n - 1 - i, j))

    return pl.pallas_call(
        body, name=name, grid=(n,),
        in_specs=[blk(0), blk(0), blk(0), blk(1), blk(2), pl.BlockSpec((cc, 1), lambda i: (n - 1 - i, 0)),
                  pl.BlockSpec((8, LANES), lambda i: (0, 0)),
                  pl.BlockSpec((1, RET_HEADS, HP, RET_DV), lambda i: (n - 1 - i, 0, 0, 0))],
        out_specs=[blk(0), blk(0), blk(0)],
        out_shape=[jax.ShapeDtypeStruct((T, RW), BF16)] * 3,
        scratch_shapes=[pltpu.VMEM((RET_HEADS, HP, RET_DV), F32)],
        compiler_params=_params(("arbitrary",)),
    )(dyn, y, proj, proj, proj, pos, tab, rprev)


def _merge_fwd(o, yn, proj, gn_w, w_bm, w_br, w_out, h, post_w, *, name):
    T, D = h.shape
    tT = min(TOKEN_TILE, T)
    g_blk = PROJ_FIXED // D

    def body(o_ref, yn_ref, rg_ref, gm_ref, gr_ref, gnw_ref, wbm_ref, wbr_ref, wout_ref, h_ref, post_ref,
             omla_ref, oret_ref, m_ref, ho_ref):
        groups = [slice(c * (tT // FFN_CHAINS), (c + 1) * (tT // FFN_CHAINS)) for c in range(FFN_CHAINS)]
        o_mlas = [_dot(o_ref[rs, :], wbm_ref[...]) for rs in groups]
        for rs, o_mla in zip(groups, o_mlas):
            rg = rg_ref[rs, :].astype(F32)
            gated = (rg * _sigmoid(rg) * (yn_ref[rs, :].astype(F32) * gnw_ref[...])).astype(BF16)
            o_ret = _dot(gated, wbr_ref[...])
            omla_ref[rs, :] = o_mla.astype(BF16)
            oret_ref[rs, :] = o_ret.astype(BF16)
            merged = _sigmoid(gm_ref[rs, :].astype(F32)) * o_mla + _sigmoid(gr_ref[rs, :].astype(F32)) * o_ret
            m = _dot(merged.astype(BF16), wout_ref[...])
            m_ref[rs, :] = m
            ho_ref[rs, :] = h_ref[rs, :] + _rms_fwd(m, post_ref[...])

    def full(r, c):
        return pl.BlockSpec((r, c), lambda i: (0, 0))

    def rows(c, j=0):
        return pl.BlockSpec((tT, c), lambda i: (i, j))

    return pl.pallas_call(
        body, name=name, grid=(T // tT,),
        in_specs=[rows(QW), rows(RW), rows(RW, 3), rows(D, g_blk), rows(D, g_blk + 1), full(1, RW),
                  full(QW, D), full(RW, D), full(D, D), rows(D), full(1, D)],
        out_specs=[rows(D), rows(D), rows(D), rows(D)],
        out_shape=[jax.ShapeDtypeStruct((T, D), BF16), jax.ShapeDtypeStruct((T, D), BF16),
                   jax.ShapeDtypeStruct((T, D), F32), jax.ShapeDtypeStruct((T, D), F32)],
        compiler_params=_params(("parallel",)),
    )(o, yn, proj, proj, proj, gn_w, w_bm, w_br, w_out, h, post_w)


def _merge_bwd(dho, m, post_w, omla, oret, proj, yn, gn_w, o, w_out, w_bm, w_br, *, name):
    T, D = dho.shape
    tT = min(MERGE_TILE, T)
    g_blk = PROJ_FIXED // D

    nT = T // tT

    def body(dho_ref, m_ref, post_ref, omla_ref, oret_ref, rg_ref, gm_ref, gr_ref, yn_ref, gnw_ref, o_ref,
             wout_ref, wbm_ref, wbr_ref,
             dgm_ref, dgr_ref, do_ref, delta_ref, drg_ref, dyn_ref, gpost_ref, ggn_ref,
             dwout_ref, dwbm_ref, dwbr_ref, acc_out, acc_bm, acc_br):
        @pl.when(pl.program_id(0) == 0)
        def _():
            gpost_ref[...] = jnp.zeros_like(gpost_ref)
            ggn_ref[...] = jnp.zeros_like(ggn_ref)
            acc_out[...] = jnp.zeros_like(acc_out)
            acc_bm[...] = jnp.zeros_like(acc_bm)
            acc_br[...] = jnp.zeros_like(acc_br)

        dm, gp = _rms_bwd(m_ref[...], post_ref[...], dho_ref[...])
        gpost_ref[...] += gp
        dmb = dm.astype(BF16)
        dmerged = _dot_nt(dmb, wout_ref[...])
        o_mla = omla_ref[...].astype(F32)
        o_ret = oret_ref[...].astype(F32)
        sgm = _sigmoid(gm_ref[...].astype(F32))
        sgr = _sigmoid(gr_ref[...].astype(F32))
        acc_out[...] += _dot_tn((sgm * o_mla + sgr * o_ret).astype(BF16), dmb)
        dgm_ref[...] = (dmerged * o_mla * sgm * (1.0 - sgm)).astype(BF16)
        dgr_ref[...] = (dmerged * o_ret * sgr * (1.0 - sgr)).astype(BF16)
        domla = (dmerged * sgm).astype(BF16)
        acc_bm[...] += _dot_tn(o_ref[...], domla)
        do = _dot_nt(domla, wbm_ref[...])
        do_ref[...] = do.astype(BF16)
        for hd in range(MLA_HEADS):
            sl = slice(hd * HP, (hd + 1) * HP)
            d = jnp.sum(do[:, sl] * o_ref[:, sl].astype(F32), axis=-1, keepdims=True)
            delta_ref[:, sl] = jnp.broadcast_to(d, (tT, HP))
        doret = (dmerged * sgr).astype(BF16)
        dgated = _dot_nt(doret, wbr_ref[...])
        rg = rg_ref[...].astype(F32)
        sg = _sigmoid(rg)
        srg = rg * sg
        ynv = yn_ref[...].astype(F32)
        yw = ynv * gnw_ref[...]
        acc_br[...] += _dot_tn((srg * yw).astype(BF16), doret)
        drg_ref[...] = (dgated * yw * (sg * (1.0 + rg * (1.0 - sg)))).astype(BF16)
        dgs = dgated * srg
        dyn_ref[...] = dgs * gnw_ref[...]
        ggn_ref[...] += jnp.sum(dgs * ynv, axis=0, keepdims=True)

        @pl.when(pl.program_id(0) == nT - 1)
        def _():
            dwout_ref[...] = acc_out[...].astype(BF16)
            dwbm_ref[...] = acc_bm[...].astype(BF16)
            dwbr_ref[...] = acc_br[...].astype(BF16)

    def full(r, c):
        return pl.BlockSpec((r, c), lambda i: (0, 0), pipeline_mode=pl.Buffered(1))

    def rows(c, j=0):
        return pl.BlockSpec((tT, c), lambda i: (i, j))

    return pl.pallas_call(
        body, name=name, grid=(nT,),
        in_specs=[rows(D), rows(D), full(1, D), rows(D), rows(D), rows(RW, 3), rows(D, g_blk), rows(D, g_blk + 1),
                  rows(RW), full(1, RW), rows(QW), full(D, D), full(QW, D), full(RW, D)],
        out_specs=[rows(D), rows(D), rows(QW), rows(QW), rows(RW), rows(RW), full(1, D), full(1, RW),
                   full(D, D), full(QW, D), full(RW, D)],
        out_shape=[jax.ShapeDtypeStruct((T, D), BF16)] * 2
        + [jax.ShapeDtypeStruct((T, QW), BF16), jax.ShapeDtypeStruct((T, QW), F32),
           jax.ShapeDtypeStruct((T, RW), BF16), jax.ShapeDtypeStruct((T, RW), F32),
           jax.ShapeDtypeStruct((1, D), F32), jax.ShapeDtypeStruct((1, RW), F32),
           jax.ShapeDtypeStruct((D, D), BF16), jax.ShapeDtypeStruct((QW, D), BF16), jax.ShapeDtypeStruct((RW, D), BF16)],
        scratch_shapes=[pltpu.VMEM((D, D), F32), pltpu.VMEM((QW, D), F32), pltpu.VMEM((RW, D), F32)],
        compiler_params=_params(("arbitrary",)),
    )(dho, m, post_w, omla, oret, proj, proj, proj, yn, gn_w, o, w_out, w_bm, w_br)


def _mesh_pos():
    return lax.axis_index("x"), lax.axis_index("y"), lax.axis_index("c")


class _Gather:
    def __init__(self, shards):
        self.operands = list(shards)
        self.n = len(shards)
        self.out_shape = [jax.ShapeDtypeStruct((N_DEV,) + s.shape, s.dtype) for s in shards]
        self.scratch = [pltpu.SemaphoreType.DMA((7 * self.n,)), pltpu.SemaphoreType.DMA((7 * self.n,)),
                        pltpu.SemaphoreType.DMA((self.n,))]

    def phase(self, p, x_refs, out_refs, sems):
        send_sems, recv_sems, local_sems = sems
        x, y, c = _mesh_pos()
        me, sibling = (x, y, c), (x, y, 1 - c)
        chips = [(1 - x, y), (x, 1 - y), (1 - x, 1 - y)]

        def copy(w, k, block, to, src=None):
            slot = out_refs[w].at[4 * block[0] + 2 * block[1] + block[2]]
            return pltpu.make_async_remote_copy(
                src_ref=slot if src is None else src, dst_ref=slot,
                send_sem=send_sems.at[7 * w + k], recv_sem=recv_sems.at[7 * w + k],
                device_id=to, device_id_type=pl.DeviceIdType.MESH)

        for w in range(self.n):
            mine = pltpu.make_async_copy(x_refs[w], out_refs[w].at[4 * x + 2 * y + c], local_sems.at[w])
            first = [copy(w, 0, me, sibling, src=x_refs[w])]
            first += [copy(w, 1 + j, me, (*chip, c), src=x_refs[w]) for j, chip in enumerate(chips)]
            passed = [copy(w, 4 + j, (*chip, c), sibling) for j, chip in enumerate(chips)]
            if p == 0:
                mine.start()
                for cp in first:
                    cp.start()
            elif p == 1:
                for j, chip in enumerate(chips):
                    copy(w, 1 + j, (*chip, c), me).wait_recv()
                    passed[j].start()
            else:
                copy(w, 0, sibling, me).wait_recv()
                for j, chip in enumerate(chips):
                    copy(w, 4 + j, (*chip, 1 - c), me).wait_recv()
                for cp in first + passed:
                    cp.wait_send()
                mine.wait()


class _Scatter:
    def __init__(self, grads, whole=()):
        self.n_sliced = len(grads)
        self.operands = list(grads) + list(whole)
        self.n = len(self.operands)
        self.out_shape = [jax.ShapeDtypeStruct(g.shape, g.dtype) for g in grads]
        self.out_shape += [jax.ShapeDtypeStruct((N_DEV,) + a.shape, a.dtype) for a in whole]
        n_sem = (N_DEV - 1) * self.n
        self.scratch = [pltpu.SemaphoreType.DMA((n_sem,)), pltpu.SemaphoreType.DMA((n_sem,)),
                        pltpu.SemaphoreType.DMA((self.n,))]

    def phase(self, p, in_refs, out_refs, sems):
        if p == 1:
            return
        send_sems, recv_sems, local_sems = sems
        x, y, c = _mesh_pos()
        me = 4 * x + 2 * y + c

        def src(w, dev):
            return in_refs[w].at[dev] if w < self.n_sliced else in_refs[w]

        for w in range(self.n):
            own = None if local_sems is None else pltpu.make_async_copy(src(w, me), out_refs[w].at[me], local_sems.at[w])
            sends, recvs = [], []
            for r in range(1, N_DEV):
                px = 1 - x if r & 4 else x
                py = 1 - y if r & 2 else y
                pc = 1 - c if r & 1 else c
                peer, pidx = (px, py, pc), 4 * px + 2 * py + pc
                k = (N_DEV - 1) * w + r - 1
                sends.append(pltpu.make_async_remote_copy(
                    src_ref=src(w, pidx), dst_ref=out_refs[w].at[me], send_sem=send_sems.at[k],
                    recv_sem=recv_sems.at[k], device_id=peer, device_id_type=pl.DeviceIdType.MESH))
                recvs.append(pltpu.make_async_remote_copy(
                    src_ref=src(w, me), dst_ref=out_refs[w].at[pidx], send_sem=send_sems.at[k],
                    recv_sem=recv_sems.at[k], device_id=peer, device_id_type=pl.DeviceIdType.MESH))
            if p == 0:
                if own is not None:
                    own.start()
                for cp in sends:
                    cp.start()
            else:
                for cp in recvs:
                    cp.wait_recv()
                for cp in sends:
                    cp.wait_send()
                if own is not None:
                    own.wait()


class _SplitScatter:
    def __init__(self, ex, name):
        self.ex, self.name = ex, name

    def _specs(self):
        ex = self.ex
        hbm = pl.BlockSpec(memory_space=pltpu.HBM)
        sem = pl.BlockSpec(memory_space=pltpu.SEMAPHORE)
        effect = pltpu.CompilerParams(has_side_effects=pltpu.SideEffectType.DATAFLOW_SIDE_EFFECTING)
        buffers = [pltpu.HBM(a.shape, a.dtype) for a in ex.operands] + [pltpu.HBM(s.shape, s.dtype) for s in ex.out_shape]
        return hbm, sem, effect, buffers

    def start(self):
        ex, n = self.ex, self.ex.n
        n_sem = (N_DEV - 1) * n
        hbm, sem, effect, buffers = self._specs()
        in_hbm = lambda a: pltpu.with_memory_space_constraint(a, pltpu.HBM)

        me = 4 * lax.axis_index("x") + 2 * lax.axis_index("y") + lax.axis_index("c")
        lands = []
        for w, (a, s) in enumerate(zip(ex.operands, ex.out_shape)):
            mine = lax.dynamic_index_in_dim(a, me, 0, keepdims=True) if w < ex.n_sliced else a[None]
            lands.append(lax.dynamic_update_slice_in_dim(lax.empty(s.shape, s.dtype), mine, me, 0))

        def start_body(*refs):
            ex.phase(0, refs[:n], refs[n:2 * n], (refs[2 * n], refs[2 * n + 1], None))
            refs[-1][...] = jnp.zeros_like(refs[-1])

        self.started = pl.pallas_call(
            start_body, name=self.name + "_start",
            out_shape=[pltpu.SemaphoreType.DMA((n_sem,)), pltpu.SemaphoreType.DMA((n_sem,))] + buffers
            + [jax.ShapeDtypeStruct((8, LANES), F32)],
            in_specs=[hbm] * (2 * n), out_specs=[sem, sem] + [hbm] * (2 * n) + [pl.BlockSpec(memory_space=pltpu.VMEM)],
            input_output_aliases={i: 2 + i for i in range(2 * n)}, compiler_params=effect,
        )(*[in_hbm(a) for a in ex.operands], *[in_hbm(a) for a in lands])
        return self.started[-1]

    def wait(self, after):
        ex, n = self.ex, self.ex.n
        hbm, sem, effect, buffers = self._specs()
        anyspec = pl.BlockSpec(memory_space=pl.ANY)

        def wait_body(*refs):
            ex.phase(2, refs[:n], refs[n:2 * n], (refs[2 * n], refs[2 * n + 1], None))

        done = pl.pallas_call(
            wait_body, name=self.name + "_wait", out_shape=buffers,
            in_specs=[hbm] * (2 * n) + [sem, sem] + [anyspec] * len(after), out_specs=[hbm] * (2 * n),
            input_output_aliases={i: i for i in range(2 * n)}, compiler_params=effect,
        )(*self.started[2:2 + 2 * n], self.started[0], self.started[1], *after)
        return done[n:]


def _exchange_alone(ex, *, name):
    n = ex.n

    def body(*refs):
        for p in range(3):
            ex.phase(p, refs[:n], refs[n:2 * n], refs[2 * n:])

    anyspec = pl.BlockSpec(memory_space=pl.ANY)
    return pl.pallas_call(body, name=name, out_shape=ex.out_shape, in_specs=[anyspec] * n,
                          out_specs=[anyspec] * n, scratch_shapes=ex.scratch)(*ex.operands)


def _adam_step(w_ref, p_ref, m_ref, v_ref, g_ref, d_ref, nm_ref, nv_ref):
    g = p_ref[0].astype(F32)
    for j in range(1, N_DEV):
        g = g + p_ref[j].astype(F32)
    g_ref[...] = g
    nm = ADAM_B1 * m_ref[...] + (1.0 - ADAM_B1) * g
    nv = ADAM_B2 * v_ref[...] + (1.0 - ADAM_B2) * (g * g)
    nm_ref[...] = nm
    nv_ref[...] = nv
    m_hat = nm / (1.0 - ADAM_B1 ** ADAM_STEP)
    v_hat = nv / (1.0 - ADAM_B2 ** ADAM_STEP)
    d_ref[...] = -ADAM_LR * (m_hat / (jnp.sqrt(v_hat) + ADAM_EPS) + ADAM_WD * w_ref[...])


def _adamw_vectors(ws, parts, ms, vs, *, name):
    n = len(ws)

    def body(*refs):
        w_refs, p_refs, m_refs, v_refs = (refs[i * n:(i + 1) * n] for i in range(4))
        outs = refs[4 * n:]
        for i in range(n):
            _adam_step(w_refs[i], p_refs[i], m_refs[i], v_refs[i], *outs[4 * i:4 * i + 4])

    return pl.pallas_call(
        body, name=name,
        out_shape=[jax.ShapeDtypeStruct(w.shape, F32) for w in ws for _ in range(4)],
    )(*ws, *parts, *ms, *vs)


def _adamw(w, parts, m, v, after, *, name):
    G, R, n = w.shape
    tn = 512 if (n > 512 and n % 512 == 0) else n
    tr = R
    for t in range(16, R, 16):
        if R % t == 0 and t * tn <= ADAM_BLOCK_CAP:
            tr = t
    if R * tn <= ADAM_BLOCK_CAP:
        tr = R

    def body(w_ref, p_ref, m_ref, v_ref, after_ref, g_ref, d_ref, nm_ref, nv_ref):
        _adam_step(w_ref, p_ref, m_ref, v_ref, g_ref, d_ref, nm_ref, nv_ref)

    blk = pl.BlockSpec((None, tr, tn), lambda g, i, j: (g, i, j))
    return pl.pallas_call(
        body, name=name, grid=(G, R // tr, n // tn),
        in_specs=[blk, pl.BlockSpec((N_DEV, None, tr, tn), lambda g, i, j: (0, g, i, j)), blk, blk,
                  pl.BlockSpec((8, LANES), lambda g, i, j: (0, 0))],
        out_specs=[blk, blk, blk, blk],
        out_shape=[jax.ShapeDtypeStruct((G, R, n), F32)] * 4,
        compiler_params=_params(("parallel", "parallel", "parallel")),
    )(w, parts, m, v, after)


def _pad_last(a, width):
    return jnp.pad(a, [(0, 0)] * (a.ndim - 1) + [(0, width - a.shape[-1])])


def _cols_of(g):
    return g.transpose(1, 0, 2).reshape(g.shape[1], N_DEV * g.shape[2])


def _col_shards(w):
    return w.reshape(w.shape[0], N_DEV, w.shape[1] // N_DEV).transpose(1, 0, 2)


def kernel(x, positions, ffn1_pre_w, ffn1_w1, ffn1_w2, ffn1_post_w, mix_pre_w, w_in, mla_q_norm_w, mla_w_uq, mla_kv_norm_w, mla_w_ukv, ret_gn_w, w_branch_mla, w_branch_ret, w_out, mix_post_w, ffn2_pre_w, ffn2_w1, ffn2_w2, ffn2_post_w, loss_target, m_ffn1_pre_w, m_ffn1_w1, m_ffn1_w2, m_ffn1_post_w, m_mix_pre_w, m_w_in, m_mla_q_norm_w, m_mla_w_uq, m_mla_kv_norm_w, m_mla_w_ukv, m_ret_gn_w, m_w_branch_mla, m_w_branch_ret, m_w_out, m_mix_post_w, m_ffn2_pre_w, m_ffn2_w1, m_ffn2_w2, m_ffn2_post_w, v_ffn1_pre_w, v_ffn1_w1, v_ffn1_w2, v_ffn1_post_w, v_mix_pre_w, v_w_in, v_mla_q_norm_w, v_mla_w_uq, v_mla_kv_norm_w, v_mla_w_ukv, v_ret_gn_w, v_w_branch_mla, v_w_branch_ret, v_w_out, v_mix_post_w, v_ffn2_pre_w, v_ffn2_w1, v_ffn2_w2, v_ffn2_post_w):
    T, D = x.shape[1], x.shape[2]
    h0 = x[0]
    tgt = loss_target[0]
    pos = positions.reshape(T, 1).astype(F32)

    big = [("ffn1_w1", ffn1_w1, m_ffn1_w1, v_ffn1_w1), ("ffn1_w2", ffn1_w2, m_ffn1_w2, v_ffn1_w2),
           ("w_in", w_in, m_w_in, v_w_in), ("mla_w_uq", mla_w_uq, m_mla_w_uq, v_mla_w_uq),
           ("mla_w_ukv", mla_w_ukv, m_mla_w_ukv, v_mla_w_ukv),
           ("w_branch_mla", w_branch_mla, m_w_branch_mla, v_w_branch_mla),
           ("w_branch_ret", w_branch_ret, m_w_branch_ret, v_w_branch_ret),
           ("w_out", w_out, m_w_out, v_w_out),
           ("ffn2_w1", ffn2_w1, m_ffn2_w1, v_ffn2_w1), ("ffn2_w2", ffn2_w2, m_ffn2_w2, v_ffn2_w2)]
    small = [("ffn1_pre_w", ffn1_pre_w, m_ffn1_pre_w, v_ffn1_pre_w), ("ffn1_post_w", ffn1_post_w, m_ffn1_post_w, v_ffn1_post_w),
             ("mix_pre_w", mix_pre_w, m_mix_pre_w, v_mix_pre_w), ("mla_q_norm_w", mla_q_norm_w, m_mla_q_norm_w, v_mla_q_norm_w),
             ("mla_kv_norm_w", mla_kv_norm_w, m_mla_kv_norm_w, v_mla_kv_norm_w), ("ret_gn_w", ret_gn_w, m_ret_gn_w, v_ret_gn_w),
             ("mix_post_w", mix_post_w, m_mix_post_w, v_mix_post_w), ("ffn2_pre_w", ffn2_pre_w, m_ffn2_pre_w, v_ffn2_pre_w),
             ("ffn2_post_w", ffn2_post_w, m_ffn2_post_w, v_ffn2_post_w)]

    half = ffn1_w2.shape[1]
    hp = -(-half // LANES) * LANES

    def rows_view(w):
        return w[0].T

    def send_w1(w):
        return jnp.pad(rows_view(w).reshape(2, half, D), ((0, 0), (0, hp - half), (0, 0))).reshape(2 * hp, D).astype(BF16)

    def send_w2(w):
        return jnp.pad(w[0], ((0, hp - half), (0, 0))).astype(BF16)

    mixer = ["w_in", "mla_w_uq", "mla_w_ukv", "w_branch_mla", "w_branch_ret", "w_out"]
    uq_w = MLA_NOPE + MLA_ROPE
    mixer_send = [rows_view(w_in).astype(BF16), jnp.pad(rows_view(mla_w_uq), ((0, HP - uq_w), (0, 0))).astype(BF16),
                  mla_w_ukv[0].astype(BF16), w_branch_mla[0].astype(BF16), w_branch_ret[0].astype(BF16),
                  w_out[0].astype(BF16)]

    w1a, w2a = _exchange_alone(_Gather([send_w1(ffn1_w1), send_w2(ffn1_w2)]), name="gather_ffn1")
    w2a = w2a.reshape(N_DEV // 2, 2 * hp, D)
    u1, f1, h1, a0, *got = _ffn_fwd(h0, ffn1_pre_w, w1a, w2a, ffn1_post_w, None, name="ffn1_fwd_gather_mixer",
                                exchange=_Gather(mixer_send))
    fw = dict(zip(mixer, got))

    wi = fw["w_in"].reshape(-1, D)
    cq_w, ckv_w, kr_w = wi[0:384], wi[384:640], wi[640:672]
    rq_w, rk_w = wi[672:928], wi[928:1184]
    rv_w, rg_w = wi[1184:1696], wi[1696:2208]
    gm_w, gr_w = wi[2208:2208 + D], wi[2208 + D:2208 + 2 * D]
    zer = lambda n: jnp.zeros((n, D), BF16)
    head_rows = lambda a, h: jnp.pad(a.reshape(h, -1, D), ((0, 0), (0, HP - a.shape[0] // h), (0, 0))).reshape(h * HP, D)
    w_in_p = jnp.concatenate([head_rows(rq_w, RET_HEADS), head_rows(rk_w, RET_HEADS), rv_w, rg_w,
                              cq_w, ckv_w, zer(MLA_NOPE), kr_w, zer(HP - MLA_NOPE - MLA_ROPE), zer(AW - 768),
                              gm_w, gr_w], axis=0)
    w_uq_p = fw["mla_w_uq"].reshape(QW, MLA_Q_RANK)
    ukv = fw["mla_w_ukv"].transpose(1, 0, 2)
    w_kv_p = jnp.concatenate([_pad_last(ukv[:, :, :MLA_NOPE], HP).reshape(MLA_KV_RANK, QW),
                              _pad_last(ukv[:, :, MLA_NOPE:], HP).reshape(MLA_KV_RANK, QW)], axis=1)
    w_bm_p = jnp.pad(_cols_of(fw["w_branch_mla"]).reshape(MLA_HEADS, MLA_V, D),
                     ((0, 0), (0, HP - MLA_V), (0, 0))).reshape(QW, D)
    w_br, w_o = _cols_of(fw["w_branch_ret"]), fw["w_out"].reshape(D, D)
    tab_mla = _rope_table(MLA_NOPE, MLA_ROPE // 2)
    tab_ret = _rope_table(0, RET_DK // 2)

    proj, a1 = _rms_matmul(h1, mix_pre_w, w_in_p, name="mixer_in_proj")
    q, k, v = _mla_prep_fwd(proj, pos, mla_q_norm_w, mla_kv_norm_w, w_uq_p, w_kv_p, tab_mla, name="mla_prep_fwd")
    o, lse, w1b, w2b = _flash_fwd(q, k, v, name="mla_attn_fwd_gather_ffn2",
                                  exchange=_Gather([send_w1(ffn2_w1), send_w2(ffn2_w2)]))
    w2b = w2b.reshape(N_DEV // 2, 2 * hp, D)
    ypre, yn, rprev = _ret_fwd(proj, pos, tab_ret, name="retention_fwd")
    omla, oret, m, h2 = _merge_fwd(o, yn, proj, ret_gn_w, w_bm_p, w_br, w_o, h1, mix_post_w, name="merge_fwd")
    u2, f2, _, a2, dy, lossp = _ffn_fwd(h2, ffn2_pre_w, w1b, w2b, ffn2_post_w, tgt, name="ffn2_fwd_loss")

    def grad(x, dy, tag, after=None):
        return _matmul_tn(x if x.ndim == 3 else x[None], dy if dy.ndim == 3 else dy[None], name=tag, after=after)

    g2, du2, df2, dh2, gpost2, gpre2 = _ffn_bwd(dy, f2, ffn2_post_w, h2, ffn2_pre_w, u2, w2b, w1b, name="ffn2_bwd")
    dw1b, = grad(du2.reshape(N_DEV, T, 2 * hp), a2, "ffn2_dw1")
    dw2b = grad(g2, df2, "ffn2_dw2")[0].reshape(N_DEV, hp, D)
    (dgm, dgr, do, delta, drg, dyn, gpostm, ggn, dw_out, dw_bm_p, dw_br) = _merge_bwd(
        dh2, m, mix_post_w, omla, oret, proj, yn, ret_gn_w, o, w_o, w_bm_p, w_br, name="merge_bwd")
    sc_ffn2 = _SplitScatter(_Scatter([dw1b, dw2b]), "scatter_ffn2")
    dq, dk, dv = _flash_bwd(q, k, v, do, lse, delta, name="mla_attn_bwd", after=sc_ffn2.start())
    da, gqn, gkvn, dw_uq_p, dw_kv_p = _mla_prep_bwd(dq, dk, dv, proj, pos, mla_q_norm_w, mla_kv_norm_w, w_uq_p, w_kv_p, tab_mla, name="mla_prep_bwd")
    drq, drk, drv = _ret_bwd(dyn, ypre, proj, pos, tab_ret, rprev, name="retention_bwd")
    dproj = jnp.concatenate([drq, drk, drv, drg, da, dgm, dgr], axis=1)
    dw_in_p = grad(dproj, a1, "dw_in")[0][0]

    dw_uq = dw_uq_p.reshape(MLA_HEADS, HP, MLA_Q_RANK)[:, :uq_w]
    dkp = dw_kv_p[:, :QW].reshape(MLA_KV_RANK, MLA_HEADS, HP)[:, :, :MLA_NOPE]
    dvp = dw_kv_p[:, QW:].reshape(MLA_KV_RANK, MLA_HEADS, HP)[:, :, :MLA_V]
    dw_ukv = jnp.concatenate([dkp, dvp], axis=2).transpose(1, 0, 2)
    dw_bm = dw_bm_p.reshape(MLA_HEADS, HP, D)[:, :MLA_V].reshape(MLA_HEADS * MLA_V, D)
    small_mixer_grads = [dw_uq, dw_ukv, _col_shards(dw_bm), _col_shards(dw_br), dw_out.reshape(N_DEV, D // N_DEV, D)]
    sc_small = _SplitScatter(_Scatter(small_mixer_grads), "scatter_mixer_small")
    dh1, gmixpre = _proj_bwd(dproj, w_in_p, h1, mix_pre_w, dh2, name="mixer_in_bwd", after=sc_small.start())
    unhead = lambda a, h, wd: a.reshape(h, HP, D)[:, :wd].reshape(h * wd, D)
    c0 = 4 * RW
    dw_in = jnp.concatenate([
        dw_in_p[c0:c0 + 384], dw_in_p[c0 + 384:c0 + 640], dw_in_p[c0 + 640 + MLA_NOPE:c0 + 640 + MLA_NOPE + MLA_ROPE],
        unhead(dw_in_p[0:RW], RET_HEADS, RET_DK), unhead(dw_in_p[RW:2 * RW], RET_HEADS, RET_DK),
        dw_in_p[2 * RW:3 * RW], dw_in_p[3 * RW:4 * RW],
        dw_in_p[PROJ_FIXED:PROJ_FIXED + D], dw_in_p[PROJ_FIXED + D:PROJ_FIXED + 2 * D]], axis=0).reshape(N_DEV, -1, D)
    sc_w_in = _SplitScatter(_Scatter([dw_in]), "scatter_w_in")
    g1, du1, df1, dx, gpost1, gpre1 = _ffn_bwd(
        dh1, f1, ffn1_post_w, h0, ffn1_pre_w, u1, w2a, w1a, name="ffn1_bwd", after=sc_w_in.start())
    dw2a = grad(g1, df1, "ffn1_dw2")[0].reshape(N_DEV, hp, D)
    sc_dw2a = _SplitScatter(_Scatter([dw2a]), "scatter_ffn1_dw2")
    dw1a, = grad(du1.reshape(N_DEV, T, 2 * hp), a0, "ffn1_dw1", after=sc_dw2a.start())

    small_g = {"ffn1_pre_w": gpre1, "ffn1_post_w": gpost1, "mix_pre_w": gmixpre, "mla_q_norm_w": gqn,
               "mla_kv_norm_w": gkvn, "ret_gn_w": ggn, "mix_post_w": gpostm, "ffn2_pre_w": gpre2, "ffn2_post_w": gpost2}
    sc_last = _SplitScatter(_Scatter([dw1a], whole=[small_g[nm] for nm, *_ in small] + [lossp]), "scatter_ffn1_dw1")
    token = sc_last.start()
    recv_ffn2 = sc_ffn2.wait([token])
    recv_mixer = sc_w_in.wait([token]) + sc_small.wait([token])
    recv_w2a, = sc_dw2a.wait([token])
    parts = dict(zip(mixer, recv_mixer))
    parts.update(ffn1_w2=recv_w2a, ffn2_w1=recv_ffn2[0], ffn2_w2=recv_ffn2[1])
    as_is = (lambda a: a, lambda p: p[:, None], lambda a: a)
    views = {nm: as_is for nm, *_ in big}
    for nm in ("ffn1_w1", "ffn2_w1"):
        views[nm] = (lambda a: rows_view(a).reshape(2, half, D), lambda p: p.reshape(N_DEV, 2, hp, D),
                     lambda a: a.reshape(2 * half, D).T[None])
    for nm in ("w_in", "mla_w_uq"):
        views[nm] = (lambda a: rows_view(a)[None], lambda p: p[:, None], lambda a: a[0].T[None])

    def update(nm, w, m_, v_, after):
        to_view, parts_view, back = views[nm]
        return [back(a) for a in _adamw(to_view(w), parts_view(parts[nm]), to_view(m_), to_view(v_), after,
                                        name="adamw_" + nm)]

    big_out = {nm: update(nm, w, m_, v_, token) for nm, w, m_, v_ in big if nm != "ffn1_w1"}
    recv_w1a, *small_parts, loss_parts = sc_last.wait([d[0] for d in big_out.values()])
    loss = jnp.sum(loss_parts[:, ::8, 0])
    parts["ffn1_w1"] = recv_w1a
    big_out["ffn1_w1"] = update("ffn1_w1", ffn1_w1, m_ffn1_w1, v_ffn1_w1, jnp.zeros((8, LANES), F32))
    small_out = _adamw_vectors([w for _, w, _, _ in small], small_parts, [a for _, _, a, _ in small],
                               [a for _, _, _, a in small], name="adamw_replicated")

    order = ["ffn1_pre_w", "ffn1_w1", "ffn1_w2", "ffn1_post_w", "mix_pre_w", "w_in", "mla_q_norm_w", "mla_w_uq",
             "mla_kv_norm_w", "mla_w_ukv", "ret_gn_w", "w_branch_mla", "w_branch_ret", "w_out", "mix_post_w",
             "ffn2_pre_w", "ffn2_w1", "ffn2_w2", "ffn2_post_w"]
    outs = [loss, dx[None]]
    for i in range(4):
        both = {nm: big_out[nm][i] for nm in big_out}
        both.update({nm: small_out[4 * j + i] for j, (nm, *_) in enumerate(small)})
        outs += [both[nm] for nm in order]
    return tuple(outs)
```

```python
import math

import numpy as np
import jax
import jax.numpy as jnp
from jax import lax
from jax.experimental import pallas as pl
from jax.experimental.pallas import tpu as pltpu

F32, BF16 = jnp.float32, jnp.bfloat16

MLA_HEADS, MLA_NOPE, MLA_ROPE, MLA_V = 8, 64, 32, 64
MLA_Q_RANK, MLA_KV_RANK = 384, 256
RET_HEADS, RET_DK, RET_DV = 4, 64, 128
ROPE_BASE, NORM_EPS, GN_EPS = 10000.0, 1e-6, 1e-6
ADAM_LR, ADAM_B1, ADAM_B2, ADAM_EPS, ADAM_WD, ADAM_STEP = 0.001, 0.9, 0.999, 1e-08, 0.01, 10
ATTN_SCALE = 1.0 / math.sqrt(MLA_NOPE + MLA_ROPE)

N_DEV = 8
LANES = 128
HP = LANES
QW = MLA_HEADS * HP
RW = RET_HEADS * HP
AW = 1024
PROJ_FIXED = 4 * RW + AW
NEG = -1e30

TOKEN_TILE = 512
ATTN_TILE = 1024
ATTN_CHAINS = 2
FFN_CHAINS = 2
RET_TILE = 256
PROJ_TILE_CAP = 2560
PROJ_TOKEN_TILE = 1024
GRAD_TILE_CAP = 1408
GRAD_TOKEN_TILE = 4096
ADAM_BLOCK_CAP = 192 * 1024
MERGE_TILE = 256
VMEM_LIMIT = 56 * 1024 * 1024


def _tile(n, cap, mult=LANES):
    if n <= cap:
        return n
    best = None
    for t in range(mult, cap + 1, mult):
        if n % t == 0:
            best = t
    assert best is not None, (n, cap, mult)
    return best


def _params(sem):
    return pltpu.CompilerParams(dimension_semantics=sem, vmem_limit_bytes=VMEM_LIMIT)


def _dot(a, b):
    return lax.dot_general(a, b, (((1,), (0,)), ((), ())), preferred_element_type=F32)


def _dot_nt(a, b):
    return lax.dot_general(a, b, (((1,), (1,)), ((), ())), preferred_element_type=F32)


def _dot_tn(a, b):
    return lax.dot_general(a, b, (((0,), (0,)), ((), ())), preferred_element_type=F32)


def _sigmoid(x):
    return pl.reciprocal(1.0 + jnp.exp(-x), approx=True)


def _rms_fwd(x, w):
    r = lax.rsqrt(jnp.mean(x * x, axis=-1, keepdims=True) + NORM_EPS)
    return x * r * w


def _rms_bwd(x, w, dy):
    r = lax.rsqrt(jnp.mean(x * x, axis=-1, keepdims=True) + NORM_EPS)
    xh = x * r
    g = dy * w
    dx = r * (g - xh * jnp.mean(g * xh, axis=-1, keepdims=True))
    return dx, jnp.sum(dy * xh, axis=0, keepdims=True)


def _rope_table(first, half):
    inv = (np.float32(ROPE_BASE) ** (-(np.arange(half, dtype=np.float32) / np.float32(half)))).astype(np.float32)
    tab = np.zeros((8, LANES), np.float32)
    tab[0, first:first + half] = inv
    tab[0, first + half:first + 2 * half] = inv
    tab[1, first:first + half] = -1.0
    tab[2, first + half:first + 2 * half] = 1.0
    return jnp.asarray(tab)


def _rope_cs(pos, tab_ref):
    ang = pos * tab_ref[0:1, :]
    s = jnp.sin(ang)
    return jnp.cos(ang), s * tab_ref[1:2, :], s * tab_ref[2:3, :]


def _rope(x, cs, half, inverse=False):
    c, s1, s2 = cs
    a = pltpu.roll(x, LANES - half, 1) * s1 + pltpu.roll(x, half, 1) * s2
    return x * c - a if inverse else x * c + a


def _call(body, *, name, grid, in_specs, out_specs, out_shape, scratch_shapes, args, exchange=None, after=None):
    sem = ("arbitrary",) * len(grid)
    anyspec = pl.BlockSpec(memory_space=pl.ANY)
    if exchange is None and after is not None:
        n_own = len(in_specs)

        def behind(*refs):
            body(*refs[:n_own], *refs[n_own + 1:])

        return pl.pallas_call(behind, name=name, grid=grid, in_specs=list(in_specs) + [anyspec], out_specs=out_specs,
                              out_shape=out_shape, scratch_shapes=scratch_shapes, compiler_params=_params(sem))(*args, after)
    if exchange is None:
        return pl.pallas_call(body, name=name, grid=grid, in_specs=in_specs, out_specs=out_specs,
                              out_shape=out_shape, scratch_shapes=scratch_shapes, compiler_params=_params(sem))(*args)
    n_in, n_out, e = len(in_specs), len(out_specs), exchange.n
    total = math.prod(grid)

    def carried(*refs):
        own = refs[:n_in] + refs[n_in + e:n_in + e + n_out] + refs[n_in + 2 * e + n_out:len(refs) - 3]
        ex_refs = (refs[n_in:n_in + e], refs[n_in + e + n_out:n_in + 2 * e + n_out], refs[len(refs) - 3:])
        step = pl.program_id(0)
        for d in range(1, len(grid)):
            step = step * grid[d] + pl.program_id(d)

        @pl.when(step == 0)
        def _():
            exchange.phase(0, *ex_refs)

        @pl.when(step == (3 * total) // 4)
        def _():
            exchange.phase(1, *ex_refs)

        body(*own)

        @pl.when(step == total - 1)
        def _():
            exchange.phase(2, *ex_refs)

    return pl.pallas_call(
        carried, name=name, grid=grid, in_specs=list(in_specs) + [anyspec] * e,
        out_specs=list(out_specs) + [anyspec] * e, out_shape=list(out_shape) + exchange.out_shape,
        scratch_shapes=list(scratch_shapes) + exchange.scratch, compiler_params=_params(sem),
    )(*args, *exchange.operands)


def _ffn_fwd(h, pre_w, w1, w2, post_w, target, *, name, exchange=None):
    T, D = h.shape
    nk, ck = w2.shape[0], w2.shape[1]
    tT = min(TOKEN_TILE, T)
    nT = T // tT
    with_loss = target is not None

    def body(*refs):
        if with_loss:
            (h_ref, pre_ref, w1g_ref, w1u_ref, w2_ref, post_ref, tgt_ref,
             u_ref, f_ref, ho_ref, a_s, dy_ref, loss_ref, acc) = refs
        else:
            (h_ref, pre_ref, w1g_ref, w1u_ref, w2_ref, post_ref,
             u_ref, f_ref, ho_ref, a_s, acc) = refs
        k = pl.program_id(1)

        @pl.when(k == 0)
        def _():
            a_s[...] = _rms_fwd(h_ref[...], pre_ref[...]).astype(BF16)
            acc[...] = jnp.zeros_like(acc)

        for c in range(FFN_CHAINS):
            rs = slice(c * (tT // FFN_CHAINS), (c + 1) * (tT // FFN_CHAINS))
            a = a_s[rs, :]
            ug = _dot_nt(a, w1g_ref[...])
            uu = _dot_nt(a, w1u_ref[...])
            u_ref[0, rs, :] = ug.astype(BF16)
            u_ref[1, rs, :] = uu.astype(BF16)
            acc[rs, :] += _dot((ug * _sigmoid(ug) * uu).astype(BF16), w2_ref[...])

        @pl.when(k == nk - 1)
        def _():
            f = acc[...]
            f_ref[...] = f
            ho = h_ref[...] + 0.5 * _rms_fwd(f, post_ref[...])
            ho_ref[...] = ho
            if with_loss:
                e = ho - tgt_ref[...]
                dy_ref[...] = e * (1.0 / D)
                loss_ref[...] = jnp.full(loss_ref.shape, (0.5 / D) * jnp.sum(e * e), F32)

    row = pl.BlockSpec((tT, D), lambda i, k: (i, 0))
    vec = pl.BlockSpec((1, D), lambda i, k: (0, 0))
    in_specs = [row, vec,
                pl.BlockSpec((None, ck, D), lambda i, k: (k, 0, 0)),
                pl.BlockSpec((None, ck, D), lambda i, k: (nk + k, 0, 0)),
                pl.BlockSpec((None, ck, D), lambda i, k: (k, 0, 0)),
                vec]
    out_shape = [jax.ShapeDtypeStruct((2, nk, T, ck), BF16),
                 jax.ShapeDtypeStruct((T, D), F32),
                 jax.ShapeDtypeStruct((T, D), F32),
                 jax.ShapeDtypeStruct((T, D), BF16)]
    out_specs = [pl.BlockSpec((2, None, tT, ck), lambda i, k: (0, k, i, 0)), row, row, row]
    args = [h, pre_w, w1, w1, w2, post_w]
    if with_loss:
        in_specs.append(row)
        args.append(target)
        out_shape += [jax.ShapeDtypeStruct((T, D), F32), jax.ShapeDtypeStruct((nT * 8, LANES), F32)]
        out_specs += [row, pl.BlockSpec((8, LANES), lambda i, k: (i, 0))]
    return _call(body, name=name, grid=(nT, nk), in_specs=in_specs, out_specs=out_specs, out_shape=out_shape,
                 scratch_shapes=[pltpu.VMEM((tT, D), F32)], args=args, exchange=exchange)


def _ffn_bwd(dho, f, post_w, h, pre_w, u, w2, w1, *, name, exchange=None, after=None):
    T, D = h.shape
    nk, ck = w2.shape[0], w2.shape[1]
    tT = min(TOKEN_TILE, T)
    nT = T // tT

    def body(dho_ref, f_ref, post_ref, h_ref, pre_ref, u_ref, w2_ref, w1g_ref, w1u_ref,
             g_ref, du_ref, df_s, dh_ref, gpost_ref, gpre_ref, da_acc):
        i, k = pl.program_id(0), pl.program_id(1)

        @pl.when(jnp.logical_and(i == 0, k == 0))
        def _():
            gpost_ref[...] = jnp.zeros_like(gpost_ref)
            gpre_ref[...] = jnp.zeros_like(gpre_ref)

        @pl.when(k == 0)
        def _():
            dx, dw = _rms_bwd(f_ref[...], post_ref[...], 0.5 * dho_ref[...])
            df_s[...] = dx.astype(BF16)
            gpost_ref[...] += dw
            da_acc[...] = jnp.zeros_like(da_acc)

        groups = [slice(c * (tT // FFN_CHAINS), (c + 1) * (tT // FFN_CHAINS)) for c in range(FFN_CHAINS)]
        dgs = [_dot_nt(df_s[rs, :], w2_ref[...]) for rs in groups]
        for rs, dg in zip(groups, dgs):
            ug = u_ref[0, rs, :].astype(F32)
            uu = u_ref[1, rs, :].astype(F32)
            sg = _sigmoid(ug)
            sl = ug * sg
            g_ref[rs, :] = (sl * uu).astype(BF16)
            dug = (dg * uu * (sg + sl * (1.0 - sg))).astype(BF16)
            duu = (dg * sl).astype(BF16)
            du_ref[0, rs, :] = dug
            du_ref[1, rs, :] = duu
            da_acc[rs, :] += _dot(dug, w1g_ref[...]) + _dot(duu, w1u_ref[...])

        @pl.when(k == nk - 1)
        def _():
            dx, dw = _rms_bwd(h_ref[...], pre_ref[...], da_acc[...])
            dh_ref[...] = dho_ref[...] + dx
            gpre_ref[...] += dw

    row = pl.BlockSpec((tT, D), lambda i, k: (i, 0))
    vec = pl.BlockSpec((1, D), lambda i, k: (0, 0))
    return _call(
        body, name=name, grid=(nT, nk),
        in_specs=[row, row, vec, row, vec,
                  pl.BlockSpec((2, None, tT, ck), lambda i, k: (0, k, i, 0)),
                  pl.BlockSpec((None, ck, D), lambda i, k: (k, 0, 0)),
                  pl.BlockSpec((None, ck, D), lambda i, k: (k, 0, 0)),
                  pl.BlockSpec((None, ck, D), lambda i, k: (nk + k, 0, 0))],
        out_specs=[pl.BlockSpec((None, tT, ck), lambda i, k: (k, i, 0)),
                   pl.BlockSpec((2, None, tT, ck), lambda i, k: (0, k, i, 0)),
                   row, row, vec, vec],
        out_shape=[jax.ShapeDtypeStruct((nk, T, ck), BF16),
                   jax.ShapeDtypeStruct((2, nk, T, ck), BF16),
                   jax.ShapeDtypeStruct((T, D), BF16),
                   jax.ShapeDtypeStruct((T, D), F32),
                   jax.ShapeDtypeStruct((1, D), F32),
                   jax.ShapeDtypeStruct((1, D), F32)],
        scratch_shapes=[pltpu.VMEM((tT, D), F32)],
        args=(dho, f, post_w, h, pre_w, u, w2, w1, w1), exchange=exchange, after=after)


def _matmul_tn(x, dy, *, name, exchange=None, after=None):
    Px, T, K = x.shape
    Py, _, N = dy.shape
    P = max(Px, Py)
    tT, tK, tN = min(GRAD_TOKEN_TILE, T), _tile(K, GRAD_TILE_CAP), _tile(N, GRAD_TILE_CAP)
    nt = T // tT

    def body(x_ref, dy_ref, o_ref, acc):
        t = pl.program_id(3)

        @pl.when(t == 0)
        def _():
            acc[...] = jnp.zeros_like(acc)

        acc[...] += _dot_tn(x_ref[...], dy_ref[...])

        @pl.when(t == nt - 1)
        def _():
            o_ref[...] = acc[...].astype(BF16)

    return _call(
        body, name=name, grid=(P, K // tK, N // tN, nt),
        in_specs=[pl.BlockSpec((None, tT, tK), lambda p, a, b, t: (p if Px > 1 else 0, t, a)),
                  pl.BlockSpec((None, tT, tN), lambda p, a, b, t: (p if Py > 1 else 0, t, b))],
        out_specs=[pl.BlockSpec((None, tK, tN), lambda p, a, b, t: (p, a, b))],
        out_shape=[jax.ShapeDtypeStruct((P, K, N), BF16)],
        scratch_shapes=[pltpu.VMEM((tK, tN), F32)], args=(x, dy), exchange=exchange, after=after)


def _rms_matmul(h, wn, w, *, name):
    T, D = h.shape
    N = w.shape[0]
    tT, tN = min(PROJ_TOKEN_TILE, T), _tile(N, PROJ_TILE_CAP)

    def body(h_ref, wn_ref, w_ref, y_ref, a_ref):
        @pl.when(pl.program_id(1) == 0)
        def _():
            a_ref[...] = _rms_fwd(h_ref[...], wn_ref[...]).astype(BF16)

        y_ref[...] = _dot_nt(a_ref[...], w_ref[...]).astype(BF16)

    return pl.pallas_call(
        body, name=name, grid=(T // tT, N // tN),
        in_specs=[pl.BlockSpec((tT, D), lambda i, j: (i, 0)),
                  pl.BlockSpec((1, D), lambda i, j: (0, 0)),
                  pl.BlockSpec((tN, D), lambda i, j: (j, 0))],
        out_specs=[pl.BlockSpec((tT, tN), lambda i, j: (i, j)),
                   pl.BlockSpec((tT, D), lambda i, j: (i, 0))],
        out_shape=[jax.ShapeDtypeStruct((T, N), BF16), jax.ShapeDtypeStruct((T, D), BF16)],
        compiler_params=_params(("parallel", "arbitrary")),
    )(h, wn, w)


def _proj_bwd(dproj, w, h, wn, dres, *, name, exchange=None, after=None):
    T, D = h.shape
    N = w.shape[0]
    tT, tN = min(TOKEN_TILE, T), _tile(N, PROJ_TILE_CAP)
    nn = N // tN

    def body(dp_ref, w_ref, h_ref, wn_ref, dres_ref, dh_ref, gw_ref, acc):
        i, j = pl.program_id(0), pl.program_id(1)

        @pl.when(jnp.logical_and(i == 0, j == 0))
        def _():
            gw_ref[...] = jnp.zeros_like(gw_ref)

        @pl.when(j == 0)
        def _():
            acc[...] = jnp.zeros_like(acc)

        acc[...] += _dot(dp_ref[...], w_ref[...])

        @pl.when(j == nn - 1)
        def _():
            dx, dw = _rms_bwd(h_ref[...], wn_ref[...], acc[...])
            dh_ref[...] = dres_ref[...] + dx
            gw_ref[...] += dw

    row = pl.BlockSpec((tT, D), lambda i, j: (i, 0))
    vec = pl.BlockSpec((1, D), lambda i, j: (0, 0))
    return _call(
        body, name=name, grid=(T // tT, nn),
        in_specs=[pl.BlockSpec((tT, tN), lambda i, j: (i, j)),
                  pl.BlockSpec((tN, D), lambda i, j: (j, 0)), row, vec, row],
        out_specs=[row, vec],
        out_shape=[jax.ShapeDtypeStruct((T, D), F32), jax.ShapeDtypeStruct((1, D), F32)],
        scratch_shapes=[pltpu.VMEM((tT, D), F32)], args=(dproj, w, h, wn, dres), exchange=exchange, after=after)


def _mla_prep_fwd(proj, pos, qn_w, kvn_w, w_uq, w_kv, tab, *, name):
    T = proj.shape[0]
    tT = min(TOKEN_TILE, T)
    a_blk = PROJ_FIXED // AW - 1

    def body(a_ref, pos_ref, qnw_ref, kvnw_ref, wuq_ref, wkv_ref, tab_ref,
             q_ref, k_ref, v_ref):
        cq = a_ref[:, 0:MLA_Q_RANK].astype(F32)
        ckv = a_ref[:, MLA_Q_RANK:MLA_Q_RANK + MLA_KV_RANK].astype(F32)
        kr = a_ref[:, 640:768].astype(F32)
        qn = _rms_fwd(cq, qnw_ref[...]).astype(BF16)
        kvn = _rms_fwd(ckv, kvnw_ref[...]).astype(BF16)
        cs = _rope_cs(pos_ref[...], tab_ref)
        q = _dot_nt(qn, wuq_ref[...])
        kv = _dot(kvn, wkv_ref[...])
        krr = _rope(kr, cs, MLA_ROPE // 2)
        for hd in range(MLA_HEADS):
            sl = slice(hd * HP, (hd + 1) * HP)
            q_ref[:, sl] = (_rope(q[:, sl], cs, MLA_ROPE // 2) * ATTN_SCALE).astype(BF16)
            k_ref[:, sl] = (kv[:, sl] + krr).astype(BF16)
        v_ref[...] = kv[:, QW:].astype(BF16)

    def full(r, c):
        return pl.BlockSpec((r, c), lambda i: (0, 0))

    def rows(c):
        return pl.BlockSpec((tT, c), lambda i: (i, 0))

    return pl.pallas_call(
        body, name=name, grid=(T // tT,),
        in_specs=[pl.BlockSpec((tT, AW), lambda i: (i, a_blk)), rows(1),
                  full(1, MLA_Q_RANK), full(1, MLA_KV_RANK),
                  full(QW, MLA_Q_RANK), full(MLA_KV_RANK, 2 * QW), full(8, LANES)],
        out_specs=[rows(QW), rows(QW), rows(QW)],
        out_shape=[jax.ShapeDtypeStruct((T, QW), BF16)] * 3,
        compiler_params=_params(("parallel",)),
    )(proj, pos, qn_w, kvn_w, w_uq, w_kv, tab)


def _mla_prep_bwd(dq, dk, dv, proj, pos, qn_w, kvn_w, w_uq, w_kv, tab, *, name):
    T = proj.shape[0]
    tT = min(TOKEN_TILE, T)
    nT = T // tT
    a_blk = PROJ_FIXED // AW - 1

    def body(dq_ref, dk_ref, dv_ref, a_ref, pos_ref, qnw_ref, kvnw_ref, wuq_ref, wkv_ref, tab_ref,
             da_ref, gqn_ref, gkvn_ref, dwuq_ref, dwkv_ref, dql_ref, dkvl_ref, acc_uq, acc_kv):
        @pl.when(pl.program_id(0) == 0)
        def _():
            gqn_ref[...] = jnp.zeros_like(gqn_ref)
            gkvn_ref[...] = jnp.zeros_like(gkvn_ref)
            acc_uq[...] = jnp.zeros_like(acc_uq)
            acc_kv[...] = jnp.zeros_like(acc_kv)

        cs = _rope_cs(pos_ref[...], tab_ref)
        dkr = jnp.zeros((tT, HP), F32)
        for hd in range(MLA_HEADS):
            sl = slice(hd * HP, (hd + 1) * HP)
            dql_ref[:, sl] = (_rope(dq_ref[:, sl], cs, MLA_ROPE // 2, inverse=True) * ATTN_SCALE).astype(BF16)
            dkh = dk_ref[:, sl]
            dkr = dkr + dkh
            dkvl_ref[:, sl] = dkh.astype(BF16)
        dkvl_ref[:, QW:] = dv_ref[...]
        dqn = _dot(dql_ref[...], wuq_ref[...])
        dkvn = _dot_nt(dkvl_ref[...], wkv_ref[...])
        cq = a_ref[:, 0:MLA_Q_RANK].astype(F32)
        ckv = a_ref[:, MLA_Q_RANK:MLA_Q_RANK + MLA_KV_RANK].astype(F32)
        dcq, gq = _rms_bwd(cq, qnw_ref[...], dqn)
        dckv, gkv = _rms_bwd(ckv, kvnw_ref[...], dkvn)
        gqn_ref[...] += gq
        gkvn_ref[...] += gkv
        da_ref[:, 0:MLA_Q_RANK] = dcq.astype(BF16)
        da_ref[:, MLA_Q_RANK:MLA_Q_RANK + MLA_KV_RANK] = dckv.astype(BF16)
        da_ref[:, 640:768] = _rope(dkr, cs, MLA_ROPE // 2, inverse=True).astype(BF16)
        da_ref[:, 768:AW] = jnp.zeros((tT, AW - 768), BF16)
        acc_uq[...] += _dot_tn(dql_ref[...], _rms_fwd(cq, qnw_ref[...]).astype(BF16))
        acc_kv[...] += _dot_tn(_rms_fwd(ckv, kvnw_ref[...]).astype(BF16), dkvl_ref[...])

        @pl.when(pl.program_id(0) == nT - 1)
        def _():
            dwuq_ref[...] = acc_uq[...].astype(BF16)
            dwkv_ref[...] = acc_kv[...].astype(BF16)

    def full(r, c):
        return pl.BlockSpec((r, c), lambda i: (0, 0))

    def rows(c):
        return pl.BlockSpec((tT, c), lambda i: (i, 0))

    return pl.pallas_call(
        body, name=name, grid=(nT,),
        in_specs=[rows(QW), rows(QW), rows(QW), pl.BlockSpec((tT, AW), lambda i: (i, a_blk)), rows(1),
                  full(1, MLA_Q_RANK), full(1, MLA_KV_RANK),
                  full(QW, MLA_Q_RANK), full(MLA_KV_RANK, 2 * QW), full(8, LANES)],
        out_specs=[rows(AW), full(1, MLA_Q_RANK), full(1, MLA_KV_RANK),
                   full(QW, MLA_Q_RANK), full(MLA_KV_RANK, 2 * QW)],
        out_shape=[jax.ShapeDtypeStruct((T, AW), BF16),
                   jax.ShapeDtypeStruct((1, MLA_Q_RANK), F32), jax.ShapeDtypeStruct((1, MLA_KV_RANK), F32),
                   jax.ShapeDtypeStruct((QW, MLA_Q_RANK), BF16), jax.ShapeDtypeStruct((MLA_KV_RANK, 2 * QW), BF16)],
        scratch_shapes=[pltpu.VMEM((tT, QW), BF16), pltpu.VMEM((tT, 2 * QW), BF16),
                        pltpu.VMEM((QW, MLA_Q_RANK), F32), pltpu.VMEM((MLA_KV_RANK, 2 * QW), F32)],
        compiler_params=_params(("arbitrary",)),
    )(dq, dk, dv, proj, pos, qn_w, kvn_w, w_uq, w_kv, tab)


def _flash_fwd(q, k, v, *, name, exchange=None):
    T = q.shape[0]
    H = q.shape[1] // HP
    tq = min(ATTN_TILE, T)
    nq = T // tq

    sub = tq // ATTN_CHAINS

    def body(q_ref, k_ref, v_ref, o_ref, lse_ref):
        qi = pl.program_id(1)
        qs = [q_ref[c * sub:(c + 1) * sub, :] for c in range(ATTN_CHAINS)]

        def update(carry, off, masked):
            nks = [(c + 1) * sub if masked else tq for c in range(ATTN_CHAINS)]
            scores = [_dot_nt(qs[c], k_ref[pl.ds(off, nks[c]), :]) for c in range(ATTN_CHAINS)]
            out = []
            for c in range(ATTN_CHAINS):
                m_prev, l_prev, acc = carry[c]
                nk, s = nks[c], scores[c]
                vb = v_ref[pl.ds(off, nk), :]
                if masked:
                    rows = lax.broadcasted_iota(jnp.int32, (sub, nk), 0) + c * sub
                    s = jnp.where(rows >= lax.broadcasted_iota(jnp.int32, (sub, nk), 1), s, NEG)
                m_new = jnp.maximum(m_prev, jnp.max(s, axis=1, keepdims=True))
                alpha = jnp.exp(m_prev - m_new)
                p = jnp.exp(s - m_new)
                out.append((m_new, alpha * l_prev + jnp.sum(p, axis=1, keepdims=True),
                            alpha * acc + _dot(p.astype(BF16), vb)))
            return tuple(out)

        init = tuple((jnp.full((sub, 1), NEG, F32), jnp.zeros((sub, 1), F32), jnp.zeros((sub, HP), F32))
                     for _ in range(ATTN_CHAINS))
        carry = lax.fori_loop(0, qi, lambda j, cr: update(cr, pl.multiple_of(j * tq, tq), False), init)
        carry = update(carry, pl.multiple_of(qi * tq, tq), True)
        for c in range(ATTN_CHAINS):
            m_fin, l_fin, acc = carry[c]
            o_ref[c * sub:(c + 1) * sub, :] = (acc / l_fin).astype(BF16)
            lse_ref[c * sub:(c + 1) * sub, :] = jnp.broadcast_to(m_fin + jnp.log(l_fin), (sub, HP))

    qspec = pl.BlockSpec((tq, HP), lambda h, i: (i, h))
    kspec = pl.BlockSpec((T, HP), lambda h, i: (0, h))
    return _call(
        body, name=name, grid=(H, nq),
        in_specs=[qspec, kspec, kspec], out_specs=[qspec, qspec],
        out_shape=[jax.ShapeDtypeStruct((T, H * HP), BF16), jax.ShapeDtypeStruct((T, H * HP), F32)],
        scratch_shapes=[], args=(q, k, v), exchange=exchange)


def _flash_bwd(q, k, v, do, lse, delta, *, name, exchange=None, after=None):
    T = q.shape[0]
    H = q.shape[1] // HP
    tq = min(ATTN_TILE, T)
    nq = T // tq
    sub = tq // ATTN_CHAINS

    def body(k_ref, v_ref, q_ref, do_ref, lse_ref, dl_ref, dq_ref, dk_ref, dv_ref):
        ki = pl.program_id(1)

        @pl.when(ki == 0)
        def _():
            dq_ref[...] = jnp.zeros_like(dq_ref)

        def grow(a):
            return a if a.shape[0] == tq else jnp.concatenate([a, jnp.zeros((tq - a.shape[0], HP), F32)], axis=0)

        def step(carry, j, masked):
            dk_acc, dv_acc = carry
            nks = [(c + 1) * sub if masked else tq for c in range(ATTN_CHAINS)]
            rws = [pl.ds(pl.multiple_of(j * tq + c * sub, sub), sub) for c in range(ATTN_CHAINS)]
            scores = [_dot_nt(q_ref[rws[c], :], k_ref[0:nks[c], :]) for c in range(ATTN_CHAINS)]
            dps = [_dot_nt(do_ref[rws[c], :], v_ref[0:nks[c], :]) for c in range(ATTN_CHAINS)]
            for c in range(ATTN_CHAINS):
                rows, nk, s, dp = rws[c], nks[c], scores[c], dps[c]
                kb = k_ref[0:nk, :]
                qb = q_ref[rows, :]
                dob = do_ref[rows, :]
                if masked:
                    ri = lax.broadcasted_iota(jnp.int32, (sub, nk), 0) + c * sub
                    s = jnp.where(ri >= lax.broadcasted_iota(jnp.int32, (sub, nk), 1), s, NEG)
                p = jnp.exp(s - lse_ref[rows, 0:1])
                dv_acc = dv_acc + grow(_dot_tn(p.astype(BF16), dob))
                ds = (p * (dp - dl_ref[rows, 0:1])).astype(BF16)
                dk_acc = dk_acc + grow(_dot_tn(ds, qb))
                dq_ref[rows, :] += _dot(ds, kb)
            return dk_acc, dv_acc

        carry = step((jnp.zeros((tq, HP), F32), jnp.zeros((tq, HP), F32)), ki, True)
        dk_acc, dv_acc = lax.fori_loop(ki + 1, nq, lambda j, cr: step(cr, j, False), carry)
        dk_ref[...] = dk_acc
        dv_ref[...] = dv_acc.astype(BF16)

    kspec = pl.BlockSpec((tq, HP), lambda h, j: (j, h))
    full = pl.BlockSpec((T, HP), lambda h, j: (0, h))
    return _call(
        body, name=name, grid=(H, nq),
        in_specs=[kspec, kspec, full, full, full, full], out_specs=[full, kspec, kspec],
        out_shape=[jax.ShapeDtypeStruct((T, H * HP), F32), jax.ShapeDtypeStruct((T, H * HP), F32),
                   jax.ShapeDtypeStruct((T, H * HP), BF16)],
        scratch_shapes=[], args=(k, v, q, do, lse, delta), exchange=exchange, after=after)


def _ret_consts(cc, hd):
    lg = math.log(1.0 - 2.0 ** (-5.0 - hd))
    diff = (lax.broadcasted_iota(jnp.int32, (cc, cc), 0) - lax.broadcasted_iota(jnp.int32, (cc, cc), 1)).astype(F32)
    decay = jnp.where(diff >= 0, jnp.exp(jnp.maximum(diff, 0.0) * lg), 0.0)
    idx = lax.broadcasted_iota(jnp.int32, (cc, 1), 0).astype(F32)
    zeta = jnp.exp((cc - 1.0 - idx) * lg)
    xi = jnp.exp((idx + 1.0) * lg)
    return decay, zeta, xi, math.exp(cc * lg)


def _ret_fwd(proj, pos, tab, *, name):
    T = proj.shape[0]
    cc = min(RET_TILE, T)
    n = T // cc

    def body(rq_ref, rk_ref, rv_ref, pos_ref, tab_ref, y_ref, yn_ref, rprev_ref, r_s):
        @pl.when(pl.program_id(0) == 0)
        def _():
            r_s[...] = jnp.zeros_like(r_s)

        cs = _rope_cs(pos_ref[...], tab_ref)
        for hd in range(RET_HEADS):
            sl = slice(hd * HP, (hd + 1) * HP)
            decay, zeta, xi, gc = _ret_consts(cc, hd)
            q = _rope(rq_ref[:, sl].astype(F32), cs, RET_DK // 2).astype(BF16)
            kf = _rope(rk_ref[:, sl].astype(F32), cs, RET_DK // 2) * (RET_DK ** -0.5)
            k = kf.astype(BF16)
            v = rv_ref[:, sl]
            r = r_s[hd]
            rprev_ref[0, hd] = r
            inner = (_dot_nt(q, k) * decay).astype(BF16)
            y = _dot(inner, v) + _dot(q, r.astype(BF16)) * xi
            r_s[hd] = r * gc + _dot_tn((kf * zeta).astype(BF16), v)
            y_ref[:, sl] = y
            mu = jnp.mean(y, axis=-1, keepdims=True)
            yc = y - mu
            var = jnp.mean(yc * yc, axis=-1, keepdims=True)
            yn_ref[:, sl] = (yc * lax.rsqrt(var + GN_EPS)).astype(BF16)

    def blk(j):
        return pl.BlockSpec((cc, RW), lambda i: (i, j))

    return pl.pallas_call(
        body, name=name, grid=(n,),
        in_specs=[blk(0), blk(1), blk(2), pl.BlockSpec((cc, 1), lambda i: (i, 0)),
                  pl.BlockSpec((8, LANES), lambda i: (0, 0))],
        out_specs=[blk(0), blk(0), pl.BlockSpec((1, RET_HEADS, HP, RET_DV), lambda i: (i, 0, 0, 0))],
        out_shape=[jax.ShapeDtypeStruct((T, RW), F32), jax.ShapeDtypeStruct((T, RW), BF16),
                   jax.ShapeDtypeStruct((n, RET_HEADS, HP, RET_DV), F32)],
        scratch_shapes=[pltpu.VMEM((RET_HEADS, HP, RET_DV), F32)],
        compiler_params=_params(("arbitrary",)),
    )(proj, proj, proj, pos, tab)


def _ret_bwd(dyn, y, proj, pos, tab, rprev, *, name):
    T = proj.shape[0]
    cc = min(RET_TILE, T)
    n = T // cc

    def body(dyn_ref, y_ref, rq_ref, rk_ref, rv_ref, pos_ref, tab_ref, rprev_ref,
             drq_ref, drk_ref, drv_ref, dr_s):
        @pl.when(pl.program_id(0) == 0)
        def _():
            dr_s[...] = jnp.zeros_like(dr_s)

        cs = _rope_cs(pos_ref[...], tab_ref)
        for hd in range(RET_HEADS):
            sl = slice(hd * HP, (hd + 1) * HP)
            decay, zeta, xi, gc = _ret_consts(cc, hd)
            q = _rope(rq_ref[:, sl].astype(F32), cs, RET_DK // 2).astype(BF16)
            kf = _rope(rk_ref[:, sl].astype(F32), cs, RET_DK // 2) * (RET_DK ** -0.5)
            k = kf.astype(BF16)
            v = rv_ref[:, sl]
            yv = y_ref[:, sl]
            mu = jnp.mean(yv, axis=-1, keepdims=True)
            yc = yv - mu
            rs = lax.rsqrt(jnp.mean(yc * yc, axis=-1, keepdims=True) + GN_EPS)
            yn = yc * rs
            dn = dyn_ref[:, sl]
            dy = rs * (dn - jnp.mean(dn, axis=-1, keepdims=True) - yn * jnp.mean(dn * yn, axis=-1, keepdims=True))
            dyb = dy.astype(BF16)
            dyx = (dy * xi).astype(BF16)
            dr = dr_s[hd]
            drb = dr.astype(BF16)
            inner = (_dot_nt(q, k) * decay).astype(BF16)
            da = (_dot_nt(dyb, v) * decay).astype(BF16)
            dv = _dot_tn(inner, dyb) + _dot((kf * zeta).astype(BF16), drb)
            dq = _dot(da, k) + _dot_nt(dyx, rprev_ref[0, hd].astype(BF16))
            dk = _dot_tn(da, q) + _dot_nt(v, drb) * zeta
            dr_s[hd] = dr * gc + _dot_tn(q, dyx)
            drq_ref[:, sl] = _rope(dq, cs, RET_DK // 2, inverse=True).astype(BF16)
            drk_ref[:, sl] = _rope(dk * (RET_DK ** -0.5), cs, RET_DK // 2, inverse=True).astype(BF16)
            drv_ref[:, sl] = dv.astype(BF16)

    def blk(j):
        return pl.BlockSpec((cc, RW), lambda i: (n - 1 - i, j))

    return pl.pallas_call(
        body, name=name, grid=(n,),
        in_specs=[blk(0), blk(0), blk(0), blk(1), blk(2), pl.BlockSpec((cc, 1), lambda i: (n - 1 - i, 0)),
                  pl.BlockSpec((8, LANES), lambda i: (0, 0)),
                  pl.BlockSpec((1, RET_HEADS, HP, RET_DV), lambda i: (n - 1 - i, 0, 0, 0))],
        out_specs=[blk(0), blk(0), blk(0)],
        out_shape=[jax.ShapeDtypeStruct((T, RW), BF16)] * 3,
        scratch_shapes=[pltpu.VMEM((RET_HEADS, HP, RET_DV), F32)],
        compiler_params=_params(("arbitrary",)),
    )(dyn, y, proj, proj, proj, pos, tab, rprev)


def _merge_fwd(o, yn, proj, gn_w, w_bm, w_br, w_out, h, post_w, *, name):
    T, D = h.shape
    tT = min(TOKEN_TILE, T)
    g_blk = PROJ_FIXED // D

    def body(o_ref, yn_ref, rg_ref, gm_ref, gr_ref, gnw_ref, wbm_ref, wbr_ref, wout_ref, h_ref, post_ref,
             omla_ref, oret_ref, m_ref, ho_ref):
        groups = [slice(c * (tT // FFN_CHAINS), (c + 1) * (tT // FFN_CHAINS)) for c in range(FFN_CHAINS)]
        o_mlas = [_dot(o_ref[rs, :], wbm_ref[...]) for rs in groups]
        for rs, o_mla in zip(groups, o_mlas):
            rg = rg_ref[rs, :].astype(F32)
            gated = (rg * _sigmoid(rg) * (yn_ref[rs, :].astype(F32) * gnw_ref[...])).astype(BF16)
            o_ret = _dot(gated, wbr_ref[...])
            omla_ref[rs, :] = o_mla.astype(BF16)
            oret_ref[rs, :] = o_ret.astype(BF16)
            merged = _sigmoid(gm_ref[rs, :].astype(F32)) * o_mla + _sigmoid(gr_ref[rs, :].astype(F32)) * o_ret
            m = _dot(merged.astype(BF16), wout_ref[...])
            m_ref[rs, :] = m
            ho_ref[rs, :] = h_ref[rs, :] + _rms_fwd(m, post_ref[...])

    def full(r, c):
        return pl.BlockSpec((r, c), lambda i: (0, 0))

    def rows(c, j=0):
        return pl.BlockSpec((tT, c), lambda i: (i, j))

    return pl.pallas_call(
        body, name=name, grid=(T // tT,),
        in_specs=[rows(QW), rows(RW), rows(RW, 3), rows(D, g_blk), rows(D, g_blk + 1), full(1, RW),
                  full(QW, D), full(RW, D), full(D, D), rows(D), full(1, D)],
        out_specs=[rows(D), rows(D), rows(D), rows(D)],
        out_shape=[jax.ShapeDtypeStruct((T, D), BF16), jax.ShapeDtypeStruct((T, D), BF16),
                   jax.ShapeDtypeStruct((T, D), F32), jax.ShapeDtypeStruct((T, D), F32)],
        compiler_params=_params(("parallel",)),
    )(o, yn, proj, proj, proj, gn_w, w_bm, w_br, w_out, h, post_w)


def _merge_bwd(dho, m, post_w, omla, oret, proj, yn, gn_w, o, w_out, w_bm, w_br, *, name):
    T, D = dho.shape
    tT = min(MERGE_TILE, T)
    g_blk = PROJ_FIXED // D

    nT = T // tT

    def body(dho_ref, m_ref, post_ref, omla_ref, oret_ref, rg_ref, gm_ref, gr_ref, yn_ref, gnw_ref, o_ref,
             wout_ref, wbm_ref, wbr_ref,
             dgm_ref, dgr_ref, do_ref, delta_ref, drg_ref, dyn_ref, gpost_ref, ggn_ref,
             dwout_ref, dwbm_ref, dwbr_ref, acc_out, acc_bm, acc_br):
        @pl.when(pl.program_id(0) == 0)
        def _():
            gpost_ref[...] = jnp.zeros_like(gpost_ref)
            ggn_ref[...] = jnp.zeros_like(ggn_ref)
            acc_out[...] = jnp.zeros_like(acc_out)
            acc_bm[...] = jnp.zeros_like(acc_bm)
            acc_br[...] = jnp.zeros_like(acc_br)

        dm, gp = _rms_bwd(m_ref[...], post_ref[...], dho_ref[...])
        gpost_ref[...] += gp
        dmb = dm.astype(BF16)
        dmerged = _dot_nt(dmb, wout_ref[...])
        o_mla = omla_ref[...].astype(F32)
        o_ret = oret_ref[...].astype(F32)
        sgm = _sigmoid(gm_ref[...].astype(F32))
        sgr = _sigmoid(gr_ref[...].astype(F32))
        acc_out[...] += _dot_tn((sgm * o_mla + sgr * o_ret).astype(BF16), dmb)
        dgm_ref[...] = (dmerged * o_mla * sgm * (1.0 - sgm)).astype(BF16)
        dgr_ref[...] = (dmerged * o_ret * sgr * (1.0 - sgr)).astype(BF16)
        domla = (dmerged * sgm).astype(BF16)
        acc_bm[...] += _dot_tn(o_ref[...], domla)
        do = _dot_nt(domla, wbm_ref[...])
        do_ref[...] = do.astype(BF16)
        for hd in range(MLA_HEADS):
            sl = slice(hd * HP, (hd + 1) * HP)
            d = jnp.sum(do[:, sl] * o_ref[:, sl].astype(F32), axis=-1, keepdims=True)
            delta_ref[:, sl] = jnp.broadcast_to(d, (tT, HP))
        doret = (dmerged * sgr).astype(BF16)
        dgated = _dot_nt(doret, wbr_ref[...])
        rg = rg_ref[...].astype(F32)
        sg = _sigmoid(rg)
        srg = rg * sg
        ynv = yn_ref[...].astype(F32)
        yw = ynv * gnw_ref[...]
        acc_br[...] += _dot_tn((srg * yw).astype(BF16), doret)
        drg_ref[...] = (dgated * yw * (sg * (1.0 + rg * (1.0 - sg)))).astype(BF16)
        dgs = dgated * srg
        dyn_ref[...] = dgs * gnw_ref[...]
        ggn_ref[...] += jnp.sum(dgs * ynv, axis=0, keepdims=True)

        @pl.when(pl.program_id(0) == nT - 1)
        def _():
            dwout_ref[...] = acc_out[...].astype(BF16)
            dwbm_ref[...] = acc_bm[...].astype(BF16)
            dwbr_ref[...] = acc_br[...].astype(BF16)

    def full(r, c):
        return pl.BlockSpec((r, c), lambda i: (0, 0), pipeline_mode=pl.Buffered(1))

    def rows(c, j=0):
        return pl.BlockSpec((tT, c), lambda i: (i, j))

    return pl.pallas_call(
        body, name=name, grid=(nT,),
        in_specs=[rows(D), rows(D), full(1, D), rows(D), rows(D), rows(RW, 3), rows(D, g_blk), rows(D, g_blk + 1),
                  rows(RW), full(1, RW), rows(QW), full(D, D), full(QW, D), full(RW, D)],
        out_specs=[rows(D), rows(D), rows(QW), rows(QW), rows(RW), rows(RW), full(1, D), full(1, RW),
                   full(D, D), full(QW, D), full(RW, D)],
        out_shape=[jax.ShapeDtypeStruct((T, D), BF16)] * 2
        + [jax.ShapeDtypeStruct((T, QW), BF16), jax.ShapeDtypeStruct((T, QW), F32),
           jax.ShapeDtypeStruct((T, RW), BF16), jax.ShapeDtypeStruct((T, RW), F32),
           jax.ShapeDtypeStruct((1, D), F32), jax.ShapeDtypeStruct((1, RW), F32),
           jax.ShapeDtypeStruct((D, D), BF16), jax.ShapeDtypeStruct((QW, D), BF16), jax.ShapeDtypeStruct((RW, D), BF16)],
        scratch_shapes=[pltpu.VMEM((D, D), F32), pltpu.VMEM((QW, D), F32), pltpu.VMEM((RW, D), F32)],
        compiler_params=_params(("arbitrary",)),
    )(dho, m, post_w, omla, oret, proj, proj, proj, yn, gn_w, o, w_out, w_bm, w_br)


def _mesh_pos():
    return lax.axis_index("x"), lax.axis_index("y"), lax.axis_index("c")


class _Gather:
    def __init__(self, shards):
        self.operands = list(shards)
        self.n = len(shards)
        self.out_shape = [jax.ShapeDtypeStruct((N_DEV,) + s.shape, s.dtype) for s in shards]
        self.scratch = [pltpu.SemaphoreType.DMA((7 * self.n,)), pltpu.SemaphoreType.DMA((7 * self.n,)),
                        pltpu.SemaphoreType.DMA((self.n,))]

    def phase(self, p, x_refs, out_refs, sems):
        send_sems, recv_sems, local_sems = sems
        x, y, c = _mesh_pos()
        me, sibling = (x, y, c), (x, y, 1 - c)
        chips = [(1 - x, y), (x, 1 - y), (1 - x, 1 - y)]

        def copy(w, k, block, to, src=None):
            slot = out_refs[w].at[4 * block[0] + 2 * block[1] + block[2]]
            return pltpu.make_async_remote_copy(
                src_ref=slot if src is None else src, dst_ref=slot,
                send_sem=send_sems.at[7 * w + k], recv_sem=recv_sems.at[7 * w + k],
                device_id=to, device_id_type=pl.DeviceIdType.MESH)

        for w in range(self.n):
            mine = pltpu.make_async_copy(x_refs[w], out_refs[w].at[4 * x + 2 * y + c], local_sems.at[w])
            first = [copy(w, 0, me, sibling, src=x_refs[w])]
            first += [copy(w, 1 + j, me, (*chip, c), src=x_refs[w]) for j, chip in enumerate(chips)]
            passed = [copy(w, 4 + j, (*chip, c), sibling) for j, chip in enumerate(chips)]
            if p == 0:
                mine.start()
                for cp in first:
                    cp.start()
            elif p == 1:
                for j, chip in enumerate(chips):
                    copy(w, 1 + j, (*chip, c), me).wait_recv()
                    passed[j].start()
            else:
                copy(w, 0, sibling, me).wait_recv()
                for j, chip in enumerate(chips):
                    copy(w, 4 + j, (*chip, 1 - c), me).wait_recv()
                for cp in first + passed:
                    cp.wait_send()
                mine.wait()


class _Scatter:
    def __init__(self, grads, whole=()):
        self.n_sliced = len(grads)
        self.operands = list(grads) + list(whole)
        self.n = len(self.operands)
        self.out_shape = [jax.ShapeDtypeStruct(g.shape, g.dtype) for g in grads]
        self.out_shape += [jax.ShapeDtypeStruct((N_DEV,) + a.shape, a.dtype) for a in whole]
        n_sem = (N_DEV - 1) * self.n
        self.scratch = [pltpu.SemaphoreType.DMA((n_sem,)), pltpu.SemaphoreType.DMA((n_sem,)),
                        pltpu.SemaphoreType.DMA((self.n,))]

    def phase(self, p, in_refs, out_refs, sems):
        if p == 1:
            return
        send_sems, recv_sems, local_sems = sems
        x, y, c = _mesh_pos()
        me = 4 * x + 2 * y + c

        def src(w, dev):
            return in_refs[w].at[dev] if w < self.n_sliced else in_refs[w]

        for w in range(self.n):
            own = None if local_sems is None else pltpu.make_async_copy(src(w, me), out_refs[w].at[me], local_sems.at[w])
            sends, recvs = [], []
            for r in range(1, N_DEV):
                px = 1 - x if r & 4 else x
                py = 1 - y if r & 2 else y
                pc = 1 - c if r & 1 else c
                peer, pidx = (px, py, pc), 4 * px + 2 * py + pc
                k = (N_DEV - 1) * w + r - 1
                sends.append(pltpu.make_async_remote_copy(
                    src_ref=src(w, pidx), dst_ref=out_refs[w].at[me], send_sem=send_sems.at[k],
                    recv_sem=recv_sems.at[k], device_id=peer, device_id_type=pl.DeviceIdType.MESH))
                recvs.append(pltpu.make_async_remote_copy(
                    src_ref=src(w, me), dst_ref=out_refs[w].at[pidx], send_sem=send_sems.at[k],
                    recv_sem=recv_sems.at[k], device_id=peer, device_id_type=pl.DeviceIdType.MESH))
            if p == 0:
                if own is not None:
                    own.start()
                for cp in sends:
                    cp.start()
            else:
                for cp in recvs:
                    cp.wait_recv()
                for cp in sends:
                    cp.wait_send()
                if own is not None:
                    own.wait()


class _SplitScatter:
    def __init__(self, ex, name):
        self.ex, self.name = ex, name

    def _specs(self):
        ex = self.ex
        hbm = pl.BlockSpec(memory_space=pltpu.HBM)
        sem = pl.BlockSpec(memory_space=pltpu.SEMAPHORE)
        effect = pltpu.CompilerParams(has_side_effects=pltpu.SideEffectType.DATAFLOW_SIDE_EFFECTING)
        buffers = [pltpu.HBM(a.shape, a.dtype) for a in ex.operands] + [pltpu.HBM(s.shape, s.dtype) for s in ex.out_shape]
        return hbm, sem, effect, buffers

    def start(self):
        ex, n = self.ex, self.ex.n
        n_sem = (N_DEV - 1) * n
        hbm, sem, effect, buffers = self._specs()
        in_hbm = lambda a: pltpu.with_memory_space_constraint(a, pltpu.HBM)

        me = 4 * lax.axis_index("x") + 2 * lax.axis_index("y") + lax.axis_index("c")
        lands = []
        for w, (a, s) in enumerate(zip(ex.operands, ex.out_shape)):
            mine = lax.dynamic_index_in_dim(a, me, 0, keepdims=True) if w < ex.n_sliced else a[None]
            lands.append(lax.dynamic_update_slice_in_dim(lax.empty(s.shape, s.dtype), mine, me, 0))

        def start_body(*refs):
            ex.phase(0, refs[:n], refs[n:2 * n], (refs[2 * n], refs[2 * n + 1], None))
            refs[-1][...] = jnp.zeros_like(refs[-1])

        self.started = pl.pallas_call(
            start_body, name=self.name + "_start",
            out_shape=[pltpu.SemaphoreType.DMA((n_sem,)), pltpu.SemaphoreType.DMA((n_sem,))] + buffers
            + [jax.ShapeDtypeStruct((8, LANES), F32)],
            in_specs=[hbm] * (2 * n), out_specs=[sem, sem] + [hbm] * (2 * n) + [pl.BlockSpec(memory_space=pltpu.VMEM)],
            input_output_aliases={i: 2 + i for i in range(2 * n)}, compiler_params=effect,
        )(*[in_hbm(a) for a in ex.operands], *[in_hbm(a) for a in lands])
        return self.started[-1]

    def wait(self, after):
        ex, n = self.ex, self.ex.n
        hbm, sem, effect, buffers = self._specs()
        anyspec = pl.BlockSpec(memory_space=pl.ANY)

        def wait_body(*refs):
            ex.phase(2, refs[:n], refs[n:2 * n], (refs[2 * n], refs[2 * n + 1], None))

        done = pl.pallas_call(
            wait_body, name=self.name + "_wait", out_shape=buffers,
            in_specs=[hbm] * (2 * n) + [sem, sem] + [anyspec] * len(after), out_specs=[hbm] * (2 * n),
            input_output_aliases={i: i for i in range(2 * n)}, compiler_params=effect,
        )(*self.started[2:2 + 2 * n], self.started[0], self.started[1], *after)
        return done[n:]


def _exchange_alone(ex, *, name):
    n = ex.n

    def body(*refs):
        for p in range(3):
            ex.phase(p, refs[:n], refs[n:2 * n], refs[2 * n:])

    anyspec = pl.BlockSpec(memory_space=pl.ANY)
    return pl.pallas_call(body, name=name, out_shape=ex.out_shape, in_specs=[anyspec] * n,
                          out_specs=[anyspec] * n, scratch_shapes=ex.scratch)(*ex.operands)


def _adam_step(w_ref, p_ref, m_ref, v_ref, g_ref, d_ref, nm_ref, nv_ref):
    g = p_ref[0].astype(F32)
    for j in range(1, N_DEV):
        g = g + p_ref[j].astype(F32)
    g_ref[...] = g
    nm = ADAM_B1 * m_ref[...] + (1.0 - ADAM_B1) * g
    nv = ADAM_B2 * v_ref[...] + (1.0 - ADAM_B2) * (g * g)
    nm_ref[...] = nm
    nv_ref[...] = nv
    m_hat = nm / (1.0 - ADAM_B1 ** ADAM_STEP)
    v_hat = nv / (1.0 - ADAM_B2 ** ADAM_STEP)
    d_ref[...] = -ADAM_LR * (m_hat / (jnp.sqrt(v_hat) + ADAM_EPS) + ADAM_WD * w_ref[...])


def _adamw_vectors(ws, parts, ms, vs, *, name):
    n = len(ws)

    def body(*refs):
        w_refs, p_refs, m_refs, v_refs = (refs[i * n:(i + 1) * n] for i in range(4))
        outs = refs[4 * n:]
        for i in range(n):
            _adam_step(w_refs[i], p_refs[i], m_refs[i], v_refs[i], *outs[4 * i:4 * i + 4])

    return pl.pallas_call(
        body, name=name,
        out_shape=[jax.ShapeDtypeStruct(w.shape, F32) for w in ws for _ in range(4)],
    )(*ws, *parts, *ms, *vs)


def _adamw(w, parts, m, v, after, *, name):
    G, R, n = w.shape
    tn = 512 if (n > 512 and n % 512 == 0) else n
    tr = R
    for t in range(16, R, 16):
        if R % t == 0 and t * tn <= ADAM_BLOCK_CAP:
            tr = t
    if R * tn <= ADAM_BLOCK_CAP:
        tr = R

    def body(w_ref, p_ref, m_ref, v_ref, after_ref, g_ref, d_ref, nm_ref, nv_ref):
        _adam_step(w_ref, p_ref, m_ref, v_ref, g_ref, d_ref, nm_ref, nv_ref)

    blk = pl.BlockSpec((None, tr, tn), lambda g, i, j: (g, i, j))
    return pl.pallas_call(
        body, name=name, grid=(G, R // tr, n // tn),
        in_specs=[blk, pl.BlockSpec((N_DEV, None, tr, tn), lambda g, i, j: (0, g, i, j)), blk, blk,
                  pl.BlockSpec((8, LANES), lambda g, i, j: (0, 0))],
        out_specs=[blk, blk, blk, blk],
        out_shape=[jax.ShapeDtypeStruct((G, R, n), F32)] * 4,
        compiler_params=_params(("parallel", "parallel", "parallel")),
    )(w, parts, m, v, after)


def _pad_last(a, width):
    return jnp.pad(a, [(0, 0)] * (a.ndim - 1) + [(0, width - a.shape[-1])])


def _cols_of(g):
    return g.transpose(1, 0, 2).reshape(g.shape[1], N_DEV * g.shape[2])


def _col_shards(w):
    return w.reshape(w.shape[0], N_DEV, w.shape[1] // N_DEV).transpose(1, 0, 2)


def kernel(x, positions, ffn1_pre_w, ffn1_w1, ffn1_w2, ffn1_post_w, mix_pre_w, w_in, mla_q_norm_w, mla_w_uq, mla_kv_norm_w, mla_w_ukv, ret_gn_w, w_branch_mla, w_branch_ret, w_out, mix_post_w, ffn2_pre_w, ffn2_w1, ffn2_w2, ffn2_post_w, loss_target, m_ffn1_pre_w, m_ffn1_w1, m_ffn1_w2, m_ffn1_post_w, m_mix_pre_w, m_w_in, m_mla_q_norm_w, m_mla_w_uq, m_mla_kv_norm_w, m_mla_w_ukv, m_ret_gn_w, m_w_branch_mla, m_w_branch_ret, m_w_out, m_mix_post_w, m_ffn2_pre_w, m_ffn2_w1, m_ffn2_w2, m_ffn2_post_w, v_ffn1_pre_w, v_ffn1_w1, v_ffn1_w2, v_ffn1_post_w, v_mix_pre_w, v_w_in, v_mla_q_norm_w, v_mla_w_uq, v_mla_kv_norm_w, v_mla_w_ukv, v_ret_gn_w, v_w_branch_mla, v_w_branch_ret, v_w_out, v_mix_post_w, v_ffn2_pre_w, v_ffn2_w1, v_ffn2_w2, v_ffn2_post_w):
    T, D = x.shape[1], x.shape[2]
    h0 = x[0]
    tgt = loss_target[0]
    pos = positions.reshape(T, 1).astype(F32)

    big = [("ffn1_w1", ffn1_w1, m_ffn1_w1, v_ffn1_w1), ("ffn1_w2", ffn1_w2, m_ffn1_w2, v_ffn1_w2),
           ("w_in", w_in, m_w_in, v_w_in), ("mla_w_uq", mla_w_uq, m_mla_w_uq, v_mla_w_uq),
           ("mla_w_ukv", mla_w_ukv, m_mla_w_ukv, v_mla_w_ukv),
           ("w_branch_mla", w_branch_mla, m_w_branch_mla, v_w_branch_mla),
           ("w_branch_ret", w_branch_ret, m_w_branch_ret, v_w_branch_ret),
           ("w_out", w_out, m_w_out, v_w_out),
           ("ffn2_w1", ffn2_w1, m_ffn2_w1, v_ffn2_w1), ("ffn2_w2", ffn2_w2, m_ffn2_w2, v_ffn2_w2)]
    small = [("ffn1_pre_w", ffn1_pre_w, m_ffn1_pre_w, v_ffn1_pre_w), ("ffn1_post_w", ffn1_post_w, m_ffn1_post_w, v_ffn1_post_w),
             ("mix_pre_w", mix_pre_w, m_mix_pre_w, v_mix_pre_w), ("mla_q_norm_w", mla_q_norm_w, m_mla_q_norm_w, v_mla_q_norm_w),
             ("mla_kv_norm_w", mla_kv_norm_w, m_mla_kv_norm_w, v_mla_kv_norm_w), ("ret_gn_w", ret_gn_w, m_ret_gn_w, v_ret_gn_w),
             ("mix_post_w", mix_post_w, m_mix_post_w, v_mix_post_w), ("ffn2_pre_w", ffn2_pre_w, m_ffn2_pre_w, v_ffn2_pre_w),
             ("ffn2_post_w", ffn2_post_w, m_ffn2_post_w, v_ffn2_post_w)]

    half = ffn1_w2.shape[1]
    hp = -(-half // LANES) * LANES

    def rows_view(w):
        return w[0].T

    def send_w1(w):
        return jnp.pad(rows_view(w).reshape(2, half, D), ((0, 0), (0, hp - half), (0, 0))).reshape(2 * hp, D).astype(BF16)

    def send_w2(w):
        return jnp.pad(w[0], ((0, hp - half), (0, 0))).astype(BF16)

    mixer = ["w_in", "mla_w_uq", "mla_w_ukv", "w_branch_mla", "w_branch_ret", "w_out"]
    uq_w = MLA_NOPE + MLA_ROPE
    mixer_send = [rows_view(w_in).astype(BF16), jnp.pad(rows_view(mla_w_uq), ((0, HP - uq_w), (0, 0))).astype(BF16),
                  mla_w_ukv[0].astype(BF16), w_branch_mla[0].astype(BF16), w_branch_ret[0].astype(BF16),
                  w_out[0].astype(BF16)]

    w1a, w2a = _exchange_alone(_Gather([send_w1(ffn1_w1), send_w2(ffn1_w2)]), name="gather_ffn1")
    w2a = w2a.reshape(N_DEV // 2, 2 * hp, D)
    u1, f1, h1, a0, *got = _ffn_fwd(h0, ffn1_pre_w, w1a, w2a, ffn1_post_w, None, name="ffn1_fwd_gather_mixer",
                                exchange=_Gather(mixer_send))
    fw = dict(zip(mixer, got))

    wi = fw["w_in"].reshape(-1, D)
    cq_w, ckv_w, kr_w = wi[0:384], wi[384:640], wi[640:672]
    rq_w, rk_w = wi[672:928], wi[928:1184]
    rv_w, rg_w = wi[1184:1696], wi[1696:2208]
    gm_w, gr_w = wi[2208:2208 + D], wi[2208 + D:2208 + 2 * D]
    zer = lambda n: jnp.zeros((n, D), BF16)
    head_rows = lambda a, h: jnp.pad(a.reshape(h, -1, D), ((0, 0), (0, HP - a.shape[0] // h), (0, 0))).reshape(h * HP, D)
    w_in_p = jnp.concatenate([head_rows(rq_w, RET_HEADS), head_rows(rk_w, RET_HEADS), rv_w, rg_w,
                              cq_w, ckv_w, zer(MLA_NOPE), kr_w, zer(HP - MLA_NOPE - MLA_ROPE), zer(AW - 768),
                              gm_w, gr_w], axis=0)
    w_uq_p = fw["mla_w_uq"].reshape(QW, MLA_Q_RANK)
    ukv = fw["mla_w_ukv"].transpose(1, 0, 2)
    w_kv_p = jnp.concatenate([_pad_last(ukv[:, :, :MLA_NOPE], HP).reshape(MLA_KV_RANK, QW),
                              _pad_last(ukv[:, :, MLA_NOPE:], HP).reshape(MLA_KV_RANK, QW)], axis=1)
    w_bm_p = jnp.pad(_cols_of(fw["w_branch_mla"]).reshape(MLA_HEADS, MLA_V, D),
                     ((0, 0), (0, HP - MLA_V), (0, 0))).reshape(QW, D)
    w_br, w_o = _cols_of(fw["w_branch_ret"]), fw["w_out"].reshape(D, D)
    tab_mla = _rope_table(MLA_NOPE, MLA_ROPE // 2)
    tab_ret = _rope_table(0, RET_DK // 2)

    proj, a1 = _rms_matmul(h1, mix_pre_w, w_in_p, name="mixer_in_proj")
    q, k, v = _mla_prep_fwd(proj, pos, mla_q_norm_w, mla_kv_norm_w, w_uq_p, w_kv_p, tab_mla, name="mla_prep_fwd")
    o, lse, w1b, w2b = _flash_fwd(q, k, v, name="mla_attn_fwd_gather_ffn2",
                                  exchange=_Gather([send_w1(ffn2_w1), send_w2(ffn2_w2)]))
    w2b = w2b.reshape(N_DEV // 2, 2 * hp, D)
    ypre, yn, rprev = _ret_fwd(proj, pos, tab_ret, name="retention_fwd")
    omla, oret, m, h2 = _merge_fwd(o, yn, proj, ret_gn_w, w_bm_p, w_br, w_o, h1, mix_post_w, name="merge_fwd")
    u2, f2, _, a2, dy, lossp = _ffn_fwd(h2, ffn2_pre_w, w1b, w2b, ffn2_post_w, tgt, name="ffn2_fwd_loss")

    def grad(x, dy, tag, after=None):
        return _matmul_tn(x if x.ndim == 3 else x[None], dy if dy.ndim == 3 else dy[None], name=tag, after=after)

    g2, du2, df2, dh2, gpost2, gpre2 = _ffn_bwd(dy, f2, ffn2_post_w, h2, ffn2_pre_w, u2, w2b, w1b, name="ffn2_bwd")
    dw1b, = grad(du2.reshape(N_DEV, T, 2 * hp), a2, "ffn2_dw1")
    dw2b = grad(g2, df2, "ffn2_dw2")[0].reshape(N_DEV, hp, D)
    (dgm, dgr, do, delta, drg, dyn, gpostm, ggn, dw_out, dw_bm_p, dw_br) = _merge_bwd(
        dh2, m, mix_post_w, omla, oret, proj, yn, ret_gn_w, o, w_o, w_bm_p, w_br, name="merge_bwd")
    sc_ffn2 = _SplitScatter(_Scatter([dw1b, dw2b]), "scatter_ffn2")
    dq, dk, dv = _flash_bwd(q, k, v, do, lse, delta, name="mla_attn_bwd", after=sc_ffn2.start())
    da, gqn, gkvn, dw_uq_p, dw_kv_p = _mla_prep_bwd(dq, dk, dv, proj, pos, mla_q_norm_w, mla_kv_norm_w, w_uq_p, w_kv_p, tab_mla, name="mla_prep_bwd")
    drq, drk, drv = _ret_bwd(dyn, ypre, proj, pos, tab_ret, rprev, name="retention_bwd")
    dproj = jnp.concatenate([drq, drk, drv, drg, da, dgm, dgr], axis=1)
    dw_in_p = grad(dproj, a1, "dw_in")[0][0]

    dw_uq = dw_uq_p.reshape(MLA_HEADS, HP, MLA_Q_RANK)[:, :uq_w]
    dkp = dw_kv_p[:, :QW].reshape(MLA_KV_RANK, MLA_HEADS, HP)[:, :, :MLA_NOPE]
    dvp = dw_kv_p[:, QW:].reshape(MLA_KV_RANK, MLA_HEADS, HP)[:, :, :MLA_V]
    dw_ukv = jnp.concatenate([dkp, dvp], axis=2).transpose(1, 0, 2)
    dw_bm = dw_bm_p.reshape(MLA_HEADS, HP, D)[:, :MLA_V].reshape(MLA_HEADS * MLA_V, D)
    small_mixer_grads = [dw_uq, dw_ukv, _col_shards(dw_bm), _col_shards(dw_br), dw_out.reshape(N_DEV, D // N_DEV, D)]
    sc_small = _SplitScatter(_Scatter(small_mixer_grads), "scatter_mixer_small")
    dh1, gmixpre = _proj_bwd(dproj, w_in_p, h1, mix_pre_w, dh2, name="mixer_in_bwd", after=sc_small.start())
    unhead = lambda a, h, wd: a.reshape(h, HP, D)[:, :wd].reshape(h * wd, D)
    c0 = 4 * RW
    dw_in = jnp.concatenate([
        dw_in_p[c0:c0 + 384], dw_in_p[c0 + 384:c0 + 640], dw_in_p[c0 + 640 + MLA_NOPE:c0 + 640 + MLA_NOPE + MLA_ROPE],
        unhead(dw_in_p[0:RW], RET_HEADS, RET_DK), unhead(dw_in_p[RW:2 * RW], RET_HEADS, RET_DK),
        dw_in_p[2 * RW:3 * RW], dw_in_p[3 * RW:4 * RW],
        dw_in_p[PROJ_FIXED:PROJ_FIXED + D], dw_in_p[PROJ_FIXED + D:PROJ_FIXED + 2 * D]], axis=0).reshape(N_DEV, -1, D)
    sc_w_in = _SplitScatter(_Scatter([dw_in]), "scatter_w_in")
    g1, du1, df1, dx, gpost1, gpre1 = _ffn_bwd(
        dh1, f1, ffn1_post_w, h0, ffn1_pre_w, u1, w2a, w1a, name="ffn1_bwd", after=sc_w_in.start())
    dw2a = grad(g1, df1, "ffn1_dw2")[0].reshape(N_DEV, hp, D)
    sc_dw2a = _SplitScatter(_Scatter([dw2a]), "scatter_ffn1_dw2")
    dw1a, = grad(du1.reshape(N_DEV, T, 2 * hp), a0, "ffn1_dw1", after=sc_dw2a.start())

    small_g = {"ffn1_pre_w": gpre1, "ffn1_post_w": gpost1, "mix_pre_w": gmixpre, "mla_q_norm_w": gqn,
               "mla_kv_norm_w": gkvn, "ret_gn_w": ggn, "mix_post_w": gpostm, "ffn2_pre_w": gpre2, "ffn2_post_w": gpost2}
    sc_last = _SplitScatter(_Scatter([dw1a], whole=[small_g[nm] for nm, *_ in small] + [lossp]), "scatter_ffn1_dw1")
    token = sc_last.start()
    recv_ffn2 = sc_ffn2.wait([token])
    recv_mixer = sc_w_in.wait([token]) + sc_small.wait([token])
    recv_w2a, = sc_dw2a.wait([token])
    parts = dict(zip(mixer, recv_mixer))
    parts.update(ffn1_w2=recv_w2a, ffn2_w1=recv_ffn2[0], ffn2_w2=recv_ffn2[1])
    as_is = (lambda a: a, lambda p: p[:, None], lambda a: a)
    views = {nm: as_is for nm, *_ in big}
    for nm in ("ffn1_w1", "ffn2_w1"):
        views[nm] = (lambda a: rows_view(a).reshape(2, half, D), lambda p: p.reshape(N_DEV, 2, hp, D),
                     lambda a: a.reshape(2 * half, D).T[None])
    for nm in ("w_in", "mla_w_uq"):
        views[nm] = (lambda a: rows_view(a)[None], lambda p: p[:, None], lambda a: a[0].T[None])

    def update(nm, w, m_, v_, after):
        to_view, parts_view, back = views[nm]
        return [back(a) for a in _adamw(to_view(w), parts_view(parts[nm]), to_view(m_), to_view(v_), after,
                                        name="adamw_" + nm)]

    big_out = {nm: update(nm, w, m_, v_, token) for nm, w, m_, v_ in big if nm != "ffn1_w1"}
    recv_w1a, *small_parts, loss_parts = sc_last.wait([d[0] for d in big_out.values()])
    loss = jnp.sum(loss_parts[:, ::8, 0])
    parts["ffn1_w1"] = recv_w1a
    big_out["ffn1_w1"] = update("ffn1_w1", ffn1_w1, m_ffn1_w1, v_ffn1_w1, jnp.zeros((8, LANES), F32))
    small_out = _adamw_vectors([w for _, w, _, _ in small], small_parts, [a for _, _, a, _ in small],
                               [a for _, _, _, a in small], name="adamw_replicated")

    order = ["ffn1_pre_w", "ffn1_w1", "ffn1_w2", "ffn1_post_w", "mix_pre_w", "w_in", "mla_q_norm_w", "mla_w_uq",
             "mla_kv_norm_w", "mla_w_ukv", "ret_gn_w", "w_branch_mla", "w_branch_ret", "w_out", "mix_post_w",
             "ffn2_pre_w", "ffn2_w1", "ffn2_w2", "ffn2_post_w"]
    outs = [loss, dx[None]]
    for i in range(4):
        both = {nm: big_out[nm][i] for nm in big_out}
        both.update({nm: small_out[4 * j + i] for j, (nm, *_) in enumerate(small)})
        outs += [both[nm] for nm in order]
    return tuple(outs)
```

```python
import math

import numpy as np
import jax
import jax.numpy as jnp
from jax import lax
from jax.experimental import pallas as pl
from jax.experimental.pallas import tpu as pltpu

F32, BF16 = jnp.float32, jnp.bfloat16

MLA_HEADS, MLA_NOPE, MLA_ROPE, MLA_V = 8, 64, 32, 64
MLA_Q_RANK, MLA_KV_RANK = 384, 256
RET_HEADS, RET_DK, RET_DV = 4, 64, 128
ROPE_BASE, NORM_EPS, GN_EPS = 10000.0, 1e-6, 1e-6
ADAM_LR, ADAM_B1, ADAM_B2, ADAM_EPS, ADAM_WD, ADAM_STEP = 0.001, 0.9, 0.999, 1e-08, 0.01, 10
ATTN_SCALE = 1.0 / math.sqrt(MLA_NOPE + MLA_ROPE)

N_DEV = 8
LANES = 128
HP = LANES
QW = MLA_HEADS * HP
RW = RET_HEADS * HP
AW = 1024
PROJ_FIXED = 4 * RW + AW
NEG = -1e30

TOKEN_TILE = 512
ATTN_TILE = 1024
ATTN_CHAINS = 2
FFN_CHAINS = 2
RET_TILE = 256
PROJ_TILE_CAP = 2560
PROJ_TOKEN_TILE = 1024
GRAD_TILE_CAP = 1408
GRAD_TOKEN_TILE = 4096
ADAM_BLOCK_CAP = 192 * 1024
MERGE_TILE = 256
VMEM_LIMIT = 56 * 1024 * 1024


def _tile(n, cap, mult=LANES):
    if n <= cap:
        return n
    best = None
    for t in range(mult, cap + 1, mult):
        if n % t == 0:
            best = t
    assert best is not None, (n, cap, mult)
    return best


def _params(sem):
    return pltpu.CompilerParams(dimension_semantics=sem, vmem_limit_bytes=VMEM_LIMIT)


def _dot(a, b):
    return lax.dot_general(a, b, (((1,), (0,)), ((), ())), preferred_element_type=F32)


def _dot_nt(a, b):
    return lax.dot_general(a, b, (((1,), (1,)), ((), ())), preferred_element_type=F32)


def _dot_tn(a, b):
    return lax.dot_general(a, b, (((0,), (0,)), ((), ())), preferred_element_type=F32)


def _sigmoid(x):
    return pl.reciprocal(1.0 + jnp.exp(-x), approx=True)


def _rms_fwd(x, w):
    r = lax.rsqrt(jnp.mean(x * x, axis=-1, keepdims=True) + NORM_EPS)
    return x * r * w


def _rms_bwd(x, w, dy):
    r = lax.rsqrt(jnp.mean(x * x, axis=-1, keepdims=True) + NORM_EPS)
    xh = x * r
    g = dy * w
    dx = r * (g - xh * jnp.mean(g * xh, axis=-1, keepdims=True))
    return dx, jnp.sum(dy * xh, axis=0, keepdims=True)


def _rope_table(first, half):
    inv = (np.float32(ROPE_BASE) ** (-(np.arange(half, dtype=np.float32) / np.float32(half)))).astype(np.float32)
    tab = np.zeros((8, LANES), np.float32)
    tab[0, first:first + half] = inv
    tab[0, first + half:first + 2 * half] = inv
    tab[1, first:first + half] = -1.0
    tab[2, first + half:first + 2 * half] = 1.0
    return jnp.asarray(tab)


def _rope_cs(pos, tab_ref):
    ang = pos * tab_ref[0:1, :]
    s = jnp.sin(ang)
    return jnp.cos(ang), s * tab_ref[1:2, :], s * tab_ref[2:3, :]


def _rope(x, cs, half, inverse=False):
    c, s1, s2 = cs
    a = pltpu.roll(x, LANES - half, 1) * s1 + pltpu.roll(x, half, 1) * s2
    return x * c - a if inverse else x * c + a


def _call(body, *, name, grid, in_specs, out_specs, out_shape, scratch_shapes, args, exchange=None, after=None):
    sem = ("arbitrary",) * len(grid)
    anyspec = pl.BlockSpec(memory_space=pl.ANY)
    if exchange is None and after is not None:
        n_own = len(in_specs)

        def behind(*refs):
            body(*refs[:n_own], *refs[n_own + 1:])

        return pl.pallas_call(behind, name=name, grid=grid, in_specs=list(in_specs) + [anyspec], out_specs=out_specs,
                              out_shape=out_shape, scratch_shapes=scratch_shapes, compiler_params=_params(sem))(*args, after)
    if exchange is None:
        return pl.pallas_call(body, name=name, grid=grid, in_specs=in_specs, out_specs=out_specs,
                              out_shape=out_shape, scratch_shapes=scratch_shapes, compiler_params=_params(sem))(*args)
    n_in, n_out, e = len(in_specs), len(out_specs), exchange.n
    total = math.prod(grid)

    def carried(*refs):
        own = refs[:n_in] + refs[n_in + e:n_in + e + n_out] + refs[n_in + 2 * e + n_out:len(refs) - 3]
        ex_refs = (refs[n_in:n_in + e], refs[n_in + e + n_out:n_in + 2 * e + n_out], refs[len(refs) - 3:])
        step = pl.program_id(0)
        for d in range(1, len(grid)):
            step = step * grid[d] + pl.program_id(d)

        @pl.when(step == 0)
        def _():
            exchange.phase(0, *ex_refs)

        @pl.when(step == (3 * total) // 4)
        def _():
            exchange.phase(1, *ex_refs)

        body(*own)

        @pl.when(step == total - 1)
        def _():
            exchange.phase(2, *ex_refs)

    return pl.pallas_call(
        carried, name=name, grid=grid, in_specs=list(in_specs) + [anyspec] * e,
        out_specs=list(out_specs) + [anyspec] * e, out_shape=list(out_shape) + exchange.out_shape,
        scratch_shapes=list(scratch_shapes) + exchange.scratch, compiler_params=_params(sem),
    )(*args, *exchange.operands)


def _ffn_fwd(h, pre_w, w1, w2, post_w, target, *, name, exchange=None):
    T, D = h.shape
    nk, ck = w2.shape[0], w2.shape[1]
    tT = min(TOKEN_TILE, T)
    nT = T // tT
    with_loss = target is not None

    def body(*refs):
        if with_loss:
            (h_ref, pre_ref, w1g_ref, w1u_ref, w2_ref, post_ref, tgt_ref,
             u_ref, f_ref, ho_ref, a_s, dy_ref, loss_ref, acc) = refs
        else:
            (h_ref, pre_ref, w1g_ref, w1u_ref, w2_ref, post_ref,
             u_ref, f_ref, ho_ref, a_s, acc) = refs
        k = pl.program_id(1)

        @pl.when(k == 0)
        def _():
            a_s[...] = _rms_fwd(h_ref[...], pre_ref[...]).astype(BF16)
            acc[...] = jnp.zeros_like(acc)

        for c in range(FFN_CHAINS):
            rs = slice(c * (tT // FFN_CHAINS), (c + 1) * (tT // FFN_CHAINS))
            a = a_s[rs, :]
            ug = _dot_nt(a, w1g_ref[...])
            uu = _dot_nt(a, w1u_ref[...])
            u_ref[0, rs, :] = ug.astype(BF16)
            u_ref[1, rs, :] = uu.astype(BF16)
            acc[rs, :] += _dot((ug * _sigmoid(ug) * uu).astype(BF16), w2_ref[...])

        @pl.when(k == nk - 1)
        def _():
            f = acc[...]
            f_ref[...] = f
            ho = h_ref[...] + 0.5 * _rms_fwd(f, post_ref[...])
            ho_ref[...] = ho
            if with_loss:
                e = ho - tgt_ref[...]
                dy_ref[...] = e * (1.0 / D)
                loss_ref[...] = jnp.full(loss_ref.shape, (0.5 / D) * jnp.sum(e * e), F32)

    row = pl.BlockSpec((tT, D), lambda i, k: (i, 0))
    vec = pl.BlockSpec((1, D), lambda i, k: (0, 0))
    in_specs = [row, vec,
                pl.BlockSpec((None, ck, D), lambda i, k: (k, 0, 0)),
                pl.BlockSpec((None, ck, D), lambda i, k: (nk + k, 0, 0)),
                pl.BlockSpec((None, ck, D), lambda i, k: (k, 0, 0)),
                vec]
    out_shape = [jax.ShapeDtypeStruct((2, nk, T, ck), BF16),
                 jax.ShapeDtypeStruct((T, D), F32),
                 jax.ShapeDtypeStruct((T, D), F32),
                 jax.ShapeDtypeStruct((T, D), BF16)]
    out_specs = [pl.BlockSpec((2, None, tT, ck), lambda i, k: (0, k, i, 0)), row, row, row]
    args = [h, pre_w, w1, w1, w2, post_w]
    if with_loss:
        in_specs.append(row)
        args.append(target)
        out_shape += [jax.ShapeDtypeStruct((T, D), F32), jax.ShapeDtypeStruct((nT * 8, LANES), F32)]
        out_specs += [row, pl.BlockSpec((8, LANES), lambda i, k: (i, 0))]
    return _call(body, name=name, grid=(nT, nk), in_specs=in_specs, out_specs=out_specs, out_shape=out_shape,
                 scratch_shapes=[pltpu.VMEM((tT, D), F32)], args=args, exchange=exchange)


def _ffn_bwd(dho, f, post_w, h, pre_w, u, w2, w1, *, name, exchange=None, after=None):
    T, D = h.shape
    nk, ck = w2.shape[0], w2.shape[1]
    tT = min(TOKEN_TILE, T)
    nT = T // tT

    def body(dho_ref, f_ref, post_ref, h_ref, pre_ref, u_ref, w2_ref, w1g_ref, w1u_ref,
             g_ref, du_ref, df_s, dh_ref, gpost_ref, gpre_ref, da_acc):
        i, k = pl.program_id(0), pl.program_id(1)

        @pl.when(jnp.logical_and(i == 0, k == 0))
        def _():
            gpost_ref[...] = jnp.zeros_like(gpost_ref)
            gpre_ref[...] = jnp.zeros_like(gpre_ref)

        @pl.when(k == 0)
        def _():
            dx, dw = _rms_bwd(f_ref[...], post_ref[...], 0.5 * dho_ref[...])
            df_s[...] = dx.astype(BF16)
            gpost_ref[...] += dw
            da_acc[...] = jnp.zeros_like(da_acc)

        groups = [slice(c * (tT // FFN_CHAINS), (c + 1) * (tT // FFN_CHAINS)) for c in range(FFN_CHAINS)]
        dgs = [_dot_nt(df_s[rs, :], w2_ref[...]) for rs in groups]
        for rs, dg in zip(groups, dgs):
            ug = u_ref[0, rs, :].astype(F32)
            uu = u_ref[1, rs, :].astype(F32)
            sg = _sigmoid(ug)
            sl = ug * sg
            g_ref[rs, :] = (sl * uu).astype(BF16)
            dug = (dg * uu * (sg + sl * (1.0 - sg))).astype(BF16)
            duu = (dg * sl).astype(BF16)
            du_ref[0, rs, :] = dug
            du_ref[1, rs, :] = duu
            da_acc[rs, :] += _dot(dug, w1g_ref[...]) + _dot(duu, w1u_ref[...])

        @pl.when(k == nk - 1)
        def _():
            dx, dw = _rms_bwd(h_ref[...], pre_ref[...], da_acc[...])
            dh_ref[...] = dho_ref[...] + dx
            gpre_ref[...] += dw

    row = pl.BlockSpec((tT, D), lambda i, k: (i, 0))
    vec = pl.BlockSpec((1, D), lambda i, k: (0, 0))
    return _call(
        body, name=name, grid=(nT, nk),
        in_specs=[row, row, vec, row, vec,
                  pl.BlockSpec((2, None, tT, ck), lambda i, k: (0, k, i, 0)),
                  pl.BlockSpec((None, ck, D), lambda i, k: (k, 0, 0)),
                  pl.BlockSpec((None, ck, D), lambda i, k: (k, 0, 0)),
                  pl.BlockSpec((None, ck, D), lambda i, k: (nk + k, 0, 0))],
        out_specs=[pl.BlockSpec((None, tT, ck), lambda i, k: (k, i, 0)),
                   pl.BlockSpec((2, None, tT, ck), lambda i, k: (0, k, i, 0)),
                   row, row, vec, vec],
        out_shape=[jax.ShapeDtypeStruct((nk, T, ck), BF16),
                   jax.ShapeDtypeStruct((2, nk, T, ck), BF16),
                   jax.ShapeDtypeStruct((T, D), BF16),
                   jax.ShapeDtypeStruct((T, D), F32),
                   jax.ShapeDtypeStruct((1, D), F32),
                   jax.ShapeDtypeStruct((1, D), F32)],
        scratch_shapes=[pltpu.VMEM((tT, D), F32)],
        args=(dho, f, post_w, h, pre_w, u, w2, w1, w1), exchange=exchange, after=after)


def _matmul_tn(x, dy, *, name, exchange=None, after=None):
    Px, T, K = x.shape
    Py, _, N = dy.shape
    P = max(Px, Py)
    tT, tK, tN = min(GRAD_TOKEN_TILE, T), _tile(K, GRAD_TILE_CAP), _tile(N, GRAD_TILE_CAP)
    nt = T // tT

    def body(x_ref, dy_ref, o_ref, acc):
        t = pl.program_id(3)

        @pl.when(t == 0)
        def _():
            acc[...] = jnp.zeros_like(acc)

        acc[...] += _dot_tn(x_ref[...], dy_ref[...])

        @pl.when(t == nt - 1)
        def _():
            o_ref[...] = acc[...].astype(BF16)

    return _call(
        body, name=name, grid=(P, K // tK, N // tN, nt),
        in_specs=[pl.BlockSpec((None, tT, tK), lambda p, a, b, t: (p if Px > 1 else 0, t, a)),
                  pl.BlockSpec((None, tT, tN), lambda p, a, b, t: (p if Py > 1 else 0, t, b))],
        out_specs=[pl.BlockSpec((None, tK, tN), lambda p, a, b, t: (p, a, b))],
        out_shape=[jax.ShapeDtypeStruct((P, K, N), BF16)],
        scratch_shapes=[pltpu.VMEM((tK, tN), F32)], args=(x, dy), exchange=exchange, after=after)


def _rms_matmul(h, wn, w, *, name):
    T, D = h.shape
    N = w.shape[0]
    tT, tN = min(PROJ_TOKEN_TILE, T), _tile(N, PROJ_TILE_CAP)

    def body(h_ref, wn_ref, w_ref, y_ref, a_ref):
        @pl.when(pl.program_id(1) == 0)
        def _():
            a_ref[...] = _rms_fwd(h_ref[...], wn_ref[...]).astype(BF16)

        y_ref[...] = _dot_nt(a_ref[...], w_ref[...]).astype(BF16)

    return pl.pallas_call(
        body, name=name, grid=(T // tT, N // tN),
        in_specs=[pl.BlockSpec((tT, D), lambda i, j: (i, 0)),
                  pl.BlockSpec((1, D), lambda i, j: (0, 0)),
                  pl.BlockSpec((tN, D), lambda i, j: (j, 0))],
        out_specs=[pl.BlockSpec((tT, tN), lambda i, j: (i, j)),
                   pl.BlockSpec((tT, D), lambda i, j: (i, 0))],
        out_shape=[jax.ShapeDtypeStruct((T, N), BF16), jax.ShapeDtypeStruct((T, D), BF16)],
        compiler_params=_params(("parallel", "arbitrary")),
    )(h, wn, w)


def _proj_bwd(dproj, w, h, wn, dres, *, name, exchange=None, after=None):
    T, D = h.shape
    N = w.shape[0]
    tT, tN = min(PROJ_TOKEN_TILE, T), _tile(N, PROJ_TILE_CAP)
    nn = N // tN

    def body(dp_ref, w_ref, h_ref, wn_ref, dres_ref, dh_ref, gw_ref, acc):
        i, j = pl.program_id(0), pl.program_id(1)

        @pl.when(jnp.logical_and(i == 0, j == 0))
        def _():
            gw_ref[...] = jnp.zeros_like(gw_ref)

        @pl.when(j == 0)
        def _():
            acc[...] = jnp.zeros_like(acc)

        acc[...] += _dot(dp_ref[...], w_ref[...])

        @pl.when(j == nn - 1)
        def _():
            dx, dw = _rms_bwd(h_ref[...], wn_ref[...], acc[...])
            dh_ref[...] = dres_ref[...] + dx
            gw_ref[...] += dw

    row = pl.BlockSpec((tT, D), lambda i, j: (i, 0))
    vec = pl.BlockSpec((1, D), lambda i, j: (0, 0))
    return _call(
        body, name=name, grid=(T // tT, nn),
        in_specs=[pl.BlockSpec((tT, tN), lambda i, j: (i, j)),
                  pl.BlockSpec((tN, D), lambda i, j: (j, 0)), row, vec, row],
        out_specs=[row, vec],
        out_shape=[jax.ShapeDtypeStruct((T, D), F32), jax.ShapeDtypeStruct((1, D), F32)],
        scratch_shapes=[pltpu.VMEM((tT, D), F32)], args=(dproj, w, h, wn, dres), exchange=exchange, after=after)


def _mla_prep_fwd(proj, pos, qn_w, kvn_w, w_uq, w_kv, tab, *, name):
    T = proj.shape[0]
    tT = min(TOKEN_TILE, T)
    a_blk = PROJ_FIXED // AW - 1

    def body(a_ref, pos_ref, qnw_ref, kvnw_ref, wuq_ref, wkv_ref, tab_ref,
             q_ref, k_ref, v_ref):
        cq = a_ref[:, 0:MLA_Q_RANK].astype(F32)
        ckv = a_ref[:, MLA_Q_RANK:MLA_Q_RANK + MLA_KV_RANK].astype(F32)
        kr = a_ref[:, 640:768].astype(F32)
        qn = _rms_fwd(cq, qnw_ref[...]).astype(BF16)
        kvn = _rms_fwd(ckv, kvnw_ref[...]).astype(BF16)
        cs = _rope_cs(pos_ref[...], tab_ref)
        q = _dot_nt(qn, wuq_ref[...])
        kv = _dot(kvn, wkv_ref[...])
        krr = _rope(kr, cs, MLA_ROPE // 2)
        for hd in range(MLA_HEADS):
            sl = slice(hd * HP, (hd + 1) * HP)
            q_ref[:, sl] = (_rope(q[:, sl], cs, MLA_ROPE // 2) * ATTN_SCALE).astype(BF16)
            k_ref[:, sl] = (kv[:, sl] + krr).astype(BF16)
        v_ref[...] = kv[:, QW:].astype(BF16)

    def full(r, c):
        return pl.BlockSpec((r, c), lambda i: (0, 0))

    def rows(c):
        return pl.BlockSpec((tT, c), lambda i: (i, 0))

    return pl.pallas_call(
        body, name=name, grid=(T // tT,),
        in_specs=[pl.BlockSpec((tT, AW), lambda i: (i, a_blk)), rows(1),
                  full(1, MLA_Q_RANK), full(1, MLA_KV_RANK),
                  full(QW, MLA_Q_RANK), full(MLA_KV_RANK, 2 * QW), full(8, LANES)],
        out_specs=[rows(QW), rows(QW), rows(QW)],
        out_shape=[jax.ShapeDtypeStruct((T, QW), BF16)] * 3,
        compiler_params=_params(("parallel",)),
    )(proj, pos, qn_w, kvn_w, w_uq, w_kv, tab)


def _mla_prep_bwd(dq, dk, dv, proj, pos, qn_w, kvn_w, w_uq, w_kv, tab, *, name):
    T = proj.shape[0]
    tT = min(TOKEN_TILE, T)
    nT = T // tT
    a_blk = PROJ_FIXED // AW - 1

    def body(dq_ref, dk_ref, dv_ref, a_ref, pos_ref, qnw_ref, kvnw_ref, wuq_ref, wkv_ref, tab_ref,
             da_ref, gqn_ref, gkvn_ref, dwuq_ref, dwkv_ref, dql_ref, dkvl_ref, acc_uq, acc_kv):
        @pl.when(pl.program_id(0) == 0)
        def _():
            gqn_ref[...] = jnp.zeros_like(gqn_ref)
            gkvn_ref[...] = jnp.zeros_like(gkvn_ref)
            acc_uq[...] = jnp.zeros_like(acc_uq)
            acc_kv[...] = jnp.zeros_like(acc_kv)

        cs = _rope_cs(pos_ref[...], tab_ref)
        dkr = jnp.zeros((tT, HP), F32)
        for hd in range(MLA_HEADS):
            sl = slice(hd * HP, (hd + 1) * HP)
            dql_ref[:, sl] = (_rope(dq_ref[:, sl], cs, MLA_ROPE // 2, inverse=True) * ATTN_SCALE).astype(BF16)
            dkh = dk_ref[:, sl]
            dkr = dkr + dkh
            dkvl_ref[:, sl] = dkh.astype(BF16)
        dkvl_ref[:, QW:] = dv_ref[...]
        dqn = _dot(dql_ref[...], wuq_ref[...])
        dkvn = _dot_nt(dkvl_ref[...], wkv_ref[...])
        cq = a_ref[:, 0:MLA_Q_RANK].astype(F32)
        ckv = a_ref[:, MLA_Q_RANK:MLA_Q_RANK + MLA_KV_RANK].astype(F32)
        dcq, gq = _rms_bwd(cq, qnw_ref[...], dqn)
        dckv, gkv = _rms_bwd(ckv, kvnw_ref[...], dkvn)
        gqn_ref[...] += gq
        gkvn_ref[...] += gkv
        da_ref[:, 0:MLA_Q_RANK] = dcq.astype(BF16)
        da_ref[:, MLA_Q_RANK:MLA_Q_RANK + MLA_KV_RANK] = dckv.astype(BF16)
        da_ref[:, 640:768] = _rope(dkr, cs, MLA_ROPE // 2, inverse=True).astype(BF16)
        da_ref[:, 768:AW] = jnp.zeros((tT, AW - 768), BF16)
        acc_uq[...] += _dot_tn(dql_ref[...], _rms_fwd(cq, qnw_ref[...]).astype(BF16))
        acc_kv[...] += _dot_tn(_rms_fwd(ckv, kvnw_ref[...]).astype(BF16), dkvl_ref[...])

        @pl.when(pl.program_id(0) == nT - 1)
        def _():
            dwuq_ref[...] = acc_uq[...].astype(BF16)
            dwkv_ref[...] = acc_kv[...].astype(BF16)

    def full(r, c):
        return pl.BlockSpec((r, c), lambda i: (0, 0))

    def rows(c):
        return pl.BlockSpec((tT, c), lambda i: (i, 0))

    return pl.pallas_call(
        body, name=name, grid=(nT,),
        in_specs=[rows(QW), rows(QW), rows(QW), pl.BlockSpec((tT, AW), lambda i: (i, a_blk)), rows(1),
                  full(1, MLA_Q_RANK), full(1, MLA_KV_RANK),
                  full(QW, MLA_Q_RANK), full(MLA_KV_RANK, 2 * QW), full(8, LANES)],
        out_specs=[rows(AW), full(1, MLA_Q_RANK), full(1, MLA_KV_RANK),
                   full(QW, MLA_Q_RANK), full(MLA_KV_RANK, 2 * QW)],
        out_shape=[jax.ShapeDtypeStruct((T, AW), BF16),
                   jax.ShapeDtypeStruct((1, MLA_Q_RANK), F32), jax.ShapeDtypeStruct((1, MLA_KV_RANK), F32),
                   jax.ShapeDtypeStruct((QW, MLA_Q_RANK), BF16), jax.ShapeDtypeStruct((MLA_KV_RANK, 2 * QW), BF16)],
        scratch_shapes=[pltpu.VMEM((tT, QW), BF16), pltpu.VMEM((tT, 2 * QW), BF16),
                        pltpu.VMEM((QW, MLA_Q_RANK), F32), pltpu.VMEM((MLA_KV_RANK, 2 * QW), F32)],
        compiler_params=_params(("arbitrary",)),
    )(dq, dk, dv, proj, pos, qn_w, kvn_w, w_uq, w_kv, tab)


def _flash_fwd(q, k, v, *, name, exchange=None):
    T = q.shape[0]
    H = q.shape[1] // HP
    tq = min(ATTN_TILE, T)
    nq = T // tq

    sub = tq // ATTN_CHAINS

    def body(q_ref, k_ref, v_ref, o_ref, lse_ref):
        qi = pl.program_id(1)
        qs = [q_ref[c * sub:(c + 1) * sub, :] for c in range(ATTN_CHAINS)]

        def update(carry, off, masked):
            nks = [(c + 1) * sub if masked else tq for c in range(ATTN_CHAINS)]
            scores = [_dot_nt(qs[c], k_ref[pl.ds(off, nks[c]), :]) for c in range(ATTN_CHAINS)]
            out = []
            for c in range(ATTN_CHAINS):
                m_prev, l_prev, acc = carry[c]
                nk, s = nks[c], scores[c]
                vb = v_ref[pl.ds(off, nk), :]
                if masked:
                    rows = lax.broadcasted_iota(jnp.int32, (sub, nk), 0) + c * sub
                    s = jnp.where(rows >= lax.broadcasted_iota(jnp.int32, (sub, nk), 1), s, NEG)
                m_new = jnp.maximum(m_prev, jnp.max(s, axis=1, keepdims=True))
                alpha = jnp.exp(m_prev - m_new)
                p = jnp.exp(s - m_new)
                out.append((m_new, alpha * l_prev + jnp.sum(p, axis=1, keepdims=True),
                            alpha * acc + _dot(p.astype(BF16), vb)))
            return tuple(out)

        init = tuple((jnp.full((sub, 1), NEG, F32), jnp.zeros((sub, 1), F32), jnp.zeros((sub, HP), F32))
                     for _ in range(ATTN_CHAINS))
        carry = lax.fori_loop(0, qi, lambda j, cr: update(cr, pl.multiple_of(j * tq, tq), False), init)
        carry = update(carry, pl.multiple_of(qi * tq, tq), True)
        for c in range(ATTN_CHAINS):
            m_fin, l_fin, acc = carry[c]
            o_ref[c * sub:(c + 1) * sub, :] = (acc / l_fin).astype(BF16)
            lse_ref[c * sub:(c + 1) * sub, :] = jnp.broadcast_to(m_fin + jnp.log(l_fin), (sub, HP))

    qspec = pl.BlockSpec((tq, HP), lambda h, i: (i, h))
    kspec = pl.BlockSpec((T, HP), lambda h, i: (0, h))
    return _call(
        body, name=name, grid=(H, nq),
        in_specs=[qspec, kspec, kspec], out_specs=[qspec, qspec],
        out_shape=[jax.ShapeDtypeStruct((T, H * HP), BF16), jax.ShapeDtypeStruct((T, H * HP), F32)],
        scratch_shapes=[], args=(q, k, v), exchange=exchange)


def _flash_bwd(q, k, v, do, lse, delta, *, name, exchange=None, after=None):
    T = q.shape[0]
    H = q.shape[1] // HP
    tq = min(ATTN_TILE, T)
    nq = T // tq
    sub = tq // ATTN_CHAINS

    def body(k_ref, v_ref, q_ref, do_ref, lse_ref, dl_ref, dq_ref, dk_ref, dv_ref):
        ki = pl.program_id(1)

        @pl.when(ki == 0)
        def _():
            dq_ref[...] = jnp.zeros_like(dq_ref)

        def grow(a):
            return a if a.shape[0] == tq else jnp.concatenate([a, jnp.zeros((tq - a.shape[0], HP), F32)], axis=0)

        def step(carry, j, masked):
            dk_acc, dv_acc = carry
            nks = [(c + 1) * sub if masked else tq for c in range(ATTN_CHAINS)]
            rws = [pl.ds(pl.multiple_of(j * tq + c * sub, sub), sub) for c in range(ATTN_CHAINS)]
            scores = [_dot_nt(q_ref[rws[c], :], k_ref[0:nks[c], :]) for c in range(ATTN_CHAINS)]
            dps = [_dot_nt(do_ref[rws[c], :], v_ref[0:nks[c], :]) for c in range(ATTN_CHAINS)]
            for c in range(ATTN_CHAINS):
                rows, nk, s, dp = rws[c], nks[c], scores[c], dps[c]
                kb = k_ref[0:nk, :]
                qb = q_ref[rows, :]
                dob = do_ref[rows, :]
                if masked:
                    ri = lax.broadcasted_iota(jnp.int32, (sub, nk), 0) + c * sub
                    s = jnp.where(ri >= lax.broadcasted_iota(jnp.int32, (sub, nk), 1), s, NEG)
                p = jnp.exp(s - lse_ref[rows, 0:1])
                dv_acc = dv_acc + grow(_dot_tn(p.astype(BF16), dob))
                ds = (p * (dp - dl_ref[rows, 0:1])).astype(BF16)
                dk_acc = dk_acc + grow(_dot_tn(ds, qb))
                dq_ref[rows, :] += _dot(ds, kb)
            return dk_acc, dv_acc

        carry = step((jnp.zeros((tq, HP), F32), jnp.zeros((tq, HP), F32)), ki, True)
        dk_acc, dv_acc = lax.fori_loop(ki + 1, nq, lambda j, cr: step(cr, j, False), carry)
        dk_ref[...] = dk_acc
        dv_ref[...] = dv_acc.astype(BF16)

    kspec = pl.BlockSpec((tq, HP), lambda h, j: (j, h))
    full = pl.BlockSpec((T, HP), lambda h, j: (0, h))
    return _call(
        body, name=name, grid=(H, nq),
        in_specs=[kspec, kspec, full, full, full, full], out_specs=[full, kspec, kspec],
        out_shape=[jax.ShapeDtypeStruct((T, H * HP), F32), jax.ShapeDtypeStruct((T, H * HP), F32),
                   jax.ShapeDtypeStruct((T, H * HP), BF16)],
        scratch_shapes=[], args=(k, v, q, do, lse, delta), exchange=exchange, after=after)


def _ret_consts(cc, hd):
    lg = math.log(1.0 - 2.0 ** (-5.0 - hd))
    diff = (lax.broadcasted_iota(jnp.int32, (cc, cc), 0) - lax.broadcasted_iota(jnp.int32, (cc, cc), 1)).astype(F32)
    decay = jnp.where(diff >= 0, jnp.exp(jnp.maximum(diff, 0.0) * lg), 0.0)
    idx = lax.broadcasted_iota(jnp.int32, (cc, 1), 0).astype(F32)
    zeta = jnp.exp((cc - 1.0 - idx) * lg)
    xi = jnp.exp((idx + 1.0) * lg)
    return decay, zeta, xi, math.exp(cc * lg)


def _ret_fwd(proj, pos, tab, *, name):
    T = proj.shape[0]
    cc = min(RET_TILE, T)
    n = T // cc

    def body(rq_ref, rk_ref, rv_ref, pos_ref, tab_ref, y_ref, yn_ref, rprev_ref, r_s):
        @pl.when(pl.program_id(0) == 0)
        def _():
            r_s[...] = jnp.zeros_like(r_s)

        cs = _rope_cs(pos_ref[...], tab_ref)
        for hd in range(RET_HEADS):
            sl = slice(hd * HP, (hd + 1) * HP)
            decay, zeta, xi, gc = _ret_consts(cc, hd)
            q = _rope(rq_ref[:, sl].astype(F32), cs, RET_DK // 2).astype(BF16)
            kf = _rope(rk_ref[:, sl].astype(F32), cs, RET_DK // 2) * (RET_DK ** -0.5)
            k = kf.astype(BF16)
            v = rv_ref[:, sl]
            r = r_s[hd]
            rprev_ref[0, hd] = r
            inner = (_dot_nt(q, k) * decay).astype(BF16)
            y = _dot(inner, v) + _dot(q, r.astype(BF16)) * xi
            r_s[hd] = r * gc + _dot_tn((kf * zeta).astype(BF16), v)
            y_ref[:, sl] = y
            mu = jnp.mean(y, axis=-1, keepdims=True)
            yc = y - mu
            var = jnp.mean(yc * yc, axis=-1, keepdims=True)
            yn_ref[:, sl] = (yc * lax.rsqrt(var + GN_EPS)).astype(BF16)

    def blk(j):
        return pl.BlockSpec((cc, RW), lambda i: (i, j))

    return pl.pallas_call(
        body, name=name, grid=(n,),
        in_specs=[blk(0), blk(1), blk(2), pl.BlockSpec((cc, 1), lambda i: (i, 0)),
                  pl.BlockSpec((8, LANES), lambda i: (0, 0))],
        out_specs=[blk(0), blk(0), pl.BlockSpec((1, RET_HEADS, HP, RET_DV), lambda i: (i, 0, 0, 0))],
        out_shape=[jax.ShapeDtypeStruct((T, RW), F32), jax.ShapeDtypeStruct((T, RW), BF16),
                   jax.ShapeDtypeStruct((n, RET_HEADS, HP, RET_DV), F32)],
        scratch_shapes=[pltpu.VMEM((RET_HEADS, HP, RET_DV), F32)],
        compiler_params=_params(("arbitrary",)),
    )(proj, proj, proj, pos, tab)


def _ret_bwd(dyn, y, proj, pos, tab, rprev, *, name):
    T = proj.shape[0]
    cc = min(RET_TILE, T)
    n = T // cc

    def body(dyn_ref, y_ref, rq_ref, rk_ref, rv_ref, pos_ref, tab_ref, rprev_ref,
             drq_ref, drk_ref, drv_ref, dr_s):
        @pl.when(pl.program_id(0) == 0)
        def _():
            dr_s[...] = jnp.zeros_like(dr_s)

        cs = _rope_cs(pos_ref[...], tab_ref)
        for hd in range(RET_HEADS):
            sl = slice(hd * HP, (hd + 1) * HP)
            decay, zeta, xi, gc = _ret_consts(cc, hd)
            q = _rope(rq_ref[:, sl].astype(F32), cs, RET_DK // 2).astype(BF16)
            kf = _rope(rk_ref[:, sl].astype(F32), cs, RET_DK // 2) * (RET_DK ** -0.5)
            k = kf.astype(BF16)
            v = rv_ref[:, sl]
            yv = y_ref[:, sl]
            mu = jnp.mean(yv, axis=-1, keepdims=True)
            yc = yv - mu
            rs = lax.rsqrt(jnp.mean(yc * yc, axis=-1, keepdims=True) + GN_EPS)
            yn = yc * rs
            dn = dyn_ref[:, sl]
            dy = rs * (dn - jnp.mean(dn, axis=-1, keepdims=True) - yn * jnp.mean(dn * yn, axis=-1, keepdims=True))
            dyb = dy.astype(BF16)
            dyx = (dy * xi).astype(BF16)
            dr = dr_s[hd]
            drb = dr.astype(BF16)
            inner = (_dot_nt(q, k) * decay).astype(BF16)
            da = (_dot_nt(dyb, v) * decay).astype(BF16)
            dv = _dot_tn(inner, dyb) + _dot((kf * zeta).astype(BF16), drb)
            dq = _dot(da, k) + _dot_nt(dyx, rprev_ref[0, hd].astype(BF16))
            dk = _dot_tn(da, q) + _dot_nt(v, drb) * zeta
            dr_s[hd] = dr * gc + _dot_tn(q, dyx)
            drq_ref[:, sl] = _rope(dq, cs, RET_DK // 2, inverse=True).astype(BF16)
            drk_ref[:, sl] = _rope(dk * (RET_DK ** -0.5), cs, RET_DK // 2, inverse=True).astype(BF16)
            drv_ref[:, sl] = dv.astype(BF16)

    def blk(j):
        return pl.BlockSpec((cc, RW), lambda i: (n - 1 - i, j))

    return pl.pallas_call(
        body, name=name, grid=(n,),
        in_specs=[blk(0), blk(0), blk(0), blk(1), blk(2), pl.BlockSpec((cc, 1), lambda i: (n - 1 - i, 0)),
                  pl.BlockSpec((8, LANES), lambda i: (0, 0)),
                  pl.BlockSpec((1, RET_HEADS, HP, RET_DV), lambda i: (n - 1 - i, 0, 0, 0))],
        out_specs=[blk(0), blk(0), blk(0)],
        out_shape=[jax.ShapeDtypeStruct((T, RW), BF16)] * 3,
        scratch_shapes=[pltpu.VMEM((RET_HEADS, HP, RET_DV), F32)],
        compiler_params=_params(("arbitrary",)),
    )(dyn, y, proj, proj, proj, pos, tab, rprev)


def _merge_fwd(o, yn, proj, gn_w, w_bm, w_br, w_out, h, post_w, *, name):
    T, D = h.shape
    tT = min(TOKEN_TILE, T)
    g_blk = PROJ_FIXED // D

    def body(o_ref, yn_ref, rg_ref, gm_ref, gr_ref, gnw_ref, wbm_ref, wbr_ref, wout_ref, h_ref, post_ref,
             omla_ref, oret_ref, m_ref, ho_ref):
        groups = [slice(c * (tT // FFN_CHAINS), (c + 1) * (tT // FFN_CHAINS)) for c in range(FFN_CHAINS)]
        o_mlas = [_dot(o_ref[rs, :], wbm_ref[...]) for rs in groups]
        for rs, o_mla in zip(groups, o_mlas):
            rg = rg_ref[rs, :].astype(F32)
            gated = (rg * _sigmoid(rg) * (yn_ref[rs, :].astype(F32) * gnw_ref[...])).astype(BF16)
            o_ret = _dot(gated, wbr_ref[...])
            omla_ref[rs, :] = o_mla.astype(BF16)
            oret_ref[rs, :] = o_ret.astype(BF16)
            merged = _sigmoid(gm_ref[rs, :].astype(F32)) * o_mla + _sigmoid(gr_ref[rs, :].astype(F32)) * o_ret
            m = _dot(merged.astype(BF16), wout_ref[...])
            m_ref[rs, :] = m
            ho_ref[rs, :] = h_ref[rs, :] + _rms_fwd(m, post_ref[...])

    def full(r, c):
        return pl.BlockSpec((r, c), lambda i: (0, 0))

    def rows(c, j=0):
        return pl.BlockSpec((tT, c), lambda i: (i, j))

    return pl.pallas_call(
        body, name=name, grid=(T // tT,),
        in_specs=[rows(QW), rows(RW), rows(RW, 3), rows(D, g_blk), rows(D, g_blk + 1), full(1, RW),
                  full(QW, D), full(RW, D), full(D, D), rows(D), full(1, D)],
        out_specs=[rows(D), rows(D), rows(D), rows(D)],
        out_shape=[jax.ShapeDtypeStruct((T, D), BF16), jax.ShapeDtypeStruct((T, D), BF16),
                   jax.ShapeDtypeStruct((T, D), F32), jax.ShapeDtypeStruct((T, D), F32)],
        compiler_params=_params(("parallel",)),
    )(o, yn, proj, proj, proj, gn_w, w_bm, w_br, w_out, h, post_w)


def _merge_bwd(dho, m, post_w, omla, oret, proj, yn, gn_w, o, w_out, w_bm, w_br, *, name):
    T, D = dho.shape
    tT = min(MERGE_TILE, T)
    g_blk = PROJ_FIXED // D

    nT = T // tT

    def body(dho_ref, m_ref, post_ref, omla_ref, oret_ref, rg_ref, gm_ref, gr_ref, yn_ref, gnw_ref, o_ref,
             wout_ref, wbm_ref, wbr_ref,
             dgm_ref, dgr_ref, do_ref, delta_ref, drg_ref, dyn_ref, gpost_ref, ggn_ref,
             dwout_ref, dwbm_ref, dwbr_ref, acc_out, acc_bm, acc_br):
        @pl.when(pl.program_id(0) == 0)
        def _():
            gpost_ref[...] = jnp.zeros_like(gpost_ref)
            ggn_ref[...] = jnp.zeros_like(ggn_ref)
            acc_out[...] = jnp.zeros_like(acc_out)
            acc_bm[...] = jnp.zeros_like(acc_bm)
            acc_br[...] = jnp.zeros_like(acc_br)

        dm, gp = _rms_bwd(m_ref[...], post_ref[...], dho_ref[...])
        gpost_ref[...] += gp
        dmb = dm.astype(BF16)
        dmerged = _dot_nt(dmb, wout_ref[...])
        o_mla = omla_ref[...].astype(F32)
        o_ret = oret_ref[...].astype(F32)
        sgm = _sigmoid(gm_ref[...].astype(F32))
        sgr = _sigmoid(gr_ref[...].astype(F32))
        acc_out[...] += _dot_tn((sgm * o_mla + sgr * o_ret).astype(BF16), dmb)
        dgm_ref[...] = (dmerged * o_mla * sgm * (1.0 - sgm)).astype(BF16)
        dgr_ref[...] = (dmerged * o_ret * sgr * (1.0 - sgr)).astype(BF16)
        domla = (dmerged * sgm).astype(BF16)
        acc_bm[...] += _dot_tn(o_ref[...], domla)
        do = _dot_nt(domla, wbm_ref[...])
        do_ref[...] = do.astype(BF16)
        for hd in range(MLA_HEADS):
            sl = slice(hd * HP, (hd + 1) * HP)
            d = jnp.sum(do[:, sl] * o_ref[:, sl].astype(F32), axis=-1, keepdims=True)
            delta_ref[:, sl] = jnp.broadcast_to(d, (tT, HP))
        doret = (dmerged * sgr).astype(BF16)
        dgated = _dot_nt(doret, wbr_ref[...])
        rg = rg_ref[...].astype(F32)
        sg = _sigmoid(rg)
        srg = rg * sg
        ynv = yn_ref[...].astype(F32)
        yw = ynv * gnw_ref[...]
        acc_br[...] += _dot_tn((srg * yw).astype(BF16), doret)
        drg_ref[...] = (dgated * yw * (sg * (1.0 + rg * (1.0 - sg)))).astype(BF16)
        dgs = dgated * srg
        dyn_ref[...] = dgs * gnw_ref[...]
        ggn_ref[...] += jnp.sum(dgs * ynv, axis=0, keepdims=True)

        @pl.when(pl.program_id(0) == nT - 1)
        def _():
            dwout_ref[...] = acc_out[...].astype(BF16)
            dwbm_ref[...] = acc_bm[...].astype(BF16)
            dwbr_ref[...] = acc_br[...].astype(BF16)

    def full(r, c):
        return pl.BlockSpec((r, c), lambda i: (0, 0), pipeline_mode=pl.Buffered(1))

    def rows(c, j=0):
        return pl.BlockSpec((tT, c), lambda i: (i, j))

    return pl.pallas_call(
        body, name=name, grid=(nT,),
        in_specs=[rows(D), rows(D), full(1, D), rows(D), rows(D), rows(RW, 3), rows(D, g_blk), rows(D, g_blk + 1),
                  rows(RW), full(1, RW), rows(QW), full(D, D), full(QW, D), full(RW, D)],
        out_specs=[rows(D), rows(D), rows(QW), rows(QW), rows(RW), rows(RW), full(1, D), full(1, RW),
                   full(D, D), full(QW, D), full(RW, D)],
        out_shape=[jax.ShapeDtypeStruct((T, D), BF16)] * 2
        + [jax.ShapeDtypeStruct((T, QW), BF16), jax.ShapeDtypeStruct((T, QW), F32),
           jax.ShapeDtypeStruct((T, RW), BF16), jax.ShapeDtypeStruct((T, RW), F32),
           jax.ShapeDtypeStruct((1, D), F32), jax.ShapeDtypeStruct((1, RW), F32),
           jax.ShapeDtypeStruct((D, D), BF16), jax.ShapeDtypeStruct((QW, D), BF16), jax.ShapeDtypeStruct((RW, D), BF16)],
        scratch_shapes=[pltpu.VMEM((D, D), F32), pltpu.VMEM((QW, D), F32), pltpu.VMEM((RW, D), F32)],
        compiler_params=_params(("arbitrary",)),
    )(dho, m, post_w, omla, oret, proj, proj, proj, yn, gn_w, o, w_out, w_bm, w_br)


def _mesh_pos():
    return lax.axis_index("x"), lax.axis_index("y"), lax.axis_index("c")


class _Gather:
    def __init__(self, shards):
        self.operands = list(shards)
        self.n = len(shards)
        self.out_shape = [jax.ShapeDtypeStruct((N_DEV,) + s.shape, s.dtype) for s in shards]
        self.scratch = [pltpu.SemaphoreType.DMA((7 * self.n,)), pltpu.SemaphoreType.DMA((7 * self.n,)),
                        pltpu.SemaphoreType.DMA((self.n,))]

    def phase(self, p, x_refs, out_refs, sems):
        send_sems, recv_sems, local_sems = sems
        x, y, c = _mesh_pos()
        me, sibling = (x, y, c), (x, y, 1 - c)
        chips = [(1 - x, y), (x, 1 - y), (1 - x, 1 - y)]

        def copy(w, k, block, to, src=None):
            slot = out_refs[w].at[4 * block[0] + 2 * block[1] + block[2]]
            return pltpu.make_async_remote_copy(
                src_ref=slot if src is None else src, dst_ref=slot,
                send_sem=send_sems.at[7 * w + k], recv_sem=recv_sems.at[7 * w + k],
                device_id=to, device_id_type=pl.DeviceIdType.MESH)

        for w in range(self.n):
            mine = pltpu.make_async_copy(x_refs[w], out_refs[w].at[4 * x + 2 * y + c], local_sems.at[w])
            first = [copy(w, 0, me, sibling, src=x_refs[w])]
            first += [copy(w, 1 + j, me, (*chip, c), src=x_refs[w]) for j, chip in enumerate(chips)]
            passed = [copy(w, 4 + j, (*chip, c), sibling) for j, chip in enumerate(chips)]
            if p == 0:
                mine.start()
                for cp in first:
                    cp.start()
            elif p == 1:
                for j, chip in enumerate(chips):
                    copy(w, 1 + j, (*chip, c), me).wait_recv()
                    passed[j].start()
            else:
                copy(w, 0, sibling, me).wait_recv()
                for j, chip in enumerate(chips):
                    copy(w, 4 + j, (*chip, 1 - c), me).wait_recv()
                for cp in first + passed:
                    cp.wait_send()
                mine.wait()


class _Scatter:
    def __init__(self, grads, whole=()):
        self.n_sliced = len(grads)
        self.operands = list(grads) + list(whole)
        self.n = len(self.operands)
        self.out_shape = [jax.ShapeDtypeStruct(g.shape, g.dtype) for g in grads]
        self.out_shape += [jax.ShapeDtypeStruct((N_DEV,) + a.shape, a.dtype) for a in whole]
        n_sem = (N_DEV - 1) * self.n
        self.scratch = [pltpu.SemaphoreType.DMA((n_sem,)), pltpu.SemaphoreType.DMA((n_sem,)),
                        pltpu.SemaphoreType.DMA((self.n,))]

    def phase(self, p, in_refs, out_refs, sems):
        if p == 1:
            return
        send_sems, recv_sems, local_sems = sems
        x, y, c = _mesh_pos()
        me = 4 * x + 2 * y + c

        def src(w, dev):
            return in_refs[w].at[dev] if w < self.n_sliced else in_refs[w]

        for w in range(self.n):
            own = None if local_sems is None else pltpu.make_async_copy(src(w, me), out_refs[w].at[me], local_sems.at[w])
            sends, recvs = [], []
            for r in range(1, N_DEV):
                px = 1 - x if r & 4 else x
                py = 1 - y if r & 2 else y
                pc = 1 - c if r & 1 else c
                peer, pidx = (px, py, pc), 4 * px + 2 * py + pc
                k = (N_DEV - 1) * w + r - 1
                sends.append(pltpu.make_async_remote_copy(
                    src_ref=src(w, pidx), dst_ref=out_refs[w].at[me], send_sem=send_sems.at[k],
                    recv_sem=recv_sems.at[k], device_id=peer, device_id_type=pl.DeviceIdType.MESH))
                recvs.append(pltpu.make_async_remote_copy(
                    src_ref=src(w, me), dst_ref=out_refs[w].at[pidx], send_sem=send_sems.at[k],
                    recv_sem=recv_sems.at[k], device_id=peer, device_id_type=pl.DeviceIdType.MESH))
            if p == 0:
                if own is not None:
                    own.start()
                for cp in sends:
                    cp.start()
            else:
                for cp in recvs:
                    cp.wait_recv()
                for cp in sends:
                    cp.wait_send()
                if own is not None:
                    own.wait()


class _SplitScatter:
    def __init__(self, ex, name):
        self.ex, self.name = ex, name

    def _specs(self):
        ex = self.ex
        hbm = pl.BlockSpec(memory_space=pltpu.HBM)
        sem = pl.BlockSpec(memory_space=pltpu.SEMAPHORE)
        effect = pltpu.CompilerParams(has_side_effects=pltpu.SideEffectType.DATAFLOW_SIDE_EFFECTING)
        buffers = [pltpu.HBM(a.shape, a.dtype) for a in ex.operands] + [pltpu.HBM(s.shape, s.dtype) for s in ex.out_shape]
        return hbm, sem, effect, buffers

    def start(self):
        ex, n = self.ex, self.ex.n
        n_sem = (N_DEV - 1) * n
        hbm, sem, effect, buffers = self._specs()
        in_hbm = lambda a: pltpu.with_memory_space_constraint(a, pltpu.HBM)

        me = 4 * lax.axis_index("x") + 2 * lax.axis_index("y") + lax.axis_index("c")
        lands = []
        for w, (a, s) in enumerate(zip(ex.operands, ex.out_shape)):
            mine = lax.dynamic_index_in_dim(a, me, 0, keepdims=True) if w < ex.n_sliced else a[None]
            lands.append(lax.dynamic_update_slice_in_dim(lax.empty(s.shape, s.dtype), mine, me, 0))

        def start_body(*refs):
            ex.phase(0, refs[:n], refs[n:2 * n], (refs[2 * n], refs[2 * n + 1], None))
            refs[-1][...] = jnp.zeros_like(refs[-1])

        self.started = pl.pallas_call(
            start_body, name=self.name + "_start",
            out_shape=[pltpu.SemaphoreType.DMA((n_sem,)), pltpu.SemaphoreType.DMA((n_sem,))] + buffers
            + [jax.ShapeDtypeStruct((8, LANES), F32)],
            in_specs=[hbm] * (2 * n), out_specs=[sem, sem] + [hbm] * (2 * n) + [pl.BlockSpec(memory_space=pltpu.VMEM)],
            input_output_aliases={i: 2 + i for i in range(2 * n)}, compiler_params=effect,
        )(*[in_hbm(a) for a in ex.operands], *[in_hbm(a) for a in lands])
        return self.started[-1]

    def wait(self, after):
        ex, n = self.ex, self.ex.n
        hbm, sem, effect, buffers = self._specs()
        anyspec = pl.BlockSpec(memory_space=pl.ANY)

        def wait_body(*refs):
            ex.phase(2, refs[:n], refs[n:2 * n], (refs[2 * n], refs[2 * n + 1], None))

        done = pl.pallas_call(
            wait_body, name=self.name + "_wait", out_shape=buffers,
            in_specs=[hbm] * (2 * n) + [sem, sem] + [anyspec] * len(after), out_specs=[hbm] * (2 * n),
            input_output_aliases={i: i for i in range(2 * n)}, compiler_params=effect,
        )(*self.started[2:2 + 2 * n], self.started[0], self.started[1], *after)
        return done[n:]


def _exchange_alone(ex, *, name):
    n = ex.n

    def body(*refs):
        for p in range(3):
            ex.phase(p, refs[:n], refs[n:2 * n], refs[2 * n:])

    anyspec = pl.BlockSpec(memory_space=pl.ANY)
    return pl.pallas_call(body, name=name, out_shape=ex.out_shape, in_specs=[anyspec] * n,
                          out_specs=[anyspec] * n, scratch_shapes=ex.scratch)(*ex.operands)


def _adam_step(w_ref, p_ref, m_ref, v_ref, g_ref, d_ref, nm_ref, nv_ref):
    g = p_ref[0].astype(F32)
    for j in range(1, N_DEV):
        g = g + p_ref[j].astype(F32)
    g_ref[...] = g
    nm = ADAM_B1 * m_ref[...] + (1.0 - ADAM_B1) * g
    nv = ADAM_B2 * v_ref[...] + (1.0 - ADAM_B2) * (g * g)
    nm_ref[...] = nm
    nv_ref[...] = nv
    m_hat = nm / (1.0 - ADAM_B1 ** ADAM_STEP)
    v_hat = nv / (1.0 - ADAM_B2 ** ADAM_STEP)
    d_ref[...] = -ADAM_LR * (m_hat / (jnp.sqrt(v_hat) + ADAM_EPS) + ADAM_WD * w_ref[...])


def _adamw_vectors(ws, parts, ms, vs, *, name):
    n = len(ws)

    def body(*refs):
        w_refs, p_refs, m_refs, v_refs = (refs[i * n:(i + 1) * n] for i in range(4))
        outs = refs[4 * n:]
        for i in range(n):
            _adam_step(w_refs[i], p_refs[i], m_refs[i], v_refs[i], *outs[4 * i:4 * i + 4])

    return pl.pallas_call(
        body, name=name,
        out_shape=[jax.ShapeDtypeStruct(w.shape, F32) for w in ws for _ in range(4)],
    )(*ws, *parts, *ms, *vs)


def _adamw(w, parts, m, v, after, *, name):
    G, R, n = w.shape
    tn = 512 if (n > 512 and n % 512 == 0) else n
    tr = R
    for t in range(16, R, 16):
        if R % t == 0 and t * tn <= ADAM_BLOCK_CAP:
            tr = t
    if R * tn <= ADAM_BLOCK_CAP:
        tr = R

    def body(w_ref, p_ref, m_ref, v_ref, after_ref, g_ref, d_ref, nm_ref, nv_ref):
        _adam_step(w_ref, p_ref, m_ref, v_ref, g_ref, d_ref, nm_ref, nv_ref)

    blk = pl.BlockSpec((None, tr, tn), lambda g, i, j: (g, i, j))
    return pl.pallas_call(
        body, name=name, grid=(G, R // tr, n // tn),
        in_specs=[blk, pl.BlockSpec((N_DEV, None, tr, tn), lambda g, i, j: (0, g, i, j)), blk, blk,
                  pl.BlockSpec((8, LANES), lambda g, i, j: (0, 0))],
        out_specs=[blk, blk, blk, blk],
        out_shape=[jax.ShapeDtypeStruct((G, R, n), F32)] * 4,
        compiler_params=_params(("parallel", "parallel", "parallel")),
    )(w, parts, m, v, after)


def _pad_last(a, width):
    return jnp.pad(a, [(0, 0)] * (a.ndim - 1) + [(0, width - a.shape[-1])])


def _cols_of(g):
    return g.transpose(1, 0, 2).reshape(g.shape[1], N_DEV * g.shape[2])


def _col_shards(w):
    return w.reshape(w.shape[0], N_DEV, w.shape[1] // N_DEV).transpose(1, 0, 2)


def kernel(x, positions, ffn1_pre_w, ffn1_w1, ffn1_w2, ffn1_post_w, mix_pre_w, w_in, mla_q_norm_w, mla_w_uq, mla_kv_norm_w, mla_w_ukv, ret_gn_w, w_branch_mla, w_branch_ret, w_out, mix_post_w, ffn2_pre_w, ffn2_w1, ffn2_w2, ffn2_post_w, loss_target, m_ffn1_pre_w, m_ffn1_w1, m_ffn1_w2, m_ffn1_post_w, m_mix_pre_w, m_w_in, m_mla_q_norm_w, m_mla_w_uq, m_mla_kv_norm_w, m_mla_w_ukv, m_ret_gn_w, m_w_branch_mla, m_w_branch_ret, m_w_out, m_mix_post_w, m_ffn2_pre_w, m_ffn2_w1, m_ffn2_w2, m_ffn2_post_w, v_ffn1_pre_w, v_ffn1_w1, v_ffn1_w2, v_ffn1_post_w, v_mix_pre_w, v_w_in, v_mla_q_norm_w, v_mla_w_uq, v_mla_kv_norm_w, v_mla_w_ukv, v_ret_gn_w, v_w_branch_mla, v_w_branch_ret, v_w_out, v_mix_post_w, v_ffn2_pre_w, v_ffn2_w1, v_ffn2_w2, v_ffn2_post_w):
    T, D = x.shape[1], x.shape[2]
    h0 = x[0]
    tgt = loss_target[0]
    pos = positions.reshape(T, 1).astype(F32)

    big = [("ffn1_w1", ffn1_w1, m_ffn1_w1, v_ffn1_w1), ("ffn1_w2", ffn1_w2, m_ffn1_w2, v_ffn1_w2),
           ("w_in", w_in, m_w_in, v_w_in), ("mla_w_uq", mla_w_uq, m_mla_w_uq, v_mla_w_uq),
           ("mla_w_ukv", mla_w_ukv, m_mla_w_ukv, v_mla_w_ukv),
           ("w_branch_mla", w_branch_mla, m_w_branch_mla, v_w_branch_mla),
           ("w_branch_ret", w_branch_ret, m_w_branch_ret, v_w_branch_ret),
           ("w_out", w_out, m_w_out, v_w_out),
           ("ffn2_w1", ffn2_w1, m_ffn2_w1, v_ffn2_w1), ("ffn2_w2", ffn2_w2, m_ffn2_w2, v_ffn2_w2)]
    small = [("ffn1_pre_w", ffn1_pre_w, m_ffn1_pre_w, v_ffn1_pre_w), ("ffn1_post_w", ffn1_post_w, m_ffn1_post_w, v_ffn1_post_w),
             ("mix_pre_w", mix_pre_w, m_mix_pre_w, v_mix_pre_w), ("mla_q_norm_w", mla_q_norm_w, m_mla_q_norm_w, v_mla_q_norm_w),
             ("mla_kv_norm_w", mla_kv_norm_w, m_mla_kv_norm_w, v_mla_kv_norm_w), ("ret_gn_w", ret_gn_w, m_ret_gn_w, v_ret_gn_w),
             ("mix_post_w", mix_post_w, m_mix_post_w, v_mix_post_w), ("ffn2_pre_w", ffn2_pre_w, m_ffn2_pre_w, v_ffn2_pre_w),
             ("ffn2_post_w", ffn2_post_w, m_ffn2_post_w, v_ffn2_post_w)]

    half = ffn1_w2.shape[1]
    hp = -(-half // LANES) * LANES

    def rows_view(w):
        return w[0].T

    def send_w1(w):
        return jnp.pad(rows_view(w).reshape(2, half, D), ((0, 0), (0, hp - half), (0, 0))).reshape(2 * hp, D).astype(BF16)

    def send_w2(w):
        return jnp.pad(w[0], ((0, hp - half), (0, 0))).astype(BF16)

    mixer = ["w_in", "mla_w_uq", "mla_w_ukv", "w_branch_mla", "w_branch_ret", "w_out"]
    uq_w = MLA_NOPE + MLA_ROPE
    mixer_send = [rows_view(w_in).astype(BF16), jnp.pad(rows_view(mla_w_uq), ((0, HP - uq_w), (0, 0))).astype(BF16),
                  mla_w_ukv[0].astype(BF16), w_branch_mla[0].astype(BF16), w_branch_ret[0].astype(BF16),
                  w_out[0].astype(BF16)]

    w1a, w2a = _exchange_alone(_Gather([send_w1(ffn1_w1), send_w2(ffn1_w2)]), name="gather_ffn1")
    w2a = w2a.reshape(N_DEV // 2, 2 * hp, D)
    u1, f1, h1, a0, *got = _ffn_fwd(h0, ffn1_pre_w, w1a, w2a, ffn1_post_w, None, name="ffn1_fwd_gather_mixer",
                                exchange=_Gather(mixer_send))
    fw = dict(zip(mixer, got))

    wi = fw["w_in"].reshape(-1, D)
    cq_w, ckv_w, kr_w = wi[0:384], wi[384:640], wi[640:672]
    rq_w, rk_w = wi[672:928], wi[928:1184]
    rv_w, rg_w = wi[1184:1696], wi[1696:2208]
    gm_w, gr_w = wi[2208:2208 + D], wi[2208 + D:2208 + 2 * D]
    zer = lambda n: jnp.zeros((n, D), BF16)
    head_rows = lambda a, h: jnp.pad(a.reshape(h, -1, D), ((0, 0), (0, HP - a.shape[0] // h), (0, 0))).reshape(h * HP, D)
    w_in_p = jnp.concatenate([head_rows(rq_w, RET_HEADS), head_rows(rk_w, RET_HEADS), rv_w, rg_w,
                              cq_w, ckv_w, zer(MLA_NOPE), kr_w, zer(HP - MLA_NOPE - MLA_ROPE), zer(AW - 768),
                              gm_w, gr_w], axis=0)
    w_uq_p = fw["mla_w_uq"].reshape(QW, MLA_Q_RANK)
    ukv = fw["mla_w_ukv"].transpose(1, 0, 2)
    w_kv_p = jnp.concatenate([_pad_last(ukv[:, :, :MLA_NOPE], HP).reshape(MLA_KV_RANK, QW),
                              _pad_last(ukv[:, :, MLA_NOPE:], HP).reshape(MLA_KV_RANK, QW)], axis=1)
    w_bm_p = jnp.pad(_cols_of(fw["w_branch_mla"]).reshape(MLA_HEADS, MLA_V, D),
                     ((0, 0), (0, HP - MLA_V), (0, 0))).reshape(QW, D)
    w_br, w_o = _cols_of(fw["w_branch_ret"]), fw["w_out"].reshape(D, D)
    tab_mla = _rope_table(MLA_NOPE, MLA_ROPE // 2)
    tab_ret = _rope_table(0, RET_DK // 2)

    proj, a1 = _rms_matmul(h1, mix_pre_w, w_in_p, name="mixer_in_proj")
    q, k, v = _mla_prep_fwd(proj, pos, mla_q_norm_w, mla_kv_norm_w, w_uq_p, w_kv_p, tab_mla, name="mla_prep_fwd")
    o, lse, w1b, w2b = _flash_fwd(q, k, v, name="mla_attn_fwd_gather_ffn2",
                                  exchange=_Gather([send_w1(ffn2_w1), send_w2(ffn2_w2)]))
    w2b = w2b.reshape(N_DEV // 2, 2 * hp, D)
    ypre, yn, rprev = _ret_fwd(proj, pos, tab_ret, name="retention_fwd")
    omla, oret, m, h2 = _merge_fwd(o, yn, proj, ret_gn_w, w_bm_p, w_br, w_o, h1, mix_post_w, name="merge_fwd")
    u2, f2, _, a2, dy, lossp = _ffn_fwd(h2, ffn2_pre_w, w1b, w2b, ffn2_post_w, tgt, name="ffn2_fwd_loss")

    def grad(x, dy, tag, after=None):
        return _matmul_tn(x if x.ndim == 3 else x[None], dy if dy.ndim == 3 else dy[None], name=tag, after=after)

    g2, du2, df2, dh2, gpost2, gpre2 = _ffn_bwd(dy, f2, ffn2_post_w, h2, ffn2_pre_w, u2, w2b, w1b, name="ffn2_bwd")
    dw1b, = grad(du2.reshape(N_DEV, T, 2 * hp), a2, "ffn2_dw1")
    dw2b = grad(g2, df2, "ffn2_dw2")[0].reshape(N_DEV, hp, D)
    (dgm, dgr, do, delta, drg, dyn, gpostm, ggn, dw_out, dw_bm_p, dw_br) = _merge_bwd(
        dh2, m, mix_post_w, omla, oret, proj, yn, ret_gn_w, o, w_o, w_bm_p, w_br, name="merge_bwd")
    sc_ffn2 = _SplitScatter(_Scatter([dw1b, dw2b]), "scatter_ffn2")
    dq, dk, dv = _flash_bwd(q, k, v, do, lse, delta, name="mla_attn_bwd", after=sc_ffn2.start())
    da, gqn, gkvn, dw_uq_p, dw_kv_p = _mla_prep_bwd(dq, dk, dv, proj, pos, mla_q_norm_w, mla_kv_norm_w, w_uq_p, w_kv_p, tab_mla, name="mla_prep_bwd")
    drq, drk, drv = _ret_bwd(dyn, ypre, proj, pos, tab_ret, rprev, name="retention_bwd")
    dproj = jnp.concatenate([drq, drk, drv, drg, da, dgm, dgr], axis=1)
    dw_in_p = grad(dproj, a1, "dw_in")[0][0]

    dw_uq = dw_uq_p.reshape(MLA_HEADS, HP, MLA_Q_RANK)[:, :uq_w]
    dkp = dw_kv_p[:, :QW].reshape(MLA_KV_RANK, MLA_HEADS, HP)[:, :, :MLA_NOPE]
    dvp = dw_kv_p[:, QW:].reshape(MLA_KV_RANK, MLA_HEADS, HP)[:, :, :MLA_V]
    dw_ukv = jnp.concatenate([dkp, dvp], axis=2).transpose(1, 0, 2)
    dw_bm = dw_bm_p.reshape(MLA_HEADS, HP, D)[:, :MLA_V].reshape(MLA_HEADS * MLA_V, D)
    small_mixer_grads = [dw_uq, dw_ukv, _col_shards(dw_bm), _col_shards(dw_br), dw_out.reshape(N_DEV, D // N_DEV, D)]
    sc_small = _SplitScatter(_Scatter(small_mixer_grads), "scatter_mixer_small")
    dh1, gmixpre = _proj_bwd(dproj, w_in_p, h1, mix_pre_w, dh2, name="mixer_in_bwd", after=sc_small.start())
    unhead = lambda a, h, wd: a.reshape(h, HP, D)[:, :wd].reshape(h * wd, D)
    c0 = 4 * RW
    dw_in = jnp.concatenate([
        dw_in_p[c0:c0 + 384], dw_in_p[c0 + 384:c0 + 640], dw_in_p[c0 + 640 + MLA_NOPE:c0 + 640 + MLA_NOPE + MLA_ROPE],
        unhead(dw_in_p[0:RW], RET_HEADS, RET_DK), unhead(dw_in_p[RW:2 * RW], RET_HEADS, RET_DK),
        dw_in_p[2 * RW:3 * RW], dw_in_p[3 * RW:4 * RW],
        dw_in_p[PROJ_FIXED:PROJ_FIXED + D], dw_in_p[PROJ_FIXED + D:PROJ_FIXED + 2 * D]], axis=0).reshape(N_DEV, -1, D)
    sc_w_in = _SplitScatter(_Scatter([dw_in]), "scatter_w_in")
    g1, du1, df1, dx, gpost1, gpre1 = _ffn_bwd(
        dh1, f1, ffn1_post_w, h0, ffn1_pre_w, u1, w2a, w1a, name="ffn1_bwd", after=sc_w_in.start())
    dw2a = grad(g1, df1, "ffn1_dw2")[0].reshape(N_DEV, hp, D)
    sc_dw2a = _SplitScatter(_Scatter([dw2a]), "scatter_ffn1_dw2")
    dw1a, = grad(du1.reshape(N_DEV, T, 2 * hp), a0, "ffn1_dw1", after=sc_dw2a.start())

    small_g = {"ffn1_pre_w": gpre1, "ffn1_post_w": gpost1, "mix_pre_w": gmixpre, "mla_q_norm_w": gqn,
               "mla_kv_norm_w": gkvn, "ret_gn_w": ggn, "mix_post_w": gpostm, "ffn2_pre_w": gpre2, "ffn2_post_w": gpost2}
    sc_last = _SplitScatter(_Scatter([dw1a], whole=[small_g[nm] for nm, *_ in small] + [lossp]), "scatter_ffn1_dw1")
    token = sc_last.start()
    recv_ffn2 = sc_ffn2.wait([token])
    recv_mixer = sc_w_in.wait([token]) + sc_small.wait([token])
    recv_w2a, = sc_dw2a.wait([token])
    parts = dict(zip(mixer, recv_mixer))
    parts.update(ffn1_w2=recv_w2a, ffn2_w1=recv_ffn2[0], ffn2_w2=recv_ffn2[1])
    as_is = (lambda a: a, lambda p: p[:, None], lambda a: a)
    views = {nm: as_is for nm, *_ in big}
    for nm in ("ffn1_w1", "ffn2_w1"):
        views[nm] = (lambda a: rows_view(a).reshape(2, half, D), lambda p: p.reshape(N_DEV, 2, hp, D),
                     lambda a: a.reshape(2 * half, D).T[None])
    for nm in ("w_in", "mla_w_uq"):
        views[nm] = (lambda a: rows_view(a)[None], lambda p: p[:, None], lambda a: a[0].T[None])

    def update(nm, w, m_, v_, after):
        to_view, parts_view, back = views[nm]
        return [back(a) for a in _adamw(to_view(w), parts_view(parts[nm]), to_view(m_), to_view(v_), after,
                                        name="adamw_" + nm)]

    big_out = {nm: update(nm, w, m_, v_, token) for nm, w, m_, v_ in big if nm != "ffn1_w1"}
    recv_w1a, *small_parts, loss_parts = sc_last.wait([d[0] for d in big_out.values()])
    loss = jnp.sum(loss_parts[:, ::8, 0])
    parts["ffn1_w1"] = recv_w1a
    big_out["ffn1_w1"] = update("ffn1_w1", ffn1_w1, m_ffn1_w1, v_ffn1_w1, jnp.zeros((8, LANES), F32))
    small_out = _adamw_vectors([w for _, w, _, _ in small], small_parts, [a for _, _, a, _ in small],
                               [a for _, _, _, a in small], name="adamw_replicated")

    order = ["ffn1_pre_w", "ffn1_w1", "ffn1_w2", "ffn1_post_w", "mix_pre_w", "w_in", "mla_q_norm_w", "mla_w_uq",
             "mla_kv_norm_w", "mla_w_ukv", "ret_gn_w", "w_branch_mla", "w_branch_ret", "w_out", "mix_post_w",
             "ffn2_pre_w", "ffn2_w1", "ffn2_w2", "ffn2_post_w"]
    outs = [loss, dx[None]]
    for i in range(4):
        both = {nm: big_out[nm][i] for nm in big_out}
        both.update({nm: small_out[4 * j + i] for j, (nm, *_) in enumerate(small)})
        outs += [both[nm] for nm in order]
    return tuple(outs)
```

```python
import math

import numpy as np
import jax
import jax.numpy as jnp
from jax import lax
from jax.experimental import pallas as pl
from jax.experimental.pallas import tpu as pltpu

F32, BF16 = jnp.float32, jnp.bfloat16

MLA_HEADS, MLA_NOPE, MLA_ROPE, MLA_V = 8, 64, 32, 64
MLA_Q_RANK, MLA_KV_RANK = 384, 256
RET_HEADS, RET_DK, RET_DV = 4, 64, 128
ROPE_BASE, NORM_EPS, GN_EPS = 10000.0, 1e-6, 1e-6
ADAM_LR, ADAM_B1, ADAM_B2, ADAM_EPS, ADAM_WD, ADAM_STEP = 0.001, 0.9, 0.999, 1e-08, 0.01, 10
ATTN_SCALE = 1.0 / math.sqrt(MLA_NOPE + MLA_ROPE)

N_DEV = 8
LANES = 128
HP = LANES
QW = MLA_HEADS * HP
RW = RET_HEADS * HP
AW = 1024
PROJ_FIXED = 4 * RW + AW
NEG = -1e30

TOKEN_TILE = 512
ATTN_TILE = 1024
ATTN_CHAINS = 2
FFN_CHAINS = 2
RET_TILE = 256
PROJ_TILE_CAP = 2560
PROJ_TOKEN_TILE = 1024
GRAD_TILE_CAP = 1408
GRAD_TOKEN_TILE = 4096
ADAM_BLOCK_CAP = 192 * 1024
MERGE_TILE = 256
VMEM_LIMIT = 56 * 1024 * 1024


def _tile(n, cap, mult=LANES):
    if n <= cap:
        return n
    best = None
    for t in range(mult, cap + 1, mult):
        if n % t == 0:
            best = t
    assert best is not None, (n, cap, mult)
    return best


def _params(sem):
    return pltpu.CompilerParams(dimension_semantics=sem, vmem_limit_bytes=VMEM_LIMIT)


def _dot(a, b):
    return lax.dot_general(a, b, (((1,), (0,)), ((), ())), preferred_element_type=F32)


def _dot_nt(a, b):
    return lax.dot_general(a, b, (((1,), (1,)), ((), ())), preferred_element_type=F32)


def _dot_tn(a, b):
    return lax.dot_general(a, b, (((0,), (0,)), ((), ())), preferred_element_type=F32)


def _sigmoid(x):
    return pl.reciprocal(1.0 + jnp.exp(-x), approx=True)


def _rms_fwd(x, w):
    r = lax.rsqrt(jnp.mean(x * x, axis=-1, keepdims=True) + NORM_EPS)
    return x * r * w


def _rms_bwd(x, w, dy):
    r = lax.rsqrt(jnp.mean(x * x, axis=-1, keepdims=True) + NORM_EPS)
    xh = x * r
    g = dy * w
    dx = r * (g - xh * jnp.mean(g * xh, axis=-1, keepdims=True))
    return dx, jnp.sum(dy * xh, axis=0, keepdims=True)


def _rope_table(first, half):
    inv = (np.float32(ROPE_BASE) ** (-(np.arange(half, dtype=np.float32) / np.float32(half)))).astype(np.float32)
    tab = np.zeros((8, LANES), np.float32)
    tab[0, first:first + half] = inv
    tab[0, first + half:first + 2 * half] = inv
    tab[1, first:first + half] = -1.0
    tab[2, first + half:first + 2 * half] = 1.0
    return jnp.asarray(tab)


def _rope_cs(pos, tab_ref):
    ang = pos * tab_ref[0:1, :]
    s = jnp.sin(ang)
    return jnp.cos(ang), s * tab_ref[1:2, :], s * tab_ref[2:3, :]


def _rope(x, cs, half, inverse=False):
    c, s1, s2 = cs
    a = pltpu.roll(x, LANES - half, 1) * s1 + pltpu.roll(x, half, 1) * s2
    return x * c - a if inverse else x * c + a


def _call(body, *, name, grid, in_specs, out_specs, out_shape, scratch_shapes, args, exchange=None, after=None):
    sem = ("arbitrary",) * len(grid)
    anyspec = pl.BlockSpec(memory_space=pl.ANY)
    if exchange is None and after is not None:
        n_own = len(in_specs)

        def behind(*refs):
            body(*refs[:n_own], *refs[n_own + 1:])

        return pl.pallas_call(behind, name=name, grid=grid, in_specs=list(in_specs) + [anyspec], out_specs=out_specs,
                              out_shape=out_shape, scratch_shapes=scratch_shapes, compiler_params=_params(sem))(*args, after)
    if exchange is None:
        return pl.pallas_call(body, name=name, grid=grid, in_specs=in_specs, out_specs=out_specs,
                              out_shape=out_shape, scratch_shapes=scratch_shapes, compiler_params=_params(sem))(*args)
    n_in, n_out, e = len(in_specs), len(out_specs), exchange.n
    total = math.prod(grid)

    def carried(*refs):
        own = refs[:n_in] + refs[n_in + e:n_in + e + n_out] + refs[n_in + 2 * e + n_out:len(refs) - 3]
        ex_refs = (refs[n_in:n_in + e], refs[n_in + e + n_out:n_in + 2 * e + n_out], refs[len(refs) - 3:])
        step = pl.program_id(0)
        for d in range(1, len(grid)):
            step = step * grid[d] + pl.program_id(d)

        @pl.when(step == 0)
        def _():
            exchange.phase(0, *ex_refs)

        @pl.when(step == (3 * total) // 4)
        def _():
            exchange.phase(1, *ex_refs)

        body(*own)

        @pl.when(step == total - 1)
        def _():
            exchange.phase(2, *ex_refs)

    return pl.pallas_call(
        carried, name=name, grid=grid, in_specs=list(in_specs) + [anyspec] * e,
        out_specs=list(out_specs) + [anyspec] * e, out_shape=list(out_shape) + exchange.out_shape,
        scratch_shapes=list(scratch_shapes) + exchange.scratch, compiler_params=_params(sem),
    )(*args, *exchange.operands)


def _ffn_fwd(h, pre_w, w1, w2, post_w, target, *, name, exchange=None):
    T, D = h.shape
    nk, ck = w2.shape[0], w2.shape[1]
    tT = min(TOKEN_TILE, T)
    nT = T // tT
    with_loss = target is not None

    def body(*refs):
        if with_loss:
            (h_ref, pre_ref, w1g_ref, w1u_ref, w2_ref, post_ref, tgt_ref,
             u_ref, f_ref, ho_ref, a_s, dy_ref, loss_ref, acc) = refs
        else:
            (h_ref, pre_ref, w1g_ref, w1u_ref, w2_ref, post_ref,
             u_ref, f_ref, ho_ref, a_s, acc) = refs
        k = pl.program_id(1)

        @pl.when(k == 0)
        def _():
            a_s[...] = _rms_fwd(h_ref[...], pre_ref[...]).astype(BF16)
            acc[...] = jnp.zeros_like(acc)

        for c in range(FFN_CHAINS):
            rs = slice(c * (tT // FFN_CHAINS), (c + 1) * (tT // FFN_CHAINS))
            a = a_s[rs, :]
            ug = _dot_nt(a, w1g_ref[...])
            uu = _dot_nt(a, w1u_ref[...])
            u_ref[0, rs, :] = ug.astype(BF16)
            u_ref[1, rs, :] = uu.astype(BF16)
            acc[rs, :] += _dot((ug * _sigmoid(ug) * uu).astype(BF16), w2_ref[...])

        @pl.when(k == nk - 1)
        def _():
            f = acc[...]
            f_ref[...] = f
            ho = h_ref[...] + 0.5 * _rms_fwd(f, post_ref[...])
            ho_ref[...] = ho
            if with_loss:
                e = ho - tgt_ref[...]
                dy_ref[...] = e * (1.0 / D)
                loss_ref[...] = jnp.full(loss_ref.shape, (0.5 / D) * jnp.sum(e * e), F32)

    row = pl.BlockSpec((tT, D), lambda i, k: (i, 0))
    vec = pl.BlockSpec((1, D), lambda i, k: (0, 0))
    in_specs = [row, vec,
                pl.BlockSpec((None, ck, D), lambda i, k: (k, 0, 0)),
                pl.BlockSpec((None, ck, D), lambda i, k: (nk + k, 0, 0)),
                pl.BlockSpec((None, ck, D), lambda i, k: (k, 0, 0)),
                vec]
    out_shape = [jax.ShapeDtypeStruct((2, nk, T, ck), BF16),
                 jax.ShapeDtypeStruct((T, D), F32),
                 jax.ShapeDtypeStruct((T, D), F32),
                 jax.ShapeDtypeStruct((T, D), BF16)]
    out_specs = [pl.BlockSpec((2, None, tT, ck), lambda i, k: (0, k, i, 0)), row, row, row]
    args = [h, pre_w, w1, w1, w2, post_w]
    if with_loss:
        in_specs.append(row)
        args.append(target)
        out_shape += [jax.ShapeDtypeStruct((T, D), F32), jax.ShapeDtypeStruct((nT * 8, LANES), F32)]
        out_specs += [row, pl.BlockSpec((8, LANES), lambda i, k: (i, 0))]
    return _call(body, name=name, grid=(nT, nk), in_specs=in_specs, out_specs=out_specs, out_shape=out_shape,
                 scratch_shapes=[pltpu.VMEM((tT, D), F32)], args=args, exchange=exchange)


def _ffn_bwd(dho, f, post_w, h, pre_w, u, w2, w1, *, name, exchange=None, after=None):
    T, D = h.shape
    nk, ck = w2.shape[0], w2.shape[1]
    tT = min(TOKEN_TILE, T)
    nT = T // tT

    def body(dho_ref, f_ref, post_ref, h_ref, pre_ref, u_ref, w2_ref, w1g_ref, w1u_ref,
             g_ref, du_ref, df_s, dh_ref, gpost_ref, gpre_ref, da_acc):
        i, k = pl.program_id(0), pl.program_id(1)

        @pl.when(jnp.logical_and(i == 0, k == 0))
        def _():
            gpost_ref[...] = jnp.zeros_like(gpost_ref)
            gpre_ref[...] = jnp.zeros_like(gpre_ref)

        @pl.when(k == 0)
        def _():
            dx, dw = _rms_bwd(f_ref[...], post_ref[...], 0.5 * dho_ref[...])
            df_s[...] = dx.astype(BF16)
            gpost_ref[...] += dw
            da_acc[...] = jnp.zeros_like(da_acc)

        groups = [slice(c * (tT // FFN_CHAINS), (c + 1) * (tT // FFN_CHAINS)) for c in range(FFN_CHAINS)]
        dgs = [_dot_nt(df_s[rs, :], w2_ref[...]) for rs in groups]
        for rs, dg in zip(groups, dgs):
            ug = u_ref[0, rs, :].astype(F32)
            uu = u_ref[1, rs, :].astype(F32)
            sg = _sigmoid(ug)
            sl = ug * sg
            g_ref[rs, :] = (sl * uu).astype(BF16)
            dug = (dg * uu * (sg + sl * (1.0 - sg))).astype(BF16)
            duu = (dg * sl).astype(BF16)
            du_ref[0, rs, :] = dug
            du_ref[1, rs, :] = duu
            da_acc[rs, :] += _dot(dug, w1g_ref[...]) + _dot(duu, w1u_ref[...])

        @pl.when(k == nk - 1)
        def _():
            dx, dw = _rms_bwd(h_ref[...], pre_ref[...], da_acc[...])
            dh_ref[...] = dho_ref[...] + dx
            gpre_ref[...] += dw

    row = pl.BlockSpec((tT, D), lambda i, k: (i, 0))
    vec = pl.BlockSpec((1, D), lambda i, k: (0, 0))
    return _call(
        body, name=name, grid=(nT, nk),
        in_specs=[row, row, vec, row, vec,
                  pl.BlockSpec((2, None, tT, ck), lambda i, k: (0, k, i, 0)),
                  pl.BlockSpec((None, ck, D), lambda i, k: (k, 0, 0)),
                  pl.BlockSpec((None, ck, D), lambda i, k: (k, 0, 0)),
                  pl.BlockSpec((None, ck, D), lambda i, k: (nk + k, 0, 0))],
        out_specs=[pl.BlockSpec((None, tT, ck), lambda i, k: (k, i, 0)),
                   pl.BlockSpec((2, None, tT, ck), lambda i, k: (0, k, i, 0)),
                   row, row, vec, vec],
        out_shape=[jax.ShapeDtypeStruct((nk, T, ck), BF16),
                   jax.ShapeDtypeStruct((2, nk, T, ck), BF16),
                   jax.ShapeDtypeStruct((T, D), BF16),
                   jax.ShapeDtypeStruct((T, D), F32),
                   jax.ShapeDtypeStruct((1, D), F32),
                   jax.ShapeDtypeStruct((1, D), F32)],
        scratch_shapes=[pltpu.VMEM((tT, D), F32)],
        args=(dho, f, post_w, h, pre_w, u, w2, w1, w1), exchange=exchange, after=after)


def _matmul_tn(x, dy, *, name, exchange=None, after=None):
    Px, T, K = x.shape
    Py, _, N = dy.shape
    P = max(Px, Py)
    tT, tK, tN = min(GRAD_TOKEN_TILE, T), _tile(K, GRAD_TILE_CAP), _tile(N, GRAD_TILE_CAP)
    nt = T // tT

    def body(x_ref, dy_ref, o_ref, acc):
        t = pl.program_id(3)

        @pl.when(t == 0)
        def _():
            acc[...] = jnp.zeros_like(acc)

        acc[...] += _dot_tn(x_ref[...], dy_ref[...])

        @pl.when(t == nt - 1)
        def _():
            o_ref[...] = acc[...].astype(BF16)

    return _call(
        body, name=name, grid=(P, K // tK, N // tN, nt),
        in_specs=[pl.BlockSpec((None, tT, tK), lambda p, a, b, t: (p if Px > 1 else 0, t, a)),
                  pl.BlockSpec((None, tT, tN), lambda p, a, b, t: (p if Py > 1 else 0, t, b))],
        out_specs=[pl.BlockSpec((None, tK, tN), lambda p, a, b, t: (p, a, b))],
        out_shape=[jax.ShapeDtypeStruct((P, K, N), BF16)],
        scratch_shapes=[pltpu.VMEM((tK, tN), F32)], args=(x, dy), exchange=exchange, after=after)


def _rms_matmul(h, wn, w, *, name):
    T, D = h.shape
    N = w.shape[0]
    tT, tN = min(PROJ_TOKEN_TILE, T), _tile(N, PROJ_TILE_CAP)

    def body(h_ref, wn_ref, w_ref, y_ref, a_ref):
        @pl.when(pl.program_id(1) == 0)
        def _():
            a_ref[...] = _rms_fwd(h_ref[...], wn_ref[...]).astype(BF16)

        y_ref[...] = _dot_nt(a_ref[...], w_ref[...]).astype(BF16)

    return pl.pallas_call(
        body, name=name, grid=(T // tT, N // tN),
        in_specs=[pl.BlockSpec((tT, D), lambda i, j: (i, 0)),
                  pl.BlockSpec((1, D), lambda i, j: (0, 0)),
                  pl.BlockSpec((tN, D), lambda i, j: (j, 0))],
        out_specs=[pl.BlockSpec((tT, tN), lambda i, j: (i, j)),
                   pl.BlockSpec((tT, D), lambda i, j: (i, 0))],
        out_shape=[jax.ShapeDtypeStruct((T, N), BF16), jax.ShapeDtypeStruct((T, D), BF16)],
        compiler_params=_params(("parallel", "arbitrary")),
    )(h, wn, w)


def _proj_bwd(dproj, w, h, wn, dres, *, name, exchange=None, after=None):
    T, D = h.shape
    N = w.shape[0]
    tT, tN = min(PROJ_TOKEN_TILE, T), _tile(N, PROJ_TILE_CAP)
    nn = N // tN

    def body(dp_ref, w_ref, h_ref, wn_ref, dres_ref, dh_ref, gw_ref, acc):
        i, j = pl.program_id(0), pl.program_id(1)

        @pl.when(jnp.logical_and(i == 0, j == 0))
        def _():
            gw_ref[...] = jnp.zeros_like(gw_ref)

        @pl.when(j == 0)
        def _():
            acc[...] = jnp.zeros_like(acc)

        acc[...] += _dot(dp_ref[...], w_ref[...])

        @pl.when(j == nn - 1)
        def _():
            dx, dw = _rms_bwd(h_ref[...], wn_ref[...], acc[...])
            dh_ref[...] = dres_ref[...] + dx
            gw_ref[...] += dw

    row = pl.BlockSpec((tT, D), lambda i, j: (i, 0))
    vec = pl.BlockSpec((1, D), lambda i, j: (0, 0))
    return _call(
        body, name=name, grid=(T // tT, nn),
        in_specs=[pl.BlockSpec((tT, tN), lambda i, j: (i, j)),
                  pl.BlockSpec((tN, D), lambda i, j: (j, 0)), row, vec, row],
        out_specs=[row, vec],
        out_shape=[jax.ShapeDtypeStruct((T, D), F32), jax.ShapeDtypeStruct((1, D), F32)],
        scratch_shapes=[pltpu.VMEM((tT, D), F32)], args=(dproj, w, h, wn, dres), exchange=exchange, after=after)


def _mla_prep_fwd(proj, pos, qn_w, kvn_w, w_uq, w_kv, tab, *, name):
    T = proj.shape[0]
    tT = min(PROJ_TOKEN_TILE, T)
    a_blk = PROJ_FIXED // AW - 1

    def body(a_ref, pos_ref, qnw_ref, kvnw_ref, wuq_ref, wkv_ref, tab_ref,
             q_ref, k_ref, v_ref):
        cq = a_ref[:, 0:MLA_Q_RANK].astype(F32)
        ckv = a_ref[:, MLA_Q_RANK:MLA_Q_RANK + MLA_KV_RANK].astype(F32)
        kr = a_ref[:, 640:768].astype(F32)
        qn = _rms_fwd(cq, qnw_ref[...]).astype(BF16)
        kvn = _rms_fwd(ckv, kvnw_ref[...]).astype(BF16)
        cs = _rope_cs(pos_ref[...], tab_ref)
        q = _dot_nt(qn, wuq_ref[...])
        kv = _dot(kvn, wkv_ref[...])
        krr = _rope(kr, cs, MLA_ROPE // 2)
        for hd in range(MLA_HEADS):
            sl = slice(hd * HP, (hd + 1) * HP)
            q_ref[:, sl] = (_rope(q[:, sl], cs, MLA_ROPE // 2) * ATTN_SCALE).astype(BF16)
            k_ref[:, sl] = (kv[:, sl] + krr).astype(BF16)
        v_ref[...] = kv[:, QW:].astype(BF16)

    def full(r, c):
        return pl.BlockSpec((r, c), lambda i: (0, 0))

    def rows(c):
        return pl.BlockSpec((tT, c), lambda i: (i, 0))

    return pl.pallas_call(
        body, name=name, grid=(T // tT,),
        in_specs=[pl.BlockSpec((tT, AW), lambda i: (i, a_blk)), rows(1),
                  full(1, MLA_Q_RANK), full(1, MLA_KV_RANK),
                  full(QW, MLA_Q_RANK), full(MLA_KV_RANK, 2 * QW), full(8, LANES)],
        out_specs=[rows(QW), rows(QW), rows(QW)],
        out_shape=[jax.ShapeDtypeStruct((T, QW), BF16)] * 3,
        compiler_params=_params(("parallel",)),
    )(proj, pos, qn_w, kvn_w, w_uq, w_kv, tab)


def _mla_prep_bwd(dq, dk, dv, proj, pos, qn_w, kvn_w, w_uq, w_kv, tab, *, name):
    T = proj.shape[0]
    tT = min(TOKEN_TILE, T)
    nT = T // tT
    a_blk = PROJ_FIXED // AW - 1

    def body(dq_ref, dk_ref, dv_ref, a_ref, pos_ref, qnw_ref, kvnw_ref, wuq_ref, wkv_ref, tab_ref,
             da_ref, gqn_ref, gkvn_ref, dwuq_ref, dwkv_ref, dql_ref, dkvl_ref, acc_uq, acc_kv):
        @pl.when(pl.program_id(0) == 0)
        def _():
            gqn_ref[...] = jnp.zeros_like(gqn_ref)
            gkvn_ref[...] = jnp.zeros_like(gkvn_ref)
            acc_uq[...] = jnp.zeros_like(acc_uq)
            acc_kv[...] = jnp.zeros_like(acc_kv)

        cs = _rope_cs(pos_ref[...], tab_ref)
        dkr = jnp.zeros((tT, HP), F32)
        for hd in range(MLA_HEADS):
            sl = slice(hd * HP, (hd + 1) * HP)
            dql_ref[:, sl] = (_rope(dq_ref[:, sl], cs, MLA_ROPE // 2, inverse=True) * ATTN_SCALE).astype(BF16)
            dkh = dk_ref[:, sl]
            dkr = dkr + dkh
            dkvl_ref[:, sl] = dkh.astype(BF16)
        dkvl_ref[:, QW:] = dv_ref[...]
        dqn = _dot(dql_ref[...], wuq_ref[...])
        dkvn = _dot_nt(dkvl_ref[...], wkv_ref[...])
        cq = a_ref[:, 0:MLA_Q_RANK].astype(F32)
        ckv = a_ref[:, MLA_Q_RANK:MLA_Q_RANK + MLA_KV_RANK].astype(F32)
        dcq, gq = _rms_bwd(cq, qnw_ref[...], dqn)
        dckv, gkv = _rms_bwd(ckv, kvnw_ref[...], dkvn)
        gqn_ref[...] += gq
        gkvn_ref[...] += gkv
        da_ref[:, 0:MLA_Q_RANK] = dcq.astype(BF16)
        da_ref[:, MLA_Q_RANK:MLA_Q_RANK + MLA_KV_RANK] = dckv.astype(BF16)
        da_ref[:, 640:768] = _rope(dkr, cs, MLA_ROPE // 2, inverse=True).astype(BF16)
        da_ref[:, 768:AW] = jnp.zeros((tT, AW - 768), BF16)
        acc_uq[...] += _dot_tn(dql_ref[...], _rms_fwd(cq, qnw_ref[...]).astype(BF16))
        acc_kv[...] += _dot_tn(_rms_fwd(ckv, kvnw_ref[...]).astype(BF16), dkvl_ref[...])

        @pl.when(pl.program_id(0) == nT - 1)
        def _():
            dwuq_ref[...] = acc_uq[...].astype(BF16)
            dwkv_ref[...] = acc_kv[...].astype(BF16)

    def full(r, c):
        return pl.BlockSpec((r, c), lambda i: (0, 0))

    def rows(c):
        return pl.BlockSpec((tT, c), lambda i: (i, 0))

    return pl.pallas_call(
        body, name=name, grid=(nT,),
        in_specs=[rows(QW), rows(QW), rows(QW), pl.BlockSpec((tT, AW), lambda i: (i, a_blk)), rows(1),
                  full(1, MLA_Q_RANK), full(1, MLA_KV_RANK),
                  full(QW, MLA_Q_RANK), full(MLA_KV_RANK, 2 * QW), full(8, LANES)],
        out_specs=[rows(AW), full(1, MLA_Q_RANK), full(1, MLA_KV_RANK),
                   full(QW, MLA_Q_RANK), full(MLA_KV_RANK, 2 * QW)],
        out_shape=[jax.ShapeDtypeStruct((T, AW), BF16),
                   jax.ShapeDtypeStruct((1, MLA_Q_RANK), F32), jax.ShapeDtypeStruct((1, MLA_KV_RANK), F32),
                   jax.ShapeDtypeStruct((QW, MLA_Q_RANK), BF16), jax.ShapeDtypeStruct((MLA_KV_RANK, 2 * QW), BF16)],
        scratch_shapes=[pltpu.VMEM((tT, QW), BF16), pltpu.VMEM((tT, 2 * QW), BF16),
                        pltpu.VMEM((QW, MLA_Q_RANK), F32), pltpu.VMEM((MLA_KV_RANK, 2 * QW), F32)],
        compiler_params=_params(("arbitrary",)),
    )(dq, dk, dv, proj, pos, qn_w, kvn_w, w_uq, w_kv, tab)


def _flash_fwd(q, k, v, *, name, exchange=None):
    T = q.shape[0]
    H = q.shape[1] // HP
    tq = min(ATTN_TILE, T)
    nq = T // tq

    sub = tq // ATTN_CHAINS

    def body(q_ref, k_ref, v_ref, o_ref, lse_ref):
        qi = pl.program_id(1)
        qs = [q_ref[c * sub:(c + 1) * sub, :] for c in range(ATTN_CHAINS)]

        def update(carry, off, masked):
            nks = [(c + 1) * sub if masked else tq for c in range(ATTN_CHAINS)]
            scores = [_dot_nt(qs[c], k_ref[pl.ds(off, nks[c]), :]) for c in range(ATTN_CHAINS)]
            out = []
            for c in range(ATTN_CHAINS):
                m_prev, l_prev, acc = carry[c]
                nk, s = nks[c], scores[c]
                vb = v_ref[pl.ds(off, nk), :]
                if masked:
                    rows = lax.broadcasted_iota(jnp.int32, (sub, nk), 0) + c * sub
                    s = jnp.where(rows >= lax.broadcasted_iota(jnp.int32, (sub, nk), 1), s, NEG)
                m_new = jnp.maximum(m_prev, jnp.max(s, axis=1, keepdims=True))
                alpha = jnp.exp(m_prev - m_new)
                p = jnp.exp(s - m_new)
                out.append((m_new, alpha * l_prev + jnp.sum(p, axis=1, keepdims=True),
                            alpha * acc + _dot(p.astype(BF16), vb)))
            return tuple(out)

        init = tuple((jnp.full((sub, 1), NEG, F32), jnp.zeros((sub, 1), F32), jnp.zeros((sub, HP), F32))
                     for _ in range(ATTN_CHAINS))
        carry = lax.fori_loop(0, qi, lambda j, cr: update(cr, pl.multiple_of(j * tq, tq), False), init)
        carry = update(carry, pl.multiple_of(qi * tq, tq), True)
        for c in range(ATTN_CHAINS):
            m_fin, l_fin, acc = carry[c]
            o_ref[c * sub:(c + 1) * sub, :] = (acc / l_fin).astype(BF16)
            lse_ref[c * sub:(c + 1) * sub, :] = jnp.broadcast_to(m_fin + jnp.log(l_fin), (sub, HP))

    qspec = pl.BlockSpec((tq, HP), lambda h, i: (i, h))
    kspec = pl.BlockSpec((T, HP), lambda h, i: (0, h))
    return _call(
        body, name=name, grid=(H, nq),
        in_specs=[qspec, kspec, kspec], out_specs=[qspec, qspec],
        out_shape=[jax.ShapeDtypeStruct((T, H * HP), BF16), jax.ShapeDtypeStruct((T, H * HP), F32)],
        scratch_shapes=[], args=(q, k, v), exchange=exchange)


def _flash_bwd(q, k, v, do, lse, delta, *, name, exchange=None, after=None):
    T = q.shape[0]
    H = q.shape[1] // HP
    tq = min(ATTN_TILE, T)
    nq = T // tq
    sub = tq // ATTN_CHAINS

    def body(k_ref, v_ref, q_ref, do_ref, lse_ref, dl_ref, dq_ref, dk_ref, dv_ref):
        ki = pl.program_id(1)

        @pl.when(ki == 0)
        def _():
            dq_ref[...] = jnp.zeros_like(dq_ref)

        def grow(a):
            return a if a.shape[0] == tq else jnp.concatenate([a, jnp.zeros((tq - a.shape[0], HP), F32)], axis=0)

        def step(carry, j, masked):
            dk_acc, dv_acc = carry
            nks = [(c + 1) * sub if masked else tq for c in range(ATTN_CHAINS)]
            rws = [pl.ds(pl.multiple_of(j * tq + c * sub, sub), sub) for c in range(ATTN_CHAINS)]
            scores = [_dot_nt(q_ref[rws[c], :], k_ref[0:nks[c], :]) for c in range(ATTN_CHAINS)]
            dps = [_dot_nt(do_ref[rws[c], :], v_ref[0:nks[c], :]) for c in range(ATTN_CHAINS)]
            for c in range(ATTN_CHAINS):
                rows, nk, s, dp = rws[c], nks[c], scores[c], dps[c]
                kb = k_ref[0:nk, :]
                qb = q_ref[rows, :]
                dob = do_ref[rows, :]
                if masked:
                    ri = lax.broadcasted_iota(jnp.int32, (sub, nk), 0) + c * sub
                    s = jnp.where(ri >= lax.broadcasted_iota(jnp.int32, (sub, nk), 1), s, NEG)
                p = jnp.exp(s - lse_ref[rows, 0:1])
                dv_acc = dv_acc + grow(_dot_tn(p.astype(BF16), dob))
                ds = (p * (dp - dl_ref[rows, 0:1])).astype(BF16)
                dk_acc = dk_acc + grow(_dot_tn(ds, qb))
                dq_ref[rows, :] += _dot(ds, kb)
            return dk_acc, dv_acc

        carry = step((jnp.zeros((tq, HP), F32), jnp.zeros((tq, HP), F32)), ki, True)
        dk_acc, dv_acc = lax.fori_loop(ki + 1, nq, lambda j, cr: step(cr, j, False), carry)
        dk_ref[...] = dk_acc
        dv_ref[...] = dv_acc.astype(BF16)

    kspec = pl.BlockSpec((tq, HP), lambda h, j: (j, h))
    full = pl.BlockSpec((T, HP), lambda h, j: (0, h))
    return _call(
        body, name=name, grid=(H, nq),
        in_specs=[kspec, kspec, full, full, full, full], out_specs=[full, kspec, kspec],
        out_shape=[jax.ShapeDtypeStruct((T, H * HP), F32), jax.ShapeDtypeStruct((T, H * HP), F32),
                   jax.ShapeDtypeStruct((T, H * HP), BF16)],
        scratch_shapes=[], args=(k, v, q, do, lse, delta), exchange=exchange, after=after)


def _ret_consts(cc, hd):
    lg = math.log(1.0 - 2.0 ** (-5.0 - hd))
    diff = (lax.broadcasted_iota(jnp.int32, (cc, cc), 0) - lax.broadcasted_iota(jnp.int32, (cc, cc), 1)).astype(F32)
    decay = jnp.where(diff >= 0, jnp.exp(jnp.maximum(diff, 0.0) * lg), 0.0)
    idx = lax.broadcasted_iota(jnp.int32, (cc, 1), 0).astype(F32)
    zeta = jnp.exp((cc - 1.0 - idx) * lg)
    xi = jnp.exp((idx + 1.0) * lg)
    return decay, zeta, xi, math.exp(cc * lg)


def _ret_fwd(proj, pos, tab, *, name):
    T = proj.shape[0]
    cc = min(RET_TILE, T)
    n = T // cc

    def body(rq_ref, rk_ref, rv_ref, pos_ref, tab_ref, y_ref, yn_ref, rprev_ref, r_s):
        @pl.when(pl.program_id(0) == 0)
        def _():
            r_s[...] = jnp.zeros_like(r_s)

        cs = _rope_cs(pos_ref[...], tab_ref)
        for hd in range(RET_HEADS):
            sl = slice(hd * HP, (hd + 1) * HP)
            decay, zeta, xi, gc = _ret_consts(cc, hd)
            q = _rope(rq_ref[:, sl].astype(F32), cs, RET_DK // 2).astype(BF16)
            kf = _rope(rk_ref[:, sl].astype(F32), cs, RET_DK // 2) * (RET_DK ** -0.5)
            k = kf.astype(BF16)
            v = rv_ref[:, sl]
            r = r_s[hd]
            rprev_ref[0, hd] = r
            inner = (_dot_nt(q, k) * decay).astype(BF16)
            y = _dot(inner, v) + _dot(q, r.astype(BF16)) * xi
            r_s[hd] = r * gc + _dot_tn((kf * zeta).astype(BF16), v)
            y_ref[:, sl] = y
            mu = jnp.mean(y, axis=-1, keepdims=True)
            yc = y - mu
            var = jnp.mean(yc * yc, axis=-1, keepdims=True)
            yn_ref[:, sl] = (yc * lax.rsqrt(var + GN_EPS)).astype(BF16)

    def blk(j):
        return pl.BlockSpec((cc, RW), lambda i: (i, j))

    return pl.pallas_call(
        body, name=name, grid=(n,),
        in_specs=[blk(0), blk(1), blk(2), pl.BlockSpec((cc, 1), lambda i: (i, 0)),
                  pl.BlockSpec((8, LANES), lambda i: (0, 0))],
        out_specs=[blk(0), blk(0), pl.BlockSpec((1, RET_HEADS, HP, RET_DV), lambda i: (i, 0, 0, 0))],
        out_shape=[jax.ShapeDtypeStruct((T, RW), F32), jax.ShapeDtypeStruct((T, RW), BF16),
                   jax.ShapeDtypeStruct((n, RET_HEADS, HP, RET_DV), F32)],
        scratch_shapes=[pltpu.VMEM((RET_HEADS, HP, RET_DV), F32)],
        compiler_params=_params(("arbitrary",)),
    )(proj, proj, proj, pos, tab)


def _ret_bwd(dyn, y, proj, pos, tab, rprev, *, name):
    T = proj.shape[0]
    cc = min(RET_TILE, T)
    n = T // cc

    def body(dyn_ref, y_ref, rq_ref, rk_ref, rv_ref, pos_ref, tab_ref, rprev_ref,
             drq_ref, drk_ref, drv_ref, dr_s):
        @pl.when(pl.program_id(0) == 0)
        def _():
            dr_s[...] = jnp.zeros_like(dr_s)

        cs = _rope_cs(pos_ref[...], tab_ref)
        for hd in range(RET_HEADS):
            sl = slice(hd * HP, (hd + 1) * HP)
            decay, zeta, xi, gc = _ret_consts(cc, hd)
            q = _rope(rq_ref[:, sl].astype(F32), cs, RET_DK // 2).astype(BF16)
            kf = _rope(rk_ref[:, sl].astype(F32), cs, RET_DK // 2) * (RET_DK ** -0.5)
            k = kf.astype(BF16)
            v = rv_ref[:, sl]
            yv = y_ref[:, sl]
            mu = jnp.mean(yv, axis=-1, keepdims=True)
            yc = yv - mu
            rs = lax.rsqrt(jnp.mean(yc * yc, axis=-1, keepdims=True) + GN_EPS)
            yn = yc * rs
            dn = dyn_ref[:, sl]
            dy = rs * (dn - jnp.mean(dn, axis=-1, keepdims=True) - yn * jnp.mean(dn * yn, axis=-1, keepdims=True))
            dyb = dy.astype(BF16)
            dyx = (dy * xi).astype(BF16)
            dr = dr_s[hd]
            drb = dr.astype(BF16)
            inner = (_dot_nt(q, k) * decay).astype(BF16)
            da = (_dot_nt(dyb, v) * decay).astype(BF16)
            dv = _dot_tn(inner, dyb) + _dot((kf * zeta).astype(BF16), drb)
            dq = _dot(da, k) + _dot_nt(dyx, rprev_ref[0, hd].astype(BF16))
            dk = _dot_tn(da, q) + _dot_nt(v, drb) * zeta
            dr_s[hd] = dr * gc + _dot_tn(q, dyx)
            drq_ref[:, sl] = _rope(dq, cs, RET_DK // 2, inverse=True).astype(BF16)
            drk_ref[:, sl] = _rope(dk * (RET_DK ** -0.5), cs, RET_DK // 2, inverse=True).astype(BF16)
            drv_ref[:, sl] = dv.astype(BF16)

    def blk(j):
        return pl.BlockSpec((cc, RW), lambda i: (n - 1 - i, j))

    return pl.pallas_call(
        body, name=name, grid=(n,),
        in_specs=[blk(0), blk(0), blk(0), blk(1), blk(2), pl.BlockSpec((cc, 1), lambda i: (n - 1 - i, 0)),
                  pl.BlockSpec((8, LANES), lambda i: (0, 0)),
                  pl.BlockSpec((1, RET_HEADS, HP, RET_DV), lambda i: (n - 1 - i, 0, 0, 0))],
        out_specs=[blk(0), blk(0), blk(0)],
        out_shape=[jax.ShapeDtypeStruct((T, RW), BF16)] * 3,
        scratch_shapes=[pltpu.VMEM((RET_HEADS, HP, RET_DV), F32)],
        compiler_params=_params(("arbitrary",)),
    )(dyn, y, proj, proj, proj, pos, tab, rprev)


def _merge_fwd(o, yn, proj, gn_w, w_bm, w_br, w_out, h, post_w, *, name):
    T, D = h.shape
    tT = min(TOKEN_TILE, T)
    g_blk = PROJ_FIXED // D

    def body(o_ref, yn_ref, rg_ref, gm_ref, gr_ref, gnw_ref, wbm_ref, wbr_ref, wout_ref, h_ref, post_ref,
             omla_ref, oret_ref, m_ref, ho_ref):
        groups = [slice(c * (tT // FFN_CHAINS), (c + 1) * (tT // FFN_CHAINS)) for c in range(FFN_CHAINS)]
        o_mlas = [_dot(o_ref[rs, :], wbm_ref[...]) for rs in groups]
        for rs, o_mla in zip(groups, o_mlas):
            rg = rg_ref[rs, :].astype(F32)
            gated = (rg * _sigmoid(rg) * (yn_ref[rs, :].astype(F32) * gnw_ref[...])).astype(BF16)
            o_ret = _dot(gated, wbr_ref[...])
            omla_ref[rs, :] = o_mla.astype(BF16)
            oret_ref[rs, :] = o_ret.astype(BF16)
            merged = _sigmoid(gm_ref[rs, :].astype(F32)) * o_mla + _sigmoid(gr_ref[rs, :].astype(F32)) * o_ret
            m = _dot(merged.astype(BF16), wout_ref[...])
            m_ref[rs, :] = m
            ho_ref[rs, :] = h_ref[rs, :] + _rms_fwd(m, post_ref[...])

    def full(r, c):
        return pl.BlockSpec((r, c), lambda i: (0, 0))

    def rows(c, j=0):
        return pl.BlockSpec((tT, c), lambda i: (i, j))

    return pl.pallas_call(
        body, name=name, grid=(T // tT,),
        in_specs=[rows(QW), rows(RW), rows(RW, 3), rows(D, g_blk), rows(D, g_blk + 1), full(1, RW),
                  full(QW, D), full(RW, D), full(D, D), rows(D), full(1, D)],
        out_specs=[rows(D), rows(D), rows(D), rows(D)],
        out_shape=[jax.ShapeDtypeStruct((T, D), BF16), jax.ShapeDtypeStruct((T, D), BF16),
                   jax.ShapeDtypeStruct((T, D), F32), jax.ShapeDtypeStruct((T, D), F32)],
        compiler_params=_params(("parallel",)),
    )(o, yn, proj, proj, proj, gn_w, w_bm, w_br, w_out, h, post_w)


def _merge_bwd(dho, m, post_w, omla, oret, proj, yn, gn_w, o, w_out, w_bm, w_br, *, name):
    T, D = dho.shape
    tT = min(MERGE_TILE, T)
    g_blk = PROJ_FIXED // D

    nT = T // tT

    def body(dho_ref, m_ref, post_ref, omla_ref, oret_ref, rg_ref, gm_ref, gr_ref, yn_ref, gnw_ref, o_ref,
             wout_ref, wbm_ref, wbr_ref,
             dgm_ref, dgr_ref, do_ref, delta_ref, drg_ref, dyn_ref, gpost_ref, ggn_ref,
             dwout_ref, dwbm_ref, dwbr_ref, acc_out, acc_bm, acc_br):
        @pl.when(pl.program_id(0) == 0)
        def _():
            gpost_ref[...] = jnp.zeros_like(gpost_ref)
            ggn_ref[...] = jnp.zeros_like(ggn_ref)
            acc_out[...] = jnp.zeros_like(acc_out)
            acc_bm[...] = jnp.zeros_like(acc_bm)
            acc_br[...] = jnp.zeros_like(acc_br)

        dm, gp = _rms_bwd(m_ref[...], post_ref[...], dho_ref[...])
        gpost_ref[...] += gp
        dmb = dm.astype(BF16)
        dmerged = _dot_nt(dmb, wout_ref[...])
        o_mla = omla_ref[...].astype(F32)
        o_ret = oret_ref[...].astype(F32)
        sgm = _sigmoid(gm_ref[...].astype(F32))
        sgr = _sigmoid(gr_ref[...].astype(F32))
        acc_out[...] += _dot_tn((sgm * o_mla + sgr * o_ret).astype(BF16), dmb)
        dgm_ref[...] = (dmerged * o_mla * sgm * (1.0 - sgm)).astype(BF16)
        dgr_ref[...] = (dmerged * o_ret * sgr * (1.0 - sgr)).astype(BF16)
        domla = (dmerged * sgm).astype(BF16)
        acc_bm[...] += _dot_tn(o_ref[...], domla)
        do = _dot_nt(domla, wbm_ref[...])
        do_ref[...] = do.astype(BF16)
        for hd in range(MLA_HEADS):
            sl = slice(hd * HP, (hd + 1) * HP)
            d = jnp.sum(do[:, sl] * o_ref[:, sl].astype(F32), axis=-1, keepdims=True)
            delta_ref[:, sl] = jnp.broadcast_to(d, (tT, HP))
        doret = (dmerged * sgr).astype(BF16)
        dgated = _dot_nt(doret, wbr_ref[...])
        rg = rg_ref[...].astype(F32)
        sg = _sigmoid(rg)
        srg = rg * sg
        ynv = yn_ref[...].astype(F32)
        yw = ynv * gnw_ref[...]
        acc_br[...] += _dot_tn((srg * yw).astype(BF16), doret)
        drg_ref[...] = (dgated * yw * (sg * (1.0 + rg * (1.0 - sg)))).astype(BF16)
        dgs = dgated * srg
        dyn_ref[...] = dgs * gnw_ref[...]
        ggn_ref[...] += jnp.sum(dgs * ynv, axis=0, keepdims=True)

        @pl.when(pl.program_id(0) == nT - 1)
        def _():
            dwout_ref[...] = acc_out[...].astype(BF16)
            dwbm_ref[...] = acc_bm[...].astype(BF16)
            dwbr_ref[...] = acc_br[...].astype(BF16)

    def full(r, c):
        return pl.BlockSpec((r, c), lambda i: (0, 0), pipeline_mode=pl.Buffered(1))

    def rows(c, j=0):
        return pl.BlockSpec((tT, c), lambda i: (i, j))

    return pl.pallas_call(
        body, name=name, grid=(nT,),
        in_specs=[rows(D), rows(D), full(1, D), rows(D), rows(D), rows(RW, 3), rows(D, g_blk), rows(D, g_blk + 1),
                  rows(RW), full(1, RW), rows(QW), full(D, D), full(QW, D), full(RW, D)],
        out_specs=[rows(D), rows(D), rows(QW), rows(QW), rows(RW), rows(RW), full(1, D), full(1, RW),
                   full(D, D), full(QW, D), full(RW, D)],
        out_shape=[jax.ShapeDtypeStruct((T, D), BF16)] * 2
        + [jax.ShapeDtypeStruct((T, QW), BF16), jax.ShapeDtypeStruct((T, QW), F32),
           jax.ShapeDtypeStruct((T, RW), BF16), jax.ShapeDtypeStruct((T, RW), F32),
           jax.ShapeDtypeStruct((1, D), F32), jax.ShapeDtypeStruct((1, RW), F32),
           jax.ShapeDtypeStruct((D, D), BF16), jax.ShapeDtypeStruct((QW, D), BF16), jax.ShapeDtypeStruct((RW, D), BF16)],
        scratch_shapes=[pltpu.VMEM((D, D), F32), pltpu.VMEM((QW, D), F32), pltpu.VMEM((RW, D), F32)],
        compiler_params=_params(("arbitrary",)),
    )(dho, m, post_w, omla, oret, proj, proj, proj, yn, gn_w, o, w_out, w_bm, w_br)


def _mesh_pos():
    return lax.axis_index("x"), lax.axis_index("y"), lax.axis_index("c")


class _Gather:
    def __init__(self, shards):
        self.operands = list(shards)
        self.n = len(shards)
        self.out_shape = [jax.ShapeDtypeStruct((N_DEV,) + s.shape, s.dtype) for s in shards]
        self.scratch = [pltpu.SemaphoreType.DMA((7 * self.n,)), pltpu.SemaphoreType.DMA((7 * self.n,)),
                        pltpu.SemaphoreType.DMA((self.n,))]

    def phase(self, p, x_refs, out_refs, sems):
        send_sems, recv_sems, local_sems = sems
        x, y, c = _mesh_pos()
        me, sibling = (x, y, c), (x, y, 1 - c)
        chips = [(1 - x, y), (x, 1 - y), (1 - x, 1 - y)]

        def copy(w, k, block, to, src=None):
            slot = out_refs[w].at[4 * block[0] + 2 * block[1] + block[2]]
            return pltpu.make_async_remote_copy(
                src_ref=slot if src is None else src, dst_ref=slot,
                send_sem=send_sems.at[7 * w + k], recv_sem=recv_sems.at[7 * w + k],
                device_id=to, device_id_type=pl.DeviceIdType.MESH)

        for w in range(self.n):
            mine = pltpu.make_async_copy(x_refs[w], out_refs[w].at[4 * x + 2 * y + c], local_sems.at[w])
            first = [copy(w, 0, me, sibling, src=x_refs[w])]
            first += [copy(w, 1 + j, me, (*chip, c), src=x_refs[w]) for j, chip in enumerate(chips)]
            passed = [copy(w, 4 + j, (*chip, c), sibling) for j, chip in enumerate(chips)]
            if p == 0:
                mine.start()
                for cp in first:
                    cp.start()
            elif p == 1:
                for j, chip in enumerate(chips):
                    copy(w, 1 + j, (*chip, c), me).wait_recv()
                    passed[j].start()
            else:
                copy(w, 0, sibling, me).wait_recv()
                for j, chip in enumerate(chips):
                    copy(w, 4 + j, (*chip, 1 - c), me).wait_recv()
                for cp in first + passed:
                    cp.wait_send()
                mine.wait()


class _Scatter:
    def __init__(self, grads, whole=()):
        self.n_sliced = len(grads)
        self.operands = list(grads) + list(whole)
        self.n = len(self.operands)
        self.out_shape = [jax.ShapeDtypeStruct(g.shape, g.dtype) for g in grads]
        self.out_shape += [jax.ShapeDtypeStruct((N_DEV,) + a.shape, a.dtype) for a in whole]
        n_sem = (N_DEV - 1) * self.n
        self.scratch = [pltpu.SemaphoreType.DMA((n_sem,)), pltpu.SemaphoreType.DMA((n_sem,)),
                        pltpu.SemaphoreType.DMA((self.n,))]

    def phase(self, p, in_refs, out_refs, sems):
        if p == 1:
            return
        send_sems, recv_sems, local_sems = sems
        x, y, c = _mesh_pos()
        me = 4 * x + 2 * y + c

        def src(w, dev):
            return in_refs[w].at[dev] if w < self.n_sliced else in_refs[w]

        for w in range(self.n):
            own = None if local_sems is None else pltpu.make_async_copy(src(w, me), out_refs[w].at[me], local_sems.at[w])
            sends, recvs = [], []
            for r in range(1, N_DEV):
                px = 1 - x if r & 4 else x
                py = 1 - y if r & 2 else y
                pc = 1 - c if r & 1 else c
                peer, pidx = (px, py, pc), 4 * px + 2 * py + pc
                k = (N_DEV - 1) * w + r - 1
                sends.append(pltpu.make_async_remote_copy(
                    src_ref=src(w, pidx), dst_ref=out_refs[w].at[me], send_sem=send_sems.at[k],
                    recv_sem=recv_sems.at[k], device_id=peer, device_id_type=pl.DeviceIdType.MESH))
                recvs.append(pltpu.make_async_remote_copy(
                    src_ref=src(w, me), dst_ref=out_refs[w].at[pidx], send_sem=send_sems.at[k],
                    recv_sem=recv_sems.at[k], device_id=peer, device_id_type=pl.DeviceIdType.MESH))
            if p == 0:
                if own is not None:
                    own.start()
                for cp in sends:
                    cp.start()
            else:
                for cp in recvs:
                    cp.wait_recv()
                for cp in sends:
                    cp.wait_send()
                if own is not None:
                    own.wait()


class _SplitScatter:
    def __init__(self, ex, name):
        self.ex, self.name = ex, name

    def _specs(self):
        ex = self.ex
        hbm = pl.BlockSpec(memory_space=pltpu.HBM)
        sem = pl.BlockSpec(memory_space=pltpu.SEMAPHORE)
        effect = pltpu.CompilerParams(has_side_effects=pltpu.SideEffectType.DATAFLOW_SIDE_EFFECTING)
        buffers = [pltpu.HBM(a.shape, a.dtype) for a in ex.operands] + [pltpu.HBM(s.shape, s.dtype) for s in ex.out_shape]
        return hbm, sem, effect, buffers

    def start(self):
        ex, n = self.ex, self.ex.n
        n_sem = (N_DEV - 1) * n
        hbm, sem, effect, buffers = self._specs()
        in_hbm = lambda a: pltpu.with_memory_space_constraint(a, pltpu.HBM)

        me = 4 * lax.axis_index("x") + 2 * lax.axis_index("y") + lax.axis_index("c")
        lands = []
        for w, (a, s) in enumerate(zip(ex.operands, ex.out_shape)):
            mine = lax.dynamic_index_in_dim(a, me, 0, keepdims=True) if w < ex.n_sliced else a[None]
            lands.append(lax.dynamic_update_slice_in_dim(lax.empty(s.shape, s.dtype), mine, me, 0))

        def start_body(*refs):
            ex.phase(0, refs[:n], refs[n:2 * n], (refs[2 * n], refs[2 * n + 1], None))
            refs[-1][...] = jnp.zeros_like(refs[-1])

        self.started = pl.pallas_call(
            start_body, name=self.name + "_start",
            out_shape=[pltpu.SemaphoreType.DMA((n_sem,)), pltpu.SemaphoreType.DMA((n_sem,))] + buffers
            + [jax.ShapeDtypeStruct((8, LANES), F32)],
            in_specs=[hbm] * (2 * n), out_specs=[sem, sem] + [hbm] * (2 * n) + [pl.BlockSpec(memory_space=pltpu.VMEM)],
            input_output_aliases={i: 2 + i for i in range(2 * n)}, compiler_params=effect,
        )(*[in_hbm(a) for a in ex.operands], *[in_hbm(a) for a in lands])
        return self.started[-1]

    def wait(self, after):
        ex, n = self.ex, self.ex.n
        hbm, sem, effect, buffers = self._specs()
        anyspec = pl.BlockSpec(memory_space=pl.ANY)

        def wait_body(*refs):
            ex.phase(2, refs[:n], refs[n:2 * n], (refs[2 * n], refs[2 * n + 1], None))

        done = pl.pallas_call(
            wait_body, name=self.name + "_wait", out_shape=buffers,
            in_specs=[hbm] * (2 * n) + [sem, sem] + [anyspec] * len(after), out_specs=[hbm] * (2 * n),
            input_output_aliases={i: i for i in range(2 * n)}, compiler_params=effect,
        )(*self.started[2:2 + 2 * n], self.started[0], self.started[1], *after)
        return done[n:]


def _exchange_alone(ex, *, name):
    n = ex.n

    def body(*refs):
        for p in range(3):
            ex.phase(p, refs[:n], refs[n:2 * n], refs[2 * n:])

    anyspec = pl.BlockSpec(memory_space=pl.ANY)
    return pl.pallas_call(body, name=name, out_shape=ex.out_shape, in_specs=[anyspec] * n,
                          out_specs=[anyspec] * n, scratch_shapes=ex.scratch)(*ex.operands)


def _adam_step(w_ref, p_ref, m_ref, v_ref, g_ref, d_ref, nm_ref, nv_ref):
    g = p_ref[0].astype(F32)
    for j in range(1, N_DEV):
        g = g + p_ref[j].astype(F32)
    g_ref[...] = g
    nm = ADAM_B1 * m_ref[...] + (1.0 - ADAM_B1) * g
    nv = ADAM_B2 * v_ref[...] + (1.0 - ADAM_B2) * (g * g)
    nm_ref[...] = nm
    nv_ref[...] = nv
    m_hat = nm / (1.0 - ADAM_B1 ** ADAM_STEP)
    v_hat = nv / (1.0 - ADAM_B2 ** ADAM_STEP)
    d_ref[...] = -ADAM_LR * (m_hat / (jnp.sqrt(v_hat) + ADAM_EPS) + ADAM_WD * w_ref[...])


def _adamw_vectors(ws, parts, ms, vs, *, name):
    n = len(ws)

    def body(*refs):
        w_refs, p_refs, m_refs, v_refs = (refs[i * n:(i + 1) * n] for i in range(4))
        outs = refs[4 * n:]
        for i in range(n):
            _adam_step(w_refs[i], p_refs[i], m_refs[i], v_refs[i], *outs[4 * i:4 * i + 4])

    return pl.pallas_call(
        body, name=name,
        out_shape=[jax.ShapeDtypeStruct(w.shape, F32) for w in ws for _ in range(4)],
    )(*ws, *parts, *ms, *vs)


def _adamw(w, parts, m, v, after, *, name):
    G, R, n = w.shape
    tn = 512 if (n > 512 and n % 512 == 0) else n
    tr = R
    for t in range(16, R, 16):
        if R % t == 0 and t * tn <= ADAM_BLOCK_CAP:
            tr = t
    if R * tn <= ADAM_BLOCK_CAP:
        tr = R

    def body(w_ref, p_ref, m_ref, v_ref, after_ref, g_ref, d_ref, nm_ref, nv_ref):
        _adam_step(w_ref, p_ref, m_ref, v_ref, g_ref, d_ref, nm_ref, nv_ref)

    blk = pl.BlockSpec((None, tr, tn), lambda g, i, j: (g, i, j))
    return pl.pallas_call(
        body, name=name, grid=(G, R // tr, n // tn),
        in_specs=[blk, pl.BlockSpec((N_DEV, None, tr, tn), lambda g, i, j: (0, g, i, j)), blk, blk,
                  pl.BlockSpec((8, LANES), lambda g, i, j: (0, 0))],
        out_specs=[blk, blk, blk, blk],
        out_shape=[jax.ShapeDtypeStruct((G, R, n), F32)] * 4,
        compiler_params=_params(("parallel", "parallel", "parallel")),
    )(w, parts, m, v, after)


def _pad_last(a, width):
    return jnp.pad(a, [(0, 0)] * (a.ndim - 1) + [(0, width - a.shape[-1])])


def _cols_of(g):
    return g.transpose(1, 0, 2).reshape(g.shape[1], N_DEV * g.shape[2])


def _col_shards(w):
    return w.reshape(w.shape[0], N_DEV, w.shape[1] // N_DEV).transpose(1, 0, 2)


def kernel(x, positions, ffn1_pre_w, ffn1_w1, ffn1_w2, ffn1_post_w, mix_pre_w, w_in, mla_q_norm_w, mla_w_uq, mla_kv_norm_w, mla_w_ukv, ret_gn_w, w_branch_mla, w_branch_ret, w_out, mix_post_w, ffn2_pre_w, ffn2_w1, ffn2_w2, ffn2_post_w, loss_target, m_ffn1_pre_w, m_ffn1_w1, m_ffn1_w2, m_ffn1_post_w, m_mix_pre_w, m_w_in, m_mla_q_norm_w, m_mla_w_uq, m_mla_kv_norm_w, m_mla_w_ukv, m_ret_gn_w, m_w_branch_mla, m_w_branch_ret, m_w_out, m_mix_post_w, m_ffn2_pre_w, m_ffn2_w1, m_ffn2_w2, m_ffn2_post_w, v_ffn1_pre_w, v_ffn1_w1, v_ffn1_w2, v_ffn1_post_w, v_mix_pre_w, v_w_in, v_mla_q_norm_w, v_mla_w_uq, v_mla_kv_norm_w, v_mla_w_ukv, v_ret_gn_w, v_w_branch_mla, v_w_branch_ret, v_w_out, v_mix_post_w, v_ffn2_pre_w, v_ffn2_w1, v_ffn2_w2, v_ffn2_post_w):
    T, D = x.shape[1], x.shape[2]
    h0 = x[0]
    tgt = loss_target[0]
    pos = positions.reshape(T, 1).astype(F32)

    big = [("ffn1_w1", ffn1_w1, m_ffn1_w1, v_ffn1_w1), ("ffn1_w2", ffn1_w2, m_ffn1_w2, v_ffn1_w2),
           ("w_in", w_in, m_w_in, v_w_in), ("mla_w_uq", mla_w_uq, m_mla_w_uq, v_mla_w_uq),
           ("mla_w_ukv", mla_w_ukv, m_mla_w_ukv, v_mla_w_ukv),
           ("w_branch_mla", w_branch_mla, m_w_branch_mla, v_w_branch_mla),
           ("w_branch_ret", w_branch_ret, m_w_branch_ret, v_w_branch_ret),
           ("w_out", w_out, m_w_out, v_w_out),
           ("ffn2_w1", ffn2_w1, m_ffn2_w1, v_ffn2_w1), ("ffn2_w2", ffn2_w2, m_ffn2_w2, v_ffn2_w2)]
    small = [("ffn1_pre_w", ffn1_pre_w, m_ffn1_pre_w, v_ffn1_pre_w), ("ffn1_post_w", ffn1_post_w, m_ffn1_post_w, v_ffn1_post_w),
             ("mix_pre_w", mix_pre_w, m_mix_pre_w, v_mix_pre_w), ("mla_q_norm_w", mla_q_norm_w, m_mla_q_norm_w, v_mla_q_norm_w),
             ("mla_kv_norm_w", mla_kv_norm_w, m_mla_kv_norm_w, v_mla_kv_norm_w), ("ret_gn_w", ret_gn_w, m_ret_gn_w, v_ret_gn_w),
             ("mix_post_w", mix_post_w, m_mix_post_w, v_mix_post_w), ("ffn2_pre_w", ffn2_pre_w, m_ffn2_pre_w, v_ffn2_pre_w),
             ("ffn2_post_w", ffn2_post_w, m_ffn2_post_w, v_ffn2_post_w)]

    half = ffn1_w2.shape[1]
    hp = -(-half // LANES) * LANES

    def rows_view(w):
        return w[0].T

    def send_w1(w):
        return jnp.pad(rows_view(w).reshape(2, half, D), ((0, 0), (0, hp - half), (0, 0))).reshape(2 * hp, D).astype(BF16)

    def send_w2(w):
        return jnp.pad(w[0], ((0, hp - half), (0, 0))).astype(BF16)

    mixer = ["w_in", "mla_w_uq", "mla_w_ukv", "w_branch_mla", "w_branch_ret", "w_out"]
    uq_w = MLA_NOPE + MLA_ROPE
    mixer_send = [rows_view(w_in).astype(BF16), jnp.pad(rows_view(mla_w_uq), ((0, HP - uq_w), (0, 0))).astype(BF16),
                  mla_w_ukv[0].astype(BF16), w_branch_mla[0].astype(BF16), w_branch_ret[0].astype(BF16),
                  w_out[0].astype(BF16)]

    w1a, w2a = _exchange_alone(_Gather([send_w1(ffn1_w1), send_w2(ffn1_w2)]), name="gather_ffn1")
    w2a = w2a.reshape(N_DEV // 2, 2 * hp, D)
    u1, f1, h1, a0, *got = _ffn_fwd(h0, ffn1_pre_w, w1a, w2a, ffn1_post_w, None, name="ffn1_fwd_gather_mixer",
                                exchange=_Gather(mixer_send))
    fw = dict(zip(mixer, got))

    wi = fw["w_in"].reshape(-1, D)
    cq_w, ckv_w, kr_w = wi[0:384], wi[384:640], wi[640:672]
    rq_w, rk_w = wi[672:928], wi[928:1184]
    rv_w, rg_w = wi[1184:1696], wi[1696:2208]
    gm_w, gr_w = wi[2208:2208 + D], wi[2208 + D:2208 + 2 * D]
    zer = lambda n: jnp.zeros((n, D), BF16)
    head_rows = lambda a, h: jnp.pad(a.reshape(h, -1, D), ((0, 0), (0, HP - a.shape[0] // h), (0, 0))).reshape(h * HP, D)
    w_in_p = jnp.concatenate([head_rows(rq_w, RET_HEADS), head_rows(rk_w, RET_HEADS), rv_w, rg_w,
                              cq_w, ckv_w, zer(MLA_NOPE), kr_w, zer(HP - MLA_NOPE - MLA_ROPE), zer(AW - 768),
                              gm_w, gr_w], axis=0)
    w_uq_p = fw["mla_w_uq"].reshape(QW, MLA_Q_RANK)
    ukv = fw["mla_w_ukv"].transpose(1, 0, 2)
    w_kv_p = jnp.concatenate([_pad_last(ukv[:, :, :MLA_NOPE], HP).reshape(MLA_KV_RANK, QW),
                              _pad_last(ukv[:, :, MLA_NOPE:], HP).reshape(MLA_KV_RANK, QW)], axis=1)
    w_bm_p = jnp.pad(_cols_of(fw["w_branch_mla"]).reshape(MLA_HEADS, MLA_V, D),
                     ((0, 0), (0, HP - MLA_V), (0, 0))).reshape(QW, D)
    w_br, w_o = _cols_of(fw["w_branch_ret"]), fw["w_out"].reshape(D, D)
    tab_mla = _rope_table(MLA_NOPE, MLA_ROPE // 2)
    tab_ret = _rope_table(0, RET_DK // 2)

    proj, a1 = _rms_matmul(h1, mix_pre_w, w_in_p, name="mixer_in_proj")
    q, k, v = _mla_prep_fwd(proj, pos, mla_q_norm_w, mla_kv_norm_w, w_uq_p, w_kv_p, tab_mla, name="mla_prep_fwd")
    o, lse, w1b, w2b = _flash_fwd(q, k, v, name="mla_attn_fwd_gather_ffn2",
                                  exchange=_Gather([send_w1(ffn2_w1), send_w2(ffn2_w2)]))
    w2b = w2b.reshape(N_DEV // 2, 2 * hp, D)
    ypre, yn, rprev = _ret_fwd(proj, pos, tab_ret, name="retention_fwd")
    omla, oret, m, h2 = _merge_fwd(o, yn, proj, ret_gn_w, w_bm_p, w_br, w_o, h1, mix_post_w, name="merge_fwd")
    u2, f2, _, a2, dy, lossp = _ffn_fwd(h2, ffn2_pre_w, w1b, w2b, ffn2_post_w, tgt, name="ffn2_fwd_loss")

    def grad(x, dy, tag, after=None):
        return _matmul_tn(x if x.ndim == 3 else x[None], dy if dy.ndim == 3 else dy[None], name=tag, after=after)

    g2, du2, df2, dh2, gpost2, gpre2 = _ffn_bwd(dy, f2, ffn2_post_w, h2, ffn2_pre_w, u2, w2b, w1b, name="ffn2_bwd")
    dw1b, = grad(du2.reshape(N_DEV, T, 2 * hp), a2, "ffn2_dw1")
    dw2b = grad(g2, df2, "ffn2_dw2")[0].reshape(N_DEV, hp, D)
    (dgm, dgr, do, delta, drg, dyn, gpostm, ggn, dw_out, dw_bm_p, dw_br) = _merge_bwd(
        dh2, m, mix_post_w, omla, oret, proj, yn, ret_gn_w, o, w_o, w_bm_p, w_br, name="merge_bwd")
    sc_ffn2 = _SplitScatter(_Scatter([dw1b, dw2b]), "scatter_ffn2")
    dq, dk, dv = _flash_bwd(q, k, v, do, lse, delta, name="mla_attn_bwd", after=sc_ffn2.start())
    da, gqn, gkvn, dw_uq_p, dw_kv_p = _mla_prep_bwd(dq, dk, dv, proj, pos, mla_q_norm_w, mla_kv_norm_w, w_uq_p, w_kv_p, tab_mla, name="mla_prep_bwd")
    drq, drk, drv = _ret_bwd(dyn, ypre, proj, pos, tab_ret, rprev, name="retention_bwd")
    dproj = jnp.concatenate([drq, drk, drv, drg, da, dgm, dgr], axis=1)
    dw_in_p = grad(dproj, a1, "dw_in")[0][0]

    dw_uq = dw_uq_p.reshape(MLA_HEADS, HP, MLA_Q_RANK)[:, :uq_w]
    dkp = dw_kv_p[:, :QW].reshape(MLA_KV_RANK, MLA_HEADS, HP)[:, :, :MLA_NOPE]
    dvp = dw_kv_p[:, QW:].reshape(MLA_KV_RANK, MLA_HEADS, HP)[:, :, :MLA_V]
    dw_ukv = jnp.concatenate([dkp, dvp], axis=2).transpose(1, 0, 2)
    dw_bm = dw_bm_p.reshape(MLA_HEADS, HP, D)[:, :MLA_V].reshape(MLA_HEADS * MLA_V, D)
    small_mixer_grads = [dw_uq, dw_ukv, _col_shards(dw_bm), _col_shards(dw_br), dw_out.reshape(N_DEV, D // N_DEV, D)]
    sc_small = _SplitScatter(_Scatter(small_mixer_grads), "scatter_mixer_small")
    dh1, gmixpre = _proj_bwd(dproj, w_in_p, h1, mix_pre_w, dh2, name="mixer_in_bwd", after=sc_small.start())
    unhead = lambda a, h, wd: a.reshape(h, HP, D)[:, :wd].reshape(h * wd, D)
    c0 = 4 * RW
    dw_in = jnp.concatenate([
        dw_in_p[c0:c0 + 384], dw_in_p[c0 + 384:c0 + 640], dw_in_p[c0 + 640 + MLA_NOPE:c0 + 640 + MLA_NOPE + MLA_ROPE],
        unhead(dw_in_p[0:RW], RET_HEADS, RET_DK), unhead(dw_in_p[RW:2 * RW], RET_HEADS, RET_DK),
        dw_in_p[2 * RW:3 * RW], dw_in_p[3 * RW:4 * RW],
        dw_in_p[PROJ_FIXED:PROJ_FIXED + D], dw_in_p[PROJ_FIXED + D:PROJ_FIXED + 2 * D]], axis=0).reshape(N_DEV, -1, D)
    sc_w_in = _SplitScatter(_Scatter([dw_in]), "scatter_w_in")
    g1, du1, df1, dx, gpost1, gpre1 = _ffn_bwd(
        dh1, f1, ffn1_post_w, h0, ffn1_pre_w, u1, w2a, w1a, name="ffn1_bwd", after=sc_w_in.start())
    dw2a = grad(g1, df1, "ffn1_dw2")[0].reshape(N_DEV, hp, D)
    sc_dw2a = _SplitScatter(_Scatter([dw2a]), "scatter_ffn1_dw2")
    dw1a, = grad(du1.reshape(N_DEV, T, 2 * hp), a0, "ffn1_dw1", after=sc_dw2a.start())

    small_g = {"ffn1_pre_w": gpre1, "ffn1_post_w": gpost1, "mix_pre_w": gmixpre, "mla_q_norm_w": gqn,
               "mla_kv_norm_w": gkvn, "ret_gn_w": ggn, "mix_post_w": gpostm, "ffn2_pre_w": gpre2, "ffn2_post_w": gpost2}
    sc_last = _SplitScatter(_Scatter([dw1a], whole=[small_g[nm] for nm, *_ in small] + [lossp]), "scatter_ffn1_dw1")
    token = sc_last.start()
    recv_ffn2 = sc_ffn2.wait([token])
    recv_mixer = sc_w_in.wait([token]) + sc_small.wait([token])
    recv_w2a, = sc_dw2a.wait([token])
    parts = dict(zip(mixer, recv_mixer))
    parts.update(ffn1_w2=recv_w2a, ffn2_w1=recv_ffn2[0], ffn2_w2=recv_ffn2[1])
    as_is = (lambda a: a, lambda p: p[:, None], lambda a: a)
    views = {nm: as_is for nm, *_ in big}
    for nm in ("ffn1_w1", "ffn2_w1"):
        views[nm] = (lambda a: rows_view(a).reshape(2, half, D), lambda p: p.reshape(N_DEV, 2, hp, D),
                     lambda a: a.reshape(2 * half, D).T[None])
    for nm in ("w_in", "mla_w_uq"):
        views[nm] = (lambda a: rows_view(a)[None], lambda p: p[:, None], lambda a: a[0].T[None])

    def update(nm, w, m_, v_, after):
        to_view, parts_view, back = views[nm]
        return [back(a) for a in _adamw(to_view(w), parts_view(parts[nm]), to_view(m_), to_view(v_), after,
                                        name="adamw_" + nm)]

    big_out = {nm: update(nm, w, m_, v_, token) for nm, w, m_, v_ in big if nm != "ffn1_w1"}
    recv_w1a, *small_parts, loss_parts = sc_last.wait([d[0] for d in big_out.values()])
    loss = jnp.sum(loss_parts[:, ::8, 0])
    parts["ffn1_w1"] = recv_w1a
    big_out["ffn1_w1"] = update("ffn1_w1", ffn1_w1, m_ffn1_w1, v_ffn1_w1, jnp.zeros((8, LANES), F32))
    small_out = _adamw_vectors([w for _, w, _, _ in small], small_parts, [a for _, _, a, _ in small],
                               [a for _, _, _, a in small], name="adamw_replicated")

    order = ["ffn1_pre_w", "ffn1_w1", "ffn1_w2", "ffn1_post_w", "mix_pre_w", "w_in", "mla_q_norm_w", "mla_w_uq",
             "mla_kv_norm_w", "mla_w_ukv", "ret_gn_w", "w_branch_mla", "w_branch_ret", "w_out", "mix_post_w",
             "ffn2_pre_w", "ffn2_w1", "ffn2_w2", "ffn2_post_w"]
    outs = [loss, dx[None]]
    for i in range(4):
        both = {nm: big_out[nm][i] for nm in big_out}
        both.update({nm: small_out[4 * j + i] for j, (nm, *_) in enumerate(small)})
        outs += [both[nm] for nm in order]
    return tuple(outs)
```

```python
import math

import numpy as np
import jax
import jax.numpy as jnp
from jax import lax
from jax.experimental import pallas as pl
from jax.experimental.pallas import tpu as pltpu

F32, BF16 = jnp.float32, jnp.bfloat16

MLA_HEADS, MLA_NOPE, MLA_ROPE, MLA_V = 8, 64, 32, 64
MLA_Q_RANK, MLA_KV_RANK = 384, 256
RET_HEADS, RET_DK, RET_DV = 4, 64, 128
ROPE_BASE, NORM_EPS, GN_EPS = 10000.0, 1e-6, 1e-6
ADAM_LR, ADAM_B1, ADAM_B2, ADAM_EPS, ADAM_WD, ADAM_STEP = 0.001, 0.9, 0.999, 1e-08, 0.01, 10
ATTN_SCALE = 1.0 / math.sqrt(MLA_NOPE + MLA_ROPE)

N_DEV = 8
LANES = 128
HP = LANES
QW = MLA_HEADS * HP
RW = RET_HEADS * HP
AW = 1024
PROJ_FIXED = 4 * RW + AW
NEG = -1e30

TOKEN_TILE = 512
ATTN_TILE = 1024
ATTN_CHAINS = 2
FFN_CHAINS = 2
RET_TILE = 256
PROJ_TILE_CAP = 2560
PROJ_TOKEN_TILE = 1024
GRAD_TILE_CAP = 1408
GRAD_TOKEN_TILE = 4096
ADAM_BLOCK_CAP = 192 * 1024
MERGE_TILE = 256
VMEM_LIMIT = 56 * 1024 * 1024


def _tile(n, cap, mult=LANES):
    if n <= cap:
        return n
    best = None
    for t in range(mult, cap + 1, mult):
        if n % t == 0:
            best = t
    assert best is not None, (n, cap, mult)
    return best


def _params(sem):
    return pltpu.CompilerParams(dimension_semantics=sem, vmem_limit_bytes=VMEM_LIMIT)


def _dot(a, b):
    return lax.dot_general(a, b, (((1,), (0,)), ((), ())), preferred_element_type=F32)


def _dot_nt(a, b):
    return lax.dot_general(a, b, (((1,), (1,)), ((), ())), preferred_element_type=F32)


def _dot_tn(a, b):
    return lax.dot_general(a, b, (((0,), (0,)), ((), ())), preferred_element_type=F32)


def _sigmoid(x):
    return pl.reciprocal(1.0 + jnp.exp(-x), approx=True)


def _rms_fwd(x, w):
    r = lax.rsqrt(jnp.mean(x * x, axis=-1, keepdims=True) + NORM_EPS)
    return x * r * w


def _rms_bwd(x, w, dy):
    r = lax.rsqrt(jnp.mean(x * x, axis=-1, keepdims=True) + NORM_EPS)
    xh = x * r
    g = dy * w
    dx = r * (g - xh * jnp.mean(g * xh, axis=-1, keepdims=True))
    return dx, jnp.sum(dy * xh, axis=0, keepdims=True)


def _rope_table(first, half):
    inv = (np.float32(ROPE_BASE) ** (-(np.arange(half, dtype=np.float32) / np.float32(half)))).astype(np.float32)
    tab = np.zeros((8, LANES), np.float32)
    tab[0, first:first + half] = inv
    tab[0, first + half:first + 2 * half] = inv
    tab[1, first:first + half] = -1.0
    tab[2, first + half:first + 2 * half] = 1.0
    return jnp.asarray(tab)


def _rope_cs(pos, tab_ref):
    ang = pos * tab_ref[0:1, :]
    s = jnp.sin(ang)
    return jnp.cos(ang), s * tab_ref[1:2, :], s * tab_ref[2:3, :]


def _rope(x, cs, half, inverse=False):
    c, s1, s2 = cs
    a = pltpu.roll(x, LANES - half, 1) * s1 + pltpu.roll(x, half, 1) * s2
    return x * c - a if inverse else x * c + a


def _call(body, *, name, grid, in_specs, out_specs, out_shape, scratch_shapes, args, exchange=None, after=None):
    sem = ("arbitrary",) * len(grid)
    anyspec = pl.BlockSpec(memory_space=pl.ANY)
    if exchange is None and after is not None:
        n_own = len(in_specs)

        def behind(*refs):
            body(*refs[:n_own], *refs[n_own + 1:])

        return pl.pallas_call(behind, name=name, grid=grid, in_specs=list(in_specs) + [anyspec], out_specs=out_specs,
                              out_shape=out_shape, scratch_shapes=scratch_shapes, compiler_params=_params(sem))(*args, after)
    if exchange is None:
        return pl.pallas_call(body, name=name, grid=grid, in_specs=in_specs, out_specs=out_specs,
                              out_shape=out_shape, scratch_shapes=scratch_shapes, compiler_params=_params(sem))(*args)
    n_in, n_out, e = len(in_specs), len(out_specs), exchange.n
    total = math.prod(grid)

    def carried(*refs):
        own = refs[:n_in] + refs[n_in + e:n_in + e + n_out] + refs[n_in + 2 * e + n_out:len(refs) - 3]
        ex_refs = (refs[n_in:n_in + e], refs[n_in + e + n_out:n_in + 2 * e + n_out], refs[len(refs) - 3:])
        step = pl.program_id(0)
        for d in range(1, len(grid)):
            step = step * grid[d] + pl.program_id(d)

        @pl.when(step == 0)
        def _():
            exchange.phase(0, *ex_refs)

        @pl.when(step == (3 * total) // 4)
        def _():
            exchange.phase(1, *ex_refs)

        body(*own)

        @pl.when(step == total - 1)
        def _():
            exchange.phase(2, *ex_refs)

    return pl.pallas_call(
        carried, name=name, grid=grid, in_specs=list(in_specs) + [anyspec] * e,
        out_specs=list(out_specs) + [anyspec] * e, out_shape=list(out_shape) + exchange.out_shape,
        scratch_shapes=list(scratch_shapes) + exchange.scratch, compiler_params=_params(sem),
    )(*args, *exchange.operands)


def _ffn_fwd(h, pre_w, w1, w2, post_w, target, *, name, exchange=None):
    T, D = h.shape
    nk, ck = w2.shape[0], w2.shape[1]
    tT = min(TOKEN_TILE, T)
    nT = T // tT
    with_loss = target is not None

    def body(*refs):
        if with_loss:
            (h_ref, pre_ref, w1g_ref, w1u_ref, w2_ref, post_ref, tgt_ref,
             u_ref, f_ref, ho_ref, a_s, dy_ref, loss_ref, acc) = refs
        else:
            (h_ref, pre_ref, w1g_ref, w1u_ref, w2_ref, post_ref,
             u_ref, f_ref, ho_ref, a_s, acc) = refs
        k = pl.program_id(1)

        @pl.when(k == 0)
        def _():
            a_s[...] = _rms_fwd(h_ref[...], pre_ref[...]).astype(BF16)
            acc[...] = jnp.zeros_like(acc)

        for c in range(FFN_CHAINS):
            rs = slice(c * (tT // FFN_CHAINS), (c + 1) * (tT // FFN_CHAINS))
            a = a_s[rs, :]
            ug = _dot_nt(a, w1g_ref[...])
            uu = _dot_nt(a, w1u_ref[...])
            u_ref[0, rs, :] = ug.astype(BF16)
            u_ref[1, rs, :] = uu.astype(BF16)
            acc[rs, :] += _dot((ug * _sigmoid(ug) * uu).astype(BF16), w2_ref[...])

        @pl.when(k == nk - 1)
        def _():
            f = acc[...]
            f_ref[...] = f
            ho = h_ref[...] + 0.5 * _rms_fwd(f, post_ref[...])
            ho_ref[...] = ho
            if with_loss:
                e = ho - tgt_ref[...]
                dy_ref[...] = e * (1.0 / D)
                loss_ref[...] = jnp.full(loss_ref.shape, (0.5 / D) * jnp.sum(e * e), F32)

    row = pl.BlockSpec((tT, D), lambda i, k: (i, 0))
    vec = pl.BlockSpec((1, D), lambda i, k: (0, 0))
    in_specs = [row, vec,
                pl.BlockSpec((None, ck, D), lambda i, k: (k, 0, 0)),
                pl.BlockSpec((None, ck, D), lambda i, k: (nk + k, 0, 0)),
                pl.BlockSpec((None, ck, D), lambda i, k: (k, 0, 0)),
                vec]
    out_shape = [jax.ShapeDtypeStruct((2, nk, T, ck), BF16),
                 jax.ShapeDtypeStruct((T, D), F32),
                 jax.ShapeDtypeStruct((T, D), F32),
                 jax.ShapeDtypeStruct((T, D), BF16)]
    out_specs = [pl.BlockSpec((2, None, tT, ck), lambda i, k: (0, k, i, 0)), row, row, row]
    args = [h, pre_w, w1, w1, w2, post_w]
    if with_loss:
        in_specs.append(row)
        args.append(target)
        out_shape += [jax.ShapeDtypeStruct((T, D), F32), jax.ShapeDtypeStruct((nT * 8, LANES), F32)]
        out_specs += [row, pl.BlockSpec((8, LANES), lambda i, k: (i, 0))]
    return _call(body, name=name, grid=(nT, nk), in_specs=in_specs, out_specs=out_specs, out_shape=out_shape,
                 scratch_shapes=[pltpu.VMEM((tT, D), F32)], args=args, exchange=exchange)


def _ffn_bwd(dho, f, post_w, h, pre_w, u, w2, w1, *, name, exchange=None, after=None):
    T, D = h.shape
    nk, ck = w2.shape[0], w2.shape[1]
    tT = min(TOKEN_TILE, T)
    nT = T // tT

    def body(dho_ref, f_ref, post_ref, h_ref, pre_ref, u_ref, w2_ref, w1g_ref, w1u_ref,
             g_ref, du_ref, df_s, dh_ref, gpost_ref, gpre_ref, da_acc):
        i, k = pl.program_id(0), pl.program_id(1)

        @pl.when(jnp.logical_and(i == 0, k == 0))
        def _():
            gpost_ref[...] = jnp.zeros_like(gpost_ref)
            gpre_ref[...] = jnp.zeros_like(gpre_ref)

        @pl.when(k == 0)
        def _():
            dx, dw = _rms_bwd(f_ref[...], post_ref[...], 0.5 * dho_ref[...])
            df_s[...] = dx.astype(BF16)
            gpost_ref[...] += dw
            da_acc[...] = jnp.zeros_like(da_acc)

        groups = [slice(c * (tT // FFN_CHAINS), (c + 1) * (tT // FFN_CHAINS)) for c in range(FFN_CHAINS)]
        dgs = [_dot_nt(df_s[rs, :], w2_ref[...]) for rs in groups]
        for rs, dg in zip(groups, dgs):
            ug = u_ref[0, rs, :].astype(F32)
            uu = u_ref[1, rs, :].astype(F32)
            sg = _sigmoid(ug)
            sl = ug * sg
            g_ref[rs, :] = (sl * uu).astype(BF16)
            dug = (dg * uu * (sg + sl * (1.0 - sg))).astype(BF16)
            duu = (dg * sl).astype(BF16)
            du_ref[0, rs, :] = dug
            du_ref[1, rs, :] = duu
            da_acc[rs, :] += _dot(dug, w1g_ref[...]) + _dot(duu, w1u_ref[...])

        @pl.when(k == nk - 1)
        def _():
            dx, dw = _rms_bwd(h_ref[...], pre_ref[...], da_acc[...])
            dh_ref[...] = dho_ref[...] + dx
            gpre_ref[...] += dw

    row = pl.BlockSpec((tT, D), lambda i, k: (i, 0))
    vec = pl.BlockSpec((1, D), lambda i, k: (0, 0))
    return _call(
        body, name=name, grid=(nT, nk),
        in_specs=[row, row, vec, row, vec,
                  pl.BlockSpec((2, None, tT, ck), lambda i, k: (0, k, i, 0)),
                  pl.BlockSpec((None, ck, D), lambda i, k: (k, 0, 0)),
                  pl.BlockSpec((None, ck, D), lambda i, k: (k, 0, 0)),
                  pl.BlockSpec((None, ck, D), lambda i, k: (nk + k, 0, 0))],
        out_specs=[pl.BlockSpec((None, tT, ck), lambda i, k: (k, i, 0)),
                   pl.BlockSpec((2, None, tT, ck), lambda i, k: (0, k, i, 0)),
                   row, row, vec, vec],
        out_shape=[jax.ShapeDtypeStruct((nk, T, ck), BF16),
                   jax.ShapeDtypeStruct((2, nk, T, ck), BF16),
                   jax.ShapeDtypeStruct((T, D), BF16),
                   jax.ShapeDtypeStruct((T, D), F32),
                   jax.ShapeDtypeStruct((1, D), F32),
                   jax.ShapeDtypeStruct((1, D), F32)],
        scratch_shapes=[pltpu.VMEM((tT, D), F32)],
        args=(dho, f, post_w, h, pre_w, u, w2, w1, w1), exchange=exchange, after=after)


def _matmul_tn(x, dy, *, name, exchange=None, after=None):
    Px, T, K = x.shape
    Py, _, N = dy.shape
    P = max(Px, Py)
    tT, tK, tN = min(GRAD_TOKEN_TILE, T), _tile(K, GRAD_TILE_CAP), _tile(N, GRAD_TILE_CAP)
    nt = T // tT

    def body(x_ref, dy_ref, o_ref, acc):
        t = pl.program_id(3)

        @pl.when(t == 0)
        def _():
            acc[...] = jnp.zeros_like(acc)

        acc[...] += _dot_tn(x_ref[...], dy_ref[...])

        @pl.when(t == nt - 1)
        def _():
            o_ref[...] = acc[...].astype(BF16)

    return _call(
        body, name=name, grid=(P, K // tK, N // tN, nt),
        in_specs=[pl.BlockSpec((None, tT, tK), lambda p, a, b, t: (p if Px > 1 else 0, t, a)),
                  pl.BlockSpec((None, tT, tN), lambda p, a, b, t: (p if Py > 1 else 0, t, b))],
        out_specs=[pl.BlockSpec((None, tK, tN), lambda p, a, b, t: (p, a, b))],
        out_shape=[jax.ShapeDtypeStruct((P, K, N), BF16)],
        scratch_shapes=[pltpu.VMEM((tK, tN), F32)], args=(x, dy), exchange=exchange, after=after)


def _rms_matmul(h, wn, w, *, name):
    T, D = h.shape
    N = w.shape[0]
    tT, tN = min(PROJ_TOKEN_TILE, T), _tile(N, PROJ_TILE_CAP)

    def body(h_ref, wn_ref, w_ref, y_ref, a_ref):
        @pl.when(pl.program_id(1) == 0)
        def _():
            a_ref[...] = _rms_fwd(h_ref[...], wn_ref[...]).astype(BF16)

        y_ref[...] = _dot_nt(a_ref[...], w_ref[...]).astype(BF16)

    return pl.pallas_call(
        body, name=name, grid=(T // tT, N // tN),
        in_specs=[pl.BlockSpec((tT, D), lambda i, j: (i, 0)),
                  pl.BlockSpec((1, D), lambda i, j: (0, 0)),
                  pl.BlockSpec((tN, D), lambda i, j: (j, 0))],
        out_specs=[pl.BlockSpec((tT, tN), lambda i, j: (i, j)),
                   pl.BlockSpec((tT, D), lambda i, j: (i, 0))],
        out_shape=[jax.ShapeDtypeStruct((T, N), BF16), jax.ShapeDtypeStruct((T, D), BF16)],
        compiler_params=_params(("parallel", "arbitrary")),
    )(h, wn, w)


def _proj_bwd(dproj, w, h, wn, dres, *, name, exchange=None, after=None):
    T, D = h.shape
    N = w.shape[0]
    tT, tN = min(PROJ_TOKEN_TILE, T), _tile(N, PROJ_TILE_CAP)
    nn = N // tN

    def body(dp_ref, w_ref, h_ref, wn_ref, dres_ref, dh_ref, gw_ref, acc):
        i, j = pl.program_id(0), pl.program_id(1)

        @pl.when(jnp.logical_and(i == 0, j == 0))
        def _():
            gw_ref[...] = jnp.zeros_like(gw_ref)

        @pl.when(j == 0)
        def _():
            acc[...] = jnp.zeros_like(acc)

        acc[...] += _dot(dp_ref[...], w_ref[...])

        @pl.when(j == nn - 1)
        def _():
            dx, dw = _rms_bwd(h_ref[...], wn_ref[...], acc[...])
            dh_ref[...] = dres_ref[...] + dx
            gw_ref[...] += dw

    row = pl.BlockSpec((tT, D), lambda i, j: (i, 0))
    vec = pl.BlockSpec((1, D), lambda i, j: (0, 0))
    return _call(
        body, name=name, grid=(T // tT, nn),
        in_specs=[pl.BlockSpec((tT, tN), lambda i, j: (i, j)),
                  pl.BlockSpec((tN, D), lambda i, j: (j, 0)), row, vec, row],
        out_specs=[row, vec],
        out_shape=[jax.ShapeDtypeStruct((T, D), F32), jax.ShapeDtypeStruct((1, D), F32)],
        scratch_shapes=[pltpu.VMEM((tT, D), F32)], args=(dproj, w, h, wn, dres), exchange=exchange, after=after)


def _mla_prep_fwd(proj, pos, qn_w, kvn_w, w_uq, w_kv, tab, *, name):
    T = proj.shape[0]
    tT = min(TOKEN_TILE, T)
    a_blk = PROJ_FIXED // AW - 1

    def body(a_ref, pos_ref, qnw_ref, kvnw_ref, wuq_ref, wkv_ref, tab_ref,
             q_ref, k_ref, v_ref):
        cq = a_ref[:, 0:MLA_Q_RANK].astype(F32)
        ckv = a_ref[:, MLA_Q_RANK:MLA_Q_RANK + MLA_KV_RANK].astype(F32)
        kr = a_ref[:, 640:768].astype(F32)
        qn = _rms_fwd(cq, qnw_ref[...]).astype(BF16)
        kvn = _rms_fwd(ckv, kvnw_ref[...]).astype(BF16)
        cs = _rope_cs(pos_ref[...], tab_ref)
        q = _dot_nt(qn, wuq_ref[...])
        kv = _dot(kvn, wkv_ref[...])
        krr = _rope(kr, cs, MLA_ROPE // 2)
        for hd in range(MLA_HEADS):
            sl = slice(hd * HP, (hd + 1) * HP)
            q_ref[:, sl] = (_rope(q[:, sl], cs, MLA_ROPE // 2) * ATTN_SCALE).astype(BF16)
            k_ref[:, sl] = (kv[:, sl] + krr).astype(BF16)
        v_ref[...] = kv[:, QW:].astype(BF16)

    def full(r, c):
        return pl.BlockSpec((r, c), lambda i: (0, 0))

    def rows(c):
        return pl.BlockSpec((tT, c), lambda i: (i, 0))

    return pl.pallas_call(
        body, name=name, grid=(T // tT,),
        in_specs=[pl.BlockSpec((tT, AW), lambda i: (i, a_blk)), rows(1),
                  full(1, MLA_Q_RANK), full(1, MLA_KV_RANK),
                  full(QW, MLA_Q_RANK), full(MLA_KV_RANK, 2 * QW), full(8, LANES)],
        out_specs=[rows(QW), rows(QW), rows(QW)],
        out_shape=[jax.ShapeDtypeStruct((T, QW), BF16)] * 3,
        compiler_params=_params(("parallel",)),
    )(proj, pos, qn_w, kvn_w, w_uq, w_kv, tab)


def _mla_prep_bwd(dq, dk, dv, proj, pos, qn_w, kvn_w, w_uq, w_kv, tab, *, name):
    T = proj.shape[0]
    tT = min(TOKEN_TILE, T)
    nT = T // tT
    a_blk = PROJ_FIXED // AW - 1

    def body(dq_ref, dk_ref, dv_ref, a_ref, pos_ref, qnw_ref, kvnw_ref, wuq_ref, wkv_ref, tab_ref,
             da_ref, gqn_ref, gkvn_ref, dwuq_ref, dwkv_ref, dql_ref, dkvl_ref, acc_uq, acc_kv):
        @pl.when(pl.program_id(0) == 0)
        def _():
            gqn_ref[...] = jnp.zeros_like(gqn_ref)
            gkvn_ref[...] = jnp.zeros_like(gkvn_ref)
            acc_uq[...] = jnp.zeros_like(acc_uq)
            acc_kv[...] = jnp.zeros_like(acc_kv)

        cs = _rope_cs(pos_ref[...], tab_ref)
        dkr = jnp.zeros((tT, HP), F32)
        for hd in range(MLA_HEADS):
            sl = slice(hd * HP, (hd + 1) * HP)
            dql_ref[:, sl] = (_rope(dq_ref[:, sl], cs, MLA_ROPE // 2, inverse=True) * ATTN_SCALE).astype(BF16)
            dkh = dk_ref[:, sl]
            dkr = dkr + dkh
            dkvl_ref[:, sl] = dkh.astype(BF16)
        dkvl_ref[:, QW:] = dv_ref[...]
        dqn = _dot(dql_ref[...], wuq_ref[...])
        dkvn = _dot_nt(dkvl_ref[...], wkv_ref[...])
        cq = a_ref[:, 0:MLA_Q_RANK].astype(F32)
        ckv = a_ref[:, MLA_Q_RANK:MLA_Q_RANK + MLA_KV_RANK].astype(F32)
        dcq, gq = _rms_bwd(cq, qnw_ref[...], dqn)
        dckv, gkv = _rms_bwd(ckv, kvnw_ref[...], dkvn)
        gqn_ref[...] += gq
        gkvn_ref[...] += gkv
        da_ref[:, 0:MLA_Q_RANK] = dcq.astype(BF16)
        da_ref[:, MLA_Q_RANK:MLA_Q_RANK + MLA_KV_RANK] = dckv.astype(BF16)
        da_ref[:, 640:768] = _rope(dkr, cs, MLA_ROPE // 2, inverse=True).astype(BF16)
        da_ref[:, 768:AW] = jnp.zeros((tT, AW - 768), BF16)
        acc_uq[...] += _dot_tn(dql_ref[...], _rms_fwd(cq, qnw_ref[...]).astype(BF16))
        acc_kv[...] += _dot_tn(_rms_fwd(ckv, kvnw_ref[...]).astype(BF16), dkvl_ref[...])

        @pl.when(pl.program_id(0) == nT - 1)
        def _():
            dwuq_ref[...] = acc_uq[...].astype(BF16)
            dwkv_ref[...] = acc_kv[...].astype(BF16)

    def full(r, c):
        return pl.BlockSpec((r, c), lambda i: (0, 0))

    def rows(c):
        return pl.BlockSpec((tT, c), lambda i: (i, 0))

    return pl.pallas_call(
        body, name=name, grid=(nT,),
        in_specs=[rows(QW), rows(QW), rows(QW), pl.BlockSpec((tT, AW), lambda i: (i, a_blk)), rows(1),
                  full(1, MLA_Q_RANK), full(1, MLA_KV_RANK),
                  full(QW, MLA_Q_RANK), full(MLA_KV_RANK, 2 * QW), full(8, LANES)],
        out_specs=[rows(AW), full(1, MLA_Q_RANK), full(1, MLA_KV_RANK),
                   full(QW, MLA_Q_RANK), full(MLA_KV_RANK, 2 * QW)],
        out_shape=[jax.ShapeDtypeStruct((T, AW), BF16),
                   jax.ShapeDtypeStruct((1, MLA_Q_RANK), F32), jax.ShapeDtypeStruct((1, MLA_KV_RANK), F32),
                   jax.ShapeDtypeStruct((QW, MLA_Q_RANK), BF16), jax.ShapeDtypeStruct((MLA_KV_RANK, 2 * QW), BF16)],
        scratch_shapes=[pltpu.VMEM((tT, QW), BF16), pltpu.VMEM((tT, 2 * QW), BF16),
                        pltpu.VMEM((QW, MLA_Q_RANK), F32), pltpu.VMEM((MLA_KV_RANK, 2 * QW), F32)],
        compiler_params=_params(("arbitrary",)),
    )(dq, dk, dv, proj, pos, qn_w, kvn_w, w_uq, w_kv, tab)


def _flash_fwd(q, k, v, *, name, exchange=None):
    T = q.shape[0]
    H = q.shape[1] // HP
    tq = min(ATTN_TILE, T)
    nq = T // tq

    sub = tq // ATTN_CHAINS

    def body(q_ref, k_ref, v_ref, o_ref, lse_ref):
        qi = pl.program_id(1)
        qs = [q_ref[c * sub:(c + 1) * sub, :] for c in range(ATTN_CHAINS)]

        def update(carry, off, masked, width=tq):
            nks = [(c + 1) * sub if masked else width for c in range(ATTN_CHAINS)]
            scores = [_dot_nt(qs[c], k_ref[pl.ds(off, nks[c]), :]) for c in range(ATTN_CHAINS)]
            out = []
            for c in range(ATTN_CHAINS):
                m_prev, l_prev, acc = carry[c]
                nk, s = nks[c], scores[c]
                vb = v_ref[pl.ds(off, nk), :]
                if masked:
                    rows = lax.broadcasted_iota(jnp.int32, (sub, nk), 0) + c * sub
                    s = jnp.where(rows >= lax.broadcasted_iota(jnp.int32, (sub, nk), 1), s, NEG)
                m_new = jnp.maximum(m_prev, jnp.max(s, axis=1, keepdims=True))
                alpha = jnp.exp(m_prev - m_new)
                p = jnp.exp(s - m_new)
                out.append((m_new, alpha * l_prev + jnp.sum(p, axis=1, keepdims=True),
                            alpha * acc + _dot(p.astype(BF16), vb)))
            return tuple(out)

        init = tuple((jnp.full((sub, 1), NEG, F32), jnp.zeros((sub, 1), F32), jnp.zeros((sub, HP), F32))
                     for _ in range(ATTN_CHAINS))
        def below_diagonal(j, cr):
            for hw in range(2):
                cr = update(cr, pl.multiple_of(j * tq + hw * (tq // 2), tq // 2), False, tq // 2)
            return cr

        carry = lax.fori_loop(0, qi, below_diagonal, init)
        carry = update(carry, pl.multiple_of(qi * tq, tq), True)
        for c in range(ATTN_CHAINS):
            m_fin, l_fin, acc = carry[c]
            o_ref[c * sub:(c + 1) * sub, :] = (acc / l_fin).astype(BF16)
            lse_ref[c * sub:(c + 1) * sub, :] = jnp.broadcast_to(m_fin + jnp.log(l_fin), (sub, HP))

    qspec = pl.BlockSpec((tq, HP), lambda h, i: (i, h))
    kspec = pl.BlockSpec((T, HP), lambda h, i: (0, h))
    return _call(
        body, name=name, grid=(H, nq),
        in_specs=[qspec, kspec, kspec], out_specs=[qspec, qspec],
        out_shape=[jax.ShapeDtypeStruct((T, H * HP), BF16), jax.ShapeDtypeStruct((T, H * HP), F32)],
        scratch_shapes=[], args=(q, k, v), exchange=exchange)


def _flash_bwd(q, k, v, do, lse, delta, *, name, exchange=None, after=None):
    T = q.shape[0]
    H = q.shape[1] // HP
    tq = min(ATTN_TILE, T)
    nq = T // tq
    sub = tq // ATTN_CHAINS

    def body(k_ref, v_ref, q_ref, do_ref, lse_ref, dl_ref, dq_ref, dk_ref, dv_ref):
        ki = pl.program_id(1)

        @pl.when(ki == 0)
        def _():
            dq_ref[...] = jnp.zeros_like(dq_ref)

        def grow(a):
            return a if a.shape[0] == tq else jnp.concatenate([a, jnp.zeros((tq - a.shape[0], HP), F32)], axis=0)

        def step(carry, j, masked):
            dk_acc, dv_acc = carry
            nks = [(c + 1) * sub if masked else tq for c in range(ATTN_CHAINS)]
            rws = [pl.ds(pl.multiple_of(j * tq + c * sub, sub), sub) for c in range(ATTN_CHAINS)]
            scores = [_dot_nt(q_ref[rws[c], :], k_ref[0:nks[c], :]) for c in range(ATTN_CHAINS)]
            dps = [_dot_nt(do_ref[rws[c], :], v_ref[0:nks[c], :]) for c in range(ATTN_CHAINS)]
            for c in range(ATTN_CHAINS):
                rows, nk, s, dp = rws[c], nks[c], scores[c], dps[c]
                kb = k_ref[0:nk, :]
                qb = q_ref[rows, :]
                dob = do_ref[rows, :]
                if masked:
                    ri = lax.broadcasted_iota(jnp.int32, (sub, nk), 0) + c * sub
                    s = jnp.where(ri >= lax.broadcasted_iota(jnp.int32, (sub, nk), 1), s, NEG)
                p = jnp.exp(s - lse_ref[rows, 0:1])
                dv_acc = dv_acc + grow(_dot_tn(p.astype(BF16), dob))
                ds = (p * (dp - dl_ref[rows, 0:1])).astype(BF16)
                dk_acc = dk_acc + grow(_dot_tn(ds, qb))
                dq_ref[rows, :] += _dot(ds, kb)
            return dk_acc, dv_acc

        carry = step((jnp.zeros((tq, HP), F32), jnp.zeros((tq, HP), F32)), ki, True)
        dk_acc, dv_acc = lax.fori_loop(ki + 1, nq, lambda j, cr: step(cr, j, False), carry)
        dk_ref[...] = dk_acc
        dv_ref[...] = dv_acc.astype(BF16)

    kspec = pl.BlockSpec((tq, HP), lambda h, j: (j, h))
    full = pl.BlockSpec((T, HP), lambda h, j: (0, h))
    return _call(
        body, name=name, grid=(H, nq),
        in_specs=[kspec, kspec, full, full, full, full], out_specs=[full, kspec, kspec],
        out_shape=[jax.ShapeDtypeStruct((T, H * HP), F32), jax.ShapeDtypeStruct((T, H * HP), F32),
                   jax.ShapeDtypeStruct((T, H * HP), BF16)],
        scratch_shapes=[], args=(k, v, q, do, lse, delta), exchange=exchange, after=after)


def _ret_consts(cc, hd):
    lg = math.log(1.0 - 2.0 ** (-5.0 - hd))
    diff = (lax.broadcasted_iota(jnp.int32, (cc, cc), 0) - lax.broadcasted_iota(jnp.int32, (cc, cc), 1)).astype(F32)
    decay = jnp.where(diff >= 0, jnp.exp(jnp.maximum(diff, 0.0) * lg), 0.0)
    idx = lax.broadcasted_iota(jnp.int32, (cc, 1), 0).astype(F32)
    zeta = jnp.exp((cc - 1.0 - idx) * lg)
    xi = jnp.exp((idx + 1.0) * lg)
    return decay, zeta, xi, math.exp(cc * lg)


def _ret_fwd(proj, pos, tab, *, name):
    T = proj.shape[0]
    cc = min(RET_TILE, T)
    n = T // cc

    def body(rq_ref, rk_ref, rv_ref, pos_ref, tab_ref, y_ref, yn_ref, rprev_ref, r_s):
        @pl.when(pl.program_id(0) == 0)
        def _():
            r_s[...] = jnp.zeros_like(r_s)

        cs = _rope_cs(pos_ref[...], tab_ref)
        for hd in range(RET_HEADS):
            sl = slice(hd * HP, (hd + 1) * HP)
            decay, zeta, xi, gc = _ret_consts(cc, hd)
            q = _rope(rq_ref[:, sl].astype(F32), cs, RET_DK // 2).astype(BF16)
            kf = _rope(rk_ref[:, sl].astype(F32), cs, RET_DK // 2) * (RET_DK ** -0.5)
            k = kf.astype(BF16)
            v = rv_ref[:, sl]
            r = r_s[hd]
            rprev_ref[0, hd] = r
            inner = (_dot_nt(q, k) * decay).astype(BF16)
            y = _dot(inner, v) + _dot(q, r.astype(BF16)) * xi
            r_s[hd] = r * gc + _dot_tn((kf * zeta).astype(BF16), v)
            y_ref[:, sl] = y
            mu = jnp.mean(y, axis=-1, keepdims=True)
            yc = y - mu
            var = jnp.mean(yc * yc, axis=-1, keepdims=True)
            yn_ref[:, sl] = (yc * lax.rsqrt(var + GN_EPS)).astype(BF16)

    def blk(j):
        return pl.BlockSpec((cc, RW), lambda i: (i, j))

    return pl.pallas_call(
        body, name=name, grid=(n,),
        in_specs=[blk(0), blk(1), blk(2), pl.BlockSpec((cc, 1), lambda i: (i, 0)),
                  pl.BlockSpec((8, LANES), lambda i: (0, 0))],
        out_specs=[blk(0), blk(0), pl.BlockSpec((1, RET_HEADS, HP, RET_DV), lambda i: (i, 0, 0, 0))],
        out_shape=[jax.ShapeDtypeStruct((T, RW), F32), jax.ShapeDtypeStruct((T, RW), BF16),
                   jax.ShapeDtypeStruct((n, RET_HEADS, HP, RET_DV), F32)],
        scratch_shapes=[pltpu.VMEM((RET_HEADS, HP, RET_DV), F32)],
        compiler_params=_params(("arbitrary",)),
    )(proj, proj, proj, pos, tab)


def _ret_bwd(dyn, y, proj, pos, tab, rprev, *, name):
    T = proj.shape[0]
    cc = min(RET_TILE, T)
    n = T // cc

    def body(dyn_ref, y_ref, rq_ref, rk_ref, rv_ref, pos_ref, tab_ref, rprev_ref,
             drq_ref, drk_ref, drv_ref, dr_s):
        @pl.when(pl.program_id(0) == 0)
        def _():
            dr_s[...] = jnp.zeros_like(dr_s)

        cs = _rope_cs(pos_ref[...], tab_ref)
        for hd in range(RET_HEADS):
            sl = slice(hd * HP, (hd + 1) * HP)
            decay, zeta, xi, gc = _ret_consts(cc, hd)
            q = _rope(rq_ref[:, sl].astype(F32), cs, RET_DK // 2).astype(BF16)
            kf = _rope(rk_ref[:, sl].astype(F32), cs, RET_DK // 2) * (RET_DK ** -0.5)
            k = kf.astype(BF16)
            v = rv_ref[:, sl]
            yv = y_ref[:, sl]
            mu = jnp.mean(yv, axis=-1, keepdims=True)
            yc = yv - mu
            rs = lax.rsqrt(jnp.mean(yc * yc, axis=-1, keepdims=True) + GN_EPS)
            yn = yc * rs
            dn = dyn_ref[:, sl]
            dy = rs * (dn - jnp.mean(dn, axis=-1, keepdims=True) - yn * jnp.mean(dn * yn, axis=-1, keepdims=True))
            dyb = dy.astype(BF16)
            dyx = (dy * xi).astype(BF16)
            dr = dr_s[hd]
            drb = dr.astype(BF16)
            inner = (_dot_nt(q, k) * decay).astype(BF16)
            da = (_dot_nt(dyb, v) * decay).astype(BF16)
            dv = _dot_tn(inner, dyb) + _dot((kf * zeta).astype(BF16), drb)
            dq = _dot(da, k) + _dot_nt(dyx, rprev_ref[0, hd].astype(BF16))
            dk = _dot_tn(da, q) + _dot_nt(v, drb) * zeta
            dr_s[hd] = dr * gc + _dot_tn(q, dyx)
            drq_ref[:, sl] = _rope(dq, cs, RET_DK // 2, inverse=True).astype(BF16)
            drk_ref[:, sl] = _rope(dk * (RET_DK ** -0.5), cs, RET_DK // 2, inverse=True).astype(BF16)
            drv_ref[:, sl] = dv.astype(BF16)

    def blk(j):
        return pl.BlockSpec((cc, RW), lambda i: (n - 1 - i, j))

    return pl.pallas_call(
        body, name=name, grid=(n,),
        in_specs=[blk(0), blk(0), blk(0), blk(1), blk(2), pl.BlockSpec((cc, 1), lambda i: (n - 1 - i, 0)),
                  pl.BlockSpec((8, LANES), lambda i: (0, 0)),
                  pl.BlockSpec((1, RET_HEADS, HP, RET_DV), lambda i: (n - 1 - i, 0, 0, 0))],
        out_specs=[blk(0), blk(0), blk(0)],
        out_shape=[jax.ShapeDtypeStruct((T, RW), BF16)] * 3,
        scratch_shapes=[pltpu.VMEM((RET_HEADS, HP, RET_DV), F32)],
        compiler_params=_params(("arbitrary",)),
    )(dyn, y, proj, proj, proj, pos, tab, rprev)


def _merge_fwd(o, yn, proj, gn_w, w_bm, w_br, w_out, h, post_w, *, name):
    T, D = h.shape
    tT = min(TOKEN_TILE, T)
    g_blk = PROJ_FIXED // D

    def body(o_ref, yn_ref, rg_ref, gm_ref, gr_ref, gnw_ref, wbm_ref, wbr_ref, wout_ref, h_ref, post_ref,
             omla_ref, oret_ref, m_ref, ho_ref):
        groups = [slice(c * (tT // FFN_CHAINS), (c + 1) * (tT // FFN_CHAINS)) for c in range(FFN_CHAINS)]
        o_mlas = [_dot(o_ref[rs, :], wbm_ref[...]) for rs in groups]
        for rs, o_mla in zip(groups, o_mlas):
            rg = rg_ref[rs, :].astype(F32)
            gated = (rg * _sigmoid(rg) * (yn_ref[rs, :].astype(F32) * gnw_ref[...])).astype(BF16)
            o_ret = _dot(gated, wbr_ref[...])
            omla_ref[rs, :] = o_mla.astype(BF16)
            oret_ref[rs, :] = o_ret.astype(BF16)
            merged = _sigmoid(gm_ref[rs, :].astype(F32)) * o_mla + _sigmoid(gr_ref[rs, :].astype(F32)) * o_ret
            m = _dot(merged.astype(BF16), wout_ref[...])
            m_ref[rs, :] = m
            ho_ref[rs, :] = h_ref[rs, :] + _rms_fwd(m, post_ref[...])

    def full(r, c):
        return pl.BlockSpec((r, c), lambda i: (0, 0))

    def rows(c, j=0):
        return pl.BlockSpec((tT, c), lambda i: (i, j))

    return pl.pallas_call(
        body, name=name, grid=(T // tT,),
        in_specs=[rows(QW), rows(RW), rows(RW, 3), rows(D, g_blk), rows(D, g_blk + 1), full(1, RW),
                  full(QW, D), full(RW, D), full(D, D), rows(D), full(1, D)],
        out_specs=[rows(D), rows(D), rows(D), rows(D)],
        out_shape=[jax.ShapeDtypeStruct((T, D), BF16), jax.ShapeDtypeStruct((T, D), BF16),
                   jax.ShapeDtypeStruct((T, D), F32), jax.ShapeDtypeStruct((T, D), F32)],
        compiler_params=_params(("parallel",)),
    )(o, yn, proj, proj, proj, gn_w, w_bm, w_br, w_out, h, post_w)


def _merge_bwd(dho, m, post_w, omla, oret, proj, yn, gn_w, o, w_out, w_bm, w_br, *, name):
    T, D = dho.shape
    tT = min(MERGE_TILE, T)
    g_blk = PROJ_FIXED // D

    nT = T // tT

    def body(dho_ref, m_ref, post_ref, omla_ref, oret_ref, rg_ref, gm_ref, gr_ref, yn_ref, gnw_ref, o_ref,
             wout_ref, wbm_ref, wbr_ref,
             dgm_ref, dgr_ref, do_ref, delta_ref, drg_ref, dyn_ref, gpost_ref, ggn_ref,
             dwout_ref, dwbm_ref, dwbr_ref, acc_out, acc_bm, acc_br):
        @pl.when(pl.program_id(0) == 0)
        def _():
            gpost_ref[...] = jnp.zeros_like(gpost_ref)
            ggn_ref[...] = jnp.zeros_like(ggn_ref)
            acc_out[...] = jnp.zeros_like(acc_out)
            acc_bm[...] = jnp.zeros_like(acc_bm)
            acc_br[...] = jnp.zeros_like(acc_br)

        dm, gp = _rms_bwd(m_ref[...], post_ref[...], dho_ref[...])
        gpost_ref[...] += gp
        dmb = dm.astype(BF16)
        dmerged = _dot_nt(dmb, wout_ref[...])
        o_mla = omla_ref[...].astype(F32)
        o_ret = oret_ref[...].astype(F32)
        sgm = _sigmoid(gm_ref[...].astype(F32))
        sgr = _sigmoid(gr_ref[...].astype(F32))
        acc_out[...] += _dot_tn((sgm * o_mla + sgr * o_ret).astype(BF16), dmb)
        dgm_ref[...] = (dmerged * o_mla * sgm * (1.0 - sgm)).astype(BF16)
        dgr_ref[...] = (dmerged * o_ret * sgr * (1.0 - sgr)).astype(BF16)
        domla = (dmerged * sgm).astype(BF16)
        acc_bm[...] += _dot_tn(o_ref[...], domla)
        do = _dot_nt(domla, wbm_ref[...])
        do_ref[...] = do.astype(BF16)
        for hd in range(MLA_HEADS):
            sl = slice(hd * HP, (hd + 1) * HP)
            d = jnp.sum(do[:, sl] * o_ref[:, sl].astype(F32), axis=-1, keepdims=True)
            delta_ref[:, sl] = jnp.broadcast_to(d, (tT, HP))
        doret = (dmerged * sgr).astype(BF16)
        dgated = _dot_nt(doret, wbr_ref[...])
        rg = rg_ref[...].astype(F32)
        sg = _sigmoid(rg)
        srg = rg * sg
        ynv = yn_ref[...].astype(F32)
        yw = ynv * gnw_ref[...]
        acc_br[...] += _dot_tn((srg * yw).astype(BF16), doret)
        drg_ref[...] = (dgated * yw * (sg * (1.0 + rg * (1.0 - sg)))).astype(BF16)
        dgs = dgated * srg
        dyn_ref[...] = dgs * gnw_ref[...]
        ggn_ref[...] += jnp.sum(dgs * ynv, axis=0, keepdims=True)

        @pl.when(pl.program_id(0) == nT - 1)
        def _():
            dwout_ref[...] = acc_out[...].astype(BF16)
            dwbm_ref[...] = acc_bm[...].astype(BF16)
            dwbr_ref[...] = acc_br[...].astype(BF16)

    def full(r, c):
        return pl.BlockSpec((r, c), lambda i: (0, 0), pipeline_mode=pl.Buffered(1))

    def rows(c, j=0):
        return pl.BlockSpec((tT, c), lambda i: (i, j))

    return pl.pallas_call(
        body, name=name, grid=(nT,),
        in_specs=[rows(D), rows(D), full(1, D), rows(D), rows(D), rows(RW, 3), rows(D, g_blk), rows(D, g_blk + 1),
                  rows(RW), full(1, RW), rows(QW), full(D, D), full(QW, D), full(RW, D)],
        out_specs=[rows(D), rows(D), rows(QW), rows(QW), rows(RW), rows(RW), full(1, D), full(1, RW),
                   full(D, D), full(QW, D), full(RW, D)],
        out_shape=[jax.ShapeDtypeStruct((T, D), BF16)] * 2
        + [jax.ShapeDtypeStruct((T, QW), BF16), jax.ShapeDtypeStruct((T, QW), F32),
           jax.ShapeDtypeStruct((T, RW), BF16), jax.ShapeDtypeStruct((T, RW), F32),
           jax.ShapeDtypeStruct((1, D), F32), jax.ShapeDtypeStruct((1, RW), F32),
           jax.ShapeDtypeStruct((D, D), BF16), jax.ShapeDtypeStruct((QW, D), BF16), jax.ShapeDtypeStruct((RW, D), BF16)],
        scratch_shapes=[pltpu.VMEM((D, D), F32), pltpu.VMEM((QW, D), F32), pltpu.VMEM((RW, D), F32)],
        compiler_params=_params(("arbitrary",)),
    )(dho, m, post_w, omla, oret, proj, proj, proj, yn, gn_w, o, w_out, w_bm, w_br)


def _mesh_pos():
    return lax.axis_index("x"), lax.axis_index("y"), lax.axis_index("c")


class _Gather:
    def __init__(self, shards):
        self.operands = list(shards)
        self.n = len(shards)
        self.out_shape = [jax.ShapeDtypeStruct((N_DEV,) + s.shape, s.dtype) for s in shards]
        self.scratch = [pltpu.SemaphoreType.DMA((7 * self.n,)), pltpu.SemaphoreType.DMA((7 * self.n,)),
                        pltpu.SemaphoreType.DMA((self.n,))]

    def phase(self, p, x_refs, out_refs, sems):
        send_sems, recv_sems, local_sems = sems
        x, y, c = _mesh_pos()
        me, sibling = (x, y, c), (x, y, 1 - c)
        chips = [(1 - x, y), (x, 1 - y), (1 - x, 1 - y)]

        def copy(w, k, block, to, src=None):
            slot = out_refs[w].at[4 * block[0] + 2 * block[1] + block[2]]
            return pltpu.make_async_remote_copy(
                src_ref=slot if src is None else src, dst_ref=slot,
                send_sem=send_sems.at[7 * w + k], recv_sem=recv_sems.at[7 * w + k],
                device_id=to, device_id_type=pl.DeviceIdType.MESH)

        for w in range(self.n):
            mine = pltpu.make_async_copy(x_refs[w], out_refs[w].at[4 * x + 2 * y + c], local_sems.at[w])
            first = [copy(w, 0, me, sibling, src=x_refs[w])]
            first += [copy(w, 1 + j, me, (*chip, c), src=x_refs[w]) for j, chip in enumerate(chips)]
            passed = [copy(w, 4 + j, (*chip, c), sibling) for j, chip in enumerate(chips)]
            if p == 0:
                mine.start()
                for cp in first:
                    cp.start()
            elif p == 1:
                for j, chip in enumerate(chips):
                    copy(w, 1 + j, (*chip, c), me).wait_recv()
                    passed[j].start()
            else:
                copy(w, 0, sibling, me).wait_recv()
                for j, chip in enumerate(chips):
                    copy(w, 4 + j, (*chip, 1 - c), me).wait_recv()
                for cp in first + passed:
                    cp.wait_send()
                mine.wait()


class _Scatter:
    def __init__(self, grads, whole=()):
        self.n_sliced = len(grads)
        self.operands = list(grads) + list(whole)
        self.n = len(self.operands)
        self.out_shape = [jax.ShapeDtypeStruct(g.shape, g.dtype) for g in grads]
        self.out_shape += [jax.ShapeDtypeStruct((N_DEV,) + a.shape, a.dtype) for a in whole]
        n_sem = (N_DEV - 1) * self.n
        self.scratch = [pltpu.SemaphoreType.DMA((n_sem,)), pltpu.SemaphoreType.DMA((n_sem,)),
                        pltpu.SemaphoreType.DMA((self.n,))]

    def phase(self, p, in_refs, out_refs, sems):
        if p == 1:
            return
        send_sems, recv_sems, local_sems = sems
        x, y, c = _mesh_pos()
        me = 4 * x + 2 * y + c

        def src(w, dev):
            return in_refs[w].at[dev] if w < self.n_sliced else in_refs[w]

        for w in range(self.n):
            own = None if local_sems is None else pltpu.make_async_copy(src(w, me), out_refs[w].at[me], local_sems.at[w])
            sends, recvs = [], []
            for r in range(1, N_DEV):
                px = 1 - x if r & 4 else x
                py = 1 - y if r & 2 else y
                pc = 1 - c if r & 1 else c
                peer, pidx = (px, py, pc), 4 * px + 2 * py + pc
                k = (N_DEV - 1) * w + r - 1
                sends.append(pltpu.make_async_remote_copy(
                    src_ref=src(w, pidx), dst_ref=out_refs[w].at[me], send_sem=send_sems.at[k],
                    recv_sem=recv_sems.at[k], device_id=peer, device_id_type=pl.DeviceIdType.MESH))
                recvs.append(pltpu.make_async_remote_copy(
                    src_ref=src(w, me), dst_ref=out_refs[w].at[pidx], send_sem=send_sems.at[k],
                    recv_sem=recv_sems.at[k], device_id=peer, device_id_type=pl.DeviceIdType.MESH))
            if p == 0:
                if own is not None:
                    own.start()
                for cp in sends:
                    cp.start()
            else:
                for cp in recvs:
                    cp.wait_recv()
                for cp in sends:
                    cp.wait_send()
                if own is not None:
                    own.wait()


class _SplitScatter:
    def __init__(self, ex, name):
        self.ex, self.name = ex, name

    def _specs(self):
        ex = self.ex
        hbm = pl.BlockSpec(memory_space=pltpu.HBM)
        sem = pl.BlockSpec(memory_space=pltpu.SEMAPHORE)
        effect = pltpu.CompilerParams(has_side_effects=pltpu.SideEffectType.DATAFLOW_SIDE_EFFECTING)
        buffers = [pltpu.HBM(a.shape, a.dtype) for a in ex.operands] + [pltpu.HBM(s.shape, s.dtype) for s in ex.out_shape]
        return hbm, sem, effect, buffers

    def start(self):
        ex, n = self.ex, self.ex.n
        n_sem = (N_DEV - 1) * n
        hbm, sem, effect, buffers = self._specs()
        in_hbm = lambda a: pltpu.with_memory_space_constraint(a, pltpu.HBM)

        me = 4 * lax.axis_index("x") + 2 * lax.axis_index("y") + lax.axis_index("c")
        lands = []
        for w, (a, s) in enumerate(zip(ex.operands, ex.out_shape)):
            mine = lax.dynamic_index_in_dim(a, me, 0, keepdims=True) if w < ex.n_sliced else a[None]
            lands.append(lax.dynamic_update_slice_in_dim(lax.empty(s.shape, s.dtype), mine, me, 0))

        def start_body(*refs):
            ex.phase(0, refs[:n], refs[n:2 * n], (refs[2 * n], refs[2 * n + 1], None))
            refs[-1][...] = jnp.zeros_like(refs[-1])

        self.started = pl.pallas_call(
            start_body, name=self.name + "_start",
            out_shape=[pltpu.SemaphoreType.DMA((n_sem,)), pltpu.SemaphoreType.DMA((n_sem,))] + buffers
            + [jax.ShapeDtypeStruct((8, LANES), F32)],
            in_specs=[hbm] * (2 * n), out_specs=[sem, sem] + [hbm] * (2 * n) + [pl.BlockSpec(memory_space=pltpu.VMEM)],
            input_output_aliases={i: 2 + i for i in range(2 * n)}, compiler_params=effect,
        )(*[in_hbm(a) for a in ex.operands], *[in_hbm(a) for a in lands])
        return self.started[-1]

    def wait(self, after):
        ex, n = self.ex, self.ex.n
        hbm, sem, effect, buffers = self._specs()
        anyspec = pl.BlockSpec(memory_space=pl.ANY)

        def wait_body(*refs):
            ex.phase(2, refs[:n], refs[n:2 * n], (refs[2 * n], refs[2 * n + 1], None))

        done = pl.pallas_call(
            wait_body, name=self.name + "_wait", out_shape=buffers,
            in_specs=[hbm] * (2 * n) + [sem, sem] + [anyspec] * len(after), out_specs=[hbm] * (2 * n),
            input_output_aliases={i: i for i in range(2 * n)}, compiler_params=effect,
        )(*self.started[2:2 + 2 * n], self.started[0], self.started[1], *after)
        return done[n:]


def _exchange_alone(ex, *, name):
    n = ex.n

    def body(*refs):
        for p in range(3):
            ex.phase(p, refs[:n], refs[n:2 * n], refs[2 * n:])

    anyspec = pl.BlockSpec(memory_space=pl.ANY)
    return pl.pallas_call(body, name=name, out_shape=ex.out_shape, in_specs=[anyspec] * n,
                          out_specs=[anyspec] * n, scratch_shapes=ex.scratch)(*ex.operands)


def _adam_step(w_ref, p_ref, m_ref, v_ref, g_ref, d_ref, nm_ref, nv_ref):
    g = p_ref[0].astype(F32)
    for j in range(1, N_DEV):
        g = g + p_ref[j].astype(F32)
    g_ref[...] = g
    nm = ADAM_B1 * m_ref[...] + (1.0 - ADAM_B1) * g
    nv = ADAM_B2 * v_ref[...] + (1.0 - ADAM_B2) * (g * g)
    nm_ref[...] = nm
    nv_ref[...] = nv
    m_hat = nm / (1.0 - ADAM_B1 ** ADAM_STEP)
    v_hat = nv / (1.0 - ADAM_B2 ** ADAM_STEP)
    d_ref[...] = -ADAM_LR * (m_hat / (jnp.sqrt(v_hat) + ADAM_EPS) + ADAM_WD * w_ref[...])


def _adamw_vectors(ws, parts, ms, vs, *, name):
    n = len(ws)

    def body(*refs):
        w_refs, p_refs, m_refs, v_refs = (refs[i * n:(i + 1) * n] for i in range(4))
        outs = refs[4 * n:]
        for i in range(n):
            _adam_step(w_refs[i], p_refs[i], m_refs[i], v_refs[i], *outs[4 * i:4 * i + 4])

    return pl.pallas_call(
        body, name=name,
        out_shape=[jax.ShapeDtypeStruct(w.shape, F32) for w in ws for _ in range(4)],
    )(*ws, *parts, *ms, *vs)


def _adamw(w, parts, m, v, after, *, name):
    G, R, n = w.shape
    tn = 512 if (n > 512 and n % 512 == 0) else n
    tr = R
    for t in range(16, R, 16):
        if R % t == 0 and t * tn <= ADAM_BLOCK_CAP:
            tr = t
    if R * tn <= ADAM_BLOCK_CAP:
        tr = R

    def body(w_ref, p_ref, m_ref, v_ref, after_ref, g_ref, d_ref, nm_ref, nv_ref):
        _adam_step(w_ref, p_ref, m_ref, v_ref, g_ref, d_ref, nm_ref, nv_ref)

    blk = pl.BlockSpec((None, tr, tn), lambda g, i, j: (g, i, j))
    return pl.pallas_call(
        body, name=name, grid=(G, R // tr, n // tn),
        in_specs=[blk, pl.BlockSpec((N_DEV, None, tr, tn), lambda g, i, j: (0, g, i, j)), blk, blk,
                  pl.BlockSpec((8, LANES), lambda g, i, j: (0, 0))],
        out_specs=[blk, blk, blk, blk],
        out_shape=[jax.ShapeDtypeStruct((G, R, n), F32)] * 4,
        compiler_params=_params(("parallel", "parallel", "parallel")),
    )(w, parts, m, v, after)


def _pad_last(a, width):
    return jnp.pad(a, [(0, 0)] * (a.ndim - 1) + [(0, width - a.shape[-1])])


def _cols_of(g):
    return g.transpose(1, 0, 2).reshape(g.shape[1], N_DEV * g.shape[2])


def _col_shards(w):
    return w.reshape(w.shape[0], N_DEV, w.shape[1] // N_DEV).transpose(1, 0, 2)


def kernel(x, positions, ffn1_pre_w, ffn1_w1, ffn1_w2, ffn1_post_w, mix_pre_w, w_in, mla_q_norm_w, mla_w_uq, mla_kv_norm_w, mla_w_ukv, ret_gn_w, w_branch_mla, w_branch_ret, w_out, mix_post_w, ffn2_pre_w, ffn2_w1, ffn2_w2, ffn2_post_w, loss_target, m_ffn1_pre_w, m_ffn1_w1, m_ffn1_w2, m_ffn1_post_w, m_mix_pre_w, m_w_in, m_mla_q_norm_w, m_mla_w_uq, m_mla_kv_norm_w, m_mla_w_ukv, m_ret_gn_w, m_w_branch_mla, m_w_branch_ret, m_w_out, m_mix_post_w, m_ffn2_pre_w, m_ffn2_w1, m_ffn2_w2, m_ffn2_post_w, v_ffn1_pre_w, v_ffn1_w1, v_ffn1_w2, v_ffn1_post_w, v_mix_pre_w, v_w_in, v_mla_q_norm_w, v_mla_w_uq, v_mla_kv_norm_w, v_mla_w_ukv, v_ret_gn_w, v_w_branch_mla, v_w_branch_ret, v_w_out, v_mix_post_w, v_ffn2_pre_w, v_ffn2_w1, v_ffn2_w2, v_ffn2_post_w):
    T, D = x.shape[1], x.shape[2]
    h0 = x[0]
    tgt = loss_target[0]
    pos = positions.reshape(T, 1).astype(F32)

    big = [("ffn1_w1", ffn1_w1, m_ffn1_w1, v_ffn1_w1), ("ffn1_w2", ffn1_w2, m_ffn1_w2, v_ffn1_w2),
           ("w_in", w_in, m_w_in, v_w_in), ("mla_w_uq", mla_w_uq, m_mla_w_uq, v_mla_w_uq),
           ("mla_w_ukv", mla_w_ukv, m_mla_w_ukv, v_mla_w_ukv),
           ("w_branch_mla", w_branch_mla, m_w_branch_mla, v_w_branch_mla),
           ("w_branch_ret", w_branch_ret, m_w_branch_ret, v_w_branch_ret),
           ("w_out", w_out, m_w_out, v_w_out),
           ("ffn2_w1", ffn2_w1, m_ffn2_w1, v_ffn2_w1), ("ffn2_w2", ffn2_w2, m_ffn2_w2, v_ffn2_w2)]
    small = [("ffn1_pre_w", ffn1_pre_w, m_ffn1_pre_w, v_ffn1_pre_w), ("ffn1_post_w", ffn1_post_w, m_ffn1_post_w, v_ffn1_post_w),
             ("mix_pre_w", mix_pre_w, m_mix_pre_w, v_mix_pre_w), ("mla_q_norm_w", mla_q_norm_w, m_mla_q_norm_w, v_mla_q_norm_w),
             ("mla_kv_norm_w", mla_kv_norm_w, m_mla_kv_norm_w, v_mla_kv_norm_w), ("ret_gn_w", ret_gn_w, m_ret_gn_w, v_ret_gn_w),
             ("mix_post_w", mix_post_w, m_mix_post_w, v_mix_post_w), ("ffn2_pre_w", ffn2_pre_w, m_ffn2_pre_w, v_ffn2_pre_w),
             ("ffn2_post_w", ffn2_post_w, m_ffn2_post_w, v_ffn2_post_w)]

    half = ffn1_w2.shape[1]
    hp = -(-half // LANES) * LANES

    def rows_view(w):
        return w[0].T

    def send_w1(w):
        return jnp.pad(rows_view(w).reshape(2, half, D), ((0, 0), (0, hp - half), (0, 0))).reshape(2 * hp, D).astype(BF16)

    def send_w2(w):
        return jnp.pad(w[0], ((0, hp - half), (0, 0))).astype(BF16)

    mixer = ["w_in", "mla_w_uq", "mla_w_ukv", "w_branch_mla", "w_branch_ret", "w_out"]
    uq_w = MLA_NOPE + MLA_ROPE
    mixer_send = [rows_view(w_in).astype(BF16), jnp.pad(rows_view(mla_w_uq), ((0, HP - uq_w), (0, 0))).astype(BF16),
                  mla_w_ukv[0].astype(BF16), w_branch_mla[0].astype(BF16), w_branch_ret[0].astype(BF16),
                  w_out[0].astype(BF16)]

    w1a, w2a = _exchange_alone(_Gather([send_w1(ffn1_w1), send_w2(ffn1_w2)]), name="gather_ffn1")
    w2a = w2a.reshape(N_DEV // 2, 2 * hp, D)
    u1, f1, h1, a0, *got = _ffn_fwd(h0, ffn1_pre_w, w1a, w2a, ffn1_post_w, None, name="ffn1_fwd_gather_mixer",
                                exchange=_Gather(mixer_send))
    fw = dict(zip(mixer, got))

    wi = fw["w_in"].reshape(-1, D)
    cq_w, ckv_w, kr_w = wi[0:384], wi[384:640], wi[640:672]
    rq_w, rk_w = wi[672:928], wi[928:1184]
    rv_w, rg_w = wi[1184:1696], wi[1696:2208]
    gm_w, gr_w = wi[2208:2208 + D], wi[2208 + D:2208 + 2 * D]
    zer = lambda n: jnp.zeros((n, D), BF16)
    head_rows = lambda a, h: jnp.pad(a.reshape(h, -1, D), ((0, 0), (0, HP - a.shape[0] // h), (0, 0))).reshape(h * HP, D)
    w_in_p = jnp.concatenate([head_rows(rq_w, RET_HEADS), head_rows(rk_w, RET_HEADS), rv_w, rg_w,
                              cq_w, ckv_w, zer(MLA_NOPE), kr_w, zer(HP - MLA_NOPE - MLA_ROPE), zer(AW - 768),
                              gm_w, gr_w], axis=0)
    w_uq_p = fw["mla_w_uq"].reshape(QW, MLA_Q_RANK)
    ukv = fw["mla_w_ukv"].transpose(1, 0, 2)
    w_kv_p = jnp.concatenate([_pad_last(ukv[:, :, :MLA_NOPE], HP).reshape(MLA_KV_RANK, QW),
                              _pad_last(ukv[:, :, MLA_NOPE:], HP).reshape(MLA_KV_RANK, QW)], axis=1)
    w_bm_p = jnp.pad(_cols_of(fw["w_branch_mla"]).reshape(MLA_HEADS, MLA_V, D),
                     ((0, 0), (0, HP - MLA_V), (0, 0))).reshape(QW, D)
    w_br, w_o = _cols_of(fw["w_branch_ret"]), fw["w_out"].reshape(D, D)
    tab_mla = _rope_table(MLA_NOPE, MLA_ROPE // 2)
    tab_ret = _rope_table(0, RET_DK // 2)

    proj, a1 = _rms_matmul(h1, mix_pre_w, w_in_p, name="mixer_in_proj")
    q, k, v = _mla_prep_fwd(proj, pos, mla_q_norm_w, mla_kv_norm_w, w_uq_p, w_kv_p, tab_mla, name="mla_prep_fwd")
    o, lse, w1b, w2b = _flash_fwd(q, k, v, name="mla_attn_fwd_gather_ffn2",
                                  exchange=_Gather([send_w1(ffn2_w1), send_w2(ffn2_w2)]))
    w2b = w2b.reshape(N_DEV // 2, 2 * hp, D)
    ypre, yn, rprev = _ret_fwd(proj, pos, tab_ret, name="retention_fwd")
    omla, oret, m, h2 = _merge_fwd(o, yn, proj, ret_gn_w, w_bm_p, w_br, w_o, h1, mix_post_w, name="merge_fwd")
    u2, f2, _, a2, dy, lossp = _ffn_fwd(h2, ffn2_pre_w, w1b, w2b, ffn2_post_w, tgt, name="ffn2_fwd_loss")

    def grad(x, dy, tag, after=None):
        return _matmul_tn(x if x.ndim == 3 else x[None], dy if dy.ndim == 3 else dy[None], name=tag, after=after)

    g2, du2, df2, dh2, gpost2, gpre2 = _ffn_bwd(dy, f2, ffn2_post_w, h2, ffn2_pre_w, u2, w2b, w1b, name="ffn2_bwd")
    dw1b, = grad(du2.reshape(N_DEV, T, 2 * hp), a2, "ffn2_dw1")
    dw2b = grad(g2, df2, "ffn2_dw2")[0].reshape(N_DEV, hp, D)
    (dgm, dgr, do, delta, drg, dyn, gpostm, ggn, dw_out, dw_bm_p, dw_br) = _merge_bwd(
        dh2, m, mix_post_w, omla, oret, proj, yn, ret_gn_w, o, w_o, w_bm_p, w_br, name="merge_bwd")
    sc_ffn2 = _SplitScatter(_Scatter([dw1b, dw2b]), "scatter_ffn2")
    dq, dk, dv = _flash_bwd(q, k, v, do, lse, delta, name="mla_attn_bwd", after=sc_ffn2.start())
    da, gqn, gkvn, dw_uq_p, dw_kv_p = _mla_prep_bwd(dq, dk, dv, proj, pos, mla_q_norm_w, mla_kv_norm_w, w_uq_p, w_kv_p, tab_mla, name="mla_prep_bwd")
    drq, drk, drv = _ret_bwd(dyn, ypre, proj, pos, tab_ret, rprev, name="retention_bwd")
    dproj = jnp.concatenate([drq, drk, drv, drg, da, dgm, dgr], axis=1)
    dw_in_p = grad(dproj, a1, "dw_in")[0][0]

    dw_uq = dw_uq_p.reshape(MLA_HEADS, HP, MLA_Q_RANK)[:, :uq_w]
    dkp = dw_kv_p[:, :QW].reshape(MLA_KV_RANK, MLA_HEADS, HP)[:, :, :MLA_NOPE]
    dvp = dw_kv_p[:, QW:].reshape(MLA_KV_RANK, MLA_HEADS, HP)[:, :, :MLA_V]
    dw_ukv = jnp.concatenate([dkp, dvp], axis=2).transpose(1, 0, 2)
    dw_bm = dw_bm_p.reshape(MLA_HEADS, HP, D)[:, :MLA_V].reshape(MLA_HEADS * MLA_V, D)
    small_mixer_grads = [dw_uq, dw_ukv, _col_shards(dw_bm), _col_shards(dw_br), dw_out.reshape(N_DEV, D // N_DEV, D)]
    sc_small = _SplitScatter(_Scatter(small_mixer_grads), "scatter_mixer_small")
    dh1, gmixpre = _proj_bwd(dproj, w_in_p, h1, mix_pre_w, dh2, name="mixer_in_bwd", after=sc_small.start())
    unhead = lambda a, h, wd: a.reshape(h, HP, D)[:, :wd].reshape(h * wd, D)
    c0 = 4 * RW
    dw_in = jnp.concatenate([
        dw_in_p[c0:c0 + 384], dw_in_p[c0 + 384:c0 + 640], dw_in_p[c0 + 640 + MLA_NOPE:c0 + 640 + MLA_NOPE + MLA_ROPE],
        unhead(dw_in_p[0:RW], RET_HEADS, RET_DK), unhead(dw_in_p[RW:2 * RW], RET_HEADS, RET_DK),
        dw_in_p[2 * RW:3 * RW], dw_in_p[3 * RW:4 * RW],
        dw_in_p[PROJ_FIXED:PROJ_FIXED + D], dw_in_p[PROJ_FIXED + D:PROJ_FIXED + 2 * D]], axis=0).reshape(N_DEV, -1, D)
    sc_w_in = _SplitScatter(_Scatter([dw_in]), "scatter_w_in")
    g1, du1, df1, dx, gpost1, gpre1 = _ffn_bwd(
        dh1, f1, ffn1_post_w, h0, ffn1_pre_w, u1, w2a, w1a, name="ffn1_bwd", after=sc_w_in.start())
    dw2a = grad(g1, df1, "ffn1_dw2")[0].reshape(N_DEV, hp, D)
    sc_dw2a = _SplitScatter(_Scatter([dw2a]), "scatter_ffn1_dw2")
    dw1a, = grad(du1.reshape(N_DEV, T, 2 * hp), a0, "ffn1_dw1", after=sc_dw2a.start())

    small_g = {"ffn1_pre_w": gpre1, "ffn1_post_w": gpost1, "mix_pre_w": gmixpre, "mla_q_norm_w": gqn,
               "mla_kv_norm_w": gkvn, "ret_gn_w": ggn, "mix_post_w": gpostm, "ffn2_pre_w": gpre2, "ffn2_post_w": gpost2}
    sc_last = _SplitScatter(_Scatter([dw1a], whole=[small_g[nm] for nm, *_ in small] + [lossp]), "scatter_ffn1_dw1")
    token = sc_last.start()
    recv_ffn2 = sc_ffn2.wait([token])
    recv_mixer = sc_w_in.wait([token]) + sc_small.wait([token])
    recv_w2a, = sc_dw2a.wait([token])
    parts = dict(zip(mixer, recv_mixer))
    parts.update(ffn1_w2=recv_w2a, ffn2_w1=recv_ffn2[0], ffn2_w2=recv_ffn2[1])
    as_is = (lambda a: a, lambda p: p[:, None], lambda a: a)
    views = {nm: as_is for nm, *_ in big}
    for nm in ("ffn1_w1", "ffn2_w1"):
        views[nm] = (lambda a: rows_view(a).reshape(2, half, D), lambda p: p.reshape(N_DEV, 2, hp, D),
                     lambda a: a.reshape(2 * half, D).T[None])
    for nm in ("w_in", "mla_w_uq"):
        views[nm] = (lambda a: rows_view(a)[None], lambda p: p[:, None], lambda a: a[0].T[None])

    def update(nm, w, m_, v_, after):
        to_view, parts_view, back = views[nm]
        return [back(a) for a in _adamw(to_view(w), parts_view(parts[nm]), to_view(m_), to_view(v_), after,
                                        name="adamw_" + nm)]

    big_out = {nm: update(nm, w, m_, v_, token) for nm, w, m_, v_ in big if nm != "ffn1_w1"}
    recv_w1a, *small_parts, loss_parts = sc_last.wait([d[0] for d in big_out.values()])
    loss = jnp.sum(loss_parts[:, ::8, 0])
    parts["ffn1_w1"] = recv_w1a
    big_out["ffn1_w1"] = update("ffn1_w1", ffn1_w1, m_ffn1_w1, v_ffn1_w1, jnp.zeros((8, LANES), F32))
    small_out = _adamw_vectors([w for _, w, _, _ in small], small_parts, [a for _, _, a, _ in small],
                               [a for _, _, _, a in small], name="adamw_replicated")

    order = ["ffn1_pre_w", "ffn1_w1", "ffn1_w2", "ffn1_post_w", "mix_pre_w", "w_in", "mla_q_norm_w", "mla_w_uq",
             "mla_kv_norm_w", "mla_w_ukv", "ret_gn_w", "w_branch_mla", "w_branch_ret", "w_out", "mix_post_w",
             "ffn2_pre_w", "ffn2_w1", "ffn2_w2", "ffn2_post_w"]
    outs = [loss, dx[None]]
    for i in range(4):
        both = {nm: big_out[nm][i] for nm in big_out}
        both.update({nm: small_out[4 * j + i] for j, (nm, *_) in enumerate(small)})
        outs += [both[nm] for nm in order]
    return tuple(outs)
```

```python
import math

import numpy as np
import jax
import jax.numpy as jnp
from jax import lax
from jax.experimental import pallas as pl
from jax.experimental.pallas import tpu as pltpu

F32, BF16 = jnp.float32, jnp.bfloat16

MLA_HEADS, MLA_NOPE, MLA_ROPE, MLA_V = 8, 64, 32, 64
MLA_Q_RANK, MLA_KV_RANK = 384, 256
RET_HEADS, RET_DK, RET_DV = 4, 64, 128
ROPE_BASE, NORM_EPS, GN_EPS = 10000.0, 1e-6, 1e-6
ADAM_LR, ADAM_B1, ADAM_B2, ADAM_EPS, ADAM_WD, ADAM_STEP = 0.001, 0.9, 0.999, 1e-08, 0.01, 10
ATTN_SCALE = 1.0 / math.sqrt(MLA_NOPE + MLA_ROPE)

N_DEV = 8
LANES = 128
HP = LANES
QW = MLA_HEADS * HP
RW = RET_HEADS * HP
AW = 1024
PROJ_FIXED = 4 * RW + AW
NEG = -1e30

TOKEN_TILE = 512
ATTN_TILE = 1024
ATTN_CHAINS = 2
FFN_CHAINS = 2
RET_TILE = 256
PROJ_TILE_CAP = 2560
PROJ_TOKEN_TILE = 1024
GRAD_TILE_CAP = 1408
GRAD_TOKEN_TILE = 4096
ADAM_BLOCK_CAP = 192 * 1024
MERGE_TILE = 256
VMEM_LIMIT = 56 * 1024 * 1024


def _tile(n, cap, mult=LANES):
    if n <= cap:
        return n
    best = None
    for t in range(mult, cap + 1, mult):
        if n % t == 0:
            best = t
    assert best is not None, (n, cap, mult)
    return best


def _params(sem):
    return pltpu.CompilerParams(dimension_semantics=sem, vmem_limit_bytes=VMEM_LIMIT)


def _dot(a, b):
    return lax.dot_general(a, b, (((1,), (0,)), ((), ())), preferred_element_type=F32)


def _dot_nt(a, b):
    return lax.dot_general(a, b, (((1,), (1,)), ((), ())), preferred_element_type=F32)


def _dot_tn(a, b):
    return lax.dot_general(a, b, (((0,), (0,)), ((), ())), preferred_element_type=F32)


def _sigmoid(x):
    return pl.reciprocal(1.0 + jnp.exp(-x), approx=True)


def _rms_fwd(x, w):
    r = lax.rsqrt(jnp.mean(x * x, axis=-1, keepdims=True) + NORM_EPS)
    return x * r * w


def _rms_bwd(x, w, dy):
    r = lax.rsqrt(jnp.mean(x * x, axis=-1, keepdims=True) + NORM_EPS)
    xh = x * r
    g = dy * w
    dx = r * (g - xh * jnp.mean(g * xh, axis=-1, keepdims=True))
    return dx, jnp.sum(dy * xh, axis=0, keepdims=True)


def _rope_table(first, half):
    inv = (np.float32(ROPE_BASE) ** (-(np.arange(half, dtype=np.float32) / np.float32(half)))).astype(np.float32)
    tab = np.zeros((8, LANES), np.float32)
    tab[0, first:first + half] = inv
    tab[0, first + half:first + 2 * half] = inv
    tab[1, first:first + half] = -1.0
    tab[2, first + half:first + 2 * half] = 1.0
    return jnp.asarray(tab)


def _rope_cs(pos, tab_ref):
    ang = pos * tab_ref[0:1, :]
    s = jnp.sin(ang)
    return jnp.cos(ang), s * tab_ref[1:2, :], s * tab_ref[2:3, :]


def _rope(x, cs, half, inverse=False):
    c, s1, s2 = cs
    a = pltpu.roll(x, LANES - half, 1) * s1 + pltpu.roll(x, half, 1) * s2
    return x * c - a if inverse else x * c + a


def _call(body, *, name, grid, in_specs, out_specs, out_shape, scratch_shapes, args, exchange=None, after=None):
    sem = ("arbitrary",) * len(grid)
    anyspec = pl.BlockSpec(memory_space=pl.ANY)
    if exchange is None and after is not None:
        n_own = len(in_specs)

        def behind(*refs):
            body(*refs[:n_own], *refs[n_own + 1:])

        return pl.pallas_call(behind, name=name, grid=grid, in_specs=list(in_specs) + [anyspec], out_specs=out_specs,
                              out_shape=out_shape, scratch_shapes=scratch_shapes, compiler_params=_params(sem))(*args, after)
    if exchange is None:
        return pl.pallas_call(body, name=name, grid=grid, in_specs=in_specs, out_specs=out_specs,
                              out_shape=out_shape, scratch_shapes=scratch_shapes, compiler_params=_params(sem))(*args)
    n_in, n_out, e = len(in_specs), len(out_specs), exchange.n
    total = math.prod(grid)

    def carried(*refs):
        own = refs[:n_in] + refs[n_in + e:n_in + e + n_out] + refs[n_in + 2 * e + n_out:len(refs) - 3]
        ex_refs = (refs[n_in:n_in + e], refs[n_in + e + n_out:n_in + 2 * e + n_out], refs[len(refs) - 3:])
        step = pl.program_id(0)
        for d in range(1, len(grid)):
            step = step * grid[d] + pl.program_id(d)

        @pl.when(step == 0)
        def _():
            exchange.phase(0, *ex_refs)

        @pl.when(step == (3 * total) // 4)
        def _():
            exchange.phase(1, *ex_refs)

        body(*own)

        @pl.when(step == total - 1)
        def _():
            exchange.phase(2, *ex_refs)

    return pl.pallas_call(
        carried, name=name, grid=grid, in_specs=list(in_specs) + [anyspec] * e,
        out_specs=list(out_specs) + [anyspec] * e, out_shape=list(out_shape) + exchange.out_shape,
        scratch_shapes=list(scratch_shapes) + exchange.scratch, compiler_params=_params(sem),
    )(*args, *exchange.operands)


def _ffn_fwd(h, pre_w, w1, w2, post_w, target, *, name, exchange=None):
    T, D = h.shape
    nk, ck = w2.shape[0], w2.shape[1]
    tT = min(TOKEN_TILE, T)
    nT = T // tT
    with_loss = target is not None

    def body(*refs):
        if with_loss:
            (h_ref, pre_ref, w1g_ref, w1u_ref, w2_ref, post_ref, tgt_ref,
             u_ref, f_ref, ho_ref, a_s, dy_ref, loss_ref, acc) = refs
        else:
            (h_ref, pre_ref, w1g_ref, w1u_ref, w2_ref, post_ref,
             u_ref, f_ref, ho_ref, a_s, acc) = refs
        k = pl.program_id(1)

        @pl.when(k == 0)
        def _():
            a_s[...] = _rms_fwd(h_ref[...], pre_ref[...]).astype(BF16)
            acc[...] = jnp.zeros_like(acc)

        for c in range(FFN_CHAINS):
            rs = slice(c * (tT // FFN_CHAINS), (c + 1) * (tT // FFN_CHAINS))
            a = a_s[rs, :]
            ug = _dot_nt(a, w1g_ref[...])
            uu = _dot_nt(a, w1u_ref[...])
            u_ref[0, rs, :] = ug.astype(BF16)
            u_ref[1, rs, :] = uu.astype(BF16)
            acc[rs, :] += _dot((ug * _sigmoid(ug) * uu).astype(BF16), w2_ref[...])

        @pl.when(k == nk - 1)
        def _():
            f = acc[...]
            f_ref[...] = f
            ho = h_ref[...] + 0.5 * _rms_fwd(f, post_ref[...])
            ho_ref[...] = ho
            if with_loss:
                e = ho - tgt_ref[...]
                dy_ref[...] = e * (1.0 / D)
                loss_ref[...] = jnp.full(loss_ref.shape, (0.5 / D) * jnp.sum(e * e), F32)

    row = pl.BlockSpec((tT, D), lambda i, k: (i, 0))
    vec = pl.BlockSpec((1, D), lambda i, k: (0, 0))
    in_specs = [row, vec,
                pl.BlockSpec((None, ck, D), lambda i, k: (k, 0, 0)),
                pl.BlockSpec((None, ck, D), lambda i, k: (nk + k, 0, 0)),
                pl.BlockSpec((None, ck, D), lambda i, k: (k, 0, 0)),
                vec]
    out_shape = [jax.ShapeDtypeStruct((2, nk, T, ck), BF16),
                 jax.ShapeDtypeStruct((T, D), F32),
                 jax.ShapeDtypeStruct((T, D), F32),
                 jax.ShapeDtypeStruct((T, D), BF16)]
    out_specs = [pl.BlockSpec((2, None, tT, ck), lambda i, k: (0, k, i, 0)), row, row, row]
    args = [h, pre_w, w1, w1, w2, post_w]
    if with_loss:
        in_specs.append(row)
        args.append(target)
        out_shape += [jax.ShapeDtypeStruct((T, D), F32), jax.ShapeDtypeStruct((nT * 8, LANES), F32)]
        out_specs += [row, pl.BlockSpec((8, LANES), lambda i, k: (i, 0))]
    return _call(body, name=name, grid=(nT, nk), in_specs=in_specs, out_specs=out_specs, out_shape=out_shape,
                 scratch_shapes=[pltpu.VMEM((tT, D), F32)], args=args, exchange=exchange)


def _ffn_bwd(dho, f, post_w, h, pre_w, u, w2, w1, *, name, exchange=None, after=None):
    T, D = h.shape
    nk, ck = w2.shape[0], w2.shape[1]
    tT = min(TOKEN_TILE, T)
    nT = T // tT

    def body(dho_ref, f_ref, post_ref, h_ref, pre_ref, u_ref, w2_ref, w1g_ref, w1u_ref,
             g_ref, du_ref, df_s, dh_ref, gpost_ref, gpre_ref, da_acc):
        i, k = pl.program_id(0), pl.program_id(1)

        @pl.when(jnp.logical_and(i == 0, k == 0))
        def _():
            gpost_ref[...] = jnp.zeros_like(gpost_ref)
            gpre_ref[...] = jnp.zeros_like(gpre_ref)

        @pl.when(k == 0)
        def _():
            dx, dw = _rms_bwd(f_ref[...], post_ref[...], 0.5 * dho_ref[...])
            df_s[...] = dx.astype(BF16)
            gpost_ref[...] += dw
            da_acc[...] = jnp.zeros_like(da_acc)

        groups = [slice(c * (tT // FFN_CHAINS), (c + 1) * (tT // FFN_CHAINS)) for c in range(FFN_CHAINS)]
        dgs = [_dot_nt(df_s[rs, :], w2_ref[...]) for rs in groups]
        for rs, dg in zip(groups, dgs):
            ug = u_ref[0, rs, :].astype(F32)
            uu = u_ref[1, rs, :].astype(F32)
            sg = _sigmoid(ug)
            sl = ug * sg
            g_ref[rs, :] = (sl * uu).astype(BF16)
            dug = (dg * uu * (sg + sl * (1.0 - sg))).astype(BF16)
            duu = (dg * sl).astype(BF16)
            du_ref[0, rs, :] = dug
            du_ref[1, rs, :] = duu
            da_acc[rs, :] += _dot(dug, w1g_ref[...]) + _dot(duu, w1u_ref[...])

        @pl.when(k == nk - 1)
        def _():
            dx, dw = _rms_bwd(h_ref[...], pre_ref[...], da_acc[...])
            dh_ref[...] = dho_ref[...] + dx
            gpre_ref[...] += dw

    row = pl.BlockSpec((tT, D), lambda i, k: (i, 0))
    vec = pl.BlockSpec((1, D), lambda i, k: (0, 0))
    return _call(
        body, name=name, grid=(nT, nk),
        in_specs=[row, row, vec, row, vec,
                  pl.BlockSpec((2, None, tT, ck), lambda i, k: (0, k, i, 0)),
                  pl.BlockSpec((None, ck, D), lambda i, k: (k, 0, 0)),
                  pl.BlockSpec((None, ck, D), lambda i, k: (k, 0, 0)),
                  pl.BlockSpec((None, ck, D), lambda i, k: (nk + k, 0, 0))],
        out_specs=[pl.BlockSpec((None, tT, ck), lambda i, k: (k, i, 0)),
                   pl.BlockSpec((2, None, tT, ck), lambda i, k: (0, k, i, 0)),
                   row, row, vec, vec],
        out_shape=[jax.ShapeDtypeStruct((nk, T, ck), BF16),
                   jax.ShapeDtypeStruct((2, nk, T, ck), BF16),
                   jax.ShapeDtypeStruct((T, D), BF16),
                   jax.ShapeDtypeStruct((T, D), F32),
                   jax.ShapeDtypeStruct((1, D), F32),
                   jax.ShapeDtypeStruct((1, D), F32)],
        scratch_shapes=[pltpu.VMEM((tT, D), F32)],
        args=(dho, f, post_w, h, pre_w, u, w2, w1, w1), exchange=exchange, after=after)


def _matmul_tn(x, dy, *, name, exchange=None, after=None):
    Px, T, K = x.shape
    Py, _, N = dy.shape
    P = max(Px, Py)
    tT, tK, tN = min(GRAD_TOKEN_TILE, T), _tile(K, GRAD_TILE_CAP), _tile(N, GRAD_TILE_CAP)
    nt = T // tT

    def body(x_ref, dy_ref, o_ref, acc):
        t = pl.program_id(3)

        @pl.when(t == 0)
        def _():
            acc[...] = jnp.zeros_like(acc)

        acc[...] += _dot_tn(x_ref[...], dy_ref[...])

        @pl.when(t == nt - 1)
        def _():
            o_ref[...] = acc[...].astype(BF16)

    return _call(
        body, name=name, grid=(P, K // tK, N // tN, nt),
        in_specs=[pl.BlockSpec((None, tT, tK), lambda p, a, b, t: (p if Px > 1 else 0, t, a)),
                  pl.BlockSpec((None, tT, tN), lambda p, a, b, t: (p if Py > 1 else 0, t, b))],
        out_specs=[pl.BlockSpec((None, tK, tN), lambda p, a, b, t: (p, a, b))],
        out_shape=[jax.ShapeDtypeStruct((P, K, N), BF16)],
        scratch_shapes=[pltpu.VMEM((tK, tN), F32)], args=(x, dy), exchange=exchange, after=after)


def _rms_matmul(h, wn, w, *, name):
    T, D = h.shape
    N = w.shape[0]
    tT, tN = min(PROJ_TOKEN_TILE, T), _tile(N, PROJ_TILE_CAP)

    def body(h_ref, wn_ref, w_ref, y_ref, a_ref):
        @pl.when(pl.program_id(1) == 0)
        def _():
            a_ref[...] = _rms_fwd(h_ref[...], wn_ref[...]).astype(BF16)

        y_ref[...] = _dot_nt(a_ref[...], w_ref[...]).astype(BF16)

    return pl.pallas_call(
        body, name=name, grid=(T // tT, N // tN),
        in_specs=[pl.BlockSpec((tT, D), lambda i, j: (i, 0)),
                  pl.BlockSpec((1, D), lambda i, j: (0, 0)),
                  pl.BlockSpec((tN, D), lambda i, j: (j, 0))],
        out_specs=[pl.BlockSpec((tT, tN), lambda i, j: (i, j)),
                   pl.BlockSpec((tT, D), lambda i, j: (i, 0))],
        out_shape=[jax.ShapeDtypeStruct((T, N), BF16), jax.ShapeDtypeStruct((T, D), BF16)],
        compiler_params=_params(("parallel", "arbitrary")),
    )(h, wn, w)


def _proj_bwd(dproj, w, h, wn, dres, *, name, exchange=None, after=None):
    T, D = h.shape
    N = w.shape[0]
    tT, tN = min(PROJ_TOKEN_TILE, T), _tile(N, PROJ_TILE_CAP)
    nn = N // tN

    def body(dp_ref, w_ref, h_ref, wn_ref, dres_ref, dh_ref, gw_ref, acc):
        i, j = pl.program_id(0), pl.program_id(1)

        @pl.when(jnp.logical_and(i == 0, j == 0))
        def _():
            gw_ref[...] = jnp.zeros_like(gw_ref)

        @pl.when(j == 0)
        def _():
            acc[...] = jnp.zeros_like(acc)

        acc[...] += _dot(dp_ref[...], w_ref[...])

        @pl.when(j == nn - 1)
        def _():
            dx, dw = _rms_bwd(h_ref[...], wn_ref[...], acc[...])
            dh_ref[...] = dres_ref[...] + dx
            gw_ref[...] += dw

    row = pl.BlockSpec((tT, D), lambda i, j: (i, 0))
    vec = pl.BlockSpec((1, D), lambda i, j: (0, 0))
    return _call(
        body, name=name, grid=(T // tT, nn),
        in_specs=[pl.BlockSpec((tT, tN), lambda i, j: (i, j)),
                  pl.BlockSpec((tN, D), lambda i, j: (j, 0)), row, vec, row],
        out_specs=[row, vec],
        out_shape=[jax.ShapeDtypeStruct((T, D), F32), jax.ShapeDtypeStruct((1, D), F32)],
        scratch_shapes=[pltpu.VMEM((tT, D), F32)], args=(dproj, w, h, wn, dres), exchange=exchange, after=after)


def _mla_prep_fwd(proj, pos, qn_w, kvn_w, w_uq, w_kv, tab, *, name):
    T = proj.shape[0]
    tT = min(TOKEN_TILE, T)
    a_blk = PROJ_FIXED // AW - 1

    def body(a_ref, pos_ref, qnw_ref, kvnw_ref, wuq_ref, wkv_ref, tab_ref,
             q_ref, k_ref, v_ref):
        cq = a_ref[:, 0:MLA_Q_RANK].astype(F32)
        ckv = a_ref[:, MLA_Q_RANK:MLA_Q_RANK + MLA_KV_RANK].astype(F32)
        kr = a_ref[:, 640:768].astype(F32)
        qn = _rms_fwd(cq, qnw_ref[...]).astype(BF16)
        kvn = _rms_fwd(ckv, kvnw_ref[...]).astype(BF16)
        cs = _rope_cs(pos_ref[...], tab_ref)
        q = _dot_nt(qn, wuq_ref[...])
        kv = _dot(kvn, wkv_ref[...])
        krr = _rope(kr, cs, MLA_ROPE // 2)
        for hd in range(MLA_HEADS):
            sl = slice(hd * HP, (hd + 1) * HP)
            q_ref[:, sl] = (_rope(q[:, sl], cs, MLA_ROPE // 2) * ATTN_SCALE).astype(BF16)
            k_ref[:, sl] = (kv[:, sl] + krr).astype(BF16)
        v_ref[...] = kv[:, QW:].astype(BF16)

    def full(r, c):
        return pl.BlockSpec((r, c), lambda i: (0, 0))

    def rows(c):
        return pl.BlockSpec((tT, c), lambda i: (i, 0))

    return pl.pallas_call(
        body, name=name, grid=(T // tT,),
        in_specs=[pl.BlockSpec((tT, AW), lambda i: (i, a_blk)), rows(1),
                  full(1, MLA_Q_RANK), full(1, MLA_KV_RANK),
                  full(QW, MLA_Q_RANK), full(MLA_KV_RANK, 2 * QW), full(8, LANES)],
        out_specs=[rows(QW), rows(QW), rows(QW)],
        out_shape=[jax.ShapeDtypeStruct((T, QW), BF16)] * 3,
        compiler_params=_params(("parallel",)),
    )(proj, pos, qn_w, kvn_w, w_uq, w_kv, tab)


def _mla_prep_bwd(dq, dk, dv, proj, pos, qn_w, kvn_w, w_uq, w_kv, tab, *, name):
    T = proj.shape[0]
    tT = min(TOKEN_TILE, T)
    nT = T // tT
    a_blk = PROJ_FIXED // AW - 1

    def body(dq_ref, dk_ref, dv_ref, a_ref, pos_ref, qnw_ref, kvnw_ref, wuq_ref, wkv_ref, tab_ref,
             da_ref, gqn_ref, gkvn_ref, dwuq_ref, dwkv_ref, dql_ref, dkvl_ref, acc_uq, acc_kv):
        @pl.when(pl.program_id(0) == 0)
        def _():
            gqn_ref[...] = jnp.zeros_like(gqn_ref)
            gkvn_ref[...] = jnp.zeros_like(gkvn_ref)
            acc_uq[...] = jnp.zeros_like(acc_uq)
            acc_kv[...] = jnp.zeros_like(acc_kv)

        cs = _rope_cs(pos_ref[...], tab_ref)
        dkr = jnp.zeros((tT, HP), F32)
        for hd in range(MLA_HEADS):
            sl = slice(hd * HP, (hd + 1) * HP)
            dql_ref[:, sl] = (_rope(dq_ref[:, sl], cs, MLA_ROPE // 2, inverse=True) * ATTN_SCALE).astype(BF16)
            dkh = dk_ref[:, sl]
            dkr = dkr + dkh
            dkvl_ref[:, sl] = dkh.astype(BF16)
        dkvl_ref[:, QW:] = dv_ref[...]
        dqn = _dot(dql_ref[...], wuq_ref[...])
        dkvn = _dot_nt(dkvl_ref[...], wkv_ref[...])
        cq = a_ref[:, 0:MLA_Q_RANK].astype(F32)
        ckv = a_ref[:, MLA_Q_RANK:MLA_Q_RANK + MLA_KV_RANK].astype(F32)
        dcq, gq = _rms_bwd(cq, qnw_ref[...], dqn)
        dckv, gkv = _rms_bwd(ckv, kvnw_ref[...], dkvn)
        gqn_ref[...] += gq
        gkvn_ref[...] += gkv
        da_ref[:, 0:MLA_Q_RANK] = dcq.astype(BF16)
        da_ref[:, MLA_Q_RANK:MLA_Q_RANK + MLA_KV_RANK] = dckv.astype(BF16)
        da_ref[:, 640:768] = _rope(dkr, cs, MLA_ROPE // 2, inverse=True).astype(BF16)
        da_ref[:, 768:AW] = jnp.zeros((tT, AW - 768), BF16)
        acc_uq[...] += _dot_tn(dql_ref[...], _rms_fwd(cq, qnw_ref[...]).astype(BF16))
        acc_kv[...] += _dot_tn(_rms_fwd(ckv, kvnw_ref[...]).astype(BF16), dkvl_ref[...])

        @pl.when(pl.program_id(0) == nT - 1)
        def _():
            dwuq_ref[...] = acc_uq[...].astype(BF16)
            dwkv_ref[...] = acc_kv[...].astype(BF16)

    def full(r, c):
        return pl.BlockSpec((r, c), lambda i: (0, 0))

    def rows(c):
        return pl.BlockSpec((tT, c), lambda i: (i, 0))

    return pl.pallas_call(
        body, name=name, grid=(nT,),
        in_specs=[rows(QW), rows(QW), rows(QW), pl.BlockSpec((tT, AW), lambda i: (i, a_blk)), rows(1),
                  full(1, MLA_Q_RANK), full(1, MLA_KV_RANK),
                  full(QW, MLA_Q_RANK), full(MLA_KV_RANK, 2 * QW), full(8, LANES)],
        out_specs=[rows(AW), full(1, MLA_Q_RANK), full(1, MLA_KV_RANK),
                   full(QW, MLA_Q_RANK), full(MLA_KV_RANK, 2 * QW)],
        out_shape=[jax.ShapeDtypeStruct((T, AW), BF16),
                   jax.ShapeDtypeStruct((1, MLA_Q_RANK), F32), jax.ShapeDtypeStruct((1, MLA_KV_RANK), F32),
                   jax.ShapeDtypeStruct((QW, MLA_Q_RANK), BF16), jax.ShapeDtypeStruct((MLA_KV_RANK, 2 * QW), BF16)],
        scratch_shapes=[pltpu.VMEM((tT, QW), BF16), pltpu.VMEM((tT, 2 * QW), BF16),
                        pltpu.VMEM((QW, MLA_Q_RANK), F32), pltpu.VMEM((MLA_KV_RANK, 2 * QW), F32)],
        compiler_params=_params(("arbitrary",)),
    )(dq, dk, dv, proj, pos, qn_w, kvn_w, w_uq, w_kv, tab)


def _flash_fwd(q, k, v, *, name, exchange=None):
    T = q.shape[0]
    H = q.shape[1] // HP
    tq = min(ATTN_TILE, T)
    nq = T // tq

    sub = tq // ATTN_CHAINS

    def body(q_ref, k_ref, v_ref, o_ref, lse_ref):
        qi = pl.program_id(1)
        qs = [q_ref[c * sub:(c + 1) * sub, :] for c in range(ATTN_CHAINS)]

        def update(carry, off, masked):
            nks = [(c + 1) * sub if masked else tq for c in range(ATTN_CHAINS)]
            scores = [_dot_nt(qs[c], k_ref[pl.ds(off, nks[c]), :]) for c in range(ATTN_CHAINS)]
            out = []
            for c in range(ATTN_CHAINS):
                m_prev, l_prev, acc = carry[c]
                nk, s = nks[c], scores[c]
                vb = v_ref[pl.ds(off, nk), :]
                if masked:
                    rows = lax.broadcasted_iota(jnp.int32, (sub, nk), 0) + c * sub
                    s = jnp.where(rows >= lax.broadcasted_iota(jnp.int32, (sub, nk), 1), s, NEG)
                m_new = jnp.maximum(m_prev, jnp.max(s, axis=1, keepdims=True))
                alpha = jnp.exp(m_prev - m_new)
                p = jnp.exp(s - m_new)
                out.append((m_new, alpha * l_prev + jnp.sum(p, axis=1, keepdims=True),
                            alpha * acc + _dot(p.astype(BF16), vb)))
            return tuple(out)

        init = tuple((jnp.full((sub, 1), NEG, F32), jnp.zeros((sub, 1), F32), jnp.zeros((sub, HP), F32))
                     for _ in range(ATTN_CHAINS))
        carry = lax.fori_loop(0, qi, lambda j, cr: update(cr, pl.multiple_of(j * tq, tq), False), init)
        carry = update(carry, pl.multiple_of(qi * tq, tq), True)
        for c in range(ATTN_CHAINS):
            m_fin, l_fin, acc = carry[c]
            o_ref[c * sub:(c + 1) * sub, :] = (acc / l_fin).astype(BF16)
            lse_ref[c * sub:(c + 1) * sub, :] = jnp.broadcast_to(m_fin + jnp.log(l_fin), (sub, HP))

    qspec = pl.BlockSpec((tq, HP), lambda h, i: (i, h))
    kspec = pl.BlockSpec((T, HP), lambda h, i: (0, h))
    return _call(
        body, name=name, grid=(H, nq),
        in_specs=[qspec, kspec, kspec], out_specs=[qspec, qspec],
        out_shape=[jax.ShapeDtypeStruct((T, H * HP), BF16), jax.ShapeDtypeStruct((T, H * HP), F32)],
        scratch_shapes=[], args=(q, k, v), exchange=exchange)


def _flash_bwd(q, k, v, do, lse, delta, *, name, exchange=None, after=None):
    T = q.shape[0]
    H = q.shape[1] // HP
    tq = min(ATTN_TILE, T)
    nq = T // tq
    sub = tq // ATTN_CHAINS

    def body(k_ref, v_ref, q_ref, do_ref, lse_ref, dl_ref, dq_ref, dk_ref, dv_ref):
        ki = pl.program_id(1)

        @pl.when(ki == 0)
        def _():
            dq_ref[...] = jnp.zeros_like(dq_ref)

        def grow(a):
            return a if a.shape[0] == tq else jnp.concatenate([a, jnp.zeros((tq - a.shape[0], HP), F32)], axis=0)

        def step(carry, j, masked):
            dk_acc, dv_acc = carry
            nks = [(c + 1) * sub if masked else tq for c in range(ATTN_CHAINS)]
            rws = [pl.ds(pl.multiple_of(j * tq + c * sub, sub), sub) for c in range(ATTN_CHAINS)]
            scores = [_dot_nt(q_ref[rws[c], :], k_ref[0:nks[c], :]) for c in range(ATTN_CHAINS)]
            dps = [_dot_nt(do_ref[rws[c], :], v_ref[0:nks[c], :]) for c in range(ATTN_CHAINS)]
            for c in range(ATTN_CHAINS):
                rows, nk, s, dp = rws[c], nks[c], scores[c], dps[c]
                kb = k_ref[0:nk, :]
                qb = q_ref[rows, :]
                dob = do_ref[rows, :]
                if masked:
                    ri = lax.broadcasted_iota(jnp.int32, (sub, nk), 0) + c * sub
                    s = jnp.where(ri >= lax.broadcasted_iota(jnp.int32, (sub, nk), 1), s, NEG)
                p = jnp.exp(s - lse_ref[rows, 0:1])
                dv_acc = dv_acc + grow(_dot_tn(p.astype(BF16), dob))
                ds = (p * (dp - dl_ref[rows, 0:1])).astype(BF16)
                dk_acc = dk_acc + grow(_dot_tn(ds, qb))
                dq_ref[rows, :] += _dot(ds, kb)
            return dk_acc, dv_acc

        carry = step((jnp.zeros((tq, HP), F32), jnp.zeros((tq, HP), F32)), ki, True)
        dk_acc, dv_acc = lax.fori_loop(ki + 1, nq, lambda j, cr: step(cr, j, False), carry)
        dk_ref[...] = dk_acc
        dv_ref[...] = dv_acc.astype(BF16)

    kspec = pl.BlockSpec((tq, HP), lambda h, j: (j, h))
    full = pl.BlockSpec((T, HP), lambda h, j: (0, h))
    return _call(
        body, name=name, grid=(H, nq),
        in_specs=[kspec, kspec, full, full, full, full], out_specs=[full, kspec, kspec],
        out_shape=[jax.ShapeDtypeStruct((T, H * HP), F32), jax.ShapeDtypeStruct((T, H * HP), F32),
                   jax.ShapeDtypeStruct((T, H * HP), BF16)],
        scratch_shapes=[], args=(k, v, q, do, lse, delta), exchange=exchange, after=after)


def _ret_consts(cc, hd):
    lg = math.log(1.0 - 2.0 ** (-5.0 - hd))
    diff = (lax.broadcasted_iota(jnp.int32, (cc, cc), 0) - lax.broadcasted_iota(jnp.int32, (cc, cc), 1)).astype(F32)
    decay = jnp.where(diff >= 0, jnp.exp(jnp.maximum(diff, 0.0) * lg), 0.0)
    idx = lax.broadcasted_iota(jnp.int32, (cc, 1), 0).astype(F32)
    zeta = jnp.exp((cc - 1.0 - idx) * lg)
    xi = jnp.exp((idx + 1.0) * lg)
    return decay, zeta, xi, math.exp(cc * lg)


def _ret_fwd(proj, pos, tab, *, name):
    T = proj.shape[0]
    cc = min(RET_TILE, T)
    n = T // cc

    def body(rq_ref, rk_ref, rv_ref, pos_ref, tab_ref, y_ref, yn_ref, rprev_ref, r_s):
        @pl.when(pl.program_id(0) == 0)
        def _():
            r_s[...] = jnp.zeros_like(r_s)

        cs = _rope_cs(pos_ref[...], tab_ref)
        for hd in range(RET_HEADS):
            sl = slice(hd * HP, (hd + 1) * HP)
            decay, zeta, xi, gc = _ret_consts(cc, hd)
            q = _rope(rq_ref[:, sl].astype(F32), cs, RET_DK // 2).astype(BF16)
            kf = _rope(rk_ref[:, sl].astype(F32), cs, RET_DK // 2) * (RET_DK ** -0.5)
            k = kf.astype(BF16)
            v = rv_ref[:, sl]
            r = r_s[hd]
            rprev_ref[0, hd] = r
            inner = (_dot_nt(q, k) * decay).astype(BF16)
            y = _dot(inner, v) + _dot(q, r.astype(BF16)) * xi
            r_s[hd] = r * gc + _dot_tn((kf * zeta).astype(BF16), v)
            y_ref[:, sl] = y
            mu = jnp.mean(y, axis=-1, keepdims=True)
            yc = y - mu
            var = jnp.mean(yc * yc, axis=-1, keepdims=True)
            yn_ref[:, sl] = (yc * lax.rsqrt(var + GN_EPS)).astype(BF16)

    def blk(j):
        return pl.BlockSpec((cc, RW), lambda i: (i, j))

    return pl.pallas_call(
        body, name=name, grid=(n,),
        in_specs=[blk(0), blk(1), blk(2), pl.BlockSpec((cc, 1), lambda i: (i, 0)),
                  pl.BlockSpec((8, LANES), lambda i: (0, 0))],
        out_specs=[blk(0), blk(0), pl.BlockSpec((1, RET_HEADS, HP, RET_DV), lambda i: (i, 0, 0, 0))],
        out_shape=[jax.ShapeDtypeStruct((T, RW), F32), jax.ShapeDtypeStruct((T, RW), BF16),
                   jax.ShapeDtypeStruct((n, RET_HEADS, HP, RET_DV), F32)],
        scratch_shapes=[pltpu.VMEM((RET_HEADS, HP, RET_DV), F32)],
        compiler_params=_params(("arbitrary",)),
    )(proj, proj, proj, pos, tab)


def _ret_bwd(dyn, y, proj, pos, tab, rprev, *, name):
    T = proj.shape[0]
    cc = min(RET_TILE, T)
    n = T // cc

    def body(dyn_ref, y_ref, rq_ref, rk_ref, rv_ref, pos_ref, tab_ref, rprev_ref,
             drq_ref, drk_ref, drv_ref, dr_s):
        @pl.when(pl.program_id(0) == 0)
        def _():
            dr_s[...] = jnp.zeros_like(dr_s)

        cs = _rope_cs(pos_ref[...], tab_ref)
        for hd in range(RET_HEADS):
            sl = slice(hd * HP, (hd + 1) * HP)
            decay, zeta, xi, gc = _ret_consts(cc, hd)
            q = _rope(rq_ref[:, sl].astype(F32), cs, RET_DK // 2).astype(BF16)
            kf = _rope(rk_ref[:, sl].astype(F32), cs, RET_DK // 2) * (RET_DK ** -0.5)
            k = kf.astype(BF16)
            v = rv_ref[:, sl]
            yv = y_ref[:, sl]
            mu = jnp.mean(yv, axis=-1, keepdims=True)
            yc = yv - mu
            rs = lax.rsqrt(jnp.mean(yc * yc, axis=-1, keepdims=True) + GN_EPS)
            yn = yc * rs
            dn = dyn_ref[:, sl]
            dy = rs * (dn - jnp.mean(dn, axis=-1, keepdims=True) - yn * jnp.mean(dn * yn, axis=-1, keepdims=True))
            dyb = dy.astype(BF16)
            dyx = (dy * xi).astype(BF16)
            dr = dr_s[hd]
            drb = dr.astype(BF16)
            inner = (_dot_nt(q, k) * decay).astype(BF16)
            da = (_dot_nt(dyb, v) * decay).astype(BF16)
            dv = _dot_tn(inner, dyb) + _dot((kf * zeta).astype(BF16), drb)
            dq = _dot(da, k) + _dot_nt(dyx, rprev_ref[0, hd].astype(BF16))
            dk = _dot_tn(da, q) + _dot_nt(v, drb) * zeta
            dr_s[hd] = dr * gc + _dot_tn(q, dyx)
            drq_ref[:, sl] = _rope(dq, cs, RET_DK // 2, inverse=True).astype(BF16)
            drk_ref[:, sl] = _rope(dk * (RET_DK ** -0.5), cs, RET_DK // 2, inverse=True).astype(BF16)
            drv_ref[:, sl] = dv.astype(BF16)

    def blk(j):
        return pl.BlockSpec((cc, RW), lambda i: (n - 1 - i, j))

    return pl.pallas_call(
        body, name=name, grid=(n,),
        in_specs=[blk(0), blk(0), blk(0), blk(1), blk(2), pl.BlockSpec((cc, 1), lambda i: (n - 1 - i, 0)),
                  pl.BlockSpec((8, LANES), lambda i: (0, 0)),
                  pl.BlockSpec((1, RET_HEADS, HP, RET_DV), lambda i: (n - 1 - i, 0, 0, 0))],
        out_specs=[blk(0), blk(0), blk(0)],
        out_shape=[jax.ShapeDtypeStruct((T, RW), BF16)] * 3,
        scratch_shapes=[pltpu.VMEM((RET_HEADS, HP, RET_DV), F32)],
        compiler_params=_params(("arbitrary",)),
    )(dyn, y, proj, proj, proj, pos, tab, rprev)


def _merge_fwd(o, yn, proj, gn_w, w_bm, w_br, w_out, h, post_w, *, name):
    T, D = h.shape
    tT = min(TOKEN_TILE, T)
    g_blk = PROJ_FIXED // D

    def body(o_ref, yn_ref, rg_ref, gm_ref, gr_ref, gnw_ref, wbm_ref, wbr_ref, wout_ref, h_ref, post_ref,
             omla_ref, oret_ref, m_ref, ho_ref):
        groups = [slice(c * (tT // FFN_CHAINS), (c + 1) * (tT // FFN_CHAINS)) for c in range(FFN_CHAINS)]
        o_mlas = [_dot(o_ref[rs, :], wbm_ref[...]) for rs in groups]
        for rs, o_mla in zip(groups, o_mlas):
            rg = rg_ref[rs, :].astype(F32)
            gated = (rg * _sigmoid(rg) * (yn_ref[rs, :].astype(F32) * gnw_ref[...])).astype(BF16)
            o_ret = _dot(gated, wbr_ref[...])
            omla_ref[rs, :] = o_mla.astype(BF16)
            oret_ref[rs, :] = o_ret.astype(BF16)
            merged = _sigmoid(gm_ref[rs, :].astype(F32)) * o_mla + _sigmoid(gr_ref[rs, :].astype(F32)) * o_ret
            m = _dot(merged.astype(BF16), wout_ref[...])
            m_ref[rs, :] = m
            ho_ref[rs, :] = h_ref[rs, :] + _rms_fwd(m, post_ref[...])

    def full(r, c):
        return pl.BlockSpec((r, c), lambda i: (0, 0))

    def rows(c, j=0):
        return pl.BlockSpec((tT, c), lambda i: (i, j))

    return pl.pallas_call(
        body, name=name, grid=(T // tT,),
        in_specs=[rows(QW), rows(RW), rows(RW, 3), rows(D, g_blk), rows(D, g_blk + 1), full(1, RW),
                  full(QW, D), full(RW, D), full(D, D), rows(D), full(1, D)],
        out_specs=[rows(D), rows(D), rows(D), rows(D)],
        out_shape=[jax.ShapeDtypeStruct((T, D), BF16), jax.ShapeDtypeStruct((T, D), BF16),
                   jax.ShapeDtypeStruct((T, D), F32), jax.ShapeDtypeStruct((T, D), F32)],
        compiler_params=_params(("parallel",)),
    )(o, yn, proj, proj, proj, gn_w, w_bm, w_br, w_out, h, post_w)


def _merge_bwd(dho, m, post_w, omla, oret, proj, yn, gn_w, o, w_out, w_bm, w_br, *, name):
    T, D = dho.shape
    tT = min(MERGE_TILE, T)
    g_blk = PROJ_FIXED // D

    nT = T // tT

    def body(dho_ref, m_ref, post_ref, omla_ref, oret_ref, rg_ref, gm_ref, gr_ref, yn_ref, gnw_ref, o_ref,
             wout_ref, wbm_ref, wbr_ref,
             dgm_ref, dgr_ref, do_ref, delta_ref, drg_ref, dyn_ref, gpost_ref, ggn_ref,
             dwout_ref, dwbm_ref, dwbr_ref, acc_out, acc_bm, acc_br):
        @pl.when(pl.program_id(0) == 0)
        def _():
            gpost_ref[...] = jnp.zeros_like(gpost_ref)
            ggn_ref[...] = jnp.zeros_like(ggn_ref)
            acc_out[...] = jnp.zeros_like(acc_out)
            acc_bm[...] = jnp.zeros_like(acc_bm)
            acc_br[...] = jnp.zeros_like(acc_br)

        dm, gp = _rms_bwd(m_ref[...], post_ref[...], dho_ref[...])
        gpost_ref[...] += gp
        dmb = dm.astype(BF16)
        dmerged = _dot_nt(dmb, wout_ref[...])
        o_mla = omla_ref[...].astype(F32)
        o_ret = oret_ref[...].astype(F32)
        sgm = _sigmoid(gm_ref[...].astype(F32))
        sgr = _sigmoid(gr_ref[...].astype(F32))
        acc_out[...] += _dot_tn((sgm * o_mla + sgr * o_ret).astype(BF16), dmb)
        dgm_ref[...] = (dmerged * o_mla * sgm * (1.0 - sgm)).astype(BF16)
        dgr_ref[...] = (dmerged * o_ret * sgr * (1.0 - sgr)).astype(BF16)
        domla = (dmerged * sgm).astype(BF16)
        acc_bm[...] += _dot_tn(o_ref[...], domla)
        do = _dot_nt(domla, wbm_ref[...])
        do_ref[...] = do.astype(BF16)
        for hd in range(MLA_HEADS):
            sl = slice(hd * HP, (hd + 1) * HP)
            d = jnp.sum(do[:, sl] * o_ref[:, sl].astype(F32), axis=-1, keepdims=True)
            delta_ref[:, sl] = jnp.broadcast_to(d, (tT, HP))
        doret = (dmerged * sgr).astype(BF16)
        dgated = _dot_nt(doret, wbr_ref[...])
        rg = rg_ref[...].astype(F32)
        sg = _sigmoid(rg)
        srg = rg * sg
        ynv = yn_ref[...].astype(F32)
        yw = ynv * gnw_ref[...]
        acc_br[...] += _dot_tn((srg * yw).astype(BF16), doret)
        drg_ref[...] = (dgated * yw * (sg * (1.0 + rg * (1.0 - sg)))).astype(BF16)
        dgs = dgated * srg
        dyn_ref[...] = dgs * gnw_ref[...]
        ggn_ref[...] += jnp.sum(dgs * ynv, axis=0, keepdims=True)

        @pl.when(pl.program_id(0) == nT - 1)
        def _():
            dwout_ref[...] = acc_out[...].astype(BF16)
            dwbm_ref[...] = acc_bm[...].astype(BF16)
            dwbr_ref[...] = acc_br[...].astype(BF16)

    def full(r, c):
        return pl.BlockSpec((r, c), lambda i: (0, 0), pipeline_mode=pl.Buffered(1))

    def rows(c, j=0):
        return pl.BlockSpec((tT, c), lambda i: (i, j))

    return pl.pallas_call(
        body, name=name, grid=(nT,),
        in_specs=[rows(D), rows(D), full(1, D), rows(D), rows(D), rows(RW, 3), rows(D, g_blk), rows(D, g_blk + 1),
                  rows(RW), full(1, RW), rows(QW), full(D, D), full(QW, D), full(RW, D)],
        out_specs=[rows(D), rows(D), rows(QW), rows(QW), rows(RW), rows(RW), full(1, D), full(1, RW),
                   full(D, D), full(QW, D), full(RW, D)],
        out_shape=[jax.ShapeDtypeStruct((T, D), BF16)] * 2
        + [jax.ShapeDtypeStruct((T, QW), BF16), jax.ShapeDtypeStruct((T, QW), F32),
           jax.ShapeDtypeStruct((T, RW), BF16), jax.ShapeDtypeStruct((T, RW), F32),
           jax.ShapeDtypeStruct((1, D), F32), jax.ShapeDtypeStruct((1, RW), F32),
           jax.ShapeDtypeStruct((D, D), BF16), jax.ShapeDtypeStruct((QW, D), BF16), jax.ShapeDtypeStruct((RW, D), BF16)],
        scratch_shapes=[pltpu.VMEM((D, D), F32), pltpu.VMEM((QW, D), F32), pltpu.VMEM((RW, D), F32)],
        compiler_params=_params(("arbitrary",)),
    )(dho, m, post_w, omla, oret, proj, proj, proj, yn, gn_w, o, w_out, w_bm, w_br)


def _mesh_pos():
    return lax.axis_index("x"), lax.axis_index("y"), lax.axis_index("c")


class _Gather:
    def __init__(self, shards):
        self.operands = list(shards)
        self.n = len(shards)
        self.out_shape = [jax.ShapeDtypeStruct((N_DEV,) + s.shape, s.dtype) for s in shards]
        self.scratch = [pltpu.SemaphoreType.DMA((7 * self.n,)), pltpu.SemaphoreType.DMA((7 * self.n,)),
                        pltpu.SemaphoreType.DMA((self.n,))]

    def phase(self, p, x_refs, out_refs, sems):
        send_sems, recv_sems, local_sems = sems
        x, y, c = _mesh_pos()
        me, sibling = (x, y, c), (x, y, 1 - c)
        chips = [(1 - x, y), (x, 1 - y), (1 - x, 1 - y)]

        def copy(w, k, block, to, src=None):
            slot = out_refs[w].at[4 * block[0] + 2 * block[1] + block[2]]
            return pltpu.make_async_remote_copy(
                src_ref=slot if src is None else src, dst_ref=slot,
                send_sem=send_sems.at[7 * w + k], recv_sem=recv_sems.at[7 * w + k],
                device_id=to, device_id_type=pl.DeviceIdType.MESH)

        for w in range(self.n):
            mine = pltpu.make_async_copy(x_refs[w], out_refs[w].at[4 * x + 2 * y + c], local_sems.at[w])
            first = [copy(w, 0, me, sibling, src=x_refs[w])]
            first += [copy(w, 1 + j, me, (*chip, c), src=x_refs[w]) for j, chip in enumerate(chips)]
            passed = [copy(w, 4 + j, (*chip, c), sibling) for j, chip in enumerate(chips)]
            if p == 0:
                mine.start()
                for cp in first:
                    cp.start()
            elif p == 1:
                for j, chip in enumerate(chips):
                    copy(w, 1 + j, (*chip, c), me).wait_recv()
                    passed[j].start()
            else:
                copy(w, 0, sibling, me).wait_recv()
                for j, chip in enumerate(chips):
                    copy(w, 4 + j, (*chip, 1 - c), me).wait_recv()
                for cp in first + passed:
                    cp.wait_send()
                mine.wait()


class _Scatter:
    def __init__(self, grads, whole=()):
        self.n_sliced = len(grads)
        self.operands = list(grads) + list(whole)
        self.n = len(self.operands)
        self.out_shape = [jax.ShapeDtypeStruct(g.shape, g.dtype) for g in grads]
        self.out_shape += [jax.ShapeDtypeStruct((N_DEV,) + a.shape, a.dtype) for a in whole]
        n_sem = (N_DEV - 1) * self.n
        self.scratch = [pltpu.SemaphoreType.DMA((n_sem,)), pltpu.SemaphoreType.DMA((n_sem,)),
                        pltpu.SemaphoreType.DMA((self.n,))]

    def phase(self, p, in_refs, out_refs, sems):
        if p == 1:
            return
        send_sems, recv_sems, local_sems = sems
        x, y, c = _mesh_pos()
        me = 4 * x + 2 * y + c

        def src(w, dev):
            return in_refs[w].at[dev] if w < self.n_sliced else in_refs[w]

        for w in range(self.n):
            own = None if local_sems is None else pltpu.make_async_copy(src(w, me), out_refs[w].at[me], local_sems.at[w])
            sends, recvs = [], []
            for r in range(1, N_DEV):
                px = 1 - x if r & 4 else x
                py = 1 - y if r & 2 else y
                pc = 1 - c if r & 1 else c
                peer, pidx = (px, py, pc), 4 * px + 2 * py + pc
                k = (N_DEV - 1) * w + r - 1
                sends.append(pltpu.make_async_remote_copy(
                    src_ref=src(w, pidx), dst_ref=out_refs[w].at[me], send_sem=send_sems.at[k],
                    recv_sem=recv_sems.at[k], device_id=peer, device_id_type=pl.DeviceIdType.MESH))
                recvs.append(pltpu.make_async_remote_copy(
                    src_ref=src(w, me), dst_ref=out_refs[w].at[pidx], send_sem=send_sems.at[k],
                    recv_sem=recv_sems.at[k], device_id=peer, device_id_type=pl.DeviceIdType.MESH))
            if p == 0:
                if own is not None:
                    own.start()
                for cp in sends:
                    cp.start()
            else:
                for cp in recvs:
                    cp.wait_recv()
                for cp in sends:
                    cp.wait_send()
                if own is not None:
                    own.wait()


class _SplitScatter:
    def __init__(self, ex, name):
        self.ex, self.name = ex, name

    def _specs(self):
        ex = self.ex
        hbm = pl.BlockSpec(memory_space=pltpu.HBM)
        sem = pl.BlockSpec(memory_space=pltpu.SEMAPHORE)
        effect = pltpu.CompilerParams(has_side_effects=pltpu.SideEffectType.DATAFLOW_SIDE_EFFECTING)
        buffers = [pltpu.HBM(a.shape, a.dtype) for a in ex.operands] + [pltpu.HBM(s.shape, s.dtype) for s in ex.out_shape]
        return hbm, sem, effect, buffers

    def start(self):
        ex, n = self.ex, self.ex.n
        n_sem = (N_DEV - 1) * n
        hbm, sem, effect, buffers = self._specs()
        in_hbm = lambda a: pltpu.with_memory_space_constraint(a, pltpu.HBM)

        me = 4 * lax.axis_index("x") + 2 * lax.axis_index("y") + lax.axis_index("c")
        lands = []
        for w, (a, s) in enumerate(zip(ex.operands, ex.out_shape)):
            mine = lax.dynamic_index_in_dim(a, me, 0, keepdims=True) if w < ex.n_sliced else a[None]
            lands.append(lax.dynamic_update_slice_in_dim(lax.empty(s.shape, s.dtype), mine, me, 0))

        def start_body(*refs):
            ex.phase(0, refs[:n], refs[n:2 * n], (refs[2 * n], refs[2 * n + 1], None))
            refs[-1][...] = jnp.zeros_like(refs[-1])

        self.started = pl.pallas_call(
            start_body, name=self.name + "_start",
            out_shape=[pltpu.SemaphoreType.DMA((n_sem,)), pltpu.SemaphoreType.DMA((n_sem,))] + buffers
            + [jax.ShapeDtypeStruct((8, LANES), F32)],
            in_specs=[hbm] * (2 * n), out_specs=[sem, sem] + [hbm] * (2 * n) + [pl.BlockSpec(memory_space=pltpu.VMEM)],
            input_output_aliases={i: 2 + i for i in range(2 * n)}, compiler_params=effect,
        )(*[in_hbm(a) for a in ex.operands], *[in_hbm(a) for a in lands])
        return self.started[-1]

    def wait(self, after):
        ex, n = self.ex, self.ex.n
        hbm, sem, effect, buffers = self._specs()
        anyspec = pl.BlockSpec(memory_space=pl.ANY)

        def wait_body(*refs):
            ex.phase(2, refs[:n], refs[n:2 * n], (refs[2 * n], refs[2 * n + 1], None))

        done = pl.pallas_call(
            wait_body, name=self.name + "_wait", out_shape=buffers,
            in_specs=[hbm] * (2 * n) + [sem, sem] + [anyspec] * len(after), out_specs=[hbm] * (2 * n),
            input_output_aliases={i: i for i in range(2 * n)}, compiler_params=effect,
        )(*self.started[2:2 + 2 * n], self.started[0], self.started[1], *after)
        return done[n:]


def _exchange_alone(ex, *, name):
    n = ex.n

    def body(*refs):
        for p in range(3):
            ex.phase(p, refs[:n], refs[n:2 * n], refs[2 * n:])

    anyspec = pl.BlockSpec(memory_space=pl.ANY)
    return pl.pallas_call(body, name=name, out_shape=ex.out_shape, in_specs=[anyspec] * n,
                          out_specs=[anyspec] * n, scratch_shapes=ex.scratch)(*ex.operands)


def _adam_step(w_ref, p_ref, m_ref, v_ref, g_ref, d_ref, nm_ref, nv_ref):
    g = p_ref[0].astype(F32)
    for j in range(1, N_DEV):
        g = g + p_ref[j].astype(F32)
    g_ref[...] = g
    nm = ADAM_B1 * m_ref[...] + (1.0 - ADAM_B1) * g
    nv = ADAM_B2 * v_ref[...] + (1.0 - ADAM_B2) * (g * g)
    nm_ref[...] = nm
    nv_ref[...] = nv
    m_hat = nm / (1.0 - ADAM_B1 ** ADAM_STEP)
    v_hat = nv / (1.0 - ADAM_B2 ** ADAM_STEP)
    d_ref[...] = -ADAM_LR * (m_hat / (jnp.sqrt(v_hat) + ADAM_EPS) + ADAM_WD * w_ref[...])


def _adamw_vectors(ws, parts, ms, vs, *, name):
    n = len(ws)

    def body(*refs):
        w_refs, p_refs, m_refs, v_refs = (refs[i * n:(i + 1) * n] for i in range(4))
        outs = refs[4 * n:]
        for i in range(n):
            _adam_step(w_refs[i], p_refs[i], m_refs[i], v_refs[i], *outs[4 * i:4 * i + 4])

    return pl.pallas_call(
        body, name=name,
        out_shape=[jax.ShapeDtypeStruct(w.shape, F32) for w in ws for _ in range(4)],
    )(*ws, *parts, *ms, *vs)


def _adamw(w, parts, m, v, after, *, name):
    G, R, n = w.shape
    tn = 512 if (n > 512 and n % 512 == 0) else n
    tr = R
    for t in range(16, R, 16):
        if R % t == 0 and t * tn <= ADAM_BLOCK_CAP:
            tr = t
    if R * tn <= ADAM_BLOCK_CAP:
        tr = R

    def body(w_ref, p_ref, m_ref, v_ref, after_ref, g_ref, d_ref, nm_ref, nv_ref):
        _adam_step(w_ref, p_ref, m_ref, v_ref, g_ref, d_ref, nm_ref, nv_ref)

    blk = pl.BlockSpec((None, tr, tn), lambda g, i, j: (g, i, j))
    return pl.pallas_call(
        body, name=name, grid=(G, R // tr, n // tn),
        in_specs=[blk, pl.BlockSpec((N_DEV, None, tr, tn), lambda g, i, j: (0, g, i, j)), blk, blk,
                  pl.BlockSpec((8, LANES), lambda g, i, j: (0, 0))],
        out_specs=[blk, blk, blk, blk],
        out_shape=[jax.ShapeDtypeStruct((G, R, n), F32)] * 4,
        compiler_params=_params(("parallel", "parallel", "parallel")),
    )(w, parts, m, v, after)


def _pad_last(a, width):
    return jnp.pad(a, [(0, 0)] * (a.ndim - 1) + [(0, width - a.shape[-1])])


def _cols_of(g):
    return g.transpose(1, 0, 2).reshape(g.shape[1], N_DEV * g.shape[2])


def _col_shards(w):
    return w.reshape(w.shape[0], N_DEV, w.shape[1] // N_DEV).transpose(1, 0, 2)


def kernel(x, positions, ffn1_pre_w, ffn1_w1, ffn1_w2, ffn1_post_w, mix_pre_w, w_in, mla_q_norm_w, mla_w_uq, mla_kv_norm_w, mla_w_ukv, ret_gn_w, w_branch_mla, w_branch_ret, w_out, mix_post_w, ffn2_pre_w, ffn2_w1, ffn2_w2, ffn2_post_w, loss_target, m_ffn1_pre_w, m_ffn1_w1, m_ffn1_w2, m_ffn1_post_w, m_mix_pre_w, m_w_in, m_mla_q_norm_w, m_mla_w_uq, m_mla_kv_norm_w, m_mla_w_ukv, m_ret_gn_w, m_w_branch_mla, m_w_branch_ret, m_w_out, m_mix_post_w, m_ffn2_pre_w, m_ffn2_w1, m_ffn2_w2, m_ffn2_post_w, v_ffn1_pre_w, v_ffn1_w1, v_ffn1_w2, v_ffn1_post_w, v_mix_pre_w, v_w_in, v_mla_q_norm_w, v_mla_w_uq, v_mla_kv_norm_w, v_mla_w_ukv, v_ret_gn_w, v_w_branch_mla, v_w_branch_ret, v_w_out, v_mix_post_w, v_ffn2_pre_w, v_ffn2_w1, v_ffn2_w2, v_ffn2_post_w):
    T, D = x.shape[1], x.shape[2]
    h0 = x[0]
    tgt = loss_target[0]
    pos = positions.reshape(T, 1).astype(F32)

    big = [("ffn1_w1", ffn1_w1, m_ffn1_w1, v_ffn1_w1), ("ffn1_w2", ffn1_w2, m_ffn1_w2, v_ffn1_w2),
           ("w_in", w_in, m_w_in, v_w_in), ("mla_w_uq", mla_w_uq, m_mla_w_uq, v_mla_w_uq),
           ("mla_w_ukv", mla_w_ukv, m_mla_w_ukv, v_mla_w_ukv),
           ("w_branch_mla", w_branch_mla, m_w_branch_mla, v_w_branch_mla),
           ("w_branch_ret", w_branch_ret, m_w_branch_ret, v_w_branch_ret),
           ("w_out", w_out, m_w_out, v_w_out),
           ("ffn2_w1", ffn2_w1, m_ffn2_w1, v_ffn2_w1), ("ffn2_w2", ffn2_w2, m_ffn2_w2, v_ffn2_w2)]
    small = [("ffn1_pre_w", ffn1_pre_w, m_ffn1_pre_w, v_ffn1_pre_w), ("ffn1_post_w", ffn1_post_w, m_ffn1_post_w, v_ffn1_post_w),
             ("mix_pre_w", mix_pre_w, m_mix_pre_w, v_mix_pre_w), ("mla_q_norm_w", mla_q_norm_w, m_mla_q_norm_w, v_mla_q_norm_w),
             ("mla_kv_norm_w", mla_kv_norm_w, m_mla_kv_norm_w, v_mla_kv_norm_w), ("ret_gn_w", ret_gn_w, m_ret_gn_w, v_ret_gn_w),
             ("mix_post_w", mix_post_w, m_mix_post_w, v_mix_post_w), ("ffn2_pre_w", ffn2_pre_w, m_ffn2_pre_w, v_ffn2_pre_w),
             ("ffn2_post_w", ffn2_post_w, m_ffn2_post_w, v_ffn2_post_w)]

    half = ffn1_w2.shape[1]
    hp = -(-half // LANES) * LANES

    def rows_view(w):
        return w[0].T

    def send_w1(w):
        return jnp.pad(rows_view(w).reshape(2, half, D), ((0, 0), (0, hp - half), (0, 0))).reshape(2 * hp, D).astype(BF16)

    def send_w2(w):
        return jnp.pad(w[0], ((0, hp - half), (0, 0))).astype(BF16)

    mixer = ["w_in", "mla_w_uq", "mla_w_ukv", "w_branch_mla", "w_branch_ret", "w_out"]
    uq_w = MLA_NOPE + MLA_ROPE
    mixer_send = [rows_view(w_in).astype(BF16), jnp.pad(rows_view(mla_w_uq), ((0, HP - uq_w), (0, 0))).astype(BF16),
                  mla_w_ukv[0].astype(BF16), w_branch_mla[0].astype(BF16), w_branch_ret[0].astype(BF16),
                  w_out[0].astype(BF16)]

    w1a, w2a = _exchange_alone(_Gather([send_w1(ffn1_w1), send_w2(ffn1_w2)]), name="gather_ffn1")
    w2a = w2a.reshape(N_DEV // 2, 2 * hp, D)
    u1, f1, h1, a0, *got = _ffn_fwd(h0, ffn1_pre_w, w1a, w2a, ffn1_post_w, None, name="ffn1_fwd_gather_mixer",
                                exchange=_Gather(mixer_send))
    fw = dict(zip(mixer, got))

    wi = fw["w_in"].reshape(-1, D)
    cq_w, ckv_w, kr_w = wi[0:384], wi[384:640], wi[640:672]
    rq_w, rk_w = wi[672:928], wi[928:1184]
    rv_w, rg_w = wi[1184:1696], wi[1696:2208]
    gm_w, gr_w = wi[2208:2208 + D], wi[2208 + D:2208 + 2 * D]
    zer = lambda n: jnp.zeros((n, D), BF16)
    head_rows = lambda a, h: jnp.pad(a.reshape(h, -1, D), ((0, 0), (0, HP - a.shape[0] // h), (0, 0))).reshape(h * HP, D)
    w_in_p = jnp.concatenate([head_rows(rq_w, RET_HEADS), head_rows(rk_w, RET_HEADS), rv_w, rg_w,
                              cq_w, ckv_w, zer(MLA_NOPE), kr_w, zer(HP - MLA_NOPE - MLA_ROPE), zer(AW - 768),
                              gm_w, gr_w], axis=0)
    w_uq_p = fw["mla_w_uq"].reshape(QW, MLA_Q_RANK)
    ukv = fw["mla_w_ukv"].transpose(1, 0, 2)
    w_kv_p = jnp.concatenate([_pad_last(ukv[:, :, :MLA_NOPE], HP).reshape(MLA_KV_RANK, QW),
                              _pad_last(ukv[:, :, MLA_NOPE:], HP).reshape(MLA_KV_RANK, QW)], axis=1)
    w_bm_p = jnp.pad(_cols_of(fw["w_branch_mla"]).reshape(MLA_HEADS, MLA_V, D),
                     ((0, 0), (0, HP - MLA_V), (0, 0))).reshape(QW, D)
    w_br, w_o = _cols_of(fw["w_branch_ret"]), fw["w_out"].reshape(D, D)
    tab_mla = _rope_table(MLA_NOPE, MLA_ROPE // 2)
    tab_ret = _rope_table(0, RET_DK // 2)

    proj, a1 = _rms_matmul(h1, mix_pre_w, w_in_p, name="mixer_in_proj")
    q, k, v = _mla_prep_fwd(proj, pos, mla_q_norm_w, mla_kv_norm_w, w_uq_p, w_kv_p, tab_mla, name="mla_prep_fwd")
    o, lse, w1b, w2b = _flash_fwd(q, k, v, name="mla_attn_fwd_gather_ffn2",
                                  exchange=_Gather([send_w1(ffn2_w1), send_w2(ffn2_w2)]))
    w2b = w2b.reshape(N_DEV // 2, 2 * hp, D)
    ypre, yn, rprev = _ret_fwd(proj, pos, tab_ret, name="retention_fwd")
    omla, oret, m, h2 = _merge_fwd(o, yn, proj, ret_gn_w, w_bm_p, w_br, w_o, h1, mix_post_w, name="merge_fwd")
    u2, f2, _, a2, dy, lossp = _ffn_fwd(h2, ffn2_pre_w, w1b, w2b, ffn2_post_w, tgt, name="ffn2_fwd_loss")

    def grad(x, dy, tag, after=None):
        return _matmul_tn(x if x.ndim == 3 else x[None], dy if dy.ndim == 3 else dy[None], name=tag, after=after)

    g2, du2, df2, dh2, gpost2, gpre2 = _ffn_bwd(dy, f2, ffn2_post_w, h2, ffn2_pre_w, u2, w2b, w1b, name="ffn2_bwd")
    dw1b, = grad(du2.reshape(N_DEV, T, 2 * hp), a2, "ffn2_dw1")
    dw2b = grad(g2, df2, "ffn2_dw2")[0].reshape(N_DEV, hp, D)
    (dgm, dgr, do, delta, drg, dyn, gpostm, ggn, dw_out, dw_bm_p, dw_br) = _merge_bwd(
        dh2, m, mix_post_w, omla, oret, proj, yn, ret_gn_w, o, w_o, w_bm_p, w_br, name="merge_bwd")
    sc_ffn2 = _SplitScatter(_Scatter([dw1b, dw2b]), "scatter_ffn2")
    dq, dk, dv = _flash_bwd(q, k, v, do, lse, delta, name="mla_attn_bwd", after=sc_ffn2.start())
    da, gqn, gkvn, dw_uq_p, dw_kv_p = _mla_prep_bwd(dq, dk, dv, proj, pos, mla_q_norm_w, mla_kv_norm_w, w_uq_p, w_kv_p, tab_mla, name="mla_prep_bwd")
    drq, drk, drv = _ret_bwd(dyn, ypre, proj, pos, tab_ret, rprev, name="retention_bwd")
    dproj = jnp.concatenate([drq, drk, drv, drg, da, dgm, dgr], axis=1)
    dw_in_p = grad(dproj, a1, "dw_in")[0][0]

    dw_uq = dw_uq_p.reshape(MLA_HEADS, HP, MLA_Q_RANK)[:, :uq_w]
    dkp = dw_kv_p[:, :QW].reshape(MLA_KV_RANK, MLA_HEADS, HP)[:, :, :MLA_NOPE]
    dvp = dw_kv_p[:, QW:].reshape(MLA_KV_RANK, MLA_HEADS, HP)[:, :, :MLA_V]
    dw_ukv = jnp.concatenate([dkp, dvp], axis=2).transpose(1, 0, 2)
    dw_bm = dw_bm_p.reshape(MLA_HEADS, HP, D)[:, :MLA_V].reshape(MLA_HEADS * MLA_V, D)
    small_mixer_grads = [dw_uq, dw_ukv, _col_shards(dw_bm), _col_shards(dw_br), dw_out.reshape(N_DEV, D // N_DEV, D)]
    dh1, gmixpre = _proj_bwd(dproj, w_in_p, h1, mix_pre_w, dh2, name="mixer_in_bwd")
    unhead = lambda a, h, wd: a.reshape(h, HP, D)[:, :wd].reshape(h * wd, D)
    c0 = 4 * RW
    dw_in = jnp.concatenate([
        dw_in_p[c0:c0 + 384], dw_in_p[c0 + 384:c0 + 640], dw_in_p[c0 + 640 + MLA_NOPE:c0 + 640 + MLA_NOPE + MLA_ROPE],
        unhead(dw_in_p[0:RW], RET_HEADS, RET_DK), unhead(dw_in_p[RW:2 * RW], RET_HEADS, RET_DK),
        dw_in_p[2 * RW:3 * RW], dw_in_p[3 * RW:4 * RW],
        dw_in_p[PROJ_FIXED:PROJ_FIXED + D], dw_in_p[PROJ_FIXED + D:PROJ_FIXED + 2 * D]], axis=0).reshape(N_DEV, -1, D)
    sc_w_in = _SplitScatter(_Scatter([dw_in] + small_mixer_grads), "scatter_mixer")
    g1, du1, df1, dx, gpost1, gpre1 = _ffn_bwd(
        dh1, f1, ffn1_post_w, h0, ffn1_pre_w, u1, w2a, w1a, name="ffn1_bwd", after=sc_w_in.start())
    dw2a = grad(g1, df1, "ffn1_dw2")[0].reshape(N_DEV, hp, D)
    sc_dw2a = _SplitScatter(_Scatter([dw2a]), "scatter_ffn1_dw2")
    dw1a, = grad(du1.reshape(N_DEV, T, 2 * hp), a0, "ffn1_dw1", after=sc_dw2a.start())

    small_g = {"ffn1_pre_w": gpre1, "ffn1_post_w": gpost1, "mix_pre_w": gmixpre, "mla_q_norm_w": gqn,
               "mla_kv_norm_w": gkvn, "ret_gn_w": ggn, "mix_post_w": gpostm, "ffn2_pre_w": gpre2, "ffn2_post_w": gpost2}
    sc_last = _SplitScatter(_Scatter([dw1a], whole=[small_g[nm] for nm, *_ in small] + [lossp]), "scatter_ffn1_dw1")
    token = sc_last.start()
    recv_ffn2 = sc_ffn2.wait([token])
    recv_mixer = sc_w_in.wait([token])
    recv_w2a, = sc_dw2a.wait([token])
    parts = dict(zip(mixer, recv_mixer))
    parts.update(ffn1_w2=recv_w2a, ffn2_w1=recv_ffn2[0], ffn2_w2=recv_ffn2[1])
    as_is = (lambda a: a, lambda p: p[:, None], lambda a: a)
    views = {nm: as_is for nm, *_ in big}
    for nm in ("ffn1_w1", "ffn2_w1"):
        views[nm] = (lambda a: rows_view(a).reshape(2, half, D), lambda p: p.reshape(N_DEV, 2, hp, D),
                     lambda a: a.reshape(2 * half, D).T[None])
    for nm in ("w_in", "mla_w_uq"):
        views[nm] = (lambda a: rows_view(a)[None], lambda p: p[:, None], lambda a: a[0].T[None])

    def update(nm, w, m_, v_, after):
        to_view, parts_view, back = views[nm]
        return [back(a) for a in _adamw(to_view(w), parts_view(parts[nm]), to_view(m_), to_view(v_), after,
                                        name="adamw_" + nm)]

    big_out = {nm: update(nm, w, m_, v_, token) for nm, w, m_, v_ in big if nm != "ffn1_w1"}
    recv_w1a, *small_parts, loss_parts = sc_last.wait([d[0] for d in big_out.values()])
    loss = jnp.sum(loss_parts[:, ::8, 0])
    parts["ffn1_w1"] = recv_w1a
    big_out["ffn1_w1"] = update("ffn1_w1", ffn1_w1, m_ffn1_w1, v_ffn1_w1, jnp.zeros((8, LANES), F32))
    small_out = _adamw_vectors([w for _, w, _, _ in small], small_parts, [a for _, _, a, _ in small],
                               [a for _, _, _, a in small], name="adamw_replicated")

    order = ["ffn1_pre_w", "ffn1_w1", "ffn1_w2", "ffn1_post_w", "mix_pre_w", "w_in", "mla_q_norm_w", "mla_w_uq",
             "mla_kv_norm_w", "mla_w_ukv", "ret_gn_w", "w_branch_mla", "w_branch_ret", "w_out", "mix_post_w",
             "ffn2_pre_w", "ffn2_w1", "ffn2_w2", "ffn2_post_w"]
    outs = [loss, dx[None]]
    for i in range(4):
        both = {nm: big_out[nm][i] for nm in big_out}
        both.update({nm: small_out[4 * j + i] for j, (nm, *_) in enumerate(small)})
        outs += [both[nm] for nm in order]
    return tuple(outs)
```
